```python
import jax, jax.numpy as jnp
from jax import lax
import numpy as np

D_MODEL = 1024
BATCH = 8
SEQ = 2048
DEPTH = 1

CONV_GROUPS = 16
CONV_GROUP_DIM = 64
D_CONV = CONV_GROUPS * CONV_GROUP_DIM
CONV_WIDTH = 3
N_Q_HEADS = 16
N_KV_HEADS = 2
HEAD_DIM = 64
Q_PER_KV = N_Q_HEADS // N_KV_HEADS
D_ATTN = N_Q_HEADS * HEAD_DIM
D_KV = N_KV_HEADS * HEAD_DIM
WINDOW = 128
BLOCK = 128
D_MIX = D_CONV + D_ATTN
D_IN_PROJ = 4 * D_CONV + 2 * D_ATTN + 2 * D_KV
SPLIT_POINTS = (D_CONV, 2 * D_CONV, 3 * D_CONV, 4 * D_CONV,
                4 * D_CONV + D_ATTN,
                4 * D_CONV + D_ATTN + D_KV,
                4 * D_CONV + D_ATTN + 2 * D_KV)
RMS_EPS = 1e-5

kernel_name = "hybrid_shortconv_swa_sink_alibi_parallel"


def rms_norm(x, gain):
    xf = x.astype(jnp.float32)
    y = xf * lax.rsqrt(jnp.mean(xf * xf, axis=-1, keepdims=True) + RMS_EPS)
    return (y * gain.astype(jnp.float32)).astype(x.dtype)


def alibi_slopes(n_heads):
    return jnp.exp2(-8.0 * jnp.arange(1, n_heads + 1, dtype=jnp.float32) / n_heads)


def short_conv(u, w):
    k = w[:, None, :].astype(u.dtype)
    return lax.conv_general_dilated(
        u, k, window_strides=(1,), padding=[(CONV_WIDTH - 1, 0)],
        dimension_numbers=("NWC", "WIO", "NWC"), feature_group_count=u.shape[-1])


def sliding_window_attention(q, k, v, sinks):
    bsz, seq, _ = q.shape
    nb = seq // BLOCK
    q = q.reshape(bsz, nb, BLOCK, N_KV_HEADS, Q_PER_KV, HEAD_DIM)
    k = k.reshape(bsz, nb, BLOCK, N_KV_HEADS, HEAD_DIM)
    v = v.reshape(bsz, nb, BLOCK, N_KV_HEADS, HEAD_DIM)
    pad = ((0, 0), (1, 0), (0, 0), (0, 0), (0, 0))
    k_band = jnp.concatenate([jnp.pad(k, pad)[:, :-1], k], axis=2)
    v_band = jnp.concatenate([jnp.pad(v, pad)[:, :-1], v], axis=2)
    scores = jnp.einsum("bnqhgd,bnkhd->bnhgqk", q, k_band).astype(jnp.float32)
    scores = scores * (HEAD_DIM ** -0.5)
    qi = jnp.arange(BLOCK)
    kj = jnp.arange(2 * BLOCK)
    dist = BLOCK + qi[:, None] - kj[None, :]
    key_pos = jnp.arange(nb)[:, None] * BLOCK - BLOCK + kj[None, :]
    valid = ((dist >= 0) & (dist < WINDOW))[None, :, :] & (key_pos >= 0)[:, None, :]
    slopes = alibi_slopes(N_Q_HEADS).reshape(N_KV_HEADS, Q_PER_KV)
    alibi = -slopes[:, :, None, None] * dist.astype(jnp.float32)
    scores = jnp.where(valid[None, :, None, None], scores + alibi[None, None], -jnp.inf)
    sink = sinks.astype(jnp.float32).reshape(N_KV_HEADS, Q_PER_KV)[None, None, :, :, None, None]
    m = jnp.maximum(jnp.max(scores, axis=-1, keepdims=True), sink)
    p = jnp.exp(scores - m)
    denom = jnp.sum(p, axis=-1, keepdims=True) + jnp.exp(sink - m)
    probs = (p / denom).astype(v.dtype)
    out = jnp.einsum("bnhgqk,bnkhd->bnqhgd", probs, v_band)
    return out.reshape(bsz, seq, D_ATTN)


def _fwd_setup_inputs(seed: int = 0) -> dict:
    key = jax.random.key(seed)
    ks = jax.random.split(key, 10)
    f32 = jnp.float32
    x = jax.random.normal(ks[0], (BATCH, SEQ, D_MODEL), f32)
    norm_in = 1.0 + 0.1 * jax.random.normal(ks[1], (DEPTH, D_MODEL), f32)
    w_in = jax.random.normal(ks[2], (DEPTH, D_MODEL, D_IN_PROJ), f32) * D_MODEL ** -0.5
    conv_w = jax.random.normal(ks[3], (DEPTH, CONV_WIDTH, D_CONV), f32) * CONV_WIDTH ** -0.5
    attn_sinks = jax.random.normal(ks[4], (DEPTH, N_Q_HEADS), f32)
    norm_conv_out = 1.0 + 0.1 * jax.random.normal(ks[5], (DEPTH, D_CONV), f32)
    norm_attn_out = 1.0 + 0.1 * jax.random.normal(ks[6], (DEPTH, D_ATTN), f32)
    w_out = jax.random.normal(ks[7], (DEPTH, D_MIX, D_MODEL), f32) * D_MIX ** -0.5
    norm_final = 1.0 + 0.1 * jax.random.normal(ks[8], (D_MODEL,), f32)
    return {"x": x, "norm_in": norm_in, "w_in": w_in, "conv_w": conv_w,
            "attn_sinks": attn_sinks, "norm_conv_out": norm_conv_out,
            "norm_attn_out": norm_attn_out, "w_out": w_out, "norm_final": norm_final}


def _fwd_reference(x, norm_in, w_in, conv_w, attn_sinks, norm_conv_out, norm_attn_out, w_out, norm_final):
    for layer in range(DEPTH):
        h = rms_norm(x, norm_in[layer])
        proj = jnp.einsum("bsd,de->bse", h, w_in[layer])
        cb, cc, cu, gate_c, q, k, v, gate_a = jnp.split(proj, SPLIT_POINTS, axis=-1)
        conv_y = cb * short_conv(cc * cu, conv_w[layer])
        conv_y = rms_norm(conv_y, norm_conv_out[layer]) * jax.nn.silu(gate_c)
        attn_y = sliding_window_attention(q, k, v, attn_sinks[layer])
        attn_y = rms_norm(attn_y, norm_attn_out[layer]) * jax.nn.silu(gate_a)
        mixed = jnp.concatenate([conv_y, attn_y], axis=-1)
        x = x + jnp.einsum("bse,ed->bsd", mixed, w_out[layer])
    return rms_norm(x, norm_final)


import jax as _jax
import jax.numpy as _jnp

TWIN_FORMAT = 'train_step'
FWD_PARAMS = ['x', 'norm_in', 'w_in', 'conv_w', 'attn_sinks', 'norm_conv_out', 'norm_attn_out', 'w_out', 'norm_final']
TWIN_WEIGHTS = ['norm_in', 'w_in', 'conv_w', 'attn_sinks', 'norm_conv_out', 'norm_attn_out', 'w_out', 'norm_final']
TWIN_DIFF_INPUT = 'x'
TWIN_INPUTS = ['x', 'norm_in', 'w_in', 'conv_w', 'attn_sinks', 'norm_conv_out', 'norm_attn_out', 'w_out', 'norm_final', 'loss_target', 'm_norm_in', 'm_w_in', 'm_conv_w', 'm_attn_sinks', 'm_norm_conv_out', 'm_norm_attn_out', 'm_w_out', 'm_norm_final', 'v_norm_in', 'v_w_in', 'v_conv_w', 'v_attn_sinks', 'v_norm_conv_out', 'v_norm_attn_out', 'v_w_out', 'v_norm_final']
TWIN_OUTPUTS = ['loss', 'grad_x', 'grad_norm_in', 'grad_w_in', 'grad_conv_w', 'grad_attn_sinks', 'grad_norm_conv_out', 'grad_norm_attn_out', 'grad_w_out', 'grad_norm_final', 'delta_norm_in', 'delta_w_in', 'delta_conv_w', 'delta_attn_sinks', 'delta_norm_conv_out', 'delta_norm_attn_out', 'delta_w_out', 'delta_norm_final', 'new_m_norm_in', 'new_m_w_in', 'new_m_conv_w', 'new_m_attn_sinks', 'new_m_norm_conv_out', 'new_m_norm_attn_out', 'new_m_w_out', 'new_m_norm_final', 'new_v_norm_in', 'new_v_w_in', 'new_v_conv_w', 'new_v_attn_sinks', 'new_v_norm_conv_out', 'new_v_norm_attn_out', 'new_v_w_out', 'new_v_norm_final']
TWIN_LEAF_KINDS = {'loss': 'loss', 'grad_x': 'grad_x', 'grad_norm_in': 'grad_w', 'grad_w_in': 'grad_w', 'grad_conv_w': 'grad_w', 'grad_attn_sinks': 'grad_w', 'grad_norm_conv_out': 'grad_w', 'grad_norm_attn_out': 'grad_w', 'grad_w_out': 'grad_w', 'grad_norm_final': 'grad_w', 'delta_norm_in': 'delta_w', 'delta_w_in': 'delta_w', 'delta_conv_w': 'delta_w', 'delta_attn_sinks': 'delta_w', 'delta_norm_conv_out': 'delta_w', 'delta_norm_attn_out': 'delta_w', 'delta_w_out': 'delta_w', 'delta_norm_final': 'delta_w', 'new_m_norm_in': 'new_m', 'new_m_w_in': 'new_m', 'new_m_conv_w': 'new_m', 'new_m_attn_sinks': 'new_m', 'new_m_norm_conv_out': 'new_m', 'new_m_norm_attn_out': 'new_m', 'new_m_w_out': 'new_m', 'new_m_norm_final': 'new_m', 'new_v_norm_in': 'new_v', 'new_v_w_in': 'new_v', 'new_v_conv_w': 'new_v', 'new_v_attn_sinks': 'new_v', 'new_v_norm_conv_out': 'new_v', 'new_v_norm_attn_out': 'new_v', 'new_v_w_out': 'new_v', 'new_v_norm_final': 'new_v'}


def _forward(args):
    return _fwd_reference(*[args[k] for k in FWD_PARAMS])


def _output_shape():
    out = _jax.eval_shape(lambda: _forward(_fwd_setup_inputs(0)))
    return out.shape, out.dtype

N_MICROBATCH = 1
ADAM_LR = 0.001
ADAM_B1 = 0.9
ADAM_B2 = 0.999
ADAM_EPS = 1e-08
ADAM_WD = 0.01
ADAM_STEP = 10
PER_EXAMPLE_BATCH_AXIS = {'x': 0, 'loss_target': 0}
SHARED_INPUTS = []
_WEIGHT_DTYPES = {'norm_in': _jnp.float32, 'w_in': _jnp.float32, 'conv_w': _jnp.float32, 'attn_sinks': _jnp.float32, 'norm_conv_out': _jnp.float32, 'norm_attn_out': _jnp.float32, 'w_out': _jnp.float32, 'norm_final': _jnp.float32}
MOMENT_SCALE = {'norm_in': 1.276706e-01, 'w_in': 5.060461e-02, 'conv_w': 4.904292e-02, 'attn_sinks': 5.282393e-02, 'norm_conv_out': 4.892967e-02, 'norm_attn_out': 4.977396e-02, 'w_out': 6.791244e-02, 'norm_final': 1.611923e+01}


def _to_microbatches(a, axis):
    t = _jnp.moveaxis(a, axis, 0)
    t = t.reshape((N_MICROBATCH, t.shape[0] // N_MICROBATCH) + t.shape[1:])
    return _jnp.moveaxis(t, 1, axis + 1)


def setup_inputs(seed: int = 0) -> dict:
    inp = _fwd_setup_inputs(seed)
    key = _jax.random.fold_in(_jax.random.key(seed), 7919)
    shape, _ = _output_shape()
    out = dict(inp)
    out["loss_target"] = _jax.random.normal(_jax.random.fold_in(key, 0), shape, _jnp.float32)
    for i, name in enumerate(TWIN_WEIGHTS):
        w = inp[name].astype(_jnp.float32)
        if MOMENT_SCALE is None:
            s = _jnp.sqrt(_jnp.mean(_jnp.square(w)) + 1e-30)
        else:
            s = MOMENT_SCALE[name]
        km, kv = _jax.random.split(_jax.random.fold_in(key, i + 1))
        out[name] = w
        out["m_" + name] = s * _jax.random.normal(km, w.shape, _jnp.float32)
        out["v_" + name] = (s * s) * _jax.random.uniform(kv, w.shape, _jnp.float32, 0.5, 1.5)
    if N_MICROBATCH > 1:
        for name, axis in PER_EXAMPLE_BATCH_AXIS.items():
            out[name] = _to_microbatches(out[name], axis)
    return {'x': out['x'], 'norm_in': out['norm_in'], 'w_in': out['w_in'], 'conv_w': out['conv_w'], 'attn_sinks': out['attn_sinks'], 'norm_conv_out': out['norm_conv_out'], 'norm_attn_out': out['norm_attn_out'], 'w_out': out['w_out'], 'norm_final': out['norm_final'], 'loss_target': out['loss_target'], 'm_norm_in': out['m_norm_in'], 'm_w_in': out['m_w_in'], 'm_conv_w': out['m_conv_w'], 'm_attn_sinks': out['m_attn_sinks'], 'm_norm_conv_out': out['m_norm_conv_out'], 'm_norm_attn_out': out['m_norm_attn_out'], 'm_w_out': out['m_w_out'], 'm_norm_final': out['m_norm_final'], 'v_norm_in': out['v_norm_in'], 'v_w_in': out['v_w_in'], 'v_conv_w': out['v_conv_w'], 'v_attn_sinks': out['v_attn_sinks'], 'v_norm_conv_out': out['v_norm_conv_out'], 'v_norm_attn_out': out['v_norm_attn_out'], 'v_w_out': out['v_w_out'], 'v_norm_final': out['v_norm_final']}


def _loss(weights, diff, rest, loss_target):
    with _jax.named_scope("forward"):
        args = {**rest, TWIN_DIFF_INPUT: diff, **{k: w.astype(_WEIGHT_DTYPES[k]) for k, w in weights.items()}}
        y = _forward(args)
    with _jax.named_scope("loss_head"):
        err = _jnp.square(y.astype(_jnp.float32) - loss_target)
        return 0.5 * _jnp.sum(_jnp.mean(err, axis=-1)) if err.ndim else 0.5 * err


def _adamw(w, g, m, v):
    m = ADAM_B1 * m + (1.0 - ADAM_B1) * g
    v = ADAM_B2 * v + (1.0 - ADAM_B2) * _jnp.square(g)
    m_hat = m / (1.0 - ADAM_B1 ** ADAM_STEP)
    v_hat = v / (1.0 - ADAM_B2 ** ADAM_STEP)
    delta = -ADAM_LR * (m_hat / (_jnp.sqrt(v_hat) + ADAM_EPS) + ADAM_WD * w)
    return delta, m, v


def reference(x, norm_in, w_in, conv_w, attn_sinks, norm_conv_out, norm_attn_out, w_out, norm_final, loss_target, m_norm_in, m_w_in, m_conv_w, m_attn_sinks, m_norm_conv_out, m_norm_attn_out, m_w_out, m_norm_final, v_norm_in, v_w_in, v_conv_w, v_attn_sinks, v_norm_conv_out, v_norm_attn_out, v_w_out, v_norm_final):
    given = dict(x=x, norm_in=norm_in, w_in=w_in, conv_w=conv_w, attn_sinks=attn_sinks, norm_conv_out=norm_conv_out, norm_attn_out=norm_attn_out, w_out=w_out, norm_final=norm_final, loss_target=loss_target, m_norm_in=m_norm_in, m_w_in=m_w_in, m_conv_w=m_conv_w, m_attn_sinks=m_attn_sinks, m_norm_conv_out=m_norm_conv_out, m_norm_attn_out=m_norm_attn_out, m_w_out=m_w_out, m_norm_final=m_norm_final, v_norm_in=v_norm_in, v_w_in=v_w_in, v_conv_w=v_conv_w, v_attn_sinks=v_attn_sinks, v_norm_conv_out=v_norm_conv_out, v_norm_attn_out=v_norm_attn_out, v_w_out=v_w_out, v_norm_final=v_norm_final)
    weights = {n: given[n] for n in TWIN_WEIGHTS}
    shared = {n: given[n] for n in SHARED_INPUTS}
    per_example = {n: given[n] for n in ['x']}
    grad_fn = _jax.value_and_grad(_loss, argnums=(0, 1))

    def one_microbatch(ex, loss_target):
        ex = dict(ex)
        diff = ex.pop(TWIN_DIFF_INPUT)
        return grad_fn(weights, diff, {**shared, **ex}, loss_target)

    if N_MICROBATCH == 1:
        loss, (grad_w, grad_x) = one_microbatch(per_example, given["loss_target"])
    else:
        def body(carry, xs):
            loss_sum, grad_sum = carry
            l_k, (gw_k, gx_k) = one_microbatch(xs[0], xs[1])
            with _jax.named_scope("update"):
                return (loss_sum + l_k, _jax.tree.map(_jnp.add, grad_sum, gw_k)), gx_k

        init = (_jnp.zeros((), _jnp.float32), _jax.tree.map(_jnp.zeros_like, weights))
        (loss, grad_w), grad_x = _jax.lax.scan(body, init, (per_example, given["loss_target"]))
    with _jax.named_scope("update"):
        delta_w, new_m, new_v = {}, {}, {}
        for n in TWIN_WEIGHTS:
            delta_w[n], new_m[n], new_v[n] = _adamw(weights[n], grad_w[n], given["m_" + n], given["v_" + n])
    return (loss, grad_x, *[grad_w[n] for n in TWIN_WEIGHTS], *[delta_w[n] for n in TWIN_WEIGHTS],
            *[new_m[n] for n in TWIN_WEIGHTS], *[new_v[n] for n in TWIN_WEIGHTS])
```

```python
import jax
import jax.numpy as jnp
from jax import lax
from jax.experimental import pallas as pl
from jax.experimental.pallas import tpu as pltpu

F32 = jnp.float32
BF16 = jnp.bfloat16
MESH = pl.DeviceIdType.MESH

D_MODEL = 1024
D_CONV = 1024
D_ATTN = 1024
D_KV = 128
D_MIX = D_CONV + D_ATTN
D_PC = 4 * D_CONV
D_IN_PROJ = D_PC + 2 * D_ATTN + 2 * D_KV
ROW_Q = D_PC
ROW_KV = ROW_Q + D_ATTN
ROW_GA = ROW_KV + 2 * D_KV
N_HEADS = 16
HEAD_DIM = 64
HEADS_PER_KV = 8
BLK = 128
N_CHIPS = 4
RMS_EPS = 1e-5
SCALE = HEAD_DIM ** -0.5
SLOPES = tuple(2.0 ** (-8.0 * (h + 1) / N_HEADS) for h in range(N_HEADS))

ADAM_LR, ADAM_B1, ADAM_B2, ADAM_EPS, ADAM_WD, ADAM_STEP = 0.001, 0.9, 0.999, 1e-08, 0.01, 10

TOK_TILE = 256
MIB = 1 << 20


def _params(vmem_mib, semantics=None):
    return pltpu.CompilerParams(dimension_semantics=semantics, vmem_limit_bytes=vmem_mib * MIB)


def _nn(a, b):
    return jnp.dot(a, b, preferred_element_type=F32)


def _nt(a, b):
    return lax.dot_general(a, b, (((1,), (1,)), ((), ())), preferred_element_type=F32)


def _tn(a, b):
    return lax.dot_general(a, b, (((0,), (0,)), ((), ())), preferred_element_type=F32)


def _rstd(v):
    return lax.rsqrt(jnp.mean(v * v, axis=-1, keepdims=True) + RMS_EPS)


def _rms_bwd(g, xhat, rstd):
    return rstd * (g - xhat * jnp.mean(g * xhat, axis=-1, keepdims=True))


def _silu_and_grad(g):
    s = jax.nn.sigmoid(g)
    return g * s, s * (1.0 + g * (1.0 - s))


def _resident(shape):
    return pl.BlockSpec(shape, lambda *_: (0,) * len(shape), pipeline_mode=pl.Buffered(1))


def _xor(a, b):
    return a + b - 2 * a * b


def _ag_weights(wt_sh, wo_sh, cw_sh):
    wt_rows, wo_rows = wt_sh.shape[0], wo_sh.shape[0]
    wt_half, wo_half = wt_rows // 2, wo_rows // 2

    def body(wt_ref, wo_ref, cw_ref, wt_out, wo_out, cw_out, send_sems, recv_sems, local_sems):
        x, y, c = lax.axis_index("x"), lax.axis_index("y"), lax.axis_index("c")
        j = 2 * x + y
        p1 = (_xor(x, c), _xor(y, 1 - c), c)
        p2 = (_xor(x, 1 - c), _xor(y, c), c)
        sib = (x, y, 1 - c)
        j1 = 2 * p1[0] + p1[1]
        j2 = 2 * p2[0] + p2[1]
        j3 = 3 - j

        def wt_rows_of(chip, half):
            return wt_out.at[pl.ds(pl.multiple_of(chip * wt_rows + half * wt_half, 16), wt_half), :]

        def wo_rows_of(chip, half):
            return wo_out.at[pl.ds(pl.multiple_of(chip * wo_rows + half * wo_half, 16), wo_half), :]

        def rcopy(k, src, dst, to):
            return pltpu.make_async_remote_copy(src_ref=src, dst_ref=dst, send_sem=send_sems.at[k],
                                                recv_sem=recv_sems.at[k], device_id=to, device_id_type=MESH)

        own_wt = wt_ref.at[pl.ds(pl.multiple_of(c * wt_half, 16), wt_half), :]
        own_wo = wo_ref.at[pl.ds(pl.multiple_of(c * wo_half, 16), wo_half), :]

        local = [
            pltpu.make_async_copy(wt_ref, wt_out.at[pl.ds(pl.multiple_of(j * wt_rows, 16), wt_rows), :], local_sems.at[0]),
            pltpu.make_async_copy(wo_ref, wo_out.at[pl.ds(pl.multiple_of(j * wo_rows, 16), wo_rows), :], local_sems.at[1]),
            pltpu.make_async_copy(cw_ref, cw_out.at[j], local_sems.at[2]),
        ]
        for cp in local:
            cp.start()

        hop1 = [rcopy(0, own_wt, wt_rows_of(j, c), p1), rcopy(1, own_wo, wo_rows_of(j, c), p1),
                rcopy(2, cw_ref, cw_out.at[j], p1)]
        for cp in hop1:
            cp.start()
        hop2_own = [rcopy(3, own_wt, wt_rows_of(j, c), p2), rcopy(5, own_wo, wo_rows_of(j, c), p2),
                    rcopy(7, cw_ref, cw_out.at[j], p2)]
        for cp in hop2_own:
            cp.start()
        rcopy(0, own_wt, wt_rows_of(j1, c), p1).wait_recv()
        rcopy(1, own_wo, wo_rows_of(j1, c), p1).wait_recv()
        rcopy(2, cw_ref, cw_out.at[j1], p1).wait_recv()
        hop2_fwd = [rcopy(4, wt_rows_of(j1, c), wt_rows_of(j1, c), p2), rcopy(6, wo_rows_of(j1, c), wo_rows_of(j1, c), p2),
                    rcopy(8, cw_out.at[j1], cw_out.at[j1], p2)]
        for cp in hop2_fwd:
            cp.start()
        swap = [rcopy(9, wt_rows_of(j1, c), wt_rows_of(j1, c), sib), rcopy(12, wo_rows_of(j1, c), wo_rows_of(j1, c), sib)]
        for cp in swap:
            cp.start()
        rcopy(3, own_wt, wt_rows_of(j2, c), p2).wait_recv()
        rcopy(5, own_wo, wo_rows_of(j2, c), p2).wait_recv()
        rcopy(7, cw_ref, cw_out.at[j2], p2).wait_recv()
        swap2 = [rcopy(10, wt_rows_of(j2, c), wt_rows_of(j2, c), sib), rcopy(13, wo_rows_of(j2, c), wo_rows_of(j2, c), sib)]
        for cp in swap2:
            cp.start()
        rcopy(4, own_wt, wt_rows_of(j3, c), p2).wait_recv()
        rcopy(6, own_wo, wo_rows_of(j3, c), p2).wait_recv()
        rcopy(8, cw_ref, cw_out.at[j3], p2).wait_recv()
        swap3 = [rcopy(11, wt_rows_of(j3, c), wt_rows_of(j3, c), sib), rcopy(14, wo_rows_of(j3, c), wo_rows_of(j3, c), sib)]
        for cp in swap3:
            cp.start()
        for k, chip in ((9, j2), (10, j1), (11, j3)):
            rcopy(k, own_wt, wt_rows_of(chip, 1 - c), sib).wait_recv()
        for k, chip in ((12, j2), (13, j1), (14, j3)):
            rcopy(k, own_wo, wo_rows_of(chip, 1 - c), sib).wait_recv()
        for cp in hop1 + hop2_own + hop2_fwd + swap + swap2 + swap3:
            cp.wait_send()
        for cp in local:
            cp.wait()

    any_spec = pl.BlockSpec(memory_space=pl.ANY)
    return pl.pallas_call(
        body, name="ag_weights",
        out_shape=(jax.ShapeDtypeStruct((N_CHIPS * wt_rows, wt_sh.shape[1]), wt_sh.dtype),
                   jax.ShapeDtypeStruct((N_CHIPS * wo_rows, wo_sh.shape[1]), wo_sh.dtype),
                   jax.ShapeDtypeStruct((N_CHIPS,) + cw_sh.shape, cw_sh.dtype)),
        in_specs=[any_spec, any_spec, any_spec],
        out_specs=(any_spec, any_spec, any_spec),
        scratch_shapes=[pltpu.SemaphoreType.DMA((15,)), pltpu.SemaphoreType.DMA((15,)), pltpu.SemaphoreType.DMA((3,))],
    )(wt_sh, wo_sh, cw_sh)


def _fwd_in(x, norm_in, wt):
    seq = x.shape[0]
    tile = TOK_TILE

    def body(x_ref, g_ref, wt_ref, h_ref, pc_ref, q_ref, kv_ref, ga_ref):
        xv = x_ref[...]
        h = (xv * _rstd(xv) * g_ref[...]).astype(BF16)
        h_ref[...] = h
        for blk in range(D_PC // D_CONV):
            pc_ref[:, blk * D_CONV:(blk + 1) * D_CONV] = _nt(h, wt_ref[blk * D_CONV:(blk + 1) * D_CONV, :])
        q_ref[...] = _nt(h, wt_ref[ROW_Q:ROW_KV, :])
        kv_ref[...] = _nt(h, wt_ref[ROW_KV:ROW_GA, :])
        ga_ref[...] = _nt(h, wt_ref[ROW_GA:D_IN_PROJ, :])

    def row(width):
        return pl.BlockSpec((tile, width), lambda i: (i, 0))

    return pl.pallas_call(
        body, name="fwd_in", grid=(seq // tile,),
        out_shape=(jax.ShapeDtypeStruct((seq, D_MODEL), BF16), jax.ShapeDtypeStruct((seq, D_PC), F32),
                   jax.ShapeDtypeStruct((seq, D_ATTN), F32), jax.ShapeDtypeStruct((seq, 2 * D_KV), F32),
                   jax.ShapeDtypeStruct((seq, D_ATTN), F32)),
        in_specs=[row(D_MODEL), _resident((1, D_MODEL)), _resident(wt.shape)],
        out_specs=(row(D_MODEL), row(D_PC), row(D_ATTN), row(2 * D_KV), row(D_ATTN)),
        compiler_params=_params(48, ("arbitrary",)),
    )(x, norm_in, wt)


def _conv_core(pc_ref, zbuf, cw_ref):
    tile = pc_ref.shape[0]
    cb = pc_ref[:, 0:D_CONV]
    cc = pc_ref[:, D_CONV:2 * D_CONV]
    cu = pc_ref[:, 2 * D_CONV:3 * D_CONV]
    z = cc * cu
    zbuf[8:tile + 8, :] = z
    z1 = zbuf[7:tile + 7, :]
    z2 = zbuf[6:tile + 6, :]
    conv = cw_ref[0:1, :] * z2 + cw_ref[1:2, :] * z1 + cw_ref[2:3, :] * z
    return cb, cc, cu, z, z1, z2, conv


def _conv_fwd(pc, conv_w, norm_conv_out):
    seq = pc.shape[0]
    tile = TOK_TILE

    def body(pc_ref, cw_ref, gn_ref, oc_ref, zbuf):
        @pl.when(pl.program_id(0) == 0)
        def _():
            zbuf[0:8, :] = jnp.zeros((8, D_CONV), F32)

        cb, _, _, _, _, _, conv = _conv_core(pc_ref, zbuf, cw_ref)
        yc = cb * conv
        silu, _ = _silu_and_grad(pc_ref[:, 3 * D_CONV:4 * D_CONV])
        oc_ref[...] = (yc * _rstd(yc) * gn_ref[...] * silu).astype(BF16)
        zbuf[0:8, :] = zbuf[tile:tile + 8, :]

    return pl.pallas_call(
        body, name="conv_fwd", grid=(seq // tile,),
        out_shape=jax.ShapeDtypeStruct((seq, D_CONV), BF16),
        in_specs=[pl.BlockSpec((tile, D_PC), lambda i: (i, 0)), _resident(conv_w.shape), _resident((1, D_CONV))],
        out_specs=pl.BlockSpec((tile, D_CONV), lambda i: (i, 0)),
        scratch_shapes=[pltpu.VMEM((tile + 8, D_CONV), F32)],
        compiler_params=_params(40, ("arbitrary",)),
    )(pc, conv_w, norm_conv_out)


def _band_masks(block_index):
    qi = lax.broadcasted_iota(jnp.int32, (BLK, 2 * BLK), 0)
    kj = lax.broadcasted_iota(jnp.int32, (BLK, 2 * BLK), 1)
    dist = BLK + qi - kj
    valid = (dist >= 0) & (dist < BLK) & ((kj >= BLK) | (block_index > 0))
    return dist.astype(F32), valid


def _block_diag(band, group):
    lane = lax.broadcasted_iota(jnp.int32, band.shape, 1)
    other = pltpu.roll(band, 64, 1)
    lo, hi = (band, other) if group == 0 else (other, band)
    zero = jnp.zeros_like(band)
    return jnp.concatenate([jnp.where(lane < 64, lo, zero), jnp.where(lane >= 64, hi, zero)], axis=0).astype(BF16)


def _softmax_head(s_raw, head, sink, dist, valid):
    sc = jnp.where(valid, s_raw * SCALE - SLOPES[head] * dist, -jnp.inf)
    m = jnp.maximum(jnp.max(sc, axis=-1, keepdims=True), sink)
    p = jnp.exp(sc - m)
    es = jnp.exp(sink - m)
    inv = 1.0 / (jnp.sum(p, axis=-1, keepdims=True) + es)
    return p * inv, es * inv


def _attn_pair_probs(q_ref, pair, kbd, sink_ref, dist, valid):
    qp = q_ref[:, pair * 128:(pair + 1) * 128].astype(BF16)
    s = _nt(qp, kbd)
    pa, sa = _softmax_head(s[:, 0:2 * BLK], 2 * pair, sink_ref[0, 2 * pair], dist, valid)
    pb, sb = _softmax_head(s[:, 2 * BLK:4 * BLK], 2 * pair + 1, sink_ref[0, 2 * pair + 1], dist, valid)
    return qp, jnp.concatenate([pa, pb], axis=1), (sa, sb)


def _kv_specs(n_blocks, reverse):
    def blk(i):
        return (n_blocks - 1 - i) if reverse else i
    cur = pl.BlockSpec((BLK, 2 * D_KV), lambda i: (blk(i), 0))
    prev = pl.BlockSpec((BLK, 2 * D_KV), lambda i: (jnp.maximum(blk(i) - 1, 0), 0))
    return cur, prev


def _attn_fwd(q, kv, ga, sinks, norm_attn_out):
    seq = q.shape[0]
    n_blocks = seq // BLK

    def body(q_ref, kvc_ref, kvp_ref, ga_ref, sink_ref, gn_ref, ya_ref, oa_ref):
        dist, valid = _band_masks(pl.program_id(0))
        kband = jnp.concatenate([kvp_ref[:, 0:D_KV], kvc_ref[:, 0:D_KV]], axis=0)
        vband = jnp.concatenate([kvp_ref[:, D_KV:2 * D_KV], kvc_ref[:, D_KV:2 * D_KV]], axis=0)
        for group in range(N_HEADS // HEADS_PER_KV):
            kbd = _block_diag(kband, group)
            vbd = _block_diag(vband, group)
            for pair in range(group * 4, group * 4 + 4):
                _, probs, _ = _attn_pair_probs(q_ref, pair, kbd, sink_ref, dist, valid)
                ya_ref[:, pair * 128:(pair + 1) * 128] = _nn(probs.astype(BF16), vbd)
        ya = ya_ref[...]
        silu, _ = _silu_and_grad(ga_ref[...])
        oa_ref[...] = (ya * _rstd(ya) * gn_ref[...] * silu).astype(BF16)

    row = pl.BlockSpec((BLK, D_ATTN), lambda i: (i, 0))
    kv_cur, kv_prev = _kv_specs(n_blocks, reverse=False)
    return pl.pallas_call(
        body, name="attn_fwd", grid=(n_blocks,),
        out_shape=(jax.ShapeDtypeStruct((seq, D_ATTN), F32), jax.ShapeDtypeStruct((seq, D_ATTN), BF16)),
        in_specs=[row, kv_cur, kv_prev, row, pl.BlockSpec(memory_space=pltpu.SMEM), _resident((1, D_ATTN))],
        out_specs=(row, row),
        compiler_params=_params(32, ("arbitrary",)),
    )(q, kv, kv, ga, sinks, norm_attn_out)


def _out_proj_loss(x, oc, oa, wo, norm_final, target):
    seq = x.shape[0]
    tile = TOK_TILE

    def body(x_ref, oc_ref, oa_ref, wo_ref, gf_ref, t_ref, dx2_ref, doc_ref, doa_ref, gwo_ref, gnf_ref, loss_ref):
        @pl.when(pl.program_id(0) == 0)
        def _():
            gwo_ref[...] = jnp.zeros_like(gwo_ref)
            gnf_ref[...] = jnp.zeros_like(gnf_ref)
            loss_ref[...] = jnp.zeros_like(loss_ref)

        oc, oa = oc_ref[...], oa_ref[...]
        x2 = x_ref[...] + _nn(oc, wo_ref[0:D_CONV, :]) + _nn(oa, wo_ref[D_CONV:D_MIX, :])
        r = _rstd(x2)
        xhat = x2 * r
        err = xhat * gf_ref[...] - t_ref[...]
        loss_ref[...] += jnp.sum(err * err, axis=0, keepdims=True) * (0.5 / D_MODEL)
        dy = err * (1.0 / D_MODEL)
        gnf_ref[...] += jnp.sum(dy * xhat, axis=0, keepdims=True)
        dx2 = _rms_bwd(dy * gf_ref[...], xhat, r)
        dx2_ref[...] = dx2
        db = dx2.astype(BF16)
        doc_ref[...] = _nt(db, wo_ref[0:D_CONV, :])
        doa_ref[...] = _nt(db, wo_ref[D_CONV:D_MIX, :])
        gwo_ref[0:D_CONV, :] += _tn(oc, db)
        gwo_ref[D_CONV:D_MIX, :] += _tn(oa, db)

    row = pl.BlockSpec((tile, D_MODEL), lambda i: (i, 0))
    vec = pl.BlockSpec((1, D_MODEL), lambda i: (0, 0))
    return pl.pallas_call(
        body, name="out_proj_loss", grid=(seq // tile,),
        out_shape=(jax.ShapeDtypeStruct((seq, D_MODEL), F32), jax.ShapeDtypeStruct((seq, D_CONV), F32),
                   jax.ShapeDtypeStruct((seq, D_ATTN), F32), jax.ShapeDtypeStruct((D_MIX, D_MODEL), F32),
                   jax.ShapeDtypeStruct((1, D_MODEL), F32), jax.ShapeDtypeStruct((1, D_MODEL), F32)),
        in_specs=[row, row, row, _resident(wo.shape), _resident((1, D_MODEL)), row],
        out_specs=(row, row, row, pl.BlockSpec((D_MIX, D_MODEL), lambda i: (0, 0)), vec, vec),
        compiler_params=_params(48, ("arbitrary",)),
    )(x, oc, oa, wo, norm_final, target)


def _conv_bwd(pc, doc, conv_w, norm_conv_out):
    seq = pc.shape[0]
    tile = TOK_TILE
    n_tiles = seq // tile

    def body(pc_ref, hcc_ref, hcu_ref, doc_ref, cw_ref, gn_ref, dpc_ref, gnc_ref, gcw_ref, zbuf, dbuf):
        step = pl.program_id(0)

        @pl.when(step == 0)
        def _():
            gnc_ref[...] = jnp.zeros_like(gnc_ref)
            gcw_ref[...] = jnp.zeros_like(gcw_ref)
            dbuf[tile:tile + 8, :] = jnp.zeros((8, D_CONV), F32)

        is_first_tile = step == n_tiles - 1
        zbuf[0:8, :] = jnp.where(is_first_tile, 0.0, hcc_ref[...] * hcu_ref[...])
        cb, cc, cu, z, z1, z2, conv = _conv_core(pc_ref, zbuf, cw_ref)
        gc = pc_ref[:, 3 * D_CONV:4 * D_CONV]
        silu, dsilu = _silu_and_grad(gc)
        yc = cb * conv
        r = _rstd(yc)
        xhat = yc * r
        do = doc_ref[...]
        dn = do * silu
        dpc_ref[:, 3 * D_CONV:4 * D_CONV] = (do * (xhat * gn_ref[...]) * dsilu).astype(BF16)
        gnc_ref[...] += jnp.sum(dn * xhat, axis=0, keepdims=True)
        dyc = _rms_bwd(dn * gn_ref[...], xhat, r)
        dpc_ref[:, 0:D_CONV] = (dyc * conv).astype(BF16)
        dconv = dyc * cb
        gcw_ref[0:1, :] += jnp.sum(dconv * z2, axis=0, keepdims=True)
        gcw_ref[1:2, :] += jnp.sum(dconv * z1, axis=0, keepdims=True)
        gcw_ref[2:3, :] += jnp.sum(dconv * z, axis=0, keepdims=True)
        dbuf[0:tile, :] = dconv
        dz = cw_ref[2:3, :] * dconv + cw_ref[1:2, :] * dbuf[1:tile + 1, :] + cw_ref[0:1, :] * dbuf[2:tile + 2, :]
        dpc_ref[:, D_CONV:2 * D_CONV] = (dz * cu).astype(BF16)
        dpc_ref[:, 2 * D_CONV:3 * D_CONV] = (dz * cc).astype(BF16)
        dbuf[tile:tile + 8, :] = dbuf[0:8, :]

    def rev(i):
        return n_tiles - 1 - i

    def halo(col_block):
        return pl.BlockSpec((8, D_CONV), lambda i: (jnp.maximum(rev(i) * (tile // 8) - 1, 0), col_block))

    return pl.pallas_call(
        body, name="conv_bwd", grid=(n_tiles,),
        out_shape=(jax.ShapeDtypeStruct((seq, D_PC), BF16), jax.ShapeDtypeStruct((1, D_CONV), F32),
                   jax.ShapeDtypeStruct((8, D_CONV), F32)),
        in_specs=[pl.BlockSpec((tile, D_PC), lambda i: (rev(i), 0)), halo(1), halo(2),
                  pl.BlockSpec((tile, D_CONV), lambda i: (rev(i), 0)), _resident(conv_w.shape), _resident((1, D_CONV))],
        out_specs=(pl.BlockSpec((tile, D_PC), lambda i: (rev(i), 0)), pl.BlockSpec((1, D_CONV), lambda i: (0, 0)),
                   pl.BlockSpec((8, D_CONV), lambda i: (0, 0))),
        scratch_shapes=[pltpu.VMEM((tile + 8, D_CONV), F32), pltpu.VMEM((tile + 8, D_CONV), F32)],
        compiler_params=_params(48, ("arbitrary",)),
    )(pc, pc, pc, doc, conv_w, norm_conv_out)


def _attn_bwd(q, kv, ga, ya, doa, sinks, norm_attn_out):
    seq = q.shape[0]
    n_blocks = seq // BLK

    def body(q_ref, kvc_ref, kvp_ref, ga_ref, ya_ref, doa_ref, sink_ref, gn_ref,
             dq_ref, dkv_ref, dga_ref, gna_ref, gs_ref, carry, dya_buf):
        step = pl.program_id(0)

        @pl.when(step == 0)
        def _():
            gna_ref[...] = jnp.zeros_like(gna_ref)
            gs_ref[...] = jnp.zeros_like(gs_ref)
            carry[...] = jnp.zeros_like(carry)

        ya = ya_ref[...]
        r = _rstd(ya)
        xhat = ya * r
        silu, dsilu = _silu_and_grad(ga_ref[...])
        do = doa_ref[...]
        dn = do * silu
        dga_ref[...] = (do * (xhat * gn_ref[...]) * dsilu).astype(BF16)
        gna_ref[...] += jnp.sum(dn * xhat, axis=0, keepdims=True)
        dya_buf[...] = _rms_bwd(dn * gn_ref[...], xhat, r).astype(BF16)

        dist, valid = _band_masks(n_blocks - 1 - step)
        kband = jnp.concatenate([kvp_ref[:, 0:D_KV], kvc_ref[:, 0:D_KV]], axis=0)
        vband = jnp.concatenate([kvp_ref[:, D_KV:2 * D_KV], kvc_ref[:, D_KV:2 * D_KV]], axis=0)
        lane = lax.broadcasted_iota(jnp.int32, (2 * BLK, 128), 1)
        sink_lane = lax.broadcasted_iota(jnp.int32, (1, 128), 1)
        gsink = jnp.zeros((1, 128), F32)
        dk_groups, dv_groups = [], []
        for group in range(N_HEADS // HEADS_PER_KV):
            kbd = _block_diag(kband, group)
            vbd = _block_diag(vband, group)
            acc_k = jnp.zeros((2 * BLK, 128), F32)
            acc_v = jnp.zeros((2 * BLK, 128), F32)
            for pair in range(group * 4, group * 4 + 4):
                qp, probs, sink_probs = _attn_pair_probs(q_ref, pair, kbd, sink_ref, dist, valid)
                dyp = dya_buf[:, pair * 128:(pair + 1) * 128]
                dp = _nt(dyp, vbd)
                ds_halves = []
                for u in range(2):
                    p_u = probs[:, u * 2 * BLK:(u + 1) * 2 * BLK]
                    dp_u = dp[:, u * 2 * BLK:(u + 1) * 2 * BLK]
                    delta = jnp.sum(p_u * dp_u, axis=-1, keepdims=True)
                    ds_halves.append(p_u * (dp_u - delta) * SCALE)
                    dsink = -jnp.sum(sink_probs[u] * delta, axis=0, keepdims=True)
                    gsink = gsink + jnp.where(sink_lane == 2 * pair + u, dsink, 0.0)
                ds = jnp.concatenate(ds_halves, axis=1).astype(BF16)
                dq_ref[:, pair * 128:(pair + 1) * 128] = _nn(ds, kbd).astype(BF16)
                dkbd = _tn(ds, qp)
                dvbd = _tn(probs.astype(BF16), dyp)
                acc_k += jnp.where(lane < 64, dkbd[0:2 * BLK], 0.0) + jnp.where(lane >= 64, dkbd[2 * BLK:4 * BLK], 0.0)
                acc_v += jnp.where(lane < 64, dvbd[0:2 * BLK], 0.0) + jnp.where(lane >= 64, dvbd[2 * BLK:4 * BLK], 0.0)
            dk_groups.append(acc_k + pltpu.roll(acc_k, 64, 1))
            dv_groups.append(acc_v + pltpu.roll(acc_v, 64, 1))
        gs_ref[...] += gsink
        dkv_band = jnp.concatenate([jnp.where(lane < 64, dk_groups[0], dk_groups[1]),
                                    jnp.where(lane < 64, dv_groups[0], dv_groups[1])], axis=1)
        dkv_ref[...] = (dkv_band[BLK:2 * BLK] + carry[...]).astype(BF16)
        carry[...] = dkv_band[0:BLK]

    row = pl.BlockSpec((BLK, D_ATTN), lambda i: (n_blocks - 1 - i, 0))
    kv_cur, kv_prev = _kv_specs(n_blocks, reverse=True)
    return pl.pallas_call(
        body, name="attn_bwd", grid=(n_blocks,),
        out_shape=(jax.ShapeDtypeStruct((seq, D_ATTN), BF16), jax.ShapeDtypeStruct((seq, 2 * D_KV), BF16),
                   jax.ShapeDtypeStruct((seq, D_ATTN), BF16), jax.ShapeDtypeStruct((1, D_ATTN), F32),
                   jax.ShapeDtypeStruct((1, 128), F32)),
        in_specs=[row, kv_cur, kv_prev, row, row, row, pl.BlockSpec(memory_space=pltpu.SMEM), _resident((1, D_ATTN))],
        out_specs=(row, kv_cur, row, pl.BlockSpec((1, D_ATTN), lambda i: (0, 0)), pl.BlockSpec((1, 128), lambda i: (0, 0))),
        scratch_shapes=[pltpu.VMEM((BLK, 2 * D_KV), F32), pltpu.VMEM((BLK, D_ATTN), BF16)],
        compiler_params=_params(32, ("arbitrary",)),
    )(q, kv, kv, ga, ya, doa, sinks, norm_attn_out)


def _dw_in(dpc, dq, dkv, dga, h):
    seq = h.shape[0]
    rows = 256
    first = (0, ROW_Q // rows, ROW_KV // rows, ROW_GA // rows, D_IN_PROJ // rows)

    def body(dpc_ref, dq_ref, dkv_ref, dga_ref, h_ref, out_ref):
        i = pl.program_id(0)
        for piece, ref in enumerate((dpc_ref, dq_ref, dkv_ref, dga_ref)):
            @pl.when((i >= first[piece]) & (i < first[piece + 1]))
            def _(ref=ref):
                out_ref[...] = _tn(ref[...], h_ref[...])

    def cols(piece):
        n = first[piece + 1] - first[piece]
        return pl.BlockSpec((seq, rows), lambda i: (0, jnp.clip(i - first[piece], 0, n - 1)))

    return pl.pallas_call(
        body, name="dw_in", grid=(D_IN_PROJ // rows,),
        out_shape=jax.ShapeDtypeStruct((D_IN_PROJ, D_MODEL), F32),
        in_specs=[cols(0), cols(1), cols(2), cols(3), _resident(h.shape)],
        out_specs=pl.BlockSpec((rows, D_MODEL), lambda i: (i, 0)),
        compiler_params=_params(32, ("arbitrary",)),
    )(dpc, dq, dkv, dga, h)


def _dh_gradx(dpc, dq, dkv, dga, wt, x, norm_in, dx2):
    seq = x.shape[0]
    tile = TOK_TILE

    def body(dpc_ref, dq_ref, dkv_ref, dga_ref, wt_ref, x_ref, g_ref, dx2_ref, gx_ref, gni_ref):
        @pl.when(pl.program_id(0) == 0)
        def _():
            gni_ref[...] = jnp.zeros_like(gni_ref)

        dh = (_nn(dpc_ref[...], wt_ref[0:ROW_Q, :]) + _nn(dq_ref[...], wt_ref[ROW_Q:ROW_KV, :])
              + _nn(dkv_ref[...], wt_ref[ROW_KV:ROW_GA, :]) + _nn(dga_ref[...], wt_ref[ROW_GA:D_IN_PROJ, :]))
        xv = x_ref[...]
        r = _rstd(xv)
        xhat = xv * r
        gni_ref[...] += jnp.sum(dh * xhat, axis=0, keepdims=True)
        gx_ref[...] = _rms_bwd(dh * g_ref[...], xhat, r) + dx2_ref[...]

    def row(width):
        return pl.BlockSpec((tile, width), lambda i: (i, 0))

    return pl.pallas_call(
        body, name="dh_gradx", grid=(seq // tile,),
        out_shape=(jax.ShapeDtypeStruct((seq, D_MODEL), F32), jax.ShapeDtypeStruct((1, D_MODEL), F32)),
        in_specs=[row(D_PC), row(D_ATTN), row(2 * D_KV), row(D_ATTN), _resident(wt.shape), row(D_MODEL),
                  _resident((1, D_MODEL)), row(D_MODEL)],
        out_specs=(row(D_MODEL), pl.BlockSpec((1, D_MODEL), lambda i: (0, 0))),
        compiler_params=_params(48, ("arbitrary",)),
    )(dpc, dq, dkv, dga, wt, x, norm_in, dx2)


def _accumulate(dst_ref, src_ref, rows=16):
    def step(i, carry):
        sl = pl.ds(pl.multiple_of(i * rows, rows), rows)
        dst_ref[sl, :] = dst_ref[sl, :] + src_ref[sl, :]
        return carry
    lax.fori_loop(0, dst_ref.shape[0] // rows, step, 0)


def _rs_grads(gwt, gwo, small):
    t_rows, o_rows = gwt.shape[0] // N_CHIPS, gwo.shape[0] // N_CHIPS
    t_half, o_half = t_rows // 2, o_rows // 2
    width = gwt.shape[1]

    def body(gwt_ref, gwo_ref, small_ref, gwt_sh, gwo_sh, small_sum, r1t, r1o, r2t, r2o, r3t, r3o,
             acc_t, acc_o, stage_t, stage_o, small_land, send_sems, recv_sems, local_sems):
        x, y, c = lax.axis_index("x"), lax.axis_index("y"), lax.axis_index("c")
        me = 4 * x + 2 * y + c
        j = 2 * x + y
        pa = (_xor(x, 1 - c), _xor(y, c), c)
        pb = (_xor(x, c), _xor(y, 1 - c), c)
        sib = (x, y, 1 - c)
        ja = 2 * pa[0] + pa[1]
        jb = 2 * pb[0] + pb[1]
        jd = 3 - j

        def rcopy(k, src, dst, to):
            return pltpu.make_async_remote_copy(src_ref=src, dst_ref=dst, send_sem=send_sems.at[k],
                                                recv_sem=recv_sems.at[k], device_id=to, device_id_type=MESH)

        def t_rows_of(chip, half):
            return gwt_ref.at[pl.ds(pl.multiple_of(chip * t_rows + half * t_half, 8), t_half), :]

        def o_rows_of(chip, half):
            return gwo_ref.at[pl.ds(pl.multiple_of(chip * o_rows + half * o_half, 8), o_half), :]

        small_land[me] = small_ref[...]
        others = [(dx, dy, dc) for dx in (0, 1) for dy in (0, 1) for dc in (0, 1)][1:]
        small_sends = [rcopy(k, small_ref, small_land.at[me], (_xor(x, dx), _xor(y, dy), _xor(c, dc)))
                       for k, (dx, dy, dc) in enumerate(others)]
        for cp in small_sends:
            cp.start()

        order = (ja, jd, jb, j)
        sib_order = (jb, jd, ja, j)
        pair_sends = []
        for s, chip in enumerate(sib_order):
            pair_sends += [rcopy(7 + s, t_rows_of(chip, 1 - c), r1t.at[s], sib),
                           rcopy(11 + s, o_rows_of(chip, 1 - c), r1o.at[s], sib)]
        for cp in pair_sends:
            cp.start()

        def load(src, dst, k):
            cp = pltpu.make_async_copy(src, dst, local_sems.at[k])
            cp.start()
            return cp

        def add_landed(s, mine_t, mine_o, land_t, land_o):
            cps = [load(mine_t, acc_t.at[s], 0), load(mine_o, acc_o.at[s], 1),
                   load(land_t, stage_t, 2), load(land_o, stage_o, 3)]
            for cp in cps:
                cp.wait()
            _accumulate(acc_t.at[s], stage_t)
            _accumulate(acc_o.at[s], stage_o)

        hop1 = []
        for s, chip in enumerate(order):
            rcopy(7 + s, t_rows_of(chip, c), r1t.at[s], sib).wait_recv()
            rcopy(11 + s, o_rows_of(chip, c), r1o.at[s], sib).wait_recv()
            add_landed(s, t_rows_of(chip, c), o_rows_of(chip, c), r1t.at[s], r1o.at[s])
            if s < 2:
                hop1 += [rcopy(15 + s, acc_t.at[s], r2t.at[s], pa), rcopy(17 + s, acc_o.at[s], r2o.at[s], pa)]
                hop1[-2].start()
                hop1[-1].start()

        def add_hop(s, land_t, land_o):
            cps = [load(land_t, stage_t, 2), load(land_o, stage_o, 3)]
            for cp in cps:
                cp.wait()
            _accumulate(acc_t.at[s], stage_t)
            _accumulate(acc_o.at[s], stage_o)

        rcopy(16, acc_t.at[1], r2t.at[1], pa).wait_recv()
        rcopy(18, acc_o.at[1], r2o.at[1], pa).wait_recv()
        add_hop(2, r2t.at[1], r2o.at[1])
        hop2 = [rcopy(19, acc_t.at[2], r3t, pb), rcopy(20, acc_o.at[2], r3o, pb)]
        for cp in hop2:
            cp.start()
        rcopy(15, acc_t.at[0], r2t.at[0], pa).wait_recv()
        rcopy(17, acc_o.at[0], r2o.at[0], pa).wait_recv()
        add_hop(3, r2t.at[0], r2o.at[0])
        rcopy(19, acc_t.at[2], r3t, pb).wait_recv()
        rcopy(20, acc_o.at[2], r3o, pb).wait_recv()
        add_hop(3, r3t, r3o)

        out_t = gwt_sh.at[pl.ds(pl.multiple_of(c * t_half, 8), t_half), :]
        out_o = gwo_sh.at[pl.ds(pl.multiple_of(c * o_half, 8), o_half), :]
        keep = [load(acc_t.at[3], out_t, 0), load(acc_o.at[3], out_o, 1)]
        swap = [rcopy(21, acc_t.at[3], out_t, sib), rcopy(22, acc_o.at[3], out_o, sib)]
        for cp in swap:
            cp.start()

        for cp in small_sends:
            cp.wait_recv()
        total = small_land[0]
        for dev in range(1, 8):
            total = total + small_land[dev]
        small_sum[...] = total

        rcopy(21, acc_t.at[3], gwt_sh.at[pl.ds(pl.multiple_of((1 - c) * t_half, 8), t_half), :], sib).wait_recv()
        rcopy(22, acc_o.at[3], gwo_sh.at[pl.ds(pl.multiple_of((1 - c) * o_half, 8), o_half), :], sib).wait_recv()
        for cp in small_sends + pair_sends + hop1 + hop2 + swap:
            cp.wait_send()
        for cp in keep:
            cp.wait()

    any_spec = pl.BlockSpec(memory_space=pl.ANY)
    vmem_spec = pl.BlockSpec(memory_space=pltpu.VMEM)
    outs = pl.pallas_call(
        body, name="rs_grads",
        out_shape=(jax.ShapeDtypeStruct((t_rows, width), F32), jax.ShapeDtypeStruct((o_rows, width), F32),
                   jax.ShapeDtypeStruct(small.shape, F32),
                   jax.ShapeDtypeStruct((4, t_half, width), F32), jax.ShapeDtypeStruct((4, o_half, width), F32),
                   jax.ShapeDtypeStruct((2, t_half, width), F32), jax.ShapeDtypeStruct((2, o_half, width), F32),
                   jax.ShapeDtypeStruct((t_half, width), F32), jax.ShapeDtypeStruct((o_half, width), F32)),
        in_specs=[any_spec, any_spec, vmem_spec],
        out_specs=(any_spec, any_spec, vmem_spec) + (any_spec,) * 6,
        scratch_shapes=[pltpu.VMEM((4, t_half, width), F32), pltpu.VMEM((4, o_half, width), F32),
                        pltpu.VMEM((t_half, width), F32), pltpu.VMEM((o_half, width), F32),
                        pltpu.VMEM((8,) + small.shape, F32),
                        pltpu.SemaphoreType.DMA((23,)), pltpu.SemaphoreType.DMA((23,)), pltpu.SemaphoreType.DMA((4,))],
        compiler_params=_params(40),
    )(gwt, gwo, small)
    return outs[0], outs[1], outs[2]


def _adamw(name, w, g, m, v, rows):
    def body(w_ref, g_ref, m_ref, v_ref, d_ref, nm_ref, nv_ref):
        gv = g_ref[...]
        nm = ADAM_B1 * m_ref[...] + (1.0 - ADAM_B1) * gv
        nv = ADAM_B2 * v_ref[...] + (1.0 - ADAM_B2) * (gv * gv)
        m_hat = nm / (1.0 - ADAM_B1 ** ADAM_STEP)
        v_hat = nv / (1.0 - ADAM_B2 ** ADAM_STEP)
        d_ref[...] = -ADAM_LR * (m_hat / (jnp.sqrt(v_hat) + ADAM_EPS) + ADAM_WD * w_ref[...])
        nm_ref[...] = nm
        nv_ref[...] = nv

    spec = pl.BlockSpec((rows, w.shape[1]), lambda i: (i, 0))
    shape = jax.ShapeDtypeStruct(w.shape, F32)
    return pl.pallas_call(
        body, name="adamw_" + name, grid=(w.shape[0] // rows,),
        out_shape=(shape, shape, shape), in_specs=[spec] * 4, out_specs=(spec,) * 3,
        compiler_params=_params(32, ("arbitrary",)),
    )(w, g, m, v)


def kernel(x, norm_in, w_in, conv_w, attn_sinks, norm_conv_out, norm_attn_out, w_out, norm_final, loss_target, m_norm_in, m_w_in, m_conv_w, m_attn_sinks, m_norm_conv_out, m_norm_attn_out, m_w_out, m_norm_final, v_norm_in, v_w_in, v_conv_w, v_attn_sinks, v_norm_conv_out, v_norm_attn_out, v_w_out, v_norm_final):
    chip = 2 * lax.axis_index("x") + lax.axis_index("y")
    xs, target = x[0], loss_target[0]
    norm_final2 = norm_final.reshape(1, D_MODEL)

    wt, wo, cw4 = _ag_weights(w_in[0].T.astype(BF16), w_out[0].astype(BF16), conv_w[0])
    cw = jnp.transpose(cw4, (1, 0, 2)).reshape(3, D_CONV)

    h, pc, q, kv, ga = _fwd_in(xs, norm_in, wt)
    oc = _conv_fwd(pc, cw, norm_conv_out)
    ya, oa = _attn_fwd(q, kv, ga, attn_sinks, norm_attn_out)
    dx2, doc, doa, gwo, gnf, loss_lanes = _out_proj_loss(xs, oc, oa, wo, norm_final2, target)

    dpc, gnc, gcw = _conv_bwd(pc, doc, cw, norm_conv_out)
    dq, dkv, dga, gna, gsink = _attn_bwd(q, kv, ga, ya, doa, attn_sinks, norm_attn_out)
    gwt = _dw_in(dpc, dq, dkv, dga, h)
    grad_x, gni = _dh_gradx(dpc, dq, dkv, dga, wt, xs, norm_in, dx2)

    last = jnp.zeros((1, D_MODEL), F32).at[:, 0:N_HEADS].set(gsink[:, 0:N_HEADS]).at[0, N_HEADS].set(jnp.sum(loss_lanes))
    small = jnp.concatenate([gni, gnc, gna, gnf, gcw[0:3], last], axis=0)
    gwt_sh, gwo_sh, small_sum = _rs_grads(gwt, gwo, small)

    loss = small_sum[7, N_HEADS]
    g_norm_in, g_norm_conv, g_norm_attn = small_sum[0:1], small_sum[1:2], small_sum[2:3]
    g_norm_final = small_sum[3]
    g_conv_w = lax.dynamic_slice(small_sum[4:7], (0, chip * (D_CONV // N_CHIPS)), (3, D_CONV // N_CHIPS))[None]
    g_sinks = small_sum[7:8, 0:N_HEADS]
    g_w_in = gwt_sh.T[None]
    g_w_out = gwo_sh[None]

    def adam(name, w, g, m, v, rows):
        shape = w.shape
        two_d = (-1, shape[-1])
        out = _adamw(name, w.reshape(two_d), g.reshape(two_d), m.reshape(two_d), v.reshape(two_d), rows)
        return tuple(o.reshape(shape) for o in out)

    weights = (norm_in, w_in, conv_w, attn_sinks, norm_conv_out, norm_attn_out, w_out, norm_final)
    grads = (g_norm_in, g_w_in, g_conv_w, g_sinks, g_norm_conv, g_norm_attn, g_w_out, g_norm_final)
    moments_m = (m_norm_in, m_w_in, m_conv_w, m_attn_sinks, m_norm_conv_out, m_norm_attn_out, m_w_out, m_norm_final)
    moments_v = (v_norm_in, v_w_in, v_conv_w, v_attn_sinks, v_norm_conv_out, v_norm_attn_out, v_w_out, v_norm_final)
    rows = (1, 128, 3, 1, 1, 1, 128, 1)
    names = ("norm_in", "w_in", "conv_w", "attn_sinks", "norm_conv_out", "norm_attn_out", "w_out", "norm_final")
    updates = [adam(*args) for args in zip(names, weights, grads, moments_m, moments_v, rows)]
    deltas, new_m, new_v = zip(*updates)
    return (loss, grad_x[None], *grads, *deltas, *new_m, *new_v)
```

```python
import jax
import jax.numpy as jnp
from jax import lax
from jax.experimental import pallas as pl
from jax.experimental.pallas import tpu as pltpu

F32 = jnp.float32
BF16 = jnp.bfloat16
MESH = pl.DeviceIdType.MESH

D_MODEL = 1024
D_CONV = 1024
D_ATTN = 1024
D_KV = 128
D_MIX = D_CONV + D_ATTN
D_PC = 4 * D_CONV
D_IN_PROJ = D_PC + 2 * D_ATTN + 2 * D_KV
ROW_Q = D_PC
ROW_KV = ROW_Q + D_ATTN
ROW_GA = ROW_KV + 2 * D_KV
N_HEADS = 16
HEAD_DIM = 64
HEADS_PER_KV = 8
BLK = 128
N_CHIPS = 4
RMS_EPS = 1e-5
SCALE = HEAD_DIM ** -0.5
SLOPES = tuple(2.0 ** (-8.0 * (h + 1) / N_HEADS) for h in range(N_HEADS))

ADAM_LR, ADAM_B1, ADAM_B2, ADAM_EPS, ADAM_WD, ADAM_STEP = 0.001, 0.9, 0.999, 1e-08, 0.01, 10

TOK_TILE = 256
MIB = 1 << 20


def _params(vmem_mib, semantics=None):
    return pltpu.CompilerParams(dimension_semantics=semantics, vmem_limit_bytes=vmem_mib * MIB)


def _nn(a, b):
    return jnp.dot(a, b, preferred_element_type=F32)


def _nt(a, b):
    return lax.dot_general(a, b, (((1,), (1,)), ((), ())), preferred_element_type=F32)


def _tn(a, b):
    return lax.dot_general(a, b, (((0,), (0,)), ((), ())), preferred_element_type=F32)


def _rstd(v):
    return lax.rsqrt(jnp.mean(v * v, axis=-1, keepdims=True) + RMS_EPS)


def _rms_bwd(g, xhat, rstd):
    return rstd * (g - xhat * jnp.mean(g * xhat, axis=-1, keepdims=True))


def _silu_and_grad(g):
    s = jax.nn.sigmoid(g)
    return g * s, s * (1.0 + g * (1.0 - s))


def _resident(shape):
    return pl.BlockSpec(shape, lambda *_: (0,) * len(shape), pipeline_mode=pl.Buffered(1))


def _xor(a, b):
    return a + b - 2 * a * b


def _ag_weights(wt_sh, wo_sh, cw_sh):
    wt_rows, wo_rows = wt_sh.shape[0], wo_sh.shape[0]
    wt_half, wo_half = wt_rows // 2, wo_rows // 2

    def body(wt_ref, wo_ref, cw_ref, wt_out, wo_out, cw_out, send_sems, recv_sems, local_sems):
        x, y, c = lax.axis_index("x"), lax.axis_index("y"), lax.axis_index("c")
        j = 2 * x + y
        p1 = (_xor(x, c), _xor(y, 1 - c), c)
        p2 = (_xor(x, 1 - c), _xor(y, c), c)
        sib = (x, y, 1 - c)
        j1 = 2 * p1[0] + p1[1]
        j2 = 2 * p2[0] + p2[1]
        j3 = 3 - j

        def wt_rows_of(chip, half):
            return wt_out.at[pl.ds(pl.multiple_of(chip * wt_rows + half * wt_half, 16), wt_half), :]

        def wo_rows_of(chip, half):
            return wo_out.at[pl.ds(pl.multiple_of(chip * wo_rows + half * wo_half, 16), wo_half), :]

        def rcopy(k, src, dst, to):
            return pltpu.make_async_remote_copy(src_ref=src, dst_ref=dst, send_sem=send_sems.at[k],
                                                recv_sem=recv_sems.at[k], device_id=to, device_id_type=MESH)

        own_wt = wt_ref.at[pl.ds(pl.multiple_of(c * wt_half, 16), wt_half), :]
        own_wo = wo_ref.at[pl.ds(pl.multiple_of(c * wo_half, 16), wo_half), :]

        local = [
            pltpu.make_async_copy(wt_ref, wt_out.at[pl.ds(pl.multiple_of(j * wt_rows, 16), wt_rows), :], local_sems.at[0]),
            pltpu.make_async_copy(wo_ref, wo_out.at[pl.ds(pl.multiple_of(j * wo_rows, 16), wo_rows), :], local_sems.at[1]),
            pltpu.make_async_copy(cw_ref, cw_out.at[j], local_sems.at[2]),
        ]
        for cp in local:
            cp.start()

        hop1 = [rcopy(0, own_wt, wt_rows_of(j, c), p1), rcopy(1, own_wo, wo_rows_of(j, c), p1),
                rcopy(2, cw_ref, cw_out.at[j], p1)]
        for cp in hop1:
            cp.start()
        rcopy(0, own_wt, wt_rows_of(j1, c), p1).wait_recv()
        rcopy(1, own_wo, wo_rows_of(j1, c), p1).wait_recv()
        rcopy(2, cw_ref, cw_out.at[j1], p1).wait_recv()
        hop2_own = [rcopy(3, own_wt, wt_rows_of(j, c), p2), rcopy(5, own_wo, wo_rows_of(j, c), p2),
                    rcopy(7, cw_ref, cw_out.at[j], p2)]
        for cp in hop2_own:
            cp.start()
        hop2_fwd = [rcopy(4, wt_rows_of(j1, c), wt_rows_of(j1, c), p2), rcopy(6, wo_rows_of(j1, c), wo_rows_of(j1, c), p2),
                    rcopy(8, cw_out.at[j1], cw_out.at[j1], p2)]
        for cp in hop2_fwd:
            cp.start()
        swap = [rcopy(9, wt_rows_of(j1, c), wt_rows_of(j1, c), sib), rcopy(12, wo_rows_of(j1, c), wo_rows_of(j1, c), sib)]
        for cp in swap:
            cp.start()
        rcopy(3, own_wt, wt_rows_of(j2, c), p2).wait_recv()
        rcopy(5, own_wo, wo_rows_of(j2, c), p2).wait_recv()
        rcopy(7, cw_ref, cw_out.at[j2], p2).wait_recv()
        swap2 = [rcopy(10, wt_rows_of(j2, c), wt_rows_of(j2, c), sib), rcopy(13, wo_rows_of(j2, c), wo_rows_of(j2, c), sib)]
        for cp in swap2:
            cp.start()
        rcopy(4, own_wt, wt_rows_of(j3, c), p2).wait_recv()
        rcopy(6, own_wo, wo_rows_of(j3, c), p2).wait_recv()
        rcopy(8, cw_ref, cw_out.at[j3], p2).wait_recv()
        swap3 = [rcopy(11, wt_rows_of(j3, c), wt_rows_of(j3, c), sib), rcopy(14, wo_rows_of(j3, c), wo_rows_of(j3, c), sib)]
        for cp in swap3:
            cp.start()
        for k, chip in ((9, j2), (10, j1), (11, j3)):
            rcopy(k, own_wt, wt_rows_of(chip, 1 - c), sib).wait_recv()
        for k, chip in ((12, j2), (13, j1), (14, j3)):
            rcopy(k, own_wo, wo_rows_of(chip, 1 - c), sib).wait_recv()
        for cp in hop1 + hop2_own + hop2_fwd + swap + swap2 + swap3:
            cp.wait_send()
        for cp in local:
            cp.wait()

    any_spec = pl.BlockSpec(memory_space=pl.ANY)
    return pl.pallas_call(
        body, name="ag_weights",
        out_shape=(jax.ShapeDtypeStruct((N_CHIPS * wt_rows, wt_sh.shape[1]), wt_sh.dtype),
                   jax.ShapeDtypeStruct((N_CHIPS * wo_rows, wo_sh.shape[1]), wo_sh.dtype),
                   jax.ShapeDtypeStruct((N_CHIPS,) + cw_sh.shape, cw_sh.dtype)),
        in_specs=[any_spec, any_spec, any_spec],
        out_specs=(any_spec, any_spec, any_spec),
        scratch_shapes=[pltpu.SemaphoreType.DMA((15,)), pltpu.SemaphoreType.DMA((15,)), pltpu.SemaphoreType.DMA((3,))],
    )(wt_sh, wo_sh, cw_sh)


def _fwd_in(x, norm_in, wt):
    seq = x.shape[0]
    tile = TOK_TILE

    def body(x_ref, g_ref, wt_ref, h_ref, pc_ref, q_ref, kv_ref, ga_ref):
        xv = x_ref[...]
        h = (xv * _rstd(xv) * g_ref[...]).astype(BF16)
        h_ref[...] = h
        for blk in range(D_PC // D_CONV):
            pc_ref[:, blk * D_CONV:(blk + 1) * D_CONV] = _nt(h, wt_ref[blk * D_CONV:(blk + 1) * D_CONV, :])
        q_ref[...] = _nt(h, wt_ref[ROW_Q:ROW_KV, :])
        kv_ref[...] = _nt(h, wt_ref[ROW_KV:ROW_GA, :])
        ga_ref[...] = _nt(h, wt_ref[ROW_GA:D_IN_PROJ, :])

    def row(width):
        return pl.BlockSpec((tile, width), lambda i: (i, 0))

    return pl.pallas_call(
        body, name="fwd_in", grid=(seq // tile,),
        out_shape=(jax.ShapeDtypeStruct((seq, D_MODEL), BF16), jax.ShapeDtypeStruct((seq, D_PC), F32),
                   jax.ShapeDtypeStruct((seq, D_ATTN), F32), jax.ShapeDtypeStruct((seq, 2 * D_KV), F32),
                   jax.ShapeDtypeStruct((seq, D_ATTN), F32)),
        in_specs=[row(D_MODEL), _resident((1, D_MODEL)), _resident(wt.shape)],
        out_specs=(row(D_MODEL), row(D_PC), row(D_ATTN), row(2 * D_KV), row(D_ATTN)),
        compiler_params=_params(48, ("arbitrary",)),
    )(x, norm_in, wt)


def _conv_core(pc_ref, zbuf, cw_ref):
    tile = pc_ref.shape[0]
    cb = pc_ref[:, 0:D_CONV]
    cc = pc_ref[:, D_CONV:2 * D_CONV]
    cu = pc_ref[:, 2 * D_CONV:3 * D_CONV]
    z = cc * cu
    zbuf[8:tile + 8, :] = z
    z1 = zbuf[7:tile + 7, :]
    z2 = zbuf[6:tile + 6, :]
    conv = cw_ref[0:1, :] * z2 + cw_ref[1:2, :] * z1 + cw_ref[2:3, :] * z
    return cb, cc, cu, z, z1, z2, conv


def _conv_fwd(pc, conv_w, norm_conv_out):
    seq = pc.shape[0]
    tile = TOK_TILE

    def body(pc_ref, cw_ref, gn_ref, oc_ref, zbuf):
        @pl.when(pl.program_id(0) == 0)
        def _():
            zbuf[0:8, :] = jnp.zeros((8, D_CONV), F32)

        cb, _, _, _, _, _, conv = _conv_core(pc_ref, zbuf, cw_ref)
        yc = cb * conv
        silu, _ = _silu_and_grad(pc_ref[:, 3 * D_CONV:4 * D_CONV])
        oc_ref[...] = (yc * _rstd(yc) * gn_ref[...] * silu).astype(BF16)
        zbuf[0:8, :] = zbuf[tile:tile + 8, :]

    return pl.pallas_call(
        body, name="conv_fwd", grid=(seq // tile,),
        out_shape=jax.ShapeDtypeStruct((seq, D_CONV), BF16),
        in_specs=[pl.BlockSpec((tile, D_PC), lambda i: (i, 0)), _resident(conv_w.shape), _resident((1, D_CONV))],
        out_specs=pl.BlockSpec((tile, D_CONV), lambda i: (i, 0)),
        scratch_shapes=[pltpu.VMEM((tile + 8, D_CONV), F32)],
        compiler_params=_params(40, ("arbitrary",)),
    )(pc, conv_w, norm_conv_out)


def _band_masks(block_index):
    qi = lax.broadcasted_iota(jnp.int32, (BLK, 2 * BLK), 0)
    kj = lax.broadcasted_iota(jnp.int32, (BLK, 2 * BLK), 1)
    dist = BLK + qi - kj
    valid = (dist >= 0) & (dist < BLK) & ((kj >= BLK) | (block_index > 0))
    return dist.astype(F32), valid


def _block_diag(band, group):
    lane = lax.broadcasted_iota(jnp.int32, band.shape, 1)
    other = pltpu.roll(band, 64, 1)
    lo, hi = (band, other) if group == 0 else (other, band)
    zero = jnp.zeros_like(band)
    return jnp.concatenate([jnp.where(lane < 64, lo, zero), jnp.where(lane >= 64, hi, zero)], axis=0).astype(BF16)


def _softmax_head(s_raw, head, sink, dist, valid):
    sc = jnp.where(valid, s_raw * SCALE - SLOPES[head] * dist, -jnp.inf)
    m = jnp.maximum(jnp.max(sc, axis=-1, keepdims=True), sink)
    p = jnp.exp(sc - m)
    es = jnp.exp(sink - m)
    inv = 1.0 / (jnp.sum(p, axis=-1, keepdims=True) + es)
    return p * inv, es * inv


def _attn_pair_probs(q_ref, pair, kbd, sink_ref, dist, valid):
    qp = q_ref[:, pair * 128:(pair + 1) * 128].astype(BF16)
    s = _nt(qp, kbd)
    pa, sa = _softmax_head(s[:, 0:2 * BLK], 2 * pair, sink_ref[0, 2 * pair], dist, valid)
    pb, sb = _softmax_head(s[:, 2 * BLK:4 * BLK], 2 * pair + 1, sink_ref[0, 2 * pair + 1], dist, valid)
    return qp, jnp.concatenate([pa, pb], axis=1), (sa, sb)


def _kv_specs(n_blocks, reverse):
    def blk(i):
        return (n_blocks - 1 - i) if reverse else i
    cur = pl.BlockSpec((BLK, 2 * D_KV), lambda i: (blk(i), 0))
    prev = pl.BlockSpec((BLK, 2 * D_KV), lambda i: (jnp.maximum(blk(i) - 1, 0), 0))
    return cur, prev


def _attn_fwd(q, kv, ga, sinks, norm_attn_out):
    seq = q.shape[0]
    n_blocks = seq // BLK

    def body(q_ref, kvc_ref, kvp_ref, ga_ref, sink_ref, gn_ref, ya_ref, oa_ref):
        dist, valid = _band_masks(pl.program_id(0))
        kband = jnp.concatenate([kvp_ref[:, 0:D_KV], kvc_ref[:, 0:D_KV]], axis=0)
        vband = jnp.concatenate([kvp_ref[:, D_KV:2 * D_KV], kvc_ref[:, D_KV:2 * D_KV]], axis=0)
        for group in range(N_HEADS // HEADS_PER_KV):
            kbd = _block_diag(kband, group)
            vbd = _block_diag(vband, group)
            for pair in range(group * 4, group * 4 + 4):
                _, probs, _ = _attn_pair_probs(q_ref, pair, kbd, sink_ref, dist, valid)
                ya_ref[:, pair * 128:(pair + 1) * 128] = _nn(probs.astype(BF16), vbd)
        ya = ya_ref[...]
        silu, _ = _silu_and_grad(ga_ref[...])
        oa_ref[...] = (ya * _rstd(ya) * gn_ref[...] * silu).astype(BF16)

    row = pl.BlockSpec((BLK, D_ATTN), lambda i: (i, 0))
    kv_cur, kv_prev = _kv_specs(n_blocks, reverse=False)
    return pl.pallas_call(
        body, name="attn_fwd", grid=(n_blocks,),
        out_shape=(jax.ShapeDtypeStruct((seq, D_ATTN), F32), jax.ShapeDtypeStruct((seq, D_ATTN), BF16)),
        in_specs=[row, kv_cur, kv_prev, row, pl.BlockSpec(memory_space=pltpu.SMEM), _resident((1, D_ATTN))],
        out_specs=(row, row),
        compiler_params=_params(32, ("arbitrary",)),
    )(q, kv, kv, ga, sinks, norm_attn_out)


def _out_proj_loss(x, oc, oa, wo, norm_final, target):
    seq = x.shape[0]
    tile = TOK_TILE

    def body(x_ref, oc_ref, oa_ref, wo_ref, gf_ref, t_ref, dx2_ref, doc_ref, doa_ref, gwo_ref, gnf_ref, loss_ref):
        @pl.when(pl.program_id(0) == 0)
        def _():
            gwo_ref[...] = jnp.zeros_like(gwo_ref)
            gnf_ref[...] = jnp.zeros_like(gnf_ref)
            loss_ref[...] = jnp.zeros_like(loss_ref)

        oc, oa = oc_ref[...], oa_ref[...]
        x2 = x_ref[...] + _nn(oc, wo_ref[0:D_CONV, :]) + _nn(oa, wo_ref[D_CONV:D_MIX, :])
        r = _rstd(x2)
        xhat = x2 * r
        err = xhat * gf_ref[...] - t_ref[...]
        loss_ref[...] += jnp.sum(err * err, axis=0, keepdims=True) * (0.5 / D_MODEL)
        dy = err * (1.0 / D_MODEL)
        gnf_ref[...] += jnp.sum(dy * xhat, axis=0, keepdims=True)
        dx2 = _rms_bwd(dy * gf_ref[...], xhat, r)
        dx2_ref[...] = dx2
        db = dx2.astype(BF16)
        doc_ref[...] = _nt(db, wo_ref[0:D_CONV, :])
        doa_ref[...] = _nt(db, wo_ref[D_CONV:D_MIX, :])
        gwo_ref[0:D_CONV, :] += _tn(oc, db)
        gwo_ref[D_CONV:D_MIX, :] += _tn(oa, db)

    row = pl.BlockSpec((tile, D_MODEL), lambda i: (i, 0))
    vec = pl.BlockSpec((1, D_MODEL), lambda i: (0, 0))
    return pl.pallas_call(
        body, name="out_proj_loss", grid=(seq // tile,),
        out_shape=(jax.ShapeDtypeStruct((seq, D_MODEL), F32), jax.ShapeDtypeStruct((seq, D_CONV), F32),
                   jax.ShapeDtypeStruct((seq, D_ATTN), F32), jax.ShapeDtypeStruct((D_MIX, D_MODEL), F32),
                   jax.ShapeDtypeStruct((1, D_MODEL), F32), jax.ShapeDtypeStruct((1, D_MODEL), F32)),
        in_specs=[row, row, row, _resident(wo.shape), _resident((1, D_MODEL)), row],
        out_specs=(row, row, row, pl.BlockSpec((D_MIX, D_MODEL), lambda i: (0, 0)), vec, vec),
        compiler_params=_params(48, ("arbitrary",)),
    )(x, oc, oa, wo, norm_final, target)


def _conv_bwd(pc, doc, conv_w, norm_conv_out):
    seq = pc.shape[0]
    tile = TOK_TILE
    n_tiles = seq // tile

    def body(pc_ref, hcc_ref, hcu_ref, doc_ref, cw_ref, gn_ref, dpc_ref, gnc_ref, gcw_ref, zbuf, dbuf):
        step = pl.program_id(0)

        @pl.when(step == 0)
        def _():
            gnc_ref[...] = jnp.zeros_like(gnc_ref)
            gcw_ref[...] = jnp.zeros_like(gcw_ref)
            dbuf[tile:tile + 8, :] = jnp.zeros((8, D_CONV), F32)

        is_first_tile = step == n_tiles - 1
        zbuf[0:8, :] = jnp.where(is_first_tile, 0.0, hcc_ref[...] * hcu_ref[...])
        cb, cc, cu, z, z1, z2, conv = _conv_core(pc_ref, zbuf, cw_ref)
        gc = pc_ref[:, 3 * D_CONV:4 * D_CONV]
        silu, dsilu = _silu_and_grad(gc)
        yc = cb * conv
        r = _rstd(yc)
        xhat = yc * r
        do = doc_ref[...]
        dn = do * silu
        dpc_ref[:, 3 * D_CONV:4 * D_CONV] = (do * (xhat * gn_ref[...]) * dsilu).astype(BF16)
        gnc_ref[...] += jnp.sum(dn * xhat, axis=0, keepdims=True)
        dyc = _rms_bwd(dn * gn_ref[...], xhat, r)
        dpc_ref[:, 0:D_CONV] = (dyc * conv).astype(BF16)
        dconv = dyc * cb
        gcw_ref[0:1, :] += jnp.sum(dconv * z2, axis=0, keepdims=True)
        gcw_ref[1:2, :] += jnp.sum(dconv * z1, axis=0, keepdims=True)
        gcw_ref[2:3, :] += jnp.sum(dconv * z, axis=0, keepdims=True)
        dbuf[0:tile, :] = dconv
        dz = cw_ref[2:3, :] * dconv + cw_ref[1:2, :] * dbuf[1:tile + 1, :] + cw_ref[0:1, :] * dbuf[2:tile + 2, :]
        dpc_ref[:, D_CONV:2 * D_CONV] = (dz * cu).astype(BF16)
        dpc_ref[:, 2 * D_CONV:3 * D_CONV] = (dz * cc).astype(BF16)
        dbuf[tile:tile + 8, :] = dbuf[0:8, :]

    def rev(i):
        return n_tiles - 1 - i

    def halo(col_block):
        return pl.BlockSpec((8, D_CONV), lambda i: (jnp.maximum(rev(i) * (tile // 8) - 1, 0), col_block))

    return pl.pallas_call(
        body, name="conv_bwd", grid=(n_tiles,),
        out_shape=(jax.ShapeDtypeStruct((seq, D_PC), BF16), jax.ShapeDtypeStruct((1, D_CONV), F32),
                   jax.ShapeDtypeStruct((8, D_CONV), F32)),
        in_specs=[pl.BlockSpec((tile, D_PC), lambda i: (rev(i), 0)), halo(1), halo(2),
                  pl.BlockSpec((tile, D_CONV), lambda i: (rev(i), 0)), _resident(conv_w.shape), _resident((1, D_CONV))],
        out_specs=(pl.BlockSpec((tile, D_PC), lambda i: (rev(i), 0)), pl.BlockSpec((1, D_CONV), lambda i: (0, 0)),
                   pl.BlockSpec((8, D_CONV), lambda i: (0, 0))),
        scratch_shapes=[pltpu.VMEM((tile + 8, D_CONV), F32), pltpu.VMEM((tile + 8, D_CONV), F32)],
        compiler_params=_params(48, ("arbitrary",)),
    )(pc, pc, pc, doc, conv_w, norm_conv_out)


def _attn_bwd(q, kv, ga, ya, doa, sinks, norm_attn_out):
    seq = q.shape[0]
    n_blocks = seq // BLK

    def body(q_ref, kvc_ref, kvp_ref, ga_ref, ya_ref, doa_ref, sink_ref, gn_ref,
             dq_ref, dkv_ref, dga_ref, gna_ref, gs_ref, carry, dya_buf):
        step = pl.program_id(0)

        @pl.when(step == 0)
        def _():
            gna_ref[...] = jnp.zeros_like(gna_ref)
            gs_ref[...] = jnp.zeros_like(gs_ref)
            carry[...] = jnp.zeros_like(carry)

        ya = ya_ref[...]
        r = _rstd(ya)
        xhat = ya * r
        silu, dsilu = _silu_and_grad(ga_ref[...])
        do = doa_ref[...]
        dn = do * silu
        dga_ref[...] = (do * (xhat * gn_ref[...]) * dsilu).astype(BF16)
        gna_ref[...] += jnp.sum(dn * xhat, axis=0, keepdims=True)
        dya_buf[...] = _rms_bwd(dn * gn_ref[...], xhat, r).astype(BF16)

        dist, valid = _band_masks(n_blocks - 1 - step)
        kband = jnp.concatenate([kvp_ref[:, 0:D_KV], kvc_ref[:, 0:D_KV]], axis=0)
        vband = jnp.concatenate([kvp_ref[:, D_KV:2 * D_KV], kvc_ref[:, D_KV:2 * D_KV]], axis=0)
        lane = lax.broadcasted_iota(jnp.int32, (2 * BLK, 128), 1)
        sink_lane = lax.broadcasted_iota(jnp.int32, (1, 128), 1)
        gsink = jnp.zeros((1, 128), F32)
        dk_groups, dv_groups = [], []
        for group in range(N_HEADS // HEADS_PER_KV):
            kbd = _block_diag(kband, group)
            vbd = _block_diag(vband, group)
            acc_k = jnp.zeros((2 * BLK, 128), F32)
            acc_v = jnp.zeros((2 * BLK, 128), F32)
            for pair in range(group * 4, group * 4 + 4):
                qp, probs, sink_probs = _attn_pair_probs(q_ref, pair, kbd, sink_ref, dist, valid)
                dyp = dya_buf[:, pair * 128:(pair + 1) * 128]
                dp = _nt(dyp, vbd)
                ds_halves = []
                for u in range(2):
                    p_u = probs[:, u * 2 * BLK:(u + 1) * 2 * BLK]
                    dp_u = dp[:, u * 2 * BLK:(u + 1) * 2 * BLK]
                    delta = jnp.sum(p_u * dp_u, axis=-1, keepdims=True)
                    ds_halves.append(p_u * (dp_u - delta) * SCALE)
                    dsink = -jnp.sum(sink_probs[u] * delta, axis=0, keepdims=True)
                    gsink = gsink + jnp.where(sink_lane == 2 * pair + u, dsink, 0.0)
                ds = jnp.concatenate(ds_halves, axis=1).astype(BF16)
                dq_ref[:, pair * 128:(pair + 1) * 128] = _nn(ds, kbd).astype(BF16)
                dkbd = _tn(ds, qp)
                dvbd = _tn(probs.astype(BF16), dyp)
                acc_k += jnp.where(lane < 64, dkbd[0:2 * BLK], 0.0) + jnp.where(lane >= 64, dkbd[2 * BLK:4 * BLK], 0.0)
                acc_v += jnp.where(lane < 64, dvbd[0:2 * BLK], 0.0) + jnp.where(lane >= 64, dvbd[2 * BLK:4 * BLK], 0.0)
            dk_groups.append(acc_k + pltpu.roll(acc_k, 64, 1))
            dv_groups.append(acc_v + pltpu.roll(acc_v, 64, 1))
        gs_ref[...] += gsink
        dkv_band = jnp.concatenate([jnp.where(lane < 64, dk_groups[0], dk_groups[1]),
                                    jnp.where(lane < 64, dv_groups[0], dv_groups[1])], axis=1)
        dkv_ref[...] = (dkv_band[BLK:2 * BLK] + carry[...]).astype(BF16)
        carry[...] = dkv_band[0:BLK]

    row = pl.BlockSpec((BLK, D_ATTN), lambda i: (n_blocks - 1 - i, 0))
    kv_cur, kv_prev = _kv_specs(n_blocks, reverse=True)
    return pl.pallas_call(
        body, name="attn_bwd", grid=(n_blocks,),
        out_shape=(jax.ShapeDtypeStruct((seq, D_ATTN), BF16), jax.ShapeDtypeStruct((seq, 2 * D_KV), BF16),
                   jax.ShapeDtypeStruct((seq, D_ATTN), BF16), jax.ShapeDtypeStruct((1, D_ATTN), F32),
                   jax.ShapeDtypeStruct((1, 128), F32)),
        in_specs=[row, kv_cur, kv_prev, row, row, row, pl.BlockSpec(memory_space=pltpu.SMEM), _resident((1, D_ATTN))],
        out_specs=(row, kv_cur, row, pl.BlockSpec((1, D_ATTN), lambda i: (0, 0)), pl.BlockSpec((1, 128), lambda i: (0, 0))),
        scratch_shapes=[pltpu.VMEM((BLK, 2 * D_KV), F32), pltpu.VMEM((BLK, D_ATTN), BF16)],
        compiler_params=_params(32, ("arbitrary",)),
    )(q, kv, kv, ga, ya, doa, sinks, norm_attn_out)


def _dw_in(dpc, dq, dkv, dga, h):
    seq = h.shape[0]
    rows = 256
    first = (0, ROW_Q // rows, ROW_KV // rows, ROW_GA // rows, D_IN_PROJ // rows)

    def body(dpc_ref, dq_ref, dkv_ref, dga_ref, h_ref, out_ref):
        i = pl.program_id(0)
        for piece, ref in enumerate((dpc_ref, dq_ref, dkv_ref, dga_ref)):
            @pl.when((i >= first[piece]) & (i < first[piece + 1]))
            def _(ref=ref):
                out_ref[...] = _tn(ref[...], h_ref[...])

    def cols(piece):
        n = first[piece + 1] - first[piece]
        return pl.BlockSpec((seq, rows), lambda i: (0, jnp.clip(i - first[piece], 0, n - 1)))

    return pl.pallas_call(
        body, name="dw_in", grid=(D_IN_PROJ // rows,),
        out_shape=jax.ShapeDtypeStruct((D_IN_PROJ, D_MODEL), F32),
        in_specs=[cols(0), cols(1), cols(2), cols(3), _resident(h.shape)],
        out_specs=pl.BlockSpec((rows, D_MODEL), lambda i: (i, 0)),
        compiler_params=_params(32, ("arbitrary",)),
    )(dpc, dq, dkv, dga, h)


def _dh_gradx(dpc, dq, dkv, dga, wt, x, norm_in, dx2):
    seq = x.shape[0]
    tile = TOK_TILE

    def body(dpc_ref, dq_ref, dkv_ref, dga_ref, wt_ref, x_ref, g_ref, dx2_ref, gx_ref, gni_ref):
        @pl.when(pl.program_id(0) == 0)
        def _():
            gni_ref[...] = jnp.zeros_like(gni_ref)

        dh = (_nn(dpc_ref[...], wt_ref[0:ROW_Q, :]) + _nn(dq_ref[...], wt_ref[ROW_Q:ROW_KV, :])
              + _nn(dkv_ref[...], wt_ref[ROW_KV:ROW_GA, :]) + _nn(dga_ref[...], wt_ref[ROW_GA:D_IN_PROJ, :]))
        xv = x_ref[...]
        r = _rstd(xv)
        xhat = xv * r
        gni_ref[...] += jnp.sum(dh * xhat, axis=0, keepdims=True)
        gx_ref[...] = _rms_bwd(dh * g_ref[...], xhat, r) + dx2_ref[...]

    def row(width):
        return pl.BlockSpec((tile, width), lambda i: (i, 0))

    return pl.pallas_call(
        body, name="dh_gradx", grid=(seq // tile,),
        out_shape=(jax.ShapeDtypeStruct((seq, D_MODEL), F32), jax.ShapeDtypeStruct((1, D_MODEL), F32)),
        in_specs=[row(D_PC), row(D_ATTN), row(2 * D_KV), row(D_ATTN), _resident(wt.shape), row(D_MODEL),
                  _resident((1, D_MODEL)), row(D_MODEL)],
        out_specs=(row(D_MODEL), pl.BlockSpec((1, D_MODEL), lambda i: (0, 0))),
        compiler_params=_params(48, ("arbitrary",)),
    )(dpc, dq, dkv, dga, wt, x, norm_in, dx2)


def _accumulate(dst_ref, src_ref, rows=16):
    def step(i, carry):
        sl = pl.ds(pl.multiple_of(i * rows, rows), rows)
        dst_ref[sl, :] = dst_ref[sl, :] + src_ref[sl, :]
        return carry
    lax.fori_loop(0, dst_ref.shape[0] // rows, step, 0)


def _rs_grads(gwt, gwo, small):
    t_rows, o_rows = gwt.shape[0] // N_CHIPS, gwo.shape[0] // N_CHIPS
    t_half, o_half = t_rows // 2, o_rows // 2
    width = gwt.shape[1]

    def body(gwt_ref, gwo_ref, small_ref, gwt_sh, gwo_sh, small_sum, r1t, r1o, r2t, r2o, r3t, r3o,
             acc_t, acc_o, stage_t, stage_o, small_land, send_sems, recv_sems, local_sems):
        x, y, c = lax.axis_index("x"), lax.axis_index("y"), lax.axis_index("c")
        me = 4 * x + 2 * y + c
        j = 2 * x + y
        pa = (_xor(x, 1 - c), _xor(y, c), c)
        pb = (_xor(x, c), _xor(y, 1 - c), c)
        sib = (x, y, 1 - c)
        ja = 2 * pa[0] + pa[1]
        jb = 2 * pb[0] + pb[1]
        jd = 3 - j

        def rcopy(k, src, dst, to):
            return pltpu.make_async_remote_copy(src_ref=src, dst_ref=dst, send_sem=send_sems.at[k],
                                                recv_sem=recv_sems.at[k], device_id=to, device_id_type=MESH)

        def t_rows_of(chip, half):
            return gwt_ref.at[pl.ds(pl.multiple_of(chip * t_rows + half * t_half, 8), t_half), :]

        def o_rows_of(chip, half):
            return gwo_ref.at[pl.ds(pl.multiple_of(chip * o_rows + half * o_half, 8), o_half), :]

        small_land[me] = small_ref[...]
        others = [(dx, dy, dc) for dx in (0, 1) for dy in (0, 1) for dc in (0, 1)][1:]
        small_sends = [rcopy(k, small_ref, small_land.at[me], (_xor(x, dx), _xor(y, dy), _xor(c, dc)))
                       for k, (dx, dy, dc) in enumerate(others)]
        for cp in small_sends:
            cp.start()

        order = (ja, jd, jb, j)
        sib_order = (jb, jd, ja, j)
        pair_sends = []
        for s, chip in enumerate(sib_order):
            pair_sends += [rcopy(7 + s, t_rows_of(chip, 1 - c), r1t.at[s], sib),
                           rcopy(11 + s, o_rows_of(chip, 1 - c), r1o.at[s], sib)]
        for cp in pair_sends:
            cp.start()

        def load(src, dst, k):
            cp = pltpu.make_async_copy(src, dst, local_sems.at[k])
            cp.start()
            return cp

        def add_landed(s, mine_t, mine_o, land_t, land_o):
            cps = [load(mine_t, acc_t.at[s], 0), load(mine_o, acc_o.at[s], 1),
                   load(land_t, stage_t, 2), load(land_o, stage_o, 3)]
            for cp in cps:
                cp.wait()
            _accumulate(acc_t.at[s], stage_t)
            _accumulate(acc_o.at[s], stage_o)

        hop1 = []
        for s, chip in enumerate(order):
            rcopy(7 + s, t_rows_of(chip, c), r1t.at[s], sib).wait_recv()
            rcopy(11 + s, o_rows_of(chip, c), r1o.at[s], sib).wait_recv()
            add_landed(s, t_rows_of(chip, c), o_rows_of(chip, c), r1t.at[s], r1o.at[s])
            if s < 2:
                hop1 += [rcopy(15 + s, acc_t.at[s], r2t.at[s], pa), rcopy(17 + s, acc_o.at[s], r2o.at[s], pa)]
                hop1[-2].start()
                hop1[-1].start()

        def add_hop(s, land_t, land_o):
            cps = [load(land_t, stage_t, 2), load(land_o, stage_o, 3)]
            for cp in cps:
                cp.wait()
            _accumulate(acc_t.at[s], stage_t)
            _accumulate(acc_o.at[s], stage_o)

        rcopy(16, acc_t.at[1], r2t.at[1], pa).wait_recv()
        rcopy(18, acc_o.at[1], r2o.at[1], pa).wait_recv()
        add_hop(2, r2t.at[1], r2o.at[1])
        hop2 = [rcopy(19, acc_t.at[2], r3t, pb), rcopy(20, acc_o.at[2], r3o, pb)]
        for cp in hop2:
            cp.start()
        rcopy(15, acc_t.at[0], r2t.at[0], pa).wait_recv()
        rcopy(17, acc_o.at[0], r2o.at[0], pa).wait_recv()
        add_hop(3, r2t.at[0], r2o.at[0])
        rcopy(19, acc_t.at[2], r3t, pb).wait_recv()
        rcopy(20, acc_o.at[2], r3o, pb).wait_recv()
        add_hop(3, r3t, r3o)

        out_t = gwt_sh.at[pl.ds(pl.multiple_of(c * t_half, 8), t_half), :]
        out_o = gwo_sh.at[pl.ds(pl.multiple_of(c * o_half, 8), o_half), :]
        keep = [load(acc_t.at[3], out_t, 0), load(acc_o.at[3], out_o, 1)]
        swap = [rcopy(21, acc_t.at[3], out_t, sib), rcopy(22, acc_o.at[3], out_o, sib)]
        for cp in swap:
            cp.start()

        for cp in small_sends:
            cp.wait_recv()
        total = small_land[0]
        for dev in range(1, 8):
            total = total + small_land[dev]
        small_sum[...] = total

        rcopy(21, acc_t.at[3], gwt_sh.at[pl.ds(pl.multiple_of((1 - c) * t_half, 8), t_half), :], sib).wait_recv()
        rcopy(22, acc_o.at[3], gwo_sh.at[pl.ds(pl.multiple_of((1 - c) * o_half, 8), o_half), :], sib).wait_recv()
        for cp in small_sends + pair_sends + hop1 + hop2 + swap:
            cp.wait_send()
        for cp in keep:
            cp.wait()

    any_spec = pl.BlockSpec(memory_space=pl.ANY)
    vmem_spec = pl.BlockSpec(memory_space=pltpu.VMEM)
    outs = pl.pallas_call(
        body, name="rs_grads",
        out_shape=(jax.ShapeDtypeStruct((t_rows, width), F32), jax.ShapeDtypeStruct((o_rows, width), F32),
                   jax.ShapeDtypeStruct(small.shape, F32),
                   jax.ShapeDtypeStruct((4, t_half, width), F32), jax.ShapeDtypeStruct((4, o_half, width), F32),
                   jax.ShapeDtypeStruct((2, t_half, width), F32), jax.ShapeDtypeStruct((2, o_half, width), F32),
                   jax.ShapeDtypeStruct((t_half, width), F32), jax.ShapeDtypeStruct((o_half, width), F32)),
        in_specs=[any_spec, any_spec, vmem_spec],
        out_specs=(any_spec, any_spec, vmem_spec) + (any_spec,) * 6,
        scratch_shapes=[pltpu.VMEM((4, t_half, width), F32), pltpu.VMEM((4, o_half, width), F32),
                        pltpu.VMEM((t_half, width), F32), pltpu.VMEM((o_half, width), F32),
                        pltpu.VMEM((8,) + small.shape, F32),
                        pltpu.SemaphoreType.DMA((23,)), pltpu.SemaphoreType.DMA((23,)), pltpu.SemaphoreType.DMA((4,))],
        compiler_params=_params(40),
    )(gwt, gwo, small)
    return outs[0], outs[1], outs[2]


def _adamw(name, w, g, m, v, rows):
    def body(w_ref, g_ref, m_ref, v_ref, d_ref, nm_ref, nv_ref):
        gv = g_ref[...]
        nm = ADAM_B1 * m_ref[...] + (1.0 - ADAM_B1) * gv
        nv = ADAM_B2 * v_ref[...] + (1.0 - ADAM_B2) * (gv * gv)
        m_hat = nm / (1.0 - ADAM_B1 ** ADAM_STEP)
        v_hat = nv / (1.0 - ADAM_B2 ** ADAM_STEP)
        d_ref[...] = -ADAM_LR * (m_hat / (jnp.sqrt(v_hat) + ADAM_EPS) + ADAM_WD * w_ref[...])
        nm_ref[...] = nm
        nv_ref[...] = nv

    spec = pl.BlockSpec((rows, w.shape[1]), lambda i: (i, 0))
    shape = jax.ShapeDtypeStruct(w.shape, F32)
    return pl.pallas_call(
        body, name="adamw_" + name, grid=(w.shape[0] // rows,),
        out_shape=(shape, shape, shape), in_specs=[spec] * 4, out_specs=(spec,) * 3,
        compiler_params=_params(32, ("arbitrary",)),
    )(w, g, m, v)


def kernel(x, norm_in, w_in, conv_w, attn_sinks, norm_conv_out, norm_attn_out, w_out, norm_final, loss_target, m_norm_in, m_w_in, m_conv_w, m_attn_sinks, m_norm_conv_out, m_norm_attn_out, m_w_out, m_norm_final, v_norm_in, v_w_in, v_conv_w, v_attn_sinks, v_norm_conv_out, v_norm_attn_out, v_w_out, v_norm_final):
    chip = 2 * lax.axis_index("x") + lax.axis_index("y")
    xs, target = x[0], loss_target[0]
    norm_final2 = norm_final.reshape(1, D_MODEL)

    wt, wo, cw4 = _ag_weights(w_in[0].T.astype(BF16), w_out[0].astype(BF16), conv_w[0])
    cw = jnp.transpose(cw4, (1, 0, 2)).reshape(3, D_CONV)

    h, pc, q, kv, ga = _fwd_in(xs, norm_in, wt)
    oc = _conv_fwd(pc, cw, norm_conv_out)
    ya, oa = _attn_fwd(q, kv, ga, attn_sinks, norm_attn_out)
    dx2, doc, doa, gwo, gnf, loss_lanes = _out_proj_loss(xs, oc, oa, wo, norm_final2, target)

    dpc, gnc, gcw = _conv_bwd(pc, doc, cw, norm_conv_out)
    dq, dkv, dga, gna, gsink = _attn_bwd(q, kv, ga, ya, doa, attn_sinks, norm_attn_out)
    gwt = _dw_in(dpc, dq, dkv, dga, h)
    grad_x, gni = _dh_gradx(dpc, dq, dkv, dga, wt, xs, norm_in, dx2)

    last = jnp.zeros((1, D_MODEL), F32).at[:, 0:N_HEADS].set(gsink[:, 0:N_HEADS]).at[0, N_HEADS].set(jnp.sum(loss_lanes))
    small = jnp.concatenate([gni, gnc, gna, gnf, gcw[0:3], last], axis=0)
    gwt_sh, gwo_sh, small_sum = _rs_grads(gwt, gwo, small)

    loss = small_sum[7, N_HEADS]
    g_norm_in, g_norm_conv, g_norm_attn = small_sum[0:1], small_sum[1:2], small_sum[2:3]
    g_norm_final = small_sum[3]
    g_conv_w = lax.dynamic_slice(small_sum[4:7], (0, chip * (D_CONV // N_CHIPS)), (3, D_CONV // N_CHIPS))[None]
    g_sinks = small_sum[7:8, 0:N_HEADS]
    g_w_in = gwt_sh.T[None]
    g_w_out = gwo_sh[None]

    def adam(name, w, g, m, v, rows):
        shape = w.shape
        two_d = (-1, shape[-1])
        out = _adamw(name, w.reshape(two_d), g.reshape(two_d), m.reshape(two_d), v.reshape(two_d), rows)
        return tuple(o.reshape(shape) for o in out)

    weights = (norm_in, w_in, conv_w, attn_sinks, norm_conv_out, norm_attn_out, w_out, norm_final)
    grads = (g_norm_in, g_w_in, g_conv_w, g_sinks, g_norm_conv, g_norm_attn, g_w_out, g_norm_final)
    moments_m = (m_norm_in, m_w_in, m_conv_w, m_attn_sinks, m_norm_conv_out, m_norm_attn_out, m_w_out, m_norm_final)
    moments_v = (v_norm_in, v_w_in, v_conv_w, v_attn_sinks, v_norm_conv_out, v_norm_attn_out, v_w_out, v_norm_final)
    rows = (1, None, 3, 1, 1, 1, 128, 1)
    names = ("norm_in", "w_in", "conv_w", "attn_sinks", "norm_conv_out", "norm_attn_out", "w_out", "norm_final")
    updates = []
    for args in zip(names, weights, grads, moments_m, moments_v, rows):
        if args[0] == "w_in":
            out_t = _adamw("w_in", w_in[0].T, gwt_sh, m_w_in[0].T, v_w_in[0].T, 200)
            updates.append(tuple(o.T[None] for o in out_t))
        else:
            updates.append(adam(*args))
    deltas, new_m, new_v = zip(*updates)
    return (loss, grad_x[None], *grads, *deltas, *new_m, *new_v)
```

```python
import jax
import jax.numpy as jnp
from jax import lax
from jax.experimental import pallas as pl
from jax.experimental.pallas import tpu as pltpu

F32 = jnp.float32
BF16 = jnp.bfloat16
MESH = pl.DeviceIdType.MESH

D_MODEL = 1024
D_CONV = 1024
D_ATTN = 1024
D_KV = 128
D_MIX = D_CONV + D_ATTN
D_PC = 4 * D_CONV
D_IN_PROJ = D_PC + 2 * D_ATTN + 2 * D_KV
ROW_Q = D_PC
ROW_KV = ROW_Q + D_ATTN
ROW_GA = ROW_KV + 2 * D_KV
N_HEADS = 16
HEAD_DIM = 64
HEADS_PER_KV = 8
BLK = 128
N_CHIPS = 4
RMS_EPS = 1e-5
SCALE = HEAD_DIM ** -0.5
SLOPES = tuple(2.0 ** (-8.0 * (h + 1) / N_HEADS) for h in range(N_HEADS))

ADAM_LR, ADAM_B1, ADAM_B2, ADAM_EPS, ADAM_WD, ADAM_STEP = 0.001, 0.9, 0.999, 1e-08, 0.01, 10

TOK_TILE = 256
MIB = 1 << 20


def _params(vmem_mib, semantics=None):
    return pltpu.CompilerParams(dimension_semantics=semantics, vmem_limit_bytes=vmem_mib * MIB)


def _nn(a, b):
    return jnp.dot(a, b, preferred_element_type=F32)


def _nt(a, b):
    return lax.dot_general(a, b, (((1,), (1,)), ((), ())), preferred_element_type=F32)


def _tn(a, b):
    return lax.dot_general(a, b, (((0,), (0,)), ((), ())), preferred_element_type=F32)


def _rstd(v):
    return lax.rsqrt(jnp.mean(v * v, axis=-1, keepdims=True) + RMS_EPS)


def _rms_bwd(g, xhat, rstd):
    return rstd * (g - xhat * jnp.mean(g * xhat, axis=-1, keepdims=True))


def _silu_and_grad(g):
    s = jax.nn.sigmoid(g)
    return g * s, s * (1.0 + g * (1.0 - s))


def _resident(shape):
    return pl.BlockSpec(shape, lambda *_: (0,) * len(shape), pipeline_mode=pl.Buffered(1))


def _xor(a, b):
    return a + b - 2 * a * b


def _cast_rows(src_ref, dst_ref, rows=32):
    def step(i, carry):
        sl = pl.ds(pl.multiple_of(i * rows, rows), rows)
        dst_ref[sl, :] = src_ref[sl, :].astype(dst_ref.dtype)
        return carry
    lax.fori_loop(0, src_ref.shape[0] // rows, step, 0)


def _ag_weights(wt_sh, wo_sh, cw_sh):
    wt_rows, wo_rows = wt_sh.shape[0], wo_sh.shape[0]
    wt_half, wo_half = wt_rows // 2, wo_rows // 2
    width = wt_sh.shape[1]

    def body(wt_ref, wo_ref, cw_ref, wt_out, wo_out, cw_out, f32_t, f32_o, own_t, own_o, land_t, land_o,
             send_sems, recv_sems, local_sems):
        x, y, c = lax.axis_index("x"), lax.axis_index("y"), lax.axis_index("c")
        j = 2 * x + y
        p1 = (_xor(x, c), _xor(y, 1 - c), c)
        p2 = (_xor(x, 1 - c), _xor(y, c), c)
        sib = (x, y, 1 - c)
        j1 = 2 * p1[0] + p1[1]
        j2 = 2 * p2[0] + p2[1]
        j3 = 3 - j

        def wt_rows_of(chip, half):
            return wt_out.at[pl.ds(pl.multiple_of(chip * wt_rows + half * wt_half, 16), wt_half), :]

        def wo_rows_of(chip, half):
            return wo_out.at[pl.ds(pl.multiple_of(chip * wo_rows + half * wo_half, 16), wo_half), :]

        def rcopy(k, src, dst, to):
            return pltpu.make_async_remote_copy(src_ref=src, dst_ref=dst, send_sem=send_sems.at[k],
                                                recv_sem=recv_sems.at[k], device_id=to, device_id_type=MESH)

        def lcopy(k, src, dst):
            cp = pltpu.make_async_copy(src, dst, local_sems.at[k])
            cp.start()
            return cp

        loads = [lcopy(0, wt_ref, f32_t), lcopy(1, wo_ref, f32_o)]
        local = [lcopy(2, cw_ref, cw_out.at[j])]
        for cp in loads:
            cp.wait()
        _cast_rows(f32_t, own_t)
        _cast_rows(f32_o, own_o)
        local += [lcopy(0, own_t, wt_out.at[pl.ds(pl.multiple_of(j * wt_rows, 16), wt_rows), :]),
                  lcopy(1, own_o, wo_out.at[pl.ds(pl.multiple_of(j * wo_rows, 16), wo_rows), :])]
        half_t = own_t.at[pl.ds(pl.multiple_of(c * wt_half, 16), wt_half), :]
        half_o = own_o.at[pl.ds(pl.multiple_of(c * wo_half, 16), wo_half), :]

        def publish(slot, chip, k_t, k_o, k_local):
            sends = [rcopy(k_t, land_t.at[slot], wt_rows_of(chip, c), sib), rcopy(k_o, land_o.at[slot], wo_rows_of(chip, c), sib)]
            for cp in sends:
                cp.start()
            local.extend([lcopy(k_local, land_t.at[slot], wt_rows_of(chip, c)),
                          lcopy(k_local + 1, land_o.at[slot], wo_rows_of(chip, c))])
            return sends

        hop1 = [rcopy(0, half_t, land_t.at[0], p1), rcopy(1, half_o, land_o.at[0], p1), rcopy(2, cw_ref, cw_out.at[j], p1)]
        for cp in hop1:
            cp.start()
        rcopy(0, half_t, land_t.at[0], p1).wait_recv()
        rcopy(1, half_o, land_o.at[0], p1).wait_recv()
        rcopy(2, cw_ref, cw_out.at[j1], p1).wait_recv()
        hop2 = [rcopy(3, half_t, land_t.at[1], p2), rcopy(5, half_o, land_o.at[1], p2), rcopy(7, cw_ref, cw_out.at[j], p2),
                rcopy(4, land_t.at[0], land_t.at[2], p2), rcopy(6, land_o.at[0], land_o.at[2], p2),
                rcopy(8, cw_out.at[j1], cw_out.at[j1], p2)]
        for cp in hop2:
            cp.start()
        swaps = publish(0, j1, 9, 12, 3)
        rcopy(3, half_t, land_t.at[1], p2).wait_recv()
        rcopy(5, half_o, land_o.at[1], p2).wait_recv()
        rcopy(7, cw_ref, cw_out.at[j2], p2).wait_recv()
        swaps += publish(1, j2, 10, 13, 5)
        rcopy(4, half_t, land_t.at[2], p2).wait_recv()
        rcopy(6, half_o, land_o.at[2], p2).wait_recv()
        rcopy(8, cw_ref, cw_out.at[j3], p2).wait_recv()
        swaps += publish(2, j3, 11, 14, 7)
        for k, chip in ((9, j2), (10, j1), (11, j3)):
            rcopy(k, half_t, wt_rows_of(chip, 1 - c), sib).wait_recv()
        for k, chip in ((12, j2), (13, j1), (14, j3)):
            rcopy(k, half_o, wo_rows_of(chip, 1 - c), sib).wait_recv()
        for cp in hop1 + hop2 + swaps:
            cp.wait_send()
        for cp in local:
            cp.wait()

    any_spec = pl.BlockSpec(memory_space=pl.ANY)
    return pl.pallas_call(
        body, name="ag_weights",
        out_shape=(jax.ShapeDtypeStruct((N_CHIPS * wt_rows, width), BF16),
                   jax.ShapeDtypeStruct((N_CHIPS * wo_rows, width), BF16),
                   jax.ShapeDtypeStruct((N_CHIPS,) + cw_sh.shape, cw_sh.dtype)),
        in_specs=[any_spec, any_spec, any_spec],
        out_specs=(any_spec, any_spec, any_spec),
        scratch_shapes=[pltpu.VMEM((wt_rows, width), F32), pltpu.VMEM((wo_rows, width), F32),
                        pltpu.VMEM((wt_rows, width), BF16), pltpu.VMEM((wo_rows, width), BF16),
                        pltpu.VMEM((3, wt_half, width), BF16), pltpu.VMEM((3, wo_half, width), BF16),
                        pltpu.SemaphoreType.DMA((15,)), pltpu.SemaphoreType.DMA((15,)), pltpu.SemaphoreType.DMA((9,))],
        compiler_params=_params(32),
    )(wt_sh, wo_sh, cw_sh)


def _fwd_in(x, norm_in, wt):
    seq = x.shape[0]
    tile = TOK_TILE

    def body(x_ref, g_ref, wt_ref, h_ref, pc_ref, q_ref, kv_ref, ga_ref):
        xv = x_ref[...]
        h = (xv * _rstd(xv) * g_ref[...]).astype(BF16)
        h_ref[...] = h
        for blk in range(D_PC // D_CONV):
            pc_ref[:, blk * D_CONV:(blk + 1) * D_CONV] = _nt(h, wt_ref[blk * D_CONV:(blk + 1) * D_CONV, :])
        q_ref[...] = _nt(h, wt_ref[ROW_Q:ROW_KV, :])
        kv_ref[...] = _nt(h, wt_ref[ROW_KV:ROW_GA, :])
        ga_ref[...] = _nt(h, wt_ref[ROW_GA:D_IN_PROJ, :])

    def row(width):
        return pl.BlockSpec((tile, width), lambda i: (i, 0))

    return pl.pallas_call(
        body, name="fwd_in", grid=(seq // tile,),
        out_shape=(jax.ShapeDtypeStruct((seq, D_MODEL), BF16), jax.ShapeDtypeStruct((seq, D_PC), F32),
                   jax.ShapeDtypeStruct((seq, D_ATTN), F32), jax.ShapeDtypeStruct((seq, 2 * D_KV), F32),
                   jax.ShapeDtypeStruct((seq, D_ATTN), F32)),
        in_specs=[row(D_MODEL), _resident((1, D_MODEL)), _resident(wt.shape)],
        out_specs=(row(D_MODEL), row(D_PC), row(D_ATTN), row(2 * D_KV), row(D_ATTN)),
        compiler_params=_params(48, ("arbitrary",)),
    )(x, norm_in, wt)


def _conv_core(pc_ref, zbuf, cw_ref):
    tile = pc_ref.shape[0]
    cb = pc_ref[:, 0:D_CONV]
    cc = pc_ref[:, D_CONV:2 * D_CONV]
    cu = pc_ref[:, 2 * D_CONV:3 * D_CONV]
    z = cc * cu
    zbuf[8:tile + 8, :] = z
    z1 = zbuf[7:tile + 7, :]
    z2 = zbuf[6:tile + 6, :]
    conv = cw_ref[0:1, :] * z2 + cw_ref[1:2, :] * z1 + cw_ref[2:3, :] * z
    return cb, cc, cu, z, z1, z2, conv


def _conv_fwd(pc, conv_w, norm_conv_out):
    seq = pc.shape[0]
    tile = TOK_TILE

    def body(pc_ref, cw_ref, gn_ref, oc_ref, zbuf):
        @pl.when(pl.program_id(0) == 0)
        def _():
            zbuf[0:8, :] = jnp.zeros((8, D_CONV), F32)

        cb, _, _, _, _, _, conv = _conv_core(pc_ref, zbuf, cw_ref)
        yc = cb * conv
        silu, _ = _silu_and_grad(pc_ref[:, 3 * D_CONV:4 * D_CONV])
        oc_ref[...] = (yc * _rstd(yc) * gn_ref[...] * silu).astype(BF16)
        zbuf[0:8, :] = zbuf[tile:tile + 8, :]

    return pl.pallas_call(
        body, name="conv_fwd", grid=(seq // tile,),
        out_shape=jax.ShapeDtypeStruct((seq, D_CONV), BF16),
        in_specs=[pl.BlockSpec((tile, D_PC), lambda i: (i, 0)), _resident(conv_w.shape), _resident((1, D_CONV))],
        out_specs=pl.BlockSpec((tile, D_CONV), lambda i: (i, 0)),
        scratch_shapes=[pltpu.VMEM((tile + 8, D_CONV), F32)],
        compiler_params=_params(40, ("arbitrary",)),
    )(pc, conv_w, norm_conv_out)


def _band_masks(block_index):
    qi = lax.broadcasted_iota(jnp.int32, (BLK, 2 * BLK), 0)
    kj = lax.broadcasted_iota(jnp.int32, (BLK, 2 * BLK), 1)
    dist = BLK + qi - kj
    valid = (dist >= 0) & (dist < BLK) & ((kj >= BLK) | (block_index > 0))
    return dist.astype(F32), valid


def _block_diag(band, group):
    lane = lax.broadcasted_iota(jnp.int32, band.shape, 1)
    other = pltpu.roll(band, 64, 1)
    lo, hi = (band, other) if group == 0 else (other, band)
    zero = jnp.zeros_like(band)
    return jnp.concatenate([jnp.where(lane < 64, lo, zero), jnp.where(lane >= 64, hi, zero)], axis=0).astype(BF16)


def _softmax_head(s_raw, head, sink, dist, valid):
    sc = jnp.where(valid, s_raw * SCALE - SLOPES[head] * dist, -jnp.inf)
    m = jnp.maximum(jnp.max(sc, axis=-1, keepdims=True), sink)
    p = jnp.exp(sc - m)
    es = jnp.exp(sink - m)
    inv = 1.0 / (jnp.sum(p, axis=-1, keepdims=True) + es)
    return p * inv, es * inv


def _attn_pair_probs(q_ref, pair, kbd, sink_ref, dist, valid):
    qp = q_ref[:, pair * 128:(pair + 1) * 128].astype(BF16)
    s = _nt(qp, kbd)
    pa, sa = _softmax_head(s[:, 0:2 * BLK], 2 * pair, sink_ref[0, 2 * pair], dist, valid)
    pb, sb = _softmax_head(s[:, 2 * BLK:4 * BLK], 2 * pair + 1, sink_ref[0, 2 * pair + 1], dist, valid)
    return qp, jnp.concatenate([pa, pb], axis=1), (sa, sb)


def _kv_specs(n_blocks, reverse):
    def blk(i):
        return (n_blocks - 1 - i) if reverse else i
    cur = pl.BlockSpec((BLK, 2 * D_KV), lambda i: (blk(i), 0))
    prev = pl.BlockSpec((BLK, 2 * D_KV), lambda i: (jnp.maximum(blk(i) - 1, 0), 0))
    return cur, prev


def _attn_fwd(q, kv, ga, sinks, norm_attn_out):
    seq = q.shape[0]
    n_blocks = seq // BLK

    def body(q_ref, kvc_ref, kvp_ref, ga_ref, sink_ref, gn_ref, ya_ref, oa_ref):
        dist, valid = _band_masks(pl.program_id(0))
        kband = jnp.concatenate([kvp_ref[:, 0:D_KV], kvc_ref[:, 0:D_KV]], axis=0)
        vband = jnp.concatenate([kvp_ref[:, D_KV:2 * D_KV], kvc_ref[:, D_KV:2 * D_KV]], axis=0)
        for group in range(N_HEADS // HEADS_PER_KV):
            kbd = _block_diag(kband, group)
            vbd = _block_diag(vband, group)
            for pair in range(group * 4, group * 4 + 4):
                _, probs, _ = _attn_pair_probs(q_ref, pair, kbd, sink_ref, dist, valid)
                ya_ref[:, pair * 128:(pair + 1) * 128] = _nn(probs.astype(BF16), vbd)
        ya = ya_ref[...]
        silu, _ = _silu_and_grad(ga_ref[...])
        oa_ref[...] = (ya * _rstd(ya) * gn_ref[...] * silu).astype(BF16)

    row = pl.BlockSpec((BLK, D_ATTN), lambda i: (i, 0))
    kv_cur, kv_prev = _kv_specs(n_blocks, reverse=False)
    return pl.pallas_call(
        body, name="attn_fwd", grid=(n_blocks,),
        out_shape=(jax.ShapeDtypeStruct((seq, D_ATTN), F32), jax.ShapeDtypeStruct((seq, D_ATTN), BF16)),
        in_specs=[row, kv_cur, kv_prev, row, pl.BlockSpec(memory_space=pltpu.SMEM), _resident((1, D_ATTN))],
        out_specs=(row, row),
        compiler_params=_params(32, ("arbitrary",)),
    )(q, kv, kv, ga, sinks, norm_attn_out)


def _out_proj_loss(x, oc, oa, wo, norm_final, target):
    seq = x.shape[0]
    tile = TOK_TILE

    def body(x_ref, oc_ref, oa_ref, wo_ref, gf_ref, t_ref, dx2_ref, doc_ref, doa_ref, gwo_ref, gnf_ref, loss_ref):
        @pl.when(pl.program_id(0) == 0)
        def _():
            gwo_ref[...] = jnp.zeros_like(gwo_ref)
            gnf_ref[...] = jnp.zeros_like(gnf_ref)
            loss_ref[...] = jnp.zeros_like(loss_ref)

        oc, oa = oc_ref[...], oa_ref[...]
        x2 = x_ref[...] + _nn(oc, wo_ref[0:D_CONV, :]) + _nn(oa, wo_ref[D_CONV:D_MIX, :])
        r = _rstd(x2)
        xhat = x2 * r
        err = xhat * gf_ref[...] - t_ref[...]
        loss_ref[...] += jnp.sum(err * err, axis=0, keepdims=True) * (0.5 / D_MODEL)
        dy = err * (1.0 / D_MODEL)
        gnf_ref[...] += jnp.sum(dy * xhat, axis=0, keepdims=True)
        dx2 = _rms_bwd(dy * gf_ref[...], xhat, r)
        dx2_ref[...] = dx2
        db = dx2.astype(BF16)
        doc_ref[...] = _nt(db, wo_ref[0:D_CONV, :])
        doa_ref[...] = _nt(db, wo_ref[D_CONV:D_MIX, :])
        gwo_ref[0:D_CONV, :] += _tn(oc, db)
        gwo_ref[D_CONV:D_MIX, :] += _tn(oa, db)

    row = pl.BlockSpec((tile, D_MODEL), lambda i: (i, 0))
    vec = pl.BlockSpec((1, D_MODEL), lambda i: (0, 0))
    return pl.pallas_call(
        body, name="out_proj_loss", grid=(seq // tile,),
        out_shape=(jax.ShapeDtypeStruct((seq, D_MODEL), F32), jax.ShapeDtypeStruct((seq, D_CONV), F32),
                   jax.ShapeDtypeStruct((seq, D_ATTN), F32), jax.ShapeDtypeStruct((D_MIX, D_MODEL), F32),
                   jax.ShapeDtypeStruct((1, D_MODEL), F32), jax.ShapeDtypeStruct((1, D_MODEL), F32)),
        in_specs=[row, row, row, _resident(wo.shape), _resident((1, D_MODEL)), row],
        out_specs=(row, row, row, pl.BlockSpec((D_MIX, D_MODEL), lambda i: (0, 0)), vec, vec),
        compiler_params=_params(48, ("arbitrary",)),
    )(x, oc, oa, wo, norm_final, target)


def _conv_bwd(pc, doc, conv_w, norm_conv_out):
    seq = pc.shape[0]
    tile = TOK_TILE
    n_tiles = seq // tile

    def body(pc_ref, hcc_ref, hcu_ref, doc_ref, cw_ref, gn_ref, dpc_ref, gnc_ref, gcw_ref, zbuf, dbuf):
        step = pl.program_id(0)

        @pl.when(step == 0)
        def _():
            gnc_ref[...] = jnp.zeros_like(gnc_ref)
            gcw_ref[...] = jnp.zeros_like(gcw_ref)
            dbuf[tile:tile + 8, :] = jnp.zeros((8, D_CONV), F32)

        is_first_tile = step == n_tiles - 1
        zbuf[0:8, :] = jnp.where(is_first_tile, 0.0, hcc_ref[...] * hcu_ref[...])
        cb, cc, cu, z, z1, z2, conv = _conv_core(pc_ref, zbuf, cw_ref)
        gc = pc_ref[:, 3 * D_CONV:4 * D_CONV]
        silu, dsilu = _silu_and_grad(gc)
        yc = cb * conv
        r = _rstd(yc)
        xhat = yc * r
        do = doc_ref[...]
        dn = do * silu
        dpc_ref[:, 3 * D_CONV:4 * D_CONV] = (do * (xhat * gn_ref[...]) * dsilu).astype(BF16)
        gnc_ref[...] += jnp.sum(dn * xhat, axis=0, keepdims=True)
        dyc = _rms_bwd(dn * gn_ref[...], xhat, r)
        dpc_ref[:, 0:D_CONV] = (dyc * conv).astype(BF16)
        dconv = dyc * cb
        gcw_ref[0:1, :] += jnp.sum(dconv * z2, axis=0, keepdims=True)
        gcw_ref[1:2, :] += jnp.sum(dconv * z1, axis=0, keepdims=True)
        gcw_ref[2:3, :] += jnp.sum(dconv * z, axis=0, keepdims=True)
        dbuf[0:tile, :] = dconv
        dz = cw_ref[2:3, :] * dconv + cw_ref[1:2, :] * dbuf[1:tile + 1, :] + cw_ref[0:1, :] * dbuf[2:tile + 2, :]
        dpc_ref[:, D_CONV:2 * D_CONV] = (dz * cu).astype(BF16)
        dpc_ref[:, 2 * D_CONV:3 * D_CONV] = (dz * cc).astype(BF16)
        dbuf[tile:tile + 8, :] = dbuf[0:8, :]

    def rev(i):
        return n_tiles - 1 - i

    def halo(col_block):
        return pl.BlockSpec((8, D_CONV), lambda i: (jnp.maximum(rev(i) * (tile // 8) - 1, 0), col_block))

    return pl.pallas_call(
        body, name="conv_bwd", grid=(n_tiles,),
        out_shape=(jax.ShapeDtypeStruct((seq, D_PC), BF16), jax.ShapeDtypeStruct((1, D_CONV), F32),
                   jax.ShapeDtypeStruct((8, D_CONV), F32)),
        in_specs=[pl.BlockSpec((tile, D_PC), lambda i: (rev(i), 0)), halo(1), halo(2),
                  pl.BlockSpec((tile, D_CONV), lambda i: (rev(i), 0)), _resident(conv_w.shape), _resident((1, D_CONV))],
        out_specs=(pl.BlockSpec((tile, D_PC), lambda i: (rev(i), 0)), pl.BlockSpec((1, D_CONV), lambda i: (0, 0)),
                   pl.BlockSpec((8, D_CONV), lambda i: (0, 0))),
        scratch_shapes=[pltpu.VMEM((tile + 8, D_CONV), F32), pltpu.VMEM((tile + 8, D_CONV), F32)],
        compiler_params=_params(48, ("arbitrary",)),
    )(pc, pc, pc, doc, conv_w, norm_conv_out)


def _attn_bwd(q, kv, ga, ya, doa, sinks, norm_attn_out):
    seq = q.shape[0]
    n_blocks = seq // BLK

    def body(q_ref, kvc_ref, kvp_ref, ga_ref, ya_ref, doa_ref, sink_ref, gn_ref,
             dq_ref, dkv_ref, dga_ref, gna_ref, gs_ref, carry, dya_buf):
        step = pl.program_id(0)

        @pl.when(step == 0)
        def _():
            gna_ref[...] = jnp.zeros_like(gna_ref)
            gs_ref[...] = jnp.zeros_like(gs_ref)
            carry[...] = jnp.zeros_like(carry)

        ya = ya_ref[...]
        r = _rstd(ya)
        xhat = ya * r
        silu, dsilu = _silu_and_grad(ga_ref[...])
        do = doa_ref[...]
        dn = do * silu
        dga_ref[...] = (do * (xhat * gn_ref[...]) * dsilu).astype(BF16)
        gna_ref[...] += jnp.sum(dn * xhat, axis=0, keepdims=True)
        dya_buf[...] = _rms_bwd(dn * gn_ref[...], xhat, r).astype(BF16)

        dist, valid = _band_masks(n_blocks - 1 - step)
        kband = jnp.concatenate([kvp_ref[:, 0:D_KV], kvc_ref[:, 0:D_KV]], axis=0)
        vband = jnp.concatenate([kvp_ref[:, D_KV:2 * D_KV], kvc_ref[:, D_KV:2 * D_KV]], axis=0)
        lane = lax.broadcasted_iota(jnp.int32, (2 * BLK, 128), 1)
        sink_lane = lax.broadcasted_iota(jnp.int32, (1, 128), 1)
        gsink = jnp.zeros((1, 128), F32)
        dk_groups, dv_groups = [], []
        for group in range(N_HEADS // HEADS_PER_KV):
            kbd = _block_diag(kband, group)
            vbd = _block_diag(vband, group)
            acc_k = jnp.zeros((2 * BLK, 128), F32)
            acc_v = jnp.zeros((2 * BLK, 128), F32)
            for pair in range(group * 4, group * 4 + 4):
                qp, probs, sink_probs = _attn_pair_probs(q_ref, pair, kbd, sink_ref, dist, valid)
                dyp = dya_buf[:, pair * 128:(pair + 1) * 128]
                dp = _nt(dyp, vbd)
                ds_halves = []
                for u in range(2):
                    p_u = probs[:, u * 2 * BLK:(u + 1) * 2 * BLK]
                    dp_u = dp[:, u * 2 * BLK:(u + 1) * 2 * BLK]
                    delta = jnp.sum(p_u * dp_u, axis=-1, keepdims=True)
                    ds_halves.append(p_u * (dp_u - delta) * SCALE)
                    dsink = -jnp.sum(sink_probs[u] * delta, axis=0, keepdims=True)
                    gsink = gsink + jnp.where(sink_lane == 2 * pair + u, dsink, 0.0)
                ds = jnp.concatenate(ds_halves, axis=1).astype(BF16)
                dq_ref[:, pair * 128:(pair + 1) * 128] = _nn(ds, kbd).astype(BF16)
                dkbd = _tn(ds, qp)
                dvbd = _tn(probs.astype(BF16), dyp)
                acc_k += jnp.where(lane < 64, dkbd[0:2 * BLK], 0.0) + jnp.where(lane >= 64, dkbd[2 * BLK:4 * BLK], 0.0)
                acc_v += jnp.where(lane < 64, dvbd[0:2 * BLK], 0.0) + jnp.where(lane >= 64, dvbd[2 * BLK:4 * BLK], 0.0)
            dk_groups.append(acc_k + pltpu.roll(acc_k, 64, 1))
            dv_groups.append(acc_v + pltpu.roll(acc_v, 64, 1))
        gs_ref[...] += gsink
        dkv_band = jnp.concatenate([jnp.where(lane < 64, dk_groups[0], dk_groups[1]),
                                    jnp.where(lane < 64, dv_groups[0], dv_groups[1])], axis=1)
        dkv_ref[...] = (dkv_band[BLK:2 * BLK] + carry[...]).astype(BF16)
        carry[...] = dkv_band[0:BLK]

    row = pl.BlockSpec((BLK, D_ATTN), lambda i: (n_blocks - 1 - i, 0))
    kv_cur, kv_prev = _kv_specs(n_blocks, reverse=True)
    return pl.pallas_call(
        body, name="attn_bwd", grid=(n_blocks,),
        out_shape=(jax.ShapeDtypeStruct((seq, D_ATTN), BF16), jax.ShapeDtypeStruct((seq, 2 * D_KV), BF16),
                   jax.ShapeDtypeStruct((seq, D_ATTN), BF16), jax.ShapeDtypeStruct((1, D_ATTN), F32),
                   jax.ShapeDtypeStruct((1, 128), F32)),
        in_specs=[row, kv_cur, kv_prev, row, row, row, pl.BlockSpec(memory_space=pltpu.SMEM), _resident((1, D_ATTN))],
        out_specs=(row, kv_cur, row, pl.BlockSpec((1, D_ATTN), lambda i: (0, 0)), pl.BlockSpec((1, 128), lambda i: (0, 0))),
        scratch_shapes=[pltpu.VMEM((BLK, 2 * D_KV), F32), pltpu.VMEM((BLK, D_ATTN), BF16)],
        compiler_params=_params(32, ("arbitrary",)),
    )(q, kv, kv, ga, ya, doa, sinks, norm_attn_out)


def _dw_in(dpc, dq, dkv, dga, h):
    seq = h.shape[0]
    rows = 256
    first = (0, ROW_Q // rows, ROW_KV // rows, ROW_GA // rows, D_IN_PROJ // rows)

    def body(dpc_ref, dq_ref, dkv_ref, dga_ref, h_ref, out_ref):
        i = pl.program_id(0)
        for piece, ref in enumerate((dpc_ref, dq_ref, dkv_ref, dga_ref)):
            @pl.when((i >= first[piece]) & (i < first[piece + 1]))
            def _(ref=ref):
                out_ref[...] = _tn(ref[...], h_ref[...])

    def cols(piece):
        n = first[piece + 1] - first[piece]
        return pl.BlockSpec((seq, rows), lambda i: (0, jnp.clip(i - first[piece], 0, n - 1)))

    return pl.pallas_call(
        body, name="dw_in", grid=(D_IN_PROJ // rows,),
        out_shape=jax.ShapeDtypeStruct((D_IN_PROJ, D_MODEL), F32),
        in_specs=[cols(0), cols(1), cols(2), cols(3), _resident(h.shape)],
        out_specs=pl.BlockSpec((rows, D_MODEL), lambda i: (i, 0)),
        compiler_params=_params(32, ("arbitrary",)),
    )(dpc, dq, dkv, dga, h)


def _dh_gradx(dpc, dq, dkv, dga, wt, x, norm_in, dx2):
    seq = x.shape[0]
    tile = TOK_TILE

    def body(dpc_ref, dq_ref, dkv_ref, dga_ref, wt_ref, x_ref, g_ref, dx2_ref, gx_ref, gni_ref):
        @pl.when(pl.program_id(0) == 0)
        def _():
            gni_ref[...] = jnp.zeros_like(gni_ref)

        dh = (_nn(dpc_ref[...], wt_ref[0:ROW_Q, :]) + _nn(dq_ref[...], wt_ref[ROW_Q:ROW_KV, :])
              + _nn(dkv_ref[...], wt_ref[ROW_KV:ROW_GA, :]) + _nn(dga_ref[...], wt_ref[ROW_GA:D_IN_PROJ, :]))
        xv = x_ref[...]
        r = _rstd(xv)
        xhat = xv * r
        gni_ref[...] += jnp.sum(dh * xhat, axis=0, keepdims=True)
        gx_ref[...] = _rms_bwd(dh * g_ref[...], xhat, r) + dx2_ref[...]

    def row(width):
        return pl.BlockSpec((tile, width), lambda i: (i, 0))

    return pl.pallas_call(
        body, name="dh_gradx", grid=(seq // tile,),
        out_shape=(jax.ShapeDtypeStruct((seq, D_MODEL), F32), jax.ShapeDtypeStruct((1, D_MODEL), F32)),
        in_specs=[row(D_PC), row(D_ATTN), row(2 * D_KV), row(D_ATTN), _resident(wt.shape), row(D_MODEL),
                  _resident((1, D_MODEL)), row(D_MODEL)],
        out_specs=(row(D_MODEL), pl.BlockSpec((1, D_MODEL), lambda i: (0, 0))),
        compiler_params=_params(48, ("arbitrary",)),
    )(dpc, dq, dkv, dga, wt, x, norm_in, dx2)


def _accumulate(dst_ref, src_ref, rows=16):
    def step(i, carry):
        sl = pl.ds(pl.multiple_of(i * rows, rows), rows)
        dst_ref[sl, :] = dst_ref[sl, :] + src_ref[sl, :]
        return carry
    lax.fori_loop(0, dst_ref.shape[0] // rows, step, 0)


def _rs_grads(gwt, gwo, small):
    t_rows, o_rows = gwt.shape[0] // N_CHIPS, gwo.shape[0] // N_CHIPS
    t_half, o_half = t_rows // 2, o_rows // 2
    width = gwt.shape[1]

    def body(gwt_ref, gwo_ref, small_ref, gwt_sh, gwo_sh, small_sum, r1t, r1o, r2t, r2o, r3t, r3o,
             acc_t, acc_o, stage_t, stage_o, small_land, send_sems, recv_sems, local_sems):
        x, y, c = lax.axis_index("x"), lax.axis_index("y"), lax.axis_index("c")
        me = 4 * x + 2 * y + c
        j = 2 * x + y
        pa = (_xor(x, 1 - c), _xor(y, c), c)
        pb = (_xor(x, c), _xor(y, 1 - c), c)
        sib = (x, y, 1 - c)
        ja = 2 * pa[0] + pa[1]
        jb = 2 * pb[0] + pb[1]
        jd = 3 - j

        def rcopy(k, src, dst, to):
            return pltpu.make_async_remote_copy(src_ref=src, dst_ref=dst, send_sem=send_sems.at[k],
                                                recv_sem=recv_sems.at[k], device_id=to, device_id_type=MESH)

        def t_rows_of(chip, half):
            return gwt_ref.at[pl.ds(pl.multiple_of(chip * t_rows + half * t_half, 8), t_half), :]

        def o_rows_of(chip, half):
            return gwo_ref.at[pl.ds(pl.multiple_of(chip * o_rows + half * o_half, 8), o_half), :]

        small_land[me] = small_ref[...]
        others = [(dx, dy, dc) for dx in (0, 1) for dy in (0, 1) for dc in (0, 1)][1:]
        small_sends = [rcopy(k, small_ref, small_land.at[me], (_xor(x, dx), _xor(y, dy), _xor(c, dc)))
                       for k, (dx, dy, dc) in enumerate(others)]
        for cp in small_sends:
            cp.start()

        order = (ja, jd, jb, j)
        sib_order = (jb, jd, ja, j)
        pair_sends = []
        for s, chip in enumerate(sib_order):
            pair_sends += [rcopy(7 + s, t_rows_of(chip, 1 - c), r1t.at[s], sib),
                           rcopy(11 + s, o_rows_of(chip, 1 - c), r1o.at[s], sib)]
        for cp in pair_sends:
            cp.start()

        def load(src, dst, k):
            cp = pltpu.make_async_copy(src, dst, local_sems.at[k])
            cp.start()
            return cp

        def add_landed(s, mine_t, mine_o, land_t, land_o):
            cps = [load(mine_t, acc_t.at[s], 0), load(mine_o, acc_o.at[s], 1),
                   load(land_t, stage_t, 2), load(land_o, stage_o, 3)]
            for cp in cps:
                cp.wait()
            _accumulate(acc_t.at[s], stage_t)
            _accumulate(acc_o.at[s], stage_o)

        hop1 = []
        for s, chip in enumerate(order):
            rcopy(7 + s, t_rows_of(chip, c), r1t.at[s], sib).wait_recv()
            rcopy(11 + s, o_rows_of(chip, c), r1o.at[s], sib).wait_recv()
            add_landed(s, t_rows_of(chip, c), o_rows_of(chip, c), r1t.at[s], r1o.at[s])
            if s < 2:
                hop1 += [rcopy(15 + s, acc_t.at[s], r2t.at[s], pa), rcopy(17 + s, acc_o.at[s], r2o.at[s], pa)]
                hop1[-2].start()
                hop1[-1].start()

        def add_hop(s, land_t, land_o):
            cps = [load(land_t, stage_t, 2), load(land_o, stage_o, 3)]
            for cp in cps:
                cp.wait()
            _accumulate(acc_t.at[s], stage_t)
            _accumulate(acc_o.at[s], stage_o)

        rcopy(16, acc_t.at[1], r2t.at[1], pa).wait_recv()
        rcopy(18, acc_o.at[1], r2o.at[1], pa).wait_recv()
        add_hop(2, r2t.at[1], r2o.at[1])
        hop2 = [rcopy(19, acc_t.at[2], r3t, pb), rcopy(20, acc_o.at[2], r3o, pb)]
        for cp in hop2:
            cp.start()
        rcopy(15, acc_t.at[0], r2t.at[0], pa).wait_recv()
        rcopy(17, acc_o.at[0], r2o.at[0], pa).wait_recv()
        add_hop(3, r2t.at[0], r2o.at[0])
        rcopy(19, acc_t.at[2], r3t, pb).wait_recv()
        rcopy(20, acc_o.at[2], r3o, pb).wait_recv()
        add_hop(3, r3t, r3o)

        out_t = gwt_sh.at[pl.ds(pl.multiple_of(c * t_half, 8), t_half), :]
        out_o = gwo_sh.at[pl.ds(pl.multiple_of(c * o_half, 8), o_half), :]
        keep = [load(acc_t.at[3], out_t, 0), load(acc_o.at[3], out_o, 1)]
        swap = [rcopy(21, acc_t.at[3], out_t, sib), rcopy(22, acc_o.at[3], out_o, sib)]
        for cp in swap:
            cp.start()

        for cp in small_sends:
            cp.wait_recv()
        total = small_land[0]
        for dev in range(1, 8):
            total = total + small_land[dev]
        small_sum[...] = total

        rcopy(21, acc_t.at[3], gwt_sh.at[pl.ds(pl.multiple_of((1 - c) * t_half, 8), t_half), :], sib).wait_recv()
        rcopy(22, acc_o.at[3], gwo_sh.at[pl.ds(pl.multiple_of((1 - c) * o_half, 8), o_half), :], sib).wait_recv()
        for cp in small_sends + pair_sends + hop1 + hop2 + swap:
            cp.wait_send()
        for cp in keep:
            cp.wait()

    any_spec = pl.BlockSpec(memory_space=pl.ANY)
    vmem_spec = pl.BlockSpec(memory_space=pltpu.VMEM)
    outs = pl.pallas_call(
        body, name="rs_grads",
        out_shape=(jax.ShapeDtypeStruct((t_rows, width), F32), jax.ShapeDtypeStruct((o_rows, width), F32),
                   jax.ShapeDtypeStruct(small.shape, F32),
                   jax.ShapeDtypeStruct((4, t_half, width), F32), jax.ShapeDtypeStruct((4, o_half, width), F32),
                   jax.ShapeDtypeStruct((2, t_half, width), F32), jax.ShapeDtypeStruct((2, o_half, width), F32),
                   jax.ShapeDtypeStruct((t_half, width), F32), jax.ShapeDtypeStruct((o_half, width), F32)),
        in_specs=[any_spec, any_spec, vmem_spec],
        out_specs=(any_spec, any_spec, vmem_spec) + (any_spec,) * 6,
        scratch_shapes=[pltpu.VMEM((4, t_half, width), F32), pltpu.VMEM((4, o_half, width), F32),
                        pltpu.VMEM((t_half, width), F32), pltpu.VMEM((o_half, width), F32),
                        pltpu.VMEM((8,) + small.shape, F32),
                        pltpu.SemaphoreType.DMA((23,)), pltpu.SemaphoreType.DMA((23,)), pltpu.SemaphoreType.DMA((4,))],
        compiler_params=_params(40),
    )(gwt, gwo, small)
    return outs[0], outs[1], outs[2]


def _adamw(name, w, g, m, v, rows):
    def body(w_ref, g_ref, m_ref, v_ref, go_ref, d_ref, nm_ref, nv_ref):
        gv = g_ref[...]
        go_ref[...] = gv
        nm = ADAM_B1 * m_ref[...] + (1.0 - ADAM_B1) * gv
        nv = ADAM_B2 * v_ref[...] + (1.0 - ADAM_B2) * (gv * gv)
        m_hat = nm / (1.0 - ADAM_B1 ** ADAM_STEP)
        v_hat = nv / (1.0 - ADAM_B2 ** ADAM_STEP)
        d_ref[...] = -ADAM_LR * (m_hat / (jnp.sqrt(v_hat) + ADAM_EPS) + ADAM_WD * w_ref[...])
        nm_ref[...] = nm
        nv_ref[...] = nv

    spec = pl.BlockSpec((rows, w.shape[1]), lambda i: (i, 0))
    shape = jax.ShapeDtypeStruct(w.shape, F32)
    return pl.pallas_call(
        body, name="adamw_" + name, grid=(w.shape[0] // rows,),
        out_shape=(shape,) * 4, in_specs=[spec] * 4, out_specs=(spec,) * 4,
        compiler_params=_params(32, ("arbitrary",)),
    )(w, g, m, v)


def kernel(x, norm_in, w_in, conv_w, attn_sinks, norm_conv_out, norm_attn_out, w_out, norm_final, loss_target, m_norm_in, m_w_in, m_conv_w, m_attn_sinks, m_norm_conv_out, m_norm_attn_out, m_w_out, m_norm_final, v_norm_in, v_w_in, v_conv_w, v_attn_sinks, v_norm_conv_out, v_norm_attn_out, v_w_out, v_norm_final):
    chip = 2 * lax.axis_index("x") + lax.axis_index("y")
    xs, target = x[0], loss_target[0]
    norm_final2 = norm_final.reshape(1, D_MODEL)

    wt, wo, cw4 = _ag_weights(w_in[0].T, w_out[0], conv_w[0])
    cw = jnp.transpose(cw4, (1, 0, 2)).reshape(3, D_CONV)

    h, pc, q, kv, ga = _fwd_in(xs, norm_in, wt)
    oc = _conv_fwd(pc, cw, norm_conv_out)
    ya, oa = _attn_fwd(q, kv, ga, attn_sinks, norm_attn_out)
    dx2, doc, doa, gwo, gnf, loss_lanes = _out_proj_loss(xs, oc, oa, wo, norm_final2, target)

    dpc, gnc, gcw = _conv_bwd(pc, doc, cw, norm_conv_out)
    dq, dkv, dga, gna, gsink = _attn_bwd(q, kv, ga, ya, doa, attn_sinks, norm_attn_out)
    gwt = _dw_in(dpc, dq, dkv, dga, h)
    grad_x, gni = _dh_gradx(dpc, dq, dkv, dga, wt, xs, norm_in, dx2)

    last = jnp.zeros((1, D_MODEL), F32).at[:, 0:N_HEADS].set(gsink[:, 0:N_HEADS]).at[0, N_HEADS].set(jnp.sum(loss_lanes))
    small = jnp.concatenate([gni, gnc, gna, gnf, gcw[0:3], last], axis=0)
    gwt_sh, gwo_sh, small_sum = _rs_grads(gwt, gwo, small)

    loss = small_sum[7, N_HEADS]
    g_norm_in, g_norm_conv, g_norm_attn = small_sum[0:1], small_sum[1:2], small_sum[2:3]
    g_norm_final = small_sum[3]
    g_conv_w = lax.dynamic_slice(small_sum[4:7], (0, chip * (D_CONV // N_CHIPS)), (3, D_CONV // N_CHIPS))[None]
    g_sinks = small_sum[7:8, 0:N_HEADS]

    def adam(name, w, g, m, v, rows):
        shape = w.shape
        two_d = (-1, shape[-1])
        out = _adamw(name, w.reshape(two_d), g.reshape(two_d), m.reshape(two_d), v.reshape(two_d), rows)
        return tuple(o.reshape(shape) for o in out)

    weights = (norm_in, w_in, conv_w, attn_sinks, norm_conv_out, norm_attn_out, w_out, norm_final)
    grads = (g_norm_in, None, g_conv_w, g_sinks, g_norm_conv, g_norm_attn, gwo_sh[None], g_norm_final)
    moments_m = (m_norm_in, m_w_in, m_conv_w, m_attn_sinks, m_norm_conv_out, m_norm_attn_out, m_w_out, m_norm_final)
    moments_v = (v_norm_in, v_w_in, v_conv_w, v_attn_sinks, v_norm_conv_out, v_norm_attn_out, v_w_out, v_norm_final)
    rows = (1, None, 3, 1, 1, 1, 128, 1)
    names = ("norm_in", "w_in", "conv_w", "attn_sinks", "norm_conv_out", "norm_attn_out", "w_out", "norm_final")
    updates = []
    for args in zip(names, weights, grads, moments_m, moments_v, rows):
        if args[0] == "w_in":
            out_t = _adamw("w_in", w_in[0].T, gwt_sh, m_w_in[0].T, v_w_in[0].T, 200)
            updates.append(tuple(o.T[None] for o in out_t))
        else:
            updates.append(adam(*args))
    grads_out, deltas, new_m, new_v = zip(*updates)
    return (loss, grad_x[None], *grads_out, *deltas, *new_m, *new_v)
```

```python
import jax
import jax.numpy as jnp
from jax import lax
from jax.experimental import pallas as pl
from jax.experimental.pallas import tpu as pltpu

F32 = jnp.float32
BF16 = jnp.bfloat16
MESH = pl.DeviceIdType.MESH

D_MODEL = 1024
D_CONV = 1024
D_ATTN = 1024
D_KV = 128
D_MIX = D_CONV + D_ATTN
D_PC = 4 * D_CONV
D_IN_PROJ = D_PC + 2 * D_ATTN + 2 * D_KV
ROW_Q = D_PC
ROW_KV = ROW_Q + D_ATTN
ROW_GA = ROW_KV + 2 * D_KV
N_HEADS = 16
HEAD_DIM = 64
HEADS_PER_KV = 8
BLK = 128
N_CHIPS = 4
RMS_EPS = 1e-5
SCALE = HEAD_DIM ** -0.5
SLOPES = tuple(2.0 ** (-8.0 * (h + 1) / N_HEADS) for h in range(N_HEADS))

ADAM_LR, ADAM_B1, ADAM_B2, ADAM_EPS, ADAM_WD, ADAM_STEP = 0.001, 0.9, 0.999, 1e-08, 0.01, 10

TOK_TILE = 256
MIB = 1 << 20


def _params(vmem_mib, semantics=None):
    return pltpu.CompilerParams(dimension_semantics=semantics, vmem_limit_bytes=vmem_mib * MIB)


def _nn(a, b):
    return jnp.dot(a, b, preferred_element_type=F32)


def _nt(a, b):
    return lax.dot_general(a, b, (((1,), (1,)), ((), ())), preferred_element_type=F32)


def _tn(a, b):
    return lax.dot_general(a, b, (((0,), (0,)), ((), ())), preferred_element_type=F32)


def _rstd(v):
    return lax.rsqrt(jnp.mean(v * v, axis=-1, keepdims=True) + RMS_EPS)


def _rms_bwd(g, xhat, rstd):
    return rstd * (g - xhat * jnp.mean(g * xhat, axis=-1, keepdims=True))


def _silu_and_grad(g):
    s = jax.nn.sigmoid(g)
    return g * s, s * (1.0 + g * (1.0 - s))


def _resident(shape):
    return pl.BlockSpec(shape, lambda *_: (0,) * len(shape), pipeline_mode=pl.Buffered(1))


def _xor(a, b):
    return a + b - 2 * a * b


def _cast_rows(src_ref, dst_ref, rows=32):
    def step(i, carry):
        sl = pl.ds(pl.multiple_of(i * rows, rows), rows)
        dst_ref[sl, :] = src_ref[sl, :].astype(dst_ref.dtype)
        return carry
    lax.fori_loop(0, src_ref.shape[0] // rows, step, 0)


def _ag_weights(wt_sh, wo_sh, cw_sh):
    wt_rows, wo_rows = wt_sh.shape[0], wo_sh.shape[0]
    wt_half, wo_half = wt_rows // 2, wo_rows // 2
    width = wt_sh.shape[1]

    def body(wt_ref, wo_ref, cw_ref, wt_out, wo_out, cw_out, f32_t, f32_o, own_t, own_o, land_t, land_o,
             send_sems, recv_sems, local_sems):
        x, y, c = lax.axis_index("x"), lax.axis_index("y"), lax.axis_index("c")
        j = 2 * x + y
        p1 = (_xor(x, c), _xor(y, 1 - c), c)
        p2 = (_xor(x, 1 - c), _xor(y, c), c)
        sib = (x, y, 1 - c)
        j1 = 2 * p1[0] + p1[1]
        j2 = 2 * p2[0] + p2[1]
        j3 = 3 - j

        def wt_rows_of(chip, half):
            return wt_out.at[pl.ds(pl.multiple_of(chip * wt_rows + half * wt_half, 16), wt_half), :]

        def wo_rows_of(chip, half):
            return wo_out.at[pl.ds(pl.multiple_of(chip * wo_rows + half * wo_half, 16), wo_half), :]

        def rcopy(k, src, dst, to):
            return pltpu.make_async_remote_copy(src_ref=src, dst_ref=dst, send_sem=send_sems.at[k],
                                                recv_sem=recv_sems.at[k], device_id=to, device_id_type=MESH)

        def lcopy(k, src, dst):
            cp = pltpu.make_async_copy(src, dst, local_sems.at[k])
            cp.start()
            return cp

        loads = [lcopy(0, wt_ref, f32_t), lcopy(1, wo_ref, f32_o)]
        local = [lcopy(2, cw_ref, cw_out.at[j])]
        for cp in loads:
            cp.wait()
        _cast_rows(f32_t, own_t)
        _cast_rows(f32_o, own_o)
        local += [lcopy(0, own_t, wt_out.at[pl.ds(pl.multiple_of(j * wt_rows, 16), wt_rows), :]),
                  lcopy(1, own_o, wo_out.at[pl.ds(pl.multiple_of(j * wo_rows, 16), wo_rows), :])]
        half_t = own_t.at[pl.ds(pl.multiple_of(c * wt_half, 16), wt_half), :]
        half_o = own_o.at[pl.ds(pl.multiple_of(c * wo_half, 16), wo_half), :]

        def publish(slot, chip, k_t, k_o, k_local):
            sends = [rcopy(k_t, land_t.at[slot], wt_rows_of(chip, c), sib), rcopy(k_o, land_o.at[slot], wo_rows_of(chip, c), sib)]
            for cp in sends:
                cp.start()
            local.extend([lcopy(k_local, land_t.at[slot], wt_rows_of(chip, c)),
                          lcopy(k_local + 1, land_o.at[slot], wo_rows_of(chip, c))])
            return sends

        hop1 = [rcopy(0, half_t, land_t.at[0], p1), rcopy(1, half_o, land_o.at[0], p1), rcopy(2, cw_ref, cw_out.at[j], p1)]
        for cp in hop1:
            cp.start()
        rcopy(0, half_t, land_t.at[0], p1).wait_recv()
        rcopy(1, half_o, land_o.at[0], p1).wait_recv()
        rcopy(2, cw_ref, cw_out.at[j1], p1).wait_recv()
        hop2 = [rcopy(3, half_t, land_t.at[1], p2), rcopy(5, half_o, land_o.at[1], p2), rcopy(7, cw_ref, cw_out.at[j], p2),
                rcopy(4, land_t.at[0], land_t.at[2], p2), rcopy(6, land_o.at[0], land_o.at[2], p2),
                rcopy(8, cw_out.at[j1], cw_out.at[j1], p2)]
        for cp in hop2:
            cp.start()
        swaps = publish(0, j1, 9, 12, 3)
        rcopy(3, half_t, land_t.at[1], p2).wait_recv()
        rcopy(5, half_o, land_o.at[1], p2).wait_recv()
        rcopy(7, cw_ref, cw_out.at[j2], p2).wait_recv()
        swaps += publish(1, j2, 10, 13, 5)
        rcopy(4, half_t, land_t.at[2], p2).wait_recv()
        rcopy(6, half_o, land_o.at[2], p2).wait_recv()
        rcopy(8, cw_ref, cw_out.at[j3], p2).wait_recv()
        swaps += publish(2, j3, 11, 14, 7)
        for k, chip in ((9, j2), (10, j1), (11, j3)):
            rcopy(k, half_t, wt_rows_of(chip, 1 - c), sib).wait_recv()
        for k, chip in ((12, j2), (13, j1), (14, j3)):
            rcopy(k, half_o, wo_rows_of(chip, 1 - c), sib).wait_recv()
        for cp in hop1 + hop2 + swaps:
            cp.wait_send()
        for cp in local:
            cp.wait()

    any_spec = pl.BlockSpec(memory_space=pl.ANY)
    return pl.pallas_call(
        body, name="ag_weights",
        out_shape=(jax.ShapeDtypeStruct((N_CHIPS * wt_rows, width), BF16),
                   jax.ShapeDtypeStruct((N_CHIPS * wo_rows, width), BF16),
                   jax.ShapeDtypeStruct((N_CHIPS,) + cw_sh.shape, cw_sh.dtype)),
        in_specs=[any_spec, any_spec, any_spec],
        out_specs=(any_spec, any_spec, any_spec),
        scratch_shapes=[pltpu.VMEM((wt_rows, width), F32), pltpu.VMEM((wo_rows, width), F32),
                        pltpu.VMEM((wt_rows, width), BF16), pltpu.VMEM((wo_rows, width), BF16),
                        pltpu.VMEM((3, wt_half, width), BF16), pltpu.VMEM((3, wo_half, width), BF16),
                        pltpu.SemaphoreType.DMA((15,)), pltpu.SemaphoreType.DMA((15,)), pltpu.SemaphoreType.DMA((9,))],
        compiler_params=_params(32),
    )(wt_sh, wo_sh, cw_sh)


def _fwd_in(x, norm_in, wt):
    seq = x.shape[0]
    tile = TOK_TILE

    def body(x_ref, g_ref, wt_ref, h_ref, pc_ref, q_ref, kv_ref, ga_ref):
        xv = x_ref[...]
        h = (xv * _rstd(xv) * g_ref[...]).astype(BF16)
        h_ref[...] = h
        for blk in range(D_PC // D_CONV):
            pc_ref[:, blk * D_CONV:(blk + 1) * D_CONV] = _nt(h, wt_ref[blk * D_CONV:(blk + 1) * D_CONV, :])
        q_ref[...] = _nt(h, wt_ref[ROW_Q:ROW_KV, :])
        kv_ref[...] = _nt(h, wt_ref[ROW_KV:ROW_GA, :])
        ga_ref[...] = _nt(h, wt_ref[ROW_GA:D_IN_PROJ, :])

    def row(width):
        return pl.BlockSpec((tile, width), lambda i: (i, 0))

    return pl.pallas_call(
        body, name="fwd_in", grid=(seq // tile,),
        out_shape=(jax.ShapeDtypeStruct((seq, D_MODEL), BF16), jax.ShapeDtypeStruct((seq, D_PC), F32),
                   jax.ShapeDtypeStruct((seq, D_ATTN), F32), jax.ShapeDtypeStruct((seq, 2 * D_KV), F32),
                   jax.ShapeDtypeStruct((seq, D_ATTN), F32)),
        in_specs=[row(D_MODEL), _resident((1, D_MODEL)), _resident(wt.shape)],
        out_specs=(row(D_MODEL), row(D_PC), row(D_ATTN), row(2 * D_KV), row(D_ATTN)),
        compiler_params=_params(48, ("arbitrary",)),
    )(x, norm_in, wt)


def _conv_core(pc_ref, zbuf, cw_ref):
    tile = pc_ref.shape[0]
    cb = pc_ref[:, 0:D_CONV]
    cc = pc_ref[:, D_CONV:2 * D_CONV]
    cu = pc_ref[:, 2 * D_CONV:3 * D_CONV]
    z = cc * cu
    zbuf[8:tile + 8, :] = z
    z1 = zbuf[7:tile + 7, :]
    z2 = zbuf[6:tile + 6, :]
    conv = cw_ref[0:1, :] * z2 + cw_ref[1:2, :] * z1 + cw_ref[2:3, :] * z
    return cb, cc, cu, z, z1, z2, conv


def _conv_fwd(pc, conv_w, norm_conv_out):
    seq = pc.shape[0]
    tile = TOK_TILE

    def body(pc_ref, cw_ref, gn_ref, oc_ref, zbuf):
        @pl.when(pl.program_id(0) == 0)
        def _():
            zbuf[0:8, :] = jnp.zeros((8, D_CONV), F32)

        cb, _, _, _, _, _, conv = _conv_core(pc_ref, zbuf, cw_ref)
        yc = cb * conv
        silu, _ = _silu_and_grad(pc_ref[:, 3 * D_CONV:4 * D_CONV])
        oc_ref[...] = (yc * _rstd(yc) * gn_ref[...] * silu).astype(BF16)
        zbuf[0:8, :] = zbuf[tile:tile + 8, :]

    return pl.pallas_call(
        body, name="conv_fwd", grid=(seq // tile,),
        out_shape=jax.ShapeDtypeStruct((seq, D_CONV), BF16),
        in_specs=[pl.BlockSpec((tile, D_PC), lambda i: (i, 0)), _resident(conv_w.shape), _resident((1, D_CONV))],
        out_specs=pl.BlockSpec((tile, D_CONV), lambda i: (i, 0)),
        scratch_shapes=[pltpu.VMEM((tile + 8, D_CONV), F32)],
        compiler_params=_params(40, ("arbitrary",)),
    )(pc, conv_w, norm_conv_out)


def _band_masks(block_index):
    qi = lax.broadcasted_iota(jnp.int32, (BLK, 2 * BLK), 0)
    kj = lax.broadcasted_iota(jnp.int32, (BLK, 2 * BLK), 1)
    dist = BLK + qi - kj
    valid = (dist >= 0) & (dist < BLK) & ((kj >= BLK) | (block_index > 0))
    return dist.astype(F32), valid


def _block_diag(band, group):
    lane = lax.broadcasted_iota(jnp.int32, band.shape, 1)
    other = pltpu.roll(band, 64, 1)
    lo, hi = (band, other) if group == 0 else (other, band)
    zero = jnp.zeros_like(band)
    return jnp.concatenate([jnp.where(lane < 64, lo, zero), jnp.where(lane >= 64, hi, zero)], axis=0).astype(BF16)


def _softmax_head(s_raw, head, sink, dist, valid):
    sc = jnp.where(valid, s_raw * SCALE - SLOPES[head] * dist, -jnp.inf)
    m = jnp.maximum(jnp.max(sc, axis=-1, keepdims=True), sink)
    p = jnp.exp(sc - m)
    es = jnp.exp(sink - m)
    inv = 1.0 / (jnp.sum(p, axis=-1, keepdims=True) + es)
    return p * inv, es * inv


def _attn_pair_probs(q_ref, pair, kbd, sink_ref, dist, valid):
    qp = q_ref[:, pair * 128:(pair + 1) * 128].astype(BF16)
    s = _nt(qp, kbd)
    pa, sa = _softmax_head(s[:, 0:2 * BLK], 2 * pair, sink_ref[0, 2 * pair], dist, valid)
    pb, sb = _softmax_head(s[:, 2 * BLK:4 * BLK], 2 * pair + 1, sink_ref[0, 2 * pair + 1], dist, valid)
    return qp, jnp.concatenate([pa, pb], axis=1), (sa, sb)


def _kv_specs(n_blocks, reverse):
    def blk(i):
        return (n_blocks - 1 - i) if reverse else i
    cur = pl.BlockSpec((BLK, 2 * D_KV), lambda i: (blk(i), 0))
    prev = pl.BlockSpec((BLK, 2 * D_KV), lambda i: (jnp.maximum(blk(i) - 1, 0), 0))
    return cur, prev


def _attn_fwd(q, kv, ga, sinks, norm_attn_out):
    seq = q.shape[0]
    n_blocks = seq // BLK

    def body(q_ref, kvc_ref, kvp_ref, ga_ref, sink_ref, gn_ref, ya_ref, oa_ref):
        dist, valid = _band_masks(pl.program_id(0))
        kband = jnp.concatenate([kvp_ref[:, 0:D_KV], kvc_ref[:, 0:D_KV]], axis=0)
        vband = jnp.concatenate([kvp_ref[:, D_KV:2 * D_KV], kvc_ref[:, D_KV:2 * D_KV]], axis=0)
        for group in range(N_HEADS // HEADS_PER_KV):
            kbd = _block_diag(kband, group)
            vbd = _block_diag(vband, group)
            for pair in range(group * 4, group * 4 + 4):
                _, probs, _ = _attn_pair_probs(q_ref, pair, kbd, sink_ref, dist, valid)
                ya_ref[:, pair * 128:(pair + 1) * 128] = _nn(probs.astype(BF16), vbd)
        ya = ya_ref[...]
        silu, _ = _silu_and_grad(ga_ref[...])
        oa_ref[...] = (ya * _rstd(ya) * gn_ref[...] * silu).astype(BF16)

    row = pl.BlockSpec((BLK, D_ATTN), lambda i: (i, 0))
    kv_cur, kv_prev = _kv_specs(n_blocks, reverse=False)
    return pl.pallas_call(
        body, name="attn_fwd", grid=(n_blocks,),
        out_shape=(jax.ShapeDtypeStruct((seq, D_ATTN), F32), jax.ShapeDtypeStruct((seq, D_ATTN), BF16)),
        in_specs=[row, kv_cur, kv_prev, row, pl.BlockSpec(memory_space=pltpu.SMEM), _resident((1, D_ATTN))],
        out_specs=(row, row),
        compiler_params=_params(32, ("arbitrary",)),
    )(q, kv, kv, ga, sinks, norm_attn_out)


def _out_proj_loss(x, oc, oa, wo, norm_final, target):
    seq = x.shape[0]
    tile = TOK_TILE

    def body(x_ref, oc_ref, oa_ref, wo_ref, gf_ref, t_ref, dx2_ref, doc_ref, doa_ref, gwo_ref, gnf_ref, loss_ref):
        @pl.when(pl.program_id(0) == 0)
        def _():
            gwo_ref[...] = jnp.zeros_like(gwo_ref)
            gnf_ref[...] = jnp.zeros_like(gnf_ref)
            loss_ref[...] = jnp.zeros_like(loss_ref)

        oc, oa = oc_ref[...], oa_ref[...]
        x2 = x_ref[...] + _nn(oc, wo_ref[0:D_CONV, :]) + _nn(oa, wo_ref[D_CONV:D_MIX, :])
        r = _rstd(x2)
        xhat = x2 * r
        err = xhat * gf_ref[...] - t_ref[...]
        loss_ref[...] += jnp.sum(err * err, axis=0, keepdims=True) * (0.5 / D_MODEL)
        dy = err * (1.0 / D_MODEL)
        gnf_ref[...] += jnp.sum(dy * xhat, axis=0, keepdims=True)
        dx2 = _rms_bwd(dy * gf_ref[...], xhat, r)
        dx2_ref[...] = dx2
        db = dx2.astype(BF16)
        doc_ref[...] = _nt(db, wo_ref[0:D_CONV, :])
        doa_ref[...] = _nt(db, wo_ref[D_CONV:D_MIX, :])
        gwo_ref[0:D_CONV, :] += _tn(oc, db)
        gwo_ref[D_CONV:D_MIX, :] += _tn(oa, db)

    row = pl.BlockSpec((tile, D_MODEL), lambda i: (i, 0))
    vec = pl.BlockSpec((1, D_MODEL), lambda i: (0, 0))
    return pl.pallas_call(
        body, name="out_proj_loss", grid=(seq // tile,),
        out_shape=(jax.ShapeDtypeStruct((seq, D_MODEL), F32), jax.ShapeDtypeStruct((seq, D_CONV), F32),
                   jax.ShapeDtypeStruct((seq, D_ATTN), F32), jax.ShapeDtypeStruct((D_MIX, D_MODEL), F32),
                   jax.ShapeDtypeStruct((1, D_MODEL), F32), jax.ShapeDtypeStruct((1, D_MODEL), F32)),
        in_specs=[row, row, row, _resident(wo.shape), _resident((1, D_MODEL)), row],
        out_specs=(row, row, row, pl.BlockSpec((D_MIX, D_MODEL), lambda i: (0, 0)), vec, vec),
        compiler_params=_params(48, ("arbitrary",)),
    )(x, oc, oa, wo, norm_final, target)


def _conv_bwd(pc, doc, conv_w, norm_conv_out):
    seq = pc.shape[0]
    tile = TOK_TILE
    n_tiles = seq // tile

    def body(pc_ref, hcc_ref, hcu_ref, doc_ref, cw_ref, gn_ref, dpc_ref, gnc_ref, gcw_ref, zbuf, dbuf):
        step = pl.program_id(0)

        @pl.when(step == 0)
        def _():
            gnc_ref[...] = jnp.zeros_like(gnc_ref)
            gcw_ref[...] = jnp.zeros_like(gcw_ref)
            dbuf[tile:tile + 8, :] = jnp.zeros((8, D_CONV), F32)

        is_first_tile = step == n_tiles - 1
        zbuf[0:8, :] = jnp.where(is_first_tile, 0.0, hcc_ref[...] * hcu_ref[...])
        cb, cc, cu, z, z1, z2, conv = _conv_core(pc_ref, zbuf, cw_ref)
        gc = pc_ref[:, 3 * D_CONV:4 * D_CONV]
        silu, dsilu = _silu_and_grad(gc)
        yc = cb * conv
        r = _rstd(yc)
        xhat = yc * r
        do = doc_ref[...]
        dn = do * silu
        dpc_ref[:, 3 * D_CONV:4 * D_CONV] = (do * (xhat * gn_ref[...]) * dsilu).astype(BF16)
        gnc_ref[...] += jnp.sum(dn * xhat, axis=0, keepdims=True)
        dyc = _rms_bwd(dn * gn_ref[...], xhat, r)
        dpc_ref[:, 0:D_CONV] = (dyc * conv).astype(BF16)
        dconv = dyc * cb
        gcw_ref[0:1, :] += jnp.sum(dconv * z2, axis=0, keepdims=True)
        gcw_ref[1:2, :] += jnp.sum(dconv * z1, axis=0, keepdims=True)
        gcw_ref[2:3, :] += jnp.sum(dconv * z, axis=0, keepdims=True)
        dbuf[0:tile, :] = dconv
        dz = cw_ref[2:3, :] * dconv + cw_ref[1:2, :] * dbuf[1:tile + 1, :] + cw_ref[0:1, :] * dbuf[2:tile + 2, :]
        dpc_ref[:, D_CONV:2 * D_CONV] = (dz * cu).astype(BF16)
        dpc_ref[:, 2 * D_CONV:3 * D_CONV] = (dz * cc).astype(BF16)
        dbuf[tile:tile + 8, :] = dbuf[0:8, :]

    def rev(i):
        return n_tiles - 1 - i

    def halo(col_block):
        return pl.BlockSpec((8, D_CONV), lambda i: (jnp.maximum(rev(i) * (tile // 8) - 1, 0), col_block))

    return pl.pallas_call(
        body, name="conv_bwd", grid=(n_tiles,),
        out_shape=(jax.ShapeDtypeStruct((seq, D_PC), BF16), jax.ShapeDtypeStruct((1, D_CONV), F32),
                   jax.ShapeDtypeStruct((8, D_CONV), F32)),
        in_specs=[pl.BlockSpec((tile, D_PC), lambda i: (rev(i), 0)), halo(1), halo(2),
                  pl.BlockSpec((tile, D_CONV), lambda i: (rev(i), 0)), _resident(conv_w.shape), _resident((1, D_CONV))],
        out_specs=(pl.BlockSpec((tile, D_PC), lambda i: (rev(i), 0)), pl.BlockSpec((1, D_CONV), lambda i: (0, 0)),
                   pl.BlockSpec((8, D_CONV), lambda i: (0, 0))),
        scratch_shapes=[pltpu.VMEM((tile + 8, D_CONV), F32), pltpu.VMEM((tile + 8, D_CONV), F32)],
        compiler_params=_params(48, ("arbitrary",)),
    )(pc, pc, pc, doc, conv_w, norm_conv_out)


def _attn_bwd(q, kv, ga, ya, doa, sinks, norm_attn_out):
    seq = q.shape[0]
    n_blocks = seq // BLK

    def body(q_ref, kvc_ref, kvp_ref, ga_ref, ya_ref, doa_ref, sink_ref, gn_ref,
             dq_ref, dkv_ref, dga_ref, gna_ref, gs_ref, carry, dya_buf):
        step = pl.program_id(0)

        @pl.when(step == 0)
        def _():
            gna_ref[...] = jnp.zeros_like(gna_ref)
            gs_ref[...] = jnp.zeros_like(gs_ref)
            carry[...] = jnp.zeros_like(carry)

        ya = ya_ref[...]
        r = _rstd(ya)
        xhat = ya * r
        silu, dsilu = _silu_and_grad(ga_ref[...])
        do = doa_ref[...]
        dn = do * silu
        dga_ref[...] = (do * (xhat * gn_ref[...]) * dsilu).astype(BF16)
        gna_ref[...] += jnp.sum(dn * xhat, axis=0, keepdims=True)
        dya_buf[...] = _rms_bwd(dn * gn_ref[...], xhat, r).astype(BF16)

        dist, valid = _band_masks(n_blocks - 1 - step)
        kband = jnp.concatenate([kvp_ref[:, 0:D_KV], kvc_ref[:, 0:D_KV]], axis=0)
        vband = jnp.concatenate([kvp_ref[:, D_KV:2 * D_KV], kvc_ref[:, D_KV:2 * D_KV]], axis=0)
        lane = lax.broadcasted_iota(jnp.int32, (2 * BLK, 128), 1)
        sink_lane = lax.broadcasted_iota(jnp.int32, (1, 128), 1)
        gsink = jnp.zeros((1, 128), F32)
        dk_groups, dv_groups = [], []
        for group in range(N_HEADS // HEADS_PER_KV):
            kbd = _block_diag(kband, group)
            vbd = _block_diag(vband, group)
            acc_k = jnp.zeros((2 * BLK, 128), F32)
            acc_v = jnp.zeros((2 * BLK, 128), F32)
            for pair in range(group * 4, group * 4 + 4):
                qp, probs, sink_probs = _attn_pair_probs(q_ref, pair, kbd, sink_ref, dist, valid)
                dyp = dya_buf[:, pair * 128:(pair + 1) * 128]
                dp = _nt(dyp, vbd)
                ds_halves = []
                for u in range(2):
                    p_u = probs[:, u * 2 * BLK:(u + 1) * 2 * BLK]
                    dp_u = dp[:, u * 2 * BLK:(u + 1) * 2 * BLK]
                    delta = jnp.sum(p_u * dp_u, axis=-1, keepdims=True)
                    ds_halves.append(p_u * (dp_u - delta) * SCALE)
                    dsink = -jnp.sum(sink_probs[u] * delta, axis=0, keepdims=True)
                    gsink = gsink + jnp.where(sink_lane == 2 * pair + u, dsink, 0.0)
                ds = jnp.concatenate(ds_halves, axis=1).astype(BF16)
                dq_ref[:, pair * 128:(pair + 1) * 128] = _nn(ds, kbd).astype(BF16)
                dkbd = _tn(ds, qp)
                dvbd = _tn(probs.astype(BF16), dyp)
                acc_k += jnp.where(lane < 64, dkbd[0:2 * BLK], 0.0) + jnp.where(lane >= 64, dkbd[2 * BLK:4 * BLK], 0.0)
                acc_v += jnp.where(lane < 64, dvbd[0:2 * BLK], 0.0) + jnp.where(lane >= 64, dvbd[2 * BLK:4 * BLK], 0.0)
            dk_groups.append(acc_k + pltpu.roll(acc_k, 64, 1))
            dv_groups.append(acc_v + pltpu.roll(acc_v, 64, 1))
        gs_ref[...] += gsink
        dkv_band = jnp.concatenate([jnp.where(lane < 64, dk_groups[0], dk_groups[1]),
                                    jnp.where(lane < 64, dv_groups[0], dv_groups[1])], axis=1)
        dkv_ref[...] = (dkv_band[BLK:2 * BLK] + carry[...]).astype(BF16)
        carry[...] = dkv_band[0:BLK]

    row = pl.BlockSpec((BLK, D_ATTN), lambda i: (n_blocks - 1 - i, 0))
    kv_cur, kv_prev = _kv_specs(n_blocks, reverse=True)
    return pl.pallas_call(
        body, name="attn_bwd", grid=(n_blocks,),
        out_shape=(jax.ShapeDtypeStruct((seq, D_ATTN), BF16), jax.ShapeDtypeStruct((seq, 2 * D_KV), BF16),
                   jax.ShapeDtypeStruct((seq, D_ATTN), BF16), jax.ShapeDtypeStruct((1, D_ATTN), F32),
                   jax.ShapeDtypeStruct((1, 128), F32)),
        in_specs=[row, kv_cur, kv_prev, row, row, row, pl.BlockSpec(memory_space=pltpu.SMEM), _resident((1, D_ATTN))],
        out_specs=(row, kv_cur, row, pl.BlockSpec((1, D_ATTN), lambda i: (0, 0)), pl.BlockSpec((1, 128), lambda i: (0, 0))),
        scratch_shapes=[pltpu.VMEM((BLK, 2 * D_KV), F32), pltpu.VMEM((BLK, D_ATTN), BF16)],
        compiler_params=_params(32, ("arbitrary",)),
    )(q, kv, kv, ga, ya, doa, sinks, norm_attn_out)


def _dw_in(dpc, dq, dkv, dga, h):
    seq = h.shape[0]
    rows = 256
    first = (0, ROW_Q // rows, ROW_KV // rows, ROW_GA // rows, D_IN_PROJ // rows)

    def body(dpc_ref, dq_ref, dkv_ref, dga_ref, h_ref, out_ref):
        i = pl.program_id(0)
        for piece, ref in enumerate((dpc_ref, dq_ref, dkv_ref, dga_ref)):
            @pl.when((i >= first[piece]) & (i < first[piece + 1]))
            def _(ref=ref):
                out_ref[...] = _tn(ref[...], h_ref[...])

    def cols(piece):
        n = first[piece + 1] - first[piece]
        return pl.BlockSpec((seq, rows), lambda i: (0, jnp.clip(i - first[piece], 0, n - 1)))

    return pl.pallas_call(
        body, name="dw_in", grid=(D_IN_PROJ // rows,),
        out_shape=jax.ShapeDtypeStruct((D_IN_PROJ, D_MODEL), F32),
        in_specs=[cols(0), cols(1), cols(2), cols(3), _resident(h.shape)],
        out_specs=pl.BlockSpec((rows, D_MODEL), lambda i: (i, 0)),
        compiler_params=_params(32, ("arbitrary",)),
    )(dpc, dq, dkv, dga, h)


def _dh_gradx(dpc, dq, dkv, dga, wt, x, norm_in, dx2):
    seq = x.shape[0]
    tile = TOK_TILE

    def body(dpc_ref, dq_ref, dkv_ref, dga_ref, wt_ref, x_ref, g_ref, dx2_ref, gx_ref, gni_ref):
        @pl.when(pl.program_id(0) == 0)
        def _():
            gni_ref[...] = jnp.zeros_like(gni_ref)

        dh = (_nn(dpc_ref[...], wt_ref[0:ROW_Q, :]) + _nn(dq_ref[...], wt_ref[ROW_Q:ROW_KV, :])
              + _nn(dkv_ref[...], wt_ref[ROW_KV:ROW_GA, :]) + _nn(dga_ref[...], wt_ref[ROW_GA:D_IN_PROJ, :]))
        xv = x_ref[...]
        r = _rstd(xv)
        xhat = xv * r
        gni_ref[...] += jnp.sum(dh * xhat, axis=0, keepdims=True)
        gx_ref[...] = _rms_bwd(dh * g_ref[...], xhat, r) + dx2_ref[...]

    def row(width):
        return pl.BlockSpec((tile, width), lambda i: (i, 0))

    return pl.pallas_call(
        body, name="dh_gradx", grid=(seq // tile,),
        out_shape=(jax.ShapeDtypeStruct((seq, D_MODEL), F32), jax.ShapeDtypeStruct((1, D_MODEL), F32)),
        in_specs=[row(D_PC), row(D_ATTN), row(2 * D_KV), row(D_ATTN), _resident(wt.shape), row(D_MODEL),
                  _resident((1, D_MODEL)), row(D_MODEL)],
        out_specs=(row(D_MODEL), pl.BlockSpec((1, D_MODEL), lambda i: (0, 0))),
        compiler_params=_params(48, ("arbitrary",)),
    )(dpc, dq, dkv, dga, wt, x, norm_in, dx2)


def _accumulate(dst_ref, src_ref, rows=16):
    def step(i, carry):
        sl = pl.ds(pl.multiple_of(i * rows, rows), rows)
        dst_ref[sl, :] = dst_ref[sl, :] + src_ref[sl, :].astype(F32)
        return carry
    lax.fori_loop(0, dst_ref.shape[0] // rows, step, 0)


def _rs_grads(gwt, gwo, small):
    t_rows, o_rows = gwt.shape[0] // N_CHIPS, gwo.shape[0] // N_CHIPS
    t_half, o_half = t_rows // 2, o_rows // 2
    width = gwt.shape[1]

    def body(gwt_ref, gwo_ref, small_ref, gwt_sh, gwo_sh, small_sum,
             acc_t, acc_o, stage_t, stage_o, sb_t, sb_o, r1t, r1o, r2t, r2o, r3t, r3o,
             small_land, send_sems, recv_sems, local_sems):
        x, y, c = lax.axis_index("x"), lax.axis_index("y"), lax.axis_index("c")
        me = 4 * x + 2 * y + c
        j = 2 * x + y
        pa = (_xor(x, 1 - c), _xor(y, c), c)
        pb = (_xor(x, c), _xor(y, 1 - c), c)
        sib = (x, y, 1 - c)
        ja = 2 * pa[0] + pa[1]
        jb = 2 * pb[0] + pb[1]
        jd = 3 - j

        def rcopy(k, src, dst, to):
            return pltpu.make_async_remote_copy(src_ref=src, dst_ref=dst, send_sem=send_sems.at[k],
                                                recv_sem=recv_sems.at[k], device_id=to, device_id_type=MESH)

        def t_rows_of(chip, half):
            return gwt_ref.at[pl.ds(pl.multiple_of(chip * t_rows + half * t_half, 8), t_half), :]

        def o_rows_of(chip, half):
            return gwo_ref.at[pl.ds(pl.multiple_of(chip * o_rows + half * o_half, 8), o_half), :]

        small_land[me] = small_ref[...]
        others = [(dx, dy, dc) for dx in (0, 1) for dy in (0, 1) for dc in (0, 1)][1:]
        small_sends = [rcopy(k, small_ref, small_land.at[me], (_xor(x, dx), _xor(y, dy), _xor(c, dc)))
                       for k, (dx, dy, dc) in enumerate(others)]
        for cp in small_sends:
            cp.start()

        def load(src, dst, k):
            cp = pltpu.make_async_copy(src, dst, local_sems.at[k])
            cp.start()
            return cp

        order = (ja, jd, jb, j)
        sib_order = (jb, jd, ja, j)
        pair_sends = []
        for s, chip in enumerate(sib_order):
            for cp in (load(t_rows_of(chip, 1 - c), stage_t, 0), load(o_rows_of(chip, 1 - c), stage_o, 1)):
                cp.wait()
            _cast_rows(stage_t, sb_t.at[s])
            _cast_rows(stage_o, sb_o.at[s])
            pair_sends.append((rcopy(7 + s, sb_t.at[s], r1t.at[s], sib), rcopy(11 + s, sb_o.at[s], r1o.at[s], sib)))
            for cp in pair_sends[-1]:
                cp.start()

        def resend(s, k_t, k_o, land_t, land_o, to):
            for cp in pair_sends[s]:
                cp.wait_send()
            _cast_rows(acc_t.at[s], sb_t.at[s])
            _cast_rows(acc_o.at[s], sb_o.at[s])
            cps = [rcopy(k_t, sb_t.at[s], land_t, to), rcopy(k_o, sb_o.at[s], land_o, to)]
            for cp in cps:
                cp.start()
            return cps

        hop1 = []
        for s, chip in enumerate(order):
            for cp in (load(t_rows_of(chip, c), acc_t.at[s], 0), load(o_rows_of(chip, c), acc_o.at[s], 1)):
                cp.wait()
            rcopy(7 + s, sb_t.at[s], r1t.at[s], sib).wait_recv()
            rcopy(11 + s, sb_o.at[s], r1o.at[s], sib).wait_recv()
            _accumulate(acc_t.at[s], r1t.at[s])
            _accumulate(acc_o.at[s], r1o.at[s])
            if s < 2:
                hop1 += resend(s, 15 + s, 17 + s, r2t.at[s], r2o.at[s], pa)

        rcopy(16, sb_t.at[1], r2t.at[1], pa).wait_recv()
        rcopy(18, sb_o.at[1], r2o.at[1], pa).wait_recv()
        _accumulate(acc_t.at[2], r2t.at[1])
        _accumulate(acc_o.at[2], r2o.at[1])
        hop2 = resend(2, 19, 20, r3t, r3o, pb)
        rcopy(15, sb_t.at[0], r2t.at[0], pa).wait_recv()
        rcopy(17, sb_o.at[0], r2o.at[0], pa).wait_recv()
        _accumulate(acc_t.at[3], r2t.at[0])
        _accumulate(acc_o.at[3], r2o.at[0])
        rcopy(19, sb_t.at[2], r3t, pb).wait_recv()
        rcopy(20, sb_o.at[2], r3o, pb).wait_recv()
        _accumulate(acc_t.at[3], r3t)
        _accumulate(acc_o.at[3], r3o)

        out_t = gwt_sh.at[pl.ds(pl.multiple_of(c * t_half, 8), t_half), :]
        out_o = gwo_sh.at[pl.ds(pl.multiple_of(c * o_half, 8), o_half), :]
        keep = [load(acc_t.at[3], out_t, 0), load(acc_o.at[3], out_o, 1)]
        swap = [rcopy(21, acc_t.at[3], out_t, sib), rcopy(22, acc_o.at[3], out_o, sib)]
        for cp in swap:
            cp.start()

        for cp in small_sends:
            cp.wait_recv()
        total = small_land[0]
        for dev in range(1, 8):
            total = total + small_land[dev]
        small_sum[...] = total

        rcopy(21, acc_t.at[3], gwt_sh.at[pl.ds(pl.multiple_of((1 - c) * t_half, 8), t_half), :], sib).wait_recv()
        rcopy(22, acc_o.at[3], gwo_sh.at[pl.ds(pl.multiple_of((1 - c) * o_half, 8), o_half), :], sib).wait_recv()
        for cp in small_sends + list(pair_sends[3]) + hop1 + hop2 + swap:
            cp.wait_send()
        for cp in keep:
            cp.wait()

    any_spec = pl.BlockSpec(memory_space=pl.ANY)
    vmem_spec = pl.BlockSpec(memory_space=pltpu.VMEM)
    outs = pl.pallas_call(
        body, name="rs_grads",
        out_shape=(jax.ShapeDtypeStruct((t_rows, width), F32), jax.ShapeDtypeStruct((o_rows, width), F32),
                   jax.ShapeDtypeStruct(small.shape, F32)),
        in_specs=[any_spec, any_spec, vmem_spec],
        out_specs=(any_spec, any_spec, vmem_spec),
        scratch_shapes=[pltpu.VMEM((4, t_half, width), F32), pltpu.VMEM((4, o_half, width), F32),
                        pltpu.VMEM((t_half, width), F32), pltpu.VMEM((o_half, width), F32),
                        pltpu.VMEM((4, t_half, width), BF16), pltpu.VMEM((4, o_half, width), BF16),
                        pltpu.VMEM((4, t_half, width), BF16), pltpu.VMEM((4, o_half, width), BF16),
                        pltpu.VMEM((2, t_half, width), BF16), pltpu.VMEM((2, o_half, width), BF16),
                        pltpu.VMEM((t_half, width), BF16), pltpu.VMEM((o_half, width), BF16),
                        pltpu.VMEM((8,) + small.shape, F32),
                        pltpu.SemaphoreType.DMA((23,)), pltpu.SemaphoreType.DMA((23,)), pltpu.SemaphoreType.DMA((2,))],
        compiler_params=_params(56),
    )(gwt, gwo, small)
    return outs


def _adamw(name, w, g, m, v, rows):
    def body(w_ref, g_ref, m_ref, v_ref, go_ref, d_ref, nm_ref, nv_ref):
        gv = g_ref[...]
        go_ref[...] = gv
        nm = ADAM_B1 * m_ref[...] + (1.0 - ADAM_B1) * gv
        nv = ADAM_B2 * v_ref[...] + (1.0 - ADAM_B2) * (gv * gv)
        m_hat = nm / (1.0 - ADAM_B1 ** ADAM_STEP)
        v_hat = nv / (1.0 - ADAM_B2 ** ADAM_STEP)
        d_ref[...] = -ADAM_LR * (m_hat / (jnp.sqrt(v_hat) + ADAM_EPS) + ADAM_WD * w_ref[...])
        nm_ref[...] = nm
        nv_ref[...] = nv

    spec = pl.BlockSpec((rows, w.shape[1]), lambda i: (i, 0))
    shape = jax.ShapeDtypeStruct(w.shape, F32)
    return pl.pallas_call(
        body, name="adamw_" + name, grid=(w.shape[0] // rows,),
        out_shape=(shape,) * 4, in_specs=[spec] * 4, out_specs=(spec,) * 4,
        compiler_params=_params(32, ("arbitrary",)),
    )(w, g, m, v)


def kernel(x, norm_in, w_in, conv_w, attn_sinks, norm_conv_out, norm_attn_out, w_out, norm_final, loss_target, m_norm_in, m_w_in, m_conv_w, m_attn_sinks, m_norm_conv_out, m_norm_attn_out, m_w_out, m_norm_final, v_norm_in, v_w_in, v_conv_w, v_attn_sinks, v_norm_conv_out, v_norm_attn_out, v_w_out, v_norm_final):
    chip = 2 * lax.axis_index("x") + lax.axis_index("y")
    xs, target = x[0], loss_target[0]
    norm_final2 = norm_final.reshape(1, D_MODEL)

    wt, wo, cw4 = _ag_weights(w_in[0].T, w_out[0], conv_w[0])
    cw = jnp.transpose(cw4, (1, 0, 2)).reshape(3, D_CONV)

    h, pc, q, kv, ga = _fwd_in(xs, norm_in, wt)
    oc = _conv_fwd(pc, cw, norm_conv_out)
    ya, oa = _attn_fwd(q, kv, ga, attn_sinks, norm_attn_out)
    dx2, doc, doa, gwo, gnf, loss_lanes = _out_proj_loss(xs, oc, oa, wo, norm_final2, target)

    dpc, gnc, gcw = _conv_bwd(pc, doc, cw, norm_conv_out)
    dq, dkv, dga, gna, gsink = _attn_bwd(q, kv, ga, ya, doa, attn_sinks, norm_attn_out)
    gwt = _dw_in(dpc, dq, dkv, dga, h)
    grad_x, gni = _dh_gradx(dpc, dq, dkv, dga, wt, xs, norm_in, dx2)

    last = jnp.zeros((1, D_MODEL), F32).at[:, 0:N_HEADS].set(gsink[:, 0:N_HEADS]).at[0, N_HEADS].set(jnp.sum(loss_lanes))
    small = jnp.concatenate([gni, gnc, gna, gnf, gcw[0:3], last], axis=0)
    gwt_sh, gwo_sh, small_sum = _rs_grads(gwt, gwo, small)

    loss = small_sum[7, N_HEADS]
    g_norm_in, g_norm_conv, g_norm_attn = small_sum[0:1], small_sum[1:2], small_sum[2:3]
    g_norm_final = small_sum[3]
    g_conv_w = lax.dynamic_slice(small_sum[4:7], (0, chip * (D_CONV // N_CHIPS)), (3, D_CONV // N_CHIPS))[None]
    g_sinks = small_sum[7:8, 0:N_HEADS]

    def adam(name, w, g, m, v, rows):
        shape = w.shape
        two_d = (-1, shape[-1])
        out = _adamw(name, w.reshape(two_d), g.reshape(two_d), m.reshape(two_d), v.reshape(two_d), rows)
        return tuple(o.reshape(shape) for o in out)

    weights = (norm_in, w_in, conv_w, attn_sinks, norm_conv_out, norm_attn_out, w_out, norm_final)
    grads = (g_norm_in, None, g_conv_w, g_sinks, g_norm_conv, g_norm_attn, gwo_sh[None], g_norm_final)
    moments_m = (m_norm_in, m_w_in, m_conv_w, m_attn_sinks, m_norm_conv_out, m_norm_attn_out, m_w_out, m_norm_final)
    moments_v = (v_norm_in, v_w_in, v_conv_w, v_attn_sinks, v_norm_conv_out, v_norm_attn_out, v_w_out, v_norm_final)
    rows = (1, None, 3, 1, 1, 1, 128, 1)
    names = ("norm_in", "w_in", "conv_w", "attn_sinks", "norm_conv_out", "norm_attn_out", "w_out", "norm_final")
    updates = []
    for args in zip(names, weights, grads, moments_m, moments_v, rows):
        if args[0] == "w_in":
            out_t = _adamw("w_in", w_in[0].T, gwt_sh, m_w_in[0].T, v_w_in[0].T, 200)
            updates.append(tuple(o.T[None] for o in out_t))
        else:
            updates.append(adam(*args))
    grads_out, deltas, new_m, new_v = zip(*updates)
    return (loss, grad_x[None], *grads_out, *deltas, *new_m, *new_v)
```

```python
import jax
import jax.numpy as jnp
from jax import lax
from jax.experimental import pallas as pl
from jax.experimental.pallas import tpu as pltpu

F32 = jnp.float32
BF16 = jnp.bfloat16
MESH = pl.DeviceIdType.MESH

D_MODEL = 1024
D_CONV = 1024
D_ATTN = 1024
D_KV = 128
D_MIX = D_CONV + D_ATTN
D_PC = 4 * D_CONV
D_IN_PROJ = D_PC + 2 * D_ATTN + 2 * D_KV
ROW_Q = D_PC
ROW_KV = ROW_Q + D_ATTN
ROW_GA = ROW_KV + 2 * D_KV
N_HEADS = 16
HEAD_DIM = 64
HEADS_PER_KV = 8
BLK = 128
N_CHIPS = 4
RMS_EPS = 1e-5
SCALE = HEAD_DIM ** -0.5
SLOPES = tuple(2.0 ** (-8.0 * (h + 1) / N_HEADS) for h in range(N_HEADS))

ADAM_LR, ADAM_B1, ADAM_B2, ADAM_EPS, ADAM_WD, ADAM_STEP = 0.001, 0.9, 0.999, 1e-08, 0.01, 10

TOK_TILE = 256
MIB = 1 << 20


def _params(vmem_mib, semantics=None):
    return pltpu.CompilerParams(dimension_semantics=semantics, vmem_limit_bytes=vmem_mib * MIB)


def _nn(a, b):
    return jnp.dot(a, b, preferred_element_type=F32)


def _nt(a, b):
    return lax.dot_general(a, b, (((1,), (1,)), ((), ())), preferred_element_type=F32)


def _tn(a, b):
    return lax.dot_general(a, b, (((0,), (0,)), ((), ())), preferred_element_type=F32)


def _rstd(v):
    return lax.rsqrt(jnp.mean(v * v, axis=-1, keepdims=True) + RMS_EPS)


def _rms_bwd(g, xhat, rstd):
    return rstd * (g - xhat * jnp.mean(g * xhat, axis=-1, keepdims=True))


def _silu_and_grad(g):
    s = jax.nn.sigmoid(g)
    return g * s, s * (1.0 + g * (1.0 - s))


def _resident(shape):
    return pl.BlockSpec(shape, lambda *_: (0,) * len(shape), pipeline_mode=pl.Buffered(1))


def _xor(a, b):
    return a + b - 2 * a * b


def _cast_rows(src_ref, dst_ref, rows=32):
    def step(i, carry):
        sl = pl.ds(pl.multiple_of(i * rows, rows), rows)
        dst_ref[sl, :] = src_ref[sl, :].astype(dst_ref.dtype)
        return carry
    lax.fori_loop(0, src_ref.shape[0] // rows, step, 0)


def _ag_weights(wt_sh, wo_sh, cw_sh):
    wt_rows, wo_rows = wt_sh.shape[0], wo_sh.shape[0]
    wt_half, wo_half = wt_rows // 2, wo_rows // 2
    width = wt_sh.shape[1]

    def body(wt_ref, wo_ref, cw_ref, wt_out, wo_out, cw_out, f32_t, f32_o, own_t, own_o, land_t, land_o,
             send_sems, recv_sems, local_sems):
        x, y, c = lax.axis_index("x"), lax.axis_index("y"), lax.axis_index("c")
        j = 2 * x + y
        p1 = (_xor(x, c), _xor(y, 1 - c), c)
        p2 = (_xor(x, 1 - c), _xor(y, c), c)
        sib = (x, y, 1 - c)
        j1 = 2 * p1[0] + p1[1]
        j2 = 2 * p2[0] + p2[1]
        j3 = 3 - j

        def wt_rows_of(chip, half):
            return wt_out.at[pl.ds(pl.multiple_of(chip * wt_rows + half * wt_half, 16), wt_half), :]

        def wo_rows_of(chip, half):
            return wo_out.at[pl.ds(pl.multiple_of(chip * wo_rows + half * wo_half, 16), wo_half), :]

        def rcopy(k, src, dst, to):
            return pltpu.make_async_remote_copy(src_ref=src, dst_ref=dst, send_sem=send_sems.at[k],
                                                recv_sem=recv_sems.at[k], device_id=to, device_id_type=MESH)

        def lcopy(k, src, dst):
            cp = pltpu.make_async_copy(src, dst, local_sems.at[k])
            cp.start()
            return cp

        loads = [lcopy(0, wt_ref, f32_t), lcopy(1, wo_ref, f32_o)]
        local = [lcopy(2, cw_ref, cw_out.at[j])]
        for cp in loads:
            cp.wait()
        _cast_rows(f32_t, own_t)
        _cast_rows(f32_o, own_o)
        local += [lcopy(0, own_t, wt_out.at[pl.ds(pl.multiple_of(j * wt_rows, 16), wt_rows), :]),
                  lcopy(1, own_o, wo_out.at[pl.ds(pl.multiple_of(j * wo_rows, 16), wo_rows), :])]
        half_t = own_t.at[pl.ds(pl.multiple_of(c * wt_half, 16), wt_half), :]
        half_o = own_o.at[pl.ds(pl.multiple_of(c * wo_half, 16), wo_half), :]

        def publish(slot, chip, k_t, k_o, k_local):
            sends = [rcopy(k_t, land_t.at[slot], wt_rows_of(chip, c), sib), rcopy(k_o, land_o.at[slot], wo_rows_of(chip, c), sib)]
            for cp in sends:
                cp.start()
            local.extend([lcopy(k_local, land_t.at[slot], wt_rows_of(chip, c)),
                          lcopy(k_local + 1, land_o.at[slot], wo_rows_of(chip, c))])
            return sends

        hop1 = [rcopy(0, half_t, land_t.at[0], p1), rcopy(1, half_o, land_o.at[0], p1), rcopy(2, cw_ref, cw_out.at[j], p1)]
        for cp in hop1:
            cp.start()
        rcopy(0, half_t, land_t.at[0], p1).wait_recv()
        rcopy(1, half_o, land_o.at[0], p1).wait_recv()
        rcopy(2, cw_ref, cw_out.at[j1], p1).wait_recv()
        hop2 = [rcopy(3, half_t, land_t.at[1], p2), rcopy(5, half_o, land_o.at[1], p2), rcopy(7, cw_ref, cw_out.at[j], p2),
                rcopy(4, land_t.at[0], land_t.at[2], p2), rcopy(6, land_o.at[0], land_o.at[2], p2),
                rcopy(8, cw_out.at[j1], cw_out.at[j1], p2)]
        for cp in hop2:
            cp.start()
        swaps = publish(0, j1, 9, 12, 3)
        rcopy(3, half_t, land_t.at[1], p2).wait_recv()
        rcopy(5, half_o, land_o.at[1], p2).wait_recv()
        rcopy(7, cw_ref, cw_out.at[j2], p2).wait_recv()
        swaps += publish(1, j2, 10, 13, 5)
        rcopy(4, half_t, land_t.at[2], p2).wait_recv()
        rcopy(6, half_o, land_o.at[2], p2).wait_recv()
        rcopy(8, cw_ref, cw_out.at[j3], p2).wait_recv()
        swaps += publish(2, j3, 11, 14, 7)
        for k, chip in ((9, j2), (10, j1), (11, j3)):
            rcopy(k, half_t, wt_rows_of(chip, 1 - c), sib).wait_recv()
        for k, chip in ((12, j2), (13, j1), (14, j3)):
            rcopy(k, half_o, wo_rows_of(chip, 1 - c), sib).wait_recv()
        for cp in hop1 + hop2 + swaps:
            cp.wait_send()
        for cp in local:
            cp.wait()

    any_spec = pl.BlockSpec(memory_space=pl.ANY)
    return pl.pallas_call(
        body, name="ag_weights",
        out_shape=(jax.ShapeDtypeStruct((N_CHIPS * wt_rows, width), BF16),
                   jax.ShapeDtypeStruct((N_CHIPS * wo_rows, width), BF16),
                   jax.ShapeDtypeStruct((N_CHIPS,) + cw_sh.shape, cw_sh.dtype)),
        in_specs=[any_spec, any_spec, any_spec],
        out_specs=(any_spec, any_spec, any_spec),
        scratch_shapes=[pltpu.VMEM((wt_rows, width), F32), pltpu.VMEM((wo_rows, width), F32),
                        pltpu.VMEM((wt_rows, width), BF16), pltpu.VMEM((wo_rows, width), BF16),
                        pltpu.VMEM((3, wt_half, width), BF16), pltpu.VMEM((3, wo_half, width), BF16),
                        pltpu.SemaphoreType.DMA((15,)), pltpu.SemaphoreType.DMA((15,)), pltpu.SemaphoreType.DMA((9,))],
        compiler_params=_params(32),
    )(wt_sh, wo_sh, cw_sh)


def _fwd_in(x, norm_in, wt):
    seq = x.shape[0]
    tile = TOK_TILE

    def body(x_ref, g_ref, wt_ref, h_ref, pc_ref, q_ref, kv_ref, ga_ref):
        xv = x_ref[...]
        h = (xv * _rstd(xv) * g_ref[...]).astype(BF16)
        h_ref[...] = h
        for blk in range(D_PC // D_CONV):
            pc_ref[:, blk * D_CONV:(blk + 1) * D_CONV] = _nt(h, wt_ref[blk * D_CONV:(blk + 1) * D_CONV, :])
        q_ref[...] = _nt(h, wt_ref[ROW_Q:ROW_KV, :])
        kv_ref[...] = _nt(h, wt_ref[ROW_KV:ROW_GA, :])
        ga_ref[...] = _nt(h, wt_ref[ROW_GA:D_IN_PROJ, :])

    def row(width):
        return pl.BlockSpec((tile, width), lambda i: (i, 0))

    return pl.pallas_call(
        body, name="fwd_in", grid=(seq // tile,),
        out_shape=(jax.ShapeDtypeStruct((seq, D_MODEL), BF16), jax.ShapeDtypeStruct((seq, D_PC), F32),
                   jax.ShapeDtypeStruct((seq, D_ATTN), F32), jax.ShapeDtypeStruct((seq, 2 * D_KV), F32),
                   jax.ShapeDtypeStruct((seq, D_ATTN), F32)),
        in_specs=[row(D_MODEL), _resident((1, D_MODEL)), _resident(wt.shape)],
        out_specs=(row(D_MODEL), row(D_PC), row(D_ATTN), row(2 * D_KV), row(D_ATTN)),
        compiler_params=_params(48, ("arbitrary",)),
    )(x, norm_in, wt)


def _conv_core(pc_ref, zbuf, cw_ref):
    tile = pc_ref.shape[0]
    cb = pc_ref[:, 0:D_CONV]
    cc = pc_ref[:, D_CONV:2 * D_CONV]
    cu = pc_ref[:, 2 * D_CONV:3 * D_CONV]
    z = cc * cu
    zbuf[8:tile + 8, :] = z
    z1 = zbuf[7:tile + 7, :]
    z2 = zbuf[6:tile + 6, :]
    conv = cw_ref[0:1, :] * z2 + cw_ref[1:2, :] * z1 + cw_ref[2:3, :] * z
    return cb, cc, cu, z, z1, z2, conv


def _conv_fwd(pc, conv_w, norm_conv_out):
    seq = pc.shape[0]
    tile = TOK_TILE

    def body(pc_ref, cw_ref, gn_ref, oc_ref, zbuf):
        @pl.when(pl.program_id(0) == 0)
        def _():
            zbuf[0:8, :] = jnp.zeros((8, D_CONV), F32)

        cb, _, _, _, _, _, conv = _conv_core(pc_ref, zbuf, cw_ref)
        yc = cb * conv
        silu, _ = _silu_and_grad(pc_ref[:, 3 * D_CONV:4 * D_CONV])
        oc_ref[...] = (yc * _rstd(yc) * gn_ref[...] * silu).astype(BF16)
        zbuf[0:8, :] = zbuf[tile:tile + 8, :]

    return pl.pallas_call(
        body, name="conv_fwd", grid=(seq // tile,),
        out_shape=jax.ShapeDtypeStruct((seq, D_CONV), BF16),
        in_specs=[pl.BlockSpec((tile, D_PC), lambda i: (i, 0)), _resident(conv_w.shape), _resident((1, D_CONV))],
        out_specs=pl.BlockSpec((tile, D_CONV), lambda i: (i, 0)),
        scratch_shapes=[pltpu.VMEM((tile + 8, D_CONV), F32)],
        compiler_params=_params(40, ("arbitrary",)),
    )(pc, conv_w, norm_conv_out)


def _band_geometry(block_index):
    qi = lax.broadcasted_iota(jnp.int32, (BLK, BLK), 0)
    kp = lax.broadcasted_iota(jnp.int32, (BLK, BLK), 1)
    use_cur = kp <= qi
    dist = jnp.where(use_cur, qi - kp, qi - kp + BLK).astype(F32)
    valid = use_cur | (block_index > 0)
    return use_cur, dist, valid


def _block_diag(cur, prev, group):
    lane = lax.broadcasted_iota(jnp.int32, cur.shape, 1)

    def halves(t):
        other = pltpu.roll(t, 64, 1)
        lo, hi = (t, other) if group == 0 else (other, t)
        return jnp.where(lane < 64, lo, 0.0), jnp.where(lane >= 64, hi, 0.0)

    return jnp.concatenate(halves(cur) + halves(prev), axis=0).astype(BF16)


def _merge(s4, use_cur):
    return (jnp.where(use_cur, s4[:, 0:BLK], s4[:, 2 * BLK:3 * BLK]),
            jnp.where(use_cur, s4[:, BLK:2 * BLK], s4[:, 3 * BLK:4 * BLK]))


def _split(a, b, use_cur):
    return jnp.concatenate([jnp.where(use_cur, a, 0.0), jnp.where(use_cur, b, 0.0),
                            jnp.where(use_cur, 0.0, a), jnp.where(use_cur, 0.0, b)], axis=1)


def _softmax_head(s, head, sink, dist, valid):
    sc = jnp.where(valid, s - SLOPES[head] * dist, -jnp.inf)
    m = jnp.maximum(jnp.max(sc, axis=-1, keepdims=True), sink)
    p = jnp.exp(sc - m)
    es = jnp.exp(sink - m)
    inv = 1.0 / (jnp.sum(p, axis=-1, keepdims=True) + es)
    return p * inv, es * inv


def _attn_probs(q_ref, kvc_ref, kvp_ref, sink_ref, geometry):
    use_cur, dist, valid = geometry
    groups = range(N_HEADS // HEADS_PER_KV)
    kbd = [_block_diag(kvc_ref[:, 0:D_KV], kvp_ref[:, 0:D_KV], g) for g in groups]
    vbd = [_block_diag(kvc_ref[:, D_KV:2 * D_KV], kvp_ref[:, D_KV:2 * D_KV], g) for g in groups]
    qps = [(q_ref[:, j * 128:(j + 1) * 128] * SCALE).astype(BF16) for j in range(N_HEADS // 2)]
    scores = []
    for j, qp in enumerate(qps):
        scores += _merge(_nt(qp, kbd[j // 4]), use_cur)
    sinks = [sink_ref[0, h] for h in range(N_HEADS)]
    scores = [jnp.where(valid, s - SLOPES[h] * dist, -jnp.inf) for h, s in enumerate(scores)]
    maxes = [jnp.maximum(jnp.max(s, axis=-1, keepdims=True), sinks[h]) for h, s in enumerate(scores)]
    exps = [jnp.exp(s - m) for s, m in zip(scores, maxes)]
    sink_exps = [jnp.exp(sinks[h] - m) for h, m in enumerate(maxes)]
    invs = [1.0 / (jnp.sum(e, axis=-1, keepdims=True) + se) for e, se in zip(exps, sink_exps)]
    probs = [e * inv for e, inv in zip(exps, invs)]
    sink_probs = [se * inv for se, inv in zip(sink_exps, invs)]
    return qps, kbd, vbd, probs, sink_probs


def _kv_specs(n_blocks, reverse):
    def blk(i):
        return (n_blocks - 1 - i) if reverse else i
    cur = pl.BlockSpec((BLK, 2 * D_KV), lambda i: (blk(i), 0))
    prev = pl.BlockSpec((BLK, 2 * D_KV), lambda i: (jnp.maximum(blk(i) - 1, 0), 0))
    return cur, prev


def _attn_fwd(q, kv, ga, sinks, norm_attn_out):
    seq = q.shape[0]
    n_blocks = seq // BLK

    def body(q_ref, kvc_ref, kvp_ref, ga_ref, sink_ref, gn_ref, ya_ref, oa_ref):
        geometry = _band_geometry(pl.program_id(0))
        _, _, vbd, probs, _ = _attn_probs(q_ref, kvc_ref, kvp_ref, sink_ref, geometry)
        p4s = [_split(probs[2 * j], probs[2 * j + 1], geometry[0]).astype(BF16) for j in range(N_HEADS // 2)]
        ya = jnp.concatenate([_nn(p4, vbd[j // 4]) for j, p4 in enumerate(p4s)], axis=1)
        ya_ref[...] = ya
        silu, _ = _silu_and_grad(ga_ref[...])
        oa_ref[...] = (ya * _rstd(ya) * gn_ref[...] * silu).astype(BF16)

    row = pl.BlockSpec((BLK, D_ATTN), lambda i: (i, 0))
    kv_cur, kv_prev = _kv_specs(n_blocks, reverse=False)
    return pl.pallas_call(
        body, name="attn_fwd", grid=(n_blocks,),
        out_shape=(jax.ShapeDtypeStruct((seq, D_ATTN), F32), jax.ShapeDtypeStruct((seq, D_ATTN), BF16)),
        in_specs=[row, kv_cur, kv_prev, row, pl.BlockSpec(memory_space=pltpu.SMEM), _resident((1, D_ATTN))],
        out_specs=(row, row),
        compiler_params=_params(32, ("arbitrary",)),
    )(q, kv, kv, ga, sinks, norm_attn_out)


def _out_proj_loss(x, oc, oa, wo, norm_final, target):
    seq = x.shape[0]
    tile = TOK_TILE

    def body(x_ref, oc_ref, oa_ref, wo_ref, gf_ref, t_ref, dx2_ref, doc_ref, doa_ref, gwo_ref, gnf_ref, loss_ref):
        @pl.when(pl.program_id(0) == 0)
        def _():
            gwo_ref[...] = jnp.zeros_like(gwo_ref)
            gnf_ref[...] = jnp.zeros_like(gnf_ref)
            loss_ref[...] = jnp.zeros_like(loss_ref)

        oc, oa = oc_ref[...], oa_ref[...]
        x2 = x_ref[...] + _nn(oc, wo_ref[0:D_CONV, :]) + _nn(oa, wo_ref[D_CONV:D_MIX, :])
        r = _rstd(x2)
        xhat = x2 * r
        err = xhat * gf_ref[...] - t_ref[...]
        loss_ref[...] += jnp.sum(err * err, axis=0, keepdims=True) * (0.5 / D_MODEL)
        dy = err * (1.0 / D_MODEL)
        gnf_ref[...] += jnp.sum(dy * xhat, axis=0, keepdims=True)
        dx2 = _rms_bwd(dy * gf_ref[...], xhat, r)
        dx2_ref[...] = dx2
        db = dx2.astype(BF16)
        doc_ref[...] = _nt(db, wo_ref[0:D_CONV, :])
        doa_ref[...] = _nt(db, wo_ref[D_CONV:D_MIX, :])
        gwo_ref[0:D_CONV, :] += _tn(oc, db)
        gwo_ref[D_CONV:D_MIX, :] += _tn(oa, db)

    row = pl.BlockSpec((tile, D_MODEL), lambda i: (i, 0))
    vec = pl.BlockSpec((1, D_MODEL), lambda i: (0, 0))
    return pl.pallas_call(
        body, name="out_proj_loss", grid=(seq // tile,),
        out_shape=(jax.ShapeDtypeStruct((seq, D_MODEL), F32), jax.ShapeDtypeStruct((seq, D_CONV), F32),
                   jax.ShapeDtypeStruct((seq, D_ATTN), F32), jax.ShapeDtypeStruct((D_MIX, D_MODEL), F32),
                   jax.ShapeDtypeStruct((1, D_MODEL), F32), jax.ShapeDtypeStruct((1, D_MODEL), F32)),
        in_specs=[row, row, row, _resident(wo.shape), _resident((1, D_MODEL)), row],
        out_specs=(row, row, row, pl.BlockSpec((D_MIX, D_MODEL), lambda i: (0, 0)), vec, vec),
        compiler_params=_params(48, ("arbitrary",)),
    )(x, oc, oa, wo, norm_final, target)


def _conv_bwd(pc, doc, conv_w, norm_conv_out):
    seq = pc.shape[0]
    tile = TOK_TILE
    n_tiles = seq // tile

    def body(pc_ref, hcc_ref, hcu_ref, doc_ref, cw_ref, gn_ref, dpc_ref, gnc_ref, gcw_ref, zbuf, dbuf):
        step = pl.program_id(0)

        @pl.when(step == 0)
        def _():
            gnc_ref[...] = jnp.zeros_like(gnc_ref)
            gcw_ref[...] = jnp.zeros_like(gcw_ref)
            dbuf[tile:tile + 8, :] = jnp.zeros((8, D_CONV), F32)

        is_first_tile = step == n_tiles - 1
        zbuf[0:8, :] = jnp.where(is_first_tile, 0.0, hcc_ref[...] * hcu_ref[...])
        cb, cc, cu, z, z1, z2, conv = _conv_core(pc_ref, zbuf, cw_ref)
        gc = pc_ref[:, 3 * D_CONV:4 * D_CONV]
        silu, dsilu = _silu_and_grad(gc)
        yc = cb * conv
        r = _rstd(yc)
        xhat = yc * r
        do = doc_ref[...]
        dn = do * silu
        dpc_ref[:, 3 * D_CONV:4 * D_CONV] = (do * (xhat * gn_ref[...]) * dsilu).astype(BF16)
        gnc_ref[...] += jnp.sum(dn * xhat, axis=0, keepdims=True)
        dyc = _rms_bwd(dn * gn_ref[...], xhat, r)
        dpc_ref[:, 0:D_CONV] = (dyc * conv).astype(BF16)
        dconv = dyc * cb
        gcw_ref[0:1, :] += jnp.sum(dconv * z2, axis=0, keepdims=True)
        gcw_ref[1:2, :] += jnp.sum(dconv * z1, axis=0, keepdims=True)
        gcw_ref[2:3, :] += jnp.sum(dconv * z, axis=0, keepdims=True)
        dbuf[0:tile, :] = dconv
        dz = cw_ref[2:3, :] * dconv + cw_ref[1:2, :] * dbuf[1:tile + 1, :] + cw_ref[0:1, :] * dbuf[2:tile + 2, :]
        dpc_ref[:, D_CONV:2 * D_CONV] = (dz * cu).astype(BF16)
        dpc_ref[:, 2 * D_CONV:3 * D_CONV] = (dz * cc).astype(BF16)
        dbuf[tile:tile + 8, :] = dbuf[0:8, :]

    def rev(i):
        return n_tiles - 1 - i

    def halo(col_block):
        return pl.BlockSpec((8, D_CONV), lambda i: (jnp.maximum(rev(i) * (tile // 8) - 1, 0), col_block))

    return pl.pallas_call(
        body, name="conv_bwd", grid=(n_tiles,),
        out_shape=(jax.ShapeDtypeStruct((seq, D_PC), BF16), jax.ShapeDtypeStruct((1, D_CONV), F32),
                   jax.ShapeDtypeStruct((8, D_CONV), F32)),
        in_specs=[pl.BlockSpec((tile, D_PC), lambda i: (rev(i), 0)), halo(1), halo(2),
                  pl.BlockSpec((tile, D_CONV), lambda i: (rev(i), 0)), _resident(conv_w.shape), _resident((1, D_CONV))],
        out_specs=(pl.BlockSpec((tile, D_PC), lambda i: (rev(i), 0)), pl.BlockSpec((1, D_CONV), lambda i: (0, 0)),
                   pl.BlockSpec((8, D_CONV), lambda i: (0, 0))),
        scratch_shapes=[pltpu.VMEM((tile + 8, D_CONV), F32), pltpu.VMEM((tile + 8, D_CONV), F32)],
        compiler_params=_params(48, ("arbitrary",)),
    )(pc, pc, pc, doc, conv_w, norm_conv_out)


def _attn_bwd(q, kv, ga, ya, doa, sinks, norm_attn_out):
    seq = q.shape[0]
    n_blocks = seq // BLK

    def body(q_ref, kvc_ref, kvp_ref, ga_ref, ya_ref, doa_ref, sink_ref, gn_ref,
             dq_ref, dkv_ref, dga_ref, gna_ref, gs_ref, carry, dya_buf):
        step = pl.program_id(0)

        @pl.when(step == 0)
        def _():
            gna_ref[...] = jnp.zeros_like(gna_ref)
            gs_ref[...] = jnp.zeros_like(gs_ref)
            carry[...] = jnp.zeros_like(carry)

        ya = ya_ref[...]
        r = _rstd(ya)
        xhat = ya * r
        silu, dsilu = _silu_and_grad(ga_ref[...])
        do = doa_ref[...]
        dn = do * silu
        dga_ref[...] = (do * (xhat * gn_ref[...]) * dsilu).astype(BF16)
        gna_ref[...] += jnp.sum(dn * xhat, axis=0, keepdims=True)
        dya_buf[...] = _rms_bwd(dn * gn_ref[...], xhat, r).astype(BF16)

        geometry = _band_geometry(n_blocks - 1 - step)
        use_cur = geometry[0]
        lane = lax.broadcasted_iota(jnp.int32, (BLK, 128), 1)
        sink_lane = lax.broadcasted_iota(jnp.int32, (1, 128), 1)
        gsink = jnp.zeros((1, 128), F32)

        def fold(bd):
            return (jnp.where(lane < 64, bd[0:BLK], 0.0) + jnp.where(lane >= 64, bd[BLK:2 * BLK], 0.0),
                    jnp.where(lane < 64, bd[2 * BLK:3 * BLK], 0.0) + jnp.where(lane >= 64, bd[3 * BLK:4 * BLK], 0.0))

        pairs = range(N_HEADS // 2)
        qps, kbd, vbd, probs, sink_probs = _attn_probs(q_ref, kvc_ref, kvp_ref, sink_ref, geometry)
        dyps = [dya_buf[:, j * 128:(j + 1) * 128] for j in pairs]
        dps = []
        for j in pairs:
            dps += _merge(_nt(dyps[j], vbd[j // 4]), use_cur)
        deltas = [jnp.sum(p * dp, axis=-1, keepdims=True) for p, dp in zip(probs, dps)]
        dss = [p * (dp - delta) for p, dp, delta in zip(probs, dps, deltas)]
        for h in range(N_HEADS):
            dsink = -jnp.sum(sink_probs[h] * deltas[h], axis=0, keepdims=True)
            gsink = gsink + jnp.where(sink_lane == h, dsink, 0.0)
        gs_ref[...] += gsink
        ds4s = [_split(dss[2 * j], dss[2 * j + 1], use_cur).astype(BF16) for j in pairs]
        p4s = [_split(probs[2 * j], probs[2 * j + 1], use_cur).astype(BF16) for j in pairs]
        dq_ref[...] = jnp.concatenate([_nn(ds4s[j], kbd[j // 4]) * SCALE for j in pairs], axis=1).astype(BF16)
        sums = []
        for group in range(N_HEADS // HEADS_PER_KV):
            acc = [jnp.zeros((BLK, 128), F32) for _ in range(4)]
            for j in range(group * 4, group * 4 + 4):
                for slot, part in enumerate(fold(_tn(ds4s[j], qps[j])) + fold(_tn(p4s[j], dyps[j]))):
                    acc[slot] = acc[slot] + part
            sums.append([a + pltpu.roll(a, 64, 1) for a in acc])
        dk_cur, dk_prev, dv_cur, dv_prev = (jnp.where(lane < 64, a, b) for a, b in zip(sums[0], sums[1]))
        dkv_ref[...] = (jnp.concatenate([dk_cur, dv_cur], axis=1) + carry[...]).astype(BF16)
        carry[...] = jnp.concatenate([dk_prev, dv_prev], axis=1)

    row = pl.BlockSpec((BLK, D_ATTN), lambda i: (n_blocks - 1 - i, 0))
    kv_cur, kv_prev = _kv_specs(n_blocks, reverse=True)
    return pl.pallas_call(
        body, name="attn_bwd", grid=(n_blocks,),
        out_shape=(jax.ShapeDtypeStruct((seq, D_ATTN), BF16), jax.ShapeDtypeStruct((seq, 2 * D_KV), BF16),
                   jax.ShapeDtypeStruct((seq, D_ATTN), BF16), jax.ShapeDtypeStruct((1, D_ATTN), F32),
                   jax.ShapeDtypeStruct((1, 128), F32)),
        in_specs=[row, kv_cur, kv_prev, row, row, row, pl.BlockSpec(memory_space=pltpu.SMEM), _resident((1, D_ATTN))],
        out_specs=(row, kv_cur, row, pl.BlockSpec((1, D_ATTN), lambda i: (0, 0)), pl.BlockSpec((1, 128), lambda i: (0, 0))),
        scratch_shapes=[pltpu.VMEM((BLK, 2 * D_KV), F32), pltpu.VMEM((BLK, D_ATTN), BF16)],
        compiler_params=_params(32, ("arbitrary",)),
    )(q, kv, kv, ga, ya, doa, sinks, norm_attn_out)


def _dw_in(dpc, dq, dkv, dga, h):
    seq = h.shape[0]
    rows = 256
    first = (0, ROW_Q // rows, ROW_KV // rows, ROW_GA // rows, D_IN_PROJ // rows)

    def body(dpc_ref, dq_ref, dkv_ref, dga_ref, h_ref, out_ref):
        i = pl.program_id(0)
        for piece, ref in enumerate((dpc_ref, dq_ref, dkv_ref, dga_ref)):
            @pl.when((i >= first[piece]) & (i < first[piece + 1]))
            def _(ref=ref):
                out_ref[...] = _tn(ref[...], h_ref[...])

    def cols(piece):
        n = first[piece + 1] - first[piece]
        return pl.BlockSpec((seq, rows), lambda i: (0, jnp.clip(i - first[piece], 0, n - 1)))

    return pl.pallas_call(
        body, name="dw_in", grid=(D_IN_PROJ // rows,),
        out_shape=jax.ShapeDtypeStruct((D_IN_PROJ, D_MODEL), F32),
        in_specs=[cols(0), cols(1), cols(2), cols(3), _resident(h.shape)],
        out_specs=pl.BlockSpec((rows, D_MODEL), lambda i: (i, 0)),
        compiler_params=_params(32, ("arbitrary",)),
    )(dpc, dq, dkv, dga, h)


def _dh_gradx(dpc, dq, dkv, dga, wt, x, norm_in, dx2):
    seq = x.shape[0]
    tile = TOK_TILE

    def body(dpc_ref, dq_ref, dkv_ref, dga_ref, wt_ref, x_ref, g_ref, dx2_ref, gx_ref, gni_ref):
        @pl.when(pl.program_id(0) == 0)
        def _():
            gni_ref[...] = jnp.zeros_like(gni_ref)

        dh = (_nn(dpc_ref[...], wt_ref[0:ROW_Q, :]) + _nn(dq_ref[...], wt_ref[ROW_Q:ROW_KV, :])
              + _nn(dkv_ref[...], wt_ref[ROW_KV:ROW_GA, :]) + _nn(dga_ref[...], wt_ref[ROW_GA:D_IN_PROJ, :]))
        xv = x_ref[...]
        r = _rstd(xv)
        xhat = xv * r
        gni_ref[...] += jnp.sum(dh * xhat, axis=0, keepdims=True)
        gx_ref[...] = _rms_bwd(dh * g_ref[...], xhat, r) + dx2_ref[...]

    def row(width):
        return pl.BlockSpec((tile, width), lambda i: (i, 0))

    return pl.pallas_call(
        body, name="dh_gradx", grid=(seq // tile,),
        out_shape=(jax.ShapeDtypeStruct((seq, D_MODEL), F32), jax.ShapeDtypeStruct((1, D_MODEL), F32)),
        in_specs=[row(D_PC), row(D_ATTN), row(2 * D_KV), row(D_ATTN), _resident(wt.shape), row(D_MODEL),
                  _resident((1, D_MODEL)), row(D_MODEL)],
        out_specs=(row(D_MODEL), pl.BlockSpec((1, D_MODEL), lambda i: (0, 0))),
        compiler_params=_params(48, ("arbitrary",)),
    )(dpc, dq, dkv, dga, wt, x, norm_in, dx2)


def _accumulate(dst_ref, src_ref, rows=16):
    def step(i, carry):
        sl = pl.ds(pl.multiple_of(i * rows, rows), rows)
        dst_ref[sl, :] = dst_ref[sl, :] + src_ref[sl, :].astype(F32)
        return carry
    lax.fori_loop(0, dst_ref.shape[0] // rows, step, 0)


def _rs_grads(gwt, gwo, small):
    t_rows, o_rows = gwt.shape[0] // N_CHIPS, gwo.shape[0] // N_CHIPS
    t_half, o_half = t_rows // 2, o_rows // 2
    width = gwt.shape[1]

    def body(gwt_ref, gwo_ref, small_ref, gwt_sh, gwo_sh, small_sum,
             acc_t, acc_o, stage_t, stage_o, sb_t, sb_o, r1t, r1o, r2t, r2o, r3t, r3o,
             small_land, send_sems, recv_sems, local_sems):
        x, y, c = lax.axis_index("x"), lax.axis_index("y"), lax.axis_index("c")
        me = 4 * x + 2 * y + c
        j = 2 * x + y
        pa = (_xor(x, 1 - c), _xor(y, c), c)
        pb = (_xor(x, c), _xor(y, 1 - c), c)
        sib = (x, y, 1 - c)
        ja = 2 * pa[0] + pa[1]
        jb = 2 * pb[0] + pb[1]
        jd = 3 - j

        def rcopy(k, src, dst, to):
            return pltpu.make_async_remote_copy(src_ref=src, dst_ref=dst, send_sem=send_sems.at[k],
                                                recv_sem=recv_sems.at[k], device_id=to, device_id_type=MESH)

        def t_rows_of(chip, half):
            return gwt_ref.at[pl.ds(pl.multiple_of(chip * t_rows + half * t_half, 8), t_half), :]

        def o_rows_of(chip, half):
            return gwo_ref.at[pl.ds(pl.multiple_of(chip * o_rows + half * o_half, 8), o_half), :]

        small_land[me] = small_ref[...]
        others = [(dx, dy, dc) for dx in (0, 1) for dy in (0, 1) for dc in (0, 1)][1:]
        small_sends = [rcopy(k, small_ref, small_land.at[me], (_xor(x, dx), _xor(y, dy), _xor(c, dc)))
                       for k, (dx, dy, dc) in enumerate(others)]
        for cp in small_sends:
            cp.start()

        def load(src, dst, k):
            cp = pltpu.make_async_copy(src, dst, local_sems.at[k])
            cp.start()
            return cp

        order = (ja, jd, jb, j)
        sib_order = (jb, jd, ja, j)
        pair_sends = []
        for s, chip in enumerate(sib_order):
            for cp in (load(t_rows_of(chip, 1 - c), stage_t, 0), load(o_rows_of(chip, 1 - c), stage_o, 1)):
                cp.wait()
            _cast_rows(stage_t, sb_t.at[s])
            _cast_rows(stage_o, sb_o.at[s])
            pair_sends.append((rcopy(7 + s, sb_t.at[s], r1t.at[s], sib), rcopy(11 + s, sb_o.at[s], r1o.at[s], sib)))
            for cp in pair_sends[-1]:
                cp.start()

        def resend(s, k_t, k_o, land_t, land_o, to):
            for cp in pair_sends[s]:
                cp.wait_send()
            _cast_rows(acc_t.at[s], sb_t.at[s])
            _cast_rows(acc_o.at[s], sb_o.at[s])
            cps = [rcopy(k_t, sb_t.at[s], land_t, to), rcopy(k_o, sb_o.at[s], land_o, to)]
            for cp in cps:
                cp.start()
            return cps

        hop1 = []
        for s, chip in enumerate(order):
            for cp in (load(t_rows_of(chip, c), acc_t.at[s], 0), load(o_rows_of(chip, c), acc_o.at[s], 1)):
                cp.wait()
            rcopy(7 + s, sb_t.at[s], r1t.at[s], sib).wait_recv()
            rcopy(11 + s, sb_o.at[s], r1o.at[s], sib).wait_recv()
            _accumulate(acc_t.at[s], r1t.at[s])
            _accumulate(acc_o.at[s], r1o.at[s])
            if s < 2:
                hop1 += resend(s, 15 + s, 17 + s, r2t.at[s], r2o.at[s], pa)

        rcopy(16, sb_t.at[1], r2t.at[1], pa).wait_recv()
        rcopy(18, sb_o.at[1], r2o.at[1], pa).wait_recv()
        _accumulate(acc_t.at[2], r2t.at[1])
        _accumulate(acc_o.at[2], r2o.at[1])
        hop2 = resend(2, 19, 20, r3t, r3o, pb)
        rcopy(15, sb_t.at[0], r2t.at[0], pa).wait_recv()
        rcopy(17, sb_o.at[0], r2o.at[0], pa).wait_recv()
        _accumulate(acc_t.at[3], r2t.at[0])
        _accumulate(acc_o.at[3], r2o.at[0])
        rcopy(19, sb_t.at[2], r3t, pb).wait_recv()
        rcopy(20, sb_o.at[2], r3o, pb).wait_recv()
        _accumulate(acc_t.at[3], r3t)
        _accumulate(acc_o.at[3], r3o)

        out_t = gwt_sh.at[pl.ds(pl.multiple_of(c * t_half, 8), t_half), :]
        out_o = gwo_sh.at[pl.ds(pl.multiple_of(c * o_half, 8), o_half), :]
        keep = [load(acc_t.at[3], out_t, 0), load(acc_o.at[3], out_o, 1)]
        swap = [rcopy(21, acc_t.at[3], out_t, sib), rcopy(22, acc_o.at[3], out_o, sib)]
        for cp in swap:
            cp.start()

        for cp in small_sends:
            cp.wait_recv()
        total = small_land[0]
        for dev in range(1, 8):
            total = total + small_land[dev]
        small_sum[...] = total

        rcopy(21, acc_t.at[3], gwt_sh.at[pl.ds(pl.multiple_of((1 - c) * t_half, 8), t_half), :], sib).wait_recv()
        rcopy(22, acc_o.at[3], gwo_sh.at[pl.ds(pl.multiple_of((1 - c) * o_half, 8), o_half), :], sib).wait_recv()
        for cp in small_sends + list(pair_sends[3]) + hop1 + hop2 + swap:
            cp.wait_send()
        for cp in keep:
            cp.wait()

    any_spec = pl.BlockSpec(memory_space=pl.ANY)
    vmem_spec = pl.BlockSpec(memory_space=pltpu.VMEM)
    outs = pl.pallas_call(
        body, name="rs_grads",
        out_shape=(jax.ShapeDtypeStruct((t_rows, width), F32), jax.ShapeDtypeStruct((o_rows, width), F32),
                   jax.ShapeDtypeStruct(small.shape, F32)),
        in_specs=[any_spec, any_spec, vmem_spec],
        out_specs=(any_spec, any_spec, vmem_spec),
        scratch_shapes=[pltpu.VMEM((4, t_half, width), F32), pltpu.VMEM((4, o_half, width), F32),
                        pltpu.VMEM((t_half, width), F32), pltpu.VMEM((o_half, width), F32),
                        pltpu.VMEM((4, t_half, width), BF16), pltpu.VMEM((4, o_half, width), BF16),
                        pltpu.VMEM((4, t_half, width), BF16), pltpu.VMEM((4, o_half, width), BF16),
                        pltpu.VMEM((2, t_half, width), BF16), pltpu.VMEM((2, o_half, width), BF16),
                        pltpu.VMEM((t_half, width), BF16), pltpu.VMEM((o_half, width), BF16),
                        pltpu.VMEM((8,) + small.shape, F32),
                        pltpu.SemaphoreType.DMA((23,)), pltpu.SemaphoreType.DMA((23,)), pltpu.SemaphoreType.DMA((2,))],
        compiler_params=_params(56),
    )(gwt, gwo, small)
    return outs


def _adamw(name, w, g, m, v, rows):
    def body(w_ref, g_ref, m_ref, v_ref, go_ref, d_ref, nm_ref, nv_ref):
        gv = g_ref[...]
        go_ref[...] = gv
        nm = ADAM_B1 * m_ref[...] + (1.0 - ADAM_B1) * gv
        nv = ADAM_B2 * v_ref[...] + (1.0 - ADAM_B2) * (gv * gv)
        m_hat = nm / (1.0 - ADAM_B1 ** ADAM_STEP)
        v_hat = nv / (1.0 - ADAM_B2 ** ADAM_STEP)
        d_ref[...] = -ADAM_LR * (m_hat / (jnp.sqrt(v_hat) + ADAM_EPS) + ADAM_WD * w_ref[...])
        nm_ref[...] = nm
        nv_ref[...] = nv

    spec = pl.BlockSpec((rows, w.shape[1]), lambda i: (i, 0))
    shape = jax.ShapeDtypeStruct(w.shape, F32)
    return pl.pallas_call(
        body, name="adamw_" + name, grid=(w.shape[0] // rows,),
        out_shape=(shape,) * 4, in_specs=[spec] * 4, out_specs=(spec,) * 4,
        compiler_params=_params(32, ("arbitrary",)),
    )(w, g, m, v)


def kernel(x, norm_in, w_in, conv_w, attn_sinks, norm_conv_out, norm_attn_out, w_out, norm_final, loss_target, m_norm_in, m_w_in, m_conv_w, m_attn_sinks, m_norm_conv_out, m_norm_attn_out, m_w_out, m_norm_final, v_norm_in, v_w_in, v_conv_w, v_attn_sinks, v_norm_conv_out, v_norm_attn_out, v_w_out, v_norm_final):
    chip = 2 * lax.axis_index("x") + lax.axis_index("y")
    xs, target = x[0], loss_target[0]
    norm_final2 = norm_final.reshape(1, D_MODEL)

    wt, wo, cw4 = _ag_weights(w_in[0].T, w_out[0], conv_w[0])
    cw = jnp.transpose(cw4, (1, 0, 2)).reshape(3, D_CONV)

    h, pc, q, kv, ga = _fwd_in(xs, norm_in, wt)
    oc = _conv_fwd(pc, cw, norm_conv_out)
    ya, oa = _attn_fwd(q, kv, ga, attn_sinks, norm_attn_out)
    dx2, doc, doa, gwo, gnf, loss_lanes = _out_proj_loss(xs, oc, oa, wo, norm_final2, target)

    dpc, gnc, gcw = _conv_bwd(pc, doc, cw, norm_conv_out)
    dq, dkv, dga, gna, gsink = _attn_bwd(q, kv, ga, ya, doa, attn_sinks, norm_attn_out)
    gwt = _dw_in(dpc, dq, dkv, dga, h)
    grad_x, gni = _dh_gradx(dpc, dq, dkv, dga, wt, xs, norm_in, dx2)

    last = jnp.zeros((1, D_MODEL), F32).at[:, 0:N_HEADS].set(gsink[:, 0:N_HEADS]).at[0, N_HEADS].set(jnp.sum(loss_lanes))
    small = jnp.concatenate([gni, gnc, gna, gnf, gcw[0:3], last], axis=0)
    gwt_sh, gwo_sh, small_sum = _rs_grads(gwt, gwo, small)

    loss = small_sum[7, N_HEADS]
    g_norm_in, g_norm_conv, g_norm_attn = small_sum[0:1], small_sum[1:2], small_sum[2:3]
    g_norm_final = small_sum[3]
    g_conv_w = lax.dynamic_slice(small_sum[4:7], (0, chip * (D_CONV // N_CHIPS)), (3, D_CONV // N_CHIPS))[None]
    g_sinks = small_sum[7:8, 0:N_HEADS]

    def adam(name, w, g, m, v, rows):
        shape = w.shape
        two_d = (-1, shape[-1])
        out = _adamw(name, w.reshape(two_d), g.reshape(two_d), m.reshape(two_d), v.reshape(two_d), rows)
        return tuple(o.reshape(shape) for o in out)

    weights = (norm_in, w_in, conv_w, attn_sinks, norm_conv_out, norm_attn_out, w_out, norm_final)
    grads = (g_norm_in, None, g_conv_w, g_sinks, g_norm_conv, g_norm_attn, gwo_sh[None], g_norm_final)
    moments_m = (m_norm_in, m_w_in, m_conv_w, m_attn_sinks, m_norm_conv_out, m_norm_attn_out, m_w_out, m_norm_final)
    moments_v = (v_norm_in, v_w_in, v_conv_w, v_attn_sinks, v_norm_conv_out, v_norm_attn_out, v_w_out, v_norm_final)
    rows = (1, None, 3, 1, 1, 1, 128, 1)
    names = ("norm_in", "w_in", "conv_w", "attn_sinks", "norm_conv_out", "norm_attn_out", "w_out", "norm_final")
    updates = []
    for args in zip(names, weights, grads, moments_m, moments_v, rows):
        if args[0] == "w_in":
            out_t = _adamw("w_in", w_in[0].T, gwt_sh, m_w_in[0].T, v_w_in[0].T, 200)
            updates.append(tuple(o.T[None] for o in out_t))
        else:
            updates.append(adam(*args))
    grads_out, deltas, new_m, new_v = zip(*updates)
    return (loss, grad_x[None], *grads_out, *deltas, *new_m, *new_v)
```

```python
import jax
import jax.numpy as jnp
from jax import lax
from jax.experimental import pallas as pl
from jax.experimental.pallas import tpu as pltpu

F32 = jnp.float32
BF16 = jnp.bfloat16
MESH = pl.DeviceIdType.MESH

D_MODEL = 1024
D_CONV = 1024
D_ATTN = 1024
D_KV = 128
D_QG = 2 * D_ATTN + 2 * D_KV
D_MIX = D_CONV + D_ATTN
D_PC = 4 * D_CONV
D_IN_PROJ = D_PC + 2 * D_ATTN + 2 * D_KV
ROW_Q = D_PC
ROW_KV = ROW_Q + D_ATTN
ROW_GA = ROW_KV + 2 * D_KV
N_HEADS = 16
HEAD_DIM = 64
HEADS_PER_KV = 8
BLK = 128
N_CHIPS = 4
RMS_EPS = 1e-5
SCALE = HEAD_DIM ** -0.5
SLOPES = tuple(2.0 ** (-8.0 * (h + 1) / N_HEADS) for h in range(N_HEADS))

ADAM_LR, ADAM_B1, ADAM_B2, ADAM_EPS, ADAM_WD, ADAM_STEP = 0.001, 0.9, 0.999, 1e-08, 0.01, 10

TOK_TILE = 256
MIB = 1 << 20


def _params(vmem_mib, semantics=None):
    return pltpu.CompilerParams(dimension_semantics=semantics, vmem_limit_bytes=vmem_mib * MIB)


def _nn(a, b):
    return jnp.dot(a, b, preferred_element_type=F32)


def _nt(a, b):
    return lax.dot_general(a, b, (((1,), (1,)), ((), ())), preferred_element_type=F32)


def _tn(a, b):
    return lax.dot_general(a, b, (((0,), (0,)), ((), ())), preferred_element_type=F32)


def _rstd(v):
    return lax.rsqrt(jnp.mean(v * v, axis=-1, keepdims=True) + RMS_EPS)


def _rms_bwd(g, xhat, rstd):
    return rstd * (g - xhat * jnp.mean(g * xhat, axis=-1, keepdims=True))


def _silu_and_grad(g):
    s = jax.nn.sigmoid(g)
    return g * s, s * (1.0 + g * (1.0 - s))


def _resident(shape):
    return pl.BlockSpec(shape, lambda *_: (0,) * len(shape), pipeline_mode=pl.Buffered(1))


def _xor(a, b):
    return a + b - 2 * a * b


def _cast_rows(src_ref, dst_ref, rows=32):
    def step(i, carry):
        sl = pl.ds(pl.multiple_of(i * rows, rows), rows)
        dst_ref[sl, :] = src_ref[sl, :].astype(dst_ref.dtype)
        return carry
    lax.fori_loop(0, src_ref.shape[0] // rows, step, 0)


def _ag_weights(wt_sh, wo_sh, cw_sh):
    wt_rows, wo_rows = wt_sh.shape[0], wo_sh.shape[0]
    wt_half, wo_half = wt_rows // 2, wo_rows // 2
    width = wt_sh.shape[1]

    def body(wt_ref, wo_ref, cw_ref, wt_out, wo_out, cw_out, f32_t, f32_o, own_t, own_o, land_t, land_o,
             send_sems, recv_sems, local_sems):
        x, y, c = lax.axis_index("x"), lax.axis_index("y"), lax.axis_index("c")
        j = 2 * x + y
        p1 = (_xor(x, c), _xor(y, 1 - c), c)
        p2 = (_xor(x, 1 - c), _xor(y, c), c)
        sib = (x, y, 1 - c)
        j1 = 2 * p1[0] + p1[1]
        j2 = 2 * p2[0] + p2[1]
        j3 = 3 - j

        def wt_rows_of(chip, half):
            return wt_out.at[pl.ds(pl.multiple_of(chip * wt_rows + half * wt_half, 16), wt_half), :]

        def wo_rows_of(chip, half):
            return wo_out.at[pl.ds(pl.multiple_of(chip * wo_rows + half * wo_half, 16), wo_half), :]

        def rcopy(k, src, dst, to):
            return pltpu.make_async_remote_copy(src_ref=src, dst_ref=dst, send_sem=send_sems.at[k],
                                                recv_sem=recv_sems.at[k], device_id=to, device_id_type=MESH)

        def lcopy(k, src, dst):
            cp = pltpu.make_async_copy(src, dst, local_sems.at[k])
            cp.start()
            return cp

        loads = [lcopy(0, wt_ref, f32_t), lcopy(1, wo_ref, f32_o)]
        local = [lcopy(2, cw_ref, cw_out.at[j])]
        for cp in loads:
            cp.wait()
        _cast_rows(f32_t, own_t)
        _cast_rows(f32_o, own_o)
        local += [lcopy(0, own_t, wt_out.at[pl.ds(pl.multiple_of(j * wt_rows, 16), wt_rows), :]),
                  lcopy(1, own_o, wo_out.at[pl.ds(pl.multiple_of(j * wo_rows, 16), wo_rows), :])]
        half_t = own_t.at[pl.ds(pl.multiple_of(c * wt_half, 16), wt_half), :]
        half_o = own_o.at[pl.ds(pl.multiple_of(c * wo_half, 16), wo_half), :]

        def publish(slot, chip, k_t, k_o, k_local):
            sends = [rcopy(k_t, land_t.at[slot], wt_rows_of(chip, c), sib), rcopy(k_o, land_o.at[slot], wo_rows_of(chip, c), sib)]
            for cp in sends:
                cp.start()
            local.extend([lcopy(k_local, land_t.at[slot], wt_rows_of(chip, c)),
                          lcopy(k_local + 1, land_o.at[slot], wo_rows_of(chip, c))])
            return sends

        hop1 = [rcopy(0, half_t, land_t.at[0], p1), rcopy(1, half_o, land_o.at[0], p1), rcopy(2, cw_ref, cw_out.at[j], p1)]
        for cp in hop1:
            cp.start()
        rcopy(0, half_t, land_t.at[0], p1).wait_recv()
        rcopy(1, half_o, land_o.at[0], p1).wait_recv()
        rcopy(2, cw_ref, cw_out.at[j1], p1).wait_recv()
        hop2 = [rcopy(3, half_t, land_t.at[1], p2), rcopy(5, half_o, land_o.at[1], p2), rcopy(7, cw_ref, cw_out.at[j], p2),
                rcopy(4, land_t.at[0], land_t.at[2], p2), rcopy(6, land_o.at[0], land_o.at[2], p2),
                rcopy(8, cw_out.at[j1], cw_out.at[j1], p2)]
        for cp in hop2:
            cp.start()
        swaps = publish(0, j1, 9, 12, 3)
        rcopy(3, half_t, land_t.at[1], p2).wait_recv()
        rcopy(5, half_o, land_o.at[1], p2).wait_recv()
        rcopy(7, cw_ref, cw_out.at[j2], p2).wait_recv()
        swaps += publish(1, j2, 10, 13, 5)
        rcopy(4, half_t, land_t.at[2], p2).wait_recv()
        rcopy(6, half_o, land_o.at[2], p2).wait_recv()
        rcopy(8, cw_ref, cw_out.at[j3], p2).wait_recv()
        swaps += publish(2, j3, 11, 14, 7)
        for k, chip in ((9, j2), (10, j1), (11, j3)):
            rcopy(k, half_t, wt_rows_of(chip, 1 - c), sib).wait_recv()
        for k, chip in ((12, j2), (13, j1), (14, j3)):
            rcopy(k, half_o, wo_rows_of(chip, 1 - c), sib).wait_recv()
        for cp in hop1 + hop2 + swaps:
            cp.wait_send()
        for cp in local:
            cp.wait()

    any_spec = pl.BlockSpec(memory_space=pl.ANY)
    return pl.pallas_call(
        body, name="ag_weights",
        out_shape=(jax.ShapeDtypeStruct((N_CHIPS * wt_rows, width), BF16),
                   jax.ShapeDtypeStruct((N_CHIPS * wo_rows, width), BF16),
                   jax.ShapeDtypeStruct((N_CHIPS,) + cw_sh.shape, cw_sh.dtype)),
        in_specs=[any_spec, any_spec, any_spec],
        out_specs=(any_spec, any_spec, any_spec),
        scratch_shapes=[pltpu.VMEM((wt_rows, width), F32), pltpu.VMEM((wo_rows, width), F32),
                        pltpu.VMEM((wt_rows, width), BF16), pltpu.VMEM((wo_rows, width), BF16),
                        pltpu.VMEM((3, wt_half, width), BF16), pltpu.VMEM((3, wo_half, width), BF16),
                        pltpu.SemaphoreType.DMA((15,)), pltpu.SemaphoreType.DMA((15,)), pltpu.SemaphoreType.DMA((9,))],
        compiler_params=_params(32),
    )(wt_sh, wo_sh, cw_sh)


def _fwd_in(x, norm_in, wt):
    seq = x.shape[0]
    tile = TOK_TILE

    def body(x_ref, g_ref, wt_ref, h_ref, pc_ref, q_ref, kv_ref, ga_ref):
        xv = x_ref[...]
        h = (xv * _rstd(xv) * g_ref[...]).astype(BF16)
        h_ref[...] = h
        for blk in range(D_PC // D_CONV):
            pc_ref[:, blk * D_CONV:(blk + 1) * D_CONV] = _nt(h, wt_ref[blk * D_CONV:(blk + 1) * D_CONV, :])
        q_ref[...] = _nt(h, wt_ref[ROW_Q:ROW_KV, :])
        kv_ref[...] = _nt(h, wt_ref[ROW_KV:ROW_GA, :])
        ga_ref[...] = _nt(h, wt_ref[ROW_GA:D_IN_PROJ, :])

    def row(width):
        return pl.BlockSpec((tile, width), lambda i: (i, 0))

    return pl.pallas_call(
        body, name="fwd_in", grid=(seq // tile,),
        out_shape=(jax.ShapeDtypeStruct((seq, D_MODEL), BF16), jax.ShapeDtypeStruct((seq, D_PC), F32),
                   jax.ShapeDtypeStruct((seq, D_ATTN), F32), jax.ShapeDtypeStruct((seq, 2 * D_KV), F32),
                   jax.ShapeDtypeStruct((seq, D_ATTN), F32)),
        in_specs=[row(D_MODEL), _resident((1, D_MODEL)), _resident(wt.shape)],
        out_specs=(row(D_MODEL), row(D_PC), row(D_ATTN), row(2 * D_KV), row(D_ATTN)),
        compiler_params=_params(48, ("arbitrary",)),
    )(x, norm_in, wt)


def _conv_core(pc_ref, zbuf, cw_ref):
    tile = pc_ref.shape[0]
    cb = pc_ref[:, 0:D_CONV]
    cc = pc_ref[:, D_CONV:2 * D_CONV]
    cu = pc_ref[:, 2 * D_CONV:3 * D_CONV]
    z = cc * cu
    zbuf[8:tile + 8, :] = z
    z1 = zbuf[7:tile + 7, :]
    z2 = zbuf[6:tile + 6, :]
    conv = cw_ref[0:1, :] * z2 + cw_ref[1:2, :] * z1 + cw_ref[2:3, :] * z
    return cb, cc, cu, z, z1, z2, conv


def _conv_fwd(pc, conv_w, norm_conv_out):
    seq = pc.shape[0]
    tile = TOK_TILE

    def body(pc_ref, cw_ref, gn_ref, oc_ref, zbuf):
        @pl.when(pl.program_id(0) == 0)
        def _():
            zbuf[0:8, :] = jnp.zeros((8, D_CONV), F32)

        cb, _, _, _, _, _, conv = _conv_core(pc_ref, zbuf, cw_ref)
        yc = cb * conv
        silu, _ = _silu_and_grad(pc_ref[:, 3 * D_CONV:4 * D_CONV])
        oc_ref[...] = (yc * _rstd(yc) * gn_ref[...] * silu).astype(BF16)
        zbuf[0:8, :] = zbuf[tile:tile + 8, :]

    return pl.pallas_call(
        body, name="conv_fwd", grid=(seq // tile,),
        out_shape=jax.ShapeDtypeStruct((seq, D_CONV), BF16),
        in_specs=[pl.BlockSpec((tile, D_PC), lambda i: (i, 0)), _resident(conv_w.shape), _resident((1, D_CONV))],
        out_specs=pl.BlockSpec((tile, D_CONV), lambda i: (i, 0)),
        scratch_shapes=[pltpu.VMEM((tile + 8, D_CONV), F32)],
        compiler_params=_params(40, ("arbitrary",)),
    )(pc, conv_w, norm_conv_out)


def _band_geometry(block_index):
    qi = lax.broadcasted_iota(jnp.int32, (BLK, BLK), 0)
    kp = lax.broadcasted_iota(jnp.int32, (BLK, BLK), 1)
    use_cur = kp <= qi
    dist = jnp.where(use_cur, qi - kp, qi - kp + BLK).astype(F32)
    valid = use_cur | (block_index > 0)
    return use_cur, dist, valid


def _block_diag(cur, prev, group):
    lane = lax.broadcasted_iota(jnp.int32, cur.shape, 1)

    def halves(t):
        other = pltpu.roll(t, 64, 1)
        lo, hi = (t, other) if group == 0 else (other, t)
        return jnp.where(lane < 64, lo, 0.0), jnp.where(lane >= 64, hi, 0.0)

    return jnp.concatenate(halves(cur) + halves(prev), axis=0).astype(BF16)


def _merge(s4, use_cur):
    return (jnp.where(use_cur, s4[:, 0:BLK], s4[:, 2 * BLK:3 * BLK]),
            jnp.where(use_cur, s4[:, BLK:2 * BLK], s4[:, 3 * BLK:4 * BLK]))


def _split(a, b, use_cur):
    return jnp.concatenate([jnp.where(use_cur, a, 0.0), jnp.where(use_cur, b, 0.0),
                            jnp.where(use_cur, 0.0, a), jnp.where(use_cur, 0.0, b)], axis=1)


def _softmax_head(s, head, sink, dist, valid):
    sc = jnp.where(valid, s - SLOPES[head] * dist, -jnp.inf)
    m = jnp.maximum(jnp.max(sc, axis=-1, keepdims=True), sink)
    p = jnp.exp(sc - m)
    es = jnp.exp(sink - m)
    inv = 1.0 / (jnp.sum(p, axis=-1, keepdims=True) + es)
    return p * inv, es * inv


def _attn_probs(q_ref, kvc_ref, kvp_ref, sink_ref, geometry):
    use_cur, dist, valid = geometry
    groups = range(N_HEADS // HEADS_PER_KV)
    kbd = [_block_diag(kvc_ref[:, 0:D_KV], kvp_ref[:, 0:D_KV], g) for g in groups]
    vbd = [_block_diag(kvc_ref[:, D_KV:2 * D_KV], kvp_ref[:, D_KV:2 * D_KV], g) for g in groups]
    qps = [(q_ref[:, j * 128:(j + 1) * 128] * SCALE).astype(BF16) for j in range(N_HEADS // 2)]
    scores = []
    for j, qp in enumerate(qps):
        scores += _merge(_nt(qp, kbd[j // 4]), use_cur)
    sinks = [sink_ref[0, h] for h in range(N_HEADS)]
    scores = [jnp.where(valid, s - SLOPES[h] * dist, -jnp.inf) for h, s in enumerate(scores)]
    maxes = [jnp.maximum(jnp.max(s, axis=-1, keepdims=True), sinks[h]) for h, s in enumerate(scores)]
    exps = [jnp.exp(s - m) for s, m in zip(scores, maxes)]
    sink_exps = [jnp.exp(sinks[h] - m) for h, m in enumerate(maxes)]
    invs = [1.0 / (jnp.sum(e, axis=-1, keepdims=True) + se) for e, se in zip(exps, sink_exps)]
    probs = [e * inv for e, inv in zip(exps, invs)]
    sink_probs = [se * inv for se, inv in zip(sink_exps, invs)]
    return qps, kbd, vbd, probs, sink_probs


def _kv_specs(n_blocks, reverse):
    def blk(i):
        return (n_blocks - 1 - i) if reverse else i
    cur = pl.BlockSpec((BLK, 2 * D_KV), lambda i: (blk(i), 0))
    prev = pl.BlockSpec((BLK, 2 * D_KV), lambda i: (jnp.maximum(blk(i) - 1, 0), 0))
    return cur, prev


def _attn_fwd(q, kv, ga, sinks, norm_attn_out):
    seq = q.shape[0]
    n_blocks = seq // BLK

    def body(q_ref, kvc_ref, kvp_ref, ga_ref, sink_ref, gn_ref, ya_ref, oa_ref):
        geometry = _band_geometry(pl.program_id(0))
        _, _, vbd, probs, _ = _attn_probs(q_ref, kvc_ref, kvp_ref, sink_ref, geometry)
        p4s = [_split(probs[2 * j], probs[2 * j + 1], geometry[0]).astype(BF16) for j in range(N_HEADS // 2)]
        ya = jnp.concatenate([_nn(p4, vbd[j // 4]) for j, p4 in enumerate(p4s)], axis=1)
        ya_ref[...] = ya
        silu, _ = _silu_and_grad(ga_ref[...])
        oa_ref[...] = (ya * _rstd(ya) * gn_ref[...] * silu).astype(BF16)

    row = pl.BlockSpec((BLK, D_ATTN), lambda i: (i, 0))
    kv_cur, kv_prev = _kv_specs(n_blocks, reverse=False)
    return pl.pallas_call(
        body, name="attn_fwd", grid=(n_blocks,),
        out_shape=(jax.ShapeDtypeStruct((seq, D_ATTN), F32), jax.ShapeDtypeStruct((seq, D_ATTN), BF16)),
        in_specs=[row, kv_cur, kv_prev, row, pl.BlockSpec(memory_space=pltpu.SMEM), _resident((1, D_ATTN))],
        out_specs=(row, row),
        compiler_params=_params(32, ("arbitrary",)),
    )(q, kv, kv, ga, sinks, norm_attn_out)


def _out_proj_loss(x, oc, oa, wo, norm_final, target):
    seq = x.shape[0]
    tile = TOK_TILE

    def body(x_ref, oc_ref, oa_ref, wo_ref, gf_ref, t_ref, dx2_ref, doc_ref, doa_ref, gwo_ref, gnf_ref, loss_ref):
        @pl.when(pl.program_id(0) == 0)
        def _():
            gwo_ref[...] = jnp.zeros_like(gwo_ref)
            gnf_ref[...] = jnp.zeros_like(gnf_ref)
            loss_ref[...] = jnp.zeros_like(loss_ref)

        oc, oa = oc_ref[...], oa_ref[...]
        x2 = x_ref[...] + _nn(oc, wo_ref[0:D_CONV, :]) + _nn(oa, wo_ref[D_CONV:D_MIX, :])
        r = _rstd(x2)
        xhat = x2 * r
        err = xhat * gf_ref[...] - t_ref[...]
        loss_ref[...] += jnp.sum(err * err, axis=0, keepdims=True) * (0.5 / D_MODEL)
        dy = err * (1.0 / D_MODEL)
        gnf_ref[...] += jnp.sum(dy * xhat, axis=0, keepdims=True)
        dx2 = _rms_bwd(dy * gf_ref[...], xhat, r)
        dx2_ref[...] = dx2
        db = dx2.astype(BF16)
        doc_ref[...] = _nt(db, wo_ref[0:D_CONV, :])
        doa_ref[...] = _nt(db, wo_ref[D_CONV:D_MIX, :])
        gwo_ref[0:D_CONV, :] += _tn(oc, db)
        gwo_ref[D_CONV:D_MIX, :] += _tn(oa, db)

    row = pl.BlockSpec((tile, D_MODEL), lambda i: (i, 0))
    vec = pl.BlockSpec((1, D_MODEL), lambda i: (0, 0))
    return pl.pallas_call(
        body, name="out_proj_loss", grid=(seq // tile,),
        out_shape=(jax.ShapeDtypeStruct((seq, D_MODEL), F32), jax.ShapeDtypeStruct((seq, D_CONV), F32),
                   jax.ShapeDtypeStruct((seq, D_ATTN), F32), jax.ShapeDtypeStruct((D_MIX, D_MODEL), F32),
                   jax.ShapeDtypeStruct((1, D_MODEL), F32), jax.ShapeDtypeStruct((1, D_MODEL), F32)),
        in_specs=[row, row, row, _resident(wo.shape), _resident((1, D_MODEL)), row],
        out_specs=(row, row, row, pl.BlockSpec((D_MIX, D_MODEL), lambda i: (0, 0)), vec, vec),
        compiler_params=_params(48, ("arbitrary",)),
    )(x, oc, oa, wo, norm_final, target)


def _conv_bwd(pc, doc, conv_w, norm_conv_out):
    seq = pc.shape[0]
    tile = TOK_TILE
    n_tiles = seq // tile

    def body(pc_ref, hcc_ref, hcu_ref, doc_ref, cw_ref, gn_ref, dpc_ref, gnc_ref, gcw_ref, zbuf, dbuf):
        step = pl.program_id(0)

        @pl.when(step == 0)
        def _():
            gnc_ref[...] = jnp.zeros_like(gnc_ref)
            gcw_ref[...] = jnp.zeros_like(gcw_ref)
            dbuf[tile:tile + 8, :] = jnp.zeros((8, D_CONV), F32)

        is_first_tile = step == n_tiles - 1
        zbuf[0:8, :] = jnp.where(is_first_tile, 0.0, hcc_ref[...] * hcu_ref[...])
        cb, cc, cu, z, z1, z2, conv = _conv_core(pc_ref, zbuf, cw_ref)
        gc = pc_ref[:, 3 * D_CONV:4 * D_CONV]
        silu, dsilu = _silu_and_grad(gc)
        yc = cb * conv
        r = _rstd(yc)
        xhat = yc * r
        do = doc_ref[...]
        dn = do * silu
        dpc_ref[:, 3 * D_CONV:4 * D_CONV] = (do * (xhat * gn_ref[...]) * dsilu).astype(BF16)
        gnc_ref[...] += jnp.sum(dn * xhat, axis=0, keepdims=True)
        dyc = _rms_bwd(dn * gn_ref[...], xhat, r)
        dpc_ref[:, 0:D_CONV] = (dyc * conv).astype(BF16)
        dconv = dyc * cb
        gcw_ref[0:1, :] += jnp.sum(dconv * z2, axis=0, keepdims=True)
        gcw_ref[1:2, :] += jnp.sum(dconv * z1, axis=0, keepdims=True)
        gcw_ref[2:3, :] += jnp.sum(dconv * z, axis=0, keepdims=True)
        dbuf[0:tile, :] = dconv
        dz = cw_ref[2:3, :] * dconv + cw_ref[1:2, :] * dbuf[1:tile + 1, :] + cw_ref[0:1, :] * dbuf[2:tile + 2, :]
        dpc_ref[:, D_CONV:2 * D_CONV] = (dz * cu).astype(BF16)
        dpc_ref[:, 2 * D_CONV:3 * D_CONV] = (dz * cc).astype(BF16)
        dbuf[tile:tile + 8, :] = dbuf[0:8, :]

    def rev(i):
        return n_tiles - 1 - i

    def halo(col_block):
        return pl.BlockSpec((8, D_CONV), lambda i: (jnp.maximum(rev(i) * (tile // 8) - 1, 0), col_block))

    return pl.pallas_call(
        body, name="conv_bwd", grid=(n_tiles,),
        out_shape=(jax.ShapeDtypeStruct((seq, D_PC), BF16), jax.ShapeDtypeStruct((1, D_CONV), F32),
                   jax.ShapeDtypeStruct((8, D_CONV), F32)),
        in_specs=[pl.BlockSpec((tile, D_PC), lambda i: (rev(i), 0)), halo(1), halo(2),
                  pl.BlockSpec((tile, D_CONV), lambda i: (rev(i), 0)), _resident(conv_w.shape), _resident((1, D_CONV))],
        out_specs=(pl.BlockSpec((tile, D_PC), lambda i: (rev(i), 0)), pl.BlockSpec((1, D_CONV), lambda i: (0, 0)),
                   pl.BlockSpec((8, D_CONV), lambda i: (0, 0))),
        scratch_shapes=[pltpu.VMEM((tile + 8, D_CONV), F32), pltpu.VMEM((tile + 8, D_CONV), F32)],
        compiler_params=_params(48, ("arbitrary",)),
    )(pc, pc, pc, doc, conv_w, norm_conv_out)


def _attn_bwd(q, kv, ga, ya, doa, sinks, norm_attn_out):
    seq = q.shape[0]
    n_blocks = seq // BLK

    def body(q_ref, kvc_ref, kvp_ref, ga_ref, ya_ref, doa_ref, sink_ref, gn_ref,
             dqg_ref, gna_ref, gs_ref, carry, dya_buf):
        step = pl.program_id(0)

        @pl.when(step == 0)
        def _():
            gna_ref[...] = jnp.zeros_like(gna_ref)
            gs_ref[...] = jnp.zeros_like(gs_ref)
            carry[...] = jnp.zeros_like(carry)

        ya = ya_ref[...]
        r = _rstd(ya)
        xhat = ya * r
        silu, dsilu = _silu_and_grad(ga_ref[...])
        do = doa_ref[...]
        dn = do * silu
        dqg_ref[:, D_ATTN:2 * D_ATTN] = (do * (xhat * gn_ref[...]) * dsilu).astype(BF16)
        gna_ref[...] += jnp.sum(dn * xhat, axis=0, keepdims=True)
        dya_buf[...] = _rms_bwd(dn * gn_ref[...], xhat, r).astype(BF16)

        geometry = _band_geometry(n_blocks - 1 - step)
        use_cur = geometry[0]
        lane = lax.broadcasted_iota(jnp.int32, (BLK, 128), 1)
        sink_lane = lax.broadcasted_iota(jnp.int32, (1, 128), 1)
        gsink = jnp.zeros((1, 128), F32)

        def fold(bd):
            return (jnp.where(lane < 64, bd[0:BLK], 0.0) + jnp.where(lane >= 64, bd[BLK:2 * BLK], 0.0),
                    jnp.where(lane < 64, bd[2 * BLK:3 * BLK], 0.0) + jnp.where(lane >= 64, bd[3 * BLK:4 * BLK], 0.0))

        pairs = range(N_HEADS // 2)
        qps, kbd, vbd, probs, sink_probs = _attn_probs(q_ref, kvc_ref, kvp_ref, sink_ref, geometry)
        dyps = [dya_buf[:, j * 128:(j + 1) * 128] for j in pairs]
        dps = []
        for j in pairs:
            dps += _merge(_nt(dyps[j], vbd[j // 4]), use_cur)
        deltas = [jnp.sum(p * dp, axis=-1, keepdims=True) for p, dp in zip(probs, dps)]
        dss = [p * (dp - delta) for p, dp, delta in zip(probs, dps, deltas)]
        for h in range(N_HEADS):
            dsink = -jnp.sum(sink_probs[h] * deltas[h], axis=0, keepdims=True)
            gsink = gsink + jnp.where(sink_lane == h, dsink, 0.0)
        gs_ref[...] += gsink
        ds4s = [_split(dss[2 * j], dss[2 * j + 1], use_cur).astype(BF16) for j in pairs]
        p4s = [_split(probs[2 * j], probs[2 * j + 1], use_cur).astype(BF16) for j in pairs]
        dqg_ref[:, 0:D_ATTN] = jnp.concatenate([_nn(ds4s[j], kbd[j // 4]) * SCALE for j in pairs], axis=1).astype(BF16)
        sums = []
        for group in range(N_HEADS // HEADS_PER_KV):
            acc = [jnp.zeros((BLK, 128), F32) for _ in range(4)]
            for j in range(group * 4, group * 4 + 4):
                for slot, part in enumerate(fold(_tn(ds4s[j], qps[j])) + fold(_tn(p4s[j], dyps[j]))):
                    acc[slot] = acc[slot] + part
            sums.append([a + pltpu.roll(a, 64, 1) for a in acc])
        dk_cur, dk_prev, dv_cur, dv_prev = (jnp.where(lane < 64, a, b) for a, b in zip(sums[0], sums[1]))
        dqg_ref[:, 2 * D_ATTN:2 * D_ATTN + 2 * D_KV] = (jnp.concatenate([dk_cur, dv_cur], axis=1) + carry[...]).astype(BF16)
        carry[...] = jnp.concatenate([dk_prev, dv_prev], axis=1)

    row = pl.BlockSpec((BLK, D_ATTN), lambda i: (n_blocks - 1 - i, 0))
    kv_cur, kv_prev = _kv_specs(n_blocks, reverse=True)
    return pl.pallas_call(
        body, name="attn_bwd", grid=(n_blocks,),
        out_shape=(jax.ShapeDtypeStruct((seq, D_QG), BF16), jax.ShapeDtypeStruct((1, D_ATTN), F32),
                   jax.ShapeDtypeStruct((1, 128), F32)),
        in_specs=[row, kv_cur, kv_prev, row, row, row, pl.BlockSpec(memory_space=pltpu.SMEM), _resident((1, D_ATTN))],
        out_specs=(pl.BlockSpec((BLK, D_QG), lambda i: (n_blocks - 1 - i, 0)),
                   pl.BlockSpec((1, D_ATTN), lambda i: (0, 0)), pl.BlockSpec((1, 128), lambda i: (0, 0))),
        scratch_shapes=[pltpu.VMEM((BLK, 2 * D_KV), F32), pltpu.VMEM((BLK, D_ATTN), BF16)],
        compiler_params=_params(32, ("arbitrary",)),
    )(q, kv, kv, ga, ya, doa, sinks, norm_attn_out)


GBLK = 256
GSUB = 64
PAIR_RING = 4
LAG_PAIR, LAG_HOP1, LAG_HOP2 = 1, 4, 8


def _bwd_in(dpc, dqg, h, wt, x, norm_in, dx2, small):
    seq = x.shape[0]
    n_blk = D_IN_PROJ // GBLK
    n_slots = (n_blk + 1) // 2
    n_sub = GBLK // GSUB
    chip_rows = D_IN_PROJ // N_CHIPS
    tile = TOK_TILE
    n_tiles = seq // tile
    n_steps = n_blk + max(n_tiles, LAG_HOP2)
    chunk = min(seq, 512)
    blk_q, blk_kv, blk_ga = ROW_Q // GBLK, ROW_KV // GBLK, ROW_GA // GBLK

    def body(dpc_ref, dqg_ref, wt_ref, h_ref, x_ref, g_ref, dx2_ref, small_ref, gx_ref, small_sum, gwt_sh,
             dh_acc, gni, keep, pbuf, xbuf, land, land2, small_land,
             pair_send, pair_recv, h1_send, h1_recv, h2_send, h2_recv, sw_send, sw_recv, sm_send, sm_recv, out_sem):
        step = pl.program_id(0)
        x_i, y_i, c = lax.axis_index("x"), lax.axis_index("y"), lax.axis_index("c")
        me = 4 * x_i + 2 * y_i + c
        j = 2 * x_i + y_i
        pa = (_xor(x_i, 1 - c), _xor(y_i, c), c)
        pb = (_xor(x_i, c), _xor(y_i, 1 - c), c)
        sib = (x_i, y_i, 1 - c)
        ja = 2 * pa[0] + pa[1]
        jb = 2 * pb[0] + pb[1]
        jd = 3 - j

        def remote(src, dst, send, recv, to):
            return pltpu.make_async_remote_copy(src_ref=src, dst_ref=dst, send_sem=send, recv_sem=recv,
                                                device_id=to, device_id_type=MESH)

        def piece(ref, slot, u):
            return ref.at[slot, pl.ds(u * GSUB, GSUB), :]

        def chip_rows_at(ref, local):
            return ref.at[pl.ds(pl.multiple_of(local, GSUB), GSUB), :]

        def pair_copy(slot):
            return remote(pbuf.at[slot % PAIR_RING], land.at[slot], pair_send.at[slot], pair_recv.at[slot], sib)

        def h1_copy(slot, u):
            k = slot * n_sub + u
            return remote(piece(xbuf, slot, u), piece(xbuf, slot, u), h1_send.at[k], h1_recv.at[k], pa)

        def h2_copy(slot, u, local):
            k = slot * n_sub + u
            return remote(piece(xbuf, slot, u), chip_rows_at(land2, local), h2_send.at[k], h2_recv.at[k], pb)

        def sw_copy(slot, u, local):
            k = slot * n_sub + u
            return remote(piece(keep, slot, u), chip_rows_at(gwt_sh, local), sw_send.at[k], sw_recv.at[k], sib)

        def owned(b):
            return (b >= 0) & (b < n_blk) & (b % 2 == c)

        def chip_of(b, u):
            row = b * GBLK + u * GSUB
            chip = row // chip_rows
            return chip, row - chip * chip_rows

        @pl.when(step == 0)
        def _():
            dh_acc[...] = jnp.zeros_like(dh_acc)
            gni[...] = jnp.zeros_like(gni)

        @pl.when(step < n_blk)
        def _():
            from_pc = step < blk_q
            block = _tn(jnp.where(from_pc, dpc_ref[...], dqg_ref[...]), h_ref[...])
            for t in range(0, seq, chunk):
                d = jnp.where(from_pc, dpc_ref[t:t + chunk, :], dqg_ref[t:t + chunk, :])
                dh_acc[t:t + chunk, :] += _nn(d, wt_ref[...])
            slot = step // 2

            @pl.when(step % 2 == c)
            def _():
                keep[slot] = block

            @pl.when(step % 2 != c)
            def _():
                @pl.when(slot >= PAIR_RING)
                def _():
                    pair_copy(slot - PAIR_RING).wait_send()
                pbuf[slot % PAIR_RING] = block.astype(BF16)
                pair_copy(slot).start()

        b1 = step - LAG_PAIR

        @pl.when(owned(b1))
        def _():
            slot = b1 // 2
            pair_copy(slot).wait_recv()
            _accumulate(keep.at[slot], land.at[slot])
            for u in range(n_sub):
                chip, _ = chip_of(b1, u)

                @pl.when((chip == ja) | (chip == jd))
                def _(u=u):
                    _cast_rows(piece(keep, slot, u), piece(xbuf, slot, u))
                    h1_copy(slot, u).start()

        b2 = step - LAG_HOP1

        @pl.when(owned(b2))
        def _():
            slot = b2 // 2
            for u in range(n_sub):
                chip, local = chip_of(b2, u)

                @pl.when((chip == j) | (chip == jb))
                def _(u=u, chip=chip, local=local):
                    h1_copy(slot, u).wait_recv()
                    _accumulate(piece(keep, slot, u), piece(xbuf, slot, u))

                    @pl.when(chip == jb)
                    def _():
                        _cast_rows(piece(keep, slot, u), piece(xbuf, slot, u))
                        h2_copy(slot, u, local).start()

                @pl.when((chip == ja) | (chip == jd))
                def _(u=u):
                    h1_copy(slot, u).wait_send()

        b3 = step - LAG_HOP2

        @pl.when(owned(b3))
        def _():
            slot = b3 // 2
            for u in range(n_sub):
                chip, local = chip_of(b3, u)

                @pl.when(chip == j)
                def _(u=u, local=local):
                    h2_copy(slot, u, local).wait_recv()
                    _accumulate(piece(keep, slot, u), chip_rows_at(land2, local))
                    mine = pltpu.make_async_copy(piece(keep, slot, u), chip_rows_at(gwt_sh, local), out_sem.at[0])
                    mine.start()
                    sw_copy(slot, u, local).start()
                    mine.wait()

                @pl.when(chip == jb)
                def _(u=u, local=local):
                    h2_copy(slot, u, local).wait_send()

        e = step - n_blk

        @pl.when((e >= 0) & (e < n_tiles))
        def _():
            dh = dh_acc[pl.ds(pl.multiple_of(e * tile, tile), tile), :]
            xv = x_ref[...]
            r = _rstd(xv)
            xhat = xv * r
            gni[...] += jnp.sum(dh * xhat, axis=0, keepdims=True)
            gx_ref[...] = _rms_bwd(dh * g_ref[...], xhat, r) + dx2_ref[...]

        @pl.when(step == n_steps - 1)
        def _():
            small_land[me] = small_ref[...]
            small_land[me, 0:1, :] = gni[...]
            others = [(dx, dy, dc) for dx in (0, 1) for dy in (0, 1) for dc in (0, 1)][1:]
            sends = [remote(small_land.at[me], small_land.at[me], sm_send.at[k], sm_recv.at[k],
                            (_xor(x_i, dx), _xor(y_i, dy), _xor(c, dc))) for k, (dx, dy, dc) in enumerate(others)]
            for cp in sends:
                cp.start()
            n_sent = (n_blk + c) // 2
            for slot in range(n_slots):
                @pl.when((slot < n_sent) & (slot >= n_sent - PAIR_RING))
                def _(slot=slot):
                    pair_copy(slot).wait_send()
            for b in range(n_blk):
                for u in range(n_sub):
                    chip, local = divmod(b * GBLK + u * GSUB, chip_rows)

                    @pl.when((j == chip) & (c == b % 2))
                    def _(b=b, u=u, local=local):
                        sw_copy(b // 2, u, local).wait_send()

                    @pl.when((j == chip) & (c != b % 2))
                    def _(b=b, u=u, local=local):
                        sw_copy(b // 2, u, local).wait_recv()
            for cp in sends:
                cp.wait_recv()
            total = small_land[0]
            for dev in range(1, 8):
                total = total + small_land[dev]
            small_sum[...] = total
            for cp in sends:
                cp.wait_send()

    def dqg_block(i):
        q_blk = jnp.clip(i - blk_q, 0, blk_kv - blk_q - 1)
        ga_blk = (D_ATTN // GBLK) + jnp.clip(i - blk_ga, 0, n_blk - blk_ga - 1)
        return jnp.where(i < blk_kv, q_blk, jnp.where(i == blk_kv, 2 * D_ATTN // GBLK, ga_blk))

    def tok(i):
        return (jnp.clip(i - n_blk, 0, n_tiles - 1), 0)

    n_piece = n_slots * n_sub
    dma = pltpu.SemaphoreType.DMA
    return pl.pallas_call(
        body, name="bwd_in", grid=(n_steps,),
        out_shape=(jax.ShapeDtypeStruct((seq, D_MODEL), F32), jax.ShapeDtypeStruct(small.shape, F32),
                   jax.ShapeDtypeStruct((chip_rows, D_MODEL), F32)),
        in_specs=[pl.BlockSpec((seq, GBLK), lambda i: (0, jnp.clip(i, 0, blk_q - 1))),
                  pl.BlockSpec((seq, GBLK), lambda i: (0, dqg_block(i))),
                  pl.BlockSpec((GBLK, D_MODEL), lambda i: (jnp.clip(i, 0, n_blk - 1), 0)),
                  _resident(h.shape),
                  pl.BlockSpec((tile, D_MODEL), tok), _resident((1, D_MODEL)), pl.BlockSpec((tile, D_MODEL), tok),
                  _resident(small.shape)],
        out_specs=(pl.BlockSpec((tile, D_MODEL), tok), pl.BlockSpec(small.shape, lambda i: (0, 0)),
                   pl.BlockSpec(memory_space=pl.ANY)),
        scratch_shapes=[pltpu.VMEM((seq, D_MODEL), F32), pltpu.VMEM((1, D_MODEL), F32),
                        pltpu.VMEM((n_slots, GBLK, D_MODEL), F32), pltpu.VMEM((PAIR_RING, GBLK, D_MODEL), BF16),
                        pltpu.VMEM((n_slots, GBLK, D_MODEL), BF16), pltpu.VMEM((n_slots, GBLK, D_MODEL), BF16),
                        pltpu.VMEM((chip_rows, D_MODEL), BF16), pltpu.VMEM((8,) + small.shape, F32),
                        dma((n_slots,)), dma((n_slots,)), dma((n_piece,)), dma((n_piece,)), dma((n_piece,)),
                        dma((n_piece,)), dma((n_piece,)), dma((n_piece,)), dma((7,)), dma((7,)), dma((1,))],
        compiler_params=_params(62, ("arbitrary",)),
    )(dpc, dqg, wt, h, x, norm_in, dx2, small)


def _accumulate(dst_ref, src_ref, rows=16):
    def step(i, carry):
        sl = pl.ds(pl.multiple_of(i * rows, rows), rows)
        dst_ref[sl, :] = dst_ref[sl, :] + src_ref[sl, :].astype(F32)
        return carry
    lax.fori_loop(0, dst_ref.shape[0] // rows, step, 0)


def _rs_wout(gwo):
    o_rows = gwo.shape[0] // N_CHIPS
    o_half = o_rows // 2
    width = gwo.shape[1]

    def body(gwo_ref, gwo_sh, acc_o, stage_o, sb_o, r1o, r2o, r3o, send_sems, recv_sems, local_sems):
        x, y, c = lax.axis_index("x"), lax.axis_index("y"), lax.axis_index("c")
        j = 2 * x + y
        pa = (_xor(x, 1 - c), _xor(y, c), c)
        pb = (_xor(x, c), _xor(y, 1 - c), c)
        sib = (x, y, 1 - c)
        ja = 2 * pa[0] + pa[1]
        jb = 2 * pb[0] + pb[1]
        jd = 3 - j

        def rcopy(k, src, dst, to):
            return pltpu.make_async_remote_copy(src_ref=src, dst_ref=dst, send_sem=send_sems.at[k],
                                                recv_sem=recv_sems.at[k], device_id=to, device_id_type=MESH)

        def o_rows_of(chip, half):
            return gwo_ref.at[pl.ds(pl.multiple_of(chip * o_rows + half * o_half, 8), o_half), :]

        def load(src, dst):
            cp = pltpu.make_async_copy(src, dst, local_sems.at[0])
            cp.start()
            return cp

        order = (ja, jd, jb, j)
        sib_order = (jb, jd, ja, j)
        pair_sends = []
        for s, chip in enumerate(sib_order):
            load(o_rows_of(chip, 1 - c), stage_o).wait()
            _cast_rows(stage_o, sb_o.at[s])
            pair_sends.append(rcopy(s, sb_o.at[s], r1o.at[s], sib))
            pair_sends[-1].start()

        def resend(s, k, landing, to):
            pair_sends[s].wait_send()
            _cast_rows(acc_o.at[s], sb_o.at[s])
            cp = rcopy(k, sb_o.at[s], landing, to)
            cp.start()
            return cp

        hop1 = []
        for s, chip in enumerate(order):
            load(o_rows_of(chip, c), acc_o.at[s]).wait()
            rcopy(s, sb_o.at[s], r1o.at[s], sib).wait_recv()
            _accumulate(acc_o.at[s], r1o.at[s])
            if s < 2:
                hop1.append(resend(s, 4 + s, r2o.at[s], pa))

        rcopy(5, sb_o.at[1], r2o.at[1], pa).wait_recv()
        _accumulate(acc_o.at[2], r2o.at[1])
        hop2 = resend(2, 6, r3o, pb)
        rcopy(4, sb_o.at[0], r2o.at[0], pa).wait_recv()
        _accumulate(acc_o.at[3], r2o.at[0])
        rcopy(6, sb_o.at[2], r3o, pb).wait_recv()
        _accumulate(acc_o.at[3], r3o)

        out_o = gwo_sh.at[pl.ds(pl.multiple_of(c * o_half, 8), o_half), :]
        keep = load(acc_o.at[3], out_o)
        swap = rcopy(7, acc_o.at[3], out_o, sib)
        swap.start()
        rcopy(7, acc_o.at[3], gwo_sh.at[pl.ds(pl.multiple_of((1 - c) * o_half, 8), o_half), :], sib).wait_recv()
        for cp in [pair_sends[3]] + hop1 + [hop2, swap]:
            cp.wait_send()
        keep.wait()

    any_spec = pl.BlockSpec(memory_space=pl.ANY)
    return pl.pallas_call(
        body, name="rs_wout",
        out_shape=jax.ShapeDtypeStruct((o_rows, width), F32),
        in_specs=[any_spec], out_specs=any_spec,
        scratch_shapes=[pltpu.VMEM((4, o_half, width), F32),
                        pltpu.VMEM((o_half, width), F32),
                        pltpu.VMEM((4, o_half, width), BF16),
                        pltpu.VMEM((4, o_half, width), BF16),
                        pltpu.VMEM((2, o_half, width), BF16),
                        pltpu.VMEM((o_half, width), BF16),
                        pltpu.SemaphoreType.DMA((8,)), pltpu.SemaphoreType.DMA((8,)), pltpu.SemaphoreType.DMA((1,))],
        compiler_params=_params(24),
    )(gwo)


def _adamw(name, w, g, m, v, rows):
    def body(w_ref, g_ref, m_ref, v_ref, go_ref, d_ref, nm_ref, nv_ref):
        gv = g_ref[...]
        go_ref[...] = gv
        nm = ADAM_B1 * m_ref[...] + (1.0 - ADAM_B1) * gv
        nv = ADAM_B2 * v_ref[...] + (1.0 - ADAM_B2) * (gv * gv)
        m_hat = nm / (1.0 - ADAM_B1 ** ADAM_STEP)
        v_hat = nv / (1.0 - ADAM_B2 ** ADAM_STEP)
        d_ref[...] = -ADAM_LR * (m_hat / (jnp.sqrt(v_hat) + ADAM_EPS) + ADAM_WD * w_ref[...])
        nm_ref[...] = nm
        nv_ref[...] = nv

    spec = pl.BlockSpec((rows, w.shape[1]), lambda i: (i, 0))
    shape = jax.ShapeDtypeStruct(w.shape, F32)
    return pl.pallas_call(
        body, name="adamw_" + name, grid=(w.shape[0] // rows,),
        out_shape=(shape,) * 4, in_specs=[spec] * 4, out_specs=(spec,) * 4,
        compiler_params=_params(32, ("arbitrary",)),
    )(w, g, m, v)


def kernel(x, norm_in, w_in, conv_w, attn_sinks, norm_conv_out, norm_attn_out, w_out, norm_final, loss_target, m_norm_in, m_w_in, m_conv_w, m_attn_sinks, m_norm_conv_out, m_norm_attn_out, m_w_out, m_norm_final, v_norm_in, v_w_in, v_conv_w, v_attn_sinks, v_norm_conv_out, v_norm_attn_out, v_w_out, v_norm_final):
    chip = 2 * lax.axis_index("x") + lax.axis_index("y")
    xs, target = x[0], loss_target[0]
    norm_final2 = norm_final.reshape(1, D_MODEL)

    wt, wo, cw4 = _ag_weights(w_in[0].T, w_out[0], conv_w[0])
    cw = jnp.transpose(cw4, (1, 0, 2)).reshape(3, D_CONV)

    h, pc, q, kv, ga = _fwd_in(xs, norm_in, wt)
    oc = _conv_fwd(pc, cw, norm_conv_out)
    ya, oa = _attn_fwd(q, kv, ga, attn_sinks, norm_attn_out)
    dx2, doc, doa, gwo, gnf, loss_lanes = _out_proj_loss(xs, oc, oa, wo, norm_final2, target)

    dpc, gnc, gcw = _conv_bwd(pc, doc, cw, norm_conv_out)
    dqg, gna, gsink = _attn_bwd(q, kv, ga, ya, doa, attn_sinks, norm_attn_out)
    gwo_sh = _rs_wout(gwo)

    last = jnp.zeros((1, D_MODEL), F32).at[:, 0:N_HEADS].set(gsink[:, 0:N_HEADS]).at[0, N_HEADS].set(jnp.sum(loss_lanes))
    small = jnp.concatenate([jnp.zeros((1, D_MODEL), F32), gnc, gna, gnf, gcw[0:3], last], axis=0)
    grad_x, small_sum, gwt_sh = _bwd_in(dpc, dqg, h, wt, xs, norm_in, dx2, small)

    loss = small_sum[7, N_HEADS]
    g_norm_in, g_norm_conv, g_norm_attn = small_sum[0:1], small_sum[1:2], small_sum[2:3]
    g_norm_final = small_sum[3]
    g_conv_w = lax.dynamic_slice(small_sum[4:7], (0, chip * (D_CONV // N_CHIPS)), (3, D_CONV // N_CHIPS))[None]
    g_sinks = small_sum[7:8, 0:N_HEADS]

    def adam(name, w, g, m, v, rows):
        shape = w.shape
        two_d = (-1, shape[-1])
        out = _adamw(name, w.reshape(two_d), g.reshape(two_d), m.reshape(two_d), v.reshape(two_d), rows)
        return tuple(o.reshape(shape) for o in out)

    weights = (norm_in, w_in, conv_w, attn_sinks, norm_conv_out, norm_attn_out, w_out, norm_final)
    grads = (g_norm_in, None, g_conv_w, g_sinks, g_norm_conv, g_norm_attn, gwo_sh[None], g_norm_final)
    moments_m = (m_norm_in, m_w_in, m_conv_w, m_attn_sinks, m_norm_conv_out, m_norm_attn_out, m_w_out, m_norm_final)
    moments_v = (v_norm_in, v_w_in, v_conv_w, v_attn_sinks, v_norm_conv_out, v_norm_attn_out, v_w_out, v_norm_final)
    rows = (1, None, 3, 1, 1, 1, 128, 1)
    names = ("norm_in", "w_in", "conv_w", "attn_sinks", "norm_conv_out", "norm_attn_out", "w_out", "norm_final")
    updates = []
    for args in zip(names, weights, grads, moments_m, moments_v, rows):
        if args[0] == "w_in":
            out_t = _adamw("w_in", w_in[0].T, gwt_sh, m_w_in[0].T, v_w_in[0].T, 200)
            updates.append(tuple(o.T[None] for o in out_t))
        else:
            updates.append(adam(*args))
    grads_out, deltas, new_m, new_v = zip(*updates)
    return (loss, grad_x[None], *grads_out, *deltas, *new_m, *new_v)
```

```python
import jax
import jax.numpy as jnp
from jax import lax
from jax.experimental import pallas as pl
from jax.experimental.pallas import tpu as pltpu

F32 = jnp.float32
BF16 = jnp.bfloat16
MESH = pl.DeviceIdType.MESH

D_MODEL = 1024
D_CONV = 1024
D_ATTN = 1024
D_KV = 128
D_QG = 2 * D_ATTN + 2 * D_KV
D_MIX = D_CONV + D_ATTN
D_PC = 4 * D_CONV
D_IN_PROJ = D_PC + 2 * D_ATTN + 2 * D_KV
ROW_Q = D_PC
ROW_KV = ROW_Q + D_ATTN
ROW_GA = ROW_KV + 2 * D_KV
N_HEADS = 16
HEAD_DIM = 64
HEADS_PER_KV = 8
BLK = 128
N_CHIPS = 4
RMS_EPS = 1e-5
SCALE = HEAD_DIM ** -0.5
SLOPES = tuple(2.0 ** (-8.0 * (h + 1) / N_HEADS) for h in range(N_HEADS))

ADAM_LR, ADAM_B1, ADAM_B2, ADAM_EPS, ADAM_WD, ADAM_STEP = 0.001, 0.9, 0.999, 1e-08, 0.01, 10

TOK_TILE = 256
MIB = 1 << 20


def _params(vmem_mib, semantics=None):
    return pltpu.CompilerParams(dimension_semantics=semantics, vmem_limit_bytes=vmem_mib * MIB)


def _nn(a, b):
    return jnp.dot(a, b, preferred_element_type=F32)


def _nt(a, b):
    return lax.dot_general(a, b, (((1,), (1,)), ((), ())), preferred_element_type=F32)


def _tn(a, b):
    return lax.dot_general(a, b, (((0,), (0,)), ((), ())), preferred_element_type=F32)


def _rstd(v):
    return lax.rsqrt(jnp.mean(v * v, axis=-1, keepdims=True) + RMS_EPS)


def _rms_bwd(g, xhat, rstd):
    return rstd * (g - xhat * jnp.mean(g * xhat, axis=-1, keepdims=True))


def _silu_and_grad(g):
    s = jax.nn.sigmoid(g)
    return g * s, s * (1.0 + g * (1.0 - s))


def _resident(shape):
    return pl.BlockSpec(shape, lambda *_: (0,) * len(shape), pipeline_mode=pl.Buffered(1))


def _xor(a, b):
    return a + b - 2 * a * b


def _cast_rows(src_ref, dst_ref, rows=32):
    def step(i, carry):
        sl = pl.ds(pl.multiple_of(i * rows, rows), rows)
        dst_ref[sl, :] = src_ref[sl, :].astype(dst_ref.dtype)
        return carry
    lax.fori_loop(0, src_ref.shape[0] // rows, step, 0)


def _ag_weights(wt_sh, wo_sh, cw_sh):
    wt_rows, wo_rows = wt_sh.shape[0], wo_sh.shape[0]
    wt_half, wo_half = wt_rows // 2, wo_rows // 2
    width = wt_sh.shape[1]

    def body(wt_ref, wo_ref, cw_ref, wt_out, wo_out, cw_out, f32_t, f32_o, own_t, own_o, land_t, land_o,
             send_sems, recv_sems, local_sems):
        x, y, c = lax.axis_index("x"), lax.axis_index("y"), lax.axis_index("c")
        j = 2 * x + y
        p1 = (_xor(x, c), _xor(y, 1 - c), c)
        p2 = (_xor(x, 1 - c), _xor(y, c), c)
        sib = (x, y, 1 - c)
        j1 = 2 * p1[0] + p1[1]
        j2 = 2 * p2[0] + p2[1]
        j3 = 3 - j

        def wt_rows_of(chip, half):
            return wt_out.at[pl.ds(pl.multiple_of(chip * wt_rows + half * wt_half, 16), wt_half), :]

        def wo_rows_of(chip, half):
            return wo_out.at[pl.ds(pl.multiple_of(chip * wo_rows + half * wo_half, 16), wo_half), :]

        def rcopy(k, src, dst, to):
            return pltpu.make_async_remote_copy(src_ref=src, dst_ref=dst, send_sem=send_sems.at[k],
                                                recv_sem=recv_sems.at[k], device_id=to, device_id_type=MESH)

        def lcopy(k, src, dst):
            cp = pltpu.make_async_copy(src, dst, local_sems.at[k])
            cp.start()
            return cp

        loads = [lcopy(0, wt_ref, f32_t), lcopy(1, wo_ref, f32_o)]
        local = [lcopy(2, cw_ref, cw_out.at[j])]
        for cp in loads:
            cp.wait()
        _cast_rows(f32_t, own_t)
        _cast_rows(f32_o, own_o)
        local += [lcopy(0, own_t, wt_out.at[pl.ds(pl.multiple_of(j * wt_rows, 16), wt_rows), :]),
                  lcopy(1, own_o, wo_out.at[pl.ds(pl.multiple_of(j * wo_rows, 16), wo_rows), :])]
        half_t = own_t.at[pl.ds(pl.multiple_of(c * wt_half, 16), wt_half), :]
        half_o = own_o.at[pl.ds(pl.multiple_of(c * wo_half, 16), wo_half), :]

        def publish(slot, chip, k_t, k_o, k_local):
            sends = [rcopy(k_t, land_t.at[slot], wt_rows_of(chip, c), sib), rcopy(k_o, land_o.at[slot], wo_rows_of(chip, c), sib)]
            for cp in sends:
                cp.start()
            local.extend([lcopy(k_local, land_t.at[slot], wt_rows_of(chip, c)),
                          lcopy(k_local + 1, land_o.at[slot], wo_rows_of(chip, c))])
            return sends

        hop1 = [rcopy(0, half_t, land_t.at[0], p1), rcopy(1, half_o, land_o.at[0], p1), rcopy(2, cw_ref, cw_out.at[j], p1)]
        for cp in hop1:
            cp.start()
        rcopy(0, half_t, land_t.at[0], p1).wait_recv()
        rcopy(1, half_o, land_o.at[0], p1).wait_recv()
        rcopy(2, cw_ref, cw_out.at[j1], p1).wait_recv()
        hop2 = [rcopy(3, half_t, land_t.at[1], p2), rcopy(5, half_o, land_o.at[1], p2), rcopy(7, cw_ref, cw_out.at[j], p2),
                rcopy(4, land_t.at[0], land_t.at[2], p2), rcopy(6, land_o.at[0], land_o.at[2], p2),
                rcopy(8, cw_out.at[j1], cw_out.at[j1], p2)]
        for cp in hop2:
            cp.start()
        swaps = publish(0, j1, 9, 12, 3)
        rcopy(3, half_t, land_t.at[1], p2).wait_recv()
        rcopy(5, half_o, land_o.at[1], p2).wait_recv()
        rcopy(7, cw_ref, cw_out.at[j2], p2).wait_recv()
        swaps += publish(1, j2, 10, 13, 5)
        rcopy(4, half_t, land_t.at[2], p2).wait_recv()
        rcopy(6, half_o, land_o.at[2], p2).wait_recv()
        rcopy(8, cw_ref, cw_out.at[j3], p2).wait_recv()
        swaps += publish(2, j3, 11, 14, 7)
        for k, chip in ((9, j2), (10, j1), (11, j3)):
            rcopy(k, half_t, wt_rows_of(chip, 1 - c), sib).wait_recv()
        for k, chip in ((12, j2), (13, j1), (14, j3)):
            rcopy(k, half_o, wo_rows_of(chip, 1 - c), sib).wait_recv()
        for cp in hop1 + hop2 + swaps:
            cp.wait_send()
        for cp in local:
            cp.wait()

    any_spec = pl.BlockSpec(memory_space=pl.ANY)
    return pl.pallas_call(
        body, name="ag_weights",
        out_shape=(jax.ShapeDtypeStruct((N_CHIPS * wt_rows, width), BF16),
                   jax.ShapeDtypeStruct((N_CHIPS * wo_rows, width), BF16),
                   jax.ShapeDtypeStruct((N_CHIPS,) + cw_sh.shape, cw_sh.dtype)),
        in_specs=[any_spec, any_spec, any_spec],
        out_specs=(any_spec, any_spec, any_spec),
        scratch_shapes=[pltpu.VMEM((wt_rows, width), F32), pltpu.VMEM((wo_rows, width), F32),
                        pltpu.VMEM((wt_rows, width), BF16), pltpu.VMEM((wo_rows, width), BF16),
                        pltpu.VMEM((3, wt_half, width), BF16), pltpu.VMEM((3, wo_half, width), BF16),
                        pltpu.SemaphoreType.DMA((15,)), pltpu.SemaphoreType.DMA((15,)), pltpu.SemaphoreType.DMA((9,))],
        compiler_params=_params(32),
    )(wt_sh, wo_sh, cw_sh)


def _fwd_in(x, norm_in, wt):
    seq = x.shape[0]
    tile = TOK_TILE

    def body(x_ref, g_ref, wt_ref, h_ref, pc_ref, q_ref, kv_ref, ga_ref):
        xv = x_ref[...]
        h = (xv * _rstd(xv) * g_ref[...]).astype(BF16)
        h_ref[...] = h
        for blk in range(D_PC // D_CONV):
            pc_ref[:, blk * D_CONV:(blk + 1) * D_CONV] = _nt(h, wt_ref[blk * D_CONV:(blk + 1) * D_CONV, :])
        q_ref[...] = _nt(h, wt_ref[ROW_Q:ROW_KV, :])
        kv_ref[...] = _nt(h, wt_ref[ROW_KV:ROW_GA, :])
        ga_ref[...] = _nt(h, wt_ref[ROW_GA:D_IN_PROJ, :])

    def row(width):
        return pl.BlockSpec((tile, width), lambda i: (i, 0))

    return pl.pallas_call(
        body, name="fwd_in", grid=(seq // tile,),
        out_shape=(jax.ShapeDtypeStruct((seq, D_MODEL), BF16), jax.ShapeDtypeStruct((seq, D_PC), F32),
                   jax.ShapeDtypeStruct((seq, D_ATTN), F32), jax.ShapeDtypeStruct((seq, 2 * D_KV), F32),
                   jax.ShapeDtypeStruct((seq, D_ATTN), F32)),
        in_specs=[row(D_MODEL), _resident((1, D_MODEL)), _resident(wt.shape)],
        out_specs=(row(D_MODEL), row(D_PC), row(D_ATTN), row(2 * D_KV), row(D_ATTN)),
        compiler_params=_params(48, ("arbitrary",)),
    )(x, norm_in, wt)


def _conv_core(pc_ref, zbuf, cw_ref):
    tile = pc_ref.shape[0]
    cb = pc_ref[:, 0:D_CONV]
    cc = pc_ref[:, D_CONV:2 * D_CONV]
    cu = pc_ref[:, 2 * D_CONV:3 * D_CONV]
    z = cc * cu
    zbuf[8:tile + 8, :] = z
    z1 = zbuf[7:tile + 7, :]
    z2 = zbuf[6:tile + 6, :]
    conv = cw_ref[0:1, :] * z2 + cw_ref[1:2, :] * z1 + cw_ref[2:3, :] * z
    return cb, cc, cu, z, z1, z2, conv


def _conv_fwd(pc, conv_w, norm_conv_out):
    seq = pc.shape[0]
    tile = TOK_TILE

    def body(pc_ref, cw_ref, gn_ref, oc_ref, zbuf):
        @pl.when(pl.program_id(0) == 0)
        def _():
            zbuf[0:8, :] = jnp.zeros((8, D_CONV), F32)

        cb, _, _, _, _, _, conv = _conv_core(pc_ref, zbuf, cw_ref)
        yc = cb * conv
        silu, _ = _silu_and_grad(pc_ref[:, 3 * D_CONV:4 * D_CONV])
        oc_ref[...] = (yc * _rstd(yc) * gn_ref[...] * silu).astype(BF16)
        zbuf[0:8, :] = zbuf[tile:tile + 8, :]

    return pl.pallas_call(
        body, name="conv_fwd", grid=(seq // tile,),
        out_shape=jax.ShapeDtypeStruct((seq, D_CONV), BF16),
        in_specs=[pl.BlockSpec((tile, D_PC), lambda i: (i, 0)), _resident(conv_w.shape), _resident((1, D_CONV))],
        out_specs=pl.BlockSpec((tile, D_CONV), lambda i: (i, 0)),
        scratch_shapes=[pltpu.VMEM((tile + 8, D_CONV), F32)],
        compiler_params=_params(40, ("arbitrary",)),
    )(pc, conv_w, norm_conv_out)


def _band_geometry(block_index):
    qi = lax.broadcasted_iota(jnp.int32, (BLK, BLK), 0)
    kp = lax.broadcasted_iota(jnp.int32, (BLK, BLK), 1)
    use_cur = kp <= qi
    dist = jnp.where(use_cur, qi - kp, qi - kp + BLK).astype(F32)
    valid = use_cur | (block_index > 0)
    return use_cur, dist, valid


def _block_diag(cur, prev, group):
    lane = lax.broadcasted_iota(jnp.int32, cur.shape, 1)

    def halves(t):
        other = pltpu.roll(t, 64, 1)
        lo, hi = (t, other) if group == 0 else (other, t)
        return jnp.where(lane < 64, lo, 0.0), jnp.where(lane >= 64, hi, 0.0)

    return jnp.concatenate(halves(cur) + halves(prev), axis=0).astype(BF16)


def _merge(s4, use_cur):
    return (jnp.where(use_cur, s4[:, 0:BLK], s4[:, 2 * BLK:3 * BLK]),
            jnp.where(use_cur, s4[:, BLK:2 * BLK], s4[:, 3 * BLK:4 * BLK]))


def _split(a, b, use_cur):
    return jnp.concatenate([jnp.where(use_cur, a, 0.0), jnp.where(use_cur, b, 0.0),
                            jnp.where(use_cur, 0.0, a), jnp.where(use_cur, 0.0, b)], axis=1)


def _softmax_head(s, head, sink, dist, valid):
    sc = jnp.where(valid, s - SLOPES[head] * dist, -jnp.inf)
    m = jnp.maximum(jnp.max(sc, axis=-1, keepdims=True), sink)
    p = jnp.exp(sc - m)
    es = jnp.exp(sink - m)
    inv = 1.0 / (jnp.sum(p, axis=-1, keepdims=True) + es)
    return p * inv, es * inv


def _attn_probs(q_ref, kvc_ref, kvp_ref, sink_ref, geometry):
    use_cur, dist, valid = geometry
    groups = range(N_HEADS // HEADS_PER_KV)
    kbd = [_block_diag(kvc_ref[:, 0:D_KV], kvp_ref[:, 0:D_KV], g) for g in groups]
    vbd = [_block_diag(kvc_ref[:, D_KV:2 * D_KV], kvp_ref[:, D_KV:2 * D_KV], g) for g in groups]
    qps = [(q_ref[:, j * 128:(j + 1) * 128] * SCALE).astype(BF16) for j in range(N_HEADS // 2)]
    scores = []
    for j, qp in enumerate(qps):
        scores += _merge(_nt(qp, kbd[j // 4]), use_cur)
    sinks = [sink_ref[0, h] for h in range(N_HEADS)]
    scores = [jnp.where(valid, s - SLOPES[h] * dist, -jnp.inf) for h, s in enumerate(scores)]
    maxes = [jnp.maximum(jnp.max(s, axis=-1, keepdims=True), sinks[h]) for h, s in enumerate(scores)]
    exps = [jnp.exp(s - m) for s, m in zip(scores, maxes)]
    sink_exps = [jnp.exp(sinks[h] - m) for h, m in enumerate(maxes)]
    invs = [1.0 / (jnp.sum(e, axis=-1, keepdims=True) + se) for e, se in zip(exps, sink_exps)]
    probs = [e * inv for e, inv in zip(exps, invs)]
    sink_probs = [se * inv for se, inv in zip(sink_exps, invs)]
    return qps, kbd, vbd, probs, sink_probs


def _kv_specs(n_blocks, reverse):
    def blk(i):
        return (n_blocks - 1 - i) if reverse else i
    cur = pl.BlockSpec((BLK, 2 * D_KV), lambda i: (blk(i), 0))
    prev = pl.BlockSpec((BLK, 2 * D_KV), lambda i: (jnp.maximum(blk(i) - 1, 0), 0))
    return cur, prev


def _attn_fwd(q, kv, ga, sinks, norm_attn_out):
    seq = q.shape[0]
    n_blocks = seq // BLK

    def body(q_ref, kvc_ref, kvp_ref, ga_ref, sink_ref, gn_ref, ya_ref, oa_ref):
        geometry = _band_geometry(pl.program_id(0))
        _, _, vbd, probs, _ = _attn_probs(q_ref, kvc_ref, kvp_ref, sink_ref, geometry)
        p4s = [_split(probs[2 * j], probs[2 * j + 1], geometry[0]).astype(BF16) for j in range(N_HEADS // 2)]
        ya = jnp.concatenate([_nn(p4, vbd[j // 4]) for j, p4 in enumerate(p4s)], axis=1)
        ya_ref[...] = ya
        silu, _ = _silu_and_grad(ga_ref[...])
        oa_ref[...] = (ya * _rstd(ya) * gn_ref[...] * silu).astype(BF16)

    row = pl.BlockSpec((BLK, D_ATTN), lambda i: (i, 0))
    kv_cur, kv_prev = _kv_specs(n_blocks, reverse=False)
    return pl.pallas_call(
        body, name="attn_fwd", grid=(n_blocks,),
        out_shape=(jax.ShapeDtypeStruct((seq, D_ATTN), F32), jax.ShapeDtypeStruct((seq, D_ATTN), BF16)),
        in_specs=[row, kv_cur, kv_prev, row, pl.BlockSpec(memory_space=pltpu.SMEM), _resident((1, D_ATTN))],
        out_specs=(row, row),
        compiler_params=_params(32, ("arbitrary",)),
    )(q, kv, kv, ga, sinks, norm_attn_out)


def _out_proj_loss(x, oc, oa, wo, norm_final, target):
    seq = x.shape[0]
    tile = TOK_TILE

    def body(x_ref, oc_ref, oa_ref, wo_ref, gf_ref, t_ref, dx2_ref, doc_ref, doa_ref, gwo_ref, gnf_ref, loss_ref):
        @pl.when(pl.program_id(0) == 0)
        def _():
            gwo_ref[...] = jnp.zeros_like(gwo_ref)
            gnf_ref[...] = jnp.zeros_like(gnf_ref)
            loss_ref[...] = jnp.zeros_like(loss_ref)

        oc, oa = oc_ref[...], oa_ref[...]
        x2 = x_ref[...] + _nn(oc, wo_ref[0:D_CONV, :]) + _nn(oa, wo_ref[D_CONV:D_MIX, :])
        r = _rstd(x2)
        xhat = x2 * r
        err = xhat * gf_ref[...] - t_ref[...]
        loss_ref[...] += jnp.sum(err * err, axis=0, keepdims=True) * (0.5 / D_MODEL)
        dy = err * (1.0 / D_MODEL)
        gnf_ref[...] += jnp.sum(dy * xhat, axis=0, keepdims=True)
        dx2 = _rms_bwd(dy * gf_ref[...], xhat, r)
        dx2_ref[...] = dx2
        db = dx2.astype(BF16)
        doc_ref[...] = _nt(db, wo_ref[0:D_CONV, :])
        doa_ref[...] = _nt(db, wo_ref[D_CONV:D_MIX, :])
        gwo_ref[0:D_CONV, :] += _tn(oc, db)
        gwo_ref[D_CONV:D_MIX, :] += _tn(oa, db)

    row = pl.BlockSpec((tile, D_MODEL), lambda i: (i, 0))
    vec = pl.BlockSpec((1, D_MODEL), lambda i: (0, 0))
    return pl.pallas_call(
        body, name="out_proj_loss", grid=(seq // tile,),
        out_shape=(jax.ShapeDtypeStruct((seq, D_MODEL), F32), jax.ShapeDtypeStruct((seq, D_CONV), F32),
                   jax.ShapeDtypeStruct((seq, D_ATTN), F32), jax.ShapeDtypeStruct((D_MIX, D_MODEL), F32),
                   jax.ShapeDtypeStruct((1, D_MODEL), F32), jax.ShapeDtypeStruct((1, D_MODEL), F32)),
        in_specs=[row, row, row, _resident(wo.shape), _resident((1, D_MODEL)), row],
        out_specs=(row, row, row, pl.BlockSpec((D_MIX, D_MODEL), lambda i: (0, 0)), vec, vec),
        compiler_params=_params(48, ("arbitrary",)),
    )(x, oc, oa, wo, norm_final, target)


def _conv_bwd(pc, doc, conv_w, norm_conv_out):
    seq = pc.shape[0]
    tile = TOK_TILE
    n_tiles = seq // tile

    def body(pc_ref, hcc_ref, hcu_ref, doc_ref, cw_ref, gn_ref, dpc_ref, gnc_ref, gcw_ref, zbuf, dbuf):
        step = pl.program_id(0)

        @pl.when(step == 0)
        def _():
            gnc_ref[...] = jnp.zeros_like(gnc_ref)
            gcw_ref[...] = jnp.zeros_like(gcw_ref)
            dbuf[tile:tile + 8, :] = jnp.zeros((8, D_CONV), F32)

        is_first_tile = step == n_tiles - 1
        zbuf[0:8, :] = jnp.where(is_first_tile, 0.0, hcc_ref[...] * hcu_ref[...])
        cb, cc, cu, z, z1, z2, conv = _conv_core(pc_ref, zbuf, cw_ref)
        gc = pc_ref[:, 3 * D_CONV:4 * D_CONV]
        silu, dsilu = _silu_and_grad(gc)
        yc = cb * conv
        r = _rstd(yc)
        xhat = yc * r
        do = doc_ref[...]
        dn = do * silu
        dpc_ref[:, 3 * D_CONV:4 * D_CONV] = (do * (xhat * gn_ref[...]) * dsilu).astype(BF16)
        gnc_ref[...] += jnp.sum(dn * xhat, axis=0, keepdims=True)
        dyc = _rms_bwd(dn * gn_ref[...], xhat, r)
        dpc_ref[:, 0:D_CONV] = (dyc * conv).astype(BF16)
        dconv = dyc * cb
        gcw_ref[0:1, :] += jnp.sum(dconv * z2, axis=0, keepdims=True)
        gcw_ref[1:2, :] += jnp.sum(dconv * z1, axis=0, keepdims=True)
        gcw_ref[2:3, :] += jnp.sum(dconv * z, axis=0, keepdims=True)
        dbuf[0:tile, :] = dconv
        dz = cw_ref[2:3, :] * dconv + cw_ref[1:2, :] * dbuf[1:tile + 1, :] + cw_ref[0:1, :] * dbuf[2:tile + 2, :]
        dpc_ref[:, D_CONV:2 * D_CONV] = (dz * cu).astype(BF16)
        dpc_ref[:, 2 * D_CONV:3 * D_CONV] = (dz * cc).astype(BF16)
        dbuf[tile:tile + 8, :] = dbuf[0:8, :]

    def rev(i):
        return n_tiles - 1 - i

    def halo(col_block):
        return pl.BlockSpec((8, D_CONV), lambda i: (jnp.maximum(rev(i) * (tile // 8) - 1, 0), col_block))

    return pl.pallas_call(
        body, name="conv_bwd", grid=(n_tiles,),
        out_shape=(jax.ShapeDtypeStruct((seq, D_PC), BF16), jax.ShapeDtypeStruct((1, D_CONV), F32),
                   jax.ShapeDtypeStruct((8, D_CONV), F32)),
        in_specs=[pl.BlockSpec((tile, D_PC), lambda i: (rev(i), 0)), halo(1), halo(2),
                  pl.BlockSpec((tile, D_CONV), lambda i: (rev(i), 0)), _resident(conv_w.shape), _resident((1, D_CONV))],
        out_specs=(pl.BlockSpec((tile, D_PC), lambda i: (rev(i), 0)), pl.BlockSpec((1, D_CONV), lambda i: (0, 0)),
                   pl.BlockSpec((8, D_CONV), lambda i: (0, 0))),
        scratch_shapes=[pltpu.VMEM((tile + 8, D_CONV), F32), pltpu.VMEM((tile + 8, D_CONV), F32)],
        compiler_params=_params(48, ("arbitrary",)),
    )(pc, pc, pc, doc, conv_w, norm_conv_out)


def _attn_bwd(q, kv, ga, ya, doa, sinks, norm_attn_out):
    seq = q.shape[0]
    n_blocks = seq // BLK

    def body(q_ref, kvc_ref, kvp_ref, ga_ref, ya_ref, doa_ref, sink_ref, gn_ref,
             dqg_ref, gna_ref, gs_ref, carry, dya_buf):
        step = pl.program_id(0)

        @pl.when(step == 0)
        def _():
            gna_ref[...] = jnp.zeros_like(gna_ref)
            gs_ref[...] = jnp.zeros_like(gs_ref)
            carry[...] = jnp.zeros_like(carry)

        ya = ya_ref[...]
        r = _rstd(ya)
        xhat = ya * r
        silu, dsilu = _silu_and_grad(ga_ref[...])
        do = doa_ref[...]
        dn = do * silu
        dqg_ref[:, D_ATTN:2 * D_ATTN] = (do * (xhat * gn_ref[...]) * dsilu).astype(BF16)
        gna_ref[...] += jnp.sum(dn * xhat, axis=0, keepdims=True)
        dya_buf[...] = _rms_bwd(dn * gn_ref[...], xhat, r).astype(BF16)

        geometry = _band_geometry(n_blocks - 1 - step)
        use_cur = geometry[0]
        lane = lax.broadcasted_iota(jnp.int32, (BLK, 128), 1)
        sink_lane = lax.broadcasted_iota(jnp.int32, (1, 128), 1)
        gsink = jnp.zeros((1, 128), F32)

        def fold(bd):
            return (jnp.where(lane < 64, bd[0:BLK], 0.0) + jnp.where(lane >= 64, bd[BLK:2 * BLK], 0.0),
                    jnp.where(lane < 64, bd[2 * BLK:3 * BLK], 0.0) + jnp.where(lane >= 64, bd[3 * BLK:4 * BLK], 0.0))

        pairs = range(N_HEADS // 2)
        qps, kbd, vbd, probs, sink_probs = _attn_probs(q_ref, kvc_ref, kvp_ref, sink_ref, geometry)
        dyps = [dya_buf[:, j * 128:(j + 1) * 128] for j in pairs]
        dps = []
        for j in pairs:
            dps += _merge(_nt(dyps[j], vbd[j // 4]), use_cur)
        deltas = [jnp.sum(p * dp, axis=-1, keepdims=True) for p, dp in zip(probs, dps)]
        dss = [p * (dp - delta) for p, dp, delta in zip(probs, dps, deltas)]
        for h in range(N_HEADS):
            dsink = -jnp.sum(sink_probs[h] * deltas[h], axis=0, keepdims=True)
            gsink = gsink + jnp.where(sink_lane == h, dsink, 0.0)
        gs_ref[...] += gsink
        ds4s = [_split(dss[2 * j], dss[2 * j + 1], use_cur).astype(BF16) for j in pairs]
        p4s = [_split(probs[2 * j], probs[2 * j + 1], use_cur).astype(BF16) for j in pairs]
        dqg_ref[:, 0:D_ATTN] = jnp.concatenate([_nn(ds4s[j], kbd[j // 4]) * SCALE for j in pairs], axis=1).astype(BF16)
        sums = []
        for group in range(N_HEADS // HEADS_PER_KV):
            acc = [jnp.zeros((BLK, 128), F32) for _ in range(4)]
            for j in range(group * 4, group * 4 + 4):
                for slot, part in enumerate(fold(_tn(ds4s[j], qps[j])) + fold(_tn(p4s[j], dyps[j]))):
                    acc[slot] = acc[slot] + part
            sums.append([a + pltpu.roll(a, 64, 1) for a in acc])
        dk_cur, dk_prev, dv_cur, dv_prev = (jnp.where(lane < 64, a, b) for a, b in zip(sums[0], sums[1]))
        dqg_ref[:, 2 * D_ATTN:2 * D_ATTN + 2 * D_KV] = (jnp.concatenate([dk_cur, dv_cur], axis=1) + carry[...]).astype(BF16)
        carry[...] = jnp.concatenate([dk_prev, dv_prev], axis=1)

    row = pl.BlockSpec((BLK, D_ATTN), lambda i: (n_blocks - 1 - i, 0))
    kv_cur, kv_prev = _kv_specs(n_blocks, reverse=True)
    return pl.pallas_call(
        body, name="attn_bwd", grid=(n_blocks,),
        out_shape=(jax.ShapeDtypeStruct((seq, D_QG), BF16), jax.ShapeDtypeStruct((1, D_ATTN), F32),
                   jax.ShapeDtypeStruct((1, 128), F32)),
        in_specs=[row, kv_cur, kv_prev, row, row, row, pl.BlockSpec(memory_space=pltpu.SMEM), _resident((1, D_ATTN))],
        out_specs=(pl.BlockSpec((BLK, D_QG), lambda i: (n_blocks - 1 - i, 0)),
                   pl.BlockSpec((1, D_ATTN), lambda i: (0, 0)), pl.BlockSpec((1, 128), lambda i: (0, 0))),
        scratch_shapes=[pltpu.VMEM((BLK, 2 * D_KV), F32), pltpu.VMEM((BLK, D_ATTN), BF16)],
        compiler_params=_params(32, ("arbitrary",)),
    )(q, kv, kv, ga, ya, doa, sinks, norm_attn_out)


GBLK = 256
GSUB = 64
PAIR_RING = 4
LAG_PAIR, LAG_HOP1, LAG_HOP2 = 1, 5, 10


def _bwd_in(dpc, dqg, h, wt, x, norm_in, dx2, small):
    seq = x.shape[0]
    n_blk = D_IN_PROJ // GBLK
    per_chip = n_blk // N_CHIPS
    n_slots = (n_blk + 1) // 2
    n_sub = GBLK // GSUB
    chip_rows = D_IN_PROJ // N_CHIPS
    tile = TOK_TILE
    n_tiles = seq // tile
    n_steps = n_blk + max(n_tiles, LAG_HOP2)
    chunk = min(seq, 512)
    blk_q, blk_kv, blk_ga = ROW_Q // GBLK, ROW_KV // GBLK, ROW_GA // GBLK

    def block_of(i):
        k = i % N_CHIPS
        robin = per_chip * ((k % 2) * 2 + k // 2) + i // N_CHIPS
        if isinstance(i, int):
            return robin if i < per_chip * N_CHIPS else i
        return jnp.where(i < per_chip * N_CHIPS, robin, i)

    def owner_of(i):
        return (i // N_CHIPS) % 2

    def slot_of(i):
        return (i // (2 * N_CHIPS)) * N_CHIPS + i % N_CHIPS

    def body(dpc_ref, dqg_ref, wt_ref, h_ref, x_ref, g_ref, dx2_ref, small_ref, gx_ref, small_sum, gwt_sh,
             dh_acc, gni, keep, pbuf, xbuf, land, land2, small_land,
             pair_send, pair_recv, h1_send, h1_recv, h2_send, h2_recv, sw_send, sw_recv, sm_send, sm_recv, out_sem):
        step = pl.program_id(0)
        x_i, y_i, c = lax.axis_index("x"), lax.axis_index("y"), lax.axis_index("c")
        me = 4 * x_i + 2 * y_i + c
        j = 2 * x_i + y_i
        pa = (_xor(x_i, 1 - c), _xor(y_i, c), c)
        pb = (_xor(x_i, c), _xor(y_i, 1 - c), c)
        sib = (x_i, y_i, 1 - c)
        ja = 2 * pa[0] + pa[1]
        jb = 2 * pb[0] + pb[1]
        jd = 3 - j

        def remote(src, dst, send, recv, to):
            return pltpu.make_async_remote_copy(src_ref=src, dst_ref=dst, send_sem=send, recv_sem=recv,
                                                device_id=to, device_id_type=MESH)

        def piece(ref, slot, u, n):
            return ref.at[slot, pl.ds(u * GSUB, n * GSUB), :]

        def chip_rows_at(ref, local, n):
            return ref.at[pl.ds(pl.multiple_of(local, GSUB), n * GSUB), :]

        def pair_copy(i):
            slot = slot_of(i)
            return remote(pbuf.at[i % PAIR_RING], land.at[slot], pair_send.at[slot], pair_recv.at[slot], sib)

        def h1_copy(slot, u, n):
            k = slot * n_sub + u
            return remote(piece(xbuf, slot, u, n), piece(xbuf, slot, u, n), h1_send.at[k], h1_recv.at[k], pa)

        def h2_copy(slot, u, n, local):
            k = slot * n_sub + u
            return remote(piece(xbuf, slot, u, n), chip_rows_at(land2, local, n), h2_send.at[k], h2_recv.at[k], pb)

        def sw_copy(slot, u, n, local):
            k = slot * n_sub + u
            return remote(piece(keep, slot, u, n), chip_rows_at(gwt_sh, local, n), sw_send.at[k], sw_recv.at[k], sib)

        def owned(i):
            return (i >= 0) & (i < n_blk) & (owner_of(i) == c)

        def chip_of(blk, u):
            row = blk * GBLK + u * GSUB
            chip = row // chip_rows
            return chip, row - chip * chip_rows

        def pieces(blk):
            first, local = chip_of(blk, 0)
            whole = first == chip_of(blk, n_sub - 1)[0]
            if isinstance(blk, int):
                return [(True, 0, n_sub, first, local)] if whole else [(True, u, 1) + chip_of(blk, u) for u in range(n_sub)]
            return [(whole, 0, n_sub, first, local)] + [(jnp.logical_not(whole), u, 1) + chip_of(blk, u) for u in range(n_sub)]

        @pl.when(step == 0)
        def _():
            dh_acc[...] = jnp.zeros_like(dh_acc)
            gni[...] = jnp.zeros_like(gni)

        @pl.when(step < n_blk)
        def _():
            from_pc = block_of(step) < blk_q
            block = _tn(jnp.where(from_pc, dpc_ref[...], dqg_ref[...]), h_ref[...])
            for t in range(0, seq, chunk):
                d = jnp.where(from_pc, dpc_ref[t:t + chunk, :], dqg_ref[t:t + chunk, :])
                dh_acc[t:t + chunk, :] += _nn(d, wt_ref[...])

            @pl.when(owner_of(step) == c)
            def _():
                keep[slot_of(step)] = block

            @pl.when(owner_of(step) != c)
            def _():
                @pl.when(step >= 2 * PAIR_RING)
                def _():
                    pair_copy(step - 2 * PAIR_RING).wait_send()
                pbuf[step % PAIR_RING] = block.astype(BF16)
                pair_copy(step).start()

        i1 = step - LAG_PAIR

        @pl.when(owned(i1))
        def _():
            slot = slot_of(i1)
            pair_copy(i1).wait_recv()
            _accumulate(keep.at[slot], land.at[slot])
            for cond, u, n, chip, _ in pieces(block_of(i1)):
                @pl.when(cond & ((chip == ja) | (chip == jd)))
                def _(u=u, n=n):
                    _cast_rows(piece(keep, slot, u, n), piece(xbuf, slot, u, n))
                    h1_copy(slot, u, n).start()

        i2 = step - LAG_HOP1

        @pl.when(owned(i2))
        def _():
            slot = slot_of(i2)
            for cond, u, n, chip, local in pieces(block_of(i2)):
                @pl.when(cond & ((chip == j) | (chip == jb)))
                def _(u=u, n=n, chip=chip, local=local):
                    h1_copy(slot, u, n).wait_recv()
                    _accumulate(piece(keep, slot, u, n), piece(xbuf, slot, u, n))

                    @pl.when(chip == jb)
                    def _():
                        _cast_rows(piece(keep, slot, u, n), piece(xbuf, slot, u, n))
                        h2_copy(slot, u, n, local).start()

                @pl.when(cond & ((chip == ja) | (chip == jd)))
                def _(u=u, n=n):
                    h1_copy(slot, u, n).wait_send()

        i3 = step - LAG_HOP2

        @pl.when(owned(i3))
        def _():
            slot = slot_of(i3)
            for cond, u, n, chip, local in pieces(block_of(i3)):
                @pl.when(cond & (chip == j))
                def _(u=u, n=n, local=local):
                    h2_copy(slot, u, n, local).wait_recv()
                    _accumulate(piece(keep, slot, u, n), chip_rows_at(land2, local, n))
                    mine = pltpu.make_async_copy(piece(keep, slot, u, n), chip_rows_at(gwt_sh, local, n), out_sem.at[0])
                    mine.start()
                    sw_copy(slot, u, n, local).start()
                    mine.wait()

                @pl.when(cond & (chip == jb))
                def _(u=u, n=n, local=local):
                    h2_copy(slot, u, n, local).wait_send()

        e = step - n_blk

        @pl.when((e >= 0) & (e < n_tiles))
        def _():
            dh = dh_acc[pl.ds(pl.multiple_of(e * tile, tile), tile), :]
            xv = x_ref[...]
            r = _rstd(xv)
            xhat = xv * r
            gni[...] += jnp.sum(dh * xhat, axis=0, keepdims=True)
            gx_ref[...] = _rms_bwd(dh * g_ref[...], xhat, r) + dx2_ref[...]

        @pl.when(step == n_steps - 1)
        def _():
            small_land[me] = small_ref[...]
            small_land[me, 0:1, :] = gni[...]
            others = [(dx, dy, dc) for dx in (0, 1) for dy in (0, 1) for dc in (0, 1)][1:]
            sends = [remote(small_land.at[me], small_land.at[me], sm_send.at[k], sm_recv.at[k],
                            (_xor(x_i, dx), _xor(y_i, dy), _xor(c, dc))) for k, (dx, dy, dc) in enumerate(others)]
            for cp in sends:
                cp.start()
            for i in range(n_blk):
                if i + 2 * PAIR_RING >= n_blk:
                    @pl.when(owner_of(i) != c)
                    def _(i=i):
                        pair_copy(i).wait_send()
                for _, u, n, chip, local in pieces(block_of(i)):
                    @pl.when((j == chip) & (c == owner_of(i)))
                    def _(i=i, u=u, n=n, local=local):
                        sw_copy(slot_of(i), u, n, local).wait_send()

                    @pl.when((j == chip) & (c != owner_of(i)))
                    def _(i=i, u=u, n=n, local=local):
                        sw_copy(slot_of(i), u, n, local).wait_recv()
            for cp in sends:
                cp.wait_recv()
            total = small_land[0]
            for dev in range(1, 8):
                total = total + small_land[dev]
            small_sum[...] = total
            for cp in sends:
                cp.wait_send()

    def blk_at(i):
        return block_of(jnp.clip(i, 0, n_blk - 1))

    def dqg_block(i):
        b = blk_at(i)
        q_blk = jnp.clip(b - blk_q, 0, blk_kv - blk_q - 1)
        ga_blk = (D_ATTN // GBLK) + jnp.clip(b - blk_ga, 0, n_blk - blk_ga - 1)
        return jnp.where(b < blk_kv, q_blk, jnp.where(b == blk_kv, 2 * D_ATTN // GBLK, ga_blk))

    def tok(i):
        return (jnp.clip(i - n_blk, 0, n_tiles - 1), 0)

    n_piece = n_slots * n_sub
    dma = pltpu.SemaphoreType.DMA
    return pl.pallas_call(
        body, name="bwd_in", grid=(n_steps,),
        out_shape=(jax.ShapeDtypeStruct((seq, D_MODEL), F32), jax.ShapeDtypeStruct(small.shape, F32),
                   jax.ShapeDtypeStruct((chip_rows, D_MODEL), F32)),
        in_specs=[pl.BlockSpec((seq, GBLK), lambda i: (0, jnp.clip(blk_at(i), 0, blk_q - 1))),
                  pl.BlockSpec((seq, GBLK), lambda i: (0, dqg_block(i))),
                  pl.BlockSpec((GBLK, D_MODEL), lambda i: (blk_at(i), 0)),
                  _resident(h.shape),
                  pl.BlockSpec((tile, D_MODEL), tok), _resident((1, D_MODEL)), pl.BlockSpec((tile, D_MODEL), tok),
                  _resident(small.shape)],
        out_specs=(pl.BlockSpec((tile, D_MODEL), tok), pl.BlockSpec(small.shape, lambda i: (0, 0)),
                   pl.BlockSpec(memory_space=pl.ANY)),
        scratch_shapes=[pltpu.VMEM((seq, D_MODEL), F32), pltpu.VMEM((1, D_MODEL), F32),
                        pltpu.VMEM((n_slots, GBLK, D_MODEL), F32), pltpu.VMEM((PAIR_RING, GBLK, D_MODEL), BF16),
                        pltpu.VMEM((n_slots, GBLK, D_MODEL), BF16), pltpu.VMEM((n_slots, GBLK, D_MODEL), BF16),
                        pltpu.VMEM((chip_rows, D_MODEL), BF16), pltpu.VMEM((8,) + small.shape, F32),
                        dma((n_slots,)), dma((n_slots,)), dma((n_piece,)), dma((n_piece,)), dma((n_piece,)),
                        dma((n_piece,)), dma((n_piece,)), dma((n_piece,)), dma((7,)), dma((7,)), dma((1,))],
        compiler_params=_params(62, ("arbitrary",)),
    )(dpc, dqg, wt, h, x, norm_in, dx2, small)


def _accumulate(dst_ref, src_ref, rows=16):
    def step(i, carry):
        sl = pl.ds(pl.multiple_of(i * rows, rows), rows)
        dst_ref[sl, :] = dst_ref[sl, :] + src_ref[sl, :].astype(F32)
        return carry
    lax.fori_loop(0, dst_ref.shape[0] // rows, step, 0)


def _rs_wout(gwo):
    o_rows = gwo.shape[0] // N_CHIPS
    o_half = o_rows // 2
    width = gwo.shape[1]

    def body(gwo_ref, gwo_sh, acc_o, stage_o, sb_o, r1o, r2o, r3o, send_sems, recv_sems, local_sems):
        x, y, c = lax.axis_index("x"), lax.axis_index("y"), lax.axis_index("c")
        j = 2 * x + y
        pa = (_xor(x, 1 - c), _xor(y, c), c)
        pb = (_xor(x, c), _xor(y, 1 - c), c)
        sib = (x, y, 1 - c)
        ja = 2 * pa[0] + pa[1]
        jb = 2 * pb[0] + pb[1]
        jd = 3 - j

        def rcopy(k, src, dst, to):
            return pltpu.make_async_remote_copy(src_ref=src, dst_ref=dst, send_sem=send_sems.at[k],
                                                recv_sem=recv_sems.at[k], device_id=to, device_id_type=MESH)

        def o_rows_of(chip, half):
            return gwo_ref.at[pl.ds(pl.multiple_of(chip * o_rows + half * o_half, 8), o_half), :]

        def load(src, dst):
            cp = pltpu.make_async_copy(src, dst, local_sems.at[0])
            cp.start()
            return cp

        order = (ja, jd, jb, j)
        sib_order = (jb, jd, ja, j)
        pair_sends = []
        for s, chip in enumerate(sib_order):
            load(o_rows_of(chip, 1 - c), stage_o).wait()
            _cast_rows(stage_o, sb_o.at[s])
            pair_sends.append(rcopy(s, sb_o.at[s], r1o.at[s], sib))
            pair_sends[-1].start()

        def resend(s, k, landing, to):
            pair_sends[s].wait_send()
            _cast_rows(acc_o.at[s], sb_o.at[s])
            cp = rcopy(k, sb_o.at[s], landing, to)
            cp.start()
            return cp

        hop1 = []
        for s, chip in enumerate(order):
            load(o_rows_of(chip, c), acc_o.at[s]).wait()
            rcopy(s, sb_o.at[s], r1o.at[s], sib).wait_recv()
            _accumulate(acc_o.at[s], r1o.at[s])
            if s < 2:
                hop1.append(resend(s, 4 + s, r2o.at[s], pa))

        rcopy(5, sb_o.at[1], r2o.at[1], pa).wait_recv()
        _accumulate(acc_o.at[2], r2o.at[1])
        hop2 = resend(2, 6, r3o, pb)
        rcopy(4, sb_o.at[0], r2o.at[0], pa).wait_recv()
        _accumulate(acc_o.at[3], r2o.at[0])
        rcopy(6, sb_o.at[2], r3o, pb).wait_recv()
        _accumulate(acc_o.at[3], r3o)

        out_o = gwo_sh.at[pl.ds(pl.multiple_of(c * o_half, 8), o_half), :]
        keep = load(acc_o.at[3], out_o)
        swap = rcopy(7, acc_o.at[3], out_o, sib)
        swap.start()
        rcopy(7, acc_o.at[3], gwo_sh.at[pl.ds(pl.multiple_of((1 - c) * o_half, 8), o_half), :], sib).wait_recv()
        for cp in [pair_sends[3]] + hop1 + [hop2, swap]:
            cp.wait_send()
        keep.wait()

    any_spec = pl.BlockSpec(memory_space=pl.ANY)
    return pl.pallas_call(
        body, name="rs_wout",
        out_shape=jax.ShapeDtypeStruct((o_rows, width), F32),
        in_specs=[any_spec], out_specs=any_spec,
        scratch_shapes=[pltpu.VMEM((4, o_half, width), F32),
                        pltpu.VMEM((o_half, width), F32),
                        pltpu.VMEM((4, o_half, width), BF16),
                        pltpu.VMEM((4, o_half, width), BF16),
                        pltpu.VMEM((2, o_half, width), BF16),
                        pltpu.VMEM((o_half, width), BF16),
                        pltpu.SemaphoreType.DMA((8,)), pltpu.SemaphoreType.DMA((8,)), pltpu.SemaphoreType.DMA((1,))],
        compiler_params=_params(24),
    )(gwo)


def _adamw(name, w, g, m, v, rows):
    def body(w_ref, g_ref, m_ref, v_ref, go_ref, d_ref, nm_ref, nv_ref):
        gv = g_ref[...]
        go_ref[...] = gv
        nm = ADAM_B1 * m_ref[...] + (1.0 - ADAM_B1) * gv
        nv = ADAM_B2 * v_ref[...] + (1.0 - ADAM_B2) * (gv * gv)
        m_hat = nm / (1.0 - ADAM_B1 ** ADAM_STEP)
        v_hat = nv / (1.0 - ADAM_B2 ** ADAM_STEP)
        d_ref[...] = -ADAM_LR * (m_hat / (jnp.sqrt(v_hat) + ADAM_EPS) + ADAM_WD * w_ref[...])
        nm_ref[...] = nm
        nv_ref[...] = nv

    spec = pl.BlockSpec((rows, w.shape[1]), lambda i: (i, 0))
    shape = jax.ShapeDtypeStruct(w.shape, F32)
    return pl.pallas_call(
        body, name="adamw_" + name, grid=(w.shape[0] // rows,),
        out_shape=(shape,) * 4, in_specs=[spec] * 4, out_specs=(spec,) * 4,
        compiler_params=_params(32, ("arbitrary",)),
    )(w, g, m, v)


def kernel(x, norm_in, w_in, conv_w, attn_sinks, norm_conv_out, norm_attn_out, w_out, norm_final, loss_target, m_norm_in, m_w_in, m_conv_w, m_attn_sinks, m_norm_conv_out, m_norm_attn_out, m_w_out, m_norm_final, v_norm_in, v_w_in, v_conv_w, v_attn_sinks, v_norm_conv_out, v_norm_attn_out, v_w_out, v_norm_final):
    chip = 2 * lax.axis_index("x") + lax.axis_index("y")
    xs, target = x[0], loss_target[0]
    norm_final2 = norm_final.reshape(1, D_MODEL)

    wt, wo, cw4 = _ag_weights(w_in[0].T, w_out[0], conv_w[0])
    cw = jnp.transpose(cw4, (1, 0, 2)).reshape(3, D_CONV)

    h, pc, q, kv, ga = _fwd_in(xs, norm_in, wt)
    oc = _conv_fwd(pc, cw, norm_conv_out)
    ya, oa = _attn_fwd(q, kv, ga, attn_sinks, norm_attn_out)
    dx2, doc, doa, gwo, gnf, loss_lanes = _out_proj_loss(xs, oc, oa, wo, norm_final2, target)

    dpc, gnc, gcw = _conv_bwd(pc, doc, cw, norm_conv_out)
    dqg, gna, gsink = _attn_bwd(q, kv, ga, ya, doa, attn_sinks, norm_attn_out)
    gwo_sh = _rs_wout(gwo)

    last = jnp.zeros((1, D_MODEL), F32).at[:, 0:N_HEADS].set(gsink[:, 0:N_HEADS]).at[0, N_HEADS].set(jnp.sum(loss_lanes))
    small = jnp.concatenate([jnp.zeros((1, D_MODEL), F32), gnc, gna, gnf, gcw[0:3], last], axis=0)
    grad_x, small_sum, gwt_sh = _bwd_in(dpc, dqg, h, wt, xs, norm_in, dx2, small)

    loss = small_sum[7, N_HEADS]
    g_norm_in, g_norm_conv, g_norm_attn = small_sum[0:1], small_sum[1:2], small_sum[2:3]
    g_norm_final = small_sum[3]
    g_conv_w = lax.dynamic_slice(small_sum[4:7], (0, chip * (D_CONV // N_CHIPS)), (3, D_CONV // N_CHIPS))[None]
    g_sinks = small_sum[7:8, 0:N_HEADS]

    def adam(name, w, g, m, v, rows):
        shape = w.shape
        two_d = (-1, shape[-1])
        out = _adamw(name, w.reshape(two_d), g.reshape(two_d), m.reshape(two_d), v.reshape(two_d), rows)
        return tuple(o.reshape(shape) for o in out)

    weights = (norm_in, w_in, conv_w, attn_sinks, norm_conv_out, norm_attn_out, w_out, norm_final)
    grads = (g_norm_in, None, g_conv_w, g_sinks, g_norm_conv, g_norm_attn, gwo_sh[None], g_norm_final)
    moments_m = (m_norm_in, m_w_in, m_conv_w, m_attn_sinks, m_norm_conv_out, m_norm_attn_out, m_w_out, m_norm_final)
    moments_v = (v_norm_in, v_w_in, v_conv_w, v_attn_sinks, v_norm_conv_out, v_norm_attn_out, v_w_out, v_norm_final)
    rows = (1, None, 3, 1, 1, 1, 128, 1)
    names = ("norm_in", "w_in", "conv_w", "attn_sinks", "norm_conv_out", "norm_attn_out", "w_out", "norm_final")
    updates = []
    for args in zip(names, weights, grads, moments_m, moments_v, rows):
        if args[0] == "w_in":
            out_t = _adamw("w_in", w_in[0].T, gwt_sh, m_w_in[0].T, v_w_in[0].T, 200)
            updates.append(tuple(o.T[None] for o in out_t))
        else:
            updates.append(adam(*args))
    grads_out, deltas, new_m, new_v = zip(*updates)
    return (loss, grad_x[None], *grads_out, *deltas, *new_m, *new_v)
```

```python
import jax
import jax.numpy as jnp
from jax import lax
from jax.experimental import pallas as pl
from jax.experimental.pallas import tpu as pltpu

F32 = jnp.float32
BF16 = jnp.bfloat16
MESH = pl.DeviceIdType.MESH

D_MODEL = 1024
D_CONV = 1024
D_ATTN = 1024
D_KV = 128
D_QG = 2 * D_ATTN + 2 * D_KV
D_MIX = D_CONV + D_ATTN
D_PC = 4 * D_CONV
D_IN_PROJ = D_PC + 2 * D_ATTN + 2 * D_KV
ROW_Q = D_PC
ROW_KV = ROW_Q + D_ATTN
ROW_GA = ROW_KV + 2 * D_KV
N_HEADS = 16
HEAD_DIM = 64
HEADS_PER_KV = 8
BLK = 128
N_CHIPS = 4
RMS_EPS = 1e-5
SCALE = HEAD_DIM ** -0.5
SLOPES = tuple(2.0 ** (-8.0 * (h + 1) / N_HEADS) for h in range(N_HEADS))

ADAM_LR, ADAM_B1, ADAM_B2, ADAM_EPS, ADAM_WD, ADAM_STEP = 0.001, 0.9, 0.999, 1e-08, 0.01, 10

TOK_TILE = 256
MIB = 1 << 20


def _params(vmem_mib, semantics=None):
    return pltpu.CompilerParams(dimension_semantics=semantics, vmem_limit_bytes=vmem_mib * MIB)


def _nn(a, b):
    return jnp.dot(a, b, preferred_element_type=F32)


def _nt(a, b):
    return lax.dot_general(a, b, (((1,), (1,)), ((), ())), preferred_element_type=F32)


def _tn(a, b):
    return lax.dot_general(a, b, (((0,), (0,)), ((), ())), preferred_element_type=F32)


def _rstd(v):
    return lax.rsqrt(jnp.mean(v * v, axis=-1, keepdims=True) + RMS_EPS)


def _rms_bwd(g, xhat, rstd):
    return rstd * (g - xhat * jnp.mean(g * xhat, axis=-1, keepdims=True))


def _silu_and_grad(g):
    s = jax.nn.sigmoid(g)
    return g * s, s * (1.0 + g * (1.0 - s))


def _resident(shape):
    return pl.BlockSpec(shape, lambda *_: (0,) * len(shape), pipeline_mode=pl.Buffered(1))


def _xor(a, b):
    return a + b - 2 * a * b


def _cast_rows(src_ref, dst_ref, rows=32):
    def step(i, carry):
        sl = pl.ds(pl.multiple_of(i * rows, rows), rows)
        dst_ref[sl, :] = src_ref[sl, :].astype(dst_ref.dtype)
        return carry
    lax.fori_loop(0, src_ref.shape[0] // rows, step, 0)


def _ag_weights(wt_sh, wo_sh, cw_sh):
    wt_rows, wo_rows = wt_sh.shape[0], wo_sh.shape[0]
    wt_half, wo_half = wt_rows // 2, wo_rows // 2
    width = wt_sh.shape[1]

    def body(wt_ref, wo_ref, cw_ref, wt_out, wo_out, cw_out, f32_t, f32_o, own_t, own_o, land_t, land_o,
             send_sems, recv_sems, local_sems):
        x, y, c = lax.axis_index("x"), lax.axis_index("y"), lax.axis_index("c")
        j = 2 * x + y
        p1 = (_xor(x, c), _xor(y, 1 - c), c)
        p2 = (_xor(x, 1 - c), _xor(y, c), c)
        sib = (x, y, 1 - c)
        j1 = 2 * p1[0] + p1[1]
        j2 = 2 * p2[0] + p2[1]
        j3 = 3 - j

        def wt_rows_of(chip, half):
            return wt_out.at[pl.ds(pl.multiple_of(chip * wt_rows + half * wt_half, 16), wt_half), :]

        def wo_rows_of(chip, half):
            return wo_out.at[pl.ds(pl.multiple_of(chip * wo_rows + half * wo_half, 16), wo_half), :]

        def rcopy(k, src, dst, to):
            return pltpu.make_async_remote_copy(src_ref=src, dst_ref=dst, send_sem=send_sems.at[k],
                                                recv_sem=recv_sems.at[k], device_id=to, device_id_type=MESH)

        def lcopy(k, src, dst):
            cp = pltpu.make_async_copy(src, dst, local_sems.at[k])
            cp.start()
            return cp

        loads = [lcopy(0, wt_ref, f32_t), lcopy(1, wo_ref, f32_o)]
        local = [lcopy(2, cw_ref, cw_out.at[j])]
        for cp in loads:
            cp.wait()
        _cast_rows(f32_t, own_t)
        _cast_rows(f32_o, own_o)
        local += [lcopy(0, own_t, wt_out.at[pl.ds(pl.multiple_of(j * wt_rows, 16), wt_rows), :]),
                  lcopy(1, own_o, wo_out.at[pl.ds(pl.multiple_of(j * wo_rows, 16), wo_rows), :])]
        half_t = own_t.at[pl.ds(pl.multiple_of(c * wt_half, 16), wt_half), :]
        half_o = own_o.at[pl.ds(pl.multiple_of(c * wo_half, 16), wo_half), :]

        def publish(slot, chip, k_t, k_o, k_local):
            sends = [rcopy(k_t, land_t.at[slot], wt_rows_of(chip, c), sib), rcopy(k_o, land_o.at[slot], wo_rows_of(chip, c), sib)]
            for cp in sends:
                cp.start()
            local.extend([lcopy(k_local, land_t.at[slot], wt_rows_of(chip, c)),
                          lcopy(k_local + 1, land_o.at[slot], wo_rows_of(chip, c))])
            return sends

        hop1 = [rcopy(0, half_t, land_t.at[0], p1), rcopy(1, half_o, land_o.at[0], p1), rcopy(2, cw_ref, cw_out.at[j], p1)]
        for cp in hop1:
            cp.start()
        rcopy(0, half_t, land_t.at[0], p1).wait_recv()
        rcopy(1, half_o, land_o.at[0], p1).wait_recv()
        rcopy(2, cw_ref, cw_out.at[j1], p1).wait_recv()
        hop2 = [rcopy(3, half_t, land_t.at[1], p2), rcopy(5, half_o, land_o.at[1], p2), rcopy(7, cw_ref, cw_out.at[j], p2),
                rcopy(4, land_t.at[0], land_t.at[2], p2), rcopy(6, land_o.at[0], land_o.at[2], p2),
                rcopy(8, cw_out.at[j1], cw_out.at[j1], p2)]
        for cp in hop2:
            cp.start()
        swaps = publish(0, j1, 9, 12, 3)
        rcopy(3, half_t, land_t.at[1], p2).wait_recv()
        rcopy(5, half_o, land_o.at[1], p2).wait_recv()
        rcopy(7, cw_ref, cw_out.at[j2], p2).wait_recv()
        swaps += publish(1, j2, 10, 13, 5)
        rcopy(4, half_t, land_t.at[2], p2).wait_recv()
        rcopy(6, half_o, land_o.at[2], p2).wait_recv()
        rcopy(8, cw_ref, cw_out.at[j3], p2).wait_recv()
        swaps += publish(2, j3, 11, 14, 7)
        for k, chip in ((9, j2), (10, j1), (11, j3)):
            rcopy(k, half_t, wt_rows_of(chip, 1 - c), sib).wait_recv()
        for k, chip in ((12, j2), (13, j1), (14, j3)):
            rcopy(k, half_o, wo_rows_of(chip, 1 - c), sib).wait_recv()
        for cp in hop1 + hop2 + swaps:
            cp.wait_send()
        for cp in local:
            cp.wait()

    any_spec = pl.BlockSpec(memory_space=pl.ANY)
    return pl.pallas_call(
        body, name="ag_weights",
        out_shape=(jax.ShapeDtypeStruct((N_CHIPS * wt_rows, width), BF16),
                   jax.ShapeDtypeStruct((N_CHIPS * wo_rows, width), BF16),
                   jax.ShapeDtypeStruct((N_CHIPS,) + cw_sh.shape, cw_sh.dtype)),
        in_specs=[any_spec, any_spec, any_spec],
        out_specs=(any_spec, any_spec, any_spec),
        scratch_shapes=[pltpu.VMEM((wt_rows, width), F32), pltpu.VMEM((wo_rows, width), F32),
                        pltpu.VMEM((wt_rows, width), BF16), pltpu.VMEM((wo_rows, width), BF16),
                        pltpu.VMEM((3, wt_half, width), BF16), pltpu.VMEM((3, wo_half, width), BF16),
                        pltpu.SemaphoreType.DMA((15,)), pltpu.SemaphoreType.DMA((15,)), pltpu.SemaphoreType.DMA((9,))],
        compiler_params=_params(32),
    )(wt_sh, wo_sh, cw_sh)


def _fwd_in(x, norm_in, wt):
    seq = x.shape[0]
    tile = TOK_TILE

    def body(x_ref, g_ref, wt_ref, h_ref, pc_ref, q_ref, kv_ref, ga_ref):
        xv = x_ref[...]
        h = (xv * _rstd(xv) * g_ref[...]).astype(BF16)
        h_ref[...] = h
        for blk in range(D_PC // D_CONV):
            pc_ref[:, blk * D_CONV:(blk + 1) * D_CONV] = _nt(h, wt_ref[blk * D_CONV:(blk + 1) * D_CONV, :])
        q_ref[...] = _nt(h, wt_ref[ROW_Q:ROW_KV, :])
        kv_ref[...] = _nt(h, wt_ref[ROW_KV:ROW_GA, :])
        ga_ref[...] = _nt(h, wt_ref[ROW_GA:D_IN_PROJ, :])

    def row(width):
        return pl.BlockSpec((tile, width), lambda i: (i, 0))

    return pl.pallas_call(
        body, name="fwd_in", grid=(seq // tile,),
        out_shape=(jax.ShapeDtypeStruct((seq, D_MODEL), BF16), jax.ShapeDtypeStruct((seq, D_PC), F32),
                   jax.ShapeDtypeStruct((seq, D_ATTN), F32), jax.ShapeDtypeStruct((seq, 2 * D_KV), F32),
                   jax.ShapeDtypeStruct((seq, D_ATTN), F32)),
        in_specs=[row(D_MODEL), _resident((1, D_MODEL)), _resident(wt.shape)],
        out_specs=(row(D_MODEL), row(D_PC), row(D_ATTN), row(2 * D_KV), row(D_ATTN)),
        compiler_params=_params(48, ("arbitrary",)),
    )(x, norm_in, wt)


def _conv_core(pc_ref, zbuf, cw_ref):
    tile = pc_ref.shape[0]
    cb = pc_ref[:, 0:D_CONV]
    cc = pc_ref[:, D_CONV:2 * D_CONV]
    cu = pc_ref[:, 2 * D_CONV:3 * D_CONV]
    z = cc * cu
    zbuf[8:tile + 8, :] = z
    z1 = zbuf[7:tile + 7, :]
    z2 = zbuf[6:tile + 6, :]
    conv = cw_ref[0:1, :] * z2 + cw_ref[1:2, :] * z1 + cw_ref[2:3, :] * z
    return cb, cc, cu, z, z1, z2, conv


def _conv_fwd(pc, conv_w, norm_conv_out):
    seq = pc.shape[0]
    tile = TOK_TILE

    def body(pc_ref, cw_ref, gn_ref, oc_ref, zbuf):
        @pl.when(pl.program_id(0) == 0)
        def _():
            zbuf[0:8, :] = jnp.zeros((8, D_CONV), F32)

        cb, _, _, _, _, _, conv = _conv_core(pc_ref, zbuf, cw_ref)
        yc = cb * conv
        silu, _ = _silu_and_grad(pc_ref[:, 3 * D_CONV:4 * D_CONV])
        oc_ref[...] = (yc * _rstd(yc) * gn_ref[...] * silu).astype(BF16)
        zbuf[0:8, :] = zbuf[tile:tile + 8, :]

    return pl.pallas_call(
        body, name="conv_fwd", grid=(seq // tile,),
        out_shape=jax.ShapeDtypeStruct((seq, D_CONV), BF16),
        in_specs=[pl.BlockSpec((tile, D_PC), lambda i: (i, 0)), _resident(conv_w.shape), _resident((1, D_CONV))],
        out_specs=pl.BlockSpec((tile, D_CONV), lambda i: (i, 0)),
        scratch_shapes=[pltpu.VMEM((tile + 8, D_CONV), F32)],
        compiler_params=_params(40, ("arbitrary",)),
    )(pc, conv_w, norm_conv_out)


def _band_geometry(block_index):
    qi = lax.broadcasted_iota(jnp.int32, (BLK, BLK), 0)
    kp = lax.broadcasted_iota(jnp.int32, (BLK, BLK), 1)
    use_cur = kp <= qi
    dist = jnp.where(use_cur, qi - kp, qi - kp + BLK).astype(F32)
    valid = use_cur | (block_index > 0)
    return use_cur, dist, valid


def _block_diag(cur, prev, group):
    lane = lax.broadcasted_iota(jnp.int32, cur.shape, 1)

    def halves(t):
        other = pltpu.roll(t, 64, 1)
        lo, hi = (t, other) if group == 0 else (other, t)
        return jnp.where(lane < 64, lo, 0.0), jnp.where(lane >= 64, hi, 0.0)

    return jnp.concatenate(halves(cur) + halves(prev), axis=0).astype(BF16)


def _merge(s4, use_cur):
    return (jnp.where(use_cur, s4[:, 0:BLK], s4[:, 2 * BLK:3 * BLK]),
            jnp.where(use_cur, s4[:, BLK:2 * BLK], s4[:, 3 * BLK:4 * BLK]))


def _split(a, b, use_cur):
    return jnp.concatenate([jnp.where(use_cur, a, 0.0), jnp.where(use_cur, b, 0.0),
                            jnp.where(use_cur, 0.0, a), jnp.where(use_cur, 0.0, b)], axis=1)


def _softmax_head(s, head, sink, dist, valid):
    sc = jnp.where(valid, s - SLOPES[head] * dist, -jnp.inf)
    m = jnp.maximum(jnp.max(sc, axis=-1, keepdims=True), sink)
    p = jnp.exp(sc - m)
    es = jnp.exp(sink - m)
    inv = 1.0 / (jnp.sum(p, axis=-1, keepdims=True) + es)
    return p * inv, es * inv


def _attn_probs(q_ref, kvc_ref, kvp_ref, sink_ref, geometry):
    use_cur, dist, valid = geometry
    groups = range(N_HEADS // HEADS_PER_KV)
    kbd = [_block_diag(kvc_ref[:, 0:D_KV], kvp_ref[:, 0:D_KV], g) for g in groups]
    vbd = [_block_diag(kvc_ref[:, D_KV:2 * D_KV], kvp_ref[:, D_KV:2 * D_KV], g) for g in groups]
    qps = [(q_ref[:, j * 128:(j + 1) * 128] * SCALE).astype(BF16) for j in range(N_HEADS // 2)]
    scores = []
    for j, qp in enumerate(qps):
        scores += _merge(_nt(qp, kbd[j // 4]), use_cur)
    sinks = [sink_ref[0, h] for h in range(N_HEADS)]
    scores = [jnp.where(valid, s - SLOPES[h] * dist, -jnp.inf) for h, s in enumerate(scores)]
    maxes = [jnp.maximum(jnp.max(s, axis=-1, keepdims=True), sinks[h]) for h, s in enumerate(scores)]
    exps = [jnp.exp(s - m) for s, m in zip(scores, maxes)]
    sink_exps = [jnp.exp(sinks[h] - m) for h, m in enumerate(maxes)]
    invs = [1.0 / (jnp.sum(e, axis=-1, keepdims=True) + se) for e, se in zip(exps, sink_exps)]
    probs = [e * inv for e, inv in zip(exps, invs)]
    sink_probs = [se * inv for se, inv in zip(sink_exps, invs)]
    return qps, kbd, vbd, probs, sink_probs


def _kv_specs(n_blocks, reverse):
    def blk(i):
        return (n_blocks - 1 - i) if reverse else i
    cur = pl.BlockSpec((BLK, 2 * D_KV), lambda i: (blk(i), 0))
    prev = pl.BlockSpec((BLK, 2 * D_KV), lambda i: (jnp.maximum(blk(i) - 1, 0), 0))
    return cur, prev


def _attn_fwd(q, kv, ga, sinks, norm_attn_out):
    seq = q.shape[0]
    n_blocks = seq // BLK

    def body(q_ref, kvc_ref, kvp_ref, ga_ref, sink_ref, gn_ref, ya_ref, oa_ref):
        geometry = _band_geometry(pl.program_id(0))
        _, _, vbd, probs, _ = _attn_probs(q_ref, kvc_ref, kvp_ref, sink_ref, geometry)
        p4s = [_split(probs[2 * j], probs[2 * j + 1], geometry[0]).astype(BF16) for j in range(N_HEADS // 2)]
        ya = jnp.concatenate([_nn(p4, vbd[j // 4]) for j, p4 in enumerate(p4s)], axis=1)
        ya_ref[...] = ya
        silu, _ = _silu_and_grad(ga_ref[...])
        oa_ref[...] = (ya * _rstd(ya) * gn_ref[...] * silu).astype(BF16)

    row = pl.BlockSpec((BLK, D_ATTN), lambda i: (i, 0))
    kv_cur, kv_prev = _kv_specs(n_blocks, reverse=False)
    return pl.pallas_call(
        body, name="attn_fwd", grid=(n_blocks,),
        out_shape=(jax.ShapeDtypeStruct((seq, D_ATTN), F32), jax.ShapeDtypeStruct((seq, D_ATTN), BF16)),
        in_specs=[row, kv_cur, kv_prev, row, pl.BlockSpec(memory_space=pltpu.SMEM), _resident((1, D_ATTN))],
        out_specs=(row, row),
        compiler_params=_params(32, ("arbitrary",)),
    )(q, kv, kv, ga, sinks, norm_attn_out)


def _out_proj_loss(x, oc, oa, wo, norm_final, target):
    seq = x.shape[0]
    tile = TOK_TILE

    def body(x_ref, oc_ref, oa_ref, wo_ref, gf_ref, t_ref, dx2_ref, doc_ref, doa_ref, gwo_ref, gnf_ref, loss_ref):
        @pl.when(pl.program_id(0) == 0)
        def _():
            gwo_ref[...] = jnp.zeros_like(gwo_ref)
            gnf_ref[...] = jnp.zeros_like(gnf_ref)
            loss_ref[...] = jnp.zeros_like(loss_ref)

        oc, oa = oc_ref[...], oa_ref[...]
        x2 = x_ref[...] + _nn(oc, wo_ref[0:D_CONV, :]) + _nn(oa, wo_ref[D_CONV:D_MIX, :])
        r = _rstd(x2)
        xhat = x2 * r
        err = xhat * gf_ref[...] - t_ref[...]
        loss_ref[...] += jnp.sum(err * err, axis=0, keepdims=True) * (0.5 / D_MODEL)
        dy = err * (1.0 / D_MODEL)
        gnf_ref[...] += jnp.sum(dy * xhat, axis=0, keepdims=True)
        dx2 = _rms_bwd(dy * gf_ref[...], xhat, r)
        dx2_ref[...] = dx2
        db = dx2.astype(BF16)
        doc_ref[...] = _nt(db, wo_ref[0:D_CONV, :])
        doa_ref[...] = _nt(db, wo_ref[D_CONV:D_MIX, :])
        gwo_ref[0:D_CONV, :] += _tn(oc, db)
        gwo_ref[D_CONV:D_MIX, :] += _tn(oa, db)

    row = pl.BlockSpec((tile, D_MODEL), lambda i: (i, 0))
    vec = pl.BlockSpec((1, D_MODEL), lambda i: (0, 0))
    return pl.pallas_call(
        body, name="out_proj_loss", grid=(seq // tile,),
        out_shape=(jax.ShapeDtypeStruct((seq, D_MODEL), F32), jax.ShapeDtypeStruct((seq, D_CONV), F32),
                   jax.ShapeDtypeStruct((seq, D_ATTN), F32), jax.ShapeDtypeStruct((D_MIX, D_MODEL), F32),
                   jax.ShapeDtypeStruct((1, D_MODEL), F32), jax.ShapeDtypeStruct((1, D_MODEL), F32)),
        in_specs=[row, row, row, _resident(wo.shape), _resident((1, D_MODEL)), row],
        out_specs=(row, row, row, pl.BlockSpec((D_MIX, D_MODEL), lambda i: (0, 0)), vec, vec),
        compiler_params=_params(48, ("arbitrary",)),
    )(x, oc, oa, wo, norm_final, target)


def _conv_bwd(pc, doc, conv_w, norm_conv_out):
    seq = pc.shape[0]
    tile = TOK_TILE
    n_tiles = seq // tile

    def body(pc_ref, hcc_ref, hcu_ref, doc_ref, cw_ref, gn_ref, dpc_ref, gnc_ref, gcw_ref, zbuf, dbuf):
        step = pl.program_id(0)

        @pl.when(step == 0)
        def _():
            gnc_ref[...] = jnp.zeros_like(gnc_ref)
            gcw_ref[...] = jnp.zeros_like(gcw_ref)
            dbuf[tile:tile + 8, :] = jnp.zeros((8, D_CONV), F32)

        is_first_tile = step == n_tiles - 1
        zbuf[0:8, :] = jnp.where(is_first_tile, 0.0, hcc_ref[...] * hcu_ref[...])
        cb, cc, cu, z, z1, z2, conv = _conv_core(pc_ref, zbuf, cw_ref)
        gc = pc_ref[:, 3 * D_CONV:4 * D_CONV]
        silu, dsilu = _silu_and_grad(gc)
        yc = cb * conv
        r = _rstd(yc)
        xhat = yc * r
        do = doc_ref[...]
        dn = do * silu
        dpc_ref[:, 3 * D_CONV:4 * D_CONV] = (do * (xhat * gn_ref[...]) * dsilu).astype(BF16)
        gnc_ref[...] += jnp.sum(dn * xhat, axis=0, keepdims=True)
        dyc = _rms_bwd(dn * gn_ref[...], xhat, r)
        dpc_ref[:, 0:D_CONV] = (dyc * conv).astype(BF16)
        dconv = dyc * cb
        gcw_ref[0:1, :] += jnp.sum(dconv * z2, axis=0, keepdims=True)
        gcw_ref[1:2, :] += jnp.sum(dconv * z1, axis=0, keepdims=True)
        gcw_ref[2:3, :] += jnp.sum(dconv * z, axis=0, keepdims=True)
        dbuf[0:tile, :] = dconv
        dz = cw_ref[2:3, :] * dconv + cw_ref[1:2, :] * dbuf[1:tile + 1, :] + cw_ref[0:1, :] * dbuf[2:tile + 2, :]
        dpc_ref[:, D_CONV:2 * D_CONV] = (dz * cu).astype(BF16)
        dpc_ref[:, 2 * D_CONV:3 * D_CONV] = (dz * cc).astype(BF16)
        dbuf[tile:tile + 8, :] = dbuf[0:8, :]

    def rev(i):
        return n_tiles - 1 - i

    def halo(col_block):
        return pl.BlockSpec((8, D_CONV), lambda i: (jnp.maximum(rev(i) * (tile // 8) - 1, 0), col_block))

    return pl.pallas_call(
        body, name="conv_bwd", grid=(n_tiles,),
        out_shape=(jax.ShapeDtypeStruct((seq, D_PC), BF16), jax.ShapeDtypeStruct((1, D_CONV), F32),
                   jax.ShapeDtypeStruct((8, D_CONV), F32)),
        in_specs=[pl.BlockSpec((tile, D_PC), lambda i: (rev(i), 0)), halo(1), halo(2),
                  pl.BlockSpec((tile, D_CONV), lambda i: (rev(i), 0)), _resident(conv_w.shape), _resident((1, D_CONV))],
        out_specs=(pl.BlockSpec((tile, D_PC), lambda i: (rev(i), 0)), pl.BlockSpec((1, D_CONV), lambda i: (0, 0)),
                   pl.BlockSpec((8, D_CONV), lambda i: (0, 0))),
        scratch_shapes=[pltpu.VMEM((tile + 8, D_CONV), F32), pltpu.VMEM((tile + 8, D_CONV), F32)],
        compiler_params=_params(48, ("arbitrary",)),
    )(pc, pc, pc, doc, conv_w, norm_conv_out)


def _attn_bwd(q, kv, ga, ya, doa, sinks, norm_attn_out):
    seq = q.shape[0]
    n_blocks = seq // BLK

    def body(q_ref, kvc_ref, kvp_ref, ga_ref, ya_ref, doa_ref, sink_ref, gn_ref,
             dqg_ref, gna_ref, gs_ref, carry, dya_buf):
        step = pl.program_id(0)

        @pl.when(step == 0)
        def _():
            gna_ref[...] = jnp.zeros_like(gna_ref)
            gs_ref[...] = jnp.zeros_like(gs_ref)
            carry[...] = jnp.zeros_like(carry)

        ya = ya_ref[...]
        r = _rstd(ya)
        xhat = ya * r
        silu, dsilu = _silu_and_grad(ga_ref[...])
        do = doa_ref[...]
        dn = do * silu
        dqg_ref[:, D_ATTN:2 * D_ATTN] = (do * (xhat * gn_ref[...]) * dsilu).astype(BF16)
        gna_ref[...] += jnp.sum(dn * xhat, axis=0, keepdims=True)
        dya_buf[...] = _rms_bwd(dn * gn_ref[...], xhat, r).astype(BF16)

        geometry = _band_geometry(n_blocks - 1 - step)
        use_cur = geometry[0]
        lane = lax.broadcasted_iota(jnp.int32, (BLK, 128), 1)
        sink_lane = lax.broadcasted_iota(jnp.int32, (1, 128), 1)
        gsink = jnp.zeros((1, 128), F32)

        def fold(bd):
            return (jnp.where(lane < 64, bd[0:BLK], 0.0) + jnp.where(lane >= 64, bd[BLK:2 * BLK], 0.0),
                    jnp.where(lane < 64, bd[2 * BLK:3 * BLK], 0.0) + jnp.where(lane >= 64, bd[3 * BLK:4 * BLK], 0.0))

        pairs = range(N_HEADS // 2)
        qps, kbd, vbd, probs, sink_probs = _attn_probs(q_ref, kvc_ref, kvp_ref, sink_ref, geometry)
        dyps = [dya_buf[:, j * 128:(j + 1) * 128] for j in pairs]
        dps = []
        for j in pairs:
            dps += _merge(_nt(dyps[j], vbd[j // 4]), use_cur)
        deltas = [jnp.sum(p * dp, axis=-1, keepdims=True) for p, dp in zip(probs, dps)]
        dss = [p * (dp - delta) for p, dp, delta in zip(probs, dps, deltas)]
        for h in range(N_HEADS):
            dsink = -jnp.sum(sink_probs[h] * deltas[h], axis=0, keepdims=True)
            gsink = gsink + jnp.where(sink_lane == h, dsink, 0.0)
        gs_ref[...] += gsink
        ds4s = [_split(dss[2 * j], dss[2 * j + 1], use_cur).astype(BF16) for j in pairs]
        p4s = [_split(probs[2 * j], probs[2 * j + 1], use_cur).astype(BF16) for j in pairs]
        dqg_ref[:, 0:D_ATTN] = jnp.concatenate([_nn(ds4s[j], kbd[j // 4]) * SCALE for j in pairs], axis=1).astype(BF16)
        sums = []
        for group in range(N_HEADS // HEADS_PER_KV):
            acc = [jnp.zeros((BLK, 128), F32) for _ in range(4)]
            for j in range(group * 4, group * 4 + 4):
                for slot, part in enumerate(fold(_tn(ds4s[j], qps[j])) + fold(_tn(p4s[j], dyps[j]))):
                    acc[slot] = acc[slot] + part
            sums.append([a + pltpu.roll(a, 64, 1) for a in acc])
        dk_cur, dk_prev, dv_cur, dv_prev = (jnp.where(lane < 64, a, b) for a, b in zip(sums[0], sums[1]))
        dqg_ref[:, 2 * D_ATTN:2 * D_ATTN + 2 * D_KV] = (jnp.concatenate([dk_cur, dv_cur], axis=1) + carry[...]).astype(BF16)
        carry[...] = jnp.concatenate([dk_prev, dv_prev], axis=1)

    row = pl.BlockSpec((BLK, D_ATTN), lambda i: (n_blocks - 1 - i, 0))
    kv_cur, kv_prev = _kv_specs(n_blocks, reverse=True)
    return pl.pallas_call(
        body, name="attn_bwd", grid=(n_blocks,),
        out_shape=(jax.ShapeDtypeStruct((seq, D_QG), BF16), jax.ShapeDtypeStruct((1, D_ATTN), F32),
                   jax.ShapeDtypeStruct((1, 128), F32)),
        in_specs=[row, kv_cur, kv_prev, row, row, row, pl.BlockSpec(memory_space=pltpu.SMEM), _resident((1, D_ATTN))],
        out_specs=(pl.BlockSpec((BLK, D_QG), lambda i: (n_blocks - 1 - i, 0)),
                   pl.BlockSpec((1, D_ATTN), lambda i: (0, 0)), pl.BlockSpec((1, 128), lambda i: (0, 0))),
        scratch_shapes=[pltpu.VMEM((BLK, 2 * D_KV), F32), pltpu.VMEM((BLK, D_ATTN), BF16)],
        compiler_params=_params(32, ("arbitrary",)),
    )(q, kv, kv, ga, ya, doa, sinks, norm_attn_out)


GBLK = 256
GSUB = 64
PAIR_RING = 4
LAG_PAIR, LAG_HOP1, LAG_HOP2 = 1, 5, 10


def _bwd_in(dpc, dqg, h, wt, x, norm_in, dx2, small):
    seq = x.shape[0]
    n_blk = D_IN_PROJ // GBLK
    per_chip = n_blk // N_CHIPS
    n_slots = (n_blk + 1) // 2
    n_sub = GBLK // GSUB
    chip_rows = D_IN_PROJ // N_CHIPS
    tile = TOK_TILE
    n_tiles = seq // tile
    n_steps = n_blk + max(n_tiles, LAG_HOP2)
    chunk = min(seq, 512)
    blk_q, blk_kv, blk_ga = ROW_Q // GBLK, ROW_KV // GBLK, ROW_GA // GBLK

    def block_of(i):
        k = i % N_CHIPS
        robin = per_chip * ((k % 2) * 2 + k // 2) + i // N_CHIPS
        if isinstance(i, int):
            return robin if i < per_chip * N_CHIPS else i
        return jnp.where(i < per_chip * N_CHIPS, robin, i)

    def owner_of(i):
        return (i // N_CHIPS) % 2

    def slot_of(i):
        return (i // (2 * N_CHIPS)) * N_CHIPS + i % N_CHIPS

    def body(dpc_ref, dqg_ref, wt_ref, h_ref, x_ref, g_ref, dx2_ref, small_ref, gx_ref, small_sum, gwt_sh,
             dh_acc, gni, keep, pbuf, xbuf, land, land2, small_land,
             pair_send, pair_recv, h1_send, h1_recv, h2_send, h2_recv, sw_send, sw_recv, sm_send, sm_recv, out_sem):
        step = pl.program_id(0)
        x_i, y_i, c = lax.axis_index("x"), lax.axis_index("y"), lax.axis_index("c")
        me = 4 * x_i + 2 * y_i + c
        j = 2 * x_i + y_i
        pa = (_xor(x_i, 1 - c), _xor(y_i, c), c)
        pb = (_xor(x_i, c), _xor(y_i, 1 - c), c)
        sib = (x_i, y_i, 1 - c)
        ja = 2 * pa[0] + pa[1]
        jb = 2 * pb[0] + pb[1]
        jd = 3 - j

        def remote(src, dst, send, recv, to):
            return pltpu.make_async_remote_copy(src_ref=src, dst_ref=dst, send_sem=send, recv_sem=recv,
                                                device_id=to, device_id_type=MESH)

        def piece(ref, slot, u, n):
            return ref.at[slot, pl.ds(u * GSUB, n * GSUB), :]

        def chip_rows_at(ref, local, n):
            return ref.at[pl.ds(pl.multiple_of(local, GSUB), n * GSUB), :]

        def pair_copy(i):
            slot = slot_of(i)
            return remote(pbuf.at[i % PAIR_RING], land.at[slot], pair_send.at[slot], pair_recv.at[slot], sib)

        def h1_copy(slot, u, n):
            k = slot * n_sub + u
            return remote(piece(xbuf, slot, u, n), piece(xbuf, slot, u, n), h1_send.at[k], h1_recv.at[k], pa)

        def h2_copy(slot, u, n, local):
            k = slot * n_sub + u
            return remote(piece(xbuf, slot, u, n), chip_rows_at(land2, local, n), h2_send.at[k], h2_recv.at[k], pb)

        def sw_copy(slot, u, n, local):
            k = slot * n_sub + u
            return remote(piece(keep, slot, u, n), chip_rows_at(gwt_sh, local, n), sw_send.at[k], sw_recv.at[k], sib)

        def owned(i):
            return (i >= 0) & (i < n_blk) & (owner_of(i) == c)

        def chip_of(blk, u):
            row = blk * GBLK + u * GSUB
            chip = row // chip_rows
            return chip, row - chip * chip_rows

        def pieces(blk):
            first, local = chip_of(blk, 0)
            whole = first == chip_of(blk, n_sub - 1)[0]
            if isinstance(blk, int):
                return [(True, 0, n_sub, first, local)] if whole else [(True, u, 1) + chip_of(blk, u) for u in range(n_sub)]
            return [(whole, 0, n_sub, first, local)] + [(jnp.logical_not(whole), u, 1) + chip_of(blk, u) for u in range(n_sub)]

        @pl.when(step == 0)
        def _():
            dh_acc[...] = jnp.zeros_like(dh_acc)
            gni[...] = jnp.zeros_like(gni)

        @pl.when(step < n_blk)
        def _():
            from_pc = block_of(step) < blk_q
            block = _tn(jnp.where(from_pc, dpc_ref[...], dqg_ref[...]), h_ref[...])
            for t in range(0, seq, chunk):
                d = jnp.where(from_pc, dpc_ref[t:t + chunk, :], dqg_ref[t:t + chunk, :])
                dh_acc[t:t + chunk, :] += _nn(d, wt_ref[...])

            @pl.when(owner_of(step) == c)
            def _():
                keep[slot_of(step)] = block

            @pl.when(owner_of(step) != c)
            def _():
                @pl.when(step >= 2 * PAIR_RING)
                def _():
                    pair_copy(step - 2 * PAIR_RING).wait_send()
                pbuf[step % PAIR_RING] = block.astype(BF16)
                pair_copy(step).start()

        i1 = step - LAG_PAIR

        @pl.when(owned(i1))
        def _():
            slot = slot_of(i1)
            pair_copy(i1).wait_recv()
            _accumulate(keep.at[slot], land.at[slot])
            for cond, u, n, chip, _ in pieces(block_of(i1)):
                @pl.when(cond & ((chip == ja) | (chip == jd)))
                def _(u=u, n=n):
                    _cast_rows(piece(keep, slot, u, n), piece(xbuf, slot, u, n))
                    h1_copy(slot, u, n).start()

        i2 = step - LAG_HOP1

        @pl.when(owned(i2))
        def _():
            slot = slot_of(i2)
            for cond, u, n, chip, local in pieces(block_of(i2)):
                @pl.when(cond & ((chip == j) | (chip == jb)))
                def _(u=u, n=n, chip=chip, local=local):
                    h1_copy(slot, u, n).wait_recv()
                    _accumulate(piece(keep, slot, u, n), piece(xbuf, slot, u, n))

                    @pl.when(chip == jb)
                    def _():
                        _cast_rows(piece(keep, slot, u, n), piece(xbuf, slot, u, n))
                        h2_copy(slot, u, n, local).start()

                @pl.when(cond & ((chip == ja) | (chip == jd)))
                def _(u=u, n=n):
                    h1_copy(slot, u, n).wait_send()

        i3 = step - LAG_HOP2

        @pl.when(owned(i3))
        def _():
            slot = slot_of(i3)
            for cond, u, n, chip, local in pieces(block_of(i3)):
                @pl.when(cond & (chip == j))
                def _(u=u, n=n, local=local):
                    h2_copy(slot, u, n, local).wait_recv()
                    _accumulate(piece(keep, slot, u, n), chip_rows_at(land2, local, n))
                    mine = pltpu.make_async_copy(piece(keep, slot, u, n), chip_rows_at(gwt_sh, local, n), out_sem.at[0])
                    mine.start()
                    sw_copy(slot, u, n, local).start()
                    mine.wait()

                @pl.when(cond & (chip == jb))
                def _(u=u, n=n, local=local):
                    h2_copy(slot, u, n, local).wait_send()

        e = step - n_blk

        @pl.when((e >= 0) & (e < n_tiles))
        def _():
            dh = dh_acc[pl.ds(pl.multiple_of(e * tile, tile), tile), :]
            xv = x_ref[...]
            r = _rstd(xv)
            xhat = xv * r
            gni[...] += jnp.sum(dh * xhat, axis=0, keepdims=True)
            gx_ref[...] = _rms_bwd(dh * g_ref[...], xhat, r) + dx2_ref[...]

        @pl.when(step == n_steps - 1)
        def _():
            small_land[me] = small_ref[...]
            small_land[me, 0:1, :] = gni[...]
            others = [(dx, dy, dc) for dx in (0, 1) for dy in (0, 1) for dc in (0, 1)][1:]
            sends = [remote(small_land.at[me], small_land.at[me], sm_send.at[k], sm_recv.at[k],
                            (_xor(x_i, dx), _xor(y_i, dy), _xor(c, dc))) for k, (dx, dy, dc) in enumerate(others)]
            for cp in sends:
                cp.start()
            for i in range(n_blk):
                if i + 2 * PAIR_RING >= n_blk:
                    @pl.when(owner_of(i) != c)
                    def _(i=i):
                        pair_copy(i).wait_send()
                for _, u, n, chip, local in pieces(block_of(i)):
                    @pl.when((j == chip) & (c == owner_of(i)))
                    def _(i=i, u=u, n=n, local=local):
                        sw_copy(slot_of(i), u, n, local).wait_send()

                    @pl.when((j == chip) & (c != owner_of(i)))
                    def _(i=i, u=u, n=n, local=local):
                        sw_copy(slot_of(i), u, n, local).wait_recv()
            for cp in sends:
                cp.wait_recv()
            total = small_land[0]
            for dev in range(1, 8):
                total = total + small_land[dev]
            small_sum[...] = total
            for cp in sends:
                cp.wait_send()

    def blk_at(i):
        return block_of(jnp.clip(i, 0, n_blk - 1))

    last_pc_step = max(i for i in range(n_blk) if block_of(i) < blk_q)

    def next_block(i, in_pc):
        i = jnp.clip(i, 0, n_blk - 1)
        step = jnp.full_like(i, last_pc_step if in_pc else n_blk - 1)
        for ahead in reversed(range(N_CHIPS)):
            cand = jnp.minimum(i + ahead, n_blk - 1)
            step = jnp.where((block_of(cand) < blk_q) == in_pc, cand, step)
        return block_of(step)

    def dqg_block(i):
        b = next_block(i, False)
        q_blk = jnp.clip(b - blk_q, 0, blk_kv - blk_q - 1)
        ga_blk = (D_ATTN // GBLK) + jnp.clip(b - blk_ga, 0, n_blk - blk_ga - 1)
        return jnp.where(b < blk_kv, q_blk, jnp.where(b == blk_kv, 2 * D_ATTN // GBLK, ga_blk))

    def tok(i):
        return (jnp.clip(i - n_blk, 0, n_tiles - 1), 0)

    n_piece = n_slots * n_sub
    dma = pltpu.SemaphoreType.DMA
    return pl.pallas_call(
        body, name="bwd_in", grid=(n_steps,),
        out_shape=(jax.ShapeDtypeStruct((seq, D_MODEL), F32), jax.ShapeDtypeStruct(small.shape, F32),
                   jax.ShapeDtypeStruct((chip_rows, D_MODEL), F32)),
        in_specs=[pl.BlockSpec((seq, GBLK), lambda i: (0, next_block(i, True))),
                  pl.BlockSpec((seq, GBLK), lambda i: (0, dqg_block(i))),
                  pl.BlockSpec((GBLK, D_MODEL), lambda i: (blk_at(i), 0)),
                  _resident(h.shape),
                  pl.BlockSpec((tile, D_MODEL), tok), _resident((1, D_MODEL)), pl.BlockSpec((tile, D_MODEL), tok),
                  _resident(small.shape)],
        out_specs=(pl.BlockSpec((tile, D_MODEL), tok), pl.BlockSpec(small.shape, lambda i: (0, 0)),
                   pl.BlockSpec(memory_space=pl.ANY)),
        scratch_shapes=[pltpu.VMEM((seq, D_MODEL), F32), pltpu.VMEM((1, D_MODEL), F32),
                        pltpu.VMEM((n_slots, GBLK, D_MODEL), F32), pltpu.VMEM((PAIR_RING, GBLK, D_MODEL), BF16),
                        pltpu.VMEM((n_slots, GBLK, D_MODEL), BF16), pltpu.VMEM((n_slots, GBLK, D_MODEL), BF16),
                        pltpu.VMEM((chip_rows, D_MODEL), BF16), pltpu.VMEM((8,) + small.shape, F32),
                        dma((n_slots,)), dma((n_slots,)), dma((n_piece,)), dma((n_piece,)), dma((n_piece,)),
                        dma((n_piece,)), dma((n_piece,)), dma((n_piece,)), dma((7,)), dma((7,)), dma((1,))],
        compiler_params=_params(62, ("arbitrary",)),
    )(dpc, dqg, wt, h, x, norm_in, dx2, small)


def _accumulate(dst_ref, src_ref, rows=16):
    def step(i, carry):
        sl = pl.ds(pl.multiple_of(i * rows, rows), rows)
        dst_ref[sl, :] = dst_ref[sl, :] + src_ref[sl, :].astype(F32)
        return carry
    lax.fori_loop(0, dst_ref.shape[0] // rows, step, 0)


def _rs_wout(gwo):
    o_rows = gwo.shape[0] // N_CHIPS
    o_half = o_rows // 2
    width = gwo.shape[1]

    def body(gwo_ref, gwo_sh, acc_o, stage_o, sb_o, r1o, r2o, r3o, send_sems, recv_sems, local_sems):
        x, y, c = lax.axis_index("x"), lax.axis_index("y"), lax.axis_index("c")
        j = 2 * x + y
        pa = (_xor(x, 1 - c), _xor(y, c), c)
        pb = (_xor(x, c), _xor(y, 1 - c), c)
        sib = (x, y, 1 - c)
        ja = 2 * pa[0] + pa[1]
        jb = 2 * pb[0] + pb[1]
        jd = 3 - j

        def rcopy(k, src, dst, to):
            return pltpu.make_async_remote_copy(src_ref=src, dst_ref=dst, send_sem=send_sems.at[k],
                                                recv_sem=recv_sems.at[k], device_id=to, device_id_type=MESH)

        def o_rows_of(chip, half):
            return gwo_ref.at[pl.ds(pl.multiple_of(chip * o_rows + half * o_half, 8), o_half), :]

        def load(src, dst):
            cp = pltpu.make_async_copy(src, dst, local_sems.at[0])
            cp.start()
            return cp

        order = (ja, jd, jb, j)
        sib_order = (jb, jd, ja, j)
        pair_sends = []
        for s, chip in enumerate(sib_order):
            load(o_rows_of(chip, 1 - c), stage_o).wait()
            _cast_rows(stage_o, sb_o.at[s])
            pair_sends.append(rcopy(s, sb_o.at[s], r1o.at[s], sib))
            pair_sends[-1].start()

        def resend(s, k, landing, to):
            pair_sends[s].wait_send()
            _cast_rows(acc_o.at[s], sb_o.at[s])
            cp = rcopy(k, sb_o.at[s], landing, to)
            cp.start()
            return cp

        hop1 = []
        for s, chip in enumerate(order):
            load(o_rows_of(chip, c), acc_o.at[s]).wait()
            rcopy(s, sb_o.at[s], r1o.at[s], sib).wait_recv()
            _accumulate(acc_o.at[s], r1o.at[s])
            if s < 2:
                hop1.append(resend(s, 4 + s, r2o.at[s], pa))

        rcopy(5, sb_o.at[1], r2o.at[1], pa).wait_recv()
        _accumulate(acc_o.at[2], r2o.at[1])
        hop2 = resend(2, 6, r3o, pb)
        rcopy(4, sb_o.at[0], r2o.at[0], pa).wait_recv()
        _accumulate(acc_o.at[3], r2o.at[0])
        rcopy(6, sb_o.at[2], r3o, pb).wait_recv()
        _accumulate(acc_o.at[3], r3o)

        out_o = gwo_sh.at[pl.ds(pl.multiple_of(c * o_half, 8), o_half), :]
        keep = load(acc_o.at[3], out_o)
        swap = rcopy(7, acc_o.at[3], out_o, sib)
        swap.start()
        rcopy(7, acc_o.at[3], gwo_sh.at[pl.ds(pl.multiple_of((1 - c) * o_half, 8), o_half), :], sib).wait_recv()
        for cp in [pair_sends[3]] + hop1 + [hop2, swap]:
            cp.wait_send()
        keep.wait()

    any_spec = pl.BlockSpec(memory_space=pl.ANY)
    return pl.pallas_call(
        body, name="rs_wout",
        out_shape=jax.ShapeDtypeStruct((o_rows, width), F32),
        in_specs=[any_spec], out_specs=any_spec,
        scratch_shapes=[pltpu.VMEM((4, o_half, width), F32),
                        pltpu.VMEM((o_half, width), F32),
                        pltpu.VMEM((4, o_half, width), BF16),
                        pltpu.VMEM((4, o_half, width), BF16),
                        pltpu.VMEM((2, o_half, width), BF16),
                        pltpu.VMEM((o_half, width), BF16),
                        pltpu.SemaphoreType.DMA((8,)), pltpu.SemaphoreType.DMA((8,)), pltpu.SemaphoreType.DMA((1,))],
        compiler_params=_params(24),
    )(gwo)


def _adamw(name, w, g, m, v, rows):
    def body(w_ref, g_ref, m_ref, v_ref, go_ref, d_ref, nm_ref, nv_ref):
        gv = g_ref[...]
        go_ref[...] = gv
        nm = ADAM_B1 * m_ref[...] + (1.0 - ADAM_B1) * gv
        nv = ADAM_B2 * v_ref[...] + (1.0 - ADAM_B2) * (gv * gv)
        m_hat = nm / (1.0 - ADAM_B1 ** ADAM_STEP)
        v_hat = nv / (1.0 - ADAM_B2 ** ADAM_STEP)
        d_ref[...] = -ADAM_LR * (m_hat / (jnp.sqrt(v_hat) + ADAM_EPS) + ADAM_WD * w_ref[...])
        nm_ref[...] = nm
        nv_ref[...] = nv

    spec = pl.BlockSpec((rows, w.shape[1]), lambda i: (i, 0))
    shape = jax.ShapeDtypeStruct(w.shape, F32)
    return pl.pallas_call(
        body, name="adamw_" + name, grid=(w.shape[0] // rows,),
        out_shape=(shape,) * 4, in_specs=[spec] * 4, out_specs=(spec,) * 4,
        compiler_params=_params(32, ("arbitrary",)),
    )(w, g, m, v)


def kernel(x, norm_in, w_in, conv_w, attn_sinks, norm_conv_out, norm_attn_out, w_out, norm_final, loss_target, m_norm_in, m_w_in, m_conv_w, m_attn_sinks, m_norm_conv_out, m_norm_attn_out, m_w_out, m_norm_final, v_norm_in, v_w_in, v_conv_w, v_attn_sinks, v_norm_conv_out, v_norm_attn_out, v_w_out, v_norm_final):
    chip = 2 * lax.axis_index("x") + lax.axis_index("y")
    def hbm(*arrays):
        pinned = tuple(pltpu.with_memory_space_constraint(a, pltpu.HBM) for a in arrays)
        return pinned if len(pinned) > 1 else pinned[0]

    xs, target = hbm(x[0], loss_target[0])
    norm_final2 = norm_final.reshape(1, D_MODEL)
    w_in_t, m_w_in_t, v_w_in_t, w_out2 = hbm(w_in[0].T, m_w_in[0].T, v_w_in[0].T, w_out[0])

    wt, wo, cw4 = _ag_weights(w_in_t, w_out2, conv_w[0])
    wt, wo = hbm(wt, wo)
    cw = jnp.transpose(cw4, (1, 0, 2)).reshape(3, D_CONV)

    h, pc, q, kv, ga = hbm(*_fwd_in(xs, norm_in, wt))
    oc = hbm(_conv_fwd(pc, cw, norm_conv_out))
    ya, oa = hbm(*_attn_fwd(q, kv, ga, attn_sinks, norm_attn_out))
    dx2, doc, doa, gwo, gnf, loss_lanes = _out_proj_loss(xs, oc, oa, wo, norm_final2, target)
    dx2, doc, doa, gwo = hbm(dx2, doc, doa, gwo)

    dpc, gnc, gcw = _conv_bwd(pc, doc, cw, norm_conv_out)
    dqg, gna, gsink = _attn_bwd(q, kv, ga, ya, doa, attn_sinks, norm_attn_out)
    dpc, dqg = hbm(dpc, dqg)
    gwo_sh = _rs_wout(gwo)

    last = jnp.zeros((1, D_MODEL), F32).at[:, 0:N_HEADS].set(gsink[:, 0:N_HEADS]).at[0, N_HEADS].set(jnp.sum(loss_lanes))
    small = jnp.concatenate([jnp.zeros((1, D_MODEL), F32), gnc, gna, gnf, gcw[0:3], last], axis=0)
    grad_x, small_sum, gwt_sh = _bwd_in(dpc, dqg, h, wt, xs, norm_in, dx2, small)
    gwt_sh = hbm(gwt_sh)

    loss = small_sum[7, N_HEADS]
    g_norm_in, g_norm_conv, g_norm_attn = small_sum[0:1], small_sum[1:2], small_sum[2:3]
    g_norm_final = small_sum[3]
    g_conv_w = lax.dynamic_slice(small_sum[4:7], (0, chip * (D_CONV // N_CHIPS)), (3, D_CONV // N_CHIPS))[None]
    g_sinks = small_sum[7:8, 0:N_HEADS]

    def adam(name, w, g, m, v, rows):
        shape = w.shape
        two_d = (-1, shape[-1])
        out = _adamw(name, w.reshape(two_d), g.reshape(two_d), m.reshape(two_d), v.reshape(two_d), rows)
        return tuple(o.reshape(shape) for o in out)

    weights = (norm_in, w_in, conv_w, attn_sinks, norm_conv_out, norm_attn_out, w_out, norm_final)
    grads = (g_norm_in, None, g_conv_w, g_sinks, g_norm_conv, g_norm_attn, gwo_sh[None], g_norm_final)
    moments_m = (m_norm_in, m_w_in, m_conv_w, m_attn_sinks, m_norm_conv_out, m_norm_attn_out, m_w_out, m_norm_final)
    moments_v = (v_norm_in, v_w_in, v_conv_w, v_attn_sinks, v_norm_conv_out, v_norm_attn_out, v_w_out, v_norm_final)
    rows = (1, None, 3, 1, 1, 1, 128, 1)
    names = ("norm_in", "w_in", "conv_w", "attn_sinks", "norm_conv_out", "norm_attn_out", "w_out", "norm_final")
    updates = []
    for args in zip(names, weights, grads, moments_m, moments_v, rows):
        if args[0] == "w_in":
            out_t = _adamw("w_in", w_in_t, gwt_sh, m_w_in_t, v_w_in_t, 200)
            updates.append(tuple(o.T[None] for o in out_t))
        else:
            updates.append(adam(*args))
    grads_out, deltas, new_m, new_v = zip(*updates)
    return (loss, grad_x[None], *grads_out, *deltas, *new_m, *new_v)
```

```python
import jax
import jax.numpy as jnp
from jax import lax
from jax.experimental import pallas as pl
from jax.experimental.pallas import tpu as pltpu

F32 = jnp.float32
BF16 = jnp.bfloat16
MESH = pl.DeviceIdType.MESH

D_MODEL = 1024
D_CONV = 1024
D_ATTN = 1024
D_KV = 128
D_QG = 2 * D_ATTN + 2 * D_KV
D_MIX = D_CONV + D_ATTN
D_PC = 4 * D_CONV
D_IN_PROJ = D_PC + 2 * D_ATTN + 2 * D_KV
ROW_Q = D_PC
ROW_KV = ROW_Q + D_ATTN
ROW_GA = ROW_KV + 2 * D_KV
N_HEADS = 16
HEAD_DIM = 64
HEADS_PER_KV = 8
BLK = 128
N_CHIPS = 4
RMS_EPS = 1e-5
SCALE = HEAD_DIM ** -0.5
SLOPES = tuple(2.0 ** (-8.0 * (h + 1) / N_HEADS) for h in range(N_HEADS))

ADAM_LR, ADAM_B1, ADAM_B2, ADAM_EPS, ADAM_WD, ADAM_STEP = 0.001, 0.9, 0.999, 1e-08, 0.01, 10

TOK_TILE = 256
MIB = 1 << 20


def _params(vmem_mib, semantics=None):
    return pltpu.CompilerParams(dimension_semantics=semantics, vmem_limit_bytes=vmem_mib * MIB)


def _nn(a, b):
    return jnp.dot(a, b, preferred_element_type=F32)


def _nt(a, b):
    return lax.dot_general(a, b, (((1,), (1,)), ((), ())), preferred_element_type=F32)


def _tn(a, b):
    return lax.dot_general(a, b, (((0,), (0,)), ((), ())), preferred_element_type=F32)


def _rstd(v):
    return lax.rsqrt(jnp.mean(v * v, axis=-1, keepdims=True) + RMS_EPS)


def _rms_bwd(g, xhat, rstd):
    return rstd * (g - xhat * jnp.mean(g * xhat, axis=-1, keepdims=True))


def _silu_and_grad(g):
    s = jax.nn.sigmoid(g)
    return g * s, s * (1.0 + g * (1.0 - s))


def _resident(shape):
    return pl.BlockSpec(shape, lambda *_: (0,) * len(shape), pipeline_mode=pl.Buffered(1))


def _xor(a, b):
    return a + b - 2 * a * b


def _cast_rows(src_ref, dst_ref, rows=32):
    def step(i, carry):
        sl = pl.ds(pl.multiple_of(i * rows, rows), rows)
        dst_ref[sl, :] = src_ref[sl, :].astype(dst_ref.dtype)
        return carry
    lax.fori_loop(0, src_ref.shape[0] // rows, step, 0)


def _ag_weights(wt_sh, wo_sh, cw_sh):
    wt_rows, wo_rows = wt_sh.shape[0], wo_sh.shape[0]
    wt_half, wo_half = wt_rows // 2, wo_rows // 2
    width = wt_sh.shape[1]

    def body(wt_ref, wo_ref, cw_ref, wt_out, wo_out, cw_out, f32_t, f32_o, own_t, own_o, land_t, land_o,
             send_sems, recv_sems, local_sems):
        x, y, c = lax.axis_index("x"), lax.axis_index("y"), lax.axis_index("c")
        j = 2 * x + y
        p1 = (_xor(x, c), _xor(y, 1 - c), c)
        p2 = (_xor(x, 1 - c), _xor(y, c), c)
        sib = (x, y, 1 - c)
        j1 = 2 * p1[0] + p1[1]
        j2 = 2 * p2[0] + p2[1]
        j3 = 3 - j

        def wt_rows_of(chip, half):
            return wt_out.at[pl.ds(pl.multiple_of(chip * wt_rows + half * wt_half, 16), wt_half), :]

        def wo_rows_of(chip, half):
            return wo_out.at[pl.ds(pl.multiple_of(chip * wo_rows + half * wo_half, 16), wo_half), :]

        def rcopy(k, src, dst, to):
            return pltpu.make_async_remote_copy(src_ref=src, dst_ref=dst, send_sem=send_sems.at[k],
                                                recv_sem=recv_sems.at[k], device_id=to, device_id_type=MESH)

        def lcopy(k, src, dst):
            cp = pltpu.make_async_copy(src, dst, local_sems.at[k])
            cp.start()
            return cp

        loads = [lcopy(0, wt_ref, f32_t), lcopy(1, wo_ref, f32_o)]
        local = [lcopy(2, cw_ref, cw_out.at[j])]
        for cp in loads:
            cp.wait()
        _cast_rows(f32_t, own_t)
        _cast_rows(f32_o, own_o)
        local += [lcopy(0, own_t, wt_out.at[pl.ds(pl.multiple_of(j * wt_rows, 16), wt_rows), :]),
                  lcopy(1, own_o, wo_out.at[pl.ds(pl.multiple_of(j * wo_rows, 16), wo_rows), :])]
        half_t = own_t.at[pl.ds(pl.multiple_of(c * wt_half, 16), wt_half), :]
        half_o = own_o.at[pl.ds(pl.multiple_of(c * wo_half, 16), wo_half), :]

        def publish(slot, chip, k_t, k_o, k_local):
            sends = [rcopy(k_t, land_t.at[slot], wt_rows_of(chip, c), sib), rcopy(k_o, land_o.at[slot], wo_rows_of(chip, c), sib)]
            for cp in sends:
                cp.start()
            local.extend([lcopy(k_local, land_t.at[slot], wt_rows_of(chip, c)),
                          lcopy(k_local + 1, land_o.at[slot], wo_rows_of(chip, c))])
            return sends

        hop1 = [rcopy(0, half_t, land_t.at[0], p1), rcopy(1, half_o, land_o.at[0], p1), rcopy(2, cw_ref, cw_out.at[j], p1)]
        for cp in hop1:
            cp.start()
        rcopy(0, half_t, land_t.at[0], p1).wait_recv()
        rcopy(1, half_o, land_o.at[0], p1).wait_recv()
        rcopy(2, cw_ref, cw_out.at[j1], p1).wait_recv()
        hop2 = [rcopy(3, half_t, land_t.at[1], p2), rcopy(5, half_o, land_o.at[1], p2), rcopy(7, cw_ref, cw_out.at[j], p2),
                rcopy(4, land_t.at[0], land_t.at[2], p2), rcopy(6, land_o.at[0], land_o.at[2], p2),
                rcopy(8, cw_out.at[j1], cw_out.at[j1], p2)]
        for cp in hop2:
            cp.start()
        swaps = publish(0, j1, 9, 12, 3)
        rcopy(3, half_t, land_t.at[1], p2).wait_recv()
        rcopy(5, half_o, land_o.at[1], p2).wait_recv()
        rcopy(7, cw_ref, cw_out.at[j2], p2).wait_recv()
        swaps += publish(1, j2, 10, 13, 5)
        rcopy(4, half_t, land_t.at[2], p2).wait_recv()
        rcopy(6, half_o, land_o.at[2], p2).wait_recv()
        rcopy(8, cw_ref, cw_out.at[j3], p2).wait_recv()
        swaps += publish(2, j3, 11, 14, 7)
        for k, chip in ((9, j2), (10, j1), (11, j3)):
            rcopy(k, half_t, wt_rows_of(chip, 1 - c), sib).wait_recv()
        for k, chip in ((12, j2), (13, j1), (14, j3)):
            rcopy(k, half_o, wo_rows_of(chip, 1 - c), sib).wait_recv()
        for cp in hop1 + hop2 + swaps:
            cp.wait_send()
        for cp in local:
            cp.wait()

    any_spec = pl.BlockSpec(memory_space=pl.ANY)
    return pl.pallas_call(
        body, name="ag_weights",
        out_shape=(jax.ShapeDtypeStruct((N_CHIPS * wt_rows, width), BF16),
                   jax.ShapeDtypeStruct((N_CHIPS * wo_rows, width), BF16),
                   jax.ShapeDtypeStruct((N_CHIPS,) + cw_sh.shape, cw_sh.dtype)),
        in_specs=[any_spec, any_spec, any_spec],
        out_specs=(any_spec, any_spec, any_spec),
        scratch_shapes=[pltpu.VMEM((wt_rows, width), F32), pltpu.VMEM((wo_rows, width), F32),
                        pltpu.VMEM((wt_rows, width), BF16), pltpu.VMEM((wo_rows, width), BF16),
                        pltpu.VMEM((3, wt_half, width), BF16), pltpu.VMEM((3, wo_half, width), BF16),
                        pltpu.SemaphoreType.DMA((15,)), pltpu.SemaphoreType.DMA((15,)), pltpu.SemaphoreType.DMA((9,))],
        compiler_params=_params(32),
    )(wt_sh, wo_sh, cw_sh)


def _fwd_in(x, norm_in, wt):
    seq = x.shape[0]
    tile = TOK_TILE

    def body(x_ref, g_ref, wt_ref, h_ref, pc_ref, q_ref, kv_ref, ga_ref):
        xv = x_ref[...]
        h = (xv * _rstd(xv) * g_ref[...]).astype(BF16)
        h_ref[...] = h
        for blk in range(D_PC // D_CONV):
            pc_ref[:, blk * D_CONV:(blk + 1) * D_CONV] = _nt(h, wt_ref[blk * D_CONV:(blk + 1) * D_CONV, :])
        q_ref[...] = _nt(h, wt_ref[ROW_Q:ROW_KV, :])
        kv_ref[...] = _nt(h, wt_ref[ROW_KV:ROW_GA, :])
        ga_ref[...] = _nt(h, wt_ref[ROW_GA:D_IN_PROJ, :])

    def row(width):
        return pl.BlockSpec((tile, width), lambda i: (i, 0))

    return pl.pallas_call(
        body, name="fwd_in", grid=(seq // tile,),
        out_shape=(jax.ShapeDtypeStruct((seq, D_MODEL), BF16), jax.ShapeDtypeStruct((seq, D_PC), F32),
                   jax.ShapeDtypeStruct((seq, D_ATTN), F32), jax.ShapeDtypeStruct((seq, 2 * D_KV), F32),
                   jax.ShapeDtypeStruct((seq, D_ATTN), F32)),
        in_specs=[row(D_MODEL), _resident((1, D_MODEL)), _resident(wt.shape)],
        out_specs=(row(D_MODEL), row(D_PC), row(D_ATTN), row(2 * D_KV), row(D_ATTN)),
        compiler_params=_params(48, ("arbitrary",)),
    )(x, norm_in, wt)


def _conv_core(pc_ref, zbuf, cw_ref):
    tile = pc_ref.shape[0]
    cb = pc_ref[:, 0:D_CONV]
    cc = pc_ref[:, D_CONV:2 * D_CONV]
    cu = pc_ref[:, 2 * D_CONV:3 * D_CONV]
    z = cc * cu
    zbuf[8:tile + 8, :] = z
    z1 = zbuf[7:tile + 7, :]
    z2 = zbuf[6:tile + 6, :]
    conv = cw_ref[0:1, :] * z2 + cw_ref[1:2, :] * z1 + cw_ref[2:3, :] * z
    return cb, cc, cu, z, z1, z2, conv


def _conv_fwd(pc, conv_w, norm_conv_out):
    seq = pc.shape[0]
    tile = TOK_TILE

    def body(pc_ref, cw_ref, gn_ref, oc_ref, zbuf):
        @pl.when(pl.program_id(0) == 0)
        def _():
            zbuf[0:8, :] = jnp.zeros((8, D_CONV), F32)

        cb, _, _, _, _, _, conv = _conv_core(pc_ref, zbuf, cw_ref)
        yc = cb * conv
        silu, _ = _silu_and_grad(pc_ref[:, 3 * D_CONV:4 * D_CONV])
        oc_ref[...] = (yc * _rstd(yc) * gn_ref[...] * silu).astype(BF16)
        zbuf[0:8, :] = zbuf[tile:tile + 8, :]

    return pl.pallas_call(
        body, name="conv_fwd", grid=(seq // tile,),
        out_shape=jax.ShapeDtypeStruct((seq, D_CONV), BF16),
        in_specs=[pl.BlockSpec((tile, D_PC), lambda i: (i, 0)), _resident(conv_w.shape), _resident((1, D_CONV))],
        out_specs=pl.BlockSpec((tile, D_CONV), lambda i: (i, 0)),
        scratch_shapes=[pltpu.VMEM((tile + 8, D_CONV), F32)],
        compiler_params=_params(40, ("arbitrary",)),
    )(pc, conv_w, norm_conv_out)


def _band_geometry(block_index):
    qi = lax.broadcasted_iota(jnp.int32, (BLK, BLK), 0)
    kp = lax.broadcasted_iota(jnp.int32, (BLK, BLK), 1)
    use_cur = kp <= qi
    dist = jnp.where(use_cur, qi - kp, qi - kp + BLK).astype(F32)
    valid = use_cur | (block_index > 0)
    return use_cur, dist, valid


def _block_diag(cur, prev, group):
    lane = lax.broadcasted_iota(jnp.int32, cur.shape, 1)

    def halves(t):
        other = pltpu.roll(t, 64, 1)
        lo, hi = (t, other) if group == 0 else (other, t)
        return jnp.where(lane < 64, lo, 0.0), jnp.where(lane >= 64, hi, 0.0)

    return jnp.concatenate(halves(cur) + halves(prev), axis=0).astype(BF16)


def _merge(s4, use_cur):
    return (jnp.where(use_cur, s4[:, 0:BLK], s4[:, 2 * BLK:3 * BLK]),
            jnp.where(use_cur, s4[:, BLK:2 * BLK], s4[:, 3 * BLK:4 * BLK]))


def _split(a, b, use_cur):
    return jnp.concatenate([jnp.where(use_cur, a, 0.0), jnp.where(use_cur, b, 0.0),
                            jnp.where(use_cur, 0.0, a), jnp.where(use_cur, 0.0, b)], axis=1)


def _softmax_head(s, head, sink, dist, valid):
    sc = jnp.where(valid, s - SLOPES[head] * dist, -jnp.inf)
    m = jnp.maximum(jnp.max(sc, axis=-1, keepdims=True), sink)
    p = jnp.exp(sc - m)
    es = jnp.exp(sink - m)
    inv = 1.0 / (jnp.sum(p, axis=-1, keepdims=True) + es)
    return p * inv, es * inv


def _attn_probs(q_ref, kvc_ref, kvp_ref, sink_ref, geometry):
    use_cur, dist, valid = geometry
    groups = range(N_HEADS // HEADS_PER_KV)
    kbd = [_block_diag(kvc_ref[:, 0:D_KV], kvp_ref[:, 0:D_KV], g) for g in groups]
    vbd = [_block_diag(kvc_ref[:, D_KV:2 * D_KV], kvp_ref[:, D_KV:2 * D_KV], g) for g in groups]
    qps = [(q_ref[:, j * 128:(j + 1) * 128] * SCALE).astype(BF16) for j in range(N_HEADS // 2)]
    scores = []
    for j, qp in enumerate(qps):
        scores += _merge(_nt(qp, kbd[j // 4]), use_cur)
    sinks = [sink_ref[0, h] for h in range(N_HEADS)]
    scores = [jnp.where(valid, s - SLOPES[h] * dist, -jnp.inf) for h, s in enumerate(scores)]
    maxes = [jnp.maximum(jnp.max(s, axis=-1, keepdims=True), sinks[h]) for h, s in enumerate(scores)]
    exps = [jnp.exp(s - m) for s, m in zip(scores, maxes)]
    sink_exps = [jnp.exp(sinks[h] - m) for h, m in enumerate(maxes)]
    invs = [1.0 / (jnp.sum(e, axis=-1, keepdims=True) + se) for e, se in zip(exps, sink_exps)]
    probs = [e * inv for e, inv in zip(exps, invs)]
    sink_probs = [se * inv for se, inv in zip(sink_exps, invs)]
    return qps, kbd, vbd, probs, sink_probs


def _kv_specs(n_blocks, reverse):
    def blk(i):
        return (n_blocks - 1 - i) if reverse else i
    cur = pl.BlockSpec((BLK, 2 * D_KV), lambda i: (blk(i), 0))
    prev = pl.BlockSpec((BLK, 2 * D_KV), lambda i: (jnp.maximum(blk(i) - 1, 0), 0))
    return cur, prev


def _attn_fwd(q, kv, ga, sinks, norm_attn_out):
    seq = q.shape[0]
    n_blocks = seq // BLK

    def body(q_ref, kvc_ref, kvp_ref, ga_ref, sink_ref, gn_ref, ya_ref, oa_ref):
        geometry = _band_geometry(pl.program_id(0))
        _, _, vbd, probs, _ = _attn_probs(q_ref, kvc_ref, kvp_ref, sink_ref, geometry)
        p4s = [_split(probs[2 * j], probs[2 * j + 1], geometry[0]).astype(BF16) for j in range(N_HEADS // 2)]
        ya = jnp.concatenate([_nn(p4, vbd[j // 4]) for j, p4 in enumerate(p4s)], axis=1)
        ya_ref[...] = ya
        silu, _ = _silu_and_grad(ga_ref[...])
        oa_ref[...] = (ya * _rstd(ya) * gn_ref[...] * silu).astype(BF16)

    row = pl.BlockSpec((BLK, D_ATTN), lambda i: (i, 0))
    kv_cur, kv_prev = _kv_specs(n_blocks, reverse=False)
    return pl.pallas_call(
        body, name="attn_fwd", grid=(n_blocks,),
        out_shape=(jax.ShapeDtypeStruct((seq, D_ATTN), F32), jax.ShapeDtypeStruct((seq, D_ATTN), BF16)),
        in_specs=[row, kv_cur, kv_prev, row, pl.BlockSpec(memory_space=pltpu.SMEM), _resident((1, D_ATTN))],
        out_specs=(row, row),
        compiler_params=_params(32, ("arbitrary",)),
    )(q, kv, kv, ga, sinks, norm_attn_out)


def _out_proj_loss(x, oc, oa, wo, norm_final, target):
    seq = x.shape[0]
    tile = TOK_TILE

    def body(x_ref, oc_ref, oa_ref, wo_ref, gf_ref, t_ref, dx2_ref, doc_ref, doa_ref, gwo_ref, gnf_ref, loss_ref):
        @pl.when(pl.program_id(0) == 0)
        def _():
            gwo_ref[...] = jnp.zeros_like(gwo_ref)
            gnf_ref[...] = jnp.zeros_like(gnf_ref)
            loss_ref[...] = jnp.zeros_like(loss_ref)

        oc, oa = oc_ref[...], oa_ref[...]
        x2 = x_ref[...] + _nn(oc, wo_ref[0:D_CONV, :]) + _nn(oa, wo_ref[D_CONV:D_MIX, :])
        r = _rstd(x2)
        xhat = x2 * r
        err = xhat * gf_ref[...] - t_ref[...]
        loss_ref[...] += jnp.sum(err * err, axis=0, keepdims=True) * (0.5 / D_MODEL)
        dy = err * (1.0 / D_MODEL)
        gnf_ref[...] += jnp.sum(dy * xhat, axis=0, keepdims=True)
        dx2 = _rms_bwd(dy * gf_ref[...], xhat, r)
        dx2_ref[...] = dx2
        db = dx2.astype(BF16)
        doc_ref[...] = _nt(db, wo_ref[0:D_CONV, :])
        doa_ref[...] = _nt(db, wo_ref[D_CONV:D_MIX, :])
        gwo_ref[0:D_CONV, :] += _tn(oc, db)
        gwo_ref[D_CONV:D_MIX, :] += _tn(oa, db)

    row = pl.BlockSpec((tile, D_MODEL), lambda i: (i, 0))
    vec = pl.BlockSpec((1, D_MODEL), lambda i: (0, 0))
    return pl.pallas_call(
        body, name="out_proj_loss", grid=(seq // tile,),
        out_shape=(jax.ShapeDtypeStruct((seq, D_MODEL), F32), jax.ShapeDtypeStruct((seq, D_CONV), F32),
                   jax.ShapeDtypeStruct((seq, D_ATTN), F32), jax.ShapeDtypeStruct((D_MIX, D_MODEL), F32),
                   jax.ShapeDtypeStruct((1, D_MODEL), F32), jax.ShapeDtypeStruct((1, D_MODEL), F32)),
        in_specs=[row, row, row, _resident(wo.shape), _resident((1, D_MODEL)), row],
        out_specs=(row, row, row, pl.BlockSpec((D_MIX, D_MODEL), lambda i: (0, 0)), vec, vec),
        compiler_params=_params(48, ("arbitrary",)),
    )(x, oc, oa, wo, norm_final, target)


def _conv_bwd(pc, doc, conv_w, norm_conv_out):
    seq = pc.shape[0]
    tile = TOK_TILE
    n_tiles = seq // tile

    def body(pc_ref, hcc_ref, hcu_ref, doc_ref, cw_ref, gn_ref, dpc_ref, gnc_ref, gcw_ref, zbuf, dbuf):
        step = pl.program_id(0)

        @pl.when(step == 0)
        def _():
            gnc_ref[...] = jnp.zeros_like(gnc_ref)
            gcw_ref[...] = jnp.zeros_like(gcw_ref)
            dbuf[tile:tile + 8, :] = jnp.zeros((8, D_CONV), F32)

        is_first_tile = step == n_tiles - 1
        zbuf[0:8, :] = jnp.where(is_first_tile, 0.0, hcc_ref[...] * hcu_ref[...])
        cb, cc, cu, z, z1, z2, conv = _conv_core(pc_ref, zbuf, cw_ref)
        gc = pc_ref[:, 3 * D_CONV:4 * D_CONV]
        silu, dsilu = _silu_and_grad(gc)
        yc = cb * conv
        r = _rstd(yc)
        xhat = yc * r
        do = doc_ref[...]
        dn = do * silu
        dpc_ref[:, 3 * D_CONV:4 * D_CONV] = (do * (xhat * gn_ref[...]) * dsilu).astype(BF16)
        gnc_ref[...] += jnp.sum(dn * xhat, axis=0, keepdims=True)
        dyc = _rms_bwd(dn * gn_ref[...], xhat, r)
        dpc_ref[:, 0:D_CONV] = (dyc * conv).astype(BF16)
        dconv = dyc * cb
        gcw_ref[0:1, :] += jnp.sum(dconv * z2, axis=0, keepdims=True)
        gcw_ref[1:2, :] += jnp.sum(dconv * z1, axis=0, keepdims=True)
        gcw_ref[2:3, :] += jnp.sum(dconv * z, axis=0, keepdims=True)
        dbuf[0:tile, :] = dconv
        dz = cw_ref[2:3, :] * dconv + cw_ref[1:2, :] * dbuf[1:tile + 1, :] + cw_ref[0:1, :] * dbuf[2:tile + 2, :]
        dpc_ref[:, D_CONV:2 * D_CONV] = (dz * cu).astype(BF16)
        dpc_ref[:, 2 * D_CONV:3 * D_CONV] = (dz * cc).astype(BF16)
        dbuf[tile:tile + 8, :] = dbuf[0:8, :]

    def rev(i):
        return n_tiles - 1 - i

    def halo(col_block):
        return pl.BlockSpec((8, D_CONV), lambda i: (jnp.maximum(rev(i) * (tile // 8) - 1, 0), col_block))

    return pl.pallas_call(
        body, name="conv_bwd", grid=(n_tiles,),
        out_shape=(jax.ShapeDtypeStruct((seq, D_PC), BF16), jax.ShapeDtypeStruct((1, D_CONV), F32),
                   jax.ShapeDtypeStruct((8, D_CONV), F32)),
        in_specs=[pl.BlockSpec((tile, D_PC), lambda i: (rev(i), 0)), halo(1), halo(2),
                  pl.BlockSpec((tile, D_CONV), lambda i: (rev(i), 0)), _resident(conv_w.shape), _resident((1, D_CONV))],
        out_specs=(pl.BlockSpec((tile, D_PC), lambda i: (rev(i), 0)), pl.BlockSpec((1, D_CONV), lambda i: (0, 0)),
                   pl.BlockSpec((8, D_CONV), lambda i: (0, 0))),
        scratch_shapes=[pltpu.VMEM((tile + 8, D_CONV), F32), pltpu.VMEM((tile + 8, D_CONV), F32)],
        compiler_params=_params(48, ("arbitrary",)),
    )(pc, pc, pc, doc, conv_w, norm_conv_out)


def _attn_bwd(q, kv, ga, ya, doa, sinks, norm_attn_out):
    seq = q.shape[0]
    n_blocks = seq // BLK

    def body(q_ref, kvc_ref, kvp_ref, ga_ref, ya_ref, doa_ref, sink_ref, gn_ref,
             dqg_ref, gna_ref, gs_ref, carry, dya_buf):
        step = pl.program_id(0)

        @pl.when(step == 0)
        def _():
            gna_ref[...] = jnp.zeros_like(gna_ref)
            gs_ref[...] = jnp.zeros_like(gs_ref)
            carry[...] = jnp.zeros_like(carry)

        ya = ya_ref[...]
        r = _rstd(ya)
        xhat = ya * r
        silu, dsilu = _silu_and_grad(ga_ref[...])
        do = doa_ref[...]
        dn = do * silu
        dqg_ref[:, D_ATTN:2 * D_ATTN] = (do * (xhat * gn_ref[...]) * dsilu).astype(BF16)
        gna_ref[...] += jnp.sum(dn * xhat, axis=0, keepdims=True)
        dya_buf[...] = _rms_bwd(dn * gn_ref[...], xhat, r).astype(BF16)

        geometry = _band_geometry(n_blocks - 1 - step)
        use_cur = geometry[0]
        lane = lax.broadcasted_iota(jnp.int32, (BLK, 128), 1)
        sink_lane = lax.broadcasted_iota(jnp.int32, (1, 128), 1)
        gsink = jnp.zeros((1, 128), F32)

        def fold(bd):
            return (jnp.where(lane < 64, bd[0:BLK], 0.0) + jnp.where(lane >= 64, bd[BLK:2 * BLK], 0.0),
                    jnp.where(lane < 64, bd[2 * BLK:3 * BLK], 0.0) + jnp.where(lane >= 64, bd[3 * BLK:4 * BLK], 0.0))

        pairs = range(N_HEADS // 2)
        qps, kbd, vbd, probs, sink_probs = _attn_probs(q_ref, kvc_ref, kvp_ref, sink_ref, geometry)
        dyps = [dya_buf[:, j * 128:(j + 1) * 128] for j in pairs]
        dps = []
        for j in pairs:
            dps += _merge(_nt(dyps[j], vbd[j // 4]), use_cur)
        deltas = [jnp.sum(p * dp, axis=-1, keepdims=True) for p, dp in zip(probs, dps)]
        dss = [p * (dp - delta) for p, dp, delta in zip(probs, dps, deltas)]
        for h in range(N_HEADS):
            dsink = -jnp.sum(sink_probs[h] * deltas[h], axis=0, keepdims=True)
            gsink = gsink + jnp.where(sink_lane == h, dsink, 0.0)
        gs_ref[...] += gsink
        ds4s = [_split(dss[2 * j], dss[2 * j + 1], use_cur).astype(BF16) for j in pairs]
        p4s = [_split(probs[2 * j], probs[2 * j + 1], use_cur).astype(BF16) for j in pairs]
        dqg_ref[:, 0:D_ATTN] = jnp.concatenate([_nn(ds4s[j], kbd[j // 4]) * SCALE for j in pairs], axis=1).astype(BF16)
        sums = []
        for group in range(N_HEADS // HEADS_PER_KV):
            acc = [jnp.zeros((BLK, 128), F32) for _ in range(4)]
            for j in range(group * 4, group * 4 + 4):
                for slot, part in enumerate(fold(_tn(ds4s[j], qps[j])) + fold(_tn(p4s[j], dyps[j]))):
                    acc[slot] = acc[slot] + part
            sums.append([a + pltpu.roll(a, 64, 1) for a in acc])
        dk_cur, dk_prev, dv_cur, dv_prev = (jnp.where(lane < 64, a, b) for a, b in zip(sums[0], sums[1]))
        dqg_ref[:, 2 * D_ATTN:2 * D_ATTN + 2 * D_KV] = (jnp.concatenate([dk_cur, dv_cur], axis=1) + carry[...]).astype(BF16)
        carry[...] = jnp.concatenate([dk_prev, dv_prev], axis=1)

    row = pl.BlockSpec((BLK, D_ATTN), lambda i: (n_blocks - 1 - i, 0))
    kv_cur, kv_prev = _kv_specs(n_blocks, reverse=True)
    return pl.pallas_call(
        body, name="attn_bwd", grid=(n_blocks,),
        out_shape=(jax.ShapeDtypeStruct((seq, D_QG), BF16), jax.ShapeDtypeStruct((1, D_ATTN), F32),
                   jax.ShapeDtypeStruct((1, 128), F32)),
        in_specs=[row, kv_cur, kv_prev, row, row, row, pl.BlockSpec(memory_space=pltpu.SMEM), _resident((1, D_ATTN))],
        out_specs=(pl.BlockSpec((BLK, D_QG), lambda i: (n_blocks - 1 - i, 0)),
                   pl.BlockSpec((1, D_ATTN), lambda i: (0, 0)), pl.BlockSpec((1, 128), lambda i: (0, 0))),
        scratch_shapes=[pltpu.VMEM((BLK, 2 * D_KV), F32), pltpu.VMEM((BLK, D_ATTN), BF16)],
        compiler_params=_params(32, ("arbitrary",)),
    )(q, kv, kv, ga, ya, doa, sinks, norm_attn_out)


GBLK = 256
GSUB = 64
PAIR_RING = 4
LAG_PAIR, LAG_HOP1, LAG_HOP2 = 1, 7, 14


def _bwd_in(dpc, dqg, h, wt, x, norm_in, dx2, small):
    seq = x.shape[0]
    n_blk = D_IN_PROJ // GBLK
    per_chip = n_blk // N_CHIPS
    n_slots = (n_blk + 1) // 2
    n_sub = GBLK // GSUB
    chip_rows = D_IN_PROJ // N_CHIPS
    tile = TOK_TILE
    n_tiles = seq // tile
    n_steps = n_blk + max(n_tiles, LAG_HOP2)
    chunk = min(seq, 512)
    blk_q, blk_kv, blk_ga = ROW_Q // GBLK, ROW_KV // GBLK, ROW_GA // GBLK

    def block_of(i):
        k = i % N_CHIPS
        robin = per_chip * ((k % 2) * 2 + k // 2) + i // N_CHIPS
        if isinstance(i, int):
            return robin if i < per_chip * N_CHIPS else i
        return jnp.where(i < per_chip * N_CHIPS, robin, i)

    def owner_of(i):
        return (i // N_CHIPS) % 2

    def slot_of(i):
        return (i // (2 * N_CHIPS)) * N_CHIPS + i % N_CHIPS

    def body(dpc_ref, dqg_ref, wt_ref, h_ref, x_ref, g_ref, dx2_ref, small_ref, gx_ref, small_sum, gwt_sh,
             dh_acc, gni, keep, pbuf, xbuf, land, land2, small_land,
             pair_send, pair_recv, h1_send, h1_recv, h2_send, h2_recv, sw_send, sw_recv, sm_send, sm_recv, out_sem):
        step = pl.program_id(0)
        x_i, y_i, c = lax.axis_index("x"), lax.axis_index("y"), lax.axis_index("c")
        me = 4 * x_i + 2 * y_i + c
        j = 2 * x_i + y_i
        pa = (_xor(x_i, 1 - c), _xor(y_i, c), c)
        pb = (_xor(x_i, c), _xor(y_i, 1 - c), c)
        sib = (x_i, y_i, 1 - c)
        ja = 2 * pa[0] + pa[1]
        jb = 2 * pb[0] + pb[1]
        jd = 3 - j

        def remote(src, dst, send, recv, to):
            return pltpu.make_async_remote_copy(src_ref=src, dst_ref=dst, send_sem=send, recv_sem=recv,
                                                device_id=to, device_id_type=MESH)

        def piece(ref, slot, u, n):
            return ref.at[slot, pl.ds(u * GSUB, n * GSUB), :]

        def chip_rows_at(ref, local, n):
            return ref.at[pl.ds(pl.multiple_of(local, GSUB), n * GSUB), :]

        def pair_copy(i):
            slot = slot_of(i)
            return remote(pbuf.at[i % PAIR_RING], land.at[slot], pair_send.at[slot], pair_recv.at[slot], sib)

        def h1_copy(slot, u, n):
            k = slot * n_sub + u
            return remote(piece(xbuf, slot, u, n), piece(xbuf, slot, u, n), h1_send.at[k], h1_recv.at[k], pa)

        def h2_copy(slot, u, n, local):
            k = slot * n_sub + u
            return remote(piece(xbuf, slot, u, n), chip_rows_at(land2, local, n), h2_send.at[k], h2_recv.at[k], pb)

        def sw_copy(slot, u, n, local):
            k = slot * n_sub + u
            return remote(piece(keep, slot, u, n), chip_rows_at(gwt_sh, local, n), sw_send.at[k], sw_recv.at[k], sib)

        def owned(i):
            return (i >= 0) & (i < n_blk) & (owner_of(i) == c)

        def chip_of(blk, u):
            row = blk * GBLK + u * GSUB
            chip = row // chip_rows
            return chip, row - chip * chip_rows

        def pieces(blk):
            first, local = chip_of(blk, 0)
            whole = first == chip_of(blk, n_sub - 1)[0]
            if isinstance(blk, int):
                return [(True, 0, n_sub, first, local)] if whole else [(True, u, 1) + chip_of(blk, u) for u in range(n_sub)]
            return [(whole, 0, n_sub, first, local)] + [(jnp.logical_not(whole), u, 1) + chip_of(blk, u) for u in range(n_sub)]

        @pl.when(step == 0)
        def _():
            dh_acc[...] = jnp.zeros_like(dh_acc)
            gni[...] = jnp.zeros_like(gni)

        @pl.when(step < n_blk)
        def _():
            from_pc = block_of(step) < blk_q
            block = _tn(jnp.where(from_pc, dpc_ref[...], dqg_ref[...]), h_ref[...])
            for t in range(0, seq, chunk):
                d = jnp.where(from_pc, dpc_ref[t:t + chunk, :], dqg_ref[t:t + chunk, :])
                dh_acc[t:t + chunk, :] += _nn(d, wt_ref[...])

            @pl.when(owner_of(step) == c)
            def _():
                keep[slot_of(step)] = block

            @pl.when(owner_of(step) != c)
            def _():
                @pl.when(step >= 2 * PAIR_RING)
                def _():
                    pair_copy(step - 2 * PAIR_RING).wait_send()
                pbuf[step % PAIR_RING] = block.astype(BF16)
                pair_copy(step).start()

        i1 = step - LAG_PAIR

        @pl.when(owned(i1))
        def _():
            slot = slot_of(i1)
            pair_copy(i1).wait_recv()
            _accumulate(keep.at[slot], land.at[slot])
            for cond, u, n, chip, _ in pieces(block_of(i1)):
                @pl.when(cond & ((chip == ja) | (chip == jd)))
                def _(u=u, n=n):
                    _cast_rows(piece(keep, slot, u, n), piece(xbuf, slot, u, n))
                    h1_copy(slot, u, n).start()

        i2 = step - LAG_HOP1

        @pl.when(owned(i2))
        def _():
            slot = slot_of(i2)
            for cond, u, n, chip, local in pieces(block_of(i2)):
                @pl.when(cond & ((chip == j) | (chip == jb)))
                def _(u=u, n=n, chip=chip, local=local):
                    h1_copy(slot, u, n).wait_recv()
                    _accumulate(piece(keep, slot, u, n), piece(xbuf, slot, u, n))

                    @pl.when(chip == jb)
                    def _():
                        _cast_rows(piece(keep, slot, u, n), piece(xbuf, slot, u, n))
                        h2_copy(slot, u, n, local).start()

                @pl.when(cond & ((chip == ja) | (chip == jd)))
                def _(u=u, n=n):
                    h1_copy(slot, u, n).wait_send()

        i3 = step - LAG_HOP2

        @pl.when(owned(i3))
        def _():
            slot = slot_of(i3)
            for cond, u, n, chip, local in pieces(block_of(i3)):
                @pl.when(cond & (chip == j))
                def _(u=u, n=n, local=local):
                    h2_copy(slot, u, n, local).wait_recv()
                    _accumulate(piece(keep, slot, u, n), chip_rows_at(land2, local, n))
                    mine = pltpu.make_async_copy(piece(keep, slot, u, n), chip_rows_at(gwt_sh, local, n), out_sem.at[0])
                    mine.start()
                    sw_copy(slot, u, n, local).start()
                    mine.wait()

                @pl.when(cond & (chip == jb))
                def _(u=u, n=n, local=local):
                    h2_copy(slot, u, n, local).wait_send()

        e = step - n_blk

        @pl.when((e >= 0) & (e < n_tiles))
        def _():
            dh = dh_acc[pl.ds(pl.multiple_of(e * tile, tile), tile), :]
            xv = x_ref[...]
            r = _rstd(xv)
            xhat = xv * r
            gni[...] += jnp.sum(dh * xhat, axis=0, keepdims=True)
            gx_ref[...] = _rms_bwd(dh * g_ref[...], xhat, r) + dx2_ref[...]

        @pl.when(step == n_steps - 1)
        def _():
            small_land[me] = small_ref[...]
            small_land[me, 0:1, :] = gni[...]
            others = [(dx, dy, dc) for dx in (0, 1) for dy in (0, 1) for dc in (0, 1)][1:]
            sends = [remote(small_land.at[me], small_land.at[me], sm_send.at[k], sm_recv.at[k],
                            (_xor(x_i, dx), _xor(y_i, dy), _xor(c, dc))) for k, (dx, dy, dc) in enumerate(others)]
            for cp in sends:
                cp.start()
            for i in range(n_blk):
                if i + 2 * PAIR_RING >= n_blk:
                    @pl.when(owner_of(i) != c)
                    def _(i=i):
                        pair_copy(i).wait_send()
                for _, u, n, chip, local in pieces(block_of(i)):
                    @pl.when((j == chip) & (c == owner_of(i)))
                    def _(i=i, u=u, n=n, local=local):
                        sw_copy(slot_of(i), u, n, local).wait_send()

                    @pl.when((j == chip) & (c != owner_of(i)))
                    def _(i=i, u=u, n=n, local=local):
                        sw_copy(slot_of(i), u, n, local).wait_recv()
            for cp in sends:
                cp.wait_recv()
            total = small_land[0]
            for dev in range(1, 8):
                total = total + small_land[dev]
            small_sum[...] = total
            for cp in sends:
                cp.wait_send()

    def blk_at(i):
        return block_of(jnp.clip(i, 0, n_blk - 1))

    last_pc_step = max(i for i in range(n_blk) if block_of(i) < blk_q)

    def next_block(i, in_pc):
        i = jnp.clip(i, 0, n_blk - 1)
        step = jnp.full_like(i, last_pc_step if in_pc else n_blk - 1)
        for ahead in reversed(range(N_CHIPS)):
            cand = jnp.minimum(i + ahead, n_blk - 1)
            step = jnp.where((block_of(cand) < blk_q) == in_pc, cand, step)
        return block_of(step)

    def dqg_block(i):
        b = next_block(i, False)
        q_blk = jnp.clip(b - blk_q, 0, blk_kv - blk_q - 1)
        ga_blk = (D_ATTN // GBLK) + jnp.clip(b - blk_ga, 0, n_blk - blk_ga - 1)
        return jnp.where(b < blk_kv, q_blk, jnp.where(b == blk_kv, 2 * D_ATTN // GBLK, ga_blk))

    def tok(i):
        return (jnp.clip(i - n_blk, 0, n_tiles - 1), 0)

    n_piece = n_slots * n_sub
    dma = pltpu.SemaphoreType.DMA
    return pl.pallas_call(
        body, name="bwd_in", grid=(n_steps,),
        out_shape=(jax.ShapeDtypeStruct((seq, D_MODEL), F32), jax.ShapeDtypeStruct(small.shape, F32),
                   jax.ShapeDtypeStruct((chip_rows, D_MODEL), F32)),
        in_specs=[pl.BlockSpec((seq, GBLK), lambda i: (0, next_block(i, True))),
                  pl.BlockSpec((seq, GBLK), lambda i: (0, dqg_block(i))),
                  pl.BlockSpec((GBLK, D_MODEL), lambda i: (blk_at(i), 0)),
                  _resident(h.shape),
                  pl.BlockSpec((tile, D_MODEL), tok), _resident((1, D_MODEL)), pl.BlockSpec((tile, D_MODEL), tok),
                  _resident(small.shape)],
        out_specs=(pl.BlockSpec((tile, D_MODEL), tok), pl.BlockSpec(small.shape, lambda i: (0, 0)),
                   pl.BlockSpec(memory_space=pl.ANY)),
        scratch_shapes=[pltpu.VMEM((seq, D_MODEL), F32), pltpu.VMEM((1, D_MODEL), F32),
                        pltpu.VMEM((n_slots, GBLK, D_MODEL), F32), pltpu.VMEM((PAIR_RING, GBLK, D_MODEL), BF16),
                        pltpu.VMEM((n_slots, GBLK, D_MODEL), BF16), pltpu.VMEM((n_slots, GBLK, D_MODEL), BF16),
                        pltpu.VMEM((chip_rows, D_MODEL), BF16), pltpu.VMEM((8,) + small.shape, F32),
                        dma((n_slots,)), dma((n_slots,)), dma((n_piece,)), dma((n_piece,)), dma((n_piece,)),
                        dma((n_piece,)), dma((n_piece,)), dma((n_piece,)), dma((7,)), dma((7,)), dma((1,))],
        compiler_params=_params(62, ("arbitrary",)),
    )(dpc, dqg, wt, h, x, norm_in, dx2, small)


def _accumulate(dst_ref, src_ref, rows=16):
    def step(i, carry):
        sl = pl.ds(pl.multiple_of(i * rows, rows), rows)
        dst_ref[sl, :] = dst_ref[sl, :] + src_ref[sl, :].astype(F32)
        return carry
    lax.fori_loop(0, dst_ref.shape[0] // rows, step, 0)


def _rs_wout(gwo):
    o_rows = gwo.shape[0] // N_CHIPS
    o_half = o_rows // 2
    width = gwo.shape[1]

    def body(gwo_ref, gwo_sh, acc_o, stage_o, sb_o, r1o, r2o, r3o, send_sems, recv_sems, local_sems):
        x, y, c = lax.axis_index("x"), lax.axis_index("y"), lax.axis_index("c")
        j = 2 * x + y
        pa = (_xor(x, 1 - c), _xor(y, c), c)
        pb = (_xor(x, c), _xor(y, 1 - c), c)
        sib = (x, y, 1 - c)
        ja = 2 * pa[0] + pa[1]
        jb = 2 * pb[0] + pb[1]
        jd = 3 - j

        def rcopy(k, src, dst, to):
            return pltpu.make_async_remote_copy(src_ref=src, dst_ref=dst, send_sem=send_sems.at[k],
                                                recv_sem=recv_sems.at[k], device_id=to, device_id_type=MESH)

        def o_rows_of(chip, half):
            return gwo_ref.at[pl.ds(pl.multiple_of(chip * o_rows + half * o_half, 8), o_half), :]

        def load(src, dst):
            cp = pltpu.make_async_copy(src, dst, local_sems.at[0])
            cp.start()
            return cp

        order = (ja, jd, jb, j)
        sib_order = (jb, jd, ja, j)
        pair_sends = []
        for s, chip in enumerate(sib_order):
            load(o_rows_of(chip, 1 - c), stage_o).wait()
            _cast_rows(stage_o, sb_o.at[s])
            pair_sends.append(rcopy(s, sb_o.at[s], r1o.at[s], sib))
            pair_sends[-1].start()

        def resend(s, k, landing, to):
            pair_sends[s].wait_send()
            _cast_rows(acc_o.at[s], sb_o.at[s])
            cp = rcopy(k, sb_o.at[s], landing, to)
            cp.start()
            return cp

        hop1 = []
        for s, chip in enumerate(order):
            load(o_rows_of(chip, c), acc_o.at[s]).wait()
            rcopy(s, sb_o.at[s], r1o.at[s], sib).wait_recv()
            _accumulate(acc_o.at[s], r1o.at[s])
            if s < 2:
                hop1.append(resend(s, 4 + s, r2o.at[s], pa))

        rcopy(5, sb_o.at[1], r2o.at[1], pa).wait_recv()
        _accumulate(acc_o.at[2], r2o.at[1])
        hop2 = resend(2, 6, r3o, pb)
        rcopy(4, sb_o.at[0], r2o.at[0], pa).wait_recv()
        _accumulate(acc_o.at[3], r2o.at[0])
        rcopy(6, sb_o.at[2], r3o, pb).wait_recv()
        _accumulate(acc_o.at[3], r3o)

        out_o = gwo_sh.at[pl.ds(pl.multiple_of(c * o_half, 8), o_half), :]
        keep = load(acc_o.at[3], out_o)
        swap = rcopy(7, acc_o.at[3], out_o, sib)
        swap.start()
        rcopy(7, acc_o.at[3], gwo_sh.at[pl.ds(pl.multiple_of((1 - c) * o_half, 8), o_half), :], sib).wait_recv()
        for cp in [pair_sends[3]] + hop1 + [hop2, swap]:
            cp.wait_send()
        keep.wait()

    any_spec = pl.BlockSpec(memory_space=pl.ANY)
    return pl.pallas_call(
        body, name="rs_wout",
        out_shape=jax.ShapeDtypeStruct((o_rows, width), F32),
        in_specs=[any_spec], out_specs=any_spec,
        scratch_shapes=[pltpu.VMEM((4, o_half, width), F32),
                        pltpu.VMEM((o_half, width), F32),
                        pltpu.VMEM((4, o_half, width), BF16),
                        pltpu.VMEM((4, o_half, width), BF16),
                        pltpu.VMEM((2, o_half, width), BF16),
                        pltpu.VMEM((o_half, width), BF16),
                        pltpu.SemaphoreType.DMA((8,)), pltpu.SemaphoreType.DMA((8,)), pltpu.SemaphoreType.DMA((1,))],
        compiler_params=_params(24),
    )(gwo)


def _adamw(name, w, g, m, v, rows):
    def body(w_ref, g_ref, m_ref, v_ref, go_ref, d_ref, nm_ref, nv_ref):
        gv = g_ref[...]
        go_ref[...] = gv
        nm = ADAM_B1 * m_ref[...] + (1.0 - ADAM_B1) * gv
        nv = ADAM_B2 * v_ref[...] + (1.0 - ADAM_B2) * (gv * gv)
        m_hat = nm / (1.0 - ADAM_B1 ** ADAM_STEP)
        v_hat = nv / (1.0 - ADAM_B2 ** ADAM_STEP)
        d_ref[...] = -ADAM_LR * (m_hat / (jnp.sqrt(v_hat) + ADAM_EPS) + ADAM_WD * w_ref[...])
        nm_ref[...] = nm
        nv_ref[...] = nv

    spec = pl.BlockSpec((rows, w.shape[1]), lambda i: (i, 0))
    shape = jax.ShapeDtypeStruct(w.shape, F32)
    return pl.pallas_call(
        body, name="adamw_" + name, grid=(w.shape[0] // rows,),
        out_shape=(shape,) * 4, in_specs=[spec] * 4, out_specs=(spec,) * 4,
        compiler_params=_params(32, ("arbitrary",)),
    )(w, g, m, v)


def kernel(x, norm_in, w_in, conv_w, attn_sinks, norm_conv_out, norm_attn_out, w_out, norm_final, loss_target, m_norm_in, m_w_in, m_conv_w, m_attn_sinks, m_norm_conv_out, m_norm_attn_out, m_w_out, m_norm_final, v_norm_in, v_w_in, v_conv_w, v_attn_sinks, v_norm_conv_out, v_norm_attn_out, v_w_out, v_norm_final):
    chip = 2 * lax.axis_index("x") + lax.axis_index("y")
    xs, target = x[0], loss_target[0]
    norm_final2 = norm_final.reshape(1, D_MODEL)
    w_in_t, m_w_in_t, v_w_in_t = w_in[0].T, m_w_in[0].T, v_w_in[0].T

    wt, wo, cw4 = _ag_weights(w_in_t, w_out[0], conv_w[0])
    cw = jnp.transpose(cw4, (1, 0, 2)).reshape(3, D_CONV)

    h, pc, q, kv, ga = _fwd_in(xs, norm_in, wt)
    oc = _conv_fwd(pc, cw, norm_conv_out)
    ya, oa = _attn_fwd(q, kv, ga, attn_sinks, norm_attn_out)
    dx2, doc, doa, gwo, gnf, loss_lanes = _out_proj_loss(xs, oc, oa, wo, norm_final2, target)

    dpc, gnc, gcw = _conv_bwd(pc, doc, cw, norm_conv_out)
    dqg, gna, gsink = _attn_bwd(q, kv, ga, ya, doa, attn_sinks, norm_attn_out)
    gwo_sh = _rs_wout(gwo)

    last = jnp.zeros((1, D_MODEL), F32).at[:, 0:N_HEADS].set(gsink[:, 0:N_HEADS]).at[0, N_HEADS].set(jnp.sum(loss_lanes))
    small = jnp.concatenate([jnp.zeros((1, D_MODEL), F32), gnc, gna, gnf, gcw[0:3], last], axis=0)
    grad_x, small_sum, gwt_sh = _bwd_in(dpc, dqg, h, wt, xs, norm_in, dx2, small)

    loss = small_sum[7, N_HEADS]
    g_norm_in, g_norm_conv, g_norm_attn = small_sum[0:1], small_sum[1:2], small_sum[2:3]
    g_norm_final = small_sum[3]
    g_conv_w = lax.dynamic_slice(small_sum[4:7], (0, chip * (D_CONV // N_CHIPS)), (3, D_CONV // N_CHIPS))[None]
    g_sinks = small_sum[7:8, 0:N_HEADS]

    def adam(name, w, g, m, v, rows):
        shape = w.shape
        two_d = (-1, shape[-1])
        out = _adamw(name, w.reshape(two_d), g.reshape(two_d), m.reshape(two_d), v.reshape(two_d), rows)
        return tuple(o.reshape(shape) for o in out)

    weights = (norm_in, w_in, conv_w, attn_sinks, norm_conv_out, norm_attn_out, w_out, norm_final)
    grads = (g_norm_in, None, g_conv_w, g_sinks, g_norm_conv, g_norm_attn, gwo_sh[None], g_norm_final)
    moments_m = (m_norm_in, m_w_in, m_conv_w, m_attn_sinks, m_norm_conv_out, m_norm_attn_out, m_w_out, m_norm_final)
    moments_v = (v_norm_in, v_w_in, v_conv_w, v_attn_sinks, v_norm_conv_out, v_norm_attn_out, v_w_out, v_norm_final)
    rows = (1, None, 3, 1, 1, 1, 128, 1)
    names = ("norm_in", "w_in", "conv_w", "attn_sinks", "norm_conv_out", "norm_attn_out", "w_out", "norm_final")
    updates = []
    for args in zip(names, weights, grads, moments_m, moments_v, rows):
        if args[0] == "w_in":
            out_t = _adamw("w_in", w_in_t, gwt_sh, m_w_in_t, v_w_in_t, 200)
            updates.append(tuple(o.T[None] for o in out_t))
        else:
            updates.append(adam(*args))
    grads_out, deltas, new_m, new_v = zip(*updates)
    return (loss, grad_x[None], *grads_out, *deltas, *new_m, *new_v)
```

```python
import jax
import jax.numpy as jnp
from jax import lax
from jax.experimental import pallas as pl
from jax.experimental.pallas import tpu as pltpu

F32 = jnp.float32
BF16 = jnp.bfloat16
MESH = pl.DeviceIdType.MESH

D_MODEL = 1024
D_CONV = 1024
D_ATTN = 1024
D_KV = 128
D_QG = 2 * D_ATTN + 2 * D_KV
D_MIX = D_CONV + D_ATTN
D_PC = 4 * D_CONV
D_IN_PROJ = D_PC + 2 * D_ATTN + 2 * D_KV
ROW_Q = D_PC
ROW_KV = ROW_Q + D_ATTN
ROW_GA = ROW_KV + 2 * D_KV
N_HEADS = 16
HEAD_DIM = 64
HEADS_PER_KV = 8
BLK = 128
N_CHIPS = 4
RMS_EPS = 1e-5
SCALE = HEAD_DIM ** -0.5
SLOPES = tuple(2.0 ** (-8.0 * (h + 1) / N_HEADS) for h in range(N_HEADS))

ADAM_LR, ADAM_B1, ADAM_B2, ADAM_EPS, ADAM_WD, ADAM_STEP = 0.001, 0.9, 0.999, 1e-08, 0.01, 10

TOK_TILE = 256
MIB = 1 << 20


def _params(vmem_mib, semantics=None):
    return pltpu.CompilerParams(dimension_semantics=semantics, vmem_limit_bytes=vmem_mib * MIB)


def _nn(a, b):
    return jnp.dot(a, b, preferred_element_type=F32)


def _nt(a, b):
    return lax.dot_general(a, b, (((1,), (1,)), ((), ())), preferred_element_type=F32)


def _tn(a, b):
    return lax.dot_general(a, b, (((0,), (0,)), ((), ())), preferred_element_type=F32)


def _rstd(v):
    return lax.rsqrt(jnp.mean(v * v, axis=-1, keepdims=True) + RMS_EPS)


def _rms_bwd(g, xhat, rstd):
    return rstd * (g - xhat * jnp.mean(g * xhat, axis=-1, keepdims=True))


def _silu_and_grad(g):
    s = jax.nn.sigmoid(g)
    return g * s, s * (1.0 + g * (1.0 - s))


def _resident(shape):
    return pl.BlockSpec(shape, lambda *_: (0,) * len(shape), pipeline_mode=pl.Buffered(1))


def _xor(a, b):
    return a + b - 2 * a * b


def _cast_rows(src_ref, dst_ref, rows=32):
    def step(i, carry):
        sl = pl.ds(pl.multiple_of(i * rows, rows), rows)
        dst_ref[sl, :] = src_ref[sl, :].astype(dst_ref.dtype)
        return carry
    lax.fori_loop(0, src_ref.shape[0] // rows, step, 0)


def _ag_weights(wt_sh, wo_sh, cw_sh):
    wt_rows, wo_rows = wt_sh.shape[0], wo_sh.shape[0]
    wt_half, wo_half = wt_rows // 2, wo_rows // 2
    width = wt_sh.shape[1]

    def body(wt_ref, wo_ref, cw_ref, wt_out, wo_out, cw_out, f32_t, f32_o, own_t, own_o, land_t, land_o,
             send_sems, recv_sems, local_sems):
        x, y, c = lax.axis_index("x"), lax.axis_index("y"), lax.axis_index("c")
        j = 2 * x + y
        p1 = (_xor(x, c), _xor(y, 1 - c), c)
        p2 = (_xor(x, 1 - c), _xor(y, c), c)
        sib = (x, y, 1 - c)
        j1 = 2 * p1[0] + p1[1]
        j2 = 2 * p2[0] + p2[1]
        j3 = 3 - j

        def wt_rows_of(chip, half):
            return wt_out.at[pl.ds(pl.multiple_of(chip * wt_rows + half * wt_half, 16), wt_half), :]

        def wo_rows_of(chip, half):
            return wo_out.at[pl.ds(pl.multiple_of(chip * wo_rows + half * wo_half, 16), wo_half), :]

        def rcopy(k, src, dst, to):
            return pltpu.make_async_remote_copy(src_ref=src, dst_ref=dst, send_sem=send_sems.at[k],
                                                recv_sem=recv_sems.at[k], device_id=to, device_id_type=MESH)

        def lcopy(k, src, dst):
            cp = pltpu.make_async_copy(src, dst, local_sems.at[k])
            cp.start()
            return cp

        loads = [lcopy(0, wt_ref, f32_t), lcopy(1, wo_ref, f32_o)]
        local = [lcopy(2, cw_ref, cw_out.at[j])]
        for cp in loads:
            cp.wait()
        _cast_rows(f32_t, own_t)
        _cast_rows(f32_o, own_o)
        local += [lcopy(0, own_t, wt_out.at[pl.ds(pl.multiple_of(j * wt_rows, 16), wt_rows), :]),
                  lcopy(1, own_o, wo_out.at[pl.ds(pl.multiple_of(j * wo_rows, 16), wo_rows), :])]
        half_t = own_t.at[pl.ds(pl.multiple_of(c * wt_half, 16), wt_half), :]
        half_o = own_o.at[pl.ds(pl.multiple_of(c * wo_half, 16), wo_half), :]

        def publish(slot, chip, k_t, k_o, k_local):
            sends = [rcopy(k_t, land_t.at[slot], wt_rows_of(chip, c), sib), rcopy(k_o, land_o.at[slot], wo_rows_of(chip, c), sib)]
            for cp in sends:
                cp.start()
            local.extend([lcopy(k_local, land_t.at[slot], wt_rows_of(chip, c)),
                          lcopy(k_local + 1, land_o.at[slot], wo_rows_of(chip, c))])
            return sends

        hop1 = [rcopy(0, half_t, land_t.at[0], p1), rcopy(1, half_o, land_o.at[0], p1), rcopy(2, cw_ref, cw_out.at[j], p1)]
        for cp in hop1:
            cp.start()
        rcopy(0, half_t, land_t.at[0], p1).wait_recv()
        rcopy(1, half_o, land_o.at[0], p1).wait_recv()
        rcopy(2, cw_ref, cw_out.at[j1], p1).wait_recv()
        hop2 = [rcopy(3, half_t, land_t.at[1], p2), rcopy(5, half_o, land_o.at[1], p2), rcopy(7, cw_ref, cw_out.at[j], p2),
                rcopy(4, land_t.at[0], land_t.at[2], p2), rcopy(6, land_o.at[0], land_o.at[2], p2),
                rcopy(8, cw_out.at[j1], cw_out.at[j1], p2)]
        for cp in hop2:
            cp.start()
        swaps = publish(0, j1, 9, 12, 3)
        rcopy(3, half_t, land_t.at[1], p2).wait_recv()
        rcopy(5, half_o, land_o.at[1], p2).wait_recv()
        rcopy(7, cw_ref, cw_out.at[j2], p2).wait_recv()
        swaps += publish(1, j2, 10, 13, 5)
        rcopy(4, half_t, land_t.at[2], p2).wait_recv()
        rcopy(6, half_o, land_o.at[2], p2).wait_recv()
        rcopy(8, cw_ref, cw_out.at[j3], p2).wait_recv()
        swaps += publish(2, j3, 11, 14, 7)
        for k, chip in ((9, j2), (10, j1), (11, j3)):
            rcopy(k, half_t, wt_rows_of(chip, 1 - c), sib).wait_recv()
        for k, chip in ((12, j2), (13, j1), (14, j3)):
            rcopy(k, half_o, wo_rows_of(chip, 1 - c), sib).wait_recv()
        for cp in hop1 + hop2 + swaps:
            cp.wait_send()
        for cp in local:
            cp.wait()

    any_spec = pl.BlockSpec(memory_space=pl.ANY)
    return pl.pallas_call(
        body, name="ag_weights",
        out_shape=(jax.ShapeDtypeStruct((N_CHIPS * wt_rows, width), BF16),
                   jax.ShapeDtypeStruct((N_CHIPS * wo_rows, width), BF16),
                   jax.ShapeDtypeStruct((N_CHIPS,) + cw_sh.shape, cw_sh.dtype)),
        in_specs=[any_spec, any_spec, any_spec],
        out_specs=(any_spec, any_spec, any_spec),
        scratch_shapes=[pltpu.VMEM((wt_rows, width), F32), pltpu.VMEM((wo_rows, width), F32),
                        pltpu.VMEM((wt_rows, width), BF16), pltpu.VMEM((wo_rows, width), BF16),
                        pltpu.VMEM((3, wt_half, width), BF16), pltpu.VMEM((3, wo_half, width), BF16),
                        pltpu.SemaphoreType.DMA((15,)), pltpu.SemaphoreType.DMA((15,)), pltpu.SemaphoreType.DMA((9,))],
        compiler_params=_params(32),
    )(wt_sh, wo_sh, cw_sh)


def _fwd_in(x, norm_in, wt):
    seq = x.shape[0]
    tile = TOK_TILE

    def body(x_ref, g_ref, wt_ref, h_ref, pc_ref, q_ref, kv_ref, ga_ref):
        xv = x_ref[...]
        h = (xv * _rstd(xv) * g_ref[...]).astype(BF16)
        h_ref[...] = h
        for blk in range(D_PC // D_CONV):
            pc_ref[:, blk * D_CONV:(blk + 1) * D_CONV] = _nt(h, wt_ref[blk * D_CONV:(blk + 1) * D_CONV, :])
        q_ref[...] = _nt(h, wt_ref[ROW_Q:ROW_KV, :])
        kv_ref[...] = _nt(h, wt_ref[ROW_KV:ROW_GA, :])
        ga_ref[...] = _nt(h, wt_ref[ROW_GA:D_IN_PROJ, :])

    def row(width):
        return pl.BlockSpec((tile, width), lambda i: (i, 0))

    return pl.pallas_call(
        body, name="fwd_in", grid=(seq // tile,),
        out_shape=(jax.ShapeDtypeStruct((seq, D_MODEL), BF16), jax.ShapeDtypeStruct((seq, D_PC), F32),
                   jax.ShapeDtypeStruct((seq, D_ATTN), F32), jax.ShapeDtypeStruct((seq, 2 * D_KV), F32),
                   jax.ShapeDtypeStruct((seq, D_ATTN), F32)),
        in_specs=[row(D_MODEL), _resident((1, D_MODEL)), _resident(wt.shape)],
        out_specs=(row(D_MODEL), row(D_PC), row(D_ATTN), row(2 * D_KV), row(D_ATTN)),
        compiler_params=_params(48, ("arbitrary",)),
    )(x, norm_in, wt)


def _conv_core(pc_ref, zbuf, cw_ref):
    tile = pc_ref.shape[0]
    cb = pc_ref[:, 0:D_CONV]
    cc = pc_ref[:, D_CONV:2 * D_CONV]
    cu = pc_ref[:, 2 * D_CONV:3 * D_CONV]
    z = cc * cu
    zbuf[8:tile + 8, :] = z
    z1 = zbuf[7:tile + 7, :]
    z2 = zbuf[6:tile + 6, :]
    conv = cw_ref[0:1, :] * z2 + cw_ref[1:2, :] * z1 + cw_ref[2:3, :] * z
    return cb, cc, cu, z, z1, z2, conv


def _conv_fwd(pc, conv_w, norm_conv_out):
    seq = pc.shape[0]
    tile = TOK_TILE

    def body(pc_ref, cw_ref, gn_ref, oc_ref, zbuf):
        @pl.when(pl.program_id(0) == 0)
        def _():
            zbuf[0:8, :] = jnp.zeros((8, D_CONV), F32)

        cb, _, _, _, _, _, conv = _conv_core(pc_ref, zbuf, cw_ref)
        yc = cb * conv
        silu, _ = _silu_and_grad(pc_ref[:, 3 * D_CONV:4 * D_CONV])
        oc_ref[...] = (yc * _rstd(yc) * gn_ref[...] * silu).astype(BF16)
        zbuf[0:8, :] = zbuf[tile:tile + 8, :]

    return pl.pallas_call(
        body, name="conv_fwd", grid=(seq // tile,),
        out_shape=jax.ShapeDtypeStruct((seq, D_CONV), BF16),
        in_specs=[pl.BlockSpec((tile, D_PC), lambda i: (i, 0)), _resident(conv_w.shape), _resident((1, D_CONV))],
        out_specs=pl.BlockSpec((tile, D_CONV), lambda i: (i, 0)),
        scratch_shapes=[pltpu.VMEM((tile + 8, D_CONV), F32)],
        compiler_params=_params(40, ("arbitrary",)),
    )(pc, conv_w, norm_conv_out)


def _band_geometry(block_index):
    qi = lax.broadcasted_iota(jnp.int32, (BLK, BLK), 0)
    kp = lax.broadcasted_iota(jnp.int32, (BLK, BLK), 1)
    use_cur = kp <= qi
    dist = jnp.where(use_cur, qi - kp, qi - kp + BLK).astype(F32)
    valid = use_cur | (block_index > 0)
    return use_cur, dist, valid


def _block_diag(cur, prev, group):
    lane = lax.broadcasted_iota(jnp.int32, cur.shape, 1)

    def halves(t):
        other = pltpu.roll(t, 64, 1)
        lo, hi = (t, other) if group == 0 else (other, t)
        return jnp.where(lane < 64, lo, 0.0), jnp.where(lane >= 64, hi, 0.0)

    return jnp.concatenate(halves(cur) + halves(prev), axis=0).astype(BF16)


def _merge(s4, use_cur):
    return (jnp.where(use_cur, s4[:, 0:BLK], s4[:, 2 * BLK:3 * BLK]),
            jnp.where(use_cur, s4[:, BLK:2 * BLK], s4[:, 3 * BLK:4 * BLK]))


def _split(a, b, use_cur):
    return jnp.concatenate([jnp.where(use_cur, a, 0.0), jnp.where(use_cur, b, 0.0),
                            jnp.where(use_cur, 0.0, a), jnp.where(use_cur, 0.0, b)], axis=1)


def _softmax_head(s, head, sink, dist, valid):
    sc = jnp.where(valid, s - SLOPES[head] * dist, -jnp.inf)
    m = jnp.maximum(jnp.max(sc, axis=-1, keepdims=True), sink)
    p = jnp.exp(sc - m)
    es = jnp.exp(sink - m)
    inv = 1.0 / (jnp.sum(p, axis=-1, keepdims=True) + es)
    return p * inv, es * inv


def _attn_probs(q_ref, kvc_ref, kvp_ref, sink_ref, geometry):
    use_cur, dist, valid = geometry
    groups = range(N_HEADS // HEADS_PER_KV)
    kbd = [_block_diag(kvc_ref[:, 0:D_KV], kvp_ref[:, 0:D_KV], g) for g in groups]
    vbd = [_block_diag(kvc_ref[:, D_KV:2 * D_KV], kvp_ref[:, D_KV:2 * D_KV], g) for g in groups]
    qps = [(q_ref[:, j * 128:(j + 1) * 128] * SCALE).astype(BF16) for j in range(N_HEADS // 2)]
    scores = []
    for j, qp in enumerate(qps):
        scores += _merge(_nt(qp, kbd[j // 4]), use_cur)
    sinks = [sink_ref[0, h] for h in range(N_HEADS)]
    scores = [jnp.where(valid, s - SLOPES[h] * dist, -jnp.inf) for h, s in enumerate(scores)]
    maxes = [jnp.maximum(jnp.max(s, axis=-1, keepdims=True), sinks[h]) for h, s in enumerate(scores)]
    exps = [jnp.exp(s - m) for s, m in zip(scores, maxes)]
    sink_exps = [jnp.exp(sinks[h] - m) for h, m in enumerate(maxes)]
    invs = [1.0 / (jnp.sum(e, axis=-1, keepdims=True) + se) for e, se in zip(exps, sink_exps)]
    probs = [e * inv for e, inv in zip(exps, invs)]
    sink_probs = [se * inv for se, inv in zip(sink_exps, invs)]
    return qps, kbd, vbd, probs, sink_probs


def _kv_specs(n_blocks, reverse):
    def blk(i):
        return (n_blocks - 1 - i) if reverse else i
    cur = pl.BlockSpec((BLK, 2 * D_KV), lambda i: (blk(i), 0))
    prev = pl.BlockSpec((BLK, 2 * D_KV), lambda i: (jnp.maximum(blk(i) - 1, 0), 0))
    return cur, prev


def _attn_fwd(q, kv, ga, sinks, norm_attn_out):
    seq = q.shape[0]
    n_blocks = seq // BLK

    def body(q_ref, kvc_ref, kvp_ref, ga_ref, sink_ref, gn_ref, ya_ref, oa_ref):
        geometry = _band_geometry(pl.program_id(0))
        _, _, vbd, probs, _ = _attn_probs(q_ref, kvc_ref, kvp_ref, sink_ref, geometry)
        p4s = [_split(probs[2 * j], probs[2 * j + 1], geometry[0]).astype(BF16) for j in range(N_HEADS // 2)]
        ya = jnp.concatenate([_nn(p4, vbd[j // 4]) for j, p4 in enumerate(p4s)], axis=1)
        ya_ref[...] = ya
        silu, _ = _silu_and_grad(ga_ref[...])
        oa_ref[...] = (ya * _rstd(ya) * gn_ref[...] * silu).astype(BF16)

    row = pl.BlockSpec((BLK, D_ATTN), lambda i: (i, 0))
    kv_cur, kv_prev = _kv_specs(n_blocks, reverse=False)
    return pl.pallas_call(
        body, name="attn_fwd", grid=(n_blocks,),
        out_shape=(jax.ShapeDtypeStruct((seq, D_ATTN), F32), jax.ShapeDtypeStruct((seq, D_ATTN), BF16)),
        in_specs=[row, kv_cur, kv_prev, row, pl.BlockSpec(memory_space=pltpu.SMEM), _resident((1, D_ATTN))],
        out_specs=(row, row),
        compiler_params=_params(32, ("arbitrary",)),
    )(q, kv, kv, ga, sinks, norm_attn_out)


def _out_proj_loss(x, oc, oa, wo, norm_final, target):
    seq = x.shape[0]
    tile = TOK_TILE

    def body(x_ref, oc_ref, oa_ref, wo_ref, gf_ref, t_ref, dx2_ref, doc_ref, doa_ref, gwo_ref, gnf_ref, loss_ref):
        @pl.when(pl.program_id(0) == 0)
        def _():
            gwo_ref[...] = jnp.zeros_like(gwo_ref)
            gnf_ref[...] = jnp.zeros_like(gnf_ref)
            loss_ref[...] = jnp.zeros_like(loss_ref)

        oc, oa = oc_ref[...], oa_ref[...]
        x2 = x_ref[...] + _nn(oc, wo_ref[0:D_CONV, :]) + _nn(oa, wo_ref[D_CONV:D_MIX, :])
        r = _rstd(x2)
        xhat = x2 * r
        err = xhat * gf_ref[...] - t_ref[...]
        loss_ref[...] += jnp.sum(err * err, axis=0, keepdims=True) * (0.5 / D_MODEL)
        dy = err * (1.0 / D_MODEL)
        gnf_ref[...] += jnp.sum(dy * xhat, axis=0, keepdims=True)
        dx2 = _rms_bwd(dy * gf_ref[...], xhat, r)
        dx2_ref[...] = dx2
        db = dx2.astype(BF16)
        doc_ref[...] = _nt(db, wo_ref[0:D_CONV, :])
        doa_ref[...] = _nt(db, wo_ref[D_CONV:D_MIX, :])
        gwo_ref[0:D_CONV, :] += _tn(oc, db)
        gwo_ref[D_CONV:D_MIX, :] += _tn(oa, db)

    row = pl.BlockSpec((tile, D_MODEL), lambda i: (i, 0))
    vec = pl.BlockSpec((1, D_MODEL), lambda i: (0, 0))
    return pl.pallas_call(
        body, name="out_proj_loss", grid=(seq // tile,),
        out_shape=(jax.ShapeDtypeStruct((seq, D_MODEL), F32), jax.ShapeDtypeStruct((seq, D_CONV), F32),
                   jax.ShapeDtypeStruct((seq, D_ATTN), F32), jax.ShapeDtypeStruct((D_MIX, D_MODEL), F32),
                   jax.ShapeDtypeStruct((1, D_MODEL), F32), jax.ShapeDtypeStruct((1, D_MODEL), F32)),
        in_specs=[row, row, row, _resident(wo.shape), _resident((1, D_MODEL)), row],
        out_specs=(row, row, row, pl.BlockSpec((D_MIX, D_MODEL), lambda i: (0, 0)), vec, vec),
        compiler_params=_params(48, ("arbitrary",)),
    )(x, oc, oa, wo, norm_final, target)


def _conv_bwd(pc, doc, conv_w, norm_conv_out):
    seq = pc.shape[0]
    tile = TOK_TILE
    n_tiles = seq // tile

    def body(pc_ref, hcc_ref, hcu_ref, doc_ref, cw_ref, gn_ref, dpc_ref, gnc_ref, gcw_ref, zbuf, dbuf):
        step = pl.program_id(0)

        @pl.when(step == 0)
        def _():
            gnc_ref[...] = jnp.zeros_like(gnc_ref)
            gcw_ref[...] = jnp.zeros_like(gcw_ref)
            dbuf[tile:tile + 8, :] = jnp.zeros((8, D_CONV), F32)

        is_first_tile = step == n_tiles - 1
        zbuf[0:8, :] = jnp.where(is_first_tile, 0.0, hcc_ref[...] * hcu_ref[...])
        cb, cc, cu, z, z1, z2, conv = _conv_core(pc_ref, zbuf, cw_ref)
        gc = pc_ref[:, 3 * D_CONV:4 * D_CONV]
        silu, dsilu = _silu_and_grad(gc)
        yc = cb * conv
        r = _rstd(yc)
        xhat = yc * r
        do = doc_ref[...]
        dn = do * silu
        dpc_ref[:, 3 * D_CONV:4 * D_CONV] = (do * (xhat * gn_ref[...]) * dsilu).astype(BF16)
        gnc_ref[...] += jnp.sum(dn * xhat, axis=0, keepdims=True)
        dyc = _rms_bwd(dn * gn_ref[...], xhat, r)
        dpc_ref[:, 0:D_CONV] = (dyc * conv).astype(BF16)
        dconv = dyc * cb
        gcw_ref[0:1, :] += jnp.sum(dconv * z2, axis=0, keepdims=True)
        gcw_ref[1:2, :] += jnp.sum(dconv * z1, axis=0, keepdims=True)
        gcw_ref[2:3, :] += jnp.sum(dconv * z, axis=0, keepdims=True)
        dbuf[0:tile, :] = dconv
        dz = cw_ref[2:3, :] * dconv + cw_ref[1:2, :] * dbuf[1:tile + 1, :] + cw_ref[0:1, :] * dbuf[2:tile + 2, :]
        dpc_ref[:, D_CONV:2 * D_CONV] = (dz * cu).astype(BF16)
        dpc_ref[:, 2 * D_CONV:3 * D_CONV] = (dz * cc).astype(BF16)
        dbuf[tile:tile + 8, :] = dbuf[0:8, :]

    def rev(i):
        return n_tiles - 1 - i

    def halo(col_block):
        return pl.BlockSpec((8, D_CONV), lambda i: (jnp.maximum(rev(i) * (tile // 8) - 1, 0), col_block))

    return pl.pallas_call(
        body, name="conv_bwd", grid=(n_tiles,),
        out_shape=(jax.ShapeDtypeStruct((seq, D_PC), BF16), jax.ShapeDtypeStruct((1, D_CONV), F32),
                   jax.ShapeDtypeStruct((8, D_CONV), F32)),
        in_specs=[pl.BlockSpec((tile, D_PC), lambda i: (rev(i), 0)), halo(1), halo(2),
                  pl.BlockSpec((tile, D_CONV), lambda i: (rev(i), 0)), _resident(conv_w.shape), _resident((1, D_CONV))],
        out_specs=(pl.BlockSpec((tile, D_PC), lambda i: (rev(i), 0)), pl.BlockSpec((1, D_CONV), lambda i: (0, 0)),
                   pl.BlockSpec((8, D_CONV), lambda i: (0, 0))),
        scratch_shapes=[pltpu.VMEM((tile + 8, D_CONV), F32), pltpu.VMEM((tile + 8, D_CONV), F32)],
        compiler_params=_params(48, ("arbitrary",)),
    )(pc, pc, pc, doc, conv_w, norm_conv_out)


def _attn_bwd(q, kv, ga, ya, doa, sinks, norm_attn_out, gwo):
    seq = q.shape[0]
    n_blocks = seq // BLK
    stage_steps = (0, n_blocks // 4, n_blocks // 2, (3 * n_blocks) // 4, n_blocks - 1)

    def body(q_ref, kvc_ref, kvp_ref, ga_ref, ya_ref, doa_ref, sink_ref, gn_ref, gwo_ref,
             dqg_ref, gna_ref, gs_ref, gwo_sh, carry, dya_buf, *rs_scratch):
        step = pl.program_id(0)

        @pl.when(step == 0)
        def _():
            gna_ref[...] = jnp.zeros_like(gna_ref)
            gs_ref[...] = jnp.zeros_like(gs_ref)
            carry[...] = jnp.zeros_like(carry)

        ya = ya_ref[...]
        r = _rstd(ya)
        xhat = ya * r
        silu, dsilu = _silu_and_grad(ga_ref[...])
        do = doa_ref[...]
        dn = do * silu
        dqg_ref[:, D_ATTN:2 * D_ATTN] = (do * (xhat * gn_ref[...]) * dsilu).astype(BF16)
        gna_ref[...] += jnp.sum(dn * xhat, axis=0, keepdims=True)
        dya_buf[...] = _rms_bwd(dn * gn_ref[...], xhat, r).astype(BF16)

        geometry = _band_geometry(n_blocks - 1 - step)
        use_cur = geometry[0]
        lane = lax.broadcasted_iota(jnp.int32, (BLK, 128), 1)
        sink_lane = lax.broadcasted_iota(jnp.int32, (1, 128), 1)
        gsink = jnp.zeros((1, 128), F32)

        def fold(bd):
            return (jnp.where(lane < 64, bd[0:BLK], 0.0) + jnp.where(lane >= 64, bd[BLK:2 * BLK], 0.0),
                    jnp.where(lane < 64, bd[2 * BLK:3 * BLK], 0.0) + jnp.where(lane >= 64, bd[3 * BLK:4 * BLK], 0.0))

        pairs = range(N_HEADS // 2)
        qps, kbd, vbd, probs, sink_probs = _attn_probs(q_ref, kvc_ref, kvp_ref, sink_ref, geometry)
        dyps = [dya_buf[:, j * 128:(j + 1) * 128] for j in pairs]
        dps = []
        for j in pairs:
            dps += _merge(_nt(dyps[j], vbd[j // 4]), use_cur)
        deltas = [jnp.sum(p * dp, axis=-1, keepdims=True) for p, dp in zip(probs, dps)]
        dss = [p * (dp - delta) for p, dp, delta in zip(probs, dps, deltas)]
        for h in range(N_HEADS):
            dsink = -jnp.sum(sink_probs[h] * deltas[h], axis=0, keepdims=True)
            gsink = gsink + jnp.where(sink_lane == h, dsink, 0.0)
        gs_ref[...] += gsink
        ds4s = [_split(dss[2 * j], dss[2 * j + 1], use_cur).astype(BF16) for j in pairs]
        p4s = [_split(probs[2 * j], probs[2 * j + 1], use_cur).astype(BF16) for j in pairs]
        dqg_ref[:, 0:D_ATTN] = jnp.concatenate([_nn(ds4s[j], kbd[j // 4]) * SCALE for j in pairs], axis=1).astype(BF16)
        sums = []
        for group in range(N_HEADS // HEADS_PER_KV):
            acc = [jnp.zeros((BLK, 128), F32) for _ in range(4)]
            for j in range(group * 4, group * 4 + 4):
                for slot, part in enumerate(fold(_tn(ds4s[j], qps[j])) + fold(_tn(p4s[j], dyps[j]))):
                    acc[slot] = acc[slot] + part
            sums.append([a + pltpu.roll(a, 64, 1) for a in acc])
        dk_cur, dk_prev, dv_cur, dv_prev = (jnp.where(lane < 64, a, b) for a, b in zip(sums[0], sums[1]))
        dqg_ref[:, 2 * D_ATTN:2 * D_ATTN + 2 * D_KV] = (jnp.concatenate([dk_cur, dv_cur], axis=1) + carry[...]).astype(BF16)
        carry[...] = jnp.concatenate([dk_prev, dv_prev], axis=1)

        for at, stage in zip(stage_steps, _rs_wout_stages(gwo_ref, gwo_sh, *rs_scratch)):
            pl.when(step == at)(stage)

    row = pl.BlockSpec((BLK, D_ATTN), lambda i: (n_blocks - 1 - i, 0))
    kv_cur, kv_prev = _kv_specs(n_blocks, reverse=True)
    any_spec = pl.BlockSpec(memory_space=pl.ANY)
    return pl.pallas_call(
        body, name="attn_bwd", grid=(n_blocks,),
        out_shape=(jax.ShapeDtypeStruct((seq, D_QG), BF16), jax.ShapeDtypeStruct((1, D_ATTN), F32),
                   jax.ShapeDtypeStruct((1, 128), F32), jax.ShapeDtypeStruct((gwo.shape[0] // N_CHIPS, gwo.shape[1]), F32)),
        in_specs=[row, kv_cur, kv_prev, row, row, row, pl.BlockSpec(memory_space=pltpu.SMEM), _resident((1, D_ATTN)),
                  any_spec],
        out_specs=(pl.BlockSpec((BLK, D_QG), lambda i: (n_blocks - 1 - i, 0)),
                   pl.BlockSpec((1, D_ATTN), lambda i: (0, 0)), pl.BlockSpec((1, 128), lambda i: (0, 0)), any_spec),
        scratch_shapes=[pltpu.VMEM((BLK, 2 * D_KV), F32), pltpu.VMEM((BLK, D_ATTN), BF16)] + _rs_wout_scratch(gwo.shape),
        compiler_params=_params(44, ("arbitrary",)),
    )(q, kv, kv, ga, ya, doa, sinks, norm_attn_out, gwo)


GBLK = 256
GSUB = 64
PAIR_RING = 4
LAG_PAIR, LAG_HOP1, LAG_HOP2 = 1, 7, 14


def _bwd_in(dpc, dqg, h, wt, x, norm_in, dx2, small):
    seq = x.shape[0]
    n_blk = D_IN_PROJ // GBLK
    per_chip = n_blk // N_CHIPS
    n_slots = (n_blk + 1) // 2
    n_sub = GBLK // GSUB
    chip_rows = D_IN_PROJ // N_CHIPS
    tile = TOK_TILE
    n_tiles = seq // tile
    n_steps = n_blk + max(n_tiles, LAG_HOP2)
    chunk = min(seq, 512)
    blk_q, blk_kv, blk_ga = ROW_Q // GBLK, ROW_KV // GBLK, ROW_GA // GBLK

    def block_of(i):
        k = i % N_CHIPS
        robin = per_chip * ((k % 2) * 2 + k // 2) + i // N_CHIPS
        if isinstance(i, int):
            return robin if i < per_chip * N_CHIPS else i
        return jnp.where(i < per_chip * N_CHIPS, robin, i)

    def owner_of(i):
        return (i // N_CHIPS) % 2

    def slot_of(i):
        return (i // (2 * N_CHIPS)) * N_CHIPS + i % N_CHIPS

    def body(dpc_ref, dqg_ref, wt_ref, h_ref, x_ref, g_ref, dx2_ref, small_ref, gx_ref, small_sum, gwt_sh,
             dh_acc, gni, keep, pbuf, xbuf, land, land2, small_land,
             pair_send, pair_recv, h1_send, h1_recv, h2_send, h2_recv, sw_send, sw_recv, sm_send, sm_recv, out_sem):
        step = pl.program_id(0)
        x_i, y_i, c = lax.axis_index("x"), lax.axis_index("y"), lax.axis_index("c")
        me = 4 * x_i + 2 * y_i + c
        j = 2 * x_i + y_i
        pa = (_xor(x_i, 1 - c), _xor(y_i, c), c)
        pb = (_xor(x_i, c), _xor(y_i, 1 - c), c)
        sib = (x_i, y_i, 1 - c)
        ja = 2 * pa[0] + pa[1]
        jb = 2 * pb[0] + pb[1]
        jd = 3 - j

        def remote(src, dst, send, recv, to):
            return pltpu.make_async_remote_copy(src_ref=src, dst_ref=dst, send_sem=send, recv_sem=recv,
                                                device_id=to, device_id_type=MESH)

        def piece(ref, slot, u, n):
            return ref.at[slot, pl.ds(u * GSUB, n * GSUB), :]

        def chip_rows_at(ref, local, n):
            return ref.at[pl.ds(pl.multiple_of(local, GSUB), n * GSUB), :]

        def pair_copy(i):
            slot = slot_of(i)
            return remote(pbuf.at[i % PAIR_RING], land.at[slot], pair_send.at[slot], pair_recv.at[slot], sib)

        def h1_copy(slot, u, n):
            k = slot * n_sub + u
            return remote(piece(xbuf, slot, u, n), piece(xbuf, slot, u, n), h1_send.at[k], h1_recv.at[k], pa)

        def h2_copy(slot, u, n, local):
            k = slot * n_sub + u
            return remote(piece(xbuf, slot, u, n), chip_rows_at(land2, local, n), h2_send.at[k], h2_recv.at[k], pb)

        def sw_copy(slot, u, n, local):
            k = slot * n_sub + u
            return remote(piece(keep, slot, u, n), chip_rows_at(gwt_sh, local, n), sw_send.at[k], sw_recv.at[k], sib)

        def owned(i):
            return (i >= 0) & (i < n_blk) & (owner_of(i) == c)

        def chip_of(blk, u):
            row = blk * GBLK + u * GSUB
            chip = row // chip_rows
            return chip, row - chip * chip_rows

        def pieces(blk):
            first, local = chip_of(blk, 0)
            whole = first == chip_of(blk, n_sub - 1)[0]
            if isinstance(blk, int):
                return [(True, 0, n_sub, first, local)] if whole else [(True, u, 1) + chip_of(blk, u) for u in range(n_sub)]
            return [(whole, 0, n_sub, first, local)] + [(jnp.logical_not(whole), u, 1) + chip_of(blk, u) for u in range(n_sub)]

        @pl.when(step == 0)
        def _():
            dh_acc[...] = jnp.zeros_like(dh_acc)
            gni[...] = jnp.zeros_like(gni)

        @pl.when(step < n_blk)
        def _():
            from_pc = block_of(step) < blk_q
            block = _tn(jnp.where(from_pc, dpc_ref[...], dqg_ref[...]), h_ref[...])
            for t in range(0, seq, chunk):
                d = jnp.where(from_pc, dpc_ref[t:t + chunk, :], dqg_ref[t:t + chunk, :])
                dh_acc[t:t + chunk, :] += _nn(d, wt_ref[...])

            @pl.when(owner_of(step) == c)
            def _():
                keep[slot_of(step)] = block

            @pl.when(owner_of(step) != c)
            def _():
                @pl.when(step >= 2 * PAIR_RING)
                def _():
                    pair_copy(step - 2 * PAIR_RING).wait_send()
                pbuf[step % PAIR_RING] = block.astype(BF16)
                pair_copy(step).start()

        i1 = step - LAG_PAIR

        @pl.when(owned(i1))
        def _():
            slot = slot_of(i1)
            pair_copy(i1).wait_recv()
            _accumulate(keep.at[slot], land.at[slot])
            for cond, u, n, chip, _ in pieces(block_of(i1)):
                @pl.when(cond & ((chip == ja) | (chip == jd)))
                def _(u=u, n=n):
                    _cast_rows(piece(keep, slot, u, n), piece(xbuf, slot, u, n))
                    h1_copy(slot, u, n).start()

        i2 = step - LAG_HOP1

        @pl.when(owned(i2))
        def _():
            slot = slot_of(i2)
            for cond, u, n, chip, local in pieces(block_of(i2)):
                @pl.when(cond & ((chip == j) | (chip == jb)))
                def _(u=u, n=n, chip=chip, local=local):
                    h1_copy(slot, u, n).wait_recv()
                    _accumulate(piece(keep, slot, u, n), piece(xbuf, slot, u, n))

                    @pl.when(chip == jb)
                    def _():
                        _cast_rows(piece(keep, slot, u, n), piece(xbuf, slot, u, n))
                        h2_copy(slot, u, n, local).start()

                @pl.when(cond & ((chip == ja) | (chip == jd)))
                def _(u=u, n=n):
                    h1_copy(slot, u, n).wait_send()

        i3 = step - LAG_HOP2

        @pl.when(owned(i3))
        def _():
            slot = slot_of(i3)
            for cond, u, n, chip, local in pieces(block_of(i3)):
                @pl.when(cond & (chip == j))
                def _(u=u, n=n, local=local):
                    h2_copy(slot, u, n, local).wait_recv()
                    _accumulate(piece(keep, slot, u, n), chip_rows_at(land2, local, n))
                    mine = pltpu.make_async_copy(piece(keep, slot, u, n), chip_rows_at(gwt_sh, local, n), out_sem.at[0])
                    mine.start()
                    sw_copy(slot, u, n, local).start()
                    mine.wait()

                @pl.when(cond & (chip == jb))
                def _(u=u, n=n, local=local):
                    h2_copy(slot, u, n, local).wait_send()

        e = step - n_blk

        @pl.when((e >= 0) & (e < n_tiles))
        def _():
            dh = dh_acc[pl.ds(pl.multiple_of(e * tile, tile), tile), :]
            xv = x_ref[...]
            r = _rstd(xv)
            xhat = xv * r
            gni[...] += jnp.sum(dh * xhat, axis=0, keepdims=True)
            gx_ref[...] = _rms_bwd(dh * g_ref[...], xhat, r) + dx2_ref[...]

        @pl.when(step == n_steps - 1)
        def _():
            small_land[me] = small_ref[...]
            small_land[me, 0:1, :] = gni[...]
            others = [(dx, dy, dc) for dx in (0, 1) for dy in (0, 1) for dc in (0, 1)][1:]
            sends = [remote(small_land.at[me], small_land.at[me], sm_send.at[k], sm_recv.at[k],
                            (_xor(x_i, dx), _xor(y_i, dy), _xor(c, dc))) for k, (dx, dy, dc) in enumerate(others)]
            for cp in sends:
                cp.start()
            for i in range(n_blk):
                if i + 2 * PAIR_RING >= n_blk:
                    @pl.when(owner_of(i) != c)
                    def _(i=i):
                        pair_copy(i).wait_send()
                for _, u, n, chip, local in pieces(block_of(i)):
                    @pl.when((j == chip) & (c == owner_of(i)))
                    def _(i=i, u=u, n=n, local=local):
                        sw_copy(slot_of(i), u, n, local).wait_send()

                    @pl.when((j == chip) & (c != owner_of(i)))
                    def _(i=i, u=u, n=n, local=local):
                        sw_copy(slot_of(i), u, n, local).wait_recv()
            for cp in sends:
                cp.wait_recv()
            total = small_land[0]
            for dev in range(1, 8):
                total = total + small_land[dev]
            small_sum[...] = total
            for cp in sends:
                cp.wait_send()

    def blk_at(i):
        return block_of(jnp.clip(i, 0, n_blk - 1))

    last_pc_step = max(i for i in range(n_blk) if block_of(i) < blk_q)

    def next_block(i, in_pc):
        i = jnp.clip(i, 0, n_blk - 1)
        step = jnp.full_like(i, last_pc_step if in_pc else n_blk - 1)
        for ahead in reversed(range(N_CHIPS)):
            cand = jnp.minimum(i + ahead, n_blk - 1)
            step = jnp.where((block_of(cand) < blk_q) == in_pc, cand, step)
        return block_of(step)

    def dqg_block(i):
        b = next_block(i, False)
        q_blk = jnp.clip(b - blk_q, 0, blk_kv - blk_q - 1)
        ga_blk = (D_ATTN // GBLK) + jnp.clip(b - blk_ga, 0, n_blk - blk_ga - 1)
        return jnp.where(b < blk_kv, q_blk, jnp.where(b == blk_kv, 2 * D_ATTN // GBLK, ga_blk))

    def tok(i):
        return (jnp.clip(i - n_blk, 0, n_tiles - 1), 0)

    n_piece = n_slots * n_sub
    dma = pltpu.SemaphoreType.DMA
    return pl.pallas_call(
        body, name="bwd_in", grid=(n_steps,),
        out_shape=(jax.ShapeDtypeStruct((seq, D_MODEL), F32), jax.ShapeDtypeStruct(small.shape, F32),
                   jax.ShapeDtypeStruct((chip_rows, D_MODEL), F32)),
        in_specs=[pl.BlockSpec((seq, GBLK), lambda i: (0, next_block(i, True))),
                  pl.BlockSpec((seq, GBLK), lambda i: (0, dqg_block(i))),
                  pl.BlockSpec((GBLK, D_MODEL), lambda i: (blk_at(i), 0)),
                  _resident(h.shape),
                  pl.BlockSpec((tile, D_MODEL), tok), _resident((1, D_MODEL)), pl.BlockSpec((tile, D_MODEL), tok),
                  _resident(small.shape)],
        out_specs=(pl.BlockSpec((tile, D_MODEL), tok), pl.BlockSpec(small.shape, lambda i: (0, 0)),
                   pl.BlockSpec(memory_space=pl.ANY)),
        scratch_shapes=[pltpu.VMEM((seq, D_MODEL), F32), pltpu.VMEM((1, D_MODEL), F32),
                        pltpu.VMEM((n_slots, GBLK, D_MODEL), F32), pltpu.VMEM((PAIR_RING, GBLK, D_MODEL), BF16),
                        pltpu.VMEM((n_slots, GBLK, D_MODEL), BF16), pltpu.VMEM((n_slots, GBLK, D_MODEL), BF16),
                        pltpu.VMEM((chip_rows, D_MODEL), BF16), pltpu.VMEM((8,) + small.shape, F32),
                        dma((n_slots,)), dma((n_slots,)), dma((n_piece,)), dma((n_piece,)), dma((n_piece,)),
                        dma((n_piece,)), dma((n_piece,)), dma((n_piece,)), dma((7,)), dma((7,)), dma((1,))],
        compiler_params=_params(62, ("arbitrary",)),
    )(dpc, dqg, wt, h, x, norm_in, dx2, small)


def _accumulate(dst_ref, src_ref, rows=16):
    def step(i, carry):
        sl = pl.ds(pl.multiple_of(i * rows, rows), rows)
        dst_ref[sl, :] = dst_ref[sl, :] + src_ref[sl, :].astype(F32)
        return carry
    lax.fori_loop(0, dst_ref.shape[0] // rows, step, 0)


def _rs_wout_scratch(gwo_shape):
    o_half, width = gwo_shape[0] // N_CHIPS // 2, gwo_shape[1]
    return [pltpu.VMEM((4, o_half, width), F32), pltpu.VMEM((4, o_half, width), BF16),
            pltpu.VMEM((4, o_half, width), BF16), pltpu.VMEM((2, o_half, width), BF16),
            pltpu.VMEM((o_half, width), BF16),
            pltpu.SemaphoreType.DMA((8,)), pltpu.SemaphoreType.DMA((8,)), pltpu.SemaphoreType.DMA((4,))]


def _rs_wout_stages(gwo_ref, gwo_sh, acc_o, sb_o, r1o, r2o, r3o, send_sems, recv_sems, local_sems):
    o_rows = gwo_ref.shape[0] // N_CHIPS
    o_half = o_rows // 2
    x, y, c = lax.axis_index("x"), lax.axis_index("y"), lax.axis_index("c")
    j = 2 * x + y
    pa = (_xor(x, 1 - c), _xor(y, c), c)
    pb = (_xor(x, c), _xor(y, 1 - c), c)
    sib = (x, y, 1 - c)
    ja = 2 * pa[0] + pa[1]
    jb = 2 * pb[0] + pb[1]
    jd = 3 - j
    order = (ja, jd, jb, j)
    sib_order = (jb, jd, ja, j)

    def rcopy(k, src, dst, to):
        return pltpu.make_async_remote_copy(src_ref=src, dst_ref=dst, send_sem=send_sems.at[k],
                                            recv_sem=recv_sems.at[k], device_id=to, device_id_type=MESH)

    def o_rows_of(chip, half):
        return gwo_ref.at[pl.ds(pl.multiple_of(chip * o_rows + half * o_half, 8), o_half), :]

    def load_all(chips, half):
        cps = [pltpu.make_async_copy(o_rows_of(chip, half), acc_o.at[s], local_sems.at[s]) for s, chip in enumerate(chips)]
        for cp in cps:
            cp.start()
        return cps

    def pair_send(s):
        return rcopy(s, sb_o.at[s], r1o.at[s], sib)

    def out_half(half):
        return gwo_sh.at[pl.ds(pl.multiple_of(half * o_half, 8), o_half), :]

    hop1 = [rcopy(4 + s, sb_o.at[s], r2o.at[s], pa) for s in range(2)]
    hop2 = rcopy(6, sb_o.at[2], r3o, pb)
    swap = rcopy(7, acc_o.at[3], out_half(c), sib)
    mine = pltpu.make_async_copy(acc_o.at[3], out_half(c), local_sems.at[0])

    def resend(s, copy):
        pair_send(s).wait_send()
        _cast_rows(acc_o.at[s], sb_o.at[s])
        copy.start()

    def stage_pair():
        for s, cp in enumerate(load_all(sib_order, 1 - c)):
            cp.wait()
            _cast_rows(acc_o.at[s], sb_o.at[s])
            pair_send(s).start()

    def stage_hop1():
        for s, cp in enumerate(load_all(order, c)):
            cp.wait()
            pair_send(s).wait_recv()
            _accumulate(acc_o.at[s], r1o.at[s])
            if s < 2:
                resend(s, hop1[s])

    def stage_hop2():
        hop1[1].wait_recv()
        _accumulate(acc_o.at[2], r2o.at[1])
        resend(2, hop2)
        hop1[0].wait_recv()
        _accumulate(acc_o.at[3], r2o.at[0])

    def stage_final():
        hop2.wait_recv()
        _accumulate(acc_o.at[3], r3o)
        mine.start()
        swap.start()

    def stage_drain():
        rcopy(7, acc_o.at[3], out_half(1 - c), sib).wait_recv()
        for cp in [pair_send(3)] + hop1 + [hop2, swap]:
            cp.wait_send()
        mine.wait()

    return stage_pair, stage_hop1, stage_hop2, stage_final, stage_drain


def _adamw(name, w, g, m, v, rows):
    def body(w_ref, g_ref, m_ref, v_ref, go_ref, d_ref, nm_ref, nv_ref):
        gv = g_ref[...]
        go_ref[...] = gv
        nm = ADAM_B1 * m_ref[...] + (1.0 - ADAM_B1) * gv
        nv = ADAM_B2 * v_ref[...] + (1.0 - ADAM_B2) * (gv * gv)
        m_hat = nm / (1.0 - ADAM_B1 ** ADAM_STEP)
        v_hat = nv / (1.0 - ADAM_B2 ** ADAM_STEP)
        d_ref[...] = -ADAM_LR * (m_hat / (jnp.sqrt(v_hat) + ADAM_EPS) + ADAM_WD * w_ref[...])
        nm_ref[...] = nm
        nv_ref[...] = nv

    spec = pl.BlockSpec((rows, w.shape[1]), lambda i: (i, 0))
    shape = jax.ShapeDtypeStruct(w.shape, F32)
    return pl.pallas_call(
        body, name="adamw_" + name, grid=(w.shape[0] // rows,),
        out_shape=(shape,) * 4, in_specs=[spec] * 4, out_specs=(spec,) * 4,
        compiler_params=_params(32, ("arbitrary",)),
    )(w, g, m, v)


def kernel(x, norm_in, w_in, conv_w, attn_sinks, norm_conv_out, norm_attn_out, w_out, norm_final, loss_target, m_norm_in, m_w_in, m_conv_w, m_attn_sinks, m_norm_conv_out, m_norm_attn_out, m_w_out, m_norm_final, v_norm_in, v_w_in, v_conv_w, v_attn_sinks, v_norm_conv_out, v_norm_attn_out, v_w_out, v_norm_final):
    chip = 2 * lax.axis_index("x") + lax.axis_index("y")
    xs, target = x[0], loss_target[0]
    norm_final2 = norm_final.reshape(1, D_MODEL)
    w_in_t, m_w_in_t, v_w_in_t = w_in[0].T, m_w_in[0].T, v_w_in[0].T

    wt, wo, cw4 = _ag_weights(w_in_t, w_out[0], conv_w[0])
    cw = jnp.transpose(cw4, (1, 0, 2)).reshape(3, D_CONV)

    h, pc, q, kv, ga = _fwd_in(xs, norm_in, wt)
    oc = _conv_fwd(pc, cw, norm_conv_out)
    ya, oa = _attn_fwd(q, kv, ga, attn_sinks, norm_attn_out)
    dx2, doc, doa, gwo, gnf, loss_lanes = _out_proj_loss(xs, oc, oa, wo, norm_final2, target)

    dpc, gnc, gcw = _conv_bwd(pc, doc, cw, norm_conv_out)
    dqg, gna, gsink, gwo_sh = _attn_bwd(q, kv, ga, ya, doa, attn_sinks, norm_attn_out, gwo)

    last = jnp.zeros((1, D_MODEL), F32).at[:, 0:N_HEADS].set(gsink[:, 0:N_HEADS]).at[0, N_HEADS].set(jnp.sum(loss_lanes))
    small = jnp.concatenate([jnp.zeros((1, D_MODEL), F32), gnc, gna, gnf, gcw[0:3], last], axis=0)
    grad_x, small_sum, gwt_sh = _bwd_in(dpc, dqg, h, wt, xs, norm_in, dx2, small)

    loss = small_sum[7, N_HEADS]
    g_norm_in, g_norm_conv, g_norm_attn = small_sum[0:1], small_sum[1:2], small_sum[2:3]
    g_norm_final = small_sum[3]
    g_conv_w = lax.dynamic_slice(small_sum[4:7], (0, chip * (D_CONV // N_CHIPS)), (3, D_CONV // N_CHIPS))[None]
    g_sinks = small_sum[7:8, 0:N_HEADS]

    def adam(name, w, g, m, v, rows):
        shape = w.shape
        two_d = (-1, shape[-1])
        out = _adamw(name, w.reshape(two_d), g.reshape(two_d), m.reshape(two_d), v.reshape(two_d), rows)
        return tuple(o.reshape(shape) for o in out)

    weights = (norm_in, w_in, conv_w, attn_sinks, norm_conv_out, norm_attn_out, w_out, norm_final)
    grads = (g_norm_in, None, g_conv_w, g_sinks, g_norm_conv, g_norm_attn, gwo_sh[None], g_norm_final)
    moments_m = (m_norm_in, m_w_in, m_conv_w, m_attn_sinks, m_norm_conv_out, m_norm_attn_out, m_w_out, m_norm_final)
    moments_v = (v_norm_in, v_w_in, v_conv_w, v_attn_sinks, v_norm_conv_out, v_norm_attn_out, v_w_out, v_norm_final)
    rows = (1, None, 3, 1, 1, 1, 128, 1)
    names = ("norm_in", "w_in", "conv_w", "attn_sinks", "norm_conv_out", "norm_attn_out", "w_out", "norm_final")
    updates = []
    for args in zip(names, weights, grads, moments_m, moments_v, rows):
        if args[0] == "w_in":
            out_t = _adamw("w_in", w_in_t, gwt_sh, m_w_in_t, v_w_in_t, 200)
            updates.append(tuple(o.T[None] for o in out_t))
        else:
            updates.append(adam(*args))
    grads_out, deltas, new_m, new_v = zip(*updates)
    return (loss, grad_x[None], *grads_out, *deltas, *new_m, *new_v)
```

```python
import jax
import jax.numpy as jnp
from jax import lax
from jax.experimental import pallas as pl
from jax.experimental.pallas import tpu as pltpu

F32 = jnp.float32
BF16 = jnp.bfloat16
MESH = pl.DeviceIdType.MESH

D_MODEL = 1024
D_CONV = 1024
D_ATTN = 1024
D_KV = 128
D_QG = 2 * D_ATTN + 2 * D_KV
D_MIX = D_CONV + D_ATTN
D_PC = 4 * D_CONV
D_IN_PROJ = D_PC + 2 * D_ATTN + 2 * D_KV
ROW_Q = D_PC
ROW_KV = ROW_Q + D_ATTN
ROW_GA = ROW_KV + 2 * D_KV
N_HEADS = 16
HEAD_DIM = 64
HEADS_PER_KV = 8
BLK = 128
N_CHIPS = 4
RMS_EPS = 1e-5
SCALE = HEAD_DIM ** -0.5
SLOPES = tuple(2.0 ** (-8.0 * (h + 1) / N_HEADS) for h in range(N_HEADS))

ADAM_LR, ADAM_B1, ADAM_B2, ADAM_EPS, ADAM_WD, ADAM_STEP = 0.001, 0.9, 0.999, 1e-08, 0.01, 10

TOK_TILE = 256
MIB = 1 << 20


def _params(vmem_mib, semantics=None):
    return pltpu.CompilerParams(dimension_semantics=semantics, vmem_limit_bytes=vmem_mib * MIB)


def _nn(a, b):
    return jnp.dot(a, b, preferred_element_type=F32)


def _nt(a, b):
    return lax.dot_general(a, b, (((1,), (1,)), ((), ())), preferred_element_type=F32)


def _tn(a, b):
    return lax.dot_general(a, b, (((0,), (0,)), ((), ())), preferred_element_type=F32)


def _rstd(v):
    return lax.rsqrt(jnp.mean(v * v, axis=-1, keepdims=True) + RMS_EPS)


def _rms_bwd(g, xhat, rstd):
    return rstd * (g - xhat * jnp.mean(g * xhat, axis=-1, keepdims=True))


def _silu_and_grad(g):
    s = jax.nn.sigmoid(g)
    return g * s, s * (1.0 + g * (1.0 - s))


def _resident(shape):
    return pl.BlockSpec(shape, lambda *_: (0,) * len(shape), pipeline_mode=pl.Buffered(1))


def _xor(a, b):
    return a + b - 2 * a * b


def _cast_rows(src_ref, dst_ref, rows=32):
    def step(i, carry):
        sl = pl.ds(pl.multiple_of(i * rows, rows), rows)
        dst_ref[sl, :] = src_ref[sl, :].astype(dst_ref.dtype)
        return carry
    lax.fori_loop(0, src_ref.shape[0] // rows, step, 0)


def _ag_scratch(shard_shape):
    rows, width = shard_shape
    return [pltpu.VMEM((rows, width), F32), pltpu.VMEM((rows, width), BF16), pltpu.VMEM((3, rows // 2, width), BF16),
            pltpu.SemaphoreType.DMA((6,)), pltpu.SemaphoreType.DMA((6,)), pltpu.SemaphoreType.DMA((4,))]


def _ag_stages(sh_ref, out, f32_buf, own, land, send_sems, recv_sems, local_sems):
    rows = sh_ref.shape[0]
    half = rows // 2
    x, y, c = lax.axis_index("x"), lax.axis_index("y"), lax.axis_index("c")
    j = 2 * x + y
    p1 = (_xor(x, c), _xor(y, 1 - c), c)
    p2 = (_xor(x, 1 - c), _xor(y, c), c)
    sib = (x, y, 1 - c)
    j1 = 2 * p1[0] + p1[1]
    j2 = 2 * p2[0] + p2[1]
    j3 = 3 - j

    def rows_of(chip, hf):
        return out.at[pl.ds(pl.multiple_of(chip * rows + hf * half, 16), half), :]

    def rcopy(k, src, dst, to):
        return pltpu.make_async_remote_copy(src_ref=src, dst_ref=dst, send_sem=send_sems.at[k],
                                            recv_sem=recv_sems.at[k], device_id=to, device_id_type=MESH)

    my_half = own.at[pl.ds(pl.multiple_of(c * half, 16), half), :]
    hop1 = rcopy(0, my_half, land.at[0], p1)
    hop2_own = rcopy(1, my_half, land.at[1], p2)
    hop2_fwd = rcopy(2, land.at[0], land.at[2], p2)
    swaps = [rcopy(3 + s, land.at[s], rows_of(chip, c), sib) for s, chip in enumerate((j1, j2, j3))]
    keeps = [pltpu.make_async_copy(land.at[s], rows_of(chip, c), local_sems.at[1 + s]) for s, chip in enumerate((j1, j2, j3))]
    load = pltpu.make_async_copy(sh_ref, f32_buf, local_sems.at[0])
    own_out = pltpu.make_async_copy(own, out.at[pl.ds(pl.multiple_of(j * rows, 16), rows), :], local_sems.at[0])

    def stage_send():
        load.start()
        load.wait()
        _cast_rows(f32_buf, own)
        own_out.start()
        hop1.start()

    def stage_forward():
        hop1.wait_recv()
        hop2_own.start()
        hop2_fwd.start()
        swaps[0].start()
        keeps[0].start()

    def stage_publish():
        hop2_own.wait_recv()
        swaps[1].start()
        keeps[1].start()
        hop2_fwd.wait_recv()
        swaps[2].start()
        keeps[2].start()

    def stage_drain():
        for s, chip in enumerate((j2, j1, j3)):
            rcopy(3 + s, my_half, rows_of(chip, 1 - c), sib).wait_recv()
        for cp in [hop1, hop2_own, hop2_fwd] + swaps:
            cp.wait_send()
        for cp in [own_out] + keeps:
            cp.wait()

    return stage_send, stage_forward, stage_publish, stage_drain


def _ag_weights(wt_sh, cw_sh):
    def body(wt_ref, cw_ref, wt_out, cw_out, *scratch):
        cw_send, cw_recv, cw_local = scratch[-3:]
        x, y, c = lax.axis_index("x"), lax.axis_index("y"), lax.axis_index("c")
        j = 2 * x + y
        p1 = (_xor(x, c), _xor(y, 1 - c), c)
        p2 = (_xor(x, 1 - c), _xor(y, c), c)
        j1 = 2 * p1[0] + p1[1]

        def cw_copy(k, chip, to):
            src = cw_ref if k != 2 else cw_out.at[chip]
            return pltpu.make_async_remote_copy(src_ref=src, dst_ref=cw_out.at[chip], send_sem=cw_send.at[k],
                                                recv_sem=cw_recv.at[k], device_id=to, device_id_type=MESH)

        mine = pltpu.make_async_copy(cw_ref, cw_out.at[j], cw_local.at[0])
        mine.start()
        first = cw_copy(0, j, p1)
        first.start()
        stages = _ag_stages(wt_ref, wt_out, *scratch[:-3])
        stages[0]()
        first.wait_recv()
        second = [cw_copy(1, j, p2), cw_copy(2, j1, p2)]
        for cp in second:
            cp.start()
        for stage in stages[1:]:
            stage()
        for cp in second:
            cp.wait_recv()
        for cp in [first] + second:
            cp.wait_send()
        mine.wait()

    any_spec = pl.BlockSpec(memory_space=pl.ANY)
    dma = pltpu.SemaphoreType.DMA
    return pl.pallas_call(
        body, name="ag_weights",
        out_shape=(jax.ShapeDtypeStruct((N_CHIPS * wt_sh.shape[0], wt_sh.shape[1]), BF16),
                   jax.ShapeDtypeStruct((N_CHIPS,) + cw_sh.shape, cw_sh.dtype)),
        in_specs=[any_spec, any_spec], out_specs=(any_spec, any_spec),
        scratch_shapes=_ag_scratch(wt_sh.shape) + [dma((3,)), dma((3,)), dma((1,))],
        compiler_params=_params(32),
    )(wt_sh, cw_sh)


def _fwd_in(x, norm_in, wt, wo_sh):
    seq = x.shape[0]
    tile = TOK_TILE
    n_tiles = seq // tile
    stage_steps = (0, n_tiles // 4, (5 * n_tiles) // 8, n_tiles - 1)

    def body(x_ref, g_ref, wt_ref, wo_ref, h_ref, pc_ref, q_ref, kv_ref, ga_ref, wo_out, *ag_scratch):
        stages = _ag_stages(wo_ref, wo_out, *ag_scratch)
        for at, stage in zip(stage_steps[:-1], stages[:-1]):
            pl.when(pl.program_id(0) == at)(stage)
        xv = x_ref[...]
        h = (xv * _rstd(xv) * g_ref[...]).astype(BF16)
        h_ref[...] = h
        for blk in range(D_PC // D_CONV):
            pc_ref[:, blk * D_CONV:(blk + 1) * D_CONV] = _nt(h, wt_ref[blk * D_CONV:(blk + 1) * D_CONV, :])
        q_ref[...] = _nt(h, wt_ref[ROW_Q:ROW_KV, :])
        kv_ref[...] = _nt(h, wt_ref[ROW_KV:ROW_GA, :])
        ga_ref[...] = _nt(h, wt_ref[ROW_GA:D_IN_PROJ, :])
        pl.when(pl.program_id(0) == stage_steps[-1])(stages[-1])

    def row(width):
        return pl.BlockSpec((tile, width), lambda i: (i, 0))

    any_spec = pl.BlockSpec(memory_space=pl.ANY)
    return pl.pallas_call(
        body, name="fwd_in", grid=(n_tiles,),
        out_shape=(jax.ShapeDtypeStruct((seq, D_MODEL), BF16), jax.ShapeDtypeStruct((seq, D_PC), F32),
                   jax.ShapeDtypeStruct((seq, D_ATTN), F32), jax.ShapeDtypeStruct((seq, 2 * D_KV), F32),
                   jax.ShapeDtypeStruct((seq, D_ATTN), F32),
                   jax.ShapeDtypeStruct((N_CHIPS * wo_sh.shape[0], wo_sh.shape[1]), BF16)),
        in_specs=[row(D_MODEL), _resident((1, D_MODEL)), _resident(wt.shape), any_spec],
        out_specs=(row(D_MODEL), row(D_PC), row(D_ATTN), row(2 * D_KV), row(D_ATTN), any_spec),
        scratch_shapes=_ag_scratch(wo_sh.shape),
        compiler_params=_params(52, ("arbitrary",)),
    )(x, norm_in, wt, wo_sh)


def _conv_core(pc_ref, zbuf, cw_ref):
    tile = pc_ref.shape[0]
    cb = pc_ref[:, 0:D_CONV]
    cc = pc_ref[:, D_CONV:2 * D_CONV]
    cu = pc_ref[:, 2 * D_CONV:3 * D_CONV]
    z = cc * cu
    zbuf[8:tile + 8, :] = z
    z1 = zbuf[7:tile + 7, :]
    z2 = zbuf[6:tile + 6, :]
    conv = cw_ref[0:1, :] * z2 + cw_ref[1:2, :] * z1 + cw_ref[2:3, :] * z
    return cb, cc, cu, z, z1, z2, conv


def _conv_fwd(pc, conv_w, norm_conv_out):
    seq = pc.shape[0]
    tile = TOK_TILE

    def body(pc_ref, cw_ref, gn_ref, oc_ref, zbuf):
        @pl.when(pl.program_id(0) == 0)
        def _():
            zbuf[0:8, :] = jnp.zeros((8, D_CONV), F32)

        cb, _, _, _, _, _, conv = _conv_core(pc_ref, zbuf, cw_ref)
        yc = cb * conv
        silu, _ = _silu_and_grad(pc_ref[:, 3 * D_CONV:4 * D_CONV])
        oc_ref[...] = (yc * _rstd(yc) * gn_ref[...] * silu).astype(BF16)
        zbuf[0:8, :] = zbuf[tile:tile + 8, :]

    return pl.pallas_call(
        body, name="conv_fwd", grid=(seq // tile,),
        out_shape=jax.ShapeDtypeStruct((seq, D_CONV), BF16),
        in_specs=[pl.BlockSpec((tile, D_PC), lambda i: (i, 0)), _resident(conv_w.shape), _resident((1, D_CONV))],
        out_specs=pl.BlockSpec((tile, D_CONV), lambda i: (i, 0)),
        scratch_shapes=[pltpu.VMEM((tile + 8, D_CONV), F32)],
        compiler_params=_params(40, ("arbitrary",)),
    )(pc, conv_w, norm_conv_out)


def _band_geometry(block_index):
    qi = lax.broadcasted_iota(jnp.int32, (BLK, BLK), 0)
    kp = lax.broadcasted_iota(jnp.int32, (BLK, BLK), 1)
    use_cur = kp <= qi
    dist = jnp.where(use_cur, qi - kp, qi - kp + BLK).astype(F32)
    valid = use_cur | (block_index > 0)
    return use_cur, dist, valid


def _block_diag(cur, prev, group):
    lane = lax.broadcasted_iota(jnp.int32, cur.shape, 1)

    def halves(t):
        other = pltpu.roll(t, 64, 1)
        lo, hi = (t, other) if group == 0 else (other, t)
        return jnp.where(lane < 64, lo, 0.0), jnp.where(lane >= 64, hi, 0.0)

    return jnp.concatenate(halves(cur) + halves(prev), axis=0).astype(BF16)


def _merge(s4, use_cur):
    return (jnp.where(use_cur, s4[:, 0:BLK], s4[:, 2 * BLK:3 * BLK]),
            jnp.where(use_cur, s4[:, BLK:2 * BLK], s4[:, 3 * BLK:4 * BLK]))


def _split(a, b, use_cur):
    return jnp.concatenate([jnp.where(use_cur, a, 0.0), jnp.where(use_cur, b, 0.0),
                            jnp.where(use_cur, 0.0, a), jnp.where(use_cur, 0.0, b)], axis=1)


def _softmax_head(s, head, sink, dist, valid):
    sc = jnp.where(valid, s - SLOPES[head] * dist, -jnp.inf)
    m = jnp.maximum(jnp.max(sc, axis=-1, keepdims=True), sink)
    p = jnp.exp(sc - m)
    es = jnp.exp(sink - m)
    inv = 1.0 / (jnp.sum(p, axis=-1, keepdims=True) + es)
    return p * inv, es * inv


def _attn_probs(q_ref, kvc_ref, kvp_ref, sink_ref, geometry):
    use_cur, dist, valid = geometry
    groups = range(N_HEADS // HEADS_PER_KV)
    kbd = [_block_diag(kvc_ref[:, 0:D_KV], kvp_ref[:, 0:D_KV], g) for g in groups]
    vbd = [_block_diag(kvc_ref[:, D_KV:2 * D_KV], kvp_ref[:, D_KV:2 * D_KV], g) for g in groups]
    qps = [(q_ref[:, j * 128:(j + 1) * 128] * SCALE).astype(BF16) for j in range(N_HEADS // 2)]
    scores = []
    for j, qp in enumerate(qps):
        scores += _merge(_nt(qp, kbd[j // 4]), use_cur)
    sinks = [sink_ref[0, h] for h in range(N_HEADS)]
    scores = [jnp.where(valid, s - SLOPES[h] * dist, -jnp.inf) for h, s in enumerate(scores)]
    maxes = [jnp.maximum(jnp.max(s, axis=-1, keepdims=True), sinks[h]) for h, s in enumerate(scores)]
    exps = [jnp.exp(s - m) for s, m in zip(scores, maxes)]
    sink_exps = [jnp.exp(sinks[h] - m) for h, m in enumerate(maxes)]
    invs = [1.0 / (jnp.sum(e, axis=-1, keepdims=True) + se) for e, se in zip(exps, sink_exps)]
    probs = [e * inv for e, inv in zip(exps, invs)]
    sink_probs = [se * inv for se, inv in zip(sink_exps, invs)]
    return qps, kbd, vbd, probs, sink_probs


def _kv_specs(n_blocks, reverse):
    def blk(i):
        return (n_blocks - 1 - i) if reverse else i
    cur = pl.BlockSpec((BLK, 2 * D_KV), lambda i: (blk(i), 0))
    prev = pl.BlockSpec((BLK, 2 * D_KV), lambda i: (jnp.maximum(blk(i) - 1, 0), 0))
    return cur, prev


def _attn_fwd(q, kv, ga, sinks, norm_attn_out):
    seq = q.shape[0]
    n_blocks = seq // BLK

    def body(q_ref, kvc_ref, kvp_ref, ga_ref, sink_ref, gn_ref, ya_ref, oa_ref):
        geometry = _band_geometry(pl.program_id(0))
        _, _, vbd, probs, _ = _attn_probs(q_ref, kvc_ref, kvp_ref, sink_ref, geometry)
        p4s = [_split(probs[2 * j], probs[2 * j + 1], geometry[0]).astype(BF16) for j in range(N_HEADS // 2)]
        ya = jnp.concatenate([_nn(p4, vbd[j // 4]) for j, p4 in enumerate(p4s)], axis=1)
        ya_ref[...] = ya
        silu, _ = _silu_and_grad(ga_ref[...])
        oa_ref[...] = (ya * _rstd(ya) * gn_ref[...] * silu).astype(BF16)

    row = pl.BlockSpec((BLK, D_ATTN), lambda i: (i, 0))
    kv_cur, kv_prev = _kv_specs(n_blocks, reverse=False)
    return pl.pallas_call(
        body, name="attn_fwd", grid=(n_blocks,),
        out_shape=(jax.ShapeDtypeStruct((seq, D_ATTN), F32), jax.ShapeDtypeStruct((seq, D_ATTN), BF16)),
        in_specs=[row, kv_cur, kv_prev, row, pl.BlockSpec(memory_space=pltpu.SMEM), _resident((1, D_ATTN))],
        out_specs=(row, row),
        compiler_params=_params(32, ("arbitrary",)),
    )(q, kv, kv, ga, sinks, norm_attn_out)


def _out_proj_loss(x, oc, oa, wo, norm_final, target):
    seq = x.shape[0]
    tile = TOK_TILE

    def body(x_ref, oc_ref, oa_ref, wo_ref, gf_ref, t_ref, dx2_ref, doc_ref, doa_ref, gwo_ref, gnf_ref, loss_ref):
        @pl.when(pl.program_id(0) == 0)
        def _():
            gwo_ref[...] = jnp.zeros_like(gwo_ref)
            gnf_ref[...] = jnp.zeros_like(gnf_ref)
            loss_ref[...] = jnp.zeros_like(loss_ref)

        oc, oa = oc_ref[...], oa_ref[...]
        x2 = x_ref[...] + _nn(oc, wo_ref[0:D_CONV, :]) + _nn(oa, wo_ref[D_CONV:D_MIX, :])
        r = _rstd(x2)
        xhat = x2 * r
        err = xhat * gf_ref[...] - t_ref[...]
        loss_ref[...] += jnp.sum(err * err, axis=0, keepdims=True) * (0.5 / D_MODEL)
        dy = err * (1.0 / D_MODEL)
        gnf_ref[...] += jnp.sum(dy * xhat, axis=0, keepdims=True)
        dx2 = _rms_bwd(dy * gf_ref[...], xhat, r)
        dx2_ref[...] = dx2
        db = dx2.astype(BF16)
        doc_ref[...] = _nt(db, wo_ref[0:D_CONV, :])
        doa_ref[...] = _nt(db, wo_ref[D_CONV:D_MIX, :])
        gwo_ref[0:D_CONV, :] += _tn(oc, db)
        gwo_ref[D_CONV:D_MIX, :] += _tn(oa, db)

    row = pl.BlockSpec((tile, D_MODEL), lambda i: (i, 0))
    vec = pl.BlockSpec((1, D_MODEL), lambda i: (0, 0))
    return pl.pallas_call(
        body, name="out_proj_loss", grid=(seq // tile,),
        out_shape=(jax.ShapeDtypeStruct((seq, D_MODEL), F32), jax.ShapeDtypeStruct((seq, D_CONV), F32),
                   jax.ShapeDtypeStruct((seq, D_ATTN), F32), jax.ShapeDtypeStruct((D_MIX, D_MODEL), F32),
                   jax.ShapeDtypeStruct((1, D_MODEL), F32), jax.ShapeDtypeStruct((1, D_MODEL), F32)),
        in_specs=[row, row, row, _resident(wo.shape), _resident((1, D_MODEL)), row],
        out_specs=(row, row, row, pl.BlockSpec((D_MIX, D_MODEL), lambda i: (0, 0)), vec, vec),
        compiler_params=_params(48, ("arbitrary",)),
    )(x, oc, oa, wo, norm_final, target)


def _conv_bwd(pc, doc, conv_w, norm_conv_out):
    seq = pc.shape[0]
    tile = TOK_TILE
    n_tiles = seq // tile

    def body(pc_ref, hcc_ref, hcu_ref, doc_ref, cw_ref, gn_ref, dpc_ref, gnc_ref, gcw_ref, zbuf, dbuf):
        step = pl.program_id(0)

        @pl.when(step == 0)
        def _():
            gnc_ref[...] = jnp.zeros_like(gnc_ref)
            gcw_ref[...] = jnp.zeros_like(gcw_ref)
            dbuf[tile:tile + 8, :] = jnp.zeros((8, D_CONV), F32)

        is_first_tile = step == n_tiles - 1
        zbuf[0:8, :] = jnp.where(is_first_tile, 0.0, hcc_ref[...] * hcu_ref[...])
        cb, cc, cu, z, z1, z2, conv = _conv_core(pc_ref, zbuf, cw_ref)
        gc = pc_ref[:, 3 * D_CONV:4 * D_CONV]
        silu, dsilu = _silu_and_grad(gc)
        yc = cb * conv
        r = _rstd(yc)
        xhat = yc * r
        do = doc_ref[...]
        dn = do * silu
        dpc_ref[:, 3 * D_CONV:4 * D_CONV] = (do * (xhat * gn_ref[...]) * dsilu).astype(BF16)
        gnc_ref[...] += jnp.sum(dn * xhat, axis=0, keepdims=True)
        dyc = _rms_bwd(dn * gn_ref[...], xhat, r)
        dpc_ref[:, 0:D_CONV] = (dyc * conv).astype(BF16)
        dconv = dyc * cb
        gcw_ref[0:1, :] += jnp.sum(dconv * z2, axis=0, keepdims=True)
        gcw_ref[1:2, :] += jnp.sum(dconv * z1, axis=0, keepdims=True)
        gcw_ref[2:3, :] += jnp.sum(dconv * z, axis=0, keepdims=True)
        dbuf[0:tile, :] = dconv
        dz = cw_ref[2:3, :] * dconv + cw_ref[1:2, :] * dbuf[1:tile + 1, :] + cw_ref[0:1, :] * dbuf[2:tile + 2, :]
        dpc_ref[:, D_CONV:2 * D_CONV] = (dz * cu).astype(BF16)
        dpc_ref[:, 2 * D_CONV:3 * D_CONV] = (dz * cc).astype(BF16)
        dbuf[tile:tile + 8, :] = dbuf[0:8, :]

    def rev(i):
        return n_tiles - 1 - i

    def halo(col_block):
        return pl.BlockSpec((8, D_CONV), lambda i: (jnp.maximum(rev(i) * (tile // 8) - 1, 0), col_block))

    return pl.pallas_call(
        body, name="conv_bwd", grid=(n_tiles,),
        out_shape=(jax.ShapeDtypeStruct((seq, D_PC), BF16), jax.ShapeDtypeStruct((1, D_CONV), F32),
                   jax.ShapeDtypeStruct((8, D_CONV), F32)),
        in_specs=[pl.BlockSpec((tile, D_PC), lambda i: (rev(i), 0)), halo(1), halo(2),
                  pl.BlockSpec((tile, D_CONV), lambda i: (rev(i), 0)), _resident(conv_w.shape), _resident((1, D_CONV))],
        out_specs=(pl.BlockSpec((tile, D_PC), lambda i: (rev(i), 0)), pl.BlockSpec((1, D_CONV), lambda i: (0, 0)),
                   pl.BlockSpec((8, D_CONV), lambda i: (0, 0))),
        scratch_shapes=[pltpu.VMEM((tile + 8, D_CONV), F32), pltpu.VMEM((tile + 8, D_CONV), F32)],
        compiler_params=_params(48, ("arbitrary",)),
    )(pc, pc, pc, doc, conv_w, norm_conv_out)


def _attn_bwd(q, kv, ga, ya, doa, sinks, norm_attn_out, gwo):
    seq = q.shape[0]
    n_blocks = seq // BLK
    stage_steps = (0, n_blocks // 4, n_blocks // 2, (3 * n_blocks) // 4, n_blocks - 1)

    def body(q_ref, kvc_ref, kvp_ref, ga_ref, ya_ref, doa_ref, sink_ref, gn_ref, gwo_ref,
             dqg_ref, gna_ref, gs_ref, gwo_sh, carry, dya_buf, *rs_scratch):
        step = pl.program_id(0)

        @pl.when(step == 0)
        def _():
            gna_ref[...] = jnp.zeros_like(gna_ref)
            gs_ref[...] = jnp.zeros_like(gs_ref)
            carry[...] = jnp.zeros_like(carry)

        ya = ya_ref[...]
        r = _rstd(ya)
        xhat = ya * r
        silu, dsilu = _silu_and_grad(ga_ref[...])
        do = doa_ref[...]
        dn = do * silu
        dqg_ref[:, D_ATTN:2 * D_ATTN] = (do * (xhat * gn_ref[...]) * dsilu).astype(BF16)
        gna_ref[...] += jnp.sum(dn * xhat, axis=0, keepdims=True)
        dya_buf[...] = _rms_bwd(dn * gn_ref[...], xhat, r).astype(BF16)

        geometry = _band_geometry(n_blocks - 1 - step)
        use_cur = geometry[0]
        lane = lax.broadcasted_iota(jnp.int32, (BLK, 128), 1)
        sink_lane = lax.broadcasted_iota(jnp.int32, (1, 128), 1)
        gsink = jnp.zeros((1, 128), F32)

        def fold(bd):
            return (jnp.where(lane < 64, bd[0:BLK], 0.0) + jnp.where(lane >= 64, bd[BLK:2 * BLK], 0.0),
                    jnp.where(lane < 64, bd[2 * BLK:3 * BLK], 0.0) + jnp.where(lane >= 64, bd[3 * BLK:4 * BLK], 0.0))

        pairs = range(N_HEADS // 2)
        qps, kbd, vbd, probs, sink_probs = _attn_probs(q_ref, kvc_ref, kvp_ref, sink_ref, geometry)
        dyps = [dya_buf[:, j * 128:(j + 1) * 128] for j in pairs]
        dps = []
        for j in pairs:
            dps += _merge(_nt(dyps[j], vbd[j // 4]), use_cur)
        deltas = [jnp.sum(p * dp, axis=-1, keepdims=True) for p, dp in zip(probs, dps)]
        dss = [p * (dp - delta) for p, dp, delta in zip(probs, dps, deltas)]
        for h in range(N_HEADS):
            dsink = -jnp.sum(sink_probs[h] * deltas[h], axis=0, keepdims=True)
            gsink = gsink + jnp.where(sink_lane == h, dsink, 0.0)
        gs_ref[...] += gsink
        ds4s = [_split(dss[2 * j], dss[2 * j + 1], use_cur).astype(BF16) for j in pairs]
        p4s = [_split(probs[2 * j], probs[2 * j + 1], use_cur).astype(BF16) for j in pairs]
        dqg_ref[:, 0:D_ATTN] = jnp.concatenate([_nn(ds4s[j], kbd[j // 4]) * SCALE for j in pairs], axis=1).astype(BF16)
        sums = []
        for group in range(N_HEADS // HEADS_PER_KV):
            acc = [jnp.zeros((BLK, 128), F32) for _ in range(4)]
            for j in range(group * 4, group * 4 + 4):
                for slot, part in enumerate(fold(_tn(ds4s[j], qps[j])) + fold(_tn(p4s[j], dyps[j]))):
                    acc[slot] = acc[slot] + part
            sums.append([a + pltpu.roll(a, 64, 1) for a in acc])
        dk_cur, dk_prev, dv_cur, dv_prev = (jnp.where(lane < 64, a, b) for a, b in zip(sums[0], sums[1]))
        dqg_ref[:, 2 * D_ATTN:2 * D_ATTN + 2 * D_KV] = (jnp.concatenate([dk_cur, dv_cur], axis=1) + carry[...]).astype(BF16)
        carry[...] = jnp.concatenate([dk_prev, dv_prev], axis=1)

        for at, stage in zip(stage_steps, _rs_wout_stages(gwo_ref, gwo_sh, *rs_scratch)):
            pl.when(step == at)(stage)

    row = pl.BlockSpec((BLK, D_ATTN), lambda i: (n_blocks - 1 - i, 0))
    kv_cur, kv_prev = _kv_specs(n_blocks, reverse=True)
    any_spec = pl.BlockSpec(memory_space=pl.ANY)
    return pl.pallas_call(
        body, name="attn_bwd", grid=(n_blocks,),
        out_shape=(jax.ShapeDtypeStruct((seq, D_QG), BF16), jax.ShapeDtypeStruct((1, D_ATTN), F32),
                   jax.ShapeDtypeStruct((1, 128), F32), jax.ShapeDtypeStruct((gwo.shape[0] // N_CHIPS, gwo.shape[1]), F32)),
        in_specs=[row, kv_cur, kv_prev, row, row, row, pl.BlockSpec(memory_space=pltpu.SMEM), _resident((1, D_ATTN)),
                  any_spec],
        out_specs=(pl.BlockSpec((BLK, D_QG), lambda i: (n_blocks - 1 - i, 0)),
                   pl.BlockSpec((1, D_ATTN), lambda i: (0, 0)), pl.BlockSpec((1, 128), lambda i: (0, 0)), any_spec),
        scratch_shapes=[pltpu.VMEM((BLK, 2 * D_KV), F32), pltpu.VMEM((BLK, D_ATTN), BF16)] + _rs_wout_scratch(gwo.shape),
        compiler_params=_params(44, ("arbitrary",)),
    )(q, kv, kv, ga, ya, doa, sinks, norm_attn_out, gwo)


GBLK = 256
GSUB = 64
PAIR_RING = 4
LAG_PAIR, LAG_HOP1, LAG_HOP2 = 1, 7, 14


def _bwd_in(dpc, dqg, h, wt, x, norm_in, dx2, small):
    seq = x.shape[0]
    n_blk = D_IN_PROJ // GBLK
    per_chip = n_blk // N_CHIPS
    n_slots = (n_blk + 1) // 2
    n_sub = GBLK // GSUB
    chip_rows = D_IN_PROJ // N_CHIPS
    tile = TOK_TILE
    n_tiles = seq // tile
    n_steps = n_blk + max(n_tiles, LAG_HOP2)
    chunk = min(seq, 512)
    blk_q, blk_kv, blk_ga = ROW_Q // GBLK, ROW_KV // GBLK, ROW_GA // GBLK

    def block_of(i):
        k = i % N_CHIPS
        robin = per_chip * ((k % 2) * 2 + k // 2) + i // N_CHIPS
        if isinstance(i, int):
            return robin if i < per_chip * N_CHIPS else i
        return jnp.where(i < per_chip * N_CHIPS, robin, i)

    def owner_of(i):
        return (i // N_CHIPS) % 2

    def slot_of(i):
        return (i // (2 * N_CHIPS)) * N_CHIPS + i % N_CHIPS

    def body(dpc_ref, dqg_ref, wt_ref, h_ref, x_ref, g_ref, dx2_ref, small_ref, gx_ref, small_sum, gwt_sh,
             dh_acc, gni, keep, pbuf, xbuf, land, land2, small_land,
             pair_send, pair_recv, h1_send, h1_recv, h2_send, h2_recv, sw_send, sw_recv, sm_send, sm_recv, out_sem):
        step = pl.program_id(0)
        x_i, y_i, c = lax.axis_index("x"), lax.axis_index("y"), lax.axis_index("c")
        me = 4 * x_i + 2 * y_i + c
        j = 2 * x_i + y_i
        pa = (_xor(x_i, 1 - c), _xor(y_i, c), c)
        pb = (_xor(x_i, c), _xor(y_i, 1 - c), c)
        sib = (x_i, y_i, 1 - c)
        ja = 2 * pa[0] + pa[1]
        jb = 2 * pb[0] + pb[1]
        jd = 3 - j

        def remote(src, dst, send, recv, to):
            return pltpu.make_async_remote_copy(src_ref=src, dst_ref=dst, send_sem=send, recv_sem=recv,
                                                device_id=to, device_id_type=MESH)

        def piece(ref, slot, u, n):
            return ref.at[slot, pl.ds(u * GSUB, n * GSUB), :]

        def chip_rows_at(ref, local, n):
            return ref.at[pl.ds(pl.multiple_of(local, GSUB), n * GSUB), :]

        def pair_copy(i):
            slot = slot_of(i)
            return remote(pbuf.at[i % PAIR_RING], land.at[slot], pair_send.at[slot], pair_recv.at[slot], sib)

        def h1_copy(slot, u, n):
            k = slot * n_sub + u
            return remote(piece(xbuf, slot, u, n), piece(xbuf, slot, u, n), h1_send.at[k], h1_recv.at[k], pa)

        def h2_copy(slot, u, n, local):
            k = slot * n_sub + u
            return remote(piece(xbuf, slot, u, n), chip_rows_at(land2, local, n), h2_send.at[k], h2_recv.at[k], pb)

        def sw_copy(slot, u, n, local):
            k = slot * n_sub + u
            return remote(piece(keep, slot, u, n), chip_rows_at(gwt_sh, local, n), sw_send.at[k], sw_recv.at[k], sib)

        def owned(i):
            return (i >= 0) & (i < n_blk) & (owner_of(i) == c)

        def chip_of(blk, u):
            row = blk * GBLK + u * GSUB
            chip = row // chip_rows
            return chip, row - chip * chip_rows

        def pieces(blk):
            first, local = chip_of(blk, 0)
            whole = first == chip_of(blk, n_sub - 1)[0]
            if isinstance(blk, int):
                return [(True, 0, n_sub, first, local)] if whole else [(True, u, 1) + chip_of(blk, u) for u in range(n_sub)]
            return [(whole, 0, n_sub, first, local)] + [(jnp.logical_not(whole), u, 1) + chip_of(blk, u) for u in range(n_sub)]

        @pl.when(step == 0)
        def _():
            dh_acc[...] = jnp.zeros_like(dh_acc)
            gni[...] = jnp.zeros_like(gni)

        @pl.when(step < n_blk)
        def _():
            from_pc = block_of(step) < blk_q
            block = _tn(jnp.where(from_pc, dpc_ref[...], dqg_ref[...]), h_ref[...])
            for t in range(0, seq, chunk):
                d = jnp.where(from_pc, dpc_ref[t:t + chunk, :], dqg_ref[t:t + chunk, :])
                dh_acc[t:t + chunk, :] += _nn(d, wt_ref[...])

            @pl.when(owner_of(step) == c)
            def _():
                keep[slot_of(step)] = block

            @pl.when(owner_of(step) != c)
            def _():
                @pl.when(step >= 2 * PAIR_RING)
                def _():
                    pair_copy(step - 2 * PAIR_RING).wait_send()
                pbuf[step % PAIR_RING] = block.astype(BF16)
                pair_copy(step).start()

        i1 = step - LAG_PAIR

        @pl.when(owned(i1))
        def _():
            slot = slot_of(i1)
            pair_copy(i1).wait_recv()
            _accumulate(keep.at[slot], land.at[slot])
            for cond, u, n, chip, _ in pieces(block_of(i1)):
                @pl.when(cond & ((chip == ja) | (chip == jd)))
                def _(u=u, n=n):
                    _cast_rows(piece(keep, slot, u, n), piece(xbuf, slot, u, n))
                    h1_copy(slot, u, n).start()

        i2 = step - LAG_HOP1

        @pl.when(owned(i2))
        def _():
            slot = slot_of(i2)
            for cond, u, n, chip, local in pieces(block_of(i2)):
                @pl.when(cond & ((chip == j) | (chip == jb)))
                def _(u=u, n=n, chip=chip, local=local):
                    h1_copy(slot, u, n).wait_recv()
                    _accumulate(piece(keep, slot, u, n), piece(xbuf, slot, u, n))

                    @pl.when(chip == jb)
                    def _():
                        _cast_rows(piece(keep, slot, u, n), piece(xbuf, slot, u, n))
                        h2_copy(slot, u, n, local).start()

                @pl.when(cond & ((chip == ja) | (chip == jd)))
                def _(u=u, n=n):
                    h1_copy(slot, u, n).wait_send()

        i3 = step - LAG_HOP2

        @pl.when(owned(i3))
        def _():
            slot = slot_of(i3)
            for cond, u, n, chip, local in pieces(block_of(i3)):
                @pl.when(cond & (chip == j))
                def _(u=u, n=n, local=local):
                    h2_copy(slot, u, n, local).wait_recv()
                    _accumulate(piece(keep, slot, u, n), chip_rows_at(land2, local, n))
                    mine = pltpu.make_async_copy(piece(keep, slot, u, n), chip_rows_at(gwt_sh, local, n), out_sem.at[0])
                    mine.start()
                    sw_copy(slot, u, n, local).start()
                    mine.wait()

                @pl.when(cond & (chip == jb))
                def _(u=u, n=n, local=local):
                    h2_copy(slot, u, n, local).wait_send()

        e = step - n_blk

        @pl.when((e >= 0) & (e < n_tiles))
        def _():
            dh = dh_acc[pl.ds(pl.multiple_of(e * tile, tile), tile), :]
            xv = x_ref[...]
            r = _rstd(xv)
            xhat = xv * r
            gni[...] += jnp.sum(dh * xhat, axis=0, keepdims=True)
            gx_ref[...] = _rms_bwd(dh * g_ref[...], xhat, r) + dx2_ref[...]

        @pl.when(step == n_steps - 1)
        def _():
            small_land[me] = small_ref[...]
            small_land[me, 0:1, :] = gni[...]
            others = [(dx, dy, dc) for dx in (0, 1) for dy in (0, 1) for dc in (0, 1)][1:]
            sends = [remote(small_land.at[me], small_land.at[me], sm_send.at[k], sm_recv.at[k],
                            (_xor(x_i, dx), _xor(y_i, dy), _xor(c, dc))) for k, (dx, dy, dc) in enumerate(others)]
            for cp in sends:
                cp.start()
            for i in range(n_blk):
                if i + 2 * PAIR_RING >= n_blk:
                    @pl.when(owner_of(i) != c)
                    def _(i=i):
                        pair_copy(i).wait_send()
                for _, u, n, chip, local in pieces(block_of(i)):
                    @pl.when((j == chip) & (c == owner_of(i)))
                    def _(i=i, u=u, n=n, local=local):
                        sw_copy(slot_of(i), u, n, local).wait_send()

                    @pl.when((j == chip) & (c != owner_of(i)))
                    def _(i=i, u=u, n=n, local=local):
                        sw_copy(slot_of(i), u, n, local).wait_recv()
            for cp in sends:
                cp.wait_recv()
            total = small_land[0]
            for dev in range(1, 8):
                total = total + small_land[dev]
            small_sum[...] = total
            for cp in sends:
                cp.wait_send()

    def blk_at(i):
        return block_of(jnp.clip(i, 0, n_blk - 1))

    last_pc_step = max(i for i in range(n_blk) if block_of(i) < blk_q)

    def next_block(i, in_pc):
        i = jnp.clip(i, 0, n_blk - 1)
        step = jnp.full_like(i, last_pc_step if in_pc else n_blk - 1)
        for ahead in reversed(range(N_CHIPS)):
            cand = jnp.minimum(i + ahead, n_blk - 1)
            step = jnp.where((block_of(cand) < blk_q) == in_pc, cand, step)
        return block_of(step)

    def dqg_block(i):
        b = next_block(i, False)
        q_blk = jnp.clip(b - blk_q, 0, blk_kv - blk_q - 1)
        ga_blk = (D_ATTN // GBLK) + jnp.clip(b - blk_ga, 0, n_blk - blk_ga - 1)
        return jnp.where(b < blk_kv, q_blk, jnp.where(b == blk_kv, 2 * D_ATTN // GBLK, ga_blk))

    def tok(i):
        return (jnp.clip(i - n_blk, 0, n_tiles - 1), 0)

    n_piece = n_slots * n_sub
    dma = pltpu.SemaphoreType.DMA
    return pl.pallas_call(
        body, name="bwd_in", grid=(n_steps,),
        out_shape=(jax.ShapeDtypeStruct((seq, D_MODEL), F32), jax.ShapeDtypeStruct(small.shape, F32),
                   jax.ShapeDtypeStruct((chip_rows, D_MODEL), F32)),
        in_specs=[pl.BlockSpec((seq, GBLK), lambda i: (0, next_block(i, True))),
                  pl.BlockSpec((seq, GBLK), lambda i: (0, dqg_block(i))),
                  pl.BlockSpec((GBLK, D_MODEL), lambda i: (blk_at(i), 0)),
                  _resident(h.shape),
                  pl.BlockSpec((tile, D_MODEL), tok), _resident((1, D_MODEL)), pl.BlockSpec((tile, D_MODEL), tok),
                  _resident(small.shape)],
        out_specs=(pl.BlockSpec((tile, D_MODEL), tok), pl.BlockSpec(small.shape, lambda i: (0, 0)),
                   pl.BlockSpec(memory_space=pl.ANY)),
        scratch_shapes=[pltpu.VMEM((seq, D_MODEL), F32), pltpu.VMEM((1, D_MODEL), F32),
                        pltpu.VMEM((n_slots, GBLK, D_MODEL), F32), pltpu.VMEM((PAIR_RING, GBLK, D_MODEL), BF16),
                        pltpu.VMEM((n_slots, GBLK, D_MODEL), BF16), pltpu.VMEM((n_slots, GBLK, D_MODEL), BF16),
                        pltpu.VMEM((chip_rows, D_MODEL), BF16), pltpu.VMEM((8,) + small.shape, F32),
                        dma((n_slots,)), dma((n_slots,)), dma((n_piece,)), dma((n_piece,)), dma((n_piece,)),
                        dma((n_piece,)), dma((n_piece,)), dma((n_piece,)), dma((7,)), dma((7,)), dma((1,))],
        compiler_params=_params(62, ("arbitrary",)),
    )(dpc, dqg, wt, h, x, norm_in, dx2, small)


def _accumulate(dst_ref, src_ref, rows=16):
    def step(i, carry):
        sl = pl.ds(pl.multiple_of(i * rows, rows), rows)
        dst_ref[sl, :] = dst_ref[sl, :] + src_ref[sl, :].astype(F32)
        return carry
    lax.fori_loop(0, dst_ref.shape[0] // rows, step, 0)


def _rs_wout_scratch(gwo_shape):
    o_half, width = gwo_shape[0] // N_CHIPS // 2, gwo_shape[1]
    return [pltpu.VMEM((4, o_half, width), F32), pltpu.VMEM((4, o_half, width), BF16),
            pltpu.VMEM((4, o_half, width), BF16), pltpu.VMEM((2, o_half, width), BF16),
            pltpu.VMEM((o_half, width), BF16),
            pltpu.SemaphoreType.DMA((8,)), pltpu.SemaphoreType.DMA((8,)), pltpu.SemaphoreType.DMA((4,))]


def _rs_wout_stages(gwo_ref, gwo_sh, acc_o, sb_o, r1o, r2o, r3o, send_sems, recv_sems, local_sems):
    o_rows = gwo_ref.shape[0] // N_CHIPS
    o_half = o_rows // 2
    x, y, c = lax.axis_index("x"), lax.axis_index("y"), lax.axis_index("c")
    j = 2 * x + y
    pa = (_xor(x, 1 - c), _xor(y, c), c)
    pb = (_xor(x, c), _xor(y, 1 - c), c)
    sib = (x, y, 1 - c)
    ja = 2 * pa[0] + pa[1]
    jb = 2 * pb[0] + pb[1]
    jd = 3 - j
    order = (ja, jd, jb, j)
    sib_order = (jb, jd, ja, j)

    def rcopy(k, src, dst, to):
        return pltpu.make_async_remote_copy(src_ref=src, dst_ref=dst, send_sem=send_sems.at[k],
                                            recv_sem=recv_sems.at[k], device_id=to, device_id_type=MESH)

    def o_rows_of(chip, half):
        return gwo_ref.at[pl.ds(pl.multiple_of(chip * o_rows + half * o_half, 8), o_half), :]

    def load_all(chips, half):
        cps = [pltpu.make_async_copy(o_rows_of(chip, half), acc_o.at[s], local_sems.at[s]) for s, chip in enumerate(chips)]
        for cp in cps:
            cp.start()
        return cps

    def pair_send(s):
        return rcopy(s, sb_o.at[s], r1o.at[s], sib)

    def out_half(half):
        return gwo_sh.at[pl.ds(pl.multiple_of(half * o_half, 8), o_half), :]

    hop1 = [rcopy(4 + s, sb_o.at[s], r2o.at[s], pa) for s in range(2)]
    hop2 = rcopy(6, sb_o.at[2], r3o, pb)
    swap = rcopy(7, acc_o.at[3], out_half(c), sib)
    mine = pltpu.make_async_copy(acc_o.at[3], out_half(c), local_sems.at[0])

    def resend(s, copy):
        pair_send(s).wait_send()
        _cast_rows(acc_o.at[s], sb_o.at[s])
        copy.start()

    def stage_pair():
        for s, cp in enumerate(load_all(sib_order, 1 - c)):
            cp.wait()
            _cast_rows(acc_o.at[s], sb_o.at[s])
            pair_send(s).start()

    def stage_hop1():
        for s, cp in enumerate(load_all(order, c)):
            cp.wait()
            pair_send(s).wait_recv()
            _accumulate(acc_o.at[s], r1o.at[s])
            if s < 2:
                resend(s, hop1[s])

    def stage_hop2():
        hop1[1].wait_recv()
        _accumulate(acc_o.at[2], r2o.at[1])
        resend(2, hop2)
        hop1[0].wait_recv()
        _accumulate(acc_o.at[3], r2o.at[0])

    def stage_final():
        hop2.wait_recv()
        _accumulate(acc_o.at[3], r3o)
        mine.start()
        swap.start()

    def stage_drain():
        rcopy(7, acc_o.at[3], out_half(1 - c), sib).wait_recv()
        for cp in [pair_send(3)] + hop1 + [hop2, swap]:
            cp.wait_send()
        mine.wait()

    return stage_pair, stage_hop1, stage_hop2, stage_final, stage_drain


def _adamw(name, w, g, m, v, rows):
    def body(w_ref, g_ref, m_ref, v_ref, go_ref, d_ref, nm_ref, nv_ref):
        gv = g_ref[...]
        go_ref[...] = gv
        nm = ADAM_B1 * m_ref[...] + (1.0 - ADAM_B1) * gv
        nv = ADAM_B2 * v_ref[...] + (1.0 - ADAM_B2) * (gv * gv)
        m_hat = nm / (1.0 - ADAM_B1 ** ADAM_STEP)
        v_hat = nv / (1.0 - ADAM_B2 ** ADAM_STEP)
        d_ref[...] = -ADAM_LR * (m_hat / (jnp.sqrt(v_hat) + ADAM_EPS) + ADAM_WD * w_ref[...])
        nm_ref[...] = nm
        nv_ref[...] = nv

    spec = pl.BlockSpec((rows, w.shape[1]), lambda i: (i, 0))
    shape = jax.ShapeDtypeStruct(w.shape, F32)
    return pl.pallas_call(
        body, name="adamw_" + name, grid=(w.shape[0] // rows,),
        out_shape=(shape,) * 4, in_specs=[spec] * 4, out_specs=(spec,) * 4,
        compiler_params=_params(32, ("arbitrary",)),
    )(w, g, m, v)


def kernel(x, norm_in, w_in, conv_w, attn_sinks, norm_conv_out, norm_attn_out, w_out, norm_final, loss_target, m_norm_in, m_w_in, m_conv_w, m_attn_sinks, m_norm_conv_out, m_norm_attn_out, m_w_out, m_norm_final, v_norm_in, v_w_in, v_conv_w, v_attn_sinks, v_norm_conv_out, v_norm_attn_out, v_w_out, v_norm_final):
    chip = 2 * lax.axis_index("x") + lax.axis_index("y")
    xs, target = x[0], loss_target[0]
    norm_final2 = norm_final.reshape(1, D_MODEL)
    w_in_t, m_w_in_t, v_w_in_t = w_in[0].T, m_w_in[0].T, v_w_in[0].T

    wt, cw4 = _ag_weights(w_in_t, conv_w[0])
    cw = jnp.transpose(cw4, (1, 0, 2)).reshape(3, D_CONV)

    h, pc, q, kv, ga, wo = _fwd_in(xs, norm_in, wt, w_out[0])
    oc = _conv_fwd(pc, cw, norm_conv_out)
    ya, oa = _attn_fwd(q, kv, ga, attn_sinks, norm_attn_out)
    dx2, doc, doa, gwo, gnf, loss_lanes = _out_proj_loss(xs, oc, oa, wo, norm_final2, target)

    dpc, gnc, gcw = _conv_bwd(pc, doc, cw, norm_conv_out)
    dqg, gna, gsink, gwo_sh = _attn_bwd(q, kv, ga, ya, doa, attn_sinks, norm_attn_out, gwo)

    last = jnp.zeros((1, D_MODEL), F32).at[:, 0:N_HEADS].set(gsink[:, 0:N_HEADS]).at[0, N_HEADS].set(jnp.sum(loss_lanes))
    small = jnp.concatenate([jnp.zeros((1, D_MODEL), F32), gnc, gna, gnf, gcw[0:3], last], axis=0)
    grad_x, small_sum, gwt_sh = _bwd_in(dpc, dqg, h, wt, xs, norm_in, dx2, small)

    loss = small_sum[7, N_HEADS]
    g_norm_in, g_norm_conv, g_norm_attn = small_sum[0:1], small_sum[1:2], small_sum[2:3]
    g_norm_final = small_sum[3]
    g_conv_w = lax.dynamic_slice(small_sum[4:7], (0, chip * (D_CONV // N_CHIPS)), (3, D_CONV // N_CHIPS))[None]
    g_sinks = small_sum[7:8, 0:N_HEADS]

    def adam(name, w, g, m, v, rows):
        shape = w.shape
        two_d = (-1, shape[-1])
        out = _adamw(name, w.reshape(two_d), g.reshape(two_d), m.reshape(two_d), v.reshape(two_d), rows)
        return tuple(o.reshape(shape) for o in out)

    weights = (norm_in, w_in, conv_w, attn_sinks, norm_conv_out, norm_attn_out, w_out, norm_final)
    grads = (g_norm_in, None, g_conv_w, g_sinks, g_norm_conv, g_norm_attn, gwo_sh[None], g_norm_final)
    moments_m = (m_norm_in, m_w_in, m_conv_w, m_attn_sinks, m_norm_conv_out, m_norm_attn_out, m_w_out, m_norm_final)
    moments_v = (v_norm_in, v_w_in, v_conv_w, v_attn_sinks, v_norm_conv_out, v_norm_attn_out, v_w_out, v_norm_final)
    rows = (1, None, 3, 1, 1, 1, 128, 1)
    names = ("norm_in", "w_in", "conv_w", "attn_sinks", "norm_conv_out", "norm_attn_out", "w_out", "norm_final")
    updates = []
    for args in zip(names, weights, grads, moments_m, moments_v, rows):
        if args[0] == "w_in":
            out_t = _adamw("w_in", w_in_t, gwt_sh, m_w_in_t, v_w_in_t, 200)
            updates.append(tuple(o.T[None] for o in out_t))
        else:
            updates.append(adam(*args))
    grads_out, deltas, new_m, new_v = zip(*updates)
    return (loss, grad_x[None], *grads_out, *deltas, *new_m, *new_v)
```

```python
import jax
import jax.numpy as jnp
from jax import lax
from jax.experimental import pallas as pl
from jax.experimental.pallas import tpu as pltpu

F32 = jnp.float32
BF16 = jnp.bfloat16
MESH = pl.DeviceIdType.MESH

D_MODEL = 1024
D_CONV = 1024
D_ATTN = 1024
D_KV = 128
D_QG = 2 * D_ATTN + 2 * D_KV
D_MIX = D_CONV + D_ATTN
D_PC = 4 * D_CONV
D_IN_PROJ = D_PC + 2 * D_ATTN + 2 * D_KV
ROW_Q = D_PC
ROW_KV = ROW_Q + D_ATTN
ROW_GA = ROW_KV + 2 * D_KV
N_HEADS = 16
HEAD_DIM = 64
HEADS_PER_KV = 8
BLK = 128
N_CHIPS = 4
RMS_EPS = 1e-5
SCALE = HEAD_DIM ** -0.5
SLOPES = tuple(2.0 ** (-8.0 * (h + 1) / N_HEADS) for h in range(N_HEADS))

ADAM_LR, ADAM_B1, ADAM_B2, ADAM_EPS, ADAM_WD, ADAM_STEP = 0.001, 0.9, 0.999, 1e-08, 0.01, 10

TOK_TILE = 256
MIB = 1 << 20


def _params(vmem_mib, semantics=None):
    return pltpu.CompilerParams(dimension_semantics=semantics, vmem_limit_bytes=vmem_mib * MIB)


def _nn(a, b):
    return jnp.dot(a, b, preferred_element_type=F32)


def _nt(a, b):
    return lax.dot_general(a, b, (((1,), (1,)), ((), ())), preferred_element_type=F32)


def _tn(a, b):
    return lax.dot_general(a, b, (((0,), (0,)), ((), ())), preferred_element_type=F32)


def _rstd(v):
    return lax.rsqrt(jnp.mean(v * v, axis=-1, keepdims=True) + RMS_EPS)


def _rms_bwd(g, xhat, rstd):
    return rstd * (g - xhat * jnp.mean(g * xhat, axis=-1, keepdims=True))


def _silu_and_grad(g):
    s = jax.nn.sigmoid(g)
    return g * s, s * (1.0 + g * (1.0 - s))


def _resident(shape):
    return pl.BlockSpec(shape, lambda *_: (0,) * len(shape), pipeline_mode=pl.Buffered(1))


def _xor(a, b):
    return a + b - 2 * a * b


def _cast_rows(src_ref, dst_ref, rows=32):
    def step(i, carry):
        sl = pl.ds(pl.multiple_of(i * rows, rows), rows)
        dst_ref[sl, :] = src_ref[sl, :].astype(dst_ref.dtype)
        return carry
    lax.fori_loop(0, src_ref.shape[0] // rows, step, 0)


def _ag_scratch(shard_shape):
    rows, width = shard_shape
    return [pltpu.VMEM((rows, width), F32), pltpu.VMEM((rows, width), BF16), pltpu.VMEM((3, rows // 2, width), BF16),
            pltpu.SemaphoreType.DMA((6,)), pltpu.SemaphoreType.DMA((6,)), pltpu.SemaphoreType.DMA((4,))]


def _ag_stages(sh_ref, out, f32_buf, own, land, send_sems, recv_sems, local_sems):
    rows = sh_ref.shape[0]
    half = rows // 2
    x, y, c = lax.axis_index("x"), lax.axis_index("y"), lax.axis_index("c")
    j = 2 * x + y
    p1 = (_xor(x, c), _xor(y, 1 - c), c)
    p2 = (_xor(x, 1 - c), _xor(y, c), c)
    sib = (x, y, 1 - c)
    j1 = 2 * p1[0] + p1[1]
    j2 = 2 * p2[0] + p2[1]
    j3 = 3 - j

    def rows_of(chip, hf):
        return out.at[pl.ds(pl.multiple_of(chip * rows + hf * half, 16), half), :]

    def rcopy(k, src, dst, to):
        return pltpu.make_async_remote_copy(src_ref=src, dst_ref=dst, send_sem=send_sems.at[k],
                                            recv_sem=recv_sems.at[k], device_id=to, device_id_type=MESH)

    my_half = own.at[pl.ds(pl.multiple_of(c * half, 16), half), :]
    hop1 = rcopy(0, my_half, land.at[0], p1)
    hop2_own = rcopy(1, my_half, land.at[1], p2)
    hop2_fwd = rcopy(2, land.at[0], land.at[2], p2)
    swaps = [rcopy(3 + s, land.at[s], rows_of(chip, c), sib) for s, chip in enumerate((j1, j2, j3))]
    keeps = [pltpu.make_async_copy(land.at[s], rows_of(chip, c), local_sems.at[1 + s]) for s, chip in enumerate((j1, j2, j3))]
    load = pltpu.make_async_copy(sh_ref, f32_buf, local_sems.at[0])
    own_out = pltpu.make_async_copy(own, out.at[pl.ds(pl.multiple_of(j * rows, 16), rows), :], local_sems.at[0])

    def stage_send():
        load.start()
        load.wait()
        _cast_rows(f32_buf, own)
        own_out.start()
        hop1.start()

    def stage_forward():
        hop1.wait_recv()
        hop2_own.start()
        hop2_fwd.start()
        swaps[0].start()
        keeps[0].start()

    def stage_publish():
        hop2_own.wait_recv()
        swaps[1].start()
        keeps[1].start()
        hop2_fwd.wait_recv()
        swaps[2].start()
        keeps[2].start()

    def stage_drain():
        for s, chip in enumerate((j2, j1, j3)):
            rcopy(3 + s, my_half, rows_of(chip, 1 - c), sib).wait_recv()
        for cp in [hop1, hop2_own, hop2_fwd] + swaps:
            cp.wait_send()
        for cp in [own_out] + keeps:
            cp.wait()

    return stage_send, stage_forward, stage_publish, stage_drain


def _ag_weights(wt_sh, cw_sh):
    def body(wt_ref, cw_ref, wt_out, cw_out, *scratch):
        cw_send, cw_recv, cw_local = scratch[-3:]
        x, y, c = lax.axis_index("x"), lax.axis_index("y"), lax.axis_index("c")
        j = 2 * x + y
        p1 = (_xor(x, c), _xor(y, 1 - c), c)
        p2 = (_xor(x, 1 - c), _xor(y, c), c)
        j1 = 2 * p1[0] + p1[1]

        def cw_copy(k, chip, to):
            src = cw_ref if k != 2 else cw_out.at[chip]
            return pltpu.make_async_remote_copy(src_ref=src, dst_ref=cw_out.at[chip], send_sem=cw_send.at[k],
                                                recv_sem=cw_recv.at[k], device_id=to, device_id_type=MESH)

        mine = pltpu.make_async_copy(cw_ref, cw_out.at[j], cw_local.at[0])
        mine.start()
        first = cw_copy(0, j, p1)
        first.start()
        stages = _ag_stages(wt_ref, wt_out, *scratch[:-3])
        stages[0]()
        first.wait_recv()
        second = [cw_copy(1, j, p2), cw_copy(2, j1, p2)]
        for cp in second:
            cp.start()
        for stage in stages[1:]:
            stage()
        for cp in second:
            cp.wait_recv()
        for cp in [first] + second:
            cp.wait_send()
        mine.wait()

    any_spec = pl.BlockSpec(memory_space=pl.ANY)
    dma = pltpu.SemaphoreType.DMA
    return pl.pallas_call(
        body, name="ag_weights",
        out_shape=(jax.ShapeDtypeStruct((N_CHIPS * wt_sh.shape[0], wt_sh.shape[1]), BF16),
                   jax.ShapeDtypeStruct((N_CHIPS,) + cw_sh.shape, cw_sh.dtype)),
        in_specs=[any_spec, any_spec], out_specs=(any_spec, any_spec),
        scratch_shapes=_ag_scratch(wt_sh.shape) + [dma((3,)), dma((3,)), dma((1,))],
        compiler_params=_params(32),
    )(wt_sh, cw_sh)


def _fwd_in(x, norm_in, wt, wo_sh, conv_w, norm_conv_out):
    seq = x.shape[0]
    tile = TOK_TILE
    n_tiles = seq // tile
    stage_steps = (0, n_tiles // 4, (5 * n_tiles) // 8, n_tiles - 1)

    def body(x_ref, g_ref, wt_ref, wo_ref, cw_ref, gn_ref, h_ref, pc_ref, q_ref, kv_ref, ga_ref, oc_ref, wo_out,
             zbuf, *ag_scratch):
        stages = _ag_stages(wo_ref, wo_out, *ag_scratch)
        for at, stage in zip(stage_steps[:-1], stages[:-1]):
            pl.when(pl.program_id(0) == at)(stage)

        @pl.when(pl.program_id(0) == 0)
        def _():
            zbuf[0:8, :] = jnp.zeros((8, D_CONV), F32)

        xv = x_ref[...]
        h = (xv * _rstd(xv) * g_ref[...]).astype(BF16)
        h_ref[...] = h
        for blk in range(D_PC // D_CONV):
            pc_ref[:, blk * D_CONV:(blk + 1) * D_CONV] = _nt(h, wt_ref[blk * D_CONV:(blk + 1) * D_CONV, :])
        q_ref[...] = _nt(h, wt_ref[ROW_Q:ROW_KV, :])
        kv_ref[...] = _nt(h, wt_ref[ROW_KV:ROW_GA, :])
        ga_ref[...] = _nt(h, wt_ref[ROW_GA:D_IN_PROJ, :])

        cb, _, _, _, _, _, conv = _conv_core(pc_ref, zbuf, cw_ref)
        yc = cb * conv
        silu, _ = _silu_and_grad(pc_ref[:, 3 * D_CONV:4 * D_CONV])
        oc_ref[...] = (yc * _rstd(yc) * gn_ref[...] * silu).astype(BF16)
        zbuf[0:8, :] = zbuf[tile:tile + 8, :]
        pl.when(pl.program_id(0) == stage_steps[-1])(stages[-1])

    def row(width):
        return pl.BlockSpec((tile, width), lambda i: (i, 0))

    any_spec = pl.BlockSpec(memory_space=pl.ANY)
    return pl.pallas_call(
        body, name="fwd_in", grid=(n_tiles,),
        out_shape=(jax.ShapeDtypeStruct((seq, D_MODEL), BF16), jax.ShapeDtypeStruct((seq, D_PC), F32),
                   jax.ShapeDtypeStruct((seq, D_ATTN), F32), jax.ShapeDtypeStruct((seq, 2 * D_KV), F32),
                   jax.ShapeDtypeStruct((seq, D_ATTN), F32), jax.ShapeDtypeStruct((seq, D_CONV), BF16),
                   jax.ShapeDtypeStruct((N_CHIPS * wo_sh.shape[0], wo_sh.shape[1]), BF16)),
        in_specs=[row(D_MODEL), _resident((1, D_MODEL)), _resident(wt.shape), any_spec,
                  _resident(conv_w.shape), _resident((1, D_CONV))],
        out_specs=(row(D_MODEL), row(D_PC), row(D_ATTN), row(2 * D_KV), row(D_ATTN), row(D_CONV), any_spec),
        scratch_shapes=[pltpu.VMEM((tile + 8, D_CONV), F32)] + _ag_scratch(wo_sh.shape),
        compiler_params=_params(56, ("arbitrary",)),
    )(x, norm_in, wt, wo_sh, conv_w, norm_conv_out)


def _conv_core(pc_ref, zbuf, cw_ref):
    tile = pc_ref.shape[0]
    cb = pc_ref[:, 0:D_CONV]
    cc = pc_ref[:, D_CONV:2 * D_CONV]
    cu = pc_ref[:, 2 * D_CONV:3 * D_CONV]
    z = cc * cu
    zbuf[8:tile + 8, :] = z
    z1 = zbuf[7:tile + 7, :]
    z2 = zbuf[6:tile + 6, :]
    conv = cw_ref[0:1, :] * z2 + cw_ref[1:2, :] * z1 + cw_ref[2:3, :] * z
    return cb, cc, cu, z, z1, z2, conv


def _band_geometry(block_index):
    qi = lax.broadcasted_iota(jnp.int32, (BLK, BLK), 0)
    kp = lax.broadcasted_iota(jnp.int32, (BLK, BLK), 1)
    use_cur = kp <= qi
    dist = jnp.where(use_cur, qi - kp, qi - kp + BLK).astype(F32)
    valid = use_cur | (block_index > 0)
    return use_cur, dist, valid


def _block_diag(cur, prev, group):
    lane = lax.broadcasted_iota(jnp.int32, cur.shape, 1)

    def halves(t):
        other = pltpu.roll(t, 64, 1)
        lo, hi = (t, other) if group == 0 else (other, t)
        return jnp.where(lane < 64, lo, 0.0), jnp.where(lane >= 64, hi, 0.0)

    return jnp.concatenate(halves(cur) + halves(prev), axis=0).astype(BF16)


def _merge(s4, use_cur):
    return (jnp.where(use_cur, s4[:, 0:BLK], s4[:, 2 * BLK:3 * BLK]),
            jnp.where(use_cur, s4[:, BLK:2 * BLK], s4[:, 3 * BLK:4 * BLK]))


def _split(a, b, use_cur):
    return jnp.concatenate([jnp.where(use_cur, a, 0.0), jnp.where(use_cur, b, 0.0),
                            jnp.where(use_cur, 0.0, a), jnp.where(use_cur, 0.0, b)], axis=1)


def _softmax_head(s, head, sink, dist, valid):
    sc = jnp.where(valid, s - SLOPES[head] * dist, -jnp.inf)
    m = jnp.maximum(jnp.max(sc, axis=-1, keepdims=True), sink)
    p = jnp.exp(sc - m)
    es = jnp.exp(sink - m)
    inv = 1.0 / (jnp.sum(p, axis=-1, keepdims=True) + es)
    return p * inv, es * inv


def _attn_probs(q_ref, kvc_ref, kvp_ref, sink_ref, geometry):
    use_cur, dist, valid = geometry
    groups = range(N_HEADS // HEADS_PER_KV)
    kbd = [_block_diag(kvc_ref[:, 0:D_KV], kvp_ref[:, 0:D_KV], g) for g in groups]
    vbd = [_block_diag(kvc_ref[:, D_KV:2 * D_KV], kvp_ref[:, D_KV:2 * D_KV], g) for g in groups]
    qps = [(q_ref[:, j * 128:(j + 1) * 128] * SCALE).astype(BF16) for j in range(N_HEADS // 2)]
    scores = []
    for j, qp in enumerate(qps):
        scores += _merge(_nt(qp, kbd[j // 4]), use_cur)
    sinks = [sink_ref[0, h] for h in range(N_HEADS)]
    scores = [jnp.where(valid, s - SLOPES[h] * dist, -jnp.inf) for h, s in enumerate(scores)]
    maxes = [jnp.maximum(jnp.max(s, axis=-1, keepdims=True), sinks[h]) for h, s in enumerate(scores)]
    exps = [jnp.exp(s - m) for s, m in zip(scores, maxes)]
    sink_exps = [jnp.exp(sinks[h] - m) for h, m in enumerate(maxes)]
    invs = [1.0 / (jnp.sum(e, axis=-1, keepdims=True) + se) for e, se in zip(exps, sink_exps)]
    probs = [e * inv for e, inv in zip(exps, invs)]
    sink_probs = [se * inv for se, inv in zip(sink_exps, invs)]
    return qps, kbd, vbd, probs, sink_probs


def _kv_specs(n_blocks, reverse):
    def blk(i):
        return (n_blocks - 1 - i) if reverse else i
    cur = pl.BlockSpec((BLK, 2 * D_KV), lambda i: (blk(i), 0))
    prev = pl.BlockSpec((BLK, 2 * D_KV), lambda i: (jnp.maximum(blk(i) - 1, 0), 0))
    return cur, prev


def _attn_fwd(q, kv, ga, sinks, norm_attn_out):
    seq = q.shape[0]
    n_blocks = seq // BLK

    def body(q_ref, kvc_ref, kvp_ref, ga_ref, sink_ref, gn_ref, ya_ref, oa_ref):
        geometry = _band_geometry(pl.program_id(0))
        _, _, vbd, probs, _ = _attn_probs(q_ref, kvc_ref, kvp_ref, sink_ref, geometry)
        p4s = [_split(probs[2 * j], probs[2 * j + 1], geometry[0]).astype(BF16) for j in range(N_HEADS // 2)]
        ya = jnp.concatenate([_nn(p4, vbd[j // 4]) for j, p4 in enumerate(p4s)], axis=1)
        ya_ref[...] = ya
        silu, _ = _silu_and_grad(ga_ref[...])
        oa_ref[...] = (ya * _rstd(ya) * gn_ref[...] * silu).astype(BF16)

    row = pl.BlockSpec((BLK, D_ATTN), lambda i: (i, 0))
    kv_cur, kv_prev = _kv_specs(n_blocks, reverse=False)
    return pl.pallas_call(
        body, name="attn_fwd", grid=(n_blocks,),
        out_shape=(jax.ShapeDtypeStruct((seq, D_ATTN), F32), jax.ShapeDtypeStruct((seq, D_ATTN), BF16)),
        in_specs=[row, kv_cur, kv_prev, row, pl.BlockSpec(memory_space=pltpu.SMEM), _resident((1, D_ATTN))],
        out_specs=(row, row),
        compiler_params=_params(32, ("arbitrary",)),
    )(q, kv, kv, ga, sinks, norm_attn_out)


def _out_proj_loss(x, oc, oa, wo, norm_final, target, pc, conv_w, norm_conv_out):
    seq = x.shape[0]
    tile = TOK_TILE
    n_tiles = seq // tile

    def body(x_ref, oc_ref, oa_ref, wo_ref, gf_ref, t_ref, pc_ref, hcc_ref, hcu_ref, cw_ref, gn_ref,
             dx2_ref, doa_ref, gwo_ref, gnf_ref, loss_ref, dpc_ref, gnc_ref, gcw_ref, zbuf, dbuf):
        step = pl.program_id(0)

        @pl.when(step == 0)
        def _():
            gwo_ref[...] = jnp.zeros_like(gwo_ref)
            gnf_ref[...] = jnp.zeros_like(gnf_ref)
            loss_ref[...] = jnp.zeros_like(loss_ref)
            gnc_ref[...] = jnp.zeros_like(gnc_ref)
            gcw_ref[...] = jnp.zeros_like(gcw_ref)
            dbuf[tile:tile + 8, :] = jnp.zeros((8, D_CONV), F32)

        oc, oa = oc_ref[...], oa_ref[...]
        x2 = x_ref[...] + _nn(oc, wo_ref[0:D_CONV, :]) + _nn(oa, wo_ref[D_CONV:D_MIX, :])
        r = _rstd(x2)
        xhat = x2 * r
        err = xhat * gf_ref[...] - t_ref[...]
        loss_ref[...] += jnp.sum(err * err, axis=0, keepdims=True) * (0.5 / D_MODEL)
        dy = err * (1.0 / D_MODEL)
        gnf_ref[...] += jnp.sum(dy * xhat, axis=0, keepdims=True)
        dx2 = _rms_bwd(dy * gf_ref[...], xhat, r)
        dx2_ref[...] = dx2
        db = dx2.astype(BF16)
        do = _nt(db, wo_ref[0:D_CONV, :])
        doa_ref[...] = _nt(db, wo_ref[D_CONV:D_MIX, :])
        gwo_ref[0:D_CONV, :] += _tn(oc, db)
        gwo_ref[D_CONV:D_MIX, :] += _tn(oa, db)

        is_first_tile = step == n_tiles - 1
        zbuf[0:8, :] = jnp.where(is_first_tile, 0.0, hcc_ref[...] * hcu_ref[...])
        cb, cc, cu, z, z1, z2, conv = _conv_core(pc_ref, zbuf, cw_ref)
        silu, dsilu = _silu_and_grad(pc_ref[:, 3 * D_CONV:4 * D_CONV])
        yc = cb * conv
        rc = _rstd(yc)
        chat = yc * rc
        dn = do * silu
        dpc_ref[:, 3 * D_CONV:4 * D_CONV] = (do * (chat * gn_ref[...]) * dsilu).astype(BF16)
        gnc_ref[...] += jnp.sum(dn * chat, axis=0, keepdims=True)
        dyc = _rms_bwd(dn * gn_ref[...], chat, rc)
        dpc_ref[:, 0:D_CONV] = (dyc * conv).astype(BF16)
        dconv = dyc * cb
        gcw_ref[0:1, :] += jnp.sum(dconv * z2, axis=0, keepdims=True)
        gcw_ref[1:2, :] += jnp.sum(dconv * z1, axis=0, keepdims=True)
        gcw_ref[2:3, :] += jnp.sum(dconv * z, axis=0, keepdims=True)
        dbuf[0:tile, :] = dconv
        dz = cw_ref[2:3, :] * dconv + cw_ref[1:2, :] * dbuf[1:tile + 1, :] + cw_ref[0:1, :] * dbuf[2:tile + 2, :]
        dpc_ref[:, D_CONV:2 * D_CONV] = (dz * cu).astype(BF16)
        dpc_ref[:, 2 * D_CONV:3 * D_CONV] = (dz * cc).astype(BF16)
        dbuf[tile:tile + 8, :] = dbuf[0:8, :]

    def rev(i):
        return n_tiles - 1 - i

    def row(width):
        return pl.BlockSpec((tile, width), lambda i: (rev(i), 0))

    def halo(col_block):
        return pl.BlockSpec((8, D_CONV), lambda i: (jnp.maximum(rev(i) * (tile // 8) - 1, 0), col_block))

    def const(shape):
        return pl.BlockSpec(shape, lambda i: (0, 0))

    return pl.pallas_call(
        body, name="out_proj_loss", grid=(n_tiles,),
        out_shape=(jax.ShapeDtypeStruct((seq, D_MODEL), F32), jax.ShapeDtypeStruct((seq, D_ATTN), F32),
                   jax.ShapeDtypeStruct((D_MIX, D_MODEL), F32), jax.ShapeDtypeStruct((1, D_MODEL), F32),
                   jax.ShapeDtypeStruct((1, D_MODEL), F32), jax.ShapeDtypeStruct((seq, D_PC), BF16),
                   jax.ShapeDtypeStruct((1, D_CONV), F32), jax.ShapeDtypeStruct((8, D_CONV), F32)),
        in_specs=[row(D_MODEL), row(D_CONV), row(D_ATTN), _resident(wo.shape), _resident((1, D_MODEL)), row(D_MODEL),
                  row(D_PC), halo(1), halo(2), _resident(conv_w.shape), _resident((1, D_CONV))],
        out_specs=(row(D_MODEL), row(D_ATTN), const((D_MIX, D_MODEL)), const((1, D_MODEL)), const((1, D_MODEL)),
                   row(D_PC), const((1, D_CONV)), const((8, D_CONV))),
        scratch_shapes=[pltpu.VMEM((tile + 8, D_CONV), F32), pltpu.VMEM((tile + 8, D_CONV), F32)],
        compiler_params=_params(62, ("arbitrary",)),
    )(x, oc, oa, wo, norm_final, target, pc, pc, pc, conv_w, norm_conv_out)


def _attn_bwd(q, kv, ga, ya, doa, sinks, norm_attn_out, gwo):
    seq = q.shape[0]
    n_blocks = seq // BLK
    stage_steps = (0, n_blocks // 4, n_blocks // 2, (3 * n_blocks) // 4, n_blocks - 1)

    def body(q_ref, kvc_ref, kvp_ref, ga_ref, ya_ref, doa_ref, sink_ref, gn_ref, gwo_ref,
             dqg_ref, gna_ref, gs_ref, gwo_sh, carry, dya_buf, *rs_scratch):
        step = pl.program_id(0)

        @pl.when(step == 0)
        def _():
            gna_ref[...] = jnp.zeros_like(gna_ref)
            gs_ref[...] = jnp.zeros_like(gs_ref)
            carry[...] = jnp.zeros_like(carry)

        ya = ya_ref[...]
        r = _rstd(ya)
        xhat = ya * r
        silu, dsilu = _silu_and_grad(ga_ref[...])
        do = doa_ref[...]
        dn = do * silu
        dqg_ref[:, D_ATTN:2 * D_ATTN] = (do * (xhat * gn_ref[...]) * dsilu).astype(BF16)
        gna_ref[...] += jnp.sum(dn * xhat, axis=0, keepdims=True)
        dya_buf[...] = _rms_bwd(dn * gn_ref[...], xhat, r).astype(BF16)

        geometry = _band_geometry(n_blocks - 1 - step)
        use_cur = geometry[0]
        lane = lax.broadcasted_iota(jnp.int32, (BLK, 128), 1)
        sink_lane = lax.broadcasted_iota(jnp.int32, (1, 128), 1)
        gsink = jnp.zeros((1, 128), F32)

        def fold(bd):
            return (jnp.where(lane < 64, bd[0:BLK], 0.0) + jnp.where(lane >= 64, bd[BLK:2 * BLK], 0.0),
                    jnp.where(lane < 64, bd[2 * BLK:3 * BLK], 0.0) + jnp.where(lane >= 64, bd[3 * BLK:4 * BLK], 0.0))

        pairs = range(N_HEADS // 2)
        qps, kbd, vbd, probs, sink_probs = _attn_probs(q_ref, kvc_ref, kvp_ref, sink_ref, geometry)
        dyps = [dya_buf[:, j * 128:(j + 1) * 128] for j in pairs]
        dps = []
        for j in pairs:
            dps += _merge(_nt(dyps[j], vbd[j // 4]), use_cur)
        deltas = [jnp.sum(p * dp, axis=-1, keepdims=True) for p, dp in zip(probs, dps)]
        dss = [p * (dp - delta) for p, dp, delta in zip(probs, dps, deltas)]
        for h in range(N_HEADS):
            dsink = -jnp.sum(sink_probs[h] * deltas[h], axis=0, keepdims=True)
            gsink = gsink + jnp.where(sink_lane == h, dsink, 0.0)
        gs_ref[...] += gsink
        ds4s = [_split(dss[2 * j], dss[2 * j + 1], use_cur).astype(BF16) for j in pairs]
        p4s = [_split(probs[2 * j], probs[2 * j + 1], use_cur).astype(BF16) for j in pairs]
        dqg_ref[:, 0:D_ATTN] = jnp.concatenate([_nn(ds4s[j], kbd[j // 4]) * SCALE for j in pairs], axis=1).astype(BF16)
        sums = []
        for group in range(N_HEADS // HEADS_PER_KV):
            acc = [jnp.zeros((BLK, 128), F32) for _ in range(4)]
            for j in range(group * 4, group * 4 + 4):
                for slot, part in enumerate(fold(_tn(ds4s[j], qps[j])) + fold(_tn(p4s[j], dyps[j]))):
                    acc[slot] = acc[slot] + part
            sums.append([a + pltpu.roll(a, 64, 1) for a in acc])
        dk_cur, dk_prev, dv_cur, dv_prev = (jnp.where(lane < 64, a, b) for a, b in zip(sums[0], sums[1]))
        dqg_ref[:, 2 * D_ATTN:2 * D_ATTN + 2 * D_KV] = (jnp.concatenate([dk_cur, dv_cur], axis=1) + carry[...]).astype(BF16)
        carry[...] = jnp.concatenate([dk_prev, dv_prev], axis=1)

        for at, stage in zip(stage_steps, _rs_wout_stages(gwo_ref, gwo_sh, *rs_scratch)):
            pl.when(step == at)(stage)

    row = pl.BlockSpec((BLK, D_ATTN), lambda i: (n_blocks - 1 - i, 0))
    kv_cur, kv_prev = _kv_specs(n_blocks, reverse=True)
    any_spec = pl.BlockSpec(memory_space=pl.ANY)
    return pl.pallas_call(
        body, name="attn_bwd", grid=(n_blocks,),
        out_shape=(jax.ShapeDtypeStruct((seq, D_QG), BF16), jax.ShapeDtypeStruct((1, D_ATTN), F32),
                   jax.ShapeDtypeStruct((1, 128), F32), jax.ShapeDtypeStruct((gwo.shape[0] // N_CHIPS, gwo.shape[1]), F32)),
        in_specs=[row, kv_cur, kv_prev, row, row, row, pl.BlockSpec(memory_space=pltpu.SMEM), _resident((1, D_ATTN)),
                  any_spec],
        out_specs=(pl.BlockSpec((BLK, D_QG), lambda i: (n_blocks - 1 - i, 0)),
                   pl.BlockSpec((1, D_ATTN), lambda i: (0, 0)), pl.BlockSpec((1, 128), lambda i: (0, 0)), any_spec),
        scratch_shapes=[pltpu.VMEM((BLK, 2 * D_KV), F32), pltpu.VMEM((BLK, D_ATTN), BF16)] + _rs_wout_scratch(gwo.shape),
        compiler_params=_params(44, ("arbitrary",)),
    )(q, kv, kv, ga, ya, doa, sinks, norm_attn_out, gwo)


GBLK = 256
GSUB = 64
PAIR_RING = 4
LAG_PAIR, LAG_HOP1, LAG_HOP2 = 1, 7, 14


def _bwd_in(dpc, dqg, h, wt, x, norm_in, dx2, small):
    seq = x.shape[0]
    n_blk = D_IN_PROJ // GBLK
    per_chip = n_blk // N_CHIPS
    n_slots = (n_blk + 1) // 2
    n_sub = GBLK // GSUB
    chip_rows = D_IN_PROJ // N_CHIPS
    tile = TOK_TILE
    n_tiles = seq // tile
    n_steps = n_blk + max(n_tiles, LAG_HOP2)
    chunk = min(seq, 512)
    blk_q, blk_kv, blk_ga = ROW_Q // GBLK, ROW_KV // GBLK, ROW_GA // GBLK

    def block_of(i):
        k = i % N_CHIPS
        robin = per_chip * ((k % 2) * 2 + k // 2) + i // N_CHIPS
        if isinstance(i, int):
            return robin if i < per_chip * N_CHIPS else i
        return jnp.where(i < per_chip * N_CHIPS, robin, i)

    def owner_of(i):
        return (i // N_CHIPS) % 2

    def slot_of(i):
        return (i // (2 * N_CHIPS)) * N_CHIPS + i % N_CHIPS

    def body(dpc_ref, dqg_ref, wt_ref, h_ref, x_ref, g_ref, dx2_ref, small_ref, gx_ref, small_sum, gwt_sh,
             dh_acc, gni, keep, pbuf, xbuf, land, land2, small_land,
             pair_send, pair_recv, h1_send, h1_recv, h2_send, h2_recv, sw_send, sw_recv, sm_send, sm_recv, out_sem):
        step = pl.program_id(0)
        x_i, y_i, c = lax.axis_index("x"), lax.axis_index("y"), lax.axis_index("c")
        me = 4 * x_i + 2 * y_i + c
        j = 2 * x_i + y_i
        pa = (_xor(x_i, 1 - c), _xor(y_i, c), c)
        pb = (_xor(x_i, c), _xor(y_i, 1 - c), c)
        sib = (x_i, y_i, 1 - c)
        ja = 2 * pa[0] + pa[1]
        jb = 2 * pb[0] + pb[1]
        jd = 3 - j

        def remote(src, dst, send, recv, to):
            return pltpu.make_async_remote_copy(src_ref=src, dst_ref=dst, send_sem=send, recv_sem=recv,
                                                device_id=to, device_id_type=MESH)

        def piece(ref, slot, u, n):
            return ref.at[slot, pl.ds(u * GSUB, n * GSUB), :]

        def chip_rows_at(ref, local, n):
            return ref.at[pl.ds(pl.multiple_of(local, GSUB), n * GSUB), :]

        def pair_copy(i):
            slot = slot_of(i)
            return remote(pbuf.at[i % PAIR_RING], land.at[slot], pair_send.at[slot], pair_recv.at[slot], sib)

        def h1_copy(slot, u, n):
            k = slot * n_sub + u
            return remote(piece(xbuf, slot, u, n), piece(xbuf, slot, u, n), h1_send.at[k], h1_recv.at[k], pa)

        def h2_copy(slot, u, n, local):
            k = slot * n_sub + u
            return remote(piece(xbuf, slot, u, n), chip_rows_at(land2, local, n), h2_send.at[k], h2_recv.at[k], pb)

        def sw_copy(slot, u, n, local):
            k = slot * n_sub + u
            return remote(piece(keep, slot, u, n), chip_rows_at(gwt_sh, local, n), sw_send.at[k], sw_recv.at[k], sib)

        def owned(i):
            return (i >= 0) & (i < n_blk) & (owner_of(i) == c)

        def chip_of(blk, u):
            row = blk * GBLK + u * GSUB
            chip = row // chip_rows
            return chip, row - chip * chip_rows

        def pieces(blk):
            first, local = chip_of(blk, 0)
            whole = first == chip_of(blk, n_sub - 1)[0]
            if isinstance(blk, int):
                return [(True, 0, n_sub, first, local)] if whole else [(True, u, 1) + chip_of(blk, u) for u in range(n_sub)]
            return [(whole, 0, n_sub, first, local)] + [(jnp.logical_not(whole), u, 1) + chip_of(blk, u) for u in range(n_sub)]

        @pl.when(step == 0)
        def _():
            dh_acc[...] = jnp.zeros_like(dh_acc)
            gni[...] = jnp.zeros_like(gni)

        @pl.when(step < n_blk)
        def _():
            from_pc = block_of(step) < blk_q
            block = _tn(jnp.where(from_pc, dpc_ref[...], dqg_ref[...]), h_ref[...])
            for t in range(0, seq, chunk):
                d = jnp.where(from_pc, dpc_ref[t:t + chunk, :], dqg_ref[t:t + chunk, :])
                dh_acc[t:t + chunk, :] += _nn(d, wt_ref[...])

            @pl.when(owner_of(step) == c)
            def _():
                keep[slot_of(step)] = block

            @pl.when(owner_of(step) != c)
            def _():
                @pl.when(step >= 2 * PAIR_RING)
                def _():
                    pair_copy(step - 2 * PAIR_RING).wait_send()
                pbuf[step % PAIR_RING] = block.astype(BF16)
                pair_copy(step).start()

        i1 = step - LAG_PAIR

        @pl.when(owned(i1))
        def _():
            slot = slot_of(i1)
            pair_copy(i1).wait_recv()
            _accumulate(keep.at[slot], land.at[slot])
            for cond, u, n, chip, _ in pieces(block_of(i1)):
                @pl.when(cond & ((chip == ja) | (chip == jd)))
                def _(u=u, n=n):
                    _cast_rows(piece(keep, slot, u, n), piece(xbuf, slot, u, n))
                    h1_copy(slot, u, n).start()

        i2 = step - LAG_HOP1

        @pl.when(owned(i2))
        def _():
            slot = slot_of(i2)
            for cond, u, n, chip, local in pieces(block_of(i2)):
                @pl.when(cond & ((chip == j) | (chip == jb)))
                def _(u=u, n=n, chip=chip, local=local):
                    h1_copy(slot, u, n).wait_recv()
                    _accumulate(piece(keep, slot, u, n), piece(xbuf, slot, u, n))

                    @pl.when(chip == jb)
                    def _():
                        _cast_rows(piece(keep, slot, u, n), piece(xbuf, slot, u, n))
                        h2_copy(slot, u, n, local).start()

                @pl.when(cond & ((chip == ja) | (chip == jd)))
                def _(u=u, n=n):
                    h1_copy(slot, u, n).wait_send()

        i3 = step - LAG_HOP2

        @pl.when(owned(i3))
        def _():
            slot = slot_of(i3)
            for cond, u, n, chip, local in pieces(block_of(i3)):
                @pl.when(cond & (chip == j))
                def _(u=u, n=n, local=local):
                    h2_copy(slot, u, n, local).wait_recv()
                    _accumulate(piece(keep, slot, u, n), chip_rows_at(land2, local, n))
                    mine = pltpu.make_async_copy(piece(keep, slot, u, n), chip_rows_at(gwt_sh, local, n), out_sem.at[0])
                    mine.start()
                    sw_copy(slot, u, n, local).start()
                    mine.wait()

                @pl.when(cond & (chip == jb))
                def _(u=u, n=n, local=local):
                    h2_copy(slot, u, n, local).wait_send()

        e = step - n_blk

        @pl.when((e >= 0) & (e < n_tiles))
        def _():
            dh = dh_acc[pl.ds(pl.multiple_of(e * tile, tile), tile), :]
            xv = x_ref[...]
            r = _rstd(xv)
            xhat = xv * r
            gni[...] += jnp.sum(dh * xhat, axis=0, keepdims=True)
            gx_ref[...] = _rms_bwd(dh * g_ref[...], xhat, r) + dx2_ref[...]

        @pl.when(step == n_steps - 1)
        def _():
            small_land[me] = small_ref[...]
            small_land[me, 0:1, :] = gni[...]
            others = [(dx, dy, dc) for dx in (0, 1) for dy in (0, 1) for dc in (0, 1)][1:]
            sends = [remote(small_land.at[me], small_land.at[me], sm_send.at[k], sm_recv.at[k],
                            (_xor(x_i, dx), _xor(y_i, dy), _xor(c, dc))) for k, (dx, dy, dc) in enumerate(others)]
            for cp in sends:
                cp.start()
            for i in range(n_blk):
                if i + 2 * PAIR_RING >= n_blk:
                    @pl.when(owner_of(i) != c)
                    def _(i=i):
                        pair_copy(i).wait_send()
                for _, u, n, chip, local in pieces(block_of(i)):
                    @pl.when((j == chip) & (c == owner_of(i)))
                    def _(i=i, u=u, n=n, local=local):
                        sw_copy(slot_of(i), u, n, local).wait_send()

                    @pl.when((j == chip) & (c != owner_of(i)))
                    def _(i=i, u=u, n=n, local=local):
                        sw_copy(slot_of(i), u, n, local).wait_recv()
            for cp in sends:
                cp.wait_recv()
            total = small_land[0]
            for dev in range(1, 8):
                total = total + small_land[dev]
            small_sum[...] = total
            for cp in sends:
                cp.wait_send()

    def blk_at(i):
        return block_of(jnp.clip(i, 0, n_blk - 1))

    last_pc_step = max(i for i in range(n_blk) if block_of(i) < blk_q)

    def next_block(i, in_pc):
        i = jnp.clip(i, 0, n_blk - 1)
        step = jnp.full_like(i, last_pc_step if in_pc else n_blk - 1)
        for ahead in reversed(range(N_CHIPS)):
            cand = jnp.minimum(i + ahead, n_blk - 1)
            step = jnp.where((block_of(cand) < blk_q) == in_pc, cand, step)
        return block_of(step)

    def dqg_block(i):
        b = next_block(i, False)
        q_blk = jnp.clip(b - blk_q, 0, blk_kv - blk_q - 1)
        ga_blk = (D_ATTN // GBLK) + jnp.clip(b - blk_ga, 0, n_blk - blk_ga - 1)
        return jnp.where(b < blk_kv, q_blk, jnp.where(b == blk_kv, 2 * D_ATTN // GBLK, ga_blk))

    def tok(i):
        return (jnp.clip(i - n_blk, 0, n_tiles - 1), 0)

    n_piece = n_slots * n_sub
    dma = pltpu.SemaphoreType.DMA
    return pl.pallas_call(
        body, name="bwd_in", grid=(n_steps,),
        out_shape=(jax.ShapeDtypeStruct((seq, D_MODEL), F32), jax.ShapeDtypeStruct(small.shape, F32),
                   jax.ShapeDtypeStruct((chip_rows, D_MODEL), F32)),
        in_specs=[pl.BlockSpec((seq, GBLK), lambda i: (0, next_block(i, True))),
                  pl.BlockSpec((seq, GBLK), lambda i: (0, dqg_block(i))),
                  pl.BlockSpec((GBLK, D_MODEL), lambda i: (blk_at(i), 0)),
                  _resident(h.shape),
                  pl.BlockSpec((tile, D_MODEL), tok), _resident((1, D_MODEL)), pl.BlockSpec((tile, D_MODEL), tok),
                  _resident(small.shape)],
        out_specs=(pl.BlockSpec((tile, D_MODEL), tok), pl.BlockSpec(small.shape, lambda i: (0, 0)),
                   pl.BlockSpec(memory_space=pl.ANY)),
        scratch_shapes=[pltpu.VMEM((seq, D_MODEL), F32), pltpu.VMEM((1, D_MODEL), F32),
                        pltpu.VMEM((n_slots, GBLK, D_MODEL), F32), pltpu.VMEM((PAIR_RING, GBLK, D_MODEL), BF16),
                        pltpu.VMEM((n_slots, GBLK, D_MODEL), BF16), pltpu.VMEM((n_slots, GBLK, D_MODEL), BF16),
                        pltpu.VMEM((chip_rows, D_MODEL), BF16), pltpu.VMEM((8,) + small.shape, F32),
                        dma((n_slots,)), dma((n_slots,)), dma((n_piece,)), dma((n_piece,)), dma((n_piece,)),
                        dma((n_piece,)), dma((n_piece,)), dma((n_piece,)), dma((7,)), dma((7,)), dma((1,))],
        compiler_params=_params(62, ("arbitrary",)),
    )(dpc, dqg, wt, h, x, norm_in, dx2, small)


def _accumulate(dst_ref, src_ref, rows=16):
    def step(i, carry):
        sl = pl.ds(pl.multiple_of(i * rows, rows), rows)
        dst_ref[sl, :] = dst_ref[sl, :] + src_ref[sl, :].astype(F32)
        return carry
    lax.fori_loop(0, dst_ref.shape[0] // rows, step, 0)


def _rs_wout_scratch(gwo_shape):
    o_half, width = gwo_shape[0] // N_CHIPS // 2, gwo_shape[1]
    return [pltpu.VMEM((4, o_half, width), F32), pltpu.VMEM((4, o_half, width), BF16),
            pltpu.VMEM((4, o_half, width), BF16), pltpu.VMEM((2, o_half, width), BF16),
            pltpu.VMEM((o_half, width), BF16),
            pltpu.SemaphoreType.DMA((8,)), pltpu.SemaphoreType.DMA((8,)), pltpu.SemaphoreType.DMA((4,))]


def _rs_wout_stages(gwo_ref, gwo_sh, acc_o, sb_o, r1o, r2o, r3o, send_sems, recv_sems, local_sems):
    o_rows = gwo_ref.shape[0] // N_CHIPS
    o_half = o_rows // 2
    x, y, c = lax.axis_index("x"), lax.axis_index("y"), lax.axis_index("c")
    j = 2 * x + y
    pa = (_xor(x, 1 - c), _xor(y, c), c)
    pb = (_xor(x, c), _xor(y, 1 - c), c)
    sib = (x, y, 1 - c)
    ja = 2 * pa[0] + pa[1]
    jb = 2 * pb[0] + pb[1]
    jd = 3 - j
    order = (ja, jd, jb, j)
    sib_order = (jb, jd, ja, j)

    def rcopy(k, src, dst, to):
        return pltpu.make_async_remote_copy(src_ref=src, dst_ref=dst, send_sem=send_sems.at[k],
                                            recv_sem=recv_sems.at[k], device_id=to, device_id_type=MESH)

    def o_rows_of(chip, half):
        return gwo_ref.at[pl.ds(pl.multiple_of(chip * o_rows + half * o_half, 8), o_half), :]

    def load_all(chips, half):
        cps = [pltpu.make_async_copy(o_rows_of(chip, half), acc_o.at[s], local_sems.at[s]) for s, chip in enumerate(chips)]
        for cp in cps:
            cp.start()
        return cps

    def pair_send(s):
        return rcopy(s, sb_o.at[s], r1o.at[s], sib)

    def out_half(half):
        return gwo_sh.at[pl.ds(pl.multiple_of(half * o_half, 8), o_half), :]

    hop1 = [rcopy(4 + s, sb_o.at[s], r2o.at[s], pa) for s in range(2)]
    hop2 = rcopy(6, sb_o.at[2], r3o, pb)
    swap = rcopy(7, acc_o.at[3], out_half(c), sib)
    mine = pltpu.make_async_copy(acc_o.at[3], out_half(c), local_sems.at[0])

    def resend(s, copy):
        pair_send(s).wait_send()
        _cast_rows(acc_o.at[s], sb_o.at[s])
        copy.start()

    def stage_pair():
        for s, cp in enumerate(load_all(sib_order, 1 - c)):
            cp.wait()
            _cast_rows(acc_o.at[s], sb_o.at[s])
            pair_send(s).start()

    def stage_hop1():
        for s, cp in enumerate(load_all(order, c)):
            cp.wait()
            pair_send(s).wait_recv()
            _accumulate(acc_o.at[s], r1o.at[s])
            if s < 2:
                resend(s, hop1[s])

    def stage_hop2():
        hop1[1].wait_recv()
        _accumulate(acc_o.at[2], r2o.at[1])
        resend(2, hop2)
        hop1[0].wait_recv()
        _accumulate(acc_o.at[3], r2o.at[0])

    def stage_final():
        hop2.wait_recv()
        _accumulate(acc_o.at[3], r3o)
        mine.start()
        swap.start()

    def stage_drain():
        rcopy(7, acc_o.at[3], out_half(1 - c), sib).wait_recv()
        for cp in [pair_send(3)] + hop1 + [hop2, swap]:
            cp.wait_send()
        mine.wait()

    return stage_pair, stage_hop1, stage_hop2, stage_final, stage_drain


def _adamw(name, w, g, m, v, rows):
    def body(w_ref, g_ref, m_ref, v_ref, go_ref, d_ref, nm_ref, nv_ref):
        gv = g_ref[...]
        go_ref[...] = gv
        nm = ADAM_B1 * m_ref[...] + (1.0 - ADAM_B1) * gv
        nv = ADAM_B2 * v_ref[...] + (1.0 - ADAM_B2) * (gv * gv)
        m_hat = nm / (1.0 - ADAM_B1 ** ADAM_STEP)
        v_hat = nv / (1.0 - ADAM_B2 ** ADAM_STEP)
        d_ref[...] = -ADAM_LR * (m_hat / (jnp.sqrt(v_hat) + ADAM_EPS) + ADAM_WD * w_ref[...])
        nm_ref[...] = nm
        nv_ref[...] = nv

    spec = pl.BlockSpec((rows, w.shape[1]), lambda i: (i, 0))
    shape = jax.ShapeDtypeStruct(w.shape, F32)
    return pl.pallas_call(
        body, name="adamw_" + name, grid=(w.shape[0] // rows,),
        out_shape=(shape,) * 4, in_specs=[spec] * 4, out_specs=(spec,) * 4,
        compiler_params=_params(32, ("arbitrary",)),
    )(w, g, m, v)


def kernel(x, norm_in, w_in, conv_w, attn_sinks, norm_conv_out, norm_attn_out, w_out, norm_final, loss_target, m_norm_in, m_w_in, m_conv_w, m_attn_sinks, m_norm_conv_out, m_norm_attn_out, m_w_out, m_norm_final, v_norm_in, v_w_in, v_conv_w, v_attn_sinks, v_norm_conv_out, v_norm_attn_out, v_w_out, v_norm_final):
    chip = 2 * lax.axis_index("x") + lax.axis_index("y")
    xs, target = x[0], loss_target[0]
    norm_final2 = norm_final.reshape(1, D_MODEL)
    w_in_t, m_w_in_t, v_w_in_t = w_in[0].T, m_w_in[0].T, v_w_in[0].T

    wt, cw4 = _ag_weights(w_in_t, conv_w[0])
    cw = jnp.transpose(cw4, (1, 0, 2)).reshape(3, D_CONV)

    h, pc, q, kv, ga, oc, wo = _fwd_in(xs, norm_in, wt, w_out[0], cw, norm_conv_out)
    ya, oa = _attn_fwd(q, kv, ga, attn_sinks, norm_attn_out)
    dx2, doa, gwo, gnf, loss_lanes, dpc, gnc, gcw = _out_proj_loss(xs, oc, oa, wo, norm_final2, target, pc, cw,
                                                                   norm_conv_out)
    dqg, gna, gsink, gwo_sh = _attn_bwd(q, kv, ga, ya, doa, attn_sinks, norm_attn_out, gwo)

    last = jnp.zeros((1, D_MODEL), F32).at[:, 0:N_HEADS].set(gsink[:, 0:N_HEADS]).at[0, N_HEADS].set(jnp.sum(loss_lanes))
    small = jnp.concatenate([jnp.zeros((1, D_MODEL), F32), gnc, gna, gnf, gcw[0:3], last], axis=0)
    grad_x, small_sum, gwt_sh = _bwd_in(dpc, dqg, h, wt, xs, norm_in, dx2, small)

    loss = small_sum[7, N_HEADS]
    g_norm_in, g_norm_conv, g_norm_attn = small_sum[0:1], small_sum[1:2], small_sum[2:3]
    g_norm_final = small_sum[3]
    g_conv_w = lax.dynamic_slice(small_sum[4:7], (0, chip * (D_CONV // N_CHIPS)), (3, D_CONV // N_CHIPS))[None]
    g_sinks = small_sum[7:8, 0:N_HEADS]

    def adam(name, w, g, m, v, rows):
        shape = w.shape
        two_d = (-1, shape[-1])
        out = _adamw(name, w.reshape(two_d), g.reshape(two_d), m.reshape(two_d), v.reshape(two_d), rows)
        return tuple(o.reshape(shape) for o in out)

    weights = (norm_in, w_in, conv_w, attn_sinks, norm_conv_out, norm_attn_out, w_out, norm_final)
    grads = (g_norm_in, None, g_conv_w, g_sinks, g_norm_conv, g_norm_attn, gwo_sh[None], g_norm_final)
    moments_m = (m_norm_in, m_w_in, m_conv_w, m_attn_sinks, m_norm_conv_out, m_norm_attn_out, m_w_out, m_norm_final)
    moments_v = (v_norm_in, v_w_in, v_conv_w, v_attn_sinks, v_norm_conv_out, v_norm_attn_out, v_w_out, v_norm_final)
    rows = (1, None, 3, 1, 1, 1, 128, 1)
    names = ("norm_in", "w_in", "conv_w", "attn_sinks", "norm_conv_out", "norm_attn_out", "w_out", "norm_final")
    updates = []
    for args in zip(names, weights, grads, moments_m, moments_v, rows):
        if args[0] == "w_in":
            out_t = _adamw("w_in", w_in_t, gwt_sh, m_w_in_t, v_w_in_t, 200)
            updates.append(tuple(o.T[None] for o in out_t))
        else:
            updates.append(adam(*args))
    grads_out, deltas, new_m, new_v = zip(*updates)
    return (loss, grad_x[None], *grads_out, *deltas, *new_m, *new_v)
```

```python
import jax
import jax.numpy as jnp
from jax import lax
from jax.experimental import pallas as pl
from jax.experimental.pallas import tpu as pltpu

F32 = jnp.float32
BF16 = jnp.bfloat16
MESH = pl.DeviceIdType.MESH

D_MODEL = 1024
D_CONV = 1024
D_ATTN = 1024
D_KV = 128
D_QG = 2 * D_ATTN + 2 * D_KV
D_MIX = D_CONV + D_ATTN
D_PC = 4 * D_CONV
D_IN_PROJ = D_PC + 2 * D_ATTN + 2 * D_KV
ROW_Q = D_PC
ROW_KV = ROW_Q + D_ATTN
ROW_GA = ROW_KV + 2 * D_KV
N_HEADS = 16
HEAD_DIM = 64
HEADS_PER_KV = 8
BLK = 128
N_CHIPS = 4
RMS_EPS = 1e-5
SCALE = HEAD_DIM ** -0.5
SLOPES = tuple(2.0 ** (-8.0 * (h + 1) / N_HEADS) for h in range(N_HEADS))

ADAM_LR, ADAM_B1, ADAM_B2, ADAM_EPS, ADAM_WD, ADAM_STEP = 0.001, 0.9, 0.999, 1e-08, 0.01, 10

SMALL_ROWS = 8
SMALL_NORM_IN, SMALL_NORM_CONV, SMALL_NORM_ATTN, SMALL_NORM_FINAL, SMALL_CONV_W, SMALL_MISC = 0, 1, 2, 3, 4, 7
SMALL_LOSS_LANE = N_HEADS

TOK_TILE = 256
MIB = 1 << 20


def _params(vmem_mib, semantics=None):
    return pltpu.CompilerParams(dimension_semantics=semantics, vmem_limit_bytes=vmem_mib * MIB)


def _nn(a, b):
    return jnp.dot(a, b, preferred_element_type=F32)


def _nt(a, b):
    return lax.dot_general(a, b, (((1,), (1,)), ((), ())), preferred_element_type=F32)


def _tn(a, b):
    return lax.dot_general(a, b, (((0,), (0,)), ((), ())), preferred_element_type=F32)


def _rstd(v):
    return lax.rsqrt(jnp.mean(v * v, axis=-1, keepdims=True) + RMS_EPS)


def _rms_bwd(g, xhat, rstd):
    return rstd * (g - xhat * jnp.mean(g * xhat, axis=-1, keepdims=True))


def _silu_and_grad(g):
    s = jax.nn.sigmoid(g)
    return g * s, s * (1.0 + g * (1.0 - s))


def _resident(shape):
    return pl.BlockSpec(shape, lambda *_: (0,) * len(shape), pipeline_mode=pl.Buffered(1))


def _xor(a, b):
    return a + b - 2 * a * b


def _cast_rows(src_ref, dst_ref, rows=32):
    def step(i, carry):
        sl = pl.ds(pl.multiple_of(i * rows, rows), rows)
        dst_ref[sl, :] = src_ref[sl, :].astype(dst_ref.dtype)
        return carry
    lax.fori_loop(0, src_ref.shape[0] // rows, step, 0)


def _ag_scratch(shard_shape):
    rows, width = shard_shape
    return [pltpu.VMEM((rows, width), F32), pltpu.VMEM((rows, width), BF16), pltpu.VMEM((3, rows // 2, width), BF16),
            pltpu.SemaphoreType.DMA((6,)), pltpu.SemaphoreType.DMA((6,)), pltpu.SemaphoreType.DMA((4,))]


def _ag_stages(sh_ref, out, f32_buf, own, land, send_sems, recv_sems, local_sems):
    rows = sh_ref.shape[0]
    half = rows // 2
    x, y, c = lax.axis_index("x"), lax.axis_index("y"), lax.axis_index("c")
    j = 2 * x + y
    p1 = (_xor(x, c), _xor(y, 1 - c), c)
    p2 = (_xor(x, 1 - c), _xor(y, c), c)
    sib = (x, y, 1 - c)
    j1 = 2 * p1[0] + p1[1]
    j2 = 2 * p2[0] + p2[1]
    j3 = 3 - j

    def rows_of(chip, hf):
        return out.at[pl.ds(pl.multiple_of(chip * rows + hf * half, 16), half), :]

    def rcopy(k, src, dst, to):
        return pltpu.make_async_remote_copy(src_ref=src, dst_ref=dst, send_sem=send_sems.at[k],
                                            recv_sem=recv_sems.at[k], device_id=to, device_id_type=MESH)

    my_half = own.at[pl.ds(pl.multiple_of(c * half, 16), half), :]
    hop1 = rcopy(0, my_half, land.at[0], p1)
    hop2_own = rcopy(1, my_half, land.at[1], p2)
    hop2_fwd = rcopy(2, land.at[0], land.at[2], p2)
    swaps = [rcopy(3 + s, land.at[s], rows_of(chip, c), sib) for s, chip in enumerate((j1, j2, j3))]
    keeps = [pltpu.make_async_copy(land.at[s], rows_of(chip, c), local_sems.at[1 + s]) for s, chip in enumerate((j1, j2, j3))]
    load = pltpu.make_async_copy(sh_ref, f32_buf, local_sems.at[0])
    own_out = pltpu.make_async_copy(own, out.at[pl.ds(pl.multiple_of(j * rows, 16), rows), :], local_sems.at[0])

    def stage_send():
        load.start()
        load.wait()
        _cast_rows(f32_buf, own)
        own_out.start()
        hop1.start()

    def stage_forward():
        hop1.wait_recv()
        hop2_own.start()
        hop2_fwd.start()
        swaps[0].start()
        keeps[0].start()

    def stage_publish():
        hop2_own.wait_recv()
        swaps[1].start()
        keeps[1].start()
        hop2_fwd.wait_recv()
        swaps[2].start()
        keeps[2].start()

    def stage_drain():
        for s, chip in enumerate((j2, j1, j3)):
            rcopy(3 + s, my_half, rows_of(chip, 1 - c), sib).wait_recv()
        for cp in [hop1, hop2_own, hop2_fwd] + swaps:
            cp.wait_send()
        for cp in [own_out] + keeps:
            cp.wait()

    return stage_send, stage_forward, stage_publish, stage_drain


def _ag_weights(wt_sh, cw_sh):
    def body(wt_ref, cw_ref, wt_out, cw_out, *scratch):
        cw_send, cw_recv, cw_local = scratch[-3:]
        x, y, c = lax.axis_index("x"), lax.axis_index("y"), lax.axis_index("c")
        j = 2 * x + y
        p1 = (_xor(x, c), _xor(y, 1 - c), c)
        p2 = (_xor(x, 1 - c), _xor(y, c), c)
        j1 = 2 * p1[0] + p1[1]

        def cw_copy(k, chip, to):
            src = cw_ref if k != 2 else cw_out.at[chip]
            return pltpu.make_async_remote_copy(src_ref=src, dst_ref=cw_out.at[chip], send_sem=cw_send.at[k],
                                                recv_sem=cw_recv.at[k], device_id=to, device_id_type=MESH)

        mine = pltpu.make_async_copy(cw_ref, cw_out.at[j], cw_local.at[0])
        mine.start()
        first = cw_copy(0, j, p1)
        first.start()
        stages = _ag_stages(wt_ref, wt_out, *scratch[:-3])
        stages[0]()
        first.wait_recv()
        second = [cw_copy(1, j, p2), cw_copy(2, j1, p2)]
        for cp in second:
            cp.start()
        for stage in stages[1:]:
            stage()
        for cp in second:
            cp.wait_recv()
        for cp in [first] + second:
            cp.wait_send()
        mine.wait()

    any_spec = pl.BlockSpec(memory_space=pl.ANY)
    dma = pltpu.SemaphoreType.DMA
    return pl.pallas_call(
        body, name="ag_weights",
        out_shape=(jax.ShapeDtypeStruct((N_CHIPS * wt_sh.shape[0], wt_sh.shape[1]), BF16),
                   jax.ShapeDtypeStruct((N_CHIPS,) + cw_sh.shape, cw_sh.dtype)),
        in_specs=[any_spec, any_spec], out_specs=(any_spec, any_spec),
        scratch_shapes=_ag_scratch(wt_sh.shape) + [dma((3,)), dma((3,)), dma((1,))],
        compiler_params=_params(32),
    )(wt_sh, cw_sh)


def _fwd_in(x, norm_in, wt, wo_sh, conv_w, norm_conv_out):
    seq = x.shape[0]
    tile = TOK_TILE
    n_tiles = seq // tile
    stage_steps = (0, n_tiles // 4, (5 * n_tiles) // 8, n_tiles - 1)

    def body(x_ref, g_ref, wt_ref, wo_ref, cw_ref, gn_ref, h_ref, pc_ref, q_ref, kv_ref, ga_ref, oc_ref, wo_out,
             zbuf, *ag_scratch):
        stages = _ag_stages(wo_ref, wo_out, *ag_scratch)
        for at, stage in zip(stage_steps[:-1], stages[:-1]):
            pl.when(pl.program_id(0) == at)(stage)

        @pl.when(pl.program_id(0) == 0)
        def _():
            zbuf[0:8, :] = jnp.zeros((8, D_CONV), F32)

        xv = x_ref[...]
        h = (xv * _rstd(xv) * g_ref[...]).astype(BF16)
        h_ref[...] = h
        for blk in range(D_PC // D_CONV):
            pc_ref[:, blk * D_CONV:(blk + 1) * D_CONV] = _nt(h, wt_ref[blk * D_CONV:(blk + 1) * D_CONV, :])
        q_ref[...] = _nt(h, wt_ref[ROW_Q:ROW_KV, :])
        kv_ref[...] = _nt(h, wt_ref[ROW_KV:ROW_GA, :])
        ga_ref[...] = _nt(h, wt_ref[ROW_GA:D_IN_PROJ, :])

        cb, _, _, _, _, _, conv = _conv_core(pc_ref, zbuf, cw_ref)
        yc = cb * conv
        silu, _ = _silu_and_grad(pc_ref[:, 3 * D_CONV:4 * D_CONV])
        oc_ref[...] = (yc * _rstd(yc) * gn_ref[...] * silu).astype(BF16)
        zbuf[0:8, :] = zbuf[tile:tile + 8, :]
        pl.when(pl.program_id(0) == stage_steps[-1])(stages[-1])

    def row(width):
        return pl.BlockSpec((tile, width), lambda i: (i, 0))

    any_spec = pl.BlockSpec(memory_space=pl.ANY)
    return pl.pallas_call(
        body, name="fwd_in", grid=(n_tiles,),
        out_shape=(jax.ShapeDtypeStruct((seq, D_MODEL), BF16), jax.ShapeDtypeStruct((seq, D_PC), F32),
                   jax.ShapeDtypeStruct((seq, D_ATTN), F32), jax.ShapeDtypeStruct((seq, 2 * D_KV), F32),
                   jax.ShapeDtypeStruct((seq, D_ATTN), F32), jax.ShapeDtypeStruct((seq, D_CONV), BF16),
                   jax.ShapeDtypeStruct((N_CHIPS * wo_sh.shape[0], wo_sh.shape[1]), BF16)),
        in_specs=[row(D_MODEL), _resident((1, D_MODEL)), _resident(wt.shape), any_spec,
                  _resident(conv_w.shape), _resident((1, D_CONV))],
        out_specs=(row(D_MODEL), row(D_PC), row(D_ATTN), row(2 * D_KV), row(D_ATTN), row(D_CONV), any_spec),
        scratch_shapes=[pltpu.VMEM((tile + 8, D_CONV), F32)] + _ag_scratch(wo_sh.shape),
        compiler_params=_params(56, ("arbitrary",)),
    )(x, norm_in, wt, wo_sh, conv_w, norm_conv_out)


def _conv_core(pc_ref, zbuf, cw_ref):
    tile = pc_ref.shape[0]
    cb = pc_ref[:, 0:D_CONV]
    cc = pc_ref[:, D_CONV:2 * D_CONV]
    cu = pc_ref[:, 2 * D_CONV:3 * D_CONV]
    z = cc * cu
    zbuf[8:tile + 8, :] = z
    z1 = zbuf[7:tile + 7, :]
    z2 = zbuf[6:tile + 6, :]
    conv = cw_ref[0:1, :] * z2 + cw_ref[1:2, :] * z1 + cw_ref[2:3, :] * z
    return cb, cc, cu, z, z1, z2, conv


def _band_geometry(block_index):
    qi = lax.broadcasted_iota(jnp.int32, (BLK, BLK), 0)
    kp = lax.broadcasted_iota(jnp.int32, (BLK, BLK), 1)
    use_cur = kp <= qi
    dist = jnp.where(use_cur, qi - kp, qi - kp + BLK).astype(F32)
    valid = use_cur | (block_index > 0)
    return use_cur, dist, valid


def _block_diag(cur, prev, group):
    lane = lax.broadcasted_iota(jnp.int32, cur.shape, 1)

    def halves(t):
        other = pltpu.roll(t, 64, 1)
        lo, hi = (t, other) if group == 0 else (other, t)
        return jnp.where(lane < 64, lo, 0.0), jnp.where(lane >= 64, hi, 0.0)

    return jnp.concatenate(halves(cur) + halves(prev), axis=0).astype(BF16)


def _merge(s4, use_cur):
    return (jnp.where(use_cur, s4[:, 0:BLK], s4[:, 2 * BLK:3 * BLK]),
            jnp.where(use_cur, s4[:, BLK:2 * BLK], s4[:, 3 * BLK:4 * BLK]))


def _split(a, b, use_cur):
    return jnp.concatenate([jnp.where(use_cur, a, 0.0), jnp.where(use_cur, b, 0.0),
                            jnp.where(use_cur, 0.0, a), jnp.where(use_cur, 0.0, b)], axis=1)


def _softmax_head(s, head, sink, dist, valid):
    sc = jnp.where(valid, s - SLOPES[head] * dist, -jnp.inf)
    m = jnp.maximum(jnp.max(sc, axis=-1, keepdims=True), sink)
    p = jnp.exp(sc - m)
    es = jnp.exp(sink - m)
    inv = 1.0 / (jnp.sum(p, axis=-1, keepdims=True) + es)
    return p * inv, es * inv


def _attn_probs(q_ref, kvc_ref, kvp_ref, sink_ref, geometry):
    use_cur, dist, valid = geometry
    groups = range(N_HEADS // HEADS_PER_KV)
    kbd = [_block_diag(kvc_ref[:, 0:D_KV], kvp_ref[:, 0:D_KV], g) for g in groups]
    vbd = [_block_diag(kvc_ref[:, D_KV:2 * D_KV], kvp_ref[:, D_KV:2 * D_KV], g) for g in groups]
    qps = [(q_ref[:, j * 128:(j + 1) * 128] * SCALE).astype(BF16) for j in range(N_HEADS // 2)]
    scores = []
    for j, qp in enumerate(qps):
        scores += _merge(_nt(qp, kbd[j // 4]), use_cur)
    sinks = [sink_ref[0, h] for h in range(N_HEADS)]
    scores = [jnp.where(valid, s - SLOPES[h] * dist, -jnp.inf) for h, s in enumerate(scores)]
    maxes = [jnp.maximum(jnp.max(s, axis=-1, keepdims=True), sinks[h]) for h, s in enumerate(scores)]
    exps = [jnp.exp(s - m) for s, m in zip(scores, maxes)]
    sink_exps = [jnp.exp(sinks[h] - m) for h, m in enumerate(maxes)]
    invs = [1.0 / (jnp.sum(e, axis=-1, keepdims=True) + se) for e, se in zip(exps, sink_exps)]
    probs = [e * inv for e, inv in zip(exps, invs)]
    sink_probs = [se * inv for se, inv in zip(sink_exps, invs)]
    return qps, kbd, vbd, probs, sink_probs


def _kv_specs(n_blocks, reverse):
    def blk(i):
        return (n_blocks - 1 - i) if reverse else i
    cur = pl.BlockSpec((BLK, 2 * D_KV), lambda i: (blk(i), 0))
    prev = pl.BlockSpec((BLK, 2 * D_KV), lambda i: (jnp.maximum(blk(i) - 1, 0), 0))
    return cur, prev


def _attn_fwd(q, kv, ga, sinks, norm_attn_out):
    seq = q.shape[0]
    n_blocks = seq // BLK

    def body(q_ref, kvc_ref, kvp_ref, ga_ref, sink_ref, gn_ref, ya_ref, oa_ref):
        geometry = _band_geometry(pl.program_id(0))
        _, _, vbd, probs, _ = _attn_probs(q_ref, kvc_ref, kvp_ref, sink_ref, geometry)
        p4s = [_split(probs[2 * j], probs[2 * j + 1], geometry[0]).astype(BF16) for j in range(N_HEADS // 2)]
        ya = jnp.concatenate([_nn(p4, vbd[j // 4]) for j, p4 in enumerate(p4s)], axis=1)
        ya_ref[...] = ya
        silu, _ = _silu_and_grad(ga_ref[...])
        oa_ref[...] = (ya * _rstd(ya) * gn_ref[...] * silu).astype(BF16)

    row = pl.BlockSpec((BLK, D_ATTN), lambda i: (i, 0))
    kv_cur, kv_prev = _kv_specs(n_blocks, reverse=False)
    return pl.pallas_call(
        body, name="attn_fwd", grid=(n_blocks,),
        out_shape=(jax.ShapeDtypeStruct((seq, D_ATTN), F32), jax.ShapeDtypeStruct((seq, D_ATTN), BF16)),
        in_specs=[row, kv_cur, kv_prev, row, pl.BlockSpec(memory_space=pltpu.SMEM), _resident((1, D_ATTN))],
        out_specs=(row, row),
        compiler_params=_params(32, ("arbitrary",)),
    )(q, kv, kv, ga, sinks, norm_attn_out)


def _out_proj_loss(x, oc, oa, wo, norm_final, target, pc, conv_w, norm_conv_out):
    seq = x.shape[0]
    tile = TOK_TILE
    n_tiles = seq // tile

    def body(x_ref, oc_ref, oa_ref, wo_ref, gf_ref, t_ref, pc_ref, hcc_ref, hcu_ref, cw_ref, gn_ref,
             dx2_ref, doa_ref, gwo_ref, dpc_ref, small_ref, loss_acc, zbuf, dbuf):
        step = pl.program_id(0)

        def small_add(row, value):
            small_ref[row:row + 1, :] += jnp.sum(value, axis=0, keepdims=True)

        @pl.when(step == 0)
        def _():
            gwo_ref[...] = jnp.zeros_like(gwo_ref)
            small_ref[...] = jnp.zeros_like(small_ref)
            loss_acc[...] = jnp.zeros_like(loss_acc)
            dbuf[tile:tile + 8, :] = jnp.zeros((8, D_CONV), F32)

        oc, oa = oc_ref[...], oa_ref[...]
        x2 = x_ref[...] + _nn(oc, wo_ref[0:D_CONV, :]) + _nn(oa, wo_ref[D_CONV:D_MIX, :])
        r = _rstd(x2)
        xhat = x2 * r
        err = xhat * gf_ref[...] - t_ref[...]
        loss_acc[...] += jnp.sum(err * err, axis=0, keepdims=True) * (0.5 / D_MODEL)
        dy = err * (1.0 / D_MODEL)
        small_add(SMALL_NORM_FINAL, dy * xhat)
        dx2 = _rms_bwd(dy * gf_ref[...], xhat, r)
        dx2_ref[...] = dx2
        db = dx2.astype(BF16)
        do = _nt(db, wo_ref[0:D_CONV, :])
        doa_ref[...] = _nt(db, wo_ref[D_CONV:D_MIX, :])
        gwo_ref[0:D_CONV, :] += _tn(oc, db)
        gwo_ref[D_CONV:D_MIX, :] += _tn(oa, db)

        is_first_tile = step == n_tiles - 1
        zbuf[0:8, :] = jnp.where(is_first_tile, 0.0, hcc_ref[...] * hcu_ref[...])
        cb, cc, cu, z, z1, z2, conv = _conv_core(pc_ref, zbuf, cw_ref)
        silu, dsilu = _silu_and_grad(pc_ref[:, 3 * D_CONV:4 * D_CONV])
        yc = cb * conv
        rc = _rstd(yc)
        chat = yc * rc
        dn = do * silu
        dpc_ref[:, 3 * D_CONV:4 * D_CONV] = (do * (chat * gn_ref[...]) * dsilu).astype(BF16)
        small_add(SMALL_NORM_CONV, dn * chat)
        dyc = _rms_bwd(dn * gn_ref[...], chat, rc)
        dpc_ref[:, 0:D_CONV] = (dyc * conv).astype(BF16)
        dconv = dyc * cb
        small_add(SMALL_CONV_W, dconv * z2)
        small_add(SMALL_CONV_W + 1, dconv * z1)
        small_add(SMALL_CONV_W + 2, dconv * z)
        dbuf[0:tile, :] = dconv
        dz = cw_ref[2:3, :] * dconv + cw_ref[1:2, :] * dbuf[1:tile + 1, :] + cw_ref[0:1, :] * dbuf[2:tile + 2, :]
        dpc_ref[:, D_CONV:2 * D_CONV] = (dz * cu).astype(BF16)
        dpc_ref[:, 2 * D_CONV:3 * D_CONV] = (dz * cc).astype(BF16)
        dbuf[tile:tile + 8, :] = dbuf[0:8, :]

        @pl.when(step == n_tiles - 1)
        def _():
            lane = lax.broadcasted_iota(jnp.int32, (1, D_MODEL), 1)
            small_ref[SMALL_MISC:SMALL_MISC + 1, :] = jnp.where(lane == SMALL_LOSS_LANE, jnp.sum(loss_acc[...]), 0.0)

    def rev(i):
        return n_tiles - 1 - i

    def row(width):
        return pl.BlockSpec((tile, width), lambda i: (rev(i), 0))

    def halo(col_block):
        return pl.BlockSpec((8, D_CONV), lambda i: (jnp.maximum(rev(i) * (tile // 8) - 1, 0), col_block))

    def const(shape):
        return pl.BlockSpec(shape, lambda i: (0, 0))

    return pl.pallas_call(
        body, name="out_proj_loss", grid=(n_tiles,),
        out_shape=(jax.ShapeDtypeStruct((seq, D_MODEL), F32), jax.ShapeDtypeStruct((seq, D_ATTN), F32),
                   jax.ShapeDtypeStruct((D_MIX, D_MODEL), F32), jax.ShapeDtypeStruct((seq, D_PC), BF16),
                   jax.ShapeDtypeStruct((SMALL_ROWS, D_MODEL), F32)),
        in_specs=[row(D_MODEL), row(D_CONV), row(D_ATTN), _resident(wo.shape), _resident((1, D_MODEL)), row(D_MODEL),
                  row(D_PC), halo(1), halo(2), _resident(conv_w.shape), _resident((1, D_CONV))],
        out_specs=(row(D_MODEL), row(D_ATTN), const((D_MIX, D_MODEL)), row(D_PC), const((SMALL_ROWS, D_MODEL))),
        scratch_shapes=[pltpu.VMEM((1, D_MODEL), F32), pltpu.VMEM((tile + 8, D_CONV), F32),
                        pltpu.VMEM((tile + 8, D_CONV), F32)],
        compiler_params=_params(62, ("arbitrary",)),
    )(x, oc, oa, wo, norm_final, target, pc, pc, pc, conv_w, norm_conv_out)


def _attn_bwd(q, kv, ga, ya, doa, sinks, norm_attn_out, gwo, small):
    seq = q.shape[0]
    n_blocks = seq // BLK
    stage_steps = (0, n_blocks // 4, n_blocks // 2, (3 * n_blocks) // 4, n_blocks - 1)

    def body(q_ref, kvc_ref, kvp_ref, ga_ref, ya_ref, doa_ref, sink_ref, gn_ref, gwo_ref, small_ref,
             dqg_ref, small_out, gwo_sh, gna_ref, gs_ref, carry, dya_buf, *rs_scratch):
        step = pl.program_id(0)

        @pl.when(step == 0)
        def _():
            gna_ref[...] = jnp.zeros_like(gna_ref)
            gs_ref[...] = jnp.zeros_like(gs_ref)
            carry[...] = jnp.zeros_like(carry)

        ya = ya_ref[...]
        r = _rstd(ya)
        xhat = ya * r
        silu, dsilu = _silu_and_grad(ga_ref[...])
        do = doa_ref[...]
        dn = do * silu
        dqg_ref[:, D_ATTN:2 * D_ATTN] = (do * (xhat * gn_ref[...]) * dsilu).astype(BF16)
        gna_ref[...] += jnp.sum(dn * xhat, axis=0, keepdims=True)
        dya_buf[...] = _rms_bwd(dn * gn_ref[...], xhat, r).astype(BF16)

        geometry = _band_geometry(n_blocks - 1 - step)
        use_cur = geometry[0]
        lane = lax.broadcasted_iota(jnp.int32, (BLK, 128), 1)
        sink_lane = lax.broadcasted_iota(jnp.int32, (1, 128), 1)
        gsink = jnp.zeros((1, 128), F32)

        def fold(bd):
            return (jnp.where(lane < 64, bd[0:BLK], 0.0) + jnp.where(lane >= 64, bd[BLK:2 * BLK], 0.0),
                    jnp.where(lane < 64, bd[2 * BLK:3 * BLK], 0.0) + jnp.where(lane >= 64, bd[3 * BLK:4 * BLK], 0.0))

        pairs = range(N_HEADS // 2)
        qps, kbd, vbd, probs, sink_probs = _attn_probs(q_ref, kvc_ref, kvp_ref, sink_ref, geometry)
        dyps = [dya_buf[:, j * 128:(j + 1) * 128] for j in pairs]
        dps = []
        for j in pairs:
            dps += _merge(_nt(dyps[j], vbd[j // 4]), use_cur)
        deltas = [jnp.sum(p * dp, axis=-1, keepdims=True) for p, dp in zip(probs, dps)]
        dss = [p * (dp - delta) for p, dp, delta in zip(probs, dps, deltas)]
        for h in range(N_HEADS):
            dsink = -jnp.sum(sink_probs[h] * deltas[h], axis=0, keepdims=True)
            gsink = gsink + jnp.where(sink_lane == h, dsink, 0.0)
        gs_ref[...] += gsink
        ds4s = [_split(dss[2 * j], dss[2 * j + 1], use_cur).astype(BF16) for j in pairs]
        p4s = [_split(probs[2 * j], probs[2 * j + 1], use_cur).astype(BF16) for j in pairs]
        dqg_ref[:, 0:D_ATTN] = jnp.concatenate([_nn(ds4s[j], kbd[j // 4]) * SCALE for j in pairs], axis=1).astype(BF16)
        sums = []
        for group in range(N_HEADS // HEADS_PER_KV):
            acc = [jnp.zeros((BLK, 128), F32) for _ in range(4)]
            for j in range(group * 4, group * 4 + 4):
                for slot, part in enumerate(fold(_tn(ds4s[j], qps[j])) + fold(_tn(p4s[j], dyps[j]))):
                    acc[slot] = acc[slot] + part
            sums.append([a + pltpu.roll(a, 64, 1) for a in acc])
        dk_cur, dk_prev, dv_cur, dv_prev = (jnp.where(lane < 64, a, b) for a, b in zip(sums[0], sums[1]))
        dqg_ref[:, 2 * D_ATTN:2 * D_ATTN + 2 * D_KV] = (jnp.concatenate([dk_cur, dv_cur], axis=1) + carry[...]).astype(BF16)
        carry[...] = jnp.concatenate([dk_prev, dv_prev], axis=1)

        @pl.when(step == n_blocks - 1)
        def _():
            small_out[...] = small_ref[...]
            small_out[SMALL_NORM_ATTN:SMALL_NORM_ATTN + 1, :] = gna_ref[...]
            small_out[SMALL_MISC:SMALL_MISC + 1, 0:128] = small_ref[SMALL_MISC:SMALL_MISC + 1, 0:128] + gs_ref[...]

        for at, stage in zip(stage_steps, _rs_wout_stages(gwo_ref, gwo_sh, *rs_scratch)):
            pl.when(step == at)(stage)

    row = pl.BlockSpec((BLK, D_ATTN), lambda i: (n_blocks - 1 - i, 0))
    kv_cur, kv_prev = _kv_specs(n_blocks, reverse=True)
    any_spec = pl.BlockSpec(memory_space=pl.ANY)
    return pl.pallas_call(
        body, name="attn_bwd", grid=(n_blocks,),
        out_shape=(jax.ShapeDtypeStruct((seq, D_QG), BF16), jax.ShapeDtypeStruct(small.shape, F32),
                   jax.ShapeDtypeStruct((gwo.shape[0] // N_CHIPS, gwo.shape[1]), F32)),
        in_specs=[row, kv_cur, kv_prev, row, row, row, pl.BlockSpec(memory_space=pltpu.SMEM), _resident((1, D_ATTN)),
                  any_spec, _resident(small.shape)],
        out_specs=(pl.BlockSpec((BLK, D_QG), lambda i: (n_blocks - 1 - i, 0)),
                   pl.BlockSpec(small.shape, lambda i: (0, 0)), any_spec),
        scratch_shapes=[pltpu.VMEM((1, D_ATTN), F32), pltpu.VMEM((1, 128), F32),
                        pltpu.VMEM((BLK, 2 * D_KV), F32), pltpu.VMEM((BLK, D_ATTN), BF16)] + _rs_wout_scratch(gwo.shape),
        compiler_params=_params(44, ("arbitrary",)),
    )(q, kv, kv, ga, ya, doa, sinks, norm_attn_out, gwo, small)


GBLK = 256
GSUB = 64
PAIR_RING = 4
LAG_PAIR, LAG_HOP1, LAG_HOP2 = 1, 7, 14


def _bwd_in(dpc, dqg, h, wt, x, norm_in, dx2, small):
    seq = x.shape[0]
    n_blk = D_IN_PROJ // GBLK
    per_chip = n_blk // N_CHIPS
    n_slots = (n_blk + 1) // 2
    n_sub = GBLK // GSUB
    chip_rows = D_IN_PROJ // N_CHIPS
    tile = TOK_TILE
    n_tiles = seq // tile
    n_steps = n_blk + max(n_tiles, LAG_HOP2)
    chunk = min(seq, 512)
    blk_q, blk_kv, blk_ga = ROW_Q // GBLK, ROW_KV // GBLK, ROW_GA // GBLK

    def block_of(i):
        k = i % N_CHIPS
        robin = per_chip * ((k % 2) * 2 + k // 2) + i // N_CHIPS
        if isinstance(i, int):
            return robin if i < per_chip * N_CHIPS else i
        return jnp.where(i < per_chip * N_CHIPS, robin, i)

    def owner_of(i):
        return (i // N_CHIPS) % 2

    def slot_of(i):
        return (i // (2 * N_CHIPS)) * N_CHIPS + i % N_CHIPS

    def body(dpc_ref, dqg_ref, wt_ref, h_ref, x_ref, g_ref, dx2_ref, small_ref, gx_ref, small_sum, gwt_sh,
             dh_acc, gni, keep, pbuf, xbuf, land, land2, small_land,
             pair_send, pair_recv, h1_send, h1_recv, h2_send, h2_recv, sw_send, sw_recv, sm_send, sm_recv, out_sem):
        step = pl.program_id(0)
        x_i, y_i, c = lax.axis_index("x"), lax.axis_index("y"), lax.axis_index("c")
        me = 4 * x_i + 2 * y_i + c
        j = 2 * x_i + y_i
        pa = (_xor(x_i, 1 - c), _xor(y_i, c), c)
        pb = (_xor(x_i, c), _xor(y_i, 1 - c), c)
        sib = (x_i, y_i, 1 - c)
        ja = 2 * pa[0] + pa[1]
        jb = 2 * pb[0] + pb[1]
        jd = 3 - j

        def remote(src, dst, send, recv, to):
            return pltpu.make_async_remote_copy(src_ref=src, dst_ref=dst, send_sem=send, recv_sem=recv,
                                                device_id=to, device_id_type=MESH)

        def piece(ref, slot, u, n):
            return ref.at[slot, pl.ds(u * GSUB, n * GSUB), :]

        def chip_rows_at(ref, local, n):
            return ref.at[pl.ds(pl.multiple_of(local, GSUB), n * GSUB), :]

        def pair_copy(i):
            slot = slot_of(i)
            return remote(pbuf.at[i % PAIR_RING], land.at[slot], pair_send.at[slot], pair_recv.at[slot], sib)

        def h1_copy(slot, u, n):
            k = slot * n_sub + u
            return remote(piece(xbuf, slot, u, n), piece(xbuf, slot, u, n), h1_send.at[k], h1_recv.at[k], pa)

        def h2_copy(slot, u, n, local):
            k = slot * n_sub + u
            return remote(piece(xbuf, slot, u, n), chip_rows_at(land2, local, n), h2_send.at[k], h2_recv.at[k], pb)

        def sw_copy(slot, u, n, local):
            k = slot * n_sub + u
            return remote(piece(keep, slot, u, n), chip_rows_at(gwt_sh, local, n), sw_send.at[k], sw_recv.at[k], sib)

        def owned(i):
            return (i >= 0) & (i < n_blk) & (owner_of(i) == c)

        def chip_of(blk, u):
            row = blk * GBLK + u * GSUB
            chip = row // chip_rows
            return chip, row - chip * chip_rows

        def pieces(blk):
            first, local = chip_of(blk, 0)
            whole = first == chip_of(blk, n_sub - 1)[0]
            if isinstance(blk, int):
                return [(True, 0, n_sub, first, local)] if whole else [(True, u, 1) + chip_of(blk, u) for u in range(n_sub)]
            return [(whole, 0, n_sub, first, local)] + [(jnp.logical_not(whole), u, 1) + chip_of(blk, u) for u in range(n_sub)]

        @pl.when(step == 0)
        def _():
            dh_acc[...] = jnp.zeros_like(dh_acc)
            gni[...] = jnp.zeros_like(gni)

        @pl.when(step < n_blk)
        def _():
            from_pc = block_of(step) < blk_q
            block = _tn(jnp.where(from_pc, dpc_ref[...], dqg_ref[...]), h_ref[...])
            for t in range(0, seq, chunk):
                d = jnp.where(from_pc, dpc_ref[t:t + chunk, :], dqg_ref[t:t + chunk, :])
                dh_acc[t:t + chunk, :] += _nn(d, wt_ref[...])

            @pl.when(owner_of(step) == c)
            def _():
                keep[slot_of(step)] = block

            @pl.when(owner_of(step) != c)
            def _():
                @pl.when(step >= 2 * PAIR_RING)
                def _():
                    pair_copy(step - 2 * PAIR_RING).wait_send()
                pbuf[step % PAIR_RING] = block.astype(BF16)
                pair_copy(step).start()

        i1 = step - LAG_PAIR

        @pl.when(owned(i1))
        def _():
            slot = slot_of(i1)
            pair_copy(i1).wait_recv()
            _accumulate(keep.at[slot], land.at[slot])
            for cond, u, n, chip, _ in pieces(block_of(i1)):
                @pl.when(cond & ((chip == ja) | (chip == jd)))
                def _(u=u, n=n):
                    _cast_rows(piece(keep, slot, u, n), piece(xbuf, slot, u, n))
                    h1_copy(slot, u, n).start()

        i2 = step - LAG_HOP1

        @pl.when(owned(i2))
        def _():
            slot = slot_of(i2)
            for cond, u, n, chip, local in pieces(block_of(i2)):
                @pl.when(cond & ((chip == j) | (chip == jb)))
                def _(u=u, n=n, chip=chip, local=local):
                    h1_copy(slot, u, n).wait_recv()
                    _accumulate(piece(keep, slot, u, n), piece(xbuf, slot, u, n))

                    @pl.when(chip == jb)
                    def _():
                        _cast_rows(piece(keep, slot, u, n), piece(xbuf, slot, u, n))
                        h2_copy(slot, u, n, local).start()

                @pl.when(cond & ((chip == ja) | (chip == jd)))
                def _(u=u, n=n):
                    h1_copy(slot, u, n).wait_send()

        i3 = step - LAG_HOP2

        @pl.when(owned(i3))
        def _():
            slot = slot_of(i3)
            for cond, u, n, chip, local in pieces(block_of(i3)):
                @pl.when(cond & (chip == j))
                def _(u=u, n=n, local=local):
                    h2_copy(slot, u, n, local).wait_recv()
                    _accumulate(piece(keep, slot, u, n), chip_rows_at(land2, local, n))
                    mine = pltpu.make_async_copy(piece(keep, slot, u, n), chip_rows_at(gwt_sh, local, n), out_sem.at[0])
                    mine.start()
                    sw_copy(slot, u, n, local).start()
                    mine.wait()

                @pl.when(cond & (chip == jb))
                def _(u=u, n=n, local=local):
                    h2_copy(slot, u, n, local).wait_send()

        e = step - n_blk

        @pl.when((e >= 0) & (e < n_tiles))
        def _():
            dh = dh_acc[pl.ds(pl.multiple_of(e * tile, tile), tile), :]
            xv = x_ref[...]
            r = _rstd(xv)
            xhat = xv * r
            gni[...] += jnp.sum(dh * xhat, axis=0, keepdims=True)
            gx_ref[...] = _rms_bwd(dh * g_ref[...], xhat, r) + dx2_ref[...]

        @pl.when(step == n_steps - 1)
        def _():
            small_land[me] = small_ref[...]
            small_land[me, SMALL_NORM_IN:SMALL_NORM_IN + 1, :] = gni[...]
            others = [(dx, dy, dc) for dx in (0, 1) for dy in (0, 1) for dc in (0, 1)][1:]
            sends = [remote(small_land.at[me], small_land.at[me], sm_send.at[k], sm_recv.at[k],
                            (_xor(x_i, dx), _xor(y_i, dy), _xor(c, dc))) for k, (dx, dy, dc) in enumerate(others)]
            for cp in sends:
                cp.start()
            for i in range(n_blk):
                if i + 2 * PAIR_RING >= n_blk:
                    @pl.when(owner_of(i) != c)
                    def _(i=i):
                        pair_copy(i).wait_send()
                for _, u, n, chip, local in pieces(block_of(i)):
                    @pl.when((j == chip) & (c == owner_of(i)))
                    def _(i=i, u=u, n=n, local=local):
                        sw_copy(slot_of(i), u, n, local).wait_send()

                    @pl.when((j == chip) & (c != owner_of(i)))
                    def _(i=i, u=u, n=n, local=local):
                        sw_copy(slot_of(i), u, n, local).wait_recv()
            for cp in sends:
                cp.wait_recv()
            total = small_land[0]
            for dev in range(1, 8):
                total = total + small_land[dev]
            small_sum[...] = total
            for cp in sends:
                cp.wait_send()

    def blk_at(i):
        return block_of(jnp.clip(i, 0, n_blk - 1))

    last_pc_step = max(i for i in range(n_blk) if block_of(i) < blk_q)

    def next_block(i, in_pc):
        i = jnp.clip(i, 0, n_blk - 1)
        step = jnp.full_like(i, last_pc_step if in_pc else n_blk - 1)
        for ahead in reversed(range(N_CHIPS)):
            cand = jnp.minimum(i + ahead, n_blk - 1)
            step = jnp.where((block_of(cand) < blk_q) == in_pc, cand, step)
        return block_of(step)

    def dqg_block(i):
        b = next_block(i, False)
        q_blk = jnp.clip(b - blk_q, 0, blk_kv - blk_q - 1)
        ga_blk = (D_ATTN // GBLK) + jnp.clip(b - blk_ga, 0, n_blk - blk_ga - 1)
        return jnp.where(b < blk_kv, q_blk, jnp.where(b == blk_kv, 2 * D_ATTN // GBLK, ga_blk))

    def tok(i):
        return (jnp.clip(i - n_blk, 0, n_tiles - 1), 0)

    n_piece = n_slots * n_sub
    dma = pltpu.SemaphoreType.DMA
    return pl.pallas_call(
        body, name="bwd_in", grid=(n_steps,),
        out_shape=(jax.ShapeDtypeStruct((seq, D_MODEL), F32), jax.ShapeDtypeStruct(small.shape, F32),
                   jax.ShapeDtypeStruct((chip_rows, D_MODEL), F32)),
        in_specs=[pl.BlockSpec((seq, GBLK), lambda i: (0, next_block(i, True))),
                  pl.BlockSpec((seq, GBLK), lambda i: (0, dqg_block(i))),
                  pl.BlockSpec((GBLK, D_MODEL), lambda i: (blk_at(i), 0)),
                  _resident(h.shape),
                  pl.BlockSpec((tile, D_MODEL), tok), _resident((1, D_MODEL)), pl.BlockSpec((tile, D_MODEL), tok),
                  _resident(small.shape)],
        out_specs=(pl.BlockSpec((tile, D_MODEL), tok), pl.BlockSpec(small.shape, lambda i: (0, 0)),
                   pl.BlockSpec(memory_space=pl.ANY)),
        scratch_shapes=[pltpu.VMEM((seq, D_MODEL), F32), pltpu.VMEM((1, D_MODEL), F32),
                        pltpu.VMEM((n_slots, GBLK, D_MODEL), F32), pltpu.VMEM((PAIR_RING, GBLK, D_MODEL), BF16),
                        pltpu.VMEM((n_slots, GBLK, D_MODEL), BF16), pltpu.VMEM((n_slots, GBLK, D_MODEL), BF16),
                        pltpu.VMEM((chip_rows, D_MODEL), BF16), pltpu.VMEM((8,) + small.shape, F32),
                        dma((n_slots,)), dma((n_slots,)), dma((n_piece,)), dma((n_piece,)), dma((n_piece,)),
                        dma((n_piece,)), dma((n_piece,)), dma((n_piece,)), dma((7,)), dma((7,)), dma((1,))],
        compiler_params=_params(62, ("arbitrary",)),
    )(dpc, dqg, wt, h, x, norm_in, dx2, small)


def _accumulate(dst_ref, src_ref, rows=16):
    def step(i, carry):
        sl = pl.ds(pl.multiple_of(i * rows, rows), rows)
        dst_ref[sl, :] = dst_ref[sl, :] + src_ref[sl, :].astype(F32)
        return carry
    lax.fori_loop(0, dst_ref.shape[0] // rows, step, 0)


def _rs_wout_scratch(gwo_shape):
    o_half, width = gwo_shape[0] // N_CHIPS // 2, gwo_shape[1]
    return [pltpu.VMEM((4, o_half, width), F32), pltpu.VMEM((4, o_half, width), BF16),
            pltpu.VMEM((4, o_half, width), BF16), pltpu.VMEM((2, o_half, width), BF16),
            pltpu.VMEM((o_half, width), BF16),
            pltpu.SemaphoreType.DMA((8,)), pltpu.SemaphoreType.DMA((8,)), pltpu.SemaphoreType.DMA((4,))]


def _rs_wout_stages(gwo_ref, gwo_sh, acc_o, sb_o, r1o, r2o, r3o, send_sems, recv_sems, local_sems):
    o_rows = gwo_ref.shape[0] // N_CHIPS
    o_half = o_rows // 2
    x, y, c = lax.axis_index("x"), lax.axis_index("y"), lax.axis_index("c")
    j = 2 * x + y
    pa = (_xor(x, 1 - c), _xor(y, c), c)
    pb = (_xor(x, c), _xor(y, 1 - c), c)
    sib = (x, y, 1 - c)
    ja = 2 * pa[0] + pa[1]
    jb = 2 * pb[0] + pb[1]
    jd = 3 - j
    order = (ja, jd, jb, j)
    sib_order = (jb, jd, ja, j)

    def rcopy(k, src, dst, to):
        return pltpu.make_async_remote_copy(src_ref=src, dst_ref=dst, send_sem=send_sems.at[k],
                                            recv_sem=recv_sems.at[k], device_id=to, device_id_type=MESH)

    def o_rows_of(chip, half):
        return gwo_ref.at[pl.ds(pl.multiple_of(chip * o_rows + half * o_half, 8), o_half), :]

    def load_all(chips, half):
        cps = [pltpu.make_async_copy(o_rows_of(chip, half), acc_o.at[s], local_sems.at[s]) for s, chip in enumerate(chips)]
        for cp in cps:
            cp.start()
        return cps

    def pair_send(s):
        return rcopy(s, sb_o.at[s], r1o.at[s], sib)

    def out_half(half):
        return gwo_sh.at[pl.ds(pl.multiple_of(half * o_half, 8), o_half), :]

    hop1 = [rcopy(4 + s, sb_o.at[s], r2o.at[s], pa) for s in range(2)]
    hop2 = rcopy(6, sb_o.at[2], r3o, pb)
    swap = rcopy(7, acc_o.at[3], out_half(c), sib)
    mine = pltpu.make_async_copy(acc_o.at[3], out_half(c), local_sems.at[0])

    def resend(s, copy):
        pair_send(s).wait_send()
        _cast_rows(acc_o.at[s], sb_o.at[s])
        copy.start()

    def stage_pair():
        for s, cp in enumerate(load_all(sib_order, 1 - c)):
            cp.wait()
            _cast_rows(acc_o.at[s], sb_o.at[s])
            pair_send(s).start()

    def stage_hop1():
        for s, cp in enumerate(load_all(order, c)):
            cp.wait()
            pair_send(s).wait_recv()
            _accumulate(acc_o.at[s], r1o.at[s])
            if s < 2:
                resend(s, hop1[s])

    def stage_hop2():
        hop1[1].wait_recv()
        _accumulate(acc_o.at[2], r2o.at[1])
        resend(2, hop2)
        hop1[0].wait_recv()
        _accumulate(acc_o.at[3], r2o.at[0])

    def stage_final():
        hop2.wait_recv()
        _accumulate(acc_o.at[3], r3o)
        mine.start()
        swap.start()

    def stage_drain():
        rcopy(7, acc_o.at[3], out_half(1 - c), sib).wait_recv()
        for cp in [pair_send(3)] + hop1 + [hop2, swap]:
            cp.wait_send()
        mine.wait()

    return stage_pair, stage_hop1, stage_hop2, stage_final, stage_drain


def _adamw(name, w, g, m, v, rows):
    def body(w_ref, g_ref, m_ref, v_ref, go_ref, d_ref, nm_ref, nv_ref):
        _adamw_update(w_ref, g_ref, m_ref, v_ref, go_ref, d_ref, nm_ref, nv_ref)

    spec = pl.BlockSpec((rows, w.shape[1]), lambda i: (i, 0))
    shape = jax.ShapeDtypeStruct(w.shape, F32)
    return pl.pallas_call(
        body, name="adamw_" + name, grid=(w.shape[0] // rows,),
        out_shape=(shape,) * 4, in_specs=[spec] * 4, out_specs=(spec,) * 4,
        compiler_params=_params(32, ("arbitrary",)),
    )(w, g, m, v)


def _adamw_update(w_ref, g_ref, m_ref, v_ref, go_ref, d_ref, nm_ref, nv_ref):
    gv = g_ref[...]
    go_ref[...] = gv
    nm = ADAM_B1 * m_ref[...] + (1.0 - ADAM_B1) * gv
    nv = ADAM_B2 * v_ref[...] + (1.0 - ADAM_B2) * (gv * gv)
    m_hat = nm / (1.0 - ADAM_B1 ** ADAM_STEP)
    v_hat = nv / (1.0 - ADAM_B2 ** ADAM_STEP)
    d_ref[...] = -ADAM_LR * (m_hat / (jnp.sqrt(v_hat) + ADAM_EPS) + ADAM_WD * w_ref[...])
    nm_ref[...] = nm
    nv_ref[...] = nv


def _adamw_small(weights, grads, ms, vs):
    n = len(weights)

    def body(*refs):
        ins, outs = refs[:4 * n], refs[4 * n:]
        for k in range(n):
            _adamw_update(*ins[4 * k:4 * k + 4], *outs[4 * k:4 * k + 4])

    flat = [a for group in zip(weights, grads, ms, vs) for a in group]
    vmem = pl.BlockSpec(memory_space=pltpu.VMEM)
    out = pl.pallas_call(
        body, name="adamw_small",
        out_shape=tuple(jax.ShapeDtypeStruct(w.shape, F32) for w in weights for _ in range(4)),
        in_specs=[vmem] * (4 * n), out_specs=(vmem,) * (4 * n),
    )(*flat)
    return [tuple(out[4 * k:4 * k + 4]) for k in range(n)]


def kernel(x, norm_in, w_in, conv_w, attn_sinks, norm_conv_out, norm_attn_out, w_out, norm_final, loss_target, m_norm_in, m_w_in, m_conv_w, m_attn_sinks, m_norm_conv_out, m_norm_attn_out, m_w_out, m_norm_final, v_norm_in, v_w_in, v_conv_w, v_attn_sinks, v_norm_conv_out, v_norm_attn_out, v_w_out, v_norm_final):
    chip = 2 * lax.axis_index("x") + lax.axis_index("y")
    xs, target = x[0], loss_target[0]
    norm_final2 = norm_final.reshape(1, D_MODEL)
    w_in_t, m_w_in_t, v_w_in_t = w_in[0].T, m_w_in[0].T, v_w_in[0].T

    wt, cw4 = _ag_weights(w_in_t, conv_w[0])
    cw = jnp.transpose(cw4, (1, 0, 2)).reshape(3, D_CONV)

    h, pc, q, kv, ga, oc, wo = _fwd_in(xs, norm_in, wt, w_out[0], cw, norm_conv_out)
    ya, oa = _attn_fwd(q, kv, ga, attn_sinks, norm_attn_out)
    dx2, doa, gwo, dpc, small = _out_proj_loss(xs, oc, oa, wo, norm_final2, target, pc, cw, norm_conv_out)
    dqg, small, gwo_sh = _attn_bwd(q, kv, ga, ya, doa, attn_sinks, norm_attn_out, gwo, small)
    grad_x, small_sum, gwt_sh = _bwd_in(dpc, dqg, h, wt, xs, norm_in, dx2, small)

    loss = small_sum[SMALL_MISC, SMALL_LOSS_LANE]
    g_norm_in, g_norm_conv, g_norm_attn = (small_sum[r:r + 1] for r in (SMALL_NORM_IN, SMALL_NORM_CONV, SMALL_NORM_ATTN))
    g_norm_final = small_sum[SMALL_NORM_FINAL]
    g_conv_w = lax.dynamic_slice(small_sum[SMALL_CONV_W:SMALL_CONV_W + 3], (0, chip * (D_CONV // N_CHIPS)),
                                 (3, D_CONV // N_CHIPS))[None]
    g_sinks = small_sum[SMALL_MISC:SMALL_MISC + 1, 0:N_HEADS]

    def two_d(a):
        return a.reshape(-1, a.shape[-1])

    up_w_in = tuple(o.T[None] for o in _adamw("w_in", w_in_t, gwt_sh, m_w_in_t, v_w_in_t, 200))
    up_w_out = tuple(o[None] for o in _adamw("w_out", w_out[0], gwo_sh, m_w_out[0], v_w_out[0], 128))
    small_w = (norm_in, conv_w, attn_sinks, norm_conv_out, norm_attn_out, norm_final)
    small_g = (g_norm_in, g_conv_w, g_sinks, g_norm_conv, g_norm_attn, g_norm_final)
    small_m = (m_norm_in, m_conv_w, m_attn_sinks, m_norm_conv_out, m_norm_attn_out, m_norm_final)
    small_v = (v_norm_in, v_conv_w, v_attn_sinks, v_norm_conv_out, v_norm_attn_out, v_norm_final)
    up_small = _adamw_small(*(tuple(two_d(a) for a in group) for group in (small_w, small_g, small_m, small_v)))
    up_small = [tuple(o.reshape(w.shape) for o in up) for up, w in zip(up_small, small_w)]
    up_norm_in, up_conv_w, up_sinks, up_norm_conv, up_norm_attn, up_norm_final = up_small
    updates = (up_norm_in, up_w_in, up_conv_w, up_sinks, up_norm_conv, up_norm_attn, up_w_out, up_norm_final)
    grads_out, deltas, new_m, new_v = zip(*updates)
    return (loss, grad_x[None], *grads_out, *deltas, *new_m, *new_v)
```

```python
import jax
import jax.numpy as jnp
from jax import lax
from jax.experimental import pallas as pl
from jax.experimental.pallas import tpu as pltpu

F32 = jnp.float32
BF16 = jnp.bfloat16
MESH = pl.DeviceIdType.MESH

D_MODEL = 1024
D_CONV = 1024
D_ATTN = 1024
D_KV = 128
D_QG = 2 * D_ATTN + 2 * D_KV
D_MIX = D_CONV + D_ATTN
D_PC = 4 * D_CONV
D_IN_PROJ = D_PC + 2 * D_ATTN + 2 * D_KV
ROW_Q = D_PC
ROW_KV = ROW_Q + D_ATTN
ROW_GA = ROW_KV + 2 * D_KV
N_HEADS = 16
HEAD_DIM = 64
HEADS_PER_KV = 8
BLK = 128
N_CHIPS = 4
RMS_EPS = 1e-5
SCALE = HEAD_DIM ** -0.5
SLOPES = tuple(2.0 ** (-8.0 * (h + 1) / N_HEADS) for h in range(N_HEADS))

ADAM_LR, ADAM_B1, ADAM_B2, ADAM_EPS, ADAM_WD, ADAM_STEP = 0.001, 0.9, 0.999, 1e-08, 0.01, 10

SMALL_ROWS = 8
SMALL_NORM_IN, SMALL_NORM_CONV, SMALL_NORM_ATTN, SMALL_NORM_FINAL, SMALL_CONV_W, SMALL_MISC = 0, 1, 2, 3, 4, 7
SMALL_LOSS_LANE = N_HEADS

TOK_TILE = 256
MIB = 1 << 20


def _params(vmem_mib, semantics=None):
    return pltpu.CompilerParams(dimension_semantics=semantics, vmem_limit_bytes=vmem_mib * MIB)


def _nn(a, b):
    return jnp.dot(a, b, preferred_element_type=F32)


def _nt(a, b):
    return lax.dot_general(a, b, (((1,), (1,)), ((), ())), preferred_element_type=F32)


def _tn(a, b):
    return lax.dot_general(a, b, (((0,), (0,)), ((), ())), preferred_element_type=F32)


def _rstd(v):
    return lax.rsqrt(jnp.mean(v * v, axis=-1, keepdims=True) + RMS_EPS)


def _rms_bwd(g, xhat, rstd):
    return rstd * (g - xhat * jnp.mean(g * xhat, axis=-1, keepdims=True))


def _silu_and_grad(g):
    s = jax.nn.sigmoid(g)
    return g * s, s * (1.0 + g * (1.0 - s))


def _resident(shape):
    return pl.BlockSpec(shape, lambda *_: (0,) * len(shape), pipeline_mode=pl.Buffered(1))


def _xor(a, b):
    return a + b - 2 * a * b


def _cast_rows(src_ref, dst_ref, rows=32):
    def step(i, carry):
        sl = pl.ds(pl.multiple_of(i * rows, rows), rows)
        dst_ref[sl, :] = src_ref[sl, :].astype(dst_ref.dtype)
        return carry
    lax.fori_loop(0, src_ref.shape[0] // rows, step, 0)


def _ag_scratch(shard_shape):
    rows, width = shard_shape
    return [pltpu.VMEM((rows, width), F32), pltpu.VMEM((rows, width), BF16), pltpu.VMEM((3, rows // 2, width), BF16),
            pltpu.SemaphoreType.DMA((6,)), pltpu.SemaphoreType.DMA((6,)), pltpu.SemaphoreType.DMA((4,))]


def _ag_stages(sh_ref, out, f32_buf, own, land, send_sems, recv_sems, local_sems):
    rows = sh_ref.shape[0]
    half = rows // 2
    x, y, c = lax.axis_index("x"), lax.axis_index("y"), lax.axis_index("c")
    j = 2 * x + y
    p1 = (_xor(x, c), _xor(y, 1 - c), c)
    p2 = (_xor(x, 1 - c), _xor(y, c), c)
    sib = (x, y, 1 - c)
    j1 = 2 * p1[0] + p1[1]
    j2 = 2 * p2[0] + p2[1]
    j3 = 3 - j

    def rows_of(chip, hf):
        return out.at[pl.ds(pl.multiple_of(chip * rows + hf * half, 16), half), :]

    def rcopy(k, src, dst, to):
        return pltpu.make_async_remote_copy(src_ref=src, dst_ref=dst, send_sem=send_sems.at[k],
                                            recv_sem=recv_sems.at[k], device_id=to, device_id_type=MESH)

    my_half = own.at[pl.ds(pl.multiple_of(c * half, 16), half), :]
    hop1 = rcopy(0, my_half, land.at[0], p1)
    hop2_own = rcopy(1, my_half, land.at[1], p2)
    hop2_fwd = rcopy(2, land.at[0], land.at[2], p2)
    swaps = [rcopy(3 + s, land.at[s], rows_of(chip, c), sib) for s, chip in enumerate((j1, j2, j3))]
    keeps = [pltpu.make_async_copy(land.at[s], rows_of(chip, c), local_sems.at[1 + s]) for s, chip in enumerate((j1, j2, j3))]
    load = pltpu.make_async_copy(sh_ref, f32_buf, local_sems.at[0])
    own_out = pltpu.make_async_copy(own, out.at[pl.ds(pl.multiple_of(j * rows, 16), rows), :], local_sems.at[0])

    def stage_send():
        load.start()
        load.wait()
        _cast_rows(f32_buf, own)
        own_out.start()
        hop1.start()

    def stage_forward():
        hop1.wait_recv()
        hop2_own.start()
        hop2_fwd.start()
        swaps[0].start()
        keeps[0].start()

    def stage_publish():
        hop2_own.wait_recv()
        swaps[1].start()
        keeps[1].start()
        hop2_fwd.wait_recv()
        swaps[2].start()
        keeps[2].start()

    def stage_drain():
        for s, chip in enumerate((j2, j1, j3)):
            rcopy(3 + s, my_half, rows_of(chip, 1 - c), sib).wait_recv()
        for cp in [hop1, hop2_own, hop2_fwd] + swaps:
            cp.wait_send()
        for cp in [own_out] + keeps:
            cp.wait()

    return stage_send, stage_forward, stage_publish, stage_drain


def _ag_weights(wt_sh, cw_sh):
    def body(wt_ref, cw_ref, wt_out, cw_out, *scratch):
        cw_send, cw_recv, cw_local = scratch[-3:]
        x, y, c = lax.axis_index("x"), lax.axis_index("y"), lax.axis_index("c")
        j = 2 * x + y
        p1 = (_xor(x, c), _xor(y, 1 - c), c)
        p2 = (_xor(x, 1 - c), _xor(y, c), c)
        j1 = 2 * p1[0] + p1[1]

        def cw_copy(k, chip, to):
            src = cw_ref if k != 2 else cw_out.at[chip]
            return pltpu.make_async_remote_copy(src_ref=src, dst_ref=cw_out.at[chip], send_sem=cw_send.at[k],
                                                recv_sem=cw_recv.at[k], device_id=to, device_id_type=MESH)

        mine = pltpu.make_async_copy(cw_ref, cw_out.at[j], cw_local.at[0])
        mine.start()
        first = cw_copy(0, j, p1)
        first.start()
        stages = _ag_stages(wt_ref, wt_out, *scratch[:-3])
        stages[0]()
        first.wait_recv()
        second = [cw_copy(1, j, p2), cw_copy(2, j1, p2)]
        for cp in second:
            cp.start()
        for stage in stages[1:]:
            stage()
        for cp in second:
            cp.wait_recv()
        for cp in [first] + second:
            cp.wait_send()
        mine.wait()

    any_spec = pl.BlockSpec(memory_space=pl.ANY)
    dma = pltpu.SemaphoreType.DMA
    return pl.pallas_call(
        body, name="ag_weights",
        out_shape=(pltpu.HBM((N_CHIPS * wt_sh.shape[0], wt_sh.shape[1]), BF16),
                   jax.ShapeDtypeStruct((N_CHIPS,) + cw_sh.shape, cw_sh.dtype)),
        in_specs=[any_spec, any_spec], out_specs=(any_spec, any_spec),
        scratch_shapes=_ag_scratch(wt_sh.shape) + [dma((3,)), dma((3,)), dma((1,))],
        compiler_params=_params(32),
    )(wt_sh, cw_sh)


def _fwd_in(x, norm_in, wt, wo_sh, conv_w, norm_conv_out):
    seq = x.shape[0]
    tile = TOK_TILE
    n_tiles = seq // tile
    stage_steps = (0, n_tiles // 4, (5 * n_tiles) // 8, n_tiles - 1)

    def body(x_ref, g_ref, wt_ref, wo_ref, cw_ref, gn_ref, h_ref, pc_ref, q_ref, kv_ref, ga_ref, oc_ref, wo_out,
             zbuf, *ag_scratch):
        stages = _ag_stages(wo_ref, wo_out, *ag_scratch)
        for at, stage in zip(stage_steps[:-1], stages[:-1]):
            pl.when(pl.program_id(0) == at)(stage)

        @pl.when(pl.program_id(0) == 0)
        def _():
            zbuf[0:8, :] = jnp.zeros((8, D_CONV), F32)

        xv = x_ref[...]
        h = (xv * _rstd(xv) * g_ref[...]).astype(BF16)
        h_ref[...] = h
        for blk in range(D_PC // D_CONV):
            pc_ref[:, blk * D_CONV:(blk + 1) * D_CONV] = _nt(h, wt_ref[blk * D_CONV:(blk + 1) * D_CONV, :])
        q_ref[...] = _nt(h, wt_ref[ROW_Q:ROW_KV, :])
        kv_ref[...] = _nt(h, wt_ref[ROW_KV:ROW_GA, :])
        ga_ref[...] = _nt(h, wt_ref[ROW_GA:D_IN_PROJ, :])

        cb, _, _, _, _, _, conv = _conv_core(pc_ref, zbuf, cw_ref)
        yc = cb * conv
        silu, _ = _silu_and_grad(pc_ref[:, 3 * D_CONV:4 * D_CONV])
        oc_ref[...] = (yc * _rstd(yc) * gn_ref[...] * silu).astype(BF16)
        zbuf[0:8, :] = zbuf[tile:tile + 8, :]
        pl.when(pl.program_id(0) == stage_steps[-1])(stages[-1])

    def row(width):
        return pl.BlockSpec((tile, width), lambda i: (i, 0))

    any_spec = pl.BlockSpec(memory_space=pl.ANY)
    return pl.pallas_call(
        body, name="fwd_in", grid=(n_tiles,),
        out_shape=(jax.ShapeDtypeStruct((seq, D_MODEL), BF16), jax.ShapeDtypeStruct((seq, D_PC), F32),
                   jax.ShapeDtypeStruct((seq, D_ATTN), F32), jax.ShapeDtypeStruct((seq, 2 * D_KV), F32),
                   jax.ShapeDtypeStruct((seq, D_ATTN), F32), jax.ShapeDtypeStruct((seq, D_CONV), BF16),
                   jax.ShapeDtypeStruct((N_CHIPS * wo_sh.shape[0], wo_sh.shape[1]), BF16)),
        in_specs=[row(D_MODEL), _resident((1, D_MODEL)), _resident(wt.shape), any_spec,
                  _resident(conv_w.shape), _resident((1, D_CONV))],
        out_specs=(row(D_MODEL), row(D_PC), row(D_ATTN), row(2 * D_KV), row(D_ATTN), row(D_CONV), any_spec),
        scratch_shapes=[pltpu.VMEM((tile + 8, D_CONV), F32)] + _ag_scratch(wo_sh.shape),
        compiler_params=_params(56, ("arbitrary",)),
    )(x, norm_in, wt, wo_sh, conv_w, norm_conv_out)


def _conv_core(pc_ref, zbuf, cw_ref):
    tile = pc_ref.shape[0]
    cb = pc_ref[:, 0:D_CONV]
    cc = pc_ref[:, D_CONV:2 * D_CONV]
    cu = pc_ref[:, 2 * D_CONV:3 * D_CONV]
    z = cc * cu
    zbuf[8:tile + 8, :] = z
    z1 = zbuf[7:tile + 7, :]
    z2 = zbuf[6:tile + 6, :]
    conv = cw_ref[0:1, :] * z2 + cw_ref[1:2, :] * z1 + cw_ref[2:3, :] * z
    return cb, cc, cu, z, z1, z2, conv


def _band_geometry(block_index):
    qi = lax.broadcasted_iota(jnp.int32, (BLK, BLK), 0)
    kp = lax.broadcasted_iota(jnp.int32, (BLK, BLK), 1)
    use_cur = kp <= qi
    dist = jnp.where(use_cur, qi - kp, qi - kp + BLK).astype(F32)
    valid = use_cur | (block_index > 0)
    return use_cur, dist, valid


def _block_diag(cur, prev, group):
    lane = lax.broadcasted_iota(jnp.int32, cur.shape, 1)

    def halves(t):
        other = pltpu.roll(t, 64, 1)
        lo, hi = (t, other) if group == 0 else (other, t)
        return jnp.where(lane < 64, lo, 0.0), jnp.where(lane >= 64, hi, 0.0)

    return jnp.concatenate(halves(cur) + halves(prev), axis=0).astype(BF16)


def _merge(s4, use_cur):
    return (jnp.where(use_cur, s4[:, 0:BLK], s4[:, 2 * BLK:3 * BLK]),
            jnp.where(use_cur, s4[:, BLK:2 * BLK], s4[:, 3 * BLK:4 * BLK]))


def _split(a, b, use_cur):
    return jnp.concatenate([jnp.where(use_cur, a, 0.0), jnp.where(use_cur, b, 0.0),
                            jnp.where(use_cur, 0.0, a), jnp.where(use_cur, 0.0, b)], axis=1)


def _softmax_head(s, head, sink, dist, valid):
    sc = jnp.where(valid, s - SLOPES[head] * dist, -jnp.inf)
    m = jnp.maximum(jnp.max(sc, axis=-1, keepdims=True), sink)
    p = jnp.exp(sc - m)
    es = jnp.exp(sink - m)
    inv = 1.0 / (jnp.sum(p, axis=-1, keepdims=True) + es)
    return p * inv, es * inv


def _attn_probs(q_ref, kvc_ref, kvp_ref, sink_ref, geometry):
    use_cur, dist, valid = geometry
    groups = range(N_HEADS // HEADS_PER_KV)
    kbd = [_block_diag(kvc_ref[:, 0:D_KV], kvp_ref[:, 0:D_KV], g) for g in groups]
    vbd = [_block_diag(kvc_ref[:, D_KV:2 * D_KV], kvp_ref[:, D_KV:2 * D_KV], g) for g in groups]
    qps = [(q_ref[:, j * 128:(j + 1) * 128] * SCALE).astype(BF16) for j in range(N_HEADS // 2)]
    scores = []
    for j, qp in enumerate(qps):
        scores += _merge(_nt(qp, kbd[j // 4]), use_cur)
    sinks = [sink_ref[0, h] for h in range(N_HEADS)]
    scores = [jnp.where(valid, s - SLOPES[h] * dist, -jnp.inf) for h, s in enumerate(scores)]
    maxes = [jnp.maximum(jnp.max(s, axis=-1, keepdims=True), sinks[h]) for h, s in enumerate(scores)]
    exps = [jnp.exp(s - m) for s, m in zip(scores, maxes)]
    sink_exps = [jnp.exp(sinks[h] - m) for h, m in enumerate(maxes)]
    invs = [1.0 / (jnp.sum(e, axis=-1, keepdims=True) + se) for e, se in zip(exps, sink_exps)]
    probs = [e * inv for e, inv in zip(exps, invs)]
    sink_probs = [se * inv for se, inv in zip(sink_exps, invs)]
    return qps, kbd, vbd, probs, sink_probs


def _kv_specs(n_blocks, reverse):
    def blk(i):
        return (n_blocks - 1 - i) if reverse else i
    cur = pl.BlockSpec((BLK, 2 * D_KV), lambda i: (blk(i), 0))
    prev = pl.BlockSpec((BLK, 2 * D_KV), lambda i: (jnp.maximum(blk(i) - 1, 0), 0))
    return cur, prev


def _attn_fwd(q, kv, ga, sinks, norm_attn_out):
    seq = q.shape[0]
    n_blocks = seq // BLK

    def body(q_ref, kvc_ref, kvp_ref, ga_ref, sink_ref, gn_ref, ya_ref, oa_ref):
        geometry = _band_geometry(pl.program_id(0))
        _, _, vbd, probs, _ = _attn_probs(q_ref, kvc_ref, kvp_ref, sink_ref, geometry)
        p4s = [_split(probs[2 * j], probs[2 * j + 1], geometry[0]).astype(BF16) for j in range(N_HEADS // 2)]
        ya = jnp.concatenate([_nn(p4, vbd[j // 4]) for j, p4 in enumerate(p4s)], axis=1)
        ya_ref[...] = ya
        silu, _ = _silu_and_grad(ga_ref[...])
        oa_ref[...] = (ya * _rstd(ya) * gn_ref[...] * silu).astype(BF16)

    row = pl.BlockSpec((BLK, D_ATTN), lambda i: (i, 0))
    kv_cur, kv_prev = _kv_specs(n_blocks, reverse=False)
    return pl.pallas_call(
        body, name="attn_fwd", grid=(n_blocks,),
        out_shape=(pltpu.HBM((seq, D_ATTN), F32), pltpu.HBM((seq, D_ATTN), BF16)),
        in_specs=[row, kv_cur, kv_prev, row, pl.BlockSpec(memory_space=pltpu.SMEM), _resident((1, D_ATTN))],
        out_specs=(row, row),
        compiler_params=_params(32, ("arbitrary",)),
    )(q, kv, kv, ga, sinks, norm_attn_out)


def _out_proj_loss(x, oc, oa, wo, norm_final, target, pc, conv_w, norm_conv_out):
    seq = x.shape[0]
    tile = TOK_TILE
    n_tiles = seq // tile

    def body(x_ref, oc_ref, oa_ref, wo_ref, gf_ref, t_ref, pc_ref, hcc_ref, hcu_ref, cw_ref, gn_ref,
             dx2_ref, doa_ref, gwo_ref, dpc_ref, small_ref, loss_acc, zbuf, dbuf):
        step = pl.program_id(0)

        def small_add(row, value):
            small_ref[row:row + 1, :] += jnp.sum(value, axis=0, keepdims=True)

        @pl.when(step == 0)
        def _():
            gwo_ref[...] = jnp.zeros_like(gwo_ref)
            small_ref[...] = jnp.zeros_like(small_ref)
            loss_acc[...] = jnp.zeros_like(loss_acc)
            dbuf[tile:tile + 8, :] = jnp.zeros((8, D_CONV), F32)

        oc, oa = oc_ref[...], oa_ref[...]
        x2 = x_ref[...] + _nn(oc, wo_ref[0:D_CONV, :]) + _nn(oa, wo_ref[D_CONV:D_MIX, :])
        r = _rstd(x2)
        xhat = x2 * r
        err = xhat * gf_ref[...] - t_ref[...]
        loss_acc[...] += jnp.sum(err * err, axis=0, keepdims=True) * (0.5 / D_MODEL)
        dy = err * (1.0 / D_MODEL)
        small_add(SMALL_NORM_FINAL, dy * xhat)
        dx2 = _rms_bwd(dy * gf_ref[...], xhat, r)
        dx2_ref[...] = dx2
        db = dx2.astype(BF16)
        do = _nt(db, wo_ref[0:D_CONV, :])
        doa_ref[...] = _nt(db, wo_ref[D_CONV:D_MIX, :])
        gwo_ref[0:D_CONV, :] += _tn(oc, db)
        gwo_ref[D_CONV:D_MIX, :] += _tn(oa, db)

        is_first_tile = step == n_tiles - 1
        zbuf[0:8, :] = jnp.where(is_first_tile, 0.0, hcc_ref[...] * hcu_ref[...])
        cb, cc, cu, z, z1, z2, conv = _conv_core(pc_ref, zbuf, cw_ref)
        silu, dsilu = _silu_and_grad(pc_ref[:, 3 * D_CONV:4 * D_CONV])
        yc = cb * conv
        rc = _rstd(yc)
        chat = yc * rc
        dn = do * silu
        dpc_ref[:, 3 * D_CONV:4 * D_CONV] = (do * (chat * gn_ref[...]) * dsilu).astype(BF16)
        small_add(SMALL_NORM_CONV, dn * chat)
        dyc = _rms_bwd(dn * gn_ref[...], chat, rc)
        dpc_ref[:, 0:D_CONV] = (dyc * conv).astype(BF16)
        dconv = dyc * cb
        small_add(SMALL_CONV_W, dconv * z2)
        small_add(SMALL_CONV_W + 1, dconv * z1)
        small_add(SMALL_CONV_W + 2, dconv * z)
        dbuf[0:tile, :] = dconv
        dz = cw_ref[2:3, :] * dconv + cw_ref[1:2, :] * dbuf[1:tile + 1, :] + cw_ref[0:1, :] * dbuf[2:tile + 2, :]
        dpc_ref[:, D_CONV:2 * D_CONV] = (dz * cu).astype(BF16)
        dpc_ref[:, 2 * D_CONV:3 * D_CONV] = (dz * cc).astype(BF16)
        dbuf[tile:tile + 8, :] = dbuf[0:8, :]

        @pl.when(step == n_tiles - 1)
        def _():
            lane = lax.broadcasted_iota(jnp.int32, (1, D_MODEL), 1)
            small_ref[SMALL_MISC:SMALL_MISC + 1, :] = jnp.where(lane == SMALL_LOSS_LANE, jnp.sum(loss_acc[...]), 0.0)

    def rev(i):
        return n_tiles - 1 - i

    def row(width):
        return pl.BlockSpec((tile, width), lambda i: (rev(i), 0))

    def halo(col_block):
        return pl.BlockSpec((8, D_CONV), lambda i: (jnp.maximum(rev(i) * (tile // 8) - 1, 0), col_block))

    def const(shape):
        return pl.BlockSpec(shape, lambda i: (0, 0))

    return pl.pallas_call(
        body, name="out_proj_loss", grid=(n_tiles,),
        out_shape=(jax.ShapeDtypeStruct((seq, D_MODEL), F32), jax.ShapeDtypeStruct((seq, D_ATTN), F32),
                   jax.ShapeDtypeStruct((D_MIX, D_MODEL), F32), jax.ShapeDtypeStruct((seq, D_PC), BF16),
                   jax.ShapeDtypeStruct((SMALL_ROWS, D_MODEL), F32)),
        in_specs=[row(D_MODEL), row(D_CONV), row(D_ATTN), _resident(wo.shape), _resident((1, D_MODEL)), row(D_MODEL),
                  row(D_PC), halo(1), halo(2), _resident(conv_w.shape), _resident((1, D_CONV))],
        out_specs=(row(D_MODEL), row(D_ATTN), const((D_MIX, D_MODEL)), row(D_PC), const((SMALL_ROWS, D_MODEL))),
        scratch_shapes=[pltpu.VMEM((1, D_MODEL), F32), pltpu.VMEM((tile + 8, D_CONV), F32),
                        pltpu.VMEM((tile + 8, D_CONV), F32)],
        compiler_params=_params(62, ("arbitrary",)),
    )(x, oc, oa, wo, norm_final, target, pc, pc, pc, conv_w, norm_conv_out)


def _attn_bwd(q, kv, ga, ya, doa, sinks, norm_attn_out, gwo, small):
    seq = q.shape[0]
    n_blocks = seq // BLK
    stage_steps = (0, n_blocks // 4, n_blocks // 2, (3 * n_blocks) // 4, n_blocks - 1)

    def body(q_ref, kvc_ref, kvp_ref, ga_ref, ya_ref, doa_ref, sink_ref, gn_ref, gwo_ref, small_ref,
             dqg_ref, small_out, gwo_sh, gna_ref, gs_ref, carry, dya_buf, *rs_scratch):
        step = pl.program_id(0)

        @pl.when(step == 0)
        def _():
            gna_ref[...] = jnp.zeros_like(gna_ref)
            gs_ref[...] = jnp.zeros_like(gs_ref)
            carry[...] = jnp.zeros_like(carry)

        ya = ya_ref[...]
        r = _rstd(ya)
        xhat = ya * r
        silu, dsilu = _silu_and_grad(ga_ref[...])
        do = doa_ref[...]
        dn = do * silu
        dqg_ref[:, D_ATTN:2 * D_ATTN] = (do * (xhat * gn_ref[...]) * dsilu).astype(BF16)
        gna_ref[...] += jnp.sum(dn * xhat, axis=0, keepdims=True)
        dya_buf[...] = _rms_bwd(dn * gn_ref[...], xhat, r).astype(BF16)

        geometry = _band_geometry(n_blocks - 1 - step)
        use_cur = geometry[0]
        lane = lax.broadcasted_iota(jnp.int32, (BLK, 128), 1)
        sink_lane = lax.broadcasted_iota(jnp.int32, (1, 128), 1)
        gsink = jnp.zeros((1, 128), F32)

        def fold(bd):
            return (jnp.where(lane < 64, bd[0:BLK], 0.0) + jnp.where(lane >= 64, bd[BLK:2 * BLK], 0.0),
                    jnp.where(lane < 64, bd[2 * BLK:3 * BLK], 0.0) + jnp.where(lane >= 64, bd[3 * BLK:4 * BLK], 0.0))

        pairs = range(N_HEADS // 2)
        qps, kbd, vbd, probs, sink_probs = _attn_probs(q_ref, kvc_ref, kvp_ref, sink_ref, geometry)
        dyps = [dya_buf[:, j * 128:(j + 1) * 128] for j in pairs]
        dps = []
        for j in pairs:
            dps += _merge(_nt(dyps[j], vbd[j // 4]), use_cur)
        deltas = [jnp.sum(p * dp, axis=-1, keepdims=True) for p, dp in zip(probs, dps)]
        dss = [p * (dp - delta) for p, dp, delta in zip(probs, dps, deltas)]
        for h in range(N_HEADS):
            dsink = -jnp.sum(sink_probs[h] * deltas[h], axis=0, keepdims=True)
            gsink = gsink + jnp.where(sink_lane == h, dsink, 0.0)
        gs_ref[...] += gsink
        ds4s = [_split(dss[2 * j], dss[2 * j + 1], use_cur).astype(BF16) for j in pairs]
        p4s = [_split(probs[2 * j], probs[2 * j + 1], use_cur).astype(BF16) for j in pairs]
        dqg_ref[:, 0:D_ATTN] = jnp.concatenate([_nn(ds4s[j], kbd[j // 4]) * SCALE for j in pairs], axis=1).astype(BF16)
        sums = []
        for group in range(N_HEADS // HEADS_PER_KV):
            acc = [jnp.zeros((BLK, 128), F32) for _ in range(4)]
            for j in range(group * 4, group * 4 + 4):
                for slot, part in enumerate(fold(_tn(ds4s[j], qps[j])) + fold(_tn(p4s[j], dyps[j]))):
                    acc[slot] = acc[slot] + part
            sums.append([a + pltpu.roll(a, 64, 1) for a in acc])
        dk_cur, dk_prev, dv_cur, dv_prev = (jnp.where(lane < 64, a, b) for a, b in zip(sums[0], sums[1]))
        dqg_ref[:, 2 * D_ATTN:2 * D_ATTN + 2 * D_KV] = (jnp.concatenate([dk_cur, dv_cur], axis=1) + carry[...]).astype(BF16)
        carry[...] = jnp.concatenate([dk_prev, dv_prev], axis=1)

        @pl.when(step == n_blocks - 1)
        def _():
            small_out[...] = small_ref[...]
            small_out[SMALL_NORM_ATTN:SMALL_NORM_ATTN + 1, :] = gna_ref[...]
            small_out[SMALL_MISC:SMALL_MISC + 1, 0:128] = small_ref[SMALL_MISC:SMALL_MISC + 1, 0:128] + gs_ref[...]

        for at, stage in zip(stage_steps, _rs_wout_stages(gwo_ref, gwo_sh, *rs_scratch)):
            pl.when(step == at)(stage)

    row = pl.BlockSpec((BLK, D_ATTN), lambda i: (n_blocks - 1 - i, 0))
    kv_cur, kv_prev = _kv_specs(n_blocks, reverse=True)
    any_spec = pl.BlockSpec(memory_space=pl.ANY)
    return pl.pallas_call(
        body, name="attn_bwd", grid=(n_blocks,),
        out_shape=(jax.ShapeDtypeStruct((seq, D_QG), BF16), jax.ShapeDtypeStruct(small.shape, F32),
                   jax.ShapeDtypeStruct((gwo.shape[0] // N_CHIPS, gwo.shape[1]), F32)),
        in_specs=[row, kv_cur, kv_prev, row, row, row, pl.BlockSpec(memory_space=pltpu.SMEM), _resident((1, D_ATTN)),
                  any_spec, _resident(small.shape)],
        out_specs=(pl.BlockSpec((BLK, D_QG), lambda i: (n_blocks - 1 - i, 0)),
                   pl.BlockSpec(small.shape, lambda i: (0, 0)), any_spec),
        scratch_shapes=[pltpu.VMEM((1, D_ATTN), F32), pltpu.VMEM((1, 128), F32),
                        pltpu.VMEM((BLK, 2 * D_KV), F32), pltpu.VMEM((BLK, D_ATTN), BF16)] + _rs_wout_scratch(gwo.shape),
        compiler_params=_params(44, ("arbitrary",)),
    )(q, kv, kv, ga, ya, doa, sinks, norm_attn_out, gwo, small)


GBLK = 256
GSUB = 64
PAIR_RING = 4
LAG_PAIR, LAG_HOP1, LAG_HOP2 = 1, 7, 14


def _bwd_in(dpc, dqg, h, wt, x, norm_in, dx2, small):
    seq = x.shape[0]
    n_blk = D_IN_PROJ // GBLK
    per_chip = n_blk // N_CHIPS
    n_slots = (n_blk + 1) // 2
    n_sub = GBLK // GSUB
    chip_rows = D_IN_PROJ // N_CHIPS
    tile = TOK_TILE
    n_tiles = seq // tile
    n_steps = n_blk + max(n_tiles, LAG_HOP2)
    chunk = min(seq, 512)
    blk_q, blk_kv, blk_ga = ROW_Q // GBLK, ROW_KV // GBLK, ROW_GA // GBLK

    def block_of(i):
        k = i % N_CHIPS
        robin = per_chip * ((k % 2) * 2 + k // 2) + i // N_CHIPS
        if isinstance(i, int):
            return robin if i < per_chip * N_CHIPS else i
        return jnp.where(i < per_chip * N_CHIPS, robin, i)

    def owner_of(i):
        return (i // N_CHIPS) % 2

    def slot_of(i):
        return (i // (2 * N_CHIPS)) * N_CHIPS + i % N_CHIPS

    def body(dpc_ref, dqg_ref, wt_ref, h_ref, x_ref, g_ref, dx2_ref, small_ref, gx_ref, small_sum, gwt_sh,
             dh_acc, gni, keep, pbuf, xbuf, land, land2, small_land,
             pair_send, pair_recv, h1_send, h1_recv, h2_send, h2_recv, sw_send, sw_recv, sm_send, sm_recv, out_sem):
        step = pl.program_id(0)
        x_i, y_i, c = lax.axis_index("x"), lax.axis_index("y"), lax.axis_index("c")
        me = 4 * x_i + 2 * y_i + c
        j = 2 * x_i + y_i
        pa = (_xor(x_i, 1 - c), _xor(y_i, c), c)
        pb = (_xor(x_i, c), _xor(y_i, 1 - c), c)
        sib = (x_i, y_i, 1 - c)
        ja = 2 * pa[0] + pa[1]
        jb = 2 * pb[0] + pb[1]
        jd = 3 - j

        def remote(src, dst, send, recv, to):
            return pltpu.make_async_remote_copy(src_ref=src, dst_ref=dst, send_sem=send, recv_sem=recv,
                                                device_id=to, device_id_type=MESH)

        def piece(ref, slot, u, n):
            return ref.at[slot, pl.ds(u * GSUB, n * GSUB), :]

        def chip_rows_at(ref, local, n):
            return ref.at[pl.ds(pl.multiple_of(local, GSUB), n * GSUB), :]

        def pair_copy(i):
            slot = slot_of(i)
            return remote(pbuf.at[i % PAIR_RING], land.at[slot], pair_send.at[slot], pair_recv.at[slot], sib)

        def h1_copy(slot, u, n):
            k = slot * n_sub + u
            return remote(piece(xbuf, slot, u, n), piece(xbuf, slot, u, n), h1_send.at[k], h1_recv.at[k], pa)

        def h2_copy(slot, u, n, local):
            k = slot * n_sub + u
            return remote(piece(xbuf, slot, u, n), chip_rows_at(land2, local, n), h2_send.at[k], h2_recv.at[k], pb)

        def sw_copy(slot, u, n, local):
            k = slot * n_sub + u
            return remote(piece(keep, slot, u, n), chip_rows_at(gwt_sh, local, n), sw_send.at[k], sw_recv.at[k], sib)

        def owned(i):
            return (i >= 0) & (i < n_blk) & (owner_of(i) == c)

        def chip_of(blk, u):
            row = blk * GBLK + u * GSUB
            chip = row // chip_rows
            return chip, row - chip * chip_rows

        def pieces(blk):
            first, local = chip_of(blk, 0)
            whole = first == chip_of(blk, n_sub - 1)[0]
            if isinstance(blk, int):
                return [(True, 0, n_sub, first, local)] if whole else [(True, u, 1) + chip_of(blk, u) for u in range(n_sub)]
            return [(whole, 0, n_sub, first, local)] + [(jnp.logical_not(whole), u, 1) + chip_of(blk, u) for u in range(n_sub)]

        @pl.when(step == 0)
        def _():
            dh_acc[...] = jnp.zeros_like(dh_acc)
            gni[...] = jnp.zeros_like(gni)

        @pl.when(step < n_blk)
        def _():
            from_pc = block_of(step) < blk_q
            block = _tn(jnp.where(from_pc, dpc_ref[...], dqg_ref[...]), h_ref[...])
            for t in range(0, seq, chunk):
                d = jnp.where(from_pc, dpc_ref[t:t + chunk, :], dqg_ref[t:t + chunk, :])
                dh_acc[t:t + chunk, :] += _nn(d, wt_ref[...])

            @pl.when(owner_of(step) == c)
            def _():
                keep[slot_of(step)] = block

            @pl.when(owner_of(step) != c)
            def _():
                @pl.when(step >= 2 * PAIR_RING)
                def _():
                    pair_copy(step - 2 * PAIR_RING).wait_send()
                pbuf[step % PAIR_RING] = block.astype(BF16)
                pair_copy(step).start()

        i1 = step - LAG_PAIR

        @pl.when(owned(i1))
        def _():
            slot = slot_of(i1)
            pair_copy(i1).wait_recv()
            _accumulate(keep.at[slot], land.at[slot])
            for cond, u, n, chip, _ in pieces(block_of(i1)):
                @pl.when(cond & ((chip == ja) | (chip == jd)))
                def _(u=u, n=n):
                    _cast_rows(piece(keep, slot, u, n), piece(xbuf, slot, u, n))
                    h1_copy(slot, u, n).start()

        i2 = step - LAG_HOP1

        @pl.when(owned(i2))
        def _():
            slot = slot_of(i2)
            for cond, u, n, chip, local in pieces(block_of(i2)):
                @pl.when(cond & ((chip == j) | (chip == jb)))
                def _(u=u, n=n, chip=chip, local=local):
                    h1_copy(slot, u, n).wait_recv()
                    _accumulate(piece(keep, slot, u, n), piece(xbuf, slot, u, n))

                    @pl.when(chip == jb)
                    def _():
                        _cast_rows(piece(keep, slot, u, n), piece(xbuf, slot, u, n))
                        h2_copy(slot, u, n, local).start()

                @pl.when(cond & ((chip == ja) | (chip == jd)))
                def _(u=u, n=n):
                    h1_copy(slot, u, n).wait_send()

        i3 = step - LAG_HOP2

        @pl.when(owned(i3))
        def _():
            slot = slot_of(i3)
            for cond, u, n, chip, local in pieces(block_of(i3)):
                @pl.when(cond & (chip == j))
                def _(u=u, n=n, local=local):
                    h2_copy(slot, u, n, local).wait_recv()
                    _accumulate(piece(keep, slot, u, n), chip_rows_at(land2, local, n))
                    mine = pltpu.make_async_copy(piece(keep, slot, u, n), chip_rows_at(gwt_sh, local, n), out_sem.at[0])
                    mine.start()
                    sw_copy(slot, u, n, local).start()
                    mine.wait()

                @pl.when(cond & (chip == jb))
                def _(u=u, n=n, local=local):
                    h2_copy(slot, u, n, local).wait_send()

        e = step - n_blk

        @pl.when((e >= 0) & (e < n_tiles))
        def _():
            dh = dh_acc[pl.ds(pl.multiple_of(e * tile, tile), tile), :]
            xv = x_ref[...]
            r = _rstd(xv)
            xhat = xv * r
            gni[...] += jnp.sum(dh * xhat, axis=0, keepdims=True)
            gx_ref[...] = _rms_bwd(dh * g_ref[...], xhat, r) + dx2_ref[...]

        @pl.when(step == n_steps - 1)
        def _():
            small_land[me] = small_ref[...]
            small_land[me, SMALL_NORM_IN:SMALL_NORM_IN + 1, :] = gni[...]
            others = [(dx, dy, dc) for dx in (0, 1) for dy in (0, 1) for dc in (0, 1)][1:]
            sends = [remote(small_land.at[me], small_land.at[me], sm_send.at[k], sm_recv.at[k],
                            (_xor(x_i, dx), _xor(y_i, dy), _xor(c, dc))) for k, (dx, dy, dc) in enumerate(others)]
            for cp in sends:
                cp.start()
            for i in range(n_blk):
                if i + 2 * PAIR_RING >= n_blk:
                    @pl.when(owner_of(i) != c)
                    def _(i=i):
                        pair_copy(i).wait_send()
                for _, u, n, chip, local in pieces(block_of(i)):
                    @pl.when((j == chip) & (c == owner_of(i)))
                    def _(i=i, u=u, n=n, local=local):
                        sw_copy(slot_of(i), u, n, local).wait_send()

                    @pl.when((j == chip) & (c != owner_of(i)))
                    def _(i=i, u=u, n=n, local=local):
                        sw_copy(slot_of(i), u, n, local).wait_recv()
            for cp in sends:
                cp.wait_recv()
            total = small_land[0]
            for dev in range(1, 8):
                total = total + small_land[dev]
            small_sum[...] = total
            for cp in sends:
                cp.wait_send()

    def blk_at(i):
        return block_of(jnp.clip(i, 0, n_blk - 1))

    last_pc_step = max(i for i in range(n_blk) if block_of(i) < blk_q)

    def next_block(i, in_pc):
        i = jnp.clip(i, 0, n_blk - 1)
        step = jnp.full_like(i, last_pc_step if in_pc else n_blk - 1)
        for ahead in reversed(range(N_CHIPS)):
            cand = jnp.minimum(i + ahead, n_blk - 1)
            step = jnp.where((block_of(cand) < blk_q) == in_pc, cand, step)
        return block_of(step)

    def dqg_block(i):
        b = next_block(i, False)
        q_blk = jnp.clip(b - blk_q, 0, blk_kv - blk_q - 1)
        ga_blk = (D_ATTN // GBLK) + jnp.clip(b - blk_ga, 0, n_blk - blk_ga - 1)
        return jnp.where(b < blk_kv, q_blk, jnp.where(b == blk_kv, 2 * D_ATTN // GBLK, ga_blk))

    def tok(i):
        return (jnp.clip(i - n_blk, 0, n_tiles - 1), 0)

    n_piece = n_slots * n_sub
    dma = pltpu.SemaphoreType.DMA
    return pl.pallas_call(
        body, name="bwd_in", grid=(n_steps,),
        out_shape=(jax.ShapeDtypeStruct((seq, D_MODEL), F32), jax.ShapeDtypeStruct(small.shape, F32),
                   jax.ShapeDtypeStruct((chip_rows, D_MODEL), F32)),
        in_specs=[pl.BlockSpec((seq, GBLK), lambda i: (0, next_block(i, True))),
                  pl.BlockSpec((seq, GBLK), lambda i: (0, dqg_block(i))),
                  pl.BlockSpec((GBLK, D_MODEL), lambda i: (blk_at(i), 0)),
                  _resident(h.shape),
                  pl.BlockSpec((tile, D_MODEL), tok), _resident((1, D_MODEL)), pl.BlockSpec((tile, D_MODEL), tok),
                  _resident(small.shape)],
        out_specs=(pl.BlockSpec((tile, D_MODEL), tok), pl.BlockSpec(small.shape, lambda i: (0, 0)),
                   pl.BlockSpec(memory_space=pl.ANY)),
        scratch_shapes=[pltpu.VMEM((seq, D_MODEL), F32), pltpu.VMEM((1, D_MODEL), F32),
                        pltpu.VMEM((n_slots, GBLK, D_MODEL), F32), pltpu.VMEM((PAIR_RING, GBLK, D_MODEL), BF16),
                        pltpu.VMEM((n_slots, GBLK, D_MODEL), BF16), pltpu.VMEM((n_slots, GBLK, D_MODEL), BF16),
                        pltpu.VMEM((chip_rows, D_MODEL), BF16), pltpu.VMEM((8,) + small.shape, F32),
                        dma((n_slots,)), dma((n_slots,)), dma((n_piece,)), dma((n_piece,)), dma((n_piece,)),
                        dma((n_piece,)), dma((n_piece,)), dma((n_piece,)), dma((7,)), dma((7,)), dma((1,))],
        compiler_params=_params(62, ("arbitrary",)),
    )(dpc, dqg, wt, h, x, norm_in, dx2, small)


def _accumulate(dst_ref, src_ref, rows=16):
    def step(i, carry):
        sl = pl.ds(pl.multiple_of(i * rows, rows), rows)
        dst_ref[sl, :] = dst_ref[sl, :] + src_ref[sl, :].astype(F32)
        return carry
    lax.fori_loop(0, dst_ref.shape[0] // rows, step, 0)


def _rs_wout_scratch(gwo_shape):
    o_half, width = gwo_shape[0] // N_CHIPS // 2, gwo_shape[1]
    return [pltpu.VMEM((4, o_half, width), F32), pltpu.VMEM((4, o_half, width), BF16),
            pltpu.VMEM((4, o_half, width), BF16), pltpu.VMEM((2, o_half, width), BF16),
            pltpu.VMEM((o_half, width), BF16),
            pltpu.SemaphoreType.DMA((8,)), pltpu.SemaphoreType.DMA((8,)), pltpu.SemaphoreType.DMA((4,))]


def _rs_wout_stages(gwo_ref, gwo_sh, acc_o, sb_o, r1o, r2o, r3o, send_sems, recv_sems, local_sems):
    o_rows = gwo_ref.shape[0] // N_CHIPS
    o_half = o_rows // 2
    x, y, c = lax.axis_index("x"), lax.axis_index("y"), lax.axis_index("c")
    j = 2 * x + y
    pa = (_xor(x, 1 - c), _xor(y, c), c)
    pb = (_xor(x, c), _xor(y, 1 - c), c)
    sib = (x, y, 1 - c)
    ja = 2 * pa[0] + pa[1]
    jb = 2 * pb[0] + pb[1]
    jd = 3 - j
    order = (ja, jd, jb, j)
    sib_order = (jb, jd, ja, j)

    def rcopy(k, src, dst, to):
        return pltpu.make_async_remote_copy(src_ref=src, dst_ref=dst, send_sem=send_sems.at[k],
                                            recv_sem=recv_sems.at[k], device_id=to, device_id_type=MESH)

    def o_rows_of(chip, half):
        return gwo_ref.at[pl.ds(pl.multiple_of(chip * o_rows + half * o_half, 8), o_half), :]

    def load_all(chips, half):
        cps = [pltpu.make_async_copy(o_rows_of(chip, half), acc_o.at[s], local_sems.at[s]) for s, chip in enumerate(chips)]
        for cp in cps:
            cp.start()
        return cps

    def pair_send(s):
        return rcopy(s, sb_o.at[s], r1o.at[s], sib)

    def out_half(half):
        return gwo_sh.at[pl.ds(pl.multiple_of(half * o_half, 8), o_half), :]

    hop1 = [rcopy(4 + s, sb_o.at[s], r2o.at[s], pa) for s in range(2)]
    hop2 = rcopy(6, sb_o.at[2], r3o, pb)
    swap = rcopy(7, acc_o.at[3], out_half(c), sib)
    mine = pltpu.make_async_copy(acc_o.at[3], out_half(c), local_sems.at[0])

    def resend(s, copy):
        pair_send(s).wait_send()
        _cast_rows(acc_o.at[s], sb_o.at[s])
        copy.start()

    def stage_pair():
        for s, cp in enumerate(load_all(sib_order, 1 - c)):
            cp.wait()
            _cast_rows(acc_o.at[s], sb_o.at[s])
            pair_send(s).start()

    def stage_hop1():
        for s, cp in enumerate(load_all(order, c)):
            cp.wait()
            pair_send(s).wait_recv()
            _accumulate(acc_o.at[s], r1o.at[s])
            if s < 2:
                resend(s, hop1[s])

    def stage_hop2():
        hop1[1].wait_recv()
        _accumulate(acc_o.at[2], r2o.at[1])
        resend(2, hop2)
        hop1[0].wait_recv()
        _accumulate(acc_o.at[3], r2o.at[0])

    def stage_final():
        hop2.wait_recv()
        _accumulate(acc_o.at[3], r3o)
        mine.start()
        swap.start()

    def stage_drain():
        rcopy(7, acc_o.at[3], out_half(1 - c), sib).wait_recv()
        for cp in [pair_send(3)] + hop1 + [hop2, swap]:
            cp.wait_send()
        mine.wait()

    return stage_pair, stage_hop1, stage_hop2, stage_final, stage_drain


def _adamw(name, w, g, m, v, rows):
    def body(w_ref, g_ref, m_ref, v_ref, go_ref, d_ref, nm_ref, nv_ref):
        _adamw_update(w_ref, g_ref, m_ref, v_ref, go_ref, d_ref, nm_ref, nv_ref)

    spec = pl.BlockSpec((rows, w.shape[1]), lambda i: (i, 0))
    shape = jax.ShapeDtypeStruct(w.shape, F32)
    return pl.pallas_call(
        body, name="adamw_" + name, grid=(w.shape[0] // rows,),
        out_shape=(shape,) * 4, in_specs=[spec] * 4, out_specs=(spec,) * 4,
        compiler_params=_params(32, ("arbitrary",)),
    )(w, g, m, v)


def _adamw_update(w_ref, g_ref, m_ref, v_ref, go_ref, d_ref, nm_ref, nv_ref):
    gv = g_ref[...]
    go_ref[...] = gv
    nm = ADAM_B1 * m_ref[...] + (1.0 - ADAM_B1) * gv
    nv = ADAM_B2 * v_ref[...] + (1.0 - ADAM_B2) * (gv * gv)
    m_hat = nm / (1.0 - ADAM_B1 ** ADAM_STEP)
    v_hat = nv / (1.0 - ADAM_B2 ** ADAM_STEP)
    d_ref[...] = -ADAM_LR * (m_hat / (jnp.sqrt(v_hat) + ADAM_EPS) + ADAM_WD * w_ref[...])
    nm_ref[...] = nm
    nv_ref[...] = nv


def _adamw_small(weights, grads, ms, vs):
    n = len(weights)

    def body(*refs):
        ins, outs = refs[:4 * n], refs[4 * n:]
        for k in range(n):
            _adamw_update(*ins[4 * k:4 * k + 4], *outs[4 * k:4 * k + 4])

    flat = [a for group in zip(weights, grads, ms, vs) for a in group]
    vmem = pl.BlockSpec(memory_space=pltpu.VMEM)
    out = pl.pallas_call(
        body, name="adamw_small",
        out_shape=tuple(jax.ShapeDtypeStruct(w.shape, F32) for w in weights for _ in range(4)),
        in_specs=[vmem] * (4 * n), out_specs=(vmem,) * (4 * n),
    )(*flat)
    return [tuple(out[4 * k:4 * k + 4]) for k in range(n)]


def kernel(x, norm_in, w_in, conv_w, attn_sinks, norm_conv_out, norm_attn_out, w_out, norm_final, loss_target, m_norm_in, m_w_in, m_conv_w, m_attn_sinks, m_norm_conv_out, m_norm_attn_out, m_w_out, m_norm_final, v_norm_in, v_w_in, v_conv_w, v_attn_sinks, v_norm_conv_out, v_norm_attn_out, v_w_out, v_norm_final):
    chip = 2 * lax.axis_index("x") + lax.axis_index("y")
    xs, target = x[0], loss_target[0]
    norm_final2 = norm_final.reshape(1, D_MODEL)
    w_in_t, m_w_in_t, v_w_in_t = w_in[0].T, m_w_in[0].T, v_w_in[0].T

    wt, cw4 = _ag_weights(w_in_t, conv_w[0])
    cw = jnp.transpose(cw4, (1, 0, 2)).reshape(3, D_CONV)

    h, pc, q, kv, ga, oc, wo = _fwd_in(xs, norm_in, wt, w_out[0], cw, norm_conv_out)
    ya, oa = _attn_fwd(q, kv, ga, attn_sinks, norm_attn_out)
    dx2, doa, gwo, dpc, small = _out_proj_loss(xs, oc, oa, wo, norm_final2, target, pc, cw, norm_conv_out)
    dqg, small, gwo_sh = _attn_bwd(q, kv, ga, ya, doa, attn_sinks, norm_attn_out, gwo, small)
    grad_x, small_sum, gwt_sh = _bwd_in(dpc, dqg, h, wt, xs, norm_in, dx2, small)

    loss = small_sum[SMALL_MISC, SMALL_LOSS_LANE]
    g_norm_in, g_norm_conv, g_norm_attn = (small_sum[r:r + 1] for r in (SMALL_NORM_IN, SMALL_NORM_CONV, SMALL_NORM_ATTN))
    g_norm_final = small_sum[SMALL_NORM_FINAL]
    g_conv_w = lax.dynamic_slice(small_sum[SMALL_CONV_W:SMALL_CONV_W + 3], (0, chip * (D_CONV // N_CHIPS)),
                                 (3, D_CONV // N_CHIPS))[None]
    g_sinks = small_sum[SMALL_MISC:SMALL_MISC + 1, 0:N_HEADS]

    def two_d(a):
        return a.reshape(-1, a.shape[-1])

    up_w_in = tuple(o.T[None] for o in _adamw("w_in", w_in_t, gwt_sh, m_w_in_t, v_w_in_t, 200))
    up_w_out = tuple(o[None] for o in _adamw("w_out", w_out[0], gwo_sh, m_w_out[0], v_w_out[0], 128))
    small_w = (norm_in, conv_w, attn_sinks, norm_conv_out, norm_attn_out, norm_final)
    small_g = (g_norm_in, g_conv_w, g_sinks, g_norm_conv, g_norm_attn, g_norm_final)
    small_m = (m_norm_in, m_conv_w, m_attn_sinks, m_norm_conv_out, m_norm_attn_out, m_norm_final)
    small_v = (v_norm_in, v_conv_w, v_attn_sinks, v_norm_conv_out, v_norm_attn_out, v_norm_final)
    up_small = _adamw_small(*(tuple(two_d(a) for a in group) for group in (small_w, small_g, small_m, small_v)))
    up_small = [tuple(o.reshape(w.shape) for o in up) for up, w in zip(up_small, small_w)]
    up_norm_in, up_conv_w, up_sinks, up_norm_conv, up_norm_attn, up_norm_final = up_small
    updates = (up_norm_in, up_w_in, up_conv_w, up_sinks, up_norm_conv, up_norm_attn, up_w_out, up_norm_final)
    grads_out, deltas, new_m, new_v = zip(*updates)
    return (loss, grad_x[None], *grads_out, *deltas, *new_m, *new_v)
```

```python
import jax
import jax.numpy as jnp
from jax import lax
from jax.experimental import pallas as pl
from jax.experimental.pallas import tpu as pltpu

F32 = jnp.float32
BF16 = jnp.bfloat16
MESH = pl.DeviceIdType.MESH

D_MODEL = 1024
D_CONV = 1024
D_ATTN = 1024
D_KV = 128
D_QG = 2 * D_ATTN + 2 * D_KV
D_MIX = D_CONV + D_ATTN
D_PC = 4 * D_CONV
D_IN_PROJ = D_PC + 2 * D_ATTN + 2 * D_KV
ROW_Q = D_PC
ROW_KV = ROW_Q + D_ATTN
ROW_GA = ROW_KV + 2 * D_KV
N_HEADS = 16
HEAD_DIM = 64
HEADS_PER_KV = 8
BLK = 128
N_CHIPS = 4
RMS_EPS = 1e-5
SCALE = HEAD_DIM ** -0.5
SLOPES = tuple(2.0 ** (-8.0 * (h + 1) / N_HEADS) for h in range(N_HEADS))

ADAM_LR, ADAM_B1, ADAM_B2, ADAM_EPS, ADAM_WD, ADAM_STEP = 0.001, 0.9, 0.999, 1e-08, 0.01, 10

SMALL_ROWS = 8
SMALL_NORM_IN, SMALL_NORM_CONV, SMALL_NORM_ATTN, SMALL_NORM_FINAL, SMALL_CONV_W, SMALL_MISC = 0, 1, 2, 3, 4, 7
SMALL_LOSS_LANE = N_HEADS

TOK_TILE = 256
MIB = 1 << 20


def _params(vmem_mib, semantics=None):
    return pltpu.CompilerParams(dimension_semantics=semantics, vmem_limit_bytes=vmem_mib * MIB)


def _nn(a, b):
    return jnp.dot(a, b, preferred_element_type=F32)


def _nt(a, b):
    return lax.dot_general(a, b, (((1,), (1,)), ((), ())), preferred_element_type=F32)


def _tn(a, b):
    return lax.dot_general(a, b, (((0,), (0,)), ((), ())), preferred_element_type=F32)


def _rstd(v):
    return lax.rsqrt(jnp.mean(v * v, axis=-1, keepdims=True) + RMS_EPS)


def _rms_bwd(g, xhat, rstd):
    return rstd * (g - xhat * jnp.mean(g * xhat, axis=-1, keepdims=True))


def _silu_and_grad(g):
    s = jax.nn.sigmoid(g)
    return g * s, s * (1.0 + g * (1.0 - s))


def _resident(shape):
    return pl.BlockSpec(shape, lambda *_: (0,) * len(shape), pipeline_mode=pl.Buffered(1))


def _xor(a, b):
    return a + b - 2 * a * b


def _cast_rows(src_ref, dst_ref, rows=32):
    def step(i, carry):
        sl = pl.ds(pl.multiple_of(i * rows, rows), rows)
        dst_ref[sl, :] = src_ref[sl, :].astype(dst_ref.dtype)
        return carry
    lax.fori_loop(0, src_ref.shape[0] // rows, step, 0)


def _ag_scratch(shard_shape):
    rows, width = shard_shape
    return [pltpu.VMEM((rows, width), F32), pltpu.VMEM((rows, width), BF16), pltpu.VMEM((3, rows // 2, width), BF16),
            pltpu.SemaphoreType.DMA((6,)), pltpu.SemaphoreType.DMA((6,)), pltpu.SemaphoreType.DMA((4,))]


def _ag_stages(sh_ref, out, f32_buf, own, land, send_sems, recv_sems, local_sems):
    rows = sh_ref.shape[0]
    half = rows // 2
    x, y, c = lax.axis_index("x"), lax.axis_index("y"), lax.axis_index("c")
    j = 2 * x + y
    p1 = (_xor(x, c), _xor(y, 1 - c), c)
    p2 = (_xor(x, 1 - c), _xor(y, c), c)
    sib = (x, y, 1 - c)
    j1 = 2 * p1[0] + p1[1]
    j2 = 2 * p2[0] + p2[1]
    j3 = 3 - j

    def rows_of(chip, hf):
        return out.at[pl.ds(pl.multiple_of(chip * rows + hf * half, 16), half), :]

    def rcopy(k, src, dst, to):
        return pltpu.make_async_remote_copy(src_ref=src, dst_ref=dst, send_sem=send_sems.at[k],
                                            recv_sem=recv_sems.at[k], device_id=to, device_id_type=MESH)

    my_half = own.at[pl.ds(pl.multiple_of(c * half, 16), half), :]
    hop1 = rcopy(0, my_half, land.at[0], p1)
    hop2_own = rcopy(1, my_half, land.at[1], p2)
    hop2_fwd = rcopy(2, land.at[0], land.at[2], p2)
    swaps = [rcopy(3 + s, land.at[s], rows_of(chip, c), sib) for s, chip in enumerate((j1, j2, j3))]
    keeps = [pltpu.make_async_copy(land.at[s], rows_of(chip, c), local_sems.at[1 + s]) for s, chip in enumerate((j1, j2, j3))]
    load = pltpu.make_async_copy(sh_ref, f32_buf, local_sems.at[0])
    own_out = pltpu.make_async_copy(own, out.at[pl.ds(pl.multiple_of(j * rows, 16), rows), :], local_sems.at[0])

    def stage_send():
        load.start()
        load.wait()
        _cast_rows(f32_buf, own)
        own_out.start()
        hop1.start()

    def stage_forward():
        hop1.wait_recv()
        hop2_own.start()
        hop2_fwd.start()
        swaps[0].start()
        keeps[0].start()

    def stage_publish():
        hop2_own.wait_recv()
        swaps[1].start()
        keeps[1].start()
        hop2_fwd.wait_recv()
        swaps[2].start()
        keeps[2].start()

    def stage_drain():
        for s, chip in enumerate((j2, j1, j3)):
            rcopy(3 + s, my_half, rows_of(chip, 1 - c), sib).wait_recv()
        for cp in [hop1, hop2_own, hop2_fwd] + swaps:
            cp.wait_send()
        for cp in [own_out] + keeps:
            cp.wait()

    return stage_send, stage_forward, stage_publish, stage_drain


def _ag_weights(wt_sh, cw_sh):
    def body(wt_ref, cw_ref, wt_out, cw_out, *scratch):
        cw_send, cw_recv, cw_local = scratch[-3:]
        x, y, c = lax.axis_index("x"), lax.axis_index("y"), lax.axis_index("c")
        j = 2 * x + y
        p1 = (_xor(x, c), _xor(y, 1 - c), c)
        p2 = (_xor(x, 1 - c), _xor(y, c), c)
        j1 = 2 * p1[0] + p1[1]

        def cw_copy(k, chip, to):
            src = cw_ref if k != 2 else cw_out.at[chip]
            return pltpu.make_async_remote_copy(src_ref=src, dst_ref=cw_out.at[chip], send_sem=cw_send.at[k],
                                                recv_sem=cw_recv.at[k], device_id=to, device_id_type=MESH)

        mine = pltpu.make_async_copy(cw_ref, cw_out.at[j], cw_local.at[0])
        mine.start()
        first = cw_copy(0, j, p1)
        first.start()
        stages = _ag_stages(wt_ref, wt_out, *scratch[:-3])
        stages[0]()
        first.wait_recv()
        second = [cw_copy(1, j, p2), cw_copy(2, j1, p2)]
        for cp in second:
            cp.start()
        for stage in stages[1:]:
            stage()
        for cp in second:
            cp.wait_recv()
        for cp in [first] + second:
            cp.wait_send()
        mine.wait()

    any_spec = pl.BlockSpec(memory_space=pl.ANY)
    dma = pltpu.SemaphoreType.DMA
    return pl.pallas_call(
        body, name="ag_weights",
        out_shape=(pltpu.HBM((N_CHIPS * wt_sh.shape[0], wt_sh.shape[1]), BF16),
                   jax.ShapeDtypeStruct((N_CHIPS,) + cw_sh.shape, cw_sh.dtype)),
        in_specs=[any_spec, any_spec], out_specs=(any_spec, any_spec),
        scratch_shapes=_ag_scratch(wt_sh.shape) + [dma((3,)), dma((3,)), dma((1,))],
        compiler_params=_params(32),
    )(wt_sh, cw_sh)


def _fwd_in(x, norm_in, wt, wo_sh, conv_w, norm_conv_out):
    seq = x.shape[0]
    tile = TOK_TILE
    n_tiles = seq // tile
    stage_steps = (0, n_tiles // 4, (5 * n_tiles) // 8, n_tiles - 1)

    def body(x_ref, g_ref, wt_ref, wo_ref, cw_ref, gn_ref, h_ref, pc_ref, q_ref, kv_ref, ga_ref, oc_ref, wo_out,
             zbuf, *ag_scratch):
        stages = _ag_stages(wo_ref, wo_out, *ag_scratch)
        for at, stage in zip(stage_steps[:-1], stages[:-1]):
            pl.when(pl.program_id(0) == at)(stage)

        @pl.when(pl.program_id(0) == 0)
        def _():
            zbuf[0:8, :] = jnp.zeros((8, D_CONV), F32)

        xv = x_ref[...]
        h = (xv * _rstd(xv) * g_ref[...]).astype(BF16)
        h_ref[...] = h
        for blk in range(D_PC // D_CONV):
            pc_ref[:, blk * D_CONV:(blk + 1) * D_CONV] = _nt(h, wt_ref[blk * D_CONV:(blk + 1) * D_CONV, :])
        q_ref[...] = _nt(h, wt_ref[ROW_Q:ROW_KV, :])
        kv_ref[...] = _nt(h, wt_ref[ROW_KV:ROW_GA, :])
        ga_ref[...] = _nt(h, wt_ref[ROW_GA:D_IN_PROJ, :])

        cb, _, _, _, _, _, conv = _conv_core(pc_ref, zbuf, cw_ref)
        yc = cb * conv
        silu, _ = _silu_and_grad(pc_ref[:, 3 * D_CONV:4 * D_CONV])
        oc_ref[...] = (yc * _rstd(yc) * gn_ref[...] * silu).astype(BF16)
        zbuf[0:8, :] = zbuf[tile:tile + 8, :]
        pl.when(pl.program_id(0) == stage_steps[-1])(stages[-1])

    def row(width):
        return pl.BlockSpec((tile, width), lambda i: (i, 0))

    any_spec = pl.BlockSpec(memory_space=pl.ANY)
    return pl.pallas_call(
        body, name="fwd_in", grid=(n_tiles,),
        out_shape=(jax.ShapeDtypeStruct((seq, D_MODEL), BF16), jax.ShapeDtypeStruct((seq, D_PC), F32),
                   jax.ShapeDtypeStruct((seq, D_ATTN), F32), jax.ShapeDtypeStruct((seq, 2 * D_KV), F32),
                   jax.ShapeDtypeStruct((seq, D_ATTN), F32), jax.ShapeDtypeStruct((seq, D_CONV), BF16),
                   jax.ShapeDtypeStruct((N_CHIPS * wo_sh.shape[0], wo_sh.shape[1]), BF16)),
        in_specs=[row(D_MODEL), _resident((1, D_MODEL)), _resident(wt.shape), any_spec,
                  _resident(conv_w.shape), _resident((1, D_CONV))],
        out_specs=(row(D_MODEL), row(D_PC), row(D_ATTN), row(2 * D_KV), row(D_ATTN), row(D_CONV), any_spec),
        scratch_shapes=[pltpu.VMEM((tile + 8, D_CONV), F32)] + _ag_scratch(wo_sh.shape),
        compiler_params=_params(56, ("arbitrary",)),
    )(x, norm_in, wt, wo_sh, conv_w, norm_conv_out)


def _conv_core(pc_ref, zbuf, cw_ref):
    tile = pc_ref.shape[0]
    cb = pc_ref[:, 0:D_CONV]
    cc = pc_ref[:, D_CONV:2 * D_CONV]
    cu = pc_ref[:, 2 * D_CONV:3 * D_CONV]
    z = cc * cu
    zbuf[8:tile + 8, :] = z
    z1 = zbuf[7:tile + 7, :]
    z2 = zbuf[6:tile + 6, :]
    conv = cw_ref[0:1, :] * z2 + cw_ref[1:2, :] * z1 + cw_ref[2:3, :] * z
    return cb, cc, cu, z, z1, z2, conv


def _band_geometry(block_index):
    qi = lax.broadcasted_iota(jnp.int32, (BLK, BLK), 0)
    kp = lax.broadcasted_iota(jnp.int32, (BLK, BLK), 1)
    use_cur = kp <= qi
    dist = jnp.where(use_cur, qi - kp, qi - kp + BLK).astype(F32)
    valid = use_cur | (block_index > 0)
    return use_cur, dist, valid


def _block_diag(cur, prev, group):
    lane = lax.broadcasted_iota(jnp.int32, cur.shape, 1)

    def halves(t):
        other = pltpu.roll(t, 64, 1)
        lo, hi = (t, other) if group == 0 else (other, t)
        return jnp.where(lane < 64, lo, 0.0), jnp.where(lane >= 64, hi, 0.0)

    return jnp.concatenate(halves(cur) + halves(prev), axis=0).astype(BF16)


def _merge(s4, use_cur):
    return (jnp.where(use_cur, s4[:, 0:BLK], s4[:, 2 * BLK:3 * BLK]),
            jnp.where(use_cur, s4[:, BLK:2 * BLK], s4[:, 3 * BLK:4 * BLK]))


def _split(a, b, use_cur):
    return jnp.concatenate([jnp.where(use_cur, a, 0.0), jnp.where(use_cur, b, 0.0),
                            jnp.where(use_cur, 0.0, a), jnp.where(use_cur, 0.0, b)], axis=1)


def _softmax_head(s, head, sink, dist, valid):
    sc = jnp.where(valid, s - SLOPES[head] * dist, -jnp.inf)
    m = jnp.maximum(jnp.max(sc, axis=-1, keepdims=True), sink)
    p = jnp.exp(sc - m)
    es = jnp.exp(sink - m)
    inv = 1.0 / (jnp.sum(p, axis=-1, keepdims=True) + es)
    return p * inv, es * inv


def _attn_probs(q_ref, kvc_ref, kvp_ref, sink_ref, geometry):
    use_cur, dist, valid = geometry
    groups = range(N_HEADS // HEADS_PER_KV)
    kbd = [_block_diag(kvc_ref[:, 0:D_KV], kvp_ref[:, 0:D_KV], g) for g in groups]
    vbd = [_block_diag(kvc_ref[:, D_KV:2 * D_KV], kvp_ref[:, D_KV:2 * D_KV], g) for g in groups]
    qps = [(q_ref[:, j * 128:(j + 1) * 128] * SCALE).astype(BF16) for j in range(N_HEADS // 2)]
    scores = []
    for j, qp in enumerate(qps):
        scores += _merge(_nt(qp, kbd[j // 4]), use_cur)
    sinks = [sink_ref[0, h] for h in range(N_HEADS)]
    scores = [jnp.where(valid, s - SLOPES[h] * dist, -jnp.inf) for h, s in enumerate(scores)]
    maxes = [jnp.maximum(jnp.max(s, axis=-1, keepdims=True), sinks[h]) for h, s in enumerate(scores)]
    exps = [jnp.exp(s - m) for s, m in zip(scores, maxes)]
    sink_exps = [jnp.exp(sinks[h] - m) for h, m in enumerate(maxes)]
    invs = [1.0 / (jnp.sum(e, axis=-1, keepdims=True) + se) for e, se in zip(exps, sink_exps)]
    probs = [e * inv for e, inv in zip(exps, invs)]
    sink_probs = [se * inv for se, inv in zip(sink_exps, invs)]
    return qps, kbd, vbd, probs, sink_probs


def _kv_specs(n_blocks, reverse):
    def blk(i):
        return (n_blocks - 1 - i) if reverse else i
    cur = pl.BlockSpec((BLK, 2 * D_KV), lambda i: (blk(i), 0))
    prev = pl.BlockSpec((BLK, 2 * D_KV), lambda i: (jnp.maximum(blk(i) - 1, 0), 0))
    return cur, prev


def _attn_fwd(q, kv, ga, sinks, norm_attn_out):
    seq = q.shape[0]
    n_blocks = seq // BLK

    def body(q_ref, kvc_ref, kvp_ref, ga_ref, sink_ref, gn_ref, ya_ref, oa_ref):
        geometry = _band_geometry(pl.program_id(0))
        _, _, vbd, probs, _ = _attn_probs(q_ref, kvc_ref, kvp_ref, sink_ref, geometry)
        p4s = [_split(probs[2 * j], probs[2 * j + 1], geometry[0]).astype(BF16) for j in range(N_HEADS // 2)]
        ya = jnp.concatenate([_nn(p4, vbd[j // 4]) for j, p4 in enumerate(p4s)], axis=1)
        ya_ref[...] = ya
        silu, _ = _silu_and_grad(ga_ref[...])
        oa_ref[...] = (ya * _rstd(ya) * gn_ref[...] * silu).astype(BF16)

    row = pl.BlockSpec((BLK, D_ATTN), lambda i: (i, 0))
    kv_cur, kv_prev = _kv_specs(n_blocks, reverse=False)
    return pl.pallas_call(
        body, name="attn_fwd", grid=(n_blocks,),
        out_shape=(pltpu.HBM((seq, D_ATTN), F32), pltpu.HBM((seq, D_ATTN), BF16)),
        in_specs=[row, kv_cur, kv_prev, row, pl.BlockSpec(memory_space=pltpu.SMEM), _resident((1, D_ATTN))],
        out_specs=(row, row),
        compiler_params=_params(32, ("arbitrary",)),
    )(q, kv, kv, ga, sinks, norm_attn_out)


def _out_proj_loss(x, oc, oa, wo, norm_final, target, pc, conv_w, norm_conv_out, wt, norm_in):
    seq = x.shape[0]
    tile = TOK_TILE
    n_tiles = seq // tile

    def body(x_ref, oc_ref, oa_ref, wo_ref, gf_ref, t_ref, pc_ref, hcc_ref, hcu_ref, cw_ref, gn_ref, wtc_ref, gin_ref,
             dx2_ref, doa_ref, gwo_ref, dpc_ref, small_ref, loss_acc, zbuf, dbuf):
        step = pl.program_id(0)

        def small_add(row, value):
            small_ref[row:row + 1, :] += jnp.sum(value, axis=0, keepdims=True)

        @pl.when(step == 0)
        def _():
            gwo_ref[...] = jnp.zeros_like(gwo_ref)
            small_ref[...] = jnp.zeros_like(small_ref)
            loss_acc[...] = jnp.zeros_like(loss_acc)
            dbuf[tile:tile + 8, :] = jnp.zeros((8, D_CONV), F32)

        oc, oa = oc_ref[...], oa_ref[...]
        x2 = x_ref[...] + _nn(oc, wo_ref[0:D_CONV, :]) + _nn(oa, wo_ref[D_CONV:D_MIX, :])
        r = _rstd(x2)
        xhat = x2 * r
        err = xhat * gf_ref[...] - t_ref[...]
        loss_acc[...] += jnp.sum(err * err, axis=0, keepdims=True) * (0.5 / D_MODEL)
        dy = err * (1.0 / D_MODEL)
        small_add(SMALL_NORM_FINAL, dy * xhat)
        dx2 = _rms_bwd(dy * gf_ref[...], xhat, r)
        db = dx2.astype(BF16)
        do = _nt(db, wo_ref[0:D_CONV, :])
        doa_ref[...] = _nt(db, wo_ref[D_CONV:D_MIX, :])
        gwo_ref[0:D_CONV, :] += _tn(oc, db)
        gwo_ref[D_CONV:D_MIX, :] += _tn(oa, db)

        is_first_tile = step == n_tiles - 1
        zbuf[0:8, :] = jnp.where(is_first_tile, 0.0, hcc_ref[...] * hcu_ref[...])
        cb, cc, cu, z, z1, z2, conv = _conv_core(pc_ref, zbuf, cw_ref)
        silu, dsilu = _silu_and_grad(pc_ref[:, 3 * D_CONV:4 * D_CONV])
        yc = cb * conv
        rc = _rstd(yc)
        chat = yc * rc
        dn = do * silu
        dpc_ref[:, 3 * D_CONV:4 * D_CONV] = (do * (chat * gn_ref[...]) * dsilu).astype(BF16)
        small_add(SMALL_NORM_CONV, dn * chat)
        dyc = _rms_bwd(dn * gn_ref[...], chat, rc)
        dpc_ref[:, 0:D_CONV] = (dyc * conv).astype(BF16)
        dconv = dyc * cb
        small_add(SMALL_CONV_W, dconv * z2)
        small_add(SMALL_CONV_W + 1, dconv * z1)
        small_add(SMALL_CONV_W + 2, dconv * z)
        dbuf[0:tile, :] = dconv
        dz = cw_ref[2:3, :] * dconv + cw_ref[1:2, :] * dbuf[1:tile + 1, :] + cw_ref[0:1, :] * dbuf[2:tile + 2, :]
        dpc_ref[:, D_CONV:2 * D_CONV] = (dz * cu).astype(BF16)
        dpc_ref[:, 2 * D_CONV:3 * D_CONV] = (dz * cc).astype(BF16)
        dbuf[tile:tile + 8, :] = dbuf[0:8, :]

        dh = _nn(dpc_ref[...], wtc_ref[...])
        x_in = x_ref[...]
        r_in = _rstd(x_in)
        xhat_in = x_in * r_in
        small_add(SMALL_NORM_IN, dh * xhat_in)
        dx2_ref[...] = dx2 + _rms_bwd(dh * gin_ref[...], xhat_in, r_in)

        @pl.when(step == n_tiles - 1)
        def _():
            lane = lax.broadcasted_iota(jnp.int32, (1, D_MODEL), 1)
            small_ref[SMALL_MISC:SMALL_MISC + 1, :] = jnp.where(lane == SMALL_LOSS_LANE, jnp.sum(loss_acc[...]), 0.0)

    def rev(i):
        return n_tiles - 1 - i

    def row(width):
        return pl.BlockSpec((tile, width), lambda i: (rev(i), 0))

    def halo(col_block):
        return pl.BlockSpec((8, D_CONV), lambda i: (jnp.maximum(rev(i) * (tile // 8) - 1, 0), col_block))

    def const(shape):
        return pl.BlockSpec(shape, lambda i: (0, 0))

    return pl.pallas_call(
        body, name="out_proj_loss", grid=(n_tiles,),
        out_shape=(jax.ShapeDtypeStruct((seq, D_MODEL), F32), jax.ShapeDtypeStruct((seq, D_ATTN), F32),
                   jax.ShapeDtypeStruct((D_MIX, D_MODEL), F32), jax.ShapeDtypeStruct((seq, D_PC), BF16),
                   jax.ShapeDtypeStruct((SMALL_ROWS, D_MODEL), F32)),
        in_specs=[row(D_MODEL), row(D_CONV), row(D_ATTN), _resident(wo.shape), _resident((1, D_MODEL)), row(D_MODEL),
                  row(D_PC), halo(1), halo(2), _resident(conv_w.shape), _resident((1, D_CONV)),
                  _resident((D_PC, D_MODEL)), _resident((1, D_MODEL))],
        out_specs=(row(D_MODEL), row(D_ATTN), const((D_MIX, D_MODEL)), row(D_PC), const((SMALL_ROWS, D_MODEL))),
        scratch_shapes=[pltpu.VMEM((1, D_MODEL), F32), pltpu.VMEM((tile + 8, D_CONV), F32),
                        pltpu.VMEM((tile + 8, D_CONV), F32)],
        compiler_params=_params(62, ("arbitrary",)),
    )(x, oc, oa, wo, norm_final, target, pc, pc, pc, conv_w, norm_conv_out, wt, norm_in)


def _attn_bwd(q, kv, ga, ya, doa, sinks, norm_attn_out, gwo, small, wt_tail, x, dx2, norm_in):
    seq = q.shape[0]
    n_blocks = seq // BLK
    stage_steps = (0, n_blocks // 4, n_blocks // 2, (3 * n_blocks) // 4, n_blocks - 1)

    def body(q_ref, kvc_ref, kvp_ref, ga_ref, ya_ref, doa_ref, sink_ref, gn_ref, gwo_ref, small_ref,
             wtt_ref, x_ref, dx2_ref, gin_ref,
             dqg_ref, small_out, gwo_sh, gx_ref, gna_ref, gs_ref, gni_ref, carry, dya_buf, *rs_scratch):
        step = pl.program_id(0)

        @pl.when(step == 0)
        def _():
            gna_ref[...] = jnp.zeros_like(gna_ref)
            gs_ref[...] = jnp.zeros_like(gs_ref)
            gni_ref[...] = jnp.zeros_like(gni_ref)
            carry[...] = jnp.zeros_like(carry)

        ya = ya_ref[...]
        r = _rstd(ya)
        xhat = ya * r
        silu, dsilu = _silu_and_grad(ga_ref[...])
        do = doa_ref[...]
        dn = do * silu
        dqg_ref[:, D_ATTN:2 * D_ATTN] = (do * (xhat * gn_ref[...]) * dsilu).astype(BF16)
        gna_ref[...] += jnp.sum(dn * xhat, axis=0, keepdims=True)
        dya_buf[...] = _rms_bwd(dn * gn_ref[...], xhat, r).astype(BF16)

        geometry = _band_geometry(n_blocks - 1 - step)
        use_cur = geometry[0]
        lane = lax.broadcasted_iota(jnp.int32, (BLK, 128), 1)
        sink_lane = lax.broadcasted_iota(jnp.int32, (1, 128), 1)
        gsink = jnp.zeros((1, 128), F32)

        def fold(bd):
            return (jnp.where(lane < 64, bd[0:BLK], 0.0) + jnp.where(lane >= 64, bd[BLK:2 * BLK], 0.0),
                    jnp.where(lane < 64, bd[2 * BLK:3 * BLK], 0.0) + jnp.where(lane >= 64, bd[3 * BLK:4 * BLK], 0.0))

        pairs = range(N_HEADS // 2)
        qps, kbd, vbd, probs, sink_probs = _attn_probs(q_ref, kvc_ref, kvp_ref, sink_ref, geometry)
        dyps = [dya_buf[:, j * 128:(j + 1) * 128] for j in pairs]
        dps = []
        for j in pairs:
            dps += _merge(_nt(dyps[j], vbd[j // 4]), use_cur)
        deltas = [jnp.sum(p * dp, axis=-1, keepdims=True) for p, dp in zip(probs, dps)]
        dss = [p * (dp - delta) for p, dp, delta in zip(probs, dps, deltas)]
        for h in range(N_HEADS):
            dsink = -jnp.sum(sink_probs[h] * deltas[h], axis=0, keepdims=True)
            gsink = gsink + jnp.where(sink_lane == h, dsink, 0.0)
        gs_ref[...] += gsink
        ds4s = [_split(dss[2 * j], dss[2 * j + 1], use_cur).astype(BF16) for j in pairs]
        p4s = [_split(probs[2 * j], probs[2 * j + 1], use_cur).astype(BF16) for j in pairs]
        dqg_ref[:, 0:D_ATTN] = jnp.concatenate([_nn(ds4s[j], kbd[j // 4]) * SCALE for j in pairs], axis=1).astype(BF16)
        sums = []
        for group in range(N_HEADS // HEADS_PER_KV):
            acc = [jnp.zeros((BLK, 128), F32) for _ in range(4)]
            for j in range(group * 4, group * 4 + 4):
                for slot, part in enumerate(fold(_tn(ds4s[j], qps[j])) + fold(_tn(p4s[j], dyps[j]))):
                    acc[slot] = acc[slot] + part
            sums.append([a + pltpu.roll(a, 64, 1) for a in acc])
        dk_cur, dk_prev, dv_cur, dv_prev = (jnp.where(lane < 64, a, b) for a, b in zip(sums[0], sums[1]))
        dqg_ref[:, 2 * D_ATTN:2 * D_ATTN + 2 * D_KV] = (jnp.concatenate([dk_cur, dv_cur], axis=1) + carry[...]).astype(BF16)
        carry[...] = jnp.concatenate([dk_prev, dv_prev], axis=1)

        dh = (_nn(dqg_ref[:, 0:D_ATTN], wtt_ref[0:D_ATTN, :])
              + _nn(dqg_ref[:, D_ATTN:2 * D_ATTN], wtt_ref[D_ATTN + 2 * D_KV:D_QG, :])
              + _nn(dqg_ref[:, 2 * D_ATTN:D_QG], wtt_ref[D_ATTN:D_ATTN + 2 * D_KV, :]))
        x_in = x_ref[...]
        r_in = _rstd(x_in)
        xhat_in = x_in * r_in
        gni_ref[...] += jnp.sum(dh * xhat_in, axis=0, keepdims=True)
        gx_ref[...] = dx2_ref[...] + _rms_bwd(dh * gin_ref[...], xhat_in, r_in)

        @pl.when(step == n_blocks - 1)
        def _():
            small_out[...] = small_ref[...]
            small_out[SMALL_NORM_IN:SMALL_NORM_IN + 1, :] = small_ref[SMALL_NORM_IN:SMALL_NORM_IN + 1, :] + gni_ref[...]
            small_out[SMALL_NORM_ATTN:SMALL_NORM_ATTN + 1, :] = gna_ref[...]
            small_out[SMALL_MISC:SMALL_MISC + 1, 0:128] = small_ref[SMALL_MISC:SMALL_MISC + 1, 0:128] + gs_ref[...]

        for at, stage in zip(stage_steps, _rs_wout_stages(gwo_ref, gwo_sh, *rs_scratch)):
            pl.when(step == at)(stage)

    row = pl.BlockSpec((BLK, D_ATTN), lambda i: (n_blocks - 1 - i, 0))
    kv_cur, kv_prev = _kv_specs(n_blocks, reverse=True)
    any_spec = pl.BlockSpec(memory_space=pl.ANY)
    return pl.pallas_call(
        body, name="attn_bwd", grid=(n_blocks,),
        out_shape=(jax.ShapeDtypeStruct((seq, D_QG), BF16), jax.ShapeDtypeStruct(small.shape, F32),
                   jax.ShapeDtypeStruct((gwo.shape[0] // N_CHIPS, gwo.shape[1]), F32),
                   jax.ShapeDtypeStruct((seq, D_MODEL), F32)),
        in_specs=[row, kv_cur, kv_prev, row, row, row, pl.BlockSpec(memory_space=pltpu.SMEM), _resident((1, D_ATTN)),
                  any_spec, _resident(small.shape), _resident(wt_tail.shape), row, row, _resident((1, D_MODEL))],
        out_specs=(pl.BlockSpec((BLK, D_QG), lambda i: (n_blocks - 1 - i, 0)),
                   pl.BlockSpec(small.shape, lambda i: (0, 0)), any_spec, row),
        scratch_shapes=[pltpu.VMEM((1, D_ATTN), F32), pltpu.VMEM((1, 128), F32), pltpu.VMEM((1, D_MODEL), F32),
                        pltpu.VMEM((BLK, 2 * D_KV), F32), pltpu.VMEM((BLK, D_ATTN), BF16)] + _rs_wout_scratch(gwo.shape),
        compiler_params=_params(52, ("arbitrary",)),
    )(q, kv, kv, ga, ya, doa, sinks, norm_attn_out, gwo, small, wt_tail, x, dx2, norm_in)


GBLK = 256
GSUB = 64
PAIR_RING = 4
LAG_PAIR, LAG_HOP1, LAG_HOP2 = 1, 7, 14


def _bwd_in(dpc, dqg, h, small):
    seq = h.shape[0]
    n_blk = D_IN_PROJ // GBLK
    per_chip = n_blk // N_CHIPS
    n_slots = (n_blk + 1) // 2
    n_sub = GBLK // GSUB
    chip_rows = D_IN_PROJ // N_CHIPS
    n_steps = n_blk + LAG_HOP2
    blk_q, blk_kv, blk_ga = ROW_Q // GBLK, ROW_KV // GBLK, ROW_GA // GBLK

    def block_of(i):
        k = i % N_CHIPS
        robin = per_chip * ((k % 2) * 2 + k // 2) + i // N_CHIPS
        if isinstance(i, int):
            return robin if i < per_chip * N_CHIPS else i
        return jnp.where(i < per_chip * N_CHIPS, robin, i)

    def owner_of(i):
        return (i // N_CHIPS) % 2

    def slot_of(i):
        return (i // (2 * N_CHIPS)) * N_CHIPS + i % N_CHIPS

    def body(dpc_ref, dqg_ref, h_ref, small_ref, small_sum, gwt_sh,
             keep, pbuf, xbuf, land, land2, small_land,
             pair_send, pair_recv, h1_send, h1_recv, h2_send, h2_recv, sw_send, sw_recv, sm_send, sm_recv, out_sem):
        step = pl.program_id(0)
        x_i, y_i, c = lax.axis_index("x"), lax.axis_index("y"), lax.axis_index("c")
        me = 4 * x_i + 2 * y_i + c
        j = 2 * x_i + y_i
        pa = (_xor(x_i, 1 - c), _xor(y_i, c), c)
        pb = (_xor(x_i, c), _xor(y_i, 1 - c), c)
        sib = (x_i, y_i, 1 - c)
        ja = 2 * pa[0] + pa[1]
        jb = 2 * pb[0] + pb[1]
        jd = 3 - j

        def remote(src, dst, send, recv, to):
            return pltpu.make_async_remote_copy(src_ref=src, dst_ref=dst, send_sem=send, recv_sem=recv,
                                                device_id=to, device_id_type=MESH)

        def piece(ref, slot, u, n):
            return ref.at[slot, pl.ds(u * GSUB, n * GSUB), :]

        def chip_rows_at(ref, local, n):
            return ref.at[pl.ds(pl.multiple_of(local, GSUB), n * GSUB), :]

        def pair_copy(i):
            slot = slot_of(i)
            return remote(pbuf.at[i % PAIR_RING], land.at[slot], pair_send.at[slot], pair_recv.at[slot], sib)

        def h1_copy(slot, u, n):
            k = slot * n_sub + u
            return remote(piece(xbuf, slot, u, n), piece(xbuf, slot, u, n), h1_send.at[k], h1_recv.at[k], pa)

        def h2_copy(slot, u, n, local):
            k = slot * n_sub + u
            return remote(piece(xbuf, slot, u, n), chip_rows_at(land2, local, n), h2_send.at[k], h2_recv.at[k], pb)

        def sw_copy(slot, u, n, local):
            k = slot * n_sub + u
            return remote(piece(keep, slot, u, n), chip_rows_at(gwt_sh, local, n), sw_send.at[k], sw_recv.at[k], sib)

        def owned(i):
            return (i >= 0) & (i < n_blk) & (owner_of(i) == c)

        def chip_of(blk, u):
            row = blk * GBLK + u * GSUB
            chip = row // chip_rows
            return chip, row - chip * chip_rows

        def pieces(blk):
            first, local = chip_of(blk, 0)
            whole = first == chip_of(blk, n_sub - 1)[0]
            if isinstance(blk, int):
                return [(True, 0, n_sub, first, local)] if whole else [(True, u, 1) + chip_of(blk, u) for u in range(n_sub)]
            return [(whole, 0, n_sub, first, local)] + [(jnp.logical_not(whole), u, 1) + chip_of(blk, u) for u in range(n_sub)]

        @pl.when(step < n_blk)
        def _():
            from_pc = block_of(step) < blk_q
            block = _tn(jnp.where(from_pc, dpc_ref[...], dqg_ref[...]), h_ref[...])

            @pl.when(owner_of(step) == c)
            def _():
                keep[slot_of(step)] = block

            @pl.when(owner_of(step) != c)
            def _():
                @pl.when(step >= 2 * PAIR_RING)
                def _():
                    pair_copy(step - 2 * PAIR_RING).wait_send()
                pbuf[step % PAIR_RING] = block.astype(BF16)
                pair_copy(step).start()

        i1 = step - LAG_PAIR

        @pl.when(owned(i1))
        def _():
            slot = slot_of(i1)
            pair_copy(i1).wait_recv()
            _accumulate(keep.at[slot], land.at[slot])
            for cond, u, n, chip, _ in pieces(block_of(i1)):
                @pl.when(cond & ((chip == ja) | (chip == jd)))
                def _(u=u, n=n):
                    _cast_rows(piece(keep, slot, u, n), piece(xbuf, slot, u, n))
                    h1_copy(slot, u, n).start()

        i2 = step - LAG_HOP1

        @pl.when(owned(i2))
        def _():
            slot = slot_of(i2)
            for cond, u, n, chip, local in pieces(block_of(i2)):
                @pl.when(cond & ((chip == j) | (chip == jb)))
                def _(u=u, n=n, chip=chip, local=local):
                    h1_copy(slot, u, n).wait_recv()
                    _accumulate(piece(keep, slot, u, n), piece(xbuf, slot, u, n))

                    @pl.when(chip == jb)
                    def _():
                        _cast_rows(piece(keep, slot, u, n), piece(xbuf, slot, u, n))
                        h2_copy(slot, u, n, local).start()

                @pl.when(cond & ((chip == ja) | (chip == jd)))
                def _(u=u, n=n):
                    h1_copy(slot, u, n).wait_send()

        i3 = step - LAG_HOP2

        @pl.when(owned(i3))
        def _():
            slot = slot_of(i3)
            for cond, u, n, chip, local in pieces(block_of(i3)):
                @pl.when(cond & (chip == j))
                def _(u=u, n=n, local=local):
                    h2_copy(slot, u, n, local).wait_recv()
                    _accumulate(piece(keep, slot, u, n), chip_rows_at(land2, local, n))
                    mine = pltpu.make_async_copy(piece(keep, slot, u, n), chip_rows_at(gwt_sh, local, n), out_sem.at[0])
                    mine.start()
                    sw_copy(slot, u, n, local).start()
                    mine.wait()

                @pl.when(cond & (chip == jb))
                def _(u=u, n=n, local=local):
                    h2_copy(slot, u, n, local).wait_send()

        @pl.when(step == n_steps - 1)
        def _():
            small_land[me] = small_ref[...]
            others = [(dx, dy, dc) for dx in (0, 1) for dy in (0, 1) for dc in (0, 1)][1:]
            sends = [remote(small_land.at[me], small_land.at[me], sm_send.at[k], sm_recv.at[k],
                            (_xor(x_i, dx), _xor(y_i, dy), _xor(c, dc))) for k, (dx, dy, dc) in enumerate(others)]
            for cp in sends:
                cp.start()
            for i in range(n_blk):
                if i + 2 * PAIR_RING >= n_blk:
                    @pl.when(owner_of(i) != c)
                    def _(i=i):
                        pair_copy(i).wait_send()
                for _, u, n, chip, local in pieces(block_of(i)):
                    @pl.when((j == chip) & (c == owner_of(i)))
                    def _(i=i, u=u, n=n, local=local):
                        sw_copy(slot_of(i), u, n, local).wait_send()

                    @pl.when((j == chip) & (c != owner_of(i)))
                    def _(i=i, u=u, n=n, local=local):
                        sw_copy(slot_of(i), u, n, local).wait_recv()
            for cp in sends:
                cp.wait_recv()
            total = small_land[0]
            for dev in range(1, 8):
                total = total + small_land[dev]
            small_sum[...] = total
            for cp in sends:
                cp.wait_send()

    last_pc_step = max(i for i in range(n_blk) if block_of(i) < blk_q)

    def next_block(i, in_pc):
        i = jnp.clip(i, 0, n_blk - 1)
        step = jnp.full_like(i, last_pc_step if in_pc else n_blk - 1)
        for ahead in reversed(range(N_CHIPS)):
            cand = jnp.minimum(i + ahead, n_blk - 1)
            step = jnp.where((block_of(cand) < blk_q) == in_pc, cand, step)
        return block_of(step)

    def dqg_block(i):
        b = next_block(i, False)
        q_blk = jnp.clip(b - blk_q, 0, blk_kv - blk_q - 1)
        ga_blk = (D_ATTN // GBLK) + jnp.clip(b - blk_ga, 0, n_blk - blk_ga - 1)
        return jnp.where(b < blk_kv, q_blk, jnp.where(b == blk_kv, 2 * D_ATTN // GBLK, ga_blk))

    n_piece = n_slots * n_sub
    dma = pltpu.SemaphoreType.DMA
    return pl.pallas_call(
        body, name="bwd_in", grid=(n_steps,),
        out_shape=(jax.ShapeDtypeStruct(small.shape, F32), jax.ShapeDtypeStruct((chip_rows, D_MODEL), F32)),
        in_specs=[pl.BlockSpec((seq, GBLK), lambda i: (0, next_block(i, True))),
                  pl.BlockSpec((seq, GBLK), lambda i: (0, dqg_block(i))),
                  _resident(h.shape), _resident(small.shape)],
        out_specs=(pl.BlockSpec(small.shape, lambda i: (0, 0)), pl.BlockSpec(memory_space=pl.ANY)),
        scratch_shapes=[pltpu.VMEM((n_slots, GBLK, D_MODEL), F32), pltpu.VMEM((PAIR_RING, GBLK, D_MODEL), BF16),
                        pltpu.VMEM((n_slots, GBLK, D_MODEL), BF16), pltpu.VMEM((n_slots, GBLK, D_MODEL), BF16),
                        pltpu.VMEM((chip_rows, D_MODEL), BF16), pltpu.VMEM((8,) + small.shape, F32),
                        dma((n_slots,)), dma((n_slots,)), dma((n_piece,)), dma((n_piece,)), dma((n_piece,)),
                        dma((n_piece,)), dma((n_piece,)), dma((n_piece,)), dma((7,)), dma((7,)), dma((1,))],
        compiler_params=_params(52, ("arbitrary",)),
    )(dpc, dqg, h, small)


def _accumulate(dst_ref, src_ref, rows=16):
    def step(i, carry):
        sl = pl.ds(pl.multiple_of(i * rows, rows), rows)
        dst_ref[sl, :] = dst_ref[sl, :] + src_ref[sl, :].astype(F32)
        return carry
    lax.fori_loop(0, dst_ref.shape[0] // rows, step, 0)


def _rs_wout_scratch(gwo_shape):
    o_half, width = gwo_shape[0] // N_CHIPS // 2, gwo_shape[1]
    return [pltpu.VMEM((4, o_half, width), F32), pltpu.VMEM((4, o_half, width), BF16),
            pltpu.VMEM((4, o_half, width), BF16), pltpu.VMEM((2, o_half, width), BF16),
            pltpu.VMEM((o_half, width), BF16),
            pltpu.SemaphoreType.DMA((8,)), pltpu.SemaphoreType.DMA((8,)), pltpu.SemaphoreType.DMA((4,))]


def _rs_wout_stages(gwo_ref, gwo_sh, acc_o, sb_o, r1o, r2o, r3o, send_sems, recv_sems, local_sems):
    o_rows = gwo_ref.shape[0] // N_CHIPS
    o_half = o_rows // 2
    x, y, c = lax.axis_index("x"), lax.axis_index("y"), lax.axis_index("c")
    j = 2 * x + y
    pa = (_xor(x, 1 - c), _xor(y, c), c)
    pb = (_xor(x, c), _xor(y, 1 - c), c)
    sib = (x, y, 1 - c)
    ja = 2 * pa[0] + pa[1]
    jb = 2 * pb[0] + pb[1]
    jd = 3 - j
    order = (ja, jd, jb, j)
    sib_order = (jb, jd, ja, j)

    def rcopy(k, src, dst, to):
        return pltpu.make_async_remote_copy(src_ref=src, dst_ref=dst, send_sem=send_sems.at[k],
                                            recv_sem=recv_sems.at[k], device_id=to, device_id_type=MESH)

    def o_rows_of(chip, half):
        return gwo_ref.at[pl.ds(pl.multiple_of(chip * o_rows + half * o_half, 8), o_half), :]

    def load_all(chips, half):
        cps = [pltpu.make_async_copy(o_rows_of(chip, half), acc_o.at[s], local_sems.at[s]) for s, chip in enumerate(chips)]
        for cp in cps:
            cp.start()
        return cps

    def pair_send(s):
        return rcopy(s, sb_o.at[s], r1o.at[s], sib)

    def out_half(half):
        return gwo_sh.at[pl.ds(pl.multiple_of(half * o_half, 8), o_half), :]

    hop1 = [rcopy(4 + s, sb_o.at[s], r2o.at[s], pa) for s in range(2)]
    hop2 = rcopy(6, sb_o.at[2], r3o, pb)
    swap = rcopy(7, acc_o.at[3], out_half(c), sib)
    mine = pltpu.make_async_copy(acc_o.at[3], out_half(c), local_sems.at[0])

    def resend(s, copy):
        pair_send(s).wait_send()
        _cast_rows(acc_o.at[s], sb_o.at[s])
        copy.start()

    def stage_pair():
        for s, cp in enumerate(load_all(sib_order, 1 - c)):
            cp.wait()
            _cast_rows(acc_o.at[s], sb_o.at[s])
            pair_send(s).start()

    def stage_hop1():
        for s, cp in enumerate(load_all(order, c)):
            cp.wait()
            pair_send(s).wait_recv()
            _accumulate(acc_o.at[s], r1o.at[s])
            if s < 2:
                resend(s, hop1[s])

    def stage_hop2():
        hop1[1].wait_recv()
        _accumulate(acc_o.at[2], r2o.at[1])
        resend(2, hop2)
        hop1[0].wait_recv()
        _accumulate(acc_o.at[3], r2o.at[0])

    def stage_final():
        hop2.wait_recv()
        _accumulate(acc_o.at[3], r3o)
        mine.start()
        swap.start()

    def stage_drain():
        rcopy(7, acc_o.at[3], out_half(1 - c), sib).wait_recv()
        for cp in [pair_send(3)] + hop1 + [hop2, swap]:
            cp.wait_send()
        mine.wait()

    return stage_pair, stage_hop1, stage_hop2, stage_final, stage_drain


def _adamw(name, w, g, m, v, rows):
    def body(w_ref, g_ref, m_ref, v_ref, go_ref, d_ref, nm_ref, nv_ref):
        _adamw_update(w_ref, g_ref, m_ref, v_ref, go_ref, d_ref, nm_ref, nv_ref)

    spec = pl.BlockSpec((rows, w.shape[1]), lambda i: (i, 0))
    shape = jax.ShapeDtypeStruct(w.shape, F32)
    return pl.pallas_call(
        body, name="adamw_" + name, grid=(w.shape[0] // rows,),
        out_shape=(shape,) * 4, in_specs=[spec] * 4, out_specs=(spec,) * 4,
        compiler_params=_params(32, ("arbitrary",)),
    )(w, g, m, v)


def _adamw_update(w_ref, g_ref, m_ref, v_ref, go_ref, d_ref, nm_ref, nv_ref):
    gv = g_ref[...]
    go_ref[...] = gv
    nm = ADAM_B1 * m_ref[...] + (1.0 - ADAM_B1) * gv
    nv = ADAM_B2 * v_ref[...] + (1.0 - ADAM_B2) * (gv * gv)
    m_hat = nm / (1.0 - ADAM_B1 ** ADAM_STEP)
    v_hat = nv / (1.0 - ADAM_B2 ** ADAM_STEP)
    d_ref[...] = -ADAM_LR * (m_hat / (jnp.sqrt(v_hat) + ADAM_EPS) + ADAM_WD * w_ref[...])
    nm_ref[...] = nm
    nv_ref[...] = nv


def _adamw_small(weights, grads, ms, vs):
    n = len(weights)

    def body(*refs):
        ins, outs = refs[:4 * n], refs[4 * n:]
        for k in range(n):
            _adamw_update(*ins[4 * k:4 * k + 4], *outs[4 * k:4 * k + 4])

    flat = [a for group in zip(weights, grads, ms, vs) for a in group]
    vmem = pl.BlockSpec(memory_space=pltpu.VMEM)
    out = pl.pallas_call(
        body, name="adamw_small",
        out_shape=tuple(jax.ShapeDtypeStruct(w.shape, F32) for w in weights for _ in range(4)),
        in_specs=[vmem] * (4 * n), out_specs=(vmem,) * (4 * n),
    )(*flat)
    return [tuple(out[4 * k:4 * k + 4]) for k in range(n)]


def kernel(x, norm_in, w_in, conv_w, attn_sinks, norm_conv_out, norm_attn_out, w_out, norm_final, loss_target, m_norm_in, m_w_in, m_conv_w, m_attn_sinks, m_norm_conv_out, m_norm_attn_out, m_w_out, m_norm_final, v_norm_in, v_w_in, v_conv_w, v_attn_sinks, v_norm_conv_out, v_norm_attn_out, v_w_out, v_norm_final):
    chip = 2 * lax.axis_index("x") + lax.axis_index("y")
    xs, target = x[0], loss_target[0]
    norm_final2 = norm_final.reshape(1, D_MODEL)
    w_in_t, m_w_in_t, v_w_in_t = w_in[0].T, m_w_in[0].T, v_w_in[0].T

    wt, cw4 = _ag_weights(w_in_t, conv_w[0])
    cw = jnp.transpose(cw4, (1, 0, 2)).reshape(3, D_CONV)

    h, pc, q, kv, ga, oc, wo = _fwd_in(xs, norm_in, wt, w_out[0], cw, norm_conv_out)
    ya, oa = _attn_fwd(q, kv, ga, attn_sinks, norm_attn_out)
    dx2, doa, gwo, dpc, small = _out_proj_loss(xs, oc, oa, wo, norm_final2, target, pc, cw, norm_conv_out, wt, norm_in)
    dqg, small, gwo_sh, grad_x = _attn_bwd(q, kv, ga, ya, doa, attn_sinks, norm_attn_out, gwo, small, wt[ROW_Q:], xs,
                                           dx2, norm_in)
    small_sum, gwt_sh = _bwd_in(dpc, dqg, h, small)

    loss = small_sum[SMALL_MISC, SMALL_LOSS_LANE]
    g_norm_in, g_norm_conv, g_norm_attn = (small_sum[r:r + 1] for r in (SMALL_NORM_IN, SMALL_NORM_CONV, SMALL_NORM_ATTN))
    g_norm_final = small_sum[SMALL_NORM_FINAL]
    g_conv_w = lax.dynamic_slice(small_sum[SMALL_CONV_W:SMALL_CONV_W + 3], (0, chip * (D_CONV // N_CHIPS)),
                                 (3, D_CONV // N_CHIPS))[None]
    g_sinks = small_sum[SMALL_MISC:SMALL_MISC + 1, 0:N_HEADS]

    def two_d(a):
        return a.reshape(-1, a.shape[-1])

    up_w_in = tuple(o.T[None] for o in _adamw("w_in", w_in_t, gwt_sh, m_w_in_t, v_w_in_t, 200))
    up_w_out = tuple(o[None] for o in _adamw("w_out", w_out[0], gwo_sh, m_w_out[0], v_w_out[0], 128))
    small_w = (norm_in, conv_w, attn_sinks, norm_conv_out, norm_attn_out, norm_final)
    small_g = (g_norm_in, g_conv_w, g_sinks, g_norm_conv, g_norm_attn, g_norm_final)
    small_m = (m_norm_in, m_conv_w, m_attn_sinks, m_norm_conv_out, m_norm_attn_out, m_norm_final)
    small_v = (v_norm_in, v_conv_w, v_attn_sinks, v_norm_conv_out, v_norm_attn_out, v_norm_final)
    up_small = _adamw_small(*(tuple(two_d(a) for a in group) for group in (small_w, small_g, small_m, small_v)))
    up_small = [tuple(o.reshape(w.shape) for o in up) for up, w in zip(up_small, small_w)]
    up_norm_in, up_conv_w, up_sinks, up_norm_conv, up_norm_attn, up_norm_final = up_small
    updates = (up_norm_in, up_w_in, up_conv_w, up_sinks, up_norm_conv, up_norm_attn, up_w_out, up_norm_final)
    grads_out, deltas, new_m, new_v = zip(*updates)
    return (loss, grad_x[None], *grads_out, *deltas, *new_m, *new_v)
```

```python
import jax
import jax.numpy as jnp
from jax import lax
from jax.experimental import pallas as pl
from jax.experimental.pallas import tpu as pltpu

F32 = jnp.float32
BF16 = jnp.bfloat16
MESH = pl.DeviceIdType.MESH

D_MODEL = 1024
D_CONV = 1024
D_ATTN = 1024
D_KV = 128
D_QG = 2 * D_ATTN + 2 * D_KV
D_MIX = D_CONV + D_ATTN
D_PC = 4 * D_CONV
D_IN_PROJ = D_PC + 2 * D_ATTN + 2 * D_KV
ROW_Q = D_PC
ROW_KV = ROW_Q + D_ATTN
ROW_GA = ROW_KV + 2 * D_KV
N_HEADS = 16
HEAD_DIM = 64
HEADS_PER_KV = 8
BLK = 128
N_CHIPS = 4
RMS_EPS = 1e-5
SCALE = HEAD_DIM ** -0.5
SLOPES = tuple(2.0 ** (-8.0 * (h + 1) / N_HEADS) for h in range(N_HEADS))

ADAM_LR, ADAM_B1, ADAM_B2, ADAM_EPS, ADAM_WD, ADAM_STEP = 0.001, 0.9, 0.999, 1e-08, 0.01, 10

SMALL_ROWS = 8
SMALL_NORM_IN, SMALL_NORM_CONV, SMALL_NORM_ATTN, SMALL_NORM_FINAL, SMALL_CONV_W, SMALL_MISC = 0, 1, 2, 3, 4, 7
SMALL_LOSS_LANE = N_HEADS

TOK_TILE = 256
MIB = 1 << 20


def _params(vmem_mib, semantics=None):
    return pltpu.CompilerParams(dimension_semantics=semantics, vmem_limit_bytes=vmem_mib * MIB)


def _nn(a, b):
    return jnp.dot(a, b, preferred_element_type=F32)


def _nt(a, b):
    return lax.dot_general(a, b, (((1,), (1,)), ((), ())), preferred_element_type=F32)


def _tn(a, b):
    return lax.dot_general(a, b, (((0,), (0,)), ((), ())), preferred_element_type=F32)


def _rstd(v):
    return lax.rsqrt(jnp.mean(v * v, axis=-1, keepdims=True) + RMS_EPS)


def _rms_bwd(g, xhat, rstd):
    return rstd * (g - xhat * jnp.mean(g * xhat, axis=-1, keepdims=True))


def _silu_and_grad(g):
    s = jax.nn.sigmoid(g)
    return g * s, s * (1.0 + g * (1.0 - s))


def _resident(shape):
    return pl.BlockSpec(shape, lambda *_: (0,) * len(shape), pipeline_mode=pl.Buffered(1))


def _xor(a, b):
    return a + b - 2 * a * b


def _cast_rows(src_ref, dst_ref, rows=32):
    def step(i, carry):
        sl = pl.ds(pl.multiple_of(i * rows, rows), rows)
        dst_ref[sl, :] = src_ref[sl, :].astype(dst_ref.dtype)
        return carry
    lax.fori_loop(0, src_ref.shape[0] // rows, step, 0)


def _ag_scratch(shard_shape):
    rows, width = shard_shape
    return [pltpu.VMEM((rows, width), F32), pltpu.VMEM((rows, width), BF16), pltpu.VMEM((3, rows // 2, width), BF16),
            pltpu.SemaphoreType.DMA((6,)), pltpu.SemaphoreType.DMA((6,)), pltpu.SemaphoreType.DMA((4,))]


def _ag_stages(sh_ref, out, f32_buf, own, land, send_sems, recv_sems, local_sems):
    rows = sh_ref.shape[0]
    half = rows // 2
    x, y, c = lax.axis_index("x"), lax.axis_index("y"), lax.axis_index("c")
    j = 2 * x + y
    p1 = (_xor(x, c), _xor(y, 1 - c), c)
    p2 = (_xor(x, 1 - c), _xor(y, c), c)
    sib = (x, y, 1 - c)
    j1 = 2 * p1[0] + p1[1]
    j2 = 2 * p2[0] + p2[1]
    j3 = 3 - j

    def rows_of(chip, hf):
        return out.at[pl.ds(pl.multiple_of(chip * rows + hf * half, 16), half), :]

    def rcopy(k, src, dst, to):
        return pltpu.make_async_remote_copy(src_ref=src, dst_ref=dst, send_sem=send_sems.at[k],
                                            recv_sem=recv_sems.at[k], device_id=to, device_id_type=MESH)

    my_half = own.at[pl.ds(pl.multiple_of(c * half, 16), half), :]
    hop1 = rcopy(0, my_half, land.at[0], p1)
    hop2_own = rcopy(1, my_half, land.at[1], p2)
    hop2_fwd = rcopy(2, land.at[0], land.at[2], p2)
    swaps = [rcopy(3 + s, land.at[s], rows_of(chip, c), sib) for s, chip in enumerate((j1, j2, j3))]
    keeps = [pltpu.make_async_copy(land.at[s], rows_of(chip, c), local_sems.at[1 + s]) for s, chip in enumerate((j1, j2, j3))]
    load = pltpu.make_async_copy(sh_ref, f32_buf, local_sems.at[0])
    own_out = pltpu.make_async_copy(own, out.at[pl.ds(pl.multiple_of(j * rows, 16), rows), :], local_sems.at[0])

    def stage_send():
        load.start()
        load.wait()
        _cast_rows(f32_buf, own)
        own_out.start()
        hop1.start()

    def stage_forward():
        hop1.wait_recv()
        hop2_own.start()
        hop2_fwd.start()
        swaps[0].start()
        keeps[0].start()

    def stage_publish():
        hop2_own.wait_recv()
        swaps[1].start()
        keeps[1].start()
        hop2_fwd.wait_recv()
        swaps[2].start()
        keeps[2].start()

    def stage_drain():
        for s, chip in enumerate((j2, j1, j3)):
            rcopy(3 + s, my_half, rows_of(chip, 1 - c), sib).wait_recv()
        for cp in [hop1, hop2_own, hop2_fwd] + swaps:
            cp.wait_send()
        for cp in [own_out] + keeps:
            cp.wait()

    return stage_send, stage_forward, stage_publish, stage_drain


def _ag_weights(wt_sh, cw_sh):
    def body(wt_ref, cw_ref, wt_out, cw_out, *scratch):
        cw_send, cw_recv, cw_local = scratch[-3:]
        x, y, c = lax.axis_index("x"), lax.axis_index("y"), lax.axis_index("c")
        j = 2 * x + y
        p1 = (_xor(x, c), _xor(y, 1 - c), c)
        p2 = (_xor(x, 1 - c), _xor(y, c), c)
        j1 = 2 * p1[0] + p1[1]

        def cw_copy(k, chip, to):
            src = cw_ref if k != 2 else cw_out.at[chip]
            return pltpu.make_async_remote_copy(src_ref=src, dst_ref=cw_out.at[chip], send_sem=cw_send.at[k],
                                                recv_sem=cw_recv.at[k], device_id=to, device_id_type=MESH)

        mine = pltpu.make_async_copy(cw_ref, cw_out.at[j], cw_local.at[0])
        mine.start()
        first = cw_copy(0, j, p1)
        first.start()
        stages = _ag_stages(wt_ref, wt_out, *scratch[:-3])
        stages[0]()
        first.wait_recv()
        second = [cw_copy(1, j, p2), cw_copy(2, j1, p2)]
        for cp in second:
            cp.start()
        for stage in stages[1:]:
            stage()
        for cp in second:
            cp.wait_recv()
        for cp in [first] + second:
            cp.wait_send()
        mine.wait()

    any_spec = pl.BlockSpec(memory_space=pl.ANY)
    dma = pltpu.SemaphoreType.DMA
    return pl.pallas_call(
        body, name="ag_weights",
        out_shape=(pltpu.HBM((N_CHIPS * wt_sh.shape[0], wt_sh.shape[1]), BF16),
                   jax.ShapeDtypeStruct((N_CHIPS,) + cw_sh.shape, cw_sh.dtype)),
        in_specs=[any_spec, any_spec], out_specs=(any_spec, any_spec),
        scratch_shapes=_ag_scratch(wt_sh.shape) + [dma((3,)), dma((3,)), dma((1,))],
        compiler_params=_params(32),
    )(wt_sh, cw_sh)


def _fwd_in(x, norm_in, wt, wo_sh, conv_w, norm_conv_out):
    seq = x.shape[0]
    tile = TOK_TILE
    n_tiles = seq // tile
    stage_steps = (0, n_tiles // 4, (5 * n_tiles) // 8, n_tiles - 1)

    def body(x_ref, g_ref, wt_ref, wo_ref, cw_ref, gn_ref, h_ref, pc_ref, q_ref, kv_ref, ga_ref, oc_ref, wo_out,
             zbuf, *ag_scratch):
        stages = _ag_stages(wo_ref, wo_out, *ag_scratch)
        for at, stage in zip(stage_steps[:-1], stages[:-1]):
            pl.when(pl.program_id(0) == at)(stage)

        @pl.when(pl.program_id(0) == 0)
        def _():
            zbuf[0:8, :] = jnp.zeros((8, D_CONV), F32)

        xv = x_ref[...]
        h = (xv * _rstd(xv) * g_ref[...]).astype(BF16)
        h_ref[...] = h
        for blk in range(D_PC // D_CONV):
            pc_ref[:, blk * D_CONV:(blk + 1) * D_CONV] = _nt(h, wt_ref[blk * D_CONV:(blk + 1) * D_CONV, :])
        q_ref[...] = _nt(h, wt_ref[ROW_Q:ROW_KV, :])
        kv_ref[...] = _nt(h, wt_ref[ROW_KV:ROW_GA, :])
        ga_ref[...] = _nt(h, wt_ref[ROW_GA:D_IN_PROJ, :])

        cb, _, _, _, _, _, conv = _conv_core(pc_ref, zbuf, cw_ref)
        yc = cb * conv
        silu, _ = _silu_and_grad(pc_ref[:, 3 * D_CONV:4 * D_CONV])
        oc_ref[...] = (yc * _rstd(yc) * gn_ref[...] * silu).astype(BF16)
        zbuf[0:8, :] = zbuf[tile:tile + 8, :]
        pl.when(pl.program_id(0) == stage_steps[-1])(stages[-1])

    def row(width):
        return pl.BlockSpec((tile, width), lambda i: (i, 0))

    any_spec = pl.BlockSpec(memory_space=pl.ANY)
    return pl.pallas_call(
        body, name="fwd_in", grid=(n_tiles,),
        out_shape=(jax.ShapeDtypeStruct((seq, D_MODEL), BF16), jax.ShapeDtypeStruct((seq, D_PC), F32),
                   jax.ShapeDtypeStruct((seq, D_ATTN), F32), jax.ShapeDtypeStruct((seq, 2 * D_KV), F32),
                   jax.ShapeDtypeStruct((seq, D_ATTN), F32), jax.ShapeDtypeStruct((seq, D_CONV), BF16),
                   jax.ShapeDtypeStruct((N_CHIPS * wo_sh.shape[0], wo_sh.shape[1]), BF16)),
        in_specs=[row(D_MODEL), _resident((1, D_MODEL)), _resident(wt.shape), any_spec,
                  _resident(conv_w.shape), _resident((1, D_CONV))],
        out_specs=(row(D_MODEL), row(D_PC), row(D_ATTN), row(2 * D_KV), row(D_ATTN), row(D_CONV), any_spec),
        scratch_shapes=[pltpu.VMEM((tile + 8, D_CONV), F32)] + _ag_scratch(wo_sh.shape),
        compiler_params=_params(56, ("arbitrary",)),
    )(x, norm_in, wt, wo_sh, conv_w, norm_conv_out)


def _conv_core(pc_ref, zbuf, cw_ref):
    tile = pc_ref.shape[0]
    cb = pc_ref[:, 0:D_CONV]
    cc = pc_ref[:, D_CONV:2 * D_CONV]
    cu = pc_ref[:, 2 * D_CONV:3 * D_CONV]
    z = cc * cu
    zbuf[8:tile + 8, :] = z
    z1 = zbuf[7:tile + 7, :]
    z2 = zbuf[6:tile + 6, :]
    conv = cw_ref[0:1, :] * z2 + cw_ref[1:2, :] * z1 + cw_ref[2:3, :] * z
    return cb, cc, cu, z, z1, z2, conv


def _band_geometry(block_index):
    qi = lax.broadcasted_iota(jnp.int32, (BLK, BLK), 0)
    kp = lax.broadcasted_iota(jnp.int32, (BLK, BLK), 1)
    use_cur = kp <= qi
    dist = jnp.where(use_cur, qi - kp, qi - kp + BLK).astype(F32)
    valid = use_cur | (block_index > 0)
    return use_cur, dist, valid


def _block_diag(cur, prev, group):
    lane = lax.broadcasted_iota(jnp.int32, cur.shape, 1)

    def halves(t):
        other = pltpu.roll(t, 64, 1)
        lo, hi = (t, other) if group == 0 else (other, t)
        return jnp.where(lane < 64, lo, 0.0), jnp.where(lane >= 64, hi, 0.0)

    return jnp.concatenate(halves(cur) + halves(prev), axis=0).astype(BF16)


def _merge(s4, use_cur):
    return (jnp.where(use_cur, s4[:, 0:BLK], s4[:, 2 * BLK:3 * BLK]),
            jnp.where(use_cur, s4[:, BLK:2 * BLK], s4[:, 3 * BLK:4 * BLK]))


def _split(a, b, use_cur):
    return jnp.concatenate([jnp.where(use_cur, a, 0.0), jnp.where(use_cur, b, 0.0),
                            jnp.where(use_cur, 0.0, a), jnp.where(use_cur, 0.0, b)], axis=1)


def _softmax_head(s, head, sink, dist, valid):
    sc = jnp.where(valid, s - SLOPES[head] * dist, -jnp.inf)
    m = jnp.maximum(jnp.max(sc, axis=-1, keepdims=True), sink)
    p = jnp.exp(sc - m)
    es = jnp.exp(sink - m)
    inv = 1.0 / (jnp.sum(p, axis=-1, keepdims=True) + es)
    return p * inv, es * inv


def _attn_probs(q_ref, kvc_ref, kvp_ref, sink_ref, geometry):
    use_cur, dist, valid = geometry
    groups = range(N_HEADS // HEADS_PER_KV)
    kbd = [_block_diag(kvc_ref[:, 0:D_KV], kvp_ref[:, 0:D_KV], g) for g in groups]
    vbd = [_block_diag(kvc_ref[:, D_KV:2 * D_KV], kvp_ref[:, D_KV:2 * D_KV], g) for g in groups]
    qps = [(q_ref[:, j * 128:(j + 1) * 128] * SCALE).astype(BF16) for j in range(N_HEADS // 2)]
    scores = []
    for j, qp in enumerate(qps):
        scores += _merge(_nt(qp, kbd[j // 4]), use_cur)
    sinks = [sink_ref[0, h] for h in range(N_HEADS)]
    scores = [jnp.where(valid, s - SLOPES[h] * dist, -jnp.inf) for h, s in enumerate(scores)]
    maxes = [jnp.maximum(jnp.max(s, axis=-1, keepdims=True), sinks[h]) for h, s in enumerate(scores)]
    exps = [jnp.exp(s - m) for s, m in zip(scores, maxes)]
    sink_exps = [jnp.exp(sinks[h] - m) for h, m in enumerate(maxes)]
    invs = [1.0 / (jnp.sum(e, axis=-1, keepdims=True) + se) for e, se in zip(exps, sink_exps)]
    probs = [e * inv for e, inv in zip(exps, invs)]
    sink_probs = [se * inv for se, inv in zip(sink_exps, invs)]
    return qps, kbd, vbd, probs, sink_probs


def _kv_specs(n_blocks, reverse):
    def blk(i):
        return (n_blocks - 1 - i) if reverse else i
    cur = pl.BlockSpec((BLK, 2 * D_KV), lambda i: (blk(i), 0))
    prev = pl.BlockSpec((BLK, 2 * D_KV), lambda i: (jnp.maximum(blk(i) - 1, 0), 0))
    return cur, prev


def _attn_fwd(q, kv, ga, sinks, norm_attn_out):
    seq = q.shape[0]
    n_blocks = seq // BLK

    def body(q_ref, kvc_ref, kvp_ref, ga_ref, sink_ref, gn_ref, ya_ref, oa_ref):
        geometry = _band_geometry(pl.program_id(0))
        _, _, vbd, probs, _ = _attn_probs(q_ref, kvc_ref, kvp_ref, sink_ref, geometry)
        p4s = [_split(probs[2 * j], probs[2 * j + 1], geometry[0]).astype(BF16) for j in range(N_HEADS // 2)]
        ya = jnp.concatenate([_nn(p4, vbd[j // 4]) for j, p4 in enumerate(p4s)], axis=1)
        ya_ref[...] = ya
        silu, _ = _silu_and_grad(ga_ref[...])
        oa_ref[...] = (ya * _rstd(ya) * gn_ref[...] * silu).astype(BF16)

    row = pl.BlockSpec((BLK, D_ATTN), lambda i: (i, 0))
    kv_cur, kv_prev = _kv_specs(n_blocks, reverse=False)
    return pl.pallas_call(
        body, name="attn_fwd", grid=(n_blocks,),
        out_shape=(pltpu.HBM((seq, D_ATTN), F32), pltpu.HBM((seq, D_ATTN), BF16)),
        in_specs=[row, kv_cur, kv_prev, row, pl.BlockSpec(memory_space=pltpu.SMEM), _resident((1, D_ATTN))],
        out_specs=(row, row),
        compiler_params=_params(32, ("arbitrary",)),
    )(q, kv, kv, ga, sinks, norm_attn_out)


def _out_proj_loss(x, oc, oa, wo, norm_final, target, pc, conv_w, norm_conv_out):
    seq = x.shape[0]
    tile = TOK_TILE
    n_tiles = seq // tile

    def body(x_ref, oc_ref, oa_ref, wo_ref, gf_ref, t_ref, pc_ref, hcc_ref, hcu_ref, cw_ref, gn_ref,
             dx2_ref, doa_ref, gwo_ref, dpc_ref, small_ref, loss_acc, zbuf, dbuf):
        step = pl.program_id(0)

        def small_add(row, value):
            small_ref[row:row + 1, :] += jnp.sum(value, axis=0, keepdims=True)

        @pl.when(step == 0)
        def _():
            gwo_ref[...] = jnp.zeros_like(gwo_ref)
            small_ref[...] = jnp.zeros_like(small_ref)
            loss_acc[...] = jnp.zeros_like(loss_acc)
            dbuf[tile:tile + 8, :] = jnp.zeros((8, D_CONV), F32)

        oc, oa = oc_ref[...], oa_ref[...]
        x2 = x_ref[...] + _nn(oc, wo_ref[0:D_CONV, :]) + _nn(oa, wo_ref[D_CONV:D_MIX, :])
        r = _rstd(x2)
        xhat = x2 * r
        err = xhat * gf_ref[...] - t_ref[...]
        loss_acc[...] += jnp.sum(err * err, axis=0, keepdims=True) * (0.5 / D_MODEL)
        dy = err * (1.0 / D_MODEL)
        small_add(SMALL_NORM_FINAL, dy * xhat)
        dx2 = _rms_bwd(dy * gf_ref[...], xhat, r)
        dx2_ref[...] = dx2
        db = dx2.astype(BF16)
        do = _nt(db, wo_ref[0:D_CONV, :])
        doa_ref[...] = _nt(db, wo_ref[D_CONV:D_MIX, :])
        gwo_ref[0:D_CONV, :] += _tn(oc, db)
        gwo_ref[D_CONV:D_MIX, :] += _tn(oa, db)

        is_first_tile = step == n_tiles - 1
        zbuf[0:8, :] = jnp.where(is_first_tile, 0.0, hcc_ref[...] * hcu_ref[...])
        cb, cc, cu, z, z1, z2, conv = _conv_core(pc_ref, zbuf, cw_ref)
        silu, dsilu = _silu_and_grad(pc_ref[:, 3 * D_CONV:4 * D_CONV])
        yc = cb * conv
        rc = _rstd(yc)
        chat = yc * rc
        dn = do * silu
        dpc_ref[:, 3 * D_CONV:4 * D_CONV] = (do * (chat * gn_ref[...]) * dsilu).astype(BF16)
        small_add(SMALL_NORM_CONV, dn * chat)
        dyc = _rms_bwd(dn * gn_ref[...], chat, rc)
        dpc_ref[:, 0:D_CONV] = (dyc * conv).astype(BF16)
        dconv = dyc * cb
        small_add(SMALL_CONV_W, dconv * z2)
        small_add(SMALL_CONV_W + 1, dconv * z1)
        small_add(SMALL_CONV_W + 2, dconv * z)
        dbuf[0:tile, :] = dconv
        dz = cw_ref[2:3, :] * dconv + cw_ref[1:2, :] * dbuf[1:tile + 1, :] + cw_ref[0:1, :] * dbuf[2:tile + 2, :]
        dpc_ref[:, D_CONV:2 * D_CONV] = (dz * cu).astype(BF16)
        dpc_ref[:, 2 * D_CONV:3 * D_CONV] = (dz * cc).astype(BF16)
        dbuf[tile:tile + 8, :] = dbuf[0:8, :]

        @pl.when(step == n_tiles - 1)
        def _():
            lane = lax.broadcasted_iota(jnp.int32, (1, D_MODEL), 1)
            small_ref[SMALL_MISC:SMALL_MISC + 1, :] = jnp.where(lane == SMALL_LOSS_LANE, jnp.sum(loss_acc[...]), 0.0)

    def rev(i):
        return n_tiles - 1 - i

    def row(width):
        return pl.BlockSpec((tile, width), lambda i: (rev(i), 0))

    def halo(col_block):
        return pl.BlockSpec((8, D_CONV), lambda i: (jnp.maximum(rev(i) * (tile // 8) - 1, 0), col_block))

    def const(shape):
        return pl.BlockSpec(shape, lambda i: (0, 0))

    return pl.pallas_call(
        body, name="out_proj_loss", grid=(n_tiles,),
        out_shape=(jax.ShapeDtypeStruct((seq, D_MODEL), F32), jax.ShapeDtypeStruct((seq, D_ATTN), F32),
                   jax.ShapeDtypeStruct((D_MIX, D_MODEL), F32), jax.ShapeDtypeStruct((seq, D_PC), BF16),
                   jax.ShapeDtypeStruct((SMALL_ROWS, D_MODEL), F32)),
        in_specs=[row(D_MODEL), row(D_CONV), row(D_ATTN), _resident(wo.shape), _resident((1, D_MODEL)), row(D_MODEL),
                  row(D_PC), halo(1), halo(2), _resident(conv_w.shape), _resident((1, D_CONV))],
        out_specs=(row(D_MODEL), row(D_ATTN), const((D_MIX, D_MODEL)), row(D_PC), const((SMALL_ROWS, D_MODEL))),
        scratch_shapes=[pltpu.VMEM((1, D_MODEL), F32), pltpu.VMEM((tile + 8, D_CONV), F32),
                        pltpu.VMEM((tile + 8, D_CONV), F32)],
        compiler_params=_params(62, ("arbitrary",)),
    )(x, oc, oa, wo, norm_final, target, pc, pc, pc, conv_w, norm_conv_out)


def _attn_bwd(q, kv, ga, ya, doa, sinks, norm_attn_out, gwo, small):
    seq = q.shape[0]
    n_blocks = seq // BLK
    stage_steps = (0, n_blocks // 4, n_blocks // 2, (3 * n_blocks) // 4, n_blocks - 1)

    def body(q_ref, kvc_ref, kvp_ref, ga_ref, ya_ref, doa_ref, sink_ref, gn_ref, gwo_ref, small_ref,
             dqg_ref, small_out, gwo_sh, gna_ref, gs_ref, carry, dya_buf, *rs_scratch):
        step = pl.program_id(0)

        @pl.when(step == 0)
        def _():
            gna_ref[...] = jnp.zeros_like(gna_ref)
            gs_ref[...] = jnp.zeros_like(gs_ref)
            carry[...] = jnp.zeros_like(carry)

        ya = ya_ref[...]
        r = _rstd(ya)
        xhat = ya * r
        silu, dsilu = _silu_and_grad(ga_ref[...])
        do = doa_ref[...]
        dn = do * silu
        dqg_ref[:, D_ATTN:2 * D_ATTN] = (do * (xhat * gn_ref[...]) * dsilu).astype(BF16)
        gna_ref[...] += jnp.sum(dn * xhat, axis=0, keepdims=True)
        dya_buf[...] = _rms_bwd(dn * gn_ref[...], xhat, r).astype(BF16)

        geometry = _band_geometry(n_blocks - 1 - step)
        use_cur = geometry[0]
        lane = lax.broadcasted_iota(jnp.int32, (BLK, 128), 1)
        sink_lane = lax.broadcasted_iota(jnp.int32, (1, 128), 1)
        gsink = jnp.zeros((1, 128), F32)

        def fold(bd):
            return (jnp.where(lane < 64, bd[0:BLK], 0.0) + jnp.where(lane >= 64, bd[BLK:2 * BLK], 0.0),
                    jnp.where(lane < 64, bd[2 * BLK:3 * BLK], 0.0) + jnp.where(lane >= 64, bd[3 * BLK:4 * BLK], 0.0))

        pairs = range(N_HEADS // 2)
        qps, kbd, vbd, probs, sink_probs = _attn_probs(q_ref, kvc_ref, kvp_ref, sink_ref, geometry)
        dyps = [dya_buf[:, j * 128:(j + 1) * 128] for j in pairs]
        dps = []
        for j in pairs:
            dps += _merge(_nt(dyps[j], vbd[j // 4]), use_cur)
        deltas = [jnp.sum(p * dp, axis=-1, keepdims=True) for p, dp in zip(probs, dps)]
        dss = [p * (dp - delta) for p, dp, delta in zip(probs, dps, deltas)]
        for h in range(N_HEADS):
            dsink = -jnp.sum(sink_probs[h] * deltas[h], axis=0, keepdims=True)
            gsink = gsink + jnp.where(sink_lane == h, dsink, 0.0)
        gs_ref[...] += gsink
        ds4s = [_split(dss[2 * j], dss[2 * j + 1], use_cur).astype(BF16) for j in pairs]
        p4s = [_split(probs[2 * j], probs[2 * j + 1], use_cur).astype(BF16) for j in pairs]
        dqg_ref[:, 0:D_ATTN] = jnp.concatenate([_nn(ds4s[j], kbd[j // 4]) * SCALE for j in pairs], axis=1).astype(BF16)
        sums = []
        for group in range(N_HEADS // HEADS_PER_KV):
            acc = [jnp.zeros((BLK, 128), F32) for _ in range(4)]
            for j in range(group * 4, group * 4 + 4):
                for slot, part in enumerate(fold(_tn(ds4s[j], qps[j])) + fold(_tn(p4s[j], dyps[j]))):
                    acc[slot] = acc[slot] + part
            sums.append([a + pltpu.roll(a, 64, 1) for a in acc])
        dk_cur, dk_prev, dv_cur, dv_prev = (jnp.where(lane < 64, a, b) for a, b in zip(sums[0], sums[1]))
        dqg_ref[:, 2 * D_ATTN:2 * D_ATTN + 2 * D_KV] = (jnp.concatenate([dk_cur, dv_cur], axis=1) + carry[...]).astype(BF16)
        carry[...] = jnp.concatenate([dk_prev, dv_prev], axis=1)

        @pl.when(step == n_blocks - 1)
        def _():
            small_out[...] = small_ref[...]
            small_out[SMALL_NORM_ATTN:SMALL_NORM_ATTN + 1, :] = gna_ref[...]
            small_out[SMALL_MISC:SMALL_MISC + 1, 0:128] = small_ref[SMALL_MISC:SMALL_MISC + 1, 0:128] + gs_ref[...]

        for at, stage in zip(stage_steps, _rs_wout_stages(gwo_ref, gwo_sh, *rs_scratch)):
            pl.when(step == at)(stage)

    row = pl.BlockSpec((BLK, D_ATTN), lambda i: (n_blocks - 1 - i, 0))
    kv_cur, kv_prev = _kv_specs(n_blocks, reverse=True)
    any_spec = pl.BlockSpec(memory_space=pl.ANY)
    return pl.pallas_call(
        body, name="attn_bwd", grid=(n_blocks,),
        out_shape=(jax.ShapeDtypeStruct((seq, D_QG), BF16), jax.ShapeDtypeStruct(small.shape, F32),
                   jax.ShapeDtypeStruct((gwo.shape[0] // N_CHIPS, gwo.shape[1]), F32)),
        in_specs=[row, kv_cur, kv_prev, row, row, row, pl.BlockSpec(memory_space=pltpu.SMEM), _resident((1, D_ATTN)),
                  any_spec, _resident(small.shape)],
        out_specs=(pl.BlockSpec((BLK, D_QG), lambda i: (n_blocks - 1 - i, 0)),
                   pl.BlockSpec(small.shape, lambda i: (0, 0)), any_spec),
        scratch_shapes=[pltpu.VMEM((1, D_ATTN), F32), pltpu.VMEM((1, 128), F32),
                        pltpu.VMEM((BLK, 2 * D_KV), F32), pltpu.VMEM((BLK, D_ATTN), BF16)] + _rs_wout_scratch(gwo.shape),
        compiler_params=_params(44, ("arbitrary",)),
    )(q, kv, kv, ga, ya, doa, sinks, norm_attn_out, gwo, small)


GBLK = 256
GSUB = 64
PAIR_RING = 4
LAG_PAIR, LAG_HOP1, LAG_HOP2 = 1, 7, 14


def _bwd_in(dpc, dqg, h, wt, x, norm_in, dx2, small):
    seq = x.shape[0]
    n_blk = D_IN_PROJ // GBLK
    per_chip = n_blk // N_CHIPS
    n_slots = (n_blk + 1) // 2
    n_sub = GBLK // GSUB
    chip_rows = D_IN_PROJ // N_CHIPS
    tile = TOK_TILE
    n_tiles = seq // tile
    n_steps = n_blk + max(n_tiles, LAG_HOP2)
    chunk = min(seq, 512)
    blk_q, blk_kv, blk_ga = ROW_Q // GBLK, ROW_KV // GBLK, ROW_GA // GBLK

    def block_of(i):
        k = i % N_CHIPS
        robin = per_chip * ((k % 2) * 2 + k // 2) + i // N_CHIPS
        if isinstance(i, int):
            return robin if i < per_chip * N_CHIPS else i
        return jnp.where(i < per_chip * N_CHIPS, robin, i)

    def owner_of(i):
        return (i // N_CHIPS) % 2

    def slot_of(i):
        return (i // (2 * N_CHIPS)) * N_CHIPS + i % N_CHIPS

    def body(dpc_ref, dqg_ref, wt_ref, h_ref, x_ref, g_ref, dx2_ref, small_ref, gx_ref, small_sum, gwt_sh,
             dh_acc, gni, keep, pbuf, xbuf, land, land2, small_land,
             pair_send, pair_recv, h1_send, h1_recv, h2_send, h2_recv, sw_send, sw_recv, sm_send, sm_recv, out_sem):
        step = pl.program_id(0)
        x_i, y_i, c = lax.axis_index("x"), lax.axis_index("y"), lax.axis_index("c")
        me = 4 * x_i + 2 * y_i + c
        j = 2 * x_i + y_i
        pa = (_xor(x_i, 1 - c), _xor(y_i, c), c)
        pb = (_xor(x_i, c), _xor(y_i, 1 - c), c)
        sib = (x_i, y_i, 1 - c)
        ja = 2 * pa[0] + pa[1]
        jb = 2 * pb[0] + pb[1]
        jd = 3 - j

        def remote(src, dst, send, recv, to):
            return pltpu.make_async_remote_copy(src_ref=src, dst_ref=dst, send_sem=send, recv_sem=recv,
                                                device_id=to, device_id_type=MESH)

        def piece(ref, slot, u, n):
            return ref.at[slot, pl.ds(u * GSUB, n * GSUB), :]

        def chip_rows_at(ref, local, n):
            return ref.at[pl.ds(pl.multiple_of(local, GSUB), n * GSUB), :]

        def pair_copy(i):
            slot = slot_of(i)
            return remote(pbuf.at[i % PAIR_RING], land.at[slot], pair_send.at[slot], pair_recv.at[slot], sib)

        def h1_copy(slot, u, n):
            k = slot * n_sub + u
            return remote(piece(xbuf, slot, u, n), piece(xbuf, slot, u, n), h1_send.at[k], h1_recv.at[k], pa)

        def h2_copy(slot, u, n, local):
            k = slot * n_sub + u
            return remote(piece(xbuf, slot, u, n), chip_rows_at(land2, local, n), h2_send.at[k], h2_recv.at[k], pb)

        def sw_copy(slot, u, n, local):
            k = slot * n_sub + u
            return remote(piece(keep, slot, u, n), chip_rows_at(gwt_sh, local, n), sw_send.at[k], sw_recv.at[k], sib)

        def owned(i):
            return (i >= 0) & (i < n_blk) & (owner_of(i) == c)

        def chip_of(blk, u):
            row = blk * GBLK + u * GSUB
            chip = row // chip_rows
            return chip, row - chip * chip_rows

        def pieces(blk):
            first, local = chip_of(blk, 0)
            whole = first == chip_of(blk, n_sub - 1)[0]
            if isinstance(blk, int):
                return [(True, 0, n_sub, first, local)] if whole else [(True, u, 1) + chip_of(blk, u) for u in range(n_sub)]
            return [(whole, 0, n_sub, first, local)] + [(jnp.logical_not(whole), u, 1) + chip_of(blk, u) for u in range(n_sub)]

        @pl.when(step == 0)
        def _():
            dh_acc[...] = jnp.zeros_like(dh_acc)
            gni[...] = jnp.zeros_like(gni)

        @pl.when(step < n_blk)
        def _():
            from_pc = block_of(step) < blk_q
            block = _tn(jnp.where(from_pc, dpc_ref[...], dqg_ref[...]), h_ref[...])
            for t in range(0, seq, chunk):
                d = jnp.where(from_pc, dpc_ref[t:t + chunk, :], dqg_ref[t:t + chunk, :])
                dh_acc[t:t + chunk, :] += _nn(d, wt_ref[...])

            @pl.when(owner_of(step) == c)
            def _():
                keep[slot_of(step)] = block

            @pl.when(owner_of(step) != c)
            def _():
                @pl.when(step >= 2 * PAIR_RING)
                def _():
                    pair_copy(step - 2 * PAIR_RING).wait_send()
                pbuf[step % PAIR_RING] = block.astype(BF16)
                pair_copy(step).start()

        i1 = step - LAG_PAIR

        @pl.when(owned(i1))
        def _():
            slot = slot_of(i1)
            pair_copy(i1).wait_recv()
            _accumulate(keep.at[slot], land.at[slot])
            for cond, u, n, chip, _ in pieces(block_of(i1)):
                @pl.when(cond & ((chip == ja) | (chip == jd)))
                def _(u=u, n=n):
                    _cast_rows(piece(keep, slot, u, n), piece(xbuf, slot, u, n))
                    h1_copy(slot, u, n).start()

        i2 = step - LAG_HOP1

        @pl.when(owned(i2))
        def _():
            slot = slot_of(i2)
            for cond, u, n, chip, local in pieces(block_of(i2)):
                @pl.when(cond & ((chip == j) | (chip == jb)))
                def _(u=u, n=n, chip=chip, local=local):
                    h1_copy(slot, u, n).wait_recv()
                    _accumulate(piece(keep, slot, u, n), piece(xbuf, slot, u, n))

                    @pl.when(chip == jb)
                    def _():
                        _cast_rows(piece(keep, slot, u, n), piece(xbuf, slot, u, n))
                        h2_copy(slot, u, n, local).start()

                @pl.when(cond & ((chip == ja) | (chip == jd)))
                def _(u=u, n=n):
                    h1_copy(slot, u, n).wait_send()

        i3 = step - LAG_HOP2

        @pl.when(owned(i3))
        def _():
            slot = slot_of(i3)
            for cond, u, n, chip, local in pieces(block_of(i3)):
                @pl.when(cond & (chip == j))
                def _(u=u, n=n, local=local):
                    h2_copy(slot, u, n, local).wait_recv()
                    _accumulate(piece(keep, slot, u, n), chip_rows_at(land2, local, n))
                    mine = pltpu.make_async_copy(piece(keep, slot, u, n), chip_rows_at(gwt_sh, local, n), out_sem.at[0])
                    mine.start()
                    sw_copy(slot, u, n, local).start()
                    mine.wait()

                @pl.when(cond & (chip == jb))
                def _(u=u, n=n, local=local):
                    h2_copy(slot, u, n, local).wait_send()

        e = step - n_blk

        @pl.when((e >= 0) & (e < n_tiles))
        def _():
            dh = dh_acc[pl.ds(pl.multiple_of(e * tile, tile), tile), :]
            xv = x_ref[...]
            r = _rstd(xv)
            xhat = xv * r
            gni[...] += jnp.sum(dh * xhat, axis=0, keepdims=True)
            gx_ref[...] = _rms_bwd(dh * g_ref[...], xhat, r) + dx2_ref[...]

        @pl.when(step == n_steps - 1)
        def _():
            small_land[me] = small_ref[...]
            small_land[me, SMALL_NORM_IN:SMALL_NORM_IN + 1, :] = gni[...]
            others = [(dx, dy, dc) for dx in (0, 1) for dy in (0, 1) for dc in (0, 1)][1:]
            sends = [remote(small_land.at[me], small_land.at[me], sm_send.at[k], sm_recv.at[k],
                            (_xor(x_i, dx), _xor(y_i, dy), _xor(c, dc))) for k, (dx, dy, dc) in enumerate(others)]
            for cp in sends:
                cp.start()
            for i in range(n_blk):
                if i + 2 * PAIR_RING >= n_blk:
                    @pl.when(owner_of(i) != c)
                    def _(i=i):
                        pair_copy(i).wait_send()
                for _, u, n, chip, local in pieces(block_of(i)):
                    @pl.when((j == chip) & (c == owner_of(i)))
                    def _(i=i, u=u, n=n, local=local):
                        sw_copy(slot_of(i), u, n, local).wait_send()

                    @pl.when((j == chip) & (c != owner_of(i)))
                    def _(i=i, u=u, n=n, local=local):
                        sw_copy(slot_of(i), u, n, local).wait_recv()
            for cp in sends:
                cp.wait_recv()
            total = small_land[0]
            for dev in range(1, 8):
                total = total + small_land[dev]
            small_sum[...] = total
            for cp in sends:
                cp.wait_send()

    def blk_at(i):
        return block_of(jnp.clip(i, 0, n_blk - 1))

    last_pc_step = max(i for i in range(n_blk) if block_of(i) < blk_q)

    def next_block(i, in_pc):
        i = jnp.clip(i, 0, n_blk - 1)
        step = jnp.full_like(i, last_pc_step if in_pc else n_blk - 1)
        for ahead in reversed(range(N_CHIPS)):
            cand = jnp.minimum(i + ahead, n_blk - 1)
            step = jnp.where((block_of(cand) < blk_q) == in_pc, cand, step)
        return block_of(step)

    def dqg_block(i):
        b = next_block(i, False)
        q_blk = jnp.clip(b - blk_q, 0, blk_kv - blk_q - 1)
        ga_blk = (D_ATTN // GBLK) + jnp.clip(b - blk_ga, 0, n_blk - blk_ga - 1)
        return jnp.where(b < blk_kv, q_blk, jnp.where(b == blk_kv, 2 * D_ATTN // GBLK, ga_blk))

    def tok(i):
        return (jnp.clip(i - n_blk, 0, n_tiles - 1), 0)

    n_piece = n_slots * n_sub
    dma = pltpu.SemaphoreType.DMA
    return pl.pallas_call(
        body, name="bwd_in", grid=(n_steps,),
        out_shape=(jax.ShapeDtypeStruct((seq, D_MODEL), F32), jax.ShapeDtypeStruct(small.shape, F32),
                   jax.ShapeDtypeStruct((chip_rows, D_MODEL), F32)),
        in_specs=[pl.BlockSpec((seq, GBLK), lambda i: (0, next_block(i, True))),
                  pl.BlockSpec((seq, GBLK), lambda i: (0, dqg_block(i))),
                  pl.BlockSpec((GBLK, D_MODEL), lambda i: (blk_at(i), 0)),
                  _resident(h.shape),
                  pl.BlockSpec((tile, D_MODEL), tok), _resident((1, D_MODEL)), pl.BlockSpec((tile, D_MODEL), tok),
                  _resident(small.shape)],
        out_specs=(pl.BlockSpec((tile, D_MODEL), tok), pl.BlockSpec(small.shape, lambda i: (0, 0)),
                   pl.BlockSpec(memory_space=pl.ANY)),
        scratch_shapes=[pltpu.VMEM((seq, D_MODEL), F32), pltpu.VMEM((1, D_MODEL), F32),
                        pltpu.VMEM((n_slots, GBLK, D_MODEL), F32), pltpu.VMEM((PAIR_RING, GBLK, D_MODEL), BF16),
                        pltpu.VMEM((n_slots, GBLK, D_MODEL), BF16), pltpu.VMEM((n_slots, GBLK, D_MODEL), BF16),
                        pltpu.VMEM((chip_rows, D_MODEL), BF16), pltpu.VMEM((8,) + small.shape, F32),
                        dma((n_slots,)), dma((n_slots,)), dma((n_piece,)), dma((n_piece,)), dma((n_piece,)),
                        dma((n_piece,)), dma((n_piece,)), dma((n_piece,)), dma((7,)), dma((7,)), dma((1,))],
        compiler_params=_params(62, ("arbitrary",)),
    )(dpc, dqg, wt, h, x, norm_in, dx2, small)


def _accumulate(dst_ref, src_ref, rows=16):
    def step(i, carry):
        sl = pl.ds(pl.multiple_of(i * rows, rows), rows)
        dst_ref[sl, :] = dst_ref[sl, :] + src_ref[sl, :].astype(F32)
        return carry
    lax.fori_loop(0, dst_ref.shape[0] // rows, step, 0)


def _rs_wout_scratch(gwo_shape):
    o_half, width = gwo_shape[0] // N_CHIPS // 2, gwo_shape[1]
    return [pltpu.VMEM((4, o_half, width), F32), pltpu.VMEM((4, o_half, width), BF16),
            pltpu.VMEM((4, o_half, width), BF16), pltpu.VMEM((2, o_half, width), BF16),
            pltpu.VMEM((o_half, width), BF16),
            pltpu.SemaphoreType.DMA((8,)), pltpu.SemaphoreType.DMA((8,)), pltpu.SemaphoreType.DMA((4,))]


def _rs_wout_stages(gwo_ref, gwo_sh, acc_o, sb_o, r1o, r2o, r3o, send_sems, recv_sems, local_sems):
    o_rows = gwo_ref.shape[0] // N_CHIPS
    o_half = o_rows // 2
    x, y, c = lax.axis_index("x"), lax.axis_index("y"), lax.axis_index("c")
    j = 2 * x + y
    pa = (_xor(x, 1 - c), _xor(y, c), c)
    pb = (_xor(x, c), _xor(y, 1 - c), c)
    sib = (x, y, 1 - c)
    ja = 2 * pa[0] + pa[1]
    jb = 2 * pb[0] + pb[1]
    jd = 3 - j
    order = (ja, jd, jb, j)
    sib_order = (jb, jd, ja, j)

    def rcopy(k, src, dst, to):
        return pltpu.make_async_remote_copy(src_ref=src, dst_ref=dst, send_sem=send_sems.at[k],
                                            recv_sem=recv_sems.at[k], device_id=to, device_id_type=MESH)

    def o_rows_of(chip, half):
        return gwo_ref.at[pl.ds(pl.multiple_of(chip * o_rows + half * o_half, 8), o_half), :]

    def load_all(chips, half):
        cps = [pltpu.make_async_copy(o_rows_of(chip, half), acc_o.at[s], local_sems.at[s]) for s, chip in enumerate(chips)]
        for cp in cps:
            cp.start()
        return cps

    def pair_send(s):
        return rcopy(s, sb_o.at[s], r1o.at[s], sib)

    def out_half(half):
        return gwo_sh.at[pl.ds(pl.multiple_of(half * o_half, 8), o_half), :]

    hop1 = [rcopy(4 + s, sb_o.at[s], r2o.at[s], pa) for s in range(2)]
    hop2 = rcopy(6, sb_o.at[2], r3o, pb)
    swap = rcopy(7, acc_o.at[3], out_half(c), sib)
    mine = pltpu.make_async_copy(acc_o.at[3], out_half(c), local_sems.at[0])

    def resend(s, copy):
        pair_send(s).wait_send()
        _cast_rows(acc_o.at[s], sb_o.at[s])
        copy.start()

    def stage_pair():
        for s, cp in enumerate(load_all(sib_order, 1 - c)):
            cp.wait()
            _cast_rows(acc_o.at[s], sb_o.at[s])
            pair_send(s).start()

    def stage_hop1():
        for s, cp in enumerate(load_all(order, c)):
            cp.wait()
            pair_send(s).wait_recv()
            _accumulate(acc_o.at[s], r1o.at[s])
            if s < 2:
                resend(s, hop1[s])

    def stage_hop2():
        hop1[1].wait_recv()
        _accumulate(acc_o.at[2], r2o.at[1])
        resend(2, hop2)
        hop1[0].wait_recv()
        _accumulate(acc_o.at[3], r2o.at[0])

    def stage_final():
        hop2.wait_recv()
        _accumulate(acc_o.at[3], r3o)
        mine.start()
        swap.start()

    def stage_drain():
        rcopy(7, acc_o.at[3], out_half(1 - c), sib).wait_recv()
        for cp in [pair_send(3)] + hop1 + [hop2, swap]:
            cp.wait_send()
        mine.wait()

    return stage_pair, stage_hop1, stage_hop2, stage_final, stage_drain


def _adamw(name, w, g, m, v, rows):
    def body(w_ref, g_ref, m_ref, v_ref, go_ref, d_ref, nm_ref, nv_ref):
        _adamw_update(w_ref, g_ref, m_ref, v_ref, go_ref, d_ref, nm_ref, nv_ref)

    spec = pl.BlockSpec((rows, w.shape[1]), lambda i: (i, 0))
    shape = jax.ShapeDtypeStruct(w.shape, F32)
    return pl.pallas_call(
        body, name="adamw_" + name, grid=(w.shape[0] // rows,),
        out_shape=(shape,) * 4, in_specs=[spec] * 4, out_specs=(spec,) * 4,
        compiler_params=_params(32, ("arbitrary",)),
    )(w, g, m, v)


def _adamw_update(w_ref, g_ref, m_ref, v_ref, go_ref, d_ref, nm_ref, nv_ref):
    gv = g_ref[...]
    go_ref[...] = gv
    nm = ADAM_B1 * m_ref[...] + (1.0 - ADAM_B1) * gv
    nv = ADAM_B2 * v_ref[...] + (1.0 - ADAM_B2) * (gv * gv)
    m_hat = nm / (1.0 - ADAM_B1 ** ADAM_STEP)
    v_hat = nv / (1.0 - ADAM_B2 ** ADAM_STEP)
    d_ref[...] = -ADAM_LR * (m_hat / (jnp.sqrt(v_hat) + ADAM_EPS) + ADAM_WD * w_ref[...])
    nm_ref[...] = nm
    nv_ref[...] = nv


def _adamw_small(weights, grads, ms, vs):
    n = len(weights)

    def body(*refs):
        ins, outs = refs[:4 * n], refs[4 * n:]
        for k in range(n):
            _adamw_update(*ins[4 * k:4 * k + 4], *outs[4 * k:4 * k + 4])

    flat = [a for group in zip(weights, grads, ms, vs) for a in group]
    vmem = pl.BlockSpec(memory_space=pltpu.VMEM)
    out = pl.pallas_call(
        body, name="adamw_small",
        out_shape=tuple(jax.ShapeDtypeStruct(w.shape, F32) for w in weights for _ in range(4)),
        in_specs=[vmem] * (4 * n), out_specs=(vmem,) * (4 * n),
    )(*flat)
    return [tuple(out[4 * k:4 * k + 4]) for k in range(n)]


def kernel(x, norm_in, w_in, conv_w, attn_sinks, norm_conv_out, norm_attn_out, w_out, norm_final, loss_target, m_norm_in, m_w_in, m_conv_w, m_attn_sinks, m_norm_conv_out, m_norm_attn_out, m_w_out, m_norm_final, v_norm_in, v_w_in, v_conv_w, v_attn_sinks, v_norm_conv_out, v_norm_attn_out, v_w_out, v_norm_final):
    chip = 2 * lax.axis_index("x") + lax.axis_index("y")
    xs, target = x[0], loss_target[0]
    norm_final2 = norm_final.reshape(1, D_MODEL)
    w_in_t, m_w_in_t, v_w_in_t = w_in[0].T, m_w_in[0].T, v_w_in[0].T

    wt, cw4 = _ag_weights(w_in_t, conv_w[0])
    cw = jnp.transpose(cw4, (1, 0, 2)).reshape(3, D_CONV)

    h, pc, q, kv, ga, oc, wo = _fwd_in(xs, norm_in, wt, w_out[0], cw, norm_conv_out)
    ya, oa = _attn_fwd(q, kv, ga, attn_sinks, norm_attn_out)
    dx2, doa, gwo, dpc, small = _out_proj_loss(xs, oc, oa, wo, norm_final2, target, pc, cw, norm_conv_out)
    dqg, small, gwo_sh = _attn_bwd(q, kv, ga, ya, doa, attn_sinks, norm_attn_out, gwo, small)
    grad_x, small_sum, gwt_sh = _bwd_in(dpc, dqg, h, wt, xs, norm_in, dx2, small)

    loss = small_sum[SMALL_MISC, SMALL_LOSS_LANE]
    g_norm_in, g_norm_conv, g_norm_attn = (small_sum[r:r + 1] for r in (SMALL_NORM_IN, SMALL_NORM_CONV, SMALL_NORM_ATTN))
    g_norm_final = small_sum[SMALL_NORM_FINAL]
    g_conv_w = lax.dynamic_slice(small_sum[SMALL_CONV_W:SMALL_CONV_W + 3], (0, chip * (D_CONV // N_CHIPS)),
                                 (3, D_CONV // N_CHIPS))[None]
    g_sinks = small_sum[SMALL_MISC:SMALL_MISC + 1, 0:N_HEADS]

    def two_d(a):
        return a.reshape(-1, a.shape[-1])

    def from_hbm(*arrays):
        return tuple(pltpu.with_memory_space_constraint(a, pltpu.HBM) for a in arrays)

    up_w_in = tuple(o.T[None] for o in _adamw("w_in", *from_hbm(w_in_t, gwt_sh, m_w_in_t, v_w_in_t), 200))
    up_w_out = tuple(o[None] for o in _adamw("w_out", *from_hbm(w_out[0], gwo_sh, m_w_out[0], v_w_out[0]), 128))
    small_w = (norm_in, conv_w, attn_sinks, norm_conv_out, norm_attn_out, norm_final)
    small_g = (g_norm_in, g_conv_w, g_sinks, g_norm_conv, g_norm_attn, g_norm_final)
    small_m = (m_norm_in, m_conv_w, m_attn_sinks, m_norm_conv_out, m_norm_attn_out, m_norm_final)
    small_v = (v_norm_in, v_conv_w, v_attn_sinks, v_norm_conv_out, v_norm_attn_out, v_norm_final)
    up_small = _adamw_small(*(tuple(two_d(a) for a in group) for group in (small_w, small_g, small_m, small_v)))
    up_small = [tuple(o.reshape(w.shape) for o in up) for up, w in zip(up_small, small_w)]
    up_norm_in, up_conv_w, up_sinks, up_norm_conv, up_norm_attn, up_norm_final = up_small
    updates = (up_norm_in, up_w_in, up_conv_w, up_sinks, up_norm_conv, up_norm_attn, up_w_out, up_norm_final)
    grads_out, deltas, new_m, new_v = zip(*updates)
    return (loss, grad_x[None], *grads_out, *deltas, *new_m, *new_v)
```

```python
import jax
import jax.numpy as jnp
from jax import lax
from jax.experimental import pallas as pl
from jax.experimental.pallas import tpu as pltpu

F32 = jnp.float32
BF16 = jnp.bfloat16
MESH = pl.DeviceIdType.MESH

D_MODEL = 1024
D_CONV = 1024
D_ATTN = 1024
D_KV = 128
D_QG = 2 * D_ATTN + 2 * D_KV
D_MIX = D_CONV + D_ATTN
D_PC = 4 * D_CONV
D_IN_PROJ = D_PC + 2 * D_ATTN + 2 * D_KV
ROW_Q = D_PC
ROW_KV = ROW_Q + D_ATTN
ROW_GA = ROW_KV + 2 * D_KV
N_HEADS = 16
HEAD_DIM = 64
HEADS_PER_KV = 8
BLK = 128
N_CHIPS = 4
RMS_EPS = 1e-5
SCALE = HEAD_DIM ** -0.5
SLOPES = tuple(2.0 ** (-8.0 * (h + 1) / N_HEADS) for h in range(N_HEADS))

ADAM_LR, ADAM_B1, ADAM_B2, ADAM_EPS, ADAM_WD, ADAM_STEP = 0.001, 0.9, 0.999, 1e-08, 0.01, 10

SMALL_ROWS = 8
SMALL_NORM_IN, SMALL_NORM_CONV, SMALL_NORM_ATTN, SMALL_NORM_FINAL, SMALL_CONV_W, SMALL_MISC = 0, 1, 2, 3, 4, 7
SMALL_LOSS_LANE = N_HEADS

TOK_TILE = 256
MIB = 1 << 20


def _params(vmem_mib, semantics=None):
    return pltpu.CompilerParams(dimension_semantics=semantics, vmem_limit_bytes=vmem_mib * MIB)


def _nn(a, b):
    return jnp.dot(a, b, preferred_element_type=F32)


def _nt(a, b):
    return lax.dot_general(a, b, (((1,), (1,)), ((), ())), preferred_element_type=F32)


def _tn(a, b):
    return lax.dot_general(a, b, (((0,), (0,)), ((), ())), preferred_element_type=F32)


def _rstd(v):
    return lax.rsqrt(jnp.mean(v * v, axis=-1, keepdims=True) + RMS_EPS)


def _rms_bwd(g, xhat, rstd):
    return rstd * (g - xhat * jnp.mean(g * xhat, axis=-1, keepdims=True))


def _silu_and_grad(g):
    s = jax.nn.sigmoid(g)
    return g * s, s * (1.0 + g * (1.0 - s))


def _resident(shape):
    return pl.BlockSpec(shape, lambda *_: (0,) * len(shape), pipeline_mode=pl.Buffered(1))


def _xor(a, b):
    return a + b - 2 * a * b


def _cast_rows(src_ref, dst_ref, rows=32):
    def step(i, carry):
        sl = pl.ds(pl.multiple_of(i * rows, rows), rows)
        dst_ref[sl, :] = src_ref[sl, :].astype(dst_ref.dtype)
        return carry
    lax.fori_loop(0, src_ref.shape[0] // rows, step, 0)


def _ag_scratch(shard_shape):
    rows, width = shard_shape
    return [pltpu.VMEM((rows, width), F32), pltpu.VMEM((rows, width), BF16), pltpu.VMEM((3, rows // 2, width), BF16),
            pltpu.SemaphoreType.DMA((6,)), pltpu.SemaphoreType.DMA((6,)), pltpu.SemaphoreType.DMA((4,))]


def _ag_stages(sh_ref, out, f32_buf, own, land, send_sems, recv_sems, local_sems):
    rows = sh_ref.shape[0]
    half = rows // 2
    x, y, c = lax.axis_index("x"), lax.axis_index("y"), lax.axis_index("c")
    j = 2 * x + y
    p1 = (_xor(x, c), _xor(y, 1 - c), c)
    p2 = (_xor(x, 1 - c), _xor(y, c), c)
    sib = (x, y, 1 - c)
    j1 = 2 * p1[0] + p1[1]
    j2 = 2 * p2[0] + p2[1]
    j3 = 3 - j

    def rows_of(chip, hf):
        return out.at[pl.ds(pl.multiple_of(chip * rows + hf * half, 16), half), :]

    def rcopy(k, src, dst, to):
        return pltpu.make_async_remote_copy(src_ref=src, dst_ref=dst, send_sem=send_sems.at[k],
                                            recv_sem=recv_sems.at[k], device_id=to, device_id_type=MESH)

    my_half = own.at[pl.ds(pl.multiple_of(c * half, 16), half), :]
    hop1 = rcopy(0, my_half, land.at[0], p1)
    hop2_own = rcopy(1, my_half, land.at[1], p2)
    hop2_fwd = rcopy(2, land.at[0], land.at[2], p2)
    swaps = [rcopy(3 + s, land.at[s], rows_of(chip, c), sib) for s, chip in enumerate((j1, j2, j3))]
    keeps = [pltpu.make_async_copy(land.at[s], rows_of(chip, c), local_sems.at[1 + s]) for s, chip in enumerate((j1, j2, j3))]
    load = pltpu.make_async_copy(sh_ref, f32_buf, local_sems.at[0])
    own_out = pltpu.make_async_copy(own, out.at[pl.ds(pl.multiple_of(j * rows, 16), rows), :], local_sems.at[0])

    def stage_send():
        load.start()
        load.wait()
        _cast_rows(f32_buf, own)
        own_out.start()
        hop1.start()

    def stage_forward():
        hop1.wait_recv()
        hop2_own.start()
        hop2_fwd.start()
        swaps[0].start()
        keeps[0].start()

    def stage_publish():
        hop2_own.wait_recv()
        swaps[1].start()
        keeps[1].start()
        hop2_fwd.wait_recv()
        swaps[2].start()
        keeps[2].start()

    def stage_drain():
        for s, chip in enumerate((j2, j1, j3)):
            rcopy(3 + s, my_half, rows_of(chip, 1 - c), sib).wait_recv()
        for cp in [hop1, hop2_own, hop2_fwd] + swaps:
            cp.wait_send()
        for cp in [own_out] + keeps:
            cp.wait()

    return stage_send, stage_forward, stage_publish, stage_drain


def _ag_weights(wt_sh, cw_sh):
    def body(wt_ref, cw_ref, wt_out, cw_out, *scratch):
        cw_send, cw_recv, cw_local = scratch[-3:]
        x, y, c = lax.axis_index("x"), lax.axis_index("y"), lax.axis_index("c")
        j = 2 * x + y
        p1 = (_xor(x, c), _xor(y, 1 - c), c)
        p2 = (_xor(x, 1 - c), _xor(y, c), c)
        j1 = 2 * p1[0] + p1[1]

        def cw_copy(k, chip, to):
            src = cw_ref if k != 2 else cw_out.at[chip]
            return pltpu.make_async_remote_copy(src_ref=src, dst_ref=cw_out.at[chip], send_sem=cw_send.at[k],
                                                recv_sem=cw_recv.at[k], device_id=to, device_id_type=MESH)

        mine = pltpu.make_async_copy(cw_ref, cw_out.at[j], cw_local.at[0])
        mine.start()
        first = cw_copy(0, j, p1)
        first.start()
        stages = _ag_stages(wt_ref, wt_out, *scratch[:-3])
        stages[0]()
        first.wait_recv()
        second = [cw_copy(1, j, p2), cw_copy(2, j1, p2)]
        for cp in second:
            cp.start()
        for stage in stages[1:]:
            stage()
        for cp in second:
            cp.wait_recv()
        for cp in [first] + second:
            cp.wait_send()
        mine.wait()

    any_spec = pl.BlockSpec(memory_space=pl.ANY)
    dma = pltpu.SemaphoreType.DMA
    return pl.pallas_call(
        body, name="ag_weights",
        out_shape=(pltpu.HBM((N_CHIPS * wt_sh.shape[0], wt_sh.shape[1]), BF16),
                   jax.ShapeDtypeStruct((N_CHIPS,) + cw_sh.shape, cw_sh.dtype)),
        in_specs=[any_spec, any_spec], out_specs=(any_spec, any_spec),
        scratch_shapes=_ag_scratch(wt_sh.shape) + [dma((3,)), dma((3,)), dma((1,))],
        compiler_params=_params(32),
    )(wt_sh, cw_sh)


def _fwd_in(x, norm_in, wt, wo_sh, conv_w, norm_conv_out, sinks, norm_attn_out):
    seq = x.shape[0]
    tile = TOK_TILE
    n_tiles = seq // tile
    stage_steps = (0, n_tiles // 4, (5 * n_tiles) // 8, n_tiles - 1)
    blocks = tile // BLK

    def body(x_ref, g_ref, wt_ref, wo_ref, cw_ref, gn_ref, sink_ref, gna_ref,
             h_ref, pc_ref, q_ref, kv_ref, ga_ref, oc_ref, ya_ref, oa_ref, wo_out, zbuf, kv_last, *ag_scratch):
        step = pl.program_id(0)
        stages = _ag_stages(wo_ref, wo_out, *ag_scratch)
        for at, stage in zip(stage_steps[:-1], stages[:-1]):
            pl.when(step == at)(stage)

        @pl.when(step == 0)
        def _():
            zbuf[0:8, :] = jnp.zeros((8, D_CONV), F32)
            kv_last[...] = jnp.zeros_like(kv_last)

        def attention(b):
            rows = pl.ds(b * BLK, BLK)
            kv_prev = kv_last if b == 0 else kv_ref.at[pl.ds((b - 1) * BLK, BLK), :]
            geometry = _band_geometry(step * blocks + b)
            _, _, vbd, probs, _ = _attn_probs(q_ref.at[rows, :], kv_ref.at[rows, :], kv_prev, sink_ref, geometry)
            p4s = [_split(probs[2 * j], probs[2 * j + 1], geometry[0]).astype(BF16) for j in range(N_HEADS // 2)]
            ya = jnp.concatenate([_nn(p4, vbd[j // 4]) for j, p4 in enumerate(p4s)], axis=1)
            ya_ref[rows, :] = ya
            silu, _ = _silu_and_grad(ga_ref[rows, :])
            oa_ref[rows, :] = (ya * _rstd(ya) * gna_ref[...] * silu).astype(BF16)

        xv = x_ref[...]
        h = (xv * _rstd(xv) * g_ref[...]).astype(BF16)
        h_ref[...] = h
        q_ref[...] = _nt(h, wt_ref[ROW_Q:ROW_KV, :])
        kv_ref[...] = _nt(h, wt_ref[ROW_KV:ROW_GA, :])
        ga_ref[...] = _nt(h, wt_ref[ROW_GA:D_IN_PROJ, :])
        for blk in range(D_PC // D_CONV):
            pc_ref[:, blk * D_CONV:(blk + 1) * D_CONV] = _nt(h, wt_ref[blk * D_CONV:(blk + 1) * D_CONV, :])
            if blk < blocks:
                attention(blk)
        kv_last[...] = kv_ref[tile - BLK:tile, :]

        cb, _, _, _, _, _, conv = _conv_core(pc_ref, zbuf, cw_ref)
        yc = cb * conv
        silu, _ = _silu_and_grad(pc_ref[:, 3 * D_CONV:4 * D_CONV])
        oc_ref[...] = (yc * _rstd(yc) * gn_ref[...] * silu).astype(BF16)
        zbuf[0:8, :] = zbuf[tile:tile + 8, :]
        pl.when(pl.program_id(0) == stage_steps[-1])(stages[-1])

    def row(width):
        return pl.BlockSpec((tile, width), lambda i: (i, 0))

    any_spec = pl.BlockSpec(memory_space=pl.ANY)
    return pl.pallas_call(
        body, name="fwd_in", grid=(n_tiles,),
        out_shape=(jax.ShapeDtypeStruct((seq, D_MODEL), BF16), jax.ShapeDtypeStruct((seq, D_PC), F32),
                   jax.ShapeDtypeStruct((seq, D_ATTN), F32), jax.ShapeDtypeStruct((seq, 2 * D_KV), F32),
                   jax.ShapeDtypeStruct((seq, D_ATTN), F32), jax.ShapeDtypeStruct((seq, D_CONV), BF16),
                   pltpu.HBM((seq, D_ATTN), F32), pltpu.HBM((seq, D_ATTN), BF16),
                   jax.ShapeDtypeStruct((N_CHIPS * wo_sh.shape[0], wo_sh.shape[1]), BF16)),
        in_specs=[row(D_MODEL), _resident((1, D_MODEL)), _resident(wt.shape), any_spec,
                  _resident(conv_w.shape), _resident((1, D_CONV)), pl.BlockSpec(memory_space=pltpu.SMEM),
                  _resident((1, D_ATTN))],
        out_specs=(row(D_MODEL), row(D_PC), row(D_ATTN), row(2 * D_KV), row(D_ATTN), row(D_CONV), row(D_ATTN),
                   row(D_ATTN), any_spec),
        scratch_shapes=[pltpu.VMEM((tile + 8, D_CONV), F32), pltpu.VMEM((BLK, 2 * D_KV), F32)] + _ag_scratch(wo_sh.shape),
        compiler_params=_params(60, ("arbitrary",)),
    )(x, norm_in, wt, wo_sh, conv_w, norm_conv_out, sinks, norm_attn_out)


def _conv_core(pc_ref, zbuf, cw_ref):
    tile = pc_ref.shape[0]
    cb = pc_ref[:, 0:D_CONV]
    cc = pc_ref[:, D_CONV:2 * D_CONV]
    cu = pc_ref[:, 2 * D_CONV:3 * D_CONV]
    z = cc * cu
    zbuf[8:tile + 8, :] = z
    z1 = zbuf[7:tile + 7, :]
    z2 = zbuf[6:tile + 6, :]
    conv = cw_ref[0:1, :] * z2 + cw_ref[1:2, :] * z1 + cw_ref[2:3, :] * z
    return cb, cc, cu, z, z1, z2, conv


def _band_geometry(block_index):
    qi = lax.broadcasted_iota(jnp.int32, (BLK, BLK), 0)
    kp = lax.broadcasted_iota(jnp.int32, (BLK, BLK), 1)
    use_cur = kp <= qi
    dist = jnp.where(use_cur, qi - kp, qi - kp + BLK).astype(F32)
    valid = use_cur | (block_index > 0)
    return use_cur, dist, valid


def _block_diag(cur, prev, group):
    lane = lax.broadcasted_iota(jnp.int32, cur.shape, 1)

    def halves(t):
        other = pltpu.roll(t, 64, 1)
        lo, hi = (t, other) if group == 0 else (other, t)
        return jnp.where(lane < 64, lo, 0.0), jnp.where(lane >= 64, hi, 0.0)

    return jnp.concatenate(halves(cur) + halves(prev), axis=0).astype(BF16)


def _merge(s4, use_cur):
    return (jnp.where(use_cur, s4[:, 0:BLK], s4[:, 2 * BLK:3 * BLK]),
            jnp.where(use_cur, s4[:, BLK:2 * BLK], s4[:, 3 * BLK:4 * BLK]))


def _split(a, b, use_cur):
    return jnp.concatenate([jnp.where(use_cur, a, 0.0), jnp.where(use_cur, b, 0.0),
                            jnp.where(use_cur, 0.0, a), jnp.where(use_cur, 0.0, b)], axis=1)


def _softmax_head(s, head, sink, dist, valid):
    sc = jnp.where(valid, s - SLOPES[head] * dist, -jnp.inf)
    m = jnp.maximum(jnp.max(sc, axis=-1, keepdims=True), sink)
    p = jnp.exp(sc - m)
    es = jnp.exp(sink - m)
    inv = 1.0 / (jnp.sum(p, axis=-1, keepdims=True) + es)
    return p * inv, es * inv


def _attn_probs(q_ref, kvc_ref, kvp_ref, sink_ref, geometry):
    use_cur, dist, valid = geometry
    groups = range(N_HEADS // HEADS_PER_KV)
    kbd = [_block_diag(kvc_ref[:, 0:D_KV], kvp_ref[:, 0:D_KV], g) for g in groups]
    vbd = [_block_diag(kvc_ref[:, D_KV:2 * D_KV], kvp_ref[:, D_KV:2 * D_KV], g) for g in groups]
    qps = [(q_ref[:, j * 128:(j + 1) * 128] * SCALE).astype(BF16) for j in range(N_HEADS // 2)]
    scores = []
    for j, qp in enumerate(qps):
        scores += _merge(_nt(qp, kbd[j // 4]), use_cur)
    sinks = [sink_ref[0, h] for h in range(N_HEADS)]
    scores = [jnp.where(valid, s - SLOPES[h] * dist, -jnp.inf) for h, s in enumerate(scores)]
    maxes = [jnp.maximum(jnp.max(s, axis=-1, keepdims=True), sinks[h]) for h, s in enumerate(scores)]
    exps = [jnp.exp(s - m) for s, m in zip(scores, maxes)]
    sink_exps = [jnp.exp(sinks[h] - m) for h, m in enumerate(maxes)]
    invs = [1.0 / (jnp.sum(e, axis=-1, keepdims=True) + se) for e, se in zip(exps, sink_exps)]
    probs = [e * inv for e, inv in zip(exps, invs)]
    sink_probs = [se * inv for se, inv in zip(sink_exps, invs)]
    return qps, kbd, vbd, probs, sink_probs


def _kv_specs(n_blocks, reverse):
    def blk(i):
        return (n_blocks - 1 - i) if reverse else i
    cur = pl.BlockSpec((BLK, 2 * D_KV), lambda i: (blk(i), 0))
    prev = pl.BlockSpec((BLK, 2 * D_KV), lambda i: (jnp.maximum(blk(i) - 1, 0), 0))
    return cur, prev


def _out_proj_loss(x, oc, oa, wo, norm_final, target, pc, conv_w, norm_conv_out):
    seq = x.shape[0]
    tile = TOK_TILE
    n_tiles = seq // tile

    def body(x_ref, oc_ref, oa_ref, wo_ref, gf_ref, t_ref, pc_ref, hcc_ref, hcu_ref, cw_ref, gn_ref,
             dx2_ref, doa_ref, gwo_ref, dpc_ref, small_ref, loss_acc, zbuf, dbuf):
        step = pl.program_id(0)

        def small_add(row, value):
            small_ref[row:row + 1, :] += jnp.sum(value, axis=0, keepdims=True)

        @pl.when(step == 0)
        def _():
            gwo_ref[...] = jnp.zeros_like(gwo_ref)
            small_ref[...] = jnp.zeros_like(small_ref)
            loss_acc[...] = jnp.zeros_like(loss_acc)
            dbuf[tile:tile + 8, :] = jnp.zeros((8, D_CONV), F32)

        oc, oa = oc_ref[...], oa_ref[...]
        x2 = x_ref[...] + _nn(oc, wo_ref[0:D_CONV, :]) + _nn(oa, wo_ref[D_CONV:D_MIX, :])
        r = _rstd(x2)
        xhat = x2 * r
        err = xhat * gf_ref[...] - t_ref[...]
        loss_acc[...] += jnp.sum(err * err, axis=0, keepdims=True) * (0.5 / D_MODEL)
        dy = err * (1.0 / D_MODEL)
        small_add(SMALL_NORM_FINAL, dy * xhat)
        dx2 = _rms_bwd(dy * gf_ref[...], xhat, r)
        dx2_ref[...] = dx2
        db = dx2.astype(BF16)
        do = _nt(db, wo_ref[0:D_CONV, :])
        doa_ref[...] = _nt(db, wo_ref[D_CONV:D_MIX, :])
        gwo_ref[0:D_CONV, :] += _tn(oc, db)
        gwo_ref[D_CONV:D_MIX, :] += _tn(oa, db)

        is_first_tile = step == n_tiles - 1
        zbuf[0:8, :] = jnp.where(is_first_tile, 0.0, hcc_ref[...] * hcu_ref[...])
        cb, cc, cu, z, z1, z2, conv = _conv_core(pc_ref, zbuf, cw_ref)
        silu, dsilu = _silu_and_grad(pc_ref[:, 3 * D_CONV:4 * D_CONV])
        yc = cb * conv
        rc = _rstd(yc)
        chat = yc * rc
        dn = do * silu
        dpc_ref[:, 3 * D_CONV:4 * D_CONV] = (do * (chat * gn_ref[...]) * dsilu).astype(BF16)
        small_add(SMALL_NORM_CONV, dn * chat)
        dyc = _rms_bwd(dn * gn_ref[...], chat, rc)
        dpc_ref[:, 0:D_CONV] = (dyc * conv).astype(BF16)
        dconv = dyc * cb
        small_add(SMALL_CONV_W, dconv * z2)
        small_add(SMALL_CONV_W + 1, dconv * z1)
        small_add(SMALL_CONV_W + 2, dconv * z)
        dbuf[0:tile, :] = dconv
        dz = cw_ref[2:3, :] * dconv + cw_ref[1:2, :] * dbuf[1:tile + 1, :] + cw_ref[0:1, :] * dbuf[2:tile + 2, :]
        dpc_ref[:, D_CONV:2 * D_CONV] = (dz * cu).astype(BF16)
        dpc_ref[:, 2 * D_CONV:3 * D_CONV] = (dz * cc).astype(BF16)
        dbuf[tile:tile + 8, :] = dbuf[0:8, :]

        @pl.when(step == n_tiles - 1)
        def _():
            lane = lax.broadcasted_iota(jnp.int32, (1, D_MODEL), 1)
            small_ref[SMALL_MISC:SMALL_MISC + 1, :] = jnp.where(lane == SMALL_LOSS_LANE, jnp.sum(loss_acc[...]), 0.0)

    def rev(i):
        return n_tiles - 1 - i

    def row(width):
        return pl.BlockSpec((tile, width), lambda i: (rev(i), 0))

    def halo(col_block):
        return pl.BlockSpec((8, D_CONV), lambda i: (jnp.maximum(rev(i) * (tile // 8) - 1, 0), col_block))

    def const(shape):
        return pl.BlockSpec(shape, lambda i: (0, 0))

    return pl.pallas_call(
        body, name="out_proj_loss", grid=(n_tiles,),
        out_shape=(jax.ShapeDtypeStruct((seq, D_MODEL), F32), jax.ShapeDtypeStruct((seq, D_ATTN), F32),
                   jax.ShapeDtypeStruct((D_MIX, D_MODEL), F32), jax.ShapeDtypeStruct((seq, D_PC), BF16),
                   jax.ShapeDtypeStruct((SMALL_ROWS, D_MODEL), F32)),
        in_specs=[row(D_MODEL), row(D_CONV), row(D_ATTN), _resident(wo.shape), _resident((1, D_MODEL)), row(D_MODEL),
                  row(D_PC), halo(1), halo(2), _resident(conv_w.shape), _resident((1, D_CONV))],
        out_specs=(row(D_MODEL), row(D_ATTN), const((D_MIX, D_MODEL)), row(D_PC), const((SMALL_ROWS, D_MODEL))),
        scratch_shapes=[pltpu.VMEM((1, D_MODEL), F32), pltpu.VMEM((tile + 8, D_CONV), F32),
                        pltpu.VMEM((tile + 8, D_CONV), F32)],
        compiler_params=_params(62, ("arbitrary",)),
    )(x, oc, oa, wo, norm_final, target, pc, pc, pc, conv_w, norm_conv_out)


def _attn_bwd(q, kv, ga, ya, doa, sinks, norm_attn_out, gwo, small):
    seq = q.shape[0]
    n_blocks = seq // BLK
    stage_steps = (0, n_blocks // 4, n_blocks // 2, (3 * n_blocks) // 4, n_blocks - 1)

    def body(q_ref, kvc_ref, kvp_ref, ga_ref, ya_ref, doa_ref, sink_ref, gn_ref, gwo_ref, small_ref,
             dqg_ref, small_out, gwo_sh, gna_ref, gs_ref, carry, dya_buf, *rs_scratch):
        step = pl.program_id(0)

        @pl.when(step == 0)
        def _():
            gna_ref[...] = jnp.zeros_like(gna_ref)
            gs_ref[...] = jnp.zeros_like(gs_ref)
            carry[...] = jnp.zeros_like(carry)

        ya = ya_ref[...]
        r = _rstd(ya)
        xhat = ya * r
        silu, dsilu = _silu_and_grad(ga_ref[...])
        do = doa_ref[...]
        dn = do * silu
        dqg_ref[:, D_ATTN:2 * D_ATTN] = (do * (xhat * gn_ref[...]) * dsilu).astype(BF16)
        gna_ref[...] += jnp.sum(dn * xhat, axis=0, keepdims=True)
        dya_buf[...] = _rms_bwd(dn * gn_ref[...], xhat, r).astype(BF16)

        geometry = _band_geometry(n_blocks - 1 - step)
        use_cur = geometry[0]
        lane = lax.broadcasted_iota(jnp.int32, (BLK, 128), 1)
        sink_lane = lax.broadcasted_iota(jnp.int32, (1, 128), 1)
        gsink = jnp.zeros((1, 128), F32)

        def fold(bd):
            return (jnp.where(lane < 64, bd[0:BLK], 0.0) + jnp.where(lane >= 64, bd[BLK:2 * BLK], 0.0),
                    jnp.where(lane < 64, bd[2 * BLK:3 * BLK], 0.0) + jnp.where(lane >= 64, bd[3 * BLK:4 * BLK], 0.0))

        pairs = range(N_HEADS // 2)
        qps, kbd, vbd, probs, sink_probs = _attn_probs(q_ref, kvc_ref, kvp_ref, sink_ref, geometry)
        dyps = [dya_buf[:, j * 128:(j + 1) * 128] for j in pairs]
        dps = []
        for j in pairs:
            dps += _merge(_nt(dyps[j], vbd[j // 4]), use_cur)
        deltas = [jnp.sum(p * dp, axis=-1, keepdims=True) for p, dp in zip(probs, dps)]
        dss = [p * (dp - delta) for p, dp, delta in zip(probs, dps, deltas)]
        for h in range(N_HEADS):
            dsink = -jnp.sum(sink_probs[h] * deltas[h], axis=0, keepdims=True)
            gsink = gsink + jnp.where(sink_lane == h, dsink, 0.0)
        gs_ref[...] += gsink
        ds4s = [_split(dss[2 * j], dss[2 * j + 1], use_cur).astype(BF16) for j in pairs]
        p4s = [_split(probs[2 * j], probs[2 * j + 1], use_cur).astype(BF16) for j in pairs]
        dqg_ref[:, 0:D_ATTN] = jnp.concatenate([_nn(ds4s[j], kbd[j // 4]) * SCALE for j in pairs], axis=1).astype(BF16)
        sums = []
        for group in range(N_HEADS // HEADS_PER_KV):
            acc = [jnp.zeros((BLK, 128), F32) for _ in range(4)]
            for j in range(group * 4, group * 4 + 4):
                for slot, part in enumerate(fold(_tn(ds4s[j], qps[j])) + fold(_tn(p4s[j], dyps[j]))):
                    acc[slot] = acc[slot] + part
            sums.append([a + pltpu.roll(a, 64, 1) for a in acc])
        dk_cur, dk_prev, dv_cur, dv_prev = (jnp.where(lane < 64, a, b) for a, b in zip(sums[0], sums[1]))
        dqg_ref[:, 2 * D_ATTN:2 * D_ATTN + 2 * D_KV] = (jnp.concatenate([dk_cur, dv_cur], axis=1) + carry[...]).astype(BF16)
        carry[...] = jnp.concatenate([dk_prev, dv_prev], axis=1)

        @pl.when(step == n_blocks - 1)
        def _():
            small_out[...] = small_ref[...]
            small_out[SMALL_NORM_ATTN:SMALL_NORM_ATTN + 1, :] = gna_ref[...]
            small_out[SMALL_MISC:SMALL_MISC + 1, 0:128] = small_ref[SMALL_MISC:SMALL_MISC + 1, 0:128] + gs_ref[...]

        for at, stage in zip(stage_steps, _rs_wout_stages(gwo_ref, gwo_sh, *rs_scratch)):
            pl.when(step == at)(stage)

    row = pl.BlockSpec((BLK, D_ATTN), lambda i: (n_blocks - 1 - i, 0))
    kv_cur, kv_prev = _kv_specs(n_blocks, reverse=True)
    any_spec = pl.BlockSpec(memory_space=pl.ANY)
    return pl.pallas_call(
        body, name="attn_bwd", grid=(n_blocks,),
        out_shape=(jax.ShapeDtypeStruct((seq, D_QG), BF16), jax.ShapeDtypeStruct(small.shape, F32),
                   jax.ShapeDtypeStruct((gwo.shape[0] // N_CHIPS, gwo.shape[1]), F32)),
        in_specs=[row, kv_cur, kv_prev, row, row, row, pl.BlockSpec(memory_space=pltpu.SMEM), _resident((1, D_ATTN)),
                  any_spec, _resident(small.shape)],
        out_specs=(pl.BlockSpec((BLK, D_QG), lambda i: (n_blocks - 1 - i, 0)),
                   pl.BlockSpec(small.shape, lambda i: (0, 0)), any_spec),
        scratch_shapes=[pltpu.VMEM((1, D_ATTN), F32), pltpu.VMEM((1, 128), F32),
                        pltpu.VMEM((BLK, 2 * D_KV), F32), pltpu.VMEM((BLK, D_ATTN), BF16)] + _rs_wout_scratch(gwo.shape),
        compiler_params=_params(44, ("arbitrary",)),
    )(q, kv, kv, ga, ya, doa, sinks, norm_attn_out, gwo, small)


GBLK = 256
GSUB = 64
PAIR_RING = 4
LAG_PAIR, LAG_HOP1, LAG_HOP2 = 1, 7, 14


def _bwd_in(dpc, dqg, h, wt, x, norm_in, dx2, small):
    seq = x.shape[0]
    n_blk = D_IN_PROJ // GBLK
    per_chip = n_blk // N_CHIPS
    n_slots = (n_blk + 1) // 2
    n_sub = GBLK // GSUB
    chip_rows = D_IN_PROJ // N_CHIPS
    tile = TOK_TILE
    n_tiles = seq // tile
    n_steps = n_blk + max(n_tiles, LAG_HOP2)
    chunk = min(seq, 512)
    blk_q, blk_kv, blk_ga = ROW_Q // GBLK, ROW_KV // GBLK, ROW_GA // GBLK

    def block_of(i):
        k = i % N_CHIPS
        robin = per_chip * ((k % 2) * 2 + k // 2) + i // N_CHIPS
        if isinstance(i, int):
            return robin if i < per_chip * N_CHIPS else i
        return jnp.where(i < per_chip * N_CHIPS, robin, i)

    def owner_of(i):
        return (i // N_CHIPS) % 2

    def slot_of(i):
        return (i // (2 * N_CHIPS)) * N_CHIPS + i % N_CHIPS

    def body(dpc_ref, dqg_ref, wt_ref, h_ref, x_ref, g_ref, dx2_ref, small_ref, gx_ref, small_sum, gwt_sh,
             dh_acc, gni, keep, pbuf, xbuf, land, land2, small_land,
             pair_send, pair_recv, h1_send, h1_recv, h2_send, h2_recv, sw_send, sw_recv, sm_send, sm_recv, out_sem):
        step = pl.program_id(0)
        x_i, y_i, c = lax.axis_index("x"), lax.axis_index("y"), lax.axis_index("c")
        me = 4 * x_i + 2 * y_i + c
        j = 2 * x_i + y_i
        pa = (_xor(x_i, 1 - c), _xor(y_i, c), c)
        pb = (_xor(x_i, c), _xor(y_i, 1 - c), c)
        sib = (x_i, y_i, 1 - c)
        ja = 2 * pa[0] + pa[1]
        jb = 2 * pb[0] + pb[1]
        jd = 3 - j

        def remote(src, dst, send, recv, to):
            return pltpu.make_async_remote_copy(src_ref=src, dst_ref=dst, send_sem=send, recv_sem=recv,
                                                device_id=to, device_id_type=MESH)

        def piece(ref, slot, u, n):
            return ref.at[slot, pl.ds(u * GSUB, n * GSUB), :]

        def chip_rows_at(ref, local, n):
            return ref.at[pl.ds(pl.multiple_of(local, GSUB), n * GSUB), :]

        def pair_copy(i):
            slot = slot_of(i)
            return remote(pbuf.at[i % PAIR_RING], land.at[slot], pair_send.at[slot], pair_recv.at[slot], sib)

        def h1_copy(slot, u, n):
            k = slot * n_sub + u
            return remote(piece(xbuf, slot, u, n), piece(xbuf, slot, u, n), h1_send.at[k], h1_recv.at[k], pa)

        def h2_copy(slot, u, n, local):
            k = slot * n_sub + u
            return remote(piece(xbuf, slot, u, n), chip_rows_at(land2, local, n), h2_send.at[k], h2_recv.at[k], pb)

        def sw_copy(slot, u, n, local):
            k = slot * n_sub + u
            return remote(piece(keep, slot, u, n), chip_rows_at(gwt_sh, local, n), sw_send.at[k], sw_recv.at[k], sib)

        def owned(i):
            return (i >= 0) & (i < n_blk) & (owner_of(i) == c)

        def chip_of(blk, u):
            row = blk * GBLK + u * GSUB
            chip = row // chip_rows
            return chip, row - chip * chip_rows

        def pieces(blk):
            first, local = chip_of(blk, 0)
            whole = first == chip_of(blk, n_sub - 1)[0]
            if isinstance(blk, int):
                return [(True, 0, n_sub, first, local)] if whole else [(True, u, 1) + chip_of(blk, u) for u in range(n_sub)]
            return [(whole, 0, n_sub, first, local)] + [(jnp.logical_not(whole), u, 1) + chip_of(blk, u) for u in range(n_sub)]

        @pl.when(step == 0)
        def _():
            dh_acc[...] = jnp.zeros_like(dh_acc)
            gni[...] = jnp.zeros_like(gni)

        @pl.when(step < n_blk)
        def _():
            from_pc = block_of(step) < blk_q
            block = _tn(jnp.where(from_pc, dpc_ref[...], dqg_ref[...]), h_ref[...])
            for t in range(0, seq, chunk):
                d = jnp.where(from_pc, dpc_ref[t:t + chunk, :], dqg_ref[t:t + chunk, :])
                dh_acc[t:t + chunk, :] += _nn(d, wt_ref[...])

            @pl.when(owner_of(step) == c)
            def _():
                keep[slot_of(step)] = block

            @pl.when(owner_of(step) != c)
            def _():
                @pl.when(step >= 2 * PAIR_RING)
                def _():
                    pair_copy(step - 2 * PAIR_RING).wait_send()
                pbuf[step % PAIR_RING] = block.astype(BF16)
                pair_copy(step).start()

        i1 = step - LAG_PAIR

        @pl.when(owned(i1))
        def _():
            slot = slot_of(i1)
            pair_copy(i1).wait_recv()
            _accumulate(keep.at[slot], land.at[slot])
            for cond, u, n, chip, _ in pieces(block_of(i1)):
                @pl.when(cond & ((chip == ja) | (chip == jd)))
                def _(u=u, n=n):
                    _cast_rows(piece(keep, slot, u, n), piece(xbuf, slot, u, n))
                    h1_copy(slot, u, n).start()

        i2 = step - LAG_HOP1

        @pl.when(owned(i2))
        def _():
            slot = slot_of(i2)
            for cond, u, n, chip, local in pieces(block_of(i2)):
                @pl.when(cond & ((chip == j) | (chip == jb)))
                def _(u=u, n=n, chip=chip, local=local):
                    h1_copy(slot, u, n).wait_recv()
                    _accumulate(piece(keep, slot, u, n), piece(xbuf, slot, u, n))

                    @pl.when(chip == jb)
                    def _():
                        _cast_rows(piece(keep, slot, u, n), piece(xbuf, slot, u, n))
                        h2_copy(slot, u, n, local).start()

                @pl.when(cond & ((chip == ja) | (chip == jd)))
                def _(u=u, n=n):
                    h1_copy(slot, u, n).wait_send()

        i3 = step - LAG_HOP2

        @pl.when(owned(i3))
        def _():
            slot = slot_of(i3)
            for cond, u, n, chip, local in pieces(block_of(i3)):
                @pl.when(cond & (chip == j))
                def _(u=u, n=n, local=local):
                    h2_copy(slot, u, n, local).wait_recv()
                    _accumulate(piece(keep, slot, u, n), chip_rows_at(land2, local, n))
                    mine = pltpu.make_async_copy(piece(keep, slot, u, n), chip_rows_at(gwt_sh, local, n), out_sem.at[0])
                    mine.start()
                    sw_copy(slot, u, n, local).start()
                    mine.wait()

                @pl.when(cond & (chip == jb))
                def _(u=u, n=n, local=local):
                    h2_copy(slot, u, n, local).wait_send()

        e = step - n_blk

        @pl.when((e >= 0) & (e < n_tiles))
        def _():
            dh = dh_acc[pl.ds(pl.multiple_of(e * tile, tile), tile), :]
            xv = x_ref[...]
            r = _rstd(xv)
            xhat = xv * r
            gni[...] += jnp.sum(dh * xhat, axis=0, keepdims=True)
            gx_ref[...] = _rms_bwd(dh * g_ref[...], xhat, r) + dx2_ref[...]

        @pl.when(step == n_steps - 1)
        def _():
            small_land[me] = small_ref[...]
            small_land[me, SMALL_NORM_IN:SMALL_NORM_IN + 1, :] = gni[...]
            others = [(dx, dy, dc) for dx in (0, 1) for dy in (0, 1) for dc in (0, 1)][1:]
            sends = [remote(small_land.at[me], small_land.at[me], sm_send.at[k], sm_recv.at[k],
                            (_xor(x_i, dx), _xor(y_i, dy), _xor(c, dc))) for k, (dx, dy, dc) in enumerate(others)]
            for cp in sends:
                cp.start()
            for i in range(n_blk):
                if i + 2 * PAIR_RING >= n_blk:
                    @pl.when(owner_of(i) != c)
                    def _(i=i):
                        pair_copy(i).wait_send()
                for _, u, n, chip, local in pieces(block_of(i)):
                    @pl.when((j == chip) & (c == owner_of(i)))
                    def _(i=i, u=u, n=n, local=local):
                        sw_copy(slot_of(i), u, n, local).wait_send()

                    @pl.when((j == chip) & (c != owner_of(i)))
                    def _(i=i, u=u, n=n, local=local):
                        sw_copy(slot_of(i), u, n, local).wait_recv()
            for cp in sends:
                cp.wait_recv()
            total = small_land[0]
            for dev in range(1, 8):
                total = total + small_land[dev]
            small_sum[...] = total
            for cp in sends:
                cp.wait_send()

    def blk_at(i):
        return block_of(jnp.clip(i, 0, n_blk - 1))

    last_pc_step = max(i for i in range(n_blk) if block_of(i) < blk_q)

    def next_block(i, in_pc):
        i = jnp.clip(i, 0, n_blk - 1)
        step = jnp.full_like(i, last_pc_step if in_pc else n_blk - 1)
        for ahead in reversed(range(N_CHIPS)):
            cand = jnp.minimum(i + ahead, n_blk - 1)
            step = jnp.where((block_of(cand) < blk_q) == in_pc, cand, step)
        return block_of(step)

    def dqg_block(i):
        b = next_block(i, False)
        q_blk = jnp.clip(b - blk_q, 0, blk_kv - blk_q - 1)
        ga_blk = (D_ATTN // GBLK) + jnp.clip(b - blk_ga, 0, n_blk - blk_ga - 1)
        return jnp.where(b < blk_kv, q_blk, jnp.where(b == blk_kv, 2 * D_ATTN // GBLK, ga_blk))

    def tok(i):
        return (jnp.clip(i - n_blk, 0, n_tiles - 1), 0)

    n_piece = n_slots * n_sub
    dma = pltpu.SemaphoreType.DMA
    return pl.pallas_call(
        body, name="bwd_in", grid=(n_steps,),
        out_shape=(jax.ShapeDtypeStruct((seq, D_MODEL), F32), jax.ShapeDtypeStruct(small.shape, F32),
                   jax.ShapeDtypeStruct((chip_rows, D_MODEL), F32)),
        in_specs=[pl.BlockSpec((seq, GBLK), lambda i: (0, next_block(i, True))),
                  pl.BlockSpec((seq, GBLK), lambda i: (0, dqg_block(i))),
                  pl.BlockSpec((GBLK, D_MODEL), lambda i: (blk_at(i), 0)),
                  _resident(h.shape),
                  pl.BlockSpec((tile, D_MODEL), tok), _resident((1, D_MODEL)), pl.BlockSpec((tile, D_MODEL), tok),
                  _resident(small.shape)],
        out_specs=(pl.BlockSpec((tile, D_MODEL), tok), pl.BlockSpec(small.shape, lambda i: (0, 0)),
                   pl.BlockSpec(memory_space=pl.ANY)),
        scratch_shapes=[pltpu.VMEM((seq, D_MODEL), F32), pltpu.VMEM((1, D_MODEL), F32),
                        pltpu.VMEM((n_slots, GBLK, D_MODEL), F32), pltpu.VMEM((PAIR_RING, GBLK, D_MODEL), BF16),
                        pltpu.VMEM((n_slots, GBLK, D_MODEL), BF16), pltpu.VMEM((n_slots, GBLK, D_MODEL), BF16),
                        pltpu.VMEM((chip_rows, D_MODEL), BF16), pltpu.VMEM((8,) + small.shape, F32),
                        dma((n_slots,)), dma((n_slots,)), dma((n_piece,)), dma((n_piece,)), dma((n_piece,)),
                        dma((n_piece,)), dma((n_piece,)), dma((n_piece,)), dma((7,)), dma((7,)), dma((1,))],
        compiler_params=_params(62, ("arbitrary",)),
    )(dpc, dqg, wt, h, x, norm_in, dx2, small)


def _accumulate(dst_ref, src_ref, rows=16):
    def step(i, carry):
        sl = pl.ds(pl.multiple_of(i * rows, rows), rows)
        dst_ref[sl, :] = dst_ref[sl, :] + src_ref[sl, :].astype(F32)
        return carry
    lax.fori_loop(0, dst_ref.shape[0] // rows, step, 0)


def _rs_wout_scratch(gwo_shape):
    o_half, width = gwo_shape[0] // N_CHIPS // 2, gwo_shape[1]
    return [pltpu.VMEM((4, o_half, width), F32), pltpu.VMEM((4, o_half, width), BF16),
            pltpu.VMEM((4, o_half, width), BF16), pltpu.VMEM((2, o_half, width), BF16),
            pltpu.VMEM((o_half, width), BF16),
            pltpu.SemaphoreType.DMA((8,)), pltpu.SemaphoreType.DMA((8,)), pltpu.SemaphoreType.DMA((4,))]


def _rs_wout_stages(gwo_ref, gwo_sh, acc_o, sb_o, r1o, r2o, r3o, send_sems, recv_sems, local_sems):
    o_rows = gwo_ref.shape[0] // N_CHIPS
    o_half = o_rows // 2
    x, y, c = lax.axis_index("x"), lax.axis_index("y"), lax.axis_index("c")
    j = 2 * x + y
    pa = (_xor(x, 1 - c), _xor(y, c), c)
    pb = (_xor(x, c), _xor(y, 1 - c), c)
    sib = (x, y, 1 - c)
    ja = 2 * pa[0] + pa[1]
    jb = 2 * pb[0] + pb[1]
    jd = 3 - j
    order = (ja, jd, jb, j)
    sib_order = (jb, jd, ja, j)

    def rcopy(k, src, dst, to):
        return pltpu.make_async_remote_copy(src_ref=src, dst_ref=dst, send_sem=send_sems.at[k],
                                            recv_sem=recv_sems.at[k], device_id=to, device_id_type=MESH)

    def o_rows_of(chip, half):
        return gwo_ref.at[pl.ds(pl.multiple_of(chip * o_rows + half * o_half, 8), o_half), :]

    def load_all(chips, half):
        cps = [pltpu.make_async_copy(o_rows_of(chip, half), acc_o.at[s], local_sems.at[s]) for s, chip in enumerate(chips)]
        for cp in cps:
            cp.start()
        return cps

    def pair_send(s):
        return rcopy(s, sb_o.at[s], r1o.at[s], sib)

    def out_half(half):
        return gwo_sh.at[pl.ds(pl.multiple_of(half * o_half, 8), o_half), :]

    hop1 = [rcopy(4 + s, sb_o.at[s], r2o.at[s], pa) for s in range(2)]
    hop2 = rcopy(6, sb_o.at[2], r3o, pb)
    swap = rcopy(7, acc_o.at[3], out_half(c), sib)
    mine = pltpu.make_async_copy(acc_o.at[3], out_half(c), local_sems.at[0])

    def resend(s, copy):
        pair_send(s).wait_send()
        _cast_rows(acc_o.at[s], sb_o.at[s])
        copy.start()

    def stage_pair():
        for s, cp in enumerate(load_all(sib_order, 1 - c)):
            cp.wait()
            _cast_rows(acc_o.at[s], sb_o.at[s])
            pair_send(s).start()

    def stage_hop1():
        for s, cp in enumerate(load_all(order, c)):
            cp.wait()
            pair_send(s).wait_recv()
            _accumulate(acc_o.at[s], r1o.at[s])
            if s < 2:
                resend(s, hop1[s])

    def stage_hop2():
        hop1[1].wait_recv()
        _accumulate(acc_o.at[2], r2o.at[1])
        resend(2, hop2)
        hop1[0].wait_recv()
        _accumulate(acc_o.at[3], r2o.at[0])

    def stage_final():
        hop2.wait_recv()
        _accumulate(acc_o.at[3], r3o)
        mine.start()
        swap.start()

    def stage_drain():
        rcopy(7, acc_o.at[3], out_half(1 - c), sib).wait_recv()
        for cp in [pair_send(3)] + hop1 + [hop2, swap]:
            cp.wait_send()
        mine.wait()

    return stage_pair, stage_hop1, stage_hop2, stage_final, stage_drain


def _adamw(name, w, g, m, v, rows):
    def body(w_ref, g_ref, m_ref, v_ref, go_ref, d_ref, nm_ref, nv_ref):
        _adamw_update(w_ref, g_ref, m_ref, v_ref, go_ref, d_ref, nm_ref, nv_ref)

    spec = pl.BlockSpec((rows, w.shape[1]), lambda i: (i, 0))
    shape = jax.ShapeDtypeStruct(w.shape, F32)
    return pl.pallas_call(
        body, name="adamw_" + name, grid=(w.shape[0] // rows,),
        out_shape=(shape,) * 4, in_specs=[spec] * 4, out_specs=(spec,) * 4,
        compiler_params=_params(32, ("arbitrary",)),
    )(w, g, m, v)


def _adamw_update(w_ref, g_ref, m_ref, v_ref, go_ref, d_ref, nm_ref, nv_ref):
    gv = g_ref[...]
    go_ref[...] = gv
    nm = ADAM_B1 * m_ref[...] + (1.0 - ADAM_B1) * gv
    nv = ADAM_B2 * v_ref[...] + (1.0 - ADAM_B2) * (gv * gv)
    m_hat = nm / (1.0 - ADAM_B1 ** ADAM_STEP)
    v_hat = nv / (1.0 - ADAM_B2 ** ADAM_STEP)
    d_ref[...] = -ADAM_LR * (m_hat / (jnp.sqrt(v_hat) + ADAM_EPS) + ADAM_WD * w_ref[...])
    nm_ref[...] = nm
    nv_ref[...] = nv


def _adamw_small(weights, grads, ms, vs):
    n = len(weights)

    def body(*refs):
        ins, outs = refs[:4 * n], refs[4 * n:]
        for k in range(n):
            _adamw_update(*ins[4 * k:4 * k + 4], *outs[4 * k:4 * k + 4])

    flat = [a for group in zip(weights, grads, ms, vs) for a in group]
    vmem = pl.BlockSpec(memory_space=pltpu.VMEM)
    out = pl.pallas_call(
        body, name="adamw_small",
        out_shape=tuple(jax.ShapeDtypeStruct(w.shape, F32) for w in weights for _ in range(4)),
        in_specs=[vmem] * (4 * n), out_specs=(vmem,) * (4 * n),
    )(*flat)
    return [tuple(out[4 * k:4 * k + 4]) for k in range(n)]


def kernel(x, norm_in, w_in, conv_w, attn_sinks, norm_conv_out, norm_attn_out, w_out, norm_final, loss_target, m_norm_in, m_w_in, m_conv_w, m_attn_sinks, m_norm_conv_out, m_norm_attn_out, m_w_out, m_norm_final, v_norm_in, v_w_in, v_conv_w, v_attn_sinks, v_norm_conv_out, v_norm_attn_out, v_w_out, v_norm_final):
    chip = 2 * lax.axis_index("x") + lax.axis_index("y")
    xs, target = x[0], loss_target[0]
    norm_final2 = norm_final.reshape(1, D_MODEL)
    w_in_t, m_w_in_t, v_w_in_t = w_in[0].T, m_w_in[0].T, v_w_in[0].T

    wt, cw4 = _ag_weights(w_in_t, conv_w[0])
    cw = jnp.transpose(cw4, (1, 0, 2)).reshape(3, D_CONV)

    h, pc, q, kv, ga, oc, ya, oa, wo = _fwd_in(xs, norm_in, wt, w_out[0], cw, norm_conv_out, attn_sinks, norm_attn_out)
    dx2, doa, gwo, dpc, small = _out_proj_loss(xs, oc, oa, wo, norm_final2, target, pc, cw, norm_conv_out)
    dqg, small, gwo_sh = _attn_bwd(q, kv, ga, ya, doa, attn_sinks, norm_attn_out, gwo, small)
    grad_x, small_sum, gwt_sh = _bwd_in(dpc, dqg, h, wt, xs, norm_in, dx2, small)

    loss = small_sum[SMALL_MISC, SMALL_LOSS_LANE]
    g_norm_in, g_norm_conv, g_norm_attn = (small_sum[r:r + 1] for r in (SMALL_NORM_IN, SMALL_NORM_CONV, SMALL_NORM_ATTN))
    g_norm_final = small_sum[SMALL_NORM_FINAL]
    g_conv_w = lax.dynamic_slice(small_sum[SMALL_CONV_W:SMALL_CONV_W + 3], (0, chip * (D_CONV // N_CHIPS)),
                                 (3, D_CONV // N_CHIPS))[None]
    g_sinks = small_sum[SMALL_MISC:SMALL_MISC + 1, 0:N_HEADS]

    def two_d(a):
        return a.reshape(-1, a.shape[-1])

    def from_hbm(*arrays):
        return tuple(pltpu.with_memory_space_constraint(a, pltpu.HBM) for a in arrays)

    up_w_in = tuple(o.T[None] for o in _adamw("w_in", *from_hbm(w_in_t, gwt_sh, m_w_in_t, v_w_in_t), 200))
    up_w_out = tuple(o[None] for o in _adamw("w_out", *from_hbm(w_out[0], gwo_sh, m_w_out[0], v_w_out[0]), 128))
    small_w = (norm_in, conv_w, attn_sinks, norm_conv_out, norm_attn_out, norm_final)
    small_g = (g_norm_in, g_conv_w, g_sinks, g_norm_conv, g_norm_attn, g_norm_final)
    small_m = (m_norm_in, m_conv_w, m_attn_sinks, m_norm_conv_out, m_norm_attn_out, m_norm_final)
    small_v = (v_norm_in, v_conv_w, v_attn_sinks, v_norm_conv_out, v_norm_attn_out, v_norm_final)
    up_small = _adamw_small(*(tuple(two_d(a) for a in group) for group in (small_w, small_g, small_m, small_v)))
    up_small = [tuple(o.reshape(w.shape) for o in up) for up, w in zip(up_small, small_w)]
    up_norm_in, up_conv_w, up_sinks, up_norm_conv, up_norm_attn, up_norm_final = up_small
    updates = (up_norm_in, up_w_in, up_conv_w, up_sinks, up_norm_conv, up_norm_attn, up_w_out, up_norm_final)
    grads_out, deltas, new_m, new_v = zip(*updates)
    return (loss, grad_x[None], *grads_out, *deltas, *new_m, *new_v)
```

```python
import jax
import jax.numpy as jnp
from jax import lax
from jax.experimental import pallas as pl
from jax.experimental.pallas import tpu as pltpu

F32 = jnp.float32
BF16 = jnp.bfloat16
MESH = pl.DeviceIdType.MESH

D_MODEL = 1024
D_CONV = 1024
D_ATTN = 1024
D_KV = 128
D_QG = 2 * D_ATTN + 2 * D_KV
D_MIX = D_CONV + D_ATTN
D_PC = 4 * D_CONV
D_IN_PROJ = D_PC + 2 * D_ATTN + 2 * D_KV
ROW_Q = D_PC
ROW_KV = ROW_Q + D_ATTN
ROW_GA = ROW_KV + 2 * D_KV
N_HEADS = 16
HEAD_DIM = 64
HEADS_PER_KV = 8
BLK = 128
N_CHIPS = 4
RMS_EPS = 1e-5
SCALE = HEAD_DIM ** -0.5
SLOPES = tuple(2.0 ** (-8.0 * (h + 1) / N_HEADS) for h in range(N_HEADS))

ADAM_LR, ADAM_B1, ADAM_B2, ADAM_EPS, ADAM_WD, ADAM_STEP = 0.001, 0.9, 0.999, 1e-08, 0.01, 10

SMALL_ROWS = 8
SMALL_NORM_IN, SMALL_NORM_CONV, SMALL_NORM_ATTN, SMALL_NORM_FINAL, SMALL_CONV_W, SMALL_MISC = 0, 1, 2, 3, 4, 7
SMALL_LOSS_LANE = N_HEADS

TOK_TILE = 256
MIB = 1 << 20


def _params(vmem_mib, semantics=None):
    return pltpu.CompilerParams(dimension_semantics=semantics, vmem_limit_bytes=vmem_mib * MIB)


def _nn(a, b):
    return jnp.dot(a, b, preferred_element_type=F32)


def _nt(a, b):
    return lax.dot_general(a, b, (((1,), (1,)), ((), ())), preferred_element_type=F32)


def _tn(a, b):
    return lax.dot_general(a, b, (((0,), (0,)), ((), ())), preferred_element_type=F32)


def _rstd(v):
    return lax.rsqrt(jnp.mean(v * v, axis=-1, keepdims=True) + RMS_EPS)


def _rms_bwd(g, xhat, rstd):
    return rstd * (g - xhat * jnp.mean(g * xhat, axis=-1, keepdims=True))


def _silu_and_grad(g):
    s = jax.nn.sigmoid(g)
    return g * s, s * (1.0 + g * (1.0 - s))


def _resident(shape):
    return pl.BlockSpec(shape, lambda *_: (0,) * len(shape), pipeline_mode=pl.Buffered(1))


def _xor(a, b):
    return a + b - 2 * a * b


def _cast_rows(src_ref, dst_ref, rows=32):
    def step(i, carry):
        sl = pl.ds(pl.multiple_of(i * rows, rows), rows)
        dst_ref[sl, :] = src_ref[sl, :].astype(dst_ref.dtype)
        return carry
    lax.fori_loop(0, src_ref.shape[0] // rows, step, 0)


def _ag_scratch(shard_shape):
    rows, width = shard_shape
    return [pltpu.VMEM((rows, width), F32), pltpu.VMEM((rows, width), BF16), pltpu.VMEM((3, rows // 2, width), BF16),
            pltpu.SemaphoreType.DMA((6,)), pltpu.SemaphoreType.DMA((6,)), pltpu.SemaphoreType.DMA((4,))]


def _ag_stages(sh_ref, out, f32_buf, own, land, send_sems, recv_sems, local_sems):
    rows = sh_ref.shape[0]
    half = rows // 2
    x, y, c = lax.axis_index("x"), lax.axis_index("y"), lax.axis_index("c")
    j = 2 * x + y
    p1 = (_xor(x, c), _xor(y, 1 - c), c)
    p2 = (_xor(x, 1 - c), _xor(y, c), c)
    sib = (x, y, 1 - c)
    j1 = 2 * p1[0] + p1[1]
    j2 = 2 * p2[0] + p2[1]
    j3 = 3 - j

    def rows_of(chip, hf):
        return out.at[pl.ds(pl.multiple_of(chip * rows + hf * half, 16), half), :]

    def rcopy(k, src, dst, to):
        return pltpu.make_async_remote_copy(src_ref=src, dst_ref=dst, send_sem=send_sems.at[k],
                                            recv_sem=recv_sems.at[k], device_id=to, device_id_type=MESH)

    my_half = own.at[pl.ds(pl.multiple_of(c * half, 16), half), :]
    hop1 = rcopy(0, my_half, land.at[0], p1)
    hop2_own = rcopy(1, my_half, land.at[1], p2)
    hop2_fwd = rcopy(2, land.at[0], land.at[2], p2)
    swaps = [rcopy(3 + s, land.at[s], rows_of(chip, c), sib) for s, chip in enumerate((j1, j2, j3))]
    keeps = [pltpu.make_async_copy(land.at[s], rows_of(chip, c), local_sems.at[1 + s]) for s, chip in enumerate((j1, j2, j3))]
    load = pltpu.make_async_copy(sh_ref, f32_buf, local_sems.at[0])
    own_out = pltpu.make_async_copy(own, out.at[pl.ds(pl.multiple_of(j * rows, 16), rows), :], local_sems.at[0])

    def stage_send():
        load.start()
        load.wait()
        _cast_rows(f32_buf, own)
        own_out.start()
        hop1.start()

    def stage_forward():
        hop1.wait_recv()
        hop2_own.start()
        hop2_fwd.start()
        swaps[0].start()
        keeps[0].start()

    def stage_publish():
        hop2_own.wait_recv()
        swaps[1].start()
        keeps[1].start()
        hop2_fwd.wait_recv()
        swaps[2].start()
        keeps[2].start()

    def stage_drain():
        for s, chip in enumerate((j2, j1, j3)):
            rcopy(3 + s, my_half, rows_of(chip, 1 - c), sib).wait_recv()
        for cp in [hop1, hop2_own, hop2_fwd] + swaps:
            cp.wait_send()
        for cp in [own_out] + keeps:
            cp.wait()

    return stage_send, stage_forward, stage_publish, stage_drain


def _ag_weights(wt_sh, cw_sh):
    def body(wt_ref, cw_ref, wt_out, cw_out, *scratch):
        cw_send, cw_recv, cw_local = scratch[-3:]
        x, y, c = lax.axis_index("x"), lax.axis_index("y"), lax.axis_index("c")
        j = 2 * x + y
        p1 = (_xor(x, c), _xor(y, 1 - c), c)
        p2 = (_xor(x, 1 - c), _xor(y, c), c)
        j1 = 2 * p1[0] + p1[1]

        def cw_copy(k, chip, to):
            src = cw_ref if k != 2 else cw_out.at[chip]
            return pltpu.make_async_remote_copy(src_ref=src, dst_ref=cw_out.at[chip], send_sem=cw_send.at[k],
                                                recv_sem=cw_recv.at[k], device_id=to, device_id_type=MESH)

        mine = pltpu.make_async_copy(cw_ref, cw_out.at[j], cw_local.at[0])
        mine.start()
        first = cw_copy(0, j, p1)
        first.start()
        stages = _ag_stages(wt_ref, wt_out, *scratch[:-3])
        stages[0]()
        first.wait_recv()
        second = [cw_copy(1, j, p2), cw_copy(2, j1, p2)]
        for cp in second:
            cp.start()
        for stage in stages[1:]:
            stage()
        for cp in second:
            cp.wait_recv()
        for cp in [first] + second:
            cp.wait_send()
        mine.wait()

    any_spec = pl.BlockSpec(memory_space=pl.ANY)
    dma = pltpu.SemaphoreType.DMA
    return pl.pallas_call(
        body, name="ag_weights",
        out_shape=(pltpu.HBM((N_CHIPS * wt_sh.shape[0], wt_sh.shape[1]), BF16),
                   jax.ShapeDtypeStruct((N_CHIPS,) + cw_sh.shape, cw_sh.dtype)),
        in_specs=[any_spec, any_spec], out_specs=(any_spec, any_spec),
        scratch_shapes=_ag_scratch(wt_sh.shape) + [dma((3,)), dma((3,)), dma((1,))],
        compiler_params=_params(32),
    )(wt_sh, cw_sh)


def _fwd_in(x, norm_in, wt, wo_sh, conv_w, norm_conv_out, sinks, norm_attn_out):
    seq = x.shape[0]
    tile = TOK_TILE
    n_tiles = seq // tile
    stage_steps = (0, n_tiles // 4, (5 * n_tiles) // 8, n_tiles - 1)
    blocks = tile // BLK

    def body(x_ref, g_ref, wt_ref, wo_ref, cw_ref, gn_ref, sink_ref, gna_ref,
             h_ref, pc_ref, q_ref, kv_ref, ga_ref, oc_ref, ya_ref, oa_ref, pr_ref, sp_ref, wo_out,
             zbuf, kv_last, *ag_scratch):
        step = pl.program_id(0)
        stages = _ag_stages(wo_ref, wo_out, *ag_scratch)
        for at, stage in zip(stage_steps[:-1], stages[:-1]):
            pl.when(step == at)(stage)

        @pl.when(step == 0)
        def _():
            zbuf[0:8, :] = jnp.zeros((8, D_CONV), F32)
            kv_last[...] = jnp.zeros_like(kv_last)

        def attention(b):
            rows = pl.ds(b * BLK, BLK)
            kv_prev = kv_last if b == 0 else kv_ref.at[pl.ds((b - 1) * BLK, BLK), :]
            geometry = _band_geometry(step * blocks + b)
            _, _, vbd, probs, sink_probs = _attn_probs(q_ref.at[rows, :], kv_ref.at[rows, :], kv_prev, sink_ref, geometry)
            pr_ref[rows, :] = jnp.concatenate(probs, axis=1).astype(BF16)
            lane = lax.broadcasted_iota(jnp.int32, (BLK, 128), 1)
            sp_ref[rows, :] = sum(jnp.where(lane == hd, sp, 0.0) for hd, sp in enumerate(sink_probs))
            p4s = [_split(probs[2 * j], probs[2 * j + 1], geometry[0]).astype(BF16) for j in range(N_HEADS // 2)]
            ya = jnp.concatenate([_nn(p4, vbd[j // 4]) for j, p4 in enumerate(p4s)], axis=1)
            ya_ref[rows, :] = ya
            silu, _ = _silu_and_grad(ga_ref[rows, :])
            oa_ref[rows, :] = (ya * _rstd(ya) * gna_ref[...] * silu).astype(BF16)

        xv = x_ref[...]
        h = (xv * _rstd(xv) * g_ref[...]).astype(BF16)
        h_ref[...] = h
        q_ref[...] = _nt(h, wt_ref[ROW_Q:ROW_KV, :])
        kv_ref[...] = _nt(h, wt_ref[ROW_KV:ROW_GA, :])
        ga_ref[...] = _nt(h, wt_ref[ROW_GA:D_IN_PROJ, :])
        for blk in range(D_PC // D_CONV):
            pc_ref[:, blk * D_CONV:(blk + 1) * D_CONV] = _nt(h, wt_ref[blk * D_CONV:(blk + 1) * D_CONV, :])
            if blk < blocks:
                attention(blk)
        kv_last[...] = kv_ref[tile - BLK:tile, :]

        cb, _, _, _, _, _, conv = _conv_core(pc_ref, zbuf, cw_ref)
        yc = cb * conv
        silu, _ = _silu_and_grad(pc_ref[:, 3 * D_CONV:4 * D_CONV])
        oc_ref[...] = (yc * _rstd(yc) * gn_ref[...] * silu).astype(BF16)
        zbuf[0:8, :] = zbuf[tile:tile + 8, :]
        pl.when(pl.program_id(0) == stage_steps[-1])(stages[-1])

    def row(width):
        return pl.BlockSpec((tile, width), lambda i: (i, 0))

    any_spec = pl.BlockSpec(memory_space=pl.ANY)
    return pl.pallas_call(
        body, name="fwd_in", grid=(n_tiles,),
        out_shape=(jax.ShapeDtypeStruct((seq, D_MODEL), BF16), jax.ShapeDtypeStruct((seq, D_PC), F32),
                   jax.ShapeDtypeStruct((seq, D_ATTN), F32), jax.ShapeDtypeStruct((seq, 2 * D_KV), F32),
                   jax.ShapeDtypeStruct((seq, D_ATTN), F32), jax.ShapeDtypeStruct((seq, D_CONV), BF16),
                   pltpu.HBM((seq, D_ATTN), F32), pltpu.HBM((seq, D_ATTN), BF16),
                   pltpu.HBM((seq, N_HEADS * BLK), BF16), pltpu.HBM((seq, 128), F32),
                   jax.ShapeDtypeStruct((N_CHIPS * wo_sh.shape[0], wo_sh.shape[1]), BF16)),
        in_specs=[row(D_MODEL), _resident((1, D_MODEL)), _resident(wt.shape), any_spec,
                  _resident(conv_w.shape), _resident((1, D_CONV)), pl.BlockSpec(memory_space=pltpu.SMEM),
                  _resident((1, D_ATTN))],
        out_specs=(row(D_MODEL), row(D_PC), row(D_ATTN), row(2 * D_KV), row(D_ATTN), row(D_CONV), row(D_ATTN),
                   row(D_ATTN), row(N_HEADS * BLK), row(128), any_spec),
        scratch_shapes=[pltpu.VMEM((tile + 8, D_CONV), F32), pltpu.VMEM((BLK, 2 * D_KV), F32)] + _ag_scratch(wo_sh.shape),
        compiler_params=_params(60, ("arbitrary",)),
    )(x, norm_in, wt, wo_sh, conv_w, norm_conv_out, sinks, norm_attn_out)


def _conv_core(pc_ref, zbuf, cw_ref):
    tile = pc_ref.shape[0]
    cb = pc_ref[:, 0:D_CONV]
    cc = pc_ref[:, D_CONV:2 * D_CONV]
    cu = pc_ref[:, 2 * D_CONV:3 * D_CONV]
    z = cc * cu
    zbuf[8:tile + 8, :] = z
    z1 = zbuf[7:tile + 7, :]
    z2 = zbuf[6:tile + 6, :]
    conv = cw_ref[0:1, :] * z2 + cw_ref[1:2, :] * z1 + cw_ref[2:3, :] * z
    return cb, cc, cu, z, z1, z2, conv


def _band_geometry(block_index):
    qi = lax.broadcasted_iota(jnp.int32, (BLK, BLK), 0)
    kp = lax.broadcasted_iota(jnp.int32, (BLK, BLK), 1)
    use_cur = kp <= qi
    dist = jnp.where(use_cur, qi - kp, qi - kp + BLK).astype(F32)
    valid = use_cur | (block_index > 0)
    return use_cur, dist, valid


def _block_diag(cur, prev, group):
    lane = lax.broadcasted_iota(jnp.int32, cur.shape, 1)

    def halves(t):
        other = pltpu.roll(t, 64, 1)
        lo, hi = (t, other) if group == 0 else (other, t)
        return jnp.where(lane < 64, lo, 0.0), jnp.where(lane >= 64, hi, 0.0)

    return jnp.concatenate(halves(cur) + halves(prev), axis=0).astype(BF16)


def _merge(s4, use_cur):
    return (jnp.where(use_cur, s4[:, 0:BLK], s4[:, 2 * BLK:3 * BLK]),
            jnp.where(use_cur, s4[:, BLK:2 * BLK], s4[:, 3 * BLK:4 * BLK]))


def _split(a, b, use_cur):
    return jnp.concatenate([jnp.where(use_cur, a, 0.0), jnp.where(use_cur, b, 0.0),
                            jnp.where(use_cur, 0.0, a), jnp.where(use_cur, 0.0, b)], axis=1)


def _softmax_head(s, head, sink, dist, valid):
    sc = jnp.where(valid, s - SLOPES[head] * dist, -jnp.inf)
    m = jnp.maximum(jnp.max(sc, axis=-1, keepdims=True), sink)
    p = jnp.exp(sc - m)
    es = jnp.exp(sink - m)
    inv = 1.0 / (jnp.sum(p, axis=-1, keepdims=True) + es)
    return p * inv, es * inv


def _attn_operands(q_ref, kvc_ref, kvp_ref):
    groups = range(N_HEADS // HEADS_PER_KV)
    kbd = [_block_diag(kvc_ref[:, 0:D_KV], kvp_ref[:, 0:D_KV], g) for g in groups]
    vbd = [_block_diag(kvc_ref[:, D_KV:2 * D_KV], kvp_ref[:, D_KV:2 * D_KV], g) for g in groups]
    qps = [(q_ref[:, j * 128:(j + 1) * 128] * SCALE).astype(BF16) for j in range(N_HEADS // 2)]
    return qps, kbd, vbd


def _attn_probs(q_ref, kvc_ref, kvp_ref, sink_ref, geometry):
    use_cur, dist, valid = geometry
    qps, kbd, vbd = _attn_operands(q_ref, kvc_ref, kvp_ref)
    scores = []
    for j, qp in enumerate(qps):
        scores += _merge(_nt(qp, kbd[j // 4]), use_cur)
    sinks = [sink_ref[0, h] for h in range(N_HEADS)]
    scores = [jnp.where(valid, s - SLOPES[h] * dist, -jnp.inf) for h, s in enumerate(scores)]
    maxes = [jnp.maximum(jnp.max(s, axis=-1, keepdims=True), sinks[h]) for h, s in enumerate(scores)]
    exps = [jnp.exp(s - m) for s, m in zip(scores, maxes)]
    sink_exps = [jnp.exp(sinks[h] - m) for h, m in enumerate(maxes)]
    invs = [1.0 / (jnp.sum(e, axis=-1, keepdims=True) + se) for e, se in zip(exps, sink_exps)]
    probs = [e * inv for e, inv in zip(exps, invs)]
    sink_probs = [se * inv for se, inv in zip(sink_exps, invs)]
    return qps, kbd, vbd, probs, sink_probs


def _kv_specs(n_blocks, reverse):
    def blk(i):
        return (n_blocks - 1 - i) if reverse else i
    cur = pl.BlockSpec((BLK, 2 * D_KV), lambda i: (blk(i), 0))
    prev = pl.BlockSpec((BLK, 2 * D_KV), lambda i: (jnp.maximum(blk(i) - 1, 0), 0))
    return cur, prev


def _out_proj_loss(x, oc, oa, wo, norm_final, target, pc, conv_w, norm_conv_out):
    seq = x.shape[0]
    tile = TOK_TILE
    n_tiles = seq // tile

    def body(x_ref, oc_ref, oa_ref, wo_ref, gf_ref, t_ref, pc_ref, hcc_ref, hcu_ref, cw_ref, gn_ref,
             dx2_ref, doa_ref, gwo_ref, dpc_ref, small_ref, loss_acc, zbuf, dbuf):
        step = pl.program_id(0)

        def small_add(row, value):
            small_ref[row:row + 1, :] += jnp.sum(value, axis=0, keepdims=True)

        @pl.when(step == 0)
        def _():
            gwo_ref[...] = jnp.zeros_like(gwo_ref)
            small_ref[...] = jnp.zeros_like(small_ref)
            loss_acc[...] = jnp.zeros_like(loss_acc)
            dbuf[tile:tile + 8, :] = jnp.zeros((8, D_CONV), F32)

        oc, oa = oc_ref[...], oa_ref[...]
        x2 = x_ref[...] + _nn(oc, wo_ref[0:D_CONV, :]) + _nn(oa, wo_ref[D_CONV:D_MIX, :])
        r = _rstd(x2)
        xhat = x2 * r
        err = xhat * gf_ref[...] - t_ref[...]
        loss_acc[...] += jnp.sum(err * err, axis=0, keepdims=True) * (0.5 / D_MODEL)
        dy = err * (1.0 / D_MODEL)
        small_add(SMALL_NORM_FINAL, dy * xhat)
        dx2 = _rms_bwd(dy * gf_ref[...], xhat, r)
        dx2_ref[...] = dx2
        db = dx2.astype(BF16)
        do = _nt(db, wo_ref[0:D_CONV, :])
        doa_ref[...] = _nt(db, wo_ref[D_CONV:D_MIX, :])
        gwo_ref[0:D_CONV, :] += _tn(oc, db)
        gwo_ref[D_CONV:D_MIX, :] += _tn(oa, db)

        is_first_tile = step == n_tiles - 1
        zbuf[0:8, :] = jnp.where(is_first_tile, 0.0, hcc_ref[...] * hcu_ref[...])
        cb, cc, cu, z, z1, z2, conv = _conv_core(pc_ref, zbuf, cw_ref)
        silu, dsilu = _silu_and_grad(pc_ref[:, 3 * D_CONV:4 * D_CONV])
        yc = cb * conv
        rc = _rstd(yc)
        chat = yc * rc
        dn = do * silu
        dpc_ref[:, 3 * D_CONV:4 * D_CONV] = (do * (chat * gn_ref[...]) * dsilu).astype(BF16)
        small_add(SMALL_NORM_CONV, dn * chat)
        dyc = _rms_bwd(dn * gn_ref[...], chat, rc)
        dpc_ref[:, 0:D_CONV] = (dyc * conv).astype(BF16)
        dconv = dyc * cb
        small_add(SMALL_CONV_W, dconv * z2)
        small_add(SMALL_CONV_W + 1, dconv * z1)
        small_add(SMALL_CONV_W + 2, dconv * z)
        dbuf[0:tile, :] = dconv
        dz = cw_ref[2:3, :] * dconv + cw_ref[1:2, :] * dbuf[1:tile + 1, :] + cw_ref[0:1, :] * dbuf[2:tile + 2, :]
        dpc_ref[:, D_CONV:2 * D_CONV] = (dz * cu).astype(BF16)
        dpc_ref[:, 2 * D_CONV:3 * D_CONV] = (dz * cc).astype(BF16)
        dbuf[tile:tile + 8, :] = dbuf[0:8, :]

        @pl.when(step == n_tiles - 1)
        def _():
            lane = lax.broadcasted_iota(jnp.int32, (1, D_MODEL), 1)
            small_ref[SMALL_MISC:SMALL_MISC + 1, :] = jnp.where(lane == SMALL_LOSS_LANE, jnp.sum(loss_acc[...]), 0.0)

    def rev(i):
        return n_tiles - 1 - i

    def row(width):
        return pl.BlockSpec((tile, width), lambda i: (rev(i), 0))

    def halo(col_block):
        return pl.BlockSpec((8, D_CONV), lambda i: (jnp.maximum(rev(i) * (tile // 8) - 1, 0), col_block))

    def const(shape):
        return pl.BlockSpec(shape, lambda i: (0, 0))

    return pl.pallas_call(
        body, name="out_proj_loss", grid=(n_tiles,),
        out_shape=(jax.ShapeDtypeStruct((seq, D_MODEL), F32), jax.ShapeDtypeStruct((seq, D_ATTN), F32),
                   jax.ShapeDtypeStruct((D_MIX, D_MODEL), F32), jax.ShapeDtypeStruct((seq, D_PC), BF16),
                   jax.ShapeDtypeStruct((SMALL_ROWS, D_MODEL), F32)),
        in_specs=[row(D_MODEL), row(D_CONV), row(D_ATTN), _resident(wo.shape), _resident((1, D_MODEL)), row(D_MODEL),
                  row(D_PC), halo(1), halo(2), _resident(conv_w.shape), _resident((1, D_CONV))],
        out_specs=(row(D_MODEL), row(D_ATTN), const((D_MIX, D_MODEL)), row(D_PC), const((SMALL_ROWS, D_MODEL))),
        scratch_shapes=[pltpu.VMEM((1, D_MODEL), F32), pltpu.VMEM((tile + 8, D_CONV), F32),
                        pltpu.VMEM((tile + 8, D_CONV), F32)],
        compiler_params=_params(62, ("arbitrary",)),
    )(x, oc, oa, wo, norm_final, target, pc, pc, pc, conv_w, norm_conv_out)


def _attn_bwd(q, kv, ga, ya, doa, probs, sink_probs, norm_attn_out, gwo, small):
    seq = q.shape[0]
    n_blocks = seq // BLK
    stage_steps = (0, n_blocks // 4, n_blocks // 2, (3 * n_blocks) // 4, n_blocks - 1)

    def body(q_ref, kvc_ref, kvp_ref, ga_ref, ya_ref, doa_ref, pr_ref, sp_ref, gn_ref, gwo_ref, small_ref,
             dqg_ref, small_out, gwo_sh, gna_ref, gs_ref, carry, dya_buf, *rs_scratch):
        step = pl.program_id(0)

        @pl.when(step == 0)
        def _():
            gna_ref[...] = jnp.zeros_like(gna_ref)
            gs_ref[...] = jnp.zeros_like(gs_ref)
            carry[...] = jnp.zeros_like(carry)

        ya = ya_ref[...]
        r = _rstd(ya)
        xhat = ya * r
        silu, dsilu = _silu_and_grad(ga_ref[...])
        do = doa_ref[...]
        dn = do * silu
        dqg_ref[:, D_ATTN:2 * D_ATTN] = (do * (xhat * gn_ref[...]) * dsilu).astype(BF16)
        gna_ref[...] += jnp.sum(dn * xhat, axis=0, keepdims=True)
        dya_buf[...] = _rms_bwd(dn * gn_ref[...], xhat, r).astype(BF16)

        use_cur = _band_geometry(n_blocks - 1 - step)[0]
        lane = lax.broadcasted_iota(jnp.int32, (BLK, 128), 1)

        def fold(bd):
            return (jnp.where(lane < 64, bd[0:BLK], 0.0) + jnp.where(lane >= 64, bd[BLK:2 * BLK], 0.0),
                    jnp.where(lane < 64, bd[2 * BLK:3 * BLK], 0.0) + jnp.where(lane >= 64, bd[3 * BLK:4 * BLK], 0.0))

        pairs = range(N_HEADS // 2)
        qps, kbd, vbd = _attn_operands(q_ref, kvc_ref, kvp_ref)
        probs = [pr_ref[:, h * BLK:(h + 1) * BLK].astype(F32) for h in range(N_HEADS)]
        dyps = [dya_buf[:, j * 128:(j + 1) * 128] for j in pairs]
        dps = []
        for j in pairs:
            dps += _merge(_nt(dyps[j], vbd[j // 4]), use_cur)
        deltas = [jnp.sum(p * dp, axis=-1, keepdims=True) for p, dp in zip(probs, dps)]
        dss = [p * (dp - delta) for p, dp, delta in zip(probs, dps, deltas)]
        delta_lanes = sum(jnp.where(lane == h, deltas[h], 0.0) for h in range(N_HEADS))
        gs_ref[...] -= jnp.sum(sp_ref[...] * delta_lanes, axis=0, keepdims=True)
        ds4s = [_split(dss[2 * j], dss[2 * j + 1], use_cur).astype(BF16) for j in pairs]
        p4s = [_split(probs[2 * j], probs[2 * j + 1], use_cur).astype(BF16) for j in pairs]
        dqg_ref[:, 0:D_ATTN] = jnp.concatenate([_nn(ds4s[j], kbd[j // 4]) * SCALE for j in pairs], axis=1).astype(BF16)
        sums = []
        for group in range(N_HEADS // HEADS_PER_KV):
            acc = [jnp.zeros((BLK, 128), F32) for _ in range(4)]
            for j in range(group * 4, group * 4 + 4):
                for slot, part in enumerate(fold(_tn(ds4s[j], qps[j])) + fold(_tn(p4s[j], dyps[j]))):
                    acc[slot] = acc[slot] + part
            sums.append([a + pltpu.roll(a, 64, 1) for a in acc])
        dk_cur, dk_prev, dv_cur, dv_prev = (jnp.where(lane < 64, a, b) for a, b in zip(sums[0], sums[1]))
        dqg_ref[:, 2 * D_ATTN:2 * D_ATTN + 2 * D_KV] = (jnp.concatenate([dk_cur, dv_cur], axis=1) + carry[...]).astype(BF16)
        carry[...] = jnp.concatenate([dk_prev, dv_prev], axis=1)

        @pl.when(step == n_blocks - 1)
        def _():
            small_out[...] = small_ref[...]
            small_out[SMALL_NORM_ATTN:SMALL_NORM_ATTN + 1, :] = gna_ref[...]
            small_out[SMALL_MISC:SMALL_MISC + 1, 0:128] = small_ref[SMALL_MISC:SMALL_MISC + 1, 0:128] + gs_ref[...]

        for at, stage in zip(stage_steps, _rs_wout_stages(gwo_ref, gwo_sh, *rs_scratch)):
            pl.when(step == at)(stage)

    row = pl.BlockSpec((BLK, D_ATTN), lambda i: (n_blocks - 1 - i, 0))
    kv_cur, kv_prev = _kv_specs(n_blocks, reverse=True)
    any_spec = pl.BlockSpec(memory_space=pl.ANY)
    return pl.pallas_call(
        body, name="attn_bwd", grid=(n_blocks,),
        out_shape=(jax.ShapeDtypeStruct((seq, D_QG), BF16), jax.ShapeDtypeStruct(small.shape, F32),
                   jax.ShapeDtypeStruct((gwo.shape[0] // N_CHIPS, gwo.shape[1]), F32)),
        in_specs=[row, kv_cur, kv_prev, row, row, row,
                  pl.BlockSpec((BLK, N_HEADS * BLK), lambda i: (n_blocks - 1 - i, 0)),
                  pl.BlockSpec((BLK, 128), lambda i: (n_blocks - 1 - i, 0)), _resident((1, D_ATTN)),
                  any_spec, _resident(small.shape)],
        out_specs=(pl.BlockSpec((BLK, D_QG), lambda i: (n_blocks - 1 - i, 0)),
                   pl.BlockSpec(small.shape, lambda i: (0, 0)), any_spec),
        scratch_shapes=[pltpu.VMEM((1, D_ATTN), F32), pltpu.VMEM((1, 128), F32),
                        pltpu.VMEM((BLK, 2 * D_KV), F32), pltpu.VMEM((BLK, D_ATTN), BF16)] + _rs_wout_scratch(gwo.shape),
        compiler_params=_params(44, ("arbitrary",)),
    )(q, kv, kv, ga, ya, doa, probs, sink_probs, norm_attn_out, gwo, small)


GBLK = 256
GSUB = 64
PAIR_RING = 4
LAG_PAIR, LAG_HOP1, LAG_HOP2 = 1, 7, 14


def _bwd_in(dpc, dqg, h, wt, x, norm_in, dx2, small):
    seq = x.shape[0]
    n_blk = D_IN_PROJ // GBLK
    per_chip = n_blk // N_CHIPS
    n_slots = (n_blk + 1) // 2
    n_sub = GBLK // GSUB
    chip_rows = D_IN_PROJ // N_CHIPS
    tile = TOK_TILE
    n_tiles = seq // tile
    n_steps = n_blk + max(n_tiles, LAG_HOP2)
    chunk = min(seq, 512)
    blk_q, blk_kv, blk_ga = ROW_Q // GBLK, ROW_KV // GBLK, ROW_GA // GBLK

    def block_of(i):
        k = i % N_CHIPS
        robin = per_chip * ((k % 2) * 2 + k // 2) + i // N_CHIPS
        if isinstance(i, int):
            return robin if i < per_chip * N_CHIPS else i
        return jnp.where(i < per_chip * N_CHIPS, robin, i)

    def owner_of(i):
        return (i // N_CHIPS) % 2

    def slot_of(i):
        return (i // (2 * N_CHIPS)) * N_CHIPS + i % N_CHIPS

    def body(dpc_ref, dqg_ref, wt_ref, h_ref, x_ref, g_ref, dx2_ref, small_ref, gx_ref, small_sum, gwt_sh,
             dh_acc, gni, keep, pbuf, xbuf, land, land2, small_land,
             pair_send, pair_recv, h1_send, h1_recv, h2_send, h2_recv, sw_send, sw_recv, sm_send, sm_recv, out_sem):
        step = pl.program_id(0)
        x_i, y_i, c = lax.axis_index("x"), lax.axis_index("y"), lax.axis_index("c")
        me = 4 * x_i + 2 * y_i + c
        j = 2 * x_i + y_i
        pa = (_xor(x_i, 1 - c), _xor(y_i, c), c)
        pb = (_xor(x_i, c), _xor(y_i, 1 - c), c)
        sib = (x_i, y_i, 1 - c)
        ja = 2 * pa[0] + pa[1]
        jb = 2 * pb[0] + pb[1]
        jd = 3 - j

        def remote(src, dst, send, recv, to):
            return pltpu.make_async_remote_copy(src_ref=src, dst_ref=dst, send_sem=send, recv_sem=recv,
                                                device_id=to, device_id_type=MESH)

        def piece(ref, slot, u, n):
            return ref.at[slot, pl.ds(u * GSUB, n * GSUB), :]

        def chip_rows_at(ref, local, n):
            return ref.at[pl.ds(pl.multiple_of(local, GSUB), n * GSUB), :]

        def pair_copy(i):
            slot = slot_of(i)
            return remote(pbuf.at[i % PAIR_RING], land.at[slot], pair_send.at[slot], pair_recv.at[slot], sib)

        def h1_copy(slot, u, n):
            k = slot * n_sub + u
            return remote(piece(xbuf, slot, u, n), piece(xbuf, slot, u, n), h1_send.at[k], h1_recv.at[k], pa)

        def h2_copy(slot, u, n, local):
            k = slot * n_sub + u
            return remote(piece(xbuf, slot, u, n), chip_rows_at(land2, local, n), h2_send.at[k], h2_recv.at[k], pb)

        def sw_copy(slot, u, n, local):
            k = slot * n_sub + u
            return remote(piece(keep, slot, u, n), chip_rows_at(gwt_sh, local, n), sw_send.at[k], sw_recv.at[k], sib)

        def owned(i):
            return (i >= 0) & (i < n_blk) & (owner_of(i) == c)

        def chip_of(blk, u):
            row = blk * GBLK + u * GSUB
            chip = row // chip_rows
            return chip, row - chip * chip_rows

        def pieces(blk):
            first, local = chip_of(blk, 0)
            whole = first == chip_of(blk, n_sub - 1)[0]
            if isinstance(blk, int):
                return [(True, 0, n_sub, first, local)] if whole else [(True, u, 1) + chip_of(blk, u) for u in range(n_sub)]
            return [(whole, 0, n_sub, first, local)] + [(jnp.logical_not(whole), u, 1) + chip_of(blk, u) for u in range(n_sub)]

        @pl.when(step == 0)
        def _():
            dh_acc[...] = jnp.zeros_like(dh_acc)
            gni[...] = jnp.zeros_like(gni)

        @pl.when(step < n_blk)
        def _():
            from_pc = block_of(step) < blk_q
            block = _tn(jnp.where(from_pc, dpc_ref[...], dqg_ref[...]), h_ref[...])
            for t in range(0, seq, chunk):
                d = jnp.where(from_pc, dpc_ref[t:t + chunk, :], dqg_ref[t:t + chunk, :])
                dh_acc[t:t + chunk, :] += _nn(d, wt_ref[...])

            @pl.when(owner_of(step) == c)
            def _():
                keep[slot_of(step)] = block

            @pl.when(owner_of(step) != c)
            def _():
                @pl.when(step >= 2 * PAIR_RING)
                def _():
                    pair_copy(step - 2 * PAIR_RING).wait_send()
                pbuf[step % PAIR_RING] = block.astype(BF16)
                pair_copy(step).start()

        i1 = step - LAG_PAIR

        @pl.when(owned(i1))
        def _():
            slot = slot_of(i1)
            pair_copy(i1).wait_recv()
            _accumulate(keep.at[slot], land.at[slot])
            for cond, u, n, chip, _ in pieces(block_of(i1)):
                @pl.when(cond & ((chip == ja) | (chip == jd)))
                def _(u=u, n=n):
                    _cast_rows(piece(keep, slot, u, n), piece(xbuf, slot, u, n))
                    h1_copy(slot, u, n).start()

        i2 = step - LAG_HOP1

        @pl.when(owned(i2))
        def _():
            slot = slot_of(i2)
            for cond, u, n, chip, local in pieces(block_of(i2)):
                @pl.when(cond & ((chip == j) | (chip == jb)))
                def _(u=u, n=n, chip=chip, local=local):
                    h1_copy(slot, u, n).wait_recv()
                    _accumulate(piece(keep, slot, u, n), piece(xbuf, slot, u, n))

                    @pl.when(chip == jb)
                    def _():
                        _cast_rows(piece(keep, slot, u, n), piece(xbuf, slot, u, n))
                        h2_copy(slot, u, n, local).start()

                @pl.when(cond & ((chip == ja) | (chip == jd)))
                def _(u=u, n=n):
                    h1_copy(slot, u, n).wait_send()

        i3 = step - LAG_HOP2

        @pl.when(owned(i3))
        def _():
            slot = slot_of(i3)
            for cond, u, n, chip, local in pieces(block_of(i3)):
                @pl.when(cond & (chip == j))
                def _(u=u, n=n, local=local):
                    h2_copy(slot, u, n, local).wait_recv()
                    _accumulate(piece(keep, slot, u, n), chip_rows_at(land2, local, n))
                    mine = pltpu.make_async_copy(piece(keep, slot, u, n), chip_rows_at(gwt_sh, local, n), out_sem.at[0])
                    mine.start()
                    sw_copy(slot, u, n, local).start()
                    mine.wait()

                @pl.when(cond & (chip == jb))
                def _(u=u, n=n, local=local):
                    h2_copy(slot, u, n, local).wait_send()

        e = step - n_blk

        @pl.when((e >= 0) & (e < n_tiles))
        def _():
            dh = dh_acc[pl.ds(pl.multiple_of(e * tile, tile), tile), :]
            xv = x_ref[...]
            r = _rstd(xv)
            xhat = xv * r
            gni[...] += jnp.sum(dh * xhat, axis=0, keepdims=True)
            gx_ref[...] = _rms_bwd(dh * g_ref[...], xhat, r) + dx2_ref[...]

        @pl.when(step == n_steps - 1)
        def _():
            small_land[me] = small_ref[...]
            small_land[me, SMALL_NORM_IN:SMALL_NORM_IN + 1, :] = gni[...]
            others = [(dx, dy, dc) for dx in (0, 1) for dy in (0, 1) for dc in (0, 1)][1:]
            sends = [remote(small_land.at[me], small_land.at[me], sm_send.at[k], sm_recv.at[k],
                            (_xor(x_i, dx), _xor(y_i, dy), _xor(c, dc))) for k, (dx, dy, dc) in enumerate(others)]
            for cp in sends:
                cp.start()
            for i in range(n_blk):
                if i + 2 * PAIR_RING >= n_blk:
                    @pl.when(owner_of(i) != c)
                    def _(i=i):
                        pair_copy(i).wait_send()
                for _, u, n, chip, local in pieces(block_of(i)):
                    @pl.when((j == chip) & (c == owner_of(i)))
                    def _(i=i, u=u, n=n, local=local):
                        sw_copy(slot_of(i), u, n, local).wait_send()

                    @pl.when((j == chip) & (c != owner_of(i)))
                    def _(i=i, u=u, n=n, local=local):
                        sw_copy(slot_of(i), u, n, local).wait_recv()
            for cp in sends:
                cp.wait_recv()
            total = small_land[0]
            for dev in range(1, 8):
                total = total + small_land[dev]
            small_sum[...] = total
            for cp in sends:
                cp.wait_send()

    def blk_at(i):
        return block_of(jnp.clip(i, 0, n_blk - 1))

    last_pc_step = max(i for i in range(n_blk) if block_of(i) < blk_q)

    def next_block(i, in_pc):
        i = jnp.clip(i, 0, n_blk - 1)
        step = jnp.full_like(i, last_pc_step if in_pc else n_blk - 1)
        for ahead in reversed(range(N_CHIPS)):
            cand = jnp.minimum(i + ahead, n_blk - 1)
            step = jnp.where((block_of(cand) < blk_q) == in_pc, cand, step)
        return block_of(step)

    def dqg_block(i):
        b = next_block(i, False)
        q_blk = jnp.clip(b - blk_q, 0, blk_kv - blk_q - 1)
        ga_blk = (D_ATTN // GBLK) + jnp.clip(b - blk_ga, 0, n_blk - blk_ga - 1)
        return jnp.where(b < blk_kv, q_blk, jnp.where(b == blk_kv, 2 * D_ATTN // GBLK, ga_blk))

    def tok(i):
        return (jnp.clip(i - n_blk, 0, n_tiles - 1), 0)

    n_piece = n_slots * n_sub
    dma = pltpu.SemaphoreType.DMA
    return pl.pallas_call(
        body, name="bwd_in", grid=(n_steps,),
        out_shape=(jax.ShapeDtypeStruct((seq, D_MODEL), F32), jax.ShapeDtypeStruct(small.shape, F32),
                   jax.ShapeDtypeStruct((chip_rows, D_MODEL), F32)),
        in_specs=[pl.BlockSpec((seq, GBLK), lambda i: (0, next_block(i, True))),
                  pl.BlockSpec((seq, GBLK), lambda i: (0, dqg_block(i))),
                  pl.BlockSpec((GBLK, D_MODEL), lambda i: (blk_at(i), 0)),
                  _resident(h.shape),
                  pl.BlockSpec((tile, D_MODEL), tok), _resident((1, D_MODEL)), pl.BlockSpec((tile, D_MODEL), tok),
                  _resident(small.shape)],
        out_specs=(pl.BlockSpec((tile, D_MODEL), tok), pl.BlockSpec(small.shape, lambda i: (0, 0)),
                   pl.BlockSpec(memory_space=pl.ANY)),
        scratch_shapes=[pltpu.VMEM((seq, D_MODEL), F32), pltpu.VMEM((1, D_MODEL), F32),
                        pltpu.VMEM((n_slots, GBLK, D_MODEL), F32), pltpu.VMEM((PAIR_RING, GBLK, D_MODEL), BF16),
                        pltpu.VMEM((n_slots, GBLK, D_MODEL), BF16), pltpu.VMEM((n_slots, GBLK, D_MODEL), BF16),
                        pltpu.VMEM((chip_rows, D_MODEL), BF16), pltpu.VMEM((8,) + small.shape, F32),
                        dma((n_slots,)), dma((n_slots,)), dma((n_piece,)), dma((n_piece,)), dma((n_piece,)),
                        dma((n_piece,)), dma((n_piece,)), dma((n_piece,)), dma((7,)), dma((7,)), dma((1,))],
        compiler_params=_params(62, ("arbitrary",)),
    )(dpc, dqg, wt, h, x, norm_in, dx2, small)


def _accumulate(dst_ref, src_ref, rows=16):
    def step(i, carry):
        sl = pl.ds(pl.multiple_of(i * rows, rows), rows)
        dst_ref[sl, :] = dst_ref[sl, :] + src_ref[sl, :].astype(F32)
        return carry
    lax.fori_loop(0, dst_ref.shape[0] // rows, step, 0)


def _rs_wout_scratch(gwo_shape):
    o_half, width = gwo_shape[0] // N_CHIPS // 2, gwo_shape[1]
    return [pltpu.VMEM((4, o_half, width), F32), pltpu.VMEM((4, o_half, width), BF16),
            pltpu.VMEM((4, o_half, width), BF16), pltpu.VMEM((2, o_half, width), BF16),
            pltpu.VMEM((o_half, width), BF16),
            pltpu.SemaphoreType.DMA((8,)), pltpu.SemaphoreType.DMA((8,)), pltpu.SemaphoreType.DMA((4,))]


def _rs_wout_stages(gwo_ref, gwo_sh, acc_o, sb_o, r1o, r2o, r3o, send_sems, recv_sems, local_sems):
    o_rows = gwo_ref.shape[0] // N_CHIPS
    o_half = o_rows // 2
    x, y, c = lax.axis_index("x"), lax.axis_index("y"), lax.axis_index("c")
    j = 2 * x + y
    pa = (_xor(x, 1 - c), _xor(y, c), c)
    pb = (_xor(x, c), _xor(y, 1 - c), c)
    sib = (x, y, 1 - c)
    ja = 2 * pa[0] + pa[1]
    jb = 2 * pb[0] + pb[1]
    jd = 3 - j
    order = (ja, jd, jb, j)
    sib_order = (jb, jd, ja, j)

    def rcopy(k, src, dst, to):
        return pltpu.make_async_remote_copy(src_ref=src, dst_ref=dst, send_sem=send_sems.at[k],
                                            recv_sem=recv_sems.at[k], device_id=to, device_id_type=MESH)

    def o_rows_of(chip, half):
        return gwo_ref.at[pl.ds(pl.multiple_of(chip * o_rows + half * o_half, 8), o_half), :]

    def load_all(chips, half):
        cps = [pltpu.make_async_copy(o_rows_of(chip, half), acc_o.at[s], local_sems.at[s]) for s, chip in enumerate(chips)]
        for cp in cps:
            cp.start()
        return cps

    def pair_send(s):
        return rcopy(s, sb_o.at[s], r1o.at[s], sib)

    def out_half(half):
        return gwo_sh.at[pl.ds(pl.multiple_of(half * o_half, 8), o_half), :]

    hop1 = [rcopy(4 + s, sb_o.at[s], r2o.at[s], pa) for s in range(2)]
    hop2 = rcopy(6, sb_o.at[2], r3o, pb)
    swap = rcopy(7, acc_o.at[3], out_half(c), sib)
    mine = pltpu.make_async_copy(acc_o.at[3], out_half(c), local_sems.at[0])

    def resend(s, copy):
        pair_send(s).wait_send()
        _cast_rows(acc_o.at[s], sb_o.at[s])
        copy.start()

    def stage_pair():
        for s, cp in enumerate(load_all(sib_order, 1 - c)):
            cp.wait()
            _cast_rows(acc_o.at[s], sb_o.at[s])
            pair_send(s).start()

    def stage_hop1():
        for s, cp in enumerate(load_all(order, c)):
            cp.wait()
            pair_send(s).wait_recv()
            _accumulate(acc_o.at[s], r1o.at[s])
            if s < 2:
                resend(s, hop1[s])

    def stage_hop2():
        hop1[1].wait_recv()
        _accumulate(acc_o.at[2], r2o.at[1])
        resend(2, hop2)
        hop1[0].wait_recv()
        _accumulate(acc_o.at[3], r2o.at[0])

    def stage_final():
        hop2.wait_recv()
        _accumulate(acc_o.at[3], r3o)
        mine.start()
        swap.start()

    def stage_drain():
        rcopy(7, acc_o.at[3], out_half(1 - c), sib).wait_recv()
        for cp in [pair_send(3)] + hop1 + [hop2, swap]:
            cp.wait_send()
        mine.wait()

    return stage_pair, stage_hop1, stage_hop2, stage_final, stage_drain


def _adamw(name, w, g, m, v, rows):
    def body(w_ref, g_ref, m_ref, v_ref, go_ref, d_ref, nm_ref, nv_ref):
        _adamw_update(w_ref, g_ref, m_ref, v_ref, go_ref, d_ref, nm_ref, nv_ref)

    spec = pl.BlockSpec((rows, w.shape[1]), lambda i: (i, 0))
    shape = jax.ShapeDtypeStruct(w.shape, F32)
    return pl.pallas_call(
        body, name="adamw_" + name, grid=(w.shape[0] // rows,),
        out_shape=(shape,) * 4, in_specs=[spec] * 4, out_specs=(spec,) * 4,
        compiler_params=_params(32, ("arbitrary",)),
    )(w, g, m, v)


def _adamw_update(w_ref, g_ref, m_ref, v_ref, go_ref, d_ref, nm_ref, nv_ref):
    gv = g_ref[...]
    go_ref[...] = gv
    nm = ADAM_B1 * m_ref[...] + (1.0 - ADAM_B1) * gv
    nv = ADAM_B2 * v_ref[...] + (1.0 - ADAM_B2) * (gv * gv)
    m_hat = nm / (1.0 - ADAM_B1 ** ADAM_STEP)
    v_hat = nv / (1.0 - ADAM_B2 ** ADAM_STEP)
    d_ref[...] = -ADAM_LR * (m_hat / (jnp.sqrt(v_hat) + ADAM_EPS) + ADAM_WD * w_ref[...])
    nm_ref[...] = nm
    nv_ref[...] = nv


def _adamw_small(weights, grads, ms, vs):
    n = len(weights)

    def body(*refs):
        ins, outs = refs[:4 * n], refs[4 * n:]
        for k in range(n):
            _adamw_update(*ins[4 * k:4 * k + 4], *outs[4 * k:4 * k + 4])

    flat = [a for group in zip(weights, grads, ms, vs) for a in group]
    vmem = pl.BlockSpec(memory_space=pltpu.VMEM)
    out = pl.pallas_call(
        body, name="adamw_small",
        out_shape=tuple(jax.ShapeDtypeStruct(w.shape, F32) for w in weights for _ in range(4)),
        in_specs=[vmem] * (4 * n), out_specs=(vmem,) * (4 * n),
    )(*flat)
    return [tuple(out[4 * k:4 * k + 4]) for k in range(n)]


def kernel(x, norm_in, w_in, conv_w, attn_sinks, norm_conv_out, norm_attn_out, w_out, norm_final, loss_target, m_norm_in, m_w_in, m_conv_w, m_attn_sinks, m_norm_conv_out, m_norm_attn_out, m_w_out, m_norm_final, v_norm_in, v_w_in, v_conv_w, v_attn_sinks, v_norm_conv_out, v_norm_attn_out, v_w_out, v_norm_final):
    chip = 2 * lax.axis_index("x") + lax.axis_index("y")
    xs, target = x[0], loss_target[0]
    norm_final2 = norm_final.reshape(1, D_MODEL)
    w_in_t, m_w_in_t, v_w_in_t = w_in[0].T, m_w_in[0].T, v_w_in[0].T

    wt, cw4 = _ag_weights(w_in_t, conv_w[0])
    cw = jnp.transpose(cw4, (1, 0, 2)).reshape(3, D_CONV)

    h, pc, q, kv, ga, oc, ya, oa, probs, sink_probs, wo = _fwd_in(xs, norm_in, wt, w_out[0], cw, norm_conv_out,
                                                                   attn_sinks, norm_attn_out)
    dx2, doa, gwo, dpc, small = _out_proj_loss(xs, oc, oa, wo, norm_final2, target, pc, cw, norm_conv_out)
    dqg, small, gwo_sh = _attn_bwd(q, kv, ga, ya, doa, probs, sink_probs, norm_attn_out, gwo, small)
    grad_x, small_sum, gwt_sh = _bwd_in(dpc, dqg, h, wt, xs, norm_in, dx2, small)

    loss = small_sum[SMALL_MISC, SMALL_LOSS_LANE]
    g_norm_in, g_norm_conv, g_norm_attn = (small_sum[r:r + 1] for r in (SMALL_NORM_IN, SMALL_NORM_CONV, SMALL_NORM_ATTN))
    g_norm_final = small_sum[SMALL_NORM_FINAL]
    g_conv_w = lax.dynamic_slice(small_sum[SMALL_CONV_W:SMALL_CONV_W + 3], (0, chip * (D_CONV // N_CHIPS)),
                                 (3, D_CONV // N_CHIPS))[None]
    g_sinks = small_sum[SMALL_MISC:SMALL_MISC + 1, 0:N_HEADS]

    def two_d(a):
        return a.reshape(-1, a.shape[-1])

    def from_hbm(*arrays):
        return tuple(pltpu.with_memory_space_constraint(a, pltpu.HBM) for a in arrays)

    up_w_in = tuple(o.T[None] for o in _adamw("w_in", *from_hbm(w_in_t, gwt_sh, m_w_in_t, v_w_in_t), 200))
    up_w_out = tuple(o[None] for o in _adamw("w_out", *from_hbm(w_out[0], gwo_sh, m_w_out[0], v_w_out[0]), 128))
    small_w = (norm_in, conv_w, attn_sinks, norm_conv_out, norm_attn_out, norm_final)
    small_g = (g_norm_in, g_conv_w, g_sinks, g_norm_conv, g_norm_attn, g_norm_final)
    small_m = (m_norm_in, m_conv_w, m_attn_sinks, m_norm_conv_out, m_norm_attn_out, m_norm_final)
    small_v = (v_norm_in, v_conv_w, v_attn_sinks, v_norm_conv_out, v_norm_attn_out, v_norm_final)
    up_small = _adamw_small(*(tuple(two_d(a) for a in group) for group in (small_w, small_g, small_m, small_v)))
    up_small = [tuple(o.reshape(w.shape) for o in up) for up, w in zip(up_small, small_w)]
    up_norm_in, up_conv_w, up_sinks, up_norm_conv, up_norm_attn, up_norm_final = up_small
    updates = (up_norm_in, up_w_in, up_conv_w, up_sinks, up_norm_conv, up_norm_attn, up_w_out, up_norm_final)
    grads_out, deltas, new_m, new_v = zip(*updates)
    return (loss, grad_x[None], *grads_out, *deltas, *new_m, *new_v)
```

```python
import jax
import jax.numpy as jnp
from jax import lax
from jax.experimental import pallas as pl
from jax.experimental.pallas import tpu as pltpu

F32 = jnp.float32
BF16 = jnp.bfloat16
MESH = pl.DeviceIdType.MESH

D_MODEL = 1024
D_CONV = 1024
D_ATTN = 1024
D_KV = 128
D_QG = 2 * D_ATTN + 2 * D_KV
D_MIX = D_CONV + D_ATTN
D_PC = 4 * D_CONV
D_IN_PROJ = D_PC + 2 * D_ATTN + 2 * D_KV
ROW_Q = D_PC
ROW_KV = ROW_Q + D_ATTN
ROW_GA = ROW_KV + 2 * D_KV
N_HEADS = 16
HEAD_DIM = 64
HEADS_PER_KV = 8
BLK = 128
N_CHIPS = 4
RMS_EPS = 1e-5
SCALE = HEAD_DIM ** -0.5
SLOPES = tuple(2.0 ** (-8.0 * (h + 1) / N_HEADS) for h in range(N_HEADS))

ADAM_LR, ADAM_B1, ADAM_B2, ADAM_EPS, ADAM_WD, ADAM_STEP = 0.001, 0.9, 0.999, 1e-08, 0.01, 10

SMALL_ROWS = 8
SMALL_NORM_IN, SMALL_NORM_CONV, SMALL_NORM_ATTN, SMALL_NORM_FINAL, SMALL_CONV_W, SMALL_MISC = 0, 1, 2, 3, 4, 7
SMALL_LOSS_LANE = N_HEADS

TOK_TILE = 256
PC_PIECE = 512
MIB = 1 << 20


def _params(vmem_mib, semantics=None):
    return pltpu.CompilerParams(dimension_semantics=semantics, vmem_limit_bytes=vmem_mib * MIB)


def _nn(a, b):
    return jnp.dot(a, b, preferred_element_type=F32)


def _nt(a, b):
    return lax.dot_general(a, b, (((1,), (1,)), ((), ())), preferred_element_type=F32)


def _tn(a, b):
    return lax.dot_general(a, b, (((0,), (0,)), ((), ())), preferred_element_type=F32)


def _rstd(v):
    return lax.rsqrt(jnp.mean(v * v, axis=-1, keepdims=True) + RMS_EPS)


def _rms_bwd(g, xhat, rstd):
    return rstd * (g - xhat * jnp.mean(g * xhat, axis=-1, keepdims=True))


def _silu_and_grad(g):
    s = jax.nn.sigmoid(g)
    return g * s, s * (1.0 + g * (1.0 - s))


def _resident(shape):
    return pl.BlockSpec(shape, lambda *_: (0,) * len(shape), pipeline_mode=pl.Buffered(1))


def _xor(a, b):
    return a + b - 2 * a * b


def _cast_rows(src_ref, dst_ref, rows=32):
    def step(i, carry):
        sl = pl.ds(pl.multiple_of(i * rows, rows), rows)
        dst_ref[sl, :] = src_ref[sl, :].astype(dst_ref.dtype)
        return carry
    lax.fori_loop(0, src_ref.shape[0] // rows, step, 0)


def _ag_scratch(shard_shape):
    rows, width = shard_shape
    return [pltpu.VMEM((rows, width), F32), pltpu.VMEM((rows, width), BF16), pltpu.VMEM((3, rows // 2, width), BF16),
            pltpu.SemaphoreType.DMA((6,)), pltpu.SemaphoreType.DMA((6,)), pltpu.SemaphoreType.DMA((4,))]


def _ag_stages(sh_ref, out, f32_buf, own, land, send_sems, recv_sems, local_sems):
    rows = sh_ref.shape[0]
    half = rows // 2
    x, y, c = lax.axis_index("x"), lax.axis_index("y"), lax.axis_index("c")
    j = 2 * x + y
    p1 = (_xor(x, c), _xor(y, 1 - c), c)
    p2 = (_xor(x, 1 - c), _xor(y, c), c)
    sib = (x, y, 1 - c)
    j1 = 2 * p1[0] + p1[1]
    j2 = 2 * p2[0] + p2[1]
    j3 = 3 - j

    def rows_of(chip, hf):
        return out.at[pl.ds(pl.multiple_of(chip * rows + hf * half, 16), half), :]

    def rcopy(k, src, dst, to):
        return pltpu.make_async_remote_copy(src_ref=src, dst_ref=dst, send_sem=send_sems.at[k],
                                            recv_sem=recv_sems.at[k], device_id=to, device_id_type=MESH)

    my_half = own.at[pl.ds(pl.multiple_of(c * half, 16), half), :]
    hop1 = rcopy(0, my_half, land.at[0], p1)
    hop2_own = rcopy(1, my_half, land.at[1], p2)
    hop2_fwd = rcopy(2, land.at[0], land.at[2], p2)
    swaps = [rcopy(3 + s, land.at[s], rows_of(chip, c), sib) for s, chip in enumerate((j1, j2, j3))]
    keeps = [pltpu.make_async_copy(land.at[s], rows_of(chip, c), local_sems.at[1 + s]) for s, chip in enumerate((j1, j2, j3))]
    load = pltpu.make_async_copy(sh_ref, f32_buf, local_sems.at[0])
    own_out = pltpu.make_async_copy(own, out.at[pl.ds(pl.multiple_of(j * rows, 16), rows), :], local_sems.at[0])

    def stage_send():
        load.start()
        load.wait()
        _cast_rows(f32_buf, own)
        own_out.start()
        hop1.start()

    def stage_forward():
        hop1.wait_recv()
        hop2_own.start()
        hop2_fwd.start()
        swaps[0].start()
        keeps[0].start()

    def stage_publish():
        hop2_own.wait_recv()
        swaps[1].start()
        keeps[1].start()
        hop2_fwd.wait_recv()
        swaps[2].start()
        keeps[2].start()

    def stage_drain():
        for s, chip in enumerate((j2, j1, j3)):
            rcopy(3 + s, my_half, rows_of(chip, 1 - c), sib).wait_recv()
        for cp in [hop1, hop2_own, hop2_fwd] + swaps:
            cp.wait_send()
        for cp in [own_out] + keeps:
            cp.wait()

    return stage_send, stage_forward, stage_publish, stage_drain


def _ag_weights(wt_sh, cw_sh):
    def body(wt_ref, cw_ref, wt_out, cw_out, *scratch):
        cw_send, cw_recv, cw_local = scratch[-3:]
        x, y, c = lax.axis_index("x"), lax.axis_index("y"), lax.axis_index("c")
        j = 2 * x + y
        p1 = (_xor(x, c), _xor(y, 1 - c), c)
        p2 = (_xor(x, 1 - c), _xor(y, c), c)
        j1 = 2 * p1[0] + p1[1]

        def cw_copy(k, chip, to):
            src = cw_ref if k != 2 else cw_out.at[chip]
            return pltpu.make_async_remote_copy(src_ref=src, dst_ref=cw_out.at[chip], send_sem=cw_send.at[k],
                                                recv_sem=cw_recv.at[k], device_id=to, device_id_type=MESH)

        mine = pltpu.make_async_copy(cw_ref, cw_out.at[j], cw_local.at[0])
        mine.start()
        first = cw_copy(0, j, p1)
        first.start()
        stages = _ag_stages(wt_ref, wt_out, *scratch[:-3])
        stages[0]()
        first.wait_recv()
        second = [cw_copy(1, j, p2), cw_copy(2, j1, p2)]
        for cp in second:
            cp.start()
        for stage in stages[1:]:
            stage()
        for cp in second:
            cp.wait_recv()
        for cp in [first] + second:
            cp.wait_send()
        mine.wait()

    any_spec = pl.BlockSpec(memory_space=pl.ANY)
    dma = pltpu.SemaphoreType.DMA
    return pl.pallas_call(
        body, name="ag_weights",
        out_shape=(pltpu.HBM((N_CHIPS * wt_sh.shape[0], wt_sh.shape[1]), BF16),
                   jax.ShapeDtypeStruct((N_CHIPS,) + cw_sh.shape, cw_sh.dtype)),
        in_specs=[any_spec, any_spec], out_specs=(any_spec, any_spec),
        scratch_shapes=_ag_scratch(wt_sh.shape) + [dma((3,)), dma((3,)), dma((1,))],
        compiler_params=_params(32),
    )(wt_sh, cw_sh)


def _fwd_in(x, norm_in, wt, wo_sh, conv_w, norm_conv_out, sinks, norm_attn_out):
    seq = x.shape[0]
    tile = TOK_TILE
    n_tiles = seq // tile
    stage_steps = (0, n_tiles // 4, (5 * n_tiles) // 8, n_tiles - 1)
    blocks = tile // BLK

    def body(x_ref, g_ref, wt_ref, wo_ref, cw_ref, gn_ref, sink_ref, gna_ref,
             h_ref, pc_ref, q_ref, kv_ref, ga_ref, oc_ref, ya_ref, oa_ref, pr_ref, sp_ref, wo_out,
             zbuf, kv_last, *ag_scratch):
        step = pl.program_id(0)
        stages = _ag_stages(wo_ref, wo_out, *ag_scratch)
        for at, stage in zip(stage_steps[:-1], stages[:-1]):
            pl.when(step == at)(stage)

        @pl.when(step == 0)
        def _():
            zbuf[0:8, :] = jnp.zeros((8, D_CONV), F32)
            kv_last[...] = jnp.zeros_like(kv_last)

        def attention(b):
            rows = pl.ds(b * BLK, BLK)
            kv_prev = kv_last if b == 0 else kv_ref.at[pl.ds((b - 1) * BLK, BLK), :]
            return _attn_forward(q_ref.at[rows, :], kv_ref.at[rows, :], kv_prev, ga_ref.at[rows, :], sink_ref, gna_ref,
                                 _band_geometry(step * blocks + b), ya_ref.at[rows, :], oa_ref.at[rows, :],
                                 pr_ref.at[rows, :], sp_ref.at[rows, :])

        xv = x_ref[...]
        h = (xv * _rstd(xv) * g_ref[...]).astype(BF16)
        h_ref[...] = h
        q_ref[...] = _nt(h, wt_ref[ROW_Q:ROW_KV, :])
        kv_ref[...] = _nt(h, wt_ref[ROW_KV:ROW_GA, :])
        ga_ref[...] = _nt(h, wt_ref[ROW_GA:D_IN_PROJ, :])
        attention_blocks = [attention(b) for b in range(blocks)]
        for lo in range(0, D_PC, PC_PIECE):
            pc_ref[:, lo:lo + PC_PIECE] = _nt(h, wt_ref[lo:lo + PC_PIECE, :])
            for stages_of_block in attention_blocks:
                next(stages_of_block, None)
        for stages_of_block in attention_blocks:
            for _ in stages_of_block:
                pass
        kv_last[...] = kv_ref[tile - BLK:tile, :]

        cb, _, _, _, _, _, conv = _conv_core(pc_ref, zbuf, cw_ref)
        yc = cb * conv
        silu, _ = _silu_and_grad(pc_ref[:, 3 * D_CONV:4 * D_CONV])
        oc_ref[...] = (yc * _rstd(yc) * gn_ref[...] * silu).astype(BF16)
        zbuf[0:8, :] = zbuf[tile:tile + 8, :]
        pl.when(pl.program_id(0) == stage_steps[-1])(stages[-1])

    def row(width):
        return pl.BlockSpec((tile, width), lambda i: (i, 0))

    any_spec = pl.BlockSpec(memory_space=pl.ANY)
    return pl.pallas_call(
        body, name="fwd_in", grid=(n_tiles,),
        out_shape=(jax.ShapeDtypeStruct((seq, D_MODEL), BF16), jax.ShapeDtypeStruct((seq, D_PC), F32),
                   jax.ShapeDtypeStruct((seq, D_ATTN), F32), jax.ShapeDtypeStruct((seq, 2 * D_KV), F32),
                   jax.ShapeDtypeStruct((seq, D_ATTN), F32), jax.ShapeDtypeStruct((seq, D_CONV), BF16),
                   pltpu.HBM((seq, D_ATTN), F32), pltpu.HBM((seq, D_ATTN), BF16),
                   pltpu.HBM((seq, N_HEADS * BLK), BF16), pltpu.HBM((seq, 128), F32),
                   jax.ShapeDtypeStruct((N_CHIPS * wo_sh.shape[0], wo_sh.shape[1]), BF16)),
        in_specs=[row(D_MODEL), _resident((1, D_MODEL)), _resident(wt.shape), any_spec,
                  _resident(conv_w.shape), _resident((1, D_CONV)), pl.BlockSpec(memory_space=pltpu.SMEM),
                  _resident((1, D_ATTN))],
        out_specs=(row(D_MODEL), row(D_PC), row(D_ATTN), row(2 * D_KV), row(D_ATTN), row(D_CONV), row(D_ATTN),
                   row(D_ATTN), row(N_HEADS * BLK), row(128), any_spec),
        scratch_shapes=[pltpu.VMEM((tile + 8, D_CONV), F32), pltpu.VMEM((BLK, 2 * D_KV), F32)] + _ag_scratch(wo_sh.shape),
        compiler_params=_params(60, ("arbitrary",)),
    )(x, norm_in, wt, wo_sh, conv_w, norm_conv_out, sinks, norm_attn_out)


def _conv_core(pc_ref, zbuf, cw_ref):
    tile = pc_ref.shape[0]
    cb = pc_ref[:, 0:D_CONV]
    cc = pc_ref[:, D_CONV:2 * D_CONV]
    cu = pc_ref[:, 2 * D_CONV:3 * D_CONV]
    z = cc * cu
    zbuf[8:tile + 8, :] = z
    z1 = zbuf[7:tile + 7, :]
    z2 = zbuf[6:tile + 6, :]
    conv = cw_ref[0:1, :] * z2 + cw_ref[1:2, :] * z1 + cw_ref[2:3, :] * z
    return cb, cc, cu, z, z1, z2, conv


def _band_geometry(block_index):
    qi = lax.broadcasted_iota(jnp.int32, (BLK, BLK), 0)
    kp = lax.broadcasted_iota(jnp.int32, (BLK, BLK), 1)
    use_cur = kp <= qi
    dist = jnp.where(use_cur, qi - kp, qi - kp + BLK).astype(F32)
    valid = use_cur | (block_index > 0)
    return use_cur, dist, valid


def _block_diag(cur, prev, group):
    lane = lax.broadcasted_iota(jnp.int32, cur.shape, 1)

    def halves(t):
        other = pltpu.roll(t, 64, 1)
        lo, hi = (t, other) if group == 0 else (other, t)
        return jnp.where(lane < 64, lo, 0.0), jnp.where(lane >= 64, hi, 0.0)

    return jnp.concatenate(halves(cur) + halves(prev), axis=0).astype(BF16)


def _merge(s4, use_cur):
    return (jnp.where(use_cur, s4[:, 0:BLK], s4[:, 2 * BLK:3 * BLK]),
            jnp.where(use_cur, s4[:, BLK:2 * BLK], s4[:, 3 * BLK:4 * BLK]))


def _split(a, b, use_cur):
    return jnp.concatenate([jnp.where(use_cur, a, 0.0), jnp.where(use_cur, b, 0.0),
                            jnp.where(use_cur, 0.0, a), jnp.where(use_cur, 0.0, b)], axis=1)


def _softmax_head(s, head, sink, dist, valid):
    sc = jnp.where(valid, s - SLOPES[head] * dist, -jnp.inf)
    m = jnp.maximum(jnp.max(sc, axis=-1, keepdims=True), sink)
    p = jnp.exp(sc - m)
    es = jnp.exp(sink - m)
    inv = 1.0 / (jnp.sum(p, axis=-1, keepdims=True) + es)
    return p * inv, es * inv


def _attn_operands(q_ref, kvc_ref, kvp_ref):
    groups = range(N_HEADS // HEADS_PER_KV)
    kbd = [_block_diag(kvc_ref[:, 0:D_KV], kvp_ref[:, 0:D_KV], g) for g in groups]
    vbd = [_block_diag(kvc_ref[:, D_KV:2 * D_KV], kvp_ref[:, D_KV:2 * D_KV], g) for g in groups]
    qps = [(q_ref[:, j * 128:(j + 1) * 128] * SCALE).astype(BF16) for j in range(N_HEADS // 2)]
    return qps, kbd, vbd


def _attn_forward(q_ref, kvc_ref, kvp_ref, ga_ref, sink_ref, gn_ref, geometry, ya_ref, oa_ref, pr_ref, sp_ref):
    use_cur, dist, valid = geometry
    _, kbd, vbd = operands = _attn_operands(q_ref, kvc_ref, kvp_ref)
    yield
    scores = []
    for j, qp in enumerate(operands[0]):
        scores += _merge(_nt(qp, kbd[j // 4]), use_cur)
    yield
    sinks = [sink_ref[0, h] for h in range(N_HEADS)]
    scores = [jnp.where(valid, s - SLOPES[h] * dist, -jnp.inf) for h, s in enumerate(scores)]
    maxes = [jnp.maximum(jnp.max(s, axis=-1, keepdims=True), sinks[h]) for h, s in enumerate(scores)]
    yield
    exps = [jnp.exp(s - m) for s, m in zip(scores, maxes)]
    sink_exps = [jnp.exp(sinks[h] - m) for h, m in enumerate(maxes)]
    yield
    invs = [1.0 / (jnp.sum(e, axis=-1, keepdims=True) + se) for e, se in zip(exps, sink_exps)]
    probs = [e * inv for e, inv in zip(exps, invs)]
    pr_ref[...] = jnp.concatenate(probs, axis=1).astype(BF16)
    lane = lax.broadcasted_iota(jnp.int32, (BLK, 128), 1)
    sp_ref[...] = sum(jnp.where(lane == h, se * inv, 0.0) for h, (se, inv) in enumerate(zip(sink_exps, invs)))
    yield
    p4s = [_split(probs[2 * j], probs[2 * j + 1], use_cur).astype(BF16) for j in range(N_HEADS // 2)]
    ya = jnp.concatenate([_nn(p4, vbd[j // 4]) for j, p4 in enumerate(p4s)], axis=1)
    ya_ref[...] = ya
    yield
    silu, _ = _silu_and_grad(ga_ref[...])
    oa_ref[...] = (ya * _rstd(ya) * gn_ref[...] * silu).astype(BF16)


def _kv_specs(n_blocks, reverse):
    def blk(i):
        return (n_blocks - 1 - i) if reverse else i
    cur = pl.BlockSpec((BLK, 2 * D_KV), lambda i: (blk(i), 0))
    prev = pl.BlockSpec((BLK, 2 * D_KV), lambda i: (jnp.maximum(blk(i) - 1, 0), 0))
    return cur, prev


def _out_proj_loss(x, oc, oa, wo, norm_final, target, pc, conv_w, norm_conv_out):
    seq = x.shape[0]
    tile = TOK_TILE
    n_tiles = seq // tile

    def body(x_ref, oc_ref, oa_ref, wo_ref, gf_ref, t_ref, pc_ref, hcc_ref, hcu_ref, cw_ref, gn_ref,
             dx2_ref, doa_ref, gwo_ref, dpc_ref, small_ref, loss_acc, zbuf, dbuf):
        step = pl.program_id(0)

        def small_add(row, value):
            small_ref[row:row + 1, :] += jnp.sum(value, axis=0, keepdims=True)

        @pl.when(step == 0)
        def _():
            gwo_ref[...] = jnp.zeros_like(gwo_ref)
            small_ref[...] = jnp.zeros_like(small_ref)
            loss_acc[...] = jnp.zeros_like(loss_acc)
            dbuf[tile:tile + 8, :] = jnp.zeros((8, D_CONV), F32)

        oc, oa = oc_ref[...], oa_ref[...]
        x2 = x_ref[...] + _nn(oc, wo_ref[0:D_CONV, :]) + _nn(oa, wo_ref[D_CONV:D_MIX, :])
        r = _rstd(x2)
        xhat = x2 * r
        err = xhat * gf_ref[...] - t_ref[...]
        loss_acc[...] += jnp.sum(err * err, axis=0, keepdims=True) * (0.5 / D_MODEL)
        dy = err * (1.0 / D_MODEL)
        small_add(SMALL_NORM_FINAL, dy * xhat)
        dx2 = _rms_bwd(dy * gf_ref[...], xhat, r)
        dx2_ref[...] = dx2
        db = dx2.astype(BF16)
        do = _nt(db, wo_ref[0:D_CONV, :])
        doa_ref[...] = _nt(db, wo_ref[D_CONV:D_MIX, :])
        gwo_ref[0:D_CONV, :] += _tn(oc, db)
        gwo_ref[D_CONV:D_MIX, :] += _tn(oa, db)

        is_first_tile = step == n_tiles - 1
        zbuf[0:8, :] = jnp.where(is_first_tile, 0.0, hcc_ref[...] * hcu_ref[...])
        cb, cc, cu, z, z1, z2, conv = _conv_core(pc_ref, zbuf, cw_ref)
        silu, dsilu = _silu_and_grad(pc_ref[:, 3 * D_CONV:4 * D_CONV])
        yc = cb * conv
        rc = _rstd(yc)
        chat = yc * rc
        dn = do * silu
        dpc_ref[:, 3 * D_CONV:4 * D_CONV] = (do * (chat * gn_ref[...]) * dsilu).astype(BF16)
        small_add(SMALL_NORM_CONV, dn * chat)
        dyc = _rms_bwd(dn * gn_ref[...], chat, rc)
        dpc_ref[:, 0:D_CONV] = (dyc * conv).astype(BF16)
        dconv = dyc * cb
        small_add(SMALL_CONV_W, dconv * z2)
        small_add(SMALL_CONV_W + 1, dconv * z1)
        small_add(SMALL_CONV_W + 2, dconv * z)
        dbuf[0:tile, :] = dconv
        dz = cw_ref[2:3, :] * dconv + cw_ref[1:2, :] * dbuf[1:tile + 1, :] + cw_ref[0:1, :] * dbuf[2:tile + 2, :]
        dpc_ref[:, D_CONV:2 * D_CONV] = (dz * cu).astype(BF16)
        dpc_ref[:, 2 * D_CONV:3 * D_CONV] = (dz * cc).astype(BF16)
        dbuf[tile:tile + 8, :] = dbuf[0:8, :]

        @pl.when(step == n_tiles - 1)
        def _():
            lane = lax.broadcasted_iota(jnp.int32, (1, D_MODEL), 1)
            small_ref[SMALL_MISC:SMALL_MISC + 1, :] = jnp.where(lane == SMALL_LOSS_LANE, jnp.sum(loss_acc[...]), 0.0)

    def rev(i):
        return n_tiles - 1 - i

    def row(width):
        return pl.BlockSpec((tile, width), lambda i: (rev(i), 0))

    def halo(col_block):
        return pl.BlockSpec((8, D_CONV), lambda i: (jnp.maximum(rev(i) * (tile // 8) - 1, 0), col_block))

    def const(shape):
        return pl.BlockSpec(shape, lambda i: (0, 0))

    return pl.pallas_call(
        body, name="out_proj_loss", grid=(n_tiles,),
        out_shape=(jax.ShapeDtypeStruct((seq, D_MODEL), F32), jax.ShapeDtypeStruct((seq, D_ATTN), F32),
                   jax.ShapeDtypeStruct((D_MIX, D_MODEL), F32), jax.ShapeDtypeStruct((seq, D_PC), BF16),
                   jax.ShapeDtypeStruct((SMALL_ROWS, D_MODEL), F32)),
        in_specs=[row(D_MODEL), row(D_CONV), row(D_ATTN), _resident(wo.shape), _resident((1, D_MODEL)), row(D_MODEL),
                  row(D_PC), halo(1), halo(2), _resident(conv_w.shape), _resident((1, D_CONV))],
        out_specs=(row(D_MODEL), row(D_ATTN), const((D_MIX, D_MODEL)), row(D_PC), const((SMALL_ROWS, D_MODEL))),
        scratch_shapes=[pltpu.VMEM((1, D_MODEL), F32), pltpu.VMEM((tile + 8, D_CONV), F32),
                        pltpu.VMEM((tile + 8, D_CONV), F32)],
        compiler_params=_params(62, ("arbitrary",)),
    )(x, oc, oa, wo, norm_final, target, pc, pc, pc, conv_w, norm_conv_out)


def _attn_bwd(q, kv, ga, ya, doa, probs, sink_probs, norm_attn_out, gwo, small):
    seq = q.shape[0]
    n_blocks = seq // BLK
    stage_steps = (0, n_blocks // 4, n_blocks // 2, (3 * n_blocks) // 4, n_blocks - 1)

    def body(q_ref, kvc_ref, kvp_ref, ga_ref, ya_ref, doa_ref, pr_ref, sp_ref, gn_ref, gwo_ref, small_ref,
             dqg_ref, small_out, gwo_sh, gna_ref, gs_ref, carry, dya_buf, *rs_scratch):
        step = pl.program_id(0)

        @pl.when(step == 0)
        def _():
            gna_ref[...] = jnp.zeros_like(gna_ref)
            gs_ref[...] = jnp.zeros_like(gs_ref)
            carry[...] = jnp.zeros_like(carry)

        ya = ya_ref[...]
        r = _rstd(ya)
        xhat = ya * r
        silu, dsilu = _silu_and_grad(ga_ref[...])
        do = doa_ref[...]
        dn = do * silu
        dqg_ref[:, D_ATTN:2 * D_ATTN] = (do * (xhat * gn_ref[...]) * dsilu).astype(BF16)
        gna_ref[...] += jnp.sum(dn * xhat, axis=0, keepdims=True)
        dya_buf[...] = _rms_bwd(dn * gn_ref[...], xhat, r).astype(BF16)

        use_cur = _band_geometry(n_blocks - 1 - step)[0]
        lane = lax.broadcasted_iota(jnp.int32, (BLK, 128), 1)

        def fold(bd):
            return (jnp.where(lane < 64, bd[0:BLK], 0.0) + jnp.where(lane >= 64, bd[BLK:2 * BLK], 0.0),
                    jnp.where(lane < 64, bd[2 * BLK:3 * BLK], 0.0) + jnp.where(lane >= 64, bd[3 * BLK:4 * BLK], 0.0))

        pairs = range(N_HEADS // 2)
        qps, kbd, vbd = _attn_operands(q_ref, kvc_ref, kvp_ref)
        probs = [pr_ref[:, h * BLK:(h + 1) * BLK].astype(F32) for h in range(N_HEADS)]
        dyps = [dya_buf[:, j * 128:(j + 1) * 128] for j in pairs]
        dps = []
        for j in pairs:
            dps += _merge(_nt(dyps[j], vbd[j // 4]), use_cur)
        deltas = [jnp.sum(p * dp, axis=-1, keepdims=True) for p, dp in zip(probs, dps)]
        dss = [p * (dp - delta) for p, dp, delta in zip(probs, dps, deltas)]
        delta_lanes = sum(jnp.where(lane == h, deltas[h], 0.0) for h in range(N_HEADS))
        gs_ref[...] -= jnp.sum(sp_ref[...] * delta_lanes, axis=0, keepdims=True)
        ds4s = [_split(dss[2 * j], dss[2 * j + 1], use_cur).astype(BF16) for j in pairs]
        p4s = [_split(probs[2 * j], probs[2 * j + 1], use_cur).astype(BF16) for j in pairs]
        dqg_ref[:, 0:D_ATTN] = jnp.concatenate([_nn(ds4s[j], kbd[j // 4]) * SCALE for j in pairs], axis=1).astype(BF16)
        sums = []
        for group in range(N_HEADS // HEADS_PER_KV):
            acc = [jnp.zeros((BLK, 128), F32) for _ in range(4)]
            for j in range(group * 4, group * 4 + 4):
                for slot, part in enumerate(fold(_tn(ds4s[j], qps[j])) + fold(_tn(p4s[j], dyps[j]))):
                    acc[slot] = acc[slot] + part
            sums.append([a + pltpu.roll(a, 64, 1) for a in acc])
        dk_cur, dk_prev, dv_cur, dv_prev = (jnp.where(lane < 64, a, b) for a, b in zip(sums[0], sums[1]))
        dqg_ref[:, 2 * D_ATTN:2 * D_ATTN + 2 * D_KV] = (jnp.concatenate([dk_cur, dv_cur], axis=1) + carry[...]).astype(BF16)
        carry[...] = jnp.concatenate([dk_prev, dv_prev], axis=1)

        @pl.when(step == n_blocks - 1)
        def _():
            small_out[...] = small_ref[...]
            small_out[SMALL_NORM_ATTN:SMALL_NORM_ATTN + 1, :] = gna_ref[...]
            small_out[SMALL_MISC:SMALL_MISC + 1, 0:128] = small_ref[SMALL_MISC:SMALL_MISC + 1, 0:128] + gs_ref[...]

        for at, stage in zip(stage_steps, _rs_wout_stages(gwo_ref, gwo_sh, *rs_scratch)):
            pl.when(step == at)(stage)

    row = pl.BlockSpec((BLK, D_ATTN), lambda i: (n_blocks - 1 - i, 0))
    kv_cur, kv_prev = _kv_specs(n_blocks, reverse=True)
    any_spec = pl.BlockSpec(memory_space=pl.ANY)
    return pl.pallas_call(
        body, name="attn_bwd", grid=(n_blocks,),
        out_shape=(jax.ShapeDtypeStruct((seq, D_QG), BF16), jax.ShapeDtypeStruct(small.shape, F32),
                   jax.ShapeDtypeStruct((gwo.shape[0] // N_CHIPS, gwo.shape[1]), F32)),
        in_specs=[row, kv_cur, kv_prev, row, row, row,
                  pl.BlockSpec((BLK, N_HEADS * BLK), lambda i: (n_blocks - 1 - i, 0)),
                  pl.BlockSpec((BLK, 128), lambda i: (n_blocks - 1 - i, 0)), _resident((1, D_ATTN)),
                  any_spec, _resident(small.shape)],
        out_specs=(pl.BlockSpec((BLK, D_QG), lambda i: (n_blocks - 1 - i, 0)),
                   pl.BlockSpec(small.shape, lambda i: (0, 0)), any_spec),
        scratch_shapes=[pltpu.VMEM((1, D_ATTN), F32), pltpu.VMEM((1, 128), F32),
                        pltpu.VMEM((BLK, 2 * D_KV), F32), pltpu.VMEM((BLK, D_ATTN), BF16)] + _rs_wout_scratch(gwo.shape),
        compiler_params=_params(44, ("arbitrary",)),
    )(q, kv, kv, ga, ya, doa, probs, sink_probs, norm_attn_out, gwo, small)


GBLK = 256
GSUB = 64
PAIR_RING = 4
LAG_PAIR, LAG_HOP1, LAG_HOP2 = 1, 7, 14


def _bwd_in(dpc, dqg, h, wt, x, norm_in, dx2, small):
    seq = x.shape[0]
    n_blk = D_IN_PROJ // GBLK
    per_chip = n_blk // N_CHIPS
    n_slots = (n_blk + 1) // 2
    n_sub = GBLK // GSUB
    chip_rows = D_IN_PROJ // N_CHIPS
    tile = TOK_TILE
    n_tiles = seq // tile
    n_steps = n_blk + max(n_tiles, LAG_HOP2)
    chunk = min(seq, 512)
    blk_q, blk_kv, blk_ga = ROW_Q // GBLK, ROW_KV // GBLK, ROW_GA // GBLK

    def block_of(i):
        k = i % N_CHIPS
        robin = per_chip * ((k % 2) * 2 + k // 2) + i // N_CHIPS
        if isinstance(i, int):
            return robin if i < per_chip * N_CHIPS else i
        return jnp.where(i < per_chip * N_CHIPS, robin, i)

    def owner_of(i):
        return (i // N_CHIPS) % 2

    def slot_of(i):
        return (i // (2 * N_CHIPS)) * N_CHIPS + i % N_CHIPS

    def body(dpc_ref, dqg_ref, wt_ref, h_ref, x_ref, g_ref, dx2_ref, small_ref, gx_ref, small_sum, gwt_sh,
             dh_acc, gni, keep, pbuf, xbuf, land, land2, small_land,
             pair_send, pair_recv, h1_send, h1_recv, h2_send, h2_recv, sw_send, sw_recv, sm_send, sm_recv, out_sem):
        step = pl.program_id(0)
        x_i, y_i, c = lax.axis_index("x"), lax.axis_index("y"), lax.axis_index("c")
        me = 4 * x_i + 2 * y_i + c
        j = 2 * x_i + y_i
        pa = (_xor(x_i, 1 - c), _xor(y_i, c), c)
        pb = (_xor(x_i, c), _xor(y_i, 1 - c), c)
        sib = (x_i, y_i, 1 - c)
        ja = 2 * pa[0] + pa[1]
        jb = 2 * pb[0] + pb[1]
        jd = 3 - j

        def remote(src, dst, send, recv, to):
            return pltpu.make_async_remote_copy(src_ref=src, dst_ref=dst, send_sem=send, recv_sem=recv,
                                                device_id=to, device_id_type=MESH)

        def piece(ref, slot, u, n):
            return ref.at[slot, pl.ds(u * GSUB, n * GSUB), :]

        def chip_rows_at(ref, local, n):
            return ref.at[pl.ds(pl.multiple_of(local, GSUB), n * GSUB), :]

        def pair_copy(i):
            slot = slot_of(i)
            return remote(pbuf.at[i % PAIR_RING], land.at[slot], pair_send.at[slot], pair_recv.at[slot], sib)

        def h1_copy(slot, u, n):
            k = slot * n_sub + u
            return remote(piece(xbuf, slot, u, n), piece(xbuf, slot, u, n), h1_send.at[k], h1_recv.at[k], pa)

        def h2_copy(slot, u, n, local):
            k = slot * n_sub + u
            return remote(piece(xbuf, slot, u, n), chip_rows_at(land2, local, n), h2_send.at[k], h2_recv.at[k], pb)

        def sw_copy(slot, u, n, local):
            k = slot * n_sub + u
            return remote(piece(keep, slot, u, n), chip_rows_at(gwt_sh, local, n), sw_send.at[k], sw_recv.at[k], sib)

        def owned(i):
            return (i >= 0) & (i < n_blk) & (owner_of(i) == c)

        def chip_of(blk, u):
            row = blk * GBLK + u * GSUB
            chip = row // chip_rows
            return chip, row - chip * chip_rows

        def pieces(blk):
            first, local = chip_of(blk, 0)
            whole = first == chip_of(blk, n_sub - 1)[0]
            if isinstance(blk, int):
                return [(True, 0, n_sub, first, local)] if whole else [(True, u, 1) + chip_of(blk, u) for u in range(n_sub)]
            return [(whole, 0, n_sub, first, local)] + [(jnp.logical_not(whole), u, 1) + chip_of(blk, u) for u in range(n_sub)]

        @pl.when(step == 0)
        def _():
            dh_acc[...] = jnp.zeros_like(dh_acc)
            gni[...] = jnp.zeros_like(gni)

        @pl.when(step < n_blk)
        def _():
            from_pc = block_of(step) < blk_q
            block = _tn(jnp.where(from_pc, dpc_ref[...], dqg_ref[...]), h_ref[...])
            for t in range(0, seq, chunk):
                d = jnp.where(from_pc, dpc_ref[t:t + chunk, :], dqg_ref[t:t + chunk, :])
                dh_acc[t:t + chunk, :] += _nn(d, wt_ref[...])

            @pl.when(owner_of(step) == c)
            def _():
                keep[slot_of(step)] = block

            @pl.when(owner_of(step) != c)
            def _():
                @pl.when(step >= 2 * PAIR_RING)
                def _():
                    pair_copy(step - 2 * PAIR_RING).wait_send()
                pbuf[step % PAIR_RING] = block.astype(BF16)
                pair_copy(step).start()

        i1 = step - LAG_PAIR

        @pl.when(owned(i1))
        def _():
            slot = slot_of(i1)
            pair_copy(i1).wait_recv()
            _accumulate(keep.at[slot], land.at[slot])
            for cond, u, n, chip, _ in pieces(block_of(i1)):
                @pl.when(cond & ((chip == ja) | (chip == jd)))
                def _(u=u, n=n):
                    _cast_rows(piece(keep, slot, u, n), piece(xbuf, slot, u, n))
                    h1_copy(slot, u, n).start()

        i2 = step - LAG_HOP1

        @pl.when(owned(i2))
        def _():
            slot = slot_of(i2)
            for cond, u, n, chip, local in pieces(block_of(i2)):
                @pl.when(cond & ((chip == j) | (chip == jb)))
                def _(u=u, n=n, chip=chip, local=local):
                    h1_copy(slot, u, n).wait_recv()
                    _accumulate(piece(keep, slot, u, n), piece(xbuf, slot, u, n))

                    @pl.when(chip == jb)
                    def _():
                        _cast_rows(piece(keep, slot, u, n), piece(xbuf, slot, u, n))
                        h2_copy(slot, u, n, local).start()

                @pl.when(cond & ((chip == ja) | (chip == jd)))
                def _(u=u, n=n):
                    h1_copy(slot, u, n).wait_send()

        i3 = step - LAG_HOP2

        @pl.when(owned(i3))
        def _():
            slot = slot_of(i3)
            for cond, u, n, chip, local in pieces(block_of(i3)):
                @pl.when(cond & (chip == j))
                def _(u=u, n=n, local=local):
                    h2_copy(slot, u, n, local).wait_recv()
                    _accumulate(piece(keep, slot, u, n), chip_rows_at(land2, local, n))
                    mine = pltpu.make_async_copy(piece(keep, slot, u, n), chip_rows_at(gwt_sh, local, n), out_sem.at[0])
                    mine.start()
                    sw_copy(slot, u, n, local).start()
                    mine.wait()

                @pl.when(cond & (chip == jb))
                def _(u=u, n=n, local=local):
                    h2_copy(slot, u, n, local).wait_send()

        e = step - n_blk

        @pl.when((e >= 0) & (e < n_tiles))
        def _():
            dh = dh_acc[pl.ds(pl.multiple_of(e * tile, tile), tile), :]
            xv = x_ref[...]
            r = _rstd(xv)
            xhat = xv * r
            gni[...] += jnp.sum(dh * xhat, axis=0, keepdims=True)
            gx_ref[...] = _rms_bwd(dh * g_ref[...], xhat, r) + dx2_ref[...]

        @pl.when(step == n_steps - 1)
        def _():
            small_land[me] = small_ref[...]
            small_land[me, SMALL_NORM_IN:SMALL_NORM_IN + 1, :] = gni[...]
            others = [(dx, dy, dc) for dx in (0, 1) for dy in (0, 1) for dc in (0, 1)][1:]
            sends = [remote(small_land.at[me], small_land.at[me], sm_send.at[k], sm_recv.at[k],
                            (_xor(x_i, dx), _xor(y_i, dy), _xor(c, dc))) for k, (dx, dy, dc) in enumerate(others)]
            for cp in sends:
                cp.start()
            for i in range(n_blk):
                if i + 2 * PAIR_RING >= n_blk:
                    @pl.when(owner_of(i) != c)
                    def _(i=i):
                        pair_copy(i).wait_send()
                for _, u, n, chip, local in pieces(block_of(i)):
                    @pl.when((j == chip) & (c == owner_of(i)))
                    def _(i=i, u=u, n=n, local=local):
                        sw_copy(slot_of(i), u, n, local).wait_send()

                    @pl.when((j == chip) & (c != owner_of(i)))
                    def _(i=i, u=u, n=n, local=local):
                        sw_copy(slot_of(i), u, n, local).wait_recv()
            for cp in sends:
                cp.wait_recv()
            total = small_land[0]
            for dev in range(1, 8):
                total = total + small_land[dev]
            small_sum[...] = total
            for cp in sends:
                cp.wait_send()

    def blk_at(i):
        return block_of(jnp.clip(i, 0, n_blk - 1))

    last_pc_step = max(i for i in range(n_blk) if block_of(i) < blk_q)

    def next_block(i, in_pc):
        i = jnp.clip(i, 0, n_blk - 1)
        step = jnp.full_like(i, last_pc_step if in_pc else n_blk - 1)
        for ahead in reversed(range(N_CHIPS)):
            cand = jnp.minimum(i + ahead, n_blk - 1)
            step = jnp.where((block_of(cand) < blk_q) == in_pc, cand, step)
        return block_of(step)

    def dqg_block(i):
        b = next_block(i, False)
        q_blk = jnp.clip(b - blk_q, 0, blk_kv - blk_q - 1)
        ga_blk = (D_ATTN // GBLK) + jnp.clip(b - blk_ga, 0, n_blk - blk_ga - 1)
        return jnp.where(b < blk_kv, q_blk, jnp.where(b == blk_kv, 2 * D_ATTN // GBLK, ga_blk))

    def tok(i):
        return (jnp.clip(i - n_blk, 0, n_tiles - 1), 0)

    n_piece = n_slots * n_sub
    dma = pltpu.SemaphoreType.DMA
    return pl.pallas_call(
        body, name="bwd_in", grid=(n_steps,),
        out_shape=(jax.ShapeDtypeStruct((seq, D_MODEL), F32), jax.ShapeDtypeStruct(small.shape, F32),
                   jax.ShapeDtypeStruct((chip_rows, D_MODEL), F32)),
        in_specs=[pl.BlockSpec((seq, GBLK), lambda i: (0, next_block(i, True))),
                  pl.BlockSpec((seq, GBLK), lambda i: (0, dqg_block(i))),
                  pl.BlockSpec((GBLK, D_MODEL), lambda i: (blk_at(i), 0)),
                  _resident(h.shape),
                  pl.BlockSpec((tile, D_MODEL), tok), _resident((1, D_MODEL)), pl.BlockSpec((tile, D_MODEL), tok),
                  _resident(small.shape)],
        out_specs=(pl.BlockSpec((tile, D_MODEL), tok), pl.BlockSpec(small.shape, lambda i: (0, 0)),
                   pl.BlockSpec(memory_space=pl.ANY)),
        scratch_shapes=[pltpu.VMEM((seq, D_MODEL), F32), pltpu.VMEM((1, D_MODEL), F32),
                        pltpu.VMEM((n_slots, GBLK, D_MODEL), F32), pltpu.VMEM((PAIR_RING, GBLK, D_MODEL), BF16),
                        pltpu.VMEM((n_slots, GBLK, D_MODEL), BF16), pltpu.VMEM((n_slots, GBLK, D_MODEL), BF16),
                        pltpu.VMEM((chip_rows, D_MODEL), BF16), pltpu.VMEM((8,) + small.shape, F32),
                        dma((n_slots,)), dma((n_slots,)), dma((n_piece,)), dma((n_piece,)), dma((n_piece,)),
                        dma((n_piece,)), dma((n_piece,)), dma((n_piece,)), dma((7,)), dma((7,)), dma((1,))],
        compiler_params=_params(62, ("arbitrary",)),
    )(dpc, dqg, wt, h, x, norm_in, dx2, small)


def _accumulate(dst_ref, src_ref, rows=16):
    def step(i, carry):
        sl = pl.ds(pl.multiple_of(i * rows, rows), rows)
        dst_ref[sl, :] = dst_ref[sl, :] + src_ref[sl, :].astype(F32)
        return carry
    lax.fori_loop(0, dst_ref.shape[0] // rows, step, 0)


def _rs_wout_scratch(gwo_shape):
    o_half, width = gwo_shape[0] // N_CHIPS // 2, gwo_shape[1]
    return [pltpu.VMEM((4, o_half, width), F32), pltpu.VMEM((4, o_half, width), BF16),
            pltpu.VMEM((4, o_half, width), BF16), pltpu.VMEM((2, o_half, width), BF16),
            pltpu.VMEM((o_half, width), BF16),
            pltpu.SemaphoreType.DMA((8,)), pltpu.SemaphoreType.DMA((8,)), pltpu.SemaphoreType.DMA((4,))]


def _rs_wout_stages(gwo_ref, gwo_sh, acc_o, sb_o, r1o, r2o, r3o, send_sems, recv_sems, local_sems):
    o_rows = gwo_ref.shape[0] // N_CHIPS
    o_half = o_rows // 2
    x, y, c = lax.axis_index("x"), lax.axis_index("y"), lax.axis_index("c")
    j = 2 * x + y
    pa = (_xor(x, 1 - c), _xor(y, c), c)
    pb = (_xor(x, c), _xor(y, 1 - c), c)
    sib = (x, y, 1 - c)
    ja = 2 * pa[0] + pa[1]
    jb = 2 * pb[0] + pb[1]
    jd = 3 - j
    order = (ja, jd, jb, j)
    sib_order = (jb, jd, ja, j)

    def rcopy(k, src, dst, to):
        return pltpu.make_async_remote_copy(src_ref=src, dst_ref=dst, send_sem=send_sems.at[k],
                                            recv_sem=recv_sems.at[k], device_id=to, device_id_type=MESH)

    def o_rows_of(chip, half):
        return gwo_ref.at[pl.ds(pl.multiple_of(chip * o_rows + half * o_half, 8), o_half), :]

    def load_all(chips, half):
        cps = [pltpu.make_async_copy(o_rows_of(chip, half), acc_o.at[s], local_sems.at[s]) for s, chip in enumerate(chips)]
        for cp in cps:
            cp.start()
        return cps

    def pair_send(s):
        return rcopy(s, sb_o.at[s], r1o.at[s], sib)

    def out_half(half):
        return gwo_sh.at[pl.ds(pl.multiple_of(half * o_half, 8), o_half), :]

    hop1 = [rcopy(4 + s, sb_o.at[s], r2o.at[s], pa) for s in range(2)]
    hop2 = rcopy(6, sb_o.at[2], r3o, pb)
    swap = rcopy(7, acc_o.at[3], out_half(c), sib)
    mine = pltpu.make_async_copy(acc_o.at[3], out_half(c), local_sems.at[0])

    def resend(s, copy):
        pair_send(s).wait_send()
        _cast_rows(acc_o.at[s], sb_o.at[s])
        copy.start()

    def stage_pair():
        for s, cp in enumerate(load_all(sib_order, 1 - c)):
            cp.wait()
            _cast_rows(acc_o.at[s], sb_o.at[s])
            pair_send(s).start()

    def stage_hop1():
        for s, cp in enumerate(load_all(order, c)):
            cp.wait()
            pair_send(s).wait_recv()
            _accumulate(acc_o.at[s], r1o.at[s])
            if s < 2:
                resend(s, hop1[s])

    def stage_hop2():
        hop1[1].wait_recv()
        _accumulate(acc_o.at[2], r2o.at[1])
        resend(2, hop2)
        hop1[0].wait_recv()
        _accumulate(acc_o.at[3], r2o.at[0])

    def stage_final():
        hop2.wait_recv()
        _accumulate(acc_o.at[3], r3o)
        mine.start()
        swap.start()

    def stage_drain():
        rcopy(7, acc_o.at[3], out_half(1 - c), sib).wait_recv()
        for cp in [pair_send(3)] + hop1 + [hop2, swap]:
            cp.wait_send()
        mine.wait()

    return stage_pair, stage_hop1, stage_hop2, stage_final, stage_drain


def _adamw(name, w, g, m, v, rows):
    def body(w_ref, g_ref, m_ref, v_ref, go_ref, d_ref, nm_ref, nv_ref):
        _adamw_update(w_ref, g_ref, m_ref, v_ref, go_ref, d_ref, nm_ref, nv_ref)

    spec = pl.BlockSpec((rows, w.shape[1]), lambda i: (i, 0))
    shape = jax.ShapeDtypeStruct(w.shape, F32)
    return pl.pallas_call(
        body, name="adamw_" + name, grid=(w.shape[0] // rows,),
        out_shape=(shape,) * 4, in_specs=[spec] * 4, out_specs=(spec,) * 4,
        compiler_params=_params(32, ("arbitrary",)),
    )(w, g, m, v)


def _adamw_update(w_ref, g_ref, m_ref, v_ref, go_ref, d_ref, nm_ref, nv_ref):
    gv = g_ref[...]
    go_ref[...] = gv
    nm = ADAM_B1 * m_ref[...] + (1.0 - ADAM_B1) * gv
    nv = ADAM_B2 * v_ref[...] + (1.0 - ADAM_B2) * (gv * gv)
    m_hat = nm / (1.0 - ADAM_B1 ** ADAM_STEP)
    v_hat = nv / (1.0 - ADAM_B2 ** ADAM_STEP)
    d_ref[...] = -ADAM_LR * (m_hat / (jnp.sqrt(v_hat) + ADAM_EPS) + ADAM_WD * w_ref[...])
    nm_ref[...] = nm
    nv_ref[...] = nv


def _adamw_small(weights, grads, ms, vs):
    n = len(weights)

    def body(*refs):
        ins, outs = refs[:4 * n], refs[4 * n:]
        for k in range(n):
            _adamw_update(*ins[4 * k:4 * k + 4], *outs[4 * k:4 * k + 4])

    flat = [a for group in zip(weights, grads, ms, vs) for a in group]
    vmem = pl.BlockSpec(memory_space=pltpu.VMEM)
    out = pl.pallas_call(
        body, name="adamw_small",
        out_shape=tuple(jax.ShapeDtypeStruct(w.shape, F32) for w in weights for _ in range(4)),
        in_specs=[vmem] * (4 * n), out_specs=(vmem,) * (4 * n),
    )(*flat)
    return [tuple(out[4 * k:4 * k + 4]) for k in range(n)]


def kernel(x, norm_in, w_in, conv_w, attn_sinks, norm_conv_out, norm_attn_out, w_out, norm_final, loss_target, m_norm_in, m_w_in, m_conv_w, m_attn_sinks, m_norm_conv_out, m_norm_attn_out, m_w_out, m_norm_final, v_norm_in, v_w_in, v_conv_w, v_attn_sinks, v_norm_conv_out, v_norm_attn_out, v_w_out, v_norm_final):
    chip = 2 * lax.axis_index("x") + lax.axis_index("y")
    xs, target = x[0], loss_target[0]
    norm_final2 = norm_final.reshape(1, D_MODEL)
    w_in_t, m_w_in_t, v_w_in_t = w_in[0].T, m_w_in[0].T, v_w_in[0].T

    wt, cw4 = _ag_weights(w_in_t, conv_w[0])
    cw = jnp.transpose(cw4, (1, 0, 2)).reshape(3, D_CONV)

    h, pc, q, kv, ga, oc, ya, oa, probs, sink_probs, wo = _fwd_in(xs, norm_in, wt, w_out[0], cw, norm_conv_out,
                                                                   attn_sinks, norm_attn_out)
    dx2, doa, gwo, dpc, small = _out_proj_loss(xs, oc, oa, wo, norm_final2, target, pc, cw, norm_conv_out)
    dqg, small, gwo_sh = _attn_bwd(q, kv, ga, ya, doa, probs, sink_probs, norm_attn_out, gwo, small)
    grad_x, small_sum, gwt_sh = _bwd_in(dpc, dqg, h, wt, xs, norm_in, dx2, small)

    loss = small_sum[SMALL_MISC, SMALL_LOSS_LANE]
    g_norm_in, g_norm_conv, g_norm_attn = (small_sum[r:r + 1] for r in (SMALL_NORM_IN, SMALL_NORM_CONV, SMALL_NORM_ATTN))
    g_norm_final = small_sum[SMALL_NORM_FINAL]
    g_conv_w = lax.dynamic_slice(small_sum[SMALL_CONV_W:SMALL_CONV_W + 3], (0, chip * (D_CONV // N_CHIPS)),
                                 (3, D_CONV // N_CHIPS))[None]
    g_sinks = small_sum[SMALL_MISC:SMALL_MISC + 1, 0:N_HEADS]

    def two_d(a):
        return a.reshape(-1, a.shape[-1])

    def from_hbm(*arrays):
        return tuple(pltpu.with_memory_space_constraint(a, pltpu.HBM) for a in arrays)

    up_w_in = tuple(o.T[None] for o in _adamw("w_in", *from_hbm(w_in_t, gwt_sh, m_w_in_t, v_w_in_t), 200))
    up_w_out = tuple(o[None] for o in _adamw("w_out", *from_hbm(w_out[0], gwo_sh, m_w_out[0], v_w_out[0]), 128))
    small_w = (norm_in, conv_w, attn_sinks, norm_conv_out, norm_attn_out, norm_final)
    small_g = (g_norm_in, g_conv_w, g_sinks, g_norm_conv, g_norm_attn, g_norm_final)
    small_m = (m_norm_in, m_conv_w, m_attn_sinks, m_norm_conv_out, m_norm_attn_out, m_norm_final)
    small_v = (v_norm_in, v_conv_w, v_attn_sinks, v_norm_conv_out, v_norm_attn_out, v_norm_final)
    up_small = _adamw_small(*(tuple(two_d(a) for a in group) for group in (small_w, small_g, small_m, small_v)))
    up_small = [tuple(o.reshape(w.shape) for o in up) for up, w in zip(up_small, small_w)]
    up_norm_in, up_conv_w, up_sinks, up_norm_conv, up_norm_attn, up_norm_final = up_small
    updates = (up_norm_in, up_w_in, up_conv_w, up_sinks, up_norm_conv, up_norm_attn, up_w_out, up_norm_final)
    grads_out, deltas, new_m, new_v = zip(*updates)
    return (loss, grad_x[None], *grads_out, *deltas, *new_m, *new_v)
```

```python
import jax
import jax.numpy as jnp
from jax import lax
from jax.experimental import pallas as pl
from jax.experimental.pallas import tpu as pltpu

F32 = jnp.float32
BF16 = jnp.bfloat16
MESH = pl.DeviceIdType.MESH

D_MODEL = 1024
D_CONV = 1024
D_ATTN = 1024
D_KV = 128
D_QG = 2 * D_ATTN + 2 * D_KV
D_MIX = D_CONV + D_ATTN
D_PC = 4 * D_CONV
D_IN_PROJ = D_PC + 2 * D_ATTN + 2 * D_KV
ROW_Q = D_PC
ROW_KV = ROW_Q + D_ATTN
ROW_GA = ROW_KV + 2 * D_KV
N_HEADS = 16
HEAD_DIM = 64
HEADS_PER_KV = 8
BLK = 128
N_CHIPS = 4
RMS_EPS = 1e-5
SCALE = HEAD_DIM ** -0.5
SLOPES = tuple(2.0 ** (-8.0 * (h + 1) / N_HEADS) for h in range(N_HEADS))

ADAM_LR, ADAM_B1, ADAM_B2, ADAM_EPS, ADAM_WD, ADAM_STEP = 0.001, 0.9, 0.999, 1e-08, 0.01, 10

SMALL_ROWS = 8
SMALL_NORM_IN, SMALL_NORM_CONV, SMALL_NORM_ATTN, SMALL_NORM_FINAL, SMALL_CONV_W, SMALL_MISC = 0, 1, 2, 3, 4, 7
SMALL_LOSS_LANE = N_HEADS

TOK_TILE = 256
PC_PIECE = 512
MIB = 1 << 20


def _params(vmem_mib, semantics=None):
    return pltpu.CompilerParams(dimension_semantics=semantics, vmem_limit_bytes=vmem_mib * MIB)


def _nn(a, b):
    return jnp.dot(a, b, preferred_element_type=F32)


def _nt(a, b):
    return lax.dot_general(a, b, (((1,), (1,)), ((), ())), preferred_element_type=F32)


def _tn(a, b):
    return lax.dot_general(a, b, (((0,), (0,)), ((), ())), preferred_element_type=F32)


def _rstd(v):
    return lax.rsqrt(jnp.mean(v * v, axis=-1, keepdims=True) + RMS_EPS)


def _rms_bwd(g, xhat, rstd):
    return rstd * (g - xhat * jnp.mean(g * xhat, axis=-1, keepdims=True))


def _silu_and_grad(g):
    s = jax.nn.sigmoid(g)
    return g * s, s * (1.0 + g * (1.0 - s))


def _resident(shape):
    return pl.BlockSpec(shape, lambda *_: (0,) * len(shape), pipeline_mode=pl.Buffered(1))


def _xor(a, b):
    return a + b - 2 * a * b


def _cast_rows(src_ref, dst_ref, rows=32):
    def step(i, carry):
        sl = pl.ds(pl.multiple_of(i * rows, rows), rows)
        dst_ref[sl, :] = src_ref[sl, :].astype(dst_ref.dtype)
        return carry
    lax.fori_loop(0, src_ref.shape[0] // rows, step, 0)


def _ag_scratch(shard_shape):
    rows, width = shard_shape
    return [pltpu.VMEM((rows, width), F32), pltpu.VMEM((rows, width), BF16), pltpu.VMEM((3, rows // 2, width), BF16),
            pltpu.SemaphoreType.DMA((6,)), pltpu.SemaphoreType.DMA((6,)), pltpu.SemaphoreType.DMA((4,))]


def _ag_stages(sh_ref, out, f32_buf, own, land, send_sems, recv_sems, local_sems):
    rows = sh_ref.shape[0]
    half = rows // 2
    x, y, c = lax.axis_index("x"), lax.axis_index("y"), lax.axis_index("c")
    j = 2 * x + y
    p1 = (_xor(x, c), _xor(y, 1 - c), c)
    p2 = (_xor(x, 1 - c), _xor(y, c), c)
    sib = (x, y, 1 - c)
    j1 = 2 * p1[0] + p1[1]
    j2 = 2 * p2[0] + p2[1]
    j3 = 3 - j

    def rows_of(chip, hf):
        return out.at[pl.ds(pl.multiple_of(chip * rows + hf * half, 16), half), :]

    def rcopy(k, src, dst, to):
        return pltpu.make_async_remote_copy(src_ref=src, dst_ref=dst, send_sem=send_sems.at[k],
                                            recv_sem=recv_sems.at[k], device_id=to, device_id_type=MESH)

    my_half = own.at[pl.ds(pl.multiple_of(c * half, 16), half), :]
    hop1 = rcopy(0, my_half, land.at[0], p1)
    hop2_own = rcopy(1, my_half, land.at[1], p2)
    hop2_fwd = rcopy(2, land.at[0], land.at[2], p2)
    swaps = [rcopy(3 + s, land.at[s], rows_of(chip, c), sib) for s, chip in enumerate((j1, j2, j3))]
    keeps = [pltpu.make_async_copy(land.at[s], rows_of(chip, c), local_sems.at[1 + s]) for s, chip in enumerate((j1, j2, j3))]
    load = pltpu.make_async_copy(sh_ref, f32_buf, local_sems.at[0])
    own_out = pltpu.make_async_copy(own, out.at[pl.ds(pl.multiple_of(j * rows, 16), rows), :], local_sems.at[0])

    def stage_send():
        load.start()
        load.wait()
        _cast_rows(f32_buf, own)
        own_out.start()
        hop1.start()

    def stage_forward():
        hop1.wait_recv()
        hop2_own.start()
        hop2_fwd.start()
        swaps[0].start()
        keeps[0].start()

    def stage_publish():
        hop2_own.wait_recv()
        swaps[1].start()
        keeps[1].start()
        hop2_fwd.wait_recv()
        swaps[2].start()
        keeps[2].start()

    def stage_drain():
        for s, chip in enumerate((j2, j1, j3)):
            rcopy(3 + s, my_half, rows_of(chip, 1 - c), sib).wait_recv()
        for cp in [hop1, hop2_own, hop2_fwd] + swaps:
            cp.wait_send()
        for cp in [own_out] + keeps:
            cp.wait()

    return stage_send, stage_forward, stage_publish, stage_drain


AG_CAST_ROWS = 400


def _gather_resident(sh_ref, out, f32_buf, send_sems, recv_sems, local_sems):
    rows = sh_ref.shape[0]
    half = rows // 2
    x, y, c = lax.axis_index("x"), lax.axis_index("y"), lax.axis_index("c")
    j = 2 * x + y
    p1 = (_xor(x, c), _xor(y, 1 - c), c)
    p2 = (_xor(x, 1 - c), _xor(y, c), c)
    sib = (x, y, 1 - c)
    j1 = 2 * p1[0] + p1[1]
    j2 = 2 * p2[0] + p2[1]
    j3 = 3 - j

    def rows_of(chip, hf):
        return out.at[pl.ds(pl.multiple_of(chip * rows + hf * half, 16), half), :]

    def send(k, chip, to):
        return pltpu.make_async_remote_copy(src_ref=rows_of(chip, c), dst_ref=rows_of(chip, c), send_sem=send_sems.at[k],
                                            recv_sem=recv_sems.at[k], device_id=to, device_id_type=MESH)

    for lo in range(0, rows, AG_CAST_ROWS):
        load = pltpu.make_async_copy(sh_ref.at[pl.ds(lo, AG_CAST_ROWS), :], f32_buf, local_sems.at[0])
        load.start()
        load.wait()
        _cast_rows(f32_buf, out.at[pl.ds(pl.multiple_of(j * rows + lo, 16), AG_CAST_ROWS), :], rows=16)
    hop1 = send(0, j, p1)
    hop1.start()
    hop1.wait_recv()
    sends = [hop1, send(1, j, p2), send(2, j1, p2), send(3, j1, sib)]
    for cp in sends[1:]:
        cp.start()
    sends[1].wait_recv()
    sends.append(send(4, j2, sib))
    sends[-1].start()
    sends[2].wait_recv()
    sends.append(send(5, j3, sib))
    sends[-1].start()
    for k in (3, 4, 5):
        send(k, j, sib).wait_recv()
    for cp in sends:
        cp.wait_send()


def _fwd_in(x, norm_in, wt_sh, wo_sh, cw_sh, norm_conv_out, sinks, norm_attn_out):
    seq = x.shape[0]
    tile = TOK_TILE
    n_tiles = seq // tile
    stage_steps = (0, n_tiles // 4, (5 * n_tiles) // 8, n_tiles - 1)
    blocks = tile // BLK
    cw_cols = cw_sh.shape[1]

    def body(x_ref, g_ref, wtsh_ref, wo_ref, cwsh_ref, gn_ref, sink_ref, gna_ref,
             h_ref, pc_ref, q_ref, kv_ref, ga_ref, oc_ref, ya_ref, oa_ref, pr_ref, sp_ref, wt_out, cw_ref, wo_out,
             zbuf, kv_last, wt_ref, f32_buf, cw_land, wt_send, wt_recv, wt_local, cw_send, cw_recv, cw_local, *ag_scratch):
        step = pl.program_id(0)
        to_hbm = pltpu.make_async_copy(wt_ref, wt_out, wt_local.at[0])

        @pl.when(step == 0)
        def _():
            zbuf[0:8, :] = jnp.zeros((8, D_CONV), F32)
            kv_last[...] = jnp.zeros_like(kv_last)
            x_i, y_i, c = lax.axis_index("x"), lax.axis_index("y"), lax.axis_index("c")
            j = 2 * x_i + y_i
            p1 = (_xor(x_i, c), _xor(y_i, 1 - c), c)
            p2 = (_xor(x_i, 1 - c), _xor(y_i, c), c)
            j1 = 2 * p1[0] + p1[1]

            def cw_copy(k, src, chip, to):
                return pltpu.make_async_remote_copy(src_ref=src, dst_ref=cw_land.at[chip], send_sem=cw_send.at[k],
                                                    recv_sem=cw_recv.at[k], device_id=to, device_id_type=MESH)

            mine = pltpu.make_async_copy(cwsh_ref, cw_land.at[j], cw_local.at[0])
            mine.start()
            first = cw_copy(0, cwsh_ref, j, p1)
            first.start()
            _gather_resident(wtsh_ref, wt_ref, f32_buf, wt_send, wt_recv, wt_local)
            to_hbm.start()
            first.wait_recv()
            second = [cw_copy(1, cwsh_ref, j, p2), cw_copy(2, cw_land.at[j1], j1, p2)]
            for cp in second:
                cp.start()
            for cp in second:
                cp.wait_recv()
            for cp in [first] + second:
                cp.wait_send()
            mine.wait()
            for chip in range(N_CHIPS):
                cw_ref[:, chip * cw_cols:(chip + 1) * cw_cols] = cw_land[chip]

        stages = _ag_stages(wo_ref, wo_out, *ag_scratch)
        for at, stage in zip(stage_steps[:-1], stages[:-1]):
            pl.when(step == at)(stage)

        def attention(b):
            rows = pl.ds(b * BLK, BLK)
            kv_prev = kv_last if b == 0 else kv_ref.at[pl.ds((b - 1) * BLK, BLK), :]
            return _attn_forward(q_ref.at[rows, :], kv_ref.at[rows, :], kv_prev, ga_ref.at[rows, :], sink_ref, gna_ref,
                                 _band_geometry(step * blocks + b), ya_ref.at[rows, :], oa_ref.at[rows, :],
                                 pr_ref.at[rows, :], sp_ref.at[rows, :])

        xv = x_ref[...]
        h = (xv * _rstd(xv) * g_ref[...]).astype(BF16)
        h_ref[...] = h
        q_ref[...] = _nt(h, wt_ref[ROW_Q:ROW_KV, :])
        kv_ref[...] = _nt(h, wt_ref[ROW_KV:ROW_GA, :])
        ga_ref[...] = _nt(h, wt_ref[ROW_GA:D_IN_PROJ, :])
        attention_blocks = [attention(b) for b in range(blocks)]
        for lo in range(0, D_PC, PC_PIECE):
            pc_ref[:, lo:lo + PC_PIECE] = _nt(h, wt_ref[lo:lo + PC_PIECE, :])
            for stages_of_block in attention_blocks:
                next(stages_of_block, None)
        for stages_of_block in attention_blocks:
            for _ in stages_of_block:
                pass
        kv_last[...] = kv_ref[tile - BLK:tile, :]

        cb, _, _, _, _, _, conv = _conv_core(pc_ref, zbuf, cw_ref)
        yc = cb * conv
        silu, _ = _silu_and_grad(pc_ref[:, 3 * D_CONV:4 * D_CONV])
        oc_ref[...] = (yc * _rstd(yc) * gn_ref[...] * silu).astype(BF16)
        zbuf[0:8, :] = zbuf[tile:tile + 8, :]

        @pl.when(step == stage_steps[-1])
        def _():
            stages[-1]()
            to_hbm.wait()

    def row(width):
        return pl.BlockSpec((tile, width), lambda i: (i, 0))

    any_spec = pl.BlockSpec(memory_space=pl.ANY)
    dma = pltpu.SemaphoreType.DMA
    wt_shape = (N_CHIPS * wt_sh.shape[0], wt_sh.shape[1])
    cw_shape = (cw_sh.shape[0], N_CHIPS * cw_cols)
    return pl.pallas_call(
        body, name="fwd_in", grid=(n_tiles,),
        out_shape=(jax.ShapeDtypeStruct((seq, D_MODEL), BF16), jax.ShapeDtypeStruct((seq, D_PC), F32),
                   jax.ShapeDtypeStruct((seq, D_ATTN), F32), jax.ShapeDtypeStruct((seq, 2 * D_KV), F32),
                   jax.ShapeDtypeStruct((seq, D_ATTN), F32), jax.ShapeDtypeStruct((seq, D_CONV), BF16),
                   pltpu.HBM((seq, D_ATTN), F32), pltpu.HBM((seq, D_ATTN), BF16),
                   pltpu.HBM((seq, N_HEADS * BLK), BF16), pltpu.HBM((seq, 128), F32),
                   pltpu.HBM(wt_shape, BF16), jax.ShapeDtypeStruct(cw_shape, F32),
                   jax.ShapeDtypeStruct((N_CHIPS * wo_sh.shape[0], wo_sh.shape[1]), BF16)),
        in_specs=[row(D_MODEL), _resident((1, D_MODEL)), any_spec, any_spec, any_spec, _resident((1, D_CONV)),
                  pl.BlockSpec(memory_space=pltpu.SMEM), _resident((1, D_ATTN))],
        out_specs=(row(D_MODEL), row(D_PC), row(D_ATTN), row(2 * D_KV), row(D_ATTN), row(D_CONV), row(D_ATTN),
                   row(D_ATTN), row(N_HEADS * BLK), row(128), any_spec, pl.BlockSpec(cw_shape, lambda i: (0, 0)),
                   any_spec),
        scratch_shapes=[pltpu.VMEM((tile + 8, D_CONV), F32), pltpu.VMEM((BLK, 2 * D_KV), F32),
                        pltpu.VMEM(wt_shape, BF16), pltpu.VMEM((AG_CAST_ROWS, wt_sh.shape[1]), F32),
                        pltpu.VMEM((N_CHIPS,) + cw_sh.shape, F32),
                        dma((6,)), dma((6,)), dma((1,)), dma((3,)), dma((3,)), dma((1,))] + _ag_scratch(wo_sh.shape),
        compiler_params=_params(62, ("arbitrary",)),
    )(x, norm_in, wt_sh, wo_sh, cw_sh, norm_conv_out, sinks, norm_attn_out)


def _conv_core(pc_ref, zbuf, cw_ref):
    tile = pc_ref.shape[0]
    cb = pc_ref[:, 0:D_CONV]
    cc = pc_ref[:, D_CONV:2 * D_CONV]
    cu = pc_ref[:, 2 * D_CONV:3 * D_CONV]
    z = cc * cu
    zbuf[8:tile + 8, :] = z
    z1 = zbuf[7:tile + 7, :]
    z2 = zbuf[6:tile + 6, :]
    conv = cw_ref[0:1, :] * z2 + cw_ref[1:2, :] * z1 + cw_ref[2:3, :] * z
    return cb, cc, cu, z, z1, z2, conv


def _band_geometry(block_index):
    qi = lax.broadcasted_iota(jnp.int32, (BLK, BLK), 0)
    kp = lax.broadcasted_iota(jnp.int32, (BLK, BLK), 1)
    use_cur = kp <= qi
    dist = jnp.where(use_cur, qi - kp, qi - kp + BLK).astype(F32)
    valid = use_cur | (block_index > 0)
    return use_cur, dist, valid


def _block_diag(cur, prev, group):
    lane = lax.broadcasted_iota(jnp.int32, cur.shape, 1)

    def halves(t):
        other = pltpu.roll(t, 64, 1)
        lo, hi = (t, other) if group == 0 else (other, t)
        return jnp.where(lane < 64, lo, 0.0), jnp.where(lane >= 64, hi, 0.0)

    return jnp.concatenate(halves(cur) + halves(prev), axis=0).astype(BF16)


def _merge(s4, use_cur):
    return (jnp.where(use_cur, s4[:, 0:BLK], s4[:, 2 * BLK:3 * BLK]),
            jnp.where(use_cur, s4[:, BLK:2 * BLK], s4[:, 3 * BLK:4 * BLK]))


def _split(a, b, use_cur):
    return jnp.concatenate([jnp.where(use_cur, a, 0.0), jnp.where(use_cur, b, 0.0),
                            jnp.where(use_cur, 0.0, a), jnp.where(use_cur, 0.0, b)], axis=1)


def _softmax_head(s, head, sink, dist, valid):
    sc = jnp.where(valid, s - SLOPES[head] * dist, -jnp.inf)
    m = jnp.maximum(jnp.max(sc, axis=-1, keepdims=True), sink)
    p = jnp.exp(sc - m)
    es = jnp.exp(sink - m)
    inv = 1.0 / (jnp.sum(p, axis=-1, keepdims=True) + es)
    return p * inv, es * inv


def _attn_operands(q_ref, kvc_ref, kvp_ref):
    groups = range(N_HEADS // HEADS_PER_KV)
    kbd = [_block_diag(kvc_ref[:, 0:D_KV], kvp_ref[:, 0:D_KV], g) for g in groups]
    vbd = [_block_diag(kvc_ref[:, D_KV:2 * D_KV], kvp_ref[:, D_KV:2 * D_KV], g) for g in groups]
    qps = [(q_ref[:, j * 128:(j + 1) * 128] * SCALE).astype(BF16) for j in range(N_HEADS // 2)]
    return qps, kbd, vbd


def _attn_forward(q_ref, kvc_ref, kvp_ref, ga_ref, sink_ref, gn_ref, geometry, ya_ref, oa_ref, pr_ref, sp_ref):
    use_cur, dist, valid = geometry
    _, kbd, vbd = operands = _attn_operands(q_ref, kvc_ref, kvp_ref)
    yield
    scores = []
    for j, qp in enumerate(operands[0]):
        scores += _merge(_nt(qp, kbd[j // 4]), use_cur)
    yield
    sinks = [sink_ref[0, h] for h in range(N_HEADS)]
    scores = [jnp.where(valid, s - SLOPES[h] * dist, -jnp.inf) for h, s in enumerate(scores)]
    maxes = [jnp.maximum(jnp.max(s, axis=-1, keepdims=True), sinks[h]) for h, s in enumerate(scores)]
    yield
    exps = [jnp.exp(s - m) for s, m in zip(scores, maxes)]
    sink_exps = [jnp.exp(sinks[h] - m) for h, m in enumerate(maxes)]
    yield
    invs = [1.0 / (jnp.sum(e, axis=-1, keepdims=True) + se) for e, se in zip(exps, sink_exps)]
    probs = [e * inv for e, inv in zip(exps, invs)]
    pr_ref[...] = jnp.concatenate(probs, axis=1).astype(BF16)
    lane = lax.broadcasted_iota(jnp.int32, (BLK, 128), 1)
    sp_ref[...] = sum(jnp.where(lane == h, se * inv, 0.0) for h, (se, inv) in enumerate(zip(sink_exps, invs)))
    yield
    p4s = [_split(probs[2 * j], probs[2 * j + 1], use_cur).astype(BF16) for j in range(N_HEADS // 2)]
    ya = jnp.concatenate([_nn(p4, vbd[j // 4]) for j, p4 in enumerate(p4s)], axis=1)
    ya_ref[...] = ya
    yield
    silu, _ = _silu_and_grad(ga_ref[...])
    oa_ref[...] = (ya * _rstd(ya) * gn_ref[...] * silu).astype(BF16)


def _kv_specs(n_blocks, reverse):
    def blk(i):
        return (n_blocks - 1 - i) if reverse else i
    cur = pl.BlockSpec((BLK, 2 * D_KV), lambda i: (blk(i), 0))
    prev = pl.BlockSpec((BLK, 2 * D_KV), lambda i: (jnp.maximum(blk(i) - 1, 0), 0))
    return cur, prev


def _out_proj_loss(x, oc, oa, wo, norm_final, target, pc, conv_w, norm_conv_out):
    seq = x.shape[0]
    tile = TOK_TILE
    n_tiles = seq // tile

    def body(x_ref, oc_ref, oa_ref, wo_ref, gf_ref, t_ref, pc_ref, hcc_ref, hcu_ref, cw_ref, gn_ref,
             dx2_ref, doa_ref, gwo_ref, dpc_ref, small_ref, loss_acc, zbuf, dbuf):
        step = pl.program_id(0)

        def small_add(row, value):
            small_ref[row:row + 1, :] += jnp.sum(value, axis=0, keepdims=True)

        @pl.when(step == 0)
        def _():
            gwo_ref[...] = jnp.zeros_like(gwo_ref)
            small_ref[...] = jnp.zeros_like(small_ref)
            loss_acc[...] = jnp.zeros_like(loss_acc)
            dbuf[tile:tile + 8, :] = jnp.zeros((8, D_CONV), F32)

        oc, oa = oc_ref[...], oa_ref[...]
        x2 = x_ref[...] + _nn(oc, wo_ref[0:D_CONV, :]) + _nn(oa, wo_ref[D_CONV:D_MIX, :])
        r = _rstd(x2)
        xhat = x2 * r
        err = xhat * gf_ref[...] - t_ref[...]
        loss_acc[...] += jnp.sum(err * err, axis=0, keepdims=True) * (0.5 / D_MODEL)
        dy = err * (1.0 / D_MODEL)
        small_add(SMALL_NORM_FINAL, dy * xhat)
        dx2 = _rms_bwd(dy * gf_ref[...], xhat, r)
        dx2_ref[...] = dx2
        db = dx2.astype(BF16)
        do = _nt(db, wo_ref[0:D_CONV, :])
        doa_ref[...] = _nt(db, wo_ref[D_CONV:D_MIX, :])
        gwo_ref[0:D_CONV, :] += _tn(oc, db)
        gwo_ref[D_CONV:D_MIX, :] += _tn(oa, db)

        is_first_tile = step == n_tiles - 1
        zbuf[0:8, :] = jnp.where(is_first_tile, 0.0, hcc_ref[...] * hcu_ref[...])
        cb, cc, cu, z, z1, z2, conv = _conv_core(pc_ref, zbuf, cw_ref)
        silu, dsilu = _silu_and_grad(pc_ref[:, 3 * D_CONV:4 * D_CONV])
        yc = cb * conv
        rc = _rstd(yc)
        chat = yc * rc
        dn = do * silu
        dpc_ref[:, 3 * D_CONV:4 * D_CONV] = (do * (chat * gn_ref[...]) * dsilu).astype(BF16)
        small_add(SMALL_NORM_CONV, dn * chat)
        dyc = _rms_bwd(dn * gn_ref[...], chat, rc)
        dpc_ref[:, 0:D_CONV] = (dyc * conv).astype(BF16)
        dconv = dyc * cb
        small_add(SMALL_CONV_W, dconv * z2)
        small_add(SMALL_CONV_W + 1, dconv * z1)
        small_add(SMALL_CONV_W + 2, dconv * z)
        dbuf[0:tile, :] = dconv
        dz = cw_ref[2:3, :] * dconv + cw_ref[1:2, :] * dbuf[1:tile + 1, :] + cw_ref[0:1, :] * dbuf[2:tile + 2, :]
        dpc_ref[:, D_CONV:2 * D_CONV] = (dz * cu).astype(BF16)
        dpc_ref[:, 2 * D_CONV:3 * D_CONV] = (dz * cc).astype(BF16)
        dbuf[tile:tile + 8, :] = dbuf[0:8, :]

        @pl.when(step == n_tiles - 1)
        def _():
            lane = lax.broadcasted_iota(jnp.int32, (1, D_MODEL), 1)
            small_ref[SMALL_MISC:SMALL_MISC + 1, :] = jnp.where(lane == SMALL_LOSS_LANE, jnp.sum(loss_acc[...]), 0.0)

    def rev(i):
        return n_tiles - 1 - i

    def row(width):
        return pl.BlockSpec((tile, width), lambda i: (rev(i), 0))

    def halo(col_block):
        return pl.BlockSpec((8, D_CONV), lambda i: (jnp.maximum(rev(i) * (tile // 8) - 1, 0), col_block))

    def const(shape):
        return pl.BlockSpec(shape, lambda i: (0, 0))

    return pl.pallas_call(
        body, name="out_proj_loss", grid=(n_tiles,),
        out_shape=(jax.ShapeDtypeStruct((seq, D_MODEL), F32), jax.ShapeDtypeStruct((seq, D_ATTN), F32),
                   jax.ShapeDtypeStruct((D_MIX, D_MODEL), F32), jax.ShapeDtypeStruct((seq, D_PC), BF16),
                   jax.ShapeDtypeStruct((SMALL_ROWS, D_MODEL), F32)),
        in_specs=[row(D_MODEL), row(D_CONV), row(D_ATTN), _resident(wo.shape), _resident((1, D_MODEL)), row(D_MODEL),
                  row(D_PC), halo(1), halo(2), _resident(conv_w.shape), _resident((1, D_CONV))],
        out_specs=(row(D_MODEL), row(D_ATTN), const((D_MIX, D_MODEL)), row(D_PC), const((SMALL_ROWS, D_MODEL))),
        scratch_shapes=[pltpu.VMEM((1, D_MODEL), F32), pltpu.VMEM((tile + 8, D_CONV), F32),
                        pltpu.VMEM((tile + 8, D_CONV), F32)],
        compiler_params=_params(62, ("arbitrary",)),
    )(x, oc, oa, wo, norm_final, target, pc, pc, pc, conv_w, norm_conv_out)


def _attn_bwd(q, kv, ga, ya, doa, probs, sink_probs, norm_attn_out, gwo, small):
    seq = q.shape[0]
    n_blocks = seq // BLK
    stage_steps = (0, n_blocks // 4, n_blocks // 2, (3 * n_blocks) // 4, n_blocks - 1)

    def body(q_ref, kvc_ref, kvp_ref, ga_ref, ya_ref, doa_ref, pr_ref, sp_ref, gn_ref, gwo_ref, small_ref,
             dqg_ref, small_out, gwo_sh, gna_ref, gs_ref, carry, dya_buf, *rs_scratch):
        step = pl.program_id(0)

        @pl.when(step == 0)
        def _():
            gna_ref[...] = jnp.zeros_like(gna_ref)
            gs_ref[...] = jnp.zeros_like(gs_ref)
            carry[...] = jnp.zeros_like(carry)

        ya = ya_ref[...]
        r = _rstd(ya)
        xhat = ya * r
        silu, dsilu = _silu_and_grad(ga_ref[...])
        do = doa_ref[...]
        dn = do * silu
        dqg_ref[:, D_ATTN:2 * D_ATTN] = (do * (xhat * gn_ref[...]) * dsilu).astype(BF16)
        gna_ref[...] += jnp.sum(dn * xhat, axis=0, keepdims=True)
        dya_buf[...] = _rms_bwd(dn * gn_ref[...], xhat, r).astype(BF16)

        use_cur = _band_geometry(n_blocks - 1 - step)[0]
        lane = lax.broadcasted_iota(jnp.int32, (BLK, 128), 1)

        def fold(bd):
            return (jnp.where(lane < 64, bd[0:BLK], 0.0) + jnp.where(lane >= 64, bd[BLK:2 * BLK], 0.0),
                    jnp.where(lane < 64, bd[2 * BLK:3 * BLK], 0.0) + jnp.where(lane >= 64, bd[3 * BLK:4 * BLK], 0.0))

        pairs = range(N_HEADS // 2)
        qps, kbd, vbd = _attn_operands(q_ref, kvc_ref, kvp_ref)
        probs = [pr_ref[:, h * BLK:(h + 1) * BLK].astype(F32) for h in range(N_HEADS)]
        dyps = [dya_buf[:, j * 128:(j + 1) * 128] for j in pairs]
        dps = []
        for j in pairs:
            dps += _merge(_nt(dyps[j], vbd[j // 4]), use_cur)
        deltas = [jnp.sum(p * dp, axis=-1, keepdims=True) for p, dp in zip(probs, dps)]
        dss = [p * (dp - delta) for p, dp, delta in zip(probs, dps, deltas)]
        delta_lanes = sum(jnp.where(lane == h, deltas[h], 0.0) for h in range(N_HEADS))
        gs_ref[...] -= jnp.sum(sp_ref[...] * delta_lanes, axis=0, keepdims=True)
        ds4s = [_split(dss[2 * j], dss[2 * j + 1], use_cur).astype(BF16) for j in pairs]
        p4s = [_split(probs[2 * j], probs[2 * j + 1], use_cur).astype(BF16) for j in pairs]
        dqg_ref[:, 0:D_ATTN] = jnp.concatenate([_nn(ds4s[j], kbd[j // 4]) * SCALE for j in pairs], axis=1).astype(BF16)
        sums = []
        for group in range(N_HEADS // HEADS_PER_KV):
            acc = [jnp.zeros((BLK, 128), F32) for _ in range(4)]
            for j in range(group * 4, group * 4 + 4):
                for slot, part in enumerate(fold(_tn(ds4s[j], qps[j])) + fold(_tn(p4s[j], dyps[j]))):
                    acc[slot] = acc[slot] + part
            sums.append([a + pltpu.roll(a, 64, 1) for a in acc])
        dk_cur, dk_prev, dv_cur, dv_prev = (jnp.where(lane < 64, a, b) for a, b in zip(sums[0], sums[1]))
        dqg_ref[:, 2 * D_ATTN:2 * D_ATTN + 2 * D_KV] = (jnp.concatenate([dk_cur, dv_cur], axis=1) + carry[...]).astype(BF16)
        carry[...] = jnp.concatenate([dk_prev, dv_prev], axis=1)

        @pl.when(step == n_blocks - 1)
        def _():
            small_out[...] = small_ref[...]
            small_out[SMALL_NORM_ATTN:SMALL_NORM_ATTN + 1, :] = gna_ref[...]
            small_out[SMALL_MISC:SMALL_MISC + 1, 0:128] = small_ref[SMALL_MISC:SMALL_MISC + 1, 0:128] + gs_ref[...]

        for at, stage in zip(stage_steps, _rs_wout_stages(gwo_ref, gwo_sh, *rs_scratch)):
            pl.when(step == at)(stage)

    row = pl.BlockSpec((BLK, D_ATTN), lambda i: (n_blocks - 1 - i, 0))
    kv_cur, kv_prev = _kv_specs(n_blocks, reverse=True)
    any_spec = pl.BlockSpec(memory_space=pl.ANY)
    return pl.pallas_call(
        body, name="attn_bwd", grid=(n_blocks,),
        out_shape=(jax.ShapeDtypeStruct((seq, D_QG), BF16), jax.ShapeDtypeStruct(small.shape, F32),
                   jax.ShapeDtypeStruct((gwo.shape[0] // N_CHIPS, gwo.shape[1]), F32)),
        in_specs=[row, kv_cur, kv_prev, row, row, row,
                  pl.BlockSpec((BLK, N_HEADS * BLK), lambda i: (n_blocks - 1 - i, 0)),
                  pl.BlockSpec((BLK, 128), lambda i: (n_blocks - 1 - i, 0)), _resident((1, D_ATTN)),
                  any_spec, _resident(small.shape)],
        out_specs=(pl.BlockSpec((BLK, D_QG), lambda i: (n_blocks - 1 - i, 0)),
                   pl.BlockSpec(small.shape, lambda i: (0, 0)), any_spec),
        scratch_shapes=[pltpu.VMEM((1, D_ATTN), F32), pltpu.VMEM((1, 128), F32),
                        pltpu.VMEM((BLK, 2 * D_KV), F32), pltpu.VMEM((BLK, D_ATTN), BF16)] + _rs_wout_scratch(gwo.shape),
        compiler_params=_params(44, ("arbitrary",)),
    )(q, kv, kv, ga, ya, doa, probs, sink_probs, norm_attn_out, gwo, small)


GBLK = 256
GSUB = 64
PAIR_RING = 4
LAG_PAIR, LAG_HOP1, LAG_HOP2 = 1, 7, 14


def _bwd_in(dpc, dqg, h, wt, x, norm_in, dx2, small):
    seq = x.shape[0]
    n_blk = D_IN_PROJ // GBLK
    per_chip = n_blk // N_CHIPS
    n_slots = (n_blk + 1) // 2
    n_sub = GBLK // GSUB
    chip_rows = D_IN_PROJ // N_CHIPS
    tile = TOK_TILE
    n_tiles = seq // tile
    n_steps = n_blk + max(n_tiles, LAG_HOP2)
    chunk = min(seq, 512)
    blk_q, blk_kv, blk_ga = ROW_Q // GBLK, ROW_KV // GBLK, ROW_GA // GBLK

    def block_of(i):
        k = i % N_CHIPS
        robin = per_chip * ((k % 2) * 2 + k // 2) + i // N_CHIPS
        if isinstance(i, int):
            return robin if i < per_chip * N_CHIPS else i
        return jnp.where(i < per_chip * N_CHIPS, robin, i)

    def owner_of(i):
        return (i // N_CHIPS) % 2

    def slot_of(i):
        return (i // (2 * N_CHIPS)) * N_CHIPS + i % N_CHIPS

    def body(dpc_ref, dqg_ref, wt_ref, h_ref, x_ref, g_ref, dx2_ref, small_ref, gx_ref, small_sum, gwt_sh,
             dh_acc, gni, keep, pbuf, xbuf, land, land2, small_land,
             pair_send, pair_recv, h1_send, h1_recv, h2_send, h2_recv, sw_send, sw_recv, sm_send, sm_recv, out_sem):
        step = pl.program_id(0)
        x_i, y_i, c = lax.axis_index("x"), lax.axis_index("y"), lax.axis_index("c")
        me = 4 * x_i + 2 * y_i + c
        j = 2 * x_i + y_i
        pa = (_xor(x_i, 1 - c), _xor(y_i, c), c)
        pb = (_xor(x_i, c), _xor(y_i, 1 - c), c)
        sib = (x_i, y_i, 1 - c)
        ja = 2 * pa[0] + pa[1]
        jb = 2 * pb[0] + pb[1]
        jd = 3 - j

        def remote(src, dst, send, recv, to):
            return pltpu.make_async_remote_copy(src_ref=src, dst_ref=dst, send_sem=send, recv_sem=recv,
                                                device_id=to, device_id_type=MESH)

        def piece(ref, slot, u, n):
            return ref.at[slot, pl.ds(u * GSUB, n * GSUB), :]

        def chip_rows_at(ref, local, n):
            return ref.at[pl.ds(pl.multiple_of(local, GSUB), n * GSUB), :]

        def pair_copy(i):
            slot = slot_of(i)
            return remote(pbuf.at[i % PAIR_RING], land.at[slot], pair_send.at[slot], pair_recv.at[slot], sib)

        def h1_copy(slot, u, n):
            k = slot * n_sub + u
            return remote(piece(xbuf, slot, u, n), piece(xbuf, slot, u, n), h1_send.at[k], h1_recv.at[k], pa)

        def h2_copy(slot, u, n, local):
            k = slot * n_sub + u
            return remote(piece(xbuf, slot, u, n), chip_rows_at(land2, local, n), h2_send.at[k], h2_recv.at[k], pb)

        def sw_copy(slot, u, n, local):
            k = slot * n_sub + u
            return remote(piece(keep, slot, u, n), chip_rows_at(gwt_sh, local, n), sw_send.at[k], sw_recv.at[k], sib)

        def owned(i):
            return (i >= 0) & (i < n_blk) & (owner_of(i) == c)

        def chip_of(blk, u):
            row = blk * GBLK + u * GSUB
            chip = row // chip_rows
            return chip, row - chip * chip_rows

        def pieces(blk):
            first, local = chip_of(blk, 0)
            whole = first == chip_of(blk, n_sub - 1)[0]
            if isinstance(blk, int):
                return [(True, 0, n_sub, first, local)] if whole else [(True, u, 1) + chip_of(blk, u) for u in range(n_sub)]
            return [(whole, 0, n_sub, first, local)] + [(jnp.logical_not(whole), u, 1) + chip_of(blk, u) for u in range(n_sub)]

        @pl.when(step == 0)
        def _():
            dh_acc[...] = jnp.zeros_like(dh_acc)
            gni[...] = jnp.zeros_like(gni)

        @pl.when(step < n_blk)
        def _():
            from_pc = block_of(step) < blk_q
            block = _tn(jnp.where(from_pc, dpc_ref[...], dqg_ref[...]), h_ref[...])
            for t in range(0, seq, chunk):
                d = jnp.where(from_pc, dpc_ref[t:t + chunk, :], dqg_ref[t:t + chunk, :])
                dh_acc[t:t + chunk, :] += _nn(d, wt_ref[...])

            @pl.when(owner_of(step) == c)
            def _():
                keep[slot_of(step)] = block

            @pl.when(owner_of(step) != c)
            def _():
                @pl.when(step >= 2 * PAIR_RING)
                def _():
                    pair_copy(step - 2 * PAIR_RING).wait_send()
                pbuf[step % PAIR_RING] = block.astype(BF16)
                pair_copy(step).start()

        i1 = step - LAG_PAIR

        @pl.when(owned(i1))
        def _():
            slot = slot_of(i1)
            pair_copy(i1).wait_recv()
            _accumulate(keep.at[slot], land.at[slot])
            for cond, u, n, chip, _ in pieces(block_of(i1)):
                @pl.when(cond & ((chip == ja) | (chip == jd)))
                def _(u=u, n=n):
                    _cast_rows(piece(keep, slot, u, n), piece(xbuf, slot, u, n))
                    h1_copy(slot, u, n).start()

        i2 = step - LAG_HOP1

        @pl.when(owned(i2))
        def _():
            slot = slot_of(i2)
            for cond, u, n, chip, local in pieces(block_of(i2)):
                @pl.when(cond & ((chip == j) | (chip == jb)))
                def _(u=u, n=n, chip=chip, local=local):
                    h1_copy(slot, u, n).wait_recv()
                    _accumulate(piece(keep, slot, u, n), piece(xbuf, slot, u, n))

                    @pl.when(chip == jb)
                    def _():
                        _cast_rows(piece(keep, slot, u, n), piece(xbuf, slot, u, n))
                        h2_copy(slot, u, n, local).start()

                @pl.when(cond & ((chip == ja) | (chip == jd)))
                def _(u=u, n=n):
                    h1_copy(slot, u, n).wait_send()

        i3 = step - LAG_HOP2

        @pl.when(owned(i3))
        def _():
            slot = slot_of(i3)
            for cond, u, n, chip, local in pieces(block_of(i3)):
                @pl.when(cond & (chip == j))
                def _(u=u, n=n, local=local):
                    h2_copy(slot, u, n, local).wait_recv()
                    _accumulate(piece(keep, slot, u, n), chip_rows_at(land2, local, n))
                    mine = pltpu.make_async_copy(piece(keep, slot, u, n), chip_rows_at(gwt_sh, local, n), out_sem.at[0])
                    mine.start()
                    sw_copy(slot, u, n, local).start()
                    mine.wait()

                @pl.when(cond & (chip == jb))
                def _(u=u, n=n, local=local):
                    h2_copy(slot, u, n, local).wait_send()

        e = step - n_blk

        @pl.when((e >= 0) & (e < n_tiles))
        def _():
            dh = dh_acc[pl.ds(pl.multiple_of(e * tile, tile), tile), :]
            xv = x_ref[...]
            r = _rstd(xv)
            xhat = xv * r
            gni[...] += jnp.sum(dh * xhat, axis=0, keepdims=True)
            gx_ref[...] = _rms_bwd(dh * g_ref[...], xhat, r) + dx2_ref[...]

        @pl.when(step == n_steps - 1)
        def _():
            small_land[me] = small_ref[...]
            small_land[me, SMALL_NORM_IN:SMALL_NORM_IN + 1, :] = gni[...]
            others = [(dx, dy, dc) for dx in (0, 1) for dy in (0, 1) for dc in (0, 1)][1:]
            sends = [remote(small_land.at[me], small_land.at[me], sm_send.at[k], sm_recv.at[k],
                            (_xor(x_i, dx), _xor(y_i, dy), _xor(c, dc))) for k, (dx, dy, dc) in enumerate(others)]
            for cp in sends:
                cp.start()
            for i in range(n_blk):
                if i + 2 * PAIR_RING >= n_blk:
                    @pl.when(owner_of(i) != c)
                    def _(i=i):
                        pair_copy(i).wait_send()
                for _, u, n, chip, local in pieces(block_of(i)):
                    @pl.when((j == chip) & (c == owner_of(i)))
                    def _(i=i, u=u, n=n, local=local):
                        sw_copy(slot_of(i), u, n, local).wait_send()

                    @pl.when((j == chip) & (c != owner_of(i)))
                    def _(i=i, u=u, n=n, local=local):
                        sw_copy(slot_of(i), u, n, local).wait_recv()
            for cp in sends:
                cp.wait_recv()
            total = small_land[0]
            for dev in range(1, 8):
                total = total + small_land[dev]
            small_sum[...] = total
            for cp in sends:
                cp.wait_send()

    def blk_at(i):
        return block_of(jnp.clip(i, 0, n_blk - 1))

    last_pc_step = max(i for i in range(n_blk) if block_of(i) < blk_q)

    def next_block(i, in_pc):
        i = jnp.clip(i, 0, n_blk - 1)
        step = jnp.full_like(i, last_pc_step if in_pc else n_blk - 1)
        for ahead in reversed(range(N_CHIPS)):
            cand = jnp.minimum(i + ahead, n_blk - 1)
            step = jnp.where((block_of(cand) < blk_q) == in_pc, cand, step)
        return block_of(step)

    def dqg_block(i):
        b = next_block(i, False)
        q_blk = jnp.clip(b - blk_q, 0, blk_kv - blk_q - 1)
        ga_blk = (D_ATTN // GBLK) + jnp.clip(b - blk_ga, 0, n_blk - blk_ga - 1)
        return jnp.where(b < blk_kv, q_blk, jnp.where(b == blk_kv, 2 * D_ATTN // GBLK, ga_blk))

    def tok(i):
        return (jnp.clip(i - n_blk, 0, n_tiles - 1), 0)

    n_piece = n_slots * n_sub
    dma = pltpu.SemaphoreType.DMA
    return pl.pallas_call(
        body, name="bwd_in", grid=(n_steps,),
        out_shape=(jax.ShapeDtypeStruct((seq, D_MODEL), F32), jax.ShapeDtypeStruct(small.shape, F32),
                   jax.ShapeDtypeStruct((chip_rows, D_MODEL), F32)),
        in_specs=[pl.BlockSpec((seq, GBLK), lambda i: (0, next_block(i, True))),
                  pl.BlockSpec((seq, GBLK), lambda i: (0, dqg_block(i))),
                  pl.BlockSpec((GBLK, D_MODEL), lambda i: (blk_at(i), 0)),
                  _resident(h.shape),
                  pl.BlockSpec((tile, D_MODEL), tok), _resident((1, D_MODEL)), pl.BlockSpec((tile, D_MODEL), tok),
                  _resident(small.shape)],
        out_specs=(pl.BlockSpec((tile, D_MODEL), tok), pl.BlockSpec(small.shape, lambda i: (0, 0)),
                   pl.BlockSpec(memory_space=pl.ANY)),
        scratch_shapes=[pltpu.VMEM((seq, D_MODEL), F32), pltpu.VMEM((1, D_MODEL), F32),
                        pltpu.VMEM((n_slots, GBLK, D_MODEL), F32), pltpu.VMEM((PAIR_RING, GBLK, D_MODEL), BF16),
                        pltpu.VMEM((n_slots, GBLK, D_MODEL), BF16), pltpu.VMEM((n_slots, GBLK, D_MODEL), BF16),
                        pltpu.VMEM((chip_rows, D_MODEL), BF16), pltpu.VMEM((8,) + small.shape, F32),
                        dma((n_slots,)), dma((n_slots,)), dma((n_piece,)), dma((n_piece,)), dma((n_piece,)),
                        dma((n_piece,)), dma((n_piece,)), dma((n_piece,)), dma((7,)), dma((7,)), dma((1,))],
        compiler_params=_params(62, ("arbitrary",)),
    )(dpc, dqg, wt, h, x, norm_in, dx2, small)


def _accumulate(dst_ref, src_ref, rows=16):
    def step(i, carry):
        sl = pl.ds(pl.multiple_of(i * rows, rows), rows)
        dst_ref[sl, :] = dst_ref[sl, :] + src_ref[sl, :].astype(F32)
        return carry
    lax.fori_loop(0, dst_ref.shape[0] // rows, step, 0)


def _rs_wout_scratch(gwo_shape):
    o_half, width = gwo_shape[0] // N_CHIPS // 2, gwo_shape[1]
    return [pltpu.VMEM((4, o_half, width), F32), pltpu.VMEM((4, o_half, width), BF16),
            pltpu.VMEM((4, o_half, width), BF16), pltpu.VMEM((2, o_half, width), BF16),
            pltpu.VMEM((o_half, width), BF16),
            pltpu.SemaphoreType.DMA((8,)), pltpu.SemaphoreType.DMA((8,)), pltpu.SemaphoreType.DMA((4,))]


def _rs_wout_stages(gwo_ref, gwo_sh, acc_o, sb_o, r1o, r2o, r3o, send_sems, recv_sems, local_sems):
    o_rows = gwo_ref.shape[0] // N_CHIPS
    o_half = o_rows // 2
    x, y, c = lax.axis_index("x"), lax.axis_index("y"), lax.axis_index("c")
    j = 2 * x + y
    pa = (_xor(x, 1 - c), _xor(y, c), c)
    pb = (_xor(x, c), _xor(y, 1 - c), c)
    sib = (x, y, 1 - c)
    ja = 2 * pa[0] + pa[1]
    jb = 2 * pb[0] + pb[1]
    jd = 3 - j
    order = (ja, jd, jb, j)
    sib_order = (jb, jd, ja, j)

    def rcopy(k, src, dst, to):
        return pltpu.make_async_remote_copy(src_ref=src, dst_ref=dst, send_sem=send_sems.at[k],
                                            recv_sem=recv_sems.at[k], device_id=to, device_id_type=MESH)

    def o_rows_of(chip, half):
        return gwo_ref.at[pl.ds(pl.multiple_of(chip * o_rows + half * o_half, 8), o_half), :]

    def load_all(chips, half):
        cps = [pltpu.make_async_copy(o_rows_of(chip, half), acc_o.at[s], local_sems.at[s]) for s, chip in enumerate(chips)]
        for cp in cps:
            cp.start()
        return cps

    def pair_send(s):
        return rcopy(s, sb_o.at[s], r1o.at[s], sib)

    def out_half(half):
        return gwo_sh.at[pl.ds(pl.multiple_of(half * o_half, 8), o_half), :]

    hop1 = [rcopy(4 + s, sb_o.at[s], r2o.at[s], pa) for s in range(2)]
    hop2 = rcopy(6, sb_o.at[2], r3o, pb)
    swap = rcopy(7, acc_o.at[3], out_half(c), sib)
    mine = pltpu.make_async_copy(acc_o.at[3], out_half(c), local_sems.at[0])

    def resend(s, copy):
        pair_send(s).wait_send()
        _cast_rows(acc_o.at[s], sb_o.at[s])
        copy.start()

    def stage_pair():
        for s, cp in enumerate(load_all(sib_order, 1 - c)):
            cp.wait()
            _cast_rows(acc_o.at[s], sb_o.at[s])
            pair_send(s).start()

    def stage_hop1():
        for s, cp in enumerate(load_all(order, c)):
            cp.wait()
            pair_send(s).wait_recv()
            _accumulate(acc_o.at[s], r1o.at[s])
            if s < 2:
                resend(s, hop1[s])

    def stage_hop2():
        hop1[1].wait_recv()
        _accumulate(acc_o.at[2], r2o.at[1])
        resend(2, hop2)
        hop1[0].wait_recv()
        _accumulate(acc_o.at[3], r2o.at[0])

    def stage_final():
        hop2.wait_recv()
        _accumulate(acc_o.at[3], r3o)
        mine.start()
        swap.start()

    def stage_drain():
        rcopy(7, acc_o.at[3], out_half(1 - c), sib).wait_recv()
        for cp in [pair_send(3)] + hop1 + [hop2, swap]:
            cp.wait_send()
        mine.wait()

    return stage_pair, stage_hop1, stage_hop2, stage_final, stage_drain


def _adamw(name, w, g, m, v, rows):
    def body(w_ref, g_ref, m_ref, v_ref, go_ref, d_ref, nm_ref, nv_ref):
        _adamw_update(w_ref, g_ref, m_ref, v_ref, go_ref, d_ref, nm_ref, nv_ref)

    spec = pl.BlockSpec((rows, w.shape[1]), lambda i: (i, 0))
    shape = jax.ShapeDtypeStruct(w.shape, F32)
    return pl.pallas_call(
        body, name="adamw_" + name, grid=(w.shape[0] // rows,),
        out_shape=(shape,) * 4, in_specs=[spec] * 4, out_specs=(spec,) * 4,
        compiler_params=_params(32, ("arbitrary",)),
    )(w, g, m, v)


def _adamw_update(w_ref, g_ref, m_ref, v_ref, go_ref, d_ref, nm_ref, nv_ref):
    gv = g_ref[...]
    go_ref[...] = gv
    nm = ADAM_B1 * m_ref[...] + (1.0 - ADAM_B1) * gv
    nv = ADAM_B2 * v_ref[...] + (1.0 - ADAM_B2) * (gv * gv)
    m_hat = nm / (1.0 - ADAM_B1 ** ADAM_STEP)
    v_hat = nv / (1.0 - ADAM_B2 ** ADAM_STEP)
    d_ref[...] = -ADAM_LR * (m_hat / (jnp.sqrt(v_hat) + ADAM_EPS) + ADAM_WD * w_ref[...])
    nm_ref[...] = nm
    nv_ref[...] = nv


def _adamw_small(weights, grads, ms, vs):
    n = len(weights)

    def body(*refs):
        ins, outs = refs[:4 * n], refs[4 * n:]
        for k in range(n):
            _adamw_update(*ins[4 * k:4 * k + 4], *outs[4 * k:4 * k + 4])

    flat = [a for group in zip(weights, grads, ms, vs) for a in group]
    vmem = pl.BlockSpec(memory_space=pltpu.VMEM)
    out = pl.pallas_call(
        body, name="adamw_small",
        out_shape=tuple(jax.ShapeDtypeStruct(w.shape, F32) for w in weights for _ in range(4)),
        in_specs=[vmem] * (4 * n), out_specs=(vmem,) * (4 * n),
    )(*flat)
    return [tuple(out[4 * k:4 * k + 4]) for k in range(n)]


def kernel(x, norm_in, w_in, conv_w, attn_sinks, norm_conv_out, norm_attn_out, w_out, norm_final, loss_target, m_norm_in, m_w_in, m_conv_w, m_attn_sinks, m_norm_conv_out, m_norm_attn_out, m_w_out, m_norm_final, v_norm_in, v_w_in, v_conv_w, v_attn_sinks, v_norm_conv_out, v_norm_attn_out, v_w_out, v_norm_final):
    chip = 2 * lax.axis_index("x") + lax.axis_index("y")
    xs, target = x[0], loss_target[0]
    norm_final2 = norm_final.reshape(1, D_MODEL)
    w_in_t, m_w_in_t, v_w_in_t = w_in[0].T, m_w_in[0].T, v_w_in[0].T

    h, pc, q, kv, ga, oc, ya, oa, probs, sink_probs, wt, cw, wo = _fwd_in(
        xs, norm_in, w_in_t, w_out[0], conv_w[0], norm_conv_out, attn_sinks, norm_attn_out)
    dx2, doa, gwo, dpc, small = _out_proj_loss(xs, oc, oa, wo, norm_final2, target, pc, cw, norm_conv_out)
    dqg, small, gwo_sh = _attn_bwd(q, kv, ga, ya, doa, probs, sink_probs, norm_attn_out, gwo, small)
    grad_x, small_sum, gwt_sh = _bwd_in(dpc, dqg, h, wt, xs, norm_in, dx2, small)

    loss = small_sum[SMALL_MISC, SMALL_LOSS_LANE]
    g_norm_in, g_norm_conv, g_norm_attn = (small_sum[r:r + 1] for r in (SMALL_NORM_IN, SMALL_NORM_CONV, SMALL_NORM_ATTN))
    g_norm_final = small_sum[SMALL_NORM_FINAL]
    g_conv_w = lax.dynamic_slice(small_sum[SMALL_CONV_W:SMALL_CONV_W + 3], (0, chip * (D_CONV // N_CHIPS)),
                                 (3, D_CONV // N_CHIPS))[None]
    g_sinks = small_sum[SMALL_MISC:SMALL_MISC + 1, 0:N_HEADS]

    def two_d(a):
        return a.reshape(-1, a.shape[-1])

    def from_hbm(*arrays):
        return tuple(pltpu.with_memory_space_constraint(a, pltpu.HBM) for a in arrays)

    up_w_in = tuple(o.T[None] for o in _adamw("w_in", *from_hbm(w_in_t, gwt_sh, m_w_in_t, v_w_in_t), 200))
    up_w_out = tuple(o[None] for o in _adamw("w_out", *from_hbm(w_out[0], gwo_sh, m_w_out[0], v_w_out[0]), 128))
    small_w = (norm_in, conv_w, attn_sinks, norm_conv_out, norm_attn_out, norm_final)
    small_g = (g_norm_in, g_conv_w, g_sinks, g_norm_conv, g_norm_attn, g_norm_final)
    small_m = (m_norm_in, m_conv_w, m_attn_sinks, m_norm_conv_out, m_norm_attn_out, m_norm_final)
    small_v = (v_norm_in, v_conv_w, v_attn_sinks, v_norm_conv_out, v_norm_attn_out, v_norm_final)
    up_small = _adamw_small(*(tuple(two_d(a) for a in group) for group in (small_w, small_g, small_m, small_v)))
    up_small = [tuple(o.reshape(w.shape) for o in up) for up, w in zip(up_small, small_w)]
    up_norm_in, up_conv_w, up_sinks, up_norm_conv, up_norm_attn, up_norm_final = up_small
    updates = (up_norm_in, up_w_in, up_conv_w, up_sinks, up_norm_conv, up_norm_attn, up_w_out, up_norm_final)
    grads_out, deltas, new_m, new_v = zip(*updates)
    return (loss, grad_x[None], *grads_out, *deltas, *new_m, *new_v)
```

```python
import jax
import jax.numpy as jnp
from jax import lax
from jax.experimental import pallas as pl
from jax.experimental.pallas import tpu as pltpu

F32 = jnp.float32
BF16 = jnp.bfloat16
MESH = pl.DeviceIdType.MESH

D_MODEL = 1024
D_CONV = 1024
D_ATTN = 1024
D_KV = 128
D_QG = 2 * D_ATTN + 2 * D_KV
D_MIX = D_CONV + D_ATTN
D_PC = 4 * D_CONV
D_IN_PROJ = D_PC + 2 * D_ATTN + 2 * D_KV
ROW_Q = D_PC
ROW_KV = ROW_Q + D_ATTN
ROW_GA = ROW_KV + 2 * D_KV
N_HEADS = 16
HEAD_DIM = 64
HEADS_PER_KV = 8
BLK = 128
N_CHIPS = 4
RMS_EPS = 1e-5
SCALE = HEAD_DIM ** -0.5
SLOPES = tuple(2.0 ** (-8.0 * (h + 1) / N_HEADS) for h in range(N_HEADS))

ADAM_LR, ADAM_B1, ADAM_B2, ADAM_EPS, ADAM_WD, ADAM_STEP = 0.001, 0.9, 0.999, 1e-08, 0.01, 10

SMALL_ROWS = 8
SMALL_NORM_IN, SMALL_NORM_CONV, SMALL_NORM_ATTN, SMALL_NORM_FINAL, SMALL_CONV_W, SMALL_MISC = 0, 1, 2, 3, 4, 7
SMALL_LOSS_LANE = N_HEADS

TOK_TILE = 256
PC_PIECE = 512
MIB = 1 << 20


def _params(vmem_mib, semantics=None):
    return pltpu.CompilerParams(dimension_semantics=semantics, vmem_limit_bytes=vmem_mib * MIB)


def _nn(a, b):
    return jnp.dot(a, b, preferred_element_type=F32)


def _nt(a, b):
    return lax.dot_general(a, b, (((1,), (1,)), ((), ())), preferred_element_type=F32)


def _tn(a, b):
    return lax.dot_general(a, b, (((0,), (0,)), ((), ())), preferred_element_type=F32)


def _rstd(v):
    return lax.rsqrt(jnp.mean(v * v, axis=-1, keepdims=True) + RMS_EPS)


def _rms_bwd(g, xhat, rstd):
    return rstd * (g - xhat * jnp.mean(g * xhat, axis=-1, keepdims=True))


def _silu_and_grad(g):
    s = jax.nn.sigmoid(g)
    return g * s, s * (1.0 + g * (1.0 - s))


def _resident(shape):
    return pl.BlockSpec(shape, lambda *_: (0,) * len(shape), pipeline_mode=pl.Buffered(1))


def _xor(a, b):
    return a + b - 2 * a * b


def _cast_rows(src_ref, dst_ref, rows=32):
    def step(i, carry):
        sl = pl.ds(pl.multiple_of(i * rows, rows), rows)
        dst_ref[sl, :] = src_ref[sl, :].astype(dst_ref.dtype)
        return carry
    lax.fori_loop(0, src_ref.shape[0] // rows, step, 0)


def _ag_scratch(shard_shape):
    rows, width = shard_shape
    return [pltpu.VMEM((rows, width), F32), pltpu.VMEM((rows, width), BF16), pltpu.VMEM((3, rows // 2, width), BF16),
            pltpu.SemaphoreType.DMA((6,)), pltpu.SemaphoreType.DMA((6,)), pltpu.SemaphoreType.DMA((4,))]


def _ag_stages(sh_ref, out, f32_buf, own, land, send_sems, recv_sems, local_sems):
    rows = sh_ref.shape[0]
    half = rows // 2
    x, y, c = lax.axis_index("x"), lax.axis_index("y"), lax.axis_index("c")
    j = 2 * x + y
    p1 = (_xor(x, c), _xor(y, 1 - c), c)
    p2 = (_xor(x, 1 - c), _xor(y, c), c)
    sib = (x, y, 1 - c)
    j1 = 2 * p1[0] + p1[1]
    j2 = 2 * p2[0] + p2[1]
    j3 = 3 - j

    def rows_of(chip, hf):
        return out.at[pl.ds(pl.multiple_of(chip * rows + hf * half, 16), half), :]

    def rcopy(k, src, dst, to):
        return pltpu.make_async_remote_copy(src_ref=src, dst_ref=dst, send_sem=send_sems.at[k],
                                            recv_sem=recv_sems.at[k], device_id=to, device_id_type=MESH)

    my_half = own.at[pl.ds(pl.multiple_of(c * half, 16), half), :]
    hop1 = rcopy(0, my_half, land.at[0], p1)
    hop2_own = rcopy(1, my_half, land.at[1], p2)
    hop2_fwd = rcopy(2, land.at[0], land.at[2], p2)
    swaps = [rcopy(3 + s, land.at[s], rows_of(chip, c), sib) for s, chip in enumerate((j1, j2, j3))]
    keeps = [pltpu.make_async_copy(land.at[s], rows_of(chip, c), local_sems.at[1 + s]) for s, chip in enumerate((j1, j2, j3))]
    load = pltpu.make_async_copy(sh_ref, f32_buf, local_sems.at[0])
    own_out = pltpu.make_async_copy(own, out.at[pl.ds(pl.multiple_of(j * rows, 16), rows), :], local_sems.at[0])

    def stage_send():
        load.start()
        load.wait()
        _cast_rows(f32_buf, own)
        own_out.start()
        hop1.start()

    def stage_forward():
        hop1.wait_recv()
        hop2_own.start()
        hop2_fwd.start()
        swaps[0].start()
        keeps[0].start()

    def stage_publish():
        hop2_own.wait_recv()
        swaps[1].start()
        keeps[1].start()
        hop2_fwd.wait_recv()
        swaps[2].start()
        keeps[2].start()

    def stage_drain():
        for s, chip in enumerate((j2, j1, j3)):
            rcopy(3 + s, my_half, rows_of(chip, 1 - c), sib).wait_recv()
        for cp in [hop1, hop2_own, hop2_fwd] + swaps:
            cp.wait_send()
        for cp in [own_out] + keeps:
            cp.wait()

    return stage_send, stage_forward, stage_publish, stage_drain


AG_CAST_ROWS = 400


def _gather_resident(sh_ref, out, f32_buf, send_sems, recv_sems, local_sems):
    rows = sh_ref.shape[0]
    half = rows // 2
    x, y, c = lax.axis_index("x"), lax.axis_index("y"), lax.axis_index("c")
    j = 2 * x + y
    p1 = (_xor(x, c), _xor(y, 1 - c), c)
    p2 = (_xor(x, 1 - c), _xor(y, c), c)
    sib = (x, y, 1 - c)
    j1 = 2 * p1[0] + p1[1]
    j2 = 2 * p2[0] + p2[1]
    j3 = 3 - j

    def rows_of(chip, hf):
        return out.at[pl.ds(pl.multiple_of(chip * rows + hf * half, 16), half), :]

    def send(k, chip, to):
        return pltpu.make_async_remote_copy(src_ref=rows_of(chip, c), dst_ref=rows_of(chip, c), send_sem=send_sems.at[k],
                                            recv_sem=recv_sems.at[k], device_id=to, device_id_type=MESH)

    per_half = half // AG_CAST_ROWS

    def cast_own(chunk):
        lo = pl.multiple_of(chunk * AG_CAST_ROWS, 16)
        load = pltpu.make_async_copy(sh_ref.at[pl.ds(lo, AG_CAST_ROWS), :], f32_buf, local_sems.at[0])
        load.start()
        load.wait()
        _cast_rows(f32_buf, out.at[pl.ds(pl.multiple_of(j * rows + lo, 16), AG_CAST_ROWS), :], rows=16)

    for k in range(per_half):
        cast_own(c * per_half + k)
    hop1 = send(0, j, p1)
    hop1.start()
    for k in range(per_half):
        cast_own((1 - c) * per_half + k)
    hop1.wait_recv()
    sends = [hop1, send(1, j, p2), send(2, j1, p2), send(3, j1, sib)]
    for cp in sends[1:]:
        cp.start()
    sends[1].wait_recv()
    sends.append(send(4, j2, sib))
    sends[-1].start()
    sends[2].wait_recv()
    sends.append(send(5, j3, sib))
    sends[-1].start()
    for k in (3, 4, 5):
        send(k, j, sib).wait_recv()
    for cp in sends:
        cp.wait_send()


def _fwd_in(x, norm_in, wt_sh, wo_sh, cw_sh, norm_conv_out, sinks, norm_attn_out):
    seq = x.shape[0]
    tile = TOK_TILE
    n_tiles = seq // tile
    stage_steps = (0, n_tiles // 4, (5 * n_tiles) // 8, n_tiles - 1)
    blocks = tile // BLK
    cw_cols = cw_sh.shape[1]

    def body(x_ref, g_ref, wtsh_ref, wo_ref, cwsh_ref, gn_ref, sink_ref, gna_ref,
             h_ref, pc_ref, q_ref, kv_ref, ga_ref, oc_ref, ya_ref, oa_ref, pr_ref, sp_ref, wt_out, cw_ref, wo_out,
             zbuf, kv_last, wt_ref, f32_buf, cw_land, wt_send, wt_recv, wt_local, cw_send, cw_recv, cw_local, *ag_scratch):
        step = pl.program_id(0)
        to_hbm = pltpu.make_async_copy(wt_ref, wt_out, wt_local.at[0])

        @pl.when(step == 0)
        def _():
            zbuf[0:8, :] = jnp.zeros((8, D_CONV), F32)
            kv_last[...] = jnp.zeros_like(kv_last)
            x_i, y_i, c = lax.axis_index("x"), lax.axis_index("y"), lax.axis_index("c")
            j = 2 * x_i + y_i
            p1 = (_xor(x_i, c), _xor(y_i, 1 - c), c)
            p2 = (_xor(x_i, 1 - c), _xor(y_i, c), c)
            j1 = 2 * p1[0] + p1[1]

            def cw_copy(k, src, chip, to):
                return pltpu.make_async_remote_copy(src_ref=src, dst_ref=cw_land.at[chip], send_sem=cw_send.at[k],
                                                    recv_sem=cw_recv.at[k], device_id=to, device_id_type=MESH)

            mine = pltpu.make_async_copy(cwsh_ref, cw_land.at[j], cw_local.at[0])
            mine.start()
            first = cw_copy(0, cwsh_ref, j, p1)
            first.start()
            _gather_resident(wtsh_ref, wt_ref, f32_buf, wt_send, wt_recv, wt_local)
            to_hbm.start()
            first.wait_recv()
            second = [cw_copy(1, cwsh_ref, j, p2), cw_copy(2, cw_land.at[j1], j1, p2)]
            for cp in second:
                cp.start()
            for cp in second:
                cp.wait_recv()
            for cp in [first] + second:
                cp.wait_send()
            mine.wait()
            for chip in range(N_CHIPS):
                cw_ref[:, chip * cw_cols:(chip + 1) * cw_cols] = cw_land[chip]

        stages = _ag_stages(wo_ref, wo_out, *ag_scratch)
        for at, stage in zip(stage_steps[:-1], stages[:-1]):
            pl.when(step == at)(stage)

        def attention(b):
            rows = pl.ds(b * BLK, BLK)
            kv_prev = kv_last if b == 0 else kv_ref.at[pl.ds((b - 1) * BLK, BLK), :]
            return _attn_forward(q_ref.at[rows, :], kv_ref.at[rows, :], kv_prev, ga_ref.at[rows, :], sink_ref, gna_ref,
                                 _band_geometry(step * blocks + b), ya_ref.at[rows, :], oa_ref.at[rows, :],
                                 pr_ref.at[rows, :], sp_ref.at[rows, :])

        xv = x_ref[...]
        h = (xv * _rstd(xv) * g_ref[...]).astype(BF16)
        h_ref[...] = h
        q_ref[...] = _nt(h, wt_ref[ROW_Q:ROW_KV, :])
        kv_ref[...] = _nt(h, wt_ref[ROW_KV:ROW_GA, :])
        ga_ref[...] = _nt(h, wt_ref[ROW_GA:D_IN_PROJ, :])
        attention_blocks = [attention(b) for b in range(blocks)]
        for lo in range(0, D_PC, PC_PIECE):
            pc_ref[:, lo:lo + PC_PIECE] = _nt(h, wt_ref[lo:lo + PC_PIECE, :])
            for stages_of_block in attention_blocks:
                next(stages_of_block, None)
        for stages_of_block in attention_blocks:
            for _ in stages_of_block:
                pass
        kv_last[...] = kv_ref[tile - BLK:tile, :]

        cb, _, _, _, _, _, conv = _conv_core(pc_ref, zbuf, cw_ref)
        yc = cb * conv
        silu, _ = _silu_and_grad(pc_ref[:, 3 * D_CONV:4 * D_CONV])
        oc_ref[...] = (yc * _rstd(yc) * gn_ref[...] * silu).astype(BF16)
        zbuf[0:8, :] = zbuf[tile:tile + 8, :]

        @pl.when(step == stage_steps[-1])
        def _():
            stages[-1]()
            to_hbm.wait()

    def row(width):
        return pl.BlockSpec((tile, width), lambda i: (i, 0))

    any_spec = pl.BlockSpec(memory_space=pl.ANY)
    dma = pltpu.SemaphoreType.DMA
    wt_shape = (N_CHIPS * wt_sh.shape[0], wt_sh.shape[1])
    cw_shape = (cw_sh.shape[0], N_CHIPS * cw_cols)
    return pl.pallas_call(
        body, name="fwd_in", grid=(n_tiles,),
        out_shape=(jax.ShapeDtypeStruct((seq, D_MODEL), BF16), jax.ShapeDtypeStruct((seq, D_PC), F32),
                   jax.ShapeDtypeStruct((seq, D_ATTN), F32), jax.ShapeDtypeStruct((seq, 2 * D_KV), F32),
                   jax.ShapeDtypeStruct((seq, D_ATTN), F32), jax.ShapeDtypeStruct((seq, D_CONV), BF16),
                   pltpu.HBM((seq, D_ATTN), F32), pltpu.HBM((seq, D_ATTN), BF16),
                   pltpu.HBM((seq, N_HEADS * BLK), BF16), pltpu.HBM((seq, 128), F32),
                   pltpu.HBM(wt_shape, BF16), jax.ShapeDtypeStruct(cw_shape, F32),
                   jax.ShapeDtypeStruct((N_CHIPS * wo_sh.shape[0], wo_sh.shape[1]), BF16)),
        in_specs=[row(D_MODEL), _resident((1, D_MODEL)), any_spec, any_spec, any_spec, _resident((1, D_CONV)),
                  pl.BlockSpec(memory_space=pltpu.SMEM), _resident((1, D_ATTN))],
        out_specs=(row(D_MODEL), row(D_PC), row(D_ATTN), row(2 * D_KV), row(D_ATTN), row(D_CONV), row(D_ATTN),
                   row(D_ATTN), row(N_HEADS * BLK), row(128), any_spec, pl.BlockSpec(cw_shape, lambda i: (0, 0)),
                   any_spec),
        scratch_shapes=[pltpu.VMEM((tile + 8, D_CONV), F32), pltpu.VMEM((BLK, 2 * D_KV), F32),
                        pltpu.VMEM(wt_shape, BF16), pltpu.VMEM((AG_CAST_ROWS, wt_sh.shape[1]), F32),
                        pltpu.VMEM((N_CHIPS,) + cw_sh.shape, F32),
                        dma((6,)), dma((6,)), dma((1,)), dma((3,)), dma((3,)), dma((1,))] + _ag_scratch(wo_sh.shape),
        compiler_params=_params(62, ("arbitrary",)),
    )(x, norm_in, wt_sh, wo_sh, cw_sh, norm_conv_out, sinks, norm_attn_out)


def _conv_core(pc_ref, zbuf, cw_ref):
    tile = pc_ref.shape[0]
    cb = pc_ref[:, 0:D_CONV]
    cc = pc_ref[:, D_CONV:2 * D_CONV]
    cu = pc_ref[:, 2 * D_CONV:3 * D_CONV]
    z = cc * cu
    zbuf[8:tile + 8, :] = z
    z1 = zbuf[7:tile + 7, :]
    z2 = zbuf[6:tile + 6, :]
    conv = cw_ref[0:1, :] * z2 + cw_ref[1:2, :] * z1 + cw_ref[2:3, :] * z
    return cb, cc, cu, z, z1, z2, conv


def _band_geometry(block_index):
    qi = lax.broadcasted_iota(jnp.int32, (BLK, BLK), 0)
    kp = lax.broadcasted_iota(jnp.int32, (BLK, BLK), 1)
    use_cur = kp <= qi
    dist = jnp.where(use_cur, qi - kp, qi - kp + BLK).astype(F32)
    valid = use_cur | (block_index > 0)
    return use_cur, dist, valid


def _block_diag(cur, prev, group):
    lane = lax.broadcasted_iota(jnp.int32, cur.shape, 1)

    def halves(t):
        other = pltpu.roll(t, 64, 1)
        lo, hi = (t, other) if group == 0 else (other, t)
        return jnp.where(lane < 64, lo, 0.0), jnp.where(lane >= 64, hi, 0.0)

    return jnp.concatenate(halves(cur) + halves(prev), axis=0).astype(BF16)


def _merge(s4, use_cur):
    return (jnp.where(use_cur, s4[:, 0:BLK], s4[:, 2 * BLK:3 * BLK]),
            jnp.where(use_cur, s4[:, BLK:2 * BLK], s4[:, 3 * BLK:4 * BLK]))


def _split(a, b, use_cur):
    return jnp.concatenate([jnp.where(use_cur, a, 0.0), jnp.where(use_cur, b, 0.0),
                            jnp.where(use_cur, 0.0, a), jnp.where(use_cur, 0.0, b)], axis=1)


def _softmax_head(s, head, sink, dist, valid):
    sc = jnp.where(valid, s - SLOPES[head] * dist, -jnp.inf)
    m = jnp.maximum(jnp.max(sc, axis=-1, keepdims=True), sink)
    p = jnp.exp(sc - m)
    es = jnp.exp(sink - m)
    inv = 1.0 / (jnp.sum(p, axis=-1, keepdims=True) + es)
    return p * inv, es * inv


def _attn_operands(q_ref, kvc_ref, kvp_ref):
    groups = range(N_HEADS // HEADS_PER_KV)
    kbd = [_block_diag(kvc_ref[:, 0:D_KV], kvp_ref[:, 0:D_KV], g) for g in groups]
    vbd = [_block_diag(kvc_ref[:, D_KV:2 * D_KV], kvp_ref[:, D_KV:2 * D_KV], g) for g in groups]
    qps = [(q_ref[:, j * 128:(j + 1) * 128] * SCALE).astype(BF16) for j in range(N_HEADS // 2)]
    return qps, kbd, vbd


def _attn_forward(q_ref, kvc_ref, kvp_ref, ga_ref, sink_ref, gn_ref, geometry, ya_ref, oa_ref, pr_ref, sp_ref):
    use_cur, dist, valid = geometry
    _, kbd, vbd = operands = _attn_operands(q_ref, kvc_ref, kvp_ref)
    yield
    scores = []
    for j, qp in enumerate(operands[0]):
        scores += _merge(_nt(qp, kbd[j // 4]), use_cur)
    yield
    sinks = [sink_ref[0, h] for h in range(N_HEADS)]
    scores = [jnp.where(valid, s - SLOPES[h] * dist, -jnp.inf) for h, s in enumerate(scores)]
    maxes = [jnp.maximum(jnp.max(s, axis=-1, keepdims=True), sinks[h]) for h, s in enumerate(scores)]
    yield
    exps = [jnp.exp(s - m) for s, m in zip(scores, maxes)]
    sink_exps = [jnp.exp(sinks[h] - m) for h, m in enumerate(maxes)]
    yield
    invs = [1.0 / (jnp.sum(e, axis=-1, keepdims=True) + se) for e, se in zip(exps, sink_exps)]
    probs = [e * inv for e, inv in zip(exps, invs)]
    pr_ref[...] = jnp.concatenate(probs, axis=1).astype(BF16)
    lane = lax.broadcasted_iota(jnp.int32, (BLK, 128), 1)
    sp_ref[...] = sum(jnp.where(lane == h, se * inv, 0.0) for h, (se, inv) in enumerate(zip(sink_exps, invs)))
    yield
    p4s = [_split(probs[2 * j], probs[2 * j + 1], use_cur).astype(BF16) for j in range(N_HEADS // 2)]
    ya = jnp.concatenate([_nn(p4, vbd[j // 4]) for j, p4 in enumerate(p4s)], axis=1)
    ya_ref[...] = ya
    yield
    silu, _ = _silu_and_grad(ga_ref[...])
    oa_ref[...] = (ya * _rstd(ya) * gn_ref[...] * silu).astype(BF16)


def _kv_specs(n_blocks, reverse):
    def blk(i):
        return (n_blocks - 1 - i) if reverse else i
    cur = pl.BlockSpec((BLK, 2 * D_KV), lambda i: (blk(i), 0))
    prev = pl.BlockSpec((BLK, 2 * D_KV), lambda i: (jnp.maximum(blk(i) - 1, 0), 0))
    return cur, prev


def _out_proj_loss(x, oc, oa, wo, norm_final, target, pc, conv_w, norm_conv_out):
    seq = x.shape[0]
    tile = TOK_TILE
    n_tiles = seq // tile

    def body(x_ref, oc_ref, oa_ref, wo_ref, gf_ref, t_ref, pc_ref, hcc_ref, hcu_ref, cw_ref, gn_ref,
             dx2_ref, doa_ref, gwo_ref, dpc_ref, small_ref, loss_acc, zbuf, dbuf):
        step = pl.program_id(0)

        def small_add(row, value):
            small_ref[row:row + 1, :] += jnp.sum(value, axis=0, keepdims=True)

        @pl.when(step == 0)
        def _():
            gwo_ref[...] = jnp.zeros_like(gwo_ref)
            small_ref[...] = jnp.zeros_like(small_ref)
            loss_acc[...] = jnp.zeros_like(loss_acc)
            dbuf[tile:tile + 8, :] = jnp.zeros((8, D_CONV), F32)

        oc, oa = oc_ref[...], oa_ref[...]
        x2 = x_ref[...] + _nn(oc, wo_ref[0:D_CONV, :]) + _nn(oa, wo_ref[D_CONV:D_MIX, :])
        r = _rstd(x2)
        xhat = x2 * r
        err = xhat * gf_ref[...] - t_ref[...]
        loss_acc[...] += jnp.sum(err * err, axis=0, keepdims=True) * (0.5 / D_MODEL)
        dy = err * (1.0 / D_MODEL)
        small_add(SMALL_NORM_FINAL, dy * xhat)
        dx2 = _rms_bwd(dy * gf_ref[...], xhat, r)
        dx2_ref[...] = dx2
        db = dx2.astype(BF16)
        do = _nt(db, wo_ref[0:D_CONV, :])
        doa_ref[...] = _nt(db, wo_ref[D_CONV:D_MIX, :])
        gwo_ref[0:D_CONV, :] += _tn(oc, db)
        gwo_ref[D_CONV:D_MIX, :] += _tn(oa, db)

        is_first_tile = step == n_tiles - 1
        zbuf[0:8, :] = jnp.where(is_first_tile, 0.0, hcc_ref[...] * hcu_ref[...])
        cb, cc, cu, z, z1, z2, conv = _conv_core(pc_ref, zbuf, cw_ref)
        silu, dsilu = _silu_and_grad(pc_ref[:, 3 * D_CONV:4 * D_CONV])
        yc = cb * conv
        rc = _rstd(yc)
        chat = yc * rc
        dn = do * silu
        dpc_ref[:, 3 * D_CONV:4 * D_CONV] = (do * (chat * gn_ref[...]) * dsilu).astype(BF16)
        small_add(SMALL_NORM_CONV, dn * chat)
        dyc = _rms_bwd(dn * gn_ref[...], chat, rc)
        dpc_ref[:, 0:D_CONV] = (dyc * conv).astype(BF16)
        dconv = dyc * cb
        small_add(SMALL_CONV_W, dconv * z2)
        small_add(SMALL_CONV_W + 1, dconv * z1)
        small_add(SMALL_CONV_W + 2, dconv * z)
        dbuf[0:tile, :] = dconv
        dz = cw_ref[2:3, :] * dconv + cw_ref[1:2, :] * dbuf[1:tile + 1, :] + cw_ref[0:1, :] * dbuf[2:tile + 2, :]
        dpc_ref[:, D_CONV:2 * D_CONV] = (dz * cu).astype(BF16)
        dpc_ref[:, 2 * D_CONV:3 * D_CONV] = (dz * cc).astype(BF16)
        dbuf[tile:tile + 8, :] = dbuf[0:8, :]

        @pl.when(step == n_tiles - 1)
        def _():
            lane = lax.broadcasted_iota(jnp.int32, (1, D_MODEL), 1)
            small_ref[SMALL_MISC:SMALL_MISC + 1, :] = jnp.where(lane == SMALL_LOSS_LANE, jnp.sum(loss_acc[...]), 0.0)

    def rev(i):
        return n_tiles - 1 - i

    def row(width):
        return pl.BlockSpec((tile, width), lambda i: (rev(i), 0))

    def halo(col_block):
        return pl.BlockSpec((8, D_CONV), lambda i: (jnp.maximum(rev(i) * (tile // 8) - 1, 0), col_block))

    def const(shape):
        return pl.BlockSpec(shape, lambda i: (0, 0))

    return pl.pallas_call(
        body, name="out_proj_loss", grid=(n_tiles,),
        out_shape=(jax.ShapeDtypeStruct((seq, D_MODEL), F32), jax.ShapeDtypeStruct((seq, D_ATTN), F32),
                   jax.ShapeDtypeStruct((D_MIX, D_MODEL), F32), jax.ShapeDtypeStruct((seq, D_PC), BF16),
                   jax.ShapeDtypeStruct((SMALL_ROWS, D_MODEL), F32)),
        in_specs=[row(D_MODEL), row(D_CONV), row(D_ATTN), _resident(wo.shape), _resident((1, D_MODEL)), row(D_MODEL),
                  row(D_PC), halo(1), halo(2), _resident(conv_w.shape), _resident((1, D_CONV))],
        out_specs=(row(D_MODEL), row(D_ATTN), const((D_MIX, D_MODEL)), row(D_PC), const((SMALL_ROWS, D_MODEL))),
        scratch_shapes=[pltpu.VMEM((1, D_MODEL), F32), pltpu.VMEM((tile + 8, D_CONV), F32),
                        pltpu.VMEM((tile + 8, D_CONV), F32)],
        compiler_params=_params(62, ("arbitrary",)),
    )(x, oc, oa, wo, norm_final, target, pc, pc, pc, conv_w, norm_conv_out)


def _attn_bwd(q, kv, ga, ya, doa, probs, sink_probs, norm_attn_out, gwo, small):
    seq = q.shape[0]
    n_blocks = seq // BLK
    stage_steps = (0, n_blocks // 4, n_blocks // 2, (3 * n_blocks) // 4, n_blocks - 1)

    def body(q_ref, kvc_ref, kvp_ref, ga_ref, ya_ref, doa_ref, pr_ref, sp_ref, gn_ref, gwo_ref, small_ref,
             dqg_ref, small_out, gwo_sh, gna_ref, gs_ref, carry, dya_buf, *rs_scratch):
        step = pl.program_id(0)

        @pl.when(step == 0)
        def _():
            gna_ref[...] = jnp.zeros_like(gna_ref)
            gs_ref[...] = jnp.zeros_like(gs_ref)
            carry[...] = jnp.zeros_like(carry)

        ya = ya_ref[...]
        r = _rstd(ya)
        xhat = ya * r
        silu, dsilu = _silu_and_grad(ga_ref[...])
        do = doa_ref[...]
        dn = do * silu
        dqg_ref[:, D_ATTN:2 * D_ATTN] = (do * (xhat * gn_ref[...]) * dsilu).astype(BF16)
        gna_ref[...] += jnp.sum(dn * xhat, axis=0, keepdims=True)
        dya_buf[...] = _rms_bwd(dn * gn_ref[...], xhat, r).astype(BF16)

        use_cur = _band_geometry(n_blocks - 1 - step)[0]
        lane = lax.broadcasted_iota(jnp.int32, (BLK, 128), 1)

        def fold(bd):
            return (jnp.where(lane < 64, bd[0:BLK], 0.0) + jnp.where(lane >= 64, bd[BLK:2 * BLK], 0.0),
                    jnp.where(lane < 64, bd[2 * BLK:3 * BLK], 0.0) + jnp.where(lane >= 64, bd[3 * BLK:4 * BLK], 0.0))

        pairs = range(N_HEADS // 2)
        qps, kbd, vbd = _attn_operands(q_ref, kvc_ref, kvp_ref)
        probs = [pr_ref[:, h * BLK:(h + 1) * BLK].astype(F32) for h in range(N_HEADS)]
        dyps = [dya_buf[:, j * 128:(j + 1) * 128] for j in pairs]
        dps = []
        for j in pairs:
            dps += _merge(_nt(dyps[j], vbd[j // 4]), use_cur)
        deltas = [jnp.sum(p * dp, axis=-1, keepdims=True) for p, dp in zip(probs, dps)]
        dss = [p * (dp - delta) for p, dp, delta in zip(probs, dps, deltas)]
        delta_lanes = sum(jnp.where(lane == h, deltas[h], 0.0) for h in range(N_HEADS))
        gs_ref[...] -= jnp.sum(sp_ref[...] * delta_lanes, axis=0, keepdims=True)
        ds4s = [_split(dss[2 * j], dss[2 * j + 1], use_cur).astype(BF16) for j in pairs]
        p4s = [_split(probs[2 * j], probs[2 * j + 1], use_cur).astype(BF16) for j in pairs]
        dqg_ref[:, 0:D_ATTN] = jnp.concatenate([_nn(ds4s[j], kbd[j // 4]) * SCALE for j in pairs], axis=1).astype(BF16)
        sums = []
        for group in range(N_HEADS // HEADS_PER_KV):
            acc = [jnp.zeros((BLK, 128), F32) for _ in range(4)]
            for j in range(group * 4, group * 4 + 4):
                for slot, part in enumerate(fold(_tn(ds4s[j], qps[j])) + fold(_tn(p4s[j], dyps[j]))):
                    acc[slot] = acc[slot] + part
            sums.append([a + pltpu.roll(a, 64, 1) for a in acc])
        dk_cur, dk_prev, dv_cur, dv_prev = (jnp.where(lane < 64, a, b) for a, b in zip(sums[0], sums[1]))
        dqg_ref[:, 2 * D_ATTN:2 * D_ATTN + 2 * D_KV] = (jnp.concatenate([dk_cur, dv_cur], axis=1) + carry[...]).astype(BF16)
        carry[...] = jnp.concatenate([dk_prev, dv_prev], axis=1)

        @pl.when(step == n_blocks - 1)
        def _():
            small_out[...] = small_ref[...]
            small_out[SMALL_NORM_ATTN:SMALL_NORM_ATTN + 1, :] = gna_ref[...]
            small_out[SMALL_MISC:SMALL_MISC + 1, 0:128] = small_ref[SMALL_MISC:SMALL_MISC + 1, 0:128] + gs_ref[...]

        for at, stage in zip(stage_steps, _rs_wout_stages(gwo_ref, gwo_sh, *rs_scratch)):
            pl.when(step == at)(stage)

    row = pl.BlockSpec((BLK, D_ATTN), lambda i: (n_blocks - 1 - i, 0))
    kv_cur, kv_prev = _kv_specs(n_blocks, reverse=True)
    any_spec = pl.BlockSpec(memory_space=pl.ANY)
    return pl.pallas_call(
        body, name="attn_bwd", grid=(n_blocks,),
        out_shape=(jax.ShapeDtypeStruct((seq, D_QG), BF16), jax.ShapeDtypeStruct(small.shape, F32),
                   jax.ShapeDtypeStruct((gwo.shape[0] // N_CHIPS, gwo.shape[1]), F32)),
        in_specs=[row, kv_cur, kv_prev, row, row, row,
                  pl.BlockSpec((BLK, N_HEADS * BLK), lambda i: (n_blocks - 1 - i, 0)),
                  pl.BlockSpec((BLK, 128), lambda i: (n_blocks - 1 - i, 0)), _resident((1, D_ATTN)),
                  any_spec, _resident(small.shape)],
        out_specs=(pl.BlockSpec((BLK, D_QG), lambda i: (n_blocks - 1 - i, 0)),
                   pl.BlockSpec(small.shape, lambda i: (0, 0)), any_spec),
        scratch_shapes=[pltpu.VMEM((1, D_ATTN), F32), pltpu.VMEM((1, 128), F32),
                        pltpu.VMEM((BLK, 2 * D_KV), F32), pltpu.VMEM((BLK, D_ATTN), BF16)] + _rs_wout_scratch(gwo.shape),
        compiler_params=_params(44, ("arbitrary",)),
    )(q, kv, kv, ga, ya, doa, probs, sink_probs, norm_attn_out, gwo, small)


GBLK = 256
GSUB = 64
PAIR_RING = 4
LAG_PAIR, LAG_HOP1, LAG_HOP2 = 1, 7, 14


def _bwd_in(dpc, dqg, h, wt, x, norm_in, dx2, small):
    seq = x.shape[0]
    n_blk = D_IN_PROJ // GBLK
    per_chip = n_blk // N_CHIPS
    n_slots = (n_blk + 1) // 2
    n_sub = GBLK // GSUB
    chip_rows = D_IN_PROJ // N_CHIPS
    tile = TOK_TILE
    n_tiles = seq // tile
    n_steps = n_blk + max(n_tiles, LAG_HOP2)
    chunk = min(seq, 512)
    blk_q, blk_kv, blk_ga = ROW_Q // GBLK, ROW_KV // GBLK, ROW_GA // GBLK

    def block_of(i):
        k = i % N_CHIPS
        robin = per_chip * ((k % 2) * 2 + k // 2) + i // N_CHIPS
        if isinstance(i, int):
            return robin if i < per_chip * N_CHIPS else i
        return jnp.where(i < per_chip * N_CHIPS, robin, i)

    def owner_of(i):
        return (i // N_CHIPS) % 2

    def slot_of(i):
        return (i // (2 * N_CHIPS)) * N_CHIPS + i % N_CHIPS

    def body(dpc_ref, dqg_ref, wt_ref, h_ref, x_ref, g_ref, dx2_ref, small_ref, gx_ref, small_sum, gwt_sh,
             dh_acc, gni, keep, pbuf, xbuf, land, land2, small_land,
             pair_send, pair_recv, h1_send, h1_recv, h2_send, h2_recv, sw_send, sw_recv, sm_send, sm_recv, out_sem):
        step = pl.program_id(0)
        x_i, y_i, c = lax.axis_index("x"), lax.axis_index("y"), lax.axis_index("c")
        me = 4 * x_i + 2 * y_i + c
        j = 2 * x_i + y_i
        pa = (_xor(x_i, 1 - c), _xor(y_i, c), c)
        pb = (_xor(x_i, c), _xor(y_i, 1 - c), c)
        sib = (x_i, y_i, 1 - c)
        ja = 2 * pa[0] + pa[1]
        jb = 2 * pb[0] + pb[1]
        jd = 3 - j

        def remote(src, dst, send, recv, to):
            return pltpu.make_async_remote_copy(src_ref=src, dst_ref=dst, send_sem=send, recv_sem=recv,
                                                device_id=to, device_id_type=MESH)

        def piece(ref, slot, u, n):
            return ref.at[slot, pl.ds(u * GSUB, n * GSUB), :]

        def chip_rows_at(ref, local, n):
            return ref.at[pl.ds(pl.multiple_of(local, GSUB), n * GSUB), :]

        def pair_copy(i):
            slot = slot_of(i)
            return remote(pbuf.at[i % PAIR_RING], land.at[slot], pair_send.at[slot], pair_recv.at[slot], sib)

        def h1_copy(slot, u, n):
            k = slot * n_sub + u
            return remote(piece(xbuf, slot, u, n), piece(xbuf, slot, u, n), h1_send.at[k], h1_recv.at[k], pa)

        def h2_copy(slot, u, n, local):
            k = slot * n_sub + u
            return remote(piece(xbuf, slot, u, n), chip_rows_at(land2, local, n), h2_send.at[k], h2_recv.at[k], pb)

        def sw_copy(slot, u, n, local):
            k = slot * n_sub + u
            return remote(piece(keep, slot, u, n), chip_rows_at(gwt_sh, local, n), sw_send.at[k], sw_recv.at[k], sib)

        def owned(i):
            return (i >= 0) & (i < n_blk) & (owner_of(i) == c)

        def chip_of(blk, u):
            row = blk * GBLK + u * GSUB
            chip = row // chip_rows
            return chip, row - chip * chip_rows

        def pieces(blk):
            first, local = chip_of(blk, 0)
            whole = first == chip_of(blk, n_sub - 1)[0]
            if isinstance(blk, int):
                return [(True, 0, n_sub, first, local)] if whole else [(True, u, 1) + chip_of(blk, u) for u in range(n_sub)]
            return [(whole, 0, n_sub, first, local)] + [(jnp.logical_not(whole), u, 1) + chip_of(blk, u) for u in range(n_sub)]

        @pl.when(step == 0)
        def _():
            dh_acc[...] = jnp.zeros_like(dh_acc)
            gni[...] = jnp.zeros_like(gni)

        @pl.when(step < n_blk)
        def _():
            from_pc = block_of(step) < blk_q
            block = _tn(jnp.where(from_pc, dpc_ref[...], dqg_ref[...]), h_ref[...])
            for t in range(0, seq, chunk):
                d = jnp.where(from_pc, dpc_ref[t:t + chunk, :], dqg_ref[t:t + chunk, :])
                dh_acc[t:t + chunk, :] += _nn(d, wt_ref[...])

            @pl.when(owner_of(step) == c)
            def _():
                keep[slot_of(step)] = block

            @pl.when(owner_of(step) != c)
            def _():
                @pl.when(step >= 2 * PAIR_RING)
                def _():
                    pair_copy(step - 2 * PAIR_RING).wait_send()
                pbuf[step % PAIR_RING] = block.astype(BF16)
                pair_copy(step).start()

        i1 = step - LAG_PAIR

        @pl.when(owned(i1))
        def _():
            slot = slot_of(i1)
            pair_copy(i1).wait_recv()
            _accumulate(keep.at[slot], land.at[slot])
            for cond, u, n, chip, _ in pieces(block_of(i1)):
                @pl.when(cond & ((chip == ja) | (chip == jd)))
                def _(u=u, n=n):
                    _cast_rows(piece(keep, slot, u, n), piece(xbuf, slot, u, n))
                    h1_copy(slot, u, n).start()

        i2 = step - LAG_HOP1

        @pl.when(owned(i2))
        def _():
            slot = slot_of(i2)
            for cond, u, n, chip, local in pieces(block_of(i2)):
                @pl.when(cond & ((chip == j) | (chip == jb)))
                def _(u=u, n=n, chip=chip, local=local):
                    h1_copy(slot, u, n).wait_recv()
                    _accumulate(piece(keep, slot, u, n), piece(xbuf, slot, u, n))

                    @pl.when(chip == jb)
                    def _():
                        _cast_rows(piece(keep, slot, u, n), piece(xbuf, slot, u, n))
                        h2_copy(slot, u, n, local).start()

                @pl.when(cond & ((chip == ja) | (chip == jd)))
                def _(u=u, n=n):
                    h1_copy(slot, u, n).wait_send()

        i3 = step - LAG_HOP2

        @pl.when(owned(i3))
        def _():
            slot = slot_of(i3)
            for cond, u, n, chip, local in pieces(block_of(i3)):
                @pl.when(cond & (chip == j))
                def _(u=u, n=n, local=local):
                    h2_copy(slot, u, n, local).wait_recv()
                    _accumulate(piece(keep, slot, u, n), chip_rows_at(land2, local, n))
                    mine = pltpu.make_async_copy(piece(keep, slot, u, n), chip_rows_at(gwt_sh, local, n), out_sem.at[0])
                    mine.start()
                    sw_copy(slot, u, n, local).start()
                    mine.wait()

                @pl.when(cond & (chip == jb))
                def _(u=u, n=n, local=local):
                    h2_copy(slot, u, n, local).wait_send()

        e = step - n_blk

        @pl.when((e >= 0) & (e < n_tiles))
        def _():
            dh = dh_acc[pl.ds(pl.multiple_of(e * tile, tile), tile), :]
            xv = x_ref[...]
            r = _rstd(xv)
            xhat = xv * r
            gni[...] += jnp.sum(dh * xhat, axis=0, keepdims=True)
            gx_ref[...] = _rms_bwd(dh * g_ref[...], xhat, r) + dx2_ref[...]

        @pl.when(step == n_steps - 1)
        def _():
            small_land[me] = small_ref[...]
            small_land[me, SMALL_NORM_IN:SMALL_NORM_IN + 1, :] = gni[...]
            others = [(dx, dy, dc) for dx in (0, 1) for dy in (0, 1) for dc in (0, 1)][1:]
            sends = [remote(small_land.at[me], small_land.at[me], sm_send.at[k], sm_recv.at[k],
                            (_xor(x_i, dx), _xor(y_i, dy), _xor(c, dc))) for k, (dx, dy, dc) in enumerate(others)]
            for cp in sends:
                cp.start()
            for i in range(n_blk):
                if i + 2 * PAIR_RING >= n_blk:
                    @pl.when(owner_of(i) != c)
                    def _(i=i):
                        pair_copy(i).wait_send()
                for _, u, n, chip, local in pieces(block_of(i)):
                    @pl.when((j == chip) & (c == owner_of(i)))
                    def _(i=i, u=u, n=n, local=local):
                        sw_copy(slot_of(i), u, n, local).wait_send()

                    @pl.when((j == chip) & (c != owner_of(i)))
                    def _(i=i, u=u, n=n, local=local):
                        sw_copy(slot_of(i), u, n, local).wait_recv()
            for cp in sends:
                cp.wait_recv()
            total = small_land[0]
            for dev in range(1, 8):
                total = total + small_land[dev]
            small_sum[...] = total
            for cp in sends:
                cp.wait_send()

    def blk_at(i):
        return block_of(jnp.clip(i, 0, n_blk - 1))

    last_pc_step = max(i for i in range(n_blk) if block_of(i) < blk_q)

    def next_block(i, in_pc):
        i = jnp.clip(i, 0, n_blk - 1)
        step = jnp.full_like(i, last_pc_step if in_pc else n_blk - 1)
        for ahead in reversed(range(N_CHIPS)):
            cand = jnp.minimum(i + ahead, n_blk - 1)
            step = jnp.where((block_of(cand) < blk_q) == in_pc, cand, step)
        return block_of(step)

    def dqg_block(i):
        b = next_block(i, False)
        q_blk = jnp.clip(b - blk_q, 0, blk_kv - blk_q - 1)
        ga_blk = (D_ATTN // GBLK) + jnp.clip(b - blk_ga, 0, n_blk - blk_ga - 1)
        return jnp.where(b < blk_kv, q_blk, jnp.where(b == blk_kv, 2 * D_ATTN // GBLK, ga_blk))

    def tok(i):
        return (jnp.clip(i - n_blk, 0, n_tiles - 1), 0)

    n_piece = n_slots * n_sub
    dma = pltpu.SemaphoreType.DMA
    return pl.pallas_call(
        body, name="bwd_in", grid=(n_steps,),
        out_shape=(jax.ShapeDtypeStruct((seq, D_MODEL), F32), jax.ShapeDtypeStruct(small.shape, F32),
                   jax.ShapeDtypeStruct((chip_rows, D_MODEL), F32)),
        in_specs=[pl.BlockSpec((seq, GBLK), lambda i: (0, next_block(i, True))),
                  pl.BlockSpec((seq, GBLK), lambda i: (0, dqg_block(i))),
                  pl.BlockSpec((GBLK, D_MODEL), lambda i: (blk_at(i), 0)),
                  _resident(h.shape),
                  pl.BlockSpec((tile, D_MODEL), tok), _resident((1, D_MODEL)), pl.BlockSpec((tile, D_MODEL), tok),
                  _resident(small.shape)],
        out_specs=(pl.BlockSpec((tile, D_MODEL), tok), pl.BlockSpec(small.shape, lambda i: (0, 0)),
                   pl.BlockSpec(memory_space=pl.ANY)),
        scratch_shapes=[pltpu.VMEM((seq, D_MODEL), F32), pltpu.VMEM((1, D_MODEL), F32),
                        pltpu.VMEM((n_slots, GBLK, D_MODEL), F32), pltpu.VMEM((PAIR_RING, GBLK, D_MODEL), BF16),
                        pltpu.VMEM((n_slots, GBLK, D_MODEL), BF16), pltpu.VMEM((n_slots, GBLK, D_MODEL), BF16),
                        pltpu.VMEM((chip_rows, D_MODEL), BF16), pltpu.VMEM((8,) + small.shape, F32),
                        dma((n_slots,)), dma((n_slots,)), dma((n_piece,)), dma((n_piece,)), dma((n_piece,)),
                        dma((n_piece,)), dma((n_piece,)), dma((n_piece,)), dma((7,)), dma((7,)), dma((1,))],
        compiler_params=_params(62, ("arbitrary",)),
    )(dpc, dqg, wt, h, x, norm_in, dx2, small)


def _accumulate(dst_ref, src_ref, rows=16):
    def step(i, carry):
        sl = pl.ds(pl.multiple_of(i * rows, rows), rows)
        dst_ref[sl, :] = dst_ref[sl, :] + src_ref[sl, :].astype(F32)
        return carry
    lax.fori_loop(0, dst_ref.shape[0] // rows, step, 0)


def _rs_wout_scratch(gwo_shape):
    o_half, width = gwo_shape[0] // N_CHIPS // 2, gwo_shape[1]
    return [pltpu.VMEM((4, o_half, width), F32), pltpu.VMEM((4, o_half, width), BF16),
            pltpu.VMEM((4, o_half, width), BF16), pltpu.VMEM((2, o_half, width), BF16),
            pltpu.VMEM((o_half, width), BF16),
            pltpu.SemaphoreType.DMA((8,)), pltpu.SemaphoreType.DMA((8,)), pltpu.SemaphoreType.DMA((4,))]


def _rs_wout_stages(gwo_ref, gwo_sh, acc_o, sb_o, r1o, r2o, r3o, send_sems, recv_sems, local_sems):
    o_rows = gwo_ref.shape[0] // N_CHIPS
    o_half = o_rows // 2
    x, y, c = lax.axis_index("x"), lax.axis_index("y"), lax.axis_index("c")
    j = 2 * x + y
    pa = (_xor(x, 1 - c), _xor(y, c), c)
    pb = (_xor(x, c), _xor(y, 1 - c), c)
    sib = (x, y, 1 - c)
    ja = 2 * pa[0] + pa[1]
    jb = 2 * pb[0] + pb[1]
    jd = 3 - j
    order = (ja, jd, jb, j)
    sib_order = (jb, jd, ja, j)

    def rcopy(k, src, dst, to):
        return pltpu.make_async_remote_copy(src_ref=src, dst_ref=dst, send_sem=send_sems.at[k],
                                            recv_sem=recv_sems.at[k], device_id=to, device_id_type=MESH)

    def o_rows_of(chip, half):
        return gwo_ref.at[pl.ds(pl.multiple_of(chip * o_rows + half * o_half, 8), o_half), :]

    def load_all(chips, half):
        cps = [pltpu.make_async_copy(o_rows_of(chip, half), acc_o.at[s], local_sems.at[s]) for s, chip in enumerate(chips)]
        for cp in cps:
            cp.start()
        return cps

    def pair_send(s):
        return rcopy(s, sb_o.at[s], r1o.at[s], sib)

    def out_half(half):
        return gwo_sh.at[pl.ds(pl.multiple_of(half * o_half, 8), o_half), :]

    hop1 = [rcopy(4 + s, sb_o.at[s], r2o.at[s], pa) for s in range(2)]
    hop2 = rcopy(6, sb_o.at[2], r3o, pb)
    swap = rcopy(7, acc_o.at[3], out_half(c), sib)
    mine = pltpu.make_async_copy(acc_o.at[3], out_half(c), local_sems.at[0])

    def resend(s, copy):
        pair_send(s).wait_send()
        _cast_rows(acc_o.at[s], sb_o.at[s])
        copy.start()

    def stage_pair():
        for s, cp in enumerate(load_all(sib_order, 1 - c)):
            cp.wait()
            _cast_rows(acc_o.at[s], sb_o.at[s])
            pair_send(s).start()

    def stage_hop1():
        for s, cp in enumerate(load_all(order, c)):
            cp.wait()
            pair_send(s).wait_recv()
            _accumulate(acc_o.at[s], r1o.at[s])
            if s < 2:
                resend(s, hop1[s])

    def stage_hop2():
        hop1[1].wait_recv()
        _accumulate(acc_o.at[2], r2o.at[1])
        resend(2, hop2)
        hop1[0].wait_recv()
        _accumulate(acc_o.at[3], r2o.at[0])

    def stage_final():
        hop2.wait_recv()
        _accumulate(acc_o.at[3], r3o)
        mine.start()
        swap.start()

    def stage_drain():
        rcopy(7, acc_o.at[3], out_half(1 - c), sib).wait_recv()
        for cp in [pair_send(3)] + hop1 + [hop2, swap]:
            cp.wait_send()
        mine.wait()

    return stage_pair, stage_hop1, stage_hop2, stage_final, stage_drain


def _adamw(name, w, g, m, v, rows):
    def body(w_ref, g_ref, m_ref, v_ref, go_ref, d_ref, nm_ref, nv_ref):
        _adamw_update(w_ref, g_ref, m_ref, v_ref, go_ref, d_ref, nm_ref, nv_ref)

    spec = pl.BlockSpec((rows, w.shape[1]), lambda i: (i, 0))
    shape = jax.ShapeDtypeStruct(w.shape, F32)
    return pl.pallas_call(
        body, name="adamw_" + name, grid=(w.shape[0] // rows,),
        out_shape=(shape,) * 4, in_specs=[spec] * 4, out_specs=(spec,) * 4,
        compiler_params=_params(32, ("arbitrary",)),
    )(w, g, m, v)


def _adamw_update(w_ref, g_ref, m_ref, v_ref, go_ref, d_ref, nm_ref, nv_ref):
    gv = g_ref[...]
    go_ref[...] = gv
    nm = ADAM_B1 * m_ref[...] + (1.0 - ADAM_B1) * gv
    nv = ADAM_B2 * v_ref[...] + (1.0 - ADAM_B2) * (gv * gv)
    m_hat = nm / (1.0 - ADAM_B1 ** ADAM_STEP)
    v_hat = nv / (1.0 - ADAM_B2 ** ADAM_STEP)
    d_ref[...] = -ADAM_LR * (m_hat / (jnp.sqrt(v_hat) + ADAM_EPS) + ADAM_WD * w_ref[...])
    nm_ref[...] = nm
    nv_ref[...] = nv


def _adamw_small(weights, grads, ms, vs):
    n = len(weights)

    def body(*refs):
        ins, outs = refs[:4 * n], refs[4 * n:]
        for k in range(n):
            _adamw_update(*ins[4 * k:4 * k + 4], *outs[4 * k:4 * k + 4])

    flat = [a for group in zip(weights, grads, ms, vs) for a in group]
    vmem = pl.BlockSpec(memory_space=pltpu.VMEM)
    out = pl.pallas_call(
        body, name="adamw_small",
        out_shape=tuple(jax.ShapeDtypeStruct(w.shape, F32) for w in weights for _ in range(4)),
        in_specs=[vmem] * (4 * n), out_specs=(vmem,) * (4 * n),
    )(*flat)
    return [tuple(out[4 * k:4 * k + 4]) for k in range(n)]


def kernel(x, norm_in, w_in, conv_w, attn_sinks, norm_conv_out, norm_attn_out, w_out, norm_final, loss_target, m_norm_in, m_w_in, m_conv_w, m_attn_sinks, m_norm_conv_out, m_norm_attn_out, m_w_out, m_norm_final, v_norm_in, v_w_in, v_conv_w, v_attn_sinks, v_norm_conv_out, v_norm_attn_out, v_w_out, v_norm_final):
    chip = 2 * lax.axis_index("x") + lax.axis_index("y")
    xs, target = x[0], loss_target[0]
    norm_final2 = norm_final.reshape(1, D_MODEL)
    w_in_t, m_w_in_t, v_w_in_t = w_in[0].T, m_w_in[0].T, v_w_in[0].T

    h, pc, q, kv, ga, oc, ya, oa, probs, sink_probs, wt, cw, wo = _fwd_in(
        xs, norm_in, w_in_t, w_out[0], conv_w[0], norm_conv_out, attn_sinks, norm_attn_out)
    dx2, doa, gwo, dpc, small = _out_proj_loss(xs, oc, oa, wo, norm_final2, target, pc, cw, norm_conv_out)
    dqg, small, gwo_sh = _attn_bwd(q, kv, ga, ya, doa, probs, sink_probs, norm_attn_out, gwo, small)
    grad_x, small_sum, gwt_sh = _bwd_in(dpc, dqg, h, wt, xs, norm_in, dx2, small)

    loss = small_sum[SMALL_MISC, SMALL_LOSS_LANE]
    g_norm_in, g_norm_conv, g_norm_attn = (small_sum[r:r + 1] for r in (SMALL_NORM_IN, SMALL_NORM_CONV, SMALL_NORM_ATTN))
    g_norm_final = small_sum[SMALL_NORM_FINAL]
    g_conv_w = lax.dynamic_slice(small_sum[SMALL_CONV_W:SMALL_CONV_W + 3], (0, chip * (D_CONV // N_CHIPS)),
                                 (3, D_CONV // N_CHIPS))[None]
    g_sinks = small_sum[SMALL_MISC:SMALL_MISC + 1, 0:N_HEADS]

    def two_d(a):
        return a.reshape(-1, a.shape[-1])

    def from_hbm(*arrays):
        return tuple(pltpu.with_memory_space_constraint(a, pltpu.HBM) for a in arrays)

    up_w_in = tuple(o.T[None] for o in _adamw("w_in", *from_hbm(w_in_t, gwt_sh, m_w_in_t, v_w_in_t), 200))
    up_w_out = tuple(o[None] for o in _adamw("w_out", *from_hbm(w_out[0], gwo_sh, m_w_out[0], v_w_out[0]), 128))
    small_w = (norm_in, conv_w, attn_sinks, norm_conv_out, norm_attn_out, norm_final)
    small_g = (g_norm_in, g_conv_w, g_sinks, g_norm_conv, g_norm_attn, g_norm_final)
    small_m = (m_norm_in, m_conv_w, m_attn_sinks, m_norm_conv_out, m_norm_attn_out, m_norm_final)
    small_v = (v_norm_in, v_conv_w, v_attn_sinks, v_norm_conv_out, v_norm_attn_out, v_norm_final)
    up_small = _adamw_small(*(tuple(two_d(a) for a in group) for group in (small_w, small_g, small_m, small_v)))
    up_small = [tuple(o.reshape(w.shape) for o in up) for up, w in zip(up_small, small_w)]
    up_norm_in, up_conv_w, up_sinks, up_norm_conv, up_norm_attn, up_norm_final = up_small
    updates = (up_norm_in, up_w_in, up_conv_w, up_sinks, up_norm_conv, up_norm_attn, up_w_out, up_norm_final)
    grads_out, deltas, new_m, new_v = zip(*updates)
    return (loss, grad_x[None], *grads_out, *deltas, *new_m, *new_v)
```

```python
import jax
import jax.numpy as jnp
from jax import lax
from jax.experimental import pallas as pl
from jax.experimental.pallas import tpu as pltpu

F32 = jnp.float32
BF16 = jnp.bfloat16
MESH = pl.DeviceIdType.MESH

D_MODEL = 1024
D_CONV = 1024
D_ATTN = 1024
D_KV = 128
D_QG = 2 * D_ATTN + 2 * D_KV
D_MIX = D_CONV + D_ATTN
D_PC = 4 * D_CONV
D_IN_PROJ = D_PC + 2 * D_ATTN + 2 * D_KV
ROW_Q = D_PC
ROW_KV = ROW_Q + D_ATTN
ROW_GA = ROW_KV + 2 * D_KV
N_HEADS = 16
HEAD_DIM = 64
HEADS_PER_KV = 8
BLK = 128
N_CHIPS = 4
RMS_EPS = 1e-5
SCALE = HEAD_DIM ** -0.5
SLOPES = tuple(2.0 ** (-8.0 * (h + 1) / N_HEADS) for h in range(N_HEADS))

ADAM_LR, ADAM_B1, ADAM_B2, ADAM_EPS, ADAM_WD, ADAM_STEP = 0.001, 0.9, 0.999, 1e-08, 0.01, 10

SMALL_ROWS = 8
SMALL_NORM_IN, SMALL_NORM_CONV, SMALL_NORM_ATTN, SMALL_NORM_FINAL, SMALL_CONV_W, SMALL_MISC = 0, 1, 2, 3, 4, 7
SMALL_LOSS_LANE = N_HEADS

TOK_TILE = 256
PC_PIECE = 512
MIB = 1 << 20


def _params(vmem_mib, semantics=None):
    return pltpu.CompilerParams(dimension_semantics=semantics, vmem_limit_bytes=vmem_mib * MIB)


def _nn(a, b):
    return jnp.dot(a, b, preferred_element_type=F32)


def _nt(a, b):
    return lax.dot_general(a, b, (((1,), (1,)), ((), ())), preferred_element_type=F32)


def _tn(a, b):
    return lax.dot_general(a, b, (((0,), (0,)), ((), ())), preferred_element_type=F32)


def _rstd(v):
    return lax.rsqrt(jnp.mean(v * v, axis=-1, keepdims=True) + RMS_EPS)


def _rms_bwd(g, xhat, rstd):
    return rstd * (g - xhat * jnp.mean(g * xhat, axis=-1, keepdims=True))


def _silu_and_grad(g):
    s = jax.nn.sigmoid(g)
    return g * s, s * (1.0 + g * (1.0 - s))


def _resident(shape):
    return pl.BlockSpec(shape, lambda *_: (0,) * len(shape), pipeline_mode=pl.Buffered(1))


def _xor(a, b):
    return a + b - 2 * a * b


def _cast_rows(src_ref, dst_ref, rows=32):
    def step(i, carry):
        sl = pl.ds(pl.multiple_of(i * rows, rows), rows)
        dst_ref[sl, :] = src_ref[sl, :].astype(dst_ref.dtype)
        return carry
    lax.fori_loop(0, src_ref.shape[0] // rows, step, 0)


def _ag_scratch(shard_shape):
    rows, width = shard_shape
    return [pltpu.VMEM((rows, width), F32), pltpu.VMEM((rows, width), BF16), pltpu.VMEM((3, rows // 2, width), BF16),
            pltpu.SemaphoreType.DMA((6,)), pltpu.SemaphoreType.DMA((6,)), pltpu.SemaphoreType.DMA((4,))]


def _ag_stages(sh_ref, out, f32_buf, own, land, send_sems, recv_sems, local_sems):
    rows = sh_ref.shape[0]
    half = rows // 2
    x, y, c = lax.axis_index("x"), lax.axis_index("y"), lax.axis_index("c")
    j = 2 * x + y
    p1 = (_xor(x, c), _xor(y, 1 - c), c)
    p2 = (_xor(x, 1 - c), _xor(y, c), c)
    sib = (x, y, 1 - c)
    j1 = 2 * p1[0] + p1[1]
    j2 = 2 * p2[0] + p2[1]
    j3 = 3 - j

    def rows_of(chip, hf):
        return out.at[pl.ds(pl.multiple_of(chip * rows + hf * half, 16), half), :]

    def rcopy(k, src, dst, to):
        return pltpu.make_async_remote_copy(src_ref=src, dst_ref=dst, send_sem=send_sems.at[k],
                                            recv_sem=recv_sems.at[k], device_id=to, device_id_type=MESH)

    my_half = own.at[pl.ds(pl.multiple_of(c * half, 16), half), :]
    hop1 = rcopy(0, my_half, land.at[0], p1)
    hop2_own = rcopy(1, my_half, land.at[1], p2)
    hop2_fwd = rcopy(2, land.at[0], land.at[2], p2)
    swaps = [rcopy(3 + s, land.at[s], rows_of(chip, c), sib) for s, chip in enumerate((j1, j2, j3))]
    keeps = [pltpu.make_async_copy(land.at[s], rows_of(chip, c), local_sems.at[1 + s]) for s, chip in enumerate((j1, j2, j3))]
    load = pltpu.make_async_copy(sh_ref, f32_buf, local_sems.at[0])
    own_out = pltpu.make_async_copy(own, out.at[pl.ds(pl.multiple_of(j * rows, 16), rows), :], local_sems.at[0])

    def stage_send():
        load.start()
        load.wait()
        _cast_rows(f32_buf, own)
        own_out.start()
        hop1.start()

    def stage_forward():
        hop1.wait_recv()
        hop2_own.start()
        hop2_fwd.start()
        swaps[0].start()
        keeps[0].start()

    def stage_publish():
        hop2_own.wait_recv()
        swaps[1].start()
        keeps[1].start()
        hop2_fwd.wait_recv()
        swaps[2].start()
        keeps[2].start()

    def stage_drain():
        for s, chip in enumerate((j2, j1, j3)):
            rcopy(3 + s, my_half, rows_of(chip, 1 - c), sib).wait_recv()
        for cp in [hop1, hop2_own, hop2_fwd] + swaps:
            cp.wait_send()
        for cp in [own_out] + keeps:
            cp.wait()

    return stage_send, stage_forward, stage_publish, stage_drain


AG_CAST_ROWS = 400


def _gather_resident(sh_ref, out, f32_buf, send_sems, recv_sems, local_sems):
    rows = sh_ref.shape[0]
    half = rows // 2
    x, y, c = lax.axis_index("x"), lax.axis_index("y"), lax.axis_index("c")
    j = 2 * x + y
    p1 = (_xor(x, c), _xor(y, 1 - c), c)
    p2 = (_xor(x, 1 - c), _xor(y, c), c)
    sib = (x, y, 1 - c)
    j1 = 2 * p1[0] + p1[1]
    j2 = 2 * p2[0] + p2[1]
    j3 = 3 - j

    def rows_of(chip, hf):
        return out.at[pl.ds(pl.multiple_of(chip * rows + hf * half, 16), half), :]

    def send(k, chip, to):
        return pltpu.make_async_remote_copy(src_ref=rows_of(chip, c), dst_ref=rows_of(chip, c), send_sem=send_sems.at[k],
                                            recv_sem=recv_sems.at[k], device_id=to, device_id_type=MESH)

    per_half = half // AG_CAST_ROWS

    def cast_own(chunk):
        lo = pl.multiple_of(chunk * AG_CAST_ROWS, 16)
        load = pltpu.make_async_copy(sh_ref.at[pl.ds(lo, AG_CAST_ROWS), :], f32_buf, local_sems.at[0])
        load.start()
        load.wait()
        _cast_rows(f32_buf, out.at[pl.ds(pl.multiple_of(j * rows + lo, 16), AG_CAST_ROWS), :], rows=16)

    for k in range(per_half):
        cast_own(c * per_half + k)
    hop1 = send(0, j, p1)
    hop1.start()
    for k in range(per_half):
        cast_own((1 - c) * per_half + k)
    hop1.wait_recv()
    sends = [hop1, send(1, j, p2), send(2, j1, p2), send(3, j1, sib)]
    for cp in sends[1:]:
        cp.start()
    sends[1].wait_recv()
    sends.append(send(4, j2, sib))
    sends[-1].start()
    sends[2].wait_recv()
    sends.append(send(5, j3, sib))
    sends[-1].start()
    for k in (3, 4, 5):
        send(k, j, sib).wait_recv()
    for cp in sends:
        cp.wait_send()


def _fwd_in(x, norm_in, wt_sh, wo_sh, cw_sh, norm_conv_out, sinks, norm_attn_out):
    seq = x.shape[0]
    tile = TOK_TILE
    n_tiles = seq // tile
    stage_steps = (0, n_tiles // 4, (5 * n_tiles) // 8, n_tiles - 1)
    blocks = tile // BLK
    cw_cols = cw_sh.shape[1]

    def body(x_ref, g_ref, wtsh_ref, wo_ref, cwsh_ref, gn_ref, sink_ref, gna_ref,
             h_ref, pc_ref, q_ref, kv_ref, ga_ref, oc_ref, ya_ref, oa_ref, pr_ref, sp_ref, wt_out, cw_ref, wo_out,
             zbuf, kv_last, wt_ref, f32_buf, cw_land, wt_send, wt_recv, wt_local, cw_send, cw_recv, cw_local, *ag_scratch):
        step = pl.program_id(0)
        to_hbm = pltpu.make_async_copy(wt_ref, wt_out, wt_local.at[0])

        @pl.when(step == 0)
        def _():
            zbuf[0:8, :] = jnp.zeros((8, D_CONV), F32)
            kv_last[...] = jnp.zeros_like(kv_last)
            x_i, y_i, c = lax.axis_index("x"), lax.axis_index("y"), lax.axis_index("c")
            j = 2 * x_i + y_i
            p1 = (_xor(x_i, c), _xor(y_i, 1 - c), c)
            p2 = (_xor(x_i, 1 - c), _xor(y_i, c), c)
            j1 = 2 * p1[0] + p1[1]

            def cw_copy(k, src, chip, to):
                return pltpu.make_async_remote_copy(src_ref=src, dst_ref=cw_land.at[chip], send_sem=cw_send.at[k],
                                                    recv_sem=cw_recv.at[k], device_id=to, device_id_type=MESH)

            mine = pltpu.make_async_copy(cwsh_ref, cw_land.at[j], cw_local.at[0])
            mine.start()
            first = cw_copy(0, cwsh_ref, j, p1)
            first.start()
            _gather_resident(wtsh_ref, wt_ref, f32_buf, wt_send, wt_recv, wt_local)
            to_hbm.start()
            first.wait_recv()
            second = [cw_copy(1, cwsh_ref, j, p2), cw_copy(2, cw_land.at[j1], j1, p2)]
            for cp in second:
                cp.start()
            for cp in second:
                cp.wait_recv()
            for cp in [first] + second:
                cp.wait_send()
            mine.wait()
            for chip in range(N_CHIPS):
                cw_ref[:, chip * cw_cols:(chip + 1) * cw_cols] = cw_land[chip]

        stages = _ag_stages(wo_ref, wo_out, *ag_scratch)
        for at, stage in zip(stage_steps[:-1], stages[:-1]):
            pl.when(step == at)(stage)

        def attention(b):
            rows = pl.ds(b * BLK, BLK)
            kv_prev = kv_last if b == 0 else kv_ref.at[pl.ds((b - 1) * BLK, BLK), :]
            return _attn_forward(q_ref.at[rows, :], kv_ref.at[rows, :], kv_prev, ga_ref.at[rows, :], sink_ref, gna_ref,
                                 _band_geometry(step * blocks + b), ya_ref.at[rows, :], oa_ref.at[rows, :],
                                 pr_ref.at[rows, :], sp_ref.at[rows, :])

        xv = x_ref[...]
        h = (xv * _rstd(xv) * g_ref[...]).astype(BF16)
        h_ref[...] = h
        q_ref[...] = _nt(h, wt_ref[ROW_Q:ROW_KV, :])
        kv_ref[...] = _nt(h, wt_ref[ROW_KV:ROW_GA, :])
        ga_ref[...] = _nt(h, wt_ref[ROW_GA:D_IN_PROJ, :])
        attention_blocks = [attention(b) for b in range(blocks)]
        for lo in range(0, D_PC, PC_PIECE):
            pc_ref[:, lo:lo + PC_PIECE] = _nt(h, wt_ref[lo:lo + PC_PIECE, :])
            for stages_of_block in attention_blocks:
                next(stages_of_block, None)
        for stages_of_block in attention_blocks:
            for _ in stages_of_block:
                pass
        kv_last[...] = kv_ref[tile - BLK:tile, :]

        cb, _, _, _, _, _, conv = _conv_core(pc_ref, zbuf, cw_ref)
        yc = cb * conv
        silu, _ = _silu_and_grad(pc_ref[:, 3 * D_CONV:4 * D_CONV])
        oc_ref[...] = (yc * _rstd(yc) * gn_ref[...] * silu).astype(BF16)
        zbuf[0:8, :] = zbuf[tile:tile + 8, :]

        @pl.when(step == stage_steps[-1])
        def _():
            stages[-1]()
            to_hbm.wait()

    def row(width):
        return pl.BlockSpec((tile, width), lambda i: (i, 0))

    any_spec = pl.BlockSpec(memory_space=pl.ANY)
    dma = pltpu.SemaphoreType.DMA
    wt_shape = (N_CHIPS * wt_sh.shape[0], wt_sh.shape[1])
    cw_shape = (cw_sh.shape[0], N_CHIPS * cw_cols)
    return pl.pallas_call(
        body, name="fwd_in", grid=(n_tiles,),
        out_shape=(jax.ShapeDtypeStruct((seq, D_MODEL), BF16), jax.ShapeDtypeStruct((seq, D_PC), F32),
                   jax.ShapeDtypeStruct((seq, D_ATTN), F32), jax.ShapeDtypeStruct((seq, 2 * D_KV), F32),
                   jax.ShapeDtypeStruct((seq, D_ATTN), F32), jax.ShapeDtypeStruct((seq, D_CONV), BF16),
                   pltpu.HBM((seq, D_ATTN), F32), pltpu.HBM((seq, D_ATTN), BF16),
                   pltpu.HBM((seq, N_HEADS * BLK), BF16), pltpu.HBM((seq, 128), F32),
                   pltpu.HBM(wt_shape, BF16), jax.ShapeDtypeStruct(cw_shape, F32),
                   jax.ShapeDtypeStruct((N_CHIPS * wo_sh.shape[0], wo_sh.shape[1]), BF16)),
        in_specs=[row(D_MODEL), _resident((1, D_MODEL)), any_spec, any_spec, any_spec, _resident((1, D_CONV)),
                  pl.BlockSpec(memory_space=pltpu.SMEM), _resident((1, D_ATTN))],
        out_specs=(row(D_MODEL), row(D_PC), row(D_ATTN), row(2 * D_KV), row(D_ATTN), row(D_CONV), row(D_ATTN),
                   row(D_ATTN), row(N_HEADS * BLK), row(128), any_spec, pl.BlockSpec(cw_shape, lambda i: (0, 0)),
                   any_spec),
        scratch_shapes=[pltpu.VMEM((tile + 8, D_CONV), F32), pltpu.VMEM((BLK, 2 * D_KV), F32),
                        pltpu.VMEM(wt_shape, BF16), pltpu.VMEM((AG_CAST_ROWS, wt_sh.shape[1]), F32),
                        pltpu.VMEM((N_CHIPS,) + cw_sh.shape, F32),
                        dma((6,)), dma((6,)), dma((1,)), dma((3,)), dma((3,)), dma((1,))] + _ag_scratch(wo_sh.shape),
        compiler_params=_params(62, ("arbitrary",)),
    )(x, norm_in, wt_sh, wo_sh, cw_sh, norm_conv_out, sinks, norm_attn_out)


def _conv_core(pc_ref, zbuf, cw_ref):
    tile = pc_ref.shape[0]
    cb = pc_ref[:, 0:D_CONV]
    cc = pc_ref[:, D_CONV:2 * D_CONV]
    cu = pc_ref[:, 2 * D_CONV:3 * D_CONV]
    z = cc * cu
    zbuf[8:tile + 8, :] = z
    z1 = zbuf[7:tile + 7, :]
    z2 = zbuf[6:tile + 6, :]
    conv = cw_ref[0:1, :] * z2 + cw_ref[1:2, :] * z1 + cw_ref[2:3, :] * z
    return cb, cc, cu, z, z1, z2, conv


def _band_geometry(block_index):
    qi = lax.broadcasted_iota(jnp.int32, (BLK, BLK), 0)
    kp = lax.broadcasted_iota(jnp.int32, (BLK, BLK), 1)
    use_cur = kp <= qi
    dist = jnp.where(use_cur, qi - kp, qi - kp + BLK).astype(F32)
    valid = use_cur | (block_index > 0)
    return use_cur, dist, valid


def _block_diag(cur, prev, group):
    lane = lax.broadcasted_iota(jnp.int32, cur.shape, 1)

    def halves(t):
        other = pltpu.roll(t, 64, 1)
        lo, hi = (t, other) if group == 0 else (other, t)
        return jnp.where(lane < 64, lo, 0.0), jnp.where(lane >= 64, hi, 0.0)

    return jnp.concatenate(halves(cur) + halves(prev), axis=0).astype(BF16)


def _merge(s4, use_cur):
    return (jnp.where(use_cur, s4[:, 0:BLK], s4[:, 2 * BLK:3 * BLK]),
            jnp.where(use_cur, s4[:, BLK:2 * BLK], s4[:, 3 * BLK:4 * BLK]))


def _split(a, b, use_cur):
    return jnp.concatenate([jnp.where(use_cur, a, 0.0), jnp.where(use_cur, b, 0.0),
                            jnp.where(use_cur, 0.0, a), jnp.where(use_cur, 0.0, b)], axis=1)


def _softmax_head(s, head, sink, dist, valid):
    sc = jnp.where(valid, s - SLOPES[head] * dist, -jnp.inf)
    m = jnp.maximum(jnp.max(sc, axis=-1, keepdims=True), sink)
    p = jnp.exp(sc - m)
    es = jnp.exp(sink - m)
    inv = 1.0 / (jnp.sum(p, axis=-1, keepdims=True) + es)
    return p * inv, es * inv


def _attn_operands(q_ref, kvc_ref, kvp_ref):
    groups = range(N_HEADS // HEADS_PER_KV)
    kbd = [_block_diag(kvc_ref[:, 0:D_KV], kvp_ref[:, 0:D_KV], g) for g in groups]
    vbd = [_block_diag(kvc_ref[:, D_KV:2 * D_KV], kvp_ref[:, D_KV:2 * D_KV], g) for g in groups]
    qps = [(q_ref[:, j * 128:(j + 1) * 128] * SCALE).astype(BF16) for j in range(N_HEADS // 2)]
    return qps, kbd, vbd


def _attn_forward(q_ref, kvc_ref, kvp_ref, ga_ref, sink_ref, gn_ref, geometry, ya_ref, oa_ref, pr_ref, sp_ref):
    use_cur, dist, valid = geometry
    _, kbd, vbd = operands = _attn_operands(q_ref, kvc_ref, kvp_ref)
    yield
    scores = []
    for j, qp in enumerate(operands[0]):
        scores += _merge(_nt(qp, kbd[j // 4]), use_cur)
    yield
    sinks = [sink_ref[0, h] for h in range(N_HEADS)]
    scores = [jnp.where(valid, s - SLOPES[h] * dist, -jnp.inf) for h, s in enumerate(scores)]
    maxes = [jnp.maximum(jnp.max(s, axis=-1, keepdims=True), sinks[h]) for h, s in enumerate(scores)]
    yield
    exps = [jnp.exp(s - m) for s, m in zip(scores, maxes)]
    sink_exps = [jnp.exp(sinks[h] - m) for h, m in enumerate(maxes)]
    yield
    invs = [1.0 / (jnp.sum(e, axis=-1, keepdims=True) + se) for e, se in zip(exps, sink_exps)]
    probs = [e * inv for e, inv in zip(exps, invs)]
    pr_ref[...] = jnp.concatenate(probs, axis=1).astype(BF16)
    lane = lax.broadcasted_iota(jnp.int32, (BLK, 128), 1)
    sp_ref[...] = sum(jnp.where(lane == h, se * inv, 0.0) for h, (se, inv) in enumerate(zip(sink_exps, invs)))
    yield
    p4s = [_split(probs[2 * j], probs[2 * j + 1], use_cur).astype(BF16) for j in range(N_HEADS // 2)]
    ya = jnp.concatenate([_nn(p4, vbd[j // 4]) for j, p4 in enumerate(p4s)], axis=1)
    ya_ref[...] = ya
    yield
    silu, _ = _silu_and_grad(ga_ref[...])
    oa_ref[...] = (ya * _rstd(ya) * gn_ref[...] * silu).astype(BF16)


def _kv_specs(n_blocks, reverse):
    def blk(i):
        return (n_blocks - 1 - i) if reverse else i
    cur = pl.BlockSpec((BLK, 2 * D_KV), lambda i: (blk(i), 0))
    prev = pl.BlockSpec((BLK, 2 * D_KV), lambda i: (jnp.maximum(blk(i) - 1, 0), 0))
    return cur, prev


def _out_proj_loss(x, oc, oa, wo, norm_final, target, pc, conv_w, norm_conv_out):
    seq = x.shape[0]
    tile = TOK_TILE
    n_tiles = seq // tile

    def body(x_ref, oc_ref, oa_ref, wo_ref, gf_ref, t_ref, pc_ref, hcc_ref, hcu_ref, cw_ref, gn_ref,
             dx2_ref, doa_ref, gwo_ref, dpc_ref, small_ref, loss_acc, zbuf, dbuf):
        step = pl.program_id(0)

        def small_add(row, value):
            small_ref[row:row + 1, :] += jnp.sum(value, axis=0, keepdims=True)

        @pl.when(step == 0)
        def _():
            gwo_ref[...] = jnp.zeros_like(gwo_ref)
            small_ref[...] = jnp.zeros_like(small_ref)
            loss_acc[...] = jnp.zeros_like(loss_acc)
            dbuf[tile:tile + 8, :] = jnp.zeros((8, D_CONV), F32)

        oc, oa = oc_ref[...], oa_ref[...]
        x2 = x_ref[...] + _nn(oc, wo_ref[0:D_CONV, :]) + _nn(oa, wo_ref[D_CONV:D_MIX, :])
        r = _rstd(x2)
        xhat = x2 * r
        err = xhat * gf_ref[...] - t_ref[...]
        loss_acc[...] += jnp.sum(err * err, axis=0, keepdims=True) * (0.5 / D_MODEL)
        dy = err * (1.0 / D_MODEL)
        small_add(SMALL_NORM_FINAL, dy * xhat)
        dx2 = _rms_bwd(dy * gf_ref[...], xhat, r)
        dx2_ref[...] = dx2
        db = dx2.astype(BF16)
        do = _nt(db, wo_ref[0:D_CONV, :])
        doa_ref[...] = _nt(db, wo_ref[D_CONV:D_MIX, :])
        gwo_ref[0:D_CONV, :] += _tn(oc, db)
        gwo_ref[D_CONV:D_MIX, :] += _tn(oa, db)

        is_first_tile = step == n_tiles - 1
        zbuf[0:8, :] = jnp.where(is_first_tile, 0.0, hcc_ref[...] * hcu_ref[...])
        cb, cc, cu, z, z1, z2, conv = _conv_core(pc_ref, zbuf, cw_ref)
        silu, dsilu = _silu_and_grad(pc_ref[:, 3 * D_CONV:4 * D_CONV])
        yc = cb * conv
        rc = _rstd(yc)
        chat = yc * rc
        dn = do * silu
        dpc_ref[:, 3 * D_CONV:4 * D_CONV] = (do * (chat * gn_ref[...]) * dsilu).astype(BF16)
        small_add(SMALL_NORM_CONV, dn * chat)
        dyc = _rms_bwd(dn * gn_ref[...], chat, rc)
        dpc_ref[:, 0:D_CONV] = (dyc * conv).astype(BF16)
        dconv = dyc * cb
        small_add(SMALL_CONV_W, dconv * z2)
        small_add(SMALL_CONV_W + 1, dconv * z1)
        small_add(SMALL_CONV_W + 2, dconv * z)
        dbuf[0:tile, :] = dconv
        dz = cw_ref[2:3, :] * dconv + cw_ref[1:2, :] * dbuf[1:tile + 1, :] + cw_ref[0:1, :] * dbuf[2:tile + 2, :]
        dpc_ref[:, D_CONV:2 * D_CONV] = (dz * cu).astype(BF16)
        dpc_ref[:, 2 * D_CONV:3 * D_CONV] = (dz * cc).astype(BF16)
        dbuf[tile:tile + 8, :] = dbuf[0:8, :]

        @pl.when(step == n_tiles - 1)
        def _():
            lane = lax.broadcasted_iota(jnp.int32, (1, D_MODEL), 1)
            small_ref[SMALL_MISC:SMALL_MISC + 1, :] = jnp.where(lane == SMALL_LOSS_LANE, jnp.sum(loss_acc[...]), 0.0)

    def rev(i):
        return n_tiles - 1 - i

    def row(width):
        return pl.BlockSpec((tile, width), lambda i: (rev(i), 0))

    def halo(col_block):
        return pl.BlockSpec((8, D_CONV), lambda i: (jnp.maximum(rev(i) * (tile // 8) - 1, 0), col_block))

    def const(shape):
        return pl.BlockSpec(shape, lambda i: (0, 0))

    return pl.pallas_call(
        body, name="out_proj_loss", grid=(n_tiles,),
        out_shape=(jax.ShapeDtypeStruct((seq, D_MODEL), F32), jax.ShapeDtypeStruct((seq, D_ATTN), F32),
                   jax.ShapeDtypeStruct((D_MIX, D_MODEL), F32), jax.ShapeDtypeStruct((seq, D_PC), BF16),
                   jax.ShapeDtypeStruct((SMALL_ROWS, D_MODEL), F32)),
        in_specs=[row(D_MODEL), row(D_CONV), row(D_ATTN), _resident(wo.shape), _resident((1, D_MODEL)), row(D_MODEL),
                  row(D_PC), halo(1), halo(2), _resident(conv_w.shape), _resident((1, D_CONV))],
        out_specs=(row(D_MODEL), row(D_ATTN), const((D_MIX, D_MODEL)), row(D_PC), const((SMALL_ROWS, D_MODEL))),
        scratch_shapes=[pltpu.VMEM((1, D_MODEL), F32), pltpu.VMEM((tile + 8, D_CONV), F32),
                        pltpu.VMEM((tile + 8, D_CONV), F32)],
        compiler_params=_params(62, ("arbitrary",)),
    )(x, oc, oa, wo, norm_final, target, pc, pc, pc, conv_w, norm_conv_out)


def _attn_bwd(q, kv, ga, ya, doa, probs, sink_probs, norm_attn_out, gwo, small):
    seq = q.shape[0]
    n_blocks = seq // BLK
    stage_steps = (0, n_blocks // 4, n_blocks // 2, (3 * n_blocks) // 4, n_blocks - 1)

    def body(q_ref, kvc_ref, kvp_ref, ga_ref, ya_ref, doa_ref, pr_ref, sp_ref, gn_ref, gwo_ref, small_ref,
             dqg_ref, small_out, gwo_sh, gna_ref, gs_ref, carry, dya_buf, *rs_scratch):
        step = pl.program_id(0)

        @pl.when(step == 0)
        def _():
            gna_ref[...] = jnp.zeros_like(gna_ref)
            gs_ref[...] = jnp.zeros_like(gs_ref)
            carry[...] = jnp.zeros_like(carry)

        ya = ya_ref[...]
        r = _rstd(ya)
        xhat = ya * r
        silu, dsilu = _silu_and_grad(ga_ref[...])
        do = doa_ref[...]
        dn = do * silu
        dqg_ref[:, D_ATTN:2 * D_ATTN] = (do * (xhat * gn_ref[...]) * dsilu).astype(BF16)
        gna_ref[...] += jnp.sum(dn * xhat, axis=0, keepdims=True)
        dya_buf[...] = _rms_bwd(dn * gn_ref[...], xhat, r).astype(BF16)

        use_cur = _band_geometry(n_blocks - 1 - step)[0]
        lane = lax.broadcasted_iota(jnp.int32, (BLK, 128), 1)

        def fold(bd):
            return (jnp.where(lane < 64, bd[0:BLK], 0.0) + jnp.where(lane >= 64, bd[BLK:2 * BLK], 0.0),
                    jnp.where(lane < 64, bd[2 * BLK:3 * BLK], 0.0) + jnp.where(lane >= 64, bd[3 * BLK:4 * BLK], 0.0))

        pairs = range(N_HEADS // 2)
        qps, kbd, vbd = _attn_operands(q_ref, kvc_ref, kvp_ref)
        probs = [pr_ref[:, h * BLK:(h + 1) * BLK].astype(F32) for h in range(N_HEADS)]
        dyps = [dya_buf[:, j * 128:(j + 1) * 128] for j in pairs]
        dps = []
        for j in pairs:
            dps += _merge(_nt(dyps[j], vbd[j // 4]), use_cur)
        deltas = [jnp.sum(p * dp, axis=-1, keepdims=True) for p, dp in zip(probs, dps)]
        dss = [p * (dp - delta) for p, dp, delta in zip(probs, dps, deltas)]
        delta_lanes = sum(jnp.where(lane == h, deltas[h], 0.0) for h in range(N_HEADS))
        gs_ref[...] -= jnp.sum(sp_ref[...] * delta_lanes, axis=0, keepdims=True)
        ds4s = [_split(dss[2 * j], dss[2 * j + 1], use_cur).astype(BF16) for j in pairs]
        p4s = [_split(probs[2 * j], probs[2 * j + 1], use_cur).astype(BF16) for j in pairs]
        dqg_ref[:, 0:D_ATTN] = jnp.concatenate([_nn(ds4s[j], kbd[j // 4]) * SCALE for j in pairs], axis=1).astype(BF16)
        sums = []
        for group in range(N_HEADS // HEADS_PER_KV):
            acc = [jnp.zeros((BLK, 128), F32) for _ in range(4)]
            for j in range(group * 4, group * 4 + 4):
                for slot, part in enumerate(fold(_tn(ds4s[j], qps[j])) + fold(_tn(p4s[j], dyps[j]))):
                    acc[slot] = acc[slot] + part
            sums.append([a + pltpu.roll(a, 64, 1) for a in acc])
        dk_cur, dk_prev, dv_cur, dv_prev = (jnp.where(lane < 64, a, b) for a, b in zip(sums[0], sums[1]))
        dqg_ref[:, 2 * D_ATTN:2 * D_ATTN + 2 * D_KV] = (jnp.concatenate([dk_cur, dv_cur], axis=1) + carry[...]).astype(BF16)
        carry[...] = jnp.concatenate([dk_prev, dv_prev], axis=1)

        @pl.when(step == n_blocks - 1)
        def _():
            small_out[...] = small_ref[...]
            small_out[SMALL_NORM_ATTN:SMALL_NORM_ATTN + 1, :] = gna_ref[...]
            small_out[SMALL_MISC:SMALL_MISC + 1, 0:128] = small_ref[SMALL_MISC:SMALL_MISC + 1, 0:128] + gs_ref[...]

        for at, stage in zip(stage_steps, _rs_wout_stages(gwo_ref, gwo_sh, *rs_scratch)):
            pl.when(step == at)(stage)

    row = pl.BlockSpec((BLK, D_ATTN), lambda i: (n_blocks - 1 - i, 0))
    kv_cur, kv_prev = _kv_specs(n_blocks, reverse=True)
    any_spec = pl.BlockSpec(memory_space=pl.ANY)
    return pl.pallas_call(
        body, name="attn_bwd", grid=(n_blocks,),
        out_shape=(jax.ShapeDtypeStruct((seq, D_QG), BF16), jax.ShapeDtypeStruct(small.shape, F32),
                   jax.ShapeDtypeStruct((gwo.shape[0] // N_CHIPS, gwo.shape[1]), F32)),
        in_specs=[row, kv_cur, kv_prev, row, row, row,
                  pl.BlockSpec((BLK, N_HEADS * BLK), lambda i: (n_blocks - 1 - i, 0)),
                  pl.BlockSpec((BLK, 128), lambda i: (n_blocks - 1 - i, 0)), _resident((1, D_ATTN)),
                  any_spec, _resident(small.shape)],
        out_specs=(pl.BlockSpec((BLK, D_QG), lambda i: (n_blocks - 1 - i, 0)),
                   pl.BlockSpec(small.shape, lambda i: (0, 0)), any_spec),
        scratch_shapes=[pltpu.VMEM((1, D_ATTN), F32), pltpu.VMEM((1, 128), F32),
                        pltpu.VMEM((BLK, 2 * D_KV), F32), pltpu.VMEM((BLK, D_ATTN), BF16)] + _rs_wout_scratch(gwo.shape),
        compiler_params=_params(44, ("arbitrary",)),
    )(q, kv, kv, ga, ya, doa, probs, sink_probs, norm_attn_out, gwo, small)


GBLK = 256
GSUB = 64
PAIR_RING = 4
LAG_PAIR, LAG_HOP1, LAG_HOP2 = 1, 7, 14


def _bwd_in(dpc, dqg, h, wt, x, norm_in, dx2, small):
    seq = x.shape[0]
    n_blk = D_IN_PROJ // GBLK
    per_chip = n_blk // N_CHIPS
    n_slots = (n_blk + 1) // 2
    n_sub = GBLK // GSUB
    chip_rows = D_IN_PROJ // N_CHIPS
    tile = TOK_TILE
    n_tiles = seq // tile
    n_steps = n_blk + max(n_tiles, LAG_HOP2)
    chunk = min(seq, 512)
    blk_q, blk_kv, blk_ga = ROW_Q // GBLK, ROW_KV // GBLK, ROW_GA // GBLK

    def block_of(i):
        k = i % N_CHIPS
        robin = per_chip * ((k % 2) * 2 + k // 2) + i // N_CHIPS
        if isinstance(i, int):
            return robin if i < per_chip * N_CHIPS else i
        return jnp.where(i < per_chip * N_CHIPS, robin, i)

    def owner_of(i):
        return (i // N_CHIPS) % 2

    def slot_of(i):
        return (i // (2 * N_CHIPS)) * N_CHIPS + i % N_CHIPS

    def body(dpc_ref, dqg_ref, wt_ref, h_ref, x_ref, g_ref, dx2_ref, small_ref, gx_ref, small_sum, gwt_sh,
             dh_acc, gni, keep, pbuf, xbuf, land, land2, small_land,
             pair_send, pair_recv, h1_send, h1_recv, h2_send, h2_recv, sw_send, sw_recv, sm_send, sm_recv, out_sem):
        step = pl.program_id(0)
        x_i, y_i, c = lax.axis_index("x"), lax.axis_index("y"), lax.axis_index("c")
        me = 4 * x_i + 2 * y_i + c
        j = 2 * x_i + y_i
        pa = (_xor(x_i, 1 - c), _xor(y_i, c), c)
        pb = (_xor(x_i, c), _xor(y_i, 1 - c), c)
        sib = (x_i, y_i, 1 - c)
        ja = 2 * pa[0] + pa[1]
        jb = 2 * pb[0] + pb[1]
        jd = 3 - j

        def remote(src, dst, send, recv, to):
            return pltpu.make_async_remote_copy(src_ref=src, dst_ref=dst, send_sem=send, recv_sem=recv,
                                                device_id=to, device_id_type=MESH)

        def piece(ref, slot, u, n):
            return ref.at[slot, pl.ds(u * GSUB, n * GSUB), :]

        def chip_rows_at(ref, local, n):
            return ref.at[pl.ds(pl.multiple_of(local, GSUB), n * GSUB), :]

        def pair_copy(i):
            slot = slot_of(i)
            return remote(pbuf.at[i % PAIR_RING], land.at[slot], pair_send.at[slot], pair_recv.at[slot], sib)

        def h1_copy(slot, u, n):
            k = slot * n_sub + u
            return remote(piece(xbuf, slot, u, n), piece(xbuf, slot, u, n), h1_send.at[k], h1_recv.at[k], pa)

        def h2_copy(slot, u, n, local):
            k = slot * n_sub + u
            return remote(piece(xbuf, slot, u, n), chip_rows_at(land2, local, n), h2_send.at[k], h2_recv.at[k], pb)

        def sw_copy(slot, u, n, local):
            k = slot * n_sub + u
            return remote(piece(keep, slot, u, n), chip_rows_at(gwt_sh, local, n), sw_send.at[k], sw_recv.at[k], sib)

        def owned(i):
            return (i >= 0) & (i < n_blk) & (owner_of(i) == c)

        def chip_of(blk, u):
            row = blk * GBLK + u * GSUB
            chip = row // chip_rows
            return chip, row - chip * chip_rows

        def pieces(blk):
            first, local = chip_of(blk, 0)
            whole = first == chip_of(blk, n_sub - 1)[0]
            if isinstance(blk, int):
                return [(True, 0, n_sub, first, local)] if whole else [(True, u, 1) + chip_of(blk, u) for u in range(n_sub)]
            return [(whole, 0, n_sub, first, local)] + [(jnp.logical_not(whole), u, 1) + chip_of(blk, u) for u in range(n_sub)]

        @pl.when(step == 0)
        def _():
            dh_acc[...] = jnp.zeros_like(dh_acc)
            gni[...] = jnp.zeros_like(gni)

        @pl.when(step < n_blk)
        def _():
            from_pc = block_of(step) < blk_q
            block = _tn(jnp.where(from_pc, dpc_ref[...], dqg_ref[...]), h_ref[...])
            for t in range(0, seq, chunk):
                d = jnp.where(from_pc, dpc_ref[t:t + chunk, :], dqg_ref[t:t + chunk, :])
                dh_acc[t:t + chunk, :] += _nn(d, wt_ref[...])

            @pl.when(owner_of(step) == c)
            def _():
                keep[slot_of(step)] = block

            @pl.when(owner_of(step) != c)
            def _():
                @pl.when(step >= 2 * PAIR_RING)
                def _():
                    pair_copy(step - 2 * PAIR_RING).wait_send()
                pbuf[step % PAIR_RING] = block.astype(BF16)
                pair_copy(step).start()

        i1 = step - LAG_PAIR

        @pl.when(owned(i1))
        def _():
            slot = slot_of(i1)
            pair_copy(i1).wait_recv()
            _accumulate(keep.at[slot], land.at[slot])
            for cond, u, n, chip, _ in pieces(block_of(i1)):
                @pl.when(cond & ((chip == ja) | (chip == jd)))
                def _(u=u, n=n):
                    _cast_rows(piece(keep, slot, u, n), piece(xbuf, slot, u, n))
                    h1_copy(slot, u, n).start()

        i2 = step - LAG_HOP1

        @pl.when(owned(i2))
        def _():
            slot = slot_of(i2)
            for cond, u, n, chip, local in pieces(block_of(i2)):
                @pl.when(cond & ((chip == j) | (chip == jb)))
                def _(u=u, n=n, chip=chip, local=local):
                    h1_copy(slot, u, n).wait_recv()
                    _accumulate(piece(keep, slot, u, n), piece(xbuf, slot, u, n))

                    @pl.when(chip == jb)
                    def _():
                        _cast_rows(piece(keep, slot, u, n), piece(xbuf, slot, u, n))
                        h2_copy(slot, u, n, local).start()

                @pl.when(cond & ((chip == ja) | (chip == jd)))
                def _(u=u, n=n):
                    h1_copy(slot, u, n).wait_send()

        i3 = step - LAG_HOP2

        @pl.when(owned(i3))
        def _():
            slot = slot_of(i3)
            for cond, u, n, chip, local in pieces(block_of(i3)):
                @pl.when(cond & (chip == j))
                def _(u=u, n=n, local=local):
                    h2_copy(slot, u, n, local).wait_recv()
                    _accumulate(piece(keep, slot, u, n), chip_rows_at(land2, local, n))
                    mine = pltpu.make_async_copy(piece(keep, slot, u, n), chip_rows_at(gwt_sh, local, n), out_sem.at[0])
                    mine.start()
                    sw_copy(slot, u, n, local).start()
                    mine.wait()

                @pl.when(cond & (chip == jb))
                def _(u=u, n=n, local=local):
                    h2_copy(slot, u, n, local).wait_send()

        e = step - n_blk

        @pl.when((e >= 0) & (e < n_tiles))
        def _():
            dh = dh_acc[pl.ds(pl.multiple_of(e * tile, tile), tile), :]
            xv = x_ref[...]
            r = _rstd(xv)
            xhat = xv * r
            gni[...] += jnp.sum(dh * xhat, axis=0, keepdims=True)
            gx_ref[...] = _rms_bwd(dh * g_ref[...], xhat, r) + dx2_ref[...]

        @pl.when(step == n_steps - 1)
        def _():
            small_land[me] = small_ref[...]
            small_land[me, SMALL_NORM_IN:SMALL_NORM_IN + 1, :] = gni[...]
            others = [(dx, dy, dc) for dx in (0, 1) for dy in (0, 1) for dc in (0, 1)][1:]
            sends = [remote(small_land.at[me], small_land.at[me], sm_send.at[k], sm_recv.at[k],
                            (_xor(x_i, dx), _xor(y_i, dy), _xor(c, dc))) for k, (dx, dy, dc) in enumerate(others)]
            for cp in sends:
                cp.start()
            for i in range(n_blk):
                if i + 2 * PAIR_RING >= n_blk:
                    @pl.when(owner_of(i) != c)
                    def _(i=i):
                        pair_copy(i).wait_send()
                for _, u, n, chip, local in pieces(block_of(i)):
                    @pl.when((j == chip) & (c == owner_of(i)))
                    def _(i=i, u=u, n=n, local=local):
                        sw_copy(slot_of(i), u, n, local).wait_send()

                    @pl.when((j == chip) & (c != owner_of(i)))
                    def _(i=i, u=u, n=n, local=local):
                        sw_copy(slot_of(i), u, n, local).wait_recv()
            for cp in sends:
                cp.wait_recv()
            total = small_land[0]
            for dev in range(1, 8):
                total = total + small_land[dev]
            small_sum[...] = total
            for cp in sends:
                cp.wait_send()

    def blk_at(i):
        return block_of(jnp.clip(i, 0, n_blk - 1))

    last_pc_step = max(i for i in range(n_blk) if block_of(i) < blk_q)

    def next_block(i, in_pc):
        i = jnp.clip(i, 0, n_blk - 1)
        step = jnp.full_like(i, last_pc_step if in_pc else n_blk - 1)
        for ahead in reversed(range(N_CHIPS)):
            cand = jnp.minimum(i + ahead, n_blk - 1)
            step = jnp.where((block_of(cand) < blk_q) == in_pc, cand, step)
        return block_of(step)

    def dqg_block(i):
        b = next_block(i, False)
        q_blk = jnp.clip(b - blk_q, 0, blk_kv - blk_q - 1)
        ga_blk = (D_ATTN // GBLK) + jnp.clip(b - blk_ga, 0, n_blk - blk_ga - 1)
        return jnp.where(b < blk_kv, q_blk, jnp.where(b == blk_kv, 2 * D_ATTN // GBLK, ga_blk))

    def tok(i):
        return (jnp.clip(i - n_blk, 0, n_tiles - 1), 0)

    n_piece = n_slots * n_sub
    dma = pltpu.SemaphoreType.DMA
    return pl.pallas_call(
        body, name="bwd_in", grid=(n_steps,),
        out_shape=(jax.ShapeDtypeStruct((seq, D_MODEL), F32), jax.ShapeDtypeStruct(small.shape, F32),
                   jax.ShapeDtypeStruct((chip_rows, D_MODEL), F32)),
        in_specs=[pl.BlockSpec((seq, GBLK), lambda i: (0, next_block(i, True))),
                  pl.BlockSpec((seq, GBLK), lambda i: (0, dqg_block(i))),
                  pl.BlockSpec((GBLK, D_MODEL), lambda i: (blk_at(i), 0)),
                  _resident(h.shape),
                  pl.BlockSpec((tile, D_MODEL), tok), _resident((1, D_MODEL)), pl.BlockSpec((tile, D_MODEL), tok),
                  _resident(small.shape)],
        out_specs=(pl.BlockSpec((tile, D_MODEL), tok), pl.BlockSpec(small.shape, lambda i: (0, 0)),
                   pl.BlockSpec(memory_space=pl.ANY)),
        scratch_shapes=[pltpu.VMEM((seq, D_MODEL), F32), pltpu.VMEM((1, D_MODEL), F32),
                        pltpu.VMEM((n_slots, GBLK, D_MODEL), F32), pltpu.VMEM((PAIR_RING, GBLK, D_MODEL), BF16),
                        pltpu.VMEM((n_slots, GBLK, D_MODEL), BF16), pltpu.VMEM((n_slots, GBLK, D_MODEL), BF16),
                        pltpu.VMEM((chip_rows, D_MODEL), BF16), pltpu.VMEM((8,) + small.shape, F32),
                        dma((n_slots,)), dma((n_slots,)), dma((n_piece,)), dma((n_piece,)), dma((n_piece,)),
                        dma((n_piece,)), dma((n_piece,)), dma((n_piece,)), dma((7,)), dma((7,)), dma((1,))],
        compiler_params=_params(62, ("arbitrary",)),
    )(dpc, dqg, wt, h, x, norm_in, dx2, small)


def _accumulate(dst_ref, src_ref, rows=16):
    def step(i, carry):
        sl = pl.ds(pl.multiple_of(i * rows, rows), rows)
        dst_ref[sl, :] = dst_ref[sl, :] + src_ref[sl, :].astype(F32)
        return carry
    lax.fori_loop(0, dst_ref.shape[0] // rows, step, 0)


def _rs_wout_scratch(gwo_shape):
    o_half, width = gwo_shape[0] // N_CHIPS // 2, gwo_shape[1]
    return [pltpu.VMEM((4, o_half, width), F32), pltpu.VMEM((4, o_half, width), BF16),
            pltpu.VMEM((4, o_half, width), BF16), pltpu.VMEM((2, o_half, width), BF16),
            pltpu.VMEM((o_half, width), BF16),
            pltpu.SemaphoreType.DMA((8,)), pltpu.SemaphoreType.DMA((8,)), pltpu.SemaphoreType.DMA((4,))]


def _rs_wout_stages(gwo_ref, gwo_sh, acc_o, sb_o, r1o, r2o, r3o, send_sems, recv_sems, local_sems):
    o_rows = gwo_ref.shape[0] // N_CHIPS
    o_half = o_rows // 2
    x, y, c = lax.axis_index("x"), lax.axis_index("y"), lax.axis_index("c")
    j = 2 * x + y
    pa = (_xor(x, 1 - c), _xor(y, c), c)
    pb = (_xor(x, c), _xor(y, 1 - c), c)
    sib = (x, y, 1 - c)
    ja = 2 * pa[0] + pa[1]
    jb = 2 * pb[0] + pb[1]
    jd = 3 - j
    order = (ja, jd, jb, j)
    sib_order = (jb, jd, ja, j)

    def rcopy(k, src, dst, to):
        return pltpu.make_async_remote_copy(src_ref=src, dst_ref=dst, send_sem=send_sems.at[k],
                                            recv_sem=recv_sems.at[k], device_id=to, device_id_type=MESH)

    def o_rows_of(chip, half):
        return gwo_ref.at[pl.ds(pl.multiple_of(chip * o_rows + half * o_half, 8), o_half), :]

    def load_all(chips, half):
        cps = [pltpu.make_async_copy(o_rows_of(chip, half), acc_o.at[s], local_sems.at[s]) for s, chip in enumerate(chips)]
        for cp in cps:
            cp.start()
        return cps

    def pair_send(s):
        return rcopy(s, sb_o.at[s], r1o.at[s], sib)

    def out_half(half):
        return gwo_sh.at[pl.ds(pl.multiple_of(half * o_half, 8), o_half), :]

    hop1 = [rcopy(4 + s, sb_o.at[s], r2o.at[s], pa) for s in range(2)]
    hop2 = rcopy(6, sb_o.at[2], r3o, pb)
    swap = rcopy(7, acc_o.at[3], out_half(c), sib)
    mine = pltpu.make_async_copy(acc_o.at[3], out_half(c), local_sems.at[0])

    def resend(s, copy):
        pair_send(s).wait_send()
        _cast_rows(acc_o.at[s], sb_o.at[s])
        copy.start()

    def stage_pair():
        for s, cp in enumerate(load_all(sib_order, 1 - c)):
            cp.wait()
            _cast_rows(acc_o.at[s], sb_o.at[s])
            pair_send(s).start()

    def stage_hop1():
        for s, cp in enumerate(load_all(order, c)):
            cp.wait()
            pair_send(s).wait_recv()
            _accumulate(acc_o.at[s], r1o.at[s])
            if s < 2:
                resend(s, hop1[s])

    def stage_hop2():
        hop1[1].wait_recv()
        _accumulate(acc_o.at[2], r2o.at[1])
        resend(2, hop2)
        hop1[0].wait_recv()
        _accumulate(acc_o.at[3], r2o.at[0])

    def stage_final():
        hop2.wait_recv()
        _accumulate(acc_o.at[3], r3o)
        mine.start()
        swap.start()

    def stage_drain():
        rcopy(7, acc_o.at[3], out_half(1 - c), sib).wait_recv()
        for cp in [pair_send(3)] + hop1 + [hop2, swap]:
            cp.wait_send()
        mine.wait()

    return stage_pair, stage_hop1, stage_hop2, stage_final, stage_drain


def _adamw_big(groups, passed):
    arrays = [a for group in groups for a in group[:4]] + [passed[0]]
    steps = [group[0].shape[0] // group[4] for group in groups] + [passed[0].shape[0] // passed[1]]
    first = [sum(steps[:k]) for k in range(len(steps) + 1)]

    def body(*refs):
        ins, outs = refs[:len(arrays)], refs[len(arrays):]
        i = pl.program_id(0)
        for k in range(len(groups)):
            @pl.when((i >= first[k]) & (i < first[k + 1]))
            def _(k=k):
                _adamw_update(*ins[4 * k:4 * k + 4], *outs[4 * k:4 * k + 4])

        @pl.when(i >= first[-2])
        def _():
            outs[-1][...] = ins[-1][...]

    def spec(k, rows, width):
        return pl.BlockSpec((rows, width), lambda i: (jnp.clip(i - first[k], 0, steps[k] - 1), 0))

    in_specs, out_specs, out_shape = [], [], []
    for k, (w, _, _, _, rows) in enumerate(groups):
        in_specs += [spec(k, rows, w.shape[1])] * 4
        out_specs += [spec(k, rows, w.shape[1])] * 4
        out_shape += [jax.ShapeDtypeStruct(w.shape, F32)] * 4
    in_specs.append(spec(len(groups), passed[1], passed[0].shape[1]))
    out_specs.append(spec(len(groups), passed[1], passed[0].shape[1]))
    out_shape.append(jax.ShapeDtypeStruct(passed[0].shape, passed[0].dtype))
    out = pl.pallas_call(
        body, name="adamw_big", grid=(first[-1],), out_shape=tuple(out_shape), in_specs=in_specs,
        out_specs=tuple(out_specs), compiler_params=_params(40, ("arbitrary",)),
    )(*arrays)
    return [tuple(out[4 * k:4 * k + 4]) for k in range(len(groups))], out[-1]


def _adamw_update(w_ref, g_ref, m_ref, v_ref, go_ref, d_ref, nm_ref, nv_ref):
    gv = g_ref[...]
    go_ref[...] = gv
    nm = ADAM_B1 * m_ref[...] + (1.0 - ADAM_B1) * gv
    nv = ADAM_B2 * v_ref[...] + (1.0 - ADAM_B2) * (gv * gv)
    m_hat = nm / (1.0 - ADAM_B1 ** ADAM_STEP)
    v_hat = nv / (1.0 - ADAM_B2 ** ADAM_STEP)
    d_ref[...] = -ADAM_LR * (m_hat / (jnp.sqrt(v_hat) + ADAM_EPS) + ADAM_WD * w_ref[...])
    nm_ref[...] = nm
    nv_ref[...] = nv


def _adamw_small(weights, grads, ms, vs):
    n = len(weights)

    def body(*refs):
        ins, outs = refs[:4 * n], refs[4 * n:]
        for k in range(n):
            _adamw_update(*ins[4 * k:4 * k + 4], *outs[4 * k:4 * k + 4])

    flat = [a for group in zip(weights, grads, ms, vs) for a in group]
    vmem = pl.BlockSpec(memory_space=pltpu.VMEM)
    out = pl.pallas_call(
        body, name="adamw_small",
        out_shape=tuple(jax.ShapeDtypeStruct(w.shape, F32) for w in weights for _ in range(4)),
        in_specs=[vmem] * (4 * n), out_specs=(vmem,) * (4 * n),
    )(*flat)
    return [tuple(out[4 * k:4 * k + 4]) for k in range(n)]


def kernel(x, norm_in, w_in, conv_w, attn_sinks, norm_conv_out, norm_attn_out, w_out, norm_final, loss_target, m_norm_in, m_w_in, m_conv_w, m_attn_sinks, m_norm_conv_out, m_norm_attn_out, m_w_out, m_norm_final, v_norm_in, v_w_in, v_conv_w, v_attn_sinks, v_norm_conv_out, v_norm_attn_out, v_w_out, v_norm_final):
    chip = 2 * lax.axis_index("x") + lax.axis_index("y")
    xs, target = x[0], loss_target[0]
    norm_final2 = norm_final.reshape(1, D_MODEL)
    w_in_t, m_w_in_t, v_w_in_t = w_in[0].T, m_w_in[0].T, v_w_in[0].T

    h, pc, q, kv, ga, oc, ya, oa, probs, sink_probs, wt, cw, wo = _fwd_in(
        xs, norm_in, w_in_t, w_out[0], conv_w[0], norm_conv_out, attn_sinks, norm_attn_out)
    dx2, doa, gwo, dpc, small = _out_proj_loss(xs, oc, oa, wo, norm_final2, target, pc, cw, norm_conv_out)
    dqg, small, gwo_sh = _attn_bwd(q, kv, ga, ya, doa, probs, sink_probs, norm_attn_out, gwo, small)
    grad_x, small_sum, gwt_sh = _bwd_in(dpc, dqg, h, wt, xs, norm_in, dx2, small)

    loss = small_sum[SMALL_MISC, SMALL_LOSS_LANE]
    g_norm_in, g_norm_conv, g_norm_attn = (small_sum[r:r + 1] for r in (SMALL_NORM_IN, SMALL_NORM_CONV, SMALL_NORM_ATTN))
    g_norm_final = small_sum[SMALL_NORM_FINAL]
    g_conv_w = lax.dynamic_slice(small_sum[SMALL_CONV_W:SMALL_CONV_W + 3], (0, chip * (D_CONV // N_CHIPS)),
                                 (3, D_CONV // N_CHIPS))[None]
    g_sinks = small_sum[SMALL_MISC:SMALL_MISC + 1, 0:N_HEADS]

    def two_d(a):
        return a.reshape(-1, a.shape[-1])

    def from_hbm(*arrays):
        return tuple(pltpu.with_memory_space_constraint(a, pltpu.HBM) for a in arrays)

    (up_w_in, up_w_out), grad_x = _adamw_big(
        [from_hbm(w_in_t, gwt_sh, m_w_in_t, v_w_in_t) + (200,), from_hbm(w_out[0], gwo_sh, m_w_out[0], v_w_out[0]) + (128,)],
        (grad_x, TOK_TILE))
    up_w_in = tuple(o.T[None] for o in up_w_in)
    up_w_out = tuple(o[None] for o in up_w_out)
    small_w = (norm_in, conv_w, attn_sinks, norm_conv_out, norm_attn_out, norm_final)
    small_g = (g_norm_in, g_conv_w, g_sinks, g_norm_conv, g_norm_attn, g_norm_final)
    small_m = (m_norm_in, m_conv_w, m_attn_sinks, m_norm_conv_out, m_norm_attn_out, m_norm_final)
    small_v = (v_norm_in, v_conv_w, v_attn_sinks, v_norm_conv_out, v_norm_attn_out, v_norm_final)
    up_small = _adamw_small(*(tuple(two_d(a) for a in group) for group in (small_w, small_g, small_m, small_v)))
    up_small = [tuple(o.reshape(w.shape) for o in up) for up, w in zip(up_small, small_w)]
    up_norm_in, up_conv_w, up_sinks, up_norm_conv, up_norm_attn, up_norm_final = up_small
    updates = (up_norm_in, up_w_in, up_conv_w, up_sinks, up_norm_conv, up_norm_attn, up_w_out, up_norm_final)
    grads_out, deltas, new_m, new_v = zip(*updates)
    return (loss, grad_x[None], *grads_out, *deltas, *new_m, *new_v)
```

```python
import jax
import jax.numpy as jnp
from jax import lax
from jax.experimental import pallas as pl
from jax.experimental.pallas import tpu as pltpu
from jax.experimental.pallas import tpu_sc as plsc

F32 = jnp.float32
BF16 = jnp.bfloat16
MESH = pl.DeviceIdType.MESH

D_MODEL = 1024
D_CONV = 1024
D_ATTN = 1024
D_KV = 128
D_QG = 2 * D_ATTN + 2 * D_KV
D_MIX = D_CONV + D_ATTN
D_PC = 4 * D_CONV
D_IN_PROJ = D_PC + 2 * D_ATTN + 2 * D_KV
ROW_Q = D_PC
ROW_KV = ROW_Q + D_ATTN
ROW_GA = ROW_KV + 2 * D_KV
N_HEADS = 16
HEAD_DIM = 64
HEADS_PER_KV = 8
BLK = 128
N_CHIPS = 4
RMS_EPS = 1e-5
SCALE = HEAD_DIM ** -0.5
SLOPES = tuple(2.0 ** (-8.0 * (h + 1) / N_HEADS) for h in range(N_HEADS))

ADAM_LR, ADAM_B1, ADAM_B2, ADAM_EPS, ADAM_WD, ADAM_STEP = 0.001, 0.9, 0.999, 1e-08, 0.01, 10

SMALL_ROWS = 8
SMALL_NORM_IN, SMALL_NORM_CONV, SMALL_NORM_ATTN, SMALL_NORM_FINAL, SMALL_CONV_W, SMALL_MISC = 0, 1, 2, 3, 4, 7
SMALL_LOSS_LANE = N_HEADS

TOK_TILE = 256
PC_PIECE = 512
MIB = 1 << 20


def _params(vmem_mib, semantics=None):
    return pltpu.CompilerParams(dimension_semantics=semantics, vmem_limit_bytes=vmem_mib * MIB)


def _nn(a, b):
    return jnp.dot(a, b, preferred_element_type=F32)


def _nt(a, b):
    return lax.dot_general(a, b, (((1,), (1,)), ((), ())), preferred_element_type=F32)


def _tn(a, b):
    return lax.dot_general(a, b, (((0,), (0,)), ((), ())), preferred_element_type=F32)


def _rstd(v):
    return lax.rsqrt(jnp.mean(v * v, axis=-1, keepdims=True) + RMS_EPS)


def _rms_bwd(g, xhat, rstd):
    return rstd * (g - xhat * jnp.mean(g * xhat, axis=-1, keepdims=True))


def _silu_and_grad(g):
    s = jax.nn.sigmoid(g)
    return g * s, s * (1.0 + g * (1.0 - s))


def _resident(shape):
    return pl.BlockSpec(shape, lambda *_: (0,) * len(shape), pipeline_mode=pl.Buffered(1))


def _xor(a, b):
    return a + b - 2 * a * b


def _cast_rows(src_ref, dst_ref, rows=32):
    def step(i, carry):
        sl = pl.ds(pl.multiple_of(i * rows, rows), rows)
        dst_ref[sl, :] = src_ref[sl, :].astype(dst_ref.dtype)
        return carry
    lax.fori_loop(0, src_ref.shape[0] // rows, step, 0)


def _ag_scratch(shard_shape):
    rows, width = shard_shape
    return [pltpu.VMEM((rows, width), F32), pltpu.VMEM((rows, width), BF16), pltpu.VMEM((3, rows // 2, width), BF16),
            pltpu.SemaphoreType.DMA((6,)), pltpu.SemaphoreType.DMA((6,)), pltpu.SemaphoreType.DMA((4,))]


def _ag_stages(sh_ref, out, f32_buf, own, land, send_sems, recv_sems, local_sems):
    rows = sh_ref.shape[0]
    half = rows // 2
    x, y, c = lax.axis_index("x"), lax.axis_index("y"), lax.axis_index("c")
    j = 2 * x + y
    p1 = (_xor(x, c), _xor(y, 1 - c), c)
    p2 = (_xor(x, 1 - c), _xor(y, c), c)
    sib = (x, y, 1 - c)
    j1 = 2 * p1[0] + p1[1]
    j2 = 2 * p2[0] + p2[1]
    j3 = 3 - j

    def rows_of(chip, hf):
        return out.at[pl.ds(pl.multiple_of(chip * rows + hf * half, 16), half), :]

    def rcopy(k, src, dst, to):
        return pltpu.make_async_remote_copy(src_ref=src, dst_ref=dst, send_sem=send_sems.at[k],
                                            recv_sem=recv_sems.at[k], device_id=to, device_id_type=MESH)

    my_half = own.at[pl.ds(pl.multiple_of(c * half, 16), half), :]
    hop1 = rcopy(0, my_half, land.at[0], p1)
    hop2_own = rcopy(1, my_half, land.at[1], p2)
    hop2_fwd = rcopy(2, land.at[0], land.at[2], p2)
    swaps = [rcopy(3 + s, land.at[s], rows_of(chip, c), sib) for s, chip in enumerate((j1, j2, j3))]
    keeps = [pltpu.make_async_copy(land.at[s], rows_of(chip, c), local_sems.at[1 + s]) for s, chip in enumerate((j1, j2, j3))]
    load = pltpu.make_async_copy(sh_ref, f32_buf, local_sems.at[0])
    own_out = pltpu.make_async_copy(own, out.at[pl.ds(pl.multiple_of(j * rows, 16), rows), :], local_sems.at[0])

    def stage_send():
        load.start()
        load.wait()
        _cast_rows(f32_buf, own)
        own_out.start()
        hop1.start()

    def stage_forward():
        hop1.wait_recv()
        hop2_own.start()
        hop2_fwd.start()
        swaps[0].start()
        keeps[0].start()

    def stage_publish():
        hop2_own.wait_recv()
        swaps[1].start()
        keeps[1].start()
        hop2_fwd.wait_recv()
        swaps[2].start()
        keeps[2].start()

    def stage_drain():
        for s, chip in enumerate((j2, j1, j3)):
            rcopy(3 + s, my_half, rows_of(chip, 1 - c), sib).wait_recv()
        for cp in [hop1, hop2_own, hop2_fwd] + swaps:
            cp.wait_send()
        for cp in [own_out] + keeps:
            cp.wait()

    return stage_send, stage_forward, stage_publish, stage_drain


AG_CAST_ROWS = 400


def _gather_resident(sh_ref, out, f32_buf, send_sems, recv_sems, local_sems):
    rows = sh_ref.shape[0]
    half = rows // 2
    x, y, c = lax.axis_index("x"), lax.axis_index("y"), lax.axis_index("c")
    j = 2 * x + y
    p1 = (_xor(x, c), _xor(y, 1 - c), c)
    p2 = (_xor(x, 1 - c), _xor(y, c), c)
    sib = (x, y, 1 - c)
    j1 = 2 * p1[0] + p1[1]
    j2 = 2 * p2[0] + p2[1]
    j3 = 3 - j

    def rows_of(chip, hf):
        return out.at[pl.ds(pl.multiple_of(chip * rows + hf * half, 16), half), :]

    def send(k, chip, to):
        return pltpu.make_async_remote_copy(src_ref=rows_of(chip, c), dst_ref=rows_of(chip, c), send_sem=send_sems.at[k],
                                            recv_sem=recv_sems.at[k], device_id=to, device_id_type=MESH)

    per_half = half // AG_CAST_ROWS

    def cast_own(chunk):
        lo = pl.multiple_of(chunk * AG_CAST_ROWS, 16)
        load = pltpu.make_async_copy(sh_ref.at[pl.ds(lo, AG_CAST_ROWS), :], f32_buf, local_sems.at[0])
        load.start()
        load.wait()
        _cast_rows(f32_buf, out.at[pl.ds(pl.multiple_of(j * rows + lo, 16), AG_CAST_ROWS), :], rows=16)

    for k in range(per_half):
        cast_own(c * per_half + k)
    hop1 = send(0, j, p1)
    hop1.start()
    for k in range(per_half):
        cast_own((1 - c) * per_half + k)
    hop1.wait_recv()
    sends = [hop1, send(1, j, p2), send(2, j1, p2), send(3, j1, sib)]
    for cp in sends[1:]:
        cp.start()
    sends[1].wait_recv()
    sends.append(send(4, j2, sib))
    sends[-1].start()
    sends[2].wait_recv()
    sends.append(send(5, j3, sib))
    sends[-1].start()
    for k in (3, 4, 5):
        send(k, j, sib).wait_recv()
    for cp in sends:
        cp.wait_send()


def _fwd_in(x, norm_in, wt_sh, wo_sh, cw_sh, norm_conv_out, sinks, norm_attn_out):
    seq = x.shape[0]
    tile = TOK_TILE
    n_tiles = seq // tile
    stage_steps = (0, n_tiles // 4, (5 * n_tiles) // 8, n_tiles - 1)
    blocks = tile // BLK
    cw_cols = cw_sh.shape[1]

    def body(x_ref, g_ref, wtsh_ref, wo_ref, cwsh_ref, gn_ref, sink_ref, gna_ref,
             h_ref, pc_ref, q_ref, kv_ref, ga_ref, oc_ref, ya_ref, oa_ref, pr_ref, sp_ref, wt_out, cw_ref, wo_out,
             zbuf, kv_last, wt_ref, f32_buf, cw_land, wt_send, wt_recv, wt_local, cw_send, cw_recv, cw_local, *ag_scratch):
        step = pl.program_id(0)
        to_hbm = pltpu.make_async_copy(wt_ref, wt_out, wt_local.at[0])

        @pl.when(step == 0)
        def _():
            zbuf[0:8, :] = jnp.zeros((8, D_CONV), F32)
            kv_last[...] = jnp.zeros_like(kv_last)
            x_i, y_i, c = lax.axis_index("x"), lax.axis_index("y"), lax.axis_index("c")
            j = 2 * x_i + y_i
            p1 = (_xor(x_i, c), _xor(y_i, 1 - c), c)
            p2 = (_xor(x_i, 1 - c), _xor(y_i, c), c)
            j1 = 2 * p1[0] + p1[1]

            def cw_copy(k, src, chip, to):
                return pltpu.make_async_remote_copy(src_ref=src, dst_ref=cw_land.at[chip], send_sem=cw_send.at[k],
                                                    recv_sem=cw_recv.at[k], device_id=to, device_id_type=MESH)

            mine = pltpu.make_async_copy(cwsh_ref, cw_land.at[j], cw_local.at[0])
            mine.start()
            first = cw_copy(0, cwsh_ref, j, p1)
            first.start()
            _gather_resident(wtsh_ref, wt_ref, f32_buf, wt_send, wt_recv, wt_local)
            to_hbm.start()
            first.wait_recv()
            second = [cw_copy(1, cwsh_ref, j, p2), cw_copy(2, cw_land.at[j1], j1, p2)]
            for cp in second:
                cp.start()
            for cp in second:
                cp.wait_recv()
            for cp in [first] + second:
                cp.wait_send()
            mine.wait()
            for chip in range(N_CHIPS):
                cw_ref[:, chip * cw_cols:(chip + 1) * cw_cols] = cw_land[chip]

        stages = _ag_stages(wo_ref, wo_out, *ag_scratch)
        for at, stage in zip(stage_steps[:-1], stages[:-1]):
            pl.when(step == at)(stage)

        def attention(b):
            rows = pl.ds(b * BLK, BLK)
            kv_prev = kv_last if b == 0 else kv_ref.at[pl.ds((b - 1) * BLK, BLK), :]
            return _attn_forward(q_ref.at[rows, :], kv_ref.at[rows, :], kv_prev, ga_ref.at[rows, :], sink_ref, gna_ref,
                                 _band_geometry(step * blocks + b), ya_ref.at[rows, :], oa_ref.at[rows, :],
                                 pr_ref.at[rows, :], sp_ref.at[rows, :])

        xv = x_ref[...]
        h = (xv * _rstd(xv) * g_ref[...]).astype(BF16)
        h_ref[...] = h
        q_ref[...] = _nt(h, wt_ref[ROW_Q:ROW_KV, :])
        kv_ref[...] = _nt(h, wt_ref[ROW_KV:ROW_GA, :])
        ga_ref[...] = _nt(h, wt_ref[ROW_GA:D_IN_PROJ, :])
        attention_blocks = [attention(b) for b in range(blocks)]
        for lo in range(0, D_PC, PC_PIECE):
            pc_ref[:, lo:lo + PC_PIECE] = _nt(h, wt_ref[lo:lo + PC_PIECE, :])
            for stages_of_block in attention_blocks:
                next(stages_of_block, None)
        for stages_of_block in attention_blocks:
            for _ in stages_of_block:
                pass
        kv_last[...] = kv_ref[tile - BLK:tile, :]

        cb, _, _, _, _, _, conv = _conv_core(pc_ref, zbuf, cw_ref)
        yc = cb * conv
        silu, _ = _silu_and_grad(pc_ref[:, 3 * D_CONV:4 * D_CONV])
        oc_ref[...] = (yc * _rstd(yc) * gn_ref[...] * silu).astype(BF16)
        zbuf[0:8, :] = zbuf[tile:tile + 8, :]

        @pl.when(step == stage_steps[-1])
        def _():
            stages[-1]()
            to_hbm.wait()

    def row(width):
        return pl.BlockSpec((tile, width), lambda i: (i, 0))

    any_spec = pl.BlockSpec(memory_space=pl.ANY)
    dma = pltpu.SemaphoreType.DMA
    wt_shape = (N_CHIPS * wt_sh.shape[0], wt_sh.shape[1])
    cw_shape = (cw_sh.shape[0], N_CHIPS * cw_cols)
    return pl.pallas_call(
        body, name="fwd_in", grid=(n_tiles,),
        out_shape=(jax.ShapeDtypeStruct((seq, D_MODEL), BF16), jax.ShapeDtypeStruct((seq, D_PC), F32),
                   jax.ShapeDtypeStruct((seq, D_ATTN), F32), jax.ShapeDtypeStruct((seq, 2 * D_KV), F32),
                   jax.ShapeDtypeStruct((seq, D_ATTN), F32), jax.ShapeDtypeStruct((seq, D_CONV), BF16),
                   pltpu.HBM((seq, D_ATTN), F32), pltpu.HBM((seq, D_ATTN), BF16),
                   pltpu.HBM((seq, N_HEADS * BLK), BF16), pltpu.HBM((seq, 128), F32),
                   pltpu.HBM(wt_shape, BF16), jax.ShapeDtypeStruct(cw_shape, F32),
                   jax.ShapeDtypeStruct((N_CHIPS * wo_sh.shape[0], wo_sh.shape[1]), BF16)),
        in_specs=[row(D_MODEL), _resident((1, D_MODEL)), any_spec, any_spec, any_spec, _resident((1, D_CONV)),
                  pl.BlockSpec(memory_space=pltpu.SMEM), _resident((1, D_ATTN))],
        out_specs=(row(D_MODEL), row(D_PC), row(D_ATTN), row(2 * D_KV), row(D_ATTN), row(D_CONV), row(D_ATTN),
                   row(D_ATTN), row(N_HEADS * BLK), row(128), any_spec, pl.BlockSpec(cw_shape, lambda i: (0, 0)),
                   any_spec),
        scratch_shapes=[pltpu.VMEM((tile + 8, D_CONV), F32), pltpu.VMEM((BLK, 2 * D_KV), F32),
                        pltpu.VMEM(wt_shape, BF16), pltpu.VMEM((AG_CAST_ROWS, wt_sh.shape[1]), F32),
                        pltpu.VMEM((N_CHIPS,) + cw_sh.shape, F32),
                        dma((6,)), dma((6,)), dma((1,)), dma((3,)), dma((3,)), dma((1,))] + _ag_scratch(wo_sh.shape),
        compiler_params=_params(62, ("arbitrary",)),
    )(x, norm_in, wt_sh, wo_sh, cw_sh, norm_conv_out, sinks, norm_attn_out)


def _conv_core(pc_ref, zbuf, cw_ref):
    tile = pc_ref.shape[0]
    cb = pc_ref[:, 0:D_CONV]
    cc = pc_ref[:, D_CONV:2 * D_CONV]
    cu = pc_ref[:, 2 * D_CONV:3 * D_CONV]
    z = cc * cu
    zbuf[8:tile + 8, :] = z
    z1 = zbuf[7:tile + 7, :]
    z2 = zbuf[6:tile + 6, :]
    conv = cw_ref[0:1, :] * z2 + cw_ref[1:2, :] * z1 + cw_ref[2:3, :] * z
    return cb, cc, cu, z, z1, z2, conv


def _band_geometry(block_index):
    qi = lax.broadcasted_iota(jnp.int32, (BLK, BLK), 0)
    kp = lax.broadcasted_iota(jnp.int32, (BLK, BLK), 1)
    use_cur = kp <= qi
    dist = jnp.where(use_cur, qi - kp, qi - kp + BLK).astype(F32)
    valid = use_cur | (block_index > 0)
    return use_cur, dist, valid


def _block_diag(cur, prev, group):
    lane = lax.broadcasted_iota(jnp.int32, cur.shape, 1)

    def halves(t):
        other = pltpu.roll(t, 64, 1)
        lo, hi = (t, other) if group == 0 else (other, t)
        return jnp.where(lane < 64, lo, 0.0), jnp.where(lane >= 64, hi, 0.0)

    return jnp.concatenate(halves(cur) + halves(prev), axis=0).astype(BF16)


def _merge(s4, use_cur):
    return (jnp.where(use_cur, s4[:, 0:BLK], s4[:, 2 * BLK:3 * BLK]),
            jnp.where(use_cur, s4[:, BLK:2 * BLK], s4[:, 3 * BLK:4 * BLK]))


def _split(a, b, use_cur):
    return jnp.concatenate([jnp.where(use_cur, a, 0.0), jnp.where(use_cur, b, 0.0),
                            jnp.where(use_cur, 0.0, a), jnp.where(use_cur, 0.0, b)], axis=1)


def _softmax_head(s, head, sink, dist, valid):
    sc = jnp.where(valid, s - SLOPES[head] * dist, -jnp.inf)
    m = jnp.maximum(jnp.max(sc, axis=-1, keepdims=True), sink)
    p = jnp.exp(sc - m)
    es = jnp.exp(sink - m)
    inv = 1.0 / (jnp.sum(p, axis=-1, keepdims=True) + es)
    return p * inv, es * inv


def _attn_operands(q_ref, kvc_ref, kvp_ref):
    groups = range(N_HEADS // HEADS_PER_KV)
    kbd = [_block_diag(kvc_ref[:, 0:D_KV], kvp_ref[:, 0:D_KV], g) for g in groups]
    vbd = [_block_diag(kvc_ref[:, D_KV:2 * D_KV], kvp_ref[:, D_KV:2 * D_KV], g) for g in groups]
    qps = [(q_ref[:, j * 128:(j + 1) * 128] * SCALE).astype(BF16) for j in range(N_HEADS // 2)]
    return qps, kbd, vbd


def _attn_forward(q_ref, kvc_ref, kvp_ref, ga_ref, sink_ref, gn_ref, geometry, ya_ref, oa_ref, pr_ref, sp_ref):
    use_cur, dist, valid = geometry
    _, kbd, vbd = operands = _attn_operands(q_ref, kvc_ref, kvp_ref)
    yield
    scores = []
    for j, qp in enumerate(operands[0]):
        scores += _merge(_nt(qp, kbd[j // 4]), use_cur)
    yield
    sinks = [sink_ref[0, h] for h in range(N_HEADS)]
    scores = [jnp.where(valid, s - SLOPES[h] * dist, -jnp.inf) for h, s in enumerate(scores)]
    maxes = [jnp.maximum(jnp.max(s, axis=-1, keepdims=True), sinks[h]) for h, s in enumerate(scores)]
    yield
    exps = [jnp.exp(s - m) for s, m in zip(scores, maxes)]
    sink_exps = [jnp.exp(sinks[h] - m) for h, m in enumerate(maxes)]
    yield
    invs = [1.0 / (jnp.sum(e, axis=-1, keepdims=True) + se) for e, se in zip(exps, sink_exps)]
    probs = [e * inv for e, inv in zip(exps, invs)]
    pr_ref[...] = jnp.concatenate(probs, axis=1).astype(BF16)
    lane = lax.broadcasted_iota(jnp.int32, (BLK, 128), 1)
    sp_ref[...] = sum(jnp.where(lane == h, se * inv, 0.0) for h, (se, inv) in enumerate(zip(sink_exps, invs)))
    yield
    p4s = [_split(probs[2 * j], probs[2 * j + 1], use_cur).astype(BF16) for j in range(N_HEADS // 2)]
    ya = jnp.concatenate([_nn(p4, vbd[j // 4]) for j, p4 in enumerate(p4s)], axis=1)
    ya_ref[...] = ya
    yield
    silu, _ = _silu_and_grad(ga_ref[...])
    oa_ref[...] = (ya * _rstd(ya) * gn_ref[...] * silu).astype(BF16)


def _kv_specs(n_blocks, reverse):
    def blk(i):
        return (n_blocks - 1 - i) if reverse else i
    cur = pl.BlockSpec((BLK, 2 * D_KV), lambda i: (blk(i), 0))
    prev = pl.BlockSpec((BLK, 2 * D_KV), lambda i: (jnp.maximum(blk(i) - 1, 0), 0))
    return cur, prev


def _out_proj_loss(x, oc, oa, wo, norm_final, target, pc, conv_w, norm_conv_out):
    seq = x.shape[0]
    tile = TOK_TILE
    n_tiles = seq // tile

    def body(x_ref, oc_ref, oa_ref, wo_ref, gf_ref, t_ref, pc_ref, hcc_ref, hcu_ref, cw_ref, gn_ref,
             dx2_ref, doa_ref, gwo_ref, dpc_ref, small_ref, loss_acc, zbuf, dbuf):
        step = pl.program_id(0)

        def small_add(row, value):
            small_ref[row:row + 1, :] += jnp.sum(value, axis=0, keepdims=True)

        @pl.when(step == 0)
        def _():
            gwo_ref[...] = jnp.zeros_like(gwo_ref)
            small_ref[...] = jnp.zeros_like(small_ref)
            loss_acc[...] = jnp.zeros_like(loss_acc)
            dbuf[tile:tile + 8, :] = jnp.zeros((8, D_CONV), F32)

        oc, oa = oc_ref[...], oa_ref[...]
        x2 = x_ref[...] + _nn(oc, wo_ref[0:D_CONV, :]) + _nn(oa, wo_ref[D_CONV:D_MIX, :])
        r = _rstd(x2)
        xhat = x2 * r
        err = xhat * gf_ref[...] - t_ref[...]
        loss_acc[...] += jnp.sum(err * err, axis=0, keepdims=True) * (0.5 / D_MODEL)
        dy = err * (1.0 / D_MODEL)
        small_add(SMALL_NORM_FINAL, dy * xhat)
        dx2 = _rms_bwd(dy * gf_ref[...], xhat, r)
        dx2_ref[...] = dx2
        db = dx2.astype(BF16)
        do = _nt(db, wo_ref[0:D_CONV, :])
        doa_ref[...] = _nt(db, wo_ref[D_CONV:D_MIX, :])
        gwo_ref[0:D_CONV, :] += _tn(oc, db)
        gwo_ref[D_CONV:D_MIX, :] += _tn(oa, db)

        is_first_tile = step == n_tiles - 1
        zbuf[0:8, :] = jnp.where(is_first_tile, 0.0, hcc_ref[...] * hcu_ref[...])
        cb, cc, cu, z, z1, z2, conv = _conv_core(pc_ref, zbuf, cw_ref)
        silu, dsilu = _silu_and_grad(pc_ref[:, 3 * D_CONV:4 * D_CONV])
        yc = cb * conv
        rc = _rstd(yc)
        chat = yc * rc
        dn = do * silu
        dpc_ref[:, 3 * D_CONV:4 * D_CONV] = (do * (chat * gn_ref[...]) * dsilu).astype(BF16)
        small_add(SMALL_NORM_CONV, dn * chat)
        dyc = _rms_bwd(dn * gn_ref[...], chat, rc)
        dpc_ref[:, 0:D_CONV] = (dyc * conv).astype(BF16)
        dconv = dyc * cb
        small_add(SMALL_CONV_W, dconv * z2)
        small_add(SMALL_CONV_W + 1, dconv * z1)
        small_add(SMALL_CONV_W + 2, dconv * z)
        dbuf[0:tile, :] = dconv
        dz = cw_ref[2:3, :] * dconv + cw_ref[1:2, :] * dbuf[1:tile + 1, :] + cw_ref[0:1, :] * dbuf[2:tile + 2, :]
        dpc_ref[:, D_CONV:2 * D_CONV] = (dz * cu).astype(BF16)
        dpc_ref[:, 2 * D_CONV:3 * D_CONV] = (dz * cc).astype(BF16)
        dbuf[tile:tile + 8, :] = dbuf[0:8, :]

        @pl.when(step == n_tiles - 1)
        def _():
            lane = lax.broadcasted_iota(jnp.int32, (1, D_MODEL), 1)
            small_ref[SMALL_MISC:SMALL_MISC + 1, :] = jnp.where(lane == SMALL_LOSS_LANE, jnp.sum(loss_acc[...]), 0.0)

    def rev(i):
        return n_tiles - 1 - i

    def row(width):
        return pl.BlockSpec((tile, width), lambda i: (rev(i), 0))

    def halo(col_block):
        return pl.BlockSpec((8, D_CONV), lambda i: (jnp.maximum(rev(i) * (tile // 8) - 1, 0), col_block))

    def const(shape):
        return pl.BlockSpec(shape, lambda i: (0, 0))

    return pl.pallas_call(
        body, name="out_proj_loss", grid=(n_tiles,),
        out_shape=(jax.ShapeDtypeStruct((seq, D_MODEL), F32), jax.ShapeDtypeStruct((seq, D_ATTN), F32),
                   jax.ShapeDtypeStruct((D_MIX, D_MODEL), F32), jax.ShapeDtypeStruct((seq, D_PC), BF16),
                   jax.ShapeDtypeStruct((SMALL_ROWS, D_MODEL), F32)),
        in_specs=[row(D_MODEL), row(D_CONV), row(D_ATTN), _resident(wo.shape), _resident((1, D_MODEL)), row(D_MODEL),
                  row(D_PC), halo(1), halo(2), _resident(conv_w.shape), _resident((1, D_CONV))],
        out_specs=(row(D_MODEL), row(D_ATTN), const((D_MIX, D_MODEL)), row(D_PC), const((SMALL_ROWS, D_MODEL))),
        scratch_shapes=[pltpu.VMEM((1, D_MODEL), F32), pltpu.VMEM((tile + 8, D_CONV), F32),
                        pltpu.VMEM((tile + 8, D_CONV), F32)],
        compiler_params=_params(62, ("arbitrary",)),
    )(x, oc, oa, wo, norm_final, target, pc, pc, pc, conv_w, norm_conv_out)


def _attn_bwd(q, kv, ga, ya, doa, probs, sink_probs, norm_attn_out, gwo, small):
    seq = q.shape[0]
    n_blocks = seq // BLK
    stage_steps = (0, n_blocks // 4, n_blocks // 2, (3 * n_blocks) // 4, n_blocks - 1)

    def body(q_ref, kvc_ref, kvp_ref, ga_ref, ya_ref, doa_ref, pr_ref, sp_ref, gn_ref, gwo_ref, small_ref,
             dqg_ref, small_out, gwo_sh, gna_ref, gs_ref, carry, dya_buf, *rs_scratch):
        step = pl.program_id(0)

        @pl.when(step == 0)
        def _():
            gna_ref[...] = jnp.zeros_like(gna_ref)
            gs_ref[...] = jnp.zeros_like(gs_ref)
            carry[...] = jnp.zeros_like(carry)

        ya = ya_ref[...]
        r = _rstd(ya)
        xhat = ya * r
        silu, dsilu = _silu_and_grad(ga_ref[...])
        do = doa_ref[...]
        dn = do * silu
        dqg_ref[:, D_ATTN:2 * D_ATTN] = (do * (xhat * gn_ref[...]) * dsilu).astype(BF16)
        gna_ref[...] += jnp.sum(dn * xhat, axis=0, keepdims=True)
        dya_buf[...] = _rms_bwd(dn * gn_ref[...], xhat, r).astype(BF16)

        use_cur = _band_geometry(n_blocks - 1 - step)[0]
        lane = lax.broadcasted_iota(jnp.int32, (BLK, 128), 1)

        def fold(bd):
            return (jnp.where(lane < 64, bd[0:BLK], 0.0) + jnp.where(lane >= 64, bd[BLK:2 * BLK], 0.0),
                    jnp.where(lane < 64, bd[2 * BLK:3 * BLK], 0.0) + jnp.where(lane >= 64, bd[3 * BLK:4 * BLK], 0.0))

        pairs = range(N_HEADS // 2)
        qps, kbd, vbd = _attn_operands(q_ref, kvc_ref, kvp_ref)
        probs = [pr_ref[:, h * BLK:(h + 1) * BLK].astype(F32) for h in range(N_HEADS)]
        dyps = [dya_buf[:, j * 128:(j + 1) * 128] for j in pairs]
        dps = []
        for j in pairs:
            dps += _merge(_nt(dyps[j], vbd[j // 4]), use_cur)
        deltas = [jnp.sum(p * dp, axis=-1, keepdims=True) for p, dp in zip(probs, dps)]
        dss = [p * (dp - delta) for p, dp, delta in zip(probs, dps, deltas)]
        delta_lanes = sum(jnp.where(lane == h, deltas[h], 0.0) for h in range(N_HEADS))
        gs_ref[...] -= jnp.sum(sp_ref[...] * delta_lanes, axis=0, keepdims=True)
        ds4s = [_split(dss[2 * j], dss[2 * j + 1], use_cur).astype(BF16) for j in pairs]
        p4s = [_split(probs[2 * j], probs[2 * j + 1], use_cur).astype(BF16) for j in pairs]
        dqg_ref[:, 0:D_ATTN] = jnp.concatenate([_nn(ds4s[j], kbd[j // 4]) * SCALE for j in pairs], axis=1).astype(BF16)
        sums = []
        for group in range(N_HEADS // HEADS_PER_KV):
            acc = [jnp.zeros((BLK, 128), F32) for _ in range(4)]
            for j in range(group * 4, group * 4 + 4):
                for slot, part in enumerate(fold(_tn(ds4s[j], qps[j])) + fold(_tn(p4s[j], dyps[j]))):
                    acc[slot] = acc[slot] + part
            sums.append([a + pltpu.roll(a, 64, 1) for a in acc])
        dk_cur, dk_prev, dv_cur, dv_prev = (jnp.where(lane < 64, a, b) for a, b in zip(sums[0], sums[1]))
        dqg_ref[:, 2 * D_ATTN:2 * D_ATTN + 2 * D_KV] = (jnp.concatenate([dk_cur, dv_cur], axis=1) + carry[...]).astype(BF16)
        carry[...] = jnp.concatenate([dk_prev, dv_prev], axis=1)

        @pl.when(step == n_blocks - 1)
        def _():
            small_out[...] = small_ref[...]
            small_out[SMALL_NORM_ATTN:SMALL_NORM_ATTN + 1, :] = gna_ref[...]
            small_out[SMALL_MISC:SMALL_MISC + 1, 0:128] = small_ref[SMALL_MISC:SMALL_MISC + 1, 0:128] + gs_ref[...]

        for at, stage in zip(stage_steps, _rs_wout_stages(gwo_ref, gwo_sh, *rs_scratch)):
            pl.when(step == at)(stage)

    row = pl.BlockSpec((BLK, D_ATTN), lambda i: (n_blocks - 1 - i, 0))
    kv_cur, kv_prev = _kv_specs(n_blocks, reverse=True)
    any_spec = pl.BlockSpec(memory_space=pl.ANY)
    return pl.pallas_call(
        body, name="attn_bwd", grid=(n_blocks,),
        out_shape=(jax.ShapeDtypeStruct((seq, D_QG), BF16), jax.ShapeDtypeStruct(small.shape, F32),
                   jax.ShapeDtypeStruct((gwo.shape[0] // N_CHIPS, gwo.shape[1]), F32)),
        in_specs=[row, kv_cur, kv_prev, row, row, row,
                  pl.BlockSpec((BLK, N_HEADS * BLK), lambda i: (n_blocks - 1 - i, 0)),
                  pl.BlockSpec((BLK, 128), lambda i: (n_blocks - 1 - i, 0)), _resident((1, D_ATTN)),
                  any_spec, _resident(small.shape)],
        out_specs=(pl.BlockSpec((BLK, D_QG), lambda i: (n_blocks - 1 - i, 0)),
                   pl.BlockSpec(small.shape, lambda i: (0, 0)), any_spec),
        scratch_shapes=[pltpu.VMEM((1, D_ATTN), F32), pltpu.VMEM((1, 128), F32),
                        pltpu.VMEM((BLK, 2 * D_KV), F32), pltpu.VMEM((BLK, D_ATTN), BF16)] + _rs_wout_scratch(gwo.shape),
        compiler_params=_params(44, ("arbitrary",)),
    )(q, kv, kv, ga, ya, doa, probs, sink_probs, norm_attn_out, gwo, small)


GBLK = 256
GSUB = 64
PAIR_RING = 4
LAG_PAIR, LAG_HOP1, LAG_HOP2 = 1, 7, 14


def _bwd_in(dpc, dqg, h, wt, x, norm_in, dx2, small):
    seq = x.shape[0]
    n_blk = D_IN_PROJ // GBLK
    per_chip = n_blk // N_CHIPS
    n_slots = (n_blk + 1) // 2
    n_sub = GBLK // GSUB
    chip_rows = D_IN_PROJ // N_CHIPS
    tile = TOK_TILE
    n_tiles = seq // tile
    n_steps = n_blk + max(n_tiles, LAG_HOP2)
    chunk = min(seq, 512)
    blk_q, blk_kv, blk_ga = ROW_Q // GBLK, ROW_KV // GBLK, ROW_GA // GBLK

    def block_of(i):
        k = i % N_CHIPS
        robin = per_chip * ((k % 2) * 2 + k // 2) + i // N_CHIPS
        if isinstance(i, int):
            return robin if i < per_chip * N_CHIPS else i
        return jnp.where(i < per_chip * N_CHIPS, robin, i)

    def owner_of(i):
        return (i // N_CHIPS) % 2

    def slot_of(i):
        return (i // (2 * N_CHIPS)) * N_CHIPS + i % N_CHIPS

    def body(dpc_ref, dqg_ref, wt_ref, h_ref, x_ref, g_ref, dx2_ref, small_ref, gx_ref, small_sum, gwt_sh,
             dh_acc, gni, keep, pbuf, xbuf, land, land2, small_land,
             pair_send, pair_recv, h1_send, h1_recv, h2_send, h2_recv, sw_send, sw_recv, sm_send, sm_recv, out_sem):
        step = pl.program_id(0)
        x_i, y_i, c = lax.axis_index("x"), lax.axis_index("y"), lax.axis_index("c")
        me = 4 * x_i + 2 * y_i + c
        j = 2 * x_i + y_i
        pa = (_xor(x_i, 1 - c), _xor(y_i, c), c)
        pb = (_xor(x_i, c), _xor(y_i, 1 - c), c)
        sib = (x_i, y_i, 1 - c)
        ja = 2 * pa[0] + pa[1]
        jb = 2 * pb[0] + pb[1]
        jd = 3 - j

        def remote(src, dst, send, recv, to):
            return pltpu.make_async_remote_copy(src_ref=src, dst_ref=dst, send_sem=send, recv_sem=recv,
                                                device_id=to, device_id_type=MESH)

        def piece(ref, slot, u, n):
            return ref.at[slot, pl.ds(u * GSUB, n * GSUB), :]

        def chip_rows_at(ref, local, n):
            return ref.at[pl.ds(pl.multiple_of(local, GSUB), n * GSUB), :]

        def pair_copy(i):
            slot = slot_of(i)
            return remote(pbuf.at[i % PAIR_RING], land.at[slot], pair_send.at[slot], pair_recv.at[slot], sib)

        def h1_copy(slot, u, n):
            k = slot * n_sub + u
            return remote(piece(xbuf, slot, u, n), piece(xbuf, slot, u, n), h1_send.at[k], h1_recv.at[k], pa)

        def h2_copy(slot, u, n, local):
            k = slot * n_sub + u
            return remote(piece(xbuf, slot, u, n), chip_rows_at(land2, local, n), h2_send.at[k], h2_recv.at[k], pb)

        def sw_copy(slot, u, n, local):
            k = slot * n_sub + u
            return remote(piece(keep, slot, u, n), chip_rows_at(gwt_sh, local, n), sw_send.at[k], sw_recv.at[k], sib)

        def owned(i):
            return (i >= 0) & (i < n_blk) & (owner_of(i) == c)

        def chip_of(blk, u):
            row = blk * GBLK + u * GSUB
            chip = row // chip_rows
            return chip, row - chip * chip_rows

        def pieces(blk):
            first, local = chip_of(blk, 0)
            whole = first == chip_of(blk, n_sub - 1)[0]
            if isinstance(blk, int):
                return [(True, 0, n_sub, first, local)] if whole else [(True, u, 1) + chip_of(blk, u) for u in range(n_sub)]
            return [(whole, 0, n_sub, first, local)] + [(jnp.logical_not(whole), u, 1) + chip_of(blk, u) for u in range(n_sub)]

        @pl.when(step == 0)
        def _():
            dh_acc[...] = jnp.zeros_like(dh_acc)
            gni[...] = jnp.zeros_like(gni)

        @pl.when(step < n_blk)
        def _():
            from_pc = block_of(step) < blk_q
            block = _tn(jnp.where(from_pc, dpc_ref[...], dqg_ref[...]), h_ref[...])
            for t in range(0, seq, chunk):
                d = jnp.where(from_pc, dpc_ref[t:t + chunk, :], dqg_ref[t:t + chunk, :])
                dh_acc[t:t + chunk, :] += _nn(d, wt_ref[...])

            @pl.when(owner_of(step) == c)
            def _():
                keep[slot_of(step)] = block

            @pl.when(owner_of(step) != c)
            def _():
                @pl.when(step >= 2 * PAIR_RING)
                def _():
                    pair_copy(step - 2 * PAIR_RING).wait_send()
                pbuf[step % PAIR_RING] = block.astype(BF16)
                pair_copy(step).start()

        i1 = step - LAG_PAIR

        @pl.when(owned(i1))
        def _():
            slot = slot_of(i1)
            pair_copy(i1).wait_recv()
            _accumulate(keep.at[slot], land.at[slot])
            for cond, u, n, chip, _ in pieces(block_of(i1)):
                @pl.when(cond & ((chip == ja) | (chip == jd)))
                def _(u=u, n=n):
                    _cast_rows(piece(keep, slot, u, n), piece(xbuf, slot, u, n))
                    h1_copy(slot, u, n).start()

        i2 = step - LAG_HOP1

        @pl.when(owned(i2))
        def _():
            slot = slot_of(i2)
            for cond, u, n, chip, local in pieces(block_of(i2)):
                @pl.when(cond & ((chip == j) | (chip == jb)))
                def _(u=u, n=n, chip=chip, local=local):
                    h1_copy(slot, u, n).wait_recv()
                    _accumulate(piece(keep, slot, u, n), piece(xbuf, slot, u, n))

                    @pl.when(chip == jb)
                    def _():
                        _cast_rows(piece(keep, slot, u, n), piece(xbuf, slot, u, n))
                        h2_copy(slot, u, n, local).start()

                @pl.when(cond & ((chip == ja) | (chip == jd)))
                def _(u=u, n=n):
                    h1_copy(slot, u, n).wait_send()

        i3 = step - LAG_HOP2

        @pl.when(owned(i3))
        def _():
            slot = slot_of(i3)
            for cond, u, n, chip, local in pieces(block_of(i3)):
                @pl.when(cond & (chip == j))
                def _(u=u, n=n, local=local):
                    h2_copy(slot, u, n, local).wait_recv()
                    _accumulate(piece(keep, slot, u, n), chip_rows_at(land2, local, n))
                    mine = pltpu.make_async_copy(piece(keep, slot, u, n), chip_rows_at(gwt_sh, local, n), out_sem.at[0])
                    mine.start()
                    sw_copy(slot, u, n, local).start()
                    mine.wait()

                @pl.when(cond & (chip == jb))
                def _(u=u, n=n, local=local):
                    h2_copy(slot, u, n, local).wait_send()

        e = step - n_blk

        @pl.when((e >= 0) & (e < n_tiles))
        def _():
            dh = dh_acc[pl.ds(pl.multiple_of(e * tile, tile), tile), :]
            xv = x_ref[...]
            r = _rstd(xv)
            xhat = xv * r
            gni[...] += jnp.sum(dh * xhat, axis=0, keepdims=True)
            gx_ref[...] = _rms_bwd(dh * g_ref[...], xhat, r) + dx2_ref[...]

        @pl.when(step == n_steps - 1)
        def _():
            small_land[me] = small_ref[...]
            small_land[me, SMALL_NORM_IN:SMALL_NORM_IN + 1, :] = gni[...]
            others = [(dx, dy, dc) for dx in (0, 1) for dy in (0, 1) for dc in (0, 1)][1:]
            sends = [remote(small_land.at[me], small_land.at[me], sm_send.at[k], sm_recv.at[k],
                            (_xor(x_i, dx), _xor(y_i, dy), _xor(c, dc))) for k, (dx, dy, dc) in enumerate(others)]
            for cp in sends:
                cp.start()
            for i in range(n_blk):
                if i + 2 * PAIR_RING >= n_blk:
                    @pl.when(owner_of(i) != c)
                    def _(i=i):
                        pair_copy(i).wait_send()
                for _, u, n, chip, local in pieces(block_of(i)):
                    @pl.when((j == chip) & (c == owner_of(i)))
                    def _(i=i, u=u, n=n, local=local):
                        sw_copy(slot_of(i), u, n, local).wait_send()

                    @pl.when((j == chip) & (c != owner_of(i)))
                    def _(i=i, u=u, n=n, local=local):
                        sw_copy(slot_of(i), u, n, local).wait_recv()
            for cp in sends:
                cp.wait_recv()
            total = small_land[0]
            for dev in range(1, 8):
                total = total + small_land[dev]
            small_sum[...] = total
            for cp in sends:
                cp.wait_send()

    def blk_at(i):
        return block_of(jnp.clip(i, 0, n_blk - 1))

    last_pc_step = max(i for i in range(n_blk) if block_of(i) < blk_q)

    def next_block(i, in_pc):
        i = jnp.clip(i, 0, n_blk - 1)
        step = jnp.full_like(i, last_pc_step if in_pc else n_blk - 1)
        for ahead in reversed(range(N_CHIPS)):
            cand = jnp.minimum(i + ahead, n_blk - 1)
            step = jnp.where((block_of(cand) < blk_q) == in_pc, cand, step)
        return block_of(step)

    def dqg_block(i):
        b = next_block(i, False)
        q_blk = jnp.clip(b - blk_q, 0, blk_kv - blk_q - 1)
        ga_blk = (D_ATTN // GBLK) + jnp.clip(b - blk_ga, 0, n_blk - blk_ga - 1)
        return jnp.where(b < blk_kv, q_blk, jnp.where(b == blk_kv, 2 * D_ATTN // GBLK, ga_blk))

    def tok(i):
        return (jnp.clip(i - n_blk, 0, n_tiles - 1), 0)

    n_piece = n_slots * n_sub
    dma = pltpu.SemaphoreType.DMA
    return pl.pallas_call(
        body, name="bwd_in", grid=(n_steps,),
        out_shape=(jax.ShapeDtypeStruct((seq, D_MODEL), F32), jax.ShapeDtypeStruct(small.shape, F32),
                   jax.ShapeDtypeStruct((chip_rows, D_MODEL), F32)),
        in_specs=[pl.BlockSpec((seq, GBLK), lambda i: (0, next_block(i, True))),
                  pl.BlockSpec((seq, GBLK), lambda i: (0, dqg_block(i))),
                  pl.BlockSpec((GBLK, D_MODEL), lambda i: (blk_at(i), 0)),
                  _resident(h.shape),
                  pl.BlockSpec((tile, D_MODEL), tok), _resident((1, D_MODEL)), pl.BlockSpec((tile, D_MODEL), tok),
                  _resident(small.shape)],
        out_specs=(pl.BlockSpec((tile, D_MODEL), tok), pl.BlockSpec(small.shape, lambda i: (0, 0)),
                   pl.BlockSpec(memory_space=pl.ANY)),
        scratch_shapes=[pltpu.VMEM((seq, D_MODEL), F32), pltpu.VMEM((1, D_MODEL), F32),
                        pltpu.VMEM((n_slots, GBLK, D_MODEL), F32), pltpu.VMEM((PAIR_RING, GBLK, D_MODEL), BF16),
                        pltpu.VMEM((n_slots, GBLK, D_MODEL), BF16), pltpu.VMEM((n_slots, GBLK, D_MODEL), BF16),
                        pltpu.VMEM((chip_rows, D_MODEL), BF16), pltpu.VMEM((8,) + small.shape, F32),
                        dma((n_slots,)), dma((n_slots,)), dma((n_piece,)), dma((n_piece,)), dma((n_piece,)),
                        dma((n_piece,)), dma((n_piece,)), dma((n_piece,)), dma((7,)), dma((7,)), dma((1,))],
        compiler_params=_params(62, ("arbitrary",)),
    )(dpc, dqg, wt, h, x, norm_in, dx2, small)


def _accumulate(dst_ref, src_ref, rows=16):
    def step(i, carry):
        sl = pl.ds(pl.multiple_of(i * rows, rows), rows)
        dst_ref[sl, :] = dst_ref[sl, :] + src_ref[sl, :].astype(F32)
        return carry
    lax.fori_loop(0, dst_ref.shape[0] // rows, step, 0)


def _rs_wout_scratch(gwo_shape):
    o_half, width = gwo_shape[0] // N_CHIPS // 2, gwo_shape[1]
    return [pltpu.VMEM((4, o_half, width), F32), pltpu.VMEM((4, o_half, width), BF16),
            pltpu.VMEM((4, o_half, width), BF16), pltpu.VMEM((2, o_half, width), BF16),
            pltpu.VMEM((o_half, width), BF16),
            pltpu.SemaphoreType.DMA((8,)), pltpu.SemaphoreType.DMA((8,)), pltpu.SemaphoreType.DMA((4,))]


def _rs_wout_stages(gwo_ref, gwo_sh, acc_o, sb_o, r1o, r2o, r3o, send_sems, recv_sems, local_sems):
    o_rows = gwo_ref.shape[0] // N_CHIPS
    o_half = o_rows // 2
    x, y, c = lax.axis_index("x"), lax.axis_index("y"), lax.axis_index("c")
    j = 2 * x + y
    pa = (_xor(x, 1 - c), _xor(y, c), c)
    pb = (_xor(x, c), _xor(y, 1 - c), c)
    sib = (x, y, 1 - c)
    ja = 2 * pa[0] + pa[1]
    jb = 2 * pb[0] + pb[1]
    jd = 3 - j
    order = (ja, jd, jb, j)
    sib_order = (jb, jd, ja, j)

    def rcopy(k, src, dst, to):
        return pltpu.make_async_remote_copy(src_ref=src, dst_ref=dst, send_sem=send_sems.at[k],
                                            recv_sem=recv_sems.at[k], device_id=to, device_id_type=MESH)

    def o_rows_of(chip, half):
        return gwo_ref.at[pl.ds(pl.multiple_of(chip * o_rows + half * o_half, 8), o_half), :]

    def load_all(chips, half):
        cps = [pltpu.make_async_copy(o_rows_of(chip, half), acc_o.at[s], local_sems.at[s]) for s, chip in enumerate(chips)]
        for cp in cps:
            cp.start()
        return cps

    def pair_send(s):
        return rcopy(s, sb_o.at[s], r1o.at[s], sib)

    def out_half(half):
        return gwo_sh.at[pl.ds(pl.multiple_of(half * o_half, 8), o_half), :]

    hop1 = [rcopy(4 + s, sb_o.at[s], r2o.at[s], pa) for s in range(2)]
    hop2 = rcopy(6, sb_o.at[2], r3o, pb)
    swap = rcopy(7, acc_o.at[3], out_half(c), sib)
    mine = pltpu.make_async_copy(acc_o.at[3], out_half(c), local_sems.at[0])

    def resend(s, copy):
        pair_send(s).wait_send()
        _cast_rows(acc_o.at[s], sb_o.at[s])
        copy.start()

    def stage_pair():
        for s, cp in enumerate(load_all(sib_order, 1 - c)):
            cp.wait()
            _cast_rows(acc_o.at[s], sb_o.at[s])
            pair_send(s).start()

    def stage_hop1():
        for s, cp in enumerate(load_all(order, c)):
            cp.wait()
            pair_send(s).wait_recv()
            _accumulate(acc_o.at[s], r1o.at[s])
            if s < 2:
                resend(s, hop1[s])

    def stage_hop2():
        hop1[1].wait_recv()
        _accumulate(acc_o.at[2], r2o.at[1])
        resend(2, hop2)
        hop1[0].wait_recv()
        _accumulate(acc_o.at[3], r2o.at[0])

    def stage_final():
        hop2.wait_recv()
        _accumulate(acc_o.at[3], r3o)
        mine.start()
        swap.start()

    def stage_drain():
        rcopy(7, acc_o.at[3], out_half(1 - c), sib).wait_recv()
        for cp in [pair_send(3)] + hop1 + [hop2, swap]:
            cp.wait_send()
        mine.wait()

    return stage_pair, stage_hop1, stage_hop2, stage_final, stage_drain


def _adamw(name, w, g, m, v, rows):
    def body(w_ref, g_ref, m_ref, v_ref, go_ref, d_ref, nm_ref, nv_ref):
        _adamw_update(w_ref, g_ref, m_ref, v_ref, go_ref, d_ref, nm_ref, nv_ref)

    spec = pl.BlockSpec((rows, w.shape[1]), lambda i: (i, 0))
    shape = jax.ShapeDtypeStruct(w.shape, F32)
    return pl.pallas_call(
        body, name="adamw_" + name, grid=(w.shape[0] // rows,),
        out_shape=(shape,) * 4, in_specs=[spec] * 4, out_specs=(spec,) * 4,
        compiler_params=_params(32, ("arbitrary",)),
    )(w, g, m, v)


SC_TILES = 32
SC_LANES = 16
SC_CHUNK = 4096


def _adamw_sparsecore(name, w, g, m, v):
    shape = w.shape
    flat = [a.reshape(-1) for a in (w, g, m, v)]
    per_tile = flat[0].shape[0] // SC_TILES

    def body(w_hbm, g_hbm, m_hbm, v_hbm, go_hbm, d_hbm, nm_hbm, nv_hbm, wb, gb, mb, vb, db):
        tile = lax.axis_index("subcore") * 2 + lax.axis_index("core")

        @pl.loop(0, per_tile, step=SC_CHUNK)
        def _(lo):
            mine = pl.ds(tile * per_tile + lo, SC_CHUNK)
            for src, buf in ((w_hbm, wb), (g_hbm, gb), (m_hbm, mb), (v_hbm, vb)):
                pltpu.sync_copy(src.at[mine], buf)

            @pl.loop(0, SC_CHUNK, step=SC_LANES)
            def _(i):
                at = pl.ds(i, SC_LANES)
                gv = gb[at]
                nm = ADAM_B1 * mb[at] + (1.0 - ADAM_B1) * gv
                nv = ADAM_B2 * vb[at] + (1.0 - ADAM_B2) * (gv * gv)
                m_hat = nm / (1.0 - ADAM_B1 ** ADAM_STEP)
                v_hat = nv / (1.0 - ADAM_B2 ** ADAM_STEP)
                db[at] = -ADAM_LR * (m_hat / (jnp.sqrt(v_hat) + ADAM_EPS) + ADAM_WD * wb[at])
                mb[at] = nm
                vb[at] = nv

            for buf, dst in ((gb, go_hbm), (db, d_hbm), (mb, nm_hbm), (vb, nv_hbm)):
                pltpu.sync_copy(buf, dst.at[mine])

    out = pl.kernel(
        body, name="adamw_sc_" + name,
        out_type=(jax.ShapeDtypeStruct(flat[0].shape, F32),) * 4,
        mesh=plsc.VectorSubcoreMesh(core_axis_name="core", subcore_axis_name="subcore"),
        scratch_types=[pltpu.VMEM((SC_CHUNK,), F32)] * 5,
    )(*flat)
    return tuple(o.reshape(shape) for o in out)


def _adamw_update(w_ref, g_ref, m_ref, v_ref, go_ref, d_ref, nm_ref, nv_ref):
    gv = g_ref[...]
    go_ref[...] = gv
    nm = ADAM_B1 * m_ref[...] + (1.0 - ADAM_B1) * gv
    nv = ADAM_B2 * v_ref[...] + (1.0 - ADAM_B2) * (gv * gv)
    m_hat = nm / (1.0 - ADAM_B1 ** ADAM_STEP)
    v_hat = nv / (1.0 - ADAM_B2 ** ADAM_STEP)
    d_ref[...] = -ADAM_LR * (m_hat / (jnp.sqrt(v_hat) + ADAM_EPS) + ADAM_WD * w_ref[...])
    nm_ref[...] = nm
    nv_ref[...] = nv


def _adamw_small(weights, grads, ms, vs):
    n = len(weights)

    def body(*refs):
        ins, outs = refs[:4 * n], refs[4 * n:]
        for k in range(n):
            _adamw_update(*ins[4 * k:4 * k + 4], *outs[4 * k:4 * k + 4])

    flat = [a for group in zip(weights, grads, ms, vs) for a in group]
    vmem = pl.BlockSpec(memory_space=pltpu.VMEM)
    out = pl.pallas_call(
        body, name="adamw_small",
        out_shape=tuple(jax.ShapeDtypeStruct(w.shape, F32) for w in weights for _ in range(4)),
        in_specs=[vmem] * (4 * n), out_specs=(vmem,) * (4 * n),
    )(*flat)
    return [tuple(out[4 * k:4 * k + 4]) for k in range(n)]


def kernel(x, norm_in, w_in, conv_w, attn_sinks, norm_conv_out, norm_attn_out, w_out, norm_final, loss_target, m_norm_in, m_w_in, m_conv_w, m_attn_sinks, m_norm_conv_out, m_norm_attn_out, m_w_out, m_norm_final, v_norm_in, v_w_in, v_conv_w, v_attn_sinks, v_norm_conv_out, v_norm_attn_out, v_w_out, v_norm_final):
    chip = 2 * lax.axis_index("x") + lax.axis_index("y")
    xs, target = x[0], loss_target[0]
    norm_final2 = norm_final.reshape(1, D_MODEL)
    w_in_t, m_w_in_t, v_w_in_t = w_in[0].T, m_w_in[0].T, v_w_in[0].T

    h, pc, q, kv, ga, oc, ya, oa, probs, sink_probs, wt, cw, wo = _fwd_in(
        xs, norm_in, w_in_t, w_out[0], conv_w[0], norm_conv_out, attn_sinks, norm_attn_out)
    dx2, doa, gwo, dpc, small = _out_proj_loss(xs, oc, oa, wo, norm_final2, target, pc, cw, norm_conv_out)
    dqg, small, gwo_sh = _attn_bwd(q, kv, ga, ya, doa, probs, sink_probs, norm_attn_out, gwo, small)
    grad_x, small_sum, gwt_sh = _bwd_in(dpc, dqg, h, wt, xs, norm_in, dx2, small)

    loss = small_sum[SMALL_MISC, SMALL_LOSS_LANE]
    g_norm_in, g_norm_conv, g_norm_attn = (small_sum[r:r + 1] for r in (SMALL_NORM_IN, SMALL_NORM_CONV, SMALL_NORM_ATTN))
    g_norm_final = small_sum[SMALL_NORM_FINAL]
    g_conv_w = lax.dynamic_slice(small_sum[SMALL_CONV_W:SMALL_CONV_W + 3], (0, chip * (D_CONV // N_CHIPS)),
                                 (3, D_CONV // N_CHIPS))[None]
    g_sinks = small_sum[SMALL_MISC:SMALL_MISC + 1, 0:N_HEADS]

    def two_d(a):
        return a.reshape(-1, a.shape[-1])

    def from_hbm(*arrays):
        return tuple(pltpu.with_memory_space_constraint(a, pltpu.HBM) for a in arrays)

    up_w_in = tuple(o.T[None] for o in _adamw("w_in", *from_hbm(w_in_t, gwt_sh, m_w_in_t, v_w_in_t), 200))
    up_w_out = tuple(o[None] for o in _adamw_sparsecore("w_out", w_out[0], gwo_sh, m_w_out[0], v_w_out[0]))
    small_w = (norm_in, conv_w, attn_sinks, norm_conv_out, norm_attn_out, norm_final)
    small_g = (g_norm_in, g_conv_w, g_sinks, g_norm_conv, g_norm_attn, g_norm_final)
    small_m = (m_norm_in, m_conv_w, m_attn_sinks, m_norm_conv_out, m_norm_attn_out, m_norm_final)
    small_v = (v_norm_in, v_conv_w, v_attn_sinks, v_norm_conv_out, v_norm_attn_out, v_norm_final)
    up_small = _adamw_small(*(tuple(two_d(a) for a in group) for group in (small_w, small_g, small_m, small_v)))
    up_small = [tuple(o.reshape(w.shape) for o in up) for up, w in zip(up_small, small_w)]
    up_norm_in, up_conv_w, up_sinks, up_norm_conv, up_norm_attn, up_norm_final = up_small
    updates = (up_norm_in, up_w_in, up_conv_w, up_sinks, up_norm_conv, up_norm_attn, up_w_out, up_norm_final)
    grads_out, deltas, new_m, new_v = zip(*updates)
    return (loss, grad_x[None], *grads_out, *deltas, *new_m, *new_v)
```

```python
import jax
import jax.numpy as jnp
from jax import lax
from jax.experimental import pallas as pl
from jax.experimental.pallas import tpu as pltpu
from jax.experimental.pallas import tpu_sc as plsc

F32 = jnp.float32
BF16 = jnp.bfloat16
MESH = pl.DeviceIdType.MESH

D_MODEL = 1024
D_CONV = 1024
D_ATTN = 1024
D_KV = 128
D_QG = 2 * D_ATTN + 2 * D_KV
D_MIX = D_CONV + D_ATTN
D_PC = 4 * D_CONV
D_IN_PROJ = D_PC + 2 * D_ATTN + 2 * D_KV
ROW_Q = D_PC
ROW_KV = ROW_Q + D_ATTN
ROW_GA = ROW_KV + 2 * D_KV
N_HEADS = 16
HEAD_DIM = 64
HEADS_PER_KV = 8
BLK = 128
N_CHIPS = 4
RMS_EPS = 1e-5
SCALE = HEAD_DIM ** -0.5
SLOPES = tuple(2.0 ** (-8.0 * (h + 1) / N_HEADS) for h in range(N_HEADS))

ADAM_LR, ADAM_B1, ADAM_B2, ADAM_EPS, ADAM_WD, ADAM_STEP = 0.001, 0.9, 0.999, 1e-08, 0.01, 10

SMALL_ROWS = 8
SMALL_NORM_IN, SMALL_NORM_CONV, SMALL_NORM_ATTN, SMALL_NORM_FINAL, SMALL_CONV_W, SMALL_MISC = 0, 1, 2, 3, 4, 7
SMALL_LOSS_LANE = N_HEADS

TOK_TILE = 256
PC_PIECE = 512
MIB = 1 << 20


def _params(vmem_mib, semantics=None):
    return pltpu.CompilerParams(dimension_semantics=semantics, vmem_limit_bytes=vmem_mib * MIB)


def _nn(a, b):
    return jnp.dot(a, b, preferred_element_type=F32)


def _nt(a, b):
    return lax.dot_general(a, b, (((1,), (1,)), ((), ())), preferred_element_type=F32)


def _tn(a, b):
    return lax.dot_general(a, b, (((0,), (0,)), ((), ())), preferred_element_type=F32)


def _rstd(v):
    return lax.rsqrt(jnp.mean(v * v, axis=-1, keepdims=True) + RMS_EPS)


def _rms_bwd(g, xhat, rstd):
    return rstd * (g - xhat * jnp.mean(g * xhat, axis=-1, keepdims=True))


def _silu_and_grad(g):
    s = jax.nn.sigmoid(g)
    return g * s, s * (1.0 + g * (1.0 - s))


def _resident(shape):
    return pl.BlockSpec(shape, lambda *_: (0,) * len(shape), pipeline_mode=pl.Buffered(1))


def _xor(a, b):
    return a + b - 2 * a * b


def _cast_rows(src_ref, dst_ref, rows=32):
    def step(i, carry):
        sl = pl.ds(pl.multiple_of(i * rows, rows), rows)
        dst_ref[sl, :] = src_ref[sl, :].astype(dst_ref.dtype)
        return carry
    lax.fori_loop(0, src_ref.shape[0] // rows, step, 0)


def _ag_scratch(shard_shape):
    rows, width = shard_shape
    return [pltpu.VMEM((rows, width), F32), pltpu.VMEM((rows, width), BF16), pltpu.VMEM((3, rows // 2, width), BF16),
            pltpu.SemaphoreType.DMA((6,)), pltpu.SemaphoreType.DMA((6,)), pltpu.SemaphoreType.DMA((4,))]


def _ag_stages(sh_ref, out, f32_buf, own, land, send_sems, recv_sems, local_sems):
    rows = sh_ref.shape[0]
    half = rows // 2
    x, y, c = lax.axis_index("x"), lax.axis_index("y"), lax.axis_index("c")
    j = 2 * x + y
    p1 = (_xor(x, c), _xor(y, 1 - c), c)
    p2 = (_xor(x, 1 - c), _xor(y, c), c)
    sib = (x, y, 1 - c)
    j1 = 2 * p1[0] + p1[1]
    j2 = 2 * p2[0] + p2[1]
    j3 = 3 - j

    def rows_of(chip, hf):
        return out.at[pl.ds(pl.multiple_of(chip * rows + hf * half, 16), half), :]

    def rcopy(k, src, dst, to):
        return pltpu.make_async_remote_copy(src_ref=src, dst_ref=dst, send_sem=send_sems.at[k],
                                            recv_sem=recv_sems.at[k], device_id=to, device_id_type=MESH)

    my_half = own.at[pl.ds(pl.multiple_of(c * half, 16), half), :]
    hop1 = rcopy(0, my_half, land.at[0], p1)
    hop2_own = rcopy(1, my_half, land.at[1], p2)
    hop2_fwd = rcopy(2, land.at[0], land.at[2], p2)
    swaps = [rcopy(3 + s, land.at[s], rows_of(chip, c), sib) for s, chip in enumerate((j1, j2, j3))]
    keeps = [pltpu.make_async_copy(land.at[s], rows_of(chip, c), local_sems.at[1 + s]) for s, chip in enumerate((j1, j2, j3))]
    load = pltpu.make_async_copy(sh_ref, f32_buf, local_sems.at[0])
    own_out = pltpu.make_async_copy(own, out.at[pl.ds(pl.multiple_of(j * rows, 16), rows), :], local_sems.at[0])

    def stage_send():
        load.start()
        load.wait()
        _cast_rows(f32_buf, own)
        own_out.start()
        hop1.start()

    def stage_forward():
        hop1.wait_recv()
        hop2_own.start()
        hop2_fwd.start()
        swaps[0].start()
        keeps[0].start()

    def stage_publish():
        hop2_own.wait_recv()
        swaps[1].start()
        keeps[1].start()
        hop2_fwd.wait_recv()
        swaps[2].start()
        keeps[2].start()

    def stage_drain():
        for s, chip in enumerate((j2, j1, j3)):
            rcopy(3 + s, my_half, rows_of(chip, 1 - c), sib).wait_recv()
        for cp in [hop1, hop2_own, hop2_fwd] + swaps:
            cp.wait_send()
        for cp in [own_out] + keeps:
            cp.wait()

    return stage_send, stage_forward, stage_publish, stage_drain


AG_CAST_ROWS = 400


def _gather_resident(sh_ref, out, f32_buf, send_sems, recv_sems, local_sems):
    rows = sh_ref.shape[0]
    half = rows // 2
    x, y, c = lax.axis_index("x"), lax.axis_index("y"), lax.axis_index("c")
    j = 2 * x + y
    p1 = (_xor(x, c), _xor(y, 1 - c), c)
    p2 = (_xor(x, 1 - c), _xor(y, c), c)
    sib = (x, y, 1 - c)
    j1 = 2 * p1[0] + p1[1]
    j2 = 2 * p2[0] + p2[1]
    j3 = 3 - j

    def rows_of(chip, hf):
        return out.at[pl.ds(pl.multiple_of(chip * rows + hf * half, 16), half), :]

    def send(k, chip, to):
        return pltpu.make_async_remote_copy(src_ref=rows_of(chip, c), dst_ref=rows_of(chip, c), send_sem=send_sems.at[k],
                                            recv_sem=recv_sems.at[k], device_id=to, device_id_type=MESH)

    per_half = half // AG_CAST_ROWS

    def cast_own(chunk):
        lo = pl.multiple_of(chunk * AG_CAST_ROWS, 16)
        load = pltpu.make_async_copy(sh_ref.at[pl.ds(lo, AG_CAST_ROWS), :], f32_buf, local_sems.at[0])
        load.start()
        load.wait()
        _cast_rows(f32_buf, out.at[pl.ds(pl.multiple_of(j * rows + lo, 16), AG_CAST_ROWS), :], rows=16)

    for k in range(per_half):
        cast_own(c * per_half + k)
    hop1 = send(0, j, p1)
    hop1.start()
    for k in range(per_half):
        cast_own((1 - c) * per_half + k)
    hop1.wait_recv()
    sends = [hop1, send(1, j, p2), send(2, j1, p2), send(3, j1, sib)]
    for cp in sends[1:]:
        cp.start()
    sends[1].wait_recv()
    sends.append(send(4, j2, sib))
    sends[-1].start()
    sends[2].wait_recv()
    sends.append(send(5, j3, sib))
    sends[-1].start()
    for k in (3, 4, 5):
        send(k, j, sib).wait_recv()
    for cp in sends:
        cp.wait_send()


def _fwd_in(x, norm_in, wt_sh, wo_sh, cw_sh, norm_conv_out, sinks, norm_attn_out):
    seq = x.shape[0]
    tile = TOK_TILE
    n_tiles = seq // tile
    stage_steps = (0, n_tiles // 4, (5 * n_tiles) // 8, n_tiles - 1)
    blocks = tile // BLK
    cw_cols = cw_sh.shape[1]

    def body(x_ref, g_ref, wtsh_ref, wo_ref, cwsh_ref, gn_ref, sink_ref, gna_ref,
             h_ref, pc_ref, q_ref, kv_ref, ga_ref, oc_ref, ya_ref, oa_ref, pr_ref, sp_ref, wt_out, cw_ref, wo_out,
             zbuf, kv_last, wt_ref, f32_buf, cw_land, wt_send, wt_recv, wt_local, cw_send, cw_recv, cw_local, *ag_scratch):
        step = pl.program_id(0)
        to_hbm = pltpu.make_async_copy(wt_ref, wt_out, wt_local.at[0])

        @pl.when(step == 0)
        def _():
            zbuf[0:8, :] = jnp.zeros((8, D_CONV), F32)
            kv_last[...] = jnp.zeros_like(kv_last)
            x_i, y_i, c = lax.axis_index("x"), lax.axis_index("y"), lax.axis_index("c")
            j = 2 * x_i + y_i
            p1 = (_xor(x_i, c), _xor(y_i, 1 - c), c)
            p2 = (_xor(x_i, 1 - c), _xor(y_i, c), c)
            j1 = 2 * p1[0] + p1[1]

            def cw_copy(k, src, chip, to):
                return pltpu.make_async_remote_copy(src_ref=src, dst_ref=cw_land.at[chip], send_sem=cw_send.at[k],
                                                    recv_sem=cw_recv.at[k], device_id=to, device_id_type=MESH)

            mine = pltpu.make_async_copy(cwsh_ref, cw_land.at[j], cw_local.at[0])
            mine.start()
            first = cw_copy(0, cwsh_ref, j, p1)
            first.start()
            _gather_resident(wtsh_ref, wt_ref, f32_buf, wt_send, wt_recv, wt_local)
            to_hbm.start()
            first.wait_recv()
            second = [cw_copy(1, cwsh_ref, j, p2), cw_copy(2, cw_land.at[j1], j1, p2)]
            for cp in second:
                cp.start()
            for cp in second:
                cp.wait_recv()
            for cp in [first] + second:
                cp.wait_send()
            mine.wait()
            for chip in range(N_CHIPS):
                cw_ref[:, chip * cw_cols:(chip + 1) * cw_cols] = cw_land[chip]

        stages = _ag_stages(wo_ref, wo_out, *ag_scratch)
        for at, stage in zip(stage_steps[:-1], stages[:-1]):
            pl.when(step == at)(stage)

        def attention(b):
            rows = pl.ds(b * BLK, BLK)
            kv_prev = kv_last if b == 0 else kv_ref.at[pl.ds((b - 1) * BLK, BLK), :]
            return _attn_forward(q_ref.at[rows, :], kv_ref.at[rows, :], kv_prev, ga_ref.at[rows, :], sink_ref, gna_ref,
                                 _band_geometry(step * blocks + b), ya_ref.at[rows, :], oa_ref.at[rows, :],
                                 pr_ref.at[rows, :], sp_ref.at[rows, :])

        xv = x_ref[...]
        h = (xv * _rstd(xv) * g_ref[...]).astype(BF16)
        h_ref[...] = h
        q_ref[...] = _nt(h, wt_ref[ROW_Q:ROW_KV, :])
        kv_ref[...] = _nt(h, wt_ref[ROW_KV:ROW_GA, :])
        ga_ref[...] = _nt(h, wt_ref[ROW_GA:D_IN_PROJ, :])
        attention_blocks = [attention(b) for b in range(blocks)]
        for lo in range(0, D_PC, PC_PIECE):
            pc_ref[:, lo:lo + PC_PIECE] = _nt(h, wt_ref[lo:lo + PC_PIECE, :])
            for stages_of_block in attention_blocks:
                next(stages_of_block, None)
        for stages_of_block in attention_blocks:
            for _ in stages_of_block:
                pass
        kv_last[...] = kv_ref[tile - BLK:tile, :]

        cb, _, _, _, _, _, conv = _conv_core(pc_ref, zbuf, cw_ref)
        yc = cb * conv
        silu, _ = _silu_and_grad(pc_ref[:, 3 * D_CONV:4 * D_CONV])
        oc_ref[...] = (yc * _rstd(yc) * gn_ref[...] * silu).astype(BF16)
        zbuf[0:8, :] = zbuf[tile:tile + 8, :]

        @pl.when(step == stage_steps[-1])
        def _():
            stages[-1]()
            to_hbm.wait()

    def row(width):
        return pl.BlockSpec((tile, width), lambda i: (i, 0))

    any_spec = pl.BlockSpec(memory_space=pl.ANY)
    dma = pltpu.SemaphoreType.DMA
    wt_shape = (N_CHIPS * wt_sh.shape[0], wt_sh.shape[1])
    cw_shape = (cw_sh.shape[0], N_CHIPS * cw_cols)
    return pl.pallas_call(
        body, name="fwd_in", grid=(n_tiles,),
        out_shape=(jax.ShapeDtypeStruct((seq, D_MODEL), BF16), jax.ShapeDtypeStruct((seq, D_PC), F32),
                   jax.ShapeDtypeStruct((seq, D_ATTN), F32), jax.ShapeDtypeStruct((seq, 2 * D_KV), F32),
                   jax.ShapeDtypeStruct((seq, D_ATTN), F32), jax.ShapeDtypeStruct((seq, D_CONV), BF16),
                   pltpu.HBM((seq, D_ATTN), F32), pltpu.HBM((seq, D_ATTN), BF16),
                   pltpu.HBM((seq, N_HEADS * BLK), BF16), pltpu.HBM((seq, 128), F32),
                   pltpu.HBM(wt_shape, BF16), jax.ShapeDtypeStruct(cw_shape, F32),
                   jax.ShapeDtypeStruct((N_CHIPS * wo_sh.shape[0], wo_sh.shape[1]), BF16)),
        in_specs=[row(D_MODEL), _resident((1, D_MODEL)), any_spec, any_spec, any_spec, _resident((1, D_CONV)),
                  pl.BlockSpec(memory_space=pltpu.SMEM), _resident((1, D_ATTN))],
        out_specs=(row(D_MODEL), row(D_PC), row(D_ATTN), row(2 * D_KV), row(D_ATTN), row(D_CONV), row(D_ATTN),
                   row(D_ATTN), row(N_HEADS * BLK), row(128), any_spec, pl.BlockSpec(cw_shape, lambda i: (0, 0)),
                   any_spec),
        scratch_shapes=[pltpu.VMEM((tile + 8, D_CONV), F32), pltpu.VMEM((BLK, 2 * D_KV), F32),
                        pltpu.VMEM(wt_shape, BF16), pltpu.VMEM((AG_CAST_ROWS, wt_sh.shape[1]), F32),
                        pltpu.VMEM((N_CHIPS,) + cw_sh.shape, F32),
                        dma((6,)), dma((6,)), dma((1,)), dma((3,)), dma((3,)), dma((1,))] + _ag_scratch(wo_sh.shape),
        compiler_params=_params(62, ("arbitrary",)),
    )(x, norm_in, wt_sh, wo_sh, cw_sh, norm_conv_out, sinks, norm_attn_out)


def _conv_core(pc_ref, zbuf, cw_ref):
    tile = pc_ref.shape[0]
    cb = pc_ref[:, 0:D_CONV]
    cc = pc_ref[:, D_CONV:2 * D_CONV]
    cu = pc_ref[:, 2 * D_CONV:3 * D_CONV]
    z = cc * cu
    zbuf[8:tile + 8, :] = z
    z1 = zbuf[7:tile + 7, :]
    z2 = zbuf[6:tile + 6, :]
    conv = cw_ref[0:1, :] * z2 + cw_ref[1:2, :] * z1 + cw_ref[2:3, :] * z
    return cb, cc, cu, z, z1, z2, conv


def _band_geometry(block_index):
    qi = lax.broadcasted_iota(jnp.int32, (BLK, BLK), 0)
    kp = lax.broadcasted_iota(jnp.int32, (BLK, BLK), 1)
    use_cur = kp <= qi
    dist = jnp.where(use_cur, qi - kp, qi - kp + BLK).astype(F32)
    valid = use_cur | (block_index > 0)
    return use_cur, dist, valid


def _block_diag(cur, prev, group):
    lane = lax.broadcasted_iota(jnp.int32, cur.shape, 1)

    def halves(t):
        other = pltpu.roll(t, 64, 1)
        lo, hi = (t, other) if group == 0 else (other, t)
        return jnp.where(lane < 64, lo, 0.0), jnp.where(lane >= 64, hi, 0.0)

    return jnp.concatenate(halves(cur) + halves(prev), axis=0).astype(BF16)


def _merge(s4, use_cur):
    return (jnp.where(use_cur, s4[:, 0:BLK], s4[:, 2 * BLK:3 * BLK]),
            jnp.where(use_cur, s4[:, BLK:2 * BLK], s4[:, 3 * BLK:4 * BLK]))


def _split(a, b, use_cur):
    return jnp.concatenate([jnp.where(use_cur, a, 0.0), jnp.where(use_cur, b, 0.0),
                            jnp.where(use_cur, 0.0, a), jnp.where(use_cur, 0.0, b)], axis=1)


def _softmax_head(s, head, sink, dist, valid):
    sc = jnp.where(valid, s - SLOPES[head] * dist, -jnp.inf)
    m = jnp.maximum(jnp.max(sc, axis=-1, keepdims=True), sink)
    p = jnp.exp(sc - m)
    es = jnp.exp(sink - m)
    inv = 1.0 / (jnp.sum(p, axis=-1, keepdims=True) + es)
    return p * inv, es * inv


def _attn_operands(q_ref, kvc_ref, kvp_ref):
    groups = range(N_HEADS // HEADS_PER_KV)
    kbd = [_block_diag(kvc_ref[:, 0:D_KV], kvp_ref[:, 0:D_KV], g) for g in groups]
    vbd = [_block_diag(kvc_ref[:, D_KV:2 * D_KV], kvp_ref[:, D_KV:2 * D_KV], g) for g in groups]
    qps = [(q_ref[:, j * 128:(j + 1) * 128] * SCALE).astype(BF16) for j in range(N_HEADS // 2)]
    return qps, kbd, vbd


def _attn_forward(q_ref, kvc_ref, kvp_ref, ga_ref, sink_ref, gn_ref, geometry, ya_ref, oa_ref, pr_ref, sp_ref):
    use_cur, dist, valid = geometry
    _, kbd, vbd = operands = _attn_operands(q_ref, kvc_ref, kvp_ref)
    yield
    scores = []
    for j, qp in enumerate(operands[0]):
        scores += _merge(_nt(qp, kbd[j // 4]), use_cur)
    yield
    sinks = [sink_ref[0, h] for h in range(N_HEADS)]
    scores = [jnp.where(valid, s - SLOPES[h] * dist, -jnp.inf) for h, s in enumerate(scores)]
    maxes = [jnp.maximum(jnp.max(s, axis=-1, keepdims=True), sinks[h]) for h, s in enumerate(scores)]
    yield
    exps = [jnp.exp(s - m) for s, m in zip(scores, maxes)]
    sink_exps = [jnp.exp(sinks[h] - m) for h, m in enumerate(maxes)]
    yield
    invs = [1.0 / (jnp.sum(e, axis=-1, keepdims=True) + se) for e, se in zip(exps, sink_exps)]
    probs = [e * inv for e, inv in zip(exps, invs)]
    pr_ref[...] = jnp.concatenate(probs, axis=1).astype(BF16)
    lane = lax.broadcasted_iota(jnp.int32, (BLK, 128), 1)
    sp_ref[...] = sum(jnp.where(lane == h, se * inv, 0.0) for h, (se, inv) in enumerate(zip(sink_exps, invs)))
    yield
    p4s = [_split(probs[2 * j], probs[2 * j + 1], use_cur).astype(BF16) for j in range(N_HEADS // 2)]
    ya = jnp.concatenate([_nn(p4, vbd[j // 4]) for j, p4 in enumerate(p4s)], axis=1)
    ya_ref[...] = ya
    yield
    silu, _ = _silu_and_grad(ga_ref[...])
    oa_ref[...] = (ya * _rstd(ya) * gn_ref[...] * silu).astype(BF16)


def _kv_specs(n_blocks, reverse):
    def blk(i):
        return (n_blocks - 1 - i) if reverse else i
    cur = pl.BlockSpec((BLK, 2 * D_KV), lambda i: (blk(i), 0))
    prev = pl.BlockSpec((BLK, 2 * D_KV), lambda i: (jnp.maximum(blk(i) - 1, 0), 0))
    return cur, prev


def _out_proj_loss(x, oc, oa, wo, norm_final, target, pc, conv_w, norm_conv_out):
    seq = x.shape[0]
    tile = TOK_TILE
    n_tiles = seq // tile

    def body(x_ref, oc_ref, oa_ref, wo_ref, gf_ref, t_ref, pc_ref, hcc_ref, hcu_ref, cw_ref, gn_ref,
             dx2_ref, doa_ref, gwo_ref, dpc_ref, small_ref, loss_acc, zbuf, dbuf):
        step = pl.program_id(0)

        def small_add(row, value):
            small_ref[row:row + 1, :] += jnp.sum(value, axis=0, keepdims=True)

        @pl.when(step == 0)
        def _():
            gwo_ref[...] = jnp.zeros_like(gwo_ref)
            small_ref[...] = jnp.zeros_like(small_ref)
            loss_acc[...] = jnp.zeros_like(loss_acc)
            dbuf[tile:tile + 8, :] = jnp.zeros((8, D_CONV), F32)

        oc, oa = oc_ref[...], oa_ref[...]
        x2 = x_ref[...] + _nn(oc, wo_ref[0:D_CONV, :]) + _nn(oa, wo_ref[D_CONV:D_MIX, :])
        r = _rstd(x2)
        xhat = x2 * r
        err = xhat * gf_ref[...] - t_ref[...]
        loss_acc[...] += jnp.sum(err * err, axis=0, keepdims=True) * (0.5 / D_MODEL)
        dy = err * (1.0 / D_MODEL)
        small_add(SMALL_NORM_FINAL, dy * xhat)
        dx2 = _rms_bwd(dy * gf_ref[...], xhat, r)
        dx2_ref[...] = dx2
        db = dx2.astype(BF16)
        do = _nt(db, wo_ref[0:D_CONV, :])
        doa_ref[...] = _nt(db, wo_ref[D_CONV:D_MIX, :])
        gwo_ref[0:D_CONV, :] += _tn(oc, db)
        gwo_ref[D_CONV:D_MIX, :] += _tn(oa, db)

        is_first_tile = step == n_tiles - 1
        zbuf[0:8, :] = jnp.where(is_first_tile, 0.0, hcc_ref[...] * hcu_ref[...])
        cb, cc, cu, z, z1, z2, conv = _conv_core(pc_ref, zbuf, cw_ref)
        silu, dsilu = _silu_and_grad(pc_ref[:, 3 * D_CONV:4 * D_CONV])
        yc = cb * conv
        rc = _rstd(yc)
        chat = yc * rc
        dn = do * silu
        dpc_ref[:, 3 * D_CONV:4 * D_CONV] = (do * (chat * gn_ref[...]) * dsilu).astype(BF16)
        small_add(SMALL_NORM_CONV, dn * chat)
        dyc = _rms_bwd(dn * gn_ref[...], chat, rc)
        dpc_ref[:, 0:D_CONV] = (dyc * conv).astype(BF16)
        dconv = dyc * cb
        small_add(SMALL_CONV_W, dconv * z2)
        small_add(SMALL_CONV_W + 1, dconv * z1)
        small_add(SMALL_CONV_W + 2, dconv * z)
        dbuf[0:tile, :] = dconv
        dz = cw_ref[2:3, :] * dconv + cw_ref[1:2, :] * dbuf[1:tile + 1, :] + cw_ref[0:1, :] * dbuf[2:tile + 2, :]
        dpc_ref[:, D_CONV:2 * D_CONV] = (dz * cu).astype(BF16)
        dpc_ref[:, 2 * D_CONV:3 * D_CONV] = (dz * cc).astype(BF16)
        dbuf[tile:tile + 8, :] = dbuf[0:8, :]

        @pl.when(step == n_tiles - 1)
        def _():
            lane = lax.broadcasted_iota(jnp.int32, (1, D_MODEL), 1)
            small_ref[SMALL_MISC:SMALL_MISC + 1, :] = jnp.where(lane == SMALL_LOSS_LANE, jnp.sum(loss_acc[...]), 0.0)

    def rev(i):
        return n_tiles - 1 - i

    def row(width):
        return pl.BlockSpec((tile, width), lambda i: (rev(i), 0))

    def halo(col_block):
        return pl.BlockSpec((8, D_CONV), lambda i: (jnp.maximum(rev(i) * (tile // 8) - 1, 0), col_block))

    def const(shape):
        return pl.BlockSpec(shape, lambda i: (0, 0))

    return pl.pallas_call(
        body, name="out_proj_loss", grid=(n_tiles,),
        out_shape=(jax.ShapeDtypeStruct((seq, D_MODEL), F32), jax.ShapeDtypeStruct((seq, D_ATTN), F32),
                   jax.ShapeDtypeStruct((D_MIX, D_MODEL), F32), jax.ShapeDtypeStruct((seq, D_PC), BF16),
                   jax.ShapeDtypeStruct((SMALL_ROWS, D_MODEL), F32)),
        in_specs=[row(D_MODEL), row(D_CONV), row(D_ATTN), _resident(wo.shape), _resident((1, D_MODEL)), row(D_MODEL),
                  row(D_PC), halo(1), halo(2), _resident(conv_w.shape), _resident((1, D_CONV))],
        out_specs=(row(D_MODEL), row(D_ATTN), const((D_MIX, D_MODEL)), row(D_PC), const((SMALL_ROWS, D_MODEL))),
        scratch_shapes=[pltpu.VMEM((1, D_MODEL), F32), pltpu.VMEM((tile + 8, D_CONV), F32),
                        pltpu.VMEM((tile + 8, D_CONV), F32)],
        compiler_params=_params(62, ("arbitrary",)),
    )(x, oc, oa, wo, norm_final, target, pc, pc, pc, conv_w, norm_conv_out)


def _attn_bwd(q, kv, ga, ya, doa, probs, sink_probs, norm_attn_out, gwo, small):
    seq = q.shape[0]
    n_blocks = seq // BLK
    stage_steps = (0, n_blocks // 4, n_blocks // 2, (3 * n_blocks) // 4, n_blocks - 1)

    def body(q_ref, kvc_ref, kvp_ref, ga_ref, ya_ref, doa_ref, pr_ref, sp_ref, gn_ref, gwo_ref, small_ref,
             dqg_ref, small_out, gwo_sh, gna_ref, gs_ref, carry, dya_buf, *rs_scratch):
        step = pl.program_id(0)

        @pl.when(step == 0)
        def _():
            gna_ref[...] = jnp.zeros_like(gna_ref)
            gs_ref[...] = jnp.zeros_like(gs_ref)
            carry[...] = jnp.zeros_like(carry)

        ya = ya_ref[...]
        r = _rstd(ya)
        xhat = ya * r
        silu, dsilu = _silu_and_grad(ga_ref[...])
        do = doa_ref[...]
        dn = do * silu
        dqg_ref[:, D_ATTN:2 * D_ATTN] = (do * (xhat * gn_ref[...]) * dsilu).astype(BF16)
        gna_ref[...] += jnp.sum(dn * xhat, axis=0, keepdims=True)
        dya_buf[...] = _rms_bwd(dn * gn_ref[...], xhat, r).astype(BF16)

        use_cur = _band_geometry(n_blocks - 1 - step)[0]
        lane = lax.broadcasted_iota(jnp.int32, (BLK, 128), 1)

        def fold(bd):
            return (jnp.where(lane < 64, bd[0:BLK], 0.0) + jnp.where(lane >= 64, bd[BLK:2 * BLK], 0.0),
                    jnp.where(lane < 64, bd[2 * BLK:3 * BLK], 0.0) + jnp.where(lane >= 64, bd[3 * BLK:4 * BLK], 0.0))

        pairs = range(N_HEADS // 2)
        qps, kbd, vbd = _attn_operands(q_ref, kvc_ref, kvp_ref)
        probs = [pr_ref[:, h * BLK:(h + 1) * BLK].astype(F32) for h in range(N_HEADS)]
        dyps = [dya_buf[:, j * 128:(j + 1) * 128] for j in pairs]
        dps = []
        for j in pairs:
            dps += _merge(_nt(dyps[j], vbd[j // 4]), use_cur)
        deltas = [jnp.sum(p * dp, axis=-1, keepdims=True) for p, dp in zip(probs, dps)]
        dss = [p * (dp - delta) for p, dp, delta in zip(probs, dps, deltas)]
        delta_lanes = sum(jnp.where(lane == h, deltas[h], 0.0) for h in range(N_HEADS))
        gs_ref[...] -= jnp.sum(sp_ref[...] * delta_lanes, axis=0, keepdims=True)
        ds4s = [_split(dss[2 * j], dss[2 * j + 1], use_cur).astype(BF16) for j in pairs]
        p4s = [_split(probs[2 * j], probs[2 * j + 1], use_cur).astype(BF16) for j in pairs]
        dqg_ref[:, 0:D_ATTN] = jnp.concatenate([_nn(ds4s[j], kbd[j // 4]) * SCALE for j in pairs], axis=1).astype(BF16)
        sums = []
        for group in range(N_HEADS // HEADS_PER_KV):
            acc = [jnp.zeros((BLK, 128), F32) for _ in range(4)]
            for j in range(group * 4, group * 4 + 4):
                for slot, part in enumerate(fold(_tn(ds4s[j], qps[j])) + fold(_tn(p4s[j], dyps[j]))):
                    acc[slot] = acc[slot] + part
            sums.append([a + pltpu.roll(a, 64, 1) for a in acc])
        dk_cur, dk_prev, dv_cur, dv_prev = (jnp.where(lane < 64, a, b) for a, b in zip(sums[0], sums[1]))
        dqg_ref[:, 2 * D_ATTN:2 * D_ATTN + 2 * D_KV] = (jnp.concatenate([dk_cur, dv_cur], axis=1) + carry[...]).astype(BF16)
        carry[...] = jnp.concatenate([dk_prev, dv_prev], axis=1)

        @pl.when(step == n_blocks - 1)
        def _():
            small_out[...] = small_ref[...]
            small_out[SMALL_NORM_ATTN:SMALL_NORM_ATTN + 1, :] = gna_ref[...]
            small_out[SMALL_MISC:SMALL_MISC + 1, 0:128] = small_ref[SMALL_MISC:SMALL_MISC + 1, 0:128] + gs_ref[...]

        for at, stage in zip(stage_steps, _rs_wout_stages(gwo_ref, gwo_sh, *rs_scratch)):
            pl.when(step == at)(stage)

    row = pl.BlockSpec((BLK, D_ATTN), lambda i: (n_blocks - 1 - i, 0))
    kv_cur, kv_prev = _kv_specs(n_blocks, reverse=True)
    any_spec = pl.BlockSpec(memory_space=pl.ANY)
    return pl.pallas_call(
        body, name="attn_bwd", grid=(n_blocks,),
        out_shape=(jax.ShapeDtypeStruct((seq, D_QG), BF16), jax.ShapeDtypeStruct(small.shape, F32),
                   jax.ShapeDtypeStruct((gwo.shape[0] // N_CHIPS, gwo.shape[1]), F32)),
        in_specs=[row, kv_cur, kv_prev, row, row, row,
                  pl.BlockSpec((BLK, N_HEADS * BLK), lambda i: (n_blocks - 1 - i, 0)),
                  pl.BlockSpec((BLK, 128), lambda i: (n_blocks - 1 - i, 0)), _resident((1, D_ATTN)),
                  any_spec, _resident(small.shape)],
        out_specs=(pl.BlockSpec((BLK, D_QG), lambda i: (n_blocks - 1 - i, 0)),
                   pl.BlockSpec(small.shape, lambda i: (0, 0)), any_spec),
        scratch_shapes=[pltpu.VMEM((1, D_ATTN), F32), pltpu.VMEM((1, 128), F32),
                        pltpu.VMEM((BLK, 2 * D_KV), F32), pltpu.VMEM((BLK, D_ATTN), BF16)] + _rs_wout_scratch(gwo.shape),
        compiler_params=_params(44, ("arbitrary",)),
    )(q, kv, kv, ga, ya, doa, probs, sink_probs, norm_attn_out, gwo, small)


GBLK = 256
GSUB = 64
PAIR_RING = 4
LAG_PAIR, LAG_HOP1, LAG_HOP2 = 1, 7, 14


def _bwd_in(dpc, dqg, h, wt, x, norm_in, dx2, small):
    seq = x.shape[0]
    n_blk = D_IN_PROJ // GBLK
    per_chip = n_blk // N_CHIPS
    n_slots = (n_blk + 1) // 2
    n_sub = GBLK // GSUB
    chip_rows = D_IN_PROJ // N_CHIPS
    tile = TOK_TILE
    n_tiles = seq // tile
    n_steps = n_blk + max(n_tiles, LAG_HOP2)
    chunk = min(seq, 512)
    blk_q, blk_kv, blk_ga = ROW_Q // GBLK, ROW_KV // GBLK, ROW_GA // GBLK

    def block_of(i):
        k = i % N_CHIPS
        robin = per_chip * ((k % 2) * 2 + k // 2) + i // N_CHIPS
        if isinstance(i, int):
            return robin if i < per_chip * N_CHIPS else i
        return jnp.where(i < per_chip * N_CHIPS, robin, i)

    def owner_of(i):
        return (i // N_CHIPS) % 2

    def slot_of(i):
        return (i // (2 * N_CHIPS)) * N_CHIPS + i % N_CHIPS

    def body(dpc_ref, dqg_ref, wt_ref, h_ref, x_ref, g_ref, dx2_ref, small_ref, gx_ref, small_sum, gwt_sh,
             dh_acc, gni, keep, pbuf, xbuf, land, land2, small_land,
             pair_send, pair_recv, h1_send, h1_recv, h2_send, h2_recv, sw_send, sw_recv, sm_send, sm_recv, out_sem):
        step = pl.program_id(0)
        x_i, y_i, c = lax.axis_index("x"), lax.axis_index("y"), lax.axis_index("c")
        me = 4 * x_i + 2 * y_i + c
        j = 2 * x_i + y_i
        pa = (_xor(x_i, 1 - c), _xor(y_i, c), c)
        pb = (_xor(x_i, c), _xor(y_i, 1 - c), c)
        sib = (x_i, y_i, 1 - c)
        ja = 2 * pa[0] + pa[1]
        jb = 2 * pb[0] + pb[1]
        jd = 3 - j

        def remote(src, dst, send, recv, to):
            return pltpu.make_async_remote_copy(src_ref=src, dst_ref=dst, send_sem=send, recv_sem=recv,
                                                device_id=to, device_id_type=MESH)

        def piece(ref, slot, u, n):
            return ref.at[slot, pl.ds(u * GSUB, n * GSUB), :]

        def chip_rows_at(ref, local, n):
            return ref.at[pl.ds(pl.multiple_of(local, GSUB), n * GSUB), :]

        def pair_copy(i):
            slot = slot_of(i)
            return remote(pbuf.at[i % PAIR_RING], land.at[slot], pair_send.at[slot], pair_recv.at[slot], sib)

        def h1_copy(slot, u, n):
            k = slot * n_sub + u
            return remote(piece(xbuf, slot, u, n), piece(xbuf, slot, u, n), h1_send.at[k], h1_recv.at[k], pa)

        def h2_copy(slot, u, n, local):
            k = slot * n_sub + u
            return remote(piece(xbuf, slot, u, n), chip_rows_at(land2, local, n), h2_send.at[k], h2_recv.at[k], pb)

        def sw_copy(slot, u, n, local):
            k = slot * n_sub + u
            return remote(piece(keep, slot, u, n), chip_rows_at(gwt_sh, local, n), sw_send.at[k], sw_recv.at[k], sib)

        def owned(i):
            return (i >= 0) & (i < n_blk) & (owner_of(i) == c)

        def chip_of(blk, u):
            row = blk * GBLK + u * GSUB
            chip = row // chip_rows
            return chip, row - chip * chip_rows

        def pieces(blk):
            first, local = chip_of(blk, 0)
            whole = first == chip_of(blk, n_sub - 1)[0]
            if isinstance(blk, int):
                return [(True, 0, n_sub, first, local)] if whole else [(True, u, 1) + chip_of(blk, u) for u in range(n_sub)]
            return [(whole, 0, n_sub, first, local)] + [(jnp.logical_not(whole), u, 1) + chip_of(blk, u) for u in range(n_sub)]

        @pl.when(step == 0)
        def _():
            dh_acc[...] = jnp.zeros_like(dh_acc)
            gni[...] = jnp.zeros_like(gni)

        @pl.when(step < n_blk)
        def _():
            from_pc = block_of(step) < blk_q
            block = _tn(jnp.where(from_pc, dpc_ref[...], dqg_ref[...]), h_ref[...])
            for t in range(0, seq, chunk):
                d = jnp.where(from_pc, dpc_ref[t:t + chunk, :], dqg_ref[t:t + chunk, :])
                dh_acc[t:t + chunk, :] += _nn(d, wt_ref[...])

            @pl.when(owner_of(step) == c)
            def _():
                keep[slot_of(step)] = block

            @pl.when(owner_of(step) != c)
            def _():
                @pl.when(step >= 2 * PAIR_RING)
                def _():
                    pair_copy(step - 2 * PAIR_RING).wait_send()
                pbuf[step % PAIR_RING] = block.astype(BF16)
                pair_copy(step).start()

        i1 = step - LAG_PAIR

        @pl.when(owned(i1))
        def _():
            slot = slot_of(i1)
            pair_copy(i1).wait_recv()
            _accumulate(keep.at[slot], land.at[slot])
            for cond, u, n, chip, _ in pieces(block_of(i1)):
                @pl.when(cond & ((chip == ja) | (chip == jd)))
                def _(u=u, n=n):
                    _cast_rows(piece(keep, slot, u, n), piece(xbuf, slot, u, n))
                    h1_copy(slot, u, n).start()

        i2 = step - LAG_HOP1

        @pl.when(owned(i2))
        def _():
            slot = slot_of(i2)
            for cond, u, n, chip, local in pieces(block_of(i2)):
                @pl.when(cond & ((chip == j) | (chip == jb)))
                def _(u=u, n=n, chip=chip, local=local):
                    h1_copy(slot, u, n).wait_recv()
                    _accumulate(piece(keep, slot, u, n), piece(xbuf, slot, u, n))

                    @pl.when(chip == jb)
                    def _():
                        _cast_rows(piece(keep, slot, u, n), piece(xbuf, slot, u, n))
                        h2_copy(slot, u, n, local).start()

                @pl.when(cond & ((chip == ja) | (chip == jd)))
                def _(u=u, n=n):
                    h1_copy(slot, u, n).wait_send()

        i3 = step - LAG_HOP2

        @pl.when(owned(i3))
        def _():
            slot = slot_of(i3)
            for cond, u, n, chip, local in pieces(block_of(i3)):
                @pl.when(cond & (chip == j))
                def _(u=u, n=n, local=local):
                    h2_copy(slot, u, n, local).wait_recv()
                    _accumulate(piece(keep, slot, u, n), chip_rows_at(land2, local, n))
                    mine = pltpu.make_async_copy(piece(keep, slot, u, n), chip_rows_at(gwt_sh, local, n), out_sem.at[0])
                    mine.start()
                    sw_copy(slot, u, n, local).start()
                    mine.wait()

                @pl.when(cond & (chip == jb))
                def _(u=u, n=n, local=local):
                    h2_copy(slot, u, n, local).wait_send()

        e = step - n_blk

        @pl.when((e >= 0) & (e < n_tiles))
        def _():
            dh = dh_acc[pl.ds(pl.multiple_of(e * tile, tile), tile), :]
            xv = x_ref[...]
            r = _rstd(xv)
            xhat = xv * r
            gni[...] += jnp.sum(dh * xhat, axis=0, keepdims=True)
            gx_ref[...] = _rms_bwd(dh * g_ref[...], xhat, r) + dx2_ref[...]

        @pl.when(step == n_steps - 1)
        def _():
            small_land[me] = small_ref[...]
            small_land[me, SMALL_NORM_IN:SMALL_NORM_IN + 1, :] = gni[...]
            others = [(dx, dy, dc) for dx in (0, 1) for dy in (0, 1) for dc in (0, 1)][1:]
            sends = [remote(small_land.at[me], small_land.at[me], sm_send.at[k], sm_recv.at[k],
                            (_xor(x_i, dx), _xor(y_i, dy), _xor(c, dc))) for k, (dx, dy, dc) in enumerate(others)]
            for cp in sends:
                cp.start()
            for i in range(n_blk):
                if i + 2 * PAIR_RING >= n_blk:
                    @pl.when(owner_of(i) != c)
                    def _(i=i):
                        pair_copy(i).wait_send()
                for _, u, n, chip, local in pieces(block_of(i)):
                    @pl.when((j == chip) & (c == owner_of(i)))
                    def _(i=i, u=u, n=n, local=local):
                        sw_copy(slot_of(i), u, n, local).wait_send()

                    @pl.when((j == chip) & (c != owner_of(i)))
                    def _(i=i, u=u, n=n, local=local):
                        sw_copy(slot_of(i), u, n, local).wait_recv()
            for cp in sends:
                cp.wait_recv()
            total = small_land[0]
            for dev in range(1, 8):
                total = total + small_land[dev]
            small_sum[...] = total
            for cp in sends:
                cp.wait_send()

    def blk_at(i):
        return block_of(jnp.clip(i, 0, n_blk - 1))

    last_pc_step = max(i for i in range(n_blk) if block_of(i) < blk_q)

    def next_block(i, in_pc):
        i = jnp.clip(i, 0, n_blk - 1)
        step = jnp.full_like(i, last_pc_step if in_pc else n_blk - 1)
        for ahead in reversed(range(N_CHIPS)):
            cand = jnp.minimum(i + ahead, n_blk - 1)
            step = jnp.where((block_of(cand) < blk_q) == in_pc, cand, step)
        return block_of(step)

    def dqg_block(i):
        b = next_block(i, False)
        q_blk = jnp.clip(b - blk_q, 0, blk_kv - blk_q - 1)
        ga_blk = (D_ATTN // GBLK) + jnp.clip(b - blk_ga, 0, n_blk - blk_ga - 1)
        return jnp.where(b < blk_kv, q_blk, jnp.where(b == blk_kv, 2 * D_ATTN // GBLK, ga_blk))

    def tok(i):
        return (jnp.clip(i - n_blk, 0, n_tiles - 1), 0)

    n_piece = n_slots * n_sub
    dma = pltpu.SemaphoreType.DMA
    return pl.pallas_call(
        body, name="bwd_in", grid=(n_steps,),
        out_shape=(jax.ShapeDtypeStruct((seq, D_MODEL), F32), jax.ShapeDtypeStruct(small.shape, F32),
                   jax.ShapeDtypeStruct((chip_rows, D_MODEL), F32)),
        in_specs=[pl.BlockSpec((seq, GBLK), lambda i: (0, next_block(i, True))),
                  pl.BlockSpec((seq, GBLK), lambda i: (0, dqg_block(i))),
                  pl.BlockSpec((GBLK, D_MODEL), lambda i: (blk_at(i), 0)),
                  _resident(h.shape),
                  pl.BlockSpec((tile, D_MODEL), tok), _resident((1, D_MODEL)), pl.BlockSpec((tile, D_MODEL), tok),
                  _resident(small.shape)],
        out_specs=(pl.BlockSpec((tile, D_MODEL), tok), pl.BlockSpec(small.shape, lambda i: (0, 0)),
                   pl.BlockSpec(memory_space=pl.ANY)),
        scratch_shapes=[pltpu.VMEM((seq, D_MODEL), F32), pltpu.VMEM((1, D_MODEL), F32),
                        pltpu.VMEM((n_slots, GBLK, D_MODEL), F32), pltpu.VMEM((PAIR_RING, GBLK, D_MODEL), BF16),
                        pltpu.VMEM((n_slots, GBLK, D_MODEL), BF16), pltpu.VMEM((n_slots, GBLK, D_MODEL), BF16),
                        pltpu.VMEM((chip_rows, D_MODEL), BF16), pltpu.VMEM((8,) + small.shape, F32),
                        dma((n_slots,)), dma((n_slots,)), dma((n_piece,)), dma((n_piece,)), dma((n_piece,)),
                        dma((n_piece,)), dma((n_piece,)), dma((n_piece,)), dma((7,)), dma((7,)), dma((1,))],
        compiler_params=_params(62, ("arbitrary",)),
    )(dpc, dqg, wt, h, x, norm_in, dx2, small)


def _accumulate(dst_ref, src_ref, rows=16):
    def step(i, carry):
        sl = pl.ds(pl.multiple_of(i * rows, rows), rows)
        dst_ref[sl, :] = dst_ref[sl, :] + src_ref[sl, :].astype(F32)
        return carry
    lax.fori_loop(0, dst_ref.shape[0] // rows, step, 0)


def _rs_wout_scratch(gwo_shape):
    o_half, width = gwo_shape[0] // N_CHIPS // 2, gwo_shape[1]
    return [pltpu.VMEM((4, o_half, width), F32), pltpu.VMEM((4, o_half, width), BF16),
            pltpu.VMEM((4, o_half, width), BF16), pltpu.VMEM((2, o_half, width), BF16),
            pltpu.VMEM((o_half, width), BF16),
            pltpu.SemaphoreType.DMA((8,)), pltpu.SemaphoreType.DMA((8,)), pltpu.SemaphoreType.DMA((4,))]


def _rs_wout_stages(gwo_ref, gwo_sh, acc_o, sb_o, r1o, r2o, r3o, send_sems, recv_sems, local_sems):
    o_rows = gwo_ref.shape[0] // N_CHIPS
    o_half = o_rows // 2
    x, y, c = lax.axis_index("x"), lax.axis_index("y"), lax.axis_index("c")
    j = 2 * x + y
    pa = (_xor(x, 1 - c), _xor(y, c), c)
    pb = (_xor(x, c), _xor(y, 1 - c), c)
    sib = (x, y, 1 - c)
    ja = 2 * pa[0] + pa[1]
    jb = 2 * pb[0] + pb[1]
    jd = 3 - j
    order = (ja, jd, jb, j)
    sib_order = (jb, jd, ja, j)

    def rcopy(k, src, dst, to):
        return pltpu.make_async_remote_copy(src_ref=src, dst_ref=dst, send_sem=send_sems.at[k],
                                            recv_sem=recv_sems.at[k], device_id=to, device_id_type=MESH)

    def o_rows_of(chip, half):
        return gwo_ref.at[pl.ds(pl.multiple_of(chip * o_rows + half * o_half, 8), o_half), :]

    def load_all(chips, half):
        cps = [pltpu.make_async_copy(o_rows_of(chip, half), acc_o.at[s], local_sems.at[s]) for s, chip in enumerate(chips)]
        for cp in cps:
            cp.start()
        return cps

    def pair_send(s):
        return rcopy(s, sb_o.at[s], r1o.at[s], sib)

    def out_half(half):
        return gwo_sh.at[pl.ds(pl.multiple_of(half * o_half, 8), o_half), :]

    hop1 = [rcopy(4 + s, sb_o.at[s], r2o.at[s], pa) for s in range(2)]
    hop2 = rcopy(6, sb_o.at[2], r3o, pb)
    swap = rcopy(7, acc_o.at[3], out_half(c), sib)
    mine = pltpu.make_async_copy(acc_o.at[3], out_half(c), local_sems.at[0])

    def resend(s, copy):
        pair_send(s).wait_send()
        _cast_rows(acc_o.at[s], sb_o.at[s])
        copy.start()

    def stage_pair():
        for s, cp in enumerate(load_all(sib_order, 1 - c)):
            cp.wait()
            _cast_rows(acc_o.at[s], sb_o.at[s])
            pair_send(s).start()

    def stage_hop1():
        for s, cp in enumerate(load_all(order, c)):
            cp.wait()
            pair_send(s).wait_recv()
            _accumulate(acc_o.at[s], r1o.at[s])
            if s < 2:
                resend(s, hop1[s])

    def stage_hop2():
        hop1[1].wait_recv()
        _accumulate(acc_o.at[2], r2o.at[1])
        resend(2, hop2)
        hop1[0].wait_recv()
        _accumulate(acc_o.at[3], r2o.at[0])

    def stage_final():
        hop2.wait_recv()
        _accumulate(acc_o.at[3], r3o)
        mine.start()
        swap.start()

    def stage_drain():
        rcopy(7, acc_o.at[3], out_half(1 - c), sib).wait_recv()
        for cp in [pair_send(3)] + hop1 + [hop2, swap]:
            cp.wait_send()
        mine.wait()

    return stage_pair, stage_hop1, stage_hop2, stage_final, stage_drain


def _adamw(name, w, g, m, v, rows):
    def body(w_ref, g_ref, m_ref, v_ref, go_ref, d_ref, nm_ref, nv_ref):
        _adamw_update(w_ref, g_ref, m_ref, v_ref, go_ref, d_ref, nm_ref, nv_ref)

    spec = pl.BlockSpec((rows, w.shape[1]), lambda i: (i, 0))
    shape = jax.ShapeDtypeStruct(w.shape, F32)
    return pl.pallas_call(
        body, name="adamw_" + name, grid=(w.shape[0] // rows,),
        out_shape=(shape,) * 4, in_specs=[spec] * 4, out_specs=(spec,) * 4,
        compiler_params=_params(32, ("arbitrary",)),
    )(w, g, m, v)


SC_TILES = 32
SC_LANES = 16
SC_ROWS = 8
SC_COLS = 512


def _adamw_sparsecore(name, w, g, m, v):
    rows, cols = w.shape
    band = rows // SC_TILES
    col_blocks = cols // SC_COLS

    def body(w_hbm, g_hbm, m_hbm, v_hbm, d_hbm, nm_hbm, nv_hbm, wb, gb, mb, vb):
        tile = lax.axis_index("subcore") * 2 + lax.axis_index("core")

        @pl.loop(0, (band // SC_ROWS) * col_blocks)
        def _(blk):
            mine = (pl.ds(tile * band + (blk // col_blocks) * SC_ROWS, SC_ROWS),
                    pl.ds((blk % col_blocks) * SC_COLS, SC_COLS))
            for src, buf in ((w_hbm, wb), (g_hbm, gb), (m_hbm, mb), (v_hbm, vb)):
                pltpu.sync_copy(src.at[mine], buf)

            for r in range(SC_ROWS):
                @pl.loop(0, SC_COLS, step=SC_LANES)
                def _(i):
                    at = (r, pl.ds(i, SC_LANES))
                    gv = gb[at]
                    nm = ADAM_B1 * mb[at] + (1.0 - ADAM_B1) * gv
                    nv = ADAM_B2 * vb[at] + (1.0 - ADAM_B2) * (gv * gv)
                    m_hat = nm / (1.0 - ADAM_B1 ** ADAM_STEP)
                    v_hat = nv / (1.0 - ADAM_B2 ** ADAM_STEP)
                    wb[at] = -ADAM_LR * (m_hat / (jnp.sqrt(v_hat) + ADAM_EPS) + ADAM_WD * wb[at])
                    mb[at] = nm
                    vb[at] = nv

            for buf, dst in ((wb, d_hbm), (mb, nm_hbm), (vb, nv_hbm)):
                pltpu.sync_copy(buf, dst.at[mine])

    return (g,) + tuple(pl.kernel(
        body, name="adamw_sc_" + name,
        out_type=(jax.ShapeDtypeStruct(w.shape, F32),) * 3,
        mesh=plsc.VectorSubcoreMesh(core_axis_name="core", subcore_axis_name="subcore"),
        scratch_types=[pltpu.VMEM((SC_ROWS, SC_COLS), F32)] * 4,
    )(w, g, m, v))


def _adamw_update(w_ref, g_ref, m_ref, v_ref, go_ref, d_ref, nm_ref, nv_ref):
    gv = g_ref[...]
    go_ref[...] = gv
    nm = ADAM_B1 * m_ref[...] + (1.0 - ADAM_B1) * gv
    nv = ADAM_B2 * v_ref[...] + (1.0 - ADAM_B2) * (gv * gv)
    m_hat = nm / (1.0 - ADAM_B1 ** ADAM_STEP)
    v_hat = nv / (1.0 - ADAM_B2 ** ADAM_STEP)
    d_ref[...] = -ADAM_LR * (m_hat / (jnp.sqrt(v_hat) + ADAM_EPS) + ADAM_WD * w_ref[...])
    nm_ref[...] = nm
    nv_ref[...] = nv


def _adamw_small(weights, grads, ms, vs):
    n = len(weights)

    def body(*refs):
        ins, outs = refs[:4 * n], refs[4 * n:]
        for k in range(n):
            _adamw_update(*ins[4 * k:4 * k + 4], *outs[4 * k:4 * k + 4])

    flat = [a for group in zip(weights, grads, ms, vs) for a in group]
    vmem = pl.BlockSpec(memory_space=pltpu.VMEM)
    out = pl.pallas_call(
        body, name="adamw_small",
        out_shape=tuple(jax.ShapeDtypeStruct(w.shape, F32) for w in weights for _ in range(4)),
        in_specs=[vmem] * (4 * n), out_specs=(vmem,) * (4 * n),
    )(*flat)
    return [tuple(out[4 * k:4 * k + 4]) for k in range(n)]


def kernel(x, norm_in, w_in, conv_w, attn_sinks, norm_conv_out, norm_attn_out, w_out, norm_final, loss_target, m_norm_in, m_w_in, m_conv_w, m_attn_sinks, m_norm_conv_out, m_norm_attn_out, m_w_out, m_norm_final, v_norm_in, v_w_in, v_conv_w, v_attn_sinks, v_norm_conv_out, v_norm_attn_out, v_w_out, v_norm_final):
    chip = 2 * lax.axis_index("x") + lax.axis_index("y")
    xs, target = x[0], loss_target[0]
    norm_final2 = norm_final.reshape(1, D_MODEL)
    w_in_t, m_w_in_t, v_w_in_t = w_in[0].T, m_w_in[0].T, v_w_in[0].T

    h, pc, q, kv, ga, oc, ya, oa, probs, sink_probs, wt, cw, wo = _fwd_in(
        xs, norm_in, w_in_t, w_out[0], conv_w[0], norm_conv_out, attn_sinks, norm_attn_out)
    dx2, doa, gwo, dpc, small = _out_proj_loss(xs, oc, oa, wo, norm_final2, target, pc, cw, norm_conv_out)
    dqg, small, gwo_sh = _attn_bwd(q, kv, ga, ya, doa, probs, sink_probs, norm_attn_out, gwo, small)
    grad_x, small_sum, gwt_sh = _bwd_in(dpc, dqg, h, wt, xs, norm_in, dx2, small)

    loss = small_sum[SMALL_MISC, SMALL_LOSS_LANE]
    g_norm_in, g_norm_conv, g_norm_attn = (small_sum[r:r + 1] for r in (SMALL_NORM_IN, SMALL_NORM_CONV, SMALL_NORM_ATTN))
    g_norm_final = small_sum[SMALL_NORM_FINAL]
    g_conv_w = lax.dynamic_slice(small_sum[SMALL_CONV_W:SMALL_CONV_W + 3], (0, chip * (D_CONV // N_CHIPS)),
                                 (3, D_CONV // N_CHIPS))[None]
    g_sinks = small_sum[SMALL_MISC:SMALL_MISC + 1, 0:N_HEADS]

    def two_d(a):
        return a.reshape(-1, a.shape[-1])

    def from_hbm(*arrays):
        return tuple(pltpu.with_memory_space_constraint(a, pltpu.HBM) for a in arrays)

    up_w_in = tuple(o.T[None] for o in _adamw("w_in", *from_hbm(w_in_t, gwt_sh, m_w_in_t, v_w_in_t), 200))
    up_w_out = tuple(o[None] for o in _adamw_sparsecore("w_out", w_out[0], gwo_sh, m_w_out[0], v_w_out[0]))
    small_w = (norm_in, conv_w, attn_sinks, norm_conv_out, norm_attn_out, norm_final)
    small_g = (g_norm_in, g_conv_w, g_sinks, g_norm_conv, g_norm_attn, g_norm_final)
    small_m = (m_norm_in, m_conv_w, m_attn_sinks, m_norm_conv_out, m_norm_attn_out, m_norm_final)
    small_v = (v_norm_in, v_conv_w, v_attn_sinks, v_norm_conv_out, v_norm_attn_out, v_norm_final)
    up_small = _adamw_small(*(tuple(two_d(a) for a in group) for group in (small_w, small_g, small_m, small_v)))
    up_small = [tuple(o.reshape(w.shape) for o in up) for up, w in zip(up_small, small_w)]
    up_norm_in, up_conv_w, up_sinks, up_norm_conv, up_norm_attn, up_norm_final = up_small
    updates = (up_norm_in, up_w_in, up_conv_w, up_sinks, up_norm_conv, up_norm_attn, up_w_out, up_norm_final)
    grads_out, deltas, new_m, new_v = zip(*updates)
    return (loss, grad_x[None], *grads_out, *deltas, *new_m, *new_v)
```

```python
import jax
import jax.numpy as jnp
from jax import lax
from jax.experimental import pallas as pl
from jax.experimental.pallas import tpu as pltpu

F32 = jnp.float32
BF16 = jnp.bfloat16
MESH = pl.DeviceIdType.MESH

D_MODEL = 1024
D_CONV = 1024
D_ATTN = 1024
D_KV = 128
D_QG = 2 * D_ATTN + 2 * D_KV
D_MIX = D_CONV + D_ATTN
D_PC = 4 * D_CONV
D_IN_PROJ = D_PC + 2 * D_ATTN + 2 * D_KV
ROW_Q = D_PC
ROW_KV = ROW_Q + D_ATTN
ROW_GA = ROW_KV + 2 * D_KV
N_HEADS = 16
HEAD_DIM = 64
HEADS_PER_KV = 8
BLK = 128
N_CHIPS = 4
RMS_EPS = 1e-5
SCALE = HEAD_DIM ** -0.5
SLOPES = tuple(2.0 ** (-8.0 * (h + 1) / N_HEADS) for h in range(N_HEADS))

ADAM_LR, ADAM_B1, ADAM_B2, ADAM_EPS, ADAM_WD, ADAM_STEP = 0.001, 0.9, 0.999, 1e-08, 0.01, 10

SMALL_ROWS = 8
SMALL_NORM_IN, SMALL_NORM_CONV, SMALL_NORM_ATTN, SMALL_NORM_FINAL, SMALL_CONV_W, SMALL_MISC = 0, 1, 2, 3, 4, 7
SMALL_LOSS_LANE = N_HEADS

TOK_TILE = 256
PC_PIECE = 512
MIB = 1 << 20


def _params(vmem_mib, semantics=None):
    return pltpu.CompilerParams(dimension_semantics=semantics, vmem_limit_bytes=vmem_mib * MIB)


def _nn(a, b):
    return jnp.dot(a, b, preferred_element_type=F32)


def _nt(a, b):
    return lax.dot_general(a, b, (((1,), (1,)), ((), ())), preferred_element_type=F32)


def _tn(a, b):
    return lax.dot_general(a, b, (((0,), (0,)), ((), ())), preferred_element_type=F32)


def _rstd(v):
    return lax.rsqrt(jnp.mean(v * v, axis=-1, keepdims=True) + RMS_EPS)


def _rms_bwd(g, xhat, rstd):
    return rstd * (g - xhat * jnp.mean(g * xhat, axis=-1, keepdims=True))


def _silu_and_grad(g):
    s = jax.nn.sigmoid(g)
    return g * s, s * (1.0 + g * (1.0 - s))


def _resident(shape):
    return pl.BlockSpec(shape, lambda *_: (0,) * len(shape), pipeline_mode=pl.Buffered(1))


def _xor(a, b):
    return a + b - 2 * a * b


def _cast_rows(src_ref, dst_ref, rows=32):
    def step(i, carry):
        sl = pl.ds(pl.multiple_of(i * rows, rows), rows)
        dst_ref[sl, :] = src_ref[sl, :].astype(dst_ref.dtype)
        return carry
    lax.fori_loop(0, src_ref.shape[0] // rows, step, 0)


def _ag_scratch(shard_shape):
    rows, width = shard_shape
    return [pltpu.VMEM((rows, width), F32), pltpu.VMEM((rows, width), BF16), pltpu.VMEM((3, rows // 2, width), BF16),
            pltpu.SemaphoreType.DMA((6,)), pltpu.SemaphoreType.DMA((6,)), pltpu.SemaphoreType.DMA((4,))]


def _ag_stages(sh_ref, out, f32_buf, own, land, send_sems, recv_sems, local_sems):
    rows = sh_ref.shape[0]
    half = rows // 2
    x, y, c = lax.axis_index("x"), lax.axis_index("y"), lax.axis_index("c")
    j = 2 * x + y
    p1 = (_xor(x, c), _xor(y, 1 - c), c)
    p2 = (_xor(x, 1 - c), _xor(y, c), c)
    sib = (x, y, 1 - c)
    j1 = 2 * p1[0] + p1[1]
    j2 = 2 * p2[0] + p2[1]
    j3 = 3 - j

    def rows_of(chip, hf):
        return out.at[pl.ds(pl.multiple_of(chip * rows + hf * half, 16), half), :]

    def rcopy(k, src, dst, to):
        return pltpu.make_async_remote_copy(src_ref=src, dst_ref=dst, send_sem=send_sems.at[k],
                                            recv_sem=recv_sems.at[k], device_id=to, device_id_type=MESH)

    my_half = own.at[pl.ds(pl.multiple_of(c * half, 16), half), :]
    hop1 = rcopy(0, my_half, land.at[0], p1)
    hop2_own = rcopy(1, my_half, land.at[1], p2)
    hop2_fwd = rcopy(2, land.at[0], land.at[2], p2)
    swaps = [rcopy(3 + s, land.at[s], rows_of(chip, c), sib) for s, chip in enumerate((j1, j2, j3))]
    keeps = [pltpu.make_async_copy(land.at[s], rows_of(chip, c), local_sems.at[1 + s]) for s, chip in enumerate((j1, j2, j3))]
    load = pltpu.make_async_copy(sh_ref, f32_buf, local_sems.at[0])
    own_out = pltpu.make_async_copy(own, out.at[pl.ds(pl.multiple_of(j * rows, 16), rows), :], local_sems.at[0])

    def stage_send():
        load.start()
        load.wait()
        _cast_rows(f32_buf, own)
        own_out.start()
        hop1.start()

    def stage_forward():
        hop1.wait_recv()
        hop2_own.start()
        hop2_fwd.start()
        swaps[0].start()
        keeps[0].start()

    def stage_publish():
        hop2_own.wait_recv()
        swaps[1].start()
        keeps[1].start()
        hop2_fwd.wait_recv()
        swaps[2].start()
        keeps[2].start()

    def stage_drain():
        for s, chip in enumerate((j2, j1, j3)):
            rcopy(3 + s, my_half, rows_of(chip, 1 - c), sib).wait_recv()
        for cp in [hop1, hop2_own, hop2_fwd] + swaps:
            cp.wait_send()
        for cp in [own_out] + keeps:
            cp.wait()

    return stage_send, stage_forward, stage_publish, stage_drain


AG_CAST_ROWS = 400


def _gather_resident(sh_ref, out, f32_buf, send_sems, recv_sems, local_sems):
    rows = sh_ref.shape[0]
    half = rows // 2
    x, y, c = lax.axis_index("x"), lax.axis_index("y"), lax.axis_index("c")
    j = 2 * x + y
    p1 = (_xor(x, c), _xor(y, 1 - c), c)
    p2 = (_xor(x, 1 - c), _xor(y, c), c)
    sib = (x, y, 1 - c)
    j1 = 2 * p1[0] + p1[1]
    j2 = 2 * p2[0] + p2[1]
    j3 = 3 - j

    def rows_of(chip, hf):
        return out.at[pl.ds(pl.multiple_of(chip * rows + hf * half, 16), half), :]

    def send(k, chip, to):
        return pltpu.make_async_remote_copy(src_ref=rows_of(chip, c), dst_ref=rows_of(chip, c), send_sem=send_sems.at[k],
                                            recv_sem=recv_sems.at[k], device_id=to, device_id_type=MESH)

    per_half = half // AG_CAST_ROWS

    def cast_own(chunk):
        lo = pl.multiple_of(chunk * AG_CAST_ROWS, 16)
        load = pltpu.make_async_copy(sh_ref.at[pl.ds(lo, AG_CAST_ROWS), :], f32_buf, local_sems.at[0])
        load.start()
        load.wait()
        _cast_rows(f32_buf, out.at[pl.ds(pl.multiple_of(j * rows + lo, 16), AG_CAST_ROWS), :], rows=16)

    for k in range(per_half):
        cast_own(c * per_half + k)
    hop1 = send(0, j, p1)
    hop1.start()
    for k in range(per_half):
        cast_own((1 - c) * per_half + k)
    hop1.wait_recv()
    sends = [hop1, send(1, j, p2), send(2, j1, p2), send(3, j1, sib)]
    for cp in sends[1:]:
        cp.start()
    sends[1].wait_recv()
    sends.append(send(4, j2, sib))
    sends[-1].start()
    sends[2].wait_recv()
    sends.append(send(5, j3, sib))
    sends[-1].start()
    for k in (3, 4, 5):
        send(k, j, sib).wait_recv()
    for cp in sends:
        cp.wait_send()


def _fwd_in(x, norm_in, wt_sh, wo_sh, cw_sh, norm_conv_out, sinks, norm_attn_out):
    seq = x.shape[0]
    tile = TOK_TILE
    n_tiles = seq // tile
    stage_steps = (0, n_tiles // 4, (5 * n_tiles) // 8, n_tiles - 1)
    blocks = tile // BLK
    cw_cols = cw_sh.shape[-1]

    def body(x_ref, g_ref, wtsh_ref, wo_ref, cwsh_ref, gn_ref, sink_ref, gna_ref,
             h_ref, pc_ref, q_ref, kv_ref, ga_ref, oc_ref, ya_ref, oa_ref, pr_ref, sp_ref, wt_out, cw_ref, wo_out,
             zbuf, kv_last, wt_ref, f32_buf, cw_land, wt_send, wt_recv, wt_local, cw_send, cw_recv, cw_local, *ag_scratch):
        step = pl.program_id(0)
        to_hbm = pltpu.make_async_copy(wt_ref, wt_out, wt_local.at[0])

        @pl.when(step == 0)
        def _():
            zbuf[0:8, :] = jnp.zeros((8, D_CONV), F32)
            kv_last[...] = jnp.zeros_like(kv_last)
            x_i, y_i, c = lax.axis_index("x"), lax.axis_index("y"), lax.axis_index("c")
            j = 2 * x_i + y_i
            p1 = (_xor(x_i, c), _xor(y_i, 1 - c), c)
            p2 = (_xor(x_i, 1 - c), _xor(y_i, c), c)
            j1 = 2 * p1[0] + p1[1]

            def cw_copy(k, src, chip, to):
                return pltpu.make_async_remote_copy(src_ref=src, dst_ref=cw_land.at[chip], send_sem=cw_send.at[k],
                                                    recv_sem=cw_recv.at[k], device_id=to, device_id_type=MESH)

            mine = pltpu.make_async_copy(cwsh_ref, cw_land.at[j], cw_local.at[0])
            mine.start()
            first = cw_copy(0, cwsh_ref, j, p1)
            first.start()
            _gather_resident(wtsh_ref, wt_ref, f32_buf, wt_send, wt_recv, wt_local)
            to_hbm.start()
            first.wait_recv()
            second = [cw_copy(1, cwsh_ref, j, p2), cw_copy(2, cw_land.at[j1], j1, p2)]
            for cp in second:
                cp.start()
            for cp in second:
                cp.wait_recv()
            for cp in [first] + second:
                cp.wait_send()
            mine.wait()
            for chip in range(N_CHIPS):
                for tap in range(cw_sh.shape[0]):
                    cw_ref[tap:tap + 1, chip * cw_cols:(chip + 1) * cw_cols] = cw_land[chip, tap]

        stages = _ag_stages(wo_ref, wo_out, *ag_scratch)
        for at, stage in zip(stage_steps[:-1], stages[:-1]):
            pl.when(step == at)(stage)

        def attention(b):
            rows = pl.ds(b * BLK, BLK)
            kv_prev = kv_last if b == 0 else kv_ref.at[pl.ds((b - 1) * BLK, BLK), :]
            return _attn_forward(q_ref.at[rows, :], kv_ref.at[rows, :], kv_prev, ga_ref.at[rows, :], sink_ref, gna_ref,
                                 _band_geometry(step * blocks + b), ya_ref.at[rows, :], oa_ref.at[rows, :],
                                 pr_ref.at[rows, :], sp_ref.at[rows, :])

        xv = x_ref[...]
        h = (xv * _rstd(xv) * g_ref[...]).astype(BF16)
        h_ref[...] = h
        q_ref[...] = _nt(h, wt_ref[ROW_Q:ROW_KV, :])
        kv_ref[...] = _nt(h, wt_ref[ROW_KV:ROW_GA, :])
        ga_ref[...] = _nt(h, wt_ref[ROW_GA:D_IN_PROJ, :])
        attention_blocks = [attention(b) for b in range(blocks)]
        for lo in range(0, D_PC, PC_PIECE):
            pc_ref[:, lo:lo + PC_PIECE] = _nt(h, wt_ref[lo:lo + PC_PIECE, :])
            for stages_of_block in attention_blocks:
                next(stages_of_block, None)
        for stages_of_block in attention_blocks:
            for _ in stages_of_block:
                pass
        kv_last[...] = kv_ref[tile - BLK:tile, :]

        cb, _, _, _, _, _, conv = _conv_core(pc_ref, zbuf, cw_ref)
        yc = cb * conv
        silu, _ = _silu_and_grad(pc_ref[:, 3 * D_CONV:4 * D_CONV])
        oc_ref[...] = (yc * _rstd(yc) * gn_ref[...] * silu).astype(BF16)
        zbuf[0:8, :] = zbuf[tile:tile + 8, :]

        @pl.when(step == stage_steps[-1])
        def _():
            stages[-1]()
            to_hbm.wait()

    def row(width):
        return pl.BlockSpec((tile, width), lambda i: (i, 0))

    any_spec = pl.BlockSpec(memory_space=pl.ANY)
    dma = pltpu.SemaphoreType.DMA
    wt_shape = (N_CHIPS * wt_sh.shape[0], wt_sh.shape[1])
    cw_shape = (cw_sh.shape[0], N_CHIPS * cw_cols)
    return pl.pallas_call(
        body, name="fwd_in", grid=(n_tiles,),
        out_shape=(jax.ShapeDtypeStruct((seq, D_MODEL), BF16), jax.ShapeDtypeStruct((seq, D_PC), F32),
                   jax.ShapeDtypeStruct((seq, D_ATTN), F32), jax.ShapeDtypeStruct((seq, 2 * D_KV), F32),
                   jax.ShapeDtypeStruct((seq, D_ATTN), F32), jax.ShapeDtypeStruct((seq, D_CONV), BF16),
                   pltpu.HBM((seq, D_ATTN), F32), pltpu.HBM((seq, D_ATTN), BF16),
                   pltpu.HBM((seq, N_HEADS * BLK), BF16), pltpu.HBM((seq, 128), F32),
                   pltpu.HBM(wt_shape, BF16), jax.ShapeDtypeStruct(cw_shape, F32),
                   jax.ShapeDtypeStruct((N_CHIPS * wo_sh.shape[0], wo_sh.shape[1]), BF16)),
        in_specs=[row(D_MODEL), _resident((1, D_MODEL)), any_spec, any_spec, any_spec, _resident((1, D_CONV)),
                  pl.BlockSpec(memory_space=pltpu.SMEM), _resident((1, D_ATTN))],
        out_specs=(row(D_MODEL), row(D_PC), row(D_ATTN), row(2 * D_KV), row(D_ATTN), row(D_CONV), row(D_ATTN),
                   row(D_ATTN), row(N_HEADS * BLK), row(128), any_spec, pl.BlockSpec(cw_shape, lambda i: (0, 0)),
                   any_spec),
        scratch_shapes=[pltpu.VMEM((tile + 8, D_CONV), F32), pltpu.VMEM((BLK, 2 * D_KV), F32),
                        pltpu.VMEM(wt_shape, BF16), pltpu.VMEM((AG_CAST_ROWS, wt_sh.shape[1]), F32),
                        pltpu.VMEM((N_CHIPS,) + cw_sh.shape, F32),
                        dma((6,)), dma((6,)), dma((1,)), dma((3,)), dma((3,)), dma((1,))] + _ag_scratch(wo_sh.shape),
        compiler_params=_params(62, ("arbitrary",)),
    )(x, norm_in, wt_sh, wo_sh, cw_sh, norm_conv_out, sinks, norm_attn_out)


def _conv_core(pc_ref, zbuf, cw_ref):
    tile = pc_ref.shape[0]
    cb = pc_ref[:, 0:D_CONV]
    cc = pc_ref[:, D_CONV:2 * D_CONV]
    cu = pc_ref[:, 2 * D_CONV:3 * D_CONV]
    z = cc * cu
    zbuf[8:tile + 8, :] = z
    z1 = zbuf[7:tile + 7, :]
    z2 = zbuf[6:tile + 6, :]
    conv = cw_ref[0:1, :] * z2 + cw_ref[1:2, :] * z1 + cw_ref[2:3, :] * z
    return cb, cc, cu, z, z1, z2, conv


def _band_geometry(block_index):
    qi = lax.broadcasted_iota(jnp.int32, (BLK, BLK), 0)
    kp = lax.broadcasted_iota(jnp.int32, (BLK, BLK), 1)
    use_cur = kp <= qi
    dist = jnp.where(use_cur, qi - kp, qi - kp + BLK).astype(F32)
    valid = use_cur | (block_index > 0)
    return use_cur, dist, valid


def _block_diag(cur, prev, group):
    lane = lax.broadcasted_iota(jnp.int32, cur.shape, 1)

    def halves(t):
        other = pltpu.roll(t, 64, 1)
        lo, hi = (t, other) if group == 0 else (other, t)
        return jnp.where(lane < 64, lo, 0.0), jnp.where(lane >= 64, hi, 0.0)

    return jnp.concatenate(halves(cur) + halves(prev), axis=0).astype(BF16)


def _merge(s4, use_cur):
    return (jnp.where(use_cur, s4[:, 0:BLK], s4[:, 2 * BLK:3 * BLK]),
            jnp.where(use_cur, s4[:, BLK:2 * BLK], s4[:, 3 * BLK:4 * BLK]))


def _split(a, b, use_cur):
    return jnp.concatenate([jnp.where(use_cur, a, 0.0), jnp.where(use_cur, b, 0.0),
                            jnp.where(use_cur, 0.0, a), jnp.where(use_cur, 0.0, b)], axis=1)


def _softmax_head(s, head, sink, dist, valid):
    sc = jnp.where(valid, s - SLOPES[head] * dist, -jnp.inf)
    m = jnp.maximum(jnp.max(sc, axis=-1, keepdims=True), sink)
    p = jnp.exp(sc - m)
    es = jnp.exp(sink - m)
    inv = 1.0 / (jnp.sum(p, axis=-1, keepdims=True) + es)
    return p * inv, es * inv


def _attn_operands(q_ref, kvc_ref, kvp_ref):
    groups = range(N_HEADS // HEADS_PER_KV)
    kbd = [_block_diag(kvc_ref[:, 0:D_KV], kvp_ref[:, 0:D_KV], g) for g in groups]
    vbd = [_block_diag(kvc_ref[:, D_KV:2 * D_KV], kvp_ref[:, D_KV:2 * D_KV], g) for g in groups]
    qps = [(q_ref[:, j * 128:(j + 1) * 128] * SCALE).astype(BF16) for j in range(N_HEADS // 2)]
    return qps, kbd, vbd


def _attn_forward(q_ref, kvc_ref, kvp_ref, ga_ref, sink_ref, gn_ref, geometry, ya_ref, oa_ref, pr_ref, sp_ref):
    use_cur, dist, valid = geometry
    _, kbd, vbd = operands = _attn_operands(q_ref, kvc_ref, kvp_ref)
    yield
    scores = []
    for j, qp in enumerate(operands[0]):
        scores += _merge(_nt(qp, kbd[j // 4]), use_cur)
    yield
    sinks = [sink_ref[0, h] for h in range(N_HEADS)]
    scores = [jnp.where(valid, s - SLOPES[h] * dist, -jnp.inf) for h, s in enumerate(scores)]
    maxes = [jnp.maximum(jnp.max(s, axis=-1, keepdims=True), sinks[h]) for h, s in enumerate(scores)]
    yield
    exps = [jnp.exp(s - m) for s, m in zip(scores, maxes)]
    sink_exps = [jnp.exp(sinks[h] - m) for h, m in enumerate(maxes)]
    yield
    invs = [1.0 / (jnp.sum(e, axis=-1, keepdims=True) + se) for e, se in zip(exps, sink_exps)]
    probs = [e * inv for e, inv in zip(exps, invs)]
    pr_ref[...] = jnp.concatenate(probs, axis=1).astype(BF16)
    lane = lax.broadcasted_iota(jnp.int32, (BLK, 128), 1)
    sp_ref[...] = sum(jnp.where(lane == h, se * inv, 0.0) for h, (se, inv) in enumerate(zip(sink_exps, invs)))
    yield
    p4s = [_split(probs[2 * j], probs[2 * j + 1], use_cur).astype(BF16) for j in range(N_HEADS // 2)]
    ya = jnp.concatenate([_nn(p4, vbd[j // 4]) for j, p4 in enumerate(p4s)], axis=1)
    ya_ref[...] = ya
    yield
    silu, _ = _silu_and_grad(ga_ref[...])
    oa_ref[...] = (ya * _rstd(ya) * gn_ref[...] * silu).astype(BF16)


def _kv_specs(n_blocks, reverse):
    def blk(i):
        return (n_blocks - 1 - i) if reverse else i
    cur = pl.BlockSpec((BLK, 2 * D_KV), lambda i: (blk(i), 0))
    prev = pl.BlockSpec((BLK, 2 * D_KV), lambda i: (jnp.maximum(blk(i) - 1, 0), 0))
    return cur, prev


def _out_proj_loss(x, oc, oa, wo, norm_final, target, pc, conv_w, norm_conv_out):
    seq = x.shape[0]
    tile = TOK_TILE
    n_tiles = seq // tile

    def body(x_ref, oc_ref, oa_ref, wo_ref, gf_ref, t_ref, pc_ref, hcc_ref, hcu_ref, cw_ref, gn_ref,
             dx2_ref, doa_ref, gwo_ref, dpc_ref, small_ref, loss_acc, zbuf, dbuf):
        step = pl.program_id(0)

        def small_add(row, value):
            small_ref[row:row + 1, :] += jnp.sum(value, axis=0, keepdims=True)

        @pl.when(step == 0)
        def _():
            gwo_ref[...] = jnp.zeros_like(gwo_ref)
            small_ref[...] = jnp.zeros_like(small_ref)
            loss_acc[...] = jnp.zeros_like(loss_acc)
            dbuf[tile:tile + 8, :] = jnp.zeros((8, D_CONV), F32)

        oc, oa = oc_ref[...], oa_ref[...]
        x2 = x_ref[...] + _nn(oc, wo_ref[0:D_CONV, :]) + _nn(oa, wo_ref[D_CONV:D_MIX, :])
        r = _rstd(x2)
        xhat = x2 * r
        err = xhat * gf_ref[...] - t_ref[...]
        loss_acc[...] += jnp.sum(err * err, axis=0, keepdims=True) * (0.5 / D_MODEL)
        dy = err * (1.0 / D_MODEL)
        small_add(SMALL_NORM_FINAL, dy * xhat)
        dx2 = _rms_bwd(dy * gf_ref[...], xhat, r)
        dx2_ref[...] = dx2
        db = dx2.astype(BF16)
        do = _nt(db, wo_ref[0:D_CONV, :])
        doa_ref[...] = _nt(db, wo_ref[D_CONV:D_MIX, :])
        gwo_ref[0:D_CONV, :] += _tn(oc, db)
        gwo_ref[D_CONV:D_MIX, :] += _tn(oa, db)

        is_first_tile = step == n_tiles - 1
        zbuf[0:8, :] = jnp.where(is_first_tile, 0.0, hcc_ref[...] * hcu_ref[...])
        cb, cc, cu, z, z1, z2, conv = _conv_core(pc_ref, zbuf, cw_ref)
        silu, dsilu = _silu_and_grad(pc_ref[:, 3 * D_CONV:4 * D_CONV])
        yc = cb * conv
        rc = _rstd(yc)
        chat = yc * rc
        dn = do * silu
        dpc_ref[:, 3 * D_CONV:4 * D_CONV] = (do * (chat * gn_ref[...]) * dsilu).astype(BF16)
        small_add(SMALL_NORM_CONV, dn * chat)
        dyc = _rms_bwd(dn * gn_ref[...], chat, rc)
        dpc_ref[:, 0:D_CONV] = (dyc * conv).astype(BF16)
        dconv = dyc * cb
        small_add(SMALL_CONV_W, dconv * z2)
        small_add(SMALL_CONV_W + 1, dconv * z1)
        small_add(SMALL_CONV_W + 2, dconv * z)
        dbuf[0:tile, :] = dconv
        dz = cw_ref[2:3, :] * dconv + cw_ref[1:2, :] * dbuf[1:tile + 1, :] + cw_ref[0:1, :] * dbuf[2:tile + 2, :]
        dpc_ref[:, D_CONV:2 * D_CONV] = (dz * cu).astype(BF16)
        dpc_ref[:, 2 * D_CONV:3 * D_CONV] = (dz * cc).astype(BF16)
        dbuf[tile:tile + 8, :] = dbuf[0:8, :]

        @pl.when(step == n_tiles - 1)
        def _():
            lane = lax.broadcasted_iota(jnp.int32, (1, D_MODEL), 1)
            small_ref[SMALL_MISC:SMALL_MISC + 1, :] = jnp.where(lane == SMALL_LOSS_LANE, jnp.sum(loss_acc[...]), 0.0)

    def rev(i):
        return n_tiles - 1 - i

    def row(width):
        return pl.BlockSpec((tile, width), lambda i: (rev(i), 0))

    def halo(col_block):
        return pl.BlockSpec((8, D_CONV), lambda i: (jnp.maximum(rev(i) * (tile // 8) - 1, 0), col_block))

    def const(shape):
        return pl.BlockSpec(shape, lambda i: (0, 0))

    return pl.pallas_call(
        body, name="out_proj_loss", grid=(n_tiles,),
        out_shape=(jax.ShapeDtypeStruct((seq, D_MODEL), F32), jax.ShapeDtypeStruct((seq, D_ATTN), F32),
                   jax.ShapeDtypeStruct((D_MIX, D_MODEL), F32), jax.ShapeDtypeStruct((seq, D_PC), BF16),
                   jax.ShapeDtypeStruct((SMALL_ROWS, D_MODEL), F32)),
        in_specs=[row(D_MODEL), row(D_CONV), row(D_ATTN), _resident(wo.shape), _resident((1, D_MODEL)), row(D_MODEL),
                  row(D_PC), halo(1), halo(2), _resident(conv_w.shape), _resident((1, D_CONV))],
        out_specs=(row(D_MODEL), row(D_ATTN), const((D_MIX, D_MODEL)), row(D_PC), const((SMALL_ROWS, D_MODEL))),
        scratch_shapes=[pltpu.VMEM((1, D_MODEL), F32), pltpu.VMEM((tile + 8, D_CONV), F32),
                        pltpu.VMEM((tile + 8, D_CONV), F32)],
        compiler_params=_params(62, ("arbitrary",)),
    )(x, oc, oa, wo, norm_final, target, pc, pc, pc, conv_w, norm_conv_out)


def _attn_bwd(q, kv, ga, ya, doa, probs, sink_probs, norm_attn_out, gwo, small):
    seq = q.shape[0]
    n_blocks = seq // BLK
    stage_steps = (0, n_blocks // 4, n_blocks // 2, (3 * n_blocks) // 4, n_blocks - 1)

    def body(q_ref, kvc_ref, kvp_ref, ga_ref, ya_ref, doa_ref, pr_ref, sp_ref, gn_ref, gwo_ref, small_ref,
             dqg_ref, small_out, gwo_sh, gna_ref, gs_ref, carry, dya_buf, *rs_scratch):
        step = pl.program_id(0)

        @pl.when(step == 0)
        def _():
            gna_ref[...] = jnp.zeros_like(gna_ref)
            gs_ref[...] = jnp.zeros_like(gs_ref)
            carry[...] = jnp.zeros_like(carry)

        ya = ya_ref[...]
        r = _rstd(ya)
        xhat = ya * r
        silu, dsilu = _silu_and_grad(ga_ref[...])
        do = doa_ref[...]
        dn = do * silu
        dqg_ref[:, D_ATTN:2 * D_ATTN] = (do * (xhat * gn_ref[...]) * dsilu).astype(BF16)
        gna_ref[...] += jnp.sum(dn * xhat, axis=0, keepdims=True)
        dya_buf[...] = _rms_bwd(dn * gn_ref[...], xhat, r).astype(BF16)

        use_cur = _band_geometry(n_blocks - 1 - step)[0]
        lane = lax.broadcasted_iota(jnp.int32, (BLK, 128), 1)

        def fold(bd):
            return (jnp.where(lane < 64, bd[0:BLK], 0.0) + jnp.where(lane >= 64, bd[BLK:2 * BLK], 0.0),
                    jnp.where(lane < 64, bd[2 * BLK:3 * BLK], 0.0) + jnp.where(lane >= 64, bd[3 * BLK:4 * BLK], 0.0))

        pairs = range(N_HEADS // 2)
        qps, kbd, vbd = _attn_operands(q_ref, kvc_ref, kvp_ref)
        probs = [pr_ref[:, h * BLK:(h + 1) * BLK].astype(F32) for h in range(N_HEADS)]
        dyps = [dya_buf[:, j * 128:(j + 1) * 128] for j in pairs]
        dps = []
        for j in pairs:
            dps += _merge(_nt(dyps[j], vbd[j // 4]), use_cur)
        deltas = [jnp.sum(p * dp, axis=-1, keepdims=True) for p, dp in zip(probs, dps)]
        dss = [p * (dp - delta) for p, dp, delta in zip(probs, dps, deltas)]
        delta_lanes = sum(jnp.where(lane == h, deltas[h], 0.0) for h in range(N_HEADS))
        gs_ref[...] -= jnp.sum(sp_ref[...] * delta_lanes, axis=0, keepdims=True)
        ds4s = [_split(dss[2 * j], dss[2 * j + 1], use_cur).astype(BF16) for j in pairs]
        p4s = [_split(probs[2 * j], probs[2 * j + 1], use_cur).astype(BF16) for j in pairs]
        dqg_ref[:, 0:D_ATTN] = jnp.concatenate([_nn(ds4s[j], kbd[j // 4]) * SCALE for j in pairs], axis=1).astype(BF16)
        sums = []
        for group in range(N_HEADS // HEADS_PER_KV):
            acc = [jnp.zeros((BLK, 128), F32) for _ in range(4)]
            for j in range(group * 4, group * 4 + 4):
                for slot, part in enumerate(fold(_tn(ds4s[j], qps[j])) + fold(_tn(p4s[j], dyps[j]))):
                    acc[slot] = acc[slot] + part
            sums.append([a + pltpu.roll(a, 64, 1) for a in acc])
        dk_cur, dk_prev, dv_cur, dv_prev = (jnp.where(lane < 64, a, b) for a, b in zip(sums[0], sums[1]))
        dqg_ref[:, 2 * D_ATTN:2 * D_ATTN + 2 * D_KV] = (jnp.concatenate([dk_cur, dv_cur], axis=1) + carry[...]).astype(BF16)
        carry[...] = jnp.concatenate([dk_prev, dv_prev], axis=1)

        @pl.when(step == n_blocks - 1)
        def _():
            small_out[...] = small_ref[...]
            small_out[SMALL_NORM_ATTN:SMALL_NORM_ATTN + 1, :] = gna_ref[...]
            small_out[SMALL_MISC:SMALL_MISC + 1, 0:128] = small_ref[SMALL_MISC:SMALL_MISC + 1, 0:128] + gs_ref[...]

        for at, stage in zip(stage_steps, _rs_wout_stages(gwo_ref, gwo_sh, *rs_scratch)):
            pl.when(step == at)(stage)

    row = pl.BlockSpec((BLK, D_ATTN), lambda i: (n_blocks - 1 - i, 0))
    kv_cur, kv_prev = _kv_specs(n_blocks, reverse=True)
    any_spec = pl.BlockSpec(memory_space=pl.ANY)
    return pl.pallas_call(
        body, name="attn_bwd", grid=(n_blocks,),
        out_shape=(jax.ShapeDtypeStruct((seq, D_QG), BF16), jax.ShapeDtypeStruct(small.shape, F32),
                   jax.ShapeDtypeStruct((gwo.shape[0] // N_CHIPS, gwo.shape[1]), F32)),
        in_specs=[row, kv_cur, kv_prev, row, row, row,
                  pl.BlockSpec((BLK, N_HEADS * BLK), lambda i: (n_blocks - 1 - i, 0)),
                  pl.BlockSpec((BLK, 128), lambda i: (n_blocks - 1 - i, 0)), _resident((1, D_ATTN)),
                  any_spec, _resident(small.shape)],
        out_specs=(pl.BlockSpec((BLK, D_QG), lambda i: (n_blocks - 1 - i, 0)),
                   pl.BlockSpec(small.shape, lambda i: (0, 0)), any_spec),
        scratch_shapes=[pltpu.VMEM((1, D_ATTN), F32), pltpu.VMEM((1, 128), F32),
                        pltpu.VMEM((BLK, 2 * D_KV), F32), pltpu.VMEM((BLK, D_ATTN), BF16)] + _rs_wout_scratch(gwo.shape),
        compiler_params=_params(44, ("arbitrary",)),
    )(q, kv, kv, ga, ya, doa, probs, sink_probs, norm_attn_out, gwo, small)


GBLK = 256
GSUB = 64
PAIR_RING = 4
LAG_PAIR, LAG_HOP1, LAG_HOP2 = 1, 7, 14


def _bwd_in(dpc, dqg, h, wt, x, norm_in, dx2, small):
    seq = x.shape[0]
    n_blk = D_IN_PROJ // GBLK
    per_chip = n_blk // N_CHIPS
    n_slots = (n_blk + 1) // 2
    n_sub = GBLK // GSUB
    chip_rows = D_IN_PROJ // N_CHIPS
    tile = TOK_TILE
    n_tiles = seq // tile
    n_steps = n_blk + max(n_tiles, LAG_HOP2)
    chunk = min(seq, 512)
    blk_q, blk_kv, blk_ga = ROW_Q // GBLK, ROW_KV // GBLK, ROW_GA // GBLK

    def block_of(i):
        k = i % N_CHIPS
        robin = per_chip * ((k % 2) * 2 + k // 2) + i // N_CHIPS
        if isinstance(i, int):
            return robin if i < per_chip * N_CHIPS else i
        return jnp.where(i < per_chip * N_CHIPS, robin, i)

    def owner_of(i):
        return (i // N_CHIPS) % 2

    def slot_of(i):
        return (i // (2 * N_CHIPS)) * N_CHIPS + i % N_CHIPS

    def body(dpc_ref, dqg_ref, wt_ref, h_ref, x_ref, g_ref, dx2_ref, small_ref, gx_ref, small_sum, gwt_sh,
             dh_acc, gni, keep, pbuf, xbuf, land, land2, small_land,
             pair_send, pair_recv, h1_send, h1_recv, h2_send, h2_recv, sw_send, sw_recv, sm_send, sm_recv, out_sem):
        step = pl.program_id(0)
        x_i, y_i, c = lax.axis_index("x"), lax.axis_index("y"), lax.axis_index("c")
        me = 4 * x_i + 2 * y_i + c
        j = 2 * x_i + y_i
        pa = (_xor(x_i, 1 - c), _xor(y_i, c), c)
        pb = (_xor(x_i, c), _xor(y_i, 1 - c), c)
        sib = (x_i, y_i, 1 - c)
        ja = 2 * pa[0] + pa[1]
        jb = 2 * pb[0] + pb[1]
        jd = 3 - j

        def remote(src, dst, send, recv, to):
            return pltpu.make_async_remote_copy(src_ref=src, dst_ref=dst, send_sem=send, recv_sem=recv,
                                                device_id=to, device_id_type=MESH)

        def piece(ref, slot, u, n):
            return ref.at[slot, pl.ds(u * GSUB, n * GSUB), :]

        def chip_rows_at(ref, local, n):
            return ref.at[pl.ds(pl.multiple_of(local, GSUB), n * GSUB), :]

        def pair_copy(i):
            slot = slot_of(i)
            return remote(pbuf.at[i % PAIR_RING], land.at[slot], pair_send.at[slot], pair_recv.at[slot], sib)

        def h1_copy(slot, u, n):
            k = slot * n_sub + u
            return remote(piece(xbuf, slot, u, n), piece(xbuf, slot, u, n), h1_send.at[k], h1_recv.at[k], pa)

        def h2_copy(slot, u, n, local):
            k = slot * n_sub + u
            return remote(piece(xbuf, slot, u, n), chip_rows_at(land2, local, n), h2_send.at[k], h2_recv.at[k], pb)

        def sw_copy(slot, u, n, local):
            k = slot * n_sub + u
            return remote(piece(keep, slot, u, n), chip_rows_at(gwt_sh, local, n), sw_send.at[k], sw_recv.at[k], sib)

        def owned(i):
            return (i >= 0) & (i < n_blk) & (owner_of(i) == c)

        def chip_of(blk, u):
            row = blk * GBLK + u * GSUB
            chip = row // chip_rows
            return chip, row - chip * chip_rows

        def pieces(blk):
            first, local = chip_of(blk, 0)
            whole = first == chip_of(blk, n_sub - 1)[0]
            if isinstance(blk, int):
                return [(True, 0, n_sub, first, local)] if whole else [(True, u, 1) + chip_of(blk, u) for u in range(n_sub)]
            return [(whole, 0, n_sub, first, local)] + [(jnp.logical_not(whole), u, 1) + chip_of(blk, u) for u in range(n_sub)]

        @pl.when(step == 0)
        def _():
            dh_acc[...] = jnp.zeros_like(dh_acc)
            gni[...] = jnp.zeros_like(gni)

        @pl.when(step < n_blk)
        def _():
            from_pc = block_of(step) < blk_q
            block = _tn(jnp.where(from_pc, dpc_ref[...], dqg_ref[...]), h_ref[...])
            for t in range(0, seq, chunk):
                d = jnp.where(from_pc, dpc_ref[t:t + chunk, :], dqg_ref[t:t + chunk, :])
                dh_acc[t:t + chunk, :] += _nn(d, wt_ref[...])

            @pl.when(owner_of(step) == c)
            def _():
                keep[slot_of(step)] = block

            @pl.when(owner_of(step) != c)
            def _():
                @pl.when(step >= 2 * PAIR_RING)
                def _():
                    pair_copy(step - 2 * PAIR_RING).wait_send()
                pbuf[step % PAIR_RING] = block.astype(BF16)
                pair_copy(step).start()

        i1 = step - LAG_PAIR

        @pl.when(owned(i1))
        def _():
            slot = slot_of(i1)
            pair_copy(i1).wait_recv()
            _accumulate(keep.at[slot], land.at[slot])
            for cond, u, n, chip, _ in pieces(block_of(i1)):
                @pl.when(cond & ((chip == ja) | (chip == jd)))
                def _(u=u, n=n):
                    _cast_rows(piece(keep, slot, u, n), piece(xbuf, slot, u, n))
                    h1_copy(slot, u, n).start()

        i2 = step - LAG_HOP1

        @pl.when(owned(i2))
        def _():
            slot = slot_of(i2)
            for cond, u, n, chip, local in pieces(block_of(i2)):
                @pl.when(cond & ((chip == j) | (chip == jb)))
                def _(u=u, n=n, chip=chip, local=local):
                    h1_copy(slot, u, n).wait_recv()
                    _accumulate(piece(keep, slot, u, n), piece(xbuf, slot, u, n))

                    @pl.when(chip == jb)
                    def _():
                        _cast_rows(piece(keep, slot, u, n), piece(xbuf, slot, u, n))
                        h2_copy(slot, u, n, local).start()

                @pl.when(cond & ((chip == ja) | (chip == jd)))
                def _(u=u, n=n):
                    h1_copy(slot, u, n).wait_send()

        i3 = step - LAG_HOP2

        @pl.when(owned(i3))
        def _():
            slot = slot_of(i3)
            for cond, u, n, chip, local in pieces(block_of(i3)):
                @pl.when(cond & (chip == j))
                def _(u=u, n=n, local=local):
                    h2_copy(slot, u, n, local).wait_recv()
                    _accumulate(piece(keep, slot, u, n), chip_rows_at(land2, local, n))
                    mine = pltpu.make_async_copy(piece(keep, slot, u, n), chip_rows_at(gwt_sh, local, n), out_sem.at[0])
                    mine.start()
                    sw_copy(slot, u, n, local).start()
                    mine.wait()

                @pl.when(cond & (chip == jb))
                def _(u=u, n=n, local=local):
                    h2_copy(slot, u, n, local).wait_send()

        e = step - n_blk

        @pl.when((e >= 0) & (e < n_tiles))
        def _():
            dh = dh_acc[pl.ds(pl.multiple_of(e * tile, tile), tile), :]
            xv = x_ref[...]
            r = _rstd(xv)
            xhat = xv * r
            gni[...] += jnp.sum(dh * xhat, axis=0, keepdims=True)
            gx_ref[...] = _rms_bwd(dh * g_ref[...], xhat, r) + dx2_ref[...]

        @pl.when(step == n_steps - 1)
        def _():
            small_land[me] = small_ref[...]
            small_land[me, SMALL_NORM_IN:SMALL_NORM_IN + 1, :] = gni[...]
            others = [(dx, dy, dc) for dx in (0, 1) for dy in (0, 1) for dc in (0, 1)][1:]
            sends = [remote(small_land.at[me], small_land.at[me], sm_send.at[k], sm_recv.at[k],
                            (_xor(x_i, dx), _xor(y_i, dy), _xor(c, dc))) for k, (dx, dy, dc) in enumerate(others)]
            for cp in sends:
                cp.start()
            for i in range(n_blk):
                if i + 2 * PAIR_RING >= n_blk:
                    @pl.when(owner_of(i) != c)
                    def _(i=i):
                        pair_copy(i).wait_send()
                for _, u, n, chip, local in pieces(block_of(i)):
                    @pl.when((j == chip) & (c == owner_of(i)))
                    def _(i=i, u=u, n=n, local=local):
                        sw_copy(slot_of(i), u, n, local).wait_send()

                    @pl.when((j == chip) & (c != owner_of(i)))
                    def _(i=i, u=u, n=n, local=local):
                        sw_copy(slot_of(i), u, n, local).wait_recv()
            for cp in sends:
                cp.wait_recv()
            total = small_land[0]
            for dev in range(1, 8):
                total = total + small_land[dev]
            small_sum[...] = total
            for cp in sends:
                cp.wait_send()

    def blk_at(i):
        return block_of(jnp.clip(i, 0, n_blk - 1))

    last_pc_step = max(i for i in range(n_blk) if block_of(i) < blk_q)

    def next_block(i, in_pc):
        i = jnp.clip(i, 0, n_blk - 1)
        step = jnp.full_like(i, last_pc_step if in_pc else n_blk - 1)
        for ahead in reversed(range(N_CHIPS)):
            cand = jnp.minimum(i + ahead, n_blk - 1)
            step = jnp.where((block_of(cand) < blk_q) == in_pc, cand, step)
        return block_of(step)

    def dqg_block(i):
        b = next_block(i, False)
        q_blk = jnp.clip(b - blk_q, 0, blk_kv - blk_q - 1)
        ga_blk = (D_ATTN // GBLK) + jnp.clip(b - blk_ga, 0, n_blk - blk_ga - 1)
        return jnp.where(b < blk_kv, q_blk, jnp.where(b == blk_kv, 2 * D_ATTN // GBLK, ga_blk))

    def tok(i):
        return (jnp.clip(i - n_blk, 0, n_tiles - 1), 0)

    n_piece = n_slots * n_sub
    dma = pltpu.SemaphoreType.DMA
    return pl.pallas_call(
        body, name="bwd_in", grid=(n_steps,),
        out_shape=(jax.ShapeDtypeStruct((seq, D_MODEL), F32), jax.ShapeDtypeStruct(small.shape, F32),
                   jax.ShapeDtypeStruct((chip_rows, D_MODEL), F32)),
        in_specs=[pl.BlockSpec((seq, GBLK), lambda i: (0, next_block(i, True))),
                  pl.BlockSpec((seq, GBLK), lambda i: (0, dqg_block(i))),
                  pl.BlockSpec((GBLK, D_MODEL), lambda i: (blk_at(i), 0)),
                  _resident(h.shape),
                  pl.BlockSpec((tile, D_MODEL), tok), _resident((1, D_MODEL)), pl.BlockSpec((tile, D_MODEL), tok),
                  _resident(small.shape)],
        out_specs=(pl.BlockSpec((tile, D_MODEL), tok), pl.BlockSpec(small.shape, lambda i: (0, 0)),
                   pl.BlockSpec(memory_space=pl.ANY)),
        scratch_shapes=[pltpu.VMEM((seq, D_MODEL), F32), pltpu.VMEM((1, D_MODEL), F32),
                        pltpu.VMEM((n_slots, GBLK, D_MODEL), F32), pltpu.VMEM((PAIR_RING, GBLK, D_MODEL), BF16),
                        pltpu.VMEM((n_slots, GBLK, D_MODEL), BF16), pltpu.VMEM((n_slots, GBLK, D_MODEL), BF16),
                        pltpu.VMEM((chip_rows, D_MODEL), BF16), pltpu.VMEM((8,) + small.shape, F32),
                        dma((n_slots,)), dma((n_slots,)), dma((n_piece,)), dma((n_piece,)), dma((n_piece,)),
                        dma((n_piece,)), dma((n_piece,)), dma((n_piece,)), dma((7,)), dma((7,)), dma((1,))],
        compiler_params=_params(62, ("arbitrary",)),
    )(dpc, dqg, wt, h, x, norm_in, dx2, small)


def _accumulate(dst_ref, src_ref, rows=16):
    def step(i, carry):
        sl = pl.ds(pl.multiple_of(i * rows, rows), rows)
        dst_ref[sl, :] = dst_ref[sl, :] + src_ref[sl, :].astype(F32)
        return carry
    lax.fori_loop(0, dst_ref.shape[0] // rows, step, 0)


def _rs_wout_scratch(gwo_shape):
    o_half, width = gwo_shape[0] // N_CHIPS // 2, gwo_shape[1]
    return [pltpu.VMEM((4, o_half, width), F32), pltpu.VMEM((4, o_half, width), BF16),
            pltpu.VMEM((4, o_half, width), BF16), pltpu.VMEM((2, o_half, width), BF16),
            pltpu.VMEM((o_half, width), BF16),
            pltpu.SemaphoreType.DMA((8,)), pltpu.SemaphoreType.DMA((8,)), pltpu.SemaphoreType.DMA((4,))]


def _rs_wout_stages(gwo_ref, gwo_sh, acc_o, sb_o, r1o, r2o, r3o, send_sems, recv_sems, local_sems):
    o_rows = gwo_ref.shape[0] // N_CHIPS
    o_half = o_rows // 2
    x, y, c = lax.axis_index("x"), lax.axis_index("y"), lax.axis_index("c")
    j = 2 * x + y
    pa = (_xor(x, 1 - c), _xor(y, c), c)
    pb = (_xor(x, c), _xor(y, 1 - c), c)
    sib = (x, y, 1 - c)
    ja = 2 * pa[0] + pa[1]
    jb = 2 * pb[0] + pb[1]
    jd = 3 - j
    order = (ja, jd, jb, j)
    sib_order = (jb, jd, ja, j)

    def rcopy(k, src, dst, to):
        return pltpu.make_async_remote_copy(src_ref=src, dst_ref=dst, send_sem=send_sems.at[k],
                                            recv_sem=recv_sems.at[k], device_id=to, device_id_type=MESH)

    def o_rows_of(chip, half):
        return gwo_ref.at[pl.ds(pl.multiple_of(chip * o_rows + half * o_half, 8), o_half), :]

    def load_all(chips, half):
        cps = [pltpu.make_async_copy(o_rows_of(chip, half), acc_o.at[s], local_sems.at[s]) for s, chip in enumerate(chips)]
        for cp in cps:
            cp.start()
        return cps

    def pair_send(s):
        return rcopy(s, sb_o.at[s], r1o.at[s], sib)

    def out_half(half):
        return gwo_sh.at[pl.ds(pl.multiple_of(half * o_half, 8), o_half), :]

    hop1 = [rcopy(4 + s, sb_o.at[s], r2o.at[s], pa) for s in range(2)]
    hop2 = rcopy(6, sb_o.at[2], r3o, pb)
    swap = rcopy(7, acc_o.at[3], out_half(c), sib)
    mine = pltpu.make_async_copy(acc_o.at[3], out_half(c), local_sems.at[0])

    def resend(s, copy):
        pair_send(s).wait_send()
        _cast_rows(acc_o.at[s], sb_o.at[s])
        copy.start()

    def stage_pair():
        for s, cp in enumerate(load_all(sib_order, 1 - c)):
            cp.wait()
            _cast_rows(acc_o.at[s], sb_o.at[s])
            pair_send(s).start()

    def stage_hop1():
        for s, cp in enumerate(load_all(order, c)):
            cp.wait()
            pair_send(s).wait_recv()
            _accumulate(acc_o.at[s], r1o.at[s])
            if s < 2:
                resend(s, hop1[s])

    def stage_hop2():
        hop1[1].wait_recv()
        _accumulate(acc_o.at[2], r2o.at[1])
        resend(2, hop2)
        hop1[0].wait_recv()
        _accumulate(acc_o.at[3], r2o.at[0])

    def stage_final():
        hop2.wait_recv()
        _accumulate(acc_o.at[3], r3o)
        mine.start()
        swap.start()

    def stage_drain():
        rcopy(7, acc_o.at[3], out_half(1 - c), sib).wait_recv()
        for cp in [pair_send(3)] + hop1 + [hop2, swap]:
            cp.wait_send()
        mine.wait()

    return stage_pair, stage_hop1, stage_hop2, stage_final, stage_drain


def _adamw(name, w, g, m, v, rows):
    def body(w_ref, g_ref, m_ref, v_ref, go_ref, d_ref, nm_ref, nv_ref):
        _adamw_update(w_ref, g_ref[...], m_ref, v_ref, go_ref, d_ref, nm_ref, nv_ref)

    spec = pl.BlockSpec((rows, w.shape[1]), lambda i: (i, 0))
    shape = jax.ShapeDtypeStruct(w.shape, F32)
    return pl.pallas_call(
        body, name="adamw_" + name, grid=(w.shape[0] // rows,),
        out_shape=(shape,) * 4, in_specs=[spec] * 4, out_specs=(spec,) * 4,
        compiler_params=_params(32, ("arbitrary",)),
    )(w, g, m, v)


def _adamw_update(w_ref, gv, m_ref, v_ref, go_ref, d_ref, nm_ref, nv_ref, at=...):
    go_ref[at] = gv
    nm = ADAM_B1 * m_ref[at] + (1.0 - ADAM_B1) * gv
    nv = ADAM_B2 * v_ref[at] + (1.0 - ADAM_B2) * (gv * gv)
    m_hat = nm / (1.0 - ADAM_B1 ** ADAM_STEP)
    v_hat = nv / (1.0 - ADAM_B2 ** ADAM_STEP)
    d_ref[at] = -ADAM_LR * (m_hat / (jnp.sqrt(v_hat) + ADAM_EPS) + ADAM_WD * w_ref[at])
    nm_ref[at] = nm
    nv_ref[at] = nv


def _adamw_small(chip, small_sum, weights, grads_of, ms, vs):
    n = len(weights)

    def body(chip_ref, small_ref, *refs):
        ins, outs, loss_ref = refs[:3 * n], refs[3 * n:-1], refs[-1]
        for k in range(n):
            w_ref, m_ref, v_ref = ins[3 * k:3 * k + 3]
            for at, gv in grads_of[k](small_ref, chip_ref):
                _adamw_update(w_ref, gv, m_ref, v_ref, *outs[4 * k:4 * k + 4], at=at)
        loss_ref[...] = small_ref[SMALL_MISC:SMALL_MISC + 1, SMALL_LOSS_LANE:SMALL_LOSS_LANE + 1]

    flat = [a for group in zip(weights, ms, vs) for a in group]
    vmem = pl.BlockSpec(memory_space=pltpu.VMEM)
    out = pl.pallas_call(
        body, name="adamw_small",
        out_shape=tuple(jax.ShapeDtypeStruct(w.shape, F32) for w in weights for _ in range(4))
        + (jax.ShapeDtypeStruct((1, 1), F32),),
        in_specs=[pl.BlockSpec(memory_space=pltpu.SMEM)] + [vmem] * (1 + 3 * n), out_specs=(vmem,) * (4 * n + 1),
    )(chip, small_sum, *flat)
    return [tuple(out[4 * k:4 * k + 4]) for k in range(n)], out[-1][0, 0]


def kernel(x, norm_in, w_in, conv_w, attn_sinks, norm_conv_out, norm_attn_out, w_out, norm_final, loss_target, m_norm_in, m_w_in, m_conv_w, m_attn_sinks, m_norm_conv_out, m_norm_attn_out, m_w_out, m_norm_final, v_norm_in, v_w_in, v_conv_w, v_attn_sinks, v_norm_conv_out, v_norm_attn_out, v_w_out, v_norm_final):
    chip = 2 * lax.axis_index("x") + lax.axis_index("y")
    xs, target = x[0], loss_target[0]
    norm_final2 = norm_final.reshape(1, D_MODEL)
    w_in_t, m_w_in_t, v_w_in_t = w_in[0].T, m_w_in[0].T, v_w_in[0].T

    h, pc, q, kv, ga, oc, ya, oa, probs, sink_probs, wt, cw, wo = _fwd_in(
        xs, norm_in, w_in_t, w_out[0], conv_w.transpose(1, 0, 2), norm_conv_out, attn_sinks, norm_attn_out)
    dx2, doa, gwo, dpc, small = _out_proj_loss(xs, oc, oa, wo, norm_final2, target, pc, cw, norm_conv_out)
    dqg, small, gwo_sh = _attn_bwd(q, kv, ga, ya, doa, probs, sink_probs, norm_attn_out, gwo, small)
    grad_x, small_sum, gwt_sh = _bwd_in(dpc, dqg, h, wt, xs, norm_in, dx2, small)

    def from_hbm(*arrays):
        return tuple(pltpu.with_memory_space_constraint(a, pltpu.HBM) for a in arrays)

    up_w_in = tuple(o.T[None] for o in _adamw("w_in", *from_hbm(w_in_t, gwt_sh, m_w_in_t, v_w_in_t), 200))
    up_w_out = tuple(o[None] for o in _adamw("w_out", *from_hbm(w_out[0], gwo_sh, m_w_out[0], v_w_out[0]), 128))

    def row_of(r):
        return lambda small_ref, chip_ref: [(..., small_ref[r:r + 1, :])]

    def sink_lanes(small_ref, chip_ref):
        return [(..., small_ref[SMALL_MISC:SMALL_MISC + 1, 0:N_HEADS])]

    def conv_taps(small_ref, chip_ref):
        width = D_CONV // N_CHIPS
        cols = pl.ds(pl.multiple_of(chip_ref[0] * width, width), width)
        return [(k, small_ref[pl.ds(SMALL_CONV_W + k, 1), cols]) for k in range(conv_w.shape[1])]

    def small_view(a):
        return a.transpose(1, 0, 2) if a.ndim == 3 else a.reshape(-1, a.shape[-1])

    small_w = (norm_in, conv_w, attn_sinks, norm_conv_out, norm_attn_out, norm_final)
    small_m = (m_norm_in, m_conv_w, m_attn_sinks, m_norm_conv_out, m_norm_attn_out, m_norm_final)
    small_v = (v_norm_in, v_conv_w, v_attn_sinks, v_norm_conv_out, v_norm_attn_out, v_norm_final)
    small_g = (row_of(SMALL_NORM_IN), conv_taps, sink_lanes, row_of(SMALL_NORM_CONV), row_of(SMALL_NORM_ATTN),
               row_of(SMALL_NORM_FINAL))
    w_views, m_views, v_views = (tuple(small_view(a) for a in group) for group in (small_w, small_m, small_v))
    up_small, loss = _adamw_small(chip.astype(jnp.int32).reshape(1), small_sum, w_views, small_g, m_views, v_views)
    up_small = [tuple(o.transpose(1, 0, 2) if w.ndim == 3 else o.reshape(w.shape) for o in up)
                for up, w in zip(up_small, small_w)]
    up_norm_in, up_conv_w, up_sinks, up_norm_conv, up_norm_attn, up_norm_final = up_small
    updates = (up_norm_in, up_w_in, up_conv_w, up_sinks, up_norm_conv, up_norm_attn, up_w_out, up_norm_final)
    grads_out, deltas, new_m, new_v = zip(*updates)
    return (loss, grad_x[None], *grads_out, *deltas, *new_m, *new_v)
```

```python
import jax
import jax.numpy as jnp
from jax import lax
from jax.experimental import pallas as pl
from jax.experimental.pallas import tpu as pltpu

F32 = jnp.float32
BF16 = jnp.bfloat16
MESH = pl.DeviceIdType.MESH

D_MODEL = 1024
D_CONV = 1024
D_ATTN = 1024
D_KV = 128
D_QG = 2 * D_ATTN + 2 * D_KV
D_MIX = D_CONV + D_ATTN
D_PC = 4 * D_CONV
D_IN_PROJ = D_PC + 2 * D_ATTN + 2 * D_KV
ROW_Q = D_PC
ROW_KV = ROW_Q + D_ATTN
ROW_GA = ROW_KV + 2 * D_KV
N_HEADS = 16
HEAD_DIM = 64
HEADS_PER_KV = 8
BLK = 128
N_CHIPS = 4
RMS_EPS = 1e-5
SCALE = HEAD_DIM ** -0.5
SLOPES = tuple(2.0 ** (-8.0 * (h + 1) / N_HEADS) for h in range(N_HEADS))

ADAM_LR, ADAM_B1, ADAM_B2, ADAM_EPS, ADAM_WD, ADAM_STEP = 0.001, 0.9, 0.999, 1e-08, 0.01, 10

SMALL_ROWS = 8
SMALL_NORM_IN, SMALL_NORM_CONV, SMALL_NORM_ATTN, SMALL_NORM_FINAL, SMALL_CONV_W, SMALL_MISC = 0, 1, 2, 3, 4, 7
SMALL_LOSS_LANE = N_HEADS

TOK_TILE = 256
PC_PIECE = 512
MIB = 1 << 20


def _params(vmem_mib, semantics=None):
    return pltpu.CompilerParams(dimension_semantics=semantics, vmem_limit_bytes=vmem_mib * MIB)


def _nn(a, b):
    return jnp.dot(a, b, preferred_element_type=F32)


def _nt(a, b):
    return lax.dot_general(a, b, (((1,), (1,)), ((), ())), preferred_element_type=F32)


def _tn(a, b):
    return lax.dot_general(a, b, (((0,), (0,)), ((), ())), preferred_element_type=F32)


def _rstd(v):
    return lax.rsqrt(jnp.mean(v * v, axis=-1, keepdims=True) + RMS_EPS)


def _rms_bwd(g, xhat, rstd):
    return rstd * (g - xhat * jnp.mean(g * xhat, axis=-1, keepdims=True))


def _silu_and_grad(g):
    s = jax.nn.sigmoid(g)
    return g * s, s * (1.0 + g * (1.0 - s))


def _resident(shape):
    return pl.BlockSpec(shape, lambda *_: (0,) * len(shape), pipeline_mode=pl.Buffered(1))


def _xor(a, b):
    return a + b - 2 * a * b


def _cast_rows(src_ref, dst_ref, rows=32):
    def step(i, carry):
        sl = pl.ds(pl.multiple_of(i * rows, rows), rows)
        dst_ref[sl, :] = src_ref[sl, :].astype(dst_ref.dtype)
        return carry
    lax.fori_loop(0, src_ref.shape[0] // rows, step, 0)


def _ag_scratch(shard_shape):
    rows, width = shard_shape
    return [pltpu.VMEM((rows, width), F32), pltpu.VMEM((rows, width), BF16), pltpu.VMEM((3, rows // 2, width), BF16),
            pltpu.SemaphoreType.DMA((6,)), pltpu.SemaphoreType.DMA((6,)), pltpu.SemaphoreType.DMA((4,))]


def _ag_stages(sh_ref, out, f32_buf, own, land, send_sems, recv_sems, local_sems):
    rows = sh_ref.shape[0]
    half = rows // 2
    x, y, c = lax.axis_index("x"), lax.axis_index("y"), lax.axis_index("c")
    j = 2 * x + y
    p1 = (_xor(x, c), _xor(y, 1 - c), c)
    p2 = (_xor(x, 1 - c), _xor(y, c), c)
    sib = (x, y, 1 - c)
    j1 = 2 * p1[0] + p1[1]
    j2 = 2 * p2[0] + p2[1]
    j3 = 3 - j

    def rows_of(chip, hf):
        return out.at[pl.ds(pl.multiple_of(chip * rows + hf * half, 16), half), :]

    def rcopy(k, src, dst, to):
        return pltpu.make_async_remote_copy(src_ref=src, dst_ref=dst, send_sem=send_sems.at[k],
                                            recv_sem=recv_sems.at[k], device_id=to, device_id_type=MESH)

    my_half = own.at[pl.ds(pl.multiple_of(c * half, 16), half), :]
    hop1 = rcopy(0, my_half, land.at[0], p1)
    hop2_own = rcopy(1, my_half, land.at[1], p2)
    hop2_fwd = rcopy(2, land.at[0], land.at[2], p2)
    swaps = [rcopy(3 + s, land.at[s], rows_of(chip, c), sib) for s, chip in enumerate((j1, j2, j3))]
    keeps = [pltpu.make_async_copy(land.at[s], rows_of(chip, c), local_sems.at[1 + s]) for s, chip in enumerate((j1, j2, j3))]
    load = pltpu.make_async_copy(sh_ref, f32_buf, local_sems.at[0])
    own_out = pltpu.make_async_copy(own, out.at[pl.ds(pl.multiple_of(j * rows, 16), rows), :], local_sems.at[0])

    def stage_send():
        load.start()
        load.wait()
        _cast_rows(f32_buf, own)
        own_out.start()
        hop1.start()

    def stage_forward():
        hop1.wait_recv()
        hop2_own.start()
        hop2_fwd.start()
        swaps[0].start()
        keeps[0].start()

    def stage_publish():
        hop2_own.wait_recv()
        swaps[1].start()
        keeps[1].start()
        hop2_fwd.wait_recv()
        swaps[2].start()
        keeps[2].start()

    def stage_drain():
        for s, chip in enumerate((j2, j1, j3)):
            rcopy(3 + s, my_half, rows_of(chip, 1 - c), sib).wait_recv()
        for cp in [hop1, hop2_own, hop2_fwd] + swaps:
            cp.wait_send()
        for cp in [own_out] + keeps:
            cp.wait()

    return stage_send, stage_forward, stage_publish, stage_drain


AG_CAST_ROWS = 400


def _gather_resident(sh_ref, out, f32_buf, send_sems, recv_sems, local_sems):
    rows = sh_ref.shape[0]
    half = rows // 2
    x, y, c = lax.axis_index("x"), lax.axis_index("y"), lax.axis_index("c")
    j = 2 * x + y
    p1 = (_xor(x, c), _xor(y, 1 - c), c)
    p2 = (_xor(x, 1 - c), _xor(y, c), c)
    sib = (x, y, 1 - c)
    j1 = 2 * p1[0] + p1[1]
    j2 = 2 * p2[0] + p2[1]
    j3 = 3 - j

    def rows_of(chip, hf):
        return out.at[pl.ds(pl.multiple_of(chip * rows + hf * half, 16), half), :]

    def send(k, chip, to):
        return pltpu.make_async_remote_copy(src_ref=rows_of(chip, c), dst_ref=rows_of(chip, c), send_sem=send_sems.at[k],
                                            recv_sem=recv_sems.at[k], device_id=to, device_id_type=MESH)

    per_half = half // AG_CAST_ROWS

    def cast_own(chunk):
        lo = pl.multiple_of(chunk * AG_CAST_ROWS, 16)
        load = pltpu.make_async_copy(sh_ref.at[pl.ds(lo, AG_CAST_ROWS), :], f32_buf, local_sems.at[0])
        load.start()
        load.wait()
        _cast_rows(f32_buf, out.at[pl.ds(pl.multiple_of(j * rows + lo, 16), AG_CAST_ROWS), :], rows=16)

    for k in range(per_half):
        cast_own(c * per_half + k)
    hop1 = send(0, j, p1)
    hop1.start()
    for k in range(per_half):
        cast_own((1 - c) * per_half + k)
    hop1.wait_recv()
    sends = [hop1, send(1, j, p2), send(2, j1, p2), send(3, j1, sib)]
    for cp in sends[1:]:
        cp.start()
    sends[1].wait_recv()
    sends.append(send(4, j2, sib))
    sends[-1].start()
    sends[2].wait_recv()
    sends.append(send(5, j3, sib))
    sends[-1].start()
    for k in (3, 4, 5):
        send(k, j, sib).wait_recv()
    for cp in sends:
        cp.wait_send()


def _fwd_in(x, norm_in, wt_sh, wo_sh, cw_sh, norm_conv_out, sinks, norm_attn_out):
    seq = x.shape[0]
    tile = TOK_TILE
    n_tiles = seq // tile
    stage_steps = (0, n_tiles // 4, (5 * n_tiles) // 8, n_tiles - 1)
    blocks = tile // BLK
    cw_cols = cw_sh.shape[-1]

    def body(x_ref, g_ref, wtsh_ref, wo_ref, cwsh_ref, gn_ref, sink_ref, gna_ref,
             h_ref, pc_ref, q_ref, kv_ref, ga_ref, oc_ref, ya_ref, oa_ref, pr_ref, sp_ref, wt_out, cw_ref, wo_out,
             zbuf, kv_last, wt_ref, f32_buf, cw_land, wt_send, wt_recv, wt_local, cw_send, cw_recv, cw_local, *ag_scratch):
        step = pl.program_id(0)
        to_hbm = pltpu.make_async_copy(wt_ref, wt_out, wt_local.at[0])

        @pl.when(step == 0)
        def _():
            zbuf[0:8, :] = jnp.zeros((8, D_CONV), F32)
            kv_last[...] = jnp.zeros_like(kv_last)
            x_i, y_i, c = lax.axis_index("x"), lax.axis_index("y"), lax.axis_index("c")
            j = 2 * x_i + y_i
            p1 = (_xor(x_i, c), _xor(y_i, 1 - c), c)
            p2 = (_xor(x_i, 1 - c), _xor(y_i, c), c)
            j1 = 2 * p1[0] + p1[1]

            def cw_copy(k, src, chip, to):
                return pltpu.make_async_remote_copy(src_ref=src, dst_ref=cw_land.at[chip], send_sem=cw_send.at[k],
                                                    recv_sem=cw_recv.at[k], device_id=to, device_id_type=MESH)

            mine = pltpu.make_async_copy(cwsh_ref, cw_land.at[j], cw_local.at[0])
            mine.start()
            first = cw_copy(0, cwsh_ref, j, p1)
            first.start()
            _gather_resident(wtsh_ref, wt_ref, f32_buf, wt_send, wt_recv, wt_local)
            to_hbm.start()
            first.wait_recv()
            second = [cw_copy(1, cwsh_ref, j, p2), cw_copy(2, cw_land.at[j1], j1, p2)]
            for cp in second:
                cp.start()
            for cp in second:
                cp.wait_recv()
            for cp in [first] + second:
                cp.wait_send()
            mine.wait()
            for chip in range(N_CHIPS):
                for tap in range(cw_sh.shape[0]):
                    cw_ref[tap:tap + 1, chip * cw_cols:(chip + 1) * cw_cols] = cw_land[chip, tap]

        stages = _ag_stages(wo_ref, wo_out, *ag_scratch)
        for at, stage in zip(stage_steps[:-1], stages[:-1]):
            pl.when(step == at)(stage)

        def attention(b):
            rows = pl.ds(b * BLK, BLK)
            kv_prev = kv_last if b == 0 else kv_ref.at[pl.ds((b - 1) * BLK, BLK), :]
            return _attn_forward(q_ref.at[rows, :], kv_ref.at[rows, :], kv_prev, ga_ref.at[rows, :], sink_ref, gna_ref,
                                 _band_geometry(step * blocks + b), ya_ref.at[rows, :], oa_ref.at[rows, :],
                                 pr_ref.at[rows, :], sp_ref.at[rows, :])

        xv = x_ref[...]
        h = (xv * _rstd(xv) * g_ref[...]).astype(BF16)
        h_ref[...] = h
        q_ref[...] = _nt(h, wt_ref[ROW_Q:ROW_KV, :])
        kv_ref[...] = _nt(h, wt_ref[ROW_KV:ROW_GA, :])
        ga_ref[...] = _nt(h, wt_ref[ROW_GA:D_IN_PROJ, :])
        attention_blocks = [attention(b) for b in range(blocks)]
        for lo in range(0, D_PC, PC_PIECE):
            pc_ref[:, lo:lo + PC_PIECE] = _nt(h, wt_ref[lo:lo + PC_PIECE, :])
            for stages_of_block in attention_blocks:
                next(stages_of_block, None)
        for stages_of_block in attention_blocks:
            for _ in stages_of_block:
                pass
        kv_last[...] = kv_ref[tile - BLK:tile, :]

        cb, _, _, _, _, _, conv = _conv_core(pc_ref, zbuf, cw_ref)
        yc = cb * conv
        silu, _ = _silu_and_grad(pc_ref[:, 3 * D_CONV:4 * D_CONV])
        oc_ref[...] = (yc * _rstd(yc) * gn_ref[...] * silu).astype(BF16)
        zbuf[0:8, :] = zbuf[tile:tile + 8, :]

        @pl.when(step == stage_steps[-1])
        def _():
            stages[-1]()
            to_hbm.wait()

    def row(width):
        return pl.BlockSpec((tile, width), lambda i: (i, 0))

    any_spec = pl.BlockSpec(memory_space=pl.ANY)
    dma = pltpu.SemaphoreType.DMA
    wt_shape = (N_CHIPS * wt_sh.shape[0], wt_sh.shape[1])
    cw_shape = (cw_sh.shape[0], N_CHIPS * cw_cols)
    return pl.pallas_call(
        body, name="fwd_in", grid=(n_tiles,),
        out_shape=(jax.ShapeDtypeStruct((seq, D_MODEL), BF16), jax.ShapeDtypeStruct((seq, D_PC), F32),
                   jax.ShapeDtypeStruct((seq, D_ATTN), F32), jax.ShapeDtypeStruct((seq, 2 * D_KV), F32),
                   jax.ShapeDtypeStruct((seq, D_ATTN), F32), jax.ShapeDtypeStruct((seq, D_CONV), BF16),
                   pltpu.HBM((seq, D_ATTN), F32), pltpu.HBM((seq, D_ATTN), BF16),
                   pltpu.HBM((seq, N_HEADS * BLK), BF16), pltpu.HBM((seq, 128), F32),
                   pltpu.HBM(wt_shape, BF16), jax.ShapeDtypeStruct(cw_shape, F32),
                   jax.ShapeDtypeStruct((N_CHIPS * wo_sh.shape[0], wo_sh.shape[1]), BF16)),
        in_specs=[row(D_MODEL), _resident((1, D_MODEL)), any_spec, any_spec, any_spec, _resident((1, D_CONV)),
                  pl.BlockSpec(memory_space=pltpu.SMEM), _resident((1, D_ATTN))],
        out_specs=(row(D_MODEL), row(D_PC), row(D_ATTN), row(2 * D_KV), row(D_ATTN), row(D_CONV), row(D_ATTN),
                   row(D_ATTN), row(N_HEADS * BLK), row(128), any_spec, pl.BlockSpec(cw_shape, lambda i: (0, 0)),
                   any_spec),
        scratch_shapes=[pltpu.VMEM((tile + 8, D_CONV), F32), pltpu.VMEM((BLK, 2 * D_KV), F32),
                        pltpu.VMEM(wt_shape, BF16), pltpu.VMEM((AG_CAST_ROWS, wt_sh.shape[1]), F32),
                        pltpu.VMEM((N_CHIPS,) + cw_sh.shape, F32),
                        dma((6,)), dma((6,)), dma((1,)), dma((3,)), dma((3,)), dma((1,))] + _ag_scratch(wo_sh.shape),
        compiler_params=_params(62, ("arbitrary",)),
    )(x, norm_in, wt_sh, wo_sh, cw_sh, norm_conv_out, sinks, norm_attn_out)


def _conv_core(pc_ref, zbuf, cw_ref):
    tile = pc_ref.shape[0]
    cb = pc_ref[:, 0:D_CONV]
    cc = pc_ref[:, D_CONV:2 * D_CONV]
    cu = pc_ref[:, 2 * D_CONV:3 * D_CONV]
    z = cc * cu
    zbuf[8:tile + 8, :] = z
    z1 = zbuf[7:tile + 7, :]
    z2 = zbuf[6:tile + 6, :]
    conv = cw_ref[0:1, :] * z2 + cw_ref[1:2, :] * z1 + cw_ref[2:3, :] * z
    return cb, cc, cu, z, z1, z2, conv


def _band_geometry(block_index):
    qi = lax.broadcasted_iota(jnp.int32, (BLK, BLK), 0)
    kp = lax.broadcasted_iota(jnp.int32, (BLK, BLK), 1)
    use_cur = kp <= qi
    dist = jnp.where(use_cur, qi - kp, qi - kp + BLK).astype(F32)
    valid = use_cur | (block_index > 0)
    return use_cur, dist, valid


def _block_diag(cur, prev, group):
    lane = lax.broadcasted_iota(jnp.int32, cur.shape, 1)

    def halves(t):
        other = pltpu.roll(t, 64, 1)
        lo, hi = (t, other) if group == 0 else (other, t)
        return jnp.where(lane < 64, lo, 0.0), jnp.where(lane >= 64, hi, 0.0)

    return jnp.concatenate(halves(cur) + halves(prev), axis=0).astype(BF16)


def _merge(s4, use_cur):
    return (jnp.where(use_cur, s4[:, 0:BLK], s4[:, 2 * BLK:3 * BLK]),
            jnp.where(use_cur, s4[:, BLK:2 * BLK], s4[:, 3 * BLK:4 * BLK]))


def _split(a, b, use_cur):
    return jnp.concatenate([jnp.where(use_cur, a, 0.0), jnp.where(use_cur, b, 0.0),
                            jnp.where(use_cur, 0.0, a), jnp.where(use_cur, 0.0, b)], axis=1)


def _softmax_head(s, head, sink, dist, valid):
    sc = jnp.where(valid, s - SLOPES[head] * dist, -jnp.inf)
    m = jnp.maximum(jnp.max(sc, axis=-1, keepdims=True), sink)
    p = jnp.exp(sc - m)
    es = jnp.exp(sink - m)
    inv = 1.0 / (jnp.sum(p, axis=-1, keepdims=True) + es)
    return p * inv, es * inv


def _attn_operands(q_ref, kvc_ref, kvp_ref):
    groups = range(N_HEADS // HEADS_PER_KV)
    kbd = [_block_diag(kvc_ref[:, 0:D_KV], kvp_ref[:, 0:D_KV], g) for g in groups]
    vbd = [_block_diag(kvc_ref[:, D_KV:2 * D_KV], kvp_ref[:, D_KV:2 * D_KV], g) for g in groups]
    qps = [(q_ref[:, j * 128:(j + 1) * 128] * SCALE).astype(BF16) for j in range(N_HEADS // 2)]
    return qps, kbd, vbd


def _attn_forward(q_ref, kvc_ref, kvp_ref, ga_ref, sink_ref, gn_ref, geometry, ya_ref, oa_ref, pr_ref, sp_ref):
    use_cur, dist, valid = geometry
    _, kbd, vbd = operands = _attn_operands(q_ref, kvc_ref, kvp_ref)
    yield
    scores = []
    for j, qp in enumerate(operands[0]):
        scores += _merge(_nt(qp, kbd[j // 4]), use_cur)
    yield
    sinks = [sink_ref[0, h] for h in range(N_HEADS)]
    scores = [jnp.where(valid, s - SLOPES[h] * dist, -jnp.inf) for h, s in enumerate(scores)]
    maxes = [jnp.maximum(jnp.max(s, axis=-1, keepdims=True), sinks[h]) for h, s in enumerate(scores)]
    yield
    exps = [jnp.exp(s - m) for s, m in zip(scores, maxes)]
    sink_exps = [jnp.exp(sinks[h] - m) for h, m in enumerate(maxes)]
    yield
    invs = [1.0 / (jnp.sum(e, axis=-1, keepdims=True) + se) for e, se in zip(exps, sink_exps)]
    probs = [e * inv for e, inv in zip(exps, invs)]
    pr_ref[...] = jnp.concatenate(probs, axis=1).astype(BF16)
    lane = lax.broadcasted_iota(jnp.int32, (BLK, 128), 1)
    sp_ref[...] = sum(jnp.where(lane == h, se * inv, 0.0) for h, (se, inv) in enumerate(zip(sink_exps, invs)))
    yield
    p4s = [_split(probs[2 * j], probs[2 * j + 1], use_cur).astype(BF16) for j in range(N_HEADS // 2)]
    ya = jnp.concatenate([_nn(p4, vbd[j // 4]) for j, p4 in enumerate(p4s)], axis=1)
    ya_ref[...] = ya
    yield
    silu, _ = _silu_and_grad(ga_ref[...])
    oa_ref[...] = (ya * _rstd(ya) * gn_ref[...] * silu).astype(BF16)


def _kv_specs(n_blocks, reverse):
    def blk(i):
        return (n_blocks - 1 - i) if reverse else i
    cur = pl.BlockSpec((BLK, 2 * D_KV), lambda i: (blk(i), 0))
    prev = pl.BlockSpec((BLK, 2 * D_KV), lambda i: (jnp.maximum(blk(i) - 1, 0), 0))
    return cur, prev


def _out_proj_loss(x, oc, oa, wo, norm_final, target, pc, conv_w, norm_conv_out):
    seq = x.shape[0]
    tile = TOK_TILE
    n_tiles = seq // tile

    def body(x_ref, oc_ref, oa_ref, wo_ref, gf_ref, t_ref, pc_ref, hcc_ref, hcu_ref, cw_ref, gn_ref,
             dx2_ref, doa_ref, gwo_ref, dpc_ref, small_ref, loss_acc, zbuf, dbuf):
        step = pl.program_id(0)

        def small_add(row, value):
            small_ref[row:row + 1, :] += jnp.sum(value, axis=0, keepdims=True)

        @pl.when(step == 0)
        def _():
            gwo_ref[...] = jnp.zeros_like(gwo_ref)
            small_ref[...] = jnp.zeros_like(small_ref)
            loss_acc[...] = jnp.zeros_like(loss_acc)
            dbuf[tile:tile + 8, :] = jnp.zeros((8, D_CONV), F32)

        oc, oa = oc_ref[...], oa_ref[...]
        x2 = x_ref[...] + _nn(oc, wo_ref[0:D_CONV, :]) + _nn(oa, wo_ref[D_CONV:D_MIX, :])
        r = _rstd(x2)
        xhat = x2 * r
        err = xhat * gf_ref[...] - t_ref[...]
        loss_acc[...] += jnp.sum(err * err, axis=0, keepdims=True) * (0.5 / D_MODEL)
        dy = err * (1.0 / D_MODEL)
        small_add(SMALL_NORM_FINAL, dy * xhat)
        dx2 = _rms_bwd(dy * gf_ref[...], xhat, r)
        dx2_ref[...] = dx2
        db = dx2.astype(BF16)
        do = _nt(db, wo_ref[0:D_CONV, :])
        doa_ref[...] = _nt(db, wo_ref[D_CONV:D_MIX, :])
        gwo_ref[0:D_CONV, :] += _tn(oc, db)
        gwo_ref[D_CONV:D_MIX, :] += _tn(oa, db)

        is_first_tile = step == n_tiles - 1
        zbuf[0:8, :] = jnp.where(is_first_tile, 0.0, hcc_ref[...] * hcu_ref[...])
        cb, cc, cu, z, z1, z2, conv = _conv_core(pc_ref, zbuf, cw_ref)
        silu, dsilu = _silu_and_grad(pc_ref[:, 3 * D_CONV:4 * D_CONV])
        yc = cb * conv
        rc = _rstd(yc)
        chat = yc * rc
        dn = do * silu
        dpc_ref[:, 3 * D_CONV:4 * D_CONV] = (do * (chat * gn_ref[...]) * dsilu).astype(BF16)
        small_add(SMALL_NORM_CONV, dn * chat)
        dyc = _rms_bwd(dn * gn_ref[...], chat, rc)
        dpc_ref[:, 0:D_CONV] = (dyc * conv).astype(BF16)
        dconv = dyc * cb
        small_add(SMALL_CONV_W, dconv * z2)
        small_add(SMALL_CONV_W + 1, dconv * z1)
        small_add(SMALL_CONV_W + 2, dconv * z)
        dbuf[0:tile, :] = dconv
        dz = cw_ref[2:3, :] * dconv + cw_ref[1:2, :] * dbuf[1:tile + 1, :] + cw_ref[0:1, :] * dbuf[2:tile + 2, :]
        dpc_ref[:, D_CONV:2 * D_CONV] = (dz * cu).astype(BF16)
        dpc_ref[:, 2 * D_CONV:3 * D_CONV] = (dz * cc).astype(BF16)
        dbuf[tile:tile + 8, :] = dbuf[0:8, :]

        @pl.when(step == n_tiles - 1)
        def _():
            lane = lax.broadcasted_iota(jnp.int32, (1, D_MODEL), 1)
            small_ref[SMALL_MISC:SMALL_MISC + 1, :] = jnp.where(lane == SMALL_LOSS_LANE, jnp.sum(loss_acc[...]), 0.0)

    def rev(i):
        return n_tiles - 1 - i

    def row(width):
        return pl.BlockSpec((tile, width), lambda i: (rev(i), 0))

    def halo(col_block):
        return pl.BlockSpec((8, D_CONV), lambda i: (jnp.maximum(rev(i) * (tile // 8) - 1, 0), col_block))

    def const(shape):
        return pl.BlockSpec(shape, lambda i: (0, 0))

    return pl.pallas_call(
        body, name="out_proj_loss", grid=(n_tiles,),
        out_shape=(jax.ShapeDtypeStruct((seq, D_MODEL), F32), jax.ShapeDtypeStruct((seq, D_ATTN), F32),
                   jax.ShapeDtypeStruct((D_MIX, D_MODEL), F32), jax.ShapeDtypeStruct((seq, D_PC), BF16),
                   jax.ShapeDtypeStruct((SMALL_ROWS, D_MODEL), F32)),
        in_specs=[row(D_MODEL), row(D_CONV), row(D_ATTN), _resident(wo.shape), _resident((1, D_MODEL)), row(D_MODEL),
                  row(D_PC), halo(1), halo(2), _resident(conv_w.shape), _resident((1, D_CONV))],
        out_specs=(row(D_MODEL), row(D_ATTN), const((D_MIX, D_MODEL)), row(D_PC), const((SMALL_ROWS, D_MODEL))),
        scratch_shapes=[pltpu.VMEM((1, D_MODEL), F32), pltpu.VMEM((tile + 8, D_CONV), F32),
                        pltpu.VMEM((tile + 8, D_CONV), F32)],
        compiler_params=_params(62, ("arbitrary",)),
    )(x, oc, oa, wo, norm_final, target, pc, pc, pc, conv_w, norm_conv_out)


def _attn_bwd(q, kv, ga, ya, doa, probs, sink_probs, norm_attn_out, gwo, small):
    seq = q.shape[0]
    n_blocks = seq // BLK
    stage_steps = (0, n_blocks // 4, n_blocks // 2, (3 * n_blocks) // 4, n_blocks - 1)

    def body(q_ref, kvc_ref, kvp_ref, ga_ref, ya_ref, doa_ref, pr_ref, sp_ref, gn_ref, gwo_ref, small_ref,
             dqg_ref, small_out, gwo_sh, gna_ref, gs_ref, carry, dya_buf, *rs_scratch):
        step = pl.program_id(0)

        @pl.when(step == 0)
        def _():
            gna_ref[...] = jnp.zeros_like(gna_ref)
            gs_ref[...] = jnp.zeros_like(gs_ref)
            carry[...] = jnp.zeros_like(carry)

        ya = ya_ref[...]
        r = _rstd(ya)
        xhat = ya * r
        silu, dsilu = _silu_and_grad(ga_ref[...])
        do = doa_ref[...]
        dn = do * silu
        dqg_ref[:, D_ATTN:2 * D_ATTN] = (do * (xhat * gn_ref[...]) * dsilu).astype(BF16)
        gna_ref[...] += jnp.sum(dn * xhat, axis=0, keepdims=True)
        dya_buf[...] = _rms_bwd(dn * gn_ref[...], xhat, r).astype(BF16)

        use_cur = _band_geometry(n_blocks - 1 - step)[0]
        lane = lax.broadcasted_iota(jnp.int32, (BLK, 128), 1)

        def fold(bd):
            return (jnp.where(lane < 64, bd[0:BLK], 0.0) + jnp.where(lane >= 64, bd[BLK:2 * BLK], 0.0),
                    jnp.where(lane < 64, bd[2 * BLK:3 * BLK], 0.0) + jnp.where(lane >= 64, bd[3 * BLK:4 * BLK], 0.0))

        pairs = range(N_HEADS // 2)
        qps, kbd, vbd = _attn_operands(q_ref, kvc_ref, kvp_ref)
        probs = [pr_ref[:, h * BLK:(h + 1) * BLK].astype(F32) for h in range(N_HEADS)]
        dyps = [dya_buf[:, j * 128:(j + 1) * 128] for j in pairs]
        dps = []
        for j in pairs:
            dps += _merge(_nt(dyps[j], vbd[j // 4]), use_cur)
        deltas = [jnp.sum(p * dp, axis=-1, keepdims=True) for p, dp in zip(probs, dps)]
        dss = [p * (dp - delta) for p, dp, delta in zip(probs, dps, deltas)]
        delta_lanes = sum(jnp.where(lane == h, deltas[h], 0.0) for h in range(N_HEADS))
        gs_ref[...] -= jnp.sum(sp_ref[...] * delta_lanes, axis=0, keepdims=True)
        ds4s = [_split(dss[2 * j], dss[2 * j + 1], use_cur).astype(BF16) for j in pairs]
        p4s = [_split(probs[2 * j], probs[2 * j + 1], use_cur).astype(BF16) for j in pairs]
        dqg_ref[:, 0:D_ATTN] = jnp.concatenate([_nn(ds4s[j], kbd[j // 4]) * SCALE for j in pairs], axis=1).astype(BF16)
        sums = []
        for group in range(N_HEADS // HEADS_PER_KV):
            acc = [jnp.zeros((BLK, 128), F32) for _ in range(4)]
            for j in range(group * 4, group * 4 + 4):
                for slot, part in enumerate(fold(_tn(ds4s[j], qps[j])) + fold(_tn(p4s[j], dyps[j]))):
                    acc[slot] = acc[slot] + part
            sums.append([a + pltpu.roll(a, 64, 1) for a in acc])
        dk_cur, dk_prev, dv_cur, dv_prev = (jnp.where(lane < 64, a, b) for a, b in zip(sums[0], sums[1]))
        dqg_ref[:, 2 * D_ATTN:2 * D_ATTN + 2 * D_KV] = (jnp.concatenate([dk_cur, dv_cur], axis=1) + carry[...]).astype(BF16)
        carry[...] = jnp.concatenate([dk_prev, dv_prev], axis=1)

        @pl.when(step == n_blocks - 1)
        def _():
            small_out[...] = small_ref[...]
            small_out[SMALL_NORM_ATTN:SMALL_NORM_ATTN + 1, :] = gna_ref[...]
            small_out[SMALL_MISC:SMALL_MISC + 1, 0:128] = small_ref[SMALL_MISC:SMALL_MISC + 1, 0:128] + gs_ref[...]

        for at, stage in zip(stage_steps, _rs_wout_stages(gwo_ref, gwo_sh, *rs_scratch)):
            pl.when(step == at)(stage)

    row = pl.BlockSpec((BLK, D_ATTN), lambda i: (n_blocks - 1 - i, 0))
    kv_cur, kv_prev = _kv_specs(n_blocks, reverse=True)
    any_spec = pl.BlockSpec(memory_space=pl.ANY)
    return pl.pallas_call(
        body, name="attn_bwd", grid=(n_blocks,),
        out_shape=(jax.ShapeDtypeStruct((seq, D_QG), BF16), pltpu.HBM(small.shape, F32),
                   pltpu.HBM((gwo.shape[0] // N_CHIPS, gwo.shape[1]), F32)),
        in_specs=[row, kv_cur, kv_prev, row, row, row,
                  pl.BlockSpec((BLK, N_HEADS * BLK), lambda i: (n_blocks - 1 - i, 0)),
                  pl.BlockSpec((BLK, 128), lambda i: (n_blocks - 1 - i, 0)), _resident((1, D_ATTN)),
                  any_spec, _resident(small.shape)],
        out_specs=(pl.BlockSpec((BLK, D_QG), lambda i: (n_blocks - 1 - i, 0)),
                   pl.BlockSpec(small.shape, lambda i: (0, 0)), any_spec),
        scratch_shapes=[pltpu.VMEM((1, D_ATTN), F32), pltpu.VMEM((1, 128), F32),
                        pltpu.VMEM((BLK, 2 * D_KV), F32), pltpu.VMEM((BLK, D_ATTN), BF16)] + _rs_wout_scratch(gwo.shape),
        compiler_params=_params(44, ("arbitrary",)),
    )(q, kv, kv, ga, ya, doa, probs, sink_probs, norm_attn_out, gwo, small)


GBLK = 256
GSUB = 64
PAIR_RING = 4
LAG_PAIR, LAG_HOP1, LAG_HOP2 = 1, 7, 14


def _bwd_in(dpc, dqg, h, wt, x, norm_in, dx2, small):
    seq = x.shape[0]
    n_blk = D_IN_PROJ // GBLK
    per_chip = n_blk // N_CHIPS
    n_slots = (n_blk + 1) // 2
    n_sub = GBLK // GSUB
    chip_rows = D_IN_PROJ // N_CHIPS
    tile = TOK_TILE
    n_tiles = seq // tile
    n_steps = n_blk + max(n_tiles, LAG_HOP2)
    chunk = min(seq, 512)
    blk_q, blk_kv, blk_ga = ROW_Q // GBLK, ROW_KV // GBLK, ROW_GA // GBLK

    def block_of(i):
        k = i % N_CHIPS
        robin = per_chip * ((k % 2) * 2 + k // 2) + i // N_CHIPS
        if isinstance(i, int):
            return robin if i < per_chip * N_CHIPS else i
        return jnp.where(i < per_chip * N_CHIPS, robin, i)

    def owner_of(i):
        return (i // N_CHIPS) % 2

    def slot_of(i):
        return (i // (2 * N_CHIPS)) * N_CHIPS + i % N_CHIPS

    def body(dpc_ref, dqg_ref, wt_ref, h_ref, x_ref, g_ref, dx2_ref, small_ref, gx_ref, small_sum, gwt_sh,
             dh_acc, gni, keep, pbuf, xbuf, land, land2, small_land,
             pair_send, pair_recv, h1_send, h1_recv, h2_send, h2_recv, sw_send, sw_recv, sm_send, sm_recv, out_sem):
        step = pl.program_id(0)
        x_i, y_i, c = lax.axis_index("x"), lax.axis_index("y"), lax.axis_index("c")
        me = 4 * x_i + 2 * y_i + c
        j = 2 * x_i + y_i
        pa = (_xor(x_i, 1 - c), _xor(y_i, c), c)
        pb = (_xor(x_i, c), _xor(y_i, 1 - c), c)
        sib = (x_i, y_i, 1 - c)
        ja = 2 * pa[0] + pa[1]
        jb = 2 * pb[0] + pb[1]
        jd = 3 - j

        def remote(src, dst, send, recv, to):
            return pltpu.make_async_remote_copy(src_ref=src, dst_ref=dst, send_sem=send, recv_sem=recv,
                                                device_id=to, device_id_type=MESH)

        def piece(ref, slot, u, n):
            return ref.at[slot, pl.ds(u * GSUB, n * GSUB), :]

        def chip_rows_at(ref, local, n):
            return ref.at[pl.ds(pl.multiple_of(local, GSUB), n * GSUB), :]

        def pair_copy(i):
            slot = slot_of(i)
            return remote(pbuf.at[i % PAIR_RING], land.at[slot], pair_send.at[slot], pair_recv.at[slot], sib)

        def h1_copy(slot, u, n):
            k = slot * n_sub + u
            return remote(piece(xbuf, slot, u, n), piece(xbuf, slot, u, n), h1_send.at[k], h1_recv.at[k], pa)

        def h2_copy(slot, u, n, local):
            k = slot * n_sub + u
            return remote(piece(xbuf, slot, u, n), chip_rows_at(land2, local, n), h2_send.at[k], h2_recv.at[k], pb)

        def sw_copy(slot, u, n, local):
            k = slot * n_sub + u
            return remote(piece(keep, slot, u, n), chip_rows_at(gwt_sh, local, n), sw_send.at[k], sw_recv.at[k], sib)

        def owned(i):
            return (i >= 0) & (i < n_blk) & (owner_of(i) == c)

        def chip_of(blk, u):
            row = blk * GBLK + u * GSUB
            chip = row // chip_rows
            return chip, row - chip * chip_rows

        def pieces(blk):
            first, local = chip_of(blk, 0)
            whole = first == chip_of(blk, n_sub - 1)[0]
            if isinstance(blk, int):
                return [(True, 0, n_sub, first, local)] if whole else [(True, u, 1) + chip_of(blk, u) for u in range(n_sub)]
            return [(whole, 0, n_sub, first, local)] + [(jnp.logical_not(whole), u, 1) + chip_of(blk, u) for u in range(n_sub)]

        @pl.when(step == 0)
        def _():
            dh_acc[...] = jnp.zeros_like(dh_acc)
            gni[...] = jnp.zeros_like(gni)

        @pl.when(step < n_blk)
        def _():
            from_pc = block_of(step) < blk_q
            block = _tn(jnp.where(from_pc, dpc_ref[...], dqg_ref[...]), h_ref[...])
            for t in range(0, seq, chunk):
                d = jnp.where(from_pc, dpc_ref[t:t + chunk, :], dqg_ref[t:t + chunk, :])
                dh_acc[t:t + chunk, :] += _nn(d, wt_ref[...])

            @pl.when(owner_of(step) == c)
            def _():
                keep[slot_of(step)] = block

            @pl.when(owner_of(step) != c)
            def _():
                @pl.when(step >= 2 * PAIR_RING)
                def _():
                    pair_copy(step - 2 * PAIR_RING).wait_send()
                pbuf[step % PAIR_RING] = block.astype(BF16)
                pair_copy(step).start()

        i1 = step - LAG_PAIR

        @pl.when(owned(i1))
        def _():
            slot = slot_of(i1)
            pair_copy(i1).wait_recv()
            _accumulate(keep.at[slot], land.at[slot])
            for cond, u, n, chip, _ in pieces(block_of(i1)):
                @pl.when(cond & ((chip == ja) | (chip == jd)))
                def _(u=u, n=n):
                    _cast_rows(piece(keep, slot, u, n), piece(xbuf, slot, u, n))
                    h1_copy(slot, u, n).start()

        i2 = step - LAG_HOP1

        @pl.when(owned(i2))
        def _():
            slot = slot_of(i2)
            for cond, u, n, chip, local in pieces(block_of(i2)):
                @pl.when(cond & ((chip == j) | (chip == jb)))
                def _(u=u, n=n, chip=chip, local=local):
                    h1_copy(slot, u, n).wait_recv()
                    _accumulate(piece(keep, slot, u, n), piece(xbuf, slot, u, n))

                    @pl.when(chip == jb)
                    def _():
                        _cast_rows(piece(keep, slot, u, n), piece(xbuf, slot, u, n))
                        h2_copy(slot, u, n, local).start()

                @pl.when(cond & ((chip == ja) | (chip == jd)))
                def _(u=u, n=n):
                    h1_copy(slot, u, n).wait_send()

        i3 = step - LAG_HOP2

        @pl.when(owned(i3))
        def _():
            slot = slot_of(i3)
            for cond, u, n, chip, local in pieces(block_of(i3)):
                @pl.when(cond & (chip == j))
                def _(u=u, n=n, local=local):
                    h2_copy(slot, u, n, local).wait_recv()
                    _accumulate(piece(keep, slot, u, n), chip_rows_at(land2, local, n))
                    mine = pltpu.make_async_copy(piece(keep, slot, u, n), chip_rows_at(gwt_sh, local, n), out_sem.at[0])
                    mine.start()
                    sw_copy(slot, u, n, local).start()
                    mine.wait()

                @pl.when(cond & (chip == jb))
                def _(u=u, n=n, local=local):
                    h2_copy(slot, u, n, local).wait_send()

        e = step - n_blk

        @pl.when((e >= 0) & (e < n_tiles))
        def _():
            dh = dh_acc[pl.ds(pl.multiple_of(e * tile, tile), tile), :]
            xv = x_ref[...]
            r = _rstd(xv)
            xhat = xv * r
            gni[...] += jnp.sum(dh * xhat, axis=0, keepdims=True)
            gx_ref[...] = _rms_bwd(dh * g_ref[...], xhat, r) + dx2_ref[...]

        @pl.when(step == n_steps - 1)
        def _():
            small_land[me] = small_ref[...]
            small_land[me, SMALL_NORM_IN:SMALL_NORM_IN + 1, :] = gni[...]
            others = [(dx, dy, dc) for dx in (0, 1) for dy in (0, 1) for dc in (0, 1)][1:]
            sends = [remote(small_land.at[me], small_land.at[me], sm_send.at[k], sm_recv.at[k],
                            (_xor(x_i, dx), _xor(y_i, dy), _xor(c, dc))) for k, (dx, dy, dc) in enumerate(others)]
            for cp in sends:
                cp.start()
            for i in range(n_blk):
                if i + 2 * PAIR_RING >= n_blk:
                    @pl.when(owner_of(i) != c)
                    def _(i=i):
                        pair_copy(i).wait_send()
                for _, u, n, chip, local in pieces(block_of(i)):
                    @pl.when((j == chip) & (c == owner_of(i)))
                    def _(i=i, u=u, n=n, local=local):
                        sw_copy(slot_of(i), u, n, local).wait_send()

                    @pl.when((j == chip) & (c != owner_of(i)))
                    def _(i=i, u=u, n=n, local=local):
                        sw_copy(slot_of(i), u, n, local).wait_recv()
            for cp in sends:
                cp.wait_recv()
            total = small_land[0]
            for dev in range(1, 8):
                total = total + small_land[dev]
            small_sum[...] = total
            for cp in sends:
                cp.wait_send()

    def blk_at(i):
        return block_of(jnp.clip(i, 0, n_blk - 1))

    last_pc_step = max(i for i in range(n_blk) if block_of(i) < blk_q)

    def next_block(i, in_pc):
        i = jnp.clip(i, 0, n_blk - 1)
        step = jnp.full_like(i, last_pc_step if in_pc else n_blk - 1)
        for ahead in reversed(range(N_CHIPS)):
            cand = jnp.minimum(i + ahead, n_blk - 1)
            step = jnp.where((block_of(cand) < blk_q) == in_pc, cand, step)
        return block_of(step)

    def dqg_block(i):
        b = next_block(i, False)
        q_blk = jnp.clip(b - blk_q, 0, blk_kv - blk_q - 1)
        ga_blk = (D_ATTN // GBLK) + jnp.clip(b - blk_ga, 0, n_blk - blk_ga - 1)
        return jnp.where(b < blk_kv, q_blk, jnp.where(b == blk_kv, 2 * D_ATTN // GBLK, ga_blk))

    def tok(i):
        return (jnp.clip(i - n_blk, 0, n_tiles - 1), 0)

    n_piece = n_slots * n_sub
    dma = pltpu.SemaphoreType.DMA
    return pl.pallas_call(
        body, name="bwd_in", grid=(n_steps,),
        out_shape=(jax.ShapeDtypeStruct((seq, D_MODEL), F32), jax.ShapeDtypeStruct(small.shape, F32),
                   jax.ShapeDtypeStruct((chip_rows, D_MODEL), F32)),
        in_specs=[pl.BlockSpec((seq, GBLK), lambda i: (0, next_block(i, True))),
                  pl.BlockSpec((seq, GBLK), lambda i: (0, dqg_block(i))),
                  pl.BlockSpec((GBLK, D_MODEL), lambda i: (blk_at(i), 0)),
                  _resident(h.shape),
                  pl.BlockSpec((tile, D_MODEL), tok), _resident((1, D_MODEL)), pl.BlockSpec((tile, D_MODEL), tok),
                  _resident(small.shape)],
        out_specs=(pl.BlockSpec((tile, D_MODEL), tok), pl.BlockSpec(small.shape, lambda i: (0, 0)),
                   pl.BlockSpec(memory_space=pl.ANY)),
        scratch_shapes=[pltpu.VMEM((seq, D_MODEL), F32), pltpu.VMEM((1, D_MODEL), F32),
                        pltpu.VMEM((n_slots, GBLK, D_MODEL), F32), pltpu.VMEM((PAIR_RING, GBLK, D_MODEL), BF16),
                        pltpu.VMEM((n_slots, GBLK, D_MODEL), BF16), pltpu.VMEM((n_slots, GBLK, D_MODEL), BF16),
                        pltpu.VMEM((chip_rows, D_MODEL), BF16), pltpu.VMEM((8,) + small.shape, F32),
                        dma((n_slots,)), dma((n_slots,)), dma((n_piece,)), dma((n_piece,)), dma((n_piece,)),
                        dma((n_piece,)), dma((n_piece,)), dma((n_piece,)), dma((7,)), dma((7,)), dma((1,))],
        compiler_params=_params(62, ("arbitrary",)),
    )(dpc, dqg, wt, h, x, norm_in, dx2, small)


def _accumulate(dst_ref, src_ref, rows=16):
    def step(i, carry):
        sl = pl.ds(pl.multiple_of(i * rows, rows), rows)
        dst_ref[sl, :] = dst_ref[sl, :] + src_ref[sl, :].astype(F32)
        return carry
    lax.fori_loop(0, dst_ref.shape[0] // rows, step, 0)


def _rs_wout_scratch(gwo_shape):
    o_half, width = gwo_shape[0] // N_CHIPS // 2, gwo_shape[1]
    return [pltpu.VMEM((4, o_half, width), F32), pltpu.VMEM((4, o_half, width), BF16),
            pltpu.VMEM((4, o_half, width), BF16), pltpu.VMEM((2, o_half, width), BF16),
            pltpu.VMEM((o_half, width), BF16),
            pltpu.SemaphoreType.DMA((8,)), pltpu.SemaphoreType.DMA((8,)), pltpu.SemaphoreType.DMA((4,))]


def _rs_wout_stages(gwo_ref, gwo_sh, acc_o, sb_o, r1o, r2o, r3o, send_sems, recv_sems, local_sems):
    o_rows = gwo_ref.shape[0] // N_CHIPS
    o_half = o_rows // 2
    x, y, c = lax.axis_index("x"), lax.axis_index("y"), lax.axis_index("c")
    j = 2 * x + y
    pa = (_xor(x, 1 - c), _xor(y, c), c)
    pb = (_xor(x, c), _xor(y, 1 - c), c)
    sib = (x, y, 1 - c)
    ja = 2 * pa[0] + pa[1]
    jb = 2 * pb[0] + pb[1]
    jd = 3 - j
    order = (ja, jd, jb, j)
    sib_order = (jb, jd, ja, j)

    def rcopy(k, src, dst, to):
        return pltpu.make_async_remote_copy(src_ref=src, dst_ref=dst, send_sem=send_sems.at[k],
                                            recv_sem=recv_sems.at[k], device_id=to, device_id_type=MESH)

    def o_rows_of(chip, half):
        return gwo_ref.at[pl.ds(pl.multiple_of(chip * o_rows + half * o_half, 8), o_half), :]

    def load_all(chips, half):
        cps = [pltpu.make_async_copy(o_rows_of(chip, half), acc_o.at[s], local_sems.at[s]) for s, chip in enumerate(chips)]
        for cp in cps:
            cp.start()
        return cps

    def pair_send(s):
        return rcopy(s, sb_o.at[s], r1o.at[s], sib)

    def out_half(half):
        return gwo_sh.at[pl.ds(pl.multiple_of(half * o_half, 8), o_half), :]

    hop1 = [rcopy(4 + s, sb_o.at[s], r2o.at[s], pa) for s in range(2)]
    hop2 = rcopy(6, sb_o.at[2], r3o, pb)
    swap = rcopy(7, acc_o.at[3], out_half(c), sib)
    mine = pltpu.make_async_copy(acc_o.at[3], out_half(c), local_sems.at[0])

    def resend(s, copy):
        pair_send(s).wait_send()
        _cast_rows(acc_o.at[s], sb_o.at[s])
        copy.start()

    def stage_pair():
        for s, cp in enumerate(load_all(sib_order, 1 - c)):
            cp.wait()
            _cast_rows(acc_o.at[s], sb_o.at[s])
            pair_send(s).start()

    def stage_hop1():
        for s, cp in enumerate(load_all(order, c)):
            cp.wait()
            pair_send(s).wait_recv()
            _accumulate(acc_o.at[s], r1o.at[s])
            if s < 2:
                resend(s, hop1[s])

    def stage_hop2():
        hop1[1].wait_recv()
        _accumulate(acc_o.at[2], r2o.at[1])
        resend(2, hop2)
        hop1[0].wait_recv()
        _accumulate(acc_o.at[3], r2o.at[0])

    def stage_final():
        hop2.wait_recv()
        _accumulate(acc_o.at[3], r3o)
        mine.start()
        swap.start()

    def stage_drain():
        rcopy(7, acc_o.at[3], out_half(1 - c), sib).wait_recv()
        for cp in [pair_send(3)] + hop1 + [hop2, swap]:
            cp.wait_send()
        mine.wait()

    return stage_pair, stage_hop1, stage_hop2, stage_final, stage_drain


def _adamw(name, w, g, m, v, rows):
    def body(w_ref, g_ref, m_ref, v_ref, go_ref, d_ref, nm_ref, nv_ref):
        _adamw_update(w_ref, g_ref[...], m_ref, v_ref, go_ref, d_ref, nm_ref, nv_ref)

    spec = pl.BlockSpec((rows, w.shape[1]), lambda i: (i, 0))
    shape = jax.ShapeDtypeStruct(w.shape, F32)
    return pl.pallas_call(
        body, name="adamw_" + name, grid=(w.shape[0] // rows,),
        out_shape=(shape,) * 4, in_specs=[spec] * 4, out_specs=(spec,) * 4,
        compiler_params=_params(32, ("arbitrary",)),
    )(w, g, m, v)


def _adamw_update(w_ref, gv, m_ref, v_ref, go_ref, d_ref, nm_ref, nv_ref, at=...):
    go_ref[at] = gv
    nm = ADAM_B1 * m_ref[at] + (1.0 - ADAM_B1) * gv
    nv = ADAM_B2 * v_ref[at] + (1.0 - ADAM_B2) * (gv * gv)
    m_hat = nm / (1.0 - ADAM_B1 ** ADAM_STEP)
    v_hat = nv / (1.0 - ADAM_B2 ** ADAM_STEP)
    d_ref[at] = -ADAM_LR * (m_hat / (jnp.sqrt(v_hat) + ADAM_EPS) + ADAM_WD * w_ref[at])
    nm_ref[at] = nm
    nv_ref[at] = nv


def _adamw_small(chip, small_sum, weights, grads_of, ms, vs):
    n = len(weights)

    def body(chip_ref, small_ref, *refs):
        ins, outs, loss_ref = refs[:3 * n], refs[3 * n:-1], refs[-1]
        for k in range(n):
            w_ref, m_ref, v_ref = ins[3 * k:3 * k + 3]
            for at, gv in grads_of[k](small_ref, chip_ref):
                _adamw_update(w_ref, gv, m_ref, v_ref, *outs[4 * k:4 * k + 4], at=at)
        loss_ref[...] = small_ref[SMALL_MISC:SMALL_MISC + 1, SMALL_LOSS_LANE:SMALL_LOSS_LANE + 1]

    flat = [a for group in zip(weights, ms, vs) for a in group]
    vmem = pl.BlockSpec(memory_space=pltpu.VMEM)
    out = pl.pallas_call(
        body, name="adamw_small",
        out_shape=tuple(jax.ShapeDtypeStruct(w.shape, F32) for w in weights for _ in range(4))
        + (jax.ShapeDtypeStruct((1, 1), F32),),
        in_specs=[pl.BlockSpec(memory_space=pltpu.SMEM)] + [vmem] * (1 + 3 * n), out_specs=(vmem,) * (4 * n + 1),
    )(chip, small_sum, *flat)
    return [tuple(out[4 * k:4 * k + 4]) for k in range(n)], out[-1][0, 0]


def kernel(x, norm_in, w_in, conv_w, attn_sinks, norm_conv_out, norm_attn_out, w_out, norm_final, loss_target, m_norm_in, m_w_in, m_conv_w, m_attn_sinks, m_norm_conv_out, m_norm_attn_out, m_w_out, m_norm_final, v_norm_in, v_w_in, v_conv_w, v_attn_sinks, v_norm_conv_out, v_norm_attn_out, v_w_out, v_norm_final):
    chip = 2 * lax.axis_index("x") + lax.axis_index("y")
    xs, target = x[0], loss_target[0]
    norm_final2 = norm_final.reshape(1, D_MODEL)
    w_in_t, m_w_in_t, v_w_in_t = w_in[0].T, m_w_in[0].T, v_w_in[0].T

    def from_hbm(*arrays):
        return tuple(pltpu.with_memory_space_constraint(a, pltpu.HBM) for a in arrays)

    h, pc, q, kv, ga, oc, ya, oa, probs, sink_probs, wt, cw, wo = _fwd_in(
        xs, norm_in, w_in_t, w_out[0], conv_w.transpose(1, 0, 2), norm_conv_out, attn_sinks, norm_attn_out)
    dx2, doa, gwo, dpc, small = _out_proj_loss(xs, oc, oa, wo, norm_final2, target, pc, cw, norm_conv_out)
    q, kv, ga, ya, doa, probs, gwo, small = from_hbm(q, kv, ga, ya, doa, probs, gwo, small)
    dqg, small, gwo_sh = _attn_bwd(q, kv, ga, ya, doa, probs, sink_probs, norm_attn_out, gwo, small)
    grad_x, small_sum, gwt_sh = _bwd_in(dpc, dqg, h, wt, xs, norm_in, dx2, small)

    up_w_in = tuple(o.T[None] for o in _adamw("w_in", *from_hbm(w_in_t, gwt_sh, m_w_in_t, v_w_in_t), 200))
    up_w_out = tuple(o[None] for o in _adamw("w_out", *from_hbm(w_out[0], gwo_sh, m_w_out[0], v_w_out[0]), 128))

    def row_of(r):
        return lambda small_ref, chip_ref: [(..., small_ref[r:r + 1, :])]

    def sink_lanes(small_ref, chip_ref):
        return [(..., small_ref[SMALL_MISC:SMALL_MISC + 1, 0:N_HEADS])]

    def conv_taps(small_ref, chip_ref):
        width = D_CONV // N_CHIPS
        cols = pl.ds(pl.multiple_of(chip_ref[0] * width, width), width)
        return [(k, small_ref[pl.ds(SMALL_CONV_W + k, 1), cols]) for k in range(conv_w.shape[1])]

    def small_view(a):
        return a.transpose(1, 0, 2) if a.ndim == 3 else a.reshape(-1, a.shape[-1])

    small_w = (norm_in, conv_w, attn_sinks, norm_conv_out, norm_attn_out, norm_final)
    small_m = (m_norm_in, m_conv_w, m_attn_sinks, m_norm_conv_out, m_norm_attn_out, m_norm_final)
    small_v = (v_norm_in, v_conv_w, v_attn_sinks, v_norm_conv_out, v_norm_attn_out, v_norm_final)
    small_g = (row_of(SMALL_NORM_IN), conv_taps, sink_lanes, row_of(SMALL_NORM_CONV), row_of(SMALL_NORM_ATTN),
               row_of(SMALL_NORM_FINAL))
    w_views, m_views, v_views = (tuple(small_view(a) for a in group) for group in (small_w, small_m, small_v))
    up_small, loss = _adamw_small(chip.astype(jnp.int32).reshape(1), small_sum, w_views, small_g, m_views, v_views)
    up_small = [tuple(o.transpose(1, 0, 2) if w.ndim == 3 else o.reshape(w.shape) for o in up)
                for up, w in zip(up_small, small_w)]
    up_norm_in, up_conv_w, up_sinks, up_norm_conv, up_norm_attn, up_norm_final = up_small
    updates = (up_norm_in, up_w_in, up_conv_w, up_sinks, up_norm_conv, up_norm_attn, up_w_out, up_norm_final)
    grads_out, deltas, new_m, new_v = zip(*updates)
    return (loss, grad_x[None], *grads_out, *deltas, *new_m, *new_v)
```

```python
import jax
import jax.numpy as jnp
from jax import lax
from jax.experimental import pallas as pl
from jax.experimental.pallas import tpu as pltpu

F32 = jnp.float32
BF16 = jnp.bfloat16
MESH = pl.DeviceIdType.MESH

D_MODEL = 1024
D_CONV = 1024
D_ATTN = 1024
D_KV = 128
D_QG = 2 * D_ATTN + 2 * D_KV
D_MIX = D_CONV + D_ATTN
D_PC = 4 * D_CONV
D_IN_PROJ = D_PC + 2 * D_ATTN + 2 * D_KV
ROW_Q = D_PC
ROW_KV = ROW_Q + D_ATTN
ROW_GA = ROW_KV + 2 * D_KV
N_HEADS = 16
HEAD_DIM = 64
HEADS_PER_KV = 8
BLK = 128
N_CHIPS = 4
RMS_EPS = 1e-5
SCALE = HEAD_DIM ** -0.5
SLOPES = tuple(2.0 ** (-8.0 * (h + 1) / N_HEADS) for h in range(N_HEADS))

ADAM_LR, ADAM_B1, ADAM_B2, ADAM_EPS, ADAM_WD, ADAM_STEP = 0.001, 0.9, 0.999, 1e-08, 0.01, 10

SMALL_ROWS = 8
SMALL_NORM_IN, SMALL_NORM_CONV, SMALL_NORM_ATTN, SMALL_NORM_FINAL, SMALL_CONV_W, SMALL_MISC = 0, 1, 2, 3, 4, 7
SMALL_LOSS_LANE = N_HEADS

TOK_TILE = 256
PC_PIECE = 512
MIB = 1 << 20


def _params(vmem_mib, semantics=None):
    return pltpu.CompilerParams(dimension_semantics=semantics, vmem_limit_bytes=vmem_mib * MIB)


def _nn(a, b):
    return jnp.dot(a, b, preferred_element_type=F32)


def _nt(a, b):
    return lax.dot_general(a, b, (((1,), (1,)), ((), ())), preferred_element_type=F32)


def _tn(a, b):
    return lax.dot_general(a, b, (((0,), (0,)), ((), ())), preferred_element_type=F32)


def _rstd(v):
    return lax.rsqrt(jnp.mean(v * v, axis=-1, keepdims=True) + RMS_EPS)


def _rms_bwd(g, xhat, rstd):
    return rstd * (g - xhat * jnp.mean(g * xhat, axis=-1, keepdims=True))


def _silu_and_grad(g):
    s = jax.nn.sigmoid(g)
    return g * s, s * (1.0 + g * (1.0 - s))


def _resident(shape):
    return pl.BlockSpec(shape, lambda *_: (0,) * len(shape), pipeline_mode=pl.Buffered(1))


def _xor(a, b):
    return a + b - 2 * a * b


def _cast_rows(src_ref, dst_ref, rows=32):
    def step(i, carry):
        sl = pl.ds(pl.multiple_of(i * rows, rows), rows)
        dst_ref[sl, :] = src_ref[sl, :].astype(dst_ref.dtype)
        return carry
    lax.fori_loop(0, src_ref.shape[0] // rows, step, 0)


def _ag_scratch(shard_shape):
    rows, width = shard_shape
    return [pltpu.VMEM((rows, width), F32), pltpu.VMEM((rows, width), BF16), pltpu.VMEM((3, rows // 2, width), BF16),
            pltpu.SemaphoreType.DMA((6,)), pltpu.SemaphoreType.DMA((6,)), pltpu.SemaphoreType.DMA((4,))]


def _ag_stages(sh_ref, out, f32_buf, own, land, send_sems, recv_sems, local_sems):
    rows = sh_ref.shape[0]
    half = rows // 2
    x, y, c = lax.axis_index("x"), lax.axis_index("y"), lax.axis_index("c")
    j = 2 * x + y
    p1 = (_xor(x, c), _xor(y, 1 - c), c)
    p2 = (_xor(x, 1 - c), _xor(y, c), c)
    sib = (x, y, 1 - c)
    j1 = 2 * p1[0] + p1[1]
    j2 = 2 * p2[0] + p2[1]
    j3 = 3 - j

    def rows_of(chip, hf):
        return out.at[pl.ds(pl.multiple_of(chip * rows + hf * half, 16), half), :]

    def rcopy(k, src, dst, to):
        return pltpu.make_async_remote_copy(src_ref=src, dst_ref=dst, send_sem=send_sems.at[k],
                                            recv_sem=recv_sems.at[k], device_id=to, device_id_type=MESH)

    my_half = own.at[pl.ds(pl.multiple_of(c * half, 16), half), :]
    hop1 = rcopy(0, my_half, land.at[0], p1)
    hop2_own = rcopy(1, my_half, land.at[1], p2)
    hop2_fwd = rcopy(2, land.at[0], land.at[2], p2)
    swaps = [rcopy(3 + s, land.at[s], rows_of(chip, c), sib) for s, chip in enumerate((j1, j2, j3))]
    keeps = [pltpu.make_async_copy(land.at[s], rows_of(chip, c), local_sems.at[1 + s]) for s, chip in enumerate((j1, j2, j3))]
    load = pltpu.make_async_copy(sh_ref, f32_buf, local_sems.at[0])
    own_out = pltpu.make_async_copy(own, out.at[pl.ds(pl.multiple_of(j * rows, 16), rows), :], local_sems.at[0])

    def stage_send():
        load.start()
        load.wait()
        _cast_rows(f32_buf, own)
        own_out.start()
        hop1.start()

    def stage_forward():
        hop1.wait_recv()
        hop2_own.start()
        hop2_fwd.start()
        swaps[0].start()
        keeps[0].start()

    def stage_publish():
        hop2_own.wait_recv()
        swaps[1].start()
        keeps[1].start()
        hop2_fwd.wait_recv()
        swaps[2].start()
        keeps[2].start()

    def stage_drain():
        for s, chip in enumerate((j2, j1, j3)):
            rcopy(3 + s, my_half, rows_of(chip, 1 - c), sib).wait_recv()
        for cp in [hop1, hop2_own, hop2_fwd] + swaps:
            cp.wait_send()
        for cp in [own_out] + keeps:
            cp.wait()

    return stage_send, stage_forward, stage_publish, stage_drain


AG_CAST_ROWS = 400


def _gather_resident(sh_ref, out, f32_buf, send_sems, recv_sems, local_sems):
    rows = sh_ref.shape[0]
    half = rows // 2
    x, y, c = lax.axis_index("x"), lax.axis_index("y"), lax.axis_index("c")
    j = 2 * x + y
    p1 = (_xor(x, c), _xor(y, 1 - c), c)
    p2 = (_xor(x, 1 - c), _xor(y, c), c)
    sib = (x, y, 1 - c)
    j1 = 2 * p1[0] + p1[1]
    j2 = 2 * p2[0] + p2[1]
    j3 = 3 - j

    def rows_of(chip, hf):
        return out.at[pl.ds(pl.multiple_of(chip * rows + hf * half, 16), half), :]

    def send(k, chip, to):
        return pltpu.make_async_remote_copy(src_ref=rows_of(chip, c), dst_ref=rows_of(chip, c), send_sem=send_sems.at[k],
                                            recv_sem=recv_sems.at[k], device_id=to, device_id_type=MESH)

    per_half = half // AG_CAST_ROWS

    def cast_own(chunk):
        lo = pl.multiple_of(chunk * AG_CAST_ROWS, 16)
        load = pltpu.make_async_copy(sh_ref.at[pl.ds(lo, AG_CAST_ROWS), :], f32_buf, local_sems.at[0])
        load.start()
        load.wait()
        _cast_rows(f32_buf, out.at[pl.ds(pl.multiple_of(j * rows + lo, 16), AG_CAST_ROWS), :], rows=16)

    for k in range(per_half):
        cast_own(c * per_half + k)
    hop1 = send(0, j, p1)
    hop1.start()
    for k in range(per_half):
        cast_own((1 - c) * per_half + k)
    hop1.wait_recv()
    sends = [hop1, send(1, j, p2), send(2, j1, p2), send(3, j1, sib)]
    for cp in sends[1:]:
        cp.start()
    sends[1].wait_recv()
    sends.append(send(4, j2, sib))
    sends[-1].start()
    sends[2].wait_recv()
    sends.append(send(5, j3, sib))
    sends[-1].start()
    for k in (3, 4, 5):
        send(k, j, sib).wait_recv()
    for cp in sends:
        cp.wait_send()


def _fwd_in(x, norm_in, wt_sh, wo_sh, cw_sh, norm_conv_out, sinks, norm_attn_out):
    seq = x.shape[0]
    tile = TOK_TILE
    n_tiles = seq // tile
    stage_steps = (0, n_tiles // 4, (5 * n_tiles) // 8, n_tiles - 1)
    blocks = tile // BLK
    cw_cols = cw_sh.shape[-1]

    def body(x_ref, g_ref, wtsh_ref, wo_ref, cwsh_ref, gn_ref, sink_ref, gna_ref,
             h_ref, pc_ref, q_ref, kv_ref, ga_ref, oc_ref, ya_ref, oa_ref, pr_ref, sp_ref, wt_out, cw_ref, wo_out,
             zbuf, kv_last, wt_ref, f32_buf, cw_land, wt_send, wt_recv, wt_local, cw_send, cw_recv, cw_local, *ag_scratch):
        step = pl.program_id(0)
        to_hbm = pltpu.make_async_copy(wt_ref, wt_out, wt_local.at[0])

        @pl.when(step == 0)
        def _():
            zbuf[0:8, :] = jnp.zeros((8, D_CONV), F32)
            kv_last[...] = jnp.zeros_like(kv_last)
            x_i, y_i, c = lax.axis_index("x"), lax.axis_index("y"), lax.axis_index("c")
            j = 2 * x_i + y_i
            p1 = (_xor(x_i, c), _xor(y_i, 1 - c), c)
            p2 = (_xor(x_i, 1 - c), _xor(y_i, c), c)
            j1 = 2 * p1[0] + p1[1]

            def cw_copy(k, src, chip, to):
                return pltpu.make_async_remote_copy(src_ref=src, dst_ref=cw_land.at[chip], send_sem=cw_send.at[k],
                                                    recv_sem=cw_recv.at[k], device_id=to, device_id_type=MESH)

            mine = pltpu.make_async_copy(cwsh_ref, cw_land.at[j], cw_local.at[0])
            mine.start()
            first = cw_copy(0, cwsh_ref, j, p1)
            first.start()
            _gather_resident(wtsh_ref, wt_ref, f32_buf, wt_send, wt_recv, wt_local)
            to_hbm.start()
            first.wait_recv()
            second = [cw_copy(1, cwsh_ref, j, p2), cw_copy(2, cw_land.at[j1], j1, p2)]
            for cp in second:
                cp.start()
            for cp in second:
                cp.wait_recv()
            for cp in [first] + second:
                cp.wait_send()
            mine.wait()
            for chip in range(N_CHIPS):
                for tap in range(cw_sh.shape[0]):
                    cw_ref[tap:tap + 1, chip * cw_cols:(chip + 1) * cw_cols] = cw_land[chip, tap]

        stages = _ag_stages(wo_ref, wo_out, *ag_scratch)
        for at, stage in zip(stage_steps[:-1], stages[:-1]):
            pl.when(step == at)(stage)

        def attention(b):
            rows = pl.ds(b * BLK, BLK)
            kv_prev = kv_last if b == 0 else kv_ref.at[pl.ds((b - 1) * BLK, BLK), :]
            return _attn_forward(q_ref.at[rows, :], kv_ref.at[rows, :], kv_prev, ga_ref.at[rows, :], sink_ref, gna_ref,
                                 _band_geometry(step * blocks + b), ya_ref.at[rows, :], oa_ref.at[rows, :],
                                 pr_ref.at[rows, :], sp_ref.at[rows, :])

        xv = x_ref[...]
        h = (xv * _rstd(xv) * g_ref[...]).astype(BF16)
        h_ref[...] = h
        q_ref[...] = _nt(h, wt_ref[ROW_Q:ROW_KV, :])
        kv_ref[...] = _nt(h, wt_ref[ROW_KV:ROW_GA, :])
        ga_ref[...] = _nt(h, wt_ref[ROW_GA:D_IN_PROJ, :])
        attention_blocks = [attention(b) for b in range(blocks)]
        for lo in range(0, D_PC, PC_PIECE):
            pc_ref[:, lo:lo + PC_PIECE] = _nt(h, wt_ref[lo:lo + PC_PIECE, :])
            for stages_of_block in attention_blocks:
                next(stages_of_block, None)
        for stages_of_block in attention_blocks:
            for _ in stages_of_block:
                pass
        kv_last[...] = kv_ref[tile - BLK:tile, :]

        cb, _, _, _, _, _, conv = _conv_core(pc_ref, zbuf, cw_ref)
        yc = cb * conv
        silu, _ = _silu_and_grad(pc_ref[:, 3 * D_CONV:4 * D_CONV])
        oc_ref[...] = (yc * _rstd(yc) * gn_ref[...] * silu).astype(BF16)
        zbuf[0:8, :] = zbuf[tile:tile + 8, :]

        @pl.when(step == stage_steps[-1])
        def _():
            stages[-1]()
            to_hbm.wait()

    def row(width):
        return pl.BlockSpec((tile, width), lambda i: (i, 0))

    any_spec = pl.BlockSpec(memory_space=pl.ANY)
    dma = pltpu.SemaphoreType.DMA
    wt_shape = (N_CHIPS * wt_sh.shape[0], wt_sh.shape[1])
    cw_shape = (cw_sh.shape[0], N_CHIPS * cw_cols)
    return pl.pallas_call(
        body, name="fwd_in", grid=(n_tiles,),
        out_shape=(jax.ShapeDtypeStruct((seq, D_MODEL), BF16), jax.ShapeDtypeStruct((seq, D_PC), F32),
                   jax.ShapeDtypeStruct((seq, D_ATTN), F32), jax.ShapeDtypeStruct((seq, 2 * D_KV), F32),
                   jax.ShapeDtypeStruct((seq, D_ATTN), F32), jax.ShapeDtypeStruct((seq, D_CONV), BF16),
                   pltpu.HBM((seq, D_ATTN), F32), pltpu.HBM((seq, D_ATTN), BF16),
                   pltpu.HBM((seq, N_HEADS * BLK), BF16), pltpu.HBM((seq, 128), F32),
                   pltpu.HBM(wt_shape, BF16), jax.ShapeDtypeStruct(cw_shape, F32),
                   jax.ShapeDtypeStruct((N_CHIPS * wo_sh.shape[0], wo_sh.shape[1]), BF16)),
        in_specs=[row(D_MODEL), _resident((1, D_MODEL)), any_spec, any_spec, any_spec, _resident((1, D_CONV)),
                  pl.BlockSpec(memory_space=pltpu.SMEM), _resident((1, D_ATTN))],
        out_specs=(row(D_MODEL), row(D_PC), row(D_ATTN), row(2 * D_KV), row(D_ATTN), row(D_CONV), row(D_ATTN),
                   row(D_ATTN), row(N_HEADS * BLK), row(128), any_spec, pl.BlockSpec(cw_shape, lambda i: (0, 0)),
                   any_spec),
        scratch_shapes=[pltpu.VMEM((tile + 8, D_CONV), F32), pltpu.VMEM((BLK, 2 * D_KV), F32),
                        pltpu.VMEM(wt_shape, BF16), pltpu.VMEM((AG_CAST_ROWS, wt_sh.shape[1]), F32),
                        pltpu.VMEM((N_CHIPS,) + cw_sh.shape, F32),
                        dma((6,)), dma((6,)), dma((1,)), dma((3,)), dma((3,)), dma((1,))] + _ag_scratch(wo_sh.shape),
        compiler_params=_params(62, ("arbitrary",)),
    )(x, norm_in, wt_sh, wo_sh, cw_sh, norm_conv_out, sinks, norm_attn_out)


def _conv_core(pc_ref, zbuf, cw_ref):
    tile = pc_ref.shape[0]
    cb = pc_ref[:, 0:D_CONV]
    cc = pc_ref[:, D_CONV:2 * D_CONV]
    cu = pc_ref[:, 2 * D_CONV:3 * D_CONV]
    z = cc * cu
    zbuf[8:tile + 8, :] = z
    z1 = zbuf[7:tile + 7, :]
    z2 = zbuf[6:tile + 6, :]
    conv = cw_ref[0:1, :] * z2 + cw_ref[1:2, :] * z1 + cw_ref[2:3, :] * z
    return cb, cc, cu, z, z1, z2, conv


def _band_geometry(block_index):
    qi = lax.broadcasted_iota(jnp.int32, (BLK, BLK), 0)
    kp = lax.broadcasted_iota(jnp.int32, (BLK, BLK), 1)
    use_cur = kp <= qi
    dist = jnp.where(use_cur, qi - kp, qi - kp + BLK).astype(F32)
    valid = use_cur | (block_index > 0)
    return use_cur, dist, valid


def _block_diag(cur, prev, group):
    lane = lax.broadcasted_iota(jnp.int32, cur.shape, 1)

    def halves(t):
        other = pltpu.roll(t, 64, 1)
        lo, hi = (t, other) if group == 0 else (other, t)
        return jnp.where(lane < 64, lo, 0.0), jnp.where(lane >= 64, hi, 0.0)

    return jnp.concatenate(halves(cur) + halves(prev), axis=0).astype(BF16)


def _merge(s4, use_cur):
    return (jnp.where(use_cur, s4[:, 0:BLK], s4[:, 2 * BLK:3 * BLK]),
            jnp.where(use_cur, s4[:, BLK:2 * BLK], s4[:, 3 * BLK:4 * BLK]))


def _split(a, b, use_cur):
    return jnp.concatenate([jnp.where(use_cur, a, 0.0), jnp.where(use_cur, b, 0.0),
                            jnp.where(use_cur, 0.0, a), jnp.where(use_cur, 0.0, b)], axis=1)


def _softmax_head(s, head, sink, dist, valid):
    sc = jnp.where(valid, s - SLOPES[head] * dist, -jnp.inf)
    m = jnp.maximum(jnp.max(sc, axis=-1, keepdims=True), sink)
    p = jnp.exp(sc - m)
    es = jnp.exp(sink - m)
    inv = 1.0 / (jnp.sum(p, axis=-1, keepdims=True) + es)
    return p * inv, es * inv


def _attn_operands(q_ref, kvc_ref, kvp_ref):
    groups = range(N_HEADS // HEADS_PER_KV)
    kbd = [_block_diag(kvc_ref[:, 0:D_KV], kvp_ref[:, 0:D_KV], g) for g in groups]
    vbd = [_block_diag(kvc_ref[:, D_KV:2 * D_KV], kvp_ref[:, D_KV:2 * D_KV], g) for g in groups]
    qps = [(q_ref[:, j * 128:(j + 1) * 128] * SCALE).astype(BF16) for j in range(N_HEADS // 2)]
    return qps, kbd, vbd


def _attn_forward(q_ref, kvc_ref, kvp_ref, ga_ref, sink_ref, gn_ref, geometry, ya_ref, oa_ref, pr_ref, sp_ref):
    use_cur, dist, valid = geometry
    _, kbd, vbd = operands = _attn_operands(q_ref, kvc_ref, kvp_ref)
    yield
    scores = []
    for j, qp in enumerate(operands[0]):
        scores += _merge(_nt(qp, kbd[j // 4]), use_cur)
    yield
    sinks = [sink_ref[0, h] for h in range(N_HEADS)]
    scores = [jnp.where(valid, s - SLOPES[h] * dist, -jnp.inf) for h, s in enumerate(scores)]
    maxes = [jnp.maximum(jnp.max(s, axis=-1, keepdims=True), sinks[h]) for h, s in enumerate(scores)]
    yield
    exps = [jnp.exp(s - m) for s, m in zip(scores, maxes)]
    sink_exps = [jnp.exp(sinks[h] - m) for h, m in enumerate(maxes)]
    yield
    invs = [1.0 / (jnp.sum(e, axis=-1, keepdims=True) + se) for e, se in zip(exps, sink_exps)]
    probs = [e * inv for e, inv in zip(exps, invs)]
    pr_ref[...] = jnp.concatenate(probs, axis=1).astype(BF16)
    lane = lax.broadcasted_iota(jnp.int32, (BLK, 128), 1)
    sp_ref[...] = sum(jnp.where(lane == h, se * inv, 0.0) for h, (se, inv) in enumerate(zip(sink_exps, invs)))
    yield
    p4s = [_split(probs[2 * j], probs[2 * j + 1], use_cur).astype(BF16) for j in range(N_HEADS // 2)]
    ya = jnp.concatenate([_nn(p4, vbd[j // 4]) for j, p4 in enumerate(p4s)], axis=1)
    ya_ref[...] = ya
    yield
    silu, _ = _silu_and_grad(ga_ref[...])
    oa_ref[...] = (ya * _rstd(ya) * gn_ref[...] * silu).astype(BF16)


def _kv_specs(n_blocks, reverse):
    def blk(i):
        return (n_blocks - 1 - i) if reverse else i
    cur = pl.BlockSpec((BLK, 2 * D_KV), lambda i: (blk(i), 0))
    prev = pl.BlockSpec((BLK, 2 * D_KV), lambda i: (jnp.maximum(blk(i) - 1, 0), 0))
    return cur, prev


def _out_proj_loss(x, oc, oa, wo, norm_final, target, pc, conv_w, norm_conv_out):
    seq = x.shape[0]
    tile = TOK_TILE
    n_tiles = seq // tile

    def body(x_ref, oc_ref, oa_ref, wo_ref, gf_ref, t_ref, pc_ref, hcc_ref, hcu_ref, cw_ref, gn_ref,
             dx2_ref, doa_ref, gwo_ref, dpc_ref, small_ref, loss_acc, zbuf, dbuf):
        step = pl.program_id(0)

        def small_add(row, value):
            small_ref[row:row + 1, :] += jnp.sum(value, axis=0, keepdims=True)

        @pl.when(step == 0)
        def _():
            gwo_ref[...] = jnp.zeros_like(gwo_ref)
            small_ref[...] = jnp.zeros_like(small_ref)
            loss_acc[...] = jnp.zeros_like(loss_acc)
            dbuf[tile:tile + 8, :] = jnp.zeros((8, D_CONV), F32)

        oc, oa = oc_ref[...], oa_ref[...]
        x2 = x_ref[...] + _nn(oc, wo_ref[0:D_CONV, :]) + _nn(oa, wo_ref[D_CONV:D_MIX, :])
        r = _rstd(x2)
        xhat = x2 * r
        err = xhat * gf_ref[...] - t_ref[...]
        loss_acc[...] += jnp.sum(err * err, axis=0, keepdims=True) * (0.5 / D_MODEL)
        dy = err * (1.0 / D_MODEL)
        small_add(SMALL_NORM_FINAL, dy * xhat)
        dx2 = _rms_bwd(dy * gf_ref[...], xhat, r)
        dx2_ref[...] = dx2
        db = dx2.astype(BF16)
        do = _nt(db, wo_ref[0:D_CONV, :])
        doa_ref[...] = _nt(db, wo_ref[D_CONV:D_MIX, :])
        gwo_ref[0:D_CONV, :] += _tn(oc, db)
        gwo_ref[D_CONV:D_MIX, :] += _tn(oa, db)

        is_first_tile = step == n_tiles - 1
        zbuf[0:8, :] = jnp.where(is_first_tile, 0.0, hcc_ref[...] * hcu_ref[...])
        cb, cc, cu, z, z1, z2, conv = _conv_core(pc_ref, zbuf, cw_ref)
        silu, dsilu = _silu_and_grad(pc_ref[:, 3 * D_CONV:4 * D_CONV])
        yc = cb * conv
        rc = _rstd(yc)
        chat = yc * rc
        dn = do * silu
        dpc_ref[:, 3 * D_CONV:4 * D_CONV] = (do * (chat * gn_ref[...]) * dsilu).astype(BF16)
        small_add(SMALL_NORM_CONV, dn * chat)
        dyc = _rms_bwd(dn * gn_ref[...], chat, rc)
        dpc_ref[:, 0:D_CONV] = (dyc * conv).astype(BF16)
        dconv = dyc * cb
        small_add(SMALL_CONV_W, dconv * z2)
        small_add(SMALL_CONV_W + 1, dconv * z1)
        small_add(SMALL_CONV_W + 2, dconv * z)
        dbuf[0:tile, :] = dconv
        dz = cw_ref[2:3, :] * dconv + cw_ref[1:2, :] * dbuf[1:tile + 1, :] + cw_ref[0:1, :] * dbuf[2:tile + 2, :]
        dpc_ref[:, D_CONV:2 * D_CONV] = (dz * cu).astype(BF16)
        dpc_ref[:, 2 * D_CONV:3 * D_CONV] = (dz * cc).astype(BF16)
        dbuf[tile:tile + 8, :] = dbuf[0:8, :]

        @pl.when(step == n_tiles - 1)
        def _():
            lane = lax.broadcasted_iota(jnp.int32, (1, D_MODEL), 1)
            small_ref[SMALL_MISC:SMALL_MISC + 1, :] = jnp.where(lane == SMALL_LOSS_LANE, jnp.sum(loss_acc[...]), 0.0)

    def rev(i):
        return n_tiles - 1 - i

    def row(width):
        return pl.BlockSpec((tile, width), lambda i: (rev(i), 0))

    def halo(col_block):
        return pl.BlockSpec((8, D_CONV), lambda i: (jnp.maximum(rev(i) * (tile // 8) - 1, 0), col_block))

    def const(shape):
        return pl.BlockSpec(shape, lambda i: (0, 0))

    return pl.pallas_call(
        body, name="out_proj_loss", grid=(n_tiles,),
        out_shape=(jax.ShapeDtypeStruct((seq, D_MODEL), F32), jax.ShapeDtypeStruct((seq, D_ATTN), F32),
                   jax.ShapeDtypeStruct((D_MIX, D_MODEL), F32), jax.ShapeDtypeStruct((seq, D_PC), BF16),
                   jax.ShapeDtypeStruct((SMALL_ROWS, D_MODEL), F32)),
        in_specs=[row(D_MODEL), row(D_CONV), row(D_ATTN), _resident(wo.shape), _resident((1, D_MODEL)), row(D_MODEL),
                  row(D_PC), halo(1), halo(2), _resident(conv_w.shape), _resident((1, D_CONV))],
        out_specs=(row(D_MODEL), row(D_ATTN), const((D_MIX, D_MODEL)), row(D_PC), const((SMALL_ROWS, D_MODEL))),
        scratch_shapes=[pltpu.VMEM((1, D_MODEL), F32), pltpu.VMEM((tile + 8, D_CONV), F32),
                        pltpu.VMEM((tile + 8, D_CONV), F32)],
        compiler_params=_params(62, ("arbitrary",)),
    )(x, oc, oa, wo, norm_final, target, pc, pc, pc, conv_w, norm_conv_out)


def _attn_bwd(q, kv, ga, ya, doa, probs, sink_probs, norm_attn_out, gwo, small):
    seq = q.shape[0]
    n_blocks = seq // BLK
    stage_steps = (0, n_blocks // 4, n_blocks // 2, (3 * n_blocks) // 4, n_blocks - 1)

    def body(q_ref, kvc_ref, kvp_ref, ga_ref, ya_ref, doa_ref, pr_ref, sp_ref, gn_ref, gwo_ref, small_ref,
             dqg_ref, small_out, gwo_sh, gna_ref, gs_ref, carry, dya_buf, *rs_scratch):
        step = pl.program_id(0)

        @pl.when(step == 0)
        def _():
            gna_ref[...] = jnp.zeros_like(gna_ref)
            gs_ref[...] = jnp.zeros_like(gs_ref)
            carry[...] = jnp.zeros_like(carry)

        ya = ya_ref[...]
        r = _rstd(ya)
        xhat = ya * r
        silu, dsilu = _silu_and_grad(ga_ref[...])
        do = doa_ref[...]
        dn = do * silu
        dqg_ref[:, D_ATTN:2 * D_ATTN] = (do * (xhat * gn_ref[...]) * dsilu).astype(BF16)
        gna_ref[...] += jnp.sum(dn * xhat, axis=0, keepdims=True)
        dya_buf[...] = _rms_bwd(dn * gn_ref[...], xhat, r).astype(BF16)

        use_cur = _band_geometry(n_blocks - 1 - step)[0]
        lane = lax.broadcasted_iota(jnp.int32, (BLK, 128), 1)

        def fold(bd):
            return (jnp.where(lane < 64, bd[0:BLK], 0.0) + jnp.where(lane >= 64, bd[BLK:2 * BLK], 0.0),
                    jnp.where(lane < 64, bd[2 * BLK:3 * BLK], 0.0) + jnp.where(lane >= 64, bd[3 * BLK:4 * BLK], 0.0))

        pairs = range(N_HEADS // 2)
        qps, kbd, vbd = _attn_operands(q_ref, kvc_ref, kvp_ref)
        probs = [pr_ref[:, h * BLK:(h + 1) * BLK].astype(F32) for h in range(N_HEADS)]
        dyps = [dya_buf[:, j * 128:(j + 1) * 128] for j in pairs]
        dps = []
        for j in pairs:
            dps += _merge(_nt(dyps[j], vbd[j // 4]), use_cur)
        deltas = [jnp.sum(p * dp, axis=-1, keepdims=True) for p, dp in zip(probs, dps)]
        dss = [p * (dp - delta) for p, dp, delta in zip(probs, dps, deltas)]
        delta_lanes = sum(jnp.where(lane == h, deltas[h], 0.0) for h in range(N_HEADS))
        gs_ref[...] -= jnp.sum(sp_ref[...] * delta_lanes, axis=0, keepdims=True)
        ds4s = [_split(dss[2 * j], dss[2 * j + 1], use_cur).astype(BF16) for j in pairs]
        p4s = [_split(probs[2 * j], probs[2 * j + 1], use_cur).astype(BF16) for j in pairs]
        dqg_ref[:, 0:D_ATTN] = jnp.concatenate([_nn(ds4s[j], kbd[j // 4]) * SCALE for j in pairs], axis=1).astype(BF16)
        sums = []
        for group in range(N_HEADS // HEADS_PER_KV):
            acc = [jnp.zeros((BLK, 128), F32) for _ in range(4)]
            for j in range(group * 4, group * 4 + 4):
                for slot, part in enumerate(fold(_tn(ds4s[j], qps[j])) + fold(_tn(p4s[j], dyps[j]))):
                    acc[slot] = acc[slot] + part
            sums.append([a + pltpu.roll(a, 64, 1) for a in acc])
        dk_cur, dk_prev, dv_cur, dv_prev = (jnp.where(lane < 64, a, b) for a, b in zip(sums[0], sums[1]))
        dqg_ref[:, 2 * D_ATTN:2 * D_ATTN + 2 * D_KV] = (jnp.concatenate([dk_cur, dv_cur], axis=1) + carry[...]).astype(BF16)
        carry[...] = jnp.concatenate([dk_prev, dv_prev], axis=1)

        @pl.when(step == n_blocks - 1)
        def _():
            small_out[...] = small_ref[...]
            small_out[SMALL_NORM_ATTN:SMALL_NORM_ATTN + 1, :] = gna_ref[...]
            small_out[SMALL_MISC:SMALL_MISC + 1, 0:128] = small_ref[SMALL_MISC:SMALL_MISC + 1, 0:128] + gs_ref[...]

        for at, stage in zip(stage_steps, _rs_wout_stages(gwo_ref, gwo_sh, *rs_scratch)):
            pl.when(step == at)(stage)

    row = pl.BlockSpec((BLK, D_ATTN), lambda i: (n_blocks - 1 - i, 0))
    kv_cur, kv_prev = _kv_specs(n_blocks, reverse=True)
    any_spec = pl.BlockSpec(memory_space=pl.ANY)
    return pl.pallas_call(
        body, name="attn_bwd", grid=(n_blocks,),
        out_shape=(jax.ShapeDtypeStruct((seq, D_QG), BF16), pltpu.HBM(small.shape, F32),
                   pltpu.HBM((gwo.shape[0] // N_CHIPS, gwo.shape[1]), F32)),
        in_specs=[row, kv_cur, kv_prev, row, row, row,
                  pl.BlockSpec((BLK, N_HEADS * BLK), lambda i: (n_blocks - 1 - i, 0)),
                  pl.BlockSpec((BLK, 128), lambda i: (n_blocks - 1 - i, 0)), _resident((1, D_ATTN)),
                  any_spec, _resident(small.shape)],
        out_specs=(pl.BlockSpec((BLK, D_QG), lambda i: (n_blocks - 1 - i, 0)),
                   pl.BlockSpec(small.shape, lambda i: (0, 0)), any_spec),
        scratch_shapes=[pltpu.VMEM((1, D_ATTN), F32), pltpu.VMEM((1, 128), F32),
                        pltpu.VMEM((BLK, 2 * D_KV), F32), pltpu.VMEM((BLK, D_ATTN), BF16)] + _rs_wout_scratch(gwo.shape),
        compiler_params=_params(44, ("arbitrary",)),
    )(q, kv, kv, ga, ya, doa, probs, sink_probs, norm_attn_out, gwo, small)


GBLK = 256
GSUB = 64
PAIR_RING = 4
LAG_PAIR, LAG_HOP1, LAG_HOP2 = 1, 4, 8


def _bwd_in(dpc, dqg, h, wt, x, norm_in, dx2, small):
    seq = x.shape[0]
    n_blk = D_IN_PROJ // GBLK
    per_chip = n_blk // N_CHIPS
    n_slots = (n_blk + 1) // 2
    n_sub = GBLK // GSUB
    chip_rows = D_IN_PROJ // N_CHIPS
    tile = TOK_TILE
    n_tiles = seq // tile
    n_steps = n_blk + max(n_tiles, LAG_HOP2)
    chunk = min(seq, 512)
    blk_q, blk_kv, blk_ga = ROW_Q // GBLK, ROW_KV // GBLK, ROW_GA // GBLK

    def block_of(i):
        k = i % N_CHIPS
        robin = per_chip * ((k % 2) * 2 + k // 2) + i // N_CHIPS
        if isinstance(i, int):
            return robin if i < per_chip * N_CHIPS else i
        return jnp.where(i < per_chip * N_CHIPS, robin, i)

    def owner_of(i):
        return (i // N_CHIPS) % 2

    def slot_of(i):
        return (i // (2 * N_CHIPS)) * N_CHIPS + i % N_CHIPS

    def body(dpc_ref, dqg_ref, wt_ref, h_ref, x_ref, g_ref, dx2_ref, small_ref, gx_ref, small_sum, gwt_sh,
             dh_acc, gni, keep, pbuf, xbuf, land, land2, small_land,
             pair_send, pair_recv, h1_send, h1_recv, h2_send, h2_recv, sw_send, sw_recv, sm_send, sm_recv, out_sem):
        step = pl.program_id(0)
        x_i, y_i, c = lax.axis_index("x"), lax.axis_index("y"), lax.axis_index("c")
        me = 4 * x_i + 2 * y_i + c
        j = 2 * x_i + y_i
        pa = (_xor(x_i, 1 - c), _xor(y_i, c), c)
        pb = (_xor(x_i, c), _xor(y_i, 1 - c), c)
        sib = (x_i, y_i, 1 - c)
        ja = 2 * pa[0] + pa[1]
        jb = 2 * pb[0] + pb[1]
        jd = 3 - j

        def remote(src, dst, send, recv, to):
            return pltpu.make_async_remote_copy(src_ref=src, dst_ref=dst, send_sem=send, recv_sem=recv,
                                                device_id=to, device_id_type=MESH)

        def piece(ref, slot, u, n):
            return ref.at[slot, pl.ds(u * GSUB, n * GSUB), :]

        def chip_rows_at(ref, local, n):
            return ref.at[pl.ds(pl.multiple_of(local, GSUB), n * GSUB), :]

        def pair_copy(i):
            slot = slot_of(i)
            return remote(pbuf.at[i % PAIR_RING], land.at[slot], pair_send.at[slot], pair_recv.at[slot], sib)

        def h1_copy(slot, u, n):
            k = slot * n_sub + u
            return remote(piece(xbuf, slot, u, n), piece(xbuf, slot, u, n), h1_send.at[k], h1_recv.at[k], pa)

        def h2_copy(slot, u, n, local):
            k = slot * n_sub + u
            return remote(piece(xbuf, slot, u, n), chip_rows_at(land2, local, n), h2_send.at[k], h2_recv.at[k], pb)

        def sw_copy(slot, u, n, local):
            k = slot * n_sub + u
            return remote(piece(keep, slot, u, n), chip_rows_at(gwt_sh, local, n), sw_send.at[k], sw_recv.at[k], sib)

        def owned(i):
            return (i >= 0) & (i < n_blk) & (owner_of(i) == c)

        def chip_of(blk, u):
            row = blk * GBLK + u * GSUB
            chip = row // chip_rows
            return chip, row - chip * chip_rows

        def pieces(blk):
            first, local = chip_of(blk, 0)
            whole = first == chip_of(blk, n_sub - 1)[0]
            if isinstance(blk, int):
                return [(True, 0, n_sub, first, local)] if whole else [(True, u, 1) + chip_of(blk, u) for u in range(n_sub)]
            return [(whole, 0, n_sub, first, local)] + [(jnp.logical_not(whole), u, 1) + chip_of(blk, u) for u in range(n_sub)]

        @pl.when(step == 0)
        def _():
            dh_acc[...] = jnp.zeros_like(dh_acc)
            gni[...] = jnp.zeros_like(gni)

        @pl.when(step < n_blk)
        def _():
            from_pc = block_of(step) < blk_q
            block = _tn(jnp.where(from_pc, dpc_ref[...], dqg_ref[...]), h_ref[...])
            for t in range(0, seq, chunk):
                d = jnp.where(from_pc, dpc_ref[t:t + chunk, :], dqg_ref[t:t + chunk, :])
                dh_acc[t:t + chunk, :] += _nn(d, wt_ref[...])

            @pl.when(owner_of(step) == c)
            def _():
                keep[slot_of(step)] = block

            @pl.when(owner_of(step) != c)
            def _():
                @pl.when(step >= 2 * PAIR_RING)
                def _():
                    pair_copy(step - 2 * PAIR_RING).wait_send()
                pbuf[step % PAIR_RING] = block.astype(BF16)
                pair_copy(step).start()

        i1 = step - LAG_PAIR

        @pl.when(owned(i1))
        def _():
            slot = slot_of(i1)
            pair_copy(i1).wait_recv()
            _accumulate(keep.at[slot], land.at[slot])
            for cond, u, n, chip, _ in pieces(block_of(i1)):
                @pl.when(cond & ((chip == ja) | (chip == jd)))
                def _(u=u, n=n):
                    _cast_rows(piece(keep, slot, u, n), piece(xbuf, slot, u, n))
                    h1_copy(slot, u, n).start()

        i2 = step - LAG_HOP1

        @pl.when(owned(i2))
        def _():
            slot = slot_of(i2)
            for cond, u, n, chip, local in pieces(block_of(i2)):
                @pl.when(cond & ((chip == j) | (chip == jb)))
                def _(u=u, n=n, chip=chip, local=local):
                    h1_copy(slot, u, n).wait_recv()
                    _accumulate(piece(keep, slot, u, n), piece(xbuf, slot, u, n))

                    @pl.when(chip == jb)
                    def _():
                        _cast_rows(piece(keep, slot, u, n), piece(xbuf, slot, u, n))
                        h2_copy(slot, u, n, local).start()

                @pl.when(cond & ((chip == ja) | (chip == jd)))
                def _(u=u, n=n):
                    h1_copy(slot, u, n).wait_send()

        i3 = step - LAG_HOP2

        @pl.when(owned(i3))
        def _():
            slot = slot_of(i3)
            for cond, u, n, chip, local in pieces(block_of(i3)):
                @pl.when(cond & (chip == j))
                def _(u=u, n=n, local=local):
                    h2_copy(slot, u, n, local).wait_recv()
                    _accumulate(piece(keep, slot, u, n), chip_rows_at(land2, local, n))
                    mine = pltpu.make_async_copy(piece(keep, slot, u, n), chip_rows_at(gwt_sh, local, n), out_sem.at[0])
                    mine.start()
                    sw_copy(slot, u, n, local).start()
                    mine.wait()

                @pl.when(cond & (chip == jb))
                def _(u=u, n=n, local=local):
                    h2_copy(slot, u, n, local).wait_send()

        e = step - n_blk

        @pl.when((e >= 0) & (e < n_tiles))
        def _():
            dh = dh_acc[pl.ds(pl.multiple_of(e * tile, tile), tile), :]
            xv = x_ref[...]
            r = _rstd(xv)
            xhat = xv * r
            gni[...] += jnp.sum(dh * xhat, axis=0, keepdims=True)
            gx_ref[...] = _rms_bwd(dh * g_ref[...], xhat, r) + dx2_ref[...]

        @pl.when(step == n_steps - 1)
        def _():
            small_land[me] = small_ref[...]
            small_land[me, SMALL_NORM_IN:SMALL_NORM_IN + 1, :] = gni[...]
            others = [(dx, dy, dc) for dx in (0, 1) for dy in (0, 1) for dc in (0, 1)][1:]
            sends = [remote(small_land.at[me], small_land.at[me], sm_send.at[k], sm_recv.at[k],
                            (_xor(x_i, dx), _xor(y_i, dy), _xor(c, dc))) for k, (dx, dy, dc) in enumerate(others)]
            for cp in sends:
                cp.start()
            for i in range(n_blk):
                if i + 2 * PAIR_RING >= n_blk:
                    @pl.when(owner_of(i) != c)
                    def _(i=i):
                        pair_copy(i).wait_send()
                for _, u, n, chip, local in pieces(block_of(i)):
                    @pl.when((j == chip) & (c == owner_of(i)))
                    def _(i=i, u=u, n=n, local=local):
                        sw_copy(slot_of(i), u, n, local).wait_send()

                    @pl.when((j == chip) & (c != owner_of(i)))
                    def _(i=i, u=u, n=n, local=local):
                        sw_copy(slot_of(i), u, n, local).wait_recv()
            for cp in sends:
                cp.wait_recv()
            total = small_land[0]
            for dev in range(1, 8):
                total = total + small_land[dev]
            small_sum[...] = total
            for cp in sends:
                cp.wait_send()

    def blk_at(i):
        return block_of(jnp.clip(i, 0, n_blk - 1))

    last_pc_step = max(i for i in range(n_blk) if block_of(i) < blk_q)

    def next_block(i, in_pc):
        i = jnp.clip(i, 0, n_blk - 1)
        step = jnp.full_like(i, last_pc_step if in_pc else n_blk - 1)
        for ahead in reversed(range(N_CHIPS)):
            cand = jnp.minimum(i + ahead, n_blk - 1)
            step = jnp.where((block_of(cand) < blk_q) == in_pc, cand, step)
        return block_of(step)

    def dqg_block(i):
        b = next_block(i, False)
        q_blk = jnp.clip(b - blk_q, 0, blk_kv - blk_q - 1)
        ga_blk = (D_ATTN // GBLK) + jnp.clip(b - blk_ga, 0, n_blk - blk_ga - 1)
        return jnp.where(b < blk_kv, q_blk, jnp.where(b == blk_kv, 2 * D_ATTN // GBLK, ga_blk))

    def tok(i):
        return (jnp.clip(i - n_blk, 0, n_tiles - 1), 0)

    n_piece = n_slots * n_sub
    dma = pltpu.SemaphoreType.DMA
    return pl.pallas_call(
        body, name="bwd_in", grid=(n_steps,),
        out_shape=(jax.ShapeDtypeStruct((seq, D_MODEL), F32), jax.ShapeDtypeStruct(small.shape, F32),
                   jax.ShapeDtypeStruct((chip_rows, D_MODEL), F32)),
        in_specs=[pl.BlockSpec((seq, GBLK), lambda i: (0, next_block(i, True))),
                  pl.BlockSpec((seq, GBLK), lambda i: (0, dqg_block(i))),
                  pl.BlockSpec((GBLK, D_MODEL), lambda i: (blk_at(i), 0)),
                  _resident(h.shape),
                  pl.BlockSpec((tile, D_MODEL), tok), _resident((1, D_MODEL)), pl.BlockSpec((tile, D_MODEL), tok),
                  _resident(small.shape)],
        out_specs=(pl.BlockSpec((tile, D_MODEL), tok), pl.BlockSpec(small.shape, lambda i: (0, 0)),
                   pl.BlockSpec(memory_space=pl.ANY)),
        scratch_shapes=[pltpu.VMEM((seq, D_MODEL), F32), pltpu.VMEM((1, D_MODEL), F32),
                        pltpu.VMEM((n_slots, GBLK, D_MODEL), F32), pltpu.VMEM((PAIR_RING, GBLK, D_MODEL), BF16),
                        pltpu.VMEM((n_slots, GBLK, D_MODEL), BF16), pltpu.VMEM((n_slots, GBLK, D_MODEL), BF16),
                        pltpu.VMEM((chip_rows, D_MODEL), BF16), pltpu.VMEM((8,) + small.shape, F32),
                        dma((n_slots,)), dma((n_slots,)), dma((n_piece,)), dma((n_piece,)), dma((n_piece,)),
                        dma((n_piece,)), dma((n_piece,)), dma((n_piece,)), dma((7,)), dma((7,)), dma((1,))],
        compiler_params=_params(62, ("arbitrary",)),
    )(dpc, dqg, wt, h, x, norm_in, dx2, small)


def _accumulate(dst_ref, src_ref, rows=16):
    def step(i, carry):
        sl = pl.ds(pl.multiple_of(i * rows, rows), rows)
        dst_ref[sl, :] = dst_ref[sl, :] + src_ref[sl, :].astype(F32)
        return carry
    lax.fori_loop(0, dst_ref.shape[0] // rows, step, 0)


def _rs_wout_scratch(gwo_shape):
    o_half, width = gwo_shape[0] // N_CHIPS // 2, gwo_shape[1]
    return [pltpu.VMEM((4, o_half, width), F32), pltpu.VMEM((4, o_half, width), BF16),
            pltpu.VMEM((4, o_half, width), BF16), pltpu.VMEM((2, o_half, width), BF16),
            pltpu.VMEM((o_half, width), BF16),
            pltpu.SemaphoreType.DMA((8,)), pltpu.SemaphoreType.DMA((8,)), pltpu.SemaphoreType.DMA((4,))]


def _rs_wout_stages(gwo_ref, gwo_sh, acc_o, sb_o, r1o, r2o, r3o, send_sems, recv_sems, local_sems):
    o_rows = gwo_ref.shape[0] // N_CHIPS
    o_half = o_rows // 2
    x, y, c = lax.axis_index("x"), lax.axis_index("y"), lax.axis_index("c")
    j = 2 * x + y
    pa = (_xor(x, 1 - c), _xor(y, c), c)
    pb = (_xor(x, c), _xor(y, 1 - c), c)
    sib = (x, y, 1 - c)
    ja = 2 * pa[0] + pa[1]
    jb = 2 * pb[0] + pb[1]
    jd = 3 - j
    order = (ja, jd, jb, j)
    sib_order = (jb, jd, ja, j)

    def rcopy(k, src, dst, to):
        return pltpu.make_async_remote_copy(src_ref=src, dst_ref=dst, send_sem=send_sems.at[k],
                                            recv_sem=recv_sems.at[k], device_id=to, device_id_type=MESH)

    def o_rows_of(chip, half):
        return gwo_ref.at[pl.ds(pl.multiple_of(chip * o_rows + half * o_half, 8), o_half), :]

    def load_all(chips, half):
        cps = [pltpu.make_async_copy(o_rows_of(chip, half), acc_o.at[s], local_sems.at[s]) for s, chip in enumerate(chips)]
        for cp in cps:
            cp.start()
        return cps

    def pair_send(s):
        return rcopy(s, sb_o.at[s], r1o.at[s], sib)

    def out_half(half):
        return gwo_sh.at[pl.ds(pl.multiple_of(half * o_half, 8), o_half), :]

    hop1 = [rcopy(4 + s, sb_o.at[s], r2o.at[s], pa) for s in range(2)]
    hop2 = rcopy(6, sb_o.at[2], r3o, pb)
    swap = rcopy(7, acc_o.at[3], out_half(c), sib)
    mine = pltpu.make_async_copy(acc_o.at[3], out_half(c), local_sems.at[0])

    def resend(s, copy):
        pair_send(s).wait_send()
        _cast_rows(acc_o.at[s], sb_o.at[s])
        copy.start()

    def stage_pair():
        for s, cp in enumerate(load_all(sib_order, 1 - c)):
            cp.wait()
            _cast_rows(acc_o.at[s], sb_o.at[s])
            pair_send(s).start()

    def stage_hop1():
        for s, cp in enumerate(load_all(order, c)):
            cp.wait()
            pair_send(s).wait_recv()
            _accumulate(acc_o.at[s], r1o.at[s])
            if s < 2:
                resend(s, hop1[s])

    def stage_hop2():
        hop1[1].wait_recv()
        _accumulate(acc_o.at[2], r2o.at[1])
        resend(2, hop2)
        hop1[0].wait_recv()
        _accumulate(acc_o.at[3], r2o.at[0])

    def stage_final():
        hop2.wait_recv()
        _accumulate(acc_o.at[3], r3o)
        mine.start()
        swap.start()

    def stage_drain():
        rcopy(7, acc_o.at[3], out_half(1 - c), sib).wait_recv()
        for cp in [pair_send(3)] + hop1 + [hop2, swap]:
            cp.wait_send()
        mine.wait()

    return stage_pair, stage_hop1, stage_hop2, stage_final, stage_drain


def _adamw(name, w, g, m, v, rows):
    def body(w_ref, g_ref, m_ref, v_ref, go_ref, d_ref, nm_ref, nv_ref):
        _adamw_update(w_ref, g_ref[...], m_ref, v_ref, go_ref, d_ref, nm_ref, nv_ref)

    spec = pl.BlockSpec((rows, w.shape[1]), lambda i: (i, 0))
    shape = jax.ShapeDtypeStruct(w.shape, F32)
    return pl.pallas_call(
        body, name="adamw_" + name, grid=(w.shape[0] // rows,),
        out_shape=(shape,) * 4, in_specs=[spec] * 4, out_specs=(spec,) * 4,
        compiler_params=_params(32, ("arbitrary",)),
    )(w, g, m, v)


def _adamw_update(w_ref, gv, m_ref, v_ref, go_ref, d_ref, nm_ref, nv_ref, at=...):
    go_ref[at] = gv
    nm = ADAM_B1 * m_ref[at] + (1.0 - ADAM_B1) * gv
    nv = ADAM_B2 * v_ref[at] + (1.0 - ADAM_B2) * (gv * gv)
    m_hat = nm / (1.0 - ADAM_B1 ** ADAM_STEP)
    v_hat = nv / (1.0 - ADAM_B2 ** ADAM_STEP)
    d_ref[at] = -ADAM_LR * (m_hat / (jnp.sqrt(v_hat) + ADAM_EPS) + ADAM_WD * w_ref[at])
    nm_ref[at] = nm
    nv_ref[at] = nv


def _adamw_small(chip, small_sum, weights, grads_of, ms, vs):
    n = len(weights)

    def body(chip_ref, small_ref, *refs):
        ins, outs, loss_ref = refs[:3 * n], refs[3 * n:-1], refs[-1]
        for k in range(n):
            w_ref, m_ref, v_ref = ins[3 * k:3 * k + 3]
            for at, gv in grads_of[k](small_ref, chip_ref):
                _adamw_update(w_ref, gv, m_ref, v_ref, *outs[4 * k:4 * k + 4], at=at)
        loss_ref[...] = small_ref[SMALL_MISC:SMALL_MISC + 1, SMALL_LOSS_LANE:SMALL_LOSS_LANE + 1]

    flat = [a for group in zip(weights, ms, vs) for a in group]
    vmem = pl.BlockSpec(memory_space=pltpu.VMEM)
    out = pl.pallas_call(
        body, name="adamw_small",
        out_shape=tuple(jax.ShapeDtypeStruct(w.shape, F32) for w in weights for _ in range(4))
        + (jax.ShapeDtypeStruct((1, 1), F32),),
        in_specs=[pl.BlockSpec(memory_space=pltpu.SMEM)] + [vmem] * (1 + 3 * n), out_specs=(vmem,) * (4 * n + 1),
    )(chip, small_sum, *flat)
    return [tuple(out[4 * k:4 * k + 4]) for k in range(n)], out[-1][0, 0]


def kernel(x, norm_in, w_in, conv_w, attn_sinks, norm_conv_out, norm_attn_out, w_out, norm_final, loss_target, m_norm_in, m_w_in, m_conv_w, m_attn_sinks, m_norm_conv_out, m_norm_attn_out, m_w_out, m_norm_final, v_norm_in, v_w_in, v_conv_w, v_attn_sinks, v_norm_conv_out, v_norm_attn_out, v_w_out, v_norm_final):
    chip = 2 * lax.axis_index("x") + lax.axis_index("y")
    xs, target = x[0], loss_target[0]
    norm_final2 = norm_final.reshape(1, D_MODEL)
    w_in_t, m_w_in_t, v_w_in_t = w_in[0].T, m_w_in[0].T, v_w_in[0].T

    def from_hbm(*arrays):
        return tuple(pltpu.with_memory_space_constraint(a, pltpu.HBM) for a in arrays)

    h, pc, q, kv, ga, oc, ya, oa, probs, sink_probs, wt, cw, wo = _fwd_in(
        xs, norm_in, w_in_t, w_out[0], conv_w.transpose(1, 0, 2), norm_conv_out, attn_sinks, norm_attn_out)
    dx2, doa, gwo, dpc, small = _out_proj_loss(xs, oc, oa, wo, norm_final2, target, pc, cw, norm_conv_out)
    q, kv, ga, ya, doa, probs, gwo, small = from_hbm(q, kv, ga, ya, doa, probs, gwo, small)
    dqg, small, gwo_sh = _attn_bwd(q, kv, ga, ya, doa, probs, sink_probs, norm_attn_out, gwo, small)
    grad_x, small_sum, gwt_sh = _bwd_in(dpc, dqg, h, wt, xs, norm_in, dx2, small)

    up_w_in = tuple(o.T[None] for o in _adamw("w_in", *from_hbm(w_in_t, gwt_sh, m_w_in_t, v_w_in_t), 200))
    up_w_out = tuple(o[None] for o in _adamw("w_out", *from_hbm(w_out[0], gwo_sh, m_w_out[0], v_w_out[0]), 128))

    def row_of(r):
        return lambda small_ref, chip_ref: [(..., small_ref[r:r + 1, :])]

    def sink_lanes(small_ref, chip_ref):
        return [(..., small_ref[SMALL_MISC:SMALL_MISC + 1, 0:N_HEADS])]

    def conv_taps(small_ref, chip_ref):
        width = D_CONV // N_CHIPS
        cols = pl.ds(pl.multiple_of(chip_ref[0] * width, width), width)
        return [(k, small_ref[pl.ds(SMALL_CONV_W + k, 1), cols]) for k in range(conv_w.shape[1])]

    def small_view(a):
        return a.transpose(1, 0, 2) if a.ndim == 3 else a.reshape(-1, a.shape[-1])

    small_w = (norm_in, conv_w, attn_sinks, norm_conv_out, norm_attn_out, norm_final)
    small_m = (m_norm_in, m_conv_w, m_attn_sinks, m_norm_conv_out, m_norm_attn_out, m_norm_final)
    small_v = (v_norm_in, v_conv_w, v_attn_sinks, v_norm_conv_out, v_norm_attn_out, v_norm_final)
    small_g = (row_of(SMALL_NORM_IN), conv_taps, sink_lanes, row_of(SMALL_NORM_CONV), row_of(SMALL_NORM_ATTN),
               row_of(SMALL_NORM_FINAL))
    w_views, m_views, v_views = (tuple(small_view(a) for a in group) for group in (small_w, small_m, small_v))
    up_small, loss = _adamw_small(chip.astype(jnp.int32).reshape(1), small_sum, w_views, small_g, m_views, v_views)
    up_small = [tuple(o.transpose(1, 0, 2) if w.ndim == 3 else o.reshape(w.shape) for o in up)
                for up, w in zip(up_small, small_w)]
    up_norm_in, up_conv_w, up_sinks, up_norm_conv, up_norm_attn, up_norm_final = up_small
    updates = (up_norm_in, up_w_in, up_conv_w, up_sinks, up_norm_conv, up_norm_attn, up_w_out, up_norm_final)
    grads_out, deltas, new_m, new_v = zip(*updates)
    return (loss, grad_x[None], *grads_out, *deltas, *new_m, *new_v)
```

```python
import jax
import jax.numpy as jnp
from jax import lax
from jax.experimental import pallas as pl
from jax.experimental.pallas import tpu as pltpu

F32 = jnp.float32
BF16 = jnp.bfloat16
MESH = pl.DeviceIdType.MESH

D_MODEL = 1024
D_CONV = 1024
D_ATTN = 1024
D_KV = 128
D_QG = 2 * D_ATTN + 2 * D_KV
D_MIX = D_CONV + D_ATTN
D_PC = 4 * D_CONV
D_IN_PROJ = D_PC + 2 * D_ATTN + 2 * D_KV
ROW_Q = D_PC
ROW_KV = ROW_Q + D_ATTN
ROW_GA = ROW_KV + 2 * D_KV
N_HEADS = 16
HEAD_DIM = 64
HEADS_PER_KV = 8
BLK = 128
N_CHIPS = 4
RMS_EPS = 1e-5
SCALE = HEAD_DIM ** -0.5
SLOPES = tuple(2.0 ** (-8.0 * (h + 1) / N_HEADS) for h in range(N_HEADS))

ADAM_LR, ADAM_B1, ADAM_B2, ADAM_EPS, ADAM_WD, ADAM_STEP = 0.001, 0.9, 0.999, 1e-08, 0.01, 10

SMALL_ROWS = 8
SMALL_NORM_IN, SMALL_NORM_CONV, SMALL_NORM_ATTN, SMALL_NORM_FINAL, SMALL_CONV_W, SMALL_MISC = 0, 1, 2, 3, 4, 7
SMALL_LOSS_LANE = N_HEADS

TOK_TILE = 256
PC_PIECE = 512
MIB = 1 << 20


def _params(vmem_mib, semantics=None):
    return pltpu.CompilerParams(dimension_semantics=semantics, vmem_limit_bytes=vmem_mib * MIB)


def _nn(a, b):
    return jnp.dot(a, b, preferred_element_type=F32)


def _nt(a, b):
    return lax.dot_general(a, b, (((1,), (1,)), ((), ())), preferred_element_type=F32)


def _tn(a, b):
    return lax.dot_general(a, b, (((0,), (0,)), ((), ())), preferred_element_type=F32)


def _rstd(v):
    return lax.rsqrt(jnp.mean(v * v, axis=-1, keepdims=True) + RMS_EPS)


def _rms_bwd(g, xhat, rstd):
    return rstd * (g - xhat * jnp.mean(g * xhat, axis=-1, keepdims=True))


def _silu_and_grad(g):
    s = jax.nn.sigmoid(g)
    return g * s, s * (1.0 + g * (1.0 - s))


def _resident(shape):
    return pl.BlockSpec(shape, lambda *_: (0,) * len(shape), pipeline_mode=pl.Buffered(1))


def _xor(a, b):
    return a + b - 2 * a * b


def _cast_rows(src_ref, dst_ref, rows=32):
    def step(i, carry):
        sl = pl.ds(pl.multiple_of(i * rows, rows), rows)
        dst_ref[sl, :] = src_ref[sl, :].astype(dst_ref.dtype)
        return carry
    lax.fori_loop(0, src_ref.shape[0] // rows, step, 0)


def _ag_scratch(shard_shape):
    rows, width = shard_shape
    return [pltpu.VMEM((rows, width), F32), pltpu.VMEM((rows, width), BF16), pltpu.VMEM((3, rows // 2, width), BF16),
            pltpu.SemaphoreType.DMA((6,)), pltpu.SemaphoreType.DMA((6,)), pltpu.SemaphoreType.DMA((4,))]


def _ag_stages(sh_ref, out, f32_buf, own, land, send_sems, recv_sems, local_sems):
    rows = sh_ref.shape[0]
    half = rows // 2
    x, y, c = lax.axis_index("x"), lax.axis_index("y"), lax.axis_index("c")
    j = 2 * x + y
    p1 = (_xor(x, c), _xor(y, 1 - c), c)
    p2 = (_xor(x, 1 - c), _xor(y, c), c)
    sib = (x, y, 1 - c)
    j1 = 2 * p1[0] + p1[1]
    j2 = 2 * p2[0] + p2[1]
    j3 = 3 - j

    def rows_of(chip, hf):
        return out.at[pl.ds(pl.multiple_of(chip * rows + hf * half, 16), half), :]

    def rcopy(k, src, dst, to):
        return pltpu.make_async_remote_copy(src_ref=src, dst_ref=dst, send_sem=send_sems.at[k],
                                            recv_sem=recv_sems.at[k], device_id=to, device_id_type=MESH)

    my_half = own.at[pl.ds(pl.multiple_of(c * half, 16), half), :]
    hop1 = rcopy(0, my_half, land.at[0], p1)
    hop2_own = rcopy(1, my_half, land.at[1], p2)
    hop2_fwd = rcopy(2, land.at[0], land.at[2], p2)
    swaps = [rcopy(3 + s, land.at[s], rows_of(chip, c), sib) for s, chip in enumerate((j1, j2, j3))]
    keeps = [pltpu.make_async_copy(land.at[s], rows_of(chip, c), local_sems.at[1 + s]) for s, chip in enumerate((j1, j2, j3))]
    load = pltpu.make_async_copy(sh_ref, f32_buf, local_sems.at[0])
    own_out = pltpu.make_async_copy(own, out.at[pl.ds(pl.multiple_of(j * rows, 16), rows), :], local_sems.at[0])

    def stage_send():
        load.start()
        load.wait()
        _cast_rows(f32_buf, own)
        own_out.start()
        hop1.start()

    def stage_forward():
        hop1.wait_recv()
        hop2_own.start()
        hop2_fwd.start()
        swaps[0].start()
        keeps[0].start()

    def stage_publish():
        hop2_own.wait_recv()
        swaps[1].start()
        keeps[1].start()
        hop2_fwd.wait_recv()
        swaps[2].start()
        keeps[2].start()

    def stage_drain():
        for s, chip in enumerate((j2, j1, j3)):
            rcopy(3 + s, my_half, rows_of(chip, 1 - c), sib).wait_recv()
        for cp in [hop1, hop2_own, hop2_fwd] + swaps:
            cp.wait_send()
        for cp in [own_out] + keeps:
            cp.wait()

    return stage_send, stage_forward, stage_publish, stage_drain


AG_CAST_ROWS = 400


def _gather_resident(sh_ref, out, f32_buf, send_sems, recv_sems, local_sems):
    rows = sh_ref.shape[0]
    half = rows // 2
    x, y, c = lax.axis_index("x"), lax.axis_index("y"), lax.axis_index("c")
    j = 2 * x + y
    p1 = (_xor(x, c), _xor(y, 1 - c), c)
    p2 = (_xor(x, 1 - c), _xor(y, c), c)
    sib = (x, y, 1 - c)
    j1 = 2 * p1[0] + p1[1]
    j2 = 2 * p2[0] + p2[1]
    j3 = 3 - j

    def rows_of(chip, hf):
        return out.at[pl.ds(pl.multiple_of(chip * rows + hf * half, 16), half), :]

    def send(k, chip, to):
        return pltpu.make_async_remote_copy(src_ref=rows_of(chip, c), dst_ref=rows_of(chip, c), send_sem=send_sems.at[k],
                                            recv_sem=recv_sems.at[k], device_id=to, device_id_type=MESH)

    per_half = half // AG_CAST_ROWS

    def cast_own(chunk):
        lo = pl.multiple_of(chunk * AG_CAST_ROWS, 16)
        load = pltpu.make_async_copy(sh_ref.at[pl.ds(lo, AG_CAST_ROWS), :], f32_buf, local_sems.at[0])
        load.start()
        load.wait()
        _cast_rows(f32_buf, out.at[pl.ds(pl.multiple_of(j * rows + lo, 16), AG_CAST_ROWS), :], rows=16)

    for k in range(per_half):
        cast_own(c * per_half + k)
    hop1 = send(0, j, p1)
    hop1.start()
    for k in range(per_half):
        cast_own((1 - c) * per_half + k)
    hop1.wait_recv()
    sends = [hop1, send(1, j, p2), send(2, j1, p2), send(3, j1, sib)]
    for cp in sends[1:]:
        cp.start()
    sends[1].wait_recv()
    sends.append(send(4, j2, sib))
    sends[-1].start()
    sends[2].wait_recv()
    sends.append(send(5, j3, sib))
    sends[-1].start()
    for k in (3, 4, 5):
        send(k, j, sib).wait_recv()
    for cp in sends:
        cp.wait_send()


def _fwd_in(x, norm_in, wt_sh, wo_sh, cw_sh, norm_conv_out, sinks, norm_attn_out):
    seq = x.shape[0]
    tile = TOK_TILE
    n_tiles = seq // tile
    stage_steps = (0, n_tiles // 4, (5 * n_tiles) // 8, n_tiles - 1)
    blocks = tile // BLK
    cw_cols = cw_sh.shape[-1]

    def body(x_ref, g_ref, wtsh_ref, wo_ref, cwsh_ref, gn_ref, sink_ref, gna_ref,
             h_ref, pc_ref, q_ref, kv_ref, ga_ref, oc_ref, ya_ref, oa_ref, pr_ref, sp_ref, wt_out, cw_ref, wo_out,
             zbuf, kv_last, pc_buf, wt_ref, f32_buf, cw_land, wt_send, wt_recv, wt_local, cw_send, cw_recv, cw_local, *ag_scratch):
        step = pl.program_id(0)
        to_hbm = pltpu.make_async_copy(wt_ref, wt_out, wt_local.at[0])

        @pl.when(step == 0)
        def _():
            zbuf[0:8, :] = jnp.zeros((8, D_CONV), F32)
            kv_last[...] = jnp.zeros_like(kv_last)
            x_i, y_i, c = lax.axis_index("x"), lax.axis_index("y"), lax.axis_index("c")
            j = 2 * x_i + y_i
            p1 = (_xor(x_i, c), _xor(y_i, 1 - c), c)
            p2 = (_xor(x_i, 1 - c), _xor(y_i, c), c)
            j1 = 2 * p1[0] + p1[1]

            def cw_copy(k, src, chip, to):
                return pltpu.make_async_remote_copy(src_ref=src, dst_ref=cw_land.at[chip], send_sem=cw_send.at[k],
                                                    recv_sem=cw_recv.at[k], device_id=to, device_id_type=MESH)

            mine = pltpu.make_async_copy(cwsh_ref, cw_land.at[j], cw_local.at[0])
            mine.start()
            first = cw_copy(0, cwsh_ref, j, p1)
            first.start()
            _gather_resident(wtsh_ref, wt_ref, f32_buf, wt_send, wt_recv, wt_local)
            to_hbm.start()
            first.wait_recv()
            second = [cw_copy(1, cwsh_ref, j, p2), cw_copy(2, cw_land.at[j1], j1, p2)]
            for cp in second:
                cp.start()
            for cp in second:
                cp.wait_recv()
            for cp in [first] + second:
                cp.wait_send()
            mine.wait()
            for chip in range(N_CHIPS):
                for tap in range(cw_sh.shape[0]):
                    cw_ref[tap:tap + 1, chip * cw_cols:(chip + 1) * cw_cols] = cw_land[chip, tap]

        stages = _ag_stages(wo_ref, wo_out, *ag_scratch)
        for at, stage in zip(stage_steps[:-1], stages[:-1]):
            pl.when(step == at)(stage)

        def attention(b):
            rows = pl.ds(b * BLK, BLK)
            kv_prev = kv_last if b == 0 else kv_ref.at[pl.ds((b - 1) * BLK, BLK), :]
            return _attn_forward(q_ref.at[rows, :], kv_ref.at[rows, :], kv_prev, ga_ref.at[rows, :], sink_ref, gna_ref,
                                 _band_geometry(step * blocks + b), ya_ref.at[rows, :], oa_ref.at[rows, :],
                                 pr_ref.at[rows, :], sp_ref.at[rows, :])

        xv = x_ref[...]
        h = (xv * _rstd(xv) * g_ref[...]).astype(BF16)
        h_ref[...] = h
        q_ref[...] = _nt(h, wt_ref[ROW_Q:ROW_KV, :]).astype(BF16)
        kv_ref[...] = _nt(h, wt_ref[ROW_KV:ROW_GA, :]).astype(BF16)
        ga_ref[...] = _nt(h, wt_ref[ROW_GA:D_IN_PROJ, :])
        attention_blocks = [attention(b) for b in range(blocks)]
        for lo in range(0, D_PC, PC_PIECE):
            pc_buf[:, lo:lo + PC_PIECE] = _nt(h, wt_ref[lo:lo + PC_PIECE, :])
            pc_ref[:, lo:lo + PC_PIECE] = pc_buf[:, lo:lo + PC_PIECE].astype(BF16)
            for stages_of_block in attention_blocks:
                next(stages_of_block, None)
        for stages_of_block in attention_blocks:
            for _ in stages_of_block:
                pass
        kv_last[...] = kv_ref[tile - BLK:tile, :]

        cb, _, _, _, _, _, conv = _conv_core(pc_buf, zbuf, cw_ref)
        yc = cb * conv
        silu, _ = _silu_and_grad(pc_buf[:, 3 * D_CONV:4 * D_CONV])
        oc_ref[...] = (yc * _rstd(yc) * gn_ref[...] * silu).astype(BF16)
        zbuf[0:8, :] = zbuf[tile:tile + 8, :]

        @pl.when(step == stage_steps[-1])
        def _():
            stages[-1]()
            to_hbm.wait()

    def row(width):
        return pl.BlockSpec((tile, width), lambda i: (i, 0))

    any_spec = pl.BlockSpec(memory_space=pl.ANY)
    dma = pltpu.SemaphoreType.DMA
    wt_shape = (N_CHIPS * wt_sh.shape[0], wt_sh.shape[1])
    cw_shape = (cw_sh.shape[0], N_CHIPS * cw_cols)
    return pl.pallas_call(
        body, name="fwd_in", grid=(n_tiles,),
        out_shape=(jax.ShapeDtypeStruct((seq, D_MODEL), BF16), jax.ShapeDtypeStruct((seq, D_PC), BF16),
                   jax.ShapeDtypeStruct((seq, D_ATTN), BF16), jax.ShapeDtypeStruct((seq, 2 * D_KV), BF16),
                   jax.ShapeDtypeStruct((seq, D_ATTN), F32), jax.ShapeDtypeStruct((seq, D_CONV), BF16),
                   pltpu.HBM((seq, D_ATTN), F32), pltpu.HBM((seq, D_ATTN), BF16),
                   pltpu.HBM((seq, N_HEADS * BLK), BF16), pltpu.HBM((seq, 128), F32),
                   pltpu.HBM(wt_shape, BF16), jax.ShapeDtypeStruct(cw_shape, F32),
                   jax.ShapeDtypeStruct((N_CHIPS * wo_sh.shape[0], wo_sh.shape[1]), BF16)),
        in_specs=[row(D_MODEL), _resident((1, D_MODEL)), any_spec, any_spec, any_spec, _resident((1, D_CONV)),
                  pl.BlockSpec(memory_space=pltpu.SMEM), _resident((1, D_ATTN))],
        out_specs=(row(D_MODEL), row(D_PC), row(D_ATTN), row(2 * D_KV), row(D_ATTN), row(D_CONV), row(D_ATTN),
                   row(D_ATTN), row(N_HEADS * BLK), row(128), any_spec, pl.BlockSpec(cw_shape, lambda i: (0, 0)),
                   any_spec),
        scratch_shapes=[pltpu.VMEM((tile + 8, D_CONV), F32), pltpu.VMEM((BLK, 2 * D_KV), BF16),
                        pltpu.VMEM((tile, D_PC), F32), pltpu.VMEM(wt_shape, BF16), pltpu.VMEM((AG_CAST_ROWS, wt_sh.shape[1]), F32),
                        pltpu.VMEM((N_CHIPS,) + cw_sh.shape, F32),
                        dma((6,)), dma((6,)), dma((1,)), dma((3,)), dma((3,)), dma((1,))] + _ag_scratch(wo_sh.shape),
        compiler_params=_params(62, ("arbitrary",)),
    )(x, norm_in, wt_sh, wo_sh, cw_sh, norm_conv_out, sinks, norm_attn_out)


def _conv_core(pc_ref, zbuf, cw_ref):
    tile = pc_ref.shape[0]
    cb = pc_ref[:, 0:D_CONV].astype(F32)
    cc = pc_ref[:, D_CONV:2 * D_CONV].astype(F32)
    cu = pc_ref[:, 2 * D_CONV:3 * D_CONV].astype(F32)
    z = cc * cu
    zbuf[8:tile + 8, :] = z
    z1 = zbuf[7:tile + 7, :]
    z2 = zbuf[6:tile + 6, :]
    conv = cw_ref[0:1, :] * z2 + cw_ref[1:2, :] * z1 + cw_ref[2:3, :] * z
    return cb, cc, cu, z, z1, z2, conv


def _band_geometry(block_index):
    qi = lax.broadcasted_iota(jnp.int32, (BLK, BLK), 0)
    kp = lax.broadcasted_iota(jnp.int32, (BLK, BLK), 1)
    use_cur = kp <= qi
    dist = jnp.where(use_cur, qi - kp, qi - kp + BLK).astype(F32)
    valid = use_cur | (block_index > 0)
    return use_cur, dist, valid


def _block_diag(cur, prev, group):
    lane = lax.broadcasted_iota(jnp.int32, cur.shape, 1)

    def halves(t):
        other = pltpu.roll(t, 64, 1)
        lo, hi = (t, other) if group == 0 else (other, t)
        return jnp.where(lane < 64, lo, 0.0), jnp.where(lane >= 64, hi, 0.0)

    return jnp.concatenate(halves(cur) + halves(prev), axis=0).astype(BF16)


def _merge(s4, use_cur):
    return (jnp.where(use_cur, s4[:, 0:BLK], s4[:, 2 * BLK:3 * BLK]),
            jnp.where(use_cur, s4[:, BLK:2 * BLK], s4[:, 3 * BLK:4 * BLK]))


def _split(a, b, use_cur):
    return jnp.concatenate([jnp.where(use_cur, a, 0.0), jnp.where(use_cur, b, 0.0),
                            jnp.where(use_cur, 0.0, a), jnp.where(use_cur, 0.0, b)], axis=1)


def _softmax_head(s, head, sink, dist, valid):
    sc = jnp.where(valid, s - SLOPES[head] * dist, -jnp.inf)
    m = jnp.maximum(jnp.max(sc, axis=-1, keepdims=True), sink)
    p = jnp.exp(sc - m)
    es = jnp.exp(sink - m)
    inv = 1.0 / (jnp.sum(p, axis=-1, keepdims=True) + es)
    return p * inv, es * inv


def _attn_operands(q_ref, kvc_ref, kvp_ref):
    groups = range(N_HEADS // HEADS_PER_KV)
    k_cur, k_prev = kvc_ref[:, 0:D_KV].astype(F32), kvp_ref[:, 0:D_KV].astype(F32)
    v_cur, v_prev = kvc_ref[:, D_KV:2 * D_KV].astype(F32), kvp_ref[:, D_KV:2 * D_KV].astype(F32)
    kbd = [_block_diag(k_cur, k_prev, g) for g in groups]
    vbd = [_block_diag(v_cur, v_prev, g) for g in groups]
    qps = [(q_ref[:, j * 128:(j + 1) * 128].astype(F32) * SCALE).astype(BF16) for j in range(N_HEADS // 2)]
    return qps, kbd, vbd


def _attn_forward(q_ref, kvc_ref, kvp_ref, ga_ref, sink_ref, gn_ref, geometry, ya_ref, oa_ref, pr_ref, sp_ref):
    use_cur, dist, valid = geometry
    _, kbd, vbd = operands = _attn_operands(q_ref, kvc_ref, kvp_ref)
    yield
    scores = []
    for j, qp in enumerate(operands[0]):
        scores += _merge(_nt(qp, kbd[j // 4]), use_cur)
    yield
    sinks = [sink_ref[0, h] for h in range(N_HEADS)]
    scores = [jnp.where(valid, s - SLOPES[h] * dist, -jnp.inf) for h, s in enumerate(scores)]
    maxes = [jnp.maximum(jnp.max(s, axis=-1, keepdims=True), sinks[h]) for h, s in enumerate(scores)]
    yield
    exps = [jnp.exp(s - m) for s, m in zip(scores, maxes)]
    sink_exps = [jnp.exp(sinks[h] - m) for h, m in enumerate(maxes)]
    yield
    invs = [1.0 / (jnp.sum(e, axis=-1, keepdims=True) + se) for e, se in zip(exps, sink_exps)]
    probs = [e * inv for e, inv in zip(exps, invs)]
    pr_ref[...] = jnp.concatenate(probs, axis=1).astype(BF16)
    lane = lax.broadcasted_iota(jnp.int32, (BLK, 128), 1)
    sp_ref[...] = sum(jnp.where(lane == h, se * inv, 0.0) for h, (se, inv) in enumerate(zip(sink_exps, invs)))
    yield
    p4s = [_split(probs[2 * j], probs[2 * j + 1], use_cur).astype(BF16) for j in range(N_HEADS // 2)]
    ya = jnp.concatenate([_nn(p4, vbd[j // 4]) for j, p4 in enumerate(p4s)], axis=1)
    ya_ref[...] = ya
    yield
    silu, _ = _silu_and_grad(ga_ref[...])
    oa_ref[...] = (ya * _rstd(ya) * gn_ref[...] * silu).astype(BF16)


def _kv_specs(n_blocks, reverse):
    def blk(i):
        return (n_blocks - 1 - i) if reverse else i
    cur = pl.BlockSpec((BLK, 2 * D_KV), lambda i: (blk(i), 0))
    prev = pl.BlockSpec((BLK, 2 * D_KV), lambda i: (jnp.maximum(blk(i) - 1, 0), 0))
    return cur, prev


HALO = 16


def _out_proj_loss(x, oc, oa, wo, norm_final, target, pc, conv_w, norm_conv_out):
    seq = x.shape[0]
    tile = TOK_TILE
    n_tiles = seq // tile

    def body(x_ref, oc_ref, oa_ref, wo_ref, gf_ref, t_ref, pc_ref, hcc_ref, hcu_ref, cw_ref, gn_ref,
             dx2_ref, doa_ref, gwo_ref, dpc_ref, small_ref, loss_acc, zbuf, dbuf):
        step = pl.program_id(0)

        def small_add(row, value):
            small_ref[row:row + 1, :] += jnp.sum(value, axis=0, keepdims=True)

        @pl.when(step == 0)
        def _():
            gwo_ref[...] = jnp.zeros_like(gwo_ref)
            small_ref[...] = jnp.zeros_like(small_ref)
            loss_acc[...] = jnp.zeros_like(loss_acc)
            dbuf[tile:tile + 8, :] = jnp.zeros((8, D_CONV), F32)

        oc, oa = oc_ref[...], oa_ref[...]
        x2 = x_ref[...] + _nn(oc, wo_ref[0:D_CONV, :]) + _nn(oa, wo_ref[D_CONV:D_MIX, :])
        r = _rstd(x2)
        xhat = x2 * r
        err = xhat * gf_ref[...] - t_ref[...]
        loss_acc[...] += jnp.sum(err * err, axis=0, keepdims=True) * (0.5 / D_MODEL)
        dy = err * (1.0 / D_MODEL)
        small_add(SMALL_NORM_FINAL, dy * xhat)
        dx2 = _rms_bwd(dy * gf_ref[...], xhat, r)
        dx2_ref[...] = dx2
        db = dx2.astype(BF16)
        do = _nt(db, wo_ref[0:D_CONV, :])
        doa_ref[...] = _nt(db, wo_ref[D_CONV:D_MIX, :])
        gwo_ref[0:D_CONV, :] += _tn(oc, db)
        gwo_ref[D_CONV:D_MIX, :] += _tn(oa, db)

        is_first_tile = step == n_tiles - 1
        halo_z = hcc_ref[HALO - 8:HALO, :].astype(F32) * hcu_ref[HALO - 8:HALO, :].astype(F32)
        zbuf[0:8, :] = jnp.where(is_first_tile, 0.0, halo_z)
        cb, cc, cu, z, z1, z2, conv = _conv_core(pc_ref, zbuf, cw_ref)
        silu, dsilu = _silu_and_grad(pc_ref[:, 3 * D_CONV:4 * D_CONV].astype(F32))
        yc = cb * conv
        rc = _rstd(yc)
        chat = yc * rc
        dn = do * silu
        dpc_ref[:, 3 * D_CONV:4 * D_CONV] = (do * (chat * gn_ref[...]) * dsilu).astype(BF16)
        small_add(SMALL_NORM_CONV, dn * chat)
        dyc = _rms_bwd(dn * gn_ref[...], chat, rc)
        dpc_ref[:, 0:D_CONV] = (dyc * conv).astype(BF16)
        dconv = dyc * cb
        small_add(SMALL_CONV_W, dconv * z2)
        small_add(SMALL_CONV_W + 1, dconv * z1)
        small_add(SMALL_CONV_W + 2, dconv * z)
        dbuf[0:tile, :] = dconv
        dz = cw_ref[2:3, :] * dconv + cw_ref[1:2, :] * dbuf[1:tile + 1, :] + cw_ref[0:1, :] * dbuf[2:tile + 2, :]
        dpc_ref[:, D_CONV:2 * D_CONV] = (dz * cu).astype(BF16)
        dpc_ref[:, 2 * D_CONV:3 * D_CONV] = (dz * cc).astype(BF16)
        dbuf[tile:tile + 8, :] = dbuf[0:8, :]

        @pl.when(step == n_tiles - 1)
        def _():
            lane = lax.broadcasted_iota(jnp.int32, (1, D_MODEL), 1)
            small_ref[SMALL_MISC:SMALL_MISC + 1, :] = jnp.where(lane == SMALL_LOSS_LANE, jnp.sum(loss_acc[...]), 0.0)

    def rev(i):
        return n_tiles - 1 - i

    def row(width):
        return pl.BlockSpec((tile, width), lambda i: (rev(i), 0))

    def halo(col_block):
        return pl.BlockSpec((HALO, D_CONV), lambda i: (jnp.maximum(rev(i) * (tile // HALO) - 1, 0), col_block))

    def const(shape):
        return pl.BlockSpec(shape, lambda i: (0, 0))

    return pl.pallas_call(
        body, name="out_proj_loss", grid=(n_tiles,),
        out_shape=(jax.ShapeDtypeStruct((seq, D_MODEL), F32), jax.ShapeDtypeStruct((seq, D_ATTN), F32),
                   jax.ShapeDtypeStruct((D_MIX, D_MODEL), F32), jax.ShapeDtypeStruct((seq, D_PC), BF16),
                   jax.ShapeDtypeStruct((SMALL_ROWS, D_MODEL), F32)),
        in_specs=[row(D_MODEL), row(D_CONV), row(D_ATTN), _resident(wo.shape), _resident((1, D_MODEL)), row(D_MODEL),
                  row(D_PC), halo(1), halo(2), _resident(conv_w.shape), _resident((1, D_CONV))],
        out_specs=(row(D_MODEL), row(D_ATTN), const((D_MIX, D_MODEL)), row(D_PC), const((SMALL_ROWS, D_MODEL))),
        scratch_shapes=[pltpu.VMEM((1, D_MODEL), F32), pltpu.VMEM((tile + 8, D_CONV), F32),
                        pltpu.VMEM((tile + 8, D_CONV), F32)],
        compiler_params=_params(62, ("arbitrary",)),
    )(x, oc, oa, wo, norm_final, target, pc, pc, pc, conv_w, norm_conv_out)


def _attn_bwd(q, kv, ga, ya, doa, probs, sink_probs, norm_attn_out, gwo, small):
    seq = q.shape[0]
    n_blocks = seq // BLK
    stage_steps = (0, n_blocks // 4, n_blocks // 2, (3 * n_blocks) // 4, n_blocks - 1)

    def body(q_ref, kvc_ref, kvp_ref, ga_ref, ya_ref, doa_ref, pr_ref, sp_ref, gn_ref, gwo_ref, small_ref,
             dqg_ref, small_out, gwo_sh, gna_ref, gs_ref, carry, dya_buf, *rs_scratch):
        step = pl.program_id(0)

        @pl.when(step == 0)
        def _():
            gna_ref[...] = jnp.zeros_like(gna_ref)
            gs_ref[...] = jnp.zeros_like(gs_ref)
            carry[...] = jnp.zeros_like(carry)

        ya = ya_ref[...]
        r = _rstd(ya)
        xhat = ya * r
        silu, dsilu = _silu_and_grad(ga_ref[...])
        do = doa_ref[...]
        dn = do * silu
        dqg_ref[:, D_ATTN:2 * D_ATTN] = (do * (xhat * gn_ref[...]) * dsilu).astype(BF16)
        gna_ref[...] += jnp.sum(dn * xhat, axis=0, keepdims=True)
        dya_buf[...] = _rms_bwd(dn * gn_ref[...], xhat, r).astype(BF16)

        use_cur = _band_geometry(n_blocks - 1 - step)[0]
        lane = lax.broadcasted_iota(jnp.int32, (BLK, 128), 1)

        def fold(bd):
            return (jnp.where(lane < 64, bd[0:BLK], 0.0) + jnp.where(lane >= 64, bd[BLK:2 * BLK], 0.0),
                    jnp.where(lane < 64, bd[2 * BLK:3 * BLK], 0.0) + jnp.where(lane >= 64, bd[3 * BLK:4 * BLK], 0.0))

        pairs = range(N_HEADS // 2)
        qps, kbd, vbd = _attn_operands(q_ref, kvc_ref, kvp_ref)
        probs = [pr_ref[:, h * BLK:(h + 1) * BLK].astype(F32) for h in range(N_HEADS)]
        dyps = [dya_buf[:, j * 128:(j + 1) * 128] for j in pairs]
        dps = []
        for j in pairs:
            dps += _merge(_nt(dyps[j], vbd[j // 4]), use_cur)
        deltas = [jnp.sum(p * dp, axis=-1, keepdims=True) for p, dp in zip(probs, dps)]
        dss = [p * (dp - delta) for p, dp, delta in zip(probs, dps, deltas)]
        delta_lanes = sum(jnp.where(lane == h, deltas[h], 0.0) for h in range(N_HEADS))
        gs_ref[...] -= jnp.sum(sp_ref[...] * delta_lanes, axis=0, keepdims=True)
        ds4s = [_split(dss[2 * j], dss[2 * j + 1], use_cur).astype(BF16) for j in pairs]
        p4s = [_split(probs[2 * j], probs[2 * j + 1], use_cur).astype(BF16) for j in pairs]
        dqg_ref[:, 0:D_ATTN] = jnp.concatenate([_nn(ds4s[j], kbd[j // 4]) * SCALE for j in pairs], axis=1).astype(BF16)
        sums = []
        for group in range(N_HEADS // HEADS_PER_KV):
            acc = [jnp.zeros((BLK, 128), F32) for _ in range(4)]
            for j in range(group * 4, group * 4 + 4):
                for slot, part in enumerate(fold(_tn(ds4s[j], qps[j])) + fold(_tn(p4s[j], dyps[j]))):
                    acc[slot] = acc[slot] + part
            sums.append([a + pltpu.roll(a, 64, 1) for a in acc])
        dk_cur, dk_prev, dv_cur, dv_prev = (jnp.where(lane < 64, a, b) for a, b in zip(sums[0], sums[1]))
        dqg_ref[:, 2 * D_ATTN:2 * D_ATTN + 2 * D_KV] = (jnp.concatenate([dk_cur, dv_cur], axis=1) + carry[...]).astype(BF16)
        carry[...] = jnp.concatenate([dk_prev, dv_prev], axis=1)

        @pl.when(step == n_blocks - 1)
        def _():
            small_out[...] = small_ref[...]
            small_out[SMALL_NORM_ATTN:SMALL_NORM_ATTN + 1, :] = gna_ref[...]
            small_out[SMALL_MISC:SMALL_MISC + 1, 0:128] = small_ref[SMALL_MISC:SMALL_MISC + 1, 0:128] + gs_ref[...]

        for at, stage in zip(stage_steps, _rs_wout_stages(gwo_ref, gwo_sh, *rs_scratch)):
            pl.when(step == at)(stage)

    row = pl.BlockSpec((BLK, D_ATTN), lambda i: (n_blocks - 1 - i, 0))
    kv_cur, kv_prev = _kv_specs(n_blocks, reverse=True)
    any_spec = pl.BlockSpec(memory_space=pl.ANY)
    return pl.pallas_call(
        body, name="attn_bwd", grid=(n_blocks,),
        out_shape=(jax.ShapeDtypeStruct((seq, D_QG), BF16), pltpu.HBM(small.shape, F32),
                   pltpu.HBM((gwo.shape[0] // N_CHIPS, gwo.shape[1]), F32)),
        in_specs=[row, kv_cur, kv_prev, row, row, row,
                  pl.BlockSpec((BLK, N_HEADS * BLK), lambda i: (n_blocks - 1 - i, 0)),
                  pl.BlockSpec((BLK, 128), lambda i: (n_blocks - 1 - i, 0)), _resident((1, D_ATTN)),
                  any_spec, _resident(small.shape)],
        out_specs=(pl.BlockSpec((BLK, D_QG), lambda i: (n_blocks - 1 - i, 0)),
                   pl.BlockSpec(small.shape, lambda i: (0, 0)), any_spec),
        scratch_shapes=[pltpu.VMEM((1, D_ATTN), F32), pltpu.VMEM((1, 128), F32),
                        pltpu.VMEM((BLK, 2 * D_KV), F32), pltpu.VMEM((BLK, D_ATTN), BF16)] + _rs_wout_scratch(gwo.shape),
        compiler_params=_params(44, ("arbitrary",)),
    )(q, kv, kv, ga, ya, doa, probs, sink_probs, norm_attn_out, gwo, small)


GBLK = 256
GSUB = 64
PAIR_RING = 4
LAG_PAIR, LAG_HOP1, LAG_HOP2 = 1, 7, 14


def _bwd_in(dpc, dqg, h, wt, x, norm_in, dx2, small):
    seq = x.shape[0]
    n_blk = D_IN_PROJ // GBLK
    per_chip = n_blk // N_CHIPS
    n_slots = (n_blk + 1) // 2
    n_sub = GBLK // GSUB
    chip_rows = D_IN_PROJ // N_CHIPS
    tile = TOK_TILE
    n_tiles = seq // tile
    n_steps = n_blk + max(n_tiles, LAG_HOP2)
    chunk = min(seq, 512)
    blk_q, blk_kv, blk_ga = ROW_Q // GBLK, ROW_KV // GBLK, ROW_GA // GBLK

    def block_of(i):
        k = i % N_CHIPS
        robin = per_chip * ((k % 2) * 2 + k // 2) + i // N_CHIPS
        if isinstance(i, int):
            return robin if i < per_chip * N_CHIPS else i
        return jnp.where(i < per_chip * N_CHIPS, robin, i)

    def owner_of(i):
        return (i // N_CHIPS) % 2

    def slot_of(i):
        return (i // (2 * N_CHIPS)) * N_CHIPS + i % N_CHIPS

    def body(dpc_ref, dqg_ref, wt_ref, h_ref, x_ref, g_ref, dx2_ref, small_ref, gx_ref, small_sum, gwt_sh,
             dh_acc, gni, keep, pbuf, xbuf, land, land2, small_land,
             pair_send, pair_recv, h1_send, h1_recv, h2_send, h2_recv, sw_send, sw_recv, sm_send, sm_recv, out_sem):
        step = pl.program_id(0)
        x_i, y_i, c = lax.axis_index("x"), lax.axis_index("y"), lax.axis_index("c")
        me = 4 * x_i + 2 * y_i + c
        j = 2 * x_i + y_i
        pa = (_xor(x_i, 1 - c), _xor(y_i, c), c)
        pb = (_xor(x_i, c), _xor(y_i, 1 - c), c)
        sib = (x_i, y_i, 1 - c)
        ja = 2 * pa[0] + pa[1]
        jb = 2 * pb[0] + pb[1]
        jd = 3 - j

        def remote(src, dst, send, recv, to):
            return pltpu.make_async_remote_copy(src_ref=src, dst_ref=dst, send_sem=send, recv_sem=recv,
                                                device_id=to, device_id_type=MESH)

        def piece(ref, slot, u, n):
            return ref.at[slot, pl.ds(u * GSUB, n * GSUB), :]

        def chip_rows_at(ref, local, n):
            return ref.at[pl.ds(pl.multiple_of(local, GSUB), n * GSUB), :]

        def pair_copy(i):
            slot = slot_of(i)
            return remote(pbuf.at[i % PAIR_RING], land.at[slot], pair_send.at[slot], pair_recv.at[slot], sib)

        def h1_copy(slot, u, n):
            k = slot * n_sub + u
            return remote(piece(xbuf, slot, u, n), piece(xbuf, slot, u, n), h1_send.at[k], h1_recv.at[k], pa)

        def h2_copy(slot, u, n, local):
            k = slot * n_sub + u
            return remote(piece(xbuf, slot, u, n), chip_rows_at(land2, local, n), h2_send.at[k], h2_recv.at[k], pb)

        def sw_copy(slot, u, n, local):
            k = slot * n_sub + u
            return remote(piece(keep, slot, u, n), chip_rows_at(gwt_sh, local, n), sw_send.at[k], sw_recv.at[k], sib)

        def owned(i):
            return (i >= 0) & (i < n_blk) & (owner_of(i) == c)

        def chip_of(blk, u):
            row = blk * GBLK + u * GSUB
            chip = row // chip_rows
            return chip, row - chip * chip_rows

        def pieces(blk):
            first, local = chip_of(blk, 0)
            whole = first == chip_of(blk, n_sub - 1)[0]
            if isinstance(blk, int):
                return [(True, 0, n_sub, first, local)] if whole else [(True, u, 1) + chip_of(blk, u) for u in range(n_sub)]
            return [(whole, 0, n_sub, first, local)] + [(jnp.logical_not(whole), u, 1) + chip_of(blk, u) for u in range(n_sub)]

        @pl.when(step == 0)
        def _():
            dh_acc[...] = jnp.zeros_like(dh_acc)
            gni[...] = jnp.zeros_like(gni)

        @pl.when(step < n_blk)
        def _():
            from_pc = block_of(step) < blk_q
            block = _tn(jnp.where(from_pc, dpc_ref[...], dqg_ref[...]), h_ref[...])
            for t in range(0, seq, chunk):
                d = jnp.where(from_pc, dpc_ref[t:t + chunk, :], dqg_ref[t:t + chunk, :])
                dh_acc[t:t + chunk, :] += _nn(d, wt_ref[...])

            @pl.when(owner_of(step) == c)
            def _():
                keep[slot_of(step)] = block

            @pl.when(owner_of(step) != c)
            def _():
                @pl.when(step >= 2 * PAIR_RING)
                def _():
                    pair_copy(step - 2 * PAIR_RING).wait_send()
                pbuf[step % PAIR_RING] = block.astype(BF16)
                pair_copy(step).start()

        i1 = step - LAG_PAIR

        @pl.when(owned(i1))
        def _():
            slot = slot_of(i1)
            pair_copy(i1).wait_recv()
            _accumulate(keep.at[slot], land.at[slot])
            for cond, u, n, chip, _ in pieces(block_of(i1)):
                @pl.when(cond & ((chip == ja) | (chip == jd)))
                def _(u=u, n=n):
                    _cast_rows(piece(keep, slot, u, n), piece(xbuf, slot, u, n))
                    h1_copy(slot, u, n).start()

        i2 = step - LAG_HOP1

        @pl.when(owned(i2))
        def _():
            slot = slot_of(i2)
            for cond, u, n, chip, local in pieces(block_of(i2)):
                @pl.when(cond & ((chip == j) | (chip == jb)))
                def _(u=u, n=n, chip=chip, local=local):
                    h1_copy(slot, u, n).wait_recv()
                    _accumulate(piece(keep, slot, u, n), piece(xbuf, slot, u, n))

                    @pl.when(chip == jb)
                    def _():
                        _cast_rows(piece(keep, slot, u, n), piece(xbuf, slot, u, n))
                        h2_copy(slot, u, n, local).start()

                @pl.when(cond & ((chip == ja) | (chip == jd)))
                def _(u=u, n=n):
                    h1_copy(slot, u, n).wait_send()

        i3 = step - LAG_HOP2

        @pl.when(owned(i3))
        def _():
            slot = slot_of(i3)
            for cond, u, n, chip, local in pieces(block_of(i3)):
                @pl.when(cond & (chip == j))
                def _(u=u, n=n, local=local):
                    h2_copy(slot, u, n, local).wait_recv()
                    _accumulate(piece(keep, slot, u, n), chip_rows_at(land2, local, n))
                    mine = pltpu.make_async_copy(piece(keep, slot, u, n), chip_rows_at(gwt_sh, local, n), out_sem.at[0])
                    mine.start()
                    sw_copy(slot, u, n, local).start()
                    mine.wait()

                @pl.when(cond & (chip == jb))
                def _(u=u, n=n, local=local):
                    h2_copy(slot, u, n, local).wait_send()

        e = step - n_blk

        @pl.when((e >= 0) & (e < n_tiles))
        def _():
            dh = dh_acc[pl.ds(pl.multiple_of(e * tile, tile), tile), :]
            xv = x_ref[...]
            r = _rstd(xv)
            xhat = xv * r
            gni[...] += jnp.sum(dh * xhat, axis=0, keepdims=True)
            gx_ref[...] = _rms_bwd(dh * g_ref[...], xhat, r) + dx2_ref[...]

        @pl.when(step == n_steps - 1)
        def _():
            small_land[me] = small_ref[...]
            small_land[me, SMALL_NORM_IN:SMALL_NORM_IN + 1, :] = gni[...]
            others = [(dx, dy, dc) for dx in (0, 1) for dy in (0, 1) for dc in (0, 1)][1:]
            sends = [remote(small_land.at[me], small_land.at[me], sm_send.at[k], sm_recv.at[k],
                            (_xor(x_i, dx), _xor(y_i, dy), _xor(c, dc))) for k, (dx, dy, dc) in enumerate(others)]
            for cp in sends:
                cp.start()
            for i in range(n_blk):
                if i + 2 * PAIR_RING >= n_blk:
                    @pl.when(owner_of(i) != c)
                    def _(i=i):
                        pair_copy(i).wait_send()
                for _, u, n, chip, local in pieces(block_of(i)):
                    @pl.when((j == chip) & (c == owner_of(i)))
                    def _(i=i, u=u, n=n, local=local):
                        sw_copy(slot_of(i), u, n, local).wait_send()

                    @pl.when((j == chip) & (c != owner_of(i)))
                    def _(i=i, u=u, n=n, local=local):
                        sw_copy(slot_of(i), u, n, local).wait_recv()
            for cp in sends:
                cp.wait_recv()
            total = small_land[0]
            for dev in range(1, 8):
                total = total + small_land[dev]
            small_sum[...] = total
            for cp in sends:
                cp.wait_send()

    def blk_at(i):
        return block_of(jnp.clip(i, 0, n_blk - 1))

    last_pc_step = max(i for i in range(n_blk) if block_of(i) < blk_q)

    def next_block(i, in_pc):
        i = jnp.clip(i, 0, n_blk - 1)
        step = jnp.full_like(i, last_pc_step if in_pc else n_blk - 1)
        for ahead in reversed(range(N_CHIPS)):
            cand = jnp.minimum(i + ahead, n_blk - 1)
            step = jnp.where((block_of(cand) < blk_q) == in_pc, cand, step)
        return block_of(step)

    def dqg_block(i):
        b = next_block(i, False)
        q_blk = jnp.clip(b - blk_q, 0, blk_kv - blk_q - 1)
        ga_blk = (D_ATTN // GBLK) + jnp.clip(b - blk_ga, 0, n_blk - blk_ga - 1)
        return jnp.where(b < blk_kv, q_blk, jnp.where(b == blk_kv, 2 * D_ATTN // GBLK, ga_blk))

    def tok(i):
        return (jnp.clip(i - n_blk, 0, n_tiles - 1), 0)

    n_piece = n_slots * n_sub
    dma = pltpu.SemaphoreType.DMA
    return pl.pallas_call(
        body, name="bwd_in", grid=(n_steps,),
        out_shape=(jax.ShapeDtypeStruct((seq, D_MODEL), F32), jax.ShapeDtypeStruct(small.shape, F32),
                   jax.ShapeDtypeStruct((chip_rows, D_MODEL), F32)),
        in_specs=[pl.BlockSpec((seq, GBLK), lambda i: (0, next_block(i, True))),
                  pl.BlockSpec((seq, GBLK), lambda i: (0, dqg_block(i))),
                  pl.BlockSpec((GBLK, D_MODEL), lambda i: (blk_at(i), 0)),
                  _resident(h.shape),
                  pl.BlockSpec((tile, D_MODEL), tok), _resident((1, D_MODEL)), pl.BlockSpec((tile, D_MODEL), tok),
                  _resident(small.shape)],
        out_specs=(pl.BlockSpec((tile, D_MODEL), tok), pl.BlockSpec(small.shape, lambda i: (0, 0)),
                   pl.BlockSpec(memory_space=pl.ANY)),
        scratch_shapes=[pltpu.VMEM((seq, D_MODEL), F32), pltpu.VMEM((1, D_MODEL), F32),
                        pltpu.VMEM((n_slots, GBLK, D_MODEL), F32), pltpu.VMEM((PAIR_RING, GBLK, D_MODEL), BF16),
                        pltpu.VMEM((n_slots, GBLK, D_MODEL), BF16), pltpu.VMEM((n_slots, GBLK, D_MODEL), BF16),
                        pltpu.VMEM((chip_rows, D_MODEL), BF16), pltpu.VMEM((8,) + small.shape, F32),
                        dma((n_slots,)), dma((n_slots,)), dma((n_piece,)), dma((n_piece,)), dma((n_piece,)),
                        dma((n_piece,)), dma((n_piece,)), dma((n_piece,)), dma((7,)), dma((7,)), dma((1,))],
        compiler_params=_params(62, ("arbitrary",)),
    )(dpc, dqg, wt, h, x, norm_in, dx2, small)


def _accumulate(dst_ref, src_ref, rows=16):
    def step(i, carry):
        sl = pl.ds(pl.multiple_of(i * rows, rows), rows)
        dst_ref[sl, :] = dst_ref[sl, :] + src_ref[sl, :].astype(F32)
        return carry
    lax.fori_loop(0, dst_ref.shape[0] // rows, step, 0)


def _rs_wout_scratch(gwo_shape):
    o_half, width = gwo_shape[0] // N_CHIPS // 2, gwo_shape[1]
    return [pltpu.VMEM((4, o_half, width), F32), pltpu.VMEM((4, o_half, width), BF16),
            pltpu.VMEM((4, o_half, width), BF16), pltpu.VMEM((2, o_half, width), BF16),
            pltpu.VMEM((o_half, width), BF16),
            pltpu.SemaphoreType.DMA((8,)), pltpu.SemaphoreType.DMA((8,)), pltpu.SemaphoreType.DMA((4,))]


def _rs_wout_stages(gwo_ref, gwo_sh, acc_o, sb_o, r1o, r2o, r3o, send_sems, recv_sems, local_sems):
    o_rows = gwo_ref.shape[0] // N_CHIPS
    o_half = o_rows // 2
    x, y, c = lax.axis_index("x"), lax.axis_index("y"), lax.axis_index("c")
    j = 2 * x + y
    pa = (_xor(x, 1 - c), _xor(y, c), c)
    pb = (_xor(x, c), _xor(y, 1 - c), c)
    sib = (x, y, 1 - c)
    ja = 2 * pa[0] + pa[1]
    jb = 2 * pb[0] + pb[1]
    jd = 3 - j
    order = (ja, jd, jb, j)
    sib_order = (jb, jd, ja, j)

    def rcopy(k, src, dst, to):
        return pltpu.make_async_remote_copy(src_ref=src, dst_ref=dst, send_sem=send_sems.at[k],
                                            recv_sem=recv_sems.at[k], device_id=to, device_id_type=MESH)

    def o_rows_of(chip, half):
        return gwo_ref.at[pl.ds(pl.multiple_of(chip * o_rows + half * o_half, 8), o_half), :]

    def load_all(chips, half):
        cps = [pltpu.make_async_copy(o_rows_of(chip, half), acc_o.at[s], local_sems.at[s]) for s, chip in enumerate(chips)]
        for cp in cps:
            cp.start()
        return cps

    def pair_send(s):
        return rcopy(s, sb_o.at[s], r1o.at[s], sib)

    def out_half(half):
        return gwo_sh.at[pl.ds(pl.multiple_of(half * o_half, 8), o_half), :]

    hop1 = [rcopy(4 + s, sb_o.at[s], r2o.at[s], pa) for s in range(2)]
    hop2 = rcopy(6, sb_o.at[2], r3o, pb)
    swap = rcopy(7, acc_o.at[3], out_half(c), sib)
    mine = pltpu.make_async_copy(acc_o.at[3], out_half(c), local_sems.at[0])

    def resend(s, copy):
        pair_send(s).wait_send()
        _cast_rows(acc_o.at[s], sb_o.at[s])
        copy.start()

    def stage_pair():
        for s, cp in enumerate(load_all(sib_order, 1 - c)):
            cp.wait()
            _cast_rows(acc_o.at[s], sb_o.at[s])
            pair_send(s).start()

    def stage_hop1():
        for s, cp in enumerate(load_all(order, c)):
            cp.wait()
            pair_send(s).wait_recv()
            _accumulate(acc_o.at[s], r1o.at[s])
            if s < 2:
                resend(s, hop1[s])

    def stage_hop2():
        hop1[1].wait_recv()
        _accumulate(acc_o.at[2], r2o.at[1])
        resend(2, hop2)
        hop1[0].wait_recv()
        _accumulate(acc_o.at[3], r2o.at[0])

    def stage_final():
        hop2.wait_recv()
        _accumulate(acc_o.at[3], r3o)
        mine.start()
        swap.start()

    def stage_drain():
        rcopy(7, acc_o.at[3], out_half(1 - c), sib).wait_recv()
        for cp in [pair_send(3)] + hop1 + [hop2, swap]:
            cp.wait_send()
        mine.wait()

    return stage_pair, stage_hop1, stage_hop2, stage_final, stage_drain


def _adamw(name, w, g, m, v, rows):
    def body(w_ref, g_ref, m_ref, v_ref, go_ref, d_ref, nm_ref, nv_ref):
        _adamw_update(w_ref, g_ref[...], m_ref, v_ref, go_ref, d_ref, nm_ref, nv_ref)

    spec = pl.BlockSpec((rows, w.shape[1]), lambda i: (i, 0))
    shape = jax.ShapeDtypeStruct(w.shape, F32)
    return pl.pallas_call(
        body, name="adamw_" + name, grid=(w.shape[0] // rows,),
        out_shape=(shape,) * 4, in_specs=[spec] * 4, out_specs=(spec,) * 4,
        compiler_params=_params(32, ("arbitrary",)),
    )(w, g, m, v)


def _adamw_update(w_ref, gv, m_ref, v_ref, go_ref, d_ref, nm_ref, nv_ref, at=...):
    go_ref[at] = gv
    nm = ADAM_B1 * m_ref[at] + (1.0 - ADAM_B1) * gv
    nv = ADAM_B2 * v_ref[at] + (1.0 - ADAM_B2) * (gv * gv)
    m_hat = nm / (1.0 - ADAM_B1 ** ADAM_STEP)
    v_hat = nv / (1.0 - ADAM_B2 ** ADAM_STEP)
    d_ref[at] = -ADAM_LR * (m_hat / (jnp.sqrt(v_hat) + ADAM_EPS) + ADAM_WD * w_ref[at])
    nm_ref[at] = nm
    nv_ref[at] = nv


def _adamw_small(chip, small_sum, weights, grads_of, ms, vs):
    n = len(weights)

    def body(chip_ref, small_ref, *refs):
        ins, outs, loss_ref = refs[:3 * n], refs[3 * n:-1], refs[-1]
        for k in range(n):
            w_ref, m_ref, v_ref = ins[3 * k:3 * k + 3]
            for at, gv in grads_of[k](small_ref, chip_ref):
                _adamw_update(w_ref, gv, m_ref, v_ref, *outs[4 * k:4 * k + 4], at=at)
        loss_ref[...] = small_ref[SMALL_MISC:SMALL_MISC + 1, SMALL_LOSS_LANE:SMALL_LOSS_LANE + 1]

    flat = [a for group in zip(weights, ms, vs) for a in group]
    vmem = pl.BlockSpec(memory_space=pltpu.VMEM)
    out = pl.pallas_call(
        body, name="adamw_small",
        out_shape=tuple(jax.ShapeDtypeStruct(w.shape, F32) for w in weights for _ in range(4))
        + (jax.ShapeDtypeStruct((1, 1), F32),),
        in_specs=[pl.BlockSpec(memory_space=pltpu.SMEM)] + [vmem] * (1 + 3 * n), out_specs=(vmem,) * (4 * n + 1),
    )(chip, small_sum, *flat)
    return [tuple(out[4 * k:4 * k + 4]) for k in range(n)], out[-1][0, 0]


def kernel(x, norm_in, w_in, conv_w, attn_sinks, norm_conv_out, norm_attn_out, w_out, norm_final, loss_target, m_norm_in, m_w_in, m_conv_w, m_attn_sinks, m_norm_conv_out, m_norm_attn_out, m_w_out, m_norm_final, v_norm_in, v_w_in, v_conv_w, v_attn_sinks, v_norm_conv_out, v_norm_attn_out, v_w_out, v_norm_final):
    chip = 2 * lax.axis_index("x") + lax.axis_index("y")
    xs, target = x[0], loss_target[0]
    norm_final2 = norm_final.reshape(1, D_MODEL)
    w_in_t, m_w_in_t, v_w_in_t = w_in[0].T, m_w_in[0].T, v_w_in[0].T

    def from_hbm(*arrays):
        return tuple(pltpu.with_memory_space_constraint(a, pltpu.HBM) for a in arrays)

    h, pc, q, kv, ga, oc, ya, oa, probs, sink_probs, wt, cw, wo = _fwd_in(
        xs, norm_in, w_in_t, w_out[0], conv_w.transpose(1, 0, 2), norm_conv_out, attn_sinks, norm_attn_out)
    dx2, doa, gwo, dpc, small = _out_proj_loss(xs, oc, oa, wo, norm_final2, target, pc, cw, norm_conv_out)
    q, kv, ga, ya, doa, probs, gwo, small = from_hbm(q, kv, ga, ya, doa, probs, gwo, small)
    dqg, small, gwo_sh = _attn_bwd(q, kv, ga, ya, doa, probs, sink_probs, norm_attn_out, gwo, small)
    grad_x, small_sum, gwt_sh = _bwd_in(dpc, dqg, h, wt, xs, norm_in, dx2, small)

    up_w_in = tuple(o.T[None] for o in _adamw("w_in", *from_hbm(w_in_t, gwt_sh, m_w_in_t, v_w_in_t), 200))
    up_w_out = tuple(o[None] for o in _adamw("w_out", *from_hbm(w_out[0], gwo_sh, m_w_out[0], v_w_out[0]), 128))

    def row_of(r):
        return lambda small_ref, chip_ref: [(..., small_ref[r:r + 1, :])]

    def sink_lanes(small_ref, chip_ref):
        return [(..., small_ref[SMALL_MISC:SMALL_MISC + 1, 0:N_HEADS])]

    def conv_taps(small_ref, chip_ref):
        width = D_CONV // N_CHIPS
        cols = pl.ds(pl.multiple_of(chip_ref[0] * width, width), width)
        return [(k, small_ref[pl.ds(SMALL_CONV_W + k, 1), cols]) for k in range(conv_w.shape[1])]

    def small_view(a):
        return a.transpose(1, 0, 2) if a.ndim == 3 else a.reshape(-1, a.shape[-1])

    small_w = (norm_in, conv_w, attn_sinks, norm_conv_out, norm_attn_out, norm_final)
    small_m = (m_norm_in, m_conv_w, m_attn_sinks, m_norm_conv_out, m_norm_attn_out, m_norm_final)
    small_v = (v_norm_in, v_conv_w, v_attn_sinks, v_norm_conv_out, v_norm_attn_out, v_norm_final)
    small_g = (row_of(SMALL_NORM_IN), conv_taps, sink_lanes, row_of(SMALL_NORM_CONV), row_of(SMALL_NORM_ATTN),
               row_of(SMALL_NORM_FINAL))
    w_views, m_views, v_views = (tuple(small_view(a) for a in group) for group in (small_w, small_m, small_v))
    up_small, loss = _adamw_small(chip.astype(jnp.int32).reshape(1), small_sum, w_views, small_g, m_views, v_views)
    up_small = [tuple(o.transpose(1, 0, 2) if w.ndim == 3 else o.reshape(w.shape) for o in up)
                for up, w in zip(up_small, small_w)]
    up_norm_in, up_conv_w, up_sinks, up_norm_conv, up_norm_attn, up_norm_final = up_small
    updates = (up_norm_in, up_w_in, up_conv_w, up_sinks, up_norm_conv, up_norm_attn, up_w_out, up_norm_final)
    grads_out, deltas, new_m, new_v = zip(*updates)
    return (loss, grad_x[None], *grads_out, *deltas, *new_m, *new_v)
```

```python
import jax
import jax.numpy as jnp
from jax import lax
from jax.experimental import pallas as pl
from jax.experimental.pallas import tpu as pltpu

F32 = jnp.float32
BF16 = jnp.bfloat16
MESH = pl.DeviceIdType.MESH

D_MODEL = 1024
D_CONV = 1024
D_ATTN = 1024
D_KV = 128
D_QG = 2 * D_ATTN + 2 * D_KV
D_MIX = D_CONV + D_ATTN
D_PC = 4 * D_CONV
D_IN_PROJ = D_PC + 2 * D_ATTN + 2 * D_KV
ROW_Q = D_PC
ROW_KV = ROW_Q + D_ATTN
ROW_GA = ROW_KV + 2 * D_KV
N_HEADS = 16
HEAD_DIM = 64
HEADS_PER_KV = 8
BLK = 128
N_CHIPS = 4
RMS_EPS = 1e-5
SCALE = HEAD_DIM ** -0.5
SLOPES = tuple(2.0 ** (-8.0 * (h + 1) / N_HEADS) for h in range(N_HEADS))

ADAM_LR, ADAM_B1, ADAM_B2, ADAM_EPS, ADAM_WD, ADAM_STEP = 0.001, 0.9, 0.999, 1e-08, 0.01, 10

SMALL_ROWS = 8
SMALL_NORM_IN, SMALL_NORM_CONV, SMALL_NORM_ATTN, SMALL_NORM_FINAL, SMALL_CONV_W, SMALL_MISC = 0, 1, 2, 3, 4, 7
SMALL_LOSS_LANE = N_HEADS

TOK_TILE = 256
PC_PIECE = 512
MIB = 1 << 20


def _params(vmem_mib, semantics=None):
    return pltpu.CompilerParams(dimension_semantics=semantics, vmem_limit_bytes=vmem_mib * MIB)


def _nn(a, b):
    return jnp.dot(a, b, preferred_element_type=F32)


def _nt(a, b):
    return lax.dot_general(a, b, (((1,), (1,)), ((), ())), preferred_element_type=F32)


def _tn(a, b):
    return lax.dot_general(a, b, (((0,), (0,)), ((), ())), preferred_element_type=F32)


def _rstd(v):
    return lax.rsqrt(jnp.mean(v * v, axis=-1, keepdims=True) + RMS_EPS)


def _rms_bwd(g, xhat, rstd):
    return rstd * (g - xhat * jnp.mean(g * xhat, axis=-1, keepdims=True))


def _silu_and_grad(g):
    s = jax.nn.sigmoid(g)
    return g * s, s * (1.0 + g * (1.0 - s))


def _resident(shape):
    return pl.BlockSpec(shape, lambda *_: (0,) * len(shape), pipeline_mode=pl.Buffered(1))


def _xor(a, b):
    return a + b - 2 * a * b


def _cast_rows(src_ref, dst_ref, rows=32):
    def step(i, carry):
        sl = pl.ds(pl.multiple_of(i * rows, rows), rows)
        dst_ref[sl, :] = src_ref[sl, :].astype(dst_ref.dtype)
        return carry
    lax.fori_loop(0, src_ref.shape[0] // rows, step, 0)


def _ag_scratch(shard_shape):
    rows, width = shard_shape
    return [pltpu.VMEM((rows, width), F32), pltpu.VMEM((rows, width), BF16), pltpu.VMEM((3, rows // 2, width), BF16),
            pltpu.SemaphoreType.DMA((6,)), pltpu.SemaphoreType.DMA((6,)), pltpu.SemaphoreType.DMA((4,))]


def _ag_stages(sh_ref, out, f32_buf, own, land, send_sems, recv_sems, local_sems):
    rows = sh_ref.shape[0]
    half = rows // 2
    x, y, c = lax.axis_index("x"), lax.axis_index("y"), lax.axis_index("c")
    j = 2 * x + y
    p1 = (_xor(x, c), _xor(y, 1 - c), c)
    p2 = (_xor(x, 1 - c), _xor(y, c), c)
    sib = (x, y, 1 - c)
    j1 = 2 * p1[0] + p1[1]
    j2 = 2 * p2[0] + p2[1]
    j3 = 3 - j

    def rows_of(chip, hf):
        return out.at[pl.ds(pl.multiple_of(chip * rows + hf * half, 16), half), :]

    def rcopy(k, src, dst, to):
        return pltpu.make_async_remote_copy(src_ref=src, dst_ref=dst, send_sem=send_sems.at[k],
                                            recv_sem=recv_sems.at[k], device_id=to, device_id_type=MESH)

    my_half = own.at[pl.ds(pl.multiple_of(c * half, 16), half), :]
    hop1 = rcopy(0, my_half, land.at[0], p1)
    hop2_own = rcopy(1, my_half, land.at[1], p2)
    hop2_fwd = rcopy(2, land.at[0], land.at[2], p2)
    swaps = [rcopy(3 + s, land.at[s], rows_of(chip, c), sib) for s, chip in enumerate((j1, j2, j3))]
    keeps = [pltpu.make_async_copy(land.at[s], rows_of(chip, c), local_sems.at[1 + s]) for s, chip in enumerate((j1, j2, j3))]
    load = pltpu.make_async_copy(sh_ref, f32_buf, local_sems.at[0])
    own_out = pltpu.make_async_copy(own, out.at[pl.ds(pl.multiple_of(j * rows, 16), rows), :], local_sems.at[0])

    def stage_send():
        load.start()
        load.wait()
        _cast_rows(f32_buf, own)
        own_out.start()
        hop1.start()

    def stage_forward():
        hop1.wait_recv()
        hop2_own.start()
        hop2_fwd.start()
        swaps[0].start()
        keeps[0].start()

    def stage_publish():
        hop2_own.wait_recv()
        swaps[1].start()
        keeps[1].start()
        hop2_fwd.wait_recv()
        swaps[2].start()
        keeps[2].start()

    def stage_drain():
        for s, chip in enumerate((j2, j1, j3)):
            rcopy(3 + s, my_half, rows_of(chip, 1 - c), sib).wait_recv()
        for cp in [hop1, hop2_own, hop2_fwd] + swaps:
            cp.wait_send()
        for cp in [own_out] + keeps:
            cp.wait()

    return stage_send, stage_forward, stage_publish, stage_drain


AG_CAST_ROWS = 400


def _gather_resident(sh_ref, out, f32_buf, send_sems, recv_sems, local_sems):
    rows = sh_ref.shape[0]
    half = rows // 2
    x, y, c = lax.axis_index("x"), lax.axis_index("y"), lax.axis_index("c")
    j = 2 * x + y
    p1 = (_xor(x, c), _xor(y, 1 - c), c)
    p2 = (_xor(x, 1 - c), _xor(y, c), c)
    sib = (x, y, 1 - c)
    j1 = 2 * p1[0] + p1[1]
    j2 = 2 * p2[0] + p2[1]
    j3 = 3 - j

    def rows_of(chip, hf):
        return out.at[pl.ds(pl.multiple_of(chip * rows + hf * half, 16), half), :]

    def send(k, chip, to):
        return pltpu.make_async_remote_copy(src_ref=rows_of(chip, c), dst_ref=rows_of(chip, c), send_sem=send_sems.at[k],
                                            recv_sem=recv_sems.at[k], device_id=to, device_id_type=MESH)

    per_half = half // AG_CAST_ROWS

    def cast_own(chunk):
        lo = pl.multiple_of(chunk * AG_CAST_ROWS, 16)
        load = pltpu.make_async_copy(sh_ref.at[pl.ds(lo, AG_CAST_ROWS), :], f32_buf, local_sems.at[0])
        load.start()
        load.wait()
        _cast_rows(f32_buf, out.at[pl.ds(pl.multiple_of(j * rows + lo, 16), AG_CAST_ROWS), :], rows=16)

    for k in range(per_half):
        cast_own(c * per_half + k)
    hop1 = send(0, j, p1)
    hop1.start()
    for k in range(per_half):
        cast_own((1 - c) * per_half + k)
    hop1.wait_recv()
    sends = [hop1, send(1, j, p2), send(2, j1, p2), send(3, j1, sib)]
    for cp in sends[1:]:
        cp.start()
    sends[1].wait_recv()
    sends.append(send(4, j2, sib))
    sends[-1].start()
    sends[2].wait_recv()
    sends.append(send(5, j3, sib))
    sends[-1].start()
    for k in (3, 4, 5):
        send(k, j, sib).wait_recv()
    for cp in sends:
        cp.wait_send()


def _fwd_in(x, norm_in, wt_sh, wo_sh, cw_sh, norm_conv_out, sinks, norm_attn_out):
    seq = x.shape[0]
    tile = TOK_TILE
    n_tiles = seq // tile
    stage_steps = (0, n_tiles // 4, (5 * n_tiles) // 8, n_tiles - 1)
    blocks = tile // BLK
    cw_cols = cw_sh.shape[-1]

    def body(x_ref, g_ref, wtsh_ref, wo_ref, cwsh_ref, gn_ref, sink_ref, gna_ref,
             h_ref, pc_ref, q_ref, kv_ref, ga_ref, oc_ref, ya_ref, oa_ref, pr_ref, sp_ref, wt_out, cw_ref, wo_out,
             zbuf, kv_last, wt_ref, f32_buf, cw_land, wt_send, wt_recv, wt_local, cw_send, cw_recv, cw_local, *ag_scratch):
        step = pl.program_id(0)
        to_hbm = pltpu.make_async_copy(wt_ref, wt_out, wt_local.at[0])

        @pl.when(step == 0)
        def _():
            zbuf[0:8, :] = jnp.zeros((8, D_CONV), F32)
            kv_last[...] = jnp.zeros_like(kv_last)
            x_i, y_i, c = lax.axis_index("x"), lax.axis_index("y"), lax.axis_index("c")
            j = 2 * x_i + y_i
            p1 = (_xor(x_i, c), _xor(y_i, 1 - c), c)
            p2 = (_xor(x_i, 1 - c), _xor(y_i, c), c)
            j1 = 2 * p1[0] + p1[1]

            def cw_copy(k, src, chip, to):
                return pltpu.make_async_remote_copy(src_ref=src, dst_ref=cw_land.at[chip], send_sem=cw_send.at[k],
                                                    recv_sem=cw_recv.at[k], device_id=to, device_id_type=MESH)

            mine = pltpu.make_async_copy(cwsh_ref, cw_land.at[j], cw_local.at[0])
            mine.start()
            first = cw_copy(0, cwsh_ref, j, p1)
            first.start()
            _gather_resident(wtsh_ref, wt_ref, f32_buf, wt_send, wt_recv, wt_local)
            to_hbm.start()
            first.wait_recv()
            second = [cw_copy(1, cwsh_ref, j, p2), cw_copy(2, cw_land.at[j1], j1, p2)]
            for cp in second:
                cp.start()
            for cp in second:
                cp.wait_recv()
            for cp in [first] + second:
                cp.wait_send()
            mine.wait()
            for chip in range(N_CHIPS):
                for tap in range(cw_sh.shape[0]):
                    cw_ref[tap:tap + 1, chip * cw_cols:(chip + 1) * cw_cols] = cw_land[chip, tap]

        stages = _ag_stages(wo_ref, wo_out, *ag_scratch)
        for at, stage in zip(stage_steps[:-1], stages[:-1]):
            pl.when(step == at)(stage)

        def attention(b):
            rows = pl.ds(b * BLK, BLK)
            kv_prev = kv_last if b == 0 else kv_ref.at[pl.ds((b - 1) * BLK, BLK), :]
            return _attn_forward(q_ref.at[rows, :], kv_ref.at[rows, :], kv_prev, ga_ref.at[rows, :], sink_ref, gna_ref,
                                 _band_geometry(step * blocks + b), ya_ref.at[rows, :], oa_ref.at[rows, :],
                                 pr_ref.at[rows, :], sp_ref.at[rows, :])

        xv = x_ref[...]
        h = (xv * _rstd(xv) * g_ref[...]).astype(BF16)
        h_ref[...] = h
        q_ref[...] = _nt(h, wt_ref[ROW_Q:ROW_KV, :])
        kv_ref[...] = _nt(h, wt_ref[ROW_KV:ROW_GA, :])
        ga_ref[...] = _nt(h, wt_ref[ROW_GA:D_IN_PROJ, :])
        attention_blocks = [attention(b) for b in range(blocks)]
        for lo in range(0, D_PC, PC_PIECE):
            pc_ref[:, lo:lo + PC_PIECE] = _nt(h, wt_ref[lo:lo + PC_PIECE, :])
            for stages_of_block in attention_blocks:
                next(stages_of_block, None)
        for stages_of_block in attention_blocks:
            for _ in stages_of_block:
                pass
        kv_last[...] = kv_ref[tile - BLK:tile, :]

        cb, _, _, _, _, _, conv = _conv_core(pc_ref, zbuf, cw_ref)
        yc = cb * conv
        silu, _ = _silu_and_grad(pc_ref[:, 3 * D_CONV:4 * D_CONV])
        oc_ref[...] = (yc * _rstd(yc) * gn_ref[...] * silu).astype(BF16)
        zbuf[0:8, :] = zbuf[tile:tile + 8, :]

        @pl.when(step == stage_steps[-1])
        def _():
            stages[-1]()
            to_hbm.wait()

    def row(width):
        return pl.BlockSpec((tile, width), lambda i: (i, 0))

    any_spec = pl.BlockSpec(memory_space=pl.ANY)
    dma = pltpu.SemaphoreType.DMA
    wt_shape = (N_CHIPS * wt_sh.shape[0], wt_sh.shape[1])
    cw_shape = (cw_sh.shape[0], N_CHIPS * cw_cols)
    return pl.pallas_call(
        body, name="fwd_in", grid=(n_tiles,),
        out_shape=(jax.ShapeDtypeStruct((seq, D_MODEL), BF16), jax.ShapeDtypeStruct((seq, D_PC), F32),
                   jax.ShapeDtypeStruct((seq, D_ATTN), F32), jax.ShapeDtypeStruct((seq, 2 * D_KV), F32),
                   jax.ShapeDtypeStruct((seq, D_ATTN), F32), jax.ShapeDtypeStruct((seq, D_CONV), BF16),
                   pltpu.HBM((seq, D_ATTN), F32), pltpu.HBM((seq, D_ATTN), BF16),
                   pltpu.HBM((seq, N_HEADS * BLK), BF16), pltpu.HBM((seq, 128), F32),
                   pltpu.HBM(wt_shape, BF16), jax.ShapeDtypeStruct(cw_shape, F32),
                   jax.ShapeDtypeStruct((N_CHIPS * wo_sh.shape[0], wo_sh.shape[1]), BF16)),
        in_specs=[row(D_MODEL), _resident((1, D_MODEL)), any_spec, any_spec, any_spec, _resident((1, D_CONV)),
                  pl.BlockSpec(memory_space=pltpu.SMEM), _resident((1, D_ATTN))],
        out_specs=(row(D_MODEL), row(D_PC), row(D_ATTN), row(2 * D_KV), row(D_ATTN), row(D_CONV), row(D_ATTN),
                   row(D_ATTN), row(N_HEADS * BLK), row(128), any_spec, pl.BlockSpec(cw_shape, lambda i: (0, 0)),
                   any_spec),
        scratch_shapes=[pltpu.VMEM((tile + 8, D_CONV), F32), pltpu.VMEM((BLK, 2 * D_KV), F32),
                        pltpu.VMEM(wt_shape, BF16), pltpu.VMEM((AG_CAST_ROWS, wt_sh.shape[1]), F32),
                        pltpu.VMEM((N_CHIPS,) + cw_sh.shape, F32),
                        dma((6,)), dma((6,)), dma((1,)), dma((3,)), dma((3,)), dma((1,))] + _ag_scratch(wo_sh.shape),
        compiler_params=_params(62, ("arbitrary",)),
    )(x, norm_in, wt_sh, wo_sh, cw_sh, norm_conv_out, sinks, norm_attn_out)


def _conv_core(pc_ref, zbuf, cw_ref):
    tile = pc_ref.shape[0]
    cb = pc_ref[:, 0:D_CONV]
    cc = pc_ref[:, D_CONV:2 * D_CONV]
    cu = pc_ref[:, 2 * D_CONV:3 * D_CONV]
    z = cc * cu
    zbuf[8:tile + 8, :] = z
    z1 = zbuf[7:tile + 7, :]
    z2 = zbuf[6:tile + 6, :]
    conv = cw_ref[0:1, :] * z2 + cw_ref[1:2, :] * z1 + cw_ref[2:3, :] * z
    return cb, cc, cu, z, z1, z2, conv


def _band_geometry(block_index):
    qi = lax.broadcasted_iota(jnp.int32, (BLK, BLK), 0)
    kp = lax.broadcasted_iota(jnp.int32, (BLK, BLK), 1)
    use_cur = kp <= qi
    dist = jnp.where(use_cur, qi - kp, qi - kp + BLK).astype(F32)
    valid = use_cur | (block_index > 0)
    return use_cur, dist, valid


def _block_diag(cur, prev, group):
    lane = lax.broadcasted_iota(jnp.int32, cur.shape, 1)

    def halves(t):
        other = pltpu.roll(t, 64, 1)
        lo, hi = (t, other) if group == 0 else (other, t)
        return jnp.where(lane < 64, lo, 0.0), jnp.where(lane >= 64, hi, 0.0)

    return jnp.concatenate(halves(cur) + halves(prev), axis=0).astype(BF16)


def _merge(s4, use_cur):
    return (jnp.where(use_cur, s4[:, 0:BLK], s4[:, 2 * BLK:3 * BLK]),
            jnp.where(use_cur, s4[:, BLK:2 * BLK], s4[:, 3 * BLK:4 * BLK]))


def _split(a, b, use_cur):
    return jnp.concatenate([jnp.where(use_cur, a, 0.0), jnp.where(use_cur, b, 0.0),
                            jnp.where(use_cur, 0.0, a), jnp.where(use_cur, 0.0, b)], axis=1)


def _softmax_head(s, head, sink, dist, valid):
    sc = jnp.where(valid, s - SLOPES[head] * dist, -jnp.inf)
    m = jnp.maximum(jnp.max(sc, axis=-1, keepdims=True), sink)
    p = jnp.exp(sc - m)
    es = jnp.exp(sink - m)
    inv = 1.0 / (jnp.sum(p, axis=-1, keepdims=True) + es)
    return p * inv, es * inv


def _attn_operands(q_ref, kvc_ref, kvp_ref):
    groups = range(N_HEADS // HEADS_PER_KV)
    kbd = [_block_diag(kvc_ref[:, 0:D_KV], kvp_ref[:, 0:D_KV], g) for g in groups]
    vbd = [_block_diag(kvc_ref[:, D_KV:2 * D_KV], kvp_ref[:, D_KV:2 * D_KV], g) for g in groups]
    qps = [(q_ref[:, j * 128:(j + 1) * 128] * SCALE).astype(BF16) for j in range(N_HEADS // 2)]
    return qps, kbd, vbd


def _attn_forward(q_ref, kvc_ref, kvp_ref, ga_ref, sink_ref, gn_ref, geometry, ya_ref, oa_ref, pr_ref, sp_ref):
    use_cur, dist, valid = geometry
    _, kbd, vbd = operands = _attn_operands(q_ref, kvc_ref, kvp_ref)
    yield
    scores = []
    for j, qp in enumerate(operands[0]):
        scores += _merge(_nt(qp, kbd[j // 4]), use_cur)
    yield
    sinks = [sink_ref[0, h] for h in range(N_HEADS)]
    scores = [jnp.where(valid, s - SLOPES[h] * dist, -jnp.inf) for h, s in enumerate(scores)]
    maxes = [jnp.maximum(jnp.max(s, axis=-1, keepdims=True), sinks[h]) for h, s in enumerate(scores)]
    yield
    exps = [jnp.exp(s - m) for s, m in zip(scores, maxes)]
    sink_exps = [jnp.exp(sinks[h] - m) for h, m in enumerate(maxes)]
    yield
    invs = [1.0 / (jnp.sum(e, axis=-1, keepdims=True) + se) for e, se in zip(exps, sink_exps)]
    probs = [e * inv for e, inv in zip(exps, invs)]
    pr_ref[...] = jnp.concatenate(probs, axis=1).astype(BF16)
    lane = lax.broadcasted_iota(jnp.int32, (BLK, 128), 1)
    sp_ref[...] = sum(jnp.where(lane == h, se * inv, 0.0) for h, (se, inv) in enumerate(zip(sink_exps, invs)))
    yield
    p4s = [_split(probs[2 * j], probs[2 * j + 1], use_cur).astype(BF16) for j in range(N_HEADS // 2)]
    ya = jnp.concatenate([_nn(p4, vbd[j // 4]) for j, p4 in enumerate(p4s)], axis=1)
    ya_ref[...] = ya
    yield
    silu, _ = _silu_and_grad(ga_ref[...])
    oa_ref[...] = (ya * _rstd(ya) * gn_ref[...] * silu).astype(BF16)


def _kv_specs(n_blocks, reverse):
    def blk(i):
        return (n_blocks - 1 - i) if reverse else i
    cur = pl.BlockSpec((BLK, 2 * D_KV), lambda i: (blk(i), 0))
    prev = pl.BlockSpec((BLK, 2 * D_KV), lambda i: (jnp.maximum(blk(i) - 1, 0), 0))
    return cur, prev


def _out_proj_loss(x, oc, oa, wo, norm_final, target, pc, conv_w, norm_conv_out):
    seq = x.shape[0]
    tile = TOK_TILE
    n_tiles = seq // tile

    def body(x_ref, oc_ref, oa_ref, wo_ref, gf_ref, t_ref, pc_ref, hcc_ref, hcu_ref, cw_ref, gn_ref,
             dx2_ref, doa_ref, gwo_ref, dpc_ref, small_ref, loss_acc, zbuf, dbuf):
        step = pl.program_id(0)

        def small_add(row, value):
            small_ref[row:row + 1, :] += jnp.sum(value, axis=0, keepdims=True)

        @pl.when(step == 0)
        def _():
            gwo_ref[...] = jnp.zeros_like(gwo_ref)
            small_ref[...] = jnp.zeros_like(small_ref)
            loss_acc[...] = jnp.zeros_like(loss_acc)
            dbuf[tile:tile + 8, :] = jnp.zeros((8, D_CONV), F32)

        oc, oa = oc_ref[...], oa_ref[...]
        x2 = x_ref[...] + _nn(oc, wo_ref[0:D_CONV, :]) + _nn(oa, wo_ref[D_CONV:D_MIX, :])
        r = _rstd(x2)
        xhat = x2 * r
        err = xhat * gf_ref[...] - t_ref[...]
        loss_acc[...] += jnp.sum(err * err, axis=0, keepdims=True) * (0.5 / D_MODEL)
        dy = err * (1.0 / D_MODEL)
        small_add(SMALL_NORM_FINAL, dy * xhat)
        dx2 = _rms_bwd(dy * gf_ref[...], xhat, r)
        dx2_ref[...] = dx2
        db = dx2.astype(BF16)
        do = _nt(db, wo_ref[0:D_CONV, :])
        doa_ref[...] = _nt(db, wo_ref[D_CONV:D_MIX, :])
        gwo_ref[0:D_CONV, :] += _tn(oc, db)
        gwo_ref[D_CONV:D_MIX, :] += _tn(oa, db)

        is_first_tile = step == n_tiles - 1
        zbuf[0:8, :] = jnp.where(is_first_tile, 0.0, hcc_ref[...] * hcu_ref[...])
        cb, cc, cu, z, z1, z2, conv = _conv_core(pc_ref, zbuf, cw_ref)
        silu, dsilu = _silu_and_grad(pc_ref[:, 3 * D_CONV:4 * D_CONV])
        yc = cb * conv
        rc = _rstd(yc)
        chat = yc * rc
        dn = do * silu
        dpc_ref[:, 3 * D_CONV:4 * D_CONV] = (do * (chat * gn_ref[...]) * dsilu).astype(BF16)
        small_add(SMALL_NORM_CONV, dn * chat)
        dyc = _rms_bwd(dn * gn_ref[...], chat, rc)
        dpc_ref[:, 0:D_CONV] = (dyc * conv).astype(BF16)
        dconv = dyc * cb
        small_add(SMALL_CONV_W, dconv * z2)
        small_add(SMALL_CONV_W + 1, dconv * z1)
        small_add(SMALL_CONV_W + 2, dconv * z)
        dbuf[0:tile, :] = dconv
        dz = cw_ref[2:3, :] * dconv + cw_ref[1:2, :] * dbuf[1:tile + 1, :] + cw_ref[0:1, :] * dbuf[2:tile + 2, :]
        dpc_ref[:, D_CONV:2 * D_CONV] = (dz * cu).astype(BF16)
        dpc_ref[:, 2 * D_CONV:3 * D_CONV] = (dz * cc).astype(BF16)
        dbuf[tile:tile + 8, :] = dbuf[0:8, :]

        @pl.when(step == n_tiles - 1)
        def _():
            lane = lax.broadcasted_iota(jnp.int32, (1, D_MODEL), 1)
            small_ref[SMALL_MISC:SMALL_MISC + 1, :] = jnp.where(lane == SMALL_LOSS_LANE, jnp.sum(loss_acc[...]), 0.0)

    def rev(i):
        return n_tiles - 1 - i

    def row(width):
        return pl.BlockSpec((tile, width), lambda i: (rev(i), 0))

    def halo(col_block):
        return pl.BlockSpec((8, D_CONV), lambda i: (jnp.maximum(rev(i) * (tile // 8) - 1, 0), col_block))

    def const(shape):
        return pl.BlockSpec(shape, lambda i: (0, 0))

    return pl.pallas_call(
        body, name="out_proj_loss", grid=(n_tiles,),
        out_shape=(jax.ShapeDtypeStruct((seq, D_MODEL), F32), jax.ShapeDtypeStruct((seq, D_ATTN), F32),
                   jax.ShapeDtypeStruct((D_MIX, D_MODEL), F32), jax.ShapeDtypeStruct((seq, D_PC), BF16),
                   jax.ShapeDtypeStruct((SMALL_ROWS, D_MODEL), F32)),
        in_specs=[row(D_MODEL), row(D_CONV), row(D_ATTN), _resident(wo.shape), _resident((1, D_MODEL)), row(D_MODEL),
                  row(D_PC), halo(1), halo(2), _resident(conv_w.shape), _resident((1, D_CONV))],
        out_specs=(row(D_MODEL), row(D_ATTN), const((D_MIX, D_MODEL)), row(D_PC), const((SMALL_ROWS, D_MODEL))),
        scratch_shapes=[pltpu.VMEM((1, D_MODEL), F32), pltpu.VMEM((tile + 8, D_CONV), F32),
                        pltpu.VMEM((tile + 8, D_CONV), F32)],
        compiler_params=_params(62, ("arbitrary",)),
    )(x, oc, oa, wo, norm_final, target, pc, pc, pc, conv_w, norm_conv_out)


def _attn_bwd(q, kv, ga, ya, doa, probs, sink_probs, norm_attn_out, gwo, small):
    seq = q.shape[0]
    n_blocks = seq // BLK
    stage_steps = (0, n_blocks // 4, n_blocks // 2, (3 * n_blocks) // 4, n_blocks - 1)

    def body(q_ref, kvc_ref, kvp_ref, ga_ref, ya_ref, doa_ref, pr_ref, sp_ref, gn_ref, gwo_ref, small_ref,
             dqg_ref, small_out, gwo_sh, gna_ref, gs_ref, carry, dya_buf, *rs_scratch):
        step = pl.program_id(0)

        @pl.when(step == 0)
        def _():
            gna_ref[...] = jnp.zeros_like(gna_ref)
            gs_ref[...] = jnp.zeros_like(gs_ref)
            carry[...] = jnp.zeros_like(carry)

        ya = ya_ref[...]
        r = _rstd(ya)
        xhat = ya * r
        silu, dsilu = _silu_and_grad(ga_ref[...])
        do = doa_ref[...]
        dn = do * silu
        dqg_ref[:, D_ATTN:2 * D_ATTN] = (do * (xhat * gn_ref[...]) * dsilu).astype(BF16)
        gna_ref[...] += jnp.sum(dn * xhat, axis=0, keepdims=True)
        dya_buf[...] = _rms_bwd(dn * gn_ref[...], xhat, r).astype(BF16)

        use_cur = _band_geometry(n_blocks - 1 - step)[0]
        lane = lax.broadcasted_iota(jnp.int32, (BLK, 128), 1)

        def fold(bd):
            return (jnp.where(lane < 64, bd[0:BLK], 0.0) + jnp.where(lane >= 64, bd[BLK:2 * BLK], 0.0),
                    jnp.where(lane < 64, bd[2 * BLK:3 * BLK], 0.0) + jnp.where(lane >= 64, bd[3 * BLK:4 * BLK], 0.0))

        pairs = range(N_HEADS // 2)
        qps, kbd, vbd = _attn_operands(q_ref, kvc_ref, kvp_ref)
        probs = [pr_ref[:, h * BLK:(h + 1) * BLK].astype(F32) for h in range(N_HEADS)]
        dyps = [dya_buf[:, j * 128:(j + 1) * 128] for j in pairs]
        dps = []
        for j in pairs:
            dps += _merge(_nt(dyps[j], vbd[j // 4]), use_cur)
        deltas = [jnp.sum(p * dp, axis=-1, keepdims=True) for p, dp in zip(probs, dps)]
        dss = [p * (dp - delta) for p, dp, delta in zip(probs, dps, deltas)]
        delta_lanes = sum(jnp.where(lane == h, deltas[h], 0.0) for h in range(N_HEADS))
        gs_ref[...] -= jnp.sum(sp_ref[...] * delta_lanes, axis=0, keepdims=True)
        ds4s = [_split(dss[2 * j], dss[2 * j + 1], use_cur).astype(BF16) for j in pairs]
        p4s = [_split(probs[2 * j], probs[2 * j + 1], use_cur).astype(BF16) for j in pairs]
        dqg_ref[:, 0:D_ATTN] = jnp.concatenate([_nn(ds4s[j], kbd[j // 4]) * SCALE for j in pairs], axis=1).astype(BF16)
        sums = []
        for group in range(N_HEADS // HEADS_PER_KV):
            acc = [jnp.zeros((BLK, 128), F32) for _ in range(4)]
            for j in range(group * 4, group * 4 + 4):
                for slot, part in enumerate(fold(_tn(ds4s[j], qps[j])) + fold(_tn(p4s[j], dyps[j]))):
                    acc[slot] = acc[slot] + part
            sums.append([a + pltpu.roll(a, 64, 1) for a in acc])
        dk_cur, dk_prev, dv_cur, dv_prev = (jnp.where(lane < 64, a, b) for a, b in zip(sums[0], sums[1]))
        dqg_ref[:, 2 * D_ATTN:2 * D_ATTN + 2 * D_KV] = (jnp.concatenate([dk_cur, dv_cur], axis=1) + carry[...]).astype(BF16)
        carry[...] = jnp.concatenate([dk_prev, dv_prev], axis=1)

        @pl.when(step == n_blocks - 1)
        def _():
            small_out[...] = small_ref[...]
            small_out[SMALL_NORM_ATTN:SMALL_NORM_ATTN + 1, :] = gna_ref[...]
            small_out[SMALL_MISC:SMALL_MISC + 1, 0:128] = small_ref[SMALL_MISC:SMALL_MISC + 1, 0:128] + gs_ref[...]

        for at, stage in zip(stage_steps, _rs_wout_stages(gwo_ref, gwo_sh, *rs_scratch)):
            pl.when(step == at)(stage)

    row = pl.BlockSpec((BLK, D_ATTN), lambda i: (n_blocks - 1 - i, 0))
    kv_cur, kv_prev = _kv_specs(n_blocks, reverse=True)
    any_spec = pl.BlockSpec(memory_space=pl.ANY)
    return pl.pallas_call(
        body, name="attn_bwd", grid=(n_blocks,),
        out_shape=(jax.ShapeDtypeStruct((seq, D_QG), BF16), pltpu.HBM(small.shape, F32),
                   pltpu.HBM((gwo.shape[0] // N_CHIPS, gwo.shape[1]), F32)),
        in_specs=[row, kv_cur, kv_prev, row, row, row,
                  pl.BlockSpec((BLK, N_HEADS * BLK), lambda i: (n_blocks - 1 - i, 0)),
                  pl.BlockSpec((BLK, 128), lambda i: (n_blocks - 1 - i, 0)), _resident((1, D_ATTN)),
                  any_spec, _resident(small.shape)],
        out_specs=(pl.BlockSpec((BLK, D_QG), lambda i: (n_blocks - 1 - i, 0)),
                   pl.BlockSpec(small.shape, lambda i: (0, 0)), any_spec),
        scratch_shapes=[pltpu.VMEM((1, D_ATTN), F32), pltpu.VMEM((1, 128), F32),
                        pltpu.VMEM((BLK, 2 * D_KV), F32), pltpu.VMEM((BLK, D_ATTN), BF16)] + _rs_wout_scratch(gwo.shape),
        compiler_params=_params(44, ("arbitrary",)),
    )(q, kv, kv, ga, ya, doa, probs, sink_probs, norm_attn_out, gwo, small)


GBLK = 256
GSUB = 64
PAIR_RING = 4
LAG_PAIR, LAG_HOP1, LAG_HOP2 = 1, 6, 12


def _bwd_in(dpc, dqg, h, wt, x, norm_in, dx2, small):
    seq = x.shape[0]
    n_blk = D_IN_PROJ // GBLK
    per_chip = n_blk // N_CHIPS
    n_slots = (n_blk + 1) // 2
    n_sub = GBLK // GSUB
    chip_rows = D_IN_PROJ // N_CHIPS
    tile = TOK_TILE
    n_tiles = seq // tile
    n_steps = n_blk + max(n_tiles, LAG_HOP2)
    chunk = min(seq, 512)
    blk_q, blk_kv, blk_ga = ROW_Q // GBLK, ROW_KV // GBLK, ROW_GA // GBLK

    def block_of(i):
        k = i % N_CHIPS
        robin = per_chip * ((k % 2) * 2 + k // 2) + i // N_CHIPS
        if isinstance(i, int):
            return robin if i < per_chip * N_CHIPS else i
        return jnp.where(i < per_chip * N_CHIPS, robin, i)

    def owner_of(i):
        return (i // N_CHIPS) % 2

    def slot_of(i):
        return (i // (2 * N_CHIPS)) * N_CHIPS + i % N_CHIPS

    def body(dpc_ref, dqg_ref, wt_ref, h_ref, x_ref, g_ref, dx2_ref, small_ref, gx_ref, small_sum, gwt_sh,
             dh_acc, gni, keep, pbuf, xbuf, land, land2, small_land,
             pair_send, pair_recv, h1_send, h1_recv, h2_send, h2_recv, sw_send, sw_recv, sm_send, sm_recv, out_sem):
        step = pl.program_id(0)
        x_i, y_i, c = lax.axis_index("x"), lax.axis_index("y"), lax.axis_index("c")
        me = 4 * x_i + 2 * y_i + c
        j = 2 * x_i + y_i
        pa = (_xor(x_i, 1 - c), _xor(y_i, c), c)
        pb = (_xor(x_i, c), _xor(y_i, 1 - c), c)
        sib = (x_i, y_i, 1 - c)
        ja = 2 * pa[0] + pa[1]
        jb = 2 * pb[0] + pb[1]
        jd = 3 - j

        def remote(src, dst, send, recv, to):
            return pltpu.make_async_remote_copy(src_ref=src, dst_ref=dst, send_sem=send, recv_sem=recv,
                                                device_id=to, device_id_type=MESH)

        def piece(ref, slot, u, n):
            return ref.at[slot, pl.ds(u * GSUB, n * GSUB), :]

        def chip_rows_at(ref, local, n):
            return ref.at[pl.ds(pl.multiple_of(local, GSUB), n * GSUB), :]

        def pair_copy(i):
            slot = slot_of(i)
            return remote(pbuf.at[i % PAIR_RING], land.at[slot], pair_send.at[slot], pair_recv.at[slot], sib)

        def h1_copy(slot, u, n):
            k = slot * n_sub + u
            return remote(piece(xbuf, slot, u, n), piece(xbuf, slot, u, n), h1_send.at[k], h1_recv.at[k], pa)

        def h2_copy(slot, u, n, local):
            k = slot * n_sub + u
            return remote(piece(xbuf, slot, u, n), chip_rows_at(land2, local, n), h2_send.at[k], h2_recv.at[k], pb)

        def sw_copy(slot, u, n, local):
            k = slot * n_sub + u
            return remote(piece(keep, slot, u, n), chip_rows_at(gwt_sh, local, n), sw_send.at[k], sw_recv.at[k], sib)

        def owned(i):
            return (i >= 0) & (i < n_blk) & (owner_of(i) == c)

        def chip_of(blk, u):
            row = blk * GBLK + u * GSUB
            chip = row // chip_rows
            return chip, row - chip * chip_rows

        def pieces(blk):
            first, local = chip_of(blk, 0)
            whole = first == chip_of(blk, n_sub - 1)[0]
            if isinstance(blk, int):
                return [(True, 0, n_sub, first, local)] if whole else [(True, u, 1) + chip_of(blk, u) for u in range(n_sub)]
            return [(whole, 0, n_sub, first, local)] + [(jnp.logical_not(whole), u, 1) + chip_of(blk, u) for u in range(n_sub)]

        @pl.when(step == 0)
        def _():
            dh_acc[...] = jnp.zeros_like(dh_acc)
            gni[...] = jnp.zeros_like(gni)

        @pl.when(step < n_blk)
        def _():
            from_pc = block_of(step) < blk_q
            block = _tn(jnp.where(from_pc, dpc_ref[...], dqg_ref[...]), h_ref[...])
            for t in range(0, seq, chunk):
                d = jnp.where(from_pc, dpc_ref[t:t + chunk, :], dqg_ref[t:t + chunk, :])
                dh_acc[t:t + chunk, :] += _nn(d, wt_ref[...])

            @pl.when(owner_of(step) == c)
            def _():
                keep[slot_of(step)] = block

            @pl.when(owner_of(step) != c)
            def _():
                @pl.when(step >= 2 * PAIR_RING)
                def _():
                    pair_copy(step - 2 * PAIR_RING).wait_send()
                pbuf[step % PAIR_RING] = block.astype(BF16)
                pair_copy(step).start()

        i1 = step - LAG_PAIR

        @pl.when(owned(i1))
        def _():
            slot = slot_of(i1)
            pair_copy(i1).wait_recv()
            _accumulate(keep.at[slot], land.at[slot])
            for cond, u, n, chip, _ in pieces(block_of(i1)):
                @pl.when(cond & ((chip == ja) | (chip == jd)))
                def _(u=u, n=n):
                    _cast_rows(piece(keep, slot, u, n), piece(xbuf, slot, u, n))
                    h1_copy(slot, u, n).start()

        i2 = step - LAG_HOP1

        @pl.when(owned(i2))
        def _():
            slot = slot_of(i2)
            for cond, u, n, chip, local in pieces(block_of(i2)):
                @pl.when(cond & ((chip == j) | (chip == jb)))
                def _(u=u, n=n, chip=chip, local=local):
                    h1_copy(slot, u, n).wait_recv()
                    _accumulate(piece(keep, slot, u, n), piece(xbuf, slot, u, n))

                    @pl.when(chip == jb)
                    def _():
                        _cast_rows(piece(keep, slot, u, n), piece(xbuf, slot, u, n))
                        h2_copy(slot, u, n, local).start()

                @pl.when(cond & ((chip == ja) | (chip == jd)))
                def _(u=u, n=n):
                    h1_copy(slot, u, n).wait_send()

        i3 = step - LAG_HOP2

        @pl.when(owned(i3))
        def _():
            slot = slot_of(i3)
            for cond, u, n, chip, local in pieces(block_of(i3)):
                @pl.when(cond & (chip == j))
                def _(u=u, n=n, local=local):
                    h2_copy(slot, u, n, local).wait_recv()
                    _accumulate(piece(keep, slot, u, n), chip_rows_at(land2, local, n))
                    mine = pltpu.make_async_copy(piece(keep, slot, u, n), chip_rows_at(gwt_sh, local, n), out_sem.at[0])
                    mine.start()
                    sw_copy(slot, u, n, local).start()
                    mine.wait()

                @pl.when(cond & (chip == jb))
                def _(u=u, n=n, local=local):
                    h2_copy(slot, u, n, local).wait_send()

        e = step - n_blk

        @pl.when((e >= 0) & (e < n_tiles))
        def _():
            dh = dh_acc[pl.ds(pl.multiple_of(e * tile, tile), tile), :]
            xv = x_ref[...]
            r = _rstd(xv)
            xhat = xv * r
            gni[...] += jnp.sum(dh * xhat, axis=0, keepdims=True)
            gx_ref[...] = _rms_bwd(dh * g_ref[...], xhat, r) + dx2_ref[...]

        @pl.when(step == n_steps - 1)
        def _():
            small_land[me] = small_ref[...]
            small_land[me, SMALL_NORM_IN:SMALL_NORM_IN + 1, :] = gni[...]
            others = [(dx, dy, dc) for dx in (0, 1) for dy in (0, 1) for dc in (0, 1)][1:]
            sends = [remote(small_land.at[me], small_land.at[me], sm_send.at[k], sm_recv.at[k],
                            (_xor(x_i, dx), _xor(y_i, dy), _xor(c, dc))) for k, (dx, dy, dc) in enumerate(others)]
            for cp in sends:
                cp.start()
            for i in range(n_blk):
                if i + 2 * PAIR_RING >= n_blk:
                    @pl.when(owner_of(i) != c)
                    def _(i=i):
                        pair_copy(i).wait_send()
                for _, u, n, chip, local in pieces(block_of(i)):
                    @pl.when((j == chip) & (c == owner_of(i)))
                    def _(i=i, u=u, n=n, local=local):
                        sw_copy(slot_of(i), u, n, local).wait_send()

                    @pl.when((j == chip) & (c != owner_of(i)))
                    def _(i=i, u=u, n=n, local=local):
                        sw_copy(slot_of(i), u, n, local).wait_recv()
            for cp in sends:
                cp.wait_recv()
            total = small_land[0]
            for dev in range(1, 8):
                total = total + small_land[dev]
            small_sum[...] = total
            for cp in sends:
                cp.wait_send()

    def blk_at(i):
        return block_of(jnp.clip(i, 0, n_blk - 1))

    last_pc_step = max(i for i in range(n_blk) if block_of(i) < blk_q)

    def next_block(i, in_pc):
        i = jnp.clip(i, 0, n_blk - 1)
        step = jnp.full_like(i, last_pc_step if in_pc else n_blk - 1)
        for ahead in reversed(range(N_CHIPS)):
            cand = jnp.minimum(i + ahead, n_blk - 1)
            step = jnp.where((block_of(cand) < blk_q) == in_pc, cand, step)
        return block_of(step)

    def dqg_block(i):
        b = next_block(i, False)
        q_blk = jnp.clip(b - blk_q, 0, blk_kv - blk_q - 1)
        ga_blk = (D_ATTN // GBLK) + jnp.clip(b - blk_ga, 0, n_blk - blk_ga - 1)
        return jnp.where(b < blk_kv, q_blk, jnp.where(b == blk_kv, 2 * D_ATTN // GBLK, ga_blk))

    def tok(i):
        return (jnp.clip(i - n_blk, 0, n_tiles - 1), 0)

    n_piece = n_slots * n_sub
    dma = pltpu.SemaphoreType.DMA
    return pl.pallas_call(
        body, name="bwd_in", grid=(n_steps,),
        out_shape=(jax.ShapeDtypeStruct((seq, D_MODEL), F32), jax.ShapeDtypeStruct(small.shape, F32),
                   jax.ShapeDtypeStruct((chip_rows, D_MODEL), F32)),
        in_specs=[pl.BlockSpec((seq, GBLK), lambda i: (0, next_block(i, True))),
                  pl.BlockSpec((seq, GBLK), lambda i: (0, dqg_block(i))),
                  pl.BlockSpec((GBLK, D_MODEL), lambda i: (blk_at(i), 0)),
                  _resident(h.shape),
                  pl.BlockSpec((tile, D_MODEL), tok), _resident((1, D_MODEL)), pl.BlockSpec((tile, D_MODEL), tok),
                  _resident(small.shape)],
        out_specs=(pl.BlockSpec((tile, D_MODEL), tok), pl.BlockSpec(small.shape, lambda i: (0, 0)),
                   pl.BlockSpec(memory_space=pl.ANY)),
        scratch_shapes=[pltpu.VMEM((seq, D_MODEL), F32), pltpu.VMEM((1, D_MODEL), F32),
                        pltpu.VMEM((n_slots, GBLK, D_MODEL), F32), pltpu.VMEM((PAIR_RING, GBLK, D_MODEL), BF16),
                        pltpu.VMEM((n_slots, GBLK, D_MODEL), BF16), pltpu.VMEM((n_slots, GBLK, D_MODEL), BF16),
                        pltpu.VMEM((chip_rows, D_MODEL), BF16), pltpu.VMEM((8,) + small.shape, F32),
                        dma((n_slots,)), dma((n_slots,)), dma((n_piece,)), dma((n_piece,)), dma((n_piece,)),
                        dma((n_piece,)), dma((n_piece,)), dma((n_piece,)), dma((7,)), dma((7,)), dma((1,))],
        compiler_params=_params(62, ("arbitrary",)),
    )(dpc, dqg, wt, h, x, norm_in, dx2, small)


def _accumulate(dst_ref, src_ref, rows=16):
    def step(i, carry):
        sl = pl.ds(pl.multiple_of(i * rows, rows), rows)
        dst_ref[sl, :] = dst_ref[sl, :] + src_ref[sl, :].astype(F32)
        return carry
    lax.fori_loop(0, dst_ref.shape[0] // rows, step, 0)


def _rs_wout_scratch(gwo_shape):
    o_half, width = gwo_shape[0] // N_CHIPS // 2, gwo_shape[1]
    return [pltpu.VMEM((4, o_half, width), F32), pltpu.VMEM((4, o_half, width), BF16),
            pltpu.VMEM((4, o_half, width), BF16), pltpu.VMEM((2, o_half, width), BF16),
            pltpu.VMEM((o_half, width), BF16),
            pltpu.SemaphoreType.DMA((8,)), pltpu.SemaphoreType.DMA((8,)), pltpu.SemaphoreType.DMA((4,))]


def _rs_wout_stages(gwo_ref, gwo_sh, acc_o, sb_o, r1o, r2o, r3o, send_sems, recv_sems, local_sems):
    o_rows = gwo_ref.shape[0] // N_CHIPS
    o_half = o_rows // 2
    x, y, c = lax.axis_index("x"), lax.axis_index("y"), lax.axis_index("c")
    j = 2 * x + y
    pa = (_xor(x, 1 - c), _xor(y, c), c)
    pb = (_xor(x, c), _xor(y, 1 - c), c)
    sib = (x, y, 1 - c)
    ja = 2 * pa[0] + pa[1]
    jb = 2 * pb[0] + pb[1]
    jd = 3 - j
    order = (ja, jd, jb, j)
    sib_order = (jb, jd, ja, j)

    def rcopy(k, src, dst, to):
        return pltpu.make_async_remote_copy(src_ref=src, dst_ref=dst, send_sem=send_sems.at[k],
                                            recv_sem=recv_sems.at[k], device_id=to, device_id_type=MESH)

    def o_rows_of(chip, half):
        return gwo_ref.at[pl.ds(pl.multiple_of(chip * o_rows + half * o_half, 8), o_half), :]

    def load_all(chips, half):
        cps = [pltpu.make_async_copy(o_rows_of(chip, half), acc_o.at[s], local_sems.at[s]) for s, chip in enumerate(chips)]
        for cp in cps:
            cp.start()
        return cps

    def pair_send(s):
        return rcopy(s, sb_o.at[s], r1o.at[s], sib)

    def out_half(half):
        return gwo_sh.at[pl.ds(pl.multiple_of(half * o_half, 8), o_half), :]

    hop1 = [rcopy(4 + s, sb_o.at[s], r2o.at[s], pa) for s in range(2)]
    hop2 = rcopy(6, sb_o.at[2], r3o, pb)
    swap = rcopy(7, acc_o.at[3], out_half(c), sib)
    mine = pltpu.make_async_copy(acc_o.at[3], out_half(c), local_sems.at[0])

    def resend(s, copy):
        pair_send(s).wait_send()
        _cast_rows(acc_o.at[s], sb_o.at[s])
        copy.start()

    def stage_pair():
        for s, cp in enumerate(load_all(sib_order, 1 - c)):
            cp.wait()
            _cast_rows(acc_o.at[s], sb_o.at[s])
            pair_send(s).start()

    def stage_hop1():
        for s, cp in enumerate(load_all(order, c)):
            cp.wait()
            pair_send(s).wait_recv()
            _accumulate(acc_o.at[s], r1o.at[s])
            if s < 2:
                resend(s, hop1[s])

    def stage_hop2():
        hop1[1].wait_recv()
        _accumulate(acc_o.at[2], r2o.at[1])
        resend(2, hop2)
        hop1[0].wait_recv()
        _accumulate(acc_o.at[3], r2o.at[0])

    def stage_final():
        hop2.wait_recv()
        _accumulate(acc_o.at[3], r3o)
        mine.start()
        swap.start()

    def stage_drain():
        rcopy(7, acc_o.at[3], out_half(1 - c), sib).wait_recv()
        for cp in [pair_send(3)] + hop1 + [hop2, swap]:
            cp.wait_send()
        mine.wait()

    return stage_pair, stage_hop1, stage_hop2, stage_final, stage_drain


def _adamw(name, w, g, m, v, rows):
    def body(w_ref, g_ref, m_ref, v_ref, go_ref, d_ref, nm_ref, nv_ref):
        _adamw_update(w_ref, g_ref[...], m_ref, v_ref, go_ref, d_ref, nm_ref, nv_ref)

    spec = pl.BlockSpec((rows, w.shape[1]), lambda i: (i, 0))
    shape = jax.ShapeDtypeStruct(w.shape, F32)
    return pl.pallas_call(
        body, name="adamw_" + name, grid=(w.shape[0] // rows,),
        out_shape=(shape,) * 4, in_specs=[spec] * 4, out_specs=(spec,) * 4,
        compiler_params=_params(32, ("arbitrary",)),
    )(w, g, m, v)


def _adamw_update(w_ref, gv, m_ref, v_ref, go_ref, d_ref, nm_ref, nv_ref, at=...):
    go_ref[at] = gv
    nm = ADAM_B1 * m_ref[at] + (1.0 - ADAM_B1) * gv
    nv = ADAM_B2 * v_ref[at] + (1.0 - ADAM_B2) * (gv * gv)
    m_hat = nm / (1.0 - ADAM_B1 ** ADAM_STEP)
    v_hat = nv / (1.0 - ADAM_B2 ** ADAM_STEP)
    d_ref[at] = -ADAM_LR * (m_hat / (jnp.sqrt(v_hat) + ADAM_EPS) + ADAM_WD * w_ref[at])
    nm_ref[at] = nm
    nv_ref[at] = nv


def _adamw_small(chip, small_sum, weights, grads_of, ms, vs):
    n = len(weights)

    def body(chip_ref, small_ref, *refs):
        ins, outs, loss_ref = refs[:3 * n], refs[3 * n:-1], refs[-1]
        for k in range(n):
            w_ref, m_ref, v_ref = ins[3 * k:3 * k + 3]
            for at, gv in grads_of[k](small_ref, chip_ref):
                _adamw_update(w_ref, gv, m_ref, v_ref, *outs[4 * k:4 * k + 4], at=at)
        loss_ref[...] = small_ref[SMALL_MISC:SMALL_MISC + 1, SMALL_LOSS_LANE:SMALL_LOSS_LANE + 1]

    flat = [a for group in zip(weights, ms, vs) for a in group]
    vmem = pl.BlockSpec(memory_space=pltpu.VMEM)
    out = pl.pallas_call(
        body, name="adamw_small",
        out_shape=tuple(jax.ShapeDtypeStruct(w.shape, F32) for w in weights for _ in range(4))
        + (jax.ShapeDtypeStruct((1, 1), F32),),
        in_specs=[pl.BlockSpec(memory_space=pltpu.SMEM)] + [vmem] * (1 + 3 * n), out_specs=(vmem,) * (4 * n + 1),
    )(chip, small_sum, *flat)
    return [tuple(out[4 * k:4 * k + 4]) for k in range(n)], out[-1][0, 0]


def kernel(x, norm_in, w_in, conv_w, attn_sinks, norm_conv_out, norm_attn_out, w_out, norm_final, loss_target, m_norm_in, m_w_in, m_conv_w, m_attn_sinks, m_norm_conv_out, m_norm_attn_out, m_w_out, m_norm_final, v_norm_in, v_w_in, v_conv_w, v_attn_sinks, v_norm_conv_out, v_norm_attn_out, v_w_out, v_norm_final):
    chip = 2 * lax.axis_index("x") + lax.axis_index("y")
    xs, target = x[0], loss_target[0]
    norm_final2 = norm_final.reshape(1, D_MODEL)
    w_in_t, m_w_in_t, v_w_in_t = w_in[0].T, m_w_in[0].T, v_w_in[0].T

    def from_hbm(*arrays):
        return tuple(pltpu.with_memory_space_constraint(a, pltpu.HBM) for a in arrays)

    h, pc, q, kv, ga, oc, ya, oa, probs, sink_probs, wt, cw, wo = _fwd_in(
        xs, norm_in, w_in_t, w_out[0], conv_w.transpose(1, 0, 2), norm_conv_out, attn_sinks, norm_attn_out)
    dx2, doa, gwo, dpc, small = _out_proj_loss(xs, oc, oa, wo, norm_final2, target, pc, cw, norm_conv_out)
    q, kv, ga, ya, doa, probs, gwo, small = from_hbm(q, kv, ga, ya, doa, probs, gwo, small)
    dqg, small, gwo_sh = _attn_bwd(q, kv, ga, ya, doa, probs, sink_probs, norm_attn_out, gwo, small)
    grad_x, small_sum, gwt_sh = _bwd_in(dpc, dqg, h, wt, xs, norm_in, dx2, small)

    up_w_in = tuple(o.T[None] for o in _adamw("w_in", *from_hbm(w_in_t, gwt_sh, m_w_in_t, v_w_in_t), 200))
    up_w_out = tuple(o[None] for o in _adamw("w_out", *from_hbm(w_out[0], gwo_sh, m_w_out[0], v_w_out[0]), 128))

    def row_of(r):
        return lambda small_ref, chip_ref: [(..., small_ref[r:r + 1, :])]

    def sink_lanes(small_ref, chip_ref):
        return [(..., small_ref[SMALL_MISC:SMALL_MISC + 1, 0:N_HEADS])]

    def conv_taps(small_ref, chip_ref):
        width = D_CONV // N_CHIPS
        cols = pl.ds(pl.multiple_of(chip_ref[0] * width, width), width)
        return [(k, small_ref[pl.ds(SMALL_CONV_W + k, 1), cols]) for k in range(conv_w.shape[1])]

    def small_view(a):
        return a.transpose(1, 0, 2) if a.ndim == 3 else a.reshape(-1, a.shape[-1])

    small_w = (norm_in, conv_w, attn_sinks, norm_conv_out, norm_attn_out, norm_final)
    small_m = (m_norm_in, m_conv_w, m_attn_sinks, m_norm_conv_out, m_norm_attn_out, m_norm_final)
    small_v = (v_norm_in, v_conv_w, v_attn_sinks, v_norm_conv_out, v_norm_attn_out, v_norm_final)
    small_g = (row_of(SMALL_NORM_IN), conv_taps, sink_lanes, row_of(SMALL_NORM_CONV), row_of(SMALL_NORM_ATTN),
               row_of(SMALL_NORM_FINAL))
    w_views, m_views, v_views = (tuple(small_view(a) for a in group) for group in (small_w, small_m, small_v))
    up_small, loss = _adamw_small(chip.astype(jnp.int32).reshape(1), small_sum, w_views, small_g, m_views, v_views)
    up_small = [tuple(o.transpose(1, 0, 2) if w.ndim == 3 else o.reshape(w.shape) for o in up)
                for up, w in zip(up_small, small_w)]
    up_norm_in, up_conv_w, up_sinks, up_norm_conv, up_norm_attn, up_norm_final = up_small
    updates = (up_norm_in, up_w_in, up_conv_w, up_sinks, up_norm_conv, up_norm_attn, up_w_out, up_norm_final)
    grads_out, deltas, new_m, new_v = zip(*updates)
    return (loss, grad_x[None], *grads_out, *deltas, *new_m, *new_v)
```

```python
import jax
import jax.numpy as jnp
from jax import lax
from jax.experimental import pallas as pl
from jax.experimental.pallas import tpu as pltpu

F32 = jnp.float32
BF16 = jnp.bfloat16
MESH = pl.DeviceIdType.MESH

D_MODEL = 1024
D_CONV = 1024
D_ATTN = 1024
D_KV = 128
D_QG = 2 * D_ATTN + 2 * D_KV
D_MIX = D_CONV + D_ATTN
D_PC = 4 * D_CONV
D_IN_PROJ = D_PC + 2 * D_ATTN + 2 * D_KV
ROW_Q = D_PC
ROW_KV = ROW_Q + D_ATTN
ROW_GA = ROW_KV + 2 * D_KV
N_HEADS = 16
HEAD_DIM = 64
HEADS_PER_KV = 8
BLK = 128
N_CHIPS = 4
RMS_EPS = 1e-5
SCALE = HEAD_DIM ** -0.5
SLOPES = tuple(2.0 ** (-8.0 * (h + 1) / N_HEADS) for h in range(N_HEADS))

ADAM_LR, ADAM_B1, ADAM_B2, ADAM_EPS, ADAM_WD, ADAM_STEP = 0.001, 0.9, 0.999, 1e-08, 0.01, 10

SMALL_ROWS = 8
SMALL_NORM_IN, SMALL_NORM_CONV, SMALL_NORM_ATTN, SMALL_NORM_FINAL, SMALL_CONV_W, SMALL_MISC = 0, 1, 2, 3, 4, 7
SMALL_LOSS_LANE = N_HEADS

TOK_TILE = 256
PC_PIECE = 512
MIB = 1 << 20


def _params(vmem_mib, semantics=None):
    return pltpu.CompilerParams(dimension_semantics=semantics, vmem_limit_bytes=vmem_mib * MIB)


def _nn(a, b):
    return jnp.dot(a, b, preferred_element_type=F32)


def _nt(a, b):
    return lax.dot_general(a, b, (((1,), (1,)), ((), ())), preferred_element_type=F32)


def _tn(a, b):
    return lax.dot_general(a, b, (((0,), (0,)), ((), ())), preferred_element_type=F32)


def _rstd(v):
    return lax.rsqrt(jnp.mean(v * v, axis=-1, keepdims=True) + RMS_EPS)


def _rms_bwd(g, xhat, rstd):
    return rstd * (g - xhat * jnp.mean(g * xhat, axis=-1, keepdims=True))


def _silu_and_grad(g):
    s = jax.nn.sigmoid(g)
    return g * s, s * (1.0 + g * (1.0 - s))


def _resident(shape):
    return pl.BlockSpec(shape, lambda *_: (0,) * len(shape), pipeline_mode=pl.Buffered(1))


def _xor(a, b):
    return a + b - 2 * a * b


def _cast_rows(src_ref, dst_ref, rows=32):
    def step(i, carry):
        sl = pl.ds(pl.multiple_of(i * rows, rows), rows)
        dst_ref[sl, :] = src_ref[sl, :].astype(dst_ref.dtype)
        return carry
    lax.fori_loop(0, src_ref.shape[0] // rows, step, 0)


def _ag_scratch(shard_shape):
    rows, width = shard_shape
    return [pltpu.VMEM((rows, width), F32), pltpu.VMEM((rows, width), BF16), pltpu.VMEM((3, rows // 2, width), BF16),
            pltpu.SemaphoreType.DMA((6,)), pltpu.SemaphoreType.DMA((6,)), pltpu.SemaphoreType.DMA((4,))]


def _ag_stages(sh_ref, out, f32_buf, own, land, send_sems, recv_sems, local_sems):
    rows = sh_ref.shape[0]
    half = rows // 2
    x, y, c = lax.axis_index("x"), lax.axis_index("y"), lax.axis_index("c")
    j = 2 * x + y
    p1 = (_xor(x, c), _xor(y, 1 - c), c)
    p2 = (_xor(x, 1 - c), _xor(y, c), c)
    sib = (x, y, 1 - c)
    j1 = 2 * p1[0] + p1[1]
    j2 = 2 * p2[0] + p2[1]
    j3 = 3 - j

    def rows_of(chip, hf):
        return out.at[pl.ds(pl.multiple_of(chip * rows + hf * half, 16), half), :]

    def rcopy(k, src, dst, to):
        return pltpu.make_async_remote_copy(src_ref=src, dst_ref=dst, send_sem=send_sems.at[k],
                                            recv_sem=recv_sems.at[k], device_id=to, device_id_type=MESH)

    my_half = own.at[pl.ds(pl.multiple_of(c * half, 16), half), :]
    hop1 = rcopy(0, my_half, land.at[0], p1)
    hop2_own = rcopy(1, my_half, land.at[1], p2)
    hop2_fwd = rcopy(2, land.at[0], land.at[2], p2)
    swaps = [rcopy(3 + s, land.at[s], rows_of(chip, c), sib) for s, chip in enumerate((j1, j2, j3))]
    keeps = [pltpu.make_async_copy(land.at[s], rows_of(chip, c), local_sems.at[1 + s]) for s, chip in enumerate((j1, j2, j3))]
    load = pltpu.make_async_copy(sh_ref, f32_buf, local_sems.at[0])
    own_out = pltpu.make_async_copy(own, out.at[pl.ds(pl.multiple_of(j * rows, 16), rows), :], local_sems.at[0])

    def stage_send():
        load.start()
        load.wait()
        _cast_rows(f32_buf, own)
        own_out.start()
        hop1.start()

    def stage_forward():
        hop1.wait_recv()
        hop2_own.start()
        hop2_fwd.start()
        swaps[0].start()
        keeps[0].start()

    def stage_publish():
        hop2_own.wait_recv()
        swaps[1].start()
        keeps[1].start()
        hop2_fwd.wait_recv()
        swaps[2].start()
        keeps[2].start()

    def stage_drain():
        for s, chip in enumerate((j2, j1, j3)):
            rcopy(3 + s, my_half, rows_of(chip, 1 - c), sib).wait_recv()
        for cp in [hop1, hop2_own, hop2_fwd] + swaps:
            cp.wait_send()
        for cp in [own_out] + keeps:
            cp.wait()

    return stage_send, stage_forward, stage_publish, stage_drain


AG_CAST_ROWS = 400


def _gather_resident(sh_ref, out, f32_buf, send_sems, recv_sems, local_sems):
    rows = sh_ref.shape[0]
    half = rows // 2
    x, y, c = lax.axis_index("x"), lax.axis_index("y"), lax.axis_index("c")
    j = 2 * x + y
    p1 = (_xor(x, c), _xor(y, 1 - c), c)
    p2 = (_xor(x, 1 - c), _xor(y, c), c)
    sib = (x, y, 1 - c)
    j1 = 2 * p1[0] + p1[1]
    j2 = 2 * p2[0] + p2[1]
    j3 = 3 - j

    def rows_of(chip, hf):
        return out.at[pl.ds(pl.multiple_of(chip * rows + hf * half, 16), half), :]

    def send(k, chip, to):
        return pltpu.make_async_remote_copy(src_ref=rows_of(chip, c), dst_ref=rows_of(chip, c), send_sem=send_sems.at[k],
                                            recv_sem=recv_sems.at[k], device_id=to, device_id_type=MESH)

    per_half = half // AG_CAST_ROWS

    def cast_own(chunk):
        lo = pl.multiple_of(chunk * AG_CAST_ROWS, 16)
        load = pltpu.make_async_copy(sh_ref.at[pl.ds(lo, AG_CAST_ROWS), :], f32_buf, local_sems.at[0])
        load.start()
        load.wait()
        _cast_rows(f32_buf, out.at[pl.ds(pl.multiple_of(j * rows + lo, 16), AG_CAST_ROWS), :], rows=16)

    for k in range(per_half):
        cast_own(c * per_half + k)
    hop1 = send(0, j, p1)
    hop1.start()
    for k in range(per_half):
        cast_own((1 - c) * per_half + k)
    hop1.wait_recv()
    sends = [hop1, send(1, j, p2), send(2, j1, p2), send(3, j1, sib)]
    for cp in sends[1:]:
        cp.start()
    sends[1].wait_recv()
    sends.append(send(4, j2, sib))
    sends[-1].start()
    sends[2].wait_recv()
    sends.append(send(5, j3, sib))
    sends[-1].start()
    for k in (3, 4, 5):
        send(k, j, sib).wait_recv()
    for cp in sends:
        cp.wait_send()


def _fwd_in(x, norm_in, wt_sh, wo_sh, cw_sh, norm_conv_out, sinks, norm_attn_out):
    seq = x.shape[0]
    tile = TOK_TILE
    n_tiles = seq // tile
    stage_steps = (0, n_tiles // 4, (5 * n_tiles) // 8, n_tiles - 1)
    blocks = tile // BLK
    cw_cols = cw_sh.shape[-1]

    def body(x_ref, g_ref, wtsh_ref, wo_ref, cwsh_ref, gn_ref, sink_ref, gna_ref,
             h_ref, pc_ref, q_ref, kv_ref, ga_ref, oc_ref, ya_ref, oa_ref, pr_ref, sp_ref, wt_out, cw_ref, wo_out,
             zbuf, kv_last, wt_ref, f32_buf, cw_land, wt_send, wt_recv, wt_local, cw_send, cw_recv, cw_local, *ag_scratch):
        step = pl.program_id(0)
        to_hbm = pltpu.make_async_copy(wt_ref, wt_out, wt_local.at[0])

        @pl.when(step == 0)
        def _():
            zbuf[0:8, :] = jnp.zeros((8, D_CONV), F32)
            kv_last[...] = jnp.zeros_like(kv_last)
            x_i, y_i, c = lax.axis_index("x"), lax.axis_index("y"), lax.axis_index("c")
            j = 2 * x_i + y_i
            p1 = (_xor(x_i, c), _xor(y_i, 1 - c), c)
            p2 = (_xor(x_i, 1 - c), _xor(y_i, c), c)
            j1 = 2 * p1[0] + p1[1]

            def cw_copy(k, src, chip, to):
                return pltpu.make_async_remote_copy(src_ref=src, dst_ref=cw_land.at[chip], send_sem=cw_send.at[k],
                                                    recv_sem=cw_recv.at[k], device_id=to, device_id_type=MESH)

            mine = pltpu.make_async_copy(cwsh_ref, cw_land.at[j], cw_local.at[0])
            mine.start()
            first = cw_copy(0, cwsh_ref, j, p1)
            first.start()
            _gather_resident(wtsh_ref, wt_ref, f32_buf, wt_send, wt_recv, wt_local)
            to_hbm.start()
            first.wait_recv()
            second = [cw_copy(1, cwsh_ref, j, p2), cw_copy(2, cw_land.at[j1], j1, p2)]
            for cp in second:
                cp.start()
            for cp in second:
                cp.wait_recv()
            for cp in [first] + second:
                cp.wait_send()
            mine.wait()
            for chip in range(N_CHIPS):
                for tap in range(cw_sh.shape[0]):
                    cw_ref[tap:tap + 1, chip * cw_cols:(chip + 1) * cw_cols] = cw_land[chip, tap]

        stages = _ag_stages(wo_ref, wo_out, *ag_scratch)
        for at, stage in zip(stage_steps[:-1], stages[:-1]):
            pl.when(step == at)(stage)

        def attention(b):
            rows = pl.ds(b * BLK, BLK)
            kv_prev = kv_last if b == 0 else kv_ref.at[pl.ds((b - 1) * BLK, BLK), :]
            return _attn_forward(q_ref.at[rows, :], kv_ref.at[rows, :], kv_prev, ga_ref.at[rows, :], sink_ref, gna_ref,
                                 _band_geometry(step * blocks + b), ya_ref.at[rows, :], oa_ref.at[rows, :],
                                 pr_ref.at[rows, :], sp_ref.at[rows, :])

        xv = x_ref[...]
        h = (xv * _rstd(xv) * g_ref[...]).astype(BF16)
        h_ref[...] = h
        q_ref[...] = _nt(h, wt_ref[ROW_Q:ROW_KV, :])
        kv_ref[...] = _nt(h, wt_ref[ROW_KV:ROW_GA, :])
        ga_ref[...] = _nt(h, wt_ref[ROW_GA:D_IN_PROJ, :])
        attention_blocks = [attention(b) for b in range(blocks)]
        for lo in range(0, D_PC, PC_PIECE):
            pc_ref[:, lo:lo + PC_PIECE] = _nt(h, wt_ref[lo:lo + PC_PIECE, :])
            for stages_of_block in attention_blocks:
                next(stages_of_block, None)
        for stages_of_block in attention_blocks:
            for _ in stages_of_block:
                pass
        kv_last[...] = kv_ref[tile - BLK:tile, :]

        cb, _, _, _, _, _, conv = _conv_core(pc_ref, zbuf, cw_ref)
        yc = cb * conv
        silu, _ = _silu_and_grad(pc_ref[:, 3 * D_CONV:4 * D_CONV])
        oc_ref[...] = (yc * _rstd(yc) * gn_ref[...] * silu).astype(BF16)
        zbuf[0:8, :] = zbuf[tile:tile + 8, :]

        @pl.when(step == stage_steps[-1])
        def _():
            stages[-1]()
            to_hbm.wait()

    def row(width):
        return pl.BlockSpec((tile, width), lambda i: (i, 0))

    any_spec = pl.BlockSpec(memory_space=pl.ANY)
    dma = pltpu.SemaphoreType.DMA
    wt_shape = (N_CHIPS * wt_sh.shape[0], wt_sh.shape[1])
    cw_shape = (cw_sh.shape[0], N_CHIPS * cw_cols)
    return pl.pallas_call(
        body, name="fwd_in", grid=(n_tiles,),
        out_shape=(jax.ShapeDtypeStruct((seq, D_MODEL), BF16), jax.ShapeDtypeStruct((seq, D_PC), F32),
                   jax.ShapeDtypeStruct((seq, D_ATTN), F32), jax.ShapeDtypeStruct((seq, 2 * D_KV), F32),
                   jax.ShapeDtypeStruct((seq, D_ATTN), F32), jax.ShapeDtypeStruct((seq, D_CONV), BF16),
                   pltpu.HBM((seq, D_ATTN), F32), pltpu.HBM((seq, D_ATTN), BF16),
                   pltpu.HBM((seq, N_HEADS * BLK), BF16), pltpu.HBM((seq, 128), F32),
                   pltpu.HBM(wt_shape, BF16), jax.ShapeDtypeStruct(cw_shape, F32),
                   jax.ShapeDtypeStruct((N_CHIPS * wo_sh.shape[0], wo_sh.shape[1]), BF16)),
        in_specs=[row(D_MODEL), _resident((1, D_MODEL)), any_spec, any_spec, any_spec, _resident((1, D_CONV)),
                  pl.BlockSpec(memory_space=pltpu.SMEM), _resident((1, D_ATTN))],
        out_specs=(row(D_MODEL), row(D_PC), row(D_ATTN), row(2 * D_KV), row(D_ATTN), row(D_CONV), row(D_ATTN),
                   row(D_ATTN), row(N_HEADS * BLK), row(128), any_spec, pl.BlockSpec(cw_shape, lambda i: (0, 0)),
                   any_spec),
        scratch_shapes=[pltpu.VMEM((tile + 8, D_CONV), F32), pltpu.VMEM((BLK, 2 * D_KV), F32),
                        pltpu.VMEM(wt_shape, BF16), pltpu.VMEM((AG_CAST_ROWS, wt_sh.shape[1]), F32),
                        pltpu.VMEM((N_CHIPS,) + cw_sh.shape, F32),
                        dma((6,)), dma((6,)), dma((1,)), dma((3,)), dma((3,)), dma((1,))] + _ag_scratch(wo_sh.shape),
        compiler_params=_params(62, ("arbitrary",)),
    )(x, norm_in, wt_sh, wo_sh, cw_sh, norm_conv_out, sinks, norm_attn_out)


def _conv_core(pc_ref, zbuf, cw_ref):
    tile = pc_ref.shape[0]
    cb = pc_ref[:, 0:D_CONV]
    cc = pc_ref[:, D_CONV:2 * D_CONV]
    cu = pc_ref[:, 2 * D_CONV:3 * D_CONV]
    z = cc * cu
    zbuf[8:tile + 8, :] = z
    z1 = zbuf[7:tile + 7, :]
    z2 = zbuf[6:tile + 6, :]
    conv = cw_ref[0:1, :] * z2 + cw_ref[1:2, :] * z1 + cw_ref[2:3, :] * z
    return cb, cc, cu, z, z1, z2, conv


def _band_geometry(block_index):
    qi = lax.broadcasted_iota(jnp.int32, (BLK, BLK), 0)
    kp = lax.broadcasted_iota(jnp.int32, (BLK, BLK), 1)
    use_cur = kp <= qi
    dist = jnp.where(use_cur, qi - kp, qi - kp + BLK).astype(F32)
    valid = use_cur | (block_index > 0)
    return use_cur, dist, valid


def _block_diag(cur, prev, group):
    lane = lax.broadcasted_iota(jnp.int32, cur.shape, 1)

    def halves(t):
        other = pltpu.roll(t, 64, 1)
        lo, hi = (t, other) if group == 0 else (other, t)
        return jnp.where(lane < 64, lo, 0.0), jnp.where(lane >= 64, hi, 0.0)

    return jnp.concatenate(halves(cur) + halves(prev), axis=0).astype(BF16)


def _merge(s4, use_cur):
    return (jnp.where(use_cur, s4[:, 0:BLK], s4[:, 2 * BLK:3 * BLK]),
            jnp.where(use_cur, s4[:, BLK:2 * BLK], s4[:, 3 * BLK:4 * BLK]))


def _split(a, b, use_cur):
    return jnp.concatenate([jnp.where(use_cur, a, 0.0), jnp.where(use_cur, b, 0.0),
                            jnp.where(use_cur, 0.0, a), jnp.where(use_cur, 0.0, b)], axis=1)


def _softmax_head(s, head, sink, dist, valid):
    sc = jnp.where(valid, s - SLOPES[head] * dist, -jnp.inf)
    m = jnp.maximum(jnp.max(sc, axis=-1, keepdims=True), sink)
    p = jnp.exp(sc - m)
    es = jnp.exp(sink - m)
    inv = 1.0 / (jnp.sum(p, axis=-1, keepdims=True) + es)
    return p * inv, es * inv


def _attn_operands(q_ref, kvc_ref, kvp_ref):
    groups = range(N_HEADS // HEADS_PER_KV)
    kbd = [_block_diag(kvc_ref[:, 0:D_KV], kvp_ref[:, 0:D_KV], g) for g in groups]
    vbd = [_block_diag(kvc_ref[:, D_KV:2 * D_KV], kvp_ref[:, D_KV:2 * D_KV], g) for g in groups]
    qps = [(q_ref[:, j * 128:(j + 1) * 128] * SCALE).astype(BF16) for j in range(N_HEADS // 2)]
    return qps, kbd, vbd


def _attn_forward(q_ref, kvc_ref, kvp_ref, ga_ref, sink_ref, gn_ref, geometry, ya_ref, oa_ref, pr_ref, sp_ref):
    use_cur, dist, valid = geometry
    _, kbd, vbd = operands = _attn_operands(q_ref, kvc_ref, kvp_ref)
    yield
    scores = []
    for j, qp in enumerate(operands[0]):
        scores += _merge(_nt(qp, kbd[j // 4]), use_cur)
    yield
    sinks = [sink_ref[0, h] for h in range(N_HEADS)]
    scores = [jnp.where(valid, s - SLOPES[h] * dist, -jnp.inf) for h, s in enumerate(scores)]
    maxes = [jnp.maximum(jnp.max(s, axis=-1, keepdims=True), sinks[h]) for h, s in enumerate(scores)]
    yield
    exps = [jnp.exp(s - m) for s, m in zip(scores, maxes)]
    sink_exps = [jnp.exp(sinks[h] - m) for h, m in enumerate(maxes)]
    yield
    invs = [1.0 / (jnp.sum(e, axis=-1, keepdims=True) + se) for e, se in zip(exps, sink_exps)]
    probs = [e * inv for e, inv in zip(exps, invs)]
    pr_ref[...] = jnp.concatenate(probs, axis=1).astype(BF16)
    lane = lax.broadcasted_iota(jnp.int32, (BLK, 128), 1)
    sp_ref[...] = sum(jnp.where(lane == h, se * inv, 0.0) for h, (se, inv) in enumerate(zip(sink_exps, invs)))
    yield
    p4s = [_split(probs[2 * j], probs[2 * j + 1], use_cur).astype(BF16) for j in range(N_HEADS // 2)]
    ya = jnp.concatenate([_nn(p4, vbd[j // 4]) for j, p4 in enumerate(p4s)], axis=1)
    ya_ref[...] = ya
    yield
    silu, _ = _silu_and_grad(ga_ref[...])
    oa_ref[...] = (ya * _rstd(ya) * gn_ref[...] * silu).astype(BF16)


def _kv_specs(n_blocks, reverse):
    def blk(i):
        return (n_blocks - 1 - i) if reverse else i
    cur = pl.BlockSpec((BLK, 2 * D_KV), lambda i: (blk(i), 0))
    prev = pl.BlockSpec((BLK, 2 * D_KV), lambda i: (jnp.maximum(blk(i) - 1, 0), 0))
    return cur, prev


def _out_proj_loss(x, oc, oa, wo, norm_final, target, pc, conv_w, norm_conv_out):
    seq = x.shape[0]
    tile = TOK_TILE
    n_tiles = seq // tile

    def body(x_ref, oc_ref, oa_ref, wo_ref, gf_ref, t_ref, pc_ref, hcc_ref, hcu_ref, cw_ref, gn_ref,
             dx2_ref, doa_ref, gwo_ref, dpc_ref, small_ref, loss_acc, zbuf, dbuf):
        step = pl.program_id(0)

        def small_add(row, value):
            small_ref[row:row + 1, :] += jnp.sum(value, axis=0, keepdims=True)

        @pl.when(step == 0)
        def _():
            gwo_ref[...] = jnp.zeros_like(gwo_ref)
            small_ref[...] = jnp.zeros_like(small_ref)
            loss_acc[...] = jnp.zeros_like(loss_acc)
            dbuf[tile:tile + 8, :] = jnp.zeros((8, D_CONV), F32)

        oc, oa = oc_ref[...], oa_ref[...]
        x2 = x_ref[...] + _nn(oc, wo_ref[0:D_CONV, :]) + _nn(oa, wo_ref[D_CONV:D_MIX, :])
        r = _rstd(x2)
        xhat = x2 * r
        err = xhat * gf_ref[...] - t_ref[...]
        loss_acc[...] += jnp.sum(err * err, axis=0, keepdims=True) * (0.5 / D_MODEL)
        dy = err * (1.0 / D_MODEL)
        small_add(SMALL_NORM_FINAL, dy * xhat)
        dx2 = _rms_bwd(dy * gf_ref[...], xhat, r)
        dx2_ref[...] = dx2
        db = dx2.astype(BF16)
        do = _nt(db, wo_ref[0:D_CONV, :])
        doa_ref[...] = _nt(db, wo_ref[D_CONV:D_MIX, :])
        gwo_ref[0:D_CONV, :] += _tn(oc, db)
        gwo_ref[D_CONV:D_MIX, :] += _tn(oa, db)

        is_first_tile = step == n_tiles - 1
        zbuf[0:8, :] = jnp.where(is_first_tile, 0.0, hcc_ref[...] * hcu_ref[...])
        cb, cc, cu, z, z1, z2, conv = _conv_core(pc_ref, zbuf, cw_ref)
        silu, dsilu = _silu_and_grad(pc_ref[:, 3 * D_CONV:4 * D_CONV])
        yc = cb * conv
        rc = _rstd(yc)
        chat = yc * rc
        dn = do * silu
        dpc_ref[:, 3 * D_CONV:4 * D_CONV] = (do * (chat * gn_ref[...]) * dsilu).astype(BF16)
        small_add(SMALL_NORM_CONV, dn * chat)
        dyc = _rms_bwd(dn * gn_ref[...], chat, rc)
        dpc_ref[:, 0:D_CONV] = (dyc * conv).astype(BF16)
        dconv = dyc * cb
        small_add(SMALL_CONV_W, dconv * z2)
        small_add(SMALL_CONV_W + 1, dconv * z1)
        small_add(SMALL_CONV_W + 2, dconv * z)
        dbuf[0:tile, :] = dconv
        dz = cw_ref[2:3, :] * dconv + cw_ref[1:2, :] * dbuf[1:tile + 1, :] + cw_ref[0:1, :] * dbuf[2:tile + 2, :]
        dpc_ref[:, D_CONV:2 * D_CONV] = (dz * cu).astype(BF16)
        dpc_ref[:, 2 * D_CONV:3 * D_CONV] = (dz * cc).astype(BF16)
        dbuf[tile:tile + 8, :] = dbuf[0:8, :]

        @pl.when(step == n_tiles - 1)
        def _():
            lane = lax.broadcasted_iota(jnp.int32, (1, D_MODEL), 1)
            small_ref[SMALL_MISC:SMALL_MISC + 1, :] = jnp.where(lane == SMALL_LOSS_LANE, jnp.sum(loss_acc[...]), 0.0)

    def rev(i):
        return n_tiles - 1 - i

    def row(width):
        return pl.BlockSpec((tile, width), lambda i: (rev(i), 0))

    def halo(col_block):
        return pl.BlockSpec((8, D_CONV), lambda i: (jnp.maximum(rev(i) * (tile // 8) - 1, 0), col_block))

    def const(shape):
        return pl.BlockSpec(shape, lambda i: (0, 0))

    return pl.pallas_call(
        body, name="out_proj_loss", grid=(n_tiles,),
        out_shape=(jax.ShapeDtypeStruct((seq, D_MODEL), F32), jax.ShapeDtypeStruct((seq, D_ATTN), F32),
                   jax.ShapeDtypeStruct((D_MIX, D_MODEL), F32), jax.ShapeDtypeStruct((seq, D_PC), BF16),
                   jax.ShapeDtypeStruct((SMALL_ROWS, D_MODEL), F32)),
        in_specs=[row(D_MODEL), row(D_CONV), row(D_ATTN), _resident(wo.shape), _resident((1, D_MODEL)), row(D_MODEL),
                  row(D_PC), halo(1), halo(2), _resident(conv_w.shape), _resident((1, D_CONV))],
        out_specs=(row(D_MODEL), row(D_ATTN), const((D_MIX, D_MODEL)), row(D_PC), const((SMALL_ROWS, D_MODEL))),
        scratch_shapes=[pltpu.VMEM((1, D_MODEL), F32), pltpu.VMEM((tile + 8, D_CONV), F32),
                        pltpu.VMEM((tile + 8, D_CONV), F32)],
        compiler_params=_params(62, ("arbitrary",)),
    )(x, oc, oa, wo, norm_final, target, pc, pc, pc, conv_w, norm_conv_out)


def _attn_bwd(q, kv, ga, ya, doa, probs, sink_probs, norm_attn_out, gwo, small):
    seq = q.shape[0]
    n_blocks = seq // BLK
    stage_steps = (0, n_blocks // 4, n_blocks // 2, (3 * n_blocks) // 4, n_blocks - 1)

    def body(q_ref, kvc_ref, kvp_ref, ga_ref, ya_ref, doa_ref, pr_ref, sp_ref, gn_ref, gwo_ref, small_ref,
             dqg_ref, small_out, gwo_sh, gna_ref, gs_ref, carry, dya_buf, *rs_scratch):
        step = pl.program_id(0)

        @pl.when(step == 0)
        def _():
            gna_ref[...] = jnp.zeros_like(gna_ref)
            gs_ref[...] = jnp.zeros_like(gs_ref)
            carry[...] = jnp.zeros_like(carry)

        ya = ya_ref[...]
        r = _rstd(ya)
        xhat = ya * r
        silu, dsilu = _silu_and_grad(ga_ref[...])
        do = doa_ref[...]
        dn = do * silu
        dqg_ref[:, D_ATTN:2 * D_ATTN] = (do * (xhat * gn_ref[...]) * dsilu).astype(BF16)
        gna_ref[...] += jnp.sum(dn * xhat, axis=0, keepdims=True)
        dya_buf[...] = _rms_bwd(dn * gn_ref[...], xhat, r).astype(BF16)

        use_cur = _band_geometry(n_blocks - 1 - step)[0]
        lane = lax.broadcasted_iota(jnp.int32, (BLK, 128), 1)

        def fold(bd):
            return (jnp.where(lane < 64, bd[0:BLK], 0.0) + jnp.where(lane >= 64, bd[BLK:2 * BLK], 0.0),
                    jnp.where(lane < 64, bd[2 * BLK:3 * BLK], 0.0) + jnp.where(lane >= 64, bd[3 * BLK:4 * BLK], 0.0))

        pairs = range(N_HEADS // 2)
        qps, kbd, vbd = _attn_operands(q_ref, kvc_ref, kvp_ref)
        probs = [pr_ref[:, h * BLK:(h + 1) * BLK].astype(F32) for h in range(N_HEADS)]
        dyps = [dya_buf[:, j * 128:(j + 1) * 128] for j in pairs]
        dps = []
        for j in pairs:
            dps += _merge(_nt(dyps[j], vbd[j // 4]), use_cur)
        deltas = [jnp.sum(p * dp, axis=-1, keepdims=True) for p, dp in zip(probs, dps)]
        dss = [p * (dp - delta) for p, dp, delta in zip(probs, dps, deltas)]
        delta_lanes = sum(jnp.where(lane == h, deltas[h], 0.0) for h in range(N_HEADS))
        gs_ref[...] -= jnp.sum(sp_ref[...] * delta_lanes, axis=0, keepdims=True)
        ds4s = [_split(dss[2 * j], dss[2 * j + 1], use_cur).astype(BF16) for j in pairs]
        p4s = [_split(probs[2 * j], probs[2 * j + 1], use_cur).astype(BF16) for j in pairs]
        dqg_ref[:, 0:D_ATTN] = jnp.concatenate([_nn(ds4s[j], kbd[j // 4]) * SCALE for j in pairs], axis=1).astype(BF16)
        sums = []
        for group in range(N_HEADS // HEADS_PER_KV):
            acc = [jnp.zeros((BLK, 128), F32) for _ in range(4)]
            for j in range(group * 4, group * 4 + 4):
                for slot, part in enumerate(fold(_tn(ds4s[j], qps[j])) + fold(_tn(p4s[j], dyps[j]))):
                    acc[slot] = acc[slot] + part
            sums.append([a + pltpu.roll(a, 64, 1) for a in acc])
        dk_cur, dk_prev, dv_cur, dv_prev = (jnp.where(lane < 64, a, b) for a, b in zip(sums[0], sums[1]))
        dqg_ref[:, 2 * D_ATTN:2 * D_ATTN + 2 * D_KV] = (jnp.concatenate([dk_cur, dv_cur], axis=1) + carry[...]).astype(BF16)
        carry[...] = jnp.concatenate([dk_prev, dv_prev], axis=1)

        @pl.when(step == n_blocks - 1)
        def _():
            small_out[...] = small_ref[...]
            small_out[SMALL_NORM_ATTN:SMALL_NORM_ATTN + 1, :] = gna_ref[...]
            small_out[SMALL_MISC:SMALL_MISC + 1, 0:128] = small_ref[SMALL_MISC:SMALL_MISC + 1, 0:128] + gs_ref[...]

        for at, stage in zip(stage_steps, _rs_wout_stages(gwo_ref, gwo_sh, *rs_scratch)):
            pl.when(step == at)(stage)

    row = pl.BlockSpec((BLK, D_ATTN), lambda i: (n_blocks - 1 - i, 0))
    kv_cur, kv_prev = _kv_specs(n_blocks, reverse=True)
    any_spec = pl.BlockSpec(memory_space=pl.ANY)
    return pl.pallas_call(
        body, name="attn_bwd", grid=(n_blocks,),
        out_shape=(jax.ShapeDtypeStruct((seq, D_QG), BF16), pltpu.HBM(small.shape, F32),
                   pltpu.HBM((gwo.shape[0] // N_CHIPS, gwo.shape[1]), F32)),
        in_specs=[row, kv_cur, kv_prev, row, row, row,
                  pl.BlockSpec((BLK, N_HEADS * BLK), lambda i: (n_blocks - 1 - i, 0)),
                  pl.BlockSpec((BLK, 128), lambda i: (n_blocks - 1 - i, 0)), _resident((1, D_ATTN)),
                  any_spec, _resident(small.shape)],
        out_specs=(pl.BlockSpec((BLK, D_QG), lambda i: (n_blocks - 1 - i, 0)),
                   pl.BlockSpec(small.shape, lambda i: (0, 0)), any_spec),
        scratch_shapes=[pltpu.VMEM((1, D_ATTN), F32), pltpu.VMEM((1, 128), F32),
                        pltpu.VMEM((BLK, 2 * D_KV), F32), pltpu.VMEM((BLK, D_ATTN), BF16)] + _rs_wout_scratch(gwo.shape),
        compiler_params=_params(44, ("arbitrary",)),
    )(q, kv, kv, ga, ya, doa, probs, sink_probs, norm_attn_out, gwo, small)


GBLK = 256
GSUB = 64
PAIR_RING = 4
LAG_PAIR, LAG_HOP1, LAG_HOP2 = 1, 7, 14


def _bwd_in(dpc, dqg, h, wt, x, norm_in, dx2, small):
    seq = x.shape[0]
    n_blk = D_IN_PROJ // GBLK
    per_chip = n_blk // N_CHIPS
    n_slots = (n_blk + 1) // 2
    n_sub = GBLK // GSUB
    chip_rows = D_IN_PROJ // N_CHIPS
    tile = TOK_TILE
    n_tiles = seq // tile
    n_steps = n_blk + max(n_tiles, LAG_HOP2 // 2)
    chunk = min(seq, 512)
    blk_q, blk_kv, blk_ga = ROW_Q // GBLK, ROW_KV // GBLK, ROW_GA // GBLK

    def block_of(i):
        k = i % N_CHIPS
        robin = per_chip * ((k % 2) * 2 + k // 2) + i // N_CHIPS
        if isinstance(i, int):
            return robin if i < per_chip * N_CHIPS else i
        return jnp.where(i < per_chip * N_CHIPS, robin, i)

    def owner_of(i):
        return (i // N_CHIPS) % 2

    def slot_of(i):
        return (i // (2 * N_CHIPS)) * N_CHIPS + i % N_CHIPS

    def body(dpc_ref, dqg_ref, wt_ref, h_ref, x_ref, g_ref, dx2_ref, small_ref, gx_ref, small_sum, gwt_sh,
             dh_acc, gni, keep, pbuf, xbuf, land, land2, small_land,
             pair_send, pair_recv, h1_send, h1_recv, h2_send, h2_recv, sw_send, sw_recv, sm_send, sm_recv, out_sem):
        step = pl.program_id(0)
        x_i, y_i, c = lax.axis_index("x"), lax.axis_index("y"), lax.axis_index("c")
        me = 4 * x_i + 2 * y_i + c
        j = 2 * x_i + y_i
        pa = (_xor(x_i, 1 - c), _xor(y_i, c), c)
        pb = (_xor(x_i, c), _xor(y_i, 1 - c), c)
        sib = (x_i, y_i, 1 - c)
        ja = 2 * pa[0] + pa[1]
        jb = 2 * pb[0] + pb[1]
        jd = 3 - j

        def remote(src, dst, send, recv, to):
            return pltpu.make_async_remote_copy(src_ref=src, dst_ref=dst, send_sem=send, recv_sem=recv,
                                                device_id=to, device_id_type=MESH)

        def piece(ref, slot, u, n):
            return ref.at[slot, pl.ds(u * GSUB, n * GSUB), :]

        def chip_rows_at(ref, local, n):
            return ref.at[pl.ds(pl.multiple_of(local, GSUB), n * GSUB), :]

        def pair_copy(i):
            slot = slot_of(i)
            return remote(pbuf.at[i % PAIR_RING], land.at[slot], pair_send.at[slot], pair_recv.at[slot], sib)

        def h1_copy(slot, u, n):
            k = slot * n_sub + u
            return remote(piece(xbuf, slot, u, n), piece(xbuf, slot, u, n), h1_send.at[k], h1_recv.at[k], pa)

        def h2_copy(slot, u, n, local):
            k = slot * n_sub + u
            return remote(piece(xbuf, slot, u, n), chip_rows_at(land2, local, n), h2_send.at[k], h2_recv.at[k], pb)

        def sw_copy(slot, u, n, local):
            k = slot * n_sub + u
            return remote(piece(keep, slot, u, n), chip_rows_at(gwt_sh, local, n), sw_send.at[k], sw_recv.at[k], sib)

        def owned(i):
            return (i >= 0) & (i < n_blk) & (owner_of(i) == c)

        def chip_of(blk, u):
            row = blk * GBLK + u * GSUB
            chip = row // chip_rows
            return chip, row - chip * chip_rows

        def pieces(blk):
            first, local = chip_of(blk, 0)
            whole = first == chip_of(blk, n_sub - 1)[0]
            if isinstance(blk, int):
                return [(True, 0, n_sub, first, local)] if whole else [(True, u, 1) + chip_of(blk, u) for u in range(n_sub)]
            return [(whole, 0, n_sub, first, local)] + [(jnp.logical_not(whole), u, 1) + chip_of(blk, u) for u in range(n_sub)]

        @pl.when(step == 0)
        def _():
            dh_acc[...] = jnp.zeros_like(dh_acc)
            gni[...] = jnp.zeros_like(gni)

        @pl.when(step < n_blk)
        def _():
            from_pc = block_of(step) < blk_q
            block = _tn(jnp.where(from_pc, dpc_ref[...], dqg_ref[...]), h_ref[...])
            for t in range(0, seq, chunk):
                d = jnp.where(from_pc, dpc_ref[t:t + chunk, :], dqg_ref[t:t + chunk, :])
                dh_acc[t:t + chunk, :] += _nn(d, wt_ref[...])

            @pl.when(owner_of(step) == c)
            def _():
                keep[slot_of(step)] = block

            @pl.when(owner_of(step) != c)
            def _():
                @pl.when(step >= 2 * PAIR_RING)
                def _():
                    pair_copy(step - 2 * PAIR_RING).wait_send()
                pbuf[step % PAIR_RING] = block.astype(BF16)
                pair_copy(step).start()

        i1 = step - LAG_PAIR

        @pl.when(owned(i1))
        def _():
            slot = slot_of(i1)
            pair_copy(i1).wait_recv()
            _accumulate(keep.at[slot], land.at[slot])
            for cond, u, n, chip, _ in pieces(block_of(i1)):
                @pl.when(cond & ((chip == ja) | (chip == jd)))
                def _(u=u, n=n):
                    _cast_rows(piece(keep, slot, u, n), piece(xbuf, slot, u, n))
                    h1_copy(slot, u, n).start()

        def serve_hop1(i2):
            @pl.when(owned(i2))
            def _():
                slot = slot_of(i2)
                for cond, u, n, chip, local in pieces(block_of(i2)):
                    @pl.when(cond & ((chip == j) | (chip == jb)))
                    def _(u=u, n=n, chip=chip, local=local):
                        h1_copy(slot, u, n).wait_recv()
                        _accumulate(piece(keep, slot, u, n), piece(xbuf, slot, u, n))

                        @pl.when(chip == jb)
                        def _():
                            _cast_rows(piece(keep, slot, u, n), piece(xbuf, slot, u, n))
                            h2_copy(slot, u, n, local).start()

                    @pl.when(cond & ((chip == ja) | (chip == jd)))
                    def _(u=u, n=n):
                        h1_copy(slot, u, n).wait_send()

        def serve_hop2(i3):
            @pl.when(owned(i3))
            def _():
                slot = slot_of(i3)
                for cond, u, n, chip, local in pieces(block_of(i3)):
                    @pl.when(cond & (chip == j))
                    def _(u=u, n=n, local=local):
                        h2_copy(slot, u, n, local).wait_recv()
                        _accumulate(piece(keep, slot, u, n), chip_rows_at(land2, local, n))
                        mine = pltpu.make_async_copy(piece(keep, slot, u, n), chip_rows_at(gwt_sh, local, n), out_sem.at[0])
                        mine.start()
                        sw_copy(slot, u, n, local).start()
                        mine.wait()

                    @pl.when(cond & (chip == jb))
                    def _(u=u, n=n, local=local):
                        h2_copy(slot, u, n, local).wait_send()

        late = step - n_blk
        for lag, serve in ((LAG_HOP1, serve_hop1), (LAG_HOP2, serve_hop2)):
            serve(jnp.where(late >= 0, n_blk - lag + 2 * late, step - lag))
            serve(jnp.where(late >= 0, n_blk - lag + 2 * late + 1, -1))

        e = step - n_blk

        @pl.when((e >= 0) & (e < n_tiles))
        def _():
            dh = dh_acc[pl.ds(pl.multiple_of(e * tile, tile), tile), :]
            xv = x_ref[...]
            r = _rstd(xv)
            xhat = xv * r
            gni[...] += jnp.sum(dh * xhat, axis=0, keepdims=True)
            gx_ref[...] = _rms_bwd(dh * g_ref[...], xhat, r) + dx2_ref[...]

        @pl.when(step == n_steps - 1)
        def _():
            small_land[me] = small_ref[...]
            small_land[me, SMALL_NORM_IN:SMALL_NORM_IN + 1, :] = gni[...]
            others = [(dx, dy, dc) for dx in (0, 1) for dy in (0, 1) for dc in (0, 1)][1:]
            sends = [remote(small_land.at[me], small_land.at[me], sm_send.at[k], sm_recv.at[k],
                            (_xor(x_i, dx), _xor(y_i, dy), _xor(c, dc))) for k, (dx, dy, dc) in enumerate(others)]
            for cp in sends:
                cp.start()
            for i in range(n_blk):
                if i + 2 * PAIR_RING >= n_blk:
                    @pl.when(owner_of(i) != c)
                    def _(i=i):
                        pair_copy(i).wait_send()
                for _, u, n, chip, local in pieces(block_of(i)):
                    @pl.when((j == chip) & (c == owner_of(i)))
                    def _(i=i, u=u, n=n, local=local):
                        sw_copy(slot_of(i), u, n, local).wait_send()

                    @pl.when((j == chip) & (c != owner_of(i)))
                    def _(i=i, u=u, n=n, local=local):
                        sw_copy(slot_of(i), u, n, local).wait_recv()
            for cp in sends:
                cp.wait_recv()
            total = small_land[0]
            for dev in range(1, 8):
                total = total + small_land[dev]
            small_sum[...] = total
            for cp in sends:
                cp.wait_send()

    def blk_at(i):
        return block_of(jnp.clip(i, 0, n_blk - 1))

    last_pc_step = max(i for i in range(n_blk) if block_of(i) < blk_q)

    def next_block(i, in_pc):
        i = jnp.clip(i, 0, n_blk - 1)
        step = jnp.full_like(i, last_pc_step if in_pc else n_blk - 1)
        for ahead in reversed(range(N_CHIPS)):
            cand = jnp.minimum(i + ahead, n_blk - 1)
            step = jnp.where((block_of(cand) < blk_q) == in_pc, cand, step)
        return block_of(step)

    def dqg_block(i):
        b = next_block(i, False)
        q_blk = jnp.clip(b - blk_q, 0, blk_kv - blk_q - 1)
        ga_blk = (D_ATTN // GBLK) + jnp.clip(b - blk_ga, 0, n_blk - blk_ga - 1)
        return jnp.where(b < blk_kv, q_blk, jnp.where(b == blk_kv, 2 * D_ATTN // GBLK, ga_blk))

    def tok(i):
        return (jnp.clip(i - n_blk, 0, n_tiles - 1), 0)

    n_piece = n_slots * n_sub
    dma = pltpu.SemaphoreType.DMA
    return pl.pallas_call(
        body, name="bwd_in", grid=(n_steps,),
        out_shape=(jax.ShapeDtypeStruct((seq, D_MODEL), F32), jax.ShapeDtypeStruct(small.shape, F32),
                   jax.ShapeDtypeStruct((chip_rows, D_MODEL), F32)),
        in_specs=[pl.BlockSpec((seq, GBLK), lambda i: (0, next_block(i, True))),
                  pl.BlockSpec((seq, GBLK), lambda i: (0, dqg_block(i))),
                  pl.BlockSpec((GBLK, D_MODEL), lambda i: (blk_at(i), 0)),
                  _resident(h.shape),
                  pl.BlockSpec((tile, D_MODEL), tok), _resident((1, D_MODEL)), pl.BlockSpec((tile, D_MODEL), tok),
                  _resident(small.shape)],
        out_specs=(pl.BlockSpec((tile, D_MODEL), tok), pl.BlockSpec(small.shape, lambda i: (0, 0)),
                   pl.BlockSpec(memory_space=pl.ANY)),
        scratch_shapes=[pltpu.VMEM((seq, D_MODEL), F32), pltpu.VMEM((1, D_MODEL), F32),
                        pltpu.VMEM((n_slots, GBLK, D_MODEL), F32), pltpu.VMEM((PAIR_RING, GBLK, D_MODEL), BF16),
                        pltpu.VMEM((n_slots, GBLK, D_MODEL), BF16), pltpu.VMEM((n_slots, GBLK, D_MODEL), BF16),
                        pltpu.VMEM((chip_rows, D_MODEL), BF16), pltpu.VMEM((8,) + small.shape, F32),
                        dma((n_slots,)), dma((n_slots,)), dma((n_piece,)), dma((n_piece,)), dma((n_piece,)),
                        dma((n_piece,)), dma((n_piece,)), dma((n_piece,)), dma((7,)), dma((7,)), dma((1,))],
        compiler_params=_params(62, ("arbitrary",)),
    )(dpc, dqg, wt, h, x, norm_in, dx2, small)


def _accumulate(dst_ref, src_ref, rows=16):
    def step(i, carry):
        sl = pl.ds(pl.multiple_of(i * rows, rows), rows)
        dst_ref[sl, :] = dst_ref[sl, :] + src_ref[sl, :].astype(F32)
        return carry
    lax.fori_loop(0, dst_ref.shape[0] // rows, step, 0)


def _rs_wout_scratch(gwo_shape):
    o_half, width = gwo_shape[0] // N_CHIPS // 2, gwo_shape[1]
    return [pltpu.VMEM((4, o_half, width), F32), pltpu.VMEM((4, o_half, width), BF16),
            pltpu.VMEM((4, o_half, width), BF16), pltpu.VMEM((2, o_half, width), BF16),
            pltpu.VMEM((o_half, width), BF16),
            pltpu.SemaphoreType.DMA((8,)), pltpu.SemaphoreType.DMA((8,)), pltpu.SemaphoreType.DMA((4,))]


def _rs_wout_stages(gwo_ref, gwo_sh, acc_o, sb_o, r1o, r2o, r3o, send_sems, recv_sems, local_sems):
    o_rows = gwo_ref.shape[0] // N_CHIPS
    o_half = o_rows // 2
    x, y, c = lax.axis_index("x"), lax.axis_index("y"), lax.axis_index("c")
    j = 2 * x + y
    pa = (_xor(x, 1 - c), _xor(y, c), c)
    pb = (_xor(x, c), _xor(y, 1 - c), c)
    sib = (x, y, 1 - c)
    ja = 2 * pa[0] + pa[1]
    jb = 2 * pb[0] + pb[1]
    jd = 3 - j
    order = (ja, jd, jb, j)
    sib_order = (jb, jd, ja, j)

    def rcopy(k, src, dst, to):
        return pltpu.make_async_remote_copy(src_ref=src, dst_ref=dst, send_sem=send_sems.at[k],
                                            recv_sem=recv_sems.at[k], device_id=to, device_id_type=MESH)

    def o_rows_of(chip, half):
        return gwo_ref.at[pl.ds(pl.multiple_of(chip * o_rows + half * o_half, 8), o_half), :]

    def load_all(chips, half):
        cps = [pltpu.make_async_copy(o_rows_of(chip, half), acc_o.at[s], local_sems.at[s]) for s, chip in enumerate(chips)]
        for cp in cps:
            cp.start()
        return cps

    def pair_send(s):
        return rcopy(s, sb_o.at[s], r1o.at[s], sib)

    def out_half(half):
        return gwo_sh.at[pl.ds(pl.multiple_of(half * o_half, 8), o_half), :]

    hop1 = [rcopy(4 + s, sb_o.at[s], r2o.at[s], pa) for s in range(2)]
    hop2 = rcopy(6, sb_o.at[2], r3o, pb)
    swap = rcopy(7, acc_o.at[3], out_half(c), sib)
    mine = pltpu.make_async_copy(acc_o.at[3], out_half(c), local_sems.at[0])

    def resend(s, copy):
        pair_send(s).wait_send()
        _cast_rows(acc_o.at[s], sb_o.at[s])
        copy.start()

    def stage_pair():
        for s, cp in enumerate(load_all(sib_order, 1 - c)):
            cp.wait()
            _cast_rows(acc_o.at[s], sb_o.at[s])
            pair_send(s).start()

    def stage_hop1():
        for s, cp in enumerate(load_all(order, c)):
            cp.wait()
            pair_send(s).wait_recv()
            _accumulate(acc_o.at[s], r1o.at[s])
            if s < 2:
                resend(s, hop1[s])

    def stage_hop2():
        hop1[1].wait_recv()
        _accumulate(acc_o.at[2], r2o.at[1])
        resend(2, hop2)
        hop1[0].wait_recv()
        _accumulate(acc_o.at[3], r2o.at[0])

    def stage_final():
        hop2.wait_recv()
        _accumulate(acc_o.at[3], r3o)
        mine.start()
        swap.start()

    def stage_drain():
        rcopy(7, acc_o.at[3], out_half(1 - c), sib).wait_recv()
        for cp in [pair_send(3)] + hop1 + [hop2, swap]:
            cp.wait_send()
        mine.wait()

    return stage_pair, stage_hop1, stage_hop2, stage_final, stage_drain


def _adamw(name, w, g, m, v, rows):
    def body(w_ref, g_ref, m_ref, v_ref, go_ref, d_ref, nm_ref, nv_ref):
        _adamw_update(w_ref, g_ref[...], m_ref, v_ref, go_ref, d_ref, nm_ref, nv_ref)

    spec = pl.BlockSpec((rows, w.shape[1]), lambda i: (i, 0))
    shape = jax.ShapeDtypeStruct(w.shape, F32)
    return pl.pallas_call(
        body, name="adamw_" + name, grid=(w.shape[0] // rows,),
        out_shape=(shape,) * 4, in_specs=[spec] * 4, out_specs=(spec,) * 4,
        compiler_params=_params(32, ("arbitrary",)),
    )(w, g, m, v)


def _adamw_update(w_ref, gv, m_ref, v_ref, go_ref, d_ref, nm_ref, nv_ref, at=...):
    go_ref[at] = gv
    nm = ADAM_B1 * m_ref[at] + (1.0 - ADAM_B1) * gv
    nv = ADAM_B2 * v_ref[at] + (1.0 - ADAM_B2) * (gv * gv)
    m_hat = nm / (1.0 - ADAM_B1 ** ADAM_STEP)
    v_hat = nv / (1.0 - ADAM_B2 ** ADAM_STEP)
    d_ref[at] = -ADAM_LR * (m_hat / (jnp.sqrt(v_hat) + ADAM_EPS) + ADAM_WD * w_ref[at])
    nm_ref[at] = nm
    nv_ref[at] = nv


def _adamw_small(chip, small_sum, weights, grads_of, ms, vs):
    n = len(weights)

    def body(chip_ref, small_ref, *refs):
        ins, outs, loss_ref = refs[:3 * n], refs[3 * n:-1], refs[-1]
        for k in range(n):
            w_ref, m_ref, v_ref = ins[3 * k:3 * k + 3]
            for at, gv in grads_of[k](small_ref, chip_ref):
                _adamw_update(w_ref, gv, m_ref, v_ref, *outs[4 * k:4 * k + 4], at=at)
        loss_ref[...] = small_ref[SMALL_MISC:SMALL_MISC + 1, SMALL_LOSS_LANE:SMALL_LOSS_LANE + 1]

    flat = [a for group in zip(weights, ms, vs) for a in group]
    vmem = pl.BlockSpec(memory_space=pltpu.VMEM)
    out = pl.pallas_call(
        body, name="adamw_small",
        out_shape=tuple(jax.ShapeDtypeStruct(w.shape, F32) for w in weights for _ in range(4))
        + (jax.ShapeDtypeStruct((1, 1), F32),),
        in_specs=[pl.BlockSpec(memory_space=pltpu.SMEM)] + [vmem] * (1 + 3 * n), out_specs=(vmem,) * (4 * n + 1),
    )(chip, small_sum, *flat)
    return [tuple(out[4 * k:4 * k + 4]) for k in range(n)], out[-1][0, 0]


def kernel(x, norm_in, w_in, conv_w, attn_sinks, norm_conv_out, norm_attn_out, w_out, norm_final, loss_target, m_norm_in, m_w_in, m_conv_w, m_attn_sinks, m_norm_conv_out, m_norm_attn_out, m_w_out, m_norm_final, v_norm_in, v_w_in, v_conv_w, v_attn_sinks, v_norm_conv_out, v_norm_attn_out, v_w_out, v_norm_final):
    chip = 2 * lax.axis_index("x") + lax.axis_index("y")
    xs, target = x[0], loss_target[0]
    norm_final2 = norm_final.reshape(1, D_MODEL)
    w_in_t, m_w_in_t, v_w_in_t = w_in[0].T, m_w_in[0].T, v_w_in[0].T

    def from_hbm(*arrays):
        return tuple(pltpu.with_memory_space_constraint(a, pltpu.HBM) for a in arrays)

    h, pc, q, kv, ga, oc, ya, oa, probs, sink_probs, wt, cw, wo = _fwd_in(
        xs, norm_in, w_in_t, w_out[0], conv_w.transpose(1, 0, 2), norm_conv_out, attn_sinks, norm_attn_out)
    dx2, doa, gwo, dpc, small = _out_proj_loss(xs, oc, oa, wo, norm_final2, target, pc, cw, norm_conv_out)
    q, kv, ga, ya, doa, probs, gwo, small = from_hbm(q, kv, ga, ya, doa, probs, gwo, small)
    dqg, small, gwo_sh = _attn_bwd(q, kv, ga, ya, doa, probs, sink_probs, norm_attn_out, gwo, small)
    grad_x, small_sum, gwt_sh = _bwd_in(dpc, dqg, h, wt, xs, norm_in, dx2, small)

    up_w_in = tuple(o.T[None] for o in _adamw("w_in", *from_hbm(w_in_t, gwt_sh, m_w_in_t, v_w_in_t), 200))
    up_w_out = tuple(o[None] for o in _adamw("w_out", *from_hbm(w_out[0], gwo_sh, m_w_out[0], v_w_out[0]), 128))

    def row_of(r):
        return lambda small_ref, chip_ref: [(..., small_ref[r:r + 1, :])]

    def sink_lanes(small_ref, chip_ref):
        return [(..., small_ref[SMALL_MISC:SMALL_MISC + 1, 0:N_HEADS])]

    def conv_taps(small_ref, chip_ref):
        width = D_CONV // N_CHIPS
        cols = pl.ds(pl.multiple_of(chip_ref[0] * width, width), width)
        return [(k, small_ref[pl.ds(SMALL_CONV_W + k, 1), cols]) for k in range(conv_w.shape[1])]

    def small_view(a):
        return a.transpose(1, 0, 2) if a.ndim == 3 else a.reshape(-1, a.shape[-1])

    small_w = (norm_in, conv_w, attn_sinks, norm_conv_out, norm_attn_out, norm_final)
    small_m = (m_norm_in, m_conv_w, m_attn_sinks, m_norm_conv_out, m_norm_attn_out, m_norm_final)
    small_v = (v_norm_in, v_conv_w, v_attn_sinks, v_norm_conv_out, v_norm_attn_out, v_norm_final)
    small_g = (row_of(SMALL_NORM_IN), conv_taps, sink_lanes, row_of(SMALL_NORM_CONV), row_of(SMALL_NORM_ATTN),
               row_of(SMALL_NORM_FINAL))
    w_views, m_views, v_views = (tuple(small_view(a) for a in group) for group in (small_w, small_m, small_v))
    up_small, loss = _adamw_small(chip.astype(jnp.int32).reshape(1), small_sum, w_views, small_g, m_views, v_views)
    up_small = [tuple(o.transpose(1, 0, 2) if w.ndim == 3 else o.reshape(w.shape) for o in up)
                for up, w in zip(up_small, small_w)]
    up_norm_in, up_conv_w, up_sinks, up_norm_conv, up_norm_attn, up_norm_final = up_small
    updates = (up_norm_in, up_w_in, up_conv_w, up_sinks, up_norm_conv, up_norm_attn, up_w_out, up_norm_final)
    grads_out, deltas, new_m, new_v = zip(*updates)
    return (loss, grad_x[None], *grads_out, *deltas, *new_m, *new_v)
```

```python
import jax
import jax.numpy as jnp
from jax import lax
from jax.experimental import pallas as pl
from jax.experimental.pallas import tpu as pltpu

F32 = jnp.float32
BF16 = jnp.bfloat16
MESH = pl.DeviceIdType.MESH

D_MODEL = 1024
D_CONV = 1024
D_ATTN = 1024
D_KV = 128
D_QG = 2 * D_ATTN + 2 * D_KV
D_MIX = D_CONV + D_ATTN
D_PC = 4 * D_CONV
D_IN_PROJ = D_PC + 2 * D_ATTN + 2 * D_KV
ROW_Q = D_PC
ROW_KV = ROW_Q + D_ATTN
ROW_GA = ROW_KV + 2 * D_KV
N_HEADS = 16
HEAD_DIM = 64
HEADS_PER_KV = 8
BLK = 128
N_CHIPS = 4
RMS_EPS = 1e-5
SCALE = HEAD_DIM ** -0.5
SLOPES = tuple(2.0 ** (-8.0 * (h + 1) / N_HEADS) for h in range(N_HEADS))

ADAM_LR, ADAM_B1, ADAM_B2, ADAM_EPS, ADAM_WD, ADAM_STEP = 0.001, 0.9, 0.999, 1e-08, 0.01, 10

SMALL_ROWS = 8
SMALL_NORM_IN, SMALL_NORM_CONV, SMALL_NORM_ATTN, SMALL_NORM_FINAL, SMALL_CONV_W, SMALL_MISC = 0, 1, 2, 3, 4, 7
SMALL_LOSS_LANE = N_HEADS

TOK_TILE = 256
PC_PIECE = 512
MIB = 1 << 20


def _params(vmem_mib, semantics=None):
    return pltpu.CompilerParams(dimension_semantics=semantics, vmem_limit_bytes=vmem_mib * MIB)


def _nn(a, b):
    return jnp.dot(a, b, preferred_element_type=F32)


def _nt(a, b):
    return lax.dot_general(a, b, (((1,), (1,)), ((), ())), preferred_element_type=F32)


def _tn(a, b):
    return lax.dot_general(a, b, (((0,), (0,)), ((), ())), preferred_element_type=F32)


def _rstd(v):
    return lax.rsqrt(jnp.mean(v * v, axis=-1, keepdims=True) + RMS_EPS)


def _rms_bwd(g, xhat, rstd):
    return rstd * (g - xhat * jnp.mean(g * xhat, axis=-1, keepdims=True))


def _silu_and_grad(g):
    s = jax.nn.sigmoid(g)
    return g * s, s * (1.0 + g * (1.0 - s))


def _resident(shape):
    return pl.BlockSpec(shape, lambda *_: (0,) * len(shape), pipeline_mode=pl.Buffered(1))


def _xor(a, b):
    return a + b - 2 * a * b


def _cast_rows(src_ref, dst_ref, rows=32):
    def step(i, carry):
        sl = pl.ds(pl.multiple_of(i * rows, rows), rows)
        dst_ref[sl, :] = src_ref[sl, :].astype(dst_ref.dtype)
        return carry
    lax.fori_loop(0, src_ref.shape[0] // rows, step, 0)


def _ag_scratch(shard_shape):
    rows, width = shard_shape
    return [pltpu.VMEM((rows, width), F32), pltpu.VMEM((rows, width), BF16), pltpu.VMEM((3, rows // 2, width), BF16),
            pltpu.SemaphoreType.DMA((6,)), pltpu.SemaphoreType.DMA((6,)), pltpu.SemaphoreType.DMA((4,))]


def _ag_stages(sh_ref, out, f32_buf, own, land, send_sems, recv_sems, local_sems):
    rows = sh_ref.shape[0]
    half = rows // 2
    x, y, c = lax.axis_index("x"), lax.axis_index("y"), lax.axis_index("c")
    j = 2 * x + y
    p1 = (_xor(x, c), _xor(y, 1 - c), c)
    p2 = (_xor(x, 1 - c), _xor(y, c), c)
    sib = (x, y, 1 - c)
    j1 = 2 * p1[0] + p1[1]
    j2 = 2 * p2[0] + p2[1]
    j3 = 3 - j

    def rows_of(chip, hf):
        return out.at[pl.ds(pl.multiple_of(chip * rows + hf * half, 16), half), :]

    def rcopy(k, src, dst, to):
        return pltpu.make_async_remote_copy(src_ref=src, dst_ref=dst, send_sem=send_sems.at[k],
                                            recv_sem=recv_sems.at[k], device_id=to, device_id_type=MESH)

    my_half = own.at[pl.ds(pl.multiple_of(c * half, 16), half), :]
    hop1 = rcopy(0, my_half, land.at[0], p1)
    hop2_own = rcopy(1, my_half, land.at[1], p2)
    hop2_fwd = rcopy(2, land.at[0], land.at[2], p2)
    swaps = [rcopy(3 + s, land.at[s], rows_of(chip, c), sib) for s, chip in enumerate((j1, j2, j3))]
    keeps = [pltpu.make_async_copy(land.at[s], rows_of(chip, c), local_sems.at[1 + s]) for s, chip in enumerate((j1, j2, j3))]
    load = pltpu.make_async_copy(sh_ref, f32_buf, local_sems.at[0])
    own_out = pltpu.make_async_copy(own, out.at[pl.ds(pl.multiple_of(j * rows, 16), rows), :], local_sems.at[0])

    def stage_send():
        load.start()
        load.wait()
        _cast_rows(f32_buf, own)
        own_out.start()
        hop1.start()

    def stage_forward():
        hop1.wait_recv()
        hop2_own.start()
        hop2_fwd.start()
        swaps[0].start()
        keeps[0].start()

    def stage_publish():
        hop2_own.wait_recv()
        swaps[1].start()
        keeps[1].start()
        hop2_fwd.wait_recv()
        swaps[2].start()
        keeps[2].start()

    def stage_drain():
        for s, chip in enumerate((j2, j1, j3)):
            rcopy(3 + s, my_half, rows_of(chip, 1 - c), sib).wait_recv()
        for cp in [hop1, hop2_own, hop2_fwd] + swaps:
            cp.wait_send()
        for cp in [own_out] + keeps:
            cp.wait()

    return stage_send, stage_forward, stage_publish, stage_drain


AG_CAST_ROWS = 400


def _gather_resident(sh_ref, out, f32_buf, send_sems, recv_sems, local_sems):
    rows = sh_ref.shape[0]
    half = rows // 2
    x, y, c = lax.axis_index("x"), lax.axis_index("y"), lax.axis_index("c")
    j = 2 * x + y
    p1 = (_xor(x, c), _xor(y, 1 - c), c)
    p2 = (_xor(x, 1 - c), _xor(y, c), c)
    sib = (x, y, 1 - c)
    j1 = 2 * p1[0] + p1[1]
    j2 = 2 * p2[0] + p2[1]
    j3 = 3 - j

    def rows_of(chip, hf):
        return out.at[pl.ds(pl.multiple_of(chip * rows + hf * half, 16), half), :]

    def send(k, chip, to):
        return pltpu.make_async_remote_copy(src_ref=rows_of(chip, c), dst_ref=rows_of(chip, c), send_sem=send_sems.at[k],
                                            recv_sem=recv_sems.at[k], device_id=to, device_id_type=MESH)

    per_half = half // AG_CAST_ROWS

    def cast_own(chunk):
        lo = pl.multiple_of(chunk * AG_CAST_ROWS, 16)
        load = pltpu.make_async_copy(sh_ref.at[pl.ds(lo, AG_CAST_ROWS), :], f32_buf, local_sems.at[0])
        load.start()
        load.wait()
        _cast_rows(f32_buf, out.at[pl.ds(pl.multiple_of(j * rows + lo, 16), AG_CAST_ROWS), :], rows=16)

    for k in range(per_half):
        cast_own(c * per_half + k)
    hop1 = send(0, j, p1)
    hop1.start()
    for k in range(per_half):
        cast_own((1 - c) * per_half + k)
    hop1.wait_recv()
    sends = [hop1, send(1, j, p2), send(2, j1, p2), send(3, j1, sib)]
    for cp in sends[1:]:
        cp.start()
    sends[1].wait_recv()
    sends.append(send(4, j2, sib))
    sends[-1].start()
    sends[2].wait_recv()
    sends.append(send(5, j3, sib))
    sends[-1].start()
    for k in (3, 4, 5):
        send(k, j, sib).wait_recv()
    for cp in sends:
        cp.wait_send()


def _fwd_in(x, norm_in, wt_sh, wo_sh, cw_sh, norm_conv_out, sinks, norm_attn_out):
    seq = x.shape[0]
    tile = TOK_TILE
    n_tiles = seq // tile
    stage_steps = (0, n_tiles // 4, (5 * n_tiles) // 8, n_tiles - 1)
    blocks = tile // BLK
    cw_cols = cw_sh.shape[-1]

    def body(x_ref, g_ref, wtsh_ref, wo_ref, cwsh_ref, gn_ref, sink_ref, gna_ref,
             h_ref, pc_ref, q_ref, kv_ref, ga_ref, oc_ref, ya_ref, oa_ref, pr_ref, sp_ref, wt_out, cw_ref, wo_out,
             zbuf, kv_last, wt_ref, f32_buf, cw_land, wt_send, wt_recv, wt_local, cw_send, cw_recv, cw_local, *ag_scratch):
        step = pl.program_id(0)
        to_hbm = pltpu.make_async_copy(wt_ref, wt_out, wt_local.at[0])

        @pl.when(step == 0)
        def _():
            zbuf[0:8, :] = jnp.zeros((8, D_CONV), F32)
            kv_last[...] = jnp.zeros_like(kv_last)
            x_i, y_i, c = lax.axis_index("x"), lax.axis_index("y"), lax.axis_index("c")
            j = 2 * x_i + y_i
            p1 = (_xor(x_i, c), _xor(y_i, 1 - c), c)
            p2 = (_xor(x_i, 1 - c), _xor(y_i, c), c)
            j1 = 2 * p1[0] + p1[1]

            def cw_copy(k, src, chip, to):
                return pltpu.make_async_remote_copy(src_ref=src, dst_ref=cw_land.at[chip], send_sem=cw_send.at[k],
                                                    recv_sem=cw_recv.at[k], device_id=to, device_id_type=MESH)

            mine = pltpu.make_async_copy(cwsh_ref, cw_land.at[j], cw_local.at[0])
            mine.start()
            first = cw_copy(0, cwsh_ref, j, p1)
            first.start()
            _gather_resident(wtsh_ref, wt_ref, f32_buf, wt_send, wt_recv, wt_local)
            to_hbm.start()
            first.wait_recv()
            second = [cw_copy(1, cwsh_ref, j, p2), cw_copy(2, cw_land.at[j1], j1, p2)]
            for cp in second:
                cp.start()
            for cp in second:
                cp.wait_recv()
            for cp in [first] + second:
                cp.wait_send()
            mine.wait()
            for chip in range(N_CHIPS):
                for tap in range(cw_sh.shape[0]):
                    cw_ref[tap:tap + 1, chip * cw_cols:(chip + 1) * cw_cols] = cw_land[chip, tap]

        stages = _ag_stages(wo_ref, wo_out, *ag_scratch)
        for at, stage in zip(stage_steps[:-1], stages[:-1]):
            pl.when(step == at)(stage)

        def attention(b):
            rows = pl.ds(b * BLK, BLK)
            kv_prev = kv_last if b == 0 else kv_ref.at[pl.ds((b - 1) * BLK, BLK), :]
            return _attn_forward(q_ref.at[rows, :], kv_ref.at[rows, :], kv_prev, ga_ref.at[rows, :], sink_ref, gna_ref,
                                 _band_geometry(step * blocks + b), ya_ref.at[rows, :], oa_ref.at[rows, :],
                                 pr_ref.at[rows, :], sp_ref.at[rows, :])

        xv = x_ref[...]
        h = (xv * _rstd(xv) * g_ref[...]).astype(BF16)
        h_ref[...] = h
        q_ref[...] = _nt(h, wt_ref[ROW_Q:ROW_KV, :])
        kv_ref[...] = _nt(h, wt_ref[ROW_KV:ROW_GA, :])
        ga_ref[...] = _nt(h, wt_ref[ROW_GA:D_IN_PROJ, :])
        attention_blocks = [attention(b) for b in range(blocks)]
        for lo in range(0, D_PC, PC_PIECE):
            pc_ref[:, lo:lo + PC_PIECE] = _nt(h, wt_ref[lo:lo + PC_PIECE, :])
            for stages_of_block in attention_blocks:
                next(stages_of_block, None)
        for stages_of_block in attention_blocks:
            for _ in stages_of_block:
                pass
        kv_last[...] = kv_ref[tile - BLK:tile, :]

        cb, _, _, _, _, _, conv = _conv_core(pc_ref, zbuf, cw_ref)
        yc = cb * conv
        silu, _ = _silu_and_grad(pc_ref[:, 3 * D_CONV:4 * D_CONV])
        oc_ref[...] = (yc * _rstd(yc) * gn_ref[...] * silu).astype(BF16)
        zbuf[0:8, :] = zbuf[tile:tile + 8, :]

        @pl.when(step == stage_steps[-1])
        def _():
            stages[-1]()
            to_hbm.wait()

    def row(width):
        return pl.BlockSpec((tile, width), lambda i: (i, 0))

    any_spec = pl.BlockSpec(memory_space=pl.ANY)
    dma = pltpu.SemaphoreType.DMA
    wt_shape = (N_CHIPS * wt_sh.shape[0], wt_sh.shape[1])
    cw_shape = (cw_sh.shape[0], N_CHIPS * cw_cols)
    return pl.pallas_call(
        body, name="fwd_in", grid=(n_tiles,),
        out_shape=(jax.ShapeDtypeStruct((seq, D_MODEL), BF16), jax.ShapeDtypeStruct((seq, D_PC), F32),
                   jax.ShapeDtypeStruct((seq, D_ATTN), F32), jax.ShapeDtypeStruct((seq, 2 * D_KV), F32),
                   jax.ShapeDtypeStruct((seq, D_ATTN), F32), jax.ShapeDtypeStruct((seq, D_CONV), BF16),
                   pltpu.HBM((seq, D_ATTN), F32), pltpu.HBM((seq, D_ATTN), BF16),
                   pltpu.HBM((seq, N_HEADS * BLK), BF16), pltpu.HBM((seq, 128), F32),
                   pltpu.HBM(wt_shape, BF16), jax.ShapeDtypeStruct(cw_shape, F32),
                   jax.ShapeDtypeStruct((N_CHIPS * wo_sh.shape[0], wo_sh.shape[1]), BF16)),
        in_specs=[row(D_MODEL), _resident((1, D_MODEL)), any_spec, any_spec, any_spec, _resident((1, D_CONV)),
                  pl.BlockSpec(memory_space=pltpu.SMEM), _resident((1, D_ATTN))],
        out_specs=(row(D_MODEL), row(D_PC), row(D_ATTN), row(2 * D_KV), row(D_ATTN), row(D_CONV), row(D_ATTN),
                   row(D_ATTN), row(N_HEADS * BLK), row(128), any_spec, pl.BlockSpec(cw_shape, lambda i: (0, 0)),
                   any_spec),
        scratch_shapes=[pltpu.VMEM((tile + 8, D_CONV), F32), pltpu.VMEM((BLK, 2 * D_KV), F32),
                        pltpu.VMEM(wt_shape, BF16), pltpu.VMEM((AG_CAST_ROWS, wt_sh.shape[1]), F32),
                        pltpu.VMEM((N_CHIPS,) + cw_sh.shape, F32),
                        dma((6,)), dma((6,)), dma((1,)), dma((3,)), dma((3,)), dma((1,))] + _ag_scratch(wo_sh.shape),
        compiler_params=_params(62, ("arbitrary",)),
    )(x, norm_in, wt_sh, wo_sh, cw_sh, norm_conv_out, sinks, norm_attn_out)


def _conv_core(pc_ref, zbuf, cw_ref):
    tile = pc_ref.shape[0]
    cb = pc_ref[:, 0:D_CONV]
    cc = pc_ref[:, D_CONV:2 * D_CONV]
    cu = pc_ref[:, 2 * D_CONV:3 * D_CONV]
    z = cc * cu
    zbuf[8:tile + 8, :] = z
    z1 = zbuf[7:tile + 7, :]
    z2 = zbuf[6:tile + 6, :]
    conv = cw_ref[0:1, :] * z2 + cw_ref[1:2, :] * z1 + cw_ref[2:3, :] * z
    return cb, cc, cu, z, z1, z2, conv


def _band_geometry(block_index):
    qi = lax.broadcasted_iota(jnp.int32, (BLK, BLK), 0)
    kp = lax.broadcasted_iota(jnp.int32, (BLK, BLK), 1)
    use_cur = kp <= qi
    dist = jnp.where(use_cur, qi - kp, qi - kp + BLK).astype(F32)
    valid = use_cur | (block_index > 0)
    return use_cur, dist, valid


def _block_diag(cur, prev, group):
    lane = lax.broadcasted_iota(jnp.int32, cur.shape, 1)

    def halves(t):
        other = pltpu.roll(t, 64, 1)
        lo, hi = (t, other) if group == 0 else (other, t)
        return jnp.where(lane < 64, lo, 0.0), jnp.where(lane >= 64, hi, 0.0)

    return jnp.concatenate(halves(cur) + halves(prev), axis=0).astype(BF16)


def _merge(s4, use_cur):
    return (jnp.where(use_cur, s4[:, 0:BLK], s4[:, 2 * BLK:3 * BLK]),
            jnp.where(use_cur, s4[:, BLK:2 * BLK], s4[:, 3 * BLK:4 * BLK]))


def _split(a, b, use_cur):
    return jnp.concatenate([jnp.where(use_cur, a, 0.0), jnp.where(use_cur, b, 0.0),
                            jnp.where(use_cur, 0.0, a), jnp.where(use_cur, 0.0, b)], axis=1)


def _softmax_head(s, head, sink, dist, valid):
    sc = jnp.where(valid, s - SLOPES[head] * dist, -jnp.inf)
    m = jnp.maximum(jnp.max(sc, axis=-1, keepdims=True), sink)
    p = jnp.exp(sc - m)
    es = jnp.exp(sink - m)
    inv = 1.0 / (jnp.sum(p, axis=-1, keepdims=True) + es)
    return p * inv, es * inv


def _attn_operands(q_ref, kvc_ref, kvp_ref):
    groups = range(N_HEADS // HEADS_PER_KV)
    kbd = [_block_diag(kvc_ref[:, 0:D_KV], kvp_ref[:, 0:D_KV], g) for g in groups]
    vbd = [_block_diag(kvc_ref[:, D_KV:2 * D_KV], kvp_ref[:, D_KV:2 * D_KV], g) for g in groups]
    qps = [(q_ref[:, j * 128:(j + 1) * 128] * SCALE).astype(BF16) for j in range(N_HEADS // 2)]
    return qps, kbd, vbd


def _attn_forward(q_ref, kvc_ref, kvp_ref, ga_ref, sink_ref, gn_ref, geometry, ya_ref, oa_ref, pr_ref, sp_ref):
    use_cur, dist, valid = geometry
    _, kbd, vbd = operands = _attn_operands(q_ref, kvc_ref, kvp_ref)
    yield
    scores = []
    for j, qp in enumerate(operands[0]):
        scores += _merge(_nt(qp, kbd[j // 4]), use_cur)
    yield
    sinks = [sink_ref[0, h] for h in range(N_HEADS)]
    scores = [jnp.where(valid, s - SLOPES[h] * dist, -jnp.inf) for h, s in enumerate(scores)]
    maxes = [jnp.maximum(jnp.max(s, axis=-1, keepdims=True), sinks[h]) for h, s in enumerate(scores)]
    yield
    exps = [jnp.exp(s - m) for s, m in zip(scores, maxes)]
    sink_exps = [jnp.exp(sinks[h] - m) for h, m in enumerate(maxes)]
    yield
    invs = [1.0 / (jnp.sum(e, axis=-1, keepdims=True) + se) for e, se in zip(exps, sink_exps)]
    probs = [e * inv for e, inv in zip(exps, invs)]
    pr_ref[...] = jnp.concatenate(probs, axis=1).astype(BF16)
    lane = lax.broadcasted_iota(jnp.int32, (BLK, 128), 1)
    sp_ref[...] = sum(jnp.where(lane == h, se * inv, 0.0) for h, (se, inv) in enumerate(zip(sink_exps, invs)))
    yield
    p4s = [_split(probs[2 * j], probs[2 * j + 1], use_cur).astype(BF16) for j in range(N_HEADS // 2)]
    ya = jnp.concatenate([_nn(p4, vbd[j // 4]) for j, p4 in enumerate(p4s)], axis=1)
    ya_ref[...] = ya
    yield
    silu, _ = _silu_and_grad(ga_ref[...])
    oa_ref[...] = (ya * _rstd(ya) * gn_ref[...] * silu).astype(BF16)


def _kv_specs(n_blocks, reverse):
    def blk(i):
        return (n_blocks - 1 - i) if reverse else i
    cur = pl.BlockSpec((BLK, 2 * D_KV), lambda i: (blk(i), 0))
    prev = pl.BlockSpec((BLK, 2 * D_KV), lambda i: (jnp.maximum(blk(i) - 1, 0), 0))
    return cur, prev


def _out_proj_loss(x, oc, oa, wo, norm_final, target, pc, conv_w, norm_conv_out):
    seq = x.shape[0]
    tile = TOK_TILE
    n_tiles = seq // tile

    def body(x_ref, oc_ref, oa_ref, wo_ref, gf_ref, t_ref, pc_ref, hcc_ref, hcu_ref, cw_ref, gn_ref,
             dx2_ref, doa_ref, gwo_ref, dpc_ref, small_ref, loss_acc, zbuf, dbuf):
        step = pl.program_id(0)

        def small_add(row, value):
            small_ref[row:row + 1, :] += jnp.sum(value, axis=0, keepdims=True)

        @pl.when(step == 0)
        def _():
            gwo_ref[...] = jnp.zeros_like(gwo_ref)
            small_ref[...] = jnp.zeros_like(small_ref)
            loss_acc[...] = jnp.zeros_like(loss_acc)
            dbuf[tile:tile + 8, :] = jnp.zeros((8, D_CONV), F32)

        oc, oa = oc_ref[...], oa_ref[...]
        x2 = x_ref[...] + _nn(oc, wo_ref[0:D_CONV, :]) + _nn(oa, wo_ref[D_CONV:D_MIX, :])
        r = _rstd(x2)
        xhat = x2 * r
        err = xhat * gf_ref[...] - t_ref[...]
        loss_acc[...] += jnp.sum(err * err, axis=0, keepdims=True) * (0.5 / D_MODEL)
        dy = err * (1.0 / D_MODEL)
        small_add(SMALL_NORM_FINAL, dy * xhat)
        dx2 = _rms_bwd(dy * gf_ref[...], xhat, r)
        dx2_ref[...] = dx2
        db = dx2.astype(BF16)
        do = _nt(db, wo_ref[0:D_CONV, :])
        doa_ref[...] = _nt(db, wo_ref[D_CONV:D_MIX, :])
        gwo_ref[0:D_CONV, :] += _tn(oc, db)
        gwo_ref[D_CONV:D_MIX, :] += _tn(oa, db)

        is_first_tile = step == n_tiles - 1
        zbuf[0:8, :] = jnp.where(is_first_tile, 0.0, hcc_ref[...] * hcu_ref[...])
        cb, cc, cu, z, z1, z2, conv = _conv_core(pc_ref, zbuf, cw_ref)
        silu, dsilu = _silu_and_grad(pc_ref[:, 3 * D_CONV:4 * D_CONV])
        yc = cb * conv
        rc = _rstd(yc)
        chat = yc * rc
        dn = do * silu
        dpc_ref[:, 3 * D_CONV:4 * D_CONV] = (do * (chat * gn_ref[...]) * dsilu).astype(BF16)
        small_add(SMALL_NORM_CONV, dn * chat)
        dyc = _rms_bwd(dn * gn_ref[...], chat, rc)
        dpc_ref[:, 0:D_CONV] = (dyc * conv).astype(BF16)
        dconv = dyc * cb
        small_add(SMALL_CONV_W, dconv * z2)
        small_add(SMALL_CONV_W + 1, dconv * z1)
        small_add(SMALL_CONV_W + 2, dconv * z)
        dbuf[0:tile, :] = dconv
        dz = cw_ref[2:3, :] * dconv + cw_ref[1:2, :] * dbuf[1:tile + 1, :] + cw_ref[0:1, :] * dbuf[2:tile + 2, :]
        dpc_ref[:, D_CONV:2 * D_CONV] = (dz * cu).astype(BF16)
        dpc_ref[:, 2 * D_CONV:3 * D_CONV] = (dz * cc).astype(BF16)
        dbuf[tile:tile + 8, :] = dbuf[0:8, :]

        @pl.when(step == n_tiles - 1)
        def _():
            lane = lax.broadcasted_iota(jnp.int32, (1, D_MODEL), 1)
            small_ref[SMALL_MISC:SMALL_MISC + 1, :] = jnp.where(lane == SMALL_LOSS_LANE, jnp.sum(loss_acc[...]), 0.0)

    def rev(i):
        return n_tiles - 1 - i

    def row(width):
        return pl.BlockSpec((tile, width), lambda i: (rev(i), 0))

    def halo(col_block):
        return pl.BlockSpec((8, D_CONV), lambda i: (jnp.maximum(rev(i) * (tile // 8) - 1, 0), col_block))

    def const(shape):
        return pl.BlockSpec(shape, lambda i: (0, 0))

    return pl.pallas_call(
        body, name="out_proj_loss", grid=(n_tiles,),
        out_shape=(jax.ShapeDtypeStruct((seq, D_MODEL), F32), jax.ShapeDtypeStruct((seq, D_ATTN), F32),
                   jax.ShapeDtypeStruct((D_MIX, D_MODEL), F32), jax.ShapeDtypeStruct((seq, D_PC), BF16),
                   jax.ShapeDtypeStruct((SMALL_ROWS, D_MODEL), F32)),
        in_specs=[row(D_MODEL), row(D_CONV), row(D_ATTN), _resident(wo.shape), _resident((1, D_MODEL)), row(D_MODEL),
                  row(D_PC), halo(1), halo(2), _resident(conv_w.shape), _resident((1, D_CONV))],
        out_specs=(row(D_MODEL), row(D_ATTN), const((D_MIX, D_MODEL)), row(D_PC), const((SMALL_ROWS, D_MODEL))),
        scratch_shapes=[pltpu.VMEM((1, D_MODEL), F32), pltpu.VMEM((tile + 8, D_CONV), F32),
                        pltpu.VMEM((tile + 8, D_CONV), F32)],
        compiler_params=_params(62, ("arbitrary",)),
    )(x, oc, oa, wo, norm_final, target, pc, pc, pc, conv_w, norm_conv_out)


def _attn_bwd(q, kv, ga, ya, doa, probs, sink_probs, norm_attn_out, gwo, small):
    seq = q.shape[0]
    n_blocks = seq // BLK
    stage_steps = (0, n_blocks // 4, n_blocks // 2, (3 * n_blocks) // 4, n_blocks - 1)

    def body(q_ref, kvc_ref, kvp_ref, ga_ref, ya_ref, doa_ref, pr_ref, sp_ref, gn_ref, gwo_ref, small_ref,
             dqg_ref, small_out, gwo_sh, gna_ref, gs_ref, carry, dya_buf, *rs_scratch):
        step = pl.program_id(0)

        @pl.when(step == 0)
        def _():
            gna_ref[...] = jnp.zeros_like(gna_ref)
            gs_ref[...] = jnp.zeros_like(gs_ref)
            carry[...] = jnp.zeros_like(carry)

        ya = ya_ref[...]
        r = _rstd(ya)
        xhat = ya * r
        silu, dsilu = _silu_and_grad(ga_ref[...])
        do = doa_ref[...]
        dn = do * silu
        dqg_ref[:, D_ATTN:2 * D_ATTN] = (do * (xhat * gn_ref[...]) * dsilu).astype(BF16)
        gna_ref[...] += jnp.sum(dn * xhat, axis=0, keepdims=True)
        dya_buf[...] = _rms_bwd(dn * gn_ref[...], xhat, r).astype(BF16)

        use_cur = _band_geometry(n_blocks - 1 - step)[0]
        lane = lax.broadcasted_iota(jnp.int32, (BLK, 128), 1)

        def fold(bd):
            return (jnp.where(lane < 64, bd[0:BLK], 0.0) + jnp.where(lane >= 64, bd[BLK:2 * BLK], 0.0),
                    jnp.where(lane < 64, bd[2 * BLK:3 * BLK], 0.0) + jnp.where(lane >= 64, bd[3 * BLK:4 * BLK], 0.0))

        pairs = range(N_HEADS // 2)
        qps, kbd, vbd = _attn_operands(q_ref, kvc_ref, kvp_ref)
        probs = [pr_ref[:, h * BLK:(h + 1) * BLK].astype(F32) for h in range(N_HEADS)]
        dyps = [dya_buf[:, j * 128:(j + 1) * 128] for j in pairs]
        dps = []
        for j in pairs:
            dps += _merge(_nt(dyps[j], vbd[j // 4]), use_cur)
        deltas = [jnp.sum(p * dp, axis=-1, keepdims=True) for p, dp in zip(probs, dps)]
        dss = [p * (dp - delta) for p, dp, delta in zip(probs, dps, deltas)]
        delta_lanes = sum(jnp.where(lane == h, deltas[h], 0.0) for h in range(N_HEADS))
        gs_ref[...] -= jnp.sum(sp_ref[...] * delta_lanes, axis=0, keepdims=True)
        ds4s = [_split(dss[2 * j], dss[2 * j + 1], use_cur).astype(BF16) for j in pairs]
        p4s = [_split(probs[2 * j], probs[2 * j + 1], use_cur).astype(BF16) for j in pairs]
        dqg_ref[:, 0:D_ATTN] = jnp.concatenate([_nn(ds4s[j], kbd[j // 4]) * SCALE for j in pairs], axis=1).astype(BF16)
        sums = []
        for group in range(N_HEADS // HEADS_PER_KV):
            acc = [jnp.zeros((BLK, 128), F32) for _ in range(4)]
            for j in range(group * 4, group * 4 + 4):
                for slot, part in enumerate(fold(_tn(ds4s[j], qps[j])) + fold(_tn(p4s[j], dyps[j]))):
                    acc[slot] = acc[slot] + part
            sums.append([a + pltpu.roll(a, 64, 1) for a in acc])
        dk_cur, dk_prev, dv_cur, dv_prev = (jnp.where(lane < 64, a, b) for a, b in zip(sums[0], sums[1]))
        dqg_ref[:, 2 * D_ATTN:2 * D_ATTN + 2 * D_KV] = (jnp.concatenate([dk_cur, dv_cur], axis=1) + carry[...]).astype(BF16)
        carry[...] = jnp.concatenate([dk_prev, dv_prev], axis=1)

        @pl.when(step == n_blocks - 1)
        def _():
            small_out[...] = small_ref[...]
            small_out[SMALL_NORM_ATTN:SMALL_NORM_ATTN + 1, :] = gna_ref[...]
            small_out[SMALL_MISC:SMALL_MISC + 1, 0:128] = small_ref[SMALL_MISC:SMALL_MISC + 1, 0:128] + gs_ref[...]

        for at, stage in zip(stage_steps, _rs_wout_stages(gwo_ref, gwo_sh, *rs_scratch)):
            pl.when(step == at)(stage)

    row = pl.BlockSpec((BLK, D_ATTN), lambda i: (n_blocks - 1 - i, 0))
    kv_cur, kv_prev = _kv_specs(n_blocks, reverse=True)
    any_spec = pl.BlockSpec(memory_space=pl.ANY)
    return pl.pallas_call(
        body, name="attn_bwd", grid=(n_blocks,),
        out_shape=(jax.ShapeDtypeStruct((seq, D_QG), BF16), pltpu.HBM(small.shape, F32),
                   pltpu.HBM((gwo.shape[0] // N_CHIPS, gwo.shape[1]), F32)),
        in_specs=[row, kv_cur, kv_prev, row, row, row,
                  pl.BlockSpec((BLK, N_HEADS * BLK), lambda i: (n_blocks - 1 - i, 0)),
                  pl.BlockSpec((BLK, 128), lambda i: (n_blocks - 1 - i, 0)), _resident((1, D_ATTN)),
                  any_spec, _resident(small.shape)],
        out_specs=(pl.BlockSpec((BLK, D_QG), lambda i: (n_blocks - 1 - i, 0)),
                   pl.BlockSpec(small.shape, lambda i: (0, 0)), any_spec),
        scratch_shapes=[pltpu.VMEM((1, D_ATTN), F32), pltpu.VMEM((1, 128), F32),
                        pltpu.VMEM((BLK, 2 * D_KV), F32), pltpu.VMEM((BLK, D_ATTN), BF16)] + _rs_wout_scratch(gwo.shape),
        compiler_params=_params(44, ("arbitrary",)),
    )(q, kv, kv, ga, ya, doa, probs, sink_probs, norm_attn_out, gwo, small)


GBLK = 256
GSUB = 64
PAIR_RING = 4
LAG_PAIR, LAG_HOP1, LAG_HOP2 = 1, 7, 14


def _bwd_in(dpc, dqg, h, wt, x, norm_in, dx2, small):
    seq = x.shape[0]
    n_blk = D_IN_PROJ // GBLK
    per_chip = n_blk // N_CHIPS
    n_slots = (n_blk + 1) // 2
    n_sub = GBLK // GSUB
    chip_rows = D_IN_PROJ // N_CHIPS
    tile = TOK_TILE
    n_tiles = seq // tile
    n_steps = n_blk + max(n_tiles, LAG_HOP2)
    chunk = min(seq, 512)
    blk_q, blk_kv, blk_ga = ROW_Q // GBLK, ROW_KV // GBLK, ROW_GA // GBLK

    def block_of(i):
        k = i % N_CHIPS
        robin = per_chip * ((k % 2) * 2 + k // 2) + i // N_CHIPS
        if isinstance(i, int):
            return robin if i < per_chip * N_CHIPS else i
        return jnp.where(i < per_chip * N_CHIPS, robin, i)

    def owner_of(i):
        return (i // N_CHIPS) % 2

    def slot_of(i):
        return (i // (2 * N_CHIPS)) * N_CHIPS + i % N_CHIPS

    def body(dpc_ref, dqg_ref, wt_ref, h_ref, x_ref, g_ref, dx2_ref, small_ref, gx_ref, small_sum, gwt_sh,
             dh_acc, gni, keep, pbuf, xbuf, land, land2, small_land,
             pair_send, pair_recv, h1_send, h1_recv, h2_send, h2_recv, sw_send, sw_recv, sm_send, sm_recv, out_sem):
        step = pl.program_id(0)
        x_i, y_i, c = lax.axis_index("x"), lax.axis_index("y"), lax.axis_index("c")
        me = 4 * x_i + 2 * y_i + c
        j = 2 * x_i + y_i
        pa = (_xor(x_i, 1 - c), _xor(y_i, c), c)
        pb = (_xor(x_i, c), _xor(y_i, 1 - c), c)
        sib = (x_i, y_i, 1 - c)
        ja = 2 * pa[0] + pa[1]
        jb = 2 * pb[0] + pb[1]
        jd = 3 - j

        def remote(src, dst, send, recv, to):
            return pltpu.make_async_remote_copy(src_ref=src, dst_ref=dst, send_sem=send, recv_sem=recv,
                                                device_id=to, device_id_type=MESH)

        def piece(ref, slot, u, n):
            return ref.at[slot, pl.ds(u * GSUB, n * GSUB), :]

        def chip_rows_at(ref, local, n):
            return ref.at[pl.ds(pl.multiple_of(local, GSUB), n * GSUB), :]

        def pair_copy(i):
            slot = slot_of(i)
            return remote(pbuf.at[i % PAIR_RING], land.at[slot], pair_send.at[slot], pair_recv.at[slot], sib)

        def h1_copy(slot, u, n):
            k = slot * n_sub + u
            return remote(piece(xbuf, slot, u, n), piece(xbuf, slot, u, n), h1_send.at[k], h1_recv.at[k], pa)

        def h2_copy(slot, u, n, local):
            k = slot * n_sub + u
            return remote(piece(xbuf, slot, u, n), chip_rows_at(land2, local, n), h2_send.at[k], h2_recv.at[k], pb)

        def sw_copy(slot, u, n, local):
            k = slot * n_sub + u
            return remote(piece(keep, slot, u, n), chip_rows_at(gwt_sh, local, n), sw_send.at[k], sw_recv.at[k], sib)

        def owned(i):
            return (i >= 0) & (i < n_blk) & (owner_of(i) == c)

        def chip_of(blk, u):
            row = blk * GBLK + u * GSUB
            chip = row // chip_rows
            return chip, row - chip * chip_rows

        def pieces(blk):
            first, local = chip_of(blk, 0)
            whole = first == chip_of(blk, n_sub - 1)[0]
            if isinstance(blk, int):
                return [(True, 0, n_sub, first, local)] if whole else [(True, u, 1) + chip_of(blk, u) for u in range(n_sub)]
            return [(whole, 0, n_sub, first, local)] + [(jnp.logical_not(whole), u, 1) + chip_of(blk, u) for u in range(n_sub)]

        @pl.when(step == 0)
        def _():
            dh_acc[...] = jnp.zeros_like(dh_acc)
            gni[...] = jnp.zeros_like(gni)

        @pl.when(step < n_blk)
        def _():
            from_pc = block_of(step) < blk_q
            block = _tn(jnp.where(from_pc, dpc_ref[...], dqg_ref[...]), h_ref[...])
            for t in range(0, seq, chunk):
                d = jnp.where(from_pc, dpc_ref[t:t + chunk, :], dqg_ref[t:t + chunk, :])
                dh_acc[t:t + chunk, :] += _nn(d, wt_ref[...])

            @pl.when(owner_of(step) == c)
            def _():
                keep[slot_of(step)] = block

            @pl.when(owner_of(step) != c)
            def _():
                @pl.when(step >= 2 * PAIR_RING)
                def _():
                    pair_copy(step - 2 * PAIR_RING).wait_send()
                pbuf[step % PAIR_RING] = block.astype(BF16)
                pair_copy(step).start()

        i1 = step - LAG_PAIR

        @pl.when(owned(i1))
        def _():
            slot = slot_of(i1)
            pair_copy(i1).wait_recv()
            _accumulate(keep.at[slot], land.at[slot])
            for cond, u, n, chip, _ in pieces(block_of(i1)):
                @pl.when(cond & ((chip == ja) | (chip == jd)))
                def _(u=u, n=n):
                    _cast_rows(piece(keep, slot, u, n), piece(xbuf, slot, u, n))
                    h1_copy(slot, u, n).start()

        i2 = step - LAG_HOP1

        @pl.when(owned(i2))
        def _():
            slot = slot_of(i2)
            for cond, u, n, chip, local in pieces(block_of(i2)):
                @pl.when(cond & ((chip == j) | (chip == jb)))
                def _(u=u, n=n, chip=chip, local=local):
                    h1_copy(slot, u, n).wait_recv()
                    _accumulate(piece(keep, slot, u, n), piece(xbuf, slot, u, n))

                    @pl.when(chip == jb)
                    def _():
                        _cast_rows(piece(keep, slot, u, n), piece(xbuf, slot, u, n))
                        h2_copy(slot, u, n, local).start()

                @pl.when(cond & ((chip == ja) | (chip == jd)))
                def _(u=u, n=n):
                    h1_copy(slot, u, n).wait_send()

        i3 = step - LAG_HOP2

        @pl.when(owned(i3))
        def _():
            slot = slot_of(i3)
            for cond, u, n, chip, local in pieces(block_of(i3)):
                @pl.when(cond & (chip == j))
                def _(u=u, n=n, local=local):
                    h2_copy(slot, u, n, local).wait_recv()
                    _accumulate(piece(keep, slot, u, n), chip_rows_at(land2, local, n))
                    mine = pltpu.make_async_copy(piece(keep, slot, u, n), chip_rows_at(gwt_sh, local, n), out_sem.at[0])
                    mine.start()
                    sw_copy(slot, u, n, local).start()
                    mine.wait()

                @pl.when(cond & (chip == jb))
                def _(u=u, n=n, local=local):
                    h2_copy(slot, u, n, local).wait_send()

        e = step - n_blk

        @pl.when((e >= 0) & (e < n_tiles))
        def _():
            dh = dh_acc[pl.ds(pl.multiple_of(e * tile, tile), tile), :]
            xv = x_ref[...]
            r = _rstd(xv)
            xhat = xv * r
            gni[...] += jnp.sum(dh * xhat, axis=0, keepdims=True)
            gx_ref[...] = _rms_bwd(dh * g_ref[...], xhat, r) + dx2_ref[...]

        @pl.when(step == n_steps - 1)
        def _():
            small_land[me] = small_ref[...]
            small_land[me, SMALL_NORM_IN:SMALL_NORM_IN + 1, :] = gni[...]
            others = [(dx, dy, dc) for dx in (0, 1) for dy in (0, 1) for dc in (0, 1)][1:]
            sends = [remote(small_land.at[me], small_land.at[me], sm_send.at[k], sm_recv.at[k],
                            (_xor(x_i, dx), _xor(y_i, dy), _xor(c, dc))) for k, (dx, dy, dc) in enumerate(others)]
            for cp in sends:
                cp.start()
            for i in range(n_blk):
                if i + 2 * PAIR_RING >= n_blk:
                    @pl.when(owner_of(i) != c)
                    def _(i=i):
                        pair_copy(i).wait_send()
                for _, u, n, chip, local in pieces(block_of(i)):
                    @pl.when((j == chip) & (c == owner_of(i)))
                    def _(i=i, u=u, n=n, local=local):
                        sw_copy(slot_of(i), u, n, local).wait_send()

                    @pl.when((j == chip) & (c != owner_of(i)))
                    def _(i=i, u=u, n=n, local=local):
                        sw_copy(slot_of(i), u, n, local).wait_recv()
            for cp in sends:
                cp.wait_recv()
            total = small_land[0]
            for dev in range(1, 8):
                total = total + small_land[dev]
            small_sum[...] = total
            for cp in sends:
                cp.wait_send()

    def blk_at(i):
        return block_of(jnp.clip(i, 0, n_blk - 1))

    last_pc_step = max(i for i in range(n_blk) if block_of(i) < blk_q)

    def next_block(i, in_pc):
        i = jnp.clip(i, 0, n_blk - 1)
        step = jnp.full_like(i, last_pc_step if in_pc else n_blk - 1)
        for ahead in reversed(range(N_CHIPS)):
            cand = jnp.minimum(i + ahead, n_blk - 1)
            step = jnp.where((block_of(cand) < blk_q) == in_pc, cand, step)
        return block_of(step)

    def dqg_block(i):
        b = next_block(i, False)
        q_blk = jnp.clip(b - blk_q, 0, blk_kv - blk_q - 1)
        ga_blk = (D_ATTN // GBLK) + jnp.clip(b - blk_ga, 0, n_blk - blk_ga - 1)
        return jnp.where(b < blk_kv, q_blk, jnp.where(b == blk_kv, 2 * D_ATTN // GBLK, ga_blk))

    def tok(i):
        return (jnp.clip(i - n_blk, 0, n_tiles - 1), 0)

    n_piece = n_slots * n_sub
    dma = pltpu.SemaphoreType.DMA
    return pl.pallas_call(
        body, name="bwd_in", grid=(n_steps,),
        out_shape=(jax.ShapeDtypeStruct((seq, D_MODEL), F32), jax.ShapeDtypeStruct(small.shape, F32),
                   jax.ShapeDtypeStruct((chip_rows, D_MODEL), F32)),
        in_specs=[pl.BlockSpec((seq, GBLK), lambda i: (0, next_block(i, True))),
                  pl.BlockSpec((seq, GBLK), lambda i: (0, dqg_block(i))),
                  pl.BlockSpec((GBLK, D_MODEL), lambda i: (blk_at(i), 0)),
                  _resident(h.shape),
                  pl.BlockSpec((tile, D_MODEL), tok), _resident((1, D_MODEL)), pl.BlockSpec((tile, D_MODEL), tok),
                  _resident(small.shape)],
        out_specs=(pl.BlockSpec((tile, D_MODEL), tok), pl.BlockSpec(small.shape, lambda i: (0, 0)),
                   pl.BlockSpec(memory_space=pl.ANY)),
        scratch_shapes=[pltpu.VMEM((seq, D_MODEL), F32), pltpu.VMEM((1, D_MODEL), F32),
                        pltpu.VMEM((n_slots, GBLK, D_MODEL), F32), pltpu.VMEM((PAIR_RING, GBLK, D_MODEL), BF16),
                        pltpu.VMEM((n_slots, GBLK, D_MODEL), BF16), pltpu.VMEM((n_slots, GBLK, D_MODEL), BF16),
                        pltpu.VMEM((chip_rows, D_MODEL), BF16), pltpu.VMEM((8,) + small.shape, F32),
                        dma((n_slots,)), dma((n_slots,)), dma((n_piece,)), dma((n_piece,)), dma((n_piece,)),
                        dma((n_piece,)), dma((n_piece,)), dma((n_piece,)), dma((7,)), dma((7,)), dma((1,))],
        compiler_params=_params(62, ("arbitrary",)),
    )(dpc, dqg, wt, h, x, norm_in, dx2, small)


def _accumulate(dst_ref, src_ref, rows=16):
    def step(i, carry):
        sl = pl.ds(pl.multiple_of(i * rows, rows), rows)
        dst_ref[sl, :] = dst_ref[sl, :] + src_ref[sl, :].astype(F32)
        return carry
    lax.fori_loop(0, dst_ref.shape[0] // rows, step, 0)


def _rs_wout_scratch(gwo_shape):
    o_half, width = gwo_shape[0] // N_CHIPS // 2, gwo_shape[1]
    return [pltpu.VMEM((4, o_half, width), F32), pltpu.VMEM((4, o_half, width), BF16),
            pltpu.VMEM((4, o_half, width), BF16), pltpu.VMEM((2, o_half, width), BF16),
            pltpu.VMEM((o_half, width), BF16),
            pltpu.SemaphoreType.DMA((8,)), pltpu.SemaphoreType.DMA((8,)), pltpu.SemaphoreType.DMA((4,))]


def _rs_wout_stages(gwo_ref, gwo_sh, acc_o, sb_o, r1o, r2o, r3o, send_sems, recv_sems, local_sems):
    o_rows = gwo_ref.shape[0] // N_CHIPS
    o_half = o_rows // 2
    x, y, c = lax.axis_index("x"), lax.axis_index("y"), lax.axis_index("c")
    j = 2 * x + y
    pa = (_xor(x, 1 - c), _xor(y, c), c)
    pb = (_xor(x, c), _xor(y, 1 - c), c)
    sib = (x, y, 1 - c)
    ja = 2 * pa[0] + pa[1]
    jb = 2 * pb[0] + pb[1]
    jd = 3 - j
    order = (ja, jd, jb, j)
    sib_order = (jb, jd, ja, j)

    def rcopy(k, src, dst, to):
        return pltpu.make_async_remote_copy(src_ref=src, dst_ref=dst, send_sem=send_sems.at[k],
                                            recv_sem=recv_sems.at[k], device_id=to, device_id_type=MESH)

    def o_rows_of(chip, half):
        return gwo_ref.at[pl.ds(pl.multiple_of(chip * o_rows + half * o_half, 8), o_half), :]

    def load_all(chips, half):
        cps = [pltpu.make_async_copy(o_rows_of(chip, half), acc_o.at[s], local_sems.at[s]) for s, chip in enumerate(chips)]
        for cp in cps:
            cp.start()
        return cps

    def pair_send(s):
        return rcopy(s, sb_o.at[s], r1o.at[s], sib)

    def out_half(half):
        return gwo_sh.at[pl.ds(pl.multiple_of(half * o_half, 8), o_half), :]

    hop1 = [rcopy(4 + s, sb_o.at[s], r2o.at[s], pa) for s in range(2)]
    hop2 = rcopy(6, sb_o.at[2], r3o, pb)
    swap = rcopy(7, acc_o.at[3], out_half(c), sib)
    mine = pltpu.make_async_copy(acc_o.at[3], out_half(c), local_sems.at[0])

    def resend(s, copy):
        pair_send(s).wait_send()
        _cast_rows(acc_o.at[s], sb_o.at[s])
        copy.start()

    def stage_pair():
        for s, cp in enumerate(load_all(sib_order, 1 - c)):
            cp.wait()
            _cast_rows(acc_o.at[s], sb_o.at[s])
            pair_send(s).start()

    def stage_hop1():
        for s, cp in enumerate(load_all(order, c)):
            cp.wait()
            pair_send(s).wait_recv()
            _accumulate(acc_o.at[s], r1o.at[s])
            if s < 2:
                resend(s, hop1[s])

    def stage_hop2():
        hop1[1].wait_recv()
        _accumulate(acc_o.at[2], r2o.at[1])
        resend(2, hop2)
        hop1[0].wait_recv()
        _accumulate(acc_o.at[3], r2o.at[0])

    def stage_final():
        hop2.wait_recv()
        _accumulate(acc_o.at[3], r3o)
        mine.start()
        swap.start()

    def stage_drain():
        rcopy(7, acc_o.at[3], out_half(1 - c), sib).wait_recv()
        for cp in [pair_send(3)] + hop1 + [hop2, swap]:
            cp.wait_send()
        mine.wait()

    return stage_pair, stage_hop1, stage_hop2, stage_final, stage_drain


def _adamw_big(groups, passed):
    arrays = [a for group in groups for a in group[:4]]
    steps = [group[0].shape[0] // group[4] for group in groups]
    first = [sum(steps[:k]) for k in range(len(steps) + 1)]
    n = len(arrays)

    def body(*refs):
        ins, passed_in, outs, passed_out, sem = refs[:n], refs[n], refs[n + 1:2 * n + 1], refs[2 * n + 1], refs[2 * n + 2]
        i = pl.program_id(0)
        through = pltpu.make_async_copy(passed_in, passed_out, sem.at[0])
        pl.when(i == 0)(through.start)
        for k in range(len(groups)):
            @pl.when((i >= first[k]) & (i < first[k + 1]))
            def _(k=k):
                w_ref, g_ref, m_ref, v_ref = ins[4 * k:4 * k + 4]
                _adamw_update(w_ref, g_ref[...], m_ref, v_ref, *outs[4 * k:4 * k + 4])
        pl.when(i == first[-1] - 1)(through.wait)

    def spec(k, rows, width):
        return pl.BlockSpec((rows, width), lambda i: (jnp.clip(i - first[k], 0, steps[k] - 1), 0))

    any_spec = pl.BlockSpec(memory_space=pl.ANY)
    specs = [spec(k, group[4], group[0].shape[1]) for k, group in enumerate(groups) for _ in range(4)]
    out = pl.pallas_call(
        body, name="adamw_big", grid=(first[-1],),
        out_shape=tuple(jax.ShapeDtypeStruct(a.shape, F32) for a in arrays) + (jax.ShapeDtypeStruct(passed.shape, passed.dtype),),
        in_specs=specs + [any_spec], out_specs=tuple(specs) + (any_spec,),
        scratch_shapes=[pltpu.SemaphoreType.DMA((1,))],
        compiler_params=_params(32, ("arbitrary",)),
    )(*arrays, passed)
    return [tuple(out[4 * k:4 * k + 4]) for k in range(len(groups))], out[-1]


def _adamw_update(w_ref, gv, m_ref, v_ref, go_ref, d_ref, nm_ref, nv_ref, at=...):
    go_ref[at] = gv
    nm = ADAM_B1 * m_ref[at] + (1.0 - ADAM_B1) * gv
    nv = ADAM_B2 * v_ref[at] + (1.0 - ADAM_B2) * (gv * gv)
    m_hat = nm / (1.0 - ADAM_B1 ** ADAM_STEP)
    v_hat = nv / (1.0 - ADAM_B2 ** ADAM_STEP)
    d_ref[at] = -ADAM_LR * (m_hat / (jnp.sqrt(v_hat) + ADAM_EPS) + ADAM_WD * w_ref[at])
    nm_ref[at] = nm
    nv_ref[at] = nv


def _adamw_small(chip, small_sum, weights, grads_of, ms, vs):
    n = len(weights)

    def body(chip_ref, small_ref, *refs):
        ins, outs, loss_ref = refs[:3 * n], refs[3 * n:-1], refs[-1]
        for k in range(n):
            w_ref, m_ref, v_ref = ins[3 * k:3 * k + 3]
            for at, gv in grads_of[k](small_ref, chip_ref):
                _adamw_update(w_ref, gv, m_ref, v_ref, *outs[4 * k:4 * k + 4], at=at)
        loss_ref[...] = small_ref[SMALL_MISC:SMALL_MISC + 1, SMALL_LOSS_LANE:SMALL_LOSS_LANE + 1]

    flat = [a for group in zip(weights, ms, vs) for a in group]
    vmem = pl.BlockSpec(memory_space=pltpu.VMEM)
    out = pl.pallas_call(
        body, name="adamw_small",
        out_shape=tuple(jax.ShapeDtypeStruct(w.shape, F32) for w in weights for _ in range(4))
        + (jax.ShapeDtypeStruct((1, 1), F32),),
        in_specs=[pl.BlockSpec(memory_space=pltpu.SMEM)] + [vmem] * (1 + 3 * n), out_specs=(vmem,) * (4 * n + 1),
    )(chip, small_sum, *flat)
    return [tuple(out[4 * k:4 * k + 4]) for k in range(n)], out[-1][0, 0]


def kernel(x, norm_in, w_in, conv_w, attn_sinks, norm_conv_out, norm_attn_out, w_out, norm_final, loss_target, m_norm_in, m_w_in, m_conv_w, m_attn_sinks, m_norm_conv_out, m_norm_attn_out, m_w_out, m_norm_final, v_norm_in, v_w_in, v_conv_w, v_attn_sinks, v_norm_conv_out, v_norm_attn_out, v_w_out, v_norm_final):
    chip = 2 * lax.axis_index("x") + lax.axis_index("y")
    xs, target = x[0], loss_target[0]
    norm_final2 = norm_final.reshape(1, D_MODEL)
    w_in_t, m_w_in_t, v_w_in_t = w_in[0].T, m_w_in[0].T, v_w_in[0].T

    def from_hbm(*arrays):
        return tuple(pltpu.with_memory_space_constraint(a, pltpu.HBM) for a in arrays)

    h, pc, q, kv, ga, oc, ya, oa, probs, sink_probs, wt, cw, wo = _fwd_in(
        xs, norm_in, w_in_t, w_out[0], conv_w.transpose(1, 0, 2), norm_conv_out, attn_sinks, norm_attn_out)
    dx2, doa, gwo, dpc, small = _out_proj_loss(xs, oc, oa, wo, norm_final2, target, pc, cw, norm_conv_out)
    q, kv, ga, ya, doa, probs, gwo, small = from_hbm(q, kv, ga, ya, doa, probs, gwo, small)
    dqg, small, gwo_sh = _attn_bwd(q, kv, ga, ya, doa, probs, sink_probs, norm_attn_out, gwo, small)
    grad_x, small_sum, gwt_sh = _bwd_in(dpc, dqg, h, wt, xs, norm_in, dx2, small)

    (up_w_in, up_w_out), grad_x = _adamw_big(
        [from_hbm(w_in_t, gwt_sh, m_w_in_t, v_w_in_t) + (200,), from_hbm(w_out[0], gwo_sh, m_w_out[0], v_w_out[0]) + (128,)],
        *from_hbm(grad_x))
    up_w_in = tuple(o.T[None] for o in up_w_in)
    up_w_out = tuple(o[None] for o in up_w_out)

    def row_of(r):
        return lambda small_ref, chip_ref: [(..., small_ref[r:r + 1, :])]

    def sink_lanes(small_ref, chip_ref):
        return [(..., small_ref[SMALL_MISC:SMALL_MISC + 1, 0:N_HEADS])]

    def conv_taps(small_ref, chip_ref):
        width = D_CONV // N_CHIPS
        cols = pl.ds(pl.multiple_of(chip_ref[0] * width, width), width)
        return [(k, small_ref[pl.ds(SMALL_CONV_W + k, 1), cols]) for k in range(conv_w.shape[1])]

    def small_view(a):
        return a.transpose(1, 0, 2) if a.ndim == 3 else a.reshape(-1, a.shape[-1])

    small_w = (norm_in, conv_w, attn_sinks, norm_conv_out, norm_attn_out, norm_final)
    small_m = (m_norm_in, m_conv_w, m_attn_sinks, m_norm_conv_out, m_norm_attn_out, m_norm_final)
    small_v = (v_norm_in, v_conv_w, v_attn_sinks, v_norm_conv_out, v_norm_attn_out, v_norm_final)
    small_g = (row_of(SMALL_NORM_IN), conv_taps, sink_lanes, row_of(SMALL_NORM_CONV), row_of(SMALL_NORM_ATTN),
               row_of(SMALL_NORM_FINAL))
    w_views, m_views, v_views = (tuple(small_view(a) for a in group) for group in (small_w, small_m, small_v))
    up_small, loss = _adamw_small(chip.astype(jnp.int32).reshape(1), small_sum, w_views, small_g, m_views, v_views)
    up_small = [tuple(o.transpose(1, 0, 2) if w.ndim == 3 else o.reshape(w.shape) for o in up)
                for up, w in zip(up_small, small_w)]
    up_norm_in, up_conv_w, up_sinks, up_norm_conv, up_norm_attn, up_norm_final = up_small
    updates = (up_norm_in, up_w_in, up_conv_w, up_sinks, up_norm_conv, up_norm_attn, up_w_out, up_norm_final)
    grads_out, deltas, new_m, new_v = zip(*updates)
    return (loss, grad_x[None], *grads_out, *deltas, *new_m, *new_v)
```

```python
import jax
import jax.numpy as jnp
from jax import lax
from jax.experimental import pallas as pl
from jax.experimental.pallas import tpu as pltpu

F32 = jnp.float32
BF16 = jnp.bfloat16
MESH = pl.DeviceIdType.MESH

D_MODEL = 1024
D_CONV = 1024
D_ATTN = 1024
D_KV = 128
D_QG = 2 * D_ATTN + 2 * D_KV
D_MIX = D_CONV + D_ATTN
D_PC = 4 * D_CONV
D_IN_PROJ = D_PC + 2 * D_ATTN + 2 * D_KV
ROW_Q = D_PC
ROW_KV = ROW_Q + D_ATTN
ROW_GA = ROW_KV + 2 * D_KV
N_HEADS = 16
HEAD_DIM = 64
HEADS_PER_KV = 8
BLK = 128
N_CHIPS = 4
RMS_EPS = 1e-5
SCALE = HEAD_DIM ** -0.5
SLOPES = tuple(2.0 ** (-8.0 * (h + 1) / N_HEADS) for h in range(N_HEADS))

ADAM_LR, ADAM_B1, ADAM_B2, ADAM_EPS, ADAM_WD, ADAM_STEP = 0.001, 0.9, 0.999, 1e-08, 0.01, 10

SMALL_ROWS = 8
SMALL_NORM_IN, SMALL_NORM_CONV, SMALL_NORM_ATTN, SMALL_NORM_FINAL, SMALL_CONV_W, SMALL_MISC = 0, 1, 2, 3, 4, 7
SMALL_LOSS_LANE = N_HEADS

TOK_TILE = 256
PC_PIECE = 512
MIB = 1 << 20


def _params(vmem_mib, semantics=None):
    return pltpu.CompilerParams(dimension_semantics=semantics, vmem_limit_bytes=vmem_mib * MIB)


def _nn(a, b):
    return jnp.dot(a, b, preferred_element_type=F32)


def _nt(a, b):
    return lax.dot_general(a, b, (((1,), (1,)), ((), ())), preferred_element_type=F32)


def _tn(a, b):
    return lax.dot_general(a, b, (((0,), (0,)), ((), ())), preferred_element_type=F32)


def _rstd(v):
    return lax.rsqrt(jnp.mean(v * v, axis=-1, keepdims=True) + RMS_EPS)


def _rms_bwd(g, xhat, rstd):
    return rstd * (g - xhat * jnp.mean(g * xhat, axis=-1, keepdims=True))


def _silu_and_grad(g):
    s = jax.nn.sigmoid(g)
    return g * s, s * (1.0 + g * (1.0 - s))


def _resident(shape):
    return pl.BlockSpec(shape, lambda *_: (0,) * len(shape), pipeline_mode=pl.Buffered(1))


def _xor(a, b):
    return a + b - 2 * a * b


def _cast_rows(src_ref, dst_ref, rows=32):
    def step(i, carry):
        sl = pl.ds(pl.multiple_of(i * rows, rows), rows)
        dst_ref[sl, :] = src_ref[sl, :].astype(dst_ref.dtype)
        return carry
    lax.fori_loop(0, src_ref.shape[0] // rows, step, 0)


def _ag_scratch(shard_shape):
    rows, width = shard_shape
    return [pltpu.VMEM((rows, width), F32), pltpu.VMEM((rows, width), BF16), pltpu.VMEM((3, rows // 2, width), BF16),
            pltpu.SemaphoreType.DMA((6,)), pltpu.SemaphoreType.DMA((6,)), pltpu.SemaphoreType.DMA((4,))]


def _ag_stages(sh_ref, out, f32_buf, own, land, send_sems, recv_sems, local_sems):
    rows = sh_ref.shape[0]
    half = rows // 2
    x, y, c = lax.axis_index("x"), lax.axis_index("y"), lax.axis_index("c")
    j = 2 * x + y
    p1 = (_xor(x, c), _xor(y, 1 - c), c)
    p2 = (_xor(x, 1 - c), _xor(y, c), c)
    sib = (x, y, 1 - c)
    j1 = 2 * p1[0] + p1[1]
    j2 = 2 * p2[0] + p2[1]
    j3 = 3 - j

    def rows_of(chip, hf):
        return out.at[pl.ds(pl.multiple_of(chip * rows + hf * half, 16), half), :]

    def rcopy(k, src, dst, to):
        return pltpu.make_async_remote_copy(src_ref=src, dst_ref=dst, send_sem=send_sems.at[k],
                                            recv_sem=recv_sems.at[k], device_id=to, device_id_type=MESH)

    my_half = own.at[pl.ds(pl.multiple_of(c * half, 16), half), :]
    hop1 = rcopy(0, my_half, land.at[0], p1)
    hop2_own = rcopy(1, my_half, land.at[1], p2)
    hop2_fwd = rcopy(2, land.at[0], land.at[2], p2)
    swaps = [rcopy(3 + s, land.at[s], rows_of(chip, c), sib) for s, chip in enumerate((j1, j2, j3))]
    keeps = [pltpu.make_async_copy(land.at[s], rows_of(chip, c), local_sems.at[1 + s]) for s, chip in enumerate((j1, j2, j3))]
    load = pltpu.make_async_copy(sh_ref, f32_buf, local_sems.at[0])
    own_out = pltpu.make_async_copy(own, out.at[pl.ds(pl.multiple_of(j * rows, 16), rows), :], local_sems.at[0])

    def stage_send():
        load.start()
        load.wait()
        _cast_rows(f32_buf, own)
        own_out.start()
        hop1.start()

    def stage_forward():
        hop1.wait_recv()
        hop2_own.start()
        hop2_fwd.start()
        swaps[0].start()
        keeps[0].start()

    def stage_publish():
        hop2_own.wait_recv()
        swaps[1].start()
        keeps[1].start()
        hop2_fwd.wait_recv()
        swaps[2].start()
        keeps[2].start()

    def stage_drain():
        for s, chip in enumerate((j2, j1, j3)):
            rcopy(3 + s, my_half, rows_of(chip, 1 - c), sib).wait_recv()
        for cp in [hop1, hop2_own, hop2_fwd] + swaps:
            cp.wait_send()
        for cp in [own_out] + keeps:
            cp.wait()

    return stage_send, stage_forward, stage_publish, stage_drain


AG_CAST_ROWS = 400


def _gather_resident(sh_ref, out, f32_buf, send_sems, recv_sems, local_sems):
    rows = sh_ref.shape[0]
    half = rows // 2
    x, y, c = lax.axis_index("x"), lax.axis_index("y"), lax.axis_index("c")
    j = 2 * x + y
    p1 = (_xor(x, c), _xor(y, 1 - c), c)
    p2 = (_xor(x, 1 - c), _xor(y, c), c)
    sib = (x, y, 1 - c)
    j1 = 2 * p1[0] + p1[1]
    j2 = 2 * p2[0] + p2[1]
    j3 = 3 - j

    def rows_of(chip, hf):
        return out.at[pl.ds(pl.multiple_of(chip * rows + hf * half, 16), half), :]

    def send(k, chip, to):
        return pltpu.make_async_remote_copy(src_ref=rows_of(chip, c), dst_ref=rows_of(chip, c), send_sem=send_sems.at[k],
                                            recv_sem=recv_sems.at[k], device_id=to, device_id_type=MESH)

    per_half = half // AG_CAST_ROWS

    def cast_own(chunk):
        lo = pl.multiple_of(chunk * AG_CAST_ROWS, 16)
        load = pltpu.make_async_copy(sh_ref.at[pl.ds(lo, AG_CAST_ROWS), :], f32_buf, local_sems.at[0])
        load.start()
        load.wait()
        _cast_rows(f32_buf, out.at[pl.ds(pl.multiple_of(j * rows + lo, 16), AG_CAST_ROWS), :], rows=16)

    for k in range(per_half):
        cast_own(c * per_half + k)
    hop1 = send(0, j, p1)
    hop1.start()
    for k in range(per_half):
        cast_own((1 - c) * per_half + k)
    hop1.wait_recv()
    sends = [hop1, send(1, j, p2), send(2, j1, p2), send(3, j1, sib)]
    for cp in sends[1:]:
        cp.start()
    sends[1].wait_recv()
    sends.append(send(4, j2, sib))
    sends[-1].start()
    sends[2].wait_recv()
    sends.append(send(5, j3, sib))
    sends[-1].start()
    for k in (3, 4, 5):
        send(k, j, sib).wait_recv()
    for cp in sends:
        cp.wait_send()


def _fwd_in(x, norm_in, wt_sh, wo_sh, cw_sh, norm_conv_out, sinks, norm_attn_out):
    seq = x.shape[0]
    tile = TOK_TILE
    n_tiles = seq // tile
    stage_steps = (0, n_tiles // 4, (5 * n_tiles) // 8, n_tiles - 1)
    blocks = tile // BLK
    cw_cols = cw_sh.shape[-1]

    def body(x_ref, g_ref, wtsh_ref, wo_ref, cwsh_ref, gn_ref, sink_ref, gna_ref,
             h_ref, pc_ref, q_ref, kv_ref, ga_ref, oc_ref, ya_ref, oa_ref, pr_ref, sp_ref, wt_out, cw_ref, wo_out,
             zbuf, kv_last, wt_ref, f32_buf, cw_land, wt_send, wt_recv, wt_local, cw_send, cw_recv, cw_local, *ag_scratch):
        step = pl.program_id(0)
        to_hbm = pltpu.make_async_copy(wt_ref, wt_out, wt_local.at[0])

        @pl.when(step == 0)
        def _():
            zbuf[0:8, :] = jnp.zeros((8, D_CONV), F32)
            kv_last[...] = jnp.zeros_like(kv_last)
            x_i, y_i, c = lax.axis_index("x"), lax.axis_index("y"), lax.axis_index("c")
            j = 2 * x_i + y_i
            p1 = (_xor(x_i, c), _xor(y_i, 1 - c), c)
            p2 = (_xor(x_i, 1 - c), _xor(y_i, c), c)
            j1 = 2 * p1[0] + p1[1]

            def cw_copy(k, src, chip, to):
                return pltpu.make_async_remote_copy(src_ref=src, dst_ref=cw_land.at[chip], send_sem=cw_send.at[k],
                                                    recv_sem=cw_recv.at[k], device_id=to, device_id_type=MESH)

            mine = pltpu.make_async_copy(cwsh_ref, cw_land.at[j], cw_local.at[0])
            mine.start()
            first = cw_copy(0, cwsh_ref, j, p1)
            first.start()
            _gather_resident(wtsh_ref, wt_ref, f32_buf, wt_send, wt_recv, wt_local)
            to_hbm.start()
            first.wait_recv()
            second = [cw_copy(1, cwsh_ref, j, p2), cw_copy(2, cw_land.at[j1], j1, p2)]
            for cp in second:
                cp.start()
            for cp in second:
                cp.wait_recv()
            for cp in [first] + second:
                cp.wait_send()
            mine.wait()
            for chip in range(N_CHIPS):
                for tap in range(cw_sh.shape[0]):
                    cw_ref[tap:tap + 1, chip * cw_cols:(chip + 1) * cw_cols] = cw_land[chip, tap]

        stages = _ag_stages(wo_ref, wo_out, *ag_scratch)
        for at, stage in zip(stage_steps[:-1], stages[:-1]):
            pl.when(step == at)(stage)

        def attention(b):
            rows = pl.ds(b * BLK, BLK)
            kv_prev = kv_last if b == 0 else kv_ref.at[pl.ds((b - 1) * BLK, BLK), :]
            return _attn_forward(q_ref.at[rows, :], kv_ref.at[rows, :], kv_prev, ga_ref.at[rows, :], sink_ref, gna_ref,
                                 _band_geometry(step * blocks + b), ya_ref.at[rows, :], oa_ref.at[rows, :],
                                 pr_ref.at[rows, :], sp_ref.at[rows, :])

        xv = x_ref[...]
        h = (xv * _rstd(xv) * g_ref[...]).astype(BF16)
        h_ref[...] = h
        q_ref[...] = _nt(h, wt_ref[ROW_Q:ROW_KV, :])
        kv_ref[...] = _nt(h, wt_ref[ROW_KV:ROW_GA, :])
        ga_ref[...] = _nt(h, wt_ref[ROW_GA:D_IN_PROJ, :])
        attention_blocks = [attention(b) for b in range(blocks)]
        for lo in range(0, D_PC, PC_PIECE):
            pc_ref[:, lo:lo + PC_PIECE] = _nt(h, wt_ref[lo:lo + PC_PIECE, :])
            for stages_of_block in attention_blocks:
                next(stages_of_block, None)
        for stages_of_block in attention_blocks:
            for _ in stages_of_block:
                pass
        kv_last[...] = kv_ref[tile - BLK:tile, :]

        cb, _, _, _, _, _, conv = _conv_core(pc_ref, zbuf, cw_ref)
        yc = cb * conv
        silu, _ = _silu_and_grad(pc_ref[:, 3 * D_CONV:4 * D_CONV])
        oc_ref[...] = (yc * _rstd(yc) * gn_ref[...] * silu).astype(BF16)
        zbuf[0:8, :] = zbuf[tile:tile + 8, :]

        @pl.when(step == stage_steps[-1])
        def _():
            stages[-1]()
            to_hbm.wait()

    def row(width):
        return pl.BlockSpec((tile, width), lambda i: (i, 0))

    any_spec = pl.BlockSpec(memory_space=pl.ANY)
    dma = pltpu.SemaphoreType.DMA
    wt_shape = (N_CHIPS * wt_sh.shape[0], wt_sh.shape[1])
    cw_shape = (cw_sh.shape[0], N_CHIPS * cw_cols)
    return pl.pallas_call(
        body, name="fwd_in", grid=(n_tiles,),
        out_shape=(jax.ShapeDtypeStruct((seq, D_MODEL), BF16), jax.ShapeDtypeStruct((seq, D_PC), F32),
                   jax.ShapeDtypeStruct((seq, D_ATTN), F32), jax.ShapeDtypeStruct((seq, 2 * D_KV), F32),
                   jax.ShapeDtypeStruct((seq, D_ATTN), F32), jax.ShapeDtypeStruct((seq, D_CONV), BF16),
                   pltpu.HBM((seq, D_ATTN), F32), pltpu.HBM((seq, D_ATTN), BF16),
                   pltpu.HBM((seq, N_HEADS * BLK), BF16), pltpu.HBM((seq, 128), F32),
                   pltpu.HBM(wt_shape, BF16), jax.ShapeDtypeStruct(cw_shape, F32),
                   jax.ShapeDtypeStruct((N_CHIPS * wo_sh.shape[0], wo_sh.shape[1]), BF16)),
        in_specs=[row(D_MODEL), _resident((1, D_MODEL)), any_spec, any_spec, any_spec, _resident((1, D_CONV)),
                  pl.BlockSpec(memory_space=pltpu.SMEM), _resident((1, D_ATTN))],
        out_specs=(row(D_MODEL), row(D_PC), row(D_ATTN), row(2 * D_KV), row(D_ATTN), row(D_CONV), row(D_ATTN),
                   row(D_ATTN), row(N_HEADS * BLK), row(128), any_spec, pl.BlockSpec(cw_shape, lambda i: (0, 0)),
                   any_spec),
        scratch_shapes=[pltpu.VMEM((tile + 8, D_CONV), F32), pltpu.VMEM((BLK, 2 * D_KV), F32),
                        pltpu.VMEM(wt_shape, BF16), pltpu.VMEM((AG_CAST_ROWS, wt_sh.shape[1]), F32),
                        pltpu.VMEM((N_CHIPS,) + cw_sh.shape, F32),
                        dma((6,)), dma((6,)), dma((1,)), dma((3,)), dma((3,)), dma((1,))] + _ag_scratch(wo_sh.shape),
        compiler_params=_params(62, ("arbitrary",)),
    )(x, norm_in, wt_sh, wo_sh, cw_sh, norm_conv_out, sinks, norm_attn_out)


def _conv_core(pc_ref, zbuf, cw_ref):
    tile = pc_ref.shape[0]
    cb = pc_ref[:, 0:D_CONV]
    cc = pc_ref[:, D_CONV:2 * D_CONV]
    cu = pc_ref[:, 2 * D_CONV:3 * D_CONV]
    z = cc * cu
    zbuf[8:tile + 8, :] = z
    z1 = zbuf[7:tile + 7, :]
    z2 = zbuf[6:tile + 6, :]
    conv = cw_ref[0:1, :] * z2 + cw_ref[1:2, :] * z1 + cw_ref[2:3, :] * z
    return cb, cc, cu, z, z1, z2, conv


def _band_geometry(block_index):
    qi = lax.broadcasted_iota(jnp.int32, (BLK, BLK), 0)
    kp = lax.broadcasted_iota(jnp.int32, (BLK, BLK), 1)
    use_cur = kp <= qi
    dist = jnp.where(use_cur, qi - kp, qi - kp + BLK).astype(F32)
    valid = use_cur | (block_index > 0)
    return use_cur, dist, valid


def _block_diag(cur, prev, group):
    lane = lax.broadcasted_iota(jnp.int32, cur.shape, 1)

    def halves(t):
        other = pltpu.roll(t, 64, 1)
        lo, hi = (t, other) if group == 0 else (other, t)
        return jnp.where(lane < 64, lo, 0.0), jnp.where(lane >= 64, hi, 0.0)

    return jnp.concatenate(halves(cur) + halves(prev), axis=0).astype(BF16)


def _merge(s4, use_cur):
    return (jnp.where(use_cur, s4[:, 0:BLK], s4[:, 2 * BLK:3 * BLK]),
            jnp.where(use_cur, s4[:, BLK:2 * BLK], s4[:, 3 * BLK:4 * BLK]))


def _split(a, b, use_cur):
    return jnp.concatenate([jnp.where(use_cur, a, 0.0), jnp.where(use_cur, b, 0.0),
                            jnp.where(use_cur, 0.0, a), jnp.where(use_cur, 0.0, b)], axis=1)


def _softmax_head(s, head, sink, dist, valid):
    sc = jnp.where(valid, s - SLOPES[head] * dist, -jnp.inf)
    m = jnp.maximum(jnp.max(sc, axis=-1, keepdims=True), sink)
    p = jnp.exp(sc - m)
    es = jnp.exp(sink - m)
    inv = 1.0 / (jnp.sum(p, axis=-1, keepdims=True) + es)
    return p * inv, es * inv


def _attn_operands(q_ref, kvc_ref, kvp_ref):
    groups = range(N_HEADS // HEADS_PER_KV)
    kbd = [_block_diag(kvc_ref[:, 0:D_KV], kvp_ref[:, 0:D_KV], g) for g in groups]
    vbd = [_block_diag(kvc_ref[:, D_KV:2 * D_KV], kvp_ref[:, D_KV:2 * D_KV], g) for g in groups]
    qps = [(q_ref[:, j * 128:(j + 1) * 128] * SCALE).astype(BF16) for j in range(N_HEADS // 2)]
    return qps, kbd, vbd


def _attn_forward(q_ref, kvc_ref, kvp_ref, ga_ref, sink_ref, gn_ref, geometry, ya_ref, oa_ref, pr_ref, sp_ref):
    use_cur, dist, valid = geometry
    _, kbd, vbd = operands = _attn_operands(q_ref, kvc_ref, kvp_ref)
    yield
    scores = []
    for j, qp in enumerate(operands[0]):
        scores += _merge(_nt(qp, kbd[j // 4]), use_cur)
    yield
    sinks = [sink_ref[0, h] for h in range(N_HEADS)]
    scores = [jnp.where(valid, s - SLOPES[h] * dist, -jnp.inf) for h, s in enumerate(scores)]
    maxes = [jnp.maximum(jnp.max(s, axis=-1, keepdims=True), sinks[h]) for h, s in enumerate(scores)]
    yield
    exps = [jnp.exp(s - m) for s, m in zip(scores, maxes)]
    sink_exps = [jnp.exp(sinks[h] - m) for h, m in enumerate(maxes)]
    yield
    invs = [1.0 / (jnp.sum(e, axis=-1, keepdims=True) + se) for e, se in zip(exps, sink_exps)]
    probs = [e * inv for e, inv in zip(exps, invs)]
    pr_ref[...] = jnp.concatenate(probs, axis=1).astype(BF16)
    lane = lax.broadcasted_iota(jnp.int32, (BLK, 128), 1)
    sp_ref[...] = sum(jnp.where(lane == h, se * inv, 0.0) for h, (se, inv) in enumerate(zip(sink_exps, invs)))
    yield
    p4s = [_split(probs[2 * j], probs[2 * j + 1], use_cur).astype(BF16) for j in range(N_HEADS // 2)]
    ya = jnp.concatenate([_nn(p4, vbd[j // 4]) for j, p4 in enumerate(p4s)], axis=1)
    ya_ref[...] = ya
    yield
    silu, _ = _silu_and_grad(ga_ref[...])
    oa_ref[...] = (ya * _rstd(ya) * gn_ref[...] * silu).astype(BF16)


def _kv_specs(n_blocks, reverse):
    def blk(i):
        return (n_blocks - 1 - i) if reverse else i
    cur = pl.BlockSpec((BLK, 2 * D_KV), lambda i: (blk(i), 0))
    prev = pl.BlockSpec((BLK, 2 * D_KV), lambda i: (jnp.maximum(blk(i) - 1, 0), 0))
    return cur, prev


def _out_proj_loss(x, oc, oa, wo, norm_final, target, pc, conv_w, norm_conv_out):
    seq = x.shape[0]
    tile = TOK_TILE
    n_tiles = seq // tile

    def body(x_ref, oc_ref, oa_ref, wo_ref, gf_ref, t_ref, pc_ref, hcc_ref, hcu_ref, cw_ref, gn_ref,
             dx2_ref, doa_ref, gwo_ref, dpc_ref, small_ref, loss_acc, zbuf, dbuf):
        step = pl.program_id(0)

        def small_add(row, value):
            small_ref[row:row + 1, :] += jnp.sum(value, axis=0, keepdims=True)

        @pl.when(step == 0)
        def _():
            gwo_ref[...] = jnp.zeros_like(gwo_ref)
            small_ref[...] = jnp.zeros_like(small_ref)
            loss_acc[...] = jnp.zeros_like(loss_acc)
            dbuf[tile:tile + 8, :] = jnp.zeros((8, D_CONV), F32)

        oc, oa = oc_ref[...], oa_ref[...]
        x2 = x_ref[...] + _nn(oc, wo_ref[0:D_CONV, :]) + _nn(oa, wo_ref[D_CONV:D_MIX, :])
        r = _rstd(x2)
        xhat = x2 * r
        err = xhat * gf_ref[...] - t_ref[...]
        loss_acc[...] += jnp.sum(err * err, axis=0, keepdims=True) * (0.5 / D_MODEL)
        dy = err * (1.0 / D_MODEL)
        small_add(SMALL_NORM_FINAL, dy * xhat)
        dx2 = _rms_bwd(dy * gf_ref[...], xhat, r)
        dx2_ref[...] = dx2
        db = dx2.astype(BF16)
        do = _nt(db, wo_ref[0:D_CONV, :])
        doa_ref[...] = _nt(db, wo_ref[D_CONV:D_MIX, :])
        gwo_ref[0:D_CONV, :] += _tn(oc, db)
        gwo_ref[D_CONV:D_MIX, :] += _tn(oa, db)

        is_first_tile = step == n_tiles - 1
        zbuf[0:8, :] = jnp.where(is_first_tile, 0.0, hcc_ref[...] * hcu_ref[...])
        cb, cc, cu, z, z1, z2, conv = _conv_core(pc_ref, zbuf, cw_ref)
        silu, dsilu = _silu_and_grad(pc_ref[:, 3 * D_CONV:4 * D_CONV])
        yc = cb * conv
        rc = _rstd(yc)
        chat = yc * rc
        dn = do * silu
        dpc_ref[:, 3 * D_CONV:4 * D_CONV] = (do * (chat * gn_ref[...]) * dsilu).astype(BF16)
        small_add(SMALL_NORM_CONV, dn * chat)
        dyc = _rms_bwd(dn * gn_ref[...], chat, rc)
        dpc_ref[:, 0:D_CONV] = (dyc * conv).astype(BF16)
        dconv = dyc * cb
        small_add(SMALL_CONV_W, dconv * z2)
        small_add(SMALL_CONV_W + 1, dconv * z1)
        small_add(SMALL_CONV_W + 2, dconv * z)
        dbuf[0:tile, :] = dconv
        dz = cw_ref[2:3, :] * dconv + cw_ref[1:2, :] * dbuf[1:tile + 1, :] + cw_ref[0:1, :] * dbuf[2:tile + 2, :]
        dpc_ref[:, D_CONV:2 * D_CONV] = (dz * cu).astype(BF16)
        dpc_ref[:, 2 * D_CONV:3 * D_CONV] = (dz * cc).astype(BF16)
        dbuf[tile:tile + 8, :] = dbuf[0:8, :]

        @pl.when(step == n_tiles - 1)
        def _():
            lane = lax.broadcasted_iota(jnp.int32, (1, D_MODEL), 1)
            small_ref[SMALL_MISC:SMALL_MISC + 1, :] = jnp.where(lane == SMALL_LOSS_LANE, jnp.sum(loss_acc[...]), 0.0)

    def rev(i):
        return n_tiles - 1 - i

    def row(width):
        return pl.BlockSpec((tile, width), lambda i: (rev(i), 0))

    def halo(col_block):
        return pl.BlockSpec((8, D_CONV), lambda i: (jnp.maximum(rev(i) * (tile // 8) - 1, 0), col_block))

    def const(shape):
        return pl.BlockSpec(shape, lambda i: (0, 0))

    return pl.pallas_call(
        body, name="out_proj_loss", grid=(n_tiles,),
        out_shape=(jax.ShapeDtypeStruct((seq, D_MODEL), F32), jax.ShapeDtypeStruct((seq, D_ATTN), F32),
                   jax.ShapeDtypeStruct((D_MIX, D_MODEL), F32), jax.ShapeDtypeStruct((seq, D_PC), BF16),
                   jax.ShapeDtypeStruct((SMALL_ROWS, D_MODEL), F32)),
        in_specs=[row(D_MODEL), row(D_CONV), row(D_ATTN), _resident(wo.shape), _resident((1, D_MODEL)), row(D_MODEL),
                  row(D_PC), halo(1), halo(2), _resident(conv_w.shape), _resident((1, D_CONV))],
        out_specs=(row(D_MODEL), row(D_ATTN), const((D_MIX, D_MODEL)), row(D_PC), const((SMALL_ROWS, D_MODEL))),
        scratch_shapes=[pltpu.VMEM((1, D_MODEL), F32), pltpu.VMEM((tile + 8, D_CONV), F32),
                        pltpu.VMEM((tile + 8, D_CONV), F32)],
        compiler_params=_params(62, ("arbitrary",)),
    )(x, oc, oa, wo, norm_final, target, pc, pc, pc, conv_w, norm_conv_out)


def _attn_bwd(q, kv, ga, ya, doa, probs, sink_probs, norm_attn_out, gwo, small):
    seq = q.shape[0]
    n_blocks = seq // BLK
    stage_steps = (0, n_blocks // 4, n_blocks // 2, (3 * n_blocks) // 4, n_blocks - 1)

    def body(q_ref, kvc_ref, kvp_ref, ga_ref, ya_ref, doa_ref, pr_ref, sp_ref, gn_ref, gwo_ref, small_ref,
             dqg_ref, small_out, gwo_sh, gna_ref, gs_ref, carry, dya_buf, *rs_scratch):
        step = pl.program_id(0)

        @pl.when(step == 0)
        def _():
            gna_ref[...] = jnp.zeros_like(gna_ref)
            gs_ref[...] = jnp.zeros_like(gs_ref)
            carry[...] = jnp.zeros_like(carry)

        ya = ya_ref[...]
        r = _rstd(ya)
        xhat = ya * r
        silu, dsilu = _silu_and_grad(ga_ref[...])
        do = doa_ref[...]
        dn = do * silu
        dqg_ref[:, D_ATTN:2 * D_ATTN] = (do * (xhat * gn_ref[...]) * dsilu).astype(BF16)
        gna_ref[...] += jnp.sum(dn * xhat, axis=0, keepdims=True)
        dya_buf[...] = _rms_bwd(dn * gn_ref[...], xhat, r).astype(BF16)

        use_cur = _band_geometry(n_blocks - 1 - step)[0]
        lane = lax.broadcasted_iota(jnp.int32, (BLK, 128), 1)

        def fold(bd):
            return (jnp.where(lane < 64, bd[0:BLK], 0.0) + jnp.where(lane >= 64, bd[BLK:2 * BLK], 0.0),
                    jnp.where(lane < 64, bd[2 * BLK:3 * BLK], 0.0) + jnp.where(lane >= 64, bd[3 * BLK:4 * BLK], 0.0))

        pairs = range(N_HEADS // 2)
        qps, kbd, vbd = _attn_operands(q_ref, kvc_ref, kvp_ref)
        probs = [pr_ref[:, h * BLK:(h + 1) * BLK].astype(F32) for h in range(N_HEADS)]
        dyps = [dya_buf[:, j * 128:(j + 1) * 128] for j in pairs]
        dps = []
        for j in pairs:
            dps += _merge(_nt(dyps[j], vbd[j // 4]), use_cur)
        deltas = [jnp.sum(p * dp, axis=-1, keepdims=True) for p, dp in zip(probs, dps)]
        dss = [p * (dp - delta) for p, dp, delta in zip(probs, dps, deltas)]
        delta_lanes = sum(jnp.where(lane == h, deltas[h], 0.0) for h in range(N_HEADS))
        gs_ref[...] -= jnp.sum(sp_ref[...] * delta_lanes, axis=0, keepdims=True)
        ds4s = [_split(dss[2 * j], dss[2 * j + 1], use_cur).astype(BF16) for j in pairs]
        p4s = [_split(probs[2 * j], probs[2 * j + 1], use_cur).astype(BF16) for j in pairs]
        dqg_ref[:, 0:D_ATTN] = jnp.concatenate([_nn(ds4s[j], kbd[j // 4]) * SCALE for j in pairs], axis=1).astype(BF16)
        sums = []
        for group in range(N_HEADS // HEADS_PER_KV):
            acc = [jnp.zeros((BLK, 128), F32) for _ in range(4)]
            for j in range(group * 4, group * 4 + 4):
                for slot, part in enumerate(fold(_tn(ds4s[j], qps[j])) + fold(_tn(p4s[j], dyps[j]))):
                    acc[slot] = acc[slot] + part
            sums.append([a + pltpu.roll(a, 64, 1) for a in acc])
        dk_cur, dk_prev, dv_cur, dv_prev = (jnp.where(lane < 64, a, b) for a, b in zip(sums[0], sums[1]))
        dqg_ref[:, 2 * D_ATTN:2 * D_ATTN + 2 * D_KV] = (jnp.concatenate([dk_cur, dv_cur], axis=1) + carry[...]).astype(BF16)
        carry[...] = jnp.concatenate([dk_prev, dv_prev], axis=1)

        @pl.when(step == n_blocks - 1)
        def _():
            small_out[...] = small_ref[...]
            small_out[SMALL_NORM_ATTN:SMALL_NORM_ATTN + 1, :] = gna_ref[...]
            small_out[SMALL_MISC:SMALL_MISC + 1, 0:128] = small_ref[SMALL_MISC:SMALL_MISC + 1, 0:128] + gs_ref[...]

        for at, stage in zip(stage_steps, _rs_wout_stages(gwo_ref, gwo_sh, *rs_scratch)):
            pl.when(step == at)(stage)

    row = pl.BlockSpec((BLK, D_ATTN), lambda i: (n_blocks - 1 - i, 0))
    kv_cur, kv_prev = _kv_specs(n_blocks, reverse=True)
    any_spec = pl.BlockSpec(memory_space=pl.ANY)
    return pl.pallas_call(
        body, name="attn_bwd", grid=(n_blocks,),
        out_shape=(jax.ShapeDtypeStruct((seq, D_QG), BF16), pltpu.HBM(small.shape, F32),
                   pltpu.HBM((gwo.shape[0] // N_CHIPS, gwo.shape[1]), F32)),
        in_specs=[row, kv_cur, kv_prev, row, row, row,
                  pl.BlockSpec((BLK, N_HEADS * BLK), lambda i: (n_blocks - 1 - i, 0)),
                  pl.BlockSpec((BLK, 128), lambda i: (n_blocks - 1 - i, 0)), _resident((1, D_ATTN)),
                  any_spec, _resident(small.shape)],
        out_specs=(pl.BlockSpec((BLK, D_QG), lambda i: (n_blocks - 1 - i, 0)),
                   pl.BlockSpec(small.shape, lambda i: (0, 0)), any_spec),
        scratch_shapes=[pltpu.VMEM((1, D_ATTN), F32), pltpu.VMEM((1, 128), F32),
                        pltpu.VMEM((BLK, 2 * D_KV), F32), pltpu.VMEM((BLK, D_ATTN), BF16)] + _rs_wout_scratch(gwo.shape),
        compiler_params=_params(44, ("arbitrary",)),
    )(q, kv, kv, ga, ya, doa, probs, sink_probs, norm_attn_out, gwo, small)


GBLK = 256
GSUB = 64
PAIR_RING = 4
LAG_PAIR, LAG_HOP1, LAG_HOP2 = 1, 7, 14


def _bwd_in(dpc, dqg, h, wt, x, norm_in, dx2, small):
    seq = x.shape[0]
    n_blk = D_IN_PROJ // GBLK
    per_chip = n_blk // N_CHIPS
    n_slots = (n_blk + 1) // 2
    n_sub = GBLK // GSUB
    chip_rows = D_IN_PROJ // N_CHIPS
    tile = TOK_TILE
    n_tiles = seq // tile
    n_steps = n_blk + max(n_tiles, LAG_HOP2)
    chunk = min(seq, 512)
    blk_q, blk_kv, blk_ga = ROW_Q // GBLK, ROW_KV // GBLK, ROW_GA // GBLK

    def block_of(i):
        k = i % N_CHIPS
        robin = per_chip * ((k % 2) * 2 + k // 2) + i // N_CHIPS
        if isinstance(i, int):
            return robin if i < per_chip * N_CHIPS else i
        return jnp.where(i < per_chip * N_CHIPS, robin, i)

    def owner_of(i):
        return (i // N_CHIPS) % 2

    def slot_of(i):
        return (i // (2 * N_CHIPS)) * N_CHIPS + i % N_CHIPS

    def body(dpc_ref, dqg_ref, wt_ref, h_ref, x_ref, g_ref, dx2_ref, small_ref, gx_ref, small_sum, gwt_sh,
             dh_acc, gni, keep, pbuf, xbuf, land, land2, small_land,
             pair_send, pair_recv, h1_send, h1_recv, h2_send, h2_recv, sw_send, sw_recv, sm_send, sm_recv, out_sem):
        step = pl.program_id(0)
        x_i, y_i, c = lax.axis_index("x"), lax.axis_index("y"), lax.axis_index("c")
        me = 4 * x_i + 2 * y_i + c
        j = 2 * x_i + y_i
        pa = (_xor(x_i, 1 - c), _xor(y_i, c), c)
        pb = (_xor(x_i, c), _xor(y_i, 1 - c), c)
        sib = (x_i, y_i, 1 - c)
        ja = 2 * pa[0] + pa[1]
        jb = 2 * pb[0] + pb[1]
        jd = 3 - j

        def remote(src, dst, send, recv, to):
            return pltpu.make_async_remote_copy(src_ref=src, dst_ref=dst, send_sem=send, recv_sem=recv,
                                                device_id=to, device_id_type=MESH)

        def piece(ref, slot, u, n):
            return ref.at[slot, pl.ds(u * GSUB, n * GSUB), :]

        def chip_rows_at(ref, local, n):
            return ref.at[pl.ds(pl.multiple_of(local, GSUB), n * GSUB), :]

        def pair_copy(i):
            slot = slot_of(i)
            return remote(pbuf.at[i % PAIR_RING], land.at[slot], pair_send.at[slot], pair_recv.at[slot], sib)

        def h1_copy(slot, u, n):
            k = slot * n_sub + u
            return remote(piece(xbuf, slot, u, n), piece(xbuf, slot, u, n), h1_send.at[k], h1_recv.at[k], pa)

        def h2_copy(slot, u, n, local):
            k = slot * n_sub + u
            return remote(piece(xbuf, slot, u, n), chip_rows_at(land2, local, n), h2_send.at[k], h2_recv.at[k], pb)

        def sw_copy(slot, u, n, local):
            k = slot * n_sub + u
            return remote(piece(keep, slot, u, n), chip_rows_at(gwt_sh, local, n), sw_send.at[k], sw_recv.at[k], sib)

        def owned(i):
            return (i >= 0) & (i < n_blk) & (owner_of(i) == c)

        def chip_of(blk, u):
            row = blk * GBLK + u * GSUB
            chip = row // chip_rows
            return chip, row - chip * chip_rows

        def pieces(blk):
            first, local = chip_of(blk, 0)
            whole = first == chip_of(blk, n_sub - 1)[0]
            if isinstance(blk, int):
                return [(True, 0, n_sub, first, local)] if whole else [(True, u, 1) + chip_of(blk, u) for u in range(n_sub)]
            return [(whole, 0, n_sub, first, local)] + [(jnp.logical_not(whole), u, 1) + chip_of(blk, u) for u in range(n_sub)]

        @pl.when(step == 0)
        def _():
            dh_acc[...] = jnp.zeros_like(dh_acc)
            gni[...] = jnp.zeros_like(gni)

        @pl.when(step < n_blk)
        def _():
            from_pc = block_of(step) < blk_q
            block = _tn(jnp.where(from_pc, dpc_ref[...], dqg_ref[...]), h_ref[...])
            for t in range(0, seq, chunk):
                d = jnp.where(from_pc, dpc_ref[t:t + chunk, :], dqg_ref[t:t + chunk, :])
                dh_acc[t:t + chunk, :] += _nn(d, wt_ref[...])

            @pl.when(owner_of(step) == c)
            def _():
                keep[slot_of(step)] = block

            @pl.when(owner_of(step) != c)
            def _():
                @pl.when(step >= 2 * PAIR_RING)
                def _():
                    pair_copy(step - 2 * PAIR_RING).wait_send()
                pbuf[step % PAIR_RING] = block.astype(BF16)
                pair_copy(step).start()

        i1 = step - LAG_PAIR

        @pl.when(owned(i1))
        def _():
            slot = slot_of(i1)
            pair_copy(i1).wait_recv()
            _accumulate(keep.at[slot], land.at[slot])
            for cond, u, n, chip, _ in pieces(block_of(i1)):
                @pl.when(cond & ((chip == ja) | (chip == jd)))
                def _(u=u, n=n):
                    _cast_rows(piece(keep, slot, u, n), piece(xbuf, slot, u, n))
                    h1_copy(slot, u, n).start()

        i2 = step - LAG_HOP1

        @pl.when(owned(i2))
        def _():
            slot = slot_of(i2)
            for cond, u, n, chip, local in pieces(block_of(i2)):
                @pl.when(cond & ((chip == j) | (chip == jb)))
                def _(u=u, n=n, chip=chip, local=local):
                    h1_copy(slot, u, n).wait_recv()
                    _accumulate(piece(keep, slot, u, n), piece(xbuf, slot, u, n))

                    @pl.when(chip == jb)
                    def _():
                        _cast_rows(piece(keep, slot, u, n), piece(xbuf, slot, u, n))
                        h2_copy(slot, u, n, local).start()

                @pl.when(cond & ((chip == ja) | (chip == jd)))
                def _(u=u, n=n):
                    h1_copy(slot, u, n).wait_send()

        i3 = step - LAG_HOP2

        @pl.when(owned(i3))
        def _():
            slot = slot_of(i3)
            for cond, u, n, chip, local in pieces(block_of(i3)):
                @pl.when(cond & (chip == j))
                def _(u=u, n=n, local=local):
                    h2_copy(slot, u, n, local).wait_recv()
                    _accumulate(piece(keep, slot, u, n), chip_rows_at(land2, local, n))
                    mine = pltpu.make_async_copy(piece(keep, slot, u, n), chip_rows_at(gwt_sh, local, n), out_sem.at[0])
                    mine.start()
                    sw_copy(slot, u, n, local).start()
                    mine.wait()

                @pl.when(cond & (chip == jb))
                def _(u=u, n=n, local=local):
                    h2_copy(slot, u, n, local).wait_send()

        e = step - n_blk

        @pl.when((e >= 0) & (e < n_tiles))
        def _():
            dh = dh_acc[pl.ds(pl.multiple_of(e * tile, tile), tile), :]
            xv = x_ref[...]
            r = _rstd(xv)
            xhat = xv * r
            gni[...] += jnp.sum(dh * xhat, axis=0, keepdims=True)
            gx_ref[...] = _rms_bwd(dh * g_ref[...], xhat, r) + dx2_ref[...]

        @pl.when(step == n_steps - 1)
        def _():
            small_land[me] = small_ref[...]
            small_land[me, SMALL_NORM_IN:SMALL_NORM_IN + 1, :] = gni[...]
            others = [(dx, dy, dc) for dx in (0, 1) for dy in (0, 1) for dc in (0, 1)][1:]
            sends = [remote(small_land.at[me], small_land.at[me], sm_send.at[k], sm_recv.at[k],
                            (_xor(x_i, dx), _xor(y_i, dy), _xor(c, dc))) for k, (dx, dy, dc) in enumerate(others)]
            for cp in sends:
                cp.start()
            for i in range(n_blk):
                if i + 2 * PAIR_RING >= n_blk:
                    @pl.when(owner_of(i) != c)
                    def _(i=i):
                        pair_copy(i).wait_send()
                for _, u, n, chip, local in pieces(block_of(i)):
                    @pl.when((j == chip) & (c == owner_of(i)))
                    def _(i=i, u=u, n=n, local=local):
                        sw_copy(slot_of(i), u, n, local).wait_send()

                    @pl.when((j == chip) & (c != owner_of(i)))
                    def _(i=i, u=u, n=n, local=local):
                        sw_copy(slot_of(i), u, n, local).wait_recv()
            for cp in sends:
                cp.wait_recv()
            total = small_land[0]
            for dev in range(1, 8):
                total = total + small_land[dev]
            small_sum[...] = total
            for cp in sends:
                cp.wait_send()

    def blk_at(i):
        return block_of(jnp.clip(i, 0, n_blk - 1))

    last_pc_step = max(i for i in range(n_blk) if block_of(i) < blk_q)

    def next_block(i, in_pc):
        i = jnp.clip(i, 0, n_blk - 1)
        step = jnp.full_like(i, last_pc_step if in_pc else n_blk - 1)
        for ahead in reversed(range(N_CHIPS)):
            cand = jnp.minimum(i + ahead, n_blk - 1)
            step = jnp.where((block_of(cand) < blk_q) == in_pc, cand, step)
        return block_of(step)

    def dqg_block(i):
        b = next_block(i, False)
        q_blk = jnp.clip(b - blk_q, 0, blk_kv - blk_q - 1)
        ga_blk = (D_ATTN // GBLK) + jnp.clip(b - blk_ga, 0, n_blk - blk_ga - 1)
        return jnp.where(b < blk_kv, q_blk, jnp.where(b == blk_kv, 2 * D_ATTN // GBLK, ga_blk))

    def tok(i):
        return (jnp.clip(i - n_blk, 0, n_tiles - 1), 0)

    n_piece = n_slots * n_sub
    dma = pltpu.SemaphoreType.DMA
    return pl.pallas_call(
        body, name="bwd_in", grid=(n_steps,),
        out_shape=(jax.ShapeDtypeStruct((seq, D_MODEL), F32), jax.ShapeDtypeStruct(small.shape, F32),
                   jax.ShapeDtypeStruct((chip_rows, D_MODEL), F32)),
        in_specs=[pl.BlockSpec((seq, GBLK), lambda i: (0, next_block(i, True))),
                  pl.BlockSpec((seq, GBLK), lambda i: (0, dqg_block(i))),
                  pl.BlockSpec((GBLK, D_MODEL), lambda i: (blk_at(i), 0)),
                  _resident(h.shape),
                  pl.BlockSpec((tile, D_MODEL), tok), _resident((1, D_MODEL)), pl.BlockSpec((tile, D_MODEL), tok),
                  _resident(small.shape)],
        out_specs=(pl.BlockSpec((tile, D_MODEL), tok), pl.BlockSpec(small.shape, lambda i: (0, 0)),
                   pl.BlockSpec(memory_space=pl.ANY)),
        scratch_shapes=[pltpu.VMEM((seq, D_MODEL), F32), pltpu.VMEM((1, D_MODEL), F32),
                        pltpu.VMEM((n_slots, GBLK, D_MODEL), F32), pltpu.VMEM((PAIR_RING, GBLK, D_MODEL), BF16),
                        pltpu.VMEM((n_slots, GBLK, D_MODEL), BF16), pltpu.VMEM((n_slots, GBLK, D_MODEL), BF16),
                        pltpu.VMEM((chip_rows, D_MODEL), BF16), pltpu.VMEM((8,) + small.shape, F32),
                        dma((n_slots,)), dma((n_slots,)), dma((n_piece,)), dma((n_piece,)), dma((n_piece,)),
                        dma((n_piece,)), dma((n_piece,)), dma((n_piece,)), dma((7,)), dma((7,)), dma((1,))],
        compiler_params=_params(62, ("arbitrary",)),
    )(dpc, dqg, wt, h, x, norm_in, dx2, small)


def _accumulate(dst_ref, src_ref, rows=16):
    def step(i, carry):
        sl = pl.ds(pl.multiple_of(i * rows, rows), rows)
        dst_ref[sl, :] = dst_ref[sl, :] + src_ref[sl, :].astype(F32)
        return carry
    lax.fori_loop(0, dst_ref.shape[0] // rows, step, 0)


def _rs_wout_scratch(gwo_shape):
    o_half, width = gwo_shape[0] // N_CHIPS // 2, gwo_shape[1]
    return [pltpu.VMEM((4, o_half, width), F32), pltpu.VMEM((4, o_half, width), BF16),
            pltpu.VMEM((4, o_half, width), BF16), pltpu.VMEM((2, o_half, width), BF16),
            pltpu.VMEM((o_half, width), BF16),
            pltpu.SemaphoreType.DMA((8,)), pltpu.SemaphoreType.DMA((8,)), pltpu.SemaphoreType.DMA((4,))]


def _rs_wout_stages(gwo_ref, gwo_sh, acc_o, sb_o, r1o, r2o, r3o, send_sems, recv_sems, local_sems):
    o_rows = gwo_ref.shape[0] // N_CHIPS
    o_half = o_rows // 2
    x, y, c = lax.axis_index("x"), lax.axis_index("y"), lax.axis_index("c")
    j = 2 * x + y
    pa = (_xor(x, 1 - c), _xor(y, c), c)
    pb = (_xor(x, c), _xor(y, 1 - c), c)
    sib = (x, y, 1 - c)
    ja = 2 * pa[0] + pa[1]
    jb = 2 * pb[0] + pb[1]
    jd = 3 - j
    order = (ja, jd, jb, j)
    sib_order = (jb, jd, ja, j)

    def rcopy(k, src, dst, to):
        return pltpu.make_async_remote_copy(src_ref=src, dst_ref=dst, send_sem=send_sems.at[k],
                                            recv_sem=recv_sems.at[k], device_id=to, device_id_type=MESH)

    def o_rows_of(chip, half):
        return gwo_ref.at[pl.ds(pl.multiple_of(chip * o_rows + half * o_half, 8), o_half), :]

    def load_all(chips, half):
        cps = [pltpu.make_async_copy(o_rows_of(chip, half), acc_o.at[s], local_sems.at[s]) for s, chip in enumerate(chips)]
        for cp in cps:
            cp.start()
        return cps

    def pair_send(s):
        return rcopy(s, sb_o.at[s], r1o.at[s], sib)

    def out_half(half):
        return gwo_sh.at[pl.ds(pl.multiple_of(half * o_half, 8), o_half), :]

    hop1 = [rcopy(4 + s, sb_o.at[s], r2o.at[s], pa) for s in range(2)]
    hop2 = rcopy(6, sb_o.at[2], r3o, pb)
    swap = rcopy(7, acc_o.at[3], out_half(c), sib)
    mine = pltpu.make_async_copy(acc_o.at[3], out_half(c), local_sems.at[0])

    def resend(s, copy):
        pair_send(s).wait_send()
        _cast_rows(acc_o.at[s], sb_o.at[s])
        copy.start()

    def stage_pair():
        for s, cp in enumerate(load_all(sib_order, 1 - c)):
            cp.wait()
            _cast_rows(acc_o.at[s], sb_o.at[s])
            pair_send(s).start()

    def stage_hop1():
        for s, cp in enumerate(load_all(order, c)):
            cp.wait()
            pair_send(s).wait_recv()
            _accumulate(acc_o.at[s], r1o.at[s])
            if s < 2:
                resend(s, hop1[s])

    def stage_hop2():
        hop1[1].wait_recv()
        _accumulate(acc_o.at[2], r2o.at[1])
        resend(2, hop2)
        hop1[0].wait_recv()
        _accumulate(acc_o.at[3], r2o.at[0])

    def stage_final():
        hop2.wait_recv()
        _accumulate(acc_o.at[3], r3o)
        mine.start()
        swap.start()

    def stage_drain():
        rcopy(7, acc_o.at[3], out_half(1 - c), sib).wait_recv()
        for cp in [pair_send(3)] + hop1 + [hop2, swap]:
            cp.wait_send()
        mine.wait()

    return stage_pair, stage_hop1, stage_hop2, stage_final, stage_drain


def _adamw_big(groups, passed):
    arrays = [a for group in groups for a in group[:4]]
    steps = [group[0].shape[0] // group[4] for group in groups]
    first = [sum(steps[:k]) for k in range(len(steps) + 1)]
    n = len(arrays)

    def body(*refs):
        ins, passed_in, outs, passed_out = refs[:n], refs[n], refs[n + 1:2 * n + 1], refs[2 * n + 1]
        i = pl.program_id(0)
        for k in range(len(groups)):
            @pl.when((i >= first[k]) & (i < first[k + 1]))
            def _(k=k):
                w_ref, g_ref, m_ref, v_ref = ins[4 * k:4 * k + 4]
                _adamw_update(w_ref, g_ref[...], m_ref, v_ref, *outs[4 * k:4 * k + 4])

        @pl.when(i < steps[0])
        def _():
            passed_out[...] = passed_in[...]

    def spec(k, rows, width):
        return pl.BlockSpec((rows, width), lambda i: (jnp.clip(i - first[k], 0, steps[k] - 1), 0))

    specs = [spec(k, group[4], group[0].shape[1]) for k, group in enumerate(groups) for _ in range(4)]
    specs.append(spec(0, passed.shape[0] // steps[0], passed.shape[1]))
    out = pl.pallas_call(
        body, name="adamw_big", grid=(first[-1],),
        out_shape=tuple(jax.ShapeDtypeStruct(a.shape, a.dtype) for a in arrays + [passed]),
        in_specs=specs, out_specs=tuple(specs),
        compiler_params=_params(40, ("arbitrary",)),
    )(*arrays, passed)
    return [tuple(out[4 * k:4 * k + 4]) for k in range(len(groups))], out[-1]


def _adamw_update(w_ref, gv, m_ref, v_ref, go_ref, d_ref, nm_ref, nv_ref, at=...):
    go_ref[at] = gv
    nm = ADAM_B1 * m_ref[at] + (1.0 - ADAM_B1) * gv
    nv = ADAM_B2 * v_ref[at] + (1.0 - ADAM_B2) * (gv * gv)
    m_hat = nm / (1.0 - ADAM_B1 ** ADAM_STEP)
    v_hat = nv / (1.0 - ADAM_B2 ** ADAM_STEP)
    d_ref[at] = -ADAM_LR * (m_hat / (jnp.sqrt(v_hat) + ADAM_EPS) + ADAM_WD * w_ref[at])
    nm_ref[at] = nm
    nv_ref[at] = nv


def _adamw_small(chip, small_sum, weights, grads_of, ms, vs):
    n = len(weights)

    def body(chip_ref, small_ref, *refs):
        ins, outs, loss_ref = refs[:3 * n], refs[3 * n:-1], refs[-1]
        for k in range(n):
            w_ref, m_ref, v_ref = ins[3 * k:3 * k + 3]
            for at, gv in grads_of[k](small_ref, chip_ref):
                _adamw_update(w_ref, gv, m_ref, v_ref, *outs[4 * k:4 * k + 4], at=at)
        loss_ref[...] = small_ref[SMALL_MISC:SMALL_MISC + 1, SMALL_LOSS_LANE:SMALL_LOSS_LANE + 1]

    flat = [a for group in zip(weights, ms, vs) for a in group]
    vmem = pl.BlockSpec(memory_space=pltpu.VMEM)
    out = pl.pallas_call(
        body, name="adamw_small",
        out_shape=tuple(jax.ShapeDtypeStruct(w.shape, F32) for w in weights for _ in range(4))
        + (jax.ShapeDtypeStruct((1, 1), F32),),
        in_specs=[pl.BlockSpec(memory_space=pltpu.SMEM)] + [vmem] * (1 + 3 * n), out_specs=(vmem,) * (4 * n + 1),
    )(chip, small_sum, *flat)
    return [tuple(out[4 * k:4 * k + 4]) for k in range(n)], out[-1][0, 0]


def kernel(x, norm_in, w_in, conv_w, attn_sinks, norm_conv_out, norm_attn_out, w_out, norm_final, loss_target, m_norm_in, m_w_in, m_conv_w, m_attn_sinks, m_norm_conv_out, m_norm_attn_out, m_w_out, m_norm_final, v_norm_in, v_w_in, v_conv_w, v_attn_sinks, v_norm_conv_out, v_norm_attn_out, v_w_out, v_norm_final):
    chip = 2 * lax.axis_index("x") + lax.axis_index("y")
    xs, target = x[0], loss_target[0]
    norm_final2 = norm_final.reshape(1, D_MODEL)
    w_in_t, m_w_in_t, v_w_in_t = w_in[0].T, m_w_in[0].T, v_w_in[0].T

    def from_hbm(*arrays):
        return tuple(pltpu.with_memory_space_constraint(a, pltpu.HBM) for a in arrays)

    h, pc, q, kv, ga, oc, ya, oa, probs, sink_probs, wt, cw, wo = _fwd_in(
        xs, norm_in, w_in_t, w_out[0], conv_w.transpose(1, 0, 2), norm_conv_out, attn_sinks, norm_attn_out)
    dx2, doa, gwo, dpc, small = _out_proj_loss(xs, oc, oa, wo, norm_final2, target, pc, cw, norm_conv_out)
    q, kv, ga, ya, doa, probs, gwo, small = from_hbm(q, kv, ga, ya, doa, probs, gwo, small)
    dqg, small, gwo_sh = _attn_bwd(q, kv, ga, ya, doa, probs, sink_probs, norm_attn_out, gwo, small)
    grad_x, small_sum, gwt_sh = _bwd_in(dpc, dqg, h, wt, xs, norm_in, dx2, small)

    (up_w_in, up_w_out), grad_x = _adamw_big(
        [from_hbm(w_in_t, gwt_sh, m_w_in_t, v_w_in_t) + (200,), from_hbm(w_out[0], gwo_sh, m_w_out[0], v_w_out[0]) + (128,)],
        *from_hbm(grad_x))
    up_w_in = tuple(o.T[None] for o in up_w_in)
    up_w_out = tuple(o[None] for o in up_w_out)

    def row_of(r):
        return lambda small_ref, chip_ref: [(..., small_ref[r:r + 1, :])]

    def sink_lanes(small_ref, chip_ref):
        return [(..., small_ref[SMALL_MISC:SMALL_MISC + 1, 0:N_HEADS])]

    def conv_taps(small_ref, chip_ref):
        width = D_CONV // N_CHIPS
        cols = pl.ds(pl.multiple_of(chip_ref[0] * width, width), width)
        return [(k, small_ref[pl.ds(SMALL_CONV_W + k, 1), cols]) for k in range(conv_w.shape[1])]

    def small_view(a):
        return a.transpose(1, 0, 2) if a.ndim == 3 else a.reshape(-1, a.shape[-1])

    small_w = (norm_in, conv_w, attn_sinks, norm_conv_out, norm_attn_out, norm_final)
    small_m = (m_norm_in, m_conv_w, m_attn_sinks, m_norm_conv_out, m_norm_attn_out, m_norm_final)
    small_v = (v_norm_in, v_conv_w, v_attn_sinks, v_norm_conv_out, v_norm_attn_out, v_norm_final)
    small_g = (row_of(SMALL_NORM_IN), conv_taps, sink_lanes, row_of(SMALL_NORM_CONV), row_of(SMALL_NORM_ATTN),
               row_of(SMALL_NORM_FINAL))
    w_views, m_views, v_views = (tuple(small_view(a) for a in group) for group in (small_w, small_m, small_v))
    up_small, loss = _adamw_small(chip.astype(jnp.int32).reshape(1), small_sum, w_views, small_g, m_views, v_views)
    up_small = [tuple(o.transpose(1, 0, 2) if w.ndim == 3 else o.reshape(w.shape) for o in up)
                for up, w in zip(up_small, small_w)]
    up_norm_in, up_conv_w, up_sinks, up_norm_conv, up_norm_attn, up_norm_final = up_small
    updates = (up_norm_in, up_w_in, up_conv_w, up_sinks, up_norm_conv, up_norm_attn, up_w_out, up_norm_final)
    grads_out, deltas, new_m, new_v = zip(*updates)
    return (loss, grad_x[None], *grads_out, *deltas, *new_m, *new_v)
```

```python
import jax
import jax.numpy as jnp
from jax import lax
from jax.experimental import pallas as pl
from jax.experimental.pallas import tpu as pltpu

F32 = jnp.float32
BF16 = jnp.bfloat16
MESH = pl.DeviceIdType.MESH

D_MODEL = 1024
D_CONV = 1024
D_ATTN = 1024
D_KV = 128
D_QG = 2 * D_ATTN + 2 * D_KV
D_MIX = D_CONV + D_ATTN
D_PC = 4 * D_CONV
D_IN_PROJ = D_PC + 2 * D_ATTN + 2 * D_KV
ROW_Q = D_PC
ROW_KV = ROW_Q + D_ATTN
ROW_GA = ROW_KV + 2 * D_KV
N_HEADS = 16
HEAD_DIM = 64
HEADS_PER_KV = 8
BLK = 128
N_CHIPS = 4
RMS_EPS = 1e-5
SCALE = HEAD_DIM ** -0.5
SLOPES = tuple(2.0 ** (-8.0 * (h + 1) / N_HEADS) for h in range(N_HEADS))

ADAM_LR, ADAM_B1, ADAM_B2, ADAM_EPS, ADAM_WD, ADAM_STEP = 0.001, 0.9, 0.999, 1e-08, 0.01, 10

SMALL_ROWS = 8
SMALL_NORM_IN, SMALL_NORM_CONV, SMALL_NORM_ATTN, SMALL_NORM_FINAL, SMALL_CONV_W, SMALL_MISC = 0, 1, 2, 3, 4, 7
SMALL_LOSS_LANE = N_HEADS

TOK_TILE = 256
PC_PIECE = 512
MIB = 1 << 20


def _params(vmem_mib, semantics=None):
    return pltpu.CompilerParams(dimension_semantics=semantics, vmem_limit_bytes=vmem_mib * MIB)


def _nn(a, b):
    return jnp.dot(a, b, preferred_element_type=F32)


def _nt(a, b):
    return lax.dot_general(a, b, (((1,), (1,)), ((), ())), preferred_element_type=F32)


def _tn(a, b):
    return lax.dot_general(a, b, (((0,), (0,)), ((), ())), preferred_element_type=F32)


def _rstd(v):
    return lax.rsqrt(jnp.mean(v * v, axis=-1, keepdims=True) + RMS_EPS)


def _rms_bwd(g, xhat, rstd):
    return rstd * (g - xhat * jnp.mean(g * xhat, axis=-1, keepdims=True))


def _silu_and_grad(g):
    s = jax.nn.sigmoid(g)
    return g * s, s * (1.0 + g * (1.0 - s))


def _resident(shape):
    return pl.BlockSpec(shape, lambda *_: (0,) * len(shape), pipeline_mode=pl.Buffered(1))


def _xor(a, b):
    return a + b - 2 * a * b


def _cast_rows(src_ref, dst_ref, rows=32):
    def step(i, carry):
        sl = pl.ds(pl.multiple_of(i * rows, rows), rows)
        dst_ref[sl, :] = src_ref[sl, :].astype(dst_ref.dtype)
        return carry
    lax.fori_loop(0, src_ref.shape[0] // rows, step, 0)


def _ag_scratch(shard_shape):
    rows, width = shard_shape
    return [pltpu.VMEM((rows, width), F32), pltpu.VMEM((rows, width), BF16), pltpu.VMEM((3, rows // 2, width), BF16),
            pltpu.SemaphoreType.DMA((6,)), pltpu.SemaphoreType.DMA((6,)), pltpu.SemaphoreType.DMA((4,))]


def _ag_stages(sh_ref, out, f32_buf, own, land, send_sems, recv_sems, local_sems):
    rows = sh_ref.shape[0]
    half = rows // 2
    x, y, c = lax.axis_index("x"), lax.axis_index("y"), lax.axis_index("c")
    j = 2 * x + y
    p1 = (_xor(x, c), _xor(y, 1 - c), c)
    p2 = (_xor(x, 1 - c), _xor(y, c), c)
    sib = (x, y, 1 - c)
    j1 = 2 * p1[0] + p1[1]
    j2 = 2 * p2[0] + p2[1]
    j3 = 3 - j

    def rows_of(chip, hf):
        return out.at[pl.ds(pl.multiple_of(chip * rows + hf * half, 16), half), :]

    def rcopy(k, src, dst, to):
        return pltpu.make_async_remote_copy(src_ref=src, dst_ref=dst, send_sem=send_sems.at[k],
                                            recv_sem=recv_sems.at[k], device_id=to, device_id_type=MESH)

    my_half = own.at[pl.ds(pl.multiple_of(c * half, 16), half), :]
    hop1 = rcopy(0, my_half, land.at[0], p1)
    hop2_own = rcopy(1, my_half, land.at[1], p2)
    hop2_fwd = rcopy(2, land.at[0], land.at[2], p2)
    swaps = [rcopy(3 + s, land.at[s], rows_of(chip, c), sib) for s, chip in enumerate((j1, j2, j3))]
    keeps = [pltpu.make_async_copy(land.at[s], rows_of(chip, c), local_sems.at[1 + s]) for s, chip in enumerate((j1, j2, j3))]
    load = pltpu.make_async_copy(sh_ref, f32_buf, local_sems.at[0])
    own_out = pltpu.make_async_copy(own, out.at[pl.ds(pl.multiple_of(j * rows, 16), rows), :], local_sems.at[0])

    def stage_send():
        load.start()
        load.wait()
        _cast_rows(f32_buf, own)
        own_out.start()
        hop1.start()

    def stage_forward():
        hop1.wait_recv()
        hop2_own.start()
        hop2_fwd.start()
        swaps[0].start()
        keeps[0].start()

    def stage_publish():
        hop2_own.wait_recv()
        swaps[1].start()
        keeps[1].start()
        hop2_fwd.wait_recv()
        swaps[2].start()
        keeps[2].start()

    def stage_drain():
        for s, chip in enumerate((j2, j1, j3)):
            rcopy(3 + s, my_half, rows_of(chip, 1 - c), sib).wait_recv()
        for cp in [hop1, hop2_own, hop2_fwd] + swaps:
            cp.wait_send()
        for cp in [own_out] + keeps:
            cp.wait()

    return stage_send, stage_forward, stage_publish, stage_drain


AG_CAST_ROWS = 400


def _gather_resident(sh_ref, out, f32_buf, send_sems, recv_sems, local_sems):
    rows = sh_ref.shape[0]
    half = rows // 2
    x, y, c = lax.axis_index("x"), lax.axis_index("y"), lax.axis_index("c")
    j = 2 * x + y
    p1 = (_xor(x, c), _xor(y, 1 - c), c)
    p2 = (_xor(x, 1 - c), _xor(y, c), c)
    sib = (x, y, 1 - c)
    j1 = 2 * p1[0] + p1[1]
    j2 = 2 * p2[0] + p2[1]
    j3 = 3 - j

    def rows_of(chip, hf):
        return out.at[pl.ds(pl.multiple_of(chip * rows + hf * half, 16), half), :]

    def send(k, chip, to):
        return pltpu.make_async_remote_copy(src_ref=rows_of(chip, c), dst_ref=rows_of(chip, c), send_sem=send_sems.at[k],
                                            recv_sem=recv_sems.at[k], device_id=to, device_id_type=MESH)

    per_half = half // AG_CAST_ROWS

    def cast_own(chunk):
        lo = pl.multiple_of(chunk * AG_CAST_ROWS, 16)
        load = pltpu.make_async_copy(sh_ref.at[pl.ds(lo, AG_CAST_ROWS), :], f32_buf, local_sems.at[0])
        load.start()
        load.wait()
        _cast_rows(f32_buf, out.at[pl.ds(pl.multiple_of(j * rows + lo, 16), AG_CAST_ROWS), :], rows=16)

    for k in range(per_half):
        cast_own(c * per_half + k)
    hop1 = send(0, j, p1)
    hop1.start()
    for k in range(per_half):
        cast_own((1 - c) * per_half + k)
    hop1.wait_recv()
    sends = [hop1, send(1, j, p2), send(2, j1, p2), send(3, j1, sib)]
    for cp in sends[1:]:
        cp.start()
    sends[1].wait_recv()
    sends.append(send(4, j2, sib))
    sends[-1].start()
    sends[2].wait_recv()
    sends.append(send(5, j3, sib))
    sends[-1].start()
    for k in (3, 4, 5):
        send(k, j, sib).wait_recv()
    for cp in sends:
        cp.wait_send()


def _fwd_in(x, norm_in, wt_sh, wo_sh, cw_sh, norm_conv_out, sinks, norm_attn_out):
    seq = x.shape[0]
    tile = TOK_TILE
    n_tiles = seq // tile
    stage_steps = (0, n_tiles // 4, (5 * n_tiles) // 8, n_tiles - 1)
    blocks = tile // BLK
    cw_cols = cw_sh.shape[-1]

    def body(x_ref, g_ref, wtsh_ref, wo_ref, cwsh_ref, gn_ref, sink_ref, gna_ref,
             h_ref, pc_ref, q_ref, kv_ref, ga_ref, oc_ref, ya_ref, oa_ref, pr_ref, sp_ref, wt_out, cw_ref, wo_out,
             zbuf, kv_last, wt_ref, f32_buf, cw_land, wt_send, wt_recv, wt_local, cw_send, cw_recv, cw_local, *ag_scratch):
        step = pl.program_id(0)
        to_hbm = pltpu.make_async_copy(wt_ref, wt_out, wt_local.at[0])

        @pl.when(step == 0)
        def _():
            zbuf[0:8, :] = jnp.zeros((8, D_CONV), F32)
            kv_last[...] = jnp.zeros_like(kv_last)
            x_i, y_i, c = lax.axis_index("x"), lax.axis_index("y"), lax.axis_index("c")
            j = 2 * x_i + y_i
            p1 = (_xor(x_i, c), _xor(y_i, 1 - c), c)
            p2 = (_xor(x_i, 1 - c), _xor(y_i, c), c)
            j1 = 2 * p1[0] + p1[1]

            def cw_copy(k, src, chip, to):
                return pltpu.make_async_remote_copy(src_ref=src, dst_ref=cw_land.at[chip], send_sem=cw_send.at[k],
                                                    recv_sem=cw_recv.at[k], device_id=to, device_id_type=MESH)

            mine = pltpu.make_async_copy(cwsh_ref, cw_land.at[j], cw_local.at[0])
            mine.start()
            first = cw_copy(0, cwsh_ref, j, p1)
            first.start()
            _gather_resident(wtsh_ref, wt_ref, f32_buf, wt_send, wt_recv, wt_local)
            to_hbm.start()
            first.wait_recv()
            second = [cw_copy(1, cwsh_ref, j, p2), cw_copy(2, cw_land.at[j1], j1, p2)]
            for cp in second:
                cp.start()
            for cp in second:
                cp.wait_recv()
            for cp in [first] + second:
                cp.wait_send()
            mine.wait()
            for chip in range(N_CHIPS):
                for tap in range(cw_sh.shape[0]):
                    cw_ref[tap:tap + 1, chip * cw_cols:(chip + 1) * cw_cols] = cw_land[chip, tap]

        stages = _ag_stages(wo_ref, wo_out, *ag_scratch)
        for at, stage in zip(stage_steps[:-1], stages[:-1]):
            pl.when(step == at)(stage)

        def attention(b):
            rows = pl.ds(b * BLK, BLK)
            kv_prev = kv_last if b == 0 else kv_ref.at[pl.ds((b - 1) * BLK, BLK), :]
            return _attn_forward(q_ref.at[rows, :], kv_ref.at[rows, :], kv_prev, ga_ref.at[rows, :], sink_ref, gna_ref,
                                 _band_geometry(step * blocks + b), ya_ref.at[rows, :], oa_ref.at[rows, :],
                                 pr_ref.at[rows, :], sp_ref.at[rows, :])

        xv = x_ref[...]
        h = (xv * _rstd(xv) * g_ref[...]).astype(BF16)
        h_ref[...] = h
        q_ref[...] = _nt(h, wt_ref[ROW_Q:ROW_KV, :])
        kv_ref[...] = _nt(h, wt_ref[ROW_KV:ROW_GA, :])
        ga_ref[...] = _nt(h, wt_ref[ROW_GA:D_IN_PROJ, :])
        attention_blocks = [attention(b) for b in range(blocks)]
        for lo in range(0, D_PC, PC_PIECE):
            pc_ref[:, lo:lo + PC_PIECE] = _nt(h, wt_ref[lo:lo + PC_PIECE, :])
            for stages_of_block in attention_blocks:
                next(stages_of_block, None)
        for stages_of_block in attention_blocks:
            for _ in stages_of_block:
                pass
        kv_last[...] = kv_ref[tile - BLK:tile, :]

        cb, _, _, _, _, _, conv = _conv_core(pc_ref, zbuf, cw_ref)
        yc = cb * conv
        silu, _ = _silu_and_grad(pc_ref[:, 3 * D_CONV:4 * D_CONV])
        oc_ref[...] = (yc * _rstd(yc) * gn_ref[...] * silu).astype(BF16)
        zbuf[0:8, :] = zbuf[tile:tile + 8, :]

        @pl.when(step == stage_steps[-1])
        def _():
            stages[-1]()
            to_hbm.wait()

    def row(width):
        return pl.BlockSpec((tile, width), lambda i: (i, 0))

    any_spec = pl.BlockSpec(memory_space=pl.ANY)
    dma = pltpu.SemaphoreType.DMA
    wt_shape = (N_CHIPS * wt_sh.shape[0], wt_sh.shape[1])
    cw_shape = (cw_sh.shape[0], N_CHIPS * cw_cols)
    return pl.pallas_call(
        body, name="fwd_in", grid=(n_tiles,),
        out_shape=(jax.ShapeDtypeStruct((seq, D_MODEL), BF16), jax.ShapeDtypeStruct((seq, D_PC), F32),
                   jax.ShapeDtypeStruct((seq, D_ATTN), F32), jax.ShapeDtypeStruct((seq, 2 * D_KV), F32),
                   jax.ShapeDtypeStruct((seq, D_ATTN), F32), jax.ShapeDtypeStruct((seq, D_CONV), BF16),
                   pltpu.HBM((seq, D_ATTN), F32), pltpu.HBM((seq, D_ATTN), BF16),
                   pltpu.HBM((seq, N_HEADS * BLK), BF16), pltpu.HBM((seq, 128), F32),
                   pltpu.HBM(wt_shape, BF16), jax.ShapeDtypeStruct(cw_shape, F32),
                   jax.ShapeDtypeStruct((N_CHIPS * wo_sh.shape[0], wo_sh.shape[1]), BF16)),
        in_specs=[row(D_MODEL), _resident((1, D_MODEL)), any_spec, any_spec, any_spec, _resident((1, D_CONV)),
                  pl.BlockSpec(memory_space=pltpu.SMEM), _resident((1, D_ATTN))],
        out_specs=(row(D_MODEL), row(D_PC), row(D_ATTN), row(2 * D_KV), row(D_ATTN), row(D_CONV), row(D_ATTN),
                   row(D_ATTN), row(N_HEADS * BLK), row(128), any_spec, pl.BlockSpec(cw_shape, lambda i: (0, 0)),
                   any_spec),
        scratch_shapes=[pltpu.VMEM((tile + 8, D_CONV), F32), pltpu.VMEM((BLK, 2 * D_KV), F32),
                        pltpu.VMEM(wt_shape, BF16), pltpu.VMEM((AG_CAST_ROWS, wt_sh.shape[1]), F32),
                        pltpu.VMEM((N_CHIPS,) + cw_sh.shape, F32),
                        dma((6,)), dma((6,)), dma((1,)), dma((3,)), dma((3,)), dma((1,))] + _ag_scratch(wo_sh.shape),
        compiler_params=_params(62, ("arbitrary",)),
    )(x, norm_in, wt_sh, wo_sh, cw_sh, norm_conv_out, sinks, norm_attn_out)


def _conv_core(pc_ref, zbuf, cw_ref):
    tile = pc_ref.shape[0]
    cb = pc_ref[:, 0:D_CONV]
    cc = pc_ref[:, D_CONV:2 * D_CONV]
    cu = pc_ref[:, 2 * D_CONV:3 * D_CONV]
    z = cc * cu
    zbuf[8:tile + 8, :] = z
    z1 = zbuf[7:tile + 7, :]
    z2 = zbuf[6:tile + 6, :]
    conv = cw_ref[0:1, :] * z2 + cw_ref[1:2, :] * z1 + cw_ref[2:3, :] * z
    return cb, cc, cu, z, z1, z2, conv


def _band_geometry(block_index):
    qi = lax.broadcasted_iota(jnp.int32, (BLK, BLK), 0)
    kp = lax.broadcasted_iota(jnp.int32, (BLK, BLK), 1)
    use_cur = kp <= qi
    dist = jnp.where(use_cur, qi - kp, qi - kp + BLK).astype(F32)
    valid = use_cur | (block_index > 0)
    return use_cur, dist, valid


def _block_diag(cur, prev, group):
    lane = lax.broadcasted_iota(jnp.int32, cur.shape, 1)

    def halves(t):
        other = pltpu.roll(t, 64, 1)
        lo, hi = (t, other) if group == 0 else (other, t)
        return jnp.where(lane < 64, lo, 0.0), jnp.where(lane >= 64, hi, 0.0)

    return jnp.concatenate(halves(cur) + halves(prev), axis=0).astype(BF16)


def _merge(s4, use_cur):
    return (jnp.where(use_cur, s4[:, 0:BLK], s4[:, 2 * BLK:3 * BLK]),
            jnp.where(use_cur, s4[:, BLK:2 * BLK], s4[:, 3 * BLK:4 * BLK]))


def _split(a, b, use_cur):
    return jnp.concatenate([jnp.where(use_cur, a, 0.0), jnp.where(use_cur, b, 0.0),
                            jnp.where(use_cur, 0.0, a), jnp.where(use_cur, 0.0, b)], axis=1)


def _softmax_head(s, head, sink, dist, valid):
    sc = jnp.where(valid, s - SLOPES[head] * dist, -jnp.inf)
    m = jnp.maximum(jnp.max(sc, axis=-1, keepdims=True), sink)
    p = jnp.exp(sc - m)
    es = jnp.exp(sink - m)
    inv = 1.0 / (jnp.sum(p, axis=-1, keepdims=True) + es)
    return p * inv, es * inv


def _attn_operands(q_ref, kvc_ref, kvp_ref):
    groups = range(N_HEADS // HEADS_PER_KV)
    kbd = [_block_diag(kvc_ref[:, 0:D_KV], kvp_ref[:, 0:D_KV], g) for g in groups]
    vbd = [_block_diag(kvc_ref[:, D_KV:2 * D_KV], kvp_ref[:, D_KV:2 * D_KV], g) for g in groups]
    qps = [(q_ref[:, j * 128:(j + 1) * 128] * SCALE).astype(BF16) for j in range(N_HEADS // 2)]
    return qps, kbd, vbd


def _attn_forward(q_ref, kvc_ref, kvp_ref, ga_ref, sink_ref, gn_ref, geometry, ya_ref, oa_ref, pr_ref, sp_ref):
    use_cur, dist, valid = geometry
    _, kbd, vbd = operands = _attn_operands(q_ref, kvc_ref, kvp_ref)
    yield
    scores = []
    for j, qp in enumerate(operands[0]):
        scores += _merge(_nt(qp, kbd[j // 4]), use_cur)
    yield
    sinks = [sink_ref[0, h] for h in range(N_HEADS)]
    scores = [jnp.where(valid, s - SLOPES[h] * dist, -jnp.inf) for h, s in enumerate(scores)]
    maxes = [jnp.maximum(jnp.max(s, axis=-1, keepdims=True), sinks[h]) for h, s in enumerate(scores)]
    yield
    exps = [jnp.exp(s - m) for s, m in zip(scores, maxes)]
    sink_exps = [jnp.exp(sinks[h] - m) for h, m in enumerate(maxes)]
    yield
    invs = [1.0 / (jnp.sum(e, axis=-1, keepdims=True) + se) for e, se in zip(exps, sink_exps)]
    probs = [e * inv for e, inv in zip(exps, invs)]
    pr_ref[...] = jnp.concatenate(probs, axis=1).astype(BF16)
    lane = lax.broadcasted_iota(jnp.int32, (BLK, 128), 1)
    sp_ref[...] = sum(jnp.where(lane == h, se * inv, 0.0) for h, (se, inv) in enumerate(zip(sink_exps, invs)))
    yield
    p4s = [_split(probs[2 * j], probs[2 * j + 1], use_cur).astype(BF16) for j in range(N_HEADS // 2)]
    ya = jnp.concatenate([_nn(p4, vbd[j // 4]) for j, p4 in enumerate(p4s)], axis=1)
    ya_ref[...] = ya
    yield
    silu, _ = _silu_and_grad(ga_ref[...])
    oa_ref[...] = (ya * _rstd(ya) * gn_ref[...] * silu).astype(BF16)


def _out_proj_loss(x, oc, oa, wo, norm_final, target, pc, conv_w, norm_conv_out):
    seq = x.shape[0]
    tile = TOK_TILE
    n_tiles = seq // tile

    def body(x_ref, oc_ref, oa_ref, wo_ref, gf_ref, t_ref, pc_ref, hcc_ref, hcu_ref, cw_ref, gn_ref,
             dx2_ref, doa_ref, gwo_ref, dpc_ref, small_ref, loss_acc, zbuf, dbuf):
        step = pl.program_id(0)

        def small_add(row, value):
            small_ref[row:row + 1, :] += jnp.sum(value, axis=0, keepdims=True)

        @pl.when(step == 0)
        def _():
            gwo_ref[...] = jnp.zeros_like(gwo_ref)
            small_ref[...] = jnp.zeros_like(small_ref)
            loss_acc[...] = jnp.zeros_like(loss_acc)
            dbuf[tile:tile + 8, :] = jnp.zeros((8, D_CONV), F32)

        oc, oa = oc_ref[...], oa_ref[...]
        x2 = x_ref[...] + _nn(oc, wo_ref[0:D_CONV, :]) + _nn(oa, wo_ref[D_CONV:D_MIX, :])
        r = _rstd(x2)
        xhat = x2 * r
        err = xhat * gf_ref[...] - t_ref[...]
        loss_acc[...] += jnp.sum(err * err, axis=0, keepdims=True) * (0.5 / D_MODEL)
        dy = err * (1.0 / D_MODEL)
        small_add(SMALL_NORM_FINAL, dy * xhat)
        dx2 = _rms_bwd(dy * gf_ref[...], xhat, r)
        dx2_ref[...] = dx2
        db = dx2.astype(BF16)
        do = _nt(db, wo_ref[0:D_CONV, :])
        doa_ref[...] = _nt(db, wo_ref[D_CONV:D_MIX, :])
        gwo_ref[0:D_CONV, :] += _tn(oc, db)
        gwo_ref[D_CONV:D_MIX, :] += _tn(oa, db)

        is_first_tile = step == n_tiles - 1
        zbuf[0:8, :] = jnp.where(is_first_tile, 0.0, hcc_ref[...] * hcu_ref[...])
        cb, cc, cu, z, z1, z2, conv = _conv_core(pc_ref, zbuf, cw_ref)
        silu, dsilu = _silu_and_grad(pc_ref[:, 3 * D_CONV:4 * D_CONV])
        yc = cb * conv
        rc = _rstd(yc)
        chat = yc * rc
        dn = do * silu
        dpc_ref[:, 3 * D_CONV:4 * D_CONV] = (do * (chat * gn_ref[...]) * dsilu).astype(BF16)
        small_add(SMALL_NORM_CONV, dn * chat)
        dyc = _rms_bwd(dn * gn_ref[...], chat, rc)
        dpc_ref[:, 0:D_CONV] = (dyc * conv).astype(BF16)
        dconv = dyc * cb
        small_add(SMALL_CONV_W, dconv * z2)
        small_add(SMALL_CONV_W + 1, dconv * z1)
        small_add(SMALL_CONV_W + 2, dconv * z)
        dbuf[0:tile, :] = dconv
        dz = cw_ref[2:3, :] * dconv + cw_ref[1:2, :] * dbuf[1:tile + 1, :] + cw_ref[0:1, :] * dbuf[2:tile + 2, :]
        dpc_ref[:, D_CONV:2 * D_CONV] = (dz * cu).astype(BF16)
        dpc_ref[:, 2 * D_CONV:3 * D_CONV] = (dz * cc).astype(BF16)
        dbuf[tile:tile + 8, :] = dbuf[0:8, :]

        @pl.when(step == n_tiles - 1)
        def _():
            lane = lax.broadcasted_iota(jnp.int32, (1, D_MODEL), 1)
            small_ref[SMALL_MISC:SMALL_MISC + 1, :] = jnp.where(lane == SMALL_LOSS_LANE, jnp.sum(loss_acc[...]), 0.0)

    def rev(i):
        return n_tiles - 1 - i

    def row(width):
        return pl.BlockSpec((tile, width), lambda i: (rev(i), 0))

    def halo(col_block):
        return pl.BlockSpec((8, D_CONV), lambda i: (jnp.maximum(rev(i) * (tile // 8) - 1, 0), col_block))

    def const(shape):
        return pl.BlockSpec(shape, lambda i: (0, 0))

    return pl.pallas_call(
        body, name="out_proj_loss", grid=(n_tiles,),
        out_shape=(jax.ShapeDtypeStruct((seq, D_MODEL), F32), jax.ShapeDtypeStruct((seq, D_ATTN), F32),
                   jax.ShapeDtypeStruct((D_MIX, D_MODEL), F32), jax.ShapeDtypeStruct((seq, D_PC), BF16),
                   jax.ShapeDtypeStruct((SMALL_ROWS, D_MODEL), F32)),
        in_specs=[row(D_MODEL), row(D_CONV), row(D_ATTN), _resident(wo.shape), _resident((1, D_MODEL)), row(D_MODEL),
                  row(D_PC), halo(1), halo(2), _resident(conv_w.shape), _resident((1, D_CONV))],
        out_specs=(row(D_MODEL), row(D_ATTN), const((D_MIX, D_MODEL)), row(D_PC), const((SMALL_ROWS, D_MODEL))),
        scratch_shapes=[pltpu.VMEM((1, D_MODEL), F32), pltpu.VMEM((tile + 8, D_CONV), F32),
                        pltpu.VMEM((tile + 8, D_CONV), F32)],
        compiler_params=_params(62, ("arbitrary",)),
    )(x, oc, oa, wo, norm_final, target, pc, pc, pc, conv_w, norm_conv_out)


def _attn_bwd(q, kv, ga, ya, doa, probs, sink_probs, norm_attn_out, gwo, small):
    seq = q.shape[0]
    per_step = 2
    n_steps = seq // (per_step * BLK)
    stage_steps = (0, n_steps // 4, n_steps // 2, (3 * n_steps) // 4, n_steps - 1)

    def body(q_ref, kvc_ref, kvp_ref, ga_ref, ya_ref, doa_ref, pr_ref, sp_ref, gn_ref, gwo_ref, small_ref,
             dqg_ref, small_out, gwo_sh, gna_ref, gs_ref, carry, dya_buf, *rs_scratch):
        step = pl.program_id(0)

        @pl.when(step == 0)
        def _():
            gna_ref[...] = jnp.zeros_like(gna_ref)
            gs_ref[...] = jnp.zeros_like(gs_ref)
            carry[...] = jnp.zeros_like(carry)

        lane = lax.broadcasted_iota(jnp.int32, (BLK, 128), 1)

        def fold(bd):
            return (jnp.where(lane < 64, bd[0:BLK], 0.0) + jnp.where(lane >= 64, bd[BLK:2 * BLK], 0.0),
                    jnp.where(lane < 64, bd[2 * BLK:3 * BLK], 0.0) + jnp.where(lane >= 64, bd[3 * BLK:4 * BLK], 0.0))

        def one_block(b):
            rows = slice(b * BLK, (b + 1) * BLK)
            kv_prev = kvp_ref if b == 0 else kvc_ref.at[(b - 1) * BLK:b * BLK, :]
            ya = ya_ref[rows, :]
            r = _rstd(ya)
            xhat = ya * r
            silu, dsilu = _silu_and_grad(ga_ref[rows, :])
            do = doa_ref[rows, :]
            dn = do * silu
            dqg_ref[rows, D_ATTN:2 * D_ATTN] = (do * (xhat * gn_ref[...]) * dsilu).astype(BF16)
            gna_ref[...] += jnp.sum(dn * xhat, axis=0, keepdims=True)
            dya_buf[rows, :] = _rms_bwd(dn * gn_ref[...], xhat, r).astype(BF16)

            use_cur = _band_geometry(per_step * (n_steps - 1 - step) + b)[0]
            pairs = range(N_HEADS // 2)
            qps, kbd, vbd = _attn_operands(q_ref.at[rows, :], kvc_ref.at[rows, :], kv_prev)
            probs = [pr_ref[rows, h * BLK:(h + 1) * BLK].astype(F32) for h in range(N_HEADS)]
            dyps = [dya_buf[rows, j * 128:(j + 1) * 128] for j in pairs]
            dps = []
            for j in pairs:
                dps += _merge(_nt(dyps[j], vbd[j // 4]), use_cur)
            deltas = [jnp.sum(p * dp, axis=-1, keepdims=True) for p, dp in zip(probs, dps)]
            dss = [p * (dp - delta) for p, dp, delta in zip(probs, dps, deltas)]
            delta_lanes = sum(jnp.where(lane == h, deltas[h], 0.0) for h in range(N_HEADS))
            gs_ref[...] -= jnp.sum(sp_ref[rows, :] * delta_lanes, axis=0, keepdims=True)
            ds4s = [_split(dss[2 * j], dss[2 * j + 1], use_cur).astype(BF16) for j in pairs]
            p4s = [_split(probs[2 * j], probs[2 * j + 1], use_cur).astype(BF16) for j in pairs]
            dqg_ref[rows, 0:D_ATTN] = jnp.concatenate([_nn(ds4s[j], kbd[j // 4]) * SCALE for j in pairs], axis=1).astype(BF16)
            sums = []
            for group in range(N_HEADS // HEADS_PER_KV):
                acc = [jnp.zeros((BLK, 128), F32) for _ in range(4)]
                for j in range(group * 4, group * 4 + 4):
                    for slot, part in enumerate(fold(_tn(ds4s[j], qps[j])) + fold(_tn(p4s[j], dyps[j]))):
                        acc[slot] = acc[slot] + part
                sums.append([a + pltpu.roll(a, 64, 1) for a in acc])
            dk_cur, dk_prev, dv_cur, dv_prev = (jnp.where(lane < 64, lo, hi) for lo, hi in zip(sums[0], sums[1]))
            dqg_ref[rows, 2 * D_ATTN:2 * D_ATTN + 2 * D_KV] = (jnp.concatenate([dk_cur, dv_cur], axis=1) + carry[...]).astype(BF16)
            carry[...] = jnp.concatenate([dk_prev, dv_prev], axis=1)

        for b in reversed(range(per_step)):
            one_block(b)

        @pl.when(step == n_steps - 1)
        def _():
            small_out[...] = small_ref[...]
            small_out[SMALL_NORM_ATTN:SMALL_NORM_ATTN + 1, :] = gna_ref[...]
            small_out[SMALL_MISC:SMALL_MISC + 1, 0:128] = small_ref[SMALL_MISC:SMALL_MISC + 1, 0:128] + gs_ref[...]

        for at, stage in zip(stage_steps, _rs_wout_stages(gwo_ref, gwo_sh, *rs_scratch)):
            pl.when(step == at)(stage)

    def rows(width):
        return pl.BlockSpec((per_step * BLK, width), lambda i: (n_steps - 1 - i, 0))

    kv_prev = pl.BlockSpec((BLK, 2 * D_KV), lambda i: (jnp.maximum(per_step * (n_steps - 1 - i) - 1, 0), 0))
    any_spec = pl.BlockSpec(memory_space=pl.ANY)
    return pl.pallas_call(
        body, name="attn_bwd", grid=(n_steps,),
        out_shape=(jax.ShapeDtypeStruct((seq, D_QG), BF16), pltpu.HBM(small.shape, F32),
                   pltpu.HBM((gwo.shape[0] // N_CHIPS, gwo.shape[1]), F32)),
        in_specs=[rows(D_ATTN), rows(2 * D_KV), kv_prev, rows(D_ATTN), rows(D_ATTN), rows(D_ATTN),
                  rows(N_HEADS * BLK), rows(128), _resident((1, D_ATTN)), any_spec, _resident(small.shape)],
        out_specs=(rows(D_QG), pl.BlockSpec(small.shape, lambda i: (0, 0)), any_spec),
        scratch_shapes=[pltpu.VMEM((1, D_ATTN), F32), pltpu.VMEM((1, 128), F32),
                        pltpu.VMEM((BLK, 2 * D_KV), F32), pltpu.VMEM((per_step * BLK, D_ATTN), BF16)] + _rs_wout_scratch(gwo.shape),
        compiler_params=_params(44, ("arbitrary",)),
    )(q, kv, kv, ga, ya, doa, probs, sink_probs, norm_attn_out, gwo, small)


GBLK = 256
GSUB = 64
PAIR_RING = 4
LAG_PAIR, LAG_HOP1, LAG_HOP2 = 1, 7, 14


def _bwd_in(dpc, dqg, h, wt, x, norm_in, dx2, small):
    seq = x.shape[0]
    n_blk = D_IN_PROJ // GBLK
    per_chip = n_blk // N_CHIPS
    n_slots = (n_blk + 1) // 2
    n_sub = GBLK // GSUB
    chip_rows = D_IN_PROJ // N_CHIPS
    tile = TOK_TILE
    n_tiles = seq // tile
    n_steps = n_blk + max(n_tiles, LAG_HOP2)
    chunk = min(seq, 512)
    blk_q, blk_kv, blk_ga = ROW_Q // GBLK, ROW_KV // GBLK, ROW_GA // GBLK

    def block_of(i):
        k = i % N_CHIPS
        robin = per_chip * ((k % 2) * 2 + k // 2) + i // N_CHIPS
        if isinstance(i, int):
            return robin if i < per_chip * N_CHIPS else i
        return jnp.where(i < per_chip * N_CHIPS, robin, i)

    def owner_of(i):
        return (i // N_CHIPS) % 2

    def slot_of(i):
        return (i // (2 * N_CHIPS)) * N_CHIPS + i % N_CHIPS

    def body(dpc_ref, dqg_ref, wt_ref, h_ref, x_ref, g_ref, dx2_ref, small_ref, gx_ref, small_sum, gwt_sh,
             dh_acc, gni, keep, pbuf, xbuf, land, land2, small_land,
             pair_send, pair_recv, h1_send, h1_recv, h2_send, h2_recv, sw_send, sw_recv, sm_send, sm_recv, out_sem):
        step = pl.program_id(0)
        x_i, y_i, c = lax.axis_index("x"), lax.axis_index("y"), lax.axis_index("c")
        me = 4 * x_i + 2 * y_i + c
        j = 2 * x_i + y_i
        pa = (_xor(x_i, 1 - c), _xor(y_i, c), c)
        pb = (_xor(x_i, c), _xor(y_i, 1 - c), c)
        sib = (x_i, y_i, 1 - c)
        ja = 2 * pa[0] + pa[1]
        jb = 2 * pb[0] + pb[1]
        jd = 3 - j

        def remote(src, dst, send, recv, to):
            return pltpu.make_async_remote_copy(src_ref=src, dst_ref=dst, send_sem=send, recv_sem=recv,
                                                device_id=to, device_id_type=MESH)

        def piece(ref, slot, u, n):
            return ref.at[slot, pl.ds(u * GSUB, n * GSUB), :]

        def chip_rows_at(ref, local, n):
            return ref.at[pl.ds(pl.multiple_of(local, GSUB), n * GSUB), :]

        def pair_copy(i):
            slot = slot_of(i)
            return remote(pbuf.at[i % PAIR_RING], land.at[slot], pair_send.at[slot], pair_recv.at[slot], sib)

        def h1_copy(slot, u, n):
            k = slot * n_sub + u
            return remote(piece(xbuf, slot, u, n), piece(xbuf, slot, u, n), h1_send.at[k], h1_recv.at[k], pa)

        def h2_copy(slot, u, n, local):
            k = slot * n_sub + u
            return remote(piece(xbuf, slot, u, n), chip_rows_at(land2, local, n), h2_send.at[k], h2_recv.at[k], pb)

        def sw_copy(slot, u, n, local):
            k = slot * n_sub + u
            return remote(piece(keep, slot, u, n), chip_rows_at(gwt_sh, local, n), sw_send.at[k], sw_recv.at[k], sib)

        def owned(i):
            return (i >= 0) & (i < n_blk) & (owner_of(i) == c)

        def chip_of(blk, u):
            row = blk * GBLK + u * GSUB
            chip = row // chip_rows
            return chip, row - chip * chip_rows

        def pieces(blk):
            first, local = chip_of(blk, 0)
            whole = first == chip_of(blk, n_sub - 1)[0]
            if isinstance(blk, int):
                return [(True, 0, n_sub, first, local)] if whole else [(True, u, 1) + chip_of(blk, u) for u in range(n_sub)]
            return [(whole, 0, n_sub, first, local)] + [(jnp.logical_not(whole), u, 1) + chip_of(blk, u) for u in range(n_sub)]

        @pl.when(step == 0)
        def _():
            dh_acc[...] = jnp.zeros_like(dh_acc)
            gni[...] = jnp.zeros_like(gni)

        @pl.when(step < n_blk)
        def _():
            from_pc = block_of(step) < blk_q
            block = _tn(jnp.where(from_pc, dpc_ref[...], dqg_ref[...]), h_ref[...])
            for t in range(0, seq, chunk):
                d = jnp.where(from_pc, dpc_ref[t:t + chunk, :], dqg_ref[t:t + chunk, :])
                dh_acc[t:t + chunk, :] += _nn(d, wt_ref[...])

            @pl.when(owner_of(step) == c)
            def _():
                keep[slot_of(step)] = block

            @pl.when(owner_of(step) != c)
            def _():
                @pl.when(step >= 2 * PAIR_RING)
                def _():
                    pair_copy(step - 2 * PAIR_RING).wait_send()
                pbuf[step % PAIR_RING] = block.astype(BF16)
                pair_copy(step).start()

        i1 = step - LAG_PAIR

        @pl.when(owned(i1))
        def _():
            slot = slot_of(i1)
            pair_copy(i1).wait_recv()
            _accumulate(keep.at[slot], land.at[slot])
            for cond, u, n, chip, _ in pieces(block_of(i1)):
                @pl.when(cond & ((chip == ja) | (chip == jd)))
                def _(u=u, n=n):
                    _cast_rows(piece(keep, slot, u, n), piece(xbuf, slot, u, n))
                    h1_copy(slot, u, n).start()

        i2 = step - LAG_HOP1

        @pl.when(owned(i2))
        def _():
            slot = slot_of(i2)
            for cond, u, n, chip, local in pieces(block_of(i2)):
                @pl.when(cond & ((chip == j) | (chip == jb)))
                def _(u=u, n=n, chip=chip, local=local):
                    h1_copy(slot, u, n).wait_recv()
                    _accumulate(piece(keep, slot, u, n), piece(xbuf, slot, u, n))

                    @pl.when(chip == jb)
                    def _():
                        _cast_rows(piece(keep, slot, u, n), piece(xbuf, slot, u, n))
                        h2_copy(slot, u, n, local).start()

                @pl.when(cond & ((chip == ja) | (chip == jd)))
                def _(u=u, n=n):
                    h1_copy(slot, u, n).wait_send()

        i3 = step - LAG_HOP2

        @pl.when(owned(i3))
        def _():
            slot = slot_of(i3)
            for cond, u, n, chip, local in pieces(block_of(i3)):
                @pl.when(cond & (chip == j))
                def _(u=u, n=n, local=local):
                    h2_copy(slot, u, n, local).wait_recv()
                    _accumulate(piece(keep, slot, u, n), chip_rows_at(land2, local, n))
                    mine = pltpu.make_async_copy(piece(keep, slot, u, n), chip_rows_at(gwt_sh, local, n), out_sem.at[0])
                    mine.start()
                    sw_copy(slot, u, n, local).start()
                    mine.wait()

                @pl.when(cond & (chip == jb))
                def _(u=u, n=n, local=local):
                    h2_copy(slot, u, n, local).wait_send()

        e = step - n_blk

        @pl.when((e >= 0) & (e < n_tiles))
        def _():
            dh = dh_acc[pl.ds(pl.multiple_of(e * tile, tile), tile), :]
            xv = x_ref[...]
            r = _rstd(xv)
            xhat = xv * r
            gni[...] += jnp.sum(dh * xhat, axis=0, keepdims=True)
            gx_ref[...] = _rms_bwd(dh * g_ref[...], xhat, r) + dx2_ref[...]

        @pl.when(step == n_steps - 1)
        def _():
            small_land[me] = small_ref[...]
            small_land[me, SMALL_NORM_IN:SMALL_NORM_IN + 1, :] = gni[...]
            others = [(dx, dy, dc) for dx in (0, 1) for dy in (0, 1) for dc in (0, 1)][1:]
            sends = [remote(small_land.at[me], small_land.at[me], sm_send.at[k], sm_recv.at[k],
                            (_xor(x_i, dx), _xor(y_i, dy), _xor(c, dc))) for k, (dx, dy, dc) in enumerate(others)]
            for cp in sends:
                cp.start()
            for i in range(n_blk):
                if i + 2 * PAIR_RING >= n_blk:
                    @pl.when(owner_of(i) != c)
                    def _(i=i):
                        pair_copy(i).wait_send()
                for _, u, n, chip, local in pieces(block_of(i)):
                    @pl.when((j == chip) & (c == owner_of(i)))
                    def _(i=i, u=u, n=n, local=local):
                        sw_copy(slot_of(i), u, n, local).wait_send()

                    @pl.when((j == chip) & (c != owner_of(i)))
                    def _(i=i, u=u, n=n, local=local):
                        sw_copy(slot_of(i), u, n, local).wait_recv()
            for cp in sends:
                cp.wait_recv()
            total = small_land[0]
            for dev in range(1, 8):
                total = total + small_land[dev]
            small_sum[...] = total
            for cp in sends:
                cp.wait_send()

    def blk_at(i):
        return block_of(jnp.clip(i, 0, n_blk - 1))

    last_pc_step = max(i for i in range(n_blk) if block_of(i) < blk_q)

    def next_block(i, in_pc):
        i = jnp.clip(i, 0, n_blk - 1)
        step = jnp.full_like(i, last_pc_step if in_pc else n_blk - 1)
        for ahead in reversed(range(N_CHIPS)):
            cand = jnp.minimum(i + ahead, n_blk - 1)
            step = jnp.where((block_of(cand) < blk_q) == in_pc, cand, step)
        return block_of(step)

    def dqg_block(i):
        b = next_block(i, False)
        q_blk = jnp.clip(b - blk_q, 0, blk_kv - blk_q - 1)
        ga_blk = (D_ATTN // GBLK) + jnp.clip(b - blk_ga, 0, n_blk - blk_ga - 1)
        return jnp.where(b < blk_kv, q_blk, jnp.where(b == blk_kv, 2 * D_ATTN // GBLK, ga_blk))

    def tok(i):
        return (jnp.clip(i - n_blk, 0, n_tiles - 1), 0)

    n_piece = n_slots * n_sub
    dma = pltpu.SemaphoreType.DMA
    return pl.pallas_call(
        body, name="bwd_in", grid=(n_steps,),
        out_shape=(jax.ShapeDtypeStruct((seq, D_MODEL), F32), jax.ShapeDtypeStruct(small.shape, F32),
                   jax.ShapeDtypeStruct((chip_rows, D_MODEL), F32)),
        in_specs=[pl.BlockSpec((seq, GBLK), lambda i: (0, next_block(i, True))),
                  pl.BlockSpec((seq, GBLK), lambda i: (0, dqg_block(i))),
                  pl.BlockSpec((GBLK, D_MODEL), lambda i: (blk_at(i), 0)),
                  _resident(h.shape),
                  pl.BlockSpec((tile, D_MODEL), tok), _resident((1, D_MODEL)), pl.BlockSpec((tile, D_MODEL), tok),
                  _resident(small.shape)],
        out_specs=(pl.BlockSpec((tile, D_MODEL), tok), pl.BlockSpec(small.shape, lambda i: (0, 0)),
                   pl.BlockSpec(memory_space=pl.ANY)),
        scratch_shapes=[pltpu.VMEM((seq, D_MODEL), F32), pltpu.VMEM((1, D_MODEL), F32),
                        pltpu.VMEM((n_slots, GBLK, D_MODEL), F32), pltpu.VMEM((PAIR_RING, GBLK, D_MODEL), BF16),
                        pltpu.VMEM((n_slots, GBLK, D_MODEL), BF16), pltpu.VMEM((n_slots, GBLK, D_MODEL), BF16),
                        pltpu.VMEM((chip_rows, D_MODEL), BF16), pltpu.VMEM((8,) + small.shape, F32),
                        dma((n_slots,)), dma((n_slots,)), dma((n_piece,)), dma((n_piece,)), dma((n_piece,)),
                        dma((n_piece,)), dma((n_piece,)), dma((n_piece,)), dma((7,)), dma((7,)), dma((1,))],
        compiler_params=_params(62, ("arbitrary",)),
    )(dpc, dqg, wt, h, x, norm_in, dx2, small)


def _accumulate(dst_ref, src_ref, rows=16):
    def step(i, carry):
        sl = pl.ds(pl.multiple_of(i * rows, rows), rows)
        dst_ref[sl, :] = dst_ref[sl, :] + src_ref[sl, :].astype(F32)
        return carry
    lax.fori_loop(0, dst_ref.shape[0] // rows, step, 0)


def _rs_wout_scratch(gwo_shape):
    o_half, width = gwo_shape[0] // N_CHIPS // 2, gwo_shape[1]
    return [pltpu.VMEM((4, o_half, width), F32), pltpu.VMEM((4, o_half, width), BF16),
            pltpu.VMEM((4, o_half, width), BF16), pltpu.VMEM((2, o_half, width), BF16),
            pltpu.VMEM((o_half, width), BF16),
            pltpu.SemaphoreType.DMA((8,)), pltpu.SemaphoreType.DMA((8,)), pltpu.SemaphoreType.DMA((4,))]


def _rs_wout_stages(gwo_ref, gwo_sh, acc_o, sb_o, r1o, r2o, r3o, send_sems, recv_sems, local_sems):
    o_rows = gwo_ref.shape[0] // N_CHIPS
    o_half = o_rows // 2
    x, y, c = lax.axis_index("x"), lax.axis_index("y"), lax.axis_index("c")
    j = 2 * x + y
    pa = (_xor(x, 1 - c), _xor(y, c), c)
    pb = (_xor(x, c), _xor(y, 1 - c), c)
    sib = (x, y, 1 - c)
    ja = 2 * pa[0] + pa[1]
    jb = 2 * pb[0] + pb[1]
    jd = 3 - j
    order = (ja, jd, jb, j)
    sib_order = (jb, jd, ja, j)

    def rcopy(k, src, dst, to):
        return pltpu.make_async_remote_copy(src_ref=src, dst_ref=dst, send_sem=send_sems.at[k],
                                            recv_sem=recv_sems.at[k], device_id=to, device_id_type=MESH)

    def o_rows_of(chip, half):
        return gwo_ref.at[pl.ds(pl.multiple_of(chip * o_rows + half * o_half, 8), o_half), :]

    def load_all(chips, half):
        cps = [pltpu.make_async_copy(o_rows_of(chip, half), acc_o.at[s], local_sems.at[s]) for s, chip in enumerate(chips)]
        for cp in cps:
            cp.start()
        return cps

    def pair_send(s):
        return rcopy(s, sb_o.at[s], r1o.at[s], sib)

    def out_half(half):
        return gwo_sh.at[pl.ds(pl.multiple_of(half * o_half, 8), o_half), :]

    hop1 = [rcopy(4 + s, sb_o.at[s], r2o.at[s], pa) for s in range(2)]
    hop2 = rcopy(6, sb_o.at[2], r3o, pb)
    swap = rcopy(7, acc_o.at[3], out_half(c), sib)
    mine = pltpu.make_async_copy(acc_o.at[3], out_half(c), local_sems.at[0])

    def resend(s, copy):
        pair_send(s).wait_send()
        _cast_rows(acc_o.at[s], sb_o.at[s])
        copy.start()

    def stage_pair():
        for s, cp in enumerate(load_all(sib_order, 1 - c)):
            cp.wait()
            _cast_rows(acc_o.at[s], sb_o.at[s])
            pair_send(s).start()

    def stage_hop1():
        for s, cp in enumerate(load_all(order, c)):
            cp.wait()
            pair_send(s).wait_recv()
            _accumulate(acc_o.at[s], r1o.at[s])
            if s < 2:
                resend(s, hop1[s])

    def stage_hop2():
        hop1[1].wait_recv()
        _accumulate(acc_o.at[2], r2o.at[1])
        resend(2, hop2)
        hop1[0].wait_recv()
        _accumulate(acc_o.at[3], r2o.at[0])

    def stage_final():
        hop2.wait_recv()
        _accumulate(acc_o.at[3], r3o)
        mine.start()
        swap.start()

    def stage_drain():
        rcopy(7, acc_o.at[3], out_half(1 - c), sib).wait_recv()
        for cp in [pair_send(3)] + hop1 + [hop2, swap]:
            cp.wait_send()
        mine.wait()

    return stage_pair, stage_hop1, stage_hop2, stage_final, stage_drain


def _adamw_big(groups, passed):
    arrays = [a for group in groups for a in group[:4]]
    steps = [group[0].shape[0] // group[4] for group in groups]
    first = [sum(steps[:k]) for k in range(len(steps) + 1)]
    n = len(arrays)

    def body(*refs):
        ins, passed_in, outs, passed_out = refs[:n], refs[n], refs[n + 1:2 * n + 1], refs[2 * n + 1]
        i = pl.program_id(0)
        for k in range(len(groups)):
            @pl.when((i >= first[k]) & (i < first[k + 1]))
            def _(k=k):
                w_ref, g_ref, m_ref, v_ref = ins[4 * k:4 * k + 4]
                _adamw_update(w_ref, g_ref[...], m_ref, v_ref, *outs[4 * k:4 * k + 4])

        @pl.when(i < steps[0])
        def _():
            passed_out[...] = passed_in[...]

    def spec(k, rows, width):
        return pl.BlockSpec((rows, width), lambda i: (jnp.clip(i - first[k], 0, steps[k] - 1), 0))

    specs = [spec(k, group[4], group[0].shape[1]) for k, group in enumerate(groups) for _ in range(4)]
    specs.append(spec(0, passed.shape[0] // steps[0], passed.shape[1]))
    out = pl.pallas_call(
        body, name="adamw_big", grid=(first[-1],),
        out_shape=tuple(jax.ShapeDtypeStruct(a.shape, a.dtype) for a in arrays + [passed]),
        in_specs=specs, out_specs=tuple(specs),
        compiler_params=_params(40, ("arbitrary",)),
    )(*arrays, passed)
    return [tuple(out[4 * k:4 * k + 4]) for k in range(len(groups))], out[-1]


def _adamw_update(w_ref, gv, m_ref, v_ref, go_ref, d_ref, nm_ref, nv_ref, at=...):
    go_ref[at] = gv
    nm = ADAM_B1 * m_ref[at] + (1.0 - ADAM_B1) * gv
    nv = ADAM_B2 * v_ref[at] + (1.0 - ADAM_B2) * (gv * gv)
    m_hat = nm / (1.0 - ADAM_B1 ** ADAM_STEP)
    v_hat = nv / (1.0 - ADAM_B2 ** ADAM_STEP)
    d_ref[at] = -ADAM_LR * (m_hat / (jnp.sqrt(v_hat) + ADAM_EPS) + ADAM_WD * w_ref[at])
    nm_ref[at] = nm
    nv_ref[at] = nv


def _adamw_small(chip, small_sum, weights, grads_of, ms, vs):
    n = len(weights)

    def body(chip_ref, small_ref, *refs):
        ins, outs, loss_ref = refs[:3 * n], refs[3 * n:-1], refs[-1]
        for k in range(n):
            w_ref, m_ref, v_ref = ins[3 * k:3 * k + 3]
            for at, gv in grads_of[k](small_ref, chip_ref):
                _adamw_update(w_ref, gv, m_ref, v_ref, *outs[4 * k:4 * k + 4], at=at)
        loss_ref[...] = small_ref[SMALL_MISC:SMALL_MISC + 1, SMALL_LOSS_LANE:SMALL_LOSS_LANE + 1]

    flat = [a for group in zip(weights, ms, vs) for a in group]
    vmem = pl.BlockSpec(memory_space=pltpu.VMEM)
    out = pl.pallas_call(
        body, name="adamw_small",
        out_shape=tuple(jax.ShapeDtypeStruct(w.shape, F32) for w in weights for _ in range(4))
        + (jax.ShapeDtypeStruct((1, 1), F32),),
        in_specs=[pl.BlockSpec(memory_space=pltpu.SMEM)] + [vmem] * (1 + 3 * n), out_specs=(vmem,) * (4 * n + 1),
    )(chip, small_sum, *flat)
    return [tuple(out[4 * k:4 * k + 4]) for k in range(n)], out[-1][0, 0]


def kernel(x, norm_in, w_in, conv_w, attn_sinks, norm_conv_out, norm_attn_out, w_out, norm_final, loss_target, m_norm_in, m_w_in, m_conv_w, m_attn_sinks, m_norm_conv_out, m_norm_attn_out, m_w_out, m_norm_final, v_norm_in, v_w_in, v_conv_w, v_attn_sinks, v_norm_conv_out, v_norm_attn_out, v_w_out, v_norm_final):
    chip = 2 * lax.axis_index("x") + lax.axis_index("y")
    xs, target = x[0], loss_target[0]
    norm_final2 = norm_final.reshape(1, D_MODEL)
    w_in_t, m_w_in_t, v_w_in_t = w_in[0].T, m_w_in[0].T, v_w_in[0].T

    def from_hbm(*arrays):
        return tuple(pltpu.with_memory_space_constraint(a, pltpu.HBM) for a in arrays)

    h, pc, q, kv, ga, oc, ya, oa, probs, sink_probs, wt, cw, wo = _fwd_in(
        xs, norm_in, w_in_t, w_out[0], conv_w.transpose(1, 0, 2), norm_conv_out, attn_sinks, norm_attn_out)
    dx2, doa, gwo, dpc, small = _out_proj_loss(xs, oc, oa, wo, norm_final2, target, pc, cw, norm_conv_out)
    q, kv, ga, ya, doa, probs, gwo, small = from_hbm(q, kv, ga, ya, doa, probs, gwo, small)
    dqg, small, gwo_sh = _attn_bwd(q, kv, ga, ya, doa, probs, sink_probs, norm_attn_out, gwo, small)
    grad_x, small_sum, gwt_sh = _bwd_in(dpc, dqg, h, wt, xs, norm_in, dx2, small)

    (up_w_in, up_w_out), grad_x = _adamw_big(
        [from_hbm(w_in_t, gwt_sh, m_w_in_t, v_w_in_t) + (200,), from_hbm(w_out[0], gwo_sh, m_w_out[0], v_w_out[0]) + (128,)],
        *from_hbm(grad_x))
    up_w_in = tuple(o.T[None] for o in up_w_in)
    up_w_out = tuple(o[None] for o in up_w_out)

    def row_of(r):
        return lambda small_ref, chip_ref: [(..., small_ref[r:r + 1, :])]

    def sink_lanes(small_ref, chip_ref):
        return [(..., small_ref[SMALL_MISC:SMALL_MISC + 1, 0:N_HEADS])]

    def conv_taps(small_ref, chip_ref):
        width = D_CONV // N_CHIPS
        cols = pl.ds(pl.multiple_of(chip_ref[0] * width, width), width)
        return [(k, small_ref[pl.ds(SMALL_CONV_W + k, 1), cols]) for k in range(conv_w.shape[1])]

    def small_view(a):
        return a.transpose(1, 0, 2) if a.ndim == 3 else a.reshape(-1, a.shape[-1])

    small_w = (norm_in, conv_w, attn_sinks, norm_conv_out, norm_attn_out, norm_final)
    small_m = (m_norm_in, m_conv_w, m_attn_sinks, m_norm_conv_out, m_norm_attn_out, m_norm_final)
    small_v = (v_norm_in, v_conv_w, v_attn_sinks, v_norm_conv_out, v_norm_attn_out, v_norm_final)
    small_g = (row_of(SMALL_NORM_IN), conv_taps, sink_lanes, row_of(SMALL_NORM_CONV), row_of(SMALL_NORM_ATTN),
               row_of(SMALL_NORM_FINAL))
    w_views, m_views, v_views = (tuple(small_view(a) for a in group) for group in (small_w, small_m, small_v))
    up_small, loss = _adamw_small(chip.astype(jnp.int32).reshape(1), small_sum, w_views, small_g, m_views, v_views)
    up_small = [tuple(o.transpose(1, 0, 2) if w.ndim == 3 else o.reshape(w.shape) for o in up)
                for up, w in zip(up_small, small_w)]
    up_norm_in, up_conv_w, up_sinks, up_norm_conv, up_norm_attn, up_norm_final = up_small
    updates = (up_norm_in, up_w_in, up_conv_w, up_sinks, up_norm_conv, up_norm_attn, up_w_out, up_norm_final)
    grads_out, deltas, new_m, new_v = zip(*updates)
    return (loss, grad_x[None], *grads_out, *deltas, *new_m, *new_v)
```

```python
import jax
import jax.numpy as jnp
from jax import lax
from jax.experimental import pallas as pl
from jax.experimental.pallas import tpu as pltpu

F32 = jnp.float32
BF16 = jnp.bfloat16
MESH = pl.DeviceIdType.MESH

D_MODEL = 1024
D_CONV = 1024
D_ATTN = 1024
D_KV = 128
D_QG = 2 * D_ATTN + 2 * D_KV
D_MIX = D_CONV + D_ATTN
D_PC = 4 * D_CONV
D_IN_PROJ = D_PC + 2 * D_ATTN + 2 * D_KV
ROW_Q = D_PC
ROW_KV = ROW_Q + D_ATTN
ROW_GA = ROW_KV + 2 * D_KV
N_HEADS = 16
HEAD_DIM = 64
HEADS_PER_KV = 8
BLK = 128
N_CHIPS = 4
RMS_EPS = 1e-5
SCALE = HEAD_DIM ** -0.5
SLOPES = tuple(2.0 ** (-8.0 * (h + 1) / N_HEADS) for h in range(N_HEADS))

ADAM_LR, ADAM_B1, ADAM_B2, ADAM_EPS, ADAM_WD, ADAM_STEP = 0.001, 0.9, 0.999, 1e-08, 0.01, 10

SMALL_ROWS = 8
SMALL_NORM_IN, SMALL_NORM_CONV, SMALL_NORM_ATTN, SMALL_NORM_FINAL, SMALL_CONV_W, SMALL_MISC = 0, 1, 2, 3, 4, 7
SMALL_LOSS_LANE = N_HEADS

TOK_TILE = 256
PC_PIECE = 512
MIB = 1 << 20


def _params(vmem_mib, semantics=None):
    return pltpu.CompilerParams(dimension_semantics=semantics, vmem_limit_bytes=vmem_mib * MIB)


def _nn(a, b):
    return jnp.dot(a, b, preferred_element_type=F32)


def _nt(a, b):
    return lax.dot_general(a, b, (((1,), (1,)), ((), ())), preferred_element_type=F32)


def _tn(a, b):
    return lax.dot_general(a, b, (((0,), (0,)), ((), ())), preferred_element_type=F32)


def _rstd(v):
    return lax.rsqrt(jnp.mean(v * v, axis=-1, keepdims=True) + RMS_EPS)


def _rms_bwd(g, xhat, rstd):
    return rstd * (g - xhat * jnp.mean(g * xhat, axis=-1, keepdims=True))


def _silu_and_grad(g):
    s = jax.nn.sigmoid(g)
    return g * s, s * (1.0 + g * (1.0 - s))


def _resident(shape):
    return pl.BlockSpec(shape, lambda *_: (0,) * len(shape), pipeline_mode=pl.Buffered(1))


def _xor(a, b):
    return a + b - 2 * a * b


def _cast_rows(src_ref, dst_ref, rows=32):
    def step(i, carry):
        sl = pl.ds(pl.multiple_of(i * rows, rows), rows)
        dst_ref[sl, :] = src_ref[sl, :].astype(dst_ref.dtype)
        return carry
    lax.fori_loop(0, src_ref.shape[0] // rows, step, 0)


def _ag_scratch(shard_shape):
    rows, width = shard_shape
    return [pltpu.VMEM((rows, width), F32), pltpu.VMEM((rows, width), BF16), pltpu.VMEM((3, rows // 2, width), BF16),
            pltpu.SemaphoreType.DMA((6,)), pltpu.SemaphoreType.DMA((6,)), pltpu.SemaphoreType.DMA((4,))]


def _ag_stages(sh_ref, out, f32_buf, own, land, send_sems, recv_sems, local_sems):
    rows = sh_ref.shape[0]
    half = rows // 2
    x, y, c = lax.axis_index("x"), lax.axis_index("y"), lax.axis_index("c")
    j = 2 * x + y
    p1 = (_xor(x, c), _xor(y, 1 - c), c)
    p2 = (_xor(x, 1 - c), _xor(y, c), c)
    sib = (x, y, 1 - c)
    j1 = 2 * p1[0] + p1[1]
    j2 = 2 * p2[0] + p2[1]
    j3 = 3 - j

    def rows_of(chip, hf):
        return out.at[pl.ds(pl.multiple_of(chip * rows + hf * half, 16), half), :]

    def rcopy(k, src, dst, to):
        return pltpu.make_async_remote_copy(src_ref=src, dst_ref=dst, send_sem=send_sems.at[k],
                                            recv_sem=recv_sems.at[k], device_id=to, device_id_type=MESH)

    my_half = own.at[pl.ds(pl.multiple_of(c * half, 16), half), :]
    hop1 = rcopy(0, my_half, land.at[0], p1)
    hop2_own = rcopy(1, my_half, land.at[1], p2)
    hop2_fwd = rcopy(2, land.at[0], land.at[2], p2)
    swaps = [rcopy(3 + s, land.at[s], rows_of(chip, c), sib) for s, chip in enumerate((j1, j2, j3))]
    keeps = [pltpu.make_async_copy(land.at[s], rows_of(chip, c), local_sems.at[1 + s]) for s, chip in enumerate((j1, j2, j3))]
    load = pltpu.make_async_copy(sh_ref, f32_buf, local_sems.at[0])
    own_out = pltpu.make_async_copy(own, out.at[pl.ds(pl.multiple_of(j * rows, 16), rows), :], local_sems.at[0])

    def stage_send():
        load.start()
        load.wait()
        _cast_rows(f32_buf, own)
        own_out.start()
        hop1.start()

    def stage_forward():
        hop1.wait_recv()
        hop2_own.start()
        hop2_fwd.start()
        swaps[0].start()
        keeps[0].start()

    def stage_publish():
        hop2_own.wait_recv()
        swaps[1].start()
        keeps[1].start()
        hop2_fwd.wait_recv()
        swaps[2].start()
        keeps[2].start()

    def stage_drain():
        for s, chip in enumerate((j2, j1, j3)):
            rcopy(3 + s, my_half, rows_of(chip, 1 - c), sib).wait_recv()
        for cp in [hop1, hop2_own, hop2_fwd] + swaps:
            cp.wait_send()
        for cp in [own_out] + keeps:
            cp.wait()

    return stage_send, stage_forward, stage_publish, stage_drain


AG_CAST_ROWS = 400


def _gather_resident(sh_ref, out, f32_buf, send_sems, recv_sems, local_sems):
    rows = sh_ref.shape[0]
    half = rows // 2
    x, y, c = lax.axis_index("x"), lax.axis_index("y"), lax.axis_index("c")
    j = 2 * x + y
    p1 = (_xor(x, c), _xor(y, 1 - c), c)
    p2 = (_xor(x, 1 - c), _xor(y, c), c)
    sib = (x, y, 1 - c)
    j1 = 2 * p1[0] + p1[1]
    j2 = 2 * p2[0] + p2[1]
    j3 = 3 - j

    def rows_of(chip, hf):
        return out.at[pl.ds(pl.multiple_of(chip * rows + hf * half, 16), half), :]

    def send(k, chip, to):
        return pltpu.make_async_remote_copy(src_ref=rows_of(chip, c), dst_ref=rows_of(chip, c), send_sem=send_sems.at[k],
                                            recv_sem=recv_sems.at[k], device_id=to, device_id_type=MESH)

    per_half = half // AG_CAST_ROWS

    def cast_own(chunk):
        lo = pl.multiple_of(chunk * AG_CAST_ROWS, 16)
        load = pltpu.make_async_copy(sh_ref.at[pl.ds(lo, AG_CAST_ROWS), :], f32_buf, local_sems.at[0])
        load.start()
        load.wait()
        _cast_rows(f32_buf, out.at[pl.ds(pl.multiple_of(j * rows + lo, 16), AG_CAST_ROWS), :], rows=16)

    for k in range(per_half):
        cast_own(c * per_half + k)
    hop1 = send(0, j, p1)
    hop1.start()
    for k in range(per_half):
        cast_own((1 - c) * per_half + k)
    hop1.wait_recv()
    sends = [hop1, send(1, j, p2), send(2, j1, p2), send(3, j1, sib)]
    for cp in sends[1:]:
        cp.start()
    sends[1].wait_recv()
    sends.append(send(4, j2, sib))
    sends[-1].start()
    sends[2].wait_recv()
    sends.append(send(5, j3, sib))
    sends[-1].start()
    for k in (3, 4, 5):
        send(k, j, sib).wait_recv()
    for cp in sends:
        cp.wait_send()


def _fwd_in(x, norm_in, wt_sh, wo_sh, cw_sh, norm_conv_out, sinks, norm_attn_out):
    seq = x.shape[0]
    tile = TOK_TILE
    n_tiles = seq // tile
    stage_steps = (0, n_tiles // 4, (5 * n_tiles) // 8, n_tiles - 1)
    blocks = tile // BLK
    cw_cols = cw_sh.shape[-1]

    def body(x_ref, g_ref, wtsh_ref, wo_ref, cwsh_ref, gn_ref, sink_ref, gna_ref,
             h_ref, pc_ref, q_ref, kv_ref, ga_ref, oc_ref, ya_ref, oa_ref, pr_ref, sp_ref, wt_out, cw_ref, wo_out,
             zbuf, kv_last, wt_ref, f32_buf, cw_land, wt_send, wt_recv, wt_local, cw_send, cw_recv, cw_local, *ag_scratch):
        step = pl.program_id(0)
        to_hbm = pltpu.make_async_copy(wt_ref, wt_out, wt_local.at[0])

        @pl.when(step == 0)
        def _():
            zbuf[0:8, :] = jnp.zeros((8, D_CONV), F32)
            kv_last[...] = jnp.zeros_like(kv_last)
            x_i, y_i, c = lax.axis_index("x"), lax.axis_index("y"), lax.axis_index("c")
            j = 2 * x_i + y_i
            p1 = (_xor(x_i, c), _xor(y_i, 1 - c), c)
            p2 = (_xor(x_i, 1 - c), _xor(y_i, c), c)
            j1 = 2 * p1[0] + p1[1]

            def cw_copy(k, src, chip, to):
                return pltpu.make_async_remote_copy(src_ref=src, dst_ref=cw_land.at[chip], send_sem=cw_send.at[k],
                                                    recv_sem=cw_recv.at[k], device_id=to, device_id_type=MESH)

            mine = pltpu.make_async_copy(cwsh_ref, cw_land.at[j], cw_local.at[0])
            mine.start()
            first = cw_copy(0, cwsh_ref, j, p1)
            first.start()
            _gather_resident(wtsh_ref, wt_ref, f32_buf, wt_send, wt_recv, wt_local)
            to_hbm.start()
            first.wait_recv()
            second = [cw_copy(1, cwsh_ref, j, p2), cw_copy(2, cw_land.at[j1], j1, p2)]
            for cp in second:
                cp.start()
            for cp in second:
                cp.wait_recv()
            for cp in [first] + second:
                cp.wait_send()
            mine.wait()
            for chip in range(N_CHIPS):
                for tap in range(cw_sh.shape[0]):
                    cw_ref[tap:tap + 1, chip * cw_cols:(chip + 1) * cw_cols] = cw_land[chip, tap]

        stages = _ag_stages(wo_ref, wo_out, *ag_scratch)
        for at, stage in zip(stage_steps[:-1], stages[:-1]):
            pl.when(step == at)(stage)

        def attention(b):
            rows = pl.ds(b * BLK, BLK)
            kv_prev = kv_last if b == 0 else kv_ref.at[pl.ds((b - 1) * BLK, BLK), :]
            return _attn_forward(q_ref.at[rows, :], kv_ref.at[rows, :], kv_prev, ga_ref.at[rows, :], sink_ref, gna_ref,
                                 _band_geometry(step * blocks + b), ya_ref.at[rows, :], oa_ref.at[rows, :],
                                 pr_ref.at[rows, :], sp_ref.at[rows, :])

        xv = x_ref[...]
        h = (xv * _rstd(xv) * g_ref[...]).astype(BF16)
        h_ref[...] = h
        q_ref[...] = _nt(h, wt_ref[ROW_Q:ROW_KV, :])
        kv_ref[...] = _nt(h, wt_ref[ROW_KV:ROW_GA, :])
        ga_ref[...] = _nt(h, wt_ref[ROW_GA:D_IN_PROJ, :])
        attention_blocks = [attention(b) for b in range(blocks)]
        for lo in range(0, D_PC, PC_PIECE):
            pc_ref[:, lo:lo + PC_PIECE] = _nt(h, wt_ref[lo:lo + PC_PIECE, :])
            for stages_of_block in attention_blocks:
                next(stages_of_block, None)
        for stages_of_block in attention_blocks:
            for _ in stages_of_block:
                pass
        kv_last[...] = kv_ref[tile - BLK:tile, :]

        cb, _, _, _, _, _, conv = _conv_core(pc_ref, zbuf, cw_ref)
        yc = cb * conv
        silu, _ = _silu_and_grad(pc_ref[:, 3 * D_CONV:4 * D_CONV])
        oc_ref[...] = (yc * _rstd(yc) * gn_ref[...] * silu).astype(BF16)
        zbuf[0:8, :] = zbuf[tile:tile + 8, :]

        @pl.when(step == stage_steps[-1])
        def _():
            stages[-1]()
            to_hbm.wait()

    def row(width):
        return pl.BlockSpec((tile, width), lambda i: (i, 0))

    any_spec = pl.BlockSpec(memory_space=pl.ANY)
    dma = pltpu.SemaphoreType.DMA
    wt_shape = (N_CHIPS * wt_sh.shape[0], wt_sh.shape[1])
    cw_shape = (cw_sh.shape[0], N_CHIPS * cw_cols)
    return pl.pallas_call(
        body, name="fwd_in", grid=(n_tiles,),
        out_shape=(jax.ShapeDtypeStruct((seq, D_MODEL), BF16), jax.ShapeDtypeStruct((seq, D_PC), F32),
                   jax.ShapeDtypeStruct((seq, D_ATTN), F32), jax.ShapeDtypeStruct((seq, 2 * D_KV), F32),
                   jax.ShapeDtypeStruct((seq, D_ATTN), F32), jax.ShapeDtypeStruct((seq, D_CONV), BF16),
                   pltpu.HBM((seq, D_ATTN), F32), pltpu.HBM((seq, D_ATTN), BF16),
                   pltpu.HBM((seq, N_HEADS * BLK), BF16), pltpu.HBM((seq, 128), F32),
                   pltpu.HBM(wt_shape, BF16), jax.ShapeDtypeStruct(cw_shape, F32),
                   jax.ShapeDtypeStruct((N_CHIPS * wo_sh.shape[0], wo_sh.shape[1]), BF16)),
        in_specs=[row(D_MODEL), _resident((1, D_MODEL)), any_spec, any_spec, any_spec, _resident((1, D_CONV)),
                  pl.BlockSpec(memory_space=pltpu.SMEM), _resident((1, D_ATTN))],
        out_specs=(row(D_MODEL), row(D_PC), row(D_ATTN), row(2 * D_KV), row(D_ATTN), row(D_CONV), row(D_ATTN),
                   row(D_ATTN), row(N_HEADS * BLK), row(128), any_spec, pl.BlockSpec(cw_shape, lambda i: (0, 0)),
                   any_spec),
        scratch_shapes=[pltpu.VMEM((tile + 8, D_CONV), F32), pltpu.VMEM((BLK, 2 * D_KV), F32),
                        pltpu.VMEM(wt_shape, BF16), pltpu.VMEM((AG_CAST_ROWS, wt_sh.shape[1]), F32),
                        pltpu.VMEM((N_CHIPS,) + cw_sh.shape, F32),
                        dma((6,)), dma((6,)), dma((1,)), dma((3,)), dma((3,)), dma((1,))] + _ag_scratch(wo_sh.shape),
        compiler_params=_params(62, ("arbitrary",)),
    )(x, norm_in, wt_sh, wo_sh, cw_sh, norm_conv_out, sinks, norm_attn_out)


def _conv_core(pc_ref, zbuf, cw_ref):
    tile = pc_ref.shape[0]
    cb = pc_ref[:, 0:D_CONV]
    cc = pc_ref[:, D_CONV:2 * D_CONV]
    cu = pc_ref[:, 2 * D_CONV:3 * D_CONV]
    z = cc * cu
    zbuf[8:tile + 8, :] = z
    z1 = zbuf[7:tile + 7, :]
    z2 = zbuf[6:tile + 6, :]
    conv = cw_ref[0:1, :] * z2 + cw_ref[1:2, :] * z1 + cw_ref[2:3, :] * z
    return cb, cc, cu, z, z1, z2, conv


def _band_geometry(block_index):
    qi = lax.broadcasted_iota(jnp.int32, (BLK, BLK), 0)
    kp = lax.broadcasted_iota(jnp.int32, (BLK, BLK), 1)
    use_cur = kp <= qi
    dist = jnp.where(use_cur, qi - kp, qi - kp + BLK).astype(F32)
    valid = use_cur | (block_index > 0)
    return use_cur, dist, valid


def _block_diag(cur, prev, group):
    lane = lax.broadcasted_iota(jnp.int32, cur.shape, 1)

    def halves(t):
        other = pltpu.roll(t, 64, 1)
        lo, hi = (t, other) if group == 0 else (other, t)
        return jnp.where(lane < 64, lo, 0.0), jnp.where(lane >= 64, hi, 0.0)

    return jnp.concatenate(halves(cur) + halves(prev), axis=0).astype(BF16)


def _merge(s4, use_cur):
    return (jnp.where(use_cur, s4[:, 0:BLK], s4[:, 2 * BLK:3 * BLK]),
            jnp.where(use_cur, s4[:, BLK:2 * BLK], s4[:, 3 * BLK:4 * BLK]))


def _split(a, b, use_cur):
    return jnp.concatenate([jnp.where(use_cur, a, 0.0), jnp.where(use_cur, b, 0.0),
                            jnp.where(use_cur, 0.0, a), jnp.where(use_cur, 0.0, b)], axis=1)


def _softmax_head(s, head, sink, dist, valid):
    sc = jnp.where(valid, s - SLOPES[head] * dist, -jnp.inf)
    m = jnp.maximum(jnp.max(sc, axis=-1, keepdims=True), sink)
    p = jnp.exp(sc - m)
    es = jnp.exp(sink - m)
    inv = 1.0 / (jnp.sum(p, axis=-1, keepdims=True) + es)
    return p * inv, es * inv


def _attn_operands(q_ref, kvc_ref, kvp_ref):
    groups = range(N_HEADS // HEADS_PER_KV)
    kbd = [_block_diag(kvc_ref[:, 0:D_KV], kvp_ref[:, 0:D_KV], g) for g in groups]
    vbd = [_block_diag(kvc_ref[:, D_KV:2 * D_KV], kvp_ref[:, D_KV:2 * D_KV], g) for g in groups]
    qps = [(q_ref[:, j * 128:(j + 1) * 128] * SCALE).astype(BF16) for j in range(N_HEADS // 2)]
    return qps, kbd, vbd


def _attn_forward(q_ref, kvc_ref, kvp_ref, ga_ref, sink_ref, gn_ref, geometry, ya_ref, oa_ref, pr_ref, sp_ref):
    use_cur, dist, valid = geometry
    _, kbd, vbd = operands = _attn_operands(q_ref, kvc_ref, kvp_ref)
    yield
    scores = []
    for j, qp in enumerate(operands[0]):
        scores += _merge(_nt(qp, kbd[j // 4]), use_cur)
    yield
    sinks = [sink_ref[0, h] for h in range(N_HEADS)]
    scores = [jnp.where(valid, s - SLOPES[h] * dist, -jnp.inf) for h, s in enumerate(scores)]
    maxes = [jnp.maximum(jnp.max(s, axis=-1, keepdims=True), sinks[h]) for h, s in enumerate(scores)]
    yield
    exps = [jnp.exp(s - m) for s, m in zip(scores, maxes)]
    sink_exps = [jnp.exp(sinks[h] - m) for h, m in enumerate(maxes)]
    yield
    invs = [1.0 / (jnp.sum(e, axis=-1, keepdims=True) + se) for e, se in zip(exps, sink_exps)]
    probs = [e * inv for e, inv in zip(exps, invs)]
    pr_ref[...] = jnp.concatenate(probs, axis=1).astype(BF16)
    lane = lax.broadcasted_iota(jnp.int32, (BLK, 128), 1)
    sp_ref[...] = sum(jnp.where(lane == h, se * inv, 0.0) for h, (se, inv) in enumerate(zip(sink_exps, invs)))
    yield
    p4s = [_split(probs[2 * j], probs[2 * j + 1], use_cur).astype(BF16) for j in range(N_HEADS // 2)]
    ya = jnp.concatenate([_nn(p4, vbd[j // 4]) for j, p4 in enumerate(p4s)], axis=1)
    ya_ref[...] = ya
    yield
    silu, _ = _silu_and_grad(ga_ref[...])
    oa_ref[...] = (ya * _rstd(ya) * gn_ref[...] * silu).astype(BF16)


def _out_proj_loss(x, oc, oa, wo, norm_final, target, pc, conv_w, norm_conv_out):
    seq = x.shape[0]
    tile = TOK_TILE
    n_tiles = seq // tile

    def body(x_ref, oc_ref, oa_ref, wo_ref, gf_ref, t_ref, pc_ref, hcc_ref, hcu_ref, cw_ref, gn_ref,
             dx2_ref, doa_ref, gwo_ref, dpc_ref, small_ref, loss_acc, zbuf, dbuf):
        step = pl.program_id(0)

        def small_add(row, value):
            small_ref[row:row + 1, :] += jnp.sum(value, axis=0, keepdims=True)

        @pl.when(step == 0)
        def _():
            gwo_ref[...] = jnp.zeros_like(gwo_ref)
            small_ref[...] = jnp.zeros_like(small_ref)
            loss_acc[...] = jnp.zeros_like(loss_acc)
            dbuf[tile:tile + 8, :] = jnp.zeros((8, D_CONV), F32)

        oc, oa = oc_ref[...], oa_ref[...]
        x2 = x_ref[...] + _nn(oc, wo_ref[0:D_CONV, :]) + _nn(oa, wo_ref[D_CONV:D_MIX, :])
        r = _rstd(x2)
        xhat = x2 * r
        err = xhat * gf_ref[...] - t_ref[...]
        loss_acc[...] += jnp.sum(err * err, axis=0, keepdims=True) * (0.5 / D_MODEL)
        dy = err * (1.0 / D_MODEL)
        small_add(SMALL_NORM_FINAL, dy * xhat)
        dx2 = _rms_bwd(dy * gf_ref[...], xhat, r)
        dx2_ref[...] = dx2
        db = dx2.astype(BF16)
        do = _nt(db, wo_ref[0:D_CONV, :])
        doa_ref[...] = _nt(db, wo_ref[D_CONV:D_MIX, :])
        gwo_ref[0:D_CONV, :] += _tn(oc, db)
        gwo_ref[D_CONV:D_MIX, :] += _tn(oa, db)

        is_first_tile = step == n_tiles - 1
        zbuf[0:8, :] = jnp.where(is_first_tile, 0.0, hcc_ref[...] * hcu_ref[...])
        cb, cc, cu, z, z1, z2, conv = _conv_core(pc_ref, zbuf, cw_ref)
        silu, dsilu = _silu_and_grad(pc_ref[:, 3 * D_CONV:4 * D_CONV])
        yc = cb * conv
        rc = _rstd(yc)
        chat = yc * rc
        dn = do * silu
        dpc_ref[:, 3 * D_CONV:4 * D_CONV] = (do * (chat * gn_ref[...]) * dsilu).astype(BF16)
        small_add(SMALL_NORM_CONV, dn * chat)
        dyc = _rms_bwd(dn * gn_ref[...], chat, rc)
        dpc_ref[:, 0:D_CONV] = (dyc * conv).astype(BF16)
        dconv = dyc * cb
        small_add(SMALL_CONV_W, dconv * z2)
        small_add(SMALL_CONV_W + 1, dconv * z1)
        small_add(SMALL_CONV_W + 2, dconv * z)
        dbuf[0:tile, :] = dconv
        dz = cw_ref[2:3, :] * dconv + cw_ref[1:2, :] * dbuf[1:tile + 1, :] + cw_ref[0:1, :] * dbuf[2:tile + 2, :]
        dpc_ref[:, D_CONV:2 * D_CONV] = (dz * cu).astype(BF16)
        dpc_ref[:, 2 * D_CONV:3 * D_CONV] = (dz * cc).astype(BF16)
        dbuf[tile:tile + 8, :] = dbuf[0:8, :]

        @pl.when(step == n_tiles - 1)
        def _():
            lane = lax.broadcasted_iota(jnp.int32, (1, D_MODEL), 1)
            small_ref[SMALL_MISC:SMALL_MISC + 1, :] = jnp.where(lane == SMALL_LOSS_LANE, jnp.sum(loss_acc[...]), 0.0)

    def rev(i):
        return n_tiles - 1 - i

    def row(width):
        return pl.BlockSpec((tile, width), lambda i: (rev(i), 0))

    def halo(col_block):
        return pl.BlockSpec((8, D_CONV), lambda i: (jnp.maximum(rev(i) * (tile // 8) - 1, 0), col_block))

    def const(shape):
        return pl.BlockSpec(shape, lambda i: (0, 0))

    return pl.pallas_call(
        body, name="out_proj_loss", grid=(n_tiles,),
        out_shape=(jax.ShapeDtypeStruct((seq, D_MODEL), F32), jax.ShapeDtypeStruct((seq, D_ATTN), F32),
                   jax.ShapeDtypeStruct((D_MIX, D_MODEL), F32), jax.ShapeDtypeStruct((seq, D_PC), BF16),
                   jax.ShapeDtypeStruct((SMALL_ROWS, D_MODEL), F32)),
        in_specs=[row(D_MODEL), row(D_CONV), row(D_ATTN), _resident(wo.shape), _resident((1, D_MODEL)), row(D_MODEL),
                  row(D_PC), halo(1), halo(2), _resident(conv_w.shape), _resident((1, D_CONV))],
        out_specs=(row(D_MODEL), row(D_ATTN), const((D_MIX, D_MODEL)), row(D_PC), const((SMALL_ROWS, D_MODEL))),
        scratch_shapes=[pltpu.VMEM((1, D_MODEL), F32), pltpu.VMEM((tile + 8, D_CONV), F32),
                        pltpu.VMEM((tile + 8, D_CONV), F32)],
        compiler_params=_params(62, ("arbitrary",)),
    )(x, oc, oa, wo, norm_final, target, pc, pc, pc, conv_w, norm_conv_out)


def _attn_bwd(q, kv, ga, ya, doa, probs, sink_probs, norm_attn_out, gwo, small):
    seq = q.shape[0]
    per_step = 2
    n_steps = seq // (per_step * BLK)
    stage_steps = (0, 1, n_steps // 4, (5 * n_steps) // 8, n_steps - 1, n_steps - 1)

    def body(q_ref, kvc_ref, kvp_ref, ga_ref, ya_ref, doa_ref, pr_ref, sp_ref, gn_ref, gwo_ref, small_ref,
             dqg_ref, small_out, gwo_sh, gna_ref, gs_ref, carry, dya_buf, *rs_scratch):
        step = pl.program_id(0)

        @pl.when(step == 0)
        def _():
            gna_ref[...] = jnp.zeros_like(gna_ref)
            gs_ref[...] = jnp.zeros_like(gs_ref)
            carry[...] = jnp.zeros_like(carry)

        lane = lax.broadcasted_iota(jnp.int32, (BLK, 128), 1)

        def fold(bd):
            return (jnp.where(lane < 64, bd[0:BLK], 0.0) + jnp.where(lane >= 64, bd[BLK:2 * BLK], 0.0),
                    jnp.where(lane < 64, bd[2 * BLK:3 * BLK], 0.0) + jnp.where(lane >= 64, bd[3 * BLK:4 * BLK], 0.0))

        def one_block(b):
            rows = slice(b * BLK, (b + 1) * BLK)
            kv_prev = kvp_ref if b == 0 else kvc_ref.at[(b - 1) * BLK:b * BLK, :]
            ya = ya_ref[rows, :]
            r = _rstd(ya)
            xhat = ya * r
            silu, dsilu = _silu_and_grad(ga_ref[rows, :])
            do = doa_ref[rows, :]
            dn = do * silu
            dqg_ref[rows, D_ATTN:2 * D_ATTN] = (do * (xhat * gn_ref[...]) * dsilu).astype(BF16)
            gna_ref[...] += jnp.sum(dn * xhat, axis=0, keepdims=True)
            dya_buf[rows, :] = _rms_bwd(dn * gn_ref[...], xhat, r).astype(BF16)

            use_cur = _band_geometry(per_step * (n_steps - 1 - step) + b)[0]
            pairs = range(N_HEADS // 2)
            qps, kbd, vbd = _attn_operands(q_ref.at[rows, :], kvc_ref.at[rows, :], kv_prev)
            probs = [pr_ref[rows, h * BLK:(h + 1) * BLK].astype(F32) for h in range(N_HEADS)]
            dyps = [dya_buf[rows, j * 128:(j + 1) * 128] for j in pairs]
            dps = []
            for j in pairs:
                dps += _merge(_nt(dyps[j], vbd[j // 4]), use_cur)
            deltas = [jnp.sum(p * dp, axis=-1, keepdims=True) for p, dp in zip(probs, dps)]
            dss = [p * (dp - delta) for p, dp, delta in zip(probs, dps, deltas)]
            delta_lanes = sum(jnp.where(lane == h, deltas[h], 0.0) for h in range(N_HEADS))
            gs_ref[...] -= jnp.sum(sp_ref[rows, :] * delta_lanes, axis=0, keepdims=True)
            ds4s = [_split(dss[2 * j], dss[2 * j + 1], use_cur).astype(BF16) for j in pairs]
            p4s = [_split(probs[2 * j], probs[2 * j + 1], use_cur).astype(BF16) for j in pairs]
            dqg_ref[rows, 0:D_ATTN] = jnp.concatenate([_nn(ds4s[j], kbd[j // 4]) * SCALE for j in pairs], axis=1).astype(BF16)
            sums = []
            for group in range(N_HEADS // HEADS_PER_KV):
                acc = [jnp.zeros((BLK, 128), F32) for _ in range(4)]
                for j in range(group * 4, group * 4 + 4):
                    for slot, part in enumerate(fold(_tn(ds4s[j], qps[j])) + fold(_tn(p4s[j], dyps[j]))):
                        acc[slot] = acc[slot] + part
                sums.append([a + pltpu.roll(a, 64, 1) for a in acc])
            dk_cur, dk_prev, dv_cur, dv_prev = (jnp.where(lane < 64, lo, hi) for lo, hi in zip(sums[0], sums[1]))
            dqg_ref[rows, 2 * D_ATTN:2 * D_ATTN + 2 * D_KV] = (jnp.concatenate([dk_cur, dv_cur], axis=1) + carry[...]).astype(BF16)
            carry[...] = jnp.concatenate([dk_prev, dv_prev], axis=1)

        for b in reversed(range(per_step)):
            one_block(b)

        @pl.when(step == n_steps - 1)
        def _():
            small_out[...] = small_ref[...]
            small_out[SMALL_NORM_ATTN:SMALL_NORM_ATTN + 1, :] = gna_ref[...]
            small_out[SMALL_MISC:SMALL_MISC + 1, 0:128] = small_ref[SMALL_MISC:SMALL_MISC + 1, 0:128] + gs_ref[...]

        for at, stage in zip(stage_steps, _rs_wout_stages(gwo_ref, gwo_sh, *rs_scratch)):
            pl.when(step == at)(stage)

    def rows(width):
        return pl.BlockSpec((per_step * BLK, width), lambda i: (n_steps - 1 - i, 0))

    kv_prev = pl.BlockSpec((BLK, 2 * D_KV), lambda i: (jnp.maximum(per_step * (n_steps - 1 - i) - 1, 0), 0))
    any_spec = pl.BlockSpec(memory_space=pl.ANY)
    return pl.pallas_call(
        body, name="attn_bwd", grid=(n_steps,),
        out_shape=(jax.ShapeDtypeStruct((seq, D_QG), BF16), pltpu.HBM(small.shape, F32),
                   pltpu.HBM((gwo.shape[0] // N_CHIPS, gwo.shape[1]), F32)),
        in_specs=[rows(D_ATTN), rows(2 * D_KV), kv_prev, rows(D_ATTN), rows(D_ATTN), rows(D_ATTN),
                  rows(N_HEADS * BLK), rows(128), _resident((1, D_ATTN)), any_spec, _resident(small.shape)],
        out_specs=(rows(D_QG), pl.BlockSpec(small.shape, lambda i: (0, 0)), any_spec),
        scratch_shapes=[pltpu.VMEM((1, D_ATTN), F32), pltpu.VMEM((1, 128), F32),
                        pltpu.VMEM((BLK, 2 * D_KV), F32), pltpu.VMEM((per_step * BLK, D_ATTN), BF16)] + _rs_wout_scratch(gwo.shape),
        compiler_params=_params(44, ("arbitrary",)),
    )(q, kv, kv, ga, ya, doa, probs, sink_probs, norm_attn_out, gwo, small)


GBLK = 256
GSUB = 64
PAIR_RING = 4
LAG_PAIR, LAG_HOP1, LAG_HOP2 = 1, 7, 14


def _bwd_in(dpc, dqg, h, wt, x, norm_in, dx2, small):
    seq = x.shape[0]
    n_blk = D_IN_PROJ // GBLK
    per_chip = n_blk // N_CHIPS
    n_slots = (n_blk + 1) // 2
    n_sub = GBLK // GSUB
    chip_rows = D_IN_PROJ // N_CHIPS
    tile = TOK_TILE
    n_tiles = seq // tile
    n_steps = n_blk + max(n_tiles, LAG_HOP2)
    chunk = min(seq, 512)
    blk_q, blk_kv, blk_ga = ROW_Q // GBLK, ROW_KV // GBLK, ROW_GA // GBLK

    def block_of(i):
        k = i % N_CHIPS
        robin = per_chip * ((k % 2) * 2 + k // 2) + i // N_CHIPS
        if isinstance(i, int):
            return robin if i < per_chip * N_CHIPS else i
        return jnp.where(i < per_chip * N_CHIPS, robin, i)

    def owner_of(i):
        return (i // N_CHIPS) % 2

    def slot_of(i):
        return (i // (2 * N_CHIPS)) * N_CHIPS + i % N_CHIPS

    def body(dpc_ref, dqg_ref, wt_ref, h_ref, x_ref, g_ref, dx2_ref, small_ref, gx_ref, small_sum, gwt_sh,
             dh_acc, gni, keep, pbuf, xbuf, land, land2, small_land,
             pair_send, pair_recv, h1_send, h1_recv, h2_send, h2_recv, sw_send, sw_recv, sm_send, sm_recv, out_sem):
        step = pl.program_id(0)
        x_i, y_i, c = lax.axis_index("x"), lax.axis_index("y"), lax.axis_index("c")
        me = 4 * x_i + 2 * y_i + c
        j = 2 * x_i + y_i
        pa = (_xor(x_i, 1 - c), _xor(y_i, c), c)
        pb = (_xor(x_i, c), _xor(y_i, 1 - c), c)
        sib = (x_i, y_i, 1 - c)
        ja = 2 * pa[0] + pa[1]
        jb = 2 * pb[0] + pb[1]
        jd = 3 - j

        def remote(src, dst, send, recv, to):
            return pltpu.make_async_remote_copy(src_ref=src, dst_ref=dst, send_sem=send, recv_sem=recv,
                                                device_id=to, device_id_type=MESH)

        def piece(ref, slot, u, n):
            return ref.at[slot, pl.ds(u * GSUB, n * GSUB), :]

        def chip_rows_at(ref, local, n):
            return ref.at[pl.ds(pl.multiple_of(local, GSUB), n * GSUB), :]

        def pair_copy(i):
            slot = slot_of(i)
            return remote(pbuf.at[i % PAIR_RING], land.at[slot], pair_send.at[slot], pair_recv.at[slot], sib)

        def h1_copy(slot, u, n):
            k = slot * n_sub + u
            return remote(piece(xbuf, slot, u, n), piece(xbuf, slot, u, n), h1_send.at[k], h1_recv.at[k], pa)

        def h2_copy(slot, u, n, local):
            k = slot * n_sub + u
            return remote(piece(xbuf, slot, u, n), chip_rows_at(land2, local, n), h2_send.at[k], h2_recv.at[k], pb)

        def sw_copy(slot, u, n, local):
            k = slot * n_sub + u
            return remote(piece(keep, slot, u, n), chip_rows_at(gwt_sh, local, n), sw_send.at[k], sw_recv.at[k], sib)

        def owned(i):
            return (i >= 0) & (i < n_blk) & (owner_of(i) == c)

        def chip_of(blk, u):
            row = blk * GBLK + u * GSUB
            chip = row // chip_rows
            return chip, row - chip * chip_rows

        def pieces(blk):
            first, local = chip_of(blk, 0)
            whole = first == chip_of(blk, n_sub - 1)[0]
            if isinstance(blk, int):
                return [(True, 0, n_sub, first, local)] if whole else [(True, u, 1) + chip_of(blk, u) for u in range(n_sub)]
            return [(whole, 0, n_sub, first, local)] + [(jnp.logical_not(whole), u, 1) + chip_of(blk, u) for u in range(n_sub)]

        @pl.when(step == 0)
        def _():
            dh_acc[...] = jnp.zeros_like(dh_acc)
            gni[...] = jnp.zeros_like(gni)

        @pl.when(step < n_blk)
        def _():
            from_pc = block_of(step) < blk_q
            block = _tn(jnp.where(from_pc, dpc_ref[...], dqg_ref[...]), h_ref[...])
            for t in range(0, seq, chunk):
                d = jnp.where(from_pc, dpc_ref[t:t + chunk, :], dqg_ref[t:t + chunk, :])
                dh_acc[t:t + chunk, :] += _nn(d, wt_ref[...])

            @pl.when(owner_of(step) == c)
            def _():
                keep[slot_of(step)] = block

            @pl.when(owner_of(step) != c)
            def _():
                @pl.when(step >= 2 * PAIR_RING)
                def _():
                    pair_copy(step - 2 * PAIR_RING).wait_send()
                pbuf[step % PAIR_RING] = block.astype(BF16)
                pair_copy(step).start()

        i1 = step - LAG_PAIR

        @pl.when(owned(i1))
        def _():
            slot = slot_of(i1)
            pair_copy(i1).wait_recv()
            _accumulate(keep.at[slot], land.at[slot])
            for cond, u, n, chip, _ in pieces(block_of(i1)):
                @pl.when(cond & ((chip == ja) | (chip == jd)))
                def _(u=u, n=n):
                    _cast_rows(piece(keep, slot, u, n), piece(xbuf, slot, u, n))
                    h1_copy(slot, u, n).start()

        i2 = step - LAG_HOP1

        @pl.when(owned(i2))
        def _():
            slot = slot_of(i2)
            for cond, u, n, chip, local in pieces(block_of(i2)):
                @pl.when(cond & ((chip == j) | (chip == jb)))
                def _(u=u, n=n, chip=chip, local=local):
                    h1_copy(slot, u, n).wait_recv()
                    _accumulate(piece(keep, slot, u, n), piece(xbuf, slot, u, n))

                    @pl.when(chip == jb)
                    def _():
                        _cast_rows(piece(keep, slot, u, n), piece(xbuf, slot, u, n))
                        h2_copy(slot, u, n, local).start()

                @pl.when(cond & ((chip == ja) | (chip == jd)))
                def _(u=u, n=n):
                    h1_copy(slot, u, n).wait_send()

        i3 = step - LAG_HOP2

        @pl.when(owned(i3))
        def _():
            slot = slot_of(i3)
            for cond, u, n, chip, local in pieces(block_of(i3)):
                @pl.when(cond & (chip == j))
                def _(u=u, n=n, local=local):
                    h2_copy(slot, u, n, local).wait_recv()
                    _accumulate(piece(keep, slot, u, n), chip_rows_at(land2, local, n))
                    mine = pltpu.make_async_copy(piece(keep, slot, u, n), chip_rows_at(gwt_sh, local, n), out_sem.at[0])
                    mine.start()
                    sw_copy(slot, u, n, local).start()
                    mine.wait()

                @pl.when(cond & (chip == jb))
                def _(u=u, n=n, local=local):
                    h2_copy(slot, u, n, local).wait_send()

        e = step - n_blk

        @pl.when((e >= 0) & (e < n_tiles))
        def _():
            dh = dh_acc[pl.ds(pl.multiple_of(e * tile, tile), tile), :]
            xv = x_ref[...]
            r = _rstd(xv)
            xhat = xv * r
            gni[...] += jnp.sum(dh * xhat, axis=0, keepdims=True)
            gx_ref[...] = _rms_bwd(dh * g_ref[...], xhat, r) + dx2_ref[...]

        @pl.when(step == n_steps - 1)
        def _():
            small_land[me] = small_ref[...]
            small_land[me, SMALL_NORM_IN:SMALL_NORM_IN + 1, :] = gni[...]
            others = [(dx, dy, dc) for dx in (0, 1) for dy in (0, 1) for dc in (0, 1)][1:]
            sends = [remote(small_land.at[me], small_land.at[me], sm_send.at[k], sm_recv.at[k],
                            (_xor(x_i, dx), _xor(y_i, dy), _xor(c, dc))) for k, (dx, dy, dc) in enumerate(others)]
            for cp in sends:
                cp.start()
            for i in range(n_blk):
                if i + 2 * PAIR_RING >= n_blk:
                    @pl.when(owner_of(i) != c)
                    def _(i=i):
                        pair_copy(i).wait_send()
                for _, u, n, chip, local in pieces(block_of(i)):
                    @pl.when((j == chip) & (c == owner_of(i)))
                    def _(i=i, u=u, n=n, local=local):
                        sw_copy(slot_of(i), u, n, local).wait_send()

                    @pl.when((j == chip) & (c != owner_of(i)))
                    def _(i=i, u=u, n=n, local=local):
                        sw_copy(slot_of(i), u, n, local).wait_recv()
            for cp in sends:
                cp.wait_recv()
            total = small_land[0]
            for dev in range(1, 8):
                total = total + small_land[dev]
            small_sum[...] = total
            for cp in sends:
                cp.wait_send()

    def blk_at(i):
        return block_of(jnp.clip(i, 0, n_blk - 1))

    last_pc_step = max(i for i in range(n_blk) if block_of(i) < blk_q)

    def next_block(i, in_pc):
        i = jnp.clip(i, 0, n_blk - 1)
        step = jnp.full_like(i, last_pc_step if in_pc else n_blk - 1)
        for ahead in reversed(range(N_CHIPS)):
            cand = jnp.minimum(i + ahead, n_blk - 1)
            step = jnp.where((block_of(cand) < blk_q) == in_pc, cand, step)
        return block_of(step)

    def dqg_block(i):
        b = next_block(i, False)
        q_blk = jnp.clip(b - blk_q, 0, blk_kv - blk_q - 1)
        ga_blk = (D_ATTN // GBLK) + jnp.clip(b - blk_ga, 0, n_blk - blk_ga - 1)
        return jnp.where(b < blk_kv, q_blk, jnp.where(b == blk_kv, 2 * D_ATTN // GBLK, ga_blk))

    def tok(i):
        return (jnp.clip(i - n_blk, 0, n_tiles - 1), 0)

    n_piece = n_slots * n_sub
    dma = pltpu.SemaphoreType.DMA
    return pl.pallas_call(
        body, name="bwd_in", grid=(n_steps,),
        out_shape=(jax.ShapeDtypeStruct((seq, D_MODEL), F32), jax.ShapeDtypeStruct(small.shape, F32),
                   jax.ShapeDtypeStruct((chip_rows, D_MODEL), F32)),
        in_specs=[pl.BlockSpec((seq, GBLK), lambda i: (0, next_block(i, True))),
                  pl.BlockSpec((seq, GBLK), lambda i: (0, dqg_block(i))),
                  pl.BlockSpec((GBLK, D_MODEL), lambda i: (blk_at(i), 0)),
                  _resident(h.shape),
                  pl.BlockSpec((tile, D_MODEL), tok), _resident((1, D_MODEL)), pl.BlockSpec((tile, D_MODEL), tok),
                  _resident(small.shape)],
        out_specs=(pl.BlockSpec((tile, D_MODEL), tok), pl.BlockSpec(small.shape, lambda i: (0, 0)),
                   pl.BlockSpec(memory_space=pl.ANY)),
        scratch_shapes=[pltpu.VMEM((seq, D_MODEL), F32), pltpu.VMEM((1, D_MODEL), F32),
                        pltpu.VMEM((n_slots, GBLK, D_MODEL), F32), pltpu.VMEM((PAIR_RING, GBLK, D_MODEL), BF16),
                        pltpu.VMEM((n_slots, GBLK, D_MODEL), BF16), pltpu.VMEM((n_slots, GBLK, D_MODEL), BF16),
                        pltpu.VMEM((chip_rows, D_MODEL), BF16), pltpu.VMEM((8,) + small.shape, F32),
                        dma((n_slots,)), dma((n_slots,)), dma((n_piece,)), dma((n_piece,)), dma((n_piece,)),
                        dma((n_piece,)), dma((n_piece,)), dma((n_piece,)), dma((7,)), dma((7,)), dma((1,))],
        compiler_params=_params(62, ("arbitrary",)),
    )(dpc, dqg, wt, h, x, norm_in, dx2, small)


def _accumulate(dst_ref, src_ref, rows=16):
    def step(i, carry):
        sl = pl.ds(pl.multiple_of(i * rows, rows), rows)
        dst_ref[sl, :] = dst_ref[sl, :] + src_ref[sl, :].astype(F32)
        return carry
    lax.fori_loop(0, dst_ref.shape[0] // rows, step, 0)


def _rs_wout_scratch(gwo_shape):
    o_half, width = gwo_shape[0] // N_CHIPS // 2, gwo_shape[1]
    return [pltpu.VMEM((4, o_half, width), F32), pltpu.VMEM((4, o_half, width), BF16),
            pltpu.VMEM((4, o_half, width), BF16), pltpu.VMEM((2, o_half, width), BF16),
            pltpu.VMEM((o_half, width), BF16),
            pltpu.SemaphoreType.DMA((8,)), pltpu.SemaphoreType.DMA((8,)), pltpu.SemaphoreType.DMA((4,))]


def _rs_wout_stages(gwo_ref, gwo_sh, acc_o, sb_o, r1o, r2o, r3o, send_sems, recv_sems, local_sems):
    o_rows = gwo_ref.shape[0] // N_CHIPS
    o_half = o_rows // 2
    x, y, c = lax.axis_index("x"), lax.axis_index("y"), lax.axis_index("c")
    j = 2 * x + y
    pa = (_xor(x, 1 - c), _xor(y, c), c)
    pb = (_xor(x, c), _xor(y, 1 - c), c)
    sib = (x, y, 1 - c)
    ja = 2 * pa[0] + pa[1]
    jb = 2 * pb[0] + pb[1]
    jd = 3 - j
    order = (ja, jd, jb, j)
    sib_order = (jb, jd, ja, j)

    def rcopy(k, src, dst, to):
        return pltpu.make_async_remote_copy(src_ref=src, dst_ref=dst, send_sem=send_sems.at[k],
                                            recv_sem=recv_sems.at[k], device_id=to, device_id_type=MESH)

    def o_rows_of(chip, half):
        return gwo_ref.at[pl.ds(pl.multiple_of(chip * o_rows + half * o_half, 8), o_half), :]

    def loads(chips, half):
        return [pltpu.make_async_copy(o_rows_of(chip, half), acc_o.at[s], local_sems.at[s]) for s, chip in enumerate(chips)]

    def pair_send(s):
        return rcopy(s, sb_o.at[s], r1o.at[s], sib)

    def out_half(half):
        return gwo_sh.at[pl.ds(pl.multiple_of(half * o_half, 8), o_half), :]

    hop1 = [rcopy(4 + s, sb_o.at[s], r2o.at[s], pa) for s in range(2)]
    hop2 = rcopy(6, sb_o.at[2], r3o, pb)
    swap = rcopy(7, acc_o.at[3], out_half(c), sib)
    mine = pltpu.make_async_copy(acc_o.at[3], out_half(c), local_sems.at[0])

    def resend(s, copy):
        pair_send(s).wait_send()
        _cast_rows(acc_o.at[s], sb_o.at[s])
        copy.start()

    def stage_load():
        for cp in loads(sib_order, 1 - c):
            cp.start()

    def stage_pair():
        for s, (cp, mine_in) in enumerate(zip(loads(sib_order, 1 - c), loads(order, c))):
            cp.wait()
            _cast_rows(acc_o.at[s], sb_o.at[s])
            pair_send(s).start()
            mine_in.start()

    def stage_hop1():
        for s, cp in enumerate(loads(order, c)):
            cp.wait()
            pair_send(s).wait_recv()
            _accumulate(acc_o.at[s], r1o.at[s])
            if s < 2:
                resend(s, hop1[s])

    def stage_hop2():
        hop1[1].wait_recv()
        _accumulate(acc_o.at[2], r2o.at[1])
        resend(2, hop2)
        hop1[0].wait_recv()
        _accumulate(acc_o.at[3], r2o.at[0])

    def stage_final():
        hop2.wait_recv()
        _accumulate(acc_o.at[3], r3o)
        mine.start()
        swap.start()

    def stage_drain():
        rcopy(7, acc_o.at[3], out_half(1 - c), sib).wait_recv()
        for cp in [pair_send(3)] + hop1 + [hop2, swap]:
            cp.wait_send()
        mine.wait()

    return stage_load, stage_pair, stage_hop1, stage_hop2, stage_final, stage_drain


def _adamw_big(groups, passed):
    arrays = [a for group in groups for a in group[:4]]
    steps = [group[0].shape[0] // group[4] for group in groups]
    first = [sum(steps[:k]) for k in range(len(steps) + 1)]
    n = len(arrays)

    def body(*refs):
        ins, passed_in, outs, passed_out = refs[:n], refs[n], refs[n + 1:2 * n + 1], refs[2 * n + 1]
        i = pl.program_id(0)
        for k in range(len(groups)):
            @pl.when((i >= first[k]) & (i < first[k + 1]))
            def _(k=k):
                w_ref, g_ref, m_ref, v_ref = ins[4 * k:4 * k + 4]
                _adamw_update(w_ref, g_ref[...], m_ref, v_ref, *outs[4 * k:4 * k + 4])

        @pl.when(i < steps[0])
        def _():
            passed_out[...] = passed_in[...]

    def spec(k, rows, width):
        return pl.BlockSpec((rows, width), lambda i: (jnp.clip(i - first[k], 0, steps[k] - 1), 0))

    specs = [spec(k, group[4], group[0].shape[1]) for k, group in enumerate(groups) for _ in range(4)]
    specs.append(spec(0, passed.shape[0] // steps[0], passed.shape[1]))
    out = pl.pallas_call(
        body, name="adamw_big", grid=(first[-1],),
        out_shape=tuple(jax.ShapeDtypeStruct(a.shape, a.dtype) for a in arrays + [passed]),
        in_specs=specs, out_specs=tuple(specs),
        compiler_params=_params(40, ("arbitrary",)),
    )(*arrays, passed)
    return [tuple(out[4 * k:4 * k + 4]) for k in range(len(groups))], out[-1]


def _adamw_update(w_ref, gv, m_ref, v_ref, go_ref, d_ref, nm_ref, nv_ref, at=...):
    go_ref[at] = gv
    nm = ADAM_B1 * m_ref[at] + (1.0 - ADAM_B1) * gv
    nv = ADAM_B2 * v_ref[at] + (1.0 - ADAM_B2) * (gv * gv)
    m_hat = nm / (1.0 - ADAM_B1 ** ADAM_STEP)
    v_hat = nv / (1.0 - ADAM_B2 ** ADAM_STEP)
    d_ref[at] = -ADAM_LR * (m_hat / (jnp.sqrt(v_hat) + ADAM_EPS) + ADAM_WD * w_ref[at])
    nm_ref[at] = nm
    nv_ref[at] = nv


def _adamw_small(chip, small_sum, weights, grads_of, ms, vs):
    n = len(weights)

    def body(chip_ref, small_ref, *refs):
        ins, outs, loss_ref = refs[:3 * n], refs[3 * n:-1], refs[-1]
        for k in range(n):
            w_ref, m_ref, v_ref = ins[3 * k:3 * k + 3]
            for at, gv in grads_of[k](small_ref, chip_ref):
                _adamw_update(w_ref, gv, m_ref, v_ref, *outs[4 * k:4 * k + 4], at=at)
        loss_ref[...] = small_ref[SMALL_MISC:SMALL_MISC + 1, SMALL_LOSS_LANE:SMALL_LOSS_LANE + 1]

    flat = [a for group in zip(weights, ms, vs) for a in group]
    vmem = pl.BlockSpec(memory_space=pltpu.VMEM)
    out = pl.pallas_call(
        body, name="adamw_small",
        out_shape=tuple(jax.ShapeDtypeStruct(w.shape, F32) for w in weights for _ in range(4))
        + (jax.ShapeDtypeStruct((1, 1), F32),),
        in_specs=[pl.BlockSpec(memory_space=pltpu.SMEM)] + [vmem] * (1 + 3 * n), out_specs=(vmem,) * (4 * n + 1),
    )(chip, small_sum, *flat)
    return [tuple(out[4 * k:4 * k + 4]) for k in range(n)], out[-1][0, 0]


def kernel(x, norm_in, w_in, conv_w, attn_sinks, norm_conv_out, norm_attn_out, w_out, norm_final, loss_target, m_norm_in, m_w_in, m_conv_w, m_attn_sinks, m_norm_conv_out, m_norm_attn_out, m_w_out, m_norm_final, v_norm_in, v_w_in, v_conv_w, v_attn_sinks, v_norm_conv_out, v_norm_attn_out, v_w_out, v_norm_final):
    chip = 2 * lax.axis_index("x") + lax.axis_index("y")
    xs, target = x[0], loss_target[0]
    norm_final2 = norm_final.reshape(1, D_MODEL)
    w_in_t, m_w_in_t, v_w_in_t = w_in[0].T, m_w_in[0].T, v_w_in[0].T

    def from_hbm(*arrays):
        return tuple(pltpu.with_memory_space_constraint(a, pltpu.HBM) for a in arrays)

    h, pc, q, kv, ga, oc, ya, oa, probs, sink_probs, wt, cw, wo = _fwd_in(
        xs, norm_in, w_in_t, w_out[0], conv_w.transpose(1, 0, 2), norm_conv_out, attn_sinks, norm_attn_out)
    dx2, doa, gwo, dpc, small = _out_proj_loss(xs, oc, oa, wo, norm_final2, target, pc, cw, norm_conv_out)
    q, kv, ga, ya, doa, probs, gwo, small = from_hbm(q, kv, ga, ya, doa, probs, gwo, small)
    dqg, small, gwo_sh = _attn_bwd(q, kv, ga, ya, doa, probs, sink_probs, norm_attn_out, gwo, small)
    grad_x, small_sum, gwt_sh = _bwd_in(dpc, dqg, h, wt, xs, norm_in, dx2, small)

    (up_w_in, up_w_out), grad_x = _adamw_big(
        [from_hbm(w_in_t, gwt_sh, m_w_in_t, v_w_in_t) + (200,), from_hbm(w_out[0], gwo_sh, m_w_out[0], v_w_out[0]) + (128,)],
        *from_hbm(grad_x))
    up_w_in = tuple(o.T[None] for o in up_w_in)
    up_w_out = tuple(o[None] for o in up_w_out)

    def row_of(r):
        return lambda small_ref, chip_ref: [(..., small_ref[r:r + 1, :])]

    def sink_lanes(small_ref, chip_ref):
        return [(..., small_ref[SMALL_MISC:SMALL_MISC + 1, 0:N_HEADS])]

    def conv_taps(small_ref, chip_ref):
        width = D_CONV // N_CHIPS
        cols = pl.ds(pl.multiple_of(chip_ref[0] * width, width), width)
        return [(k, small_ref[pl.ds(SMALL_CONV_W + k, 1), cols]) for k in range(conv_w.shape[1])]

    def small_view(a):
        return a.transpose(1, 0, 2) if a.ndim == 3 else a.reshape(-1, a.shape[-1])

    small_w = (norm_in, conv_w, attn_sinks, norm_conv_out, norm_attn_out, norm_final)
    small_m = (m_norm_in, m_conv_w, m_attn_sinks, m_norm_conv_out, m_norm_attn_out, m_norm_final)
    small_v = (v_norm_in, v_conv_w, v_attn_sinks, v_norm_conv_out, v_norm_attn_out, v_norm_final)
    small_g = (row_of(SMALL_NORM_IN), conv_taps, sink_lanes, row_of(SMALL_NORM_CONV), row_of(SMALL_NORM_ATTN),
               row_of(SMALL_NORM_FINAL))
    w_views, m_views, v_views = (tuple(small_view(a) for a in group) for group in (small_w, small_m, small_v))
    up_small, loss = _adamw_small(chip.astype(jnp.int32).reshape(1), small_sum, w_views, small_g, m_views, v_views)
    up_small = [tuple(o.transpose(1, 0, 2) if w.ndim == 3 else o.reshape(w.shape) for o in up)
                for up, w in zip(up_small, small_w)]
    up_norm_in, up_conv_w, up_sinks, up_norm_conv, up_norm_attn, up_norm_final = up_small
    updates = (up_norm_in, up_w_in, up_conv_w, up_sinks, up_norm_conv, up_norm_attn, up_w_out, up_norm_final)
    grads_out, deltas, new_m, new_v = zip(*updates)
    return (loss, grad_x[None], *grads_out, *deltas, *new_m, *new_v)
```

```python
import jax
import jax.numpy as jnp
from jax import lax
from jax.experimental import pallas as pl
from jax.experimental.pallas import tpu as pltpu

F32 = jnp.float32
BF16 = jnp.bfloat16
MESH = pl.DeviceIdType.MESH

D_MODEL = 1024
D_CONV = 1024
D_ATTN = 1024
D_KV = 128
D_QG = 2 * D_ATTN + 2 * D_KV
D_MIX = D_CONV + D_ATTN
D_PC = 4 * D_CONV
D_IN_PROJ = D_PC + 2 * D_ATTN + 2 * D_KV
ROW_Q = D_PC
ROW_KV = ROW_Q + D_ATTN
ROW_GA = ROW_KV + 2 * D_KV
N_HEADS = 16
HEAD_DIM = 64
HEADS_PER_KV = 8
BLK = 128
N_CHIPS = 4
RMS_EPS = 1e-5
SCALE = HEAD_DIM ** -0.5
SLOPES = tuple(2.0 ** (-8.0 * (h + 1) / N_HEADS) for h in range(N_HEADS))

ADAM_LR, ADAM_B1, ADAM_B2, ADAM_EPS, ADAM_WD, ADAM_STEP = 0.001, 0.9, 0.999, 1e-08, 0.01, 10

SMALL_ROWS = 8
SMALL_NORM_IN, SMALL_NORM_CONV, SMALL_NORM_ATTN, SMALL_NORM_FINAL, SMALL_CONV_W, SMALL_MISC = 0, 1, 2, 3, 4, 7
SMALL_LOSS_LANE = N_HEADS

TOK_TILE = 256
PC_PIECE = 512
MIB = 1 << 20


def _params(vmem_mib, semantics=None):
    return pltpu.CompilerParams(dimension_semantics=semantics, vmem_limit_bytes=vmem_mib * MIB)


def _nn(a, b):
    return jnp.dot(a, b, preferred_element_type=F32)


def _nt(a, b):
    return lax.dot_general(a, b, (((1,), (1,)), ((), ())), preferred_element_type=F32)


def _tn(a, b):
    return lax.dot_general(a, b, (((0,), (0,)), ((), ())), preferred_element_type=F32)


def _rstd(v):
    return lax.rsqrt(jnp.mean(v * v, axis=-1, keepdims=True) + RMS_EPS)


def _rms_bwd(g, xhat, rstd):
    return rstd * (g - xhat * jnp.mean(g * xhat, axis=-1, keepdims=True))


def _silu_and_grad(g):
    s = jax.nn.sigmoid(g)
    return g * s, s * (1.0 + g * (1.0 - s))


def _resident(shape):
    return pl.BlockSpec(shape, lambda *_: (0,) * len(shape), pipeline_mode=pl.Buffered(1))


def _xor(a, b):
    return a + b - 2 * a * b


def _cast_rows(src_ref, dst_ref, rows=32):
    def step(i, carry):
        sl = pl.ds(pl.multiple_of(i * rows, rows), rows)
        dst_ref[sl, :] = src_ref[sl, :].astype(dst_ref.dtype)
        return carry
    lax.fori_loop(0, src_ref.shape[0] // rows, step, 0)


def _ag_scratch(shard_shape):
    rows, width = shard_shape
    return [pltpu.VMEM((rows, width), F32), pltpu.VMEM((rows, width), BF16), pltpu.VMEM((3, rows // 2, width), BF16),
            pltpu.SemaphoreType.DMA((6,)), pltpu.SemaphoreType.DMA((6,)), pltpu.SemaphoreType.DMA((4,))]


def _ag_stages(sh_ref, out, f32_buf, own, land, send_sems, recv_sems, local_sems):
    rows = sh_ref.shape[0]
    half = rows // 2
    x, y, c = lax.axis_index("x"), lax.axis_index("y"), lax.axis_index("c")
    j = 2 * x + y
    p1 = (_xor(x, c), _xor(y, 1 - c), c)
    p2 = (_xor(x, 1 - c), _xor(y, c), c)
    sib = (x, y, 1 - c)
    j1 = 2 * p1[0] + p1[1]
    j2 = 2 * p2[0] + p2[1]
    j3 = 3 - j

    def rows_of(chip, hf):
        return out.at[pl.ds(pl.multiple_of(chip * rows + hf * half, 16), half), :]

    def rcopy(k, src, dst, to):
        return pltpu.make_async_remote_copy(src_ref=src, dst_ref=dst, send_sem=send_sems.at[k],
                                            recv_sem=recv_sems.at[k], device_id=to, device_id_type=MESH)

    my_half = own.at[pl.ds(pl.multiple_of(c * half, 16), half), :]
    hop1 = rcopy(0, my_half, land.at[0], p1)
    hop2_own = rcopy(1, my_half, land.at[1], p2)
    hop2_fwd = rcopy(2, land.at[0], land.at[2], p2)
    swaps = [rcopy(3 + s, land.at[s], rows_of(chip, c), sib) for s, chip in enumerate((j1, j2, j3))]
    keeps = [pltpu.make_async_copy(land.at[s], rows_of(chip, c), local_sems.at[1 + s]) for s, chip in enumerate((j1, j2, j3))]
    load = pltpu.make_async_copy(sh_ref, f32_buf, local_sems.at[0])
    own_out = pltpu.make_async_copy(own, out.at[pl.ds(pl.multiple_of(j * rows, 16), rows), :], local_sems.at[0])

    def stage_load():
        load.start()

    def stage_send():
        load.wait()
        _cast_rows(f32_buf, own)
        own_out.start()
        hop1.start()

    def stage_forward():
        hop1.wait_recv()
        hop2_own.start()
        hop2_fwd.start()
        swaps[0].start()
        keeps[0].start()

    def stage_publish():
        hop2_own.wait_recv()
        swaps[1].start()
        keeps[1].start()
        hop2_fwd.wait_recv()
        swaps[2].start()
        keeps[2].start()

    def stage_drain():
        for s, chip in enumerate((j2, j1, j3)):
            rcopy(3 + s, my_half, rows_of(chip, 1 - c), sib).wait_recv()
        for cp in [hop1, hop2_own, hop2_fwd] + swaps:
            cp.wait_send()
        for cp in [own_out] + keeps:
            cp.wait()

    return stage_load, stage_send, stage_forward, stage_publish, stage_drain


AG_CAST_ROWS = 400


def _gather_resident(sh_ref, out, f32_buf, send_sems, recv_sems, local_sems):
    rows = sh_ref.shape[0]
    half = rows // 2
    x, y, c = lax.axis_index("x"), lax.axis_index("y"), lax.axis_index("c")
    j = 2 * x + y
    p1 = (_xor(x, c), _xor(y, 1 - c), c)
    p2 = (_xor(x, 1 - c), _xor(y, c), c)
    sib = (x, y, 1 - c)
    j1 = 2 * p1[0] + p1[1]
    j2 = 2 * p2[0] + p2[1]
    j3 = 3 - j

    def rows_of(chip, hf):
        return out.at[pl.ds(pl.multiple_of(chip * rows + hf * half, 16), half), :]

    def send(k, chip, to):
        return pltpu.make_async_remote_copy(src_ref=rows_of(chip, c), dst_ref=rows_of(chip, c), send_sem=send_sems.at[k],
                                            recv_sem=recv_sems.at[k], device_id=to, device_id_type=MESH)

    per_half = half // AG_CAST_ROWS

    def cast_own(chunk):
        lo = pl.multiple_of(chunk * AG_CAST_ROWS, 16)
        load = pltpu.make_async_copy(sh_ref.at[pl.ds(lo, AG_CAST_ROWS), :], f32_buf, local_sems.at[0])
        load.start()
        load.wait()
        _cast_rows(f32_buf, out.at[pl.ds(pl.multiple_of(j * rows + lo, 16), AG_CAST_ROWS), :], rows=16)

    for k in range(per_half):
        cast_own(c * per_half + k)
    hop1 = send(0, j, p1)
    hop1.start()
    for k in range(per_half):
        cast_own((1 - c) * per_half + k)
    hop1.wait_recv()
    sends = [hop1, send(1, j, p2), send(2, j1, p2), send(3, j1, sib)]
    for cp in sends[1:]:
        cp.start()
    sends[1].wait_recv()
    sends.append(send(4, j2, sib))
    sends[-1].start()
    sends[2].wait_recv()
    sends.append(send(5, j3, sib))
    sends[-1].start()
    for k in (3, 4, 5):
        send(k, j, sib).wait_recv()
    for cp in sends:
        cp.wait_send()


def _fwd_in(x, norm_in, wt_sh, wo_sh, cw_sh, norm_conv_out, sinks, norm_attn_out):
    seq = x.shape[0]
    tile = TOK_TILE
    n_tiles = seq // tile
    stage_steps = (0, n_tiles // 4, (5 * n_tiles) // 8, n_tiles - 1)
    blocks = tile // BLK
    cw_cols = cw_sh.shape[-1]

    def body(x_ref, g_ref, wtsh_ref, wo_ref, cwsh_ref, gn_ref, sink_ref, gna_ref,
             h_ref, pc_ref, q_ref, kv_ref, ga_ref, oc_ref, ya_ref, oa_ref, pr_ref, sp_ref, wt_out, cw_ref, wo_out,
             zbuf, kv_last, wt_ref, f32_buf, cw_land, wt_send, wt_recv, wt_local, cw_send, cw_recv, cw_local, *ag_scratch):
        step = pl.program_id(0)
        to_hbm = pltpu.make_async_copy(wt_ref, wt_out, wt_local.at[0])
        load_wo, *stages = _ag_stages(wo_ref, wo_out, *ag_scratch)

        @pl.when(step == 0)
        def _():
            load_wo()
            zbuf[0:8, :] = jnp.zeros((8, D_CONV), F32)
            kv_last[...] = jnp.zeros_like(kv_last)
            x_i, y_i, c = lax.axis_index("x"), lax.axis_index("y"), lax.axis_index("c")
            j = 2 * x_i + y_i
            p1 = (_xor(x_i, c), _xor(y_i, 1 - c), c)
            p2 = (_xor(x_i, 1 - c), _xor(y_i, c), c)
            j1 = 2 * p1[0] + p1[1]

            def cw_copy(k, src, chip, to):
                return pltpu.make_async_remote_copy(src_ref=src, dst_ref=cw_land.at[chip], send_sem=cw_send.at[k],
                                                    recv_sem=cw_recv.at[k], device_id=to, device_id_type=MESH)

            mine = pltpu.make_async_copy(cwsh_ref, cw_land.at[j], cw_local.at[0])
            mine.start()
            first = cw_copy(0, cwsh_ref, j, p1)
            first.start()
            _gather_resident(wtsh_ref, wt_ref, f32_buf, wt_send, wt_recv, wt_local)
            to_hbm.start()
            first.wait_recv()
            second = [cw_copy(1, cwsh_ref, j, p2), cw_copy(2, cw_land.at[j1], j1, p2)]
            for cp in second:
                cp.start()
            for cp in second:
                cp.wait_recv()
            for cp in [first] + second:
                cp.wait_send()
            mine.wait()
            for chip in range(N_CHIPS):
                for tap in range(cw_sh.shape[0]):
                    cw_ref[tap:tap + 1, chip * cw_cols:(chip + 1) * cw_cols] = cw_land[chip, tap]

        for at, stage in zip(stage_steps[:-1], stages[:-1]):
            pl.when(step == at)(stage)

        def attention(b):
            rows = pl.ds(b * BLK, BLK)
            kv_prev = kv_last if b == 0 else kv_ref.at[pl.ds((b - 1) * BLK, BLK), :]
            return _attn_forward(q_ref.at[rows, :], kv_ref.at[rows, :], kv_prev, ga_ref.at[rows, :], sink_ref, gna_ref,
                                 _band_geometry(step * blocks + b), ya_ref.at[rows, :], oa_ref.at[rows, :],
                                 pr_ref.at[rows, :], sp_ref.at[rows, :])

        xv = x_ref[...]
        h = (xv * _rstd(xv) * g_ref[...]).astype(BF16)
        h_ref[...] = h
        q_ref[...] = _nt(h, wt_ref[ROW_Q:ROW_KV, :])
        kv_ref[...] = _nt(h, wt_ref[ROW_KV:ROW_GA, :])
        ga_ref[...] = _nt(h, wt_ref[ROW_GA:D_IN_PROJ, :])
        attention_blocks = [attention(b) for b in range(blocks)]
        for lo in range(0, D_PC, PC_PIECE):
            pc_ref[:, lo:lo + PC_PIECE] = _nt(h, wt_ref[lo:lo + PC_PIECE, :])
            for stages_of_block in attention_blocks:
                next(stages_of_block, None)
        for stages_of_block in attention_blocks:
            for _ in stages_of_block:
                pass
        kv_last[...] = kv_ref[tile - BLK:tile, :]

        cb, _, _, _, _, _, conv = _conv_core(pc_ref, zbuf, cw_ref)
        yc = cb * conv
        silu, _ = _silu_and_grad(pc_ref[:, 3 * D_CONV:4 * D_CONV])
        oc_ref[...] = (yc * _rstd(yc) * gn_ref[...] * silu).astype(BF16)
        zbuf[0:8, :] = zbuf[tile:tile + 8, :]

        @pl.when(step == stage_steps[-1])
        def _():
            stages[-1]()
            to_hbm.wait()

    def row(width):
        return pl.BlockSpec((tile, width), lambda i: (i, 0))

    any_spec = pl.BlockSpec(memory_space=pl.ANY)
    dma = pltpu.SemaphoreType.DMA
    wt_shape = (N_CHIPS * wt_sh.shape[0], wt_sh.shape[1])
    cw_shape = (cw_sh.shape[0], N_CHIPS * cw_cols)
    return pl.pallas_call(
        body, name="fwd_in", grid=(n_tiles,),
        out_shape=(jax.ShapeDtypeStruct((seq, D_MODEL), BF16), jax.ShapeDtypeStruct((seq, D_PC), F32),
                   jax.ShapeDtypeStruct((seq, D_ATTN), F32), jax.ShapeDtypeStruct((seq, 2 * D_KV), F32),
                   jax.ShapeDtypeStruct((seq, D_ATTN), F32), jax.ShapeDtypeStruct((seq, D_CONV), BF16),
                   pltpu.HBM((seq, D_ATTN), F32), pltpu.HBM((seq, D_ATTN), BF16),
                   pltpu.HBM((seq, N_HEADS * BLK), BF16), pltpu.HBM((seq, 128), F32),
                   pltpu.HBM(wt_shape, BF16), jax.ShapeDtypeStruct(cw_shape, F32),
                   jax.ShapeDtypeStruct((N_CHIPS * wo_sh.shape[0], wo_sh.shape[1]), BF16)),
        in_specs=[row(D_MODEL), _resident((1, D_MODEL)), any_spec, any_spec, any_spec, _resident((1, D_CONV)),
                  pl.BlockSpec(memory_space=pltpu.SMEM), _resident((1, D_ATTN))],
        out_specs=(row(D_MODEL), row(D_PC), row(D_ATTN), row(2 * D_KV), row(D_ATTN), row(D_CONV), row(D_ATTN),
                   row(D_ATTN), row(N_HEADS * BLK), row(128), any_spec, pl.BlockSpec(cw_shape, lambda i: (0, 0)),
                   any_spec),
        scratch_shapes=[pltpu.VMEM((tile + 8, D_CONV), F32), pltpu.VMEM((BLK, 2 * D_KV), F32),
                        pltpu.VMEM(wt_shape, BF16), pltpu.VMEM((AG_CAST_ROWS, wt_sh.shape[1]), F32),
                        pltpu.VMEM((N_CHIPS,) + cw_sh.shape, F32),
                        dma((6,)), dma((6,)), dma((1,)), dma((3,)), dma((3,)), dma((1,))] + _ag_scratch(wo_sh.shape),
        compiler_params=_params(62, ("arbitrary",)),
    )(x, norm_in, wt_sh, wo_sh, cw_sh, norm_conv_out, sinks, norm_attn_out)


def _conv_core(pc_ref, zbuf, cw_ref):
    tile = pc_ref.shape[0]
    cb = pc_ref[:, 0:D_CONV]
    cc = pc_ref[:, D_CONV:2 * D_CONV]
    cu = pc_ref[:, 2 * D_CONV:3 * D_CONV]
    z = cc * cu
    zbuf[8:tile + 8, :] = z
    z1 = zbuf[7:tile + 7, :]
    z2 = zbuf[6:tile + 6, :]
    conv = cw_ref[0:1, :] * z2 + cw_ref[1:2, :] * z1 + cw_ref[2:3, :] * z
    return cb, cc, cu, z, z1, z2, conv


def _band_geometry(block_index):
    qi = lax.broadcasted_iota(jnp.int32, (BLK, BLK), 0)
    kp = lax.broadcasted_iota(jnp.int32, (BLK, BLK), 1)
    use_cur = kp <= qi
    dist = jnp.where(use_cur, qi - kp, qi - kp + BLK).astype(F32)
    valid = use_cur | (block_index > 0)
    return use_cur, dist, valid


def _block_diag(cur, prev, group):
    lane = lax.broadcasted_iota(jnp.int32, cur.shape, 1)

    def halves(t):
        other = pltpu.roll(t, 64, 1)
        lo, hi = (t, other) if group == 0 else (other, t)
        return jnp.where(lane < 64, lo, 0.0), jnp.where(lane >= 64, hi, 0.0)

    return jnp.concatenate(halves(cur) + halves(prev), axis=0).astype(BF16)


def _merge(s4, use_cur):
    return (jnp.where(use_cur, s4[:, 0:BLK], s4[:, 2 * BLK:3 * BLK]),
            jnp.where(use_cur, s4[:, BLK:2 * BLK], s4[:, 3 * BLK:4 * BLK]))


def _split(a, b, use_cur):
    return jnp.concatenate([jnp.where(use_cur, a, 0.0), jnp.where(use_cur, b, 0.0),
                            jnp.where(use_cur, 0.0, a), jnp.where(use_cur, 0.0, b)], axis=1)


def _softmax_head(s, head, sink, dist, valid):
    sc = jnp.where(valid, s - SLOPES[head] * dist, -jnp.inf)
    m = jnp.maximum(jnp.max(sc, axis=-1, keepdims=True), sink)
    p = jnp.exp(sc - m)
    es = jnp.exp(sink - m)
    inv = 1.0 / (jnp.sum(p, axis=-1, keepdims=True) + es)
    return p * inv, es * inv


def _attn_operands(q_ref, kvc_ref, kvp_ref):
    groups = range(N_HEADS // HEADS_PER_KV)
    kbd = [_block_diag(kvc_ref[:, 0:D_KV], kvp_ref[:, 0:D_KV], g) for g in groups]
    vbd = [_block_diag(kvc_ref[:, D_KV:2 * D_KV], kvp_ref[:, D_KV:2 * D_KV], g) for g in groups]
    qps = [(q_ref[:, j * 128:(j + 1) * 128] * SCALE).astype(BF16) for j in range(N_HEADS // 2)]
    return qps, kbd, vbd


def _attn_forward(q_ref, kvc_ref, kvp_ref, ga_ref, sink_ref, gn_ref, geometry, ya_ref, oa_ref, pr_ref, sp_ref):
    use_cur, dist, valid = geometry
    _, kbd, vbd = operands = _attn_operands(q_ref, kvc_ref, kvp_ref)
    yield
    scores = []
    for j, qp in enumerate(operands[0]):
        scores += _merge(_nt(qp, kbd[j // 4]), use_cur)
    yield
    sinks = [sink_ref[0, h] for h in range(N_HEADS)]
    scores = [jnp.where(valid, s - SLOPES[h] * dist, -jnp.inf) for h, s in enumerate(scores)]
    maxes = [jnp.maximum(jnp.max(s, axis=-1, keepdims=True), sinks[h]) for h, s in enumerate(scores)]
    yield
    exps = [jnp.exp(s - m) for s, m in zip(scores, maxes)]
    sink_exps = [jnp.exp(sinks[h] - m) for h, m in enumerate(maxes)]
    yield
    invs = [1.0 / (jnp.sum(e, axis=-1, keepdims=True) + se) for e, se in zip(exps, sink_exps)]
    probs = [e * inv for e, inv in zip(exps, invs)]
    pr_ref[...] = jnp.concatenate(probs, axis=1).astype(BF16)
    lane = lax.broadcasted_iota(jnp.int32, (BLK, 128), 1)
    sp_ref[...] = sum(jnp.where(lane == h, se * inv, 0.0) for h, (se, inv) in enumerate(zip(sink_exps, invs)))
    yield
    p4s = [_split(probs[2 * j], probs[2 * j + 1], use_cur).astype(BF16) for j in range(N_HEADS // 2)]
    ya = jnp.concatenate([_nn(p4, vbd[j // 4]) for j, p4 in enumerate(p4s)], axis=1)
    ya_ref[...] = ya
    yield
    silu, _ = _silu_and_grad(ga_ref[...])
    oa_ref[...] = (ya * _rstd(ya) * gn_ref[...] * silu).astype(BF16)


def _out_proj_loss(x, oc, oa, wo, norm_final, target, pc, conv_w, norm_conv_out):
    seq = x.shape[0]
    tile = TOK_TILE
    n_tiles = seq // tile

    def body(x_ref, oc_ref, oa_ref, wo_ref, gf_ref, t_ref, pc_ref, hcc_ref, hcu_ref, cw_ref, gn_ref,
             dx2_ref, doa_ref, gwo_ref, dpc_ref, small_ref, loss_acc, zbuf, dbuf):
        step = pl.program_id(0)

        def small_add(row, value):
            small_ref[row:row + 1, :] += jnp.sum(value, axis=0, keepdims=True)

        @pl.when(step == 0)
        def _():
            gwo_ref[...] = jnp.zeros_like(gwo_ref)
            small_ref[...] = jnp.zeros_like(small_ref)
            loss_acc[...] = jnp.zeros_like(loss_acc)
            dbuf[tile:tile + 8, :] = jnp.zeros((8, D_CONV), F32)

        oc, oa = oc_ref[...], oa_ref[...]
        x2 = x_ref[...] + _nn(oc, wo_ref[0:D_CONV, :]) + _nn(oa, wo_ref[D_CONV:D_MIX, :])
        r = _rstd(x2)
        xhat = x2 * r
        err = xhat * gf_ref[...] - t_ref[...]
        loss_acc[...] += jnp.sum(err * err, axis=0, keepdims=True) * (0.5 / D_MODEL)
        dy = err * (1.0 / D_MODEL)
        small_add(SMALL_NORM_FINAL, dy * xhat)
        dx2 = _rms_bwd(dy * gf_ref[...], xhat, r)
        dx2_ref[...] = dx2
        db = dx2.astype(BF16)
        do = _nt(db, wo_ref[0:D_CONV, :])
        doa_ref[...] = _nt(db, wo_ref[D_CONV:D_MIX, :])
        gwo_ref[0:D_CONV, :] += _tn(oc, db)
        gwo_ref[D_CONV:D_MIX, :] += _tn(oa, db)

        is_first_tile = step == n_tiles - 1
        zbuf[0:8, :] = jnp.where(is_first_tile, 0.0, hcc_ref[...] * hcu_ref[...])
        cb, cc, cu, z, z1, z2, conv = _conv_core(pc_ref, zbuf, cw_ref)
        silu, dsilu = _silu_and_grad(pc_ref[:, 3 * D_CONV:4 * D_CONV])
        yc = cb * conv
        rc = _rstd(yc)
        chat = yc * rc
        dn = do * silu
        dpc_ref[:, 3 * D_CONV:4 * D_CONV] = (do * (chat * gn_ref[...]) * dsilu).astype(BF16)
        small_add(SMALL_NORM_CONV, dn * chat)
        dyc = _rms_bwd(dn * gn_ref[...], chat, rc)
        dpc_ref[:, 0:D_CONV] = (dyc * conv).astype(BF16)
        dconv = dyc * cb
        small_add(SMALL_CONV_W, dconv * z2)
        small_add(SMALL_CONV_W + 1, dconv * z1)
        small_add(SMALL_CONV_W + 2, dconv * z)
        dbuf[0:tile, :] = dconv
        dz = cw_ref[2:3, :] * dconv + cw_ref[1:2, :] * dbuf[1:tile + 1, :] + cw_ref[0:1, :] * dbuf[2:tile + 2, :]
        dpc_ref[:, D_CONV:2 * D_CONV] = (dz * cu).astype(BF16)
        dpc_ref[:, 2 * D_CONV:3 * D_CONV] = (dz * cc).astype(BF16)
        dbuf[tile:tile + 8, :] = dbuf[0:8, :]

        @pl.when(step == n_tiles - 1)
        def _():
            lane = lax.broadcasted_iota(jnp.int32, (1, D_MODEL), 1)
            small_ref[SMALL_MISC:SMALL_MISC + 1, :] = jnp.where(lane == SMALL_LOSS_LANE, jnp.sum(loss_acc[...]), 0.0)

    def rev(i):
        return n_tiles - 1 - i

    def row(width):
        return pl.BlockSpec((tile, width), lambda i: (rev(i), 0))

    def halo(col_block):
        return pl.BlockSpec((8, D_CONV), lambda i: (jnp.maximum(rev(i) * (tile // 8) - 1, 0), col_block))

    def const(shape):
        return pl.BlockSpec(shape, lambda i: (0, 0))

    return pl.pallas_call(
        body, name="out_proj_loss", grid=(n_tiles,),
        out_shape=(jax.ShapeDtypeStruct((seq, D_MODEL), F32), jax.ShapeDtypeStruct((seq, D_ATTN), F32),
                   jax.ShapeDtypeStruct((D_MIX, D_MODEL), F32), jax.ShapeDtypeStruct((seq, D_PC), BF16),
                   jax.ShapeDtypeStruct((SMALL_ROWS, D_MODEL), F32)),
        in_specs=[row(D_MODEL), row(D_CONV), row(D_ATTN), _resident(wo.shape), _resident((1, D_MODEL)), row(D_MODEL),
                  row(D_PC), halo(1), halo(2), _resident(conv_w.shape), _resident((1, D_CONV))],
        out_specs=(row(D_MODEL), row(D_ATTN), const((D_MIX, D_MODEL)), row(D_PC), const((SMALL_ROWS, D_MODEL))),
        scratch_shapes=[pltpu.VMEM((1, D_MODEL), F32), pltpu.VMEM((tile + 8, D_CONV), F32),
                        pltpu.VMEM((tile + 8, D_CONV), F32)],
        compiler_params=_params(62, ("arbitrary",)),
    )(x, oc, oa, wo, norm_final, target, pc, pc, pc, conv_w, norm_conv_out)


def _attn_bwd(q, kv, ga, ya, doa, probs, sink_probs, norm_attn_out, gwo, small):
    seq = q.shape[0]
    per_step = 2
    n_steps = seq // (per_step * BLK)
    stage_steps = (0, 1, n_steps // 4, (5 * n_steps) // 8, n_steps - 1, n_steps - 1)

    def body(q_ref, kvc_ref, kvp_ref, ga_ref, ya_ref, doa_ref, pr_ref, sp_ref, gn_ref, gwo_ref, small_ref,
             dqg_ref, small_out, gwo_sh, gna_ref, gs_ref, carry, dya_buf, *rs_scratch):
        step = pl.program_id(0)

        @pl.when(step == 0)
        def _():
            gna_ref[...] = jnp.zeros_like(gna_ref)
            gs_ref[...] = jnp.zeros_like(gs_ref)
            carry[...] = jnp.zeros_like(carry)

        lane = lax.broadcasted_iota(jnp.int32, (BLK, 128), 1)

        def fold(bd):
            return (jnp.where(lane < 64, bd[0:BLK], 0.0) + jnp.where(lane >= 64, bd[BLK:2 * BLK], 0.0),
                    jnp.where(lane < 64, bd[2 * BLK:3 * BLK], 0.0) + jnp.where(lane >= 64, bd[3 * BLK:4 * BLK], 0.0))

        def one_block(b):
            rows = slice(b * BLK, (b + 1) * BLK)
            kv_prev = kvp_ref if b == 0 else kvc_ref.at[(b - 1) * BLK:b * BLK, :]
            ya = ya_ref[rows, :]
            r = _rstd(ya)
            xhat = ya * r
            silu, dsilu = _silu_and_grad(ga_ref[rows, :])
            do = doa_ref[rows, :]
            dn = do * silu
            dqg_ref[rows, D_ATTN:2 * D_ATTN] = (do * (xhat * gn_ref[...]) * dsilu).astype(BF16)
            gna_ref[...] += jnp.sum(dn * xhat, axis=0, keepdims=True)
            dya_buf[rows, :] = _rms_bwd(dn * gn_ref[...], xhat, r).astype(BF16)

            use_cur = _band_geometry(per_step * (n_steps - 1 - step) + b)[0]
            pairs = range(N_HEADS // 2)
            qps, kbd, vbd = _attn_operands(q_ref.at[rows, :], kvc_ref.at[rows, :], kv_prev)
            probs = [pr_ref[rows, h * BLK:(h + 1) * BLK].astype(F32) for h in range(N_HEADS)]
            dyps = [dya_buf[rows, j * 128:(j + 1) * 128] for j in pairs]
            dps = []
            for j in pairs:
                dps += _merge(_nt(dyps[j], vbd[j // 4]), use_cur)
            deltas = [jnp.sum(p * dp, axis=-1, keepdims=True) for p, dp in zip(probs, dps)]
            dss = [p * (dp - delta) for p, dp, delta in zip(probs, dps, deltas)]
            delta_lanes = sum(jnp.where(lane == h, deltas[h], 0.0) for h in range(N_HEADS))
            gs_ref[...] -= jnp.sum(sp_ref[rows, :] * delta_lanes, axis=0, keepdims=True)
            ds4s = [_split(dss[2 * j], dss[2 * j + 1], use_cur).astype(BF16) for j in pairs]
            p4s = [_split(probs[2 * j], probs[2 * j + 1], use_cur).astype(BF16) for j in pairs]
            dqg_ref[rows, 0:D_ATTN] = jnp.concatenate([_nn(ds4s[j], kbd[j // 4]) * SCALE for j in pairs], axis=1).astype(BF16)
            sums = []
            for group in range(N_HEADS // HEADS_PER_KV):
                acc = [jnp.zeros((BLK, 128), F32) for _ in range(4)]
                for j in range(group * 4, group * 4 + 4):
                    for slot, part in enumerate(fold(_tn(ds4s[j], qps[j])) + fold(_tn(p4s[j], dyps[j]))):
                        acc[slot] = acc[slot] + part
                sums.append([a + pltpu.roll(a, 64, 1) for a in acc])
            dk_cur, dk_prev, dv_cur, dv_prev = (jnp.where(lane < 64, lo, hi) for lo, hi in zip(sums[0], sums[1]))
            dqg_ref[rows, 2 * D_ATTN:2 * D_ATTN + 2 * D_KV] = (jnp.concatenate([dk_cur, dv_cur], axis=1) + carry[...]).astype(BF16)
            carry[...] = jnp.concatenate([dk_prev, dv_prev], axis=1)

        for b in reversed(range(per_step)):
            one_block(b)

        @pl.when(step == n_steps - 1)
        def _():
            small_out[...] = small_ref[...]
            small_out[SMALL_NORM_ATTN:SMALL_NORM_ATTN + 1, :] = gna_ref[...]
            small_out[SMALL_MISC:SMALL_MISC + 1, 0:128] = small_ref[SMALL_MISC:SMALL_MISC + 1, 0:128] + gs_ref[...]

        for at, stage in zip(stage_steps, _rs_wout_stages(gwo_ref, gwo_sh, *rs_scratch)):
            pl.when(step == at)(stage)

    def rows(width):
        return pl.BlockSpec((per_step * BLK, width), lambda i: (n_steps - 1 - i, 0))

    kv_prev = pl.BlockSpec((BLK, 2 * D_KV), lambda i: (jnp.maximum(per_step * (n_steps - 1 - i) - 1, 0), 0))
    any_spec = pl.BlockSpec(memory_space=pl.ANY)
    return pl.pallas_call(
        body, name="attn_bwd", grid=(n_steps,),
        out_shape=(jax.ShapeDtypeStruct((seq, D_QG), BF16), pltpu.HBM(small.shape, F32),
                   pltpu.HBM((gwo.shape[0] // N_CHIPS, gwo.shape[1]), F32)),
        in_specs=[rows(D_ATTN), rows(2 * D_KV), kv_prev, rows(D_ATTN), rows(D_ATTN), rows(D_ATTN),
                  rows(N_HEADS * BLK), rows(128), _resident((1, D_ATTN)), any_spec, _resident(small.shape)],
        out_specs=(rows(D_QG), pl.BlockSpec(small.shape, lambda i: (0, 0)), any_spec),
        scratch_shapes=[pltpu.VMEM((1, D_ATTN), F32), pltpu.VMEM((1, 128), F32),
                        pltpu.VMEM((BLK, 2 * D_KV), F32), pltpu.VMEM((per_step * BLK, D_ATTN), BF16)] + _rs_wout_scratch(gwo.shape),
        compiler_params=_params(44, ("arbitrary",)),
    )(q, kv, kv, ga, ya, doa, probs, sink_probs, norm_attn_out, gwo, small)


GBLK = 256
GSUB = 64
PAIR_RING = 4
LAG_PAIR, LAG_HOP1, LAG_HOP2 = 1, 7, 14


def _bwd_in(dpc, dqg, h, wt, x, norm_in, dx2, small):
    seq = x.shape[0]
    n_blk = D_IN_PROJ // GBLK
    per_chip = n_blk // N_CHIPS
    n_slots = (n_blk + 1) // 2
    n_sub = GBLK // GSUB
    chip_rows = D_IN_PROJ // N_CHIPS
    tile = TOK_TILE
    n_tiles = seq // tile
    n_steps = n_blk + max(n_tiles, LAG_HOP2)
    chunk = min(seq, 512)
    blk_q, blk_kv, blk_ga = ROW_Q // GBLK, ROW_KV // GBLK, ROW_GA // GBLK

    def block_of(i):
        k = i % N_CHIPS
        robin = per_chip * ((k % 2) * 2 + k // 2) + i // N_CHIPS
        if isinstance(i, int):
            return robin if i < per_chip * N_CHIPS else i
        return jnp.where(i < per_chip * N_CHIPS, robin, i)

    def owner_of(i):
        return (i // N_CHIPS) % 2

    def slot_of(i):
        return (i // (2 * N_CHIPS)) * N_CHIPS + i % N_CHIPS

    def body(dpc_ref, dqg_ref, wt_ref, h_ref, x_ref, g_ref, dx2_ref, small_ref, gx_ref, small_sum, gwt_sh,
             dh_acc, gni, keep, pbuf, xbuf, land, land2, small_land,
             pair_send, pair_recv, h1_send, h1_recv, h2_send, h2_recv, sw_send, sw_recv, sm_send, sm_recv, out_sem):
        step = pl.program_id(0)
        x_i, y_i, c = lax.axis_index("x"), lax.axis_index("y"), lax.axis_index("c")
        me = 4 * x_i + 2 * y_i + c
        j = 2 * x_i + y_i
        pa = (_xor(x_i, 1 - c), _xor(y_i, c), c)
        pb = (_xor(x_i, c), _xor(y_i, 1 - c), c)
        sib = (x_i, y_i, 1 - c)
        ja = 2 * pa[0] + pa[1]
        jb = 2 * pb[0] + pb[1]
        jd = 3 - j

        def remote(src, dst, send, recv, to):
            return pltpu.make_async_remote_copy(src_ref=src, dst_ref=dst, send_sem=send, recv_sem=recv,
                                                device_id=to, device_id_type=MESH)

        def piece(ref, slot, u, n):
            return ref.at[slot, pl.ds(u * GSUB, n * GSUB), :]

        def chip_rows_at(ref, local, n):
            return ref.at[pl.ds(pl.multiple_of(local, GSUB), n * GSUB), :]

        def pair_copy(i):
            slot = slot_of(i)
            return remote(pbuf.at[i % PAIR_RING], land.at[slot], pair_send.at[slot], pair_recv.at[slot], sib)

        def h1_copy(slot, u, n):
            k = slot * n_sub + u
            return remote(piece(xbuf, slot, u, n), piece(xbuf, slot, u, n), h1_send.at[k], h1_recv.at[k], pa)

        def h2_copy(slot, u, n, local):
            k = slot * n_sub + u
            return remote(piece(xbuf, slot, u, n), chip_rows_at(land2, local, n), h2_send.at[k], h2_recv.at[k], pb)

        def sw_copy(slot, u, n, local):
            k = slot * n_sub + u
            return remote(piece(keep, slot, u, n), chip_rows_at(gwt_sh, local, n), sw_send.at[k], sw_recv.at[k], sib)

        def own_copy(slot, u, n, local):
            return pltpu.make_async_copy(piece(keep, slot, u, n), chip_rows_at(gwt_sh, local, n), out_sem.at[slot * n_sub + u])

        def owned(i):
            return (i >= 0) & (i < n_blk) & (owner_of(i) == c)

        def chip_of(blk, u):
            row = blk * GBLK + u * GSUB
            chip = row // chip_rows
            return chip, row - chip * chip_rows

        def pieces(blk):
            first, local = chip_of(blk, 0)
            whole = first == chip_of(blk, n_sub - 1)[0]
            if isinstance(blk, int):
                return [(True, 0, n_sub, first, local)] if whole else [(True, u, 1) + chip_of(blk, u) for u in range(n_sub)]
            return [(whole, 0, n_sub, first, local)] + [(jnp.logical_not(whole), u, 1) + chip_of(blk, u) for u in range(n_sub)]

        @pl.when(step == 0)
        def _():
            dh_acc[...] = jnp.zeros_like(dh_acc)
            gni[...] = jnp.zeros_like(gni)

        @pl.when(step < n_blk)
        def _():
            from_pc = block_of(step) < blk_q
            block = _tn(jnp.where(from_pc, dpc_ref[...], dqg_ref[...]), h_ref[...])
            for t in range(0, seq, chunk):
                d = jnp.where(from_pc, dpc_ref[t:t + chunk, :], dqg_ref[t:t + chunk, :])
                dh_acc[t:t + chunk, :] += _nn(d, wt_ref[...])

            @pl.when(owner_of(step) == c)
            def _():
                keep[slot_of(step)] = block

            @pl.when(owner_of(step) != c)
            def _():
                @pl.when(step >= 2 * PAIR_RING)
                def _():
                    pair_copy(step - 2 * PAIR_RING).wait_send()
                pbuf[step % PAIR_RING] = block.astype(BF16)
                pair_copy(step).start()

        i1 = step - LAG_PAIR

        @pl.when(owned(i1))
        def _():
            slot = slot_of(i1)
            pair_copy(i1).wait_recv()
            _accumulate(keep.at[slot], land.at[slot])
            for cond, u, n, chip, _ in pieces(block_of(i1)):
                @pl.when(cond & ((chip == ja) | (chip == jd)))
                def _(u=u, n=n):
                    _cast_rows(piece(keep, slot, u, n), piece(xbuf, slot, u, n))
                    h1_copy(slot, u, n).start()

        i2 = step - LAG_HOP1

        @pl.when(owned(i2))
        def _():
            slot = slot_of(i2)
            for cond, u, n, chip, local in pieces(block_of(i2)):
                @pl.when(cond & ((chip == j) | (chip == jb)))
                def _(u=u, n=n, chip=chip, local=local):
                    h1_copy(slot, u, n).wait_recv()
                    _accumulate(piece(keep, slot, u, n), piece(xbuf, slot, u, n))

                    @pl.when(chip == jb)
                    def _():
                        _cast_rows(piece(keep, slot, u, n), piece(xbuf, slot, u, n))
                        h2_copy(slot, u, n, local).start()

                @pl.when(cond & ((chip == ja) | (chip == jd)))
                def _(u=u, n=n):
                    h1_copy(slot, u, n).wait_send()

        i3 = step - LAG_HOP2

        @pl.when(owned(i3))
        def _():
            slot = slot_of(i3)
            for cond, u, n, chip, local in pieces(block_of(i3)):
                @pl.when(cond & (chip == j))
                def _(u=u, n=n, local=local):
                    h2_copy(slot, u, n, local).wait_recv()
                    _accumulate(piece(keep, slot, u, n), chip_rows_at(land2, local, n))
                    own_copy(slot, u, n, local).start()
                    sw_copy(slot, u, n, local).start()

                @pl.when(cond & (chip == jb))
                def _(u=u, n=n, local=local):
                    h2_copy(slot, u, n, local).wait_send()

        e = step - n_blk

        @pl.when((e >= 0) & (e < n_tiles))
        def _():
            dh = dh_acc[pl.ds(pl.multiple_of(e * tile, tile), tile), :]
            xv = x_ref[...]
            r = _rstd(xv)
            xhat = xv * r
            gni[...] += jnp.sum(dh * xhat, axis=0, keepdims=True)
            gx_ref[...] = _rms_bwd(dh * g_ref[...], xhat, r) + dx2_ref[...]

        @pl.when(step == n_steps - 1)
        def _():
            small_land[me] = small_ref[...]
            small_land[me, SMALL_NORM_IN:SMALL_NORM_IN + 1, :] = gni[...]
            others = [(dx, dy, dc) for dx in (0, 1) for dy in (0, 1) for dc in (0, 1)][1:]
            sends = [remote(small_land.at[me], small_land.at[me], sm_send.at[k], sm_recv.at[k],
                            (_xor(x_i, dx), _xor(y_i, dy), _xor(c, dc))) for k, (dx, dy, dc) in enumerate(others)]
            for cp in sends:
                cp.start()
            for i in range(n_blk):
                if i + 2 * PAIR_RING >= n_blk:
                    @pl.when(owner_of(i) != c)
                    def _(i=i):
                        pair_copy(i).wait_send()
                for _, u, n, chip, local in pieces(block_of(i)):
                    @pl.when((j == chip) & (c == owner_of(i)))
                    def _(i=i, u=u, n=n, local=local):
                        sw_copy(slot_of(i), u, n, local).wait_send()
                        own_copy(slot_of(i), u, n, local).wait()

                    @pl.when((j == chip) & (c != owner_of(i)))
                    def _(i=i, u=u, n=n, local=local):
                        sw_copy(slot_of(i), u, n, local).wait_recv()
            for cp in sends:
                cp.wait_recv()
            total = small_land[0]
            for dev in range(1, 8):
                total = total + small_land[dev]
            small_sum[...] = total
            for cp in sends:
                cp.wait_send()

    def blk_at(i):
        return block_of(jnp.clip(i, 0, n_blk - 1))

    last_pc_step = max(i for i in range(n_blk) if block_of(i) < blk_q)

    def next_block(i, in_pc):
        i = jnp.clip(i, 0, n_blk - 1)
        step = jnp.full_like(i, last_pc_step if in_pc else n_blk - 1)
        for ahead in reversed(range(N_CHIPS)):
            cand = jnp.minimum(i + ahead, n_blk - 1)
            step = jnp.where((block_of(cand) < blk_q) == in_pc, cand, step)
        return block_of(step)

    def dqg_block(i):
        b = next_block(i, False)
        q_blk = jnp.clip(b - blk_q, 0, blk_kv - blk_q - 1)
        ga_blk = (D_ATTN // GBLK) + jnp.clip(b - blk_ga, 0, n_blk - blk_ga - 1)
        return jnp.where(b < blk_kv, q_blk, jnp.where(b == blk_kv, 2 * D_ATTN // GBLK, ga_blk))

    def tok(i):
        return (jnp.clip(i - n_blk, 0, n_tiles - 1), 0)

    n_piece = n_slots * n_sub
    dma = pltpu.SemaphoreType.DMA
    return pl.pallas_call(
        body, name="bwd_in", grid=(n_steps,),
        out_shape=(jax.ShapeDtypeStruct((seq, D_MODEL), F32), jax.ShapeDtypeStruct(small.shape, F32),
                   jax.ShapeDtypeStruct((chip_rows, D_MODEL), F32)),
        in_specs=[pl.BlockSpec((seq, GBLK), lambda i: (0, next_block(i, True))),
                  pl.BlockSpec((seq, GBLK), lambda i: (0, dqg_block(i))),
                  pl.BlockSpec((GBLK, D_MODEL), lambda i: (blk_at(i), 0)),
                  _resident(h.shape),
                  pl.BlockSpec((tile, D_MODEL), tok), _resident((1, D_MODEL)), pl.BlockSpec((tile, D_MODEL), tok),
                  _resident(small.shape)],
        out_specs=(pl.BlockSpec((tile, D_MODEL), tok), pl.BlockSpec(small.shape, lambda i: (0, 0)),
                   pl.BlockSpec(memory_space=pl.ANY)),
        scratch_shapes=[pltpu.VMEM((seq, D_MODEL), F32), pltpu.VMEM((1, D_MODEL), F32),
                        pltpu.VMEM((n_slots, GBLK, D_MODEL), F32), pltpu.VMEM((PAIR_RING, GBLK, D_MODEL), BF16),
                        pltpu.VMEM((n_slots, GBLK, D_MODEL), BF16), pltpu.VMEM((n_slots, GBLK, D_MODEL), BF16),
                        pltpu.VMEM((chip_rows, D_MODEL), BF16), pltpu.VMEM((8,) + small.shape, F32),
                        dma((n_slots,)), dma((n_slots,)), dma((n_piece,)), dma((n_piece,)), dma((n_piece,)),
                        dma((n_piece,)), dma((n_piece,)), dma((n_piece,)), dma((7,)), dma((7,)), dma((n_piece,))],
        compiler_params=_params(62, ("arbitrary",)),
    )(dpc, dqg, wt, h, x, norm_in, dx2, small)


def _accumulate(dst_ref, src_ref, rows=16):
    def step(i, carry):
        sl = pl.ds(pl.multiple_of(i * rows, rows), rows)
        dst_ref[sl, :] = dst_ref[sl, :] + src_ref[sl, :].astype(F32)
        return carry
    lax.fori_loop(0, dst_ref.shape[0] // rows, step, 0)


def _rs_wout_scratch(gwo_shape):
    o_half, width = gwo_shape[0] // N_CHIPS // 2, gwo_shape[1]
    return [pltpu.VMEM((4, o_half, width), F32), pltpu.VMEM((4, o_half, width), BF16),
            pltpu.VMEM((4, o_half, width), BF16), pltpu.VMEM((2, o_half, width), BF16),
            pltpu.VMEM((o_half, width), BF16),
            pltpu.SemaphoreType.DMA((8,)), pltpu.SemaphoreType.DMA((8,)), pltpu.SemaphoreType.DMA((4,))]


def _rs_wout_stages(gwo_ref, gwo_sh, acc_o, sb_o, r1o, r2o, r3o, send_sems, recv_sems, local_sems):
    o_rows = gwo_ref.shape[0] // N_CHIPS
    o_half = o_rows // 2
    x, y, c = lax.axis_index("x"), lax.axis_index("y"), lax.axis_index("c")
    j = 2 * x + y
    pa = (_xor(x, 1 - c), _xor(y, c), c)
    pb = (_xor(x, c), _xor(y, 1 - c), c)
    sib = (x, y, 1 - c)
    ja = 2 * pa[0] + pa[1]
    jb = 2 * pb[0] + pb[1]
    jd = 3 - j
    order = (ja, jd, jb, j)
    sib_order = (jb, jd, ja, j)

    def rcopy(k, src, dst, to):
        return pltpu.make_async_remote_copy(src_ref=src, dst_ref=dst, send_sem=send_sems.at[k],
                                            recv_sem=recv_sems.at[k], device_id=to, device_id_type=MESH)

    def o_rows_of(chip, half):
        return gwo_ref.at[pl.ds(pl.multiple_of(chip * o_rows + half * o_half, 8), o_half), :]

    def loads(chips, half):
        return [pltpu.make_async_copy(o_rows_of(chip, half), acc_o.at[s], local_sems.at[s]) for s, chip in enumerate(chips)]

    def pair_send(s):
        return rcopy(s, sb_o.at[s], r1o.at[s], sib)

    def out_half(half):
        return gwo_sh.at[pl.ds(pl.multiple_of(half * o_half, 8), o_half), :]

    hop1 = [rcopy(4 + s, sb_o.at[s], r2o.at[s], pa) for s in range(2)]
    hop2 = rcopy(6, sb_o.at[2], r3o, pb)
    swap = rcopy(7, acc_o.at[3], out_half(c), sib)
    mine = pltpu.make_async_copy(acc_o.at[3], out_half(c), local_sems.at[0])

    def resend(s, copy):
        pair_send(s).wait_send()
        _cast_rows(acc_o.at[s], sb_o.at[s])
        copy.start()

    def stage_load():
        for cp in loads(sib_order, 1 - c):
            cp.start()

    def stage_pair():
        for s, (cp, mine_in) in enumerate(zip(loads(sib_order, 1 - c), loads(order, c))):
            cp.wait()
            _cast_rows(acc_o.at[s], sb_o.at[s])
            pair_send(s).start()
            mine_in.start()

    def stage_hop1():
        for s, cp in enumerate(loads(order, c)):
            cp.wait()
            pair_send(s).wait_recv()
            _accumulate(acc_o.at[s], r1o.at[s])
            if s < 2:
                resend(s, hop1[s])

    def stage_hop2():
        hop1[1].wait_recv()
        _accumulate(acc_o.at[2], r2o.at[1])
        resend(2, hop2)
        hop1[0].wait_recv()
        _accumulate(acc_o.at[3], r2o.at[0])

    def stage_final():
        hop2.wait_recv()
        _accumulate(acc_o.at[3], r3o)
        mine.start()
        swap.start()

    def stage_drain():
        rcopy(7, acc_o.at[3], out_half(1 - c), sib).wait_recv()
        for cp in [pair_send(3)] + hop1 + [hop2, swap]:
            cp.wait_send()
        mine.wait()

    return stage_load, stage_pair, stage_hop1, stage_hop2, stage_final, stage_drain


def _adamw_big(groups, passed):
    arrays = [a for group in groups for a in group[:4]]
    steps = [group[0].shape[0] // group[4] for group in groups]
    first = [sum(steps[:k]) for k in range(len(steps) + 1)]
    n = len(arrays)

    def body(*refs):
        ins, passed_in, outs, passed_out = refs[:n], refs[n], refs[n + 1:2 * n + 1], refs[2 * n + 1]
        i = pl.program_id(0)
        for k in range(len(groups)):
            @pl.when((i >= first[k]) & (i < first[k + 1]))
            def _(k=k):
                w_ref, g_ref, m_ref, v_ref = ins[4 * k:4 * k + 4]
                _adamw_update(w_ref, g_ref[...], m_ref, v_ref, *outs[4 * k:4 * k + 4])

        @pl.when(i < steps[0])
        def _():
            passed_out[...] = passed_in[...]

    def spec(k, rows, width):
        return pl.BlockSpec((rows, width), lambda i: (jnp.clip(i - first[k], 0, steps[k] - 1), 0))

    specs = [spec(k, group[4], group[0].shape[1]) for k, group in enumerate(groups) for _ in range(4)]
    specs.append(spec(0, passed.shape[0] // steps[0], passed.shape[1]))
    out = pl.pallas_call(
        body, name="adamw_big", grid=(first[-1],),
        out_shape=tuple(jax.ShapeDtypeStruct(a.shape, a.dtype) for a in arrays + [passed]),
        in_specs=specs, out_specs=tuple(specs),
        compiler_params=_params(40, ("arbitrary",)),
    )(*arrays, passed)
    return [tuple(out[4 * k:4 * k + 4]) for k in range(len(groups))], out[-1]


def _adamw_update(w_ref, gv, m_ref, v_ref, go_ref, d_ref, nm_ref, nv_ref, at=...):
    go_ref[at] = gv
    nm = ADAM_B1 * m_ref[at] + (1.0 - ADAM_B1) * gv
    nv = ADAM_B2 * v_ref[at] + (1.0 - ADAM_B2) * (gv * gv)
    m_hat = nm / (1.0 - ADAM_B1 ** ADAM_STEP)
    v_hat = nv / (1.0 - ADAM_B2 ** ADAM_STEP)
    d_ref[at] = -ADAM_LR * (m_hat / (jnp.sqrt(v_hat) + ADAM_EPS) + ADAM_WD * w_ref[at])
    nm_ref[at] = nm
    nv_ref[at] = nv


def _adamw_small(chip, small_sum, weights, grads_of, ms, vs):
    n = len(weights)

    def body(chip_ref, small_ref, *refs):
        ins, outs, loss_ref = refs[:3 * n], refs[3 * n:-1], refs[-1]
        for k in range(n):
            w_ref, m_ref, v_ref = ins[3 * k:3 * k + 3]
            for at, gv in grads_of[k](small_ref, chip_ref):
                _adamw_update(w_ref, gv, m_ref, v_ref, *outs[4 * k:4 * k + 4], at=at)
        loss_ref[...] = small_ref[SMALL_MISC:SMALL_MISC + 1, SMALL_LOSS_LANE:SMALL_LOSS_LANE + 1]

    flat = [a for group in zip(weights, ms, vs) for a in group]
    vmem = pl.BlockSpec(memory_space=pltpu.VMEM)
    out = pl.pallas_call(
        body, name="adamw_small",
        out_shape=tuple(jax.ShapeDtypeStruct(w.shape, F32) for w in weights for _ in range(4))
        + (jax.ShapeDtypeStruct((1, 1), F32),),
        in_specs=[pl.BlockSpec(memory_space=pltpu.SMEM)] + [vmem] * (1 + 3 * n), out_specs=(vmem,) * (4 * n + 1),
    )(chip, small_sum, *flat)
    return [tuple(out[4 * k:4 * k + 4]) for k in range(n)], out[-1][0, 0]


def kernel(x, norm_in, w_in, conv_w, attn_sinks, norm_conv_out, norm_attn_out, w_out, norm_final, loss_target, m_norm_in, m_w_in, m_conv_w, m_attn_sinks, m_norm_conv_out, m_norm_attn_out, m_w_out, m_norm_final, v_norm_in, v_w_in, v_conv_w, v_attn_sinks, v_norm_conv_out, v_norm_attn_out, v_w_out, v_norm_final):
    chip = 2 * lax.axis_index("x") + lax.axis_index("y")
    xs, target = x[0], loss_target[0]
    norm_final2 = norm_final.reshape(1, D_MODEL)
    w_in_t, m_w_in_t, v_w_in_t = w_in[0].T, m_w_in[0].T, v_w_in[0].T

    def from_hbm(*arrays):
        return tuple(pltpu.with_memory_space_constraint(a, pltpu.HBM) for a in arrays)

    h, pc, q, kv, ga, oc, ya, oa, probs, sink_probs, wt, cw, wo = _fwd_in(
        xs, norm_in, w_in_t, w_out[0], conv_w.transpose(1, 0, 2), norm_conv_out, attn_sinks, norm_attn_out)
    dx2, doa, gwo, dpc, small = _out_proj_loss(xs, oc, oa, wo, norm_final2, target, pc, cw, norm_conv_out)
    q, kv, ga, ya, doa, probs, gwo, small = from_hbm(q, kv, ga, ya, doa, probs, gwo, small)
    dqg, small, gwo_sh = _attn_bwd(q, kv, ga, ya, doa, probs, sink_probs, norm_attn_out, gwo, small)
    grad_x, small_sum, gwt_sh = _bwd_in(dpc, dqg, h, wt, xs, norm_in, dx2, small)

    (up_w_in, up_w_out), grad_x = _adamw_big(
        [from_hbm(w_in_t, gwt_sh, m_w_in_t, v_w_in_t) + (200,), from_hbm(w_out[0], gwo_sh, m_w_out[0], v_w_out[0]) + (128,)],
        *from_hbm(grad_x))
    up_w_in = tuple(o.T[None] for o in up_w_in)
    up_w_out = tuple(o[None] for o in up_w_out)

    def row_of(r):
        return lambda small_ref, chip_ref: [(..., small_ref[r:r + 1, :])]

    def sink_lanes(small_ref, chip_ref):
        return [(..., small_ref[SMALL_MISC:SMALL_MISC + 1, 0:N_HEADS])]

    def conv_taps(small_ref, chip_ref):
        width = D_CONV // N_CHIPS
        cols = pl.ds(pl.multiple_of(chip_ref[0] * width, width), width)
        return [(k, small_ref[pl.ds(SMALL_CONV_W + k, 1), cols]) for k in range(conv_w.shape[1])]

    def small_view(a):
        return a.transpose(1, 0, 2) if a.ndim == 3 else a.reshape(-1, a.shape[-1])

    small_w = (norm_in, conv_w, attn_sinks, norm_conv_out, norm_attn_out, norm_final)
    small_m = (m_norm_in, m_conv_w, m_attn_sinks, m_norm_conv_out, m_norm_attn_out, m_norm_final)
    small_v = (v_norm_in, v_conv_w, v_attn_sinks, v_norm_conv_out, v_norm_attn_out, v_norm_final)
    small_g = (row_of(SMALL_NORM_IN), conv_taps, sink_lanes, row_of(SMALL_NORM_CONV), row_of(SMALL_NORM_ATTN),
               row_of(SMALL_NORM_FINAL))
    w_views, m_views, v_views = (tuple(small_view(a) for a in group) for group in (small_w, small_m, small_v))
    up_small, loss = _adamw_small(chip.astype(jnp.int32).reshape(1), small_sum, w_views, small_g, m_views, v_views)
    up_small = [tuple(o.transpose(1, 0, 2) if w.ndim == 3 else o.reshape(w.shape) for o in up)
                for up, w in zip(up_small, small_w)]
    up_norm_in, up_conv_w, up_sinks, up_norm_conv, up_norm_attn, up_norm_final = up_small
    updates = (up_norm_in, up_w_in, up_conv_w, up_sinks, up_norm_conv, up_norm_attn, up_w_out, up_norm_final)
    grads_out, deltas, new_m, new_v = zip(*updates)
    return (loss, grad_x[None], *grads_out, *deltas, *new_m, *new_v)
```

```python
import jax
import jax.numpy as jnp
from jax import lax
from jax.experimental import pallas as pl
from jax.experimental.pallas import tpu as pltpu

F32 = jnp.float32
BF16 = jnp.bfloat16
MESH = pl.DeviceIdType.MESH

D_MODEL = 1024
D_CONV = 1024
D_ATTN = 1024
D_KV = 128
D_QG = 2 * D_ATTN + 2 * D_KV
D_MIX = D_CONV + D_ATTN
D_PC = 4 * D_CONV
D_IN_PROJ = D_PC + 2 * D_ATTN + 2 * D_KV
ROW_Q = D_PC
ROW_KV = ROW_Q + D_ATTN
ROW_GA = ROW_KV + 2 * D_KV
N_HEADS = 16
HEAD_DIM = 64
HEADS_PER_KV = 8
BLK = 128
N_CHIPS = 4
RMS_EPS = 1e-5
SCALE = HEAD_DIM ** -0.5
SLOPES = tuple(2.0 ** (-8.0 * (h + 1) / N_HEADS) for h in range(N_HEADS))

ADAM_LR, ADAM_B1, ADAM_B2, ADAM_EPS, ADAM_WD, ADAM_STEP = 0.001, 0.9, 0.999, 1e-08, 0.01, 10

SMALL_ROWS = 8
SMALL_NORM_IN, SMALL_NORM_CONV, SMALL_NORM_ATTN, SMALL_NORM_FINAL, SMALL_CONV_W, SMALL_MISC = 0, 1, 2, 3, 4, 7
SMALL_LOSS_LANE = N_HEADS

TOK_TILE = 256
PC_PIECE = 512
MIB = 1 << 20


def _params(vmem_mib, semantics=None):
    return pltpu.CompilerParams(dimension_semantics=semantics, vmem_limit_bytes=vmem_mib * MIB)


def _nn(a, b):
    return jnp.dot(a, b, preferred_element_type=F32)


def _nt(a, b):
    return lax.dot_general(a, b, (((1,), (1,)), ((), ())), preferred_element_type=F32)


def _tn(a, b):
    return lax.dot_general(a, b, (((0,), (0,)), ((), ())), preferred_element_type=F32)


def _rstd(v):
    return lax.rsqrt(jnp.mean(v * v, axis=-1, keepdims=True) + RMS_EPS)


def _rms_bwd(g, xhat, rstd):
    return rstd * (g - xhat * jnp.mean(g * xhat, axis=-1, keepdims=True))


def _silu_and_grad(g):
    s = jax.nn.sigmoid(g)
    return g * s, s * (1.0 + g * (1.0 - s))


def _resident(shape):
    return pl.BlockSpec(shape, lambda *_: (0,) * len(shape), pipeline_mode=pl.Buffered(1))


def _xor(a, b):
    return a + b - 2 * a * b


def _cast_rows(src_ref, dst_ref, rows=32):
    def step(i, carry):
        sl = pl.ds(pl.multiple_of(i * rows, rows), rows)
        dst_ref[sl, :] = src_ref[sl, :].astype(dst_ref.dtype)
        return carry
    lax.fori_loop(0, src_ref.shape[0] // rows, step, 0)


def _ag_scratch(shard_shape):
    rows, width = shard_shape
    return [pltpu.VMEM((rows, width), F32), pltpu.VMEM((rows, width), BF16), pltpu.VMEM((3, rows // 2, width), BF16),
            pltpu.SemaphoreType.DMA((6,)), pltpu.SemaphoreType.DMA((6,)), pltpu.SemaphoreType.DMA((4,))]


def _ag_stages(sh_ref, out, f32_buf, own, land, send_sems, recv_sems, local_sems):
    rows = sh_ref.shape[0]
    half = rows // 2
    x, y, c = lax.axis_index("x"), lax.axis_index("y"), lax.axis_index("c")
    j = 2 * x + y
    p1 = (_xor(x, c), _xor(y, 1 - c), c)
    p2 = (_xor(x, 1 - c), _xor(y, c), c)
    sib = (x, y, 1 - c)
    j1 = 2 * p1[0] + p1[1]
    j2 = 2 * p2[0] + p2[1]
    j3 = 3 - j

    def rows_of(chip, hf):
        return out.at[pl.ds(pl.multiple_of(chip * rows + hf * half, 16), half), :]

    def rcopy(k, src, dst, to):
        return pltpu.make_async_remote_copy(src_ref=src, dst_ref=dst, send_sem=send_sems.at[k],
                                            recv_sem=recv_sems.at[k], device_id=to, device_id_type=MESH)

    my_half = own.at[pl.ds(pl.multiple_of(c * half, 16), half), :]
    hop1 = rcopy(0, my_half, land.at[0], p1)
    hop2_own = rcopy(1, my_half, land.at[1], p2)
    hop2_fwd = rcopy(2, land.at[0], land.at[2], p2)
    swaps = [rcopy(3 + s, land.at[s], rows_of(chip, c), sib) for s, chip in enumerate((j1, j2, j3))]
    keeps = [pltpu.make_async_copy(land.at[s], rows_of(chip, c), local_sems.at[1 + s]) for s, chip in enumerate((j1, j2, j3))]
    load = pltpu.make_async_copy(sh_ref, f32_buf, local_sems.at[0])
    own_out = pltpu.make_async_copy(own, out.at[pl.ds(pl.multiple_of(j * rows, 16), rows), :], local_sems.at[0])

    def stage_load():
        load.start()

    def stage_send():
        load.wait()
        _cast_rows(f32_buf, own)
        own_out.start()
        hop1.start()

    def stage_forward():
        hop1.wait_recv()
        hop2_own.start()
        hop2_fwd.start()
        swaps[0].start()
        keeps[0].start()

    def stage_publish():
        hop2_own.wait_recv()
        swaps[1].start()
        keeps[1].start()
        hop2_fwd.wait_recv()
        swaps[2].start()
        keeps[2].start()

    def stage_drain():
        for s, chip in enumerate((j2, j1, j3)):
            rcopy(3 + s, my_half, rows_of(chip, 1 - c), sib).wait_recv()
        for cp in [hop1, hop2_own, hop2_fwd] + swaps:
            cp.wait_send()
        for cp in [own_out] + keeps:
            cp.wait()

    return stage_load, stage_send, stage_forward, stage_publish, stage_drain


AG_CAST_ROWS = 400


def _gather_resident(sh_ref, out, f32_buf, send_sems, recv_sems, local_sems):
    rows = sh_ref.shape[0]
    half = rows // 2
    x, y, c = lax.axis_index("x"), lax.axis_index("y"), lax.axis_index("c")
    j = 2 * x + y
    p1 = (_xor(x, c), _xor(y, 1 - c), c)
    p2 = (_xor(x, 1 - c), _xor(y, c), c)
    sib = (x, y, 1 - c)
    j1 = 2 * p1[0] + p1[1]
    j2 = 2 * p2[0] + p2[1]
    j3 = 3 - j

    def rows_of(chip, hf):
        return out.at[pl.ds(pl.multiple_of(chip * rows + hf * half, 16), half), :]

    def send(k, chip, to):
        return pltpu.make_async_remote_copy(src_ref=rows_of(chip, c), dst_ref=rows_of(chip, c), send_sem=send_sems.at[k],
                                            recv_sem=recv_sems.at[k], device_id=to, device_id_type=MESH)

    per_half = half // AG_CAST_ROWS

    chunks = [c * per_half + k for k in range(per_half)] + [(1 - c) * per_half + k for k in range(per_half)]

    def load(n):
        lo = pl.multiple_of(chunks[n] * AG_CAST_ROWS, 16)
        return pltpu.make_async_copy(sh_ref.at[pl.ds(lo, AG_CAST_ROWS), :], f32_buf.at[n % 2], local_sems.at[n % 2])

    hop1 = send(0, j, p1)
    load(0).start()
    load(1).start()
    for n in range(len(chunks)):
        load(n).wait()
        lo = pl.multiple_of(chunks[n] * AG_CAST_ROWS, 16)
        _cast_rows(f32_buf.at[n % 2], out.at[pl.ds(pl.multiple_of(j * rows + lo, 16), AG_CAST_ROWS), :], rows=16)
        if n + 2 < len(chunks):
            load(n + 2).start()
        if n == per_half - 1:
            hop1.start()
    hop1.wait_recv()
    sends = [hop1, send(1, j, p2), send(2, j1, p2), send(3, j1, sib)]
    for cp in sends[1:]:
        cp.start()
    sends[1].wait_recv()
    sends.append(send(4, j2, sib))
    sends[-1].start()
    sends[2].wait_recv()
    sends.append(send(5, j3, sib))
    sends[-1].start()
    for k in (3, 4, 5):
        send(k, j, sib).wait_recv()
    for cp in sends:
        cp.wait_send()


def _fwd_in(x, norm_in, wt_sh, wo_sh, cw_sh, norm_conv_out, sinks, norm_attn_out):
    seq = x.shape[0]
    tile = TOK_TILE
    n_tiles = seq // tile
    stage_steps = (0, n_tiles // 4, (5 * n_tiles) // 8, n_tiles - 1)
    blocks = tile // BLK
    cw_cols = cw_sh.shape[-1]

    def body(x_ref, g_ref, wtsh_ref, wo_ref, cwsh_ref, gn_ref, sink_ref, gna_ref,
             h_ref, pc_ref, q_ref, kv_ref, ga_ref, oc_ref, ya_ref, oa_ref, pr_ref, sp_ref, wt_out, cw_ref, wo_out,
             zbuf, kv_last, wt_ref, f32_buf, cw_land, wt_send, wt_recv, wt_local, cw_send, cw_recv, cw_local, *ag_scratch):
        step = pl.program_id(0)
        to_hbm = pltpu.make_async_copy(wt_ref, wt_out, wt_local.at[0])
        load_wo, *stages = _ag_stages(wo_ref, wo_out, *ag_scratch)

        @pl.when(step == 0)
        def _():
            load_wo()
            zbuf[0:8, :] = jnp.zeros((8, D_CONV), F32)
            kv_last[...] = jnp.zeros_like(kv_last)
            x_i, y_i, c = lax.axis_index("x"), lax.axis_index("y"), lax.axis_index("c")
            j = 2 * x_i + y_i
            p1 = (_xor(x_i, c), _xor(y_i, 1 - c), c)
            p2 = (_xor(x_i, 1 - c), _xor(y_i, c), c)
            j1 = 2 * p1[0] + p1[1]

            def cw_copy(k, src, chip, to):
                return pltpu.make_async_remote_copy(src_ref=src, dst_ref=cw_land.at[chip], send_sem=cw_send.at[k],
                                                    recv_sem=cw_recv.at[k], device_id=to, device_id_type=MESH)

            mine = pltpu.make_async_copy(cwsh_ref, cw_land.at[j], cw_local.at[0])
            mine.start()
            first = cw_copy(0, cwsh_ref, j, p1)
            first.start()
            _gather_resident(wtsh_ref, wt_ref, f32_buf, wt_send, wt_recv, wt_local)
            to_hbm.start()
            first.wait_recv()
            second = [cw_copy(1, cwsh_ref, j, p2), cw_copy(2, cw_land.at[j1], j1, p2)]
            for cp in second:
                cp.start()
            for cp in second:
                cp.wait_recv()
            for cp in [first] + second:
                cp.wait_send()
            mine.wait()
            for chip in range(N_CHIPS):
                for tap in range(cw_sh.shape[0]):
                    cw_ref[tap:tap + 1, chip * cw_cols:(chip + 1) * cw_cols] = cw_land[chip, tap]

        for at, stage in zip(stage_steps[:-1], stages[:-1]):
            pl.when(step == at)(stage)

        def attention(b):
            rows = pl.ds(b * BLK, BLK)
            kv_prev = kv_last if b == 0 else kv_ref.at[pl.ds((b - 1) * BLK, BLK), :]
            return _attn_forward(q_ref.at[rows, :], kv_ref.at[rows, :], kv_prev, ga_ref.at[rows, :], sink_ref, gna_ref,
                                 _band_geometry(step * blocks + b), ya_ref.at[rows, :], oa_ref.at[rows, :],
                                 pr_ref.at[rows, :], sp_ref.at[rows, :])

        xv = x_ref[...]
        h = (xv * _rstd(xv) * g_ref[...]).astype(BF16)
        h_ref[...] = h
        q_ref[...] = _nt(h, wt_ref[ROW_Q:ROW_KV, :])
        kv_ref[...] = _nt(h, wt_ref[ROW_KV:ROW_GA, :])
        ga_ref[...] = _nt(h, wt_ref[ROW_GA:D_IN_PROJ, :])
        attention_blocks = [attention(b) for b in range(blocks)]
        for lo in range(0, D_PC, PC_PIECE):
            pc_ref[:, lo:lo + PC_PIECE] = _nt(h, wt_ref[lo:lo + PC_PIECE, :])
            for stages_of_block in attention_blocks:
                next(stages_of_block, None)
        for stages_of_block in attention_blocks:
            for _ in stages_of_block:
                pass
        kv_last[...] = kv_ref[tile - BLK:tile, :]

        cb, _, _, _, _, _, conv = _conv_core(pc_ref, zbuf, cw_ref)
        yc = cb * conv
        silu, _ = _silu_and_grad(pc_ref[:, 3 * D_CONV:4 * D_CONV])
        oc_ref[...] = (yc * _rstd(yc) * gn_ref[...] * silu).astype(BF16)
        zbuf[0:8, :] = zbuf[tile:tile + 8, :]

        @pl.when(step == stage_steps[-1])
        def _():
            stages[-1]()
            to_hbm.wait()

    def row(width):
        return pl.BlockSpec((tile, width), lambda i: (i, 0))

    any_spec = pl.BlockSpec(memory_space=pl.ANY)
    dma = pltpu.SemaphoreType.DMA
    wt_shape = (N_CHIPS * wt_sh.shape[0], wt_sh.shape[1])
    cw_shape = (cw_sh.shape[0], N_CHIPS * cw_cols)
    return pl.pallas_call(
        body, name="fwd_in", grid=(n_tiles,),
        out_shape=(jax.ShapeDtypeStruct((seq, D_MODEL), BF16), jax.ShapeDtypeStruct((seq, D_PC), F32),
                   jax.ShapeDtypeStruct((seq, D_ATTN), F32), jax.ShapeDtypeStruct((seq, 2 * D_KV), F32),
                   jax.ShapeDtypeStruct((seq, D_ATTN), F32), jax.ShapeDtypeStruct((seq, D_CONV), BF16),
                   pltpu.HBM((seq, D_ATTN), F32), pltpu.HBM((seq, D_ATTN), BF16),
                   pltpu.HBM((seq, N_HEADS * BLK), BF16), pltpu.HBM((seq, 128), F32),
                   pltpu.HBM(wt_shape, BF16), jax.ShapeDtypeStruct(cw_shape, F32),
                   jax.ShapeDtypeStruct((N_CHIPS * wo_sh.shape[0], wo_sh.shape[1]), BF16)),
        in_specs=[row(D_MODEL), _resident((1, D_MODEL)), any_spec, any_spec, any_spec, _resident((1, D_CONV)),
                  pl.BlockSpec(memory_space=pltpu.SMEM), _resident((1, D_ATTN))],
        out_specs=(row(D_MODEL), row(D_PC), row(D_ATTN), row(2 * D_KV), row(D_ATTN), row(D_CONV), row(D_ATTN),
                   row(D_ATTN), row(N_HEADS * BLK), row(128), any_spec, pl.BlockSpec(cw_shape, lambda i: (0, 0)),
                   any_spec),
        scratch_shapes=[pltpu.VMEM((tile + 8, D_CONV), F32), pltpu.VMEM((BLK, 2 * D_KV), F32),
                        pltpu.VMEM(wt_shape, BF16), pltpu.VMEM((2, AG_CAST_ROWS, wt_sh.shape[1]), F32),
                        pltpu.VMEM((N_CHIPS,) + cw_sh.shape, F32),
                        dma((6,)), dma((6,)), dma((2,)), dma((3,)), dma((3,)), dma((1,))] + _ag_scratch(wo_sh.shape),
        compiler_params=_params(62, ("arbitrary",)),
    )(x, norm_in, wt_sh, wo_sh, cw_sh, norm_conv_out, sinks, norm_attn_out)


def _conv_core(pc_ref, zbuf, cw_ref):
    tile = pc_ref.shape[0]
    cb = pc_ref[:, 0:D_CONV]
    cc = pc_ref[:, D_CONV:2 * D_CONV]
    cu = pc_ref[:, 2 * D_CONV:3 * D_CONV]
    z = cc * cu
    zbuf[8:tile + 8, :] = z
    z1 = zbuf[7:tile + 7, :]
    z2 = zbuf[6:tile + 6, :]
    conv = cw_ref[0:1, :] * z2 + cw_ref[1:2, :] * z1 + cw_ref[2:3, :] * z
    return cb, cc, cu, z, z1, z2, conv


def _band_geometry(block_index):
    qi = lax.broadcasted_iota(jnp.int32, (BLK, BLK), 0)
    kp = lax.broadcasted_iota(jnp.int32, (BLK, BLK), 1)
    use_cur = kp <= qi
    dist = jnp.where(use_cur, qi - kp, qi - kp + BLK).astype(F32)
    valid = use_cur | (block_index > 0)
    return use_cur, dist, valid


def _block_diag(cur, prev, group):
    lane = lax.broadcasted_iota(jnp.int32, cur.shape, 1)

    def halves(t):
        other = pltpu.roll(t, 64, 1)
        lo, hi = (t, other) if group == 0 else (other, t)
        return jnp.where(lane < 64, lo, 0.0), jnp.where(lane >= 64, hi, 0.0)

    return jnp.concatenate(halves(cur) + halves(prev), axis=0).astype(BF16)


def _merge(s4, use_cur):
    return (jnp.where(use_cur, s4[:, 0:BLK], s4[:, 2 * BLK:3 * BLK]),
            jnp.where(use_cur, s4[:, BLK:2 * BLK], s4[:, 3 * BLK:4 * BLK]))


def _split(a, b, use_cur):
    return jnp.concatenate([jnp.where(use_cur, a, 0.0), jnp.where(use_cur, b, 0.0),
                            jnp.where(use_cur, 0.0, a), jnp.where(use_cur, 0.0, b)], axis=1)


def _softmax_head(s, head, sink, dist, valid):
    sc = jnp.where(valid, s - SLOPES[head] * dist, -jnp.inf)
    m = jnp.maximum(jnp.max(sc, axis=-1, keepdims=True), sink)
    p = jnp.exp(sc - m)
    es = jnp.exp(sink - m)
    inv = 1.0 / (jnp.sum(p, axis=-1, keepdims=True) + es)
    return p * inv, es * inv


def _attn_operands(q_ref, kvc_ref, kvp_ref):
    groups = range(N_HEADS // HEADS_PER_KV)
    kbd = [_block_diag(kvc_ref[:, 0:D_KV], kvp_ref[:, 0:D_KV], g) for g in groups]
    vbd = [_block_diag(kvc_ref[:, D_KV:2 * D_KV], kvp_ref[:, D_KV:2 * D_KV], g) for g in groups]
    qps = [(q_ref[:, j * 128:(j + 1) * 128] * SCALE).astype(BF16) for j in range(N_HEADS // 2)]
    return qps, kbd, vbd


def _attn_forward(q_ref, kvc_ref, kvp_ref, ga_ref, sink_ref, gn_ref, geometry, ya_ref, oa_ref, pr_ref, sp_ref):
    use_cur, dist, valid = geometry
    _, kbd, vbd = operands = _attn_operands(q_ref, kvc_ref, kvp_ref)
    yield
    scores = []
    for j, qp in enumerate(operands[0]):
        scores += _merge(_nt(qp, kbd[j // 4]), use_cur)
    yield
    sinks = [sink_ref[0, h] for h in range(N_HEADS)]
    scores = [jnp.where(valid, s - SLOPES[h] * dist, -jnp.inf) for h, s in enumerate(scores)]
    maxes = [jnp.maximum(jnp.max(s, axis=-1, keepdims=True), sinks[h]) for h, s in enumerate(scores)]
    yield
    exps = [jnp.exp(s - m) for s, m in zip(scores, maxes)]
    sink_exps = [jnp.exp(sinks[h] - m) for h, m in enumerate(maxes)]
    yield
    invs = [1.0 / (jnp.sum(e, axis=-1, keepdims=True) + se) for e, se in zip(exps, sink_exps)]
    probs = [e * inv for e, inv in zip(exps, invs)]
    pr_ref[...] = jnp.concatenate(probs, axis=1).astype(BF16)
    lane = lax.broadcasted_iota(jnp.int32, (BLK, 128), 1)
    sp_ref[...] = sum(jnp.where(lane == h, se * inv, 0.0) for h, (se, inv) in enumerate(zip(sink_exps, invs)))
    yield
    p4s = [_split(probs[2 * j], probs[2 * j + 1], use_cur).astype(BF16) for j in range(N_HEADS // 2)]
    ya = jnp.concatenate([_nn(p4, vbd[j // 4]) for j, p4 in enumerate(p4s)], axis=1)
    ya_ref[...] = ya
    yield
    silu, _ = _silu_and_grad(ga_ref[...])
    oa_ref[...] = (ya * _rstd(ya) * gn_ref[...] * silu).astype(BF16)


def _out_proj_loss(x, oc, oa, wo, norm_final, target, pc, conv_w, norm_conv_out):
    seq = x.shape[0]
    tile = TOK_TILE
    n_tiles = seq // tile

    def body(x_ref, oc_ref, oa_ref, wo_ref, gf_ref, t_ref, pc_ref, hcc_ref, hcu_ref, cw_ref, gn_ref,
             dx2_ref, doa_ref, gwo_ref, dpc_ref, small_ref, loss_acc, zbuf, dbuf):
        step = pl.program_id(0)

        def small_add(row, value):
            small_ref[row:row + 1, :] += jnp.sum(value, axis=0, keepdims=True)

        @pl.when(step == 0)
        def _():
            gwo_ref[...] = jnp.zeros_like(gwo_ref)
            small_ref[...] = jnp.zeros_like(small_ref)
            loss_acc[...] = jnp.zeros_like(loss_acc)
            dbuf[tile:tile + 8, :] = jnp.zeros((8, D_CONV), F32)

        oc, oa = oc_ref[...], oa_ref[...]
        x2 = x_ref[...] + _nn(oc, wo_ref[0:D_CONV, :]) + _nn(oa, wo_ref[D_CONV:D_MIX, :])
        r = _rstd(x2)
        xhat = x2 * r
        err = xhat * gf_ref[...] - t_ref[...]
        loss_acc[...] += jnp.sum(err * err, axis=0, keepdims=True) * (0.5 / D_MODEL)
        dy = err * (1.0 / D_MODEL)
        small_add(SMALL_NORM_FINAL, dy * xhat)
        dx2 = _rms_bwd(dy * gf_ref[...], xhat, r)
        dx2_ref[...] = dx2
        db = dx2.astype(BF16)
        do = _nt(db, wo_ref[0:D_CONV, :])
        doa_ref[...] = _nt(db, wo_ref[D_CONV:D_MIX, :])
        gwo_ref[0:D_CONV, :] += _tn(oc, db)
        gwo_ref[D_CONV:D_MIX, :] += _tn(oa, db)

        is_first_tile = step == n_tiles - 1
        zbuf[0:8, :] = jnp.where(is_first_tile, 0.0, hcc_ref[...] * hcu_ref[...])
        cb, cc, cu, z, z1, z2, conv = _conv_core(pc_ref, zbuf, cw_ref)
        silu, dsilu = _silu_and_grad(pc_ref[:, 3 * D_CONV:4 * D_CONV])
        yc = cb * conv
        rc = _rstd(yc)
        chat = yc * rc
        dn = do * silu
        dpc_ref[:, 3 * D_CONV:4 * D_CONV] = (do * (chat * gn_ref[...]) * dsilu).astype(BF16)
        small_add(SMALL_NORM_CONV, dn * chat)
        dyc = _rms_bwd(dn * gn_ref[...], chat, rc)
        dpc_ref[:, 0:D_CONV] = (dyc * conv).astype(BF16)
        dconv = dyc * cb
        small_add(SMALL_CONV_W, dconv * z2)
        small_add(SMALL_CONV_W + 1, dconv * z1)
        small_add(SMALL_CONV_W + 2, dconv * z)
        dbuf[0:tile, :] = dconv
        dz = cw_ref[2:3, :] * dconv + cw_ref[1:2, :] * dbuf[1:tile + 1, :] + cw_ref[0:1, :] * dbuf[2:tile + 2, :]
        dpc_ref[:, D_CONV:2 * D_CONV] = (dz * cu).astype(BF16)
        dpc_ref[:, 2 * D_CONV:3 * D_CONV] = (dz * cc).astype(BF16)
        dbuf[tile:tile + 8, :] = dbuf[0:8, :]

        @pl.when(step == n_tiles - 1)
        def _():
            lane = lax.broadcasted_iota(jnp.int32, (1, D_MODEL), 1)
            small_ref[SMALL_MISC:SMALL_MISC + 1, :] = jnp.where(lane == SMALL_LOSS_LANE, jnp.sum(loss_acc[...]), 0.0)

    def rev(i):
        return n_tiles - 1 - i

    def row(width):
        return pl.BlockSpec((tile, width), lambda i: (rev(i), 0))

    def halo(col_block):
        return pl.BlockSpec((8, D_CONV), lambda i: (jnp.maximum(rev(i) * (tile // 8) - 1, 0), col_block))

    def const(shape):
        return pl.BlockSpec(shape, lambda i: (0, 0))

    return pl.pallas_call(
        body, name="out_proj_loss", grid=(n_tiles,),
        out_shape=(jax.ShapeDtypeStruct((seq, D_MODEL), F32), jax.ShapeDtypeStruct((seq, D_ATTN), F32),
                   jax.ShapeDtypeStruct((D_MIX, D_MODEL), F32), jax.ShapeDtypeStruct((seq, D_PC), BF16),
                   jax.ShapeDtypeStruct((SMALL_ROWS, D_MODEL), F32)),
        in_specs=[row(D_MODEL), row(D_CONV), row(D_ATTN), _resident(wo.shape), _resident((1, D_MODEL)), row(D_MODEL),
                  row(D_PC), halo(1), halo(2), _resident(conv_w.shape), _resident((1, D_CONV))],
        out_specs=(row(D_MODEL), row(D_ATTN), const((D_MIX, D_MODEL)), row(D_PC), const((SMALL_ROWS, D_MODEL))),
        scratch_shapes=[pltpu.VMEM((1, D_MODEL), F32), pltpu.VMEM((tile + 8, D_CONV), F32),
                        pltpu.VMEM((tile + 8, D_CONV), F32)],
        compiler_params=_params(62, ("arbitrary",)),
    )(x, oc, oa, wo, norm_final, target, pc, pc, pc, conv_w, norm_conv_out)


def _attn_bwd(q, kv, ga, ya, doa, probs, sink_probs, norm_attn_out, gwo, small):
    seq = q.shape[0]
    per_step = 2
    n_steps = seq // (per_step * BLK)
    stage_steps = (0, 1, n_steps // 4, (5 * n_steps) // 8, n_steps - 1, n_steps - 1)

    def body(q_ref, kvc_ref, kvp_ref, ga_ref, ya_ref, doa_ref, pr_ref, sp_ref, gn_ref, gwo_ref, small_ref,
             dqg_ref, small_out, gwo_sh, gna_ref, gs_ref, carry, dya_buf, *rs_scratch):
        step = pl.program_id(0)
        rs_stages = _rs_wout_stages(gwo_ref, gwo_sh, *rs_scratch)
        for at, stage in zip(stage_steps[:-1], rs_stages[:-1]):
            pl.when(step == at)(stage)

        @pl.when(step == 0)
        def _():
            gna_ref[...] = jnp.zeros_like(gna_ref)
            gs_ref[...] = jnp.zeros_like(gs_ref)
            carry[...] = jnp.zeros_like(carry)

        lane = lax.broadcasted_iota(jnp.int32, (BLK, 128), 1)

        def fold(bd):
            return (jnp.where(lane < 64, bd[0:BLK], 0.0) + jnp.where(lane >= 64, bd[BLK:2 * BLK], 0.0),
                    jnp.where(lane < 64, bd[2 * BLK:3 * BLK], 0.0) + jnp.where(lane >= 64, bd[3 * BLK:4 * BLK], 0.0))

        def one_block(b):
            rows = slice(b * BLK, (b + 1) * BLK)
            kv_prev = kvp_ref if b == 0 else kvc_ref.at[(b - 1) * BLK:b * BLK, :]
            ya = ya_ref[rows, :]
            r = _rstd(ya)
            xhat = ya * r
            silu, dsilu = _silu_and_grad(ga_ref[rows, :])
            do = doa_ref[rows, :]
            dn = do * silu
            dqg_ref[rows, D_ATTN:2 * D_ATTN] = (do * (xhat * gn_ref[...]) * dsilu).astype(BF16)
            gna_ref[...] += jnp.sum(dn * xhat, axis=0, keepdims=True)
            dya_buf[rows, :] = _rms_bwd(dn * gn_ref[...], xhat, r).astype(BF16)

            use_cur = _band_geometry(per_step * (n_steps - 1 - step) + b)[0]
            pairs = range(N_HEADS // 2)
            qps, kbd, vbd = _attn_operands(q_ref.at[rows, :], kvc_ref.at[rows, :], kv_prev)
            probs = [pr_ref[rows, h * BLK:(h + 1) * BLK].astype(F32) for h in range(N_HEADS)]
            dyps = [dya_buf[rows, j * 128:(j + 1) * 128] for j in pairs]
            dps = []
            for j in pairs:
                dps += _merge(_nt(dyps[j], vbd[j // 4]), use_cur)
            deltas = [jnp.sum(p * dp, axis=-1, keepdims=True) for p, dp in zip(probs, dps)]
            dss = [p * (dp - delta) for p, dp, delta in zip(probs, dps, deltas)]
            delta_lanes = sum(jnp.where(lane == h, deltas[h], 0.0) for h in range(N_HEADS))
            gs_ref[...] -= jnp.sum(sp_ref[rows, :] * delta_lanes, axis=0, keepdims=True)
            ds4s = [_split(dss[2 * j], dss[2 * j + 1], use_cur).astype(BF16) for j in pairs]
            p4s = [_split(probs[2 * j], probs[2 * j + 1], use_cur).astype(BF16) for j in pairs]
            dqg_ref[rows, 0:D_ATTN] = jnp.concatenate([_nn(ds4s[j], kbd[j // 4]) * SCALE for j in pairs], axis=1).astype(BF16)
            sums = []
            for group in range(N_HEADS // HEADS_PER_KV):
                acc = [jnp.zeros((BLK, 128), F32) for _ in range(4)]
                for j in range(group * 4, group * 4 + 4):
                    for slot, part in enumerate(fold(_tn(ds4s[j], qps[j])) + fold(_tn(p4s[j], dyps[j]))):
                        acc[slot] = acc[slot] + part
                sums.append([a + pltpu.roll(a, 64, 1) for a in acc])
            dk_cur, dk_prev, dv_cur, dv_prev = (jnp.where(lane < 64, lo, hi) for lo, hi in zip(sums[0], sums[1]))
            dqg_ref[rows, 2 * D_ATTN:2 * D_ATTN + 2 * D_KV] = (jnp.concatenate([dk_cur, dv_cur], axis=1) + carry[...]).astype(BF16)
            carry[...] = jnp.concatenate([dk_prev, dv_prev], axis=1)

        for b in reversed(range(per_step)):
            one_block(b)

        @pl.when(step == n_steps - 1)
        def _():
            small_out[...] = small_ref[...]
            small_out[SMALL_NORM_ATTN:SMALL_NORM_ATTN + 1, :] = gna_ref[...]
            small_out[SMALL_MISC:SMALL_MISC + 1, 0:128] = small_ref[SMALL_MISC:SMALL_MISC + 1, 0:128] + gs_ref[...]

        pl.when(step == stage_steps[-1])(rs_stages[-1])

    def rows(width):
        return pl.BlockSpec((per_step * BLK, width), lambda i: (n_steps - 1 - i, 0))

    kv_prev = pl.BlockSpec((BLK, 2 * D_KV), lambda i: (jnp.maximum(per_step * (n_steps - 1 - i) - 1, 0), 0))
    any_spec = pl.BlockSpec(memory_space=pl.ANY)
    return pl.pallas_call(
        body, name="attn_bwd", grid=(n_steps,),
        out_shape=(jax.ShapeDtypeStruct((seq, D_QG), BF16), pltpu.HBM(small.shape, F32),
                   pltpu.HBM((gwo.shape[0] // N_CHIPS, gwo.shape[1]), F32)),
        in_specs=[rows(D_ATTN), rows(2 * D_KV), kv_prev, rows(D_ATTN), rows(D_ATTN), rows(D_ATTN),
                  rows(N_HEADS * BLK), rows(128), _resident((1, D_ATTN)), any_spec, _resident(small.shape)],
        out_specs=(rows(D_QG), pl.BlockSpec(small.shape, lambda i: (0, 0)), any_spec),
        scratch_shapes=[pltpu.VMEM((1, D_ATTN), F32), pltpu.VMEM((1, 128), F32),
                        pltpu.VMEM((BLK, 2 * D_KV), F32), pltpu.VMEM((per_step * BLK, D_ATTN), BF16)] + _rs_wout_scratch(gwo.shape),
        compiler_params=_params(44, ("arbitrary",)),
    )(q, kv, kv, ga, ya, doa, probs, sink_probs, norm_attn_out, gwo, small)


GBLK = 256
GSUB = 64
PAIR_RING = 4
LAG_PAIR, LAG_HOP1, LAG_HOP2 = 1, 7, 14


def _bwd_in(dpc, dqg, h, wt, x, norm_in, dx2, small):
    seq = x.shape[0]
    n_blk = D_IN_PROJ // GBLK
    per_chip = n_blk // N_CHIPS
    n_slots = (n_blk + 1) // 2
    n_sub = GBLK // GSUB
    chip_rows = D_IN_PROJ // N_CHIPS
    tile = TOK_TILE
    n_tiles = seq // tile
    n_steps = n_blk + max(n_tiles, LAG_HOP2)
    chunk = min(seq, 512)
    blk_q, blk_kv, blk_ga = ROW_Q // GBLK, ROW_KV // GBLK, ROW_GA // GBLK

    def block_of(i):
        k = i % N_CHIPS
        robin = per_chip * ((k % 2) * 2 + k // 2) + i // N_CHIPS
        if isinstance(i, int):
            return robin if i < per_chip * N_CHIPS else i
        return jnp.where(i < per_chip * N_CHIPS, robin, i)

    def owner_of(i):
        return (i // N_CHIPS) % 2

    def slot_of(i):
        return (i // (2 * N_CHIPS)) * N_CHIPS + i % N_CHIPS

    def body(dpc_ref, dqg_ref, wt_ref, h_ref, x_ref, g_ref, dx2_ref, small_ref, gx_ref, small_sum, gwt_sh,
             dh_acc, gni, keep, pbuf, xbuf, land, land2, small_land,
             pair_send, pair_recv, h1_send, h1_recv, h2_send, h2_recv, sw_send, sw_recv, sm_send, sm_recv, out_sem):
        step = pl.program_id(0)
        x_i, y_i, c = lax.axis_index("x"), lax.axis_index("y"), lax.axis_index("c")
        me = 4 * x_i + 2 * y_i + c
        j = 2 * x_i + y_i
        pa = (_xor(x_i, 1 - c), _xor(y_i, c), c)
        pb = (_xor(x_i, c), _xor(y_i, 1 - c), c)
        sib = (x_i, y_i, 1 - c)
        ja = 2 * pa[0] + pa[1]
        jb = 2 * pb[0] + pb[1]
        jd = 3 - j

        def remote(src, dst, send, recv, to):
            return pltpu.make_async_remote_copy(src_ref=src, dst_ref=dst, send_sem=send, recv_sem=recv,
                                                device_id=to, device_id_type=MESH)

        def piece(ref, slot, u, n):
            return ref.at[slot, pl.ds(u * GSUB, n * GSUB), :]

        def chip_rows_at(ref, local, n):
            return ref.at[pl.ds(pl.multiple_of(local, GSUB), n * GSUB), :]

        def pair_copy(i):
            slot = slot_of(i)
            return remote(pbuf.at[i % PAIR_RING], land.at[slot], pair_send.at[slot], pair_recv.at[slot], sib)

        def h1_copy(slot, u, n):
            k = slot * n_sub + u
            return remote(piece(xbuf, slot, u, n), piece(xbuf, slot, u, n), h1_send.at[k], h1_recv.at[k], pa)

        def h2_copy(slot, u, n, local):
            k = slot * n_sub + u
            return remote(piece(xbuf, slot, u, n), chip_rows_at(land2, local, n), h2_send.at[k], h2_recv.at[k], pb)

        def sw_copy(slot, u, n, local):
            k = slot * n_sub + u
            return remote(piece(keep, slot, u, n), chip_rows_at(gwt_sh, local, n), sw_send.at[k], sw_recv.at[k], sib)

        def own_copy(slot, u, n, local):
            return pltpu.make_async_copy(piece(keep, slot, u, n), chip_rows_at(gwt_sh, local, n), out_sem.at[slot * n_sub + u])

        def owned(i):
            return (i >= 0) & (i < n_blk) & (owner_of(i) == c)

        def chip_of(blk, u):
            row = blk * GBLK + u * GSUB
            chip = row // chip_rows
            return chip, row - chip * chip_rows

        def pieces(blk):
            first, local = chip_of(blk, 0)
            whole = first == chip_of(blk, n_sub - 1)[0]
            if isinstance(blk, int):
                return [(True, 0, n_sub, first, local)] if whole else [(True, u, 1) + chip_of(blk, u) for u in range(n_sub)]
            return [(whole, 0, n_sub, first, local)] + [(jnp.logical_not(whole), u, 1) + chip_of(blk, u) for u in range(n_sub)]

        @pl.when(step == 0)
        def _():
            dh_acc[...] = jnp.zeros_like(dh_acc)
            gni[...] = jnp.zeros_like(gni)

        @pl.when(step < n_blk)
        def _():
            from_pc = block_of(step) < blk_q
            block = _tn(jnp.where(from_pc, dpc_ref[...], dqg_ref[...]), h_ref[...])
            for t in range(0, seq, chunk):
                d = jnp.where(from_pc, dpc_ref[t:t + chunk, :], dqg_ref[t:t + chunk, :])
                dh_acc[t:t + chunk, :] += _nn(d, wt_ref[...])

            @pl.when(owner_of(step) == c)
            def _():
                keep[slot_of(step)] = block

            @pl.when(owner_of(step) != c)
            def _():
                @pl.when(step >= 2 * PAIR_RING)
                def _():
                    pair_copy(step - 2 * PAIR_RING).wait_send()
                pbuf[step % PAIR_RING] = block.astype(BF16)
                pair_copy(step).start()

        i1 = step - LAG_PAIR

        @pl.when(owned(i1))
        def _():
            slot = slot_of(i1)
            pair_copy(i1).wait_recv()
            _accumulate(keep.at[slot], land.at[slot])
            for cond, u, n, chip, _ in pieces(block_of(i1)):
                @pl.when(cond & ((chip == ja) | (chip == jd)))
                def _(u=u, n=n):
                    _cast_rows(piece(keep, slot, u, n), piece(xbuf, slot, u, n))
                    h1_copy(slot, u, n).start()

        i2 = step - LAG_HOP1

        @pl.when(owned(i2))
        def _():
            slot = slot_of(i2)
            for cond, u, n, chip, local in pieces(block_of(i2)):
                @pl.when(cond & ((chip == j) | (chip == jb)))
                def _(u=u, n=n, chip=chip, local=local):
                    h1_copy(slot, u, n).wait_recv()
                    _accumulate(piece(keep, slot, u, n), piece(xbuf, slot, u, n))

                    @pl.when(chip == jb)
                    def _():
                        _cast_rows(piece(keep, slot, u, n), piece(xbuf, slot, u, n))
                        h2_copy(slot, u, n, local).start()

                @pl.when(cond & ((chip == ja) | (chip == jd)))
                def _(u=u, n=n):
                    h1_copy(slot, u, n).wait_send()

        i3 = step - LAG_HOP2

        @pl.when(owned(i3))
        def _():
            slot = slot_of(i3)
            for cond, u, n, chip, local in pieces(block_of(i3)):
                @pl.when(cond & (chip == j))
                def _(u=u, n=n, local=local):
                    h2_copy(slot, u, n, local).wait_recv()
                    _accumulate(piece(keep, slot, u, n), chip_rows_at(land2, local, n))
                    own_copy(slot, u, n, local).start()
                    sw_copy(slot, u, n, local).start()

                @pl.when(cond & (chip == jb))
                def _(u=u, n=n, local=local):
                    h2_copy(slot, u, n, local).wait_send()

        e = step - n_blk

        @pl.when((e >= 0) & (e < n_tiles))
        def _():
            dh = dh_acc[pl.ds(pl.multiple_of(e * tile, tile), tile), :]
            xv = x_ref[...]
            r = _rstd(xv)
            xhat = xv * r
            gni[...] += jnp.sum(dh * xhat, axis=0, keepdims=True)
            gx_ref[...] = _rms_bwd(dh * g_ref[...], xhat, r) + dx2_ref[...]

        others = [(dx, dy, dc) for dx in (0, 1) for dy in (0, 1) for dc in (0, 1)][1:]
        sends = [remote(small_land.at[me], small_land.at[me], sm_send.at[k], sm_recv.at[k],
                        (_xor(x_i, dx), _xor(y_i, dy), _xor(c, dc))) for k, (dx, dy, dc) in enumerate(others)]

        @pl.when(step == min(n_blk + n_tiles, n_steps - 1))
        def _():
            small_land[me] = small_ref[...]
            small_land[me, SMALL_NORM_IN:SMALL_NORM_IN + 1, :] = gni[...]
            for cp in sends:
                cp.start()

        @pl.when(step == n_steps - 1)
        def _():
            for i in range(n_blk):
                if i + 2 * PAIR_RING >= n_blk:
                    @pl.when(owner_of(i) != c)
                    def _(i=i):
                        pair_copy(i).wait_send()
                for _, u, n, chip, local in pieces(block_of(i)):
                    @pl.when((j == chip) & (c == owner_of(i)))
                    def _(i=i, u=u, n=n, local=local):
                        sw_copy(slot_of(i), u, n, local).wait_send()
                        own_copy(slot_of(i), u, n, local).wait()

                    @pl.when((j == chip) & (c != owner_of(i)))
                    def _(i=i, u=u, n=n, local=local):
                        sw_copy(slot_of(i), u, n, local).wait_recv()
            for cp in sends:
                cp.wait_recv()
            total = small_land[0]
            for dev in range(1, 8):
                total = total + small_land[dev]
            small_sum[...] = total
            for cp in sends:
                cp.wait_send()

    def blk_at(i):
        return block_of(jnp.clip(i, 0, n_blk - 1))

    last_pc_step = max(i for i in range(n_blk) if block_of(i) < blk_q)

    def next_block(i, in_pc):
        i = jnp.clip(i, 0, n_blk - 1)
        step = jnp.full_like(i, last_pc_step if in_pc else n_blk - 1)
        for ahead in reversed(range(N_CHIPS)):
            cand = jnp.minimum(i + ahead, n_blk - 1)
            step = jnp.where((block_of(cand) < blk_q) == in_pc, cand, step)
        return block_of(step)

    def dqg_block(i):
        b = next_block(i, False)
        q_blk = jnp.clip(b - blk_q, 0, blk_kv - blk_q - 1)
        ga_blk = (D_ATTN // GBLK) + jnp.clip(b - blk_ga, 0, n_blk - blk_ga - 1)
        return jnp.where(b < blk_kv, q_blk, jnp.where(b == blk_kv, 2 * D_ATTN // GBLK, ga_blk))

    def tok(i):
        return (jnp.clip(i - n_blk, 0, n_tiles - 1), 0)

    n_piece = n_slots * n_sub
    dma = pltpu.SemaphoreType.DMA
    return pl.pallas_call(
        body, name="bwd_in", grid=(n_steps,),
        out_shape=(jax.ShapeDtypeStruct((seq, D_MODEL), F32), jax.ShapeDtypeStruct(small.shape, F32),
                   jax.ShapeDtypeStruct((chip_rows, D_MODEL), F32)),
        in_specs=[pl.BlockSpec((seq, GBLK), lambda i: (0, next_block(i, True))),
                  pl.BlockSpec((seq, GBLK), lambda i: (0, dqg_block(i))),
                  pl.BlockSpec((GBLK, D_MODEL), lambda i: (blk_at(i), 0)),
                  _resident(h.shape),
                  pl.BlockSpec((tile, D_MODEL), tok), _resident((1, D_MODEL)), pl.BlockSpec((tile, D_MODEL), tok),
                  _resident(small.shape)],
        out_specs=(pl.BlockSpec((tile, D_MODEL), tok), pl.BlockSpec(small.shape, lambda i: (0, 0)),
                   pl.BlockSpec(memory_space=pl.ANY)),
        scratch_shapes=[pltpu.VMEM((seq, D_MODEL), F32), pltpu.VMEM((1, D_MODEL), F32),
                        pltpu.VMEM((n_slots, GBLK, D_MODEL), F32), pltpu.VMEM((PAIR_RING, GBLK, D_MODEL), BF16),
                        pltpu.VMEM((n_slots, GBLK, D_MODEL), BF16), pltpu.VMEM((n_slots, GBLK, D_MODEL), BF16),
                        pltpu.VMEM((chip_rows, D_MODEL), BF16), pltpu.VMEM((8,) + small.shape, F32),
                        dma((n_slots,)), dma((n_slots,)), dma((n_piece,)), dma((n_piece,)), dma((n_piece,)),
                        dma((n_piece,)), dma((n_piece,)), dma((n_piece,)), dma((7,)), dma((7,)), dma((n_piece,))],
        compiler_params=_params(62, ("arbitrary",)),
    )(dpc, dqg, wt, h, x, norm_in, dx2, small)


def _accumulate(dst_ref, src_ref, rows=16):
    def step(i, carry):
        sl = pl.ds(pl.multiple_of(i * rows, rows), rows)
        dst_ref[sl, :] = dst_ref[sl, :] + src_ref[sl, :].astype(F32)
        return carry
    lax.fori_loop(0, dst_ref.shape[0] // rows, step, 0)


def _rs_wout_scratch(gwo_shape):
    o_half, width = gwo_shape[0] // N_CHIPS // 2, gwo_shape[1]
    return [pltpu.VMEM((4, o_half, width), F32), pltpu.VMEM((4, o_half, width), BF16),
            pltpu.VMEM((4, o_half, width), BF16), pltpu.VMEM((2, o_half, width), BF16),
            pltpu.VMEM((o_half, width), BF16),
            pltpu.SemaphoreType.DMA((8,)), pltpu.SemaphoreType.DMA((8,)), pltpu.SemaphoreType.DMA((4,))]


def _rs_wout_stages(gwo_ref, gwo_sh, acc_o, sb_o, r1o, r2o, r3o, send_sems, recv_sems, local_sems):
    o_rows = gwo_ref.shape[0] // N_CHIPS
    o_half = o_rows // 2
    x, y, c = lax.axis_index("x"), lax.axis_index("y"), lax.axis_index("c")
    j = 2 * x + y
    pa = (_xor(x, 1 - c), _xor(y, c), c)
    pb = (_xor(x, c), _xor(y, 1 - c), c)
    sib = (x, y, 1 - c)
    ja = 2 * pa[0] + pa[1]
    jb = 2 * pb[0] + pb[1]
    jd = 3 - j
    order = (ja, jd, jb, j)
    sib_order = (jb, jd, ja, j)

    def rcopy(k, src, dst, to):
        return pltpu.make_async_remote_copy(src_ref=src, dst_ref=dst, send_sem=send_sems.at[k],
                                            recv_sem=recv_sems.at[k], device_id=to, device_id_type=MESH)

    def o_rows_of(chip, half):
        return gwo_ref.at[pl.ds(pl.multiple_of(chip * o_rows + half * o_half, 8), o_half), :]

    def loads(chips, half):
        return [pltpu.make_async_copy(o_rows_of(chip, half), acc_o.at[s], local_sems.at[s]) for s, chip in enumerate(chips)]

    def pair_send(s):
        return rcopy(s, sb_o.at[s], r1o.at[s], sib)

    def out_half(half):
        return gwo_sh.at[pl.ds(pl.multiple_of(half * o_half, 8), o_half), :]

    hop1 = [rcopy(4 + s, sb_o.at[s], r2o.at[s], pa) for s in range(2)]
    hop2 = rcopy(6, sb_o.at[2], r3o, pb)
    swap = rcopy(7, acc_o.at[3], out_half(c), sib)
    mine = pltpu.make_async_copy(acc_o.at[3], out_half(c), local_sems.at[0])

    def resend(s, copy):
        pair_send(s).wait_send()
        _cast_rows(acc_o.at[s], sb_o.at[s])
        copy.start()

    def stage_load():
        for cp in loads(sib_order, 1 - c):
            cp.start()

    def stage_pair():
        for s, (cp, mine_in) in enumerate(zip(loads(sib_order, 1 - c), loads(order, c))):
            cp.wait()
            _cast_rows(acc_o.at[s], sb_o.at[s])
            pair_send(s).start()
            mine_in.start()

    def stage_hop1():
        for s, cp in enumerate(loads(order, c)):
            cp.wait()
            pair_send(s).wait_recv()
            _accumulate(acc_o.at[s], r1o.at[s])
            if s < 2:
                resend(s, hop1[s])

    def stage_hop2():
        hop1[1].wait_recv()
        _accumulate(acc_o.at[2], r2o.at[1])
        resend(2, hop2)
        hop1[0].wait_recv()
        _accumulate(acc_o.at[3], r2o.at[0])

    def stage_final():
        hop2.wait_recv()
        _accumulate(acc_o.at[3], r3o)
        mine.start()
        swap.start()

    def stage_drain():
        rcopy(7, acc_o.at[3], out_half(1 - c), sib).wait_recv()
        for cp in [pair_send(3)] + hop1 + [hop2, swap]:
            cp.wait_send()
        mine.wait()

    return stage_load, stage_pair, stage_hop1, stage_hop2, stage_final, stage_drain


def _adamw_big(groups, passed):
    arrays = [a for group in groups for a in group[:4]]
    steps = [group[0].shape[0] // group[4] for group in groups]
    first = [sum(steps[:k]) for k in range(len(steps) + 1)]
    n = len(arrays)

    def body(*refs):
        ins, passed_in, outs, passed_out = refs[:n], refs[n], refs[n + 1:2 * n + 1], refs[2 * n + 1]
        i = pl.program_id(0)
        for k in range(len(groups)):
            @pl.when((i >= first[k]) & (i < first[k + 1]))
            def _(k=k):
                w_ref, g_ref, m_ref, v_ref = ins[4 * k:4 * k + 4]
                _adamw_update(w_ref, g_ref[...], m_ref, v_ref, *outs[4 * k:4 * k + 4])

        @pl.when(i < steps[0])
        def _():
            passed_out[...] = passed_in[...]

    def spec(k, rows, width):
        return pl.BlockSpec((rows, width), lambda i: (jnp.clip(i - first[k], 0, steps[k] - 1), 0))

    specs = [spec(k, group[4], group[0].shape[1]) for k, group in enumerate(groups) for _ in range(4)]
    specs.append(spec(0, passed.shape[0] // steps[0], passed.shape[1]))
    out = pl.pallas_call(
        body, name="adamw_big", grid=(first[-1],),
        out_shape=tuple(jax.ShapeDtypeStruct(a.shape, a.dtype) for a in arrays + [passed]),
        in_specs=specs, out_specs=tuple(specs),
        compiler_params=_params(40, ("arbitrary",)),
    )(*arrays, passed)
    return [tuple(out[4 * k:4 * k + 4]) for k in range(len(groups))], out[-1]


def _adamw_update(w_ref, gv, m_ref, v_ref, go_ref, d_ref, nm_ref, nv_ref, at=...):
    go_ref[at] = gv
    nm = ADAM_B1 * m_ref[at] + (1.0 - ADAM_B1) * gv
    nv = ADAM_B2 * v_ref[at] + (1.0 - ADAM_B2) * (gv * gv)
    m_hat = nm / (1.0 - ADAM_B1 ** ADAM_STEP)
    v_hat = nv / (1.0 - ADAM_B2 ** ADAM_STEP)
    d_ref[at] = -ADAM_LR * (m_hat / (jnp.sqrt(v_hat) + ADAM_EPS) + ADAM_WD * w_ref[at])
    nm_ref[at] = nm
    nv_ref[at] = nv


def _adamw_small(chip, small_sum, weights, grads_of, ms, vs):
    n = len(weights)

    def body(chip_ref, small_ref, *refs):
        ins, outs, loss_ref = refs[:3 * n], refs[3 * n:-1], refs[-1]
        for k in range(n):
            w_ref, m_ref, v_ref = ins[3 * k:3 * k + 3]
            for at, gv in grads_of[k](small_ref, chip_ref):
                _adamw_update(w_ref, gv, m_ref, v_ref, *outs[4 * k:4 * k + 4], at=at)
        loss_ref[...] = small_ref[SMALL_MISC:SMALL_MISC + 1, SMALL_LOSS_LANE:SMALL_LOSS_LANE + 1]

    flat = [a for group in zip(weights, ms, vs) for a in group]
    vmem = pl.BlockSpec(memory_space=pltpu.VMEM)
    out = pl.pallas_call(
        body, name="adamw_small",
        out_shape=tuple(jax.ShapeDtypeStruct(w.shape, F32) for w in weights for _ in range(4))
        + (jax.ShapeDtypeStruct((1, 1), F32),),
        in_specs=[pl.BlockSpec(memory_space=pltpu.SMEM)] + [vmem] * (1 + 3 * n), out_specs=(vmem,) * (4 * n + 1),
    )(chip, small_sum, *flat)
    return [tuple(out[4 * k:4 * k + 4]) for k in range(n)], out[-1][0, 0]


def kernel(x, norm_in, w_in, conv_w, attn_sinks, norm_conv_out, norm_attn_out, w_out, norm_final, loss_target, m_norm_in, m_w_in, m_conv_w, m_attn_sinks, m_norm_conv_out, m_norm_attn_out, m_w_out, m_norm_final, v_norm_in, v_w_in, v_conv_w, v_attn_sinks, v_norm_conv_out, v_norm_attn_out, v_w_out, v_norm_final):
    chip = 2 * lax.axis_index("x") + lax.axis_index("y")
    xs, target = x[0], loss_target[0]
    norm_final2 = norm_final.reshape(1, D_MODEL)
    w_in_t, m_w_in_t, v_w_in_t = w_in[0].T, m_w_in[0].T, v_w_in[0].T

    def from_hbm(*arrays):
        return tuple(pltpu.with_memory_space_constraint(a, pltpu.HBM) for a in arrays)

    h, pc, q, kv, ga, oc, ya, oa, probs, sink_probs, wt, cw, wo = _fwd_in(
        xs, norm_in, w_in_t, w_out[0], conv_w.transpose(1, 0, 2), norm_conv_out, attn_sinks, norm_attn_out)
    dx2, doa, gwo, dpc, small = _out_proj_loss(xs, oc, oa, wo, norm_final2, target, pc, cw, norm_conv_out)
    q, kv, ga, ya, doa, probs, gwo, small = from_hbm(q, kv, ga, ya, doa, probs, gwo, small)
    dqg, small, gwo_sh = _attn_bwd(q, kv, ga, ya, doa, probs, sink_probs, norm_attn_out, gwo, small)
    grad_x, small_sum, gwt_sh = _bwd_in(dpc, dqg, h, wt, xs, norm_in, dx2, small)

    (up_w_in, up_w_out), grad_x = _adamw_big(
        [from_hbm(w_in_t, gwt_sh, m_w_in_t, v_w_in_t) + (200,), from_hbm(w_out[0], gwo_sh, m_w_out[0], v_w_out[0]) + (128,)],
        *from_hbm(grad_x))
    up_w_in = tuple(o.T[None] for o in up_w_in)
    up_w_out = tuple(o[None] for o in up_w_out)

    def row_of(r):
        return lambda small_ref, chip_ref: [(..., small_ref[r:r + 1, :])]

    def sink_lanes(small_ref, chip_ref):
        return [(..., small_ref[SMALL_MISC:SMALL_MISC + 1, 0:N_HEADS])]

    def conv_taps(small_ref, chip_ref):
        width = D_CONV // N_CHIPS
        cols = pl.ds(pl.multiple_of(chip_ref[0] * width, width), width)
        return [(k, small_ref[pl.ds(SMALL_CONV_W + k, 1), cols]) for k in range(conv_w.shape[1])]

    def small_view(a):
        return a.transpose(1, 0, 2) if a.ndim == 3 else a.reshape(-1, a.shape[-1])

    small_w = (norm_in, conv_w, attn_sinks, norm_conv_out, norm_attn_out, norm_final)
    small_m = (m_norm_in, m_conv_w, m_attn_sinks, m_norm_conv_out, m_norm_attn_out, m_norm_final)
    small_v = (v_norm_in, v_conv_w, v_attn_sinks, v_norm_conv_out, v_norm_attn_out, v_norm_final)
    small_g = (row_of(SMALL_NORM_IN), conv_taps, sink_lanes, row_of(SMALL_NORM_CONV), row_of(SMALL_NORM_ATTN),
               row_of(SMALL_NORM_FINAL))
    w_views, m_views, v_views = (tuple(small_view(a) for a in group) for group in (small_w, small_m, small_v))
    up_small, loss = _adamw_small(chip.astype(jnp.int32).reshape(1), small_sum, w_views, small_g, m_views, v_views)
    up_small = [tuple(o.transpose(1, 0, 2) if w.ndim == 3 else o.reshape(w.shape) for o in up)
                for up, w in zip(up_small, small_w)]
    up_norm_in, up_conv_w, up_sinks, up_norm_conv, up_norm_attn, up_norm_final = up_small
    updates = (up_norm_in, up_w_in, up_conv_w, up_sinks, up_norm_conv, up_norm_attn, up_w_out, up_norm_final)
    grads_out, deltas, new_m, new_v = zip(*updates)
    return (loss, grad_x[None], *grads_out, *deltas, *new_m, *new_v)
```

```python
import jax
import jax.numpy as jnp
from jax import lax
from jax.experimental import pallas as pl
from jax.experimental.pallas import tpu as pltpu

F32 = jnp.float32
BF16 = jnp.bfloat16
MESH = pl.DeviceIdType.MESH

D_MODEL = 1024
D_CONV = 1024
D_ATTN = 1024
D_KV = 128
D_QG = 2 * D_ATTN + 2 * D_KV
D_MIX = D_CONV + D_ATTN
D_PC = 4 * D_CONV
D_IN_PROJ = D_PC + 2 * D_ATTN + 2 * D_KV
ROW_Q = D_PC
ROW_KV = ROW_Q + D_ATTN
ROW_GA = ROW_KV + 2 * D_KV
N_HEADS = 16
HEAD_DIM = 64
HEADS_PER_KV = 8
BLK = 128
N_CHIPS = 4
RMS_EPS = 1e-5
SCALE = HEAD_DIM ** -0.5
SLOPES = tuple(2.0 ** (-8.0 * (h + 1) / N_HEADS) for h in range(N_HEADS))

ADAM_LR, ADAM_B1, ADAM_B2, ADAM_EPS, ADAM_WD, ADAM_STEP = 0.001, 0.9, 0.999, 1e-08, 0.01, 10

SMALL_ROWS = 8
SMALL_NORM_IN, SMALL_NORM_CONV, SMALL_NORM_ATTN, SMALL_NORM_FINAL, SMALL_CONV_W, SMALL_MISC = 0, 1, 2, 3, 4, 7
SMALL_LOSS_LANE = N_HEADS

TOK_TILE = 256
PC_PIECE = 512
MIB = 1 << 20


def _params(vmem_mib, semantics=None):
    return pltpu.CompilerParams(dimension_semantics=semantics, vmem_limit_bytes=vmem_mib * MIB)


def _nn(a, b):
    return jnp.dot(a, b, preferred_element_type=F32)


def _nt(a, b):
    return lax.dot_general(a, b, (((1,), (1,)), ((), ())), preferred_element_type=F32)


def _tn(a, b):
    return lax.dot_general(a, b, (((0,), (0,)), ((), ())), preferred_element_type=F32)


def _rstd(v):
    return lax.rsqrt(jnp.mean(v * v, axis=-1, keepdims=True) + RMS_EPS)


def _rms_bwd(g, xhat, rstd):
    return rstd * (g - xhat * jnp.mean(g * xhat, axis=-1, keepdims=True))


def _silu_and_grad(g):
    s = jax.nn.sigmoid(g)
    return g * s, s * (1.0 + g * (1.0 - s))


def _resident(shape):
    return pl.BlockSpec(shape, lambda *_: (0,) * len(shape), pipeline_mode=pl.Buffered(1))


def _xor(a, b):
    return a + b - 2 * a * b


def _cast_rows(src_ref, dst_ref, rows=32):
    def step(i, carry):
        sl = pl.ds(pl.multiple_of(i * rows, rows), rows)
        dst_ref[sl, :] = src_ref[sl, :].astype(dst_ref.dtype)
        return carry
    lax.fori_loop(0, src_ref.shape[0] // rows, step, 0)


def _ag_scratch(shard_shape):
    rows, width = shard_shape
    return [pltpu.VMEM((rows, width), F32), pltpu.VMEM((rows, width), BF16), pltpu.VMEM((3, rows // 2, width), BF16),
            pltpu.SemaphoreType.DMA((6,)), pltpu.SemaphoreType.DMA((6,)), pltpu.SemaphoreType.DMA((4,))]


def _ag_stages(sh_ref, out, f32_buf, own, land, send_sems, recv_sems, local_sems):
    rows = sh_ref.shape[0]
    half = rows // 2
    x, y, c = lax.axis_index("x"), lax.axis_index("y"), lax.axis_index("c")
    j = 2 * x + y
    p1 = (_xor(x, c), _xor(y, 1 - c), c)
    p2 = (_xor(x, 1 - c), _xor(y, c), c)
    sib = (x, y, 1 - c)
    j1 = 2 * p1[0] + p1[1]
    j2 = 2 * p2[0] + p2[1]
    j3 = 3 - j

    def rows_of(chip, hf):
        return out.at[pl.ds(pl.multiple_of(chip * rows + hf * half, 16), half), :]

    def rcopy(k, src, dst, to):
        return pltpu.make_async_remote_copy(src_ref=src, dst_ref=dst, send_sem=send_sems.at[k],
                                            recv_sem=recv_sems.at[k], device_id=to, device_id_type=MESH)

    my_half = own.at[pl.ds(pl.multiple_of(c * half, 16), half), :]
    hop1 = rcopy(0, my_half, land.at[0], p1)
    hop2_own = rcopy(1, my_half, land.at[1], p2)
    hop2_fwd = rcopy(2, land.at[0], land.at[2], p2)
    swaps = [rcopy(3 + s, land.at[s], rows_of(chip, c), sib) for s, chip in enumerate((j1, j2, j3))]
    keeps = [pltpu.make_async_copy(land.at[s], rows_of(chip, c), local_sems.at[1 + s]) for s, chip in enumerate((j1, j2, j3))]
    load = pltpu.make_async_copy(sh_ref, f32_buf, local_sems.at[0])
    own_out = pltpu.make_async_copy(own, out.at[pl.ds(pl.multiple_of(j * rows, 16), rows), :], local_sems.at[0])

    def stage_load():
        load.start()

    def stage_send():
        load.wait()
        _cast_rows(f32_buf, own)
        own_out.start()
        hop1.start()

    def stage_forward():
        hop1.wait_recv()
        hop2_own.start()
        hop2_fwd.start()
        swaps[0].start()
        keeps[0].start()

    def stage_publish():
        hop2_own.wait_recv()
        swaps[1].start()
        keeps[1].start()
        hop2_fwd.wait_recv()
        swaps[2].start()
        keeps[2].start()

    def stage_drain():
        for s, chip in enumerate((j2, j1, j3)):
            rcopy(3 + s, my_half, rows_of(chip, 1 - c), sib).wait_recv()
        for cp in [hop1, hop2_own, hop2_fwd] + swaps:
            cp.wait_send()
        for cp in [own_out] + keeps:
            cp.wait()

    return stage_load, stage_send, stage_forward, stage_publish, stage_drain


AG_CAST_ROWS = 400


def _gather_resident(sh_ref, out, f32_buf, send_sems, recv_sems, local_sems, after_first_hop):
    rows = sh_ref.shape[0]
    half = rows // 2
    x, y, c = lax.axis_index("x"), lax.axis_index("y"), lax.axis_index("c")
    j = 2 * x + y
    p1 = (_xor(x, c), _xor(y, 1 - c), c)
    p2 = (_xor(x, 1 - c), _xor(y, c), c)
    sib = (x, y, 1 - c)
    j1 = 2 * p1[0] + p1[1]
    j2 = 2 * p2[0] + p2[1]
    j3 = 3 - j

    def rows_of(chip, hf):
        return out.at[pl.ds(pl.multiple_of(chip * rows + hf * half, 16), half), :]

    def send(k, chip, to):
        return pltpu.make_async_remote_copy(src_ref=rows_of(chip, c), dst_ref=rows_of(chip, c), send_sem=send_sems.at[k],
                                            recv_sem=recv_sems.at[k], device_id=to, device_id_type=MESH)

    per_half = half // AG_CAST_ROWS

    chunks = [c * per_half + k for k in range(per_half)] + [(1 - c) * per_half + k for k in range(per_half)]

    def load(n):
        lo = pl.multiple_of(chunks[n] * AG_CAST_ROWS, 16)
        return pltpu.make_async_copy(sh_ref.at[pl.ds(lo, AG_CAST_ROWS), :], f32_buf.at[n % 2], local_sems.at[n % 2])

    hop1 = send(0, j, p1)
    load(0).start()
    load(1).start()
    for n in range(len(chunks)):
        load(n).wait()
        lo = pl.multiple_of(chunks[n] * AG_CAST_ROWS, 16)
        _cast_rows(f32_buf.at[n % 2], out.at[pl.ds(pl.multiple_of(j * rows + lo, 16), AG_CAST_ROWS), :], rows=16)
        if n + 2 < len(chunks):
            load(n + 2).start()
        if n == per_half - 1:
            hop1.start()
    hop1.wait_recv()
    sends = [hop1, send(1, j, p2), send(2, j1, p2), send(3, j1, sib)]
    for cp in sends[1:]:
        cp.start()
    after_first_hop()
    sends[1].wait_recv()
    sends.append(send(4, j2, sib))
    sends[-1].start()
    sends[2].wait_recv()
    sends.append(send(5, j3, sib))
    sends[-1].start()
    for k in (3, 4, 5):
        send(k, j, sib).wait_recv()
    for cp in sends:
        cp.wait_send()


def _fwd_in(x, norm_in, wt_sh, wo_sh, cw_sh, norm_conv_out, sinks, norm_attn_out):
    seq = x.shape[0]
    tile = TOK_TILE
    n_tiles = seq // tile
    stage_steps = (0, n_tiles // 4, (5 * n_tiles) // 8, n_tiles - 1)
    blocks = tile // BLK
    cw_cols = cw_sh.shape[-1]

    def body(x_ref, g_ref, wtsh_ref, wo_ref, cwsh_ref, gn_ref, sink_ref, gna_ref,
             h_ref, pc_ref, q_ref, kv_ref, ga_ref, oc_ref, ya_ref, oa_ref, pr_ref, sp_ref, wt_out, cw_ref, wo_out,
             zbuf, kv_last, wt_ref, f32_buf, cw_land, wt_send, wt_recv, wt_local, cw_send, cw_recv, cw_local, *ag_scratch):
        step = pl.program_id(0)
        to_hbm = pltpu.make_async_copy(wt_ref, wt_out, wt_local.at[0])
        load_wo, *stages = _ag_stages(wo_ref, wo_out, *ag_scratch)

        @pl.when(step == 0)
        def _():
            load_wo()
            zbuf[0:8, :] = jnp.zeros((8, D_CONV), F32)
            kv_last[...] = jnp.zeros_like(kv_last)
            x_i, y_i, c = lax.axis_index("x"), lax.axis_index("y"), lax.axis_index("c")
            j = 2 * x_i + y_i
            p1 = (_xor(x_i, c), _xor(y_i, 1 - c), c)
            p2 = (_xor(x_i, 1 - c), _xor(y_i, c), c)
            j1 = 2 * p1[0] + p1[1]

            def cw_copy(k, src, chip, to):
                return pltpu.make_async_remote_copy(src_ref=src, dst_ref=cw_land.at[chip], send_sem=cw_send.at[k],
                                                    recv_sem=cw_recv.at[k], device_id=to, device_id_type=MESH)

            mine = pltpu.make_async_copy(cwsh_ref, cw_land.at[j], cw_local.at[0])
            mine.start()
            first = cw_copy(0, cwsh_ref, j, p1)
            first.start()
            second = [cw_copy(1, cwsh_ref, j, p2), cw_copy(2, cw_land.at[j1], j1, p2)]

            def cw_second_hop():
                first.wait_recv()
                for cp in second:
                    cp.start()

            _gather_resident(wtsh_ref, wt_ref, f32_buf, wt_send, wt_recv, wt_local, cw_second_hop)
            to_hbm.start()
            for cp in second:
                cp.wait_recv()
            for cp in [first] + second:
                cp.wait_send()
            mine.wait()
            for chip in range(N_CHIPS):
                for tap in range(cw_sh.shape[0]):
                    cw_ref[tap:tap + 1, chip * cw_cols:(chip + 1) * cw_cols] = cw_land[chip, tap]

        for at, stage in zip(stage_steps[:-1], stages[:-1]):
            pl.when(step == at)(stage)

        def attention(b):
            rows = pl.ds(b * BLK, BLK)
            kv_prev = kv_last if b == 0 else kv_ref.at[pl.ds((b - 1) * BLK, BLK), :]
            return _attn_forward(q_ref.at[rows, :], kv_ref.at[rows, :], kv_prev, ga_ref.at[rows, :], sink_ref, gna_ref,
                                 _band_geometry(step * blocks + b), ya_ref.at[rows, :], oa_ref.at[rows, :],
                                 pr_ref.at[rows, :], sp_ref.at[rows, :])

        xv = x_ref[...]
        h = (xv * _rstd(xv) * g_ref[...]).astype(BF16)
        h_ref[...] = h
        q_ref[...] = _nt(h, wt_ref[ROW_Q:ROW_KV, :])
        kv_ref[...] = _nt(h, wt_ref[ROW_KV:ROW_GA, :])
        ga_ref[...] = _nt(h, wt_ref[ROW_GA:D_IN_PROJ, :])
        attention_blocks = [attention(b) for b in range(blocks)]
        for lo in range(0, D_PC, PC_PIECE):
            pc_ref[:, lo:lo + PC_PIECE] = _nt(h, wt_ref[lo:lo + PC_PIECE, :])
            for stages_of_block in attention_blocks:
                next(stages_of_block, None)
        for stages_of_block in attention_blocks:
            for _ in stages_of_block:
                pass
        kv_last[...] = kv_ref[tile - BLK:tile, :]

        cb, _, _, _, _, _, conv = _conv_core(pc_ref, zbuf, cw_ref)
        yc = cb * conv
        silu, _ = _silu_and_grad(pc_ref[:, 3 * D_CONV:4 * D_CONV])
        oc_ref[...] = (yc * _rstd(yc) * gn_ref[...] * silu).astype(BF16)
        zbuf[0:8, :] = zbuf[tile:tile + 8, :]

        @pl.when(step == stage_steps[-1])
        def _():
            stages[-1]()
            to_hbm.wait()

    def row(width):
        return pl.BlockSpec((tile, width), lambda i: (i, 0))

    any_spec = pl.BlockSpec(memory_space=pl.ANY)
    dma = pltpu.SemaphoreType.DMA
    wt_shape = (N_CHIPS * wt_sh.shape[0], wt_sh.shape[1])
    cw_shape = (cw_sh.shape[0], N_CHIPS * cw_cols)
    return pl.pallas_call(
        body, name="fwd_in", grid=(n_tiles,),
        out_shape=(jax.ShapeDtypeStruct((seq, D_MODEL), BF16), jax.ShapeDtypeStruct((seq, D_PC), F32),
                   jax.ShapeDtypeStruct((seq, D_ATTN), F32), jax.ShapeDtypeStruct((seq, 2 * D_KV), F32),
                   jax.ShapeDtypeStruct((seq, D_ATTN), F32), jax.ShapeDtypeStruct((seq, D_CONV), BF16),
                   pltpu.HBM((seq, D_ATTN), F32), pltpu.HBM((seq, D_ATTN), BF16),
                   pltpu.HBM((seq, N_HEADS * BLK), BF16), pltpu.HBM((seq, 128), F32),
                   pltpu.HBM(wt_shape, BF16), jax.ShapeDtypeStruct(cw_shape, F32),
                   jax.ShapeDtypeStruct((N_CHIPS * wo_sh.shape[0], wo_sh.shape[1]), BF16)),
        in_specs=[row(D_MODEL), _resident((1, D_MODEL)), any_spec, any_spec, any_spec, _resident((1, D_CONV)),
                  pl.BlockSpec(memory_space=pltpu.SMEM), _resident((1, D_ATTN))],
        out_specs=(row(D_MODEL), row(D_PC), row(D_ATTN), row(2 * D_KV), row(D_ATTN), row(D_CONV), row(D_ATTN),
                   row(D_ATTN), row(N_HEADS * BLK), row(128), any_spec, pl.BlockSpec(cw_shape, lambda i: (0, 0)),
                   any_spec),
        scratch_shapes=[pltpu.VMEM((tile + 8, D_CONV), F32), pltpu.VMEM((BLK, 2 * D_KV), F32),
                        pltpu.VMEM(wt_shape, BF16), pltpu.VMEM((2, AG_CAST_ROWS, wt_sh.shape[1]), F32),
                        pltpu.VMEM((N_CHIPS,) + cw_sh.shape, F32),
                        dma((6,)), dma((6,)), dma((2,)), dma((3,)), dma((3,)), dma((1,))] + _ag_scratch(wo_sh.shape),
        compiler_params=_params(62, ("arbitrary",)),
    )(x, norm_in, wt_sh, wo_sh, cw_sh, norm_conv_out, sinks, norm_attn_out)


def _conv_core(pc_ref, zbuf, cw_ref):
    tile = pc_ref.shape[0]
    cb = pc_ref[:, 0:D_CONV]
    cc = pc_ref[:, D_CONV:2 * D_CONV]
    cu = pc_ref[:, 2 * D_CONV:3 * D_CONV]
    z = cc * cu
    zbuf[8:tile + 8, :] = z
    z1 = zbuf[7:tile + 7, :]
    z2 = zbuf[6:tile + 6, :]
    conv = cw_ref[0:1, :] * z2 + cw_ref[1:2, :] * z1 + cw_ref[2:3, :] * z
    return cb, cc, cu, z, z1, z2, conv


def _band_geometry(block_index):
    qi = lax.broadcasted_iota(jnp.int32, (BLK, BLK), 0)
    kp = lax.broadcasted_iota(jnp.int32, (BLK, BLK), 1)
    use_cur = kp <= qi
    dist = jnp.where(use_cur, qi - kp, qi - kp + BLK).astype(F32)
    valid = use_cur | (block_index > 0)
    return use_cur, dist, valid


def _block_diag(cur, prev, group):
    lane = lax.broadcasted_iota(jnp.int32, cur.shape, 1)

    def halves(t):
        other = pltpu.roll(t, 64, 1)
        lo, hi = (t, other) if group == 0 else (other, t)
        return jnp.where(lane < 64, lo, 0.0), jnp.where(lane >= 64, hi, 0.0)

    return jnp.concatenate(halves(cur) + halves(prev), axis=0).astype(BF16)


def _merge(s4, use_cur):
    return (jnp.where(use_cur, s4[:, 0:BLK], s4[:, 2 * BLK:3 * BLK]),
            jnp.where(use_cur, s4[:, BLK:2 * BLK], s4[:, 3 * BLK:4 * BLK]))


def _split(a, b, use_cur):
    return jnp.concatenate([jnp.where(use_cur, a, 0.0), jnp.where(use_cur, b, 0.0),
                            jnp.where(use_cur, 0.0, a), jnp.where(use_cur, 0.0, b)], axis=1)


def _softmax_head(s, head, sink, dist, valid):
    sc = jnp.where(valid, s - SLOPES[head] * dist, -jnp.inf)
    m = jnp.maximum(jnp.max(sc, axis=-1, keepdims=True), sink)
    p = jnp.exp(sc - m)
    es = jnp.exp(sink - m)
    inv = 1.0 / (jnp.sum(p, axis=-1, keepdims=True) + es)
    return p * inv, es * inv


def _attn_operands(q_ref, kvc_ref, kvp_ref):
    groups = range(N_HEADS // HEADS_PER_KV)
    kbd = [_block_diag(kvc_ref[:, 0:D_KV], kvp_ref[:, 0:D_KV], g) for g in groups]
    vbd = [_block_diag(kvc_ref[:, D_KV:2 * D_KV], kvp_ref[:, D_KV:2 * D_KV], g) for g in groups]
    qps = [(q_ref[:, j * 128:(j + 1) * 128] * SCALE).astype(BF16) for j in range(N_HEADS // 2)]
    return qps, kbd, vbd


def _attn_forward(q_ref, kvc_ref, kvp_ref, ga_ref, sink_ref, gn_ref, geometry, ya_ref, oa_ref, pr_ref, sp_ref):
    use_cur, dist, valid = geometry
    _, kbd, vbd = operands = _attn_operands(q_ref, kvc_ref, kvp_ref)
    yield
    scores = []
    for j, qp in enumerate(operands[0]):
        scores += _merge(_nt(qp, kbd[j // 4]), use_cur)
    yield
    sinks = [sink_ref[0, h] for h in range(N_HEADS)]
    scores = [jnp.where(valid, s - SLOPES[h] * dist, -jnp.inf) for h, s in enumerate(scores)]
    maxes = [jnp.maximum(jnp.max(s, axis=-1, keepdims=True), sinks[h]) for h, s in enumerate(scores)]
    yield
    exps = [jnp.exp(s - m) for s, m in zip(scores, maxes)]
    sink_exps = [jnp.exp(sinks[h] - m) for h, m in enumerate(maxes)]
    yield
    invs = [1.0 / (jnp.sum(e, axis=-1, keepdims=True) + se) for e, se in zip(exps, sink_exps)]
    probs = [e * inv for e, inv in zip(exps, invs)]
    pr_ref[...] = jnp.concatenate(probs, axis=1).astype(BF16)
    lane = lax.broadcasted_iota(jnp.int32, (BLK, 128), 1)
    sp_ref[...] = sum(jnp.where(lane == h, se * inv, 0.0) for h, (se, inv) in enumerate(zip(sink_exps, invs)))
    yield
    p4s = [_split(probs[2 * j], probs[2 * j + 1], use_cur).astype(BF16) for j in range(N_HEADS // 2)]
    ya = jnp.concatenate([_nn(p4, vbd[j // 4]) for j, p4 in enumerate(p4s)], axis=1)
    ya_ref[...] = ya
    yield
    silu, _ = _silu_and_grad(ga_ref[...])
    oa_ref[...] = (ya * _rstd(ya) * gn_ref[...] * silu).astype(BF16)


def _out_proj_loss(x, oc, oa, wo, norm_final, target, pc, conv_w, norm_conv_out):
    seq = x.shape[0]
    tile = TOK_TILE
    n_tiles = seq // tile

    def body(x_ref, oc_ref, oa_ref, wo_ref, gf_ref, t_ref, pc_ref, hcc_ref, hcu_ref, cw_ref, gn_ref,
             dx2_ref, doa_ref, gwo_ref, dpc_ref, small_ref, loss_acc, zbuf, dbuf):
        step = pl.program_id(0)

        def small_add(row, value):
            small_ref[row:row + 1, :] += jnp.sum(value, axis=0, keepdims=True)

        @pl.when(step == 0)
        def _():
            gwo_ref[...] = jnp.zeros_like(gwo_ref)
            small_ref[...] = jnp.zeros_like(small_ref)
            loss_acc[...] = jnp.zeros_like(loss_acc)
            dbuf[tile:tile + 8, :] = jnp.zeros((8, D_CONV), F32)

        oc, oa = oc_ref[...], oa_ref[...]
        x2 = x_ref[...] + _nn(oc, wo_ref[0:D_CONV, :]) + _nn(oa, wo_ref[D_CONV:D_MIX, :])
        r = _rstd(x2)
        xhat = x2 * r
        err = xhat * gf_ref[...] - t_ref[...]
        loss_acc[...] += jnp.sum(err * err, axis=0, keepdims=True) * (0.5 / D_MODEL)
        dy = err * (1.0 / D_MODEL)
        small_add(SMALL_NORM_FINAL, dy * xhat)
        dx2 = _rms_bwd(dy * gf_ref[...], xhat, r)
        dx2_ref[...] = dx2
        db = dx2.astype(BF16)
        do = _nt(db, wo_ref[0:D_CONV, :])
        doa_ref[...] = _nt(db, wo_ref[D_CONV:D_MIX, :])
        gwo_ref[0:D_CONV, :] += _tn(oc, db)
        gwo_ref[D_CONV:D_MIX, :] += _tn(oa, db)

        is_first_tile = step == n_tiles - 1
        zbuf[0:8, :] = jnp.where(is_first_tile, 0.0, hcc_ref[...] * hcu_ref[...])
        cb, cc, cu, z, z1, z2, conv = _conv_core(pc_ref, zbuf, cw_ref)
        silu, dsilu = _silu_and_grad(pc_ref[:, 3 * D_CONV:4 * D_CONV])
        yc = cb * conv
        rc = _rstd(yc)
        chat = yc * rc
        dn = do * silu
        dpc_ref[:, 3 * D_CONV:4 * D_CONV] = (do * (chat * gn_ref[...]) * dsilu).astype(BF16)
        small_add(SMALL_NORM_CONV, dn * chat)
        dyc = _rms_bwd(dn * gn_ref[...], chat, rc)
        dpc_ref[:, 0:D_CONV] = (dyc * conv).astype(BF16)
        dconv = dyc * cb
        small_add(SMALL_CONV_W, dconv * z2)
        small_add(SMALL_CONV_W + 1, dconv * z1)
        small_add(SMALL_CONV_W + 2, dconv * z)
        dbuf[0:tile, :] = dconv
        dz = cw_ref[2:3, :] * dconv + cw_ref[1:2, :] * dbuf[1:tile + 1, :] + cw_ref[0:1, :] * dbuf[2:tile + 2, :]
        dpc_ref[:, D_CONV:2 * D_CONV] = (dz * cu).astype(BF16)
        dpc_ref[:, 2 * D_CONV:3 * D_CONV] = (dz * cc).astype(BF16)
        dbuf[tile:tile + 8, :] = dbuf[0:8, :]

        @pl.when(step == n_tiles - 1)
        def _():
            lane = lax.broadcasted_iota(jnp.int32, (1, D_MODEL), 1)
            small_ref[SMALL_MISC:SMALL_MISC + 1, :] = jnp.where(lane == SMALL_LOSS_LANE, jnp.sum(loss_acc[...]), 0.0)

    def rev(i):
        return n_tiles - 1 - i

    def row(width):
        return pl.BlockSpec((tile, width), lambda i: (rev(i), 0))

    def halo(col_block):
        return pl.BlockSpec((8, D_CONV), lambda i: (jnp.maximum(rev(i) * (tile // 8) - 1, 0), col_block))

    def const(shape):
        return pl.BlockSpec(shape, lambda i: (0, 0))

    return pl.pallas_call(
        body, name="out_proj_loss", grid=(n_tiles,),
        out_shape=(jax.ShapeDtypeStruct((seq, D_MODEL), F32), jax.ShapeDtypeStruct((seq, D_ATTN), F32),
                   jax.ShapeDtypeStruct((D_MIX, D_MODEL), F32), jax.ShapeDtypeStruct((seq, D_PC), BF16),
                   jax.ShapeDtypeStruct((SMALL_ROWS, D_MODEL), F32)),
        in_specs=[row(D_MODEL), row(D_CONV), row(D_ATTN), _resident(wo.shape), _resident((1, D_MODEL)), row(D_MODEL),
                  row(D_PC), halo(1), halo(2), _resident(conv_w.shape), _resident((1, D_CONV))],
        out_specs=(row(D_MODEL), row(D_ATTN), const((D_MIX, D_MODEL)), row(D_PC), const((SMALL_ROWS, D_MODEL))),
        scratch_shapes=[pltpu.VMEM((1, D_MODEL), F32), pltpu.VMEM((tile + 8, D_CONV), F32),
                        pltpu.VMEM((tile + 8, D_CONV), F32)],
        compiler_params=_params(62, ("arbitrary",)),
    )(x, oc, oa, wo, norm_final, target, pc, pc, pc, conv_w, norm_conv_out)


def _attn_bwd(q, kv, ga, ya, doa, probs, sink_probs, norm_attn_out, gwo, small):
    seq = q.shape[0]
    per_step = 2
    n_steps = seq // (per_step * BLK)
    stage_steps = (0, 1, n_steps // 4, (5 * n_steps) // 8, n_steps - 1, n_steps - 1)

    def body(q_ref, kvc_ref, kvp_ref, ga_ref, ya_ref, doa_ref, pr_ref, sp_ref, gn_ref, gwo_ref, small_ref,
             dqg_ref, small_out, gwo_sh, gna_ref, gs_ref, carry, dya_buf, *rs_scratch):
        step = pl.program_id(0)
        rs_stages = _rs_wout_stages(gwo_ref, gwo_sh, *rs_scratch)
        for at, stage in zip(stage_steps[:-1], rs_stages[:-1]):
            pl.when(step == at)(stage)

        @pl.when(step == 0)
        def _():
            gna_ref[...] = jnp.zeros_like(gna_ref)
            gs_ref[...] = jnp.zeros_like(gs_ref)
            carry[...] = jnp.zeros_like(carry)

        lane = lax.broadcasted_iota(jnp.int32, (BLK, 128), 1)

        def fold(bd):
            return (jnp.where(lane < 64, bd[0:BLK], 0.0) + jnp.where(lane >= 64, bd[BLK:2 * BLK], 0.0),
                    jnp.where(lane < 64, bd[2 * BLK:3 * BLK], 0.0) + jnp.where(lane >= 64, bd[3 * BLK:4 * BLK], 0.0))

        def one_block(b):
            rows = slice(b * BLK, (b + 1) * BLK)
            kv_prev = kvp_ref if b == 0 else kvc_ref.at[(b - 1) * BLK:b * BLK, :]
            ya = ya_ref[rows, :]
            r = _rstd(ya)
            xhat = ya * r
            silu, dsilu = _silu_and_grad(ga_ref[rows, :])
            do = doa_ref[rows, :]
            dn = do * silu
            dqg_ref[rows, D_ATTN:2 * D_ATTN] = (do * (xhat * gn_ref[...]) * dsilu).astype(BF16)
            gna_ref[...] += jnp.sum(dn * xhat, axis=0, keepdims=True)
            dya_buf[rows, :] = _rms_bwd(dn * gn_ref[...], xhat, r).astype(BF16)

            use_cur = _band_geometry(per_step * (n_steps - 1 - step) + b)[0]
            pairs = range(N_HEADS // 2)
            qps, kbd, vbd = _attn_operands(q_ref.at[rows, :], kvc_ref.at[rows, :], kv_prev)
            probs = [pr_ref[rows, h * BLK:(h + 1) * BLK].astype(F32) for h in range(N_HEADS)]
            dyps = [dya_buf[rows, j * 128:(j + 1) * 128] for j in pairs]
            dps = []
            for j in pairs:
                dps += _merge(_nt(dyps[j], vbd[j // 4]), use_cur)
            deltas = [jnp.sum(p * dp, axis=-1, keepdims=True) for p, dp in zip(probs, dps)]
            dss = [p * (dp - delta) for p, dp, delta in zip(probs, dps, deltas)]
            delta_lanes = sum(jnp.where(lane == h, deltas[h], 0.0) for h in range(N_HEADS))
            gs_ref[...] -= jnp.sum(sp_ref[rows, :] * delta_lanes, axis=0, keepdims=True)
            ds4s = [_split(dss[2 * j], dss[2 * j + 1], use_cur).astype(BF16) for j in pairs]
            p4s = [_split(probs[2 * j], probs[2 * j + 1], use_cur).astype(BF16) for j in pairs]
            dqg_ref[rows, 0:D_ATTN] = jnp.concatenate([_nn(ds4s[j], kbd[j // 4]) * SCALE for j in pairs], axis=1).astype(BF16)
            sums = []
            for group in range(N_HEADS // HEADS_PER_KV):
                acc = [jnp.zeros((BLK, 128), F32) for _ in range(4)]
                for j in range(group * 4, group * 4 + 4):
                    for slot, part in enumerate(fold(_tn(ds4s[j], qps[j])) + fold(_tn(p4s[j], dyps[j]))):
                        acc[slot] = acc[slot] + part
                sums.append([a + pltpu.roll(a, 64, 1) for a in acc])
            dk_cur, dk_prev, dv_cur, dv_prev = (jnp.where(lane < 64, lo, hi) for lo, hi in zip(sums[0], sums[1]))
            dqg_ref[rows, 2 * D_ATTN:2 * D_ATTN + 2 * D_KV] = (jnp.concatenate([dk_cur, dv_cur], axis=1) + carry[...]).astype(BF16)
            carry[...] = jnp.concatenate([dk_prev, dv_prev], axis=1)

        for b in reversed(range(per_step)):
            one_block(b)

        @pl.when(step == n_steps - 1)
        def _():
            small_out[...] = small_ref[...]
            small_out[SMALL_NORM_ATTN:SMALL_NORM_ATTN + 1, :] = gna_ref[...]
            small_out[SMALL_MISC:SMALL_MISC + 1, 0:128] = small_ref[SMALL_MISC:SMALL_MISC + 1, 0:128] + gs_ref[...]

        pl.when(step == stage_steps[-1])(rs_stages[-1])

    def rows(width):
        return pl.BlockSpec((per_step * BLK, width), lambda i: (n_steps - 1 - i, 0))

    kv_prev = pl.BlockSpec((BLK, 2 * D_KV), lambda i: (jnp.maximum(per_step * (n_steps - 1 - i) - 1, 0), 0))
    any_spec = pl.BlockSpec(memory_space=pl.ANY)
    return pl.pallas_call(
        body, name="attn_bwd", grid=(n_steps,),
        out_shape=(jax.ShapeDtypeStruct((seq, D_QG), BF16), pltpu.HBM(small.shape, F32),
                   pltpu.HBM((gwo.shape[0] // N_CHIPS, gwo.shape[1]), F32)),
        in_specs=[rows(D_ATTN), rows(2 * D_KV), kv_prev, rows(D_ATTN), rows(D_ATTN), rows(D_ATTN),
                  rows(N_HEADS * BLK), rows(128), _resident((1, D_ATTN)), any_spec, _resident(small.shape)],
        out_specs=(rows(D_QG), pl.BlockSpec(small.shape, lambda i: (0, 0)), any_spec),
        scratch_shapes=[pltpu.VMEM((1, D_ATTN), F32), pltpu.VMEM((1, 128), F32),
                        pltpu.VMEM((BLK, 2 * D_KV), F32), pltpu.VMEM((per_step * BLK, D_ATTN), BF16)] + _rs_wout_scratch(gwo.shape),
        compiler_params=_params(44, ("arbitrary",)),
    )(q, kv, kv, ga, ya, doa, probs, sink_probs, norm_attn_out, gwo, small)


GBLK = 256
GSUB = 64
PAIR_RING = 4
LAG_PAIR, LAG_HOP1, LAG_HOP2 = 1, 7, 14


def _bwd_in(dpc, dqg, h, wt, x, norm_in, dx2, small):
    seq = x.shape[0]
    n_blk = D_IN_PROJ // GBLK
    per_chip = n_blk // N_CHIPS
    n_slots = (n_blk + 1) // 2
    n_sub = GBLK // GSUB
    chip_rows = D_IN_PROJ // N_CHIPS
    tile = TOK_TILE
    n_tiles = seq // tile
    n_steps = n_blk + max(n_tiles, LAG_HOP2)
    chunk = min(seq, 512)
    blk_q, blk_kv, blk_ga = ROW_Q // GBLK, ROW_KV // GBLK, ROW_GA // GBLK

    def block_of(i):
        k = i % N_CHIPS
        robin = per_chip * ((k % 2) * 2 + k // 2) + i // N_CHIPS
        if isinstance(i, int):
            return robin if i < per_chip * N_CHIPS else i
        return jnp.where(i < per_chip * N_CHIPS, robin, i)

    def owner_of(i):
        return (i // N_CHIPS) % 2

    def slot_of(i):
        return (i // (2 * N_CHIPS)) * N_CHIPS + i % N_CHIPS

    def body(dpc_ref, dqg_ref, wt_ref, h_ref, x_ref, g_ref, dx2_ref, small_ref, gx_ref, small_sum, gwt_sh,
             dh_acc, gni, keep, pbuf, xbuf, land, land2, small_land,
             pair_send, pair_recv, h1_send, h1_recv, h2_send, h2_recv, sw_send, sw_recv, sm_send, sm_recv, out_sem):
        step = pl.program_id(0)
        x_i, y_i, c = lax.axis_index("x"), lax.axis_index("y"), lax.axis_index("c")
        me = 4 * x_i + 2 * y_i + c
        j = 2 * x_i + y_i
        pa = (_xor(x_i, 1 - c), _xor(y_i, c), c)
        pb = (_xor(x_i, c), _xor(y_i, 1 - c), c)
        sib = (x_i, y_i, 1 - c)
        ja = 2 * pa[0] + pa[1]
        jb = 2 * pb[0] + pb[1]
        jd = 3 - j

        def remote(src, dst, send, recv, to):
            return pltpu.make_async_remote_copy(src_ref=src, dst_ref=dst, send_sem=send, recv_sem=recv,
                                                device_id=to, device_id_type=MESH)

        def piece(ref, slot, u, n):
            return ref.at[slot, pl.ds(u * GSUB, n * GSUB), :]

        def chip_rows_at(ref, local, n):
            return ref.at[pl.ds(pl.multiple_of(local, GSUB), n * GSUB), :]

        def pair_copy(i):
            slot = slot_of(i)
            return remote(pbuf.at[i % PAIR_RING], land.at[slot], pair_send.at[slot], pair_recv.at[slot], sib)

        def h1_copy(slot, u, n):
            k = slot * n_sub + u
            return remote(piece(xbuf, slot, u, n), piece(xbuf, slot, u, n), h1_send.at[k], h1_recv.at[k], pa)

        def h2_copy(slot, u, n, local):
            k = slot * n_sub + u
            return remote(piece(xbuf, slot, u, n), chip_rows_at(land2, local, n), h2_send.at[k], h2_recv.at[k], pb)

        def sw_copy(slot, u, n, local):
            k = slot * n_sub + u
            return remote(piece(keep, slot, u, n), chip_rows_at(gwt_sh, local, n), sw_send.at[k], sw_recv.at[k], sib)

        def own_copy(slot, u, n, local):
            return pltpu.make_async_copy(piece(keep, slot, u, n), chip_rows_at(gwt_sh, local, n), out_sem.at[slot * n_sub + u])

        def owned(i):
            return (i >= 0) & (i < n_blk) & (owner_of(i) == c)

        def chip_of(blk, u):
            row = blk * GBLK + u * GSUB
            chip = row // chip_rows
            return chip, row - chip * chip_rows

        def pieces(blk):
            first, local = chip_of(blk, 0)
            whole = first == chip_of(blk, n_sub - 1)[0]
            if isinstance(blk, int):
                return [(True, 0, n_sub, first, local)] if whole else [(True, u, 1) + chip_of(blk, u) for u in range(n_sub)]
            return [(whole, 0, n_sub, first, local)] + [(jnp.logical_not(whole), u, 1) + chip_of(blk, u) for u in range(n_sub)]

        @pl.when(step == 0)
        def _():
            dh_acc[...] = jnp.zeros_like(dh_acc)
            gni[...] = jnp.zeros_like(gni)

        @pl.when(step < n_blk)
        def _():
            from_pc = block_of(step) < blk_q
            block = _tn(jnp.where(from_pc, dpc_ref[...], dqg_ref[...]), h_ref[...])
            for t in range(0, seq, chunk):
                d = jnp.where(from_pc, dpc_ref[t:t + chunk, :], dqg_ref[t:t + chunk, :])
                dh_acc[t:t + chunk, :] += _nn(d, wt_ref[...])

            @pl.when(owner_of(step) == c)
            def _():
                keep[slot_of(step)] = block

            @pl.when(owner_of(step) != c)
            def _():
                @pl.when(step >= 2 * PAIR_RING)
                def _():
                    pair_copy(step - 2 * PAIR_RING).wait_send()
                pbuf[step % PAIR_RING] = block.astype(BF16)
                pair_copy(step).start()

        i1 = step - LAG_PAIR

        @pl.when(owned(i1))
        def _():
            slot = slot_of(i1)
            pair_copy(i1).wait_recv()
            _accumulate(keep.at[slot], land.at[slot])
            for cond, u, n, chip, _ in pieces(block_of(i1)):
                @pl.when(cond & ((chip == ja) | (chip == jd)))
                def _(u=u, n=n):
                    _cast_rows(piece(keep, slot, u, n), piece(xbuf, slot, u, n))
                    h1_copy(slot, u, n).start()

        i2 = step - LAG_HOP1

        @pl.when(owned(i2))
        def _():
            slot = slot_of(i2)
            for cond, u, n, chip, local in pieces(block_of(i2)):
                @pl.when(cond & ((chip == j) | (chip == jb)))
                def _(u=u, n=n, chip=chip, local=local):
                    h1_copy(slot, u, n).wait_recv()
                    _accumulate(piece(keep, slot, u, n), piece(xbuf, slot, u, n))

                    @pl.when(chip == jb)
                    def _():
                        _cast_rows(piece(keep, slot, u, n), piece(xbuf, slot, u, n))
                        h2_copy(slot, u, n, local).start()

                @pl.when(cond & ((chip == ja) | (chip == jd)))
                def _(u=u, n=n):
                    h1_copy(slot, u, n).wait_send()

        i3 = step - LAG_HOP2

        @pl.when(owned(i3))
        def _():
            slot = slot_of(i3)
            for cond, u, n, chip, local in pieces(block_of(i3)):
                @pl.when(cond & (chip == j))
                def _(u=u, n=n, local=local):
                    h2_copy(slot, u, n, local).wait_recv()
                    _accumulate(piece(keep, slot, u, n), chip_rows_at(land2, local, n))
                    own_copy(slot, u, n, local).start()
                    sw_copy(slot, u, n, local).start()

                @pl.when(cond & (chip == jb))
                def _(u=u, n=n, local=local):
                    h2_copy(slot, u, n, local).wait_send()

        e = step - n_blk

        @pl.when((e >= 0) & (e < n_tiles))
        def _():
            dh = dh_acc[pl.ds(pl.multiple_of(e * tile, tile), tile), :]
            xv = x_ref[...]
            r = _rstd(xv)
            xhat = xv * r
            gni[...] += jnp.sum(dh * xhat, axis=0, keepdims=True)
            gx_ref[...] = _rms_bwd(dh * g_ref[...], xhat, r) + dx2_ref[...]

        others = [(dx, dy, dc) for dx in (0, 1) for dy in (0, 1) for dc in (0, 1)][1:]
        sends = [remote(small_land.at[me], small_land.at[me], sm_send.at[k], sm_recv.at[k],
                        (_xor(x_i, dx), _xor(y_i, dy), _xor(c, dc))) for k, (dx, dy, dc) in enumerate(others)]

        @pl.when(step == min(n_blk + n_tiles, n_steps - 1))
        def _():
            small_land[me] = small_ref[...]
            small_land[me, SMALL_NORM_IN:SMALL_NORM_IN + 1, :] = gni[...]
            for cp in sends:
                cp.start()

        @pl.when(step == n_steps - 1)
        def _():
            for i in range(n_blk):
                if i + 2 * PAIR_RING >= n_blk:
                    @pl.when(owner_of(i) != c)
                    def _(i=i):
                        pair_copy(i).wait_send()
                for _, u, n, chip, local in pieces(block_of(i)):
                    @pl.when((j == chip) & (c == owner_of(i)))
                    def _(i=i, u=u, n=n, local=local):
                        sw_copy(slot_of(i), u, n, local).wait_send()
                        own_copy(slot_of(i), u, n, local).wait()

                    @pl.when((j == chip) & (c != owner_of(i)))
                    def _(i=i, u=u, n=n, local=local):
                        sw_copy(slot_of(i), u, n, local).wait_recv()
            for cp in sends:
                cp.wait_recv()
            total = small_land[0]
            for dev in range(1, 8):
                total = total + small_land[dev]
            small_sum[...] = total
            for cp in sends:
                cp.wait_send()

    def blk_at(i):
        return block_of(jnp.clip(i, 0, n_blk - 1))

    last_pc_step = max(i for i in range(n_blk) if block_of(i) < blk_q)

    def next_block(i, in_pc):
        i = jnp.clip(i, 0, n_blk - 1)
        step = jnp.full_like(i, last_pc_step if in_pc else n_blk - 1)
        for ahead in reversed(range(N_CHIPS)):
            cand = jnp.minimum(i + ahead, n_blk - 1)
            step = jnp.where((block_of(cand) < blk_q) == in_pc, cand, step)
        return block_of(step)

    def dqg_block(i):
        b = next_block(i, False)
        q_blk = jnp.clip(b - blk_q, 0, blk_kv - blk_q - 1)
        ga_blk = (D_ATTN // GBLK) + jnp.clip(b - blk_ga, 0, n_blk - blk_ga - 1)
        return jnp.where(b < blk_kv, q_blk, jnp.where(b == blk_kv, 2 * D_ATTN // GBLK, ga_blk))

    def tok(i):
        return (jnp.clip(i - n_blk, 0, n_tiles - 1), 0)

    n_piece = n_slots * n_sub
    dma = pltpu.SemaphoreType.DMA
    return pl.pallas_call(
        body, name="bwd_in", grid=(n_steps,),
        out_shape=(jax.ShapeDtypeStruct((seq, D_MODEL), F32), jax.ShapeDtypeStruct(small.shape, F32),
                   jax.ShapeDtypeStruct((chip_rows, D_MODEL), F32)),
        in_specs=[pl.BlockSpec((seq, GBLK), lambda i: (0, next_block(i, True))),
                  pl.BlockSpec((seq, GBLK), lambda i: (0, dqg_block(i))),
                  pl.BlockSpec((GBLK, D_MODEL), lambda i: (blk_at(i), 0)),
                  _resident(h.shape),
                  pl.BlockSpec((tile, D_MODEL), tok), _resident((1, D_MODEL)), pl.BlockSpec((tile, D_MODEL), tok),
                  _resident(small.shape)],
        out_specs=(pl.BlockSpec((tile, D_MODEL), tok), pl.BlockSpec(small.shape, lambda i: (0, 0)),
                   pl.BlockSpec(memory_space=pl.ANY)),
        scratch_shapes=[pltpu.VMEM((seq, D_MODEL), F32), pltpu.VMEM((1, D_MODEL), F32),
                        pltpu.VMEM((n_slots, GBLK, D_MODEL), F32), pltpu.VMEM((PAIR_RING, GBLK, D_MODEL), BF16),
                        pltpu.VMEM((n_slots, GBLK, D_MODEL), BF16), pltpu.VMEM((n_slots, GBLK, D_MODEL), BF16),
                        pltpu.VMEM((chip_rows, D_MODEL), BF16), pltpu.VMEM((8,) + small.shape, F32),
                        dma((n_slots,)), dma((n_slots,)), dma((n_piece,)), dma((n_piece,)), dma((n_piece,)),
                        dma((n_piece,)), dma((n_piece,)), dma((n_piece,)), dma((7,)), dma((7,)), dma((n_piece,))],
        compiler_params=_params(62, ("arbitrary",)),
    )(dpc, dqg, wt, h, x, norm_in, dx2, small)


def _accumulate(dst_ref, src_ref, rows=16):
    def step(i, carry):
        sl = pl.ds(pl.multiple_of(i * rows, rows), rows)
        dst_ref[sl, :] = dst_ref[sl, :] + src_ref[sl, :].astype(F32)
        return carry
    lax.fori_loop(0, dst_ref.shape[0] // rows, step, 0)


def _rs_wout_scratch(gwo_shape):
    o_half, width = gwo_shape[0] // N_CHIPS // 2, gwo_shape[1]
    return [pltpu.VMEM((4, o_half, width), F32), pltpu.VMEM((4, o_half, width), BF16),
            pltpu.VMEM((4, o_half, width), BF16), pltpu.VMEM((2, o_half, width), BF16),
            pltpu.VMEM((o_half, width), BF16),
            pltpu.SemaphoreType.DMA((8,)), pltpu.SemaphoreType.DMA((8,)), pltpu.SemaphoreType.DMA((4,))]


def _rs_wout_stages(gwo_ref, gwo_sh, acc_o, sb_o, r1o, r2o, r3o, send_sems, recv_sems, local_sems):
    o_rows = gwo_ref.shape[0] // N_CHIPS
    o_half = o_rows // 2
    x, y, c = lax.axis_index("x"), lax.axis_index("y"), lax.axis_index("c")
    j = 2 * x + y
    pa = (_xor(x, 1 - c), _xor(y, c), c)
    pb = (_xor(x, c), _xor(y, 1 - c), c)
    sib = (x, y, 1 - c)
    ja = 2 * pa[0] + pa[1]
    jb = 2 * pb[0] + pb[1]
    jd = 3 - j
    order = (ja, jd, jb, j)
    sib_order = (jb, jd, ja, j)

    def rcopy(k, src, dst, to):
        return pltpu.make_async_remote_copy(src_ref=src, dst_ref=dst, send_sem=send_sems.at[k],
                                            recv_sem=recv_sems.at[k], device_id=to, device_id_type=MESH)

    def o_rows_of(chip, half):
        return gwo_ref.at[pl.ds(pl.multiple_of(chip * o_rows + half * o_half, 8), o_half), :]

    def loads(chips, half):
        return [pltpu.make_async_copy(o_rows_of(chip, half), acc_o.at[s], local_sems.at[s]) for s, chip in enumerate(chips)]

    def pair_send(s):
        return rcopy(s, sb_o.at[s], r1o.at[s], sib)

    def out_half(half):
        return gwo_sh.at[pl.ds(pl.multiple_of(half * o_half, 8), o_half), :]

    hop1 = [rcopy(4 + s, sb_o.at[s], r2o.at[s], pa) for s in range(2)]
    hop2 = rcopy(6, sb_o.at[2], r3o, pb)
    swap = rcopy(7, acc_o.at[3], out_half(c), sib)
    mine = pltpu.make_async_copy(acc_o.at[3], out_half(c), local_sems.at[0])

    def resend(s, copy):
        pair_send(s).wait_send()
        _cast_rows(acc_o.at[s], sb_o.at[s])
        copy.start()

    def stage_load():
        for cp in loads(sib_order, 1 - c):
            cp.start()

    def stage_pair():
        for s, (cp, mine_in) in enumerate(zip(loads(sib_order, 1 - c), loads(order, c))):
            cp.wait()
            _cast_rows(acc_o.at[s], sb_o.at[s])
            pair_send(s).start()
            mine_in.start()

    def stage_hop1():
        for s, cp in enumerate(loads(order, c)):
            cp.wait()
            pair_send(s).wait_recv()
            _accumulate(acc_o.at[s], r1o.at[s])
            if s < 2:
                resend(s, hop1[s])

    def stage_hop2():
        hop1[1].wait_recv()
        _accumulate(acc_o.at[2], r2o.at[1])
        resend(2, hop2)
        hop1[0].wait_recv()
        _accumulate(acc_o.at[3], r2o.at[0])

    def stage_final():
        hop2.wait_recv()
        _accumulate(acc_o.at[3], r3o)
        mine.start()
        swap.start()

    def stage_drain():
        rcopy(7, acc_o.at[3], out_half(1 - c), sib).wait_recv()
        for cp in [pair_send(3)] + hop1 + [hop2, swap]:
            cp.wait_send()
        mine.wait()

    return stage_load, stage_pair, stage_hop1, stage_hop2, stage_final, stage_drain


def _adamw_big(groups, passed):
    arrays = [a for group in groups for a in group[:4]]
    steps = [group[0].shape[0] // group[4] for group in groups]
    first = [sum(steps[:k]) for k in range(len(steps) + 1)]
    n = len(arrays)

    def body(*refs):
        ins, passed_in, outs, passed_out = refs[:n], refs[n], refs[n + 1:2 * n + 1], refs[2 * n + 1]
        i = pl.program_id(0)
        for k in range(len(groups)):
            @pl.when((i >= first[k]) & (i < first[k + 1]))
            def _(k=k):
                w_ref, g_ref, m_ref, v_ref = ins[4 * k:4 * k + 4]
                _adamw_update(w_ref, g_ref[...], m_ref, v_ref, *outs[4 * k:4 * k + 4])

        @pl.when(i < steps[0])
        def _():
            passed_out[...] = passed_in[...]

    def spec(k, rows, width):
        return pl.BlockSpec((rows, width), lambda i: (jnp.clip(i - first[k], 0, steps[k] - 1), 0))

    specs = [spec(k, group[4], group[0].shape[1]) for k, group in enumerate(groups) for _ in range(4)]
    specs.append(spec(0, passed.shape[0] // steps[0], passed.shape[1]))
    out = pl.pallas_call(
        body, name="adamw_big", grid=(first[-1],),
        out_shape=tuple(jax.ShapeDtypeStruct(a.shape, a.dtype) for a in arrays + [passed]),
        in_specs=specs, out_specs=tuple(specs),
        compiler_params=_params(40, ("arbitrary",)),
    )(*arrays, passed)
    return [tuple(out[4 * k:4 * k + 4]) for k in range(len(groups))], out[-1]


def _adamw_update(w_ref, gv, m_ref, v_ref, go_ref, d_ref, nm_ref, nv_ref, at=...):
    go_ref[at] = gv
    nm = ADAM_B1 * m_ref[at] + (1.0 - ADAM_B1) * gv
    nv = ADAM_B2 * v_ref[at] + (1.0 - ADAM_B2) * (gv * gv)
    m_hat = nm / (1.0 - ADAM_B1 ** ADAM_STEP)
    v_hat = nv / (1.0 - ADAM_B2 ** ADAM_STEP)
    d_ref[at] = -ADAM_LR * (m_hat / (jnp.sqrt(v_hat) + ADAM_EPS) + ADAM_WD * w_ref[at])
    nm_ref[at] = nm
    nv_ref[at] = nv


def _adamw_small(chip, small_sum, weights, grads_of, ms, vs):
    n = len(weights)

    def body(chip_ref, small_ref, *refs):
        ins, outs, loss_ref = refs[:3 * n], refs[3 * n:-1], refs[-1]
        for k in range(n):
            w_ref, m_ref, v_ref = ins[3 * k:3 * k + 3]
            for at, gv in grads_of[k](small_ref, chip_ref):
                _adamw_update(w_ref, gv, m_ref, v_ref, *outs[4 * k:4 * k + 4], at=at)
        loss_ref[...] = small_ref[SMALL_MISC:SMALL_MISC + 1, SMALL_LOSS_LANE:SMALL_LOSS_LANE + 1]

    flat = [a for group in zip(weights, ms, vs) for a in group]
    vmem = pl.BlockSpec(memory_space=pltpu.VMEM)
    out = pl.pallas_call(
        body, name="adamw_small",
        out_shape=tuple(jax.ShapeDtypeStruct(w.shape, F32) for w in weights for _ in range(4))
        + (jax.ShapeDtypeStruct((1, 1), F32),),
        in_specs=[pl.BlockSpec(memory_space=pltpu.SMEM)] + [vmem] * (1 + 3 * n), out_specs=(vmem,) * (4 * n + 1),
    )(chip, small_sum, *flat)
    return [tuple(out[4 * k:4 * k + 4]) for k in range(n)], out[-1][0, 0]


def kernel(x, norm_in, w_in, conv_w, attn_sinks, norm_conv_out, norm_attn_out, w_out, norm_final, loss_target, m_norm_in, m_w_in, m_conv_w, m_attn_sinks, m_norm_conv_out, m_norm_attn_out, m_w_out, m_norm_final, v_norm_in, v_w_in, v_conv_w, v_attn_sinks, v_norm_conv_out, v_norm_attn_out, v_w_out, v_norm_final):
    chip = 2 * lax.axis_index("x") + lax.axis_index("y")
    xs, target = x[0], loss_target[0]
    norm_final2 = norm_final.reshape(1, D_MODEL)
    w_in_t, m_w_in_t, v_w_in_t = w_in[0].T, m_w_in[0].T, v_w_in[0].T

    def from_hbm(*arrays):
        return tuple(pltpu.with_memory_space_constraint(a, pltpu.HBM) for a in arrays)

    h, pc, q, kv, ga, oc, ya, oa, probs, sink_probs, wt, cw, wo = _fwd_in(
        xs, norm_in, w_in_t, w_out[0], conv_w.transpose(1, 0, 2), norm_conv_out, attn_sinks, norm_attn_out)
    dx2, doa, gwo, dpc, small = _out_proj_loss(xs, oc, oa, wo, norm_final2, target, pc, cw, norm_conv_out)
    q, kv, ga, ya, doa, probs, gwo, small = from_hbm(q, kv, ga, ya, doa, probs, gwo, small)
    dqg, small, gwo_sh = _attn_bwd(q, kv, ga, ya, doa, probs, sink_probs, norm_attn_out, gwo, small)
    grad_x, small_sum, gwt_sh = _bwd_in(dpc, dqg, h, wt, xs, norm_in, dx2, small)

    (up_w_in, up_w_out), grad_x = _adamw_big(
        [from_hbm(w_in_t, gwt_sh, m_w_in_t, v_w_in_t) + (200,), from_hbm(w_out[0], gwo_sh, m_w_out[0], v_w_out[0]) + (128,)],
        *from_hbm(grad_x))
    up_w_in = tuple(o.T[None] for o in up_w_in)
    up_w_out = tuple(o[None] for o in up_w_out)

    def row_of(r):
        return lambda small_ref, chip_ref: [(..., small_ref[r:r + 1, :])]

    def sink_lanes(small_ref, chip_ref):
        return [(..., small_ref[SMALL_MISC:SMALL_MISC + 1, 0:N_HEADS])]

    def conv_taps(small_ref, chip_ref):
        width = D_CONV // N_CHIPS
        cols = pl.ds(pl.multiple_of(chip_ref[0] * width, width), width)
        return [(k, small_ref[pl.ds(SMALL_CONV_W + k, 1), cols]) for k in range(conv_w.shape[1])]

    def small_view(a):
        return a.transpose(1, 0, 2) if a.ndim == 3 else a.reshape(-1, a.shape[-1])

    small_w = (norm_in, conv_w, attn_sinks, norm_conv_out, norm_attn_out, norm_final)
    small_m = (m_norm_in, m_conv_w, m_attn_sinks, m_norm_conv_out, m_norm_attn_out, m_norm_final)
    small_v = (v_norm_in, v_conv_w, v_attn_sinks, v_norm_conv_out, v_norm_attn_out, v_norm_final)
    small_g = (row_of(SMALL_NORM_IN), conv_taps, sink_lanes, row_of(SMALL_NORM_CONV), row_of(SMALL_NORM_ATTN),
               row_of(SMALL_NORM_FINAL))
    w_views, m_views, v_views = (tuple(small_view(a) for a in group) for group in (small_w, small_m, small_v))
    up_small, loss = _adamw_small(chip.astype(jnp.int32).reshape(1), small_sum, w_views, small_g, m_views, v_views)
    up_small = [tuple(o.transpose(1, 0, 2) if w.ndim == 3 else o.reshape(w.shape) for o in up)
                for up, w in zip(up_small, small_w)]
    up_norm_in, up_conv_w, up_sinks, up_norm_conv, up_norm_attn, up_norm_final = up_small
    updates = (up_norm_in, up_w_in, up_conv_w, up_sinks, up_norm_conv, up_norm_attn, up_w_out, up_norm_final)
    grads_out, deltas, new_m, new_v = zip(*updates)
    return (loss, grad_x[None], *grads_out, *deltas, *new_m, *new_v)
```

```python
import jax
import jax.numpy as jnp
from jax import lax
from jax.experimental import pallas as pl
from jax.experimental.pallas import tpu as pltpu

F32 = jnp.float32
BF16 = jnp.bfloat16
MESH = pl.DeviceIdType.MESH

D_MODEL = 1024
D_CONV = 1024
D_ATTN = 1024
D_KV = 128
D_QG = 2 * D_ATTN + 2 * D_KV
D_MIX = D_CONV + D_ATTN
D_PC = 4 * D_CONV
D_IN_PROJ = D_PC + 2 * D_ATTN + 2 * D_KV
ROW_Q = D_PC
ROW_KV = ROW_Q + D_ATTN
ROW_GA = ROW_KV + 2 * D_KV
N_HEADS = 16
HEAD_DIM = 64
HEADS_PER_KV = 8
BLK = 128
N_CHIPS = 4
RMS_EPS = 1e-5
SCALE = HEAD_DIM ** -0.5
SLOPES = tuple(2.0 ** (-8.0 * (h + 1) / N_HEADS) for h in range(N_HEADS))

ADAM_LR, ADAM_B1, ADAM_B2, ADAM_EPS, ADAM_WD, ADAM_STEP = 0.001, 0.9, 0.999, 1e-08, 0.01, 10

SMALL_ROWS = 8
SMALL_NORM_IN, SMALL_NORM_CONV, SMALL_NORM_ATTN, SMALL_NORM_FINAL, SMALL_CONV_W, SMALL_MISC = 0, 1, 2, 3, 4, 7
SMALL_LOSS_LANE = N_HEADS

TOK_TILE = 256
PC_PIECE = 512
MIB = 1 << 20


def _params(vmem_mib, semantics=None):
    return pltpu.CompilerParams(dimension_semantics=semantics, vmem_limit_bytes=vmem_mib * MIB)


def _nn(a, b):
    return jnp.dot(a, b, preferred_element_type=F32)


def _nt(a, b):
    return lax.dot_general(a, b, (((1,), (1,)), ((), ())), preferred_element_type=F32)


def _tn(a, b):
    return lax.dot_general(a, b, (((0,), (0,)), ((), ())), preferred_element_type=F32)


def _rstd(v):
    return lax.rsqrt(jnp.mean(v * v, axis=-1, keepdims=True) + RMS_EPS)


def _rms_bwd(g, xhat, rstd):
    return rstd * (g - xhat * jnp.mean(g * xhat, axis=-1, keepdims=True))


def _silu_and_grad(g):
    s = jax.nn.sigmoid(g)
    return g * s, s * (1.0 + g * (1.0 - s))


def _resident(shape):
    return pl.BlockSpec(shape, lambda *_: (0,) * len(shape), pipeline_mode=pl.Buffered(1))


def _xor(a, b):
    return a + b - 2 * a * b


def _cast_rows(src_ref, dst_ref, rows=32):
    def step(i, carry):
        sl = pl.ds(pl.multiple_of(i * rows, rows), rows)
        dst_ref[sl, :] = src_ref[sl, :].astype(dst_ref.dtype)
        return carry
    lax.fori_loop(0, src_ref.shape[0] // rows, step, 0)


def _ag_scratch(shard_shape):
    rows, width = shard_shape
    return [pltpu.VMEM((rows, width), F32), pltpu.VMEM((rows, width), BF16), pltpu.VMEM((3, rows // 2, width), BF16),
            pltpu.SemaphoreType.DMA((6,)), pltpu.SemaphoreType.DMA((6,)), pltpu.SemaphoreType.DMA((4,))]


def _ag_stages(sh_ref, out, f32_buf, own, land, send_sems, recv_sems, local_sems):
    rows = sh_ref.shape[0]
    half = rows // 2
    x, y, c = lax.axis_index("x"), lax.axis_index("y"), lax.axis_index("c")
    j = 2 * x + y
    p1 = (_xor(x, c), _xor(y, 1 - c), c)
    p2 = (_xor(x, 1 - c), _xor(y, c), c)
    sib = (x, y, 1 - c)
    j1 = 2 * p1[0] + p1[1]
    j2 = 2 * p2[0] + p2[1]
    j3 = 3 - j

    def rows_of(chip, hf):
        return out.at[pl.ds(pl.multiple_of(chip * rows + hf * half, 16), half), :]

    def rcopy(k, src, dst, to):
        return pltpu.make_async_remote_copy(src_ref=src, dst_ref=dst, send_sem=send_sems.at[k],
                                            recv_sem=recv_sems.at[k], device_id=to, device_id_type=MESH)

    my_half = own.at[pl.ds(pl.multiple_of(c * half, 16), half), :]
    hop1 = rcopy(0, my_half, land.at[0], p1)
    hop2_own = rcopy(1, my_half, land.at[1], p2)
    hop2_fwd = rcopy(2, land.at[0], land.at[2], p2)
    swaps = [rcopy(3 + s, land.at[s], rows_of(chip, c), sib) for s, chip in enumerate((j1, j2, j3))]
    keeps = [pltpu.make_async_copy(land.at[s], rows_of(chip, c), local_sems.at[1 + s]) for s, chip in enumerate((j1, j2, j3))]
    load = pltpu.make_async_copy(sh_ref, f32_buf, local_sems.at[0])
    own_out = pltpu.make_async_copy(own, out.at[pl.ds(pl.multiple_of(j * rows, 16), rows), :], local_sems.at[0])

    def stage_load():
        load.start()

    def stage_send():
        load.wait()
        _cast_rows(f32_buf, own)
        own_out.start()
        hop1.start()

    def stage_forward():
        hop1.wait_recv()
        hop2_own.start()
        hop2_fwd.start()
        swaps[0].start()
        keeps[0].start()

    def stage_publish():
        hop2_own.wait_recv()
        swaps[1].start()
        keeps[1].start()
        hop2_fwd.wait_recv()
        swaps[2].start()
        keeps[2].start()

    def stage_drain():
        for s, chip in enumerate((j2, j1, j3)):
            rcopy(3 + s, my_half, rows_of(chip, 1 - c), sib).wait_recv()
        for cp in [hop1, hop2_own, hop2_fwd] + swaps:
            cp.wait_send()
        for cp in [own_out] + keeps:
            cp.wait()

    return stage_load, stage_send, stage_forward, stage_publish, stage_drain


AG_CAST_ROWS = 400


def _gather_resident(sh_ref, out, f32_buf, send_sems, recv_sems, local_sems, after_first_hop):
    rows = sh_ref.shape[0]
    half = rows // 2
    x, y, c = lax.axis_index("x"), lax.axis_index("y"), lax.axis_index("c")
    j = 2 * x + y
    p1 = (_xor(x, c), _xor(y, 1 - c), c)
    p2 = (_xor(x, 1 - c), _xor(y, c), c)
    sib = (x, y, 1 - c)
    j1 = 2 * p1[0] + p1[1]
    j2 = 2 * p2[0] + p2[1]
    j3 = 3 - j

    def rows_of(chip, hf):
        return out.at[pl.ds(pl.multiple_of(chip * rows + hf * half, 16), half), :]

    def send(k, chip, to):
        return pltpu.make_async_remote_copy(src_ref=rows_of(chip, c), dst_ref=rows_of(chip, c), send_sem=send_sems.at[k],
                                            recv_sem=recv_sems.at[k], device_id=to, device_id_type=MESH)

    per_half = half // AG_CAST_ROWS

    chunks = [c * per_half + k for k in range(per_half)] + [(1 - c) * per_half + k for k in range(per_half)]

    def load(n):
        lo = pl.multiple_of(chunks[n] * AG_CAST_ROWS, 16)
        return pltpu.make_async_copy(sh_ref.at[pl.ds(lo, AG_CAST_ROWS), :], f32_buf.at[n % 2], local_sems.at[n % 2])

    hop1 = send(0, j, p1)
    load(0).start()
    load(1).start()
    for n in range(len(chunks)):
        load(n).wait()
        lo = pl.multiple_of(chunks[n] * AG_CAST_ROWS, 16)
        _cast_rows(f32_buf.at[n % 2], out.at[pl.ds(pl.multiple_of(j * rows + lo, 16), AG_CAST_ROWS), :], rows=16)
        if n + 2 < len(chunks):
            load(n + 2).start()
        if n == per_half - 1:
            hop1.start()
    hop1.wait_recv()
    sends = [hop1, send(1, j, p2), send(2, j1, p2), send(3, j1, sib)]
    for cp in sends[1:]:
        cp.start()
    after_first_hop()
    sends[1].wait_recv()
    sends.append(send(4, j2, sib))
    sends[-1].start()
    sends[2].wait_recv()
    sends.append(send(5, j3, sib))
    sends[-1].start()
    for k in (3, 4, 5):
        send(k, j, sib).wait_recv()
    for cp in sends:
        cp.wait_send()


def _fwd_in(x, norm_in, wt_sh, wo_sh, cw_sh, norm_conv_out, sinks, norm_attn_out):
    seq = x.shape[0]
    tile = TOK_TILE
    n_tiles = seq // tile
    stage_steps = (0, n_tiles // 4, (5 * n_tiles) // 8, n_tiles - 1)
    blocks = tile // BLK
    cw_cols = cw_sh.shape[-1]

    def body(x_ref, g_ref, wtsh_ref, wo_ref, cwsh_ref, gn_ref, sink_ref, gna_ref,
             h_ref, pc_ref, q_ref, kv_ref, ga_ref, oc_ref, ya_ref, oa_ref, pr_ref, sp_ref, wt_out, cw_ref, wo_out,
             zbuf, kv_last, wt_ref, f32_buf, cw_land, wt_send, wt_recv, wt_local, cw_send, cw_recv, cw_local, *ag_scratch):
        step = pl.program_id(0)
        to_hbm = pltpu.make_async_copy(wt_ref, wt_out, wt_local.at[0])
        load_wo, *stages = _ag_stages(wo_ref, wo_out, *ag_scratch)

        @pl.when(step == 0)
        def _():
            load_wo()
            zbuf[0:8, :] = jnp.zeros((8, D_CONV), F32)
            kv_last[...] = jnp.zeros_like(kv_last)
            x_i, y_i, c = lax.axis_index("x"), lax.axis_index("y"), lax.axis_index("c")
            j = 2 * x_i + y_i
            p1 = (_xor(x_i, c), _xor(y_i, 1 - c), c)
            p2 = (_xor(x_i, 1 - c), _xor(y_i, c), c)
            j1 = 2 * p1[0] + p1[1]

            def cw_copy(k, src, chip, to):
                return pltpu.make_async_remote_copy(src_ref=src, dst_ref=cw_land.at[chip], send_sem=cw_send.at[k],
                                                    recv_sem=cw_recv.at[k], device_id=to, device_id_type=MESH)

            mine = pltpu.make_async_copy(cwsh_ref, cw_land.at[j], cw_local.at[0])
            mine.start()
            first = cw_copy(0, cwsh_ref, j, p1)
            first.start()
            second = [cw_copy(1, cwsh_ref, j, p2), cw_copy(2, cw_land.at[j1], j1, p2)]

            def cw_second_hop():
                first.wait_recv()
                for cp in second:
                    cp.start()

            _gather_resident(wtsh_ref, wt_ref, f32_buf, wt_send, wt_recv, wt_local, cw_second_hop)
            to_hbm.start()
            for cp in second:
                cp.wait_recv()
            for cp in [first] + second:
                cp.wait_send()
            mine.wait()
            for chip in range(N_CHIPS):
                for tap in range(cw_sh.shape[0]):
                    cw_ref[tap:tap + 1, chip * cw_cols:(chip + 1) * cw_cols] = cw_land[chip, tap]

        for at, stage in zip(stage_steps[:-1], stages[:-1]):
            pl.when(step == at)(stage)

        def attention(b):
            rows = pl.ds(b * BLK, BLK)
            kv_prev = kv_last if b == 0 else kv_ref.at[pl.ds((b - 1) * BLK, BLK), :]
            return _attn_forward(q_ref.at[rows, :], kv_ref.at[rows, :], kv_prev, ga_ref.at[rows, :], sink_ref, gna_ref,
                                 _band_geometry(step * blocks + b), ya_ref.at[rows, :], oa_ref.at[rows, :],
                                 pr_ref.at[rows, :], sp_ref.at[rows, :])

        xv = x_ref[...]
        h = (xv * _rstd(xv) * g_ref[...]).astype(BF16)
        h_ref[...] = h
        q_ref[...] = _nt(h, wt_ref[ROW_Q:ROW_KV, :])
        kv_ref[...] = _nt(h, wt_ref[ROW_KV:ROW_GA, :])
        ga_ref[...] = _nt(h, wt_ref[ROW_GA:D_IN_PROJ, :])
        attention_blocks = [attention(b) for b in range(blocks)]
        for lo in range(0, D_PC, PC_PIECE):
            pc_ref[:, lo:lo + PC_PIECE] = _nt(h, wt_ref[lo:lo + PC_PIECE, :])
            for stages_of_block in attention_blocks:
                next(stages_of_block, None)
        for stages_of_block in attention_blocks:
            for _ in stages_of_block:
                pass
        kv_last[...] = kv_ref[tile - BLK:tile, :]

        cb, _, _, _, _, _, conv = _conv_core(pc_ref, zbuf, cw_ref)
        yc = cb * conv
        silu, _ = _silu_and_grad(pc_ref[:, 3 * D_CONV:4 * D_CONV])
        oc_ref[...] = (yc * _rstd(yc) * gn_ref[...] * silu).astype(BF16)
        zbuf[0:8, :] = zbuf[tile:tile + 8, :]

        @pl.when(step == stage_steps[-1])
        def _():
            stages[-1]()
            to_hbm.wait()

    def row(width):
        return pl.BlockSpec((tile, width), lambda i: (i, 0))

    any_spec = pl.BlockSpec(memory_space=pl.ANY)
    dma = pltpu.SemaphoreType.DMA
    wt_shape = (N_CHIPS * wt_sh.shape[0], wt_sh.shape[1])
    cw_shape = (cw_sh.shape[0], N_CHIPS * cw_cols)
    return pl.pallas_call(
        body, name="fwd_in", grid=(n_tiles,),
        out_shape=(jax.ShapeDtypeStruct((seq, D_MODEL), BF16), jax.ShapeDtypeStruct((seq, D_PC), F32),
                   jax.ShapeDtypeStruct((seq, D_ATTN), F32), jax.ShapeDtypeStruct((seq, 2 * D_KV), F32),
                   jax.ShapeDtypeStruct((seq, D_ATTN), F32), jax.ShapeDtypeStruct((seq, D_CONV), BF16),
                   pltpu.HBM((seq, D_ATTN), F32), pltpu.HBM((seq, D_ATTN), BF16),
                   pltpu.HBM((seq, N_HEADS * BLK), BF16), pltpu.HBM((seq, 128), F32),
                   pltpu.HBM(wt_shape, BF16), jax.ShapeDtypeStruct(cw_shape, F32),
                   jax.ShapeDtypeStruct((N_CHIPS * wo_sh.shape[0], wo_sh.shape[1]), BF16)),
        in_specs=[row(D_MODEL), _resident((1, D_MODEL)), any_spec, any_spec, any_spec, _resident((1, D_CONV)),
                  pl.BlockSpec(memory_space=pltpu.SMEM), _resident((1, D_ATTN))],
        out_specs=(row(D_MODEL), row(D_PC), row(D_ATTN), row(2 * D_KV), row(D_ATTN), row(D_CONV), row(D_ATTN),
                   row(D_ATTN), row(N_HEADS * BLK), row(128), any_spec, pl.BlockSpec(cw_shape, lambda i: (0, 0)),
                   any_spec),
        scratch_shapes=[pltpu.VMEM((tile + 8, D_CONV), F32), pltpu.VMEM((BLK, 2 * D_KV), F32),
                        pltpu.VMEM(wt_shape, BF16), pltpu.VMEM((2, AG_CAST_ROWS, wt_sh.shape[1]), F32),
                        pltpu.VMEM((N_CHIPS,) + cw_sh.shape, F32),
                        dma((6,)), dma((6,)), dma((2,)), dma((3,)), dma((3,)), dma((1,))] + _ag_scratch(wo_sh.shape),
        compiler_params=_params(62, ("arbitrary",)),
    )(x, norm_in, wt_sh, wo_sh, cw_sh, norm_conv_out, sinks, norm_attn_out)


def _conv_core(pc_ref, zbuf, cw_ref):
    tile = pc_ref.shape[0]
    cb = pc_ref[:, 0:D_CONV]
    cc = pc_ref[:, D_CONV:2 * D_CONV]
    cu = pc_ref[:, 2 * D_CONV:3 * D_CONV]
    z = cc * cu
    zbuf[8:tile + 8, :] = z
    z1 = zbuf[7:tile + 7, :]
    z2 = zbuf[6:tile + 6, :]
    conv = cw_ref[0:1, :] * z2 + cw_ref[1:2, :] * z1 + cw_ref[2:3, :] * z
    return cb, cc, cu, z, z1, z2, conv


def _band_geometry(block_index):
    qi = lax.broadcasted_iota(jnp.int32, (BLK, BLK), 0)
    kp = lax.broadcasted_iota(jnp.int32, (BLK, BLK), 1)
    use_cur = kp <= qi
    dist = jnp.where(use_cur, qi - kp, qi - kp + BLK).astype(F32)
    valid = use_cur | (block_index > 0)
    return use_cur, dist, valid


def _block_diag(cur, prev, group):
    lane = lax.broadcasted_iota(jnp.int32, cur.shape, 1)

    def halves(t):
        other = pltpu.roll(t, 64, 1)
        lo, hi = (t, other) if group == 0 else (other, t)
        return jnp.where(lane < 64, lo, 0.0), jnp.where(lane >= 64, hi, 0.0)

    return jnp.concatenate(halves(cur) + halves(prev), axis=0).astype(BF16)


def _merge(s4, use_cur):
    return (jnp.where(use_cur, s4[:, 0:BLK], s4[:, 2 * BLK:3 * BLK]),
            jnp.where(use_cur, s4[:, BLK:2 * BLK], s4[:, 3 * BLK:4 * BLK]))


def _split(a, b, use_cur):
    return jnp.concatenate([jnp.where(use_cur, a, 0.0), jnp.where(use_cur, b, 0.0),
                            jnp.where(use_cur, 0.0, a), jnp.where(use_cur, 0.0, b)], axis=1)


def _softmax_head(s, head, sink, dist, valid):
    sc = jnp.where(valid, s - SLOPES[head] * dist, -jnp.inf)
    m = jnp.maximum(jnp.max(sc, axis=-1, keepdims=True), sink)
    p = jnp.exp(sc - m)
    es = jnp.exp(sink - m)
    inv = 1.0 / (jnp.sum(p, axis=-1, keepdims=True) + es)
    return p * inv, es * inv


def _attn_operands(q_ref, kvc_ref, kvp_ref):
    groups = range(N_HEADS // HEADS_PER_KV)
    kbd = [_block_diag(kvc_ref[:, 0:D_KV], kvp_ref[:, 0:D_KV], g) for g in groups]
    vbd = [_block_diag(kvc_ref[:, D_KV:2 * D_KV], kvp_ref[:, D_KV:2 * D_KV], g) for g in groups]
    qps = [(q_ref[:, j * 128:(j + 1) * 128] * SCALE).astype(BF16) for j in range(N_HEADS // 2)]
    return qps, kbd, vbd


def _attn_forward(q_ref, kvc_ref, kvp_ref, ga_ref, sink_ref, gn_ref, geometry, ya_ref, oa_ref, pr_ref, sp_ref):
    use_cur, dist, valid = geometry
    _, kbd, vbd = operands = _attn_operands(q_ref, kvc_ref, kvp_ref)
    yield
    scores = []
    for j, qp in enumerate(operands[0]):
        scores += _merge(_nt(qp, kbd[j // 4]), use_cur)
    yield
    sinks = [sink_ref[0, h] for h in range(N_HEADS)]
    scores = [jnp.where(valid, s - SLOPES[h] * dist, -jnp.inf) for h, s in enumerate(scores)]
    maxes = [jnp.maximum(jnp.max(s, axis=-1, keepdims=True), sinks[h]) for h, s in enumerate(scores)]
    yield
    exps = [jnp.exp(s - m) for s, m in zip(scores, maxes)]
    sink_exps = [jnp.exp(sinks[h] - m) for h, m in enumerate(maxes)]
    yield
    invs = [1.0 / (jnp.sum(e, axis=-1, keepdims=True) + se) for e, se in zip(exps, sink_exps)]
    probs = [e * inv for e, inv in zip(exps, invs)]
    pr_ref[...] = jnp.concatenate(probs, axis=1).astype(BF16)
    lane = lax.broadcasted_iota(jnp.int32, (BLK, 128), 1)
    sp_ref[...] = sum(jnp.where(lane == h, se * inv, 0.0) for h, (se, inv) in enumerate(zip(sink_exps, invs)))
    yield
    p4s = [_split(probs[2 * j], probs[2 * j + 1], use_cur).astype(BF16) for j in range(N_HEADS // 2)]
    ya = jnp.concatenate([_nn(p4, vbd[j // 4]) for j, p4 in enumerate(p4s)], axis=1)
    ya_ref[...] = ya
    yield
    silu, _ = _silu_and_grad(ga_ref[...])
    oa_ref[...] = (ya * _rstd(ya) * gn_ref[...] * silu).astype(BF16)


def _out_proj_loss(x, oc, oa, wo, norm_final, target, pc, conv_w, norm_conv_out):
    seq = x.shape[0]
    tile = TOK_TILE
    n_tiles = seq // tile

    def body(x_ref, oc_ref, oa_ref, wo_ref, gf_ref, t_ref, pc_ref, hcc_ref, hcu_ref, cw_ref, gn_ref,
             dx2_ref, doa_ref, gwo_ref, dpc_ref, small_ref, loss_acc, zbuf, dbuf):
        step = pl.program_id(0)

        def small_add(row, value):
            small_ref[row:row + 1, :] += jnp.sum(value, axis=0, keepdims=True)

        @pl.when(step == 0)
        def _():
            gwo_ref[...] = jnp.zeros_like(gwo_ref)
            small_ref[...] = jnp.zeros_like(small_ref)
            loss_acc[...] = jnp.zeros_like(loss_acc)
            dbuf[tile:tile + 8, :] = jnp.zeros((8, D_CONV), F32)

        oc, oa = oc_ref[...], oa_ref[...]
        x2 = x_ref[...] + _nn(oc, wo_ref[0:D_CONV, :]) + _nn(oa, wo_ref[D_CONV:D_MIX, :])
        r = _rstd(x2)
        xhat = x2 * r
        err = xhat * gf_ref[...] - t_ref[...]
        loss_acc[...] += jnp.sum(err * err, axis=0, keepdims=True) * (0.5 / D_MODEL)
        dy = err * (1.0 / D_MODEL)
        small_add(SMALL_NORM_FINAL, dy * xhat)
        dx2 = _rms_bwd(dy * gf_ref[...], xhat, r)
        dx2_ref[...] = dx2
        db = dx2.astype(BF16)
        do = _nt(db, wo_ref[0:D_CONV, :])
        doa_ref[...] = _nt(db, wo_ref[D_CONV:D_MIX, :])
        gwo_ref[0:D_CONV, :] += _tn(oc, db)
        gwo_ref[D_CONV:D_MIX, :] += _tn(oa, db)

        is_first_tile = step == n_tiles - 1
        zbuf[0:8, :] = jnp.where(is_first_tile, 0.0, hcc_ref[...] * hcu_ref[...])
        cb, cc, cu, z, z1, z2, conv = _conv_core(pc_ref, zbuf, cw_ref)
        silu, dsilu = _silu_and_grad(pc_ref[:, 3 * D_CONV:4 * D_CONV])
        yc = cb * conv
        rc = _rstd(yc)
        chat = yc * rc
        dn = do * silu
        dpc_ref[:, 3 * D_CONV:4 * D_CONV] = (do * (chat * gn_ref[...]) * dsilu).astype(BF16)
        small_add(SMALL_NORM_CONV, dn * chat)
        dyc = _rms_bwd(dn * gn_ref[...], chat, rc)
        dpc_ref[:, 0:D_CONV] = (dyc * conv).astype(BF16)
        dconv = dyc * cb
        small_add(SMALL_CONV_W, dconv * z2)
        small_add(SMALL_CONV_W + 1, dconv * z1)
        small_add(SMALL_CONV_W + 2, dconv * z)
        dbuf[0:tile, :] = dconv
        dz = cw_ref[2:3, :] * dconv + cw_ref[1:2, :] * dbuf[1:tile + 1, :] + cw_ref[0:1, :] * dbuf[2:tile + 2, :]
        dpc_ref[:, D_CONV:2 * D_CONV] = (dz * cu).astype(BF16)
        dpc_ref[:, 2 * D_CONV:3 * D_CONV] = (dz * cc).astype(BF16)
        dbuf[tile:tile + 8, :] = dbuf[0:8, :]

        @pl.when(step == n_tiles - 1)
        def _():
            lane = lax.broadcasted_iota(jnp.int32, (1, D_MODEL), 1)
            small_ref[SMALL_MISC:SMALL_MISC + 1, :] = jnp.where(lane == SMALL_LOSS_LANE, jnp.sum(loss_acc[...]), 0.0)

    def rev(i):
        return n_tiles - 1 - i

    def row(width):
        return pl.BlockSpec((tile, width), lambda i: (rev(i), 0))

    def halo(col_block):
        return pl.BlockSpec((8, D_CONV), lambda i: (jnp.maximum(rev(i) * (tile // 8) - 1, 0), col_block))

    def const(shape):
        return pl.BlockSpec(shape, lambda i: (0, 0))

    return pl.pallas_call(
        body, name="out_proj_loss", grid=(n_tiles,),
        out_shape=(jax.ShapeDtypeStruct((seq, D_MODEL), F32), jax.ShapeDtypeStruct((seq, D_ATTN), F32),
                   jax.ShapeDtypeStruct((D_MIX, D_MODEL), F32), jax.ShapeDtypeStruct((seq, D_PC), BF16),
                   jax.ShapeDtypeStruct((SMALL_ROWS, D_MODEL), F32)),
        in_specs=[row(D_MODEL), row(D_CONV), row(D_ATTN), _resident(wo.shape), _resident((1, D_MODEL)), row(D_MODEL),
                  row(D_PC), halo(1), halo(2), _resident(conv_w.shape), _resident((1, D_CONV))],
        out_specs=(row(D_MODEL), row(D_ATTN), const((D_MIX, D_MODEL)), row(D_PC), const((SMALL_ROWS, D_MODEL))),
        scratch_shapes=[pltpu.VMEM((1, D_MODEL), F32), pltpu.VMEM((tile + 8, D_CONV), F32),
                        pltpu.VMEM((tile + 8, D_CONV), F32)],
        compiler_params=_params(62, ("arbitrary",)),
    )(x, oc, oa, wo, norm_final, target, pc, pc, pc, conv_w, norm_conv_out)


def _attn_bwd(q, kv, ga, ya, doa, probs, sink_probs, norm_attn_out, gwo, small):
    seq = q.shape[0]
    per_step = 2
    n_steps = seq // (per_step * BLK)
    stage_steps = (0, 1, n_steps // 4, (5 * n_steps) // 8, n_steps - 1, n_steps - 1)

    def body(q_ref, kvc_ref, kvp_ref, ga_ref, ya_ref, doa_ref, pr_ref, sp_ref, gn_ref, gwo_ref, small_ref,
             dqg_ref, small_out, gwo_sh, gna_ref, gs_ref, carry, dya_buf, *rs_scratch):
        step = pl.program_id(0)
        rs_stages = _rs_wout_stages(gwo_ref, gwo_sh, *rs_scratch)
        for at, stage in zip(stage_steps[:-1], rs_stages[:-1]):
            pl.when(step == at)(stage)

        @pl.when(step == 0)
        def _():
            gna_ref[...] = jnp.zeros_like(gna_ref)
            gs_ref[...] = jnp.zeros_like(gs_ref)
            carry[...] = jnp.zeros_like(carry)

        lane = lax.broadcasted_iota(jnp.int32, (BLK, 128), 1)

        def fold(bd):
            return (jnp.where(lane < 64, bd[0:BLK], 0.0) + jnp.where(lane >= 64, bd[BLK:2 * BLK], 0.0),
                    jnp.where(lane < 64, bd[2 * BLK:3 * BLK], 0.0) + jnp.where(lane >= 64, bd[3 * BLK:4 * BLK], 0.0))

        def one_block(b):
            rows = slice(b * BLK, (b + 1) * BLK)
            kv_prev = kvp_ref if b == 0 else kvc_ref.at[(b - 1) * BLK:b * BLK, :]
            ya = ya_ref[rows, :]
            r = _rstd(ya)
            xhat = ya * r
            silu, dsilu = _silu_and_grad(ga_ref[rows, :])
            do = doa_ref[rows, :]
            dn = do * silu
            dqg_ref[rows, D_ATTN:2 * D_ATTN] = (do * (xhat * gn_ref[...]) * dsilu).astype(BF16)
            gna_ref[...] += jnp.sum(dn * xhat, axis=0, keepdims=True)
            dya_buf[rows, :] = _rms_bwd(dn * gn_ref[...], xhat, r).astype(BF16)

            use_cur = _band_geometry(per_step * (n_steps - 1 - step) + b)[0]
            pairs = range(N_HEADS // 2)
            qps, kbd, vbd = _attn_operands(q_ref.at[rows, :], kvc_ref.at[rows, :], kv_prev)
            probs = [pr_ref[rows, h * BLK:(h + 1) * BLK].astype(F32) for h in range(N_HEADS)]
            dyps = [dya_buf[rows, j * 128:(j + 1) * 128] for j in pairs]
            dps = []
            for j in pairs:
                dps += _merge(_nt(dyps[j], vbd[j // 4]), use_cur)
            deltas = [jnp.sum(p * dp, axis=-1, keepdims=True) for p, dp in zip(probs, dps)]
            dss = [p * (dp - delta) for p, dp, delta in zip(probs, dps, deltas)]
            delta_lanes = sum(jnp.where(lane == h, deltas[h], 0.0) for h in range(N_HEADS))
            gs_ref[...] -= jnp.sum(sp_ref[rows, :] * delta_lanes, axis=0, keepdims=True)
            ds4s = [_split(dss[2 * j], dss[2 * j + 1], use_cur).astype(BF16) for j in pairs]
            p4s = [_split(probs[2 * j], probs[2 * j + 1], use_cur).astype(BF16) for j in pairs]
            dqg_ref[rows, 0:D_ATTN] = jnp.concatenate([_nn(ds4s[j], kbd[j // 4]) * SCALE for j in pairs], axis=1).astype(BF16)
            sums = []
            for group in range(N_HEADS // HEADS_PER_KV):
                acc = [jnp.zeros((BLK, 128), F32) for _ in range(4)]
                for j in range(group * 4, group * 4 + 4):
                    for slot, part in enumerate(fold(_tn(ds4s[j], qps[j])) + fold(_tn(p4s[j], dyps[j]))):
                        acc[slot] = acc[slot] + part
                sums.append([a + pltpu.roll(a, 64, 1) for a in acc])
            dk_cur, dk_prev, dv_cur, dv_prev = (jnp.where(lane < 64, lo, hi) for lo, hi in zip(sums[0], sums[1]))
            dqg_ref[rows, 2 * D_ATTN:2 * D_ATTN + 2 * D_KV] = (jnp.concatenate([dk_cur, dv_cur], axis=1) + carry[...]).astype(BF16)
            carry[...] = jnp.concatenate([dk_prev, dv_prev], axis=1)

        for b in reversed(range(per_step)):
            one_block(b)

        @pl.when(step == n_steps - 1)
        def _():
            small_out[...] = small_ref[...]
            small_out[SMALL_NORM_ATTN:SMALL_NORM_ATTN + 1, :] = gna_ref[...]
            small_out[SMALL_MISC:SMALL_MISC + 1, 0:128] = small_ref[SMALL_MISC:SMALL_MISC + 1, 0:128] + gs_ref[...]

        pl.when(step == stage_steps[-1])(rs_stages[-1])

    def rows(width):
        return pl.BlockSpec((per_step * BLK, width), lambda i: (n_steps - 1 - i, 0))

    kv_prev = pl.BlockSpec((BLK, 2 * D_KV), lambda i: (jnp.maximum(per_step * (n_steps - 1 - i) - 1, 0), 0))
    any_spec = pl.BlockSpec(memory_space=pl.ANY)
    return pl.pallas_call(
        body, name="attn_bwd", grid=(n_steps,),
        out_shape=(jax.ShapeDtypeStruct((seq, D_QG), BF16), pltpu.HBM(small.shape, F32),
                   pltpu.HBM((gwo.shape[0] // N_CHIPS, gwo.shape[1]), F32)),
        in_specs=[rows(D_ATTN), rows(2 * D_KV), kv_prev, rows(D_ATTN), rows(D_ATTN), rows(D_ATTN),
                  rows(N_HEADS * BLK), rows(128), _resident((1, D_ATTN)), any_spec, _resident(small.shape)],
        out_specs=(rows(D_QG), pl.BlockSpec(small.shape, lambda i: (0, 0)), any_spec),
        scratch_shapes=[pltpu.VMEM((1, D_ATTN), F32), pltpu.VMEM((1, 128), F32),
                        pltpu.VMEM((BLK, 2 * D_KV), F32), pltpu.VMEM((per_step * BLK, D_ATTN), BF16)] + _rs_wout_scratch(gwo.shape),
        compiler_params=_params(44, ("arbitrary",)),
    )(q, kv, kv, ga, ya, doa, probs, sink_probs, norm_attn_out, gwo, small)


GBLK = 256
GSUB = 64
PAIR_RING = 4
LAG_PAIR, LAG_HOP1, LAG_HOP2 = 1, 7, 14


def _bwd_in(dpc, dqg, h, wt, x, norm_in, dx2, small):
    seq = x.shape[0]
    n_blk = D_IN_PROJ // GBLK
    per_chip = n_blk // N_CHIPS
    n_slots = (n_blk + 1) // 2
    n_sub = GBLK // GSUB
    chip_rows = D_IN_PROJ // N_CHIPS
    tile = TOK_TILE
    n_tiles = seq // tile
    n_steps = n_blk + max(n_tiles, LAG_HOP2)
    chunk = min(seq, 512)
    blk_q, blk_kv, blk_ga = ROW_Q // GBLK, ROW_KV // GBLK, ROW_GA // GBLK

    def block_of(i):
        k = i % N_CHIPS
        robin = per_chip * ((k % 2) * 2 + k // 2) + i // N_CHIPS
        if isinstance(i, int):
            return robin if i < per_chip * N_CHIPS else i
        return jnp.where(i < per_chip * N_CHIPS, robin, i)

    def owner_of(i):
        return (i // N_CHIPS) % 2

    def slot_of(i):
        return (i // (2 * N_CHIPS)) * N_CHIPS + i % N_CHIPS

    def body(dpc_ref, dqg_ref, wt_ref, h_ref, x_lo, x_hi, g_ref, dx2_lo, dx2_hi, small_ref, gx_out, small_sum, gwt_sh,
             dh_acc, gni, keep, pbuf, xbuf, land, land2, small_land,
             pair_send, pair_recv, h1_send, h1_recv, h2_send, h2_recv, sw_send, sw_recv, sm_send, sm_recv, out_sem,
             gx_buf, gx_sem):
        step = pl.program_id(0)
        x_i, y_i, c = lax.axis_index("x"), lax.axis_index("y"), lax.axis_index("c")
        me = 4 * x_i + 2 * y_i + c
        j = 2 * x_i + y_i
        pa = (_xor(x_i, 1 - c), _xor(y_i, c), c)
        pb = (_xor(x_i, c), _xor(y_i, 1 - c), c)
        sib = (x_i, y_i, 1 - c)
        ja = 2 * pa[0] + pa[1]
        jb = 2 * pb[0] + pb[1]
        jd = 3 - j

        def remote(src, dst, send, recv, to):
            return pltpu.make_async_remote_copy(src_ref=src, dst_ref=dst, send_sem=send, recv_sem=recv,
                                                device_id=to, device_id_type=MESH)

        def piece(ref, slot, u, n):
            return ref.at[slot, pl.ds(u * GSUB, n * GSUB), :]

        def chip_rows_at(ref, local, n):
            return ref.at[pl.ds(pl.multiple_of(local, GSUB), n * GSUB), :]

        def pair_copy(i):
            slot = slot_of(i)
            return remote(pbuf.at[i % PAIR_RING], land.at[slot], pair_send.at[slot], pair_recv.at[slot], sib)

        def h1_copy(slot, u, n):
            k = slot * n_sub + u
            return remote(piece(xbuf, slot, u, n), piece(xbuf, slot, u, n), h1_send.at[k], h1_recv.at[k], pa)

        def h2_copy(slot, u, n, local):
            k = slot * n_sub + u
            return remote(piece(xbuf, slot, u, n), chip_rows_at(land2, local, n), h2_send.at[k], h2_recv.at[k], pb)

        def sw_copy(slot, u, n, local):
            k = slot * n_sub + u
            return remote(piece(keep, slot, u, n), chip_rows_at(gwt_sh, local, n), sw_send.at[k], sw_recv.at[k], sib)

        def own_copy(slot, u, n, local):
            return pltpu.make_async_copy(piece(keep, slot, u, n), chip_rows_at(gwt_sh, local, n), out_sem.at[slot * n_sub + u])

        def owned(i):
            return (i >= 0) & (i < n_blk) & (owner_of(i) == c)

        def chip_of(blk, u):
            row = blk * GBLK + u * GSUB
            chip = row // chip_rows
            return chip, row - chip * chip_rows

        def pieces(blk):
            first, local = chip_of(blk, 0)
            whole = first == chip_of(blk, n_sub - 1)[0]
            if isinstance(blk, int):
                return [(True, 0, n_sub, first, local)] if whole else [(True, u, 1) + chip_of(blk, u) for u in range(n_sub)]
            return [(whole, 0, n_sub, first, local)] + [(jnp.logical_not(whole), u, 1) + chip_of(blk, u) for u in range(n_sub)]

        @pl.when(step == 0)
        def _():
            dh_acc[...] = jnp.zeros_like(dh_acc)
            gni[...] = jnp.zeros_like(gni)

        @pl.when(step < n_blk)
        def _():
            from_pc = block_of(step) < blk_q
            block = _tn(jnp.where(from_pc, dpc_ref[...], dqg_ref[...]), h_ref[...])
            for t in range(0, seq, chunk):
                d = jnp.where(from_pc, dpc_ref[t:t + chunk, :], dqg_ref[t:t + chunk, :])
                dh_acc[t:t + chunk, :] += _nn(d, wt_ref[...])

            @pl.when(owner_of(step) == c)
            def _():
                keep[slot_of(step)] = block

            @pl.when(owner_of(step) != c)
            def _():
                @pl.when(step >= 2 * PAIR_RING)
                def _():
                    pair_copy(step - 2 * PAIR_RING).wait_send()
                pbuf[step % PAIR_RING] = block.astype(BF16)
                pair_copy(step).start()

        i1 = step - LAG_PAIR

        @pl.when(owned(i1))
        def _():
            slot = slot_of(i1)
            pair_copy(i1).wait_recv()
            _accumulate(keep.at[slot], land.at[slot])
            for cond, u, n, chip, _ in pieces(block_of(i1)):
                @pl.when(cond & ((chip == ja) | (chip == jd)))
                def _(u=u, n=n):
                    _cast_rows(piece(keep, slot, u, n), piece(xbuf, slot, u, n))
                    h1_copy(slot, u, n).start()

        i2 = step - LAG_HOP1

        @pl.when(owned(i2))
        def _():
            slot = slot_of(i2)
            for cond, u, n, chip, local in pieces(block_of(i2)):
                @pl.when(cond & ((chip == j) | (chip == jb)))
                def _(u=u, n=n, chip=chip, local=local):
                    h1_copy(slot, u, n).wait_recv()
                    _accumulate(piece(keep, slot, u, n), piece(xbuf, slot, u, n))

                    @pl.when(chip == jb)
                    def _():
                        _cast_rows(piece(keep, slot, u, n), piece(xbuf, slot, u, n))
                        h2_copy(slot, u, n, local).start()

                @pl.when(cond & ((chip == ja) | (chip == jd)))
                def _(u=u, n=n):
                    h1_copy(slot, u, n).wait_send()

        i3 = step - LAG_HOP2

        @pl.when(owned(i3))
        def _():
            slot = slot_of(i3)
            for cond, u, n, chip, local in pieces(block_of(i3)):
                @pl.when(cond & (chip == j))
                def _(u=u, n=n, local=local):
                    h2_copy(slot, u, n, local).wait_recv()
                    _accumulate(piece(keep, slot, u, n), chip_rows_at(land2, local, n))
                    own_copy(slot, u, n, local).start()
                    sw_copy(slot, u, n, local).start()

                @pl.when(cond & (chip == jb))
                def _(u=u, n=n, local=local):
                    h2_copy(slot, u, n, local).wait_send()

        e = step - n_blk

        def gx_copies(t):
            half = tile // 2
            return [pltpu.make_async_copy(gx_buf.at[t % 2, pl.ds(k * half, half), :],
                                          gx_out.at[pl.ds(pl.multiple_of(t * tile + k * half, half), half), :],
                                          gx_sem.at[(t % 2) * 2 + k]) for k in range(2)]

        @pl.when((e >= 0) & (e < n_tiles))
        def _():
            @pl.when(e >= 2)
            def _():
                for cp in gx_copies(e - 2):
                    cp.wait()
            dh = dh_acc[pl.ds(pl.multiple_of(e * tile, tile), tile), :]
            xv = jnp.concatenate([x_lo[...], x_hi[...]], axis=1)
            r = _rstd(xv)
            xhat = xv * r
            gni[...] += jnp.sum(dh * xhat, axis=0, keepdims=True)
            gx_buf[e % 2] = _rms_bwd(dh * g_ref[...], xhat, r) + jnp.concatenate([dx2_lo[...], dx2_hi[...]], axis=1)
            for cp in gx_copies(e):
                cp.start()

        others = [(dx, dy, dc) for dx in (0, 1) for dy in (0, 1) for dc in (0, 1)][1:]
        sends = [remote(small_land.at[me], small_land.at[me], sm_send.at[k], sm_recv.at[k],
                        (_xor(x_i, dx), _xor(y_i, dy), _xor(c, dc))) for k, (dx, dy, dc) in enumerate(others)]

        @pl.when(step == min(n_blk + n_tiles, n_steps - 1))
        def _():
            small_land[me] = small_ref[...]
            small_land[me, SMALL_NORM_IN:SMALL_NORM_IN + 1, :] = gni[...]
            for cp in sends:
                cp.start()

        @pl.when(step == n_steps - 1)
        def _():
            for t in range(max(n_tiles - 2, 0), n_tiles):
                for cp in gx_copies(t):
                    cp.wait()
            for i in range(n_blk):
                if i + 2 * PAIR_RING >= n_blk:
                    @pl.when(owner_of(i) != c)
                    def _(i=i):
                        pair_copy(i).wait_send()
                for _, u, n, chip, local in pieces(block_of(i)):
                    @pl.when((j == chip) & (c == owner_of(i)))
                    def _(i=i, u=u, n=n, local=local):
                        sw_copy(slot_of(i), u, n, local).wait_send()
                        own_copy(slot_of(i), u, n, local).wait()

                    @pl.when((j == chip) & (c != owner_of(i)))
                    def _(i=i, u=u, n=n, local=local):
                        sw_copy(slot_of(i), u, n, local).wait_recv()
            for cp in sends:
                cp.wait_recv()
            total = small_land[0]
            for dev in range(1, 8):
                total = total + small_land[dev]
            small_sum[...] = total
            for cp in sends:
                cp.wait_send()

    def blk_at(i):
        return block_of(jnp.clip(i, 0, n_blk - 1))

    last_pc_step = max(i for i in range(n_blk) if block_of(i) < blk_q)

    def next_block(i, in_pc):
        i = jnp.clip(i, 0, n_blk - 1)
        step = jnp.full_like(i, last_pc_step if in_pc else n_blk - 1)
        for ahead in reversed(range(N_CHIPS)):
            cand = jnp.minimum(i + ahead, n_blk - 1)
            step = jnp.where((block_of(cand) < blk_q) == in_pc, cand, step)
        return block_of(step)

    def dqg_block(i):
        b = next_block(i, False)
        q_blk = jnp.clip(b - blk_q, 0, blk_kv - blk_q - 1)
        ga_blk = (D_ATTN // GBLK) + jnp.clip(b - blk_ga, 0, n_blk - blk_ga - 1)
        return jnp.where(b < blk_kv, q_blk, jnp.where(b == blk_kv, 2 * D_ATTN // GBLK, ga_blk))

    def tok_half(k):
        return pl.BlockSpec((tile, D_MODEL // 2), lambda i: (jnp.clip(i - n_blk, 0, n_tiles - 1), k))

    n_piece = n_slots * n_sub
    dma = pltpu.SemaphoreType.DMA
    return pl.pallas_call(
        body, name="bwd_in", grid=(n_steps,),
        out_shape=(jax.ShapeDtypeStruct((seq, D_MODEL), F32), jax.ShapeDtypeStruct(small.shape, F32),
                   jax.ShapeDtypeStruct((chip_rows, D_MODEL), F32)),
        in_specs=[pl.BlockSpec((seq, GBLK), lambda i: (0, next_block(i, True))),
                  pl.BlockSpec((seq, GBLK), lambda i: (0, dqg_block(i))),
                  pl.BlockSpec((GBLK, D_MODEL), lambda i: (blk_at(i), 0)),
                  _resident(h.shape),
                  tok_half(0), tok_half(1), _resident((1, D_MODEL)), tok_half(0), tok_half(1),
                  _resident(small.shape)],
        out_specs=(pl.BlockSpec(memory_space=pl.ANY), pl.BlockSpec(small.shape, lambda i: (0, 0)),
                   pl.BlockSpec(memory_space=pl.ANY)),
        scratch_shapes=[pltpu.VMEM((seq, D_MODEL), F32), pltpu.VMEM((1, D_MODEL), F32),
                        pltpu.VMEM((n_slots, GBLK, D_MODEL), F32), pltpu.VMEM((PAIR_RING, GBLK, D_MODEL), BF16),
                        pltpu.VMEM((n_slots, GBLK, D_MODEL), BF16), pltpu.VMEM((n_slots, GBLK, D_MODEL), BF16),
                        pltpu.VMEM((chip_rows, D_MODEL), BF16), pltpu.VMEM((8,) + small.shape, F32),
                        dma((n_slots,)), dma((n_slots,)), dma((n_piece,)), dma((n_piece,)), dma((n_piece,)),
                        dma((n_piece,)), dma((n_piece,)), dma((n_piece,)), dma((7,)), dma((7,)), dma((n_piece,)),
                        pltpu.VMEM((2, tile, D_MODEL), F32), dma((4,))],
        compiler_params=_params(62, ("arbitrary",)),
    )(dpc, dqg, wt, h, x, x, norm_in, dx2, dx2, small)


def _accumulate(dst_ref, src_ref, rows=16):
    def step(i, carry):
        sl = pl.ds(pl.multiple_of(i * rows, rows), rows)
        dst_ref[sl, :] = dst_ref[sl, :] + src_ref[sl, :].astype(F32)
        return carry
    lax.fori_loop(0, dst_ref.shape[0] // rows, step, 0)


def _rs_wout_scratch(gwo_shape):
    o_half, width = gwo_shape[0] // N_CHIPS // 2, gwo_shape[1]
    return [pltpu.VMEM((4, o_half, width), F32), pltpu.VMEM((4, o_half, width), BF16),
            pltpu.VMEM((4, o_half, width), BF16), pltpu.VMEM((2, o_half, width), BF16),
            pltpu.VMEM((o_half, width), BF16),
            pltpu.SemaphoreType.DMA((8,)), pltpu.SemaphoreType.DMA((8,)), pltpu.SemaphoreType.DMA((4,))]


def _rs_wout_stages(gwo_ref, gwo_sh, acc_o, sb_o, r1o, r2o, r3o, send_sems, recv_sems, local_sems):
    o_rows = gwo_ref.shape[0] // N_CHIPS
    o_half = o_rows // 2
    x, y, c = lax.axis_index("x"), lax.axis_index("y"), lax.axis_index("c")
    j = 2 * x + y
    pa = (_xor(x, 1 - c), _xor(y, c), c)
    pb = (_xor(x, c), _xor(y, 1 - c), c)
    sib = (x, y, 1 - c)
    ja = 2 * pa[0] + pa[1]
    jb = 2 * pb[0] + pb[1]
    jd = 3 - j
    order = (ja, jd, jb, j)
    sib_order = (jb, jd, ja, j)

    def rcopy(k, src, dst, to):
        return pltpu.make_async_remote_copy(src_ref=src, dst_ref=dst, send_sem=send_sems.at[k],
                                            recv_sem=recv_sems.at[k], device_id=to, device_id_type=MESH)

    def o_rows_of(chip, half):
        return gwo_ref.at[pl.ds(pl.multiple_of(chip * o_rows + half * o_half, 8), o_half), :]

    def loads(chips, half):
        return [pltpu.make_async_copy(o_rows_of(chip, half), acc_o.at[s], local_sems.at[s]) for s, chip in enumerate(chips)]

    def pair_send(s):
        return rcopy(s, sb_o.at[s], r1o.at[s], sib)

    def out_half(half):
        return gwo_sh.at[pl.ds(pl.multiple_of(half * o_half, 8), o_half), :]

    hop1 = [rcopy(4 + s, sb_o.at[s], r2o.at[s], pa) for s in range(2)]
    hop2 = rcopy(6, sb_o.at[2], r3o, pb)
    swap = rcopy(7, acc_o.at[3], out_half(c), sib)
    mine = pltpu.make_async_copy(acc_o.at[3], out_half(c), local_sems.at[0])

    def resend(s, copy):
        pair_send(s).wait_send()
        _cast_rows(acc_o.at[s], sb_o.at[s])
        copy.start()

    def stage_load():
        for cp in loads(sib_order, 1 - c):
            cp.start()

    def stage_pair():
        for s, (cp, mine_in) in enumerate(zip(loads(sib_order, 1 - c), loads(order, c))):
            cp.wait()
            _cast_rows(acc_o.at[s], sb_o.at[s])
            pair_send(s).start()
            mine_in.start()

    def stage_hop1():
        for s, cp in enumerate(loads(order, c)):
            cp.wait()
            pair_send(s).wait_recv()
            _accumulate(acc_o.at[s], r1o.at[s])
            if s < 2:
                resend(s, hop1[s])

    def stage_hop2():
        hop1[1].wait_recv()
        _accumulate(acc_o.at[2], r2o.at[1])
        resend(2, hop2)
        hop1[0].wait_recv()
        _accumulate(acc_o.at[3], r2o.at[0])

    def stage_final():
        hop2.wait_recv()
        _accumulate(acc_o.at[3], r3o)
        mine.start()
        swap.start()

    def stage_drain():
        rcopy(7, acc_o.at[3], out_half(1 - c), sib).wait_recv()
        for cp in [pair_send(3)] + hop1 + [hop2, swap]:
            cp.wait_send()
        mine.wait()

    return stage_load, stage_pair, stage_hop1, stage_hop2, stage_final, stage_drain


def _adamw_big(groups, passed):
    arrays = [a for group in groups for a in group[:4]]
    steps = [group[0].shape[0] // group[4] for group in groups]
    first = [sum(steps[:k]) for k in range(len(steps) + 1)]
    n = len(arrays)

    def body(*refs):
        ins, passed_in, outs, passed_out = refs[:n], refs[n], refs[n + 1:2 * n + 1], refs[2 * n + 1]
        i = pl.program_id(0)
        for k in range(len(groups)):
            @pl.when((i >= first[k]) & (i < first[k + 1]))
            def _(k=k):
                w_ref, g_ref, m_ref, v_ref = ins[4 * k:4 * k + 4]
                _adamw_update(w_ref, g_ref[...], m_ref, v_ref, *outs[4 * k:4 * k + 4])

        @pl.when(i < steps[0])
        def _():
            passed_out[...] = passed_in[...]

    def spec(k, rows, width):
        return pl.BlockSpec((rows, width), lambda i: (jnp.clip(i - first[k], 0, steps[k] - 1), 0))

    specs = [spec(k, group[4], group[0].shape[1]) for k, group in enumerate(groups) for _ in range(4)]
    specs.append(spec(0, passed.shape[0] // steps[0], passed.shape[1]))
    out = pl.pallas_call(
        body, name="adamw_big", grid=(first[-1],),
        out_shape=tuple(jax.ShapeDtypeStruct(a.shape, a.dtype) for a in arrays + [passed]),
        in_specs=specs, out_specs=tuple(specs),
        compiler_params=_params(40, ("arbitrary",)),
    )(*arrays, passed)
    return [tuple(out[4 * k:4 * k + 4]) for k in range(len(groups))], out[-1]


def _adamw_update(w_ref, gv, m_ref, v_ref, go_ref, d_ref, nm_ref, nv_ref, at=...):
    go_ref[at] = gv
    nm = ADAM_B1 * m_ref[at] + (1.0 - ADAM_B1) * gv
    nv = ADAM_B2 * v_ref[at] + (1.0 - ADAM_B2) * (gv * gv)
    m_hat = nm / (1.0 - ADAM_B1 ** ADAM_STEP)
    v_hat = nv / (1.0 - ADAM_B2 ** ADAM_STEP)
    d_ref[at] = -ADAM_LR * (m_hat / (jnp.sqrt(v_hat) + ADAM_EPS) + ADAM_WD * w_ref[at])
    nm_ref[at] = nm
    nv_ref[at] = nv


def _adamw_small(chip, small_sum, weights, grads_of, ms, vs):
    n = len(weights)

    def body(chip_ref, small_ref, *refs):
        ins, outs, loss_ref = refs[:3 * n], refs[3 * n:-1], refs[-1]
        for k in range(n):
            w_ref, m_ref, v_ref = ins[3 * k:3 * k + 3]
            for at, gv in grads_of[k](small_ref, chip_ref):
                _adamw_update(w_ref, gv, m_ref, v_ref, *outs[4 * k:4 * k + 4], at=at)
        loss_ref[...] = small_ref[SMALL_MISC:SMALL_MISC + 1, SMALL_LOSS_LANE:SMALL_LOSS_LANE + 1]

    flat = [a for group in zip(weights, ms, vs) for a in group]
    vmem = pl.BlockSpec(memory_space=pltpu.VMEM)
    out = pl.pallas_call(
        body, name="adamw_small",
        out_shape=tuple(jax.ShapeDtypeStruct(w.shape, F32) for w in weights for _ in range(4))
        + (jax.ShapeDtypeStruct((1, 1), F32),),
        in_specs=[pl.BlockSpec(memory_space=pltpu.SMEM)] + [vmem] * (1 + 3 * n), out_specs=(vmem,) * (4 * n + 1),
    )(chip, small_sum, *flat)
    return [tuple(out[4 * k:4 * k + 4]) for k in range(n)], out[-1][0, 0]


def kernel(x, norm_in, w_in, conv_w, attn_sinks, norm_conv_out, norm_attn_out, w_out, norm_final, loss_target, m_norm_in, m_w_in, m_conv_w, m_attn_sinks, m_norm_conv_out, m_norm_attn_out, m_w_out, m_norm_final, v_norm_in, v_w_in, v_conv_w, v_attn_sinks, v_norm_conv_out, v_norm_attn_out, v_w_out, v_norm_final):
    chip = 2 * lax.axis_index("x") + lax.axis_index("y")
    xs, target = x[0], loss_target[0]
    norm_final2 = norm_final.reshape(1, D_MODEL)
    w_in_t, m_w_in_t, v_w_in_t = w_in[0].T, m_w_in[0].T, v_w_in[0].T

    def from_hbm(*arrays):
        return tuple(pltpu.with_memory_space_constraint(a, pltpu.HBM) for a in arrays)

    h, pc, q, kv, ga, oc, ya, oa, probs, sink_probs, wt, cw, wo = _fwd_in(
        xs, norm_in, w_in_t, w_out[0], conv_w.transpose(1, 0, 2), norm_conv_out, attn_sinks, norm_attn_out)
    dx2, doa, gwo, dpc, small = _out_proj_loss(xs, oc, oa, wo, norm_final2, target, pc, cw, norm_conv_out)
    q, kv, ga, ya, doa, probs, gwo, small = from_hbm(q, kv, ga, ya, doa, probs, gwo, small)
    dqg, small, gwo_sh = _attn_bwd(q, kv, ga, ya, doa, probs, sink_probs, norm_attn_out, gwo, small)
    grad_x, small_sum, gwt_sh = _bwd_in(dpc, dqg, h, wt, xs, norm_in, dx2, small)

    (up_w_in, up_w_out), grad_x = _adamw_big(
        [from_hbm(w_in_t, gwt_sh, m_w_in_t, v_w_in_t) + (200,), from_hbm(w_out[0], gwo_sh, m_w_out[0], v_w_out[0]) + (128,)],
        *from_hbm(grad_x))
    up_w_in = tuple(o.T[None] for o in up_w_in)
    up_w_out = tuple(o[None] for o in up_w_out)

    def row_of(r):
        return lambda small_ref, chip_ref: [(..., small_ref[r:r + 1, :])]

    def sink_lanes(small_ref, chip_ref):
        return [(..., small_ref[SMALL_MISC:SMALL_MISC + 1, 0:N_HEADS])]

    def conv_taps(small_ref, chip_ref):
        width = D_CONV // N_CHIPS
        cols = pl.ds(pl.multiple_of(chip_ref[0] * width, width), width)
        return [(k, small_ref[pl.ds(SMALL_CONV_W + k, 1), cols]) for k in range(conv_w.shape[1])]

    def small_view(a):
        return a.transpose(1, 0, 2) if a.ndim == 3 else a.reshape(-1, a.shape[-1])

    small_w = (norm_in, conv_w, attn_sinks, norm_conv_out, norm_attn_out, norm_final)
    small_m = (m_norm_in, m_conv_w, m_attn_sinks, m_norm_conv_out, m_norm_attn_out, m_norm_final)
    small_v = (v_norm_in, v_conv_w, v_attn_sinks, v_norm_conv_out, v_norm_attn_out, v_norm_final)
    small_g = (row_of(SMALL_NORM_IN), conv_taps, sink_lanes, row_of(SMALL_NORM_CONV), row_of(SMALL_NORM_ATTN),
               row_of(SMALL_NORM_FINAL))
    w_views, m_views, v_views = (tuple(small_view(a) for a in group) for group in (small_w, small_m, small_v))
    up_small, loss = _adamw_small(chip.astype(jnp.int32).reshape(1), small_sum, w_views, small_g, m_views, v_views)
    up_small = [tuple(o.transpose(1, 0, 2) if w.ndim == 3 else o.reshape(w.shape) for o in up)
                for up, w in zip(up_small, small_w)]
    up_norm_in, up_conv_w, up_sinks, up_norm_conv, up_norm_attn, up_norm_final = up_small
    updates = (up_norm_in, up_w_in, up_conv_w, up_sinks, up_norm_conv, up_norm_attn, up_w_out, up_norm_final)
    grads_out, deltas, new_m, new_v = zip(*updates)
    return (loss, grad_x[None], *grads_out, *deltas, *new_m, *new_v)
```

```python
import jax
import jax.numpy as jnp
from jax import lax
from jax.experimental import pallas as pl
from jax.experimental.pallas import tpu as pltpu

F32 = jnp.float32
BF16 = jnp.bfloat16
MESH = pl.DeviceIdType.MESH

D_MODEL = 1024
D_CONV = 1024
D_ATTN = 1024
D_KV = 128
D_QG = 2 * D_ATTN + 2 * D_KV
D_MIX = D_CONV + D_ATTN
D_PC = 4 * D_CONV
D_IN_PROJ = D_PC + 2 * D_ATTN + 2 * D_KV
ROW_Q = D_PC
ROW_KV = ROW_Q + D_ATTN
ROW_GA = ROW_KV + 2 * D_KV
N_HEADS = 16
HEAD_DIM = 64
HEADS_PER_KV = 8
BLK = 128
N_CHIPS = 4
RMS_EPS = 1e-5
SCALE = HEAD_DIM ** -0.5
SLOPES = tuple(2.0 ** (-8.0 * (h + 1) / N_HEADS) for h in range(N_HEADS))

ADAM_LR, ADAM_B1, ADAM_B2, ADAM_EPS, ADAM_WD, ADAM_STEP = 0.001, 0.9, 0.999, 1e-08, 0.01, 10

SMALL_ROWS = 8
SMALL_NORM_IN, SMALL_NORM_CONV, SMALL_NORM_ATTN, SMALL_NORM_FINAL, SMALL_CONV_W, SMALL_MISC = 0, 1, 2, 3, 4, 7
SMALL_LOSS_LANE = N_HEADS

TOK_TILE = 256
PC_PIECE = 512
MIB = 1 << 20


def _params(vmem_mib, semantics=None):
    return pltpu.CompilerParams(dimension_semantics=semantics, vmem_limit_bytes=vmem_mib * MIB)


def _nn(a, b):
    return jnp.dot(a, b, preferred_element_type=F32)


def _nt(a, b):
    return lax.dot_general(a, b, (((1,), (1,)), ((), ())), preferred_element_type=F32)


def _tn(a, b):
    return lax.dot_general(a, b, (((0,), (0,)), ((), ())), preferred_element_type=F32)


def _rstd(v):
    return lax.rsqrt(jnp.mean(v * v, axis=-1, keepdims=True) + RMS_EPS)


def _rms_bwd(g, xhat, rstd):
    return rstd * (g - xhat * jnp.mean(g * xhat, axis=-1, keepdims=True))


def _silu_and_grad(g):
    s = jax.nn.sigmoid(g)
    return g * s, s * (1.0 + g * (1.0 - s))


def _resident(shape):
    return pl.BlockSpec(shape, lambda *_: (0,) * len(shape), pipeline_mode=pl.Buffered(1))


def _xor(a, b):
    return a + b - 2 * a * b


def _cast_rows(src_ref, dst_ref, rows=32):
    def step(i, carry):
        sl = pl.ds(pl.multiple_of(i * rows, rows), rows)
        dst_ref[sl, :] = src_ref[sl, :].astype(dst_ref.dtype)
        return carry
    lax.fori_loop(0, src_ref.shape[0] // rows, step, 0)


def _ag_scratch(shard_shape):
    rows, width = shard_shape
    return [pltpu.VMEM((rows, width), F32), pltpu.VMEM((rows, width), BF16), pltpu.VMEM((3, rows // 2, width), BF16),
            pltpu.SemaphoreType.DMA((6,)), pltpu.SemaphoreType.DMA((6,)), pltpu.SemaphoreType.DMA((4,))]


def _ag_stages(sh_ref, out, f32_buf, own, land, send_sems, recv_sems, local_sems):
    rows = sh_ref.shape[0]
    half = rows // 2
    x, y, c = lax.axis_index("x"), lax.axis_index("y"), lax.axis_index("c")
    j = 2 * x + y
    p1 = (_xor(x, c), _xor(y, 1 - c), c)
    p2 = (_xor(x, 1 - c), _xor(y, c), c)
    sib = (x, y, 1 - c)
    j1 = 2 * p1[0] + p1[1]
    j2 = 2 * p2[0] + p2[1]
    j3 = 3 - j

    def rows_of(chip, hf):
        return out.at[pl.ds(pl.multiple_of(chip * rows + hf * half, 16), half), :]

    def rcopy(k, src, dst, to):
        return pltpu.make_async_remote_copy(src_ref=src, dst_ref=dst, send_sem=send_sems.at[k],
                                            recv_sem=recv_sems.at[k], device_id=to, device_id_type=MESH)

    my_half = own.at[pl.ds(pl.multiple_of(c * half, 16), half), :]
    hop1 = rcopy(0, my_half, land.at[0], p1)
    hop2_own = rcopy(1, my_half, land.at[1], p2)
    hop2_fwd = rcopy(2, land.at[0], land.at[2], p2)
    swaps = [rcopy(3 + s, land.at[s], rows_of(chip, c), sib) for s, chip in enumerate((j1, j2, j3))]
    keeps = [pltpu.make_async_copy(land.at[s], rows_of(chip, c), local_sems.at[1 + s]) for s, chip in enumerate((j1, j2, j3))]
    load = pltpu.make_async_copy(sh_ref, f32_buf, local_sems.at[0])
    own_out = pltpu.make_async_copy(own, out.at[pl.ds(pl.multiple_of(j * rows, 16), rows), :], local_sems.at[0])

    def stage_load():
        load.start()

    def stage_send():
        load.wait()
        _cast_rows(f32_buf, own)
        own_out.start()
        hop1.start()

    def stage_forward():
        hop1.wait_recv()
        hop2_own.start()
        hop2_fwd.start()
        swaps[0].start()
        keeps[0].start()

    def stage_publish():
        hop2_own.wait_recv()
        swaps[1].start()
        keeps[1].start()
        hop2_fwd.wait_recv()
        swaps[2].start()
        keeps[2].start()

    def stage_drain():
        for s, chip in enumerate((j2, j1, j3)):
            rcopy(3 + s, my_half, rows_of(chip, 1 - c), sib).wait_recv()
        for cp in [hop1, hop2_own, hop2_fwd] + swaps:
            cp.wait_send()
        for cp in [own_out] + keeps:
            cp.wait()

    return stage_load, stage_send, stage_forward, stage_publish, stage_drain


AG_CAST_ROWS = 400


def _gather_resident(sh_ref, out, f32_buf, send_sems, recv_sems, local_sems, after_first_hop):
    rows = sh_ref.shape[0]
    half = rows // 2
    x, y, c = lax.axis_index("x"), lax.axis_index("y"), lax.axis_index("c")
    j = 2 * x + y
    p1 = (_xor(x, c), _xor(y, 1 - c), c)
    p2 = (_xor(x, 1 - c), _xor(y, c), c)
    sib = (x, y, 1 - c)
    j1 = 2 * p1[0] + p1[1]
    j2 = 2 * p2[0] + p2[1]
    j3 = 3 - j

    def rows_of(chip, hf):
        return out.at[pl.ds(pl.multiple_of(chip * rows + hf * half, 16), half), :]

    def send(k, chip, to):
        return pltpu.make_async_remote_copy(src_ref=rows_of(chip, c), dst_ref=rows_of(chip, c), send_sem=send_sems.at[k],
                                            recv_sem=recv_sems.at[k], device_id=to, device_id_type=MESH)

    per_half = half // AG_CAST_ROWS

    chunks = [c * per_half + k for k in range(per_half)] + [(1 - c) * per_half + k for k in range(per_half)]

    def load(n):
        lo = pl.multiple_of(chunks[n] * AG_CAST_ROWS, 16)
        return pltpu.make_async_copy(sh_ref.at[pl.ds(lo, AG_CAST_ROWS), :], f32_buf.at[n % 2], local_sems.at[n % 2])

    sends = [send(0, j, p1), send(1, j, p2), send(2, j1, p2), send(3, j1, sib)]
    load(0).start()
    load(1).start()
    for n in range(len(chunks)):
        load(n).wait()
        lo = pl.multiple_of(chunks[n] * AG_CAST_ROWS, 16)
        _cast_rows(f32_buf.at[n % 2], out.at[pl.ds(pl.multiple_of(j * rows + lo, 16), AG_CAST_ROWS), :], rows=16)
        if n + 2 < len(chunks):
            load(n + 2).start()
        if n == per_half - 1:
            sends[0].start()
            sends[1].start()
    sends[0].wait_recv()
    for cp in sends[2:]:
        cp.start()
    after_first_hop()
    sends[1].wait_recv()
    sends.append(send(4, j2, sib))
    sends[-1].start()
    sends[2].wait_recv()
    sends.append(send(5, j3, sib))
    sends[-1].start()
    for k in (3, 4, 5):
        send(k, j, sib).wait_recv()
    for cp in sends:
        cp.wait_send()


def _fwd_in(x, norm_in, wt_sh, wo_sh, cw_sh, norm_conv_out, sinks, norm_attn_out):
    seq = x.shape[0]
    tile = TOK_TILE
    n_tiles = seq // tile
    stage_steps = (0, n_tiles // 4, (5 * n_tiles) // 8, n_tiles - 1)
    blocks = tile // BLK
    cw_cols = cw_sh.shape[-1]

    def body(x_ref, g_ref, wtsh_ref, wo_ref, cwsh_ref, gn_ref, sink_ref, gna_ref,
             h_ref, pc_ref, q_ref, kv_ref, ga_ref, oc_ref, ya_ref, oa_ref, pr_ref, sp_ref, wt_out, cw_ref, wo_out,
             zbuf, kv_last, wt_ref, f32_buf, cw_land, wt_send, wt_recv, wt_local, cw_send, cw_recv, cw_local, *ag_scratch):
        step = pl.program_id(0)
        to_hbm = pltpu.make_async_copy(wt_ref, wt_out, wt_local.at[0])
        load_wo, *stages = _ag_stages(wo_ref, wo_out, *ag_scratch)

        @pl.when(step == 0)
        def _():
            load_wo()
            zbuf[0:8, :] = jnp.zeros((8, D_CONV), F32)
            kv_last[...] = jnp.zeros_like(kv_last)
            x_i, y_i, c = lax.axis_index("x"), lax.axis_index("y"), lax.axis_index("c")
            j = 2 * x_i + y_i
            p1 = (_xor(x_i, c), _xor(y_i, 1 - c), c)
            p2 = (_xor(x_i, 1 - c), _xor(y_i, c), c)
            j1 = 2 * p1[0] + p1[1]

            def cw_copy(k, src, chip, to):
                return pltpu.make_async_remote_copy(src_ref=src, dst_ref=cw_land.at[chip], send_sem=cw_send.at[k],
                                                    recv_sem=cw_recv.at[k], device_id=to, device_id_type=MESH)

            mine = pltpu.make_async_copy(cwsh_ref, cw_land.at[j], cw_local.at[0])
            mine.start()
            first = cw_copy(0, cwsh_ref, j, p1)
            first.start()
            second = [cw_copy(1, cwsh_ref, j, p2), cw_copy(2, cw_land.at[j1], j1, p2)]

            def cw_second_hop():
                first.wait_recv()
                for cp in second:
                    cp.start()

            _gather_resident(wtsh_ref, wt_ref, f32_buf, wt_send, wt_recv, wt_local, cw_second_hop)
            to_hbm.start()
            for cp in second:
                cp.wait_recv()
            for cp in [first] + second:
                cp.wait_send()
            mine.wait()
            for chip in range(N_CHIPS):
                for tap in range(cw_sh.shape[0]):
                    cw_ref[tap:tap + 1, chip * cw_cols:(chip + 1) * cw_cols] = cw_land[chip, tap]

        for at, stage in zip(stage_steps[:-1], stages[:-1]):
            pl.when(step == at)(stage)

        def attention(b):
            rows = pl.ds(b * BLK, BLK)
            kv_prev = kv_last if b == 0 else kv_ref.at[pl.ds((b - 1) * BLK, BLK), :]
            return _attn_forward(q_ref.at[rows, :], kv_ref.at[rows, :], kv_prev, ga_ref.at[rows, :], sink_ref, gna_ref,
                                 _band_geometry(step * blocks + b), ya_ref.at[rows, :], oa_ref.at[rows, :],
                                 pr_ref.at[rows, :], sp_ref.at[rows, :])

        xv = x_ref[...]
        h = (xv * _rstd(xv) * g_ref[...]).astype(BF16)
        h_ref[...] = h
        q_ref[...] = _nt(h, wt_ref[ROW_Q:ROW_KV, :])
        kv_ref[...] = _nt(h, wt_ref[ROW_KV:ROW_GA, :])
        ga_ref[...] = _nt(h, wt_ref[ROW_GA:D_IN_PROJ, :])
        attention_blocks = [attention(b) for b in range(blocks)]
        for lo in range(0, D_PC, PC_PIECE):
            pc_ref[:, lo:lo + PC_PIECE] = _nt(h, wt_ref[lo:lo + PC_PIECE, :])
            for stages_of_block in attention_blocks:
                next(stages_of_block, None)
        for stages_of_block in attention_blocks:
            for _ in stages_of_block:
                pass
        kv_last[...] = kv_ref[tile - BLK:tile, :]

        cb, _, _, _, _, _, conv = _conv_core(pc_ref, zbuf, cw_ref)
        yc = cb * conv
        silu, _ = _silu_and_grad(pc_ref[:, 3 * D_CONV:4 * D_CONV])
        oc_ref[...] = (yc * _rstd(yc) * gn_ref[...] * silu).astype(BF16)
        zbuf[0:8, :] = zbuf[tile:tile + 8, :]

        @pl.when(step == stage_steps[-1])
        def _():
            stages[-1]()
            to_hbm.wait()

    def row(width):
        return pl.BlockSpec((tile, width), lambda i: (i, 0))

    any_spec = pl.BlockSpec(memory_space=pl.ANY)
    dma = pltpu.SemaphoreType.DMA
    wt_shape = (N_CHIPS * wt_sh.shape[0], wt_sh.shape[1])
    cw_shape = (cw_sh.shape[0], N_CHIPS * cw_cols)
    return pl.pallas_call(
        body, name="fwd_in", grid=(n_tiles,),
        out_shape=(jax.ShapeDtypeStruct((seq, D_MODEL), BF16), jax.ShapeDtypeStruct((seq, D_PC), F32),
                   jax.ShapeDtypeStruct((seq, D_ATTN), F32), jax.ShapeDtypeStruct((seq, 2 * D_KV), F32),
                   jax.ShapeDtypeStruct((seq, D_ATTN), F32), jax.ShapeDtypeStruct((seq, D_CONV), BF16),
                   pltpu.HBM((seq, D_ATTN), F32), pltpu.HBM((seq, D_ATTN), BF16),
                   pltpu.HBM((seq, N_HEADS * BLK), BF16), pltpu.HBM((seq, 128), F32),
                   pltpu.HBM(wt_shape, BF16), jax.ShapeDtypeStruct(cw_shape, F32),
                   jax.ShapeDtypeStruct((N_CHIPS * wo_sh.shape[0], wo_sh.shape[1]), BF16)),
        in_specs=[row(D_MODEL), _resident((1, D_MODEL)), any_spec, any_spec, any_spec, _resident((1, D_CONV)),
                  pl.BlockSpec(memory_space=pltpu.SMEM), _resident((1, D_ATTN))],
        out_specs=(row(D_MODEL), row(D_PC), row(D_ATTN), row(2 * D_KV), row(D_ATTN), row(D_CONV), row(D_ATTN),
                   row(D_ATTN), row(N_HEADS * BLK), row(128), any_spec, pl.BlockSpec(cw_shape, lambda i: (0, 0)),
                   any_spec),
        scratch_shapes=[pltpu.VMEM((tile + 8, D_CONV), F32), pltpu.VMEM((BLK, 2 * D_KV), F32),
                        pltpu.VMEM(wt_shape, BF16), pltpu.VMEM((2, AG_CAST_ROWS, wt_sh.shape[1]), F32),
                        pltpu.VMEM((N_CHIPS,) + cw_sh.shape, F32),
                        dma((6,)), dma((6,)), dma((2,)), dma((3,)), dma((3,)), dma((1,))] + _ag_scratch(wo_sh.shape),
        compiler_params=_params(62, ("arbitrary",)),
    )(x, norm_in, wt_sh, wo_sh, cw_sh, norm_conv_out, sinks, norm_attn_out)


def _conv_core(pc_ref, zbuf, cw_ref):
    tile = pc_ref.shape[0]
    cb = pc_ref[:, 0:D_CONV]
    cc = pc_ref[:, D_CONV:2 * D_CONV]
    cu = pc_ref[:, 2 * D_CONV:3 * D_CONV]
    z = cc * cu
    zbuf[8:tile + 8, :] = z
    z1 = zbuf[7:tile + 7, :]
    z2 = zbuf[6:tile + 6, :]
    conv = cw_ref[0:1, :] * z2 + cw_ref[1:2, :] * z1 + cw_ref[2:3, :] * z
    return cb, cc, cu, z, z1, z2, conv


def _band_geometry(block_index):
    qi = lax.broadcasted_iota(jnp.int32, (BLK, BLK), 0)
    kp = lax.broadcasted_iota(jnp.int32, (BLK, BLK), 1)
    use_cur = kp <= qi
    dist = jnp.where(use_cur, qi - kp, qi - kp + BLK).astype(F32)
    valid = use_cur | (block_index > 0)
    return use_cur, dist, valid


def _block_diag(cur, prev, group):
    lane = lax.broadcasted_iota(jnp.int32, cur.shape, 1)

    def halves(t):
        other = pltpu.roll(t, 64, 1)
        lo, hi = (t, other) if group == 0 else (other, t)
        return jnp.where(lane < 64, lo, 0.0), jnp.where(lane >= 64, hi, 0.0)

    return jnp.concatenate(halves(cur) + halves(prev), axis=0).astype(BF16)


def _merge(s4, use_cur):
    return (jnp.where(use_cur, s4[:, 0:BLK], s4[:, 2 * BLK:3 * BLK]),
            jnp.where(use_cur, s4[:, BLK:2 * BLK], s4[:, 3 * BLK:4 * BLK]))


def _split(a, b, use_cur):
    return jnp.concatenate([jnp.where(use_cur, a, 0.0), jnp.where(use_cur, b, 0.0),
                            jnp.where(use_cur, 0.0, a), jnp.where(use_cur, 0.0, b)], axis=1)


def _softmax_head(s, head, sink, dist, valid):
    sc = jnp.where(valid, s - SLOPES[head] * dist, -jnp.inf)
    m = jnp.maximum(jnp.max(sc, axis=-1, keepdims=True), sink)
    p = jnp.exp(sc - m)
    es = jnp.exp(sink - m)
    inv = 1.0 / (jnp.sum(p, axis=-1, keepdims=True) + es)
    return p * inv, es * inv


def _attn_operands(q_ref, kvc_ref, kvp_ref):
    groups = range(N_HEADS // HEADS_PER_KV)
    kbd = [_block_diag(kvc_ref[:, 0:D_KV], kvp_ref[:, 0:D_KV], g) for g in groups]
    vbd = [_block_diag(kvc_ref[:, D_KV:2 * D_KV], kvp_ref[:, D_KV:2 * D_KV], g) for g in groups]
    qps = [(q_ref[:, j * 128:(j + 1) * 128] * SCALE).astype(BF16) for j in range(N_HEADS // 2)]
    return qps, kbd, vbd


def _attn_forward(q_ref, kvc_ref, kvp_ref, ga_ref, sink_ref, gn_ref, geometry, ya_ref, oa_ref, pr_ref, sp_ref):
    use_cur, dist, valid = geometry
    _, kbd, vbd = operands = _attn_operands(q_ref, kvc_ref, kvp_ref)
    yield
    scores = []
    for j, qp in enumerate(operands[0]):
        scores += _merge(_nt(qp, kbd[j // 4]), use_cur)
    yield
    sinks = [sink_ref[0, h] for h in range(N_HEADS)]
    scores = [jnp.where(valid, s - SLOPES[h] * dist, -jnp.inf) for h, s in enumerate(scores)]
    maxes = [jnp.maximum(jnp.max(s, axis=-1, keepdims=True), sinks[h]) for h, s in enumerate(scores)]
    yield
    exps = [jnp.exp(s - m) for s, m in zip(scores, maxes)]
    sink_exps = [jnp.exp(sinks[h] - m) for h, m in enumerate(maxes)]
    yield
    invs = [1.0 / (jnp.sum(e, axis=-1, keepdims=True) + se) for e, se in zip(exps, sink_exps)]
    probs = [e * inv for e, inv in zip(exps, invs)]
    pr_ref[...] = jnp.concatenate(probs, axis=1).astype(BF16)
    lane = lax.broadcasted_iota(jnp.int32, (BLK, 128), 1)
    sp_ref[...] = sum(jnp.where(lane == h, se * inv, 0.0) for h, (se, inv) in enumerate(zip(sink_exps, invs)))
    yield
    p4s = [_split(probs[2 * j], probs[2 * j + 1], use_cur).astype(BF16) for j in range(N_HEADS // 2)]
    ya = jnp.concatenate([_nn(p4, vbd[j // 4]) for j, p4 in enumerate(p4s)], axis=1)
    ya_ref[...] = ya
    yield
    silu, _ = _silu_and_grad(ga_ref[...])
    oa_ref[...] = (ya * _rstd(ya) * gn_ref[...] * silu).astype(BF16)


def _out_proj_loss(x, oc, oa, wo, norm_final, target, pc, conv_w, norm_conv_out):
    seq = x.shape[0]
    tile = TOK_TILE
    n_tiles = seq // tile

    def body(x_ref, oc_ref, oa_ref, wo_ref, gf_ref, t_ref, pc_ref, hcc_ref, hcu_ref, cw_ref, gn_ref,
             dx2_ref, doa_ref, gwo_ref, dpc_ref, small_ref, loss_acc, zbuf, dbuf):
        step = pl.program_id(0)

        def small_add(row, value):
            small_ref[row:row + 1, :] += jnp.sum(value, axis=0, keepdims=True)

        @pl.when(step == 0)
        def _():
            gwo_ref[...] = jnp.zeros_like(gwo_ref)
            small_ref[...] = jnp.zeros_like(small_ref)
            loss_acc[...] = jnp.zeros_like(loss_acc)
            dbuf[tile:tile + 8, :] = jnp.zeros((8, D_CONV), F32)

        oc, oa = oc_ref[...], oa_ref[...]
        x2 = x_ref[...] + _nn(oc, wo_ref[0:D_CONV, :]) + _nn(oa, wo_ref[D_CONV:D_MIX, :])
        r = _rstd(x2)
        xhat = x2 * r
        err = xhat * gf_ref[...] - t_ref[...]
        loss_acc[...] += jnp.sum(err * err, axis=0, keepdims=True) * (0.5 / D_MODEL)
        dy = err * (1.0 / D_MODEL)
        small_add(SMALL_NORM_FINAL, dy * xhat)
        dx2 = _rms_bwd(dy * gf_ref[...], xhat, r)
        dx2_ref[...] = dx2
        db = dx2.astype(BF16)
        do = _nt(db, wo_ref[0:D_CONV, :])
        doa_ref[...] = _nt(db, wo_ref[D_CONV:D_MIX, :])
        gwo_ref[0:D_CONV, :] += _tn(oc, db)
        gwo_ref[D_CONV:D_MIX, :] += _tn(oa, db)

        is_first_tile = step == n_tiles - 1
        zbuf[0:8, :] = jnp.where(is_first_tile, 0.0, hcc_ref[...] * hcu_ref[...])
        cb, cc, cu, z, z1, z2, conv = _conv_core(pc_ref, zbuf, cw_ref)
        silu, dsilu = _silu_and_grad(pc_ref[:, 3 * D_CONV:4 * D_CONV])
        yc = cb * conv
        rc = _rstd(yc)
        chat = yc * rc
        dn = do * silu
        dpc_ref[:, 3 * D_CONV:4 * D_CONV] = (do * (chat * gn_ref[...]) * dsilu).astype(BF16)
        small_add(SMALL_NORM_CONV, dn * chat)
        dyc = _rms_bwd(dn * gn_ref[...], chat, rc)
        dpc_ref[:, 0:D_CONV] = (dyc * conv).astype(BF16)
        dconv = dyc * cb
        small_add(SMALL_CONV_W, dconv * z2)
        small_add(SMALL_CONV_W + 1, dconv * z1)
        small_add(SMALL_CONV_W + 2, dconv * z)
        dbuf[0:tile, :] = dconv
        dz = cw_ref[2:3, :] * dconv + cw_ref[1:2, :] * dbuf[1:tile + 1, :] + cw_ref[0:1, :] * dbuf[2:tile + 2, :]
        dpc_ref[:, D_CONV:2 * D_CONV] = (dz * cu).astype(BF16)
        dpc_ref[:, 2 * D_CONV:3 * D_CONV] = (dz * cc).astype(BF16)
        dbuf[tile:tile + 8, :] = dbuf[0:8, :]

        @pl.when(step == n_tiles - 1)
        def _():
            lane = lax.broadcasted_iota(jnp.int32, (1, D_MODEL), 1)
            small_ref[SMALL_MISC:SMALL_MISC + 1, :] = jnp.where(lane == SMALL_LOSS_LANE, jnp.sum(loss_acc[...]), 0.0)

    def rev(i):
        return n_tiles - 1 - i

    def row(width):
        return pl.BlockSpec((tile, width), lambda i: (rev(i), 0))

    def halo(col_block):
        return pl.BlockSpec((8, D_CONV), lambda i: (jnp.maximum(rev(i) * (tile // 8) - 1, 0), col_block))

    def const(shape):
        return pl.BlockSpec(shape, lambda i: (0, 0))

    return pl.pallas_call(
        body, name="out_proj_loss", grid=(n_tiles,),
        out_shape=(jax.ShapeDtypeStruct((seq, D_MODEL), F32), jax.ShapeDtypeStruct((seq, D_ATTN), F32),
                   jax.ShapeDtypeStruct((D_MIX, D_MODEL), F32), jax.ShapeDtypeStruct((seq, D_PC), BF16),
                   jax.ShapeDtypeStruct((SMALL_ROWS, D_MODEL), F32)),
        in_specs=[row(D_MODEL), row(D_CONV), row(D_ATTN), _resident(wo.shape), _resident((1, D_MODEL)), row(D_MODEL),
                  row(D_PC), halo(1), halo(2), _resident(conv_w.shape), _resident((1, D_CONV))],
        out_specs=(row(D_MODEL), row(D_ATTN), const((D_MIX, D_MODEL)), row(D_PC), const((SMALL_ROWS, D_MODEL))),
        scratch_shapes=[pltpu.VMEM((1, D_MODEL), F32), pltpu.VMEM((tile + 8, D_CONV), F32),
                        pltpu.VMEM((tile + 8, D_CONV), F32)],
        compiler_params=_params(62, ("arbitrary",)),
    )(x, oc, oa, wo, norm_final, target, pc, pc, pc, conv_w, norm_conv_out)


def _attn_bwd(q, kv, ga, ya, doa, probs, sink_probs, norm_attn_out, gwo, small):
    seq = q.shape[0]
    per_step = 2
    n_steps = seq // (per_step * BLK)
    stage_steps = (0, 1, n_steps // 4, (5 * n_steps) // 8, n_steps - 1, n_steps - 1)

    def body(q_ref, kvc_ref, kvp_ref, ga_ref, ya_ref, doa_ref, pr_ref, sp_ref, gn_ref, gwo_ref, small_ref,
             dqg_ref, small_out, gwo_sh, gna_ref, gs_ref, carry, dya_buf, *rs_scratch):
        step = pl.program_id(0)
        rs_stages = _rs_wout_stages(gwo_ref, gwo_sh, *rs_scratch)
        for at, stage in zip(stage_steps[:-1], rs_stages[:-1]):
            pl.when(step == at)(stage)

        @pl.when(step == 0)
        def _():
            gna_ref[...] = jnp.zeros_like(gna_ref)
            gs_ref[...] = jnp.zeros_like(gs_ref)
            carry[...] = jnp.zeros_like(carry)

        lane = lax.broadcasted_iota(jnp.int32, (BLK, 128), 1)

        def fold(bd):
            return (jnp.where(lane < 64, bd[0:BLK], 0.0) + jnp.where(lane >= 64, bd[BLK:2 * BLK], 0.0),
                    jnp.where(lane < 64, bd[2 * BLK:3 * BLK], 0.0) + jnp.where(lane >= 64, bd[3 * BLK:4 * BLK], 0.0))

        def one_block(b):
            rows = slice(b * BLK, (b + 1) * BLK)
            kv_prev = kvp_ref if b == 0 else kvc_ref.at[(b - 1) * BLK:b * BLK, :]
            ya = ya_ref[rows, :]
            r = _rstd(ya)
            xhat = ya * r
            silu, dsilu = _silu_and_grad(ga_ref[rows, :])
            do = doa_ref[rows, :]
            dn = do * silu
            dqg_ref[rows, D_ATTN:2 * D_ATTN] = (do * (xhat * gn_ref[...]) * dsilu).astype(BF16)
            gna_ref[...] += jnp.sum(dn * xhat, axis=0, keepdims=True)
            dya_buf[rows, :] = _rms_bwd(dn * gn_ref[...], xhat, r).astype(BF16)

            use_cur = _band_geometry(per_step * (n_steps - 1 - step) + b)[0]
            pairs = range(N_HEADS // 2)
            qps, kbd, vbd = _attn_operands(q_ref.at[rows, :], kvc_ref.at[rows, :], kv_prev)
            probs = [pr_ref[rows, h * BLK:(h + 1) * BLK].astype(F32) for h in range(N_HEADS)]
            dyps = [dya_buf[rows, j * 128:(j + 1) * 128] for j in pairs]
            dps = []
            for j in pairs:
                dps += _merge(_nt(dyps[j], vbd[j // 4]), use_cur)
            deltas = [jnp.sum(p * dp, axis=-1, keepdims=True) for p, dp in zip(probs, dps)]
            dss = [p * (dp - delta) for p, dp, delta in zip(probs, dps, deltas)]
            delta_lanes = sum(jnp.where(lane == h, deltas[h], 0.0) for h in range(N_HEADS))
            gs_ref[...] -= jnp.sum(sp_ref[rows, :] * delta_lanes, axis=0, keepdims=True)
            ds4s = [_split(dss[2 * j], dss[2 * j + 1], use_cur).astype(BF16) for j in pairs]
            p4s = [_split(probs[2 * j], probs[2 * j + 1], use_cur).astype(BF16) for j in pairs]
            dqg_ref[rows, 0:D_ATTN] = jnp.concatenate([_nn(ds4s[j], kbd[j // 4]) * SCALE for j in pairs], axis=1).astype(BF16)
            sums = []
            for group in range(N_HEADS // HEADS_PER_KV):
                acc = [jnp.zeros((BLK, 128), F32) for _ in range(4)]
                for j in range(group * 4, group * 4 + 4):
                    for slot, part in enumerate(fold(_tn(ds4s[j], qps[j])) + fold(_tn(p4s[j], dyps[j]))):
                        acc[slot] = acc[slot] + part
                sums.append([a + pltpu.roll(a, 64, 1) for a in acc])
            dk_cur, dk_prev, dv_cur, dv_prev = (jnp.where(lane < 64, lo, hi) for lo, hi in zip(sums[0], sums[1]))
            dqg_ref[rows, 2 * D_ATTN:2 * D_ATTN + 2 * D_KV] = (jnp.concatenate([dk_cur, dv_cur], axis=1) + carry[...]).astype(BF16)
            carry[...] = jnp.concatenate([dk_prev, dv_prev], axis=1)

        for b in reversed(range(per_step)):
            one_block(b)

        @pl.when(step == n_steps - 1)
        def _():
            small_out[...] = small_ref[...]
            small_out[SMALL_NORM_ATTN:SMALL_NORM_ATTN + 1, :] = gna_ref[...]
            small_out[SMALL_MISC:SMALL_MISC + 1, 0:128] = small_ref[SMALL_MISC:SMALL_MISC + 1, 0:128] + gs_ref[...]

        pl.when(step == stage_steps[-1])(rs_stages[-1])

    def rows(width):
        return pl.BlockSpec((per_step * BLK, width), lambda i: (n_steps - 1 - i, 0))

    kv_prev = pl.BlockSpec((BLK, 2 * D_KV), lambda i: (jnp.maximum(per_step * (n_steps - 1 - i) - 1, 0), 0))
    any_spec = pl.BlockSpec(memory_space=pl.ANY)
    return pl.pallas_call(
        body, name="attn_bwd", grid=(n_steps,),
        out_shape=(jax.ShapeDtypeStruct((seq, D_QG), BF16), pltpu.HBM(small.shape, F32),
                   pltpu.HBM((gwo.shape[0] // N_CHIPS, gwo.shape[1]), F32)),
        in_specs=[rows(D_ATTN), rows(2 * D_KV), kv_prev, rows(D_ATTN), rows(D_ATTN), rows(D_ATTN),
                  rows(N_HEADS * BLK), rows(128), _resident((1, D_ATTN)), any_spec, _resident(small.shape)],
        out_specs=(rows(D_QG), pl.BlockSpec(small.shape, lambda i: (0, 0)), any_spec),
        scratch_shapes=[pltpu.VMEM((1, D_ATTN), F32), pltpu.VMEM((1, 128), F32),
                        pltpu.VMEM((BLK, 2 * D_KV), F32), pltpu.VMEM((per_step * BLK, D_ATTN), BF16)] + _rs_wout_scratch(gwo.shape),
        compiler_params=_params(44, ("arbitrary",)),
    )(q, kv, kv, ga, ya, doa, probs, sink_probs, norm_attn_out, gwo, small)


GBLK = 256
GSUB = 64
PAIR_RING = 4
LAG_PAIR, LAG_HOP1, LAG_HOP2 = 1, 7, 14


def _bwd_in(dpc, dqg, h, wt, x, norm_in, dx2, small):
    seq = x.shape[0]
    n_blk = D_IN_PROJ // GBLK
    per_chip = n_blk // N_CHIPS
    n_slots = (n_blk + 1) // 2
    n_sub = GBLK // GSUB
    chip_rows = D_IN_PROJ // N_CHIPS
    tile = TOK_TILE
    n_tiles = seq // tile
    n_steps = n_blk + max(n_tiles, LAG_HOP2)
    chunk = min(seq, 512)
    blk_q, blk_kv, blk_ga = ROW_Q // GBLK, ROW_KV // GBLK, ROW_GA // GBLK

    def block_of(i):
        k = i % N_CHIPS
        robin = per_chip * ((k % 2) * 2 + k // 2) + i // N_CHIPS
        if isinstance(i, int):
            return robin if i < per_chip * N_CHIPS else i
        return jnp.where(i < per_chip * N_CHIPS, robin, i)

    def owner_of(i):
        return (i // N_CHIPS) % 2

    def slot_of(i):
        return (i // (2 * N_CHIPS)) * N_CHIPS + i % N_CHIPS

    def body(dpc_ref, dqg_ref, wt_ref, h_ref, x_ref, g_ref, dx2_ref, small_ref, gx_ref, small_sum, gwt_sh,
             dh_acc, gni, keep, pbuf, xbuf, land, land2, small_land,
             pair_send, pair_recv, h1_send, h1_recv, h2_send, h2_recv, sw_send, sw_recv, sm_send, sm_recv, out_sem):
        step = pl.program_id(0)
        x_i, y_i, c = lax.axis_index("x"), lax.axis_index("y"), lax.axis_index("c")
        me = 4 * x_i + 2 * y_i + c
        j = 2 * x_i + y_i
        pa = (_xor(x_i, 1 - c), _xor(y_i, c), c)
        pb = (_xor(x_i, c), _xor(y_i, 1 - c), c)
        sib = (x_i, y_i, 1 - c)
        ja = 2 * pa[0] + pa[1]
        jb = 2 * pb[0] + pb[1]
        jd = 3 - j

        def remote(src, dst, send, recv, to):
            return pltpu.make_async_remote_copy(src_ref=src, dst_ref=dst, send_sem=send, recv_sem=recv,
                                                device_id=to, device_id_type=MESH)

        def piece(ref, slot, u, n):
            return ref.at[slot, pl.ds(u * GSUB, n * GSUB), :]

        def chip_rows_at(ref, local, n):
            return ref.at[pl.ds(pl.multiple_of(local, GSUB), n * GSUB), :]

        def pair_copy(i):
            slot = slot_of(i)
            return remote(pbuf.at[i % PAIR_RING], land.at[slot], pair_send.at[slot], pair_recv.at[slot], sib)

        def h1_copy(slot, u, n):
            k = slot * n_sub + u
            return remote(piece(xbuf, slot, u, n), piece(xbuf, slot, u, n), h1_send.at[k], h1_recv.at[k], pa)

        def h2_copy(slot, u, n, local):
            k = slot * n_sub + u
            return remote(piece(xbuf, slot, u, n), chip_rows_at(land2, local, n), h2_send.at[k], h2_recv.at[k], pb)

        def sw_copy(slot, u, n, local):
            k = slot * n_sub + u
            return remote(piece(keep, slot, u, n), chip_rows_at(gwt_sh, local, n), sw_send.at[k], sw_recv.at[k], sib)

        def own_copy(slot, u, n, local):
            return pltpu.make_async_copy(piece(keep, slot, u, n), chip_rows_at(gwt_sh, local, n), out_sem.at[slot * n_sub + u])

        def owned(i):
            return (i >= 0) & (i < n_blk) & (owner_of(i) == c)

        def chip_of(blk, u):
            row = blk * GBLK + u * GSUB
            chip = row // chip_rows
            return chip, row - chip * chip_rows

        def pieces(blk):
            first, local = chip_of(blk, 0)
            whole = first == chip_of(blk, n_sub - 1)[0]
            if isinstance(blk, int):
                return [(True, 0, n_sub, first, local)] if whole else [(True, u, 1) + chip_of(blk, u) for u in range(n_sub)]
            return [(whole, 0, n_sub, first, local)] + [(jnp.logical_not(whole), u, 1) + chip_of(blk, u) for u in range(n_sub)]

        @pl.when(step == 0)
        def _():
            dh_acc[...] = jnp.zeros_like(dh_acc)
            gni[...] = jnp.zeros_like(gni)

        @pl.when(step < n_blk)
        def _():
            from_pc = block_of(step) < blk_q
            block = _tn(jnp.where(from_pc, dpc_ref[...], dqg_ref[...]), h_ref[...])
            for t in range(0, seq, chunk):
                d = jnp.where(from_pc, dpc_ref[t:t + chunk, :], dqg_ref[t:t + chunk, :])
                dh_acc[t:t + chunk, :] += _nn(d, wt_ref[...])

            @pl.when(owner_of(step) == c)
            def _():
                keep[slot_of(step)] = block

            @pl.when(owner_of(step) != c)
            def _():
                @pl.when(step >= 2 * PAIR_RING)
                def _():
                    pair_copy(step - 2 * PAIR_RING).wait_send()
                pbuf[step % PAIR_RING] = block.astype(BF16)
                pair_copy(step).start()

        i1 = step - LAG_PAIR

        @pl.when(owned(i1))
        def _():
            slot = slot_of(i1)
            pair_copy(i1).wait_recv()
            _accumulate(keep.at[slot], land.at[slot])
            for cond, u, n, chip, _ in pieces(block_of(i1)):
                @pl.when(cond & ((chip == ja) | (chip == jd)))
                def _(u=u, n=n):
                    _cast_rows(piece(keep, slot, u, n), piece(xbuf, slot, u, n))
                    h1_copy(slot, u, n).start()

        i2 = step - LAG_HOP1

        @pl.when(owned(i2))
        def _():
            slot = slot_of(i2)
            for cond, u, n, chip, local in pieces(block_of(i2)):
                @pl.when(cond & ((chip == j) | (chip == jb)))
                def _(u=u, n=n, chip=chip, local=local):
                    h1_copy(slot, u, n).wait_recv()
                    _accumulate(piece(keep, slot, u, n), piece(xbuf, slot, u, n))

                    @pl.when(chip == jb)
                    def _():
                        _cast_rows(piece(keep, slot, u, n), piece(xbuf, slot, u, n))
                        h2_copy(slot, u, n, local).start()

                @pl.when(cond & ((chip == ja) | (chip == jd)))
                def _(u=u, n=n):
                    h1_copy(slot, u, n).wait_send()

        i3 = step - LAG_HOP2

        @pl.when(owned(i3))
        def _():
            slot = slot_of(i3)
            for cond, u, n, chip, local in pieces(block_of(i3)):
                @pl.when(cond & (chip == j))
                def _(u=u, n=n, local=local):
                    h2_copy(slot, u, n, local).wait_recv()
                    _accumulate(piece(keep, slot, u, n), chip_rows_at(land2, local, n))
                    own_copy(slot, u, n, local).start()
                    sw_copy(slot, u, n, local).start()

                @pl.when(cond & (chip == jb))
                def _(u=u, n=n, local=local):
                    h2_copy(slot, u, n, local).wait_send()

        e = step - n_blk

        @pl.when((e >= 0) & (e < n_tiles))
        def _():
            dh = dh_acc[pl.ds(pl.multiple_of(e * tile, tile), tile), :]
            xv = x_ref[...]
            r = _rstd(xv)
            xhat = xv * r
            gni[...] += jnp.sum(dh * xhat, axis=0, keepdims=True)
            gx_ref[...] = _rms_bwd(dh * g_ref[...], xhat, r) + dx2_ref[...]

        others = [(dx, dy, dc) for dx in (0, 1) for dy in (0, 1) for dc in (0, 1)][1:]
        sends = [remote(small_land.at[me], small_land.at[me], sm_send.at[k], sm_recv.at[k],
                        (_xor(x_i, dx), _xor(y_i, dy), _xor(c, dc))) for k, (dx, dy, dc) in enumerate(others)]

        @pl.when(step == min(n_blk + n_tiles, n_steps - 1))
        def _():
            small_land[me] = small_ref[...]
            small_land[me, SMALL_NORM_IN:SMALL_NORM_IN + 1, :] = gni[...]
            for cp in sends:
                cp.start()

        @pl.when(step == n_steps - 1)
        def _():
            for i in range(n_blk):
                if i + 2 * PAIR_RING >= n_blk:
                    @pl.when(owner_of(i) != c)
                    def _(i=i):
                        pair_copy(i).wait_send()
                for _, u, n, chip, local in pieces(block_of(i)):
                    @pl.when((j == chip) & (c == owner_of(i)))
                    def _(i=i, u=u, n=n, local=local):
                        sw_copy(slot_of(i), u, n, local).wait_send()
                        own_copy(slot_of(i), u, n, local).wait()

                    @pl.when((j == chip) & (c != owner_of(i)))
                    def _(i=i, u=u, n=n, local=local):
                        sw_copy(slot_of(i), u, n, local).wait_recv()
            for cp in sends:
                cp.wait_recv()
            total = small_land[0]
            for dev in range(1, 8):
                total = total + small_land[dev]
            small_sum[...] = total
            for cp in sends:
                cp.wait_send()

    def blk_at(i):
        return block_of(jnp.clip(i, 0, n_blk - 1))

    last_pc_step = max(i for i in range(n_blk) if block_of(i) < blk_q)

    def next_block(i, in_pc):
        i = jnp.clip(i, 0, n_blk - 1)
        step = jnp.full_like(i, last_pc_step if in_pc else n_blk - 1)
        for ahead in reversed(range(N_CHIPS)):
            cand = jnp.minimum(i + ahead, n_blk - 1)
            step = jnp.where((block_of(cand) < blk_q) == in_pc, cand, step)
        return block_of(step)

    def dqg_block(i):
        b = next_block(i, False)
        q_blk = jnp.clip(b - blk_q, 0, blk_kv - blk_q - 1)
        ga_blk = (D_ATTN // GBLK) + jnp.clip(b - blk_ga, 0, n_blk - blk_ga - 1)
        return jnp.where(b < blk_kv, q_blk, jnp.where(b == blk_kv, 2 * D_ATTN // GBLK, ga_blk))

    def tok(i):
        return (jnp.clip(i - n_blk, 0, n_tiles - 1), 0)

    n_piece = n_slots * n_sub
    dma = pltpu.SemaphoreType.DMA
    return pl.pallas_call(
        body, name="bwd_in", grid=(n_steps,),
        out_shape=(jax.ShapeDtypeStruct((seq, D_MODEL), F32), jax.ShapeDtypeStruct(small.shape, F32),
                   jax.ShapeDtypeStruct((chip_rows, D_MODEL), F32)),
        in_specs=[pl.BlockSpec((seq, GBLK), lambda i: (0, next_block(i, True))),
                  pl.BlockSpec((seq, GBLK), lambda i: (0, dqg_block(i))),
                  pl.BlockSpec((GBLK, D_MODEL), lambda i: (blk_at(i), 0)),
                  _resident(h.shape),
                  pl.BlockSpec((tile, D_MODEL), tok), _resident((1, D_MODEL)), pl.BlockSpec((tile, D_MODEL), tok),
                  _resident(small.shape)],
        out_specs=(pl.BlockSpec((tile, D_MODEL), tok), pl.BlockSpec(small.shape, lambda i: (0, 0)),
                   pl.BlockSpec(memory_space=pl.ANY)),
        scratch_shapes=[pltpu.VMEM((seq, D_MODEL), F32), pltpu.VMEM((1, D_MODEL), F32),
                        pltpu.VMEM((n_slots, GBLK, D_MODEL), F32), pltpu.VMEM((PAIR_RING, GBLK, D_MODEL), BF16),
                        pltpu.VMEM((n_slots, GBLK, D_MODEL), BF16), pltpu.VMEM((n_slots, GBLK, D_MODEL), BF16),
                        pltpu.VMEM((chip_rows, D_MODEL), BF16), pltpu.VMEM((8,) + small.shape, F32),
                        dma((n_slots,)), dma((n_slots,)), dma((n_piece,)), dma((n_piece,)), dma((n_piece,)),
                        dma((n_piece,)), dma((n_piece,)), dma((n_piece,)), dma((7,)), dma((7,)), dma((n_piece,))],
        compiler_params=_params(62, ("arbitrary",)),
    )(dpc, dqg, wt, h, x, norm_in, dx2, small)


def _accumulate(dst_ref, src_ref, rows=16):
    def step(i, carry):
        sl = pl.ds(pl.multiple_of(i * rows, rows), rows)
        dst_ref[sl, :] = dst_ref[sl, :] + src_ref[sl, :].astype(F32)
        return carry
    lax.fori_loop(0, dst_ref.shape[0] // rows, step, 0)


def _rs_wout_scratch(gwo_shape):
    o_half, width = gwo_shape[0] // N_CHIPS // 2, gwo_shape[1]
    return [pltpu.VMEM((4, o_half, width), F32), pltpu.VMEM((4, o_half, width), BF16),
            pltpu.VMEM((4, o_half, width), BF16), pltpu.VMEM((2, o_half, width), BF16),
            pltpu.VMEM((o_half, width), BF16),
            pltpu.SemaphoreType.DMA((8,)), pltpu.SemaphoreType.DMA((8,)), pltpu.SemaphoreType.DMA((4,))]


def _rs_wout_stages(gwo_ref, gwo_sh, acc_o, sb_o, r1o, r2o, r3o, send_sems, recv_sems, local_sems):
    o_rows = gwo_ref.shape[0] // N_CHIPS
    o_half = o_rows // 2
    x, y, c = lax.axis_index("x"), lax.axis_index("y"), lax.axis_index("c")
    j = 2 * x + y
    pa = (_xor(x, 1 - c), _xor(y, c), c)
    pb = (_xor(x, c), _xor(y, 1 - c), c)
    sib = (x, y, 1 - c)
    ja = 2 * pa[0] + pa[1]
    jb = 2 * pb[0] + pb[1]
    jd = 3 - j
    order = (ja, jd, jb, j)
    sib_order = (jb, jd, ja, j)

    def rcopy(k, src, dst, to):
        return pltpu.make_async_remote_copy(src_ref=src, dst_ref=dst, send_sem=send_sems.at[k],
                                            recv_sem=recv_sems.at[k], device_id=to, device_id_type=MESH)

    def o_rows_of(chip, half):
        return gwo_ref.at[pl.ds(pl.multiple_of(chip * o_rows + half * o_half, 8), o_half), :]

    def loads(chips, half):
        return [pltpu.make_async_copy(o_rows_of(chip, half), acc_o.at[s], local_sems.at[s]) for s, chip in enumerate(chips)]

    def pair_send(s):
        return rcopy(s, sb_o.at[s], r1o.at[s], sib)

    def out_half(half):
        return gwo_sh.at[pl.ds(pl.multiple_of(half * o_half, 8), o_half), :]

    hop1 = [rcopy(4 + s, sb_o.at[s], r2o.at[s], pa) for s in range(2)]
    hop2 = rcopy(6, sb_o.at[2], r3o, pb)
    swap = rcopy(7, acc_o.at[3], out_half(c), sib)
    mine = pltpu.make_async_copy(acc_o.at[3], out_half(c), local_sems.at[0])

    def resend(s, copy):
        pair_send(s).wait_send()
        _cast_rows(acc_o.at[s], sb_o.at[s])
        copy.start()

    def stage_load():
        for cp in loads(sib_order, 1 - c):
            cp.start()

    def stage_pair():
        for s, (cp, mine_in) in enumerate(zip(loads(sib_order, 1 - c), loads(order, c))):
            cp.wait()
            _cast_rows(acc_o.at[s], sb_o.at[s])
            pair_send(s).start()
            mine_in.start()

    def stage_hop1():
        for s, cp in enumerate(loads(order, c)):
            cp.wait()
            pair_send(s).wait_recv()
            _accumulate(acc_o.at[s], r1o.at[s])
            if s < 2:
                resend(s, hop1[s])

    def stage_hop2():
        hop1[1].wait_recv()
        _accumulate(acc_o.at[2], r2o.at[1])
        resend(2, hop2)
        hop1[0].wait_recv()
        _accumulate(acc_o.at[3], r2o.at[0])

    def stage_final():
        hop2.wait_recv()
        _accumulate(acc_o.at[3], r3o)
        mine.start()
        swap.start()

    def stage_drain():
        rcopy(7, acc_o.at[3], out_half(1 - c), sib).wait_recv()
        for cp in [pair_send(3)] + hop1 + [hop2, swap]:
            cp.wait_send()
        mine.wait()

    return stage_load, stage_pair, stage_hop1, stage_hop2, stage_final, stage_drain


def _adamw_big(groups, passed):
    arrays = [a for group in groups for a in group[:4]]
    steps = [group[0].shape[0] // group[4] for group in groups]
    first = [sum(steps[:k]) for k in range(len(steps) + 1)]
    n = len(arrays)

    def body(*refs):
        ins, passed_in, outs, passed_out = refs[:n], refs[n], refs[n + 1:2 * n + 1], refs[2 * n + 1]
        i = pl.program_id(0)
        for k in range(len(groups)):
            @pl.when((i >= first[k]) & (i < first[k + 1]))
            def _(k=k):
                w_ref, g_ref, m_ref, v_ref = ins[4 * k:4 * k + 4]
                _adamw_update(w_ref, g_ref[...], m_ref, v_ref, *outs[4 * k:4 * k + 4])

        @pl.when(i < steps[0])
        def _():
            passed_out[...] = passed_in[...]

    def spec(k, rows, width):
        return pl.BlockSpec((rows, width), lambda i: (jnp.clip(i - first[k], 0, steps[k] - 1), 0))

    specs = [spec(k, group[4], group[0].shape[1]) for k, group in enumerate(groups) for _ in range(4)]
    specs.append(spec(0, passed.shape[0] // steps[0], passed.shape[1]))
    out = pl.pallas_call(
        body, name="adamw_big", grid=(first[-1],),
        out_shape=tuple(jax.ShapeDtypeStruct(a.shape, a.dtype) for a in arrays + [passed]),
        in_specs=specs, out_specs=tuple(specs),
        compiler_params=_params(40, ("arbitrary",)),
    )(*arrays, passed)
    return [tuple(out[4 * k:4 * k + 4]) for k in range(len(groups))], out[-1]


def _adamw_update(w_ref, gv, m_ref, v_ref, go_ref, d_ref, nm_ref, nv_ref, at=...):
    go_ref[at] = gv
    nm = ADAM_B1 * m_ref[at] + (1.0 - ADAM_B1) * gv
    nv = ADAM_B2 * v_ref[at] + (1.0 - ADAM_B2) * (gv * gv)
    m_hat = nm / (1.0 - ADAM_B1 ** ADAM_STEP)
    v_hat = nv / (1.0 - ADAM_B2 ** ADAM_STEP)
    d_ref[at] = -ADAM_LR * (m_hat / (jnp.sqrt(v_hat) + ADAM_EPS) + ADAM_WD * w_ref[at])
    nm_ref[at] = nm
    nv_ref[at] = nv


def _adamw_small(chip, small_sum, weights, grads_of, ms, vs):
    n = len(weights)

    def body(chip_ref, small_ref, *refs):
        ins, outs, loss_ref = refs[:3 * n], refs[3 * n:-1], refs[-1]
        for k in range(n):
            w_ref, m_ref, v_ref = ins[3 * k:3 * k + 3]
            for at, gv in grads_of[k](small_ref, chip_ref):
                _adamw_update(w_ref, gv, m_ref, v_ref, *outs[4 * k:4 * k + 4], at=at)
        loss_ref[...] = small_ref[SMALL_MISC:SMALL_MISC + 1, SMALL_LOSS_LANE:SMALL_LOSS_LANE + 1]

    flat = [a for group in zip(weights, ms, vs) for a in group]
    vmem = pl.BlockSpec(memory_space=pltpu.VMEM)
    out = pl.pallas_call(
        body, name="adamw_small",
        out_shape=tuple(jax.ShapeDtypeStruct(w.shape, F32) for w in weights for _ in range(4))
        + (jax.ShapeDtypeStruct((1, 1), F32),),
        in_specs=[pl.BlockSpec(memory_space=pltpu.SMEM)] + [vmem] * (1 + 3 * n), out_specs=(vmem,) * (4 * n + 1),
    )(chip, small_sum, *flat)
    return [tuple(out[4 * k:4 * k + 4]) for k in range(n)], out[-1][0, 0]


def kernel(x, norm_in, w_in, conv_w, attn_sinks, norm_conv_out, norm_attn_out, w_out, norm_final, loss_target, m_norm_in, m_w_in, m_conv_w, m_attn_sinks, m_norm_conv_out, m_norm_attn_out, m_w_out, m_norm_final, v_norm_in, v_w_in, v_conv_w, v_attn_sinks, v_norm_conv_out, v_norm_attn_out, v_w_out, v_norm_final):
    chip = 2 * lax.axis_index("x") + lax.axis_index("y")
    xs, target = x[0], loss_target[0]
    norm_final2 = norm_final.reshape(1, D_MODEL)
    w_in_t, m_w_in_t, v_w_in_t = w_in[0].T, m_w_in[0].T, v_w_in[0].T

    def from_hbm(*arrays):
        return tuple(pltpu.with_memory_space_constraint(a, pltpu.HBM) for a in arrays)

    h, pc, q, kv, ga, oc, ya, oa, probs, sink_probs, wt, cw, wo = _fwd_in(
        xs, norm_in, w_in_t, w_out[0], conv_w.transpose(1, 0, 2), norm_conv_out, attn_sinks, norm_attn_out)
    dx2, doa, gwo, dpc, small = _out_proj_loss(xs, oc, oa, wo, norm_final2, target, pc, cw, norm_conv_out)
    q, kv, ga, ya, doa, probs, gwo, small = from_hbm(q, kv, ga, ya, doa, probs, gwo, small)
    dqg, small, gwo_sh = _attn_bwd(q, kv, ga, ya, doa, probs, sink_probs, norm_attn_out, gwo, small)
    grad_x, small_sum, gwt_sh = _bwd_in(dpc, dqg, h, wt, xs, norm_in, dx2, small)

    (up_w_in, up_w_out), grad_x = _adamw_big(
        [from_hbm(w_in_t, gwt_sh, m_w_in_t, v_w_in_t) + (200,), from_hbm(w_out[0], gwo_sh, m_w_out[0], v_w_out[0]) + (128,)],
        *from_hbm(grad_x))
    up_w_in = tuple(o.T[None] for o in up_w_in)
    up_w_out = tuple(o[None] for o in up_w_out)

    def row_of(r):
        return lambda small_ref, chip_ref: [(..., small_ref[r:r + 1, :])]

    def sink_lanes(small_ref, chip_ref):
        return [(..., small_ref[SMALL_MISC:SMALL_MISC + 1, 0:N_HEADS])]

    def conv_taps(small_ref, chip_ref):
        width = D_CONV // N_CHIPS
        cols = pl.ds(pl.multiple_of(chip_ref[0] * width, width), width)
        return [(k, small_ref[pl.ds(SMALL_CONV_W + k, 1), cols]) for k in range(conv_w.shape[1])]

    def small_view(a):
        return a.transpose(1, 0, 2) if a.ndim == 3 else a.reshape(-1, a.shape[-1])

    small_w = (norm_in, conv_w, attn_sinks, norm_conv_out, norm_attn_out, norm_final)
    small_m = (m_norm_in, m_conv_w, m_attn_sinks, m_norm_conv_out, m_norm_attn_out, m_norm_final)
    small_v = (v_norm_in, v_conv_w, v_attn_sinks, v_norm_conv_out, v_norm_attn_out, v_norm_final)
    small_g = (row_of(SMALL_NORM_IN), conv_taps, sink_lanes, row_of(SMALL_NORM_CONV), row_of(SMALL_NORM_ATTN),
               row_of(SMALL_NORM_FINAL))
    w_views, m_views, v_views = (tuple(small_view(a) for a in group) for group in (small_w, small_m, small_v))
    up_small, loss = _adamw_small(chip.astype(jnp.int32).reshape(1), small_sum, w_views, small_g, m_views, v_views)
    up_small = [tuple(o.transpose(1, 0, 2) if w.ndim == 3 else o.reshape(w.shape) for o in up)
                for up, w in zip(up_small, small_w)]
    up_norm_in, up_conv_w, up_sinks, up_norm_conv, up_norm_attn, up_norm_final = up_small
    updates = (up_norm_in, up_w_in, up_conv_w, up_sinks, up_norm_conv, up_norm_attn, up_w_out, up_norm_final)
    grads_out, deltas, new_m, new_v = zip(*updates)
    return (loss, grad_x[None], *grads_out, *deltas, *new_m, *new_v)
```

```python
import jax
import jax.numpy as jnp
from jax import lax
from jax.experimental import pallas as pl
from jax.experimental.pallas import tpu as pltpu

F32 = jnp.float32
BF16 = jnp.bfloat16
MESH = pl.DeviceIdType.MESH

D_MODEL = 1024
D_CONV = 1024
D_ATTN = 1024
D_KV = 128
D_QG = 2 * D_ATTN + 2 * D_KV
D_MIX = D_CONV + D_ATTN
D_PC = 4 * D_CONV
D_IN_PROJ = D_PC + 2 * D_ATTN + 2 * D_KV
ROW_Q = D_PC
ROW_KV = ROW_Q + D_ATTN
ROW_GA = ROW_KV + 2 * D_KV
N_HEADS = 16
HEAD_DIM = 64
HEADS_PER_KV = 8
BLK = 128
N_CHIPS = 4
RMS_EPS = 1e-5
SCALE = HEAD_DIM ** -0.5
SLOPES = tuple(2.0 ** (-8.0 * (h + 1) / N_HEADS) for h in range(N_HEADS))

ADAM_LR, ADAM_B1, ADAM_B2, ADAM_EPS, ADAM_WD, ADAM_STEP = 0.001, 0.9, 0.999, 1e-08, 0.01, 10

SMALL_ROWS = 8
SMALL_NORM_IN, SMALL_NORM_CONV, SMALL_NORM_ATTN, SMALL_NORM_FINAL, SMALL_CONV_W, SMALL_MISC = 0, 1, 2, 3, 4, 7
SMALL_LOSS_LANE = N_HEADS

TOK_TILE = 256
PC_PIECE = 512
MIB = 1 << 20


def _params(vmem_mib, semantics=None):
    return pltpu.CompilerParams(dimension_semantics=semantics, vmem_limit_bytes=vmem_mib * MIB)


def _nn(a, b):
    return jnp.dot(a, b, preferred_element_type=F32)


def _nt(a, b):
    return lax.dot_general(a, b, (((1,), (1,)), ((), ())), preferred_element_type=F32)


def _tn(a, b):
    return lax.dot_general(a, b, (((0,), (0,)), ((), ())), preferred_element_type=F32)


def _rstd(v):
    return lax.rsqrt(jnp.mean(v * v, axis=-1, keepdims=True) + RMS_EPS)


def _rms_bwd(g, xhat, rstd):
    return rstd * (g - xhat * jnp.mean(g * xhat, axis=-1, keepdims=True))


def _silu_and_grad(g):
    s = jax.nn.sigmoid(g)
    return g * s, s * (1.0 + g * (1.0 - s))


def _resident(shape):
    return pl.BlockSpec(shape, lambda *_: (0,) * len(shape), pipeline_mode=pl.Buffered(1))


def _xor(a, b):
    return a + b - 2 * a * b


def _cast_rows(src_ref, dst_ref, rows=32):
    def step(i, carry):
        sl = pl.ds(pl.multiple_of(i * rows, rows), rows)
        dst_ref[sl, :] = src_ref[sl, :].astype(dst_ref.dtype)
        return carry
    lax.fori_loop(0, src_ref.shape[0] // rows, step, 0)


def _ag_scratch(shard_shape):
    rows, width = shard_shape
    return [pltpu.VMEM((rows, width), F32), pltpu.VMEM((rows, width), BF16), pltpu.VMEM((3, rows // 2, width), BF16),
            pltpu.SemaphoreType.DMA((6,)), pltpu.SemaphoreType.DMA((6,)), pltpu.SemaphoreType.DMA((4,))]


def _ag_stages(sh_ref, out, f32_buf, own, land, send_sems, recv_sems, local_sems):
    rows = sh_ref.shape[0]
    half = rows // 2
    x, y, c = lax.axis_index("x"), lax.axis_index("y"), lax.axis_index("c")
    j = 2 * x + y
    p1 = (_xor(x, c), _xor(y, 1 - c), c)
    p2 = (_xor(x, 1 - c), _xor(y, c), c)
    sib = (x, y, 1 - c)
    j1 = 2 * p1[0] + p1[1]
    j2 = 2 * p2[0] + p2[1]
    j3 = 3 - j

    def rows_of(chip, hf):
        return out.at[pl.ds(pl.multiple_of(chip * rows + hf * half, 16), half), :]

    def rcopy(k, src, dst, to):
        return pltpu.make_async_remote_copy(src_ref=src, dst_ref=dst, send_sem=send_sems.at[k],
                                            recv_sem=recv_sems.at[k], device_id=to, device_id_type=MESH)

    my_half = own.at[pl.ds(pl.multiple_of(c * half, 16), half), :]
    hop1 = rcopy(0, my_half, land.at[0], p1)
    hop2_own = rcopy(1, my_half, land.at[1], p2)
    hop2_fwd = rcopy(2, land.at[0], land.at[2], p2)
    swaps = [rcopy(3 + s, land.at[s], rows_of(chip, c), sib) for s, chip in enumerate((j1, j2, j3))]
    keeps = [pltpu.make_async_copy(land.at[s], rows_of(chip, c), local_sems.at[1 + s]) for s, chip in enumerate((j1, j2, j3))]
    load = pltpu.make_async_copy(sh_ref, f32_buf, local_sems.at[0])
    own_out = pltpu.make_async_copy(own, out.at[pl.ds(pl.multiple_of(j * rows, 16), rows), :], local_sems.at[0])

    def stage_load():
        load.start()

    def stage_send():
        load.wait()
        _cast_rows(f32_buf, own)
        own_out.start()
        hop1.start()

    def stage_forward():
        hop1.wait_recv()
        hop2_own.start()
        hop2_fwd.start()
        swaps[0].start()
        keeps[0].start()

    def stage_publish():
        hop2_own.wait_recv()
        swaps[1].start()
        keeps[1].start()
        hop2_fwd.wait_recv()
        swaps[2].start()
        keeps[2].start()

    def stage_drain():
        for s, chip in enumerate((j2, j1, j3)):
            rcopy(3 + s, my_half, rows_of(chip, 1 - c), sib).wait_recv()
        for cp in [hop1, hop2_own, hop2_fwd] + swaps:
            cp.wait_send()
        for cp in [own_out] + keeps:
            cp.wait()

    return stage_load, stage_send, stage_forward, stage_publish, stage_drain


AG_CAST_ROWS = 400


def _gather_resident(sh_ref, out, f32_buf, send_sems, recv_sems, local_sems, after_first_hop):
    rows = sh_ref.shape[0]
    half = rows // 2
    x, y, c = lax.axis_index("x"), lax.axis_index("y"), lax.axis_index("c")
    j = 2 * x + y
    p1 = (_xor(x, c), _xor(y, 1 - c), c)
    p2 = (_xor(x, 1 - c), _xor(y, c), c)
    sib = (x, y, 1 - c)
    j1 = 2 * p1[0] + p1[1]
    j2 = 2 * p2[0] + p2[1]
    j3 = 3 - j

    def rows_of(chip, hf):
        return out.at[pl.ds(pl.multiple_of(chip * rows + hf * half, 16), half), :]

    def send(k, chip, to):
        return pltpu.make_async_remote_copy(src_ref=rows_of(chip, c), dst_ref=rows_of(chip, c), send_sem=send_sems.at[k],
                                            recv_sem=recv_sems.at[k], device_id=to, device_id_type=MESH)

    per_half = half // AG_CAST_ROWS

    chunks = [c * per_half + k for k in range(per_half)] + [(1 - c) * per_half + k for k in range(per_half)]

    def load(n):
        lo = pl.multiple_of(chunks[n] * AG_CAST_ROWS, 16)
        return pltpu.make_async_copy(sh_ref.at[pl.ds(lo, AG_CAST_ROWS), :], f32_buf.at[n % 2], local_sems.at[n % 2])

    def send_piece(k, chip, q, to):
        at = out.at[pl.ds(pl.multiple_of(chip * rows + c * half + q * (half // 2), 16), half // 2), :]
        return pltpu.make_async_remote_copy(src_ref=at, dst_ref=at, send_sem=send_sems.at[k], recv_sem=recv_sems.at[k],
                                            device_id=to, device_id_type=MESH)

    forwards = [send_piece(2, j1, 0, p2), send_piece(6, j1, 1, p2)]
    sends = [send(0, j, p1), send(1, j, p2)] + forwards + [send(3, j1, sib)]
    load(0).start()
    load(1).start()
    for n in range(len(chunks)):
        load(n).wait()
        lo = pl.multiple_of(chunks[n] * AG_CAST_ROWS, 16)
        _cast_rows(f32_buf.at[n % 2], out.at[pl.ds(pl.multiple_of(j * rows + lo, 16), AG_CAST_ROWS), :], rows=16)
        if n + 2 < len(chunks):
            load(n + 2).start()
        if n == per_half - 1:
            sends[0].start()
            sends[1].start()
    sends[0].wait_recv()
    for cp in sends[2:]:
        cp.start()
    after_first_hop()
    sends[1].wait_recv()
    sends.append(send(4, j2, sib))
    sends[-1].start()
    for q, (k, arrived) in enumerate(zip((5, 7), forwards)):
        arrived.wait_recv()
        sends.append(send_piece(k, j3, q, sib))
        sends[-1].start()
    for k in (3, 4):
        send(k, j, sib).wait_recv()
    for k in (5, 7):
        send_piece(k, j, 0, sib).wait_recv()
    for cp in sends:
        cp.wait_send()


def _fwd_in(x, norm_in, wt_sh, wo_sh, cw_sh, norm_conv_out, sinks, norm_attn_out):
    seq = x.shape[0]
    tile = TOK_TILE
    n_tiles = seq // tile
    stage_steps = (0, n_tiles // 4, (5 * n_tiles) // 8, n_tiles - 1)
    blocks = tile // BLK
    cw_cols = cw_sh.shape[-1]

    def body(x_ref, g_ref, wtsh_ref, wo_ref, cwsh_ref, gn_ref, sink_ref, gna_ref,
             h_ref, pc_ref, q_ref, kv_ref, ga_ref, oc_ref, ya_ref, oa_ref, pr_ref, sp_ref, wt_out, cw_ref, wo_out,
             zbuf, kv_last, wt_ref, f32_buf, cw_land, wt_send, wt_recv, wt_local, cw_send, cw_recv, cw_local, *ag_scratch):
        step = pl.program_id(0)
        to_hbm = pltpu.make_async_copy(wt_ref, wt_out, wt_local.at[0])
        load_wo, *stages = _ag_stages(wo_ref, wo_out, *ag_scratch)

        @pl.when(step == 0)
        def _():
            load_wo()
            zbuf[0:8, :] = jnp.zeros((8, D_CONV), F32)
            kv_last[...] = jnp.zeros_like(kv_last)
            x_i, y_i, c = lax.axis_index("x"), lax.axis_index("y"), lax.axis_index("c")
            j = 2 * x_i + y_i
            p1 = (_xor(x_i, c), _xor(y_i, 1 - c), c)
            p2 = (_xor(x_i, 1 - c), _xor(y_i, c), c)
            j1 = 2 * p1[0] + p1[1]

            def cw_copy(k, src, chip, to):
                return pltpu.make_async_remote_copy(src_ref=src, dst_ref=cw_land.at[chip], send_sem=cw_send.at[k],
                                                    recv_sem=cw_recv.at[k], device_id=to, device_id_type=MESH)

            mine = pltpu.make_async_copy(cwsh_ref, cw_land.at[j], cw_local.at[0])
            mine.start()
            first = cw_copy(0, cwsh_ref, j, p1)
            first.start()
            second = [cw_copy(1, cwsh_ref, j, p2), cw_copy(2, cw_land.at[j1], j1, p2)]

            def cw_second_hop():
                first.wait_recv()
                for cp in second:
                    cp.start()

            _gather_resident(wtsh_ref, wt_ref, f32_buf, wt_send, wt_recv, wt_local, cw_second_hop)
            to_hbm.start()
            for cp in second:
                cp.wait_recv()
            for cp in [first] + second:
                cp.wait_send()
            mine.wait()
            for chip in range(N_CHIPS):
                for tap in range(cw_sh.shape[0]):
                    cw_ref[tap:tap + 1, chip * cw_cols:(chip + 1) * cw_cols] = cw_land[chip, tap]

        for at, stage in zip(stage_steps[:-1], stages[:-1]):
            pl.when(step == at)(stage)

        def attention(b):
            rows = pl.ds(b * BLK, BLK)
            kv_prev = kv_last if b == 0 else kv_ref.at[pl.ds((b - 1) * BLK, BLK), :]
            return _attn_forward(q_ref.at[rows, :], kv_ref.at[rows, :], kv_prev, ga_ref.at[rows, :], sink_ref, gna_ref,
                                 _band_geometry(step * blocks + b), ya_ref.at[rows, :], oa_ref.at[rows, :],
                                 pr_ref.at[rows, :], sp_ref.at[rows, :])

        xv = x_ref[...]
        h = (xv * _rstd(xv) * g_ref[...]).astype(BF16)
        h_ref[...] = h
        q_ref[...] = _nt(h, wt_ref[ROW_Q:ROW_KV, :])
        kv_ref[...] = _nt(h, wt_ref[ROW_KV:ROW_GA, :])
        ga_ref[...] = _nt(h, wt_ref[ROW_GA:D_IN_PROJ, :])
        attention_blocks = [attention(b) for b in range(blocks)]
        for lo in range(0, D_PC, PC_PIECE):
            pc_ref[:, lo:lo + PC_PIECE] = _nt(h, wt_ref[lo:lo + PC_PIECE, :])
            for stages_of_block in attention_blocks:
                next(stages_of_block, None)
        for stages_of_block in attention_blocks:
            for _ in stages_of_block:
                pass
        kv_last[...] = kv_ref[tile - BLK:tile, :]

        cb, _, _, _, _, _, conv = _conv_core(pc_ref, zbuf, cw_ref)
        yc = cb * conv
        silu, _ = _silu_and_grad(pc_ref[:, 3 * D_CONV:4 * D_CONV])
        oc_ref[...] = (yc * _rstd(yc) * gn_ref[...] * silu).astype(BF16)
        zbuf[0:8, :] = zbuf[tile:tile + 8, :]

        @pl.when(step == stage_steps[-1])
        def _():
            stages[-1]()
            to_hbm.wait()

    def row(width):
        return pl.BlockSpec((tile, width), lambda i: (i, 0))

    any_spec = pl.BlockSpec(memory_space=pl.ANY)
    dma = pltpu.SemaphoreType.DMA
    wt_shape = (N_CHIPS * wt_sh.shape[0], wt_sh.shape[1])
    cw_shape = (cw_sh.shape[0], N_CHIPS * cw_cols)
    return pl.pallas_call(
        body, name="fwd_in", grid=(n_tiles,),
        out_shape=(jax.ShapeDtypeStruct((seq, D_MODEL), BF16), jax.ShapeDtypeStruct((seq, D_PC), F32),
                   jax.ShapeDtypeStruct((seq, D_ATTN), F32), jax.ShapeDtypeStruct((seq, 2 * D_KV), F32),
                   jax.ShapeDtypeStruct((seq, D_ATTN), F32), jax.ShapeDtypeStruct((seq, D_CONV), BF16),
                   pltpu.HBM((seq, D_ATTN), F32), pltpu.HBM((seq, D_ATTN), BF16),
                   pltpu.HBM((seq, N_HEADS * BLK), BF16), pltpu.HBM((seq, 128), F32),
                   pltpu.HBM(wt_shape, BF16), jax.ShapeDtypeStruct(cw_shape, F32),
                   jax.ShapeDtypeStruct((N_CHIPS * wo_sh.shape[0], wo_sh.shape[1]), BF16)),
        in_specs=[row(D_MODEL), _resident((1, D_MODEL)), any_spec, any_spec, any_spec, _resident((1, D_CONV)),
                  pl.BlockSpec(memory_space=pltpu.SMEM), _resident((1, D_ATTN))],
        out_specs=(row(D_MODEL), row(D_PC), row(D_ATTN), row(2 * D_KV), row(D_ATTN), row(D_CONV), row(D_ATTN),
                   row(D_ATTN), row(N_HEADS * BLK), row(128), any_spec, pl.BlockSpec(cw_shape, lambda i: (0, 0)),
                   any_spec),
        scratch_shapes=[pltpu.VMEM((tile + 8, D_CONV), F32), pltpu.VMEM((BLK, 2 * D_KV), F32),
                        pltpu.VMEM(wt_shape, BF16), pltpu.VMEM((2, AG_CAST_ROWS, wt_sh.shape[1]), F32),
                        pltpu.VMEM((N_CHIPS,) + cw_sh.shape, F32),
                        dma((8,)), dma((8,)), dma((2,)), dma((3,)), dma((3,)), dma((1,))] + _ag_scratch(wo_sh.shape),
        compiler_params=_params(62, ("arbitrary",)),
    )(x, norm_in, wt_sh, wo_sh, cw_sh, norm_conv_out, sinks, norm_attn_out)


def _conv_core(pc_ref, zbuf, cw_ref):
    tile = pc_ref.shape[0]
    cb = pc_ref[:, 0:D_CONV]
    cc = pc_ref[:, D_CONV:2 * D_CONV]
    cu = pc_ref[:, 2 * D_CONV:3 * D_CONV]
    z = cc * cu
    zbuf[8:tile + 8, :] = z
    z1 = zbuf[7:tile + 7, :]
    z2 = zbuf[6:tile + 6, :]
    conv = cw_ref[0:1, :] * z2 + cw_ref[1:2, :] * z1 + cw_ref[2:3, :] * z
    return cb, cc, cu, z, z1, z2, conv


def _band_geometry(block_index):
    qi = lax.broadcasted_iota(jnp.int32, (BLK, BLK), 0)
    kp = lax.broadcasted_iota(jnp.int32, (BLK, BLK), 1)
    use_cur = kp <= qi
    dist = jnp.where(use_cur, qi - kp, qi - kp + BLK).astype(F32)
    valid = use_cur | (block_index > 0)
    return use_cur, dist, valid


def _block_diag(cur, prev, group):
    lane = lax.broadcasted_iota(jnp.int32, cur.shape, 1)

    def halves(t):
        other = pltpu.roll(t, 64, 1)
        lo, hi = (t, other) if group == 0 else (other, t)
        return jnp.where(lane < 64, lo, 0.0), jnp.where(lane >= 64, hi, 0.0)

    return jnp.concatenate(halves(cur) + halves(prev), axis=0).astype(BF16)


def _merge(s4, use_cur):
    return (jnp.where(use_cur, s4[:, 0:BLK], s4[:, 2 * BLK:3 * BLK]),
            jnp.where(use_cur, s4[:, BLK:2 * BLK], s4[:, 3 * BLK:4 * BLK]))


def _split(a, b, use_cur):
    return jnp.concatenate([jnp.where(use_cur, a, 0.0), jnp.where(use_cur, b, 0.0),
                            jnp.where(use_cur, 0.0, a), jnp.where(use_cur, 0.0, b)], axis=1)


def _softmax_head(s, head, sink, dist, valid):
    sc = jnp.where(valid, s - SLOPES[head] * dist, -jnp.inf)
    m = jnp.maximum(jnp.max(sc, axis=-1, keepdims=True), sink)
    p = jnp.exp(sc - m)
    es = jnp.exp(sink - m)
    inv = 1.0 / (jnp.sum(p, axis=-1, keepdims=True) + es)
    return p * inv, es * inv


def _attn_operands(q_ref, kvc_ref, kvp_ref):
    groups = range(N_HEADS // HEADS_PER_KV)
    kbd = [_block_diag(kvc_ref[:, 0:D_KV], kvp_ref[:, 0:D_KV], g) for g in groups]
    vbd = [_block_diag(kvc_ref[:, D_KV:2 * D_KV], kvp_ref[:, D_KV:2 * D_KV], g) for g in groups]
    qps = [(q_ref[:, j * 128:(j + 1) * 128] * SCALE).astype(BF16) for j in range(N_HEADS // 2)]
    return qps, kbd, vbd


def _attn_forward(q_ref, kvc_ref, kvp_ref, ga_ref, sink_ref, gn_ref, geometry, ya_ref, oa_ref, pr_ref, sp_ref):
    use_cur, dist, valid = geometry
    _, kbd, vbd = operands = _attn_operands(q_ref, kvc_ref, kvp_ref)
    yield
    scores = []
    for j, qp in enumerate(operands[0]):
        scores += _merge(_nt(qp, kbd[j // 4]), use_cur)
    yield
    sinks = [sink_ref[0, h] for h in range(N_HEADS)]
    scores = [jnp.where(valid, s - SLOPES[h] * dist, -jnp.inf) for h, s in enumerate(scores)]
    maxes = [jnp.maximum(jnp.max(s, axis=-1, keepdims=True), sinks[h]) for h, s in enumerate(scores)]
    yield
    exps = [jnp.exp(s - m) for s, m in zip(scores, maxes)]
    sink_exps = [jnp.exp(sinks[h] - m) for h, m in enumerate(maxes)]
    yield
    invs = [1.0 / (jnp.sum(e, axis=-1, keepdims=True) + se) for e, se in zip(exps, sink_exps)]
    probs = [e * inv for e, inv in zip(exps, invs)]
    pr_ref[...] = jnp.concatenate(probs, axis=1).astype(BF16)
    lane = lax.broadcasted_iota(jnp.int32, (BLK, 128), 1)
    sp_ref[...] = sum(jnp.where(lane == h, se * inv, 0.0) for h, (se, inv) in enumerate(zip(sink_exps, invs)))
    yield
    p4s = [_split(probs[2 * j], probs[2 * j + 1], use_cur).astype(BF16) for j in range(N_HEADS // 2)]
    ya = jnp.concatenate([_nn(p4, vbd[j // 4]) for j, p4 in enumerate(p4s)], axis=1)
    ya_ref[...] = ya
    yield
    silu, _ = _silu_and_grad(ga_ref[...])
    oa_ref[...] = (ya * _rstd(ya) * gn_ref[...] * silu).astype(BF16)


def _out_proj_loss(x, oc, oa, wo, norm_final, target, pc, conv_w, norm_conv_out):
    seq = x.shape[0]
    tile = TOK_TILE
    n_tiles = seq // tile

    def body(x_ref, oc_ref, oa_ref, wo_ref, gf_ref, t_ref, pc_ref, hcc_ref, hcu_ref, cw_ref, gn_ref,
             dx2_ref, doa_ref, gwo_ref, dpc_ref, small_ref, loss_acc, zbuf, dbuf):
        step = pl.program_id(0)

        def small_add(row, value):
            small_ref[row:row + 1, :] += jnp.sum(value, axis=0, keepdims=True)

        @pl.when(step == 0)
        def _():
            gwo_ref[...] = jnp.zeros_like(gwo_ref)
            small_ref[...] = jnp.zeros_like(small_ref)
            loss_acc[...] = jnp.zeros_like(loss_acc)
            dbuf[tile:tile + 8, :] = jnp.zeros((8, D_CONV), F32)

        oc, oa = oc_ref[...], oa_ref[...]
        x2 = x_ref[...] + _nn(oc, wo_ref[0:D_CONV, :]) + _nn(oa, wo_ref[D_CONV:D_MIX, :])
        r = _rstd(x2)
        xhat = x2 * r
        err = xhat * gf_ref[...] - t_ref[...]
        loss_acc[...] += jnp.sum(err * err, axis=0, keepdims=True) * (0.5 / D_MODEL)
        dy = err * (1.0 / D_MODEL)
        small_add(SMALL_NORM_FINAL, dy * xhat)
        dx2 = _rms_bwd(dy * gf_ref[...], xhat, r)
        dx2_ref[...] = dx2
        db = dx2.astype(BF16)
        do = _nt(db, wo_ref[0:D_CONV, :])
        doa_ref[...] = _nt(db, wo_ref[D_CONV:D_MIX, :])
        gwo_ref[0:D_CONV, :] += _tn(oc, db)
        gwo_ref[D_CONV:D_MIX, :] += _tn(oa, db)

        is_first_tile = step == n_tiles - 1
        zbuf[0:8, :] = jnp.where(is_first_tile, 0.0, hcc_ref[...] * hcu_ref[...])
        cb, cc, cu, z, z1, z2, conv = _conv_core(pc_ref, zbuf, cw_ref)
        silu, dsilu = _silu_and_grad(pc_ref[:, 3 * D_CONV:4 * D_CONV])
        yc = cb * conv
        rc = _rstd(yc)
        chat = yc * rc
        dn = do * silu
        dpc_ref[:, 3 * D_CONV:4 * D_CONV] = (do * (chat * gn_ref[...]) * dsilu).astype(BF16)
        small_add(SMALL_NORM_CONV, dn * chat)
        dyc = _rms_bwd(dn * gn_ref[...], chat, rc)
        dpc_ref[:, 0:D_CONV] = (dyc * conv).astype(BF16)
        dconv = dyc * cb
        small_add(SMALL_CONV_W, dconv * z2)
        small_add(SMALL_CONV_W + 1, dconv * z1)
        small_add(SMALL_CONV_W + 2, dconv * z)
        dbuf[0:tile, :] = dconv
        dz = cw_ref[2:3, :] * dconv + cw_ref[1:2, :] * dbuf[1:tile + 1, :] + cw_ref[0:1, :] * dbuf[2:tile + 2, :]
        dpc_ref[:, D_CONV:2 * D_CONV] = (dz * cu).astype(BF16)
        dpc_ref[:, 2 * D_CONV:3 * D_CONV] = (dz * cc).astype(BF16)
        dbuf[tile:tile + 8, :] = dbuf[0:8, :]

        @pl.when(step == n_tiles - 1)
        def _():
            lane = lax.broadcasted_iota(jnp.int32, (1, D_MODEL), 1)
            small_ref[SMALL_MISC:SMALL_MISC + 1, :] = jnp.where(lane == SMALL_LOSS_LANE, jnp.sum(loss_acc[...]), 0.0)

    def rev(i):
        return n_tiles - 1 - i

    def row(width):
        return pl.BlockSpec((tile, width), lambda i: (rev(i), 0))

    def halo(col_block):
        return pl.BlockSpec((8, D_CONV), lambda i: (jnp.maximum(rev(i) * (tile // 8) - 1, 0), col_block))

    def const(shape):
        return pl.BlockSpec(shape, lambda i: (0, 0))

    return pl.pallas_call(
        body, name="out_proj_loss", grid=(n_tiles,),
        out_shape=(jax.ShapeDtypeStruct((seq, D_MODEL), F32), jax.ShapeDtypeStruct((seq, D_ATTN), F32),
                   jax.ShapeDtypeStruct((D_MIX, D_MODEL), F32), jax.ShapeDtypeStruct((seq, D_PC), BF16),
                   jax.ShapeDtypeStruct((SMALL_ROWS, D_MODEL), F32)),
        in_specs=[row(D_MODEL), row(D_CONV), row(D_ATTN), _resident(wo.shape), _resident((1, D_MODEL)), row(D_MODEL),
                  row(D_PC), halo(1), halo(2), _resident(conv_w.shape), _resident((1, D_CONV))],
        out_specs=(row(D_MODEL), row(D_ATTN), const((D_MIX, D_MODEL)), row(D_PC), const((SMALL_ROWS, D_MODEL))),
        scratch_shapes=[pltpu.VMEM((1, D_MODEL), F32), pltpu.VMEM((tile + 8, D_CONV), F32),
                        pltpu.VMEM((tile + 8, D_CONV), F32)],
        compiler_params=_params(62, ("arbitrary",)),
    )(x, oc, oa, wo, norm_final, target, pc, pc, pc, conv_w, norm_conv_out)


def _attn_bwd(q, kv, ga, ya, doa, probs, sink_probs, norm_attn_out, gwo, small):
    seq = q.shape[0]
    per_step = 2
    n_steps = seq // (per_step * BLK)
    stage_steps = (0, 1, n_steps // 4, (5 * n_steps) // 8, n_steps - 1, n_steps - 1)

    def body(q_ref, kvc_ref, kvp_ref, ga_ref, ya_ref, doa_ref, pr_ref, sp_ref, gn_ref, gwo_ref, small_ref,
             dqg_ref, small_out, gwo_sh, gna_ref, gs_ref, carry, dya_buf, *rs_scratch):
        step = pl.program_id(0)
        rs_stages = _rs_wout_stages(gwo_ref, gwo_sh, *rs_scratch)
        for at, stage in zip(stage_steps[:-1], rs_stages[:-1]):
            pl.when(step == at)(stage)

        @pl.when(step == 0)
        def _():
            gna_ref[...] = jnp.zeros_like(gna_ref)
            gs_ref[...] = jnp.zeros_like(gs_ref)
            carry[...] = jnp.zeros_like(carry)

        lane = lax.broadcasted_iota(jnp.int32, (BLK, 128), 1)

        def fold(bd):
            return (jnp.where(lane < 64, bd[0:BLK], 0.0) + jnp.where(lane >= 64, bd[BLK:2 * BLK], 0.0),
                    jnp.where(lane < 64, bd[2 * BLK:3 * BLK], 0.0) + jnp.where(lane >= 64, bd[3 * BLK:4 * BLK], 0.0))

        def one_block(b):
            rows = slice(b * BLK, (b + 1) * BLK)
            kv_prev = kvp_ref if b == 0 else kvc_ref.at[(b - 1) * BLK:b * BLK, :]
            ya = ya_ref[rows, :]
            r = _rstd(ya)
            xhat = ya * r
            silu, dsilu = _silu_and_grad(ga_ref[rows, :])
            do = doa_ref[rows, :]
            dn = do * silu
            dqg_ref[rows, D_ATTN:2 * D_ATTN] = (do * (xhat * gn_ref[...]) * dsilu).astype(BF16)
            gna_ref[...] += jnp.sum(dn * xhat, axis=0, keepdims=True)
            dya_buf[rows, :] = _rms_bwd(dn * gn_ref[...], xhat, r).astype(BF16)

            use_cur = _band_geometry(per_step * (n_steps - 1 - step) + b)[0]
            pairs = range(N_HEADS // 2)
            qps, kbd, vbd = _attn_operands(q_ref.at[rows, :], kvc_ref.at[rows, :], kv_prev)
            probs = [pr_ref[rows, h * BLK:(h + 1) * BLK].astype(F32) for h in range(N_HEADS)]
            dyps = [dya_buf[rows, j * 128:(j + 1) * 128] for j in pairs]
            dps = []
            for j in pairs:
                dps += _merge(_nt(dyps[j], vbd[j // 4]), use_cur)
            deltas = [jnp.sum(p * dp, axis=-1, keepdims=True) for p, dp in zip(probs, dps)]
            dss = [p * (dp - delta) for p, dp, delta in zip(probs, dps, deltas)]
            delta_lanes = sum(jnp.where(lane == h, deltas[h], 0.0) for h in range(N_HEADS))
            gs_ref[...] -= jnp.sum(sp_ref[rows, :] * delta_lanes, axis=0, keepdims=True)
            ds4s = [_split(dss[2 * j], dss[2 * j + 1], use_cur).astype(BF16) for j in pairs]
            p4s = [_split(probs[2 * j], probs[2 * j + 1], use_cur).astype(BF16) for j in pairs]
            dqg_ref[rows, 0:D_ATTN] = jnp.concatenate([_nn(ds4s[j], kbd[j // 4]) * SCALE for j in pairs], axis=1).astype(BF16)
            sums = []
            for group in range(N_HEADS // HEADS_PER_KV):
                acc = [jnp.zeros((BLK, 128), F32) for _ in range(4)]
                for j in range(group * 4, group * 4 + 4):
                    for slot, part in enumerate(fold(_tn(ds4s[j], qps[j])) + fold(_tn(p4s[j], dyps[j]))):
                        acc[slot] = acc[slot] + part
                sums.append([a + pltpu.roll(a, 64, 1) for a in acc])
            dk_cur, dk_prev, dv_cur, dv_prev = (jnp.where(lane < 64, lo, hi) for lo, hi in zip(sums[0], sums[1]))
            dqg_ref[rows, 2 * D_ATTN:2 * D_ATTN + 2 * D_KV] = (jnp.concatenate([dk_cur, dv_cur], axis=1) + carry[...]).astype(BF16)
            carry[...] = jnp.concatenate([dk_prev, dv_prev], axis=1)

        for b in reversed(range(per_step)):
            one_block(b)

        @pl.when(step == n_steps - 1)
        def _():
            small_out[...] = small_ref[...]
            small_out[SMALL_NORM_ATTN:SMALL_NORM_ATTN + 1, :] = gna_ref[...]
            small_out[SMALL_MISC:SMALL_MISC + 1, 0:128] = small_ref[SMALL_MISC:SMALL_MISC + 1, 0:128] + gs_ref[...]

        pl.when(step == stage_steps[-1])(rs_stages[-1])

    def rows(width):
        return pl.BlockSpec((per_step * BLK, width), lambda i: (n_steps - 1 - i, 0))

    kv_prev = pl.BlockSpec((BLK, 2 * D_KV), lambda i: (jnp.maximum(per_step * (n_steps - 1 - i) - 1, 0), 0))
    any_spec = pl.BlockSpec(memory_space=pl.ANY)
    return pl.pallas_call(
        body, name="attn_bwd", grid=(n_steps,),
        out_shape=(jax.ShapeDtypeStruct((seq, D_QG), BF16), pltpu.HBM(small.shape, F32),
                   pltpu.HBM((gwo.shape[0] // N_CHIPS, gwo.shape[1]), F32)),
        in_specs=[rows(D_ATTN), rows(2 * D_KV), kv_prev, rows(D_ATTN), rows(D_ATTN), rows(D_ATTN),
                  rows(N_HEADS * BLK), rows(128), _resident((1, D_ATTN)), any_spec, _resident(small.shape)],
        out_specs=(rows(D_QG), pl.BlockSpec(small.shape, lambda i: (0, 0)), any_spec),
        scratch_shapes=[pltpu.VMEM((1, D_ATTN), F32), pltpu.VMEM((1, 128), F32),
                        pltpu.VMEM((BLK, 2 * D_KV), F32), pltpu.VMEM((per_step * BLK, D_ATTN), BF16)] + _rs_wout_scratch(gwo.shape),
        compiler_params=_params(44, ("arbitrary",)),
    )(q, kv, kv, ga, ya, doa, probs, sink_probs, norm_attn_out, gwo, small)


GBLK = 256
GSUB = 64
PAIR_RING = 4
LAG_PAIR, LAG_HOP1, LAG_HOP2 = 1, 7, 14


def _bwd_in(dpc, dqg, h, wt, x, norm_in, dx2, small):
    seq = x.shape[0]
    n_blk = D_IN_PROJ // GBLK
    per_chip = n_blk // N_CHIPS
    n_slots = (n_blk + 1) // 2
    n_sub = GBLK // GSUB
    chip_rows = D_IN_PROJ // N_CHIPS
    tile = TOK_TILE
    n_tiles = seq // tile
    n_steps = n_blk + max(n_tiles, LAG_HOP2)
    chunk = min(seq, 512)
    blk_q, blk_kv, blk_ga = ROW_Q // GBLK, ROW_KV // GBLK, ROW_GA // GBLK

    def block_of(i):
        k = i % N_CHIPS
        robin = per_chip * ((k % 2) * 2 + k // 2) + i // N_CHIPS
        if isinstance(i, int):
            return robin if i < per_chip * N_CHIPS else i
        return jnp.where(i < per_chip * N_CHIPS, robin, i)

    def owner_of(i):
        return (i // N_CHIPS) % 2

    def slot_of(i):
        return (i // (2 * N_CHIPS)) * N_CHIPS + i % N_CHIPS

    def body(dpc_ref, dqg_ref, wt_ref, h_ref, x_ref, g_ref, dx2_ref, small_ref, gx_ref, small_sum, gwt_sh,
             dh_acc, gni, keep, pbuf, xbuf, land, land2, small_land,
             pair_send, pair_recv, h1_send, h1_recv, h2_send, h2_recv, sw_send, sw_recv, sm_send, sm_recv, out_sem):
        step = pl.program_id(0)
        x_i, y_i, c = lax.axis_index("x"), lax.axis_index("y"), lax.axis_index("c")
        me = 4 * x_i + 2 * y_i + c
        j = 2 * x_i + y_i
        pa = (_xor(x_i, 1 - c), _xor(y_i, c), c)
        pb = (_xor(x_i, c), _xor(y_i, 1 - c), c)
        sib = (x_i, y_i, 1 - c)
        ja = 2 * pa[0] + pa[1]
        jb = 2 * pb[0] + pb[1]
        jd = 3 - j

        def remote(src, dst, send, recv, to):
            return pltpu.make_async_remote_copy(src_ref=src, dst_ref=dst, send_sem=send, recv_sem=recv,
                                                device_id=to, device_id_type=MESH)

        def piece(ref, slot, u, n):
            return ref.at[slot, pl.ds(u * GSUB, n * GSUB), :]

        def chip_rows_at(ref, local, n):
            return ref.at[pl.ds(pl.multiple_of(local, GSUB), n * GSUB), :]

        def pair_copy(i):
            slot = slot_of(i)
            return remote(pbuf.at[i % PAIR_RING], land.at[slot], pair_send.at[slot], pair_recv.at[slot], sib)

        def h1_copy(slot, u, n):
            k = slot * n_sub + u
            return remote(piece(xbuf, slot, u, n), piece(xbuf, slot, u, n), h1_send.at[k], h1_recv.at[k], pa)

        def h2_copy(slot, u, n, local):
            k = slot * n_sub + u
            return remote(piece(xbuf, slot, u, n), chip_rows_at(land2, local, n), h2_send.at[k], h2_recv.at[k], pb)

        def sw_copy(slot, u, n, local):
            k = slot * n_sub + u
            return remote(piece(keep, slot, u, n), chip_rows_at(gwt_sh, local, n), sw_send.at[k], sw_recv.at[k], sib)

        def own_copy(slot, u, n, local):
            return pltpu.make_async_copy(piece(keep, slot, u, n), chip_rows_at(gwt_sh, local, n), out_sem.at[slot * n_sub + u])

        def owned(i):
            return (i >= 0) & (i < n_blk) & (owner_of(i) == c)

        def chip_of(blk, u):
            row = blk * GBLK + u * GSUB
            chip = row // chip_rows
            return chip, row - chip * chip_rows

        def pieces(blk):
            first, local = chip_of(blk, 0)
            whole = first == chip_of(blk, n_sub - 1)[0]
            if isinstance(blk, int):
                return [(True, 0, n_sub, first, local)] if whole else [(True, u, 1) + chip_of(blk, u) for u in range(n_sub)]
            return [(whole, 0, n_sub, first, local)] + [(jnp.logical_not(whole), u, 1) + chip_of(blk, u) for u in range(n_sub)]

        @pl.when(step == 0)
        def _():
            dh_acc[...] = jnp.zeros_like(dh_acc)
            gni[...] = jnp.zeros_like(gni)

        @pl.when(step < n_blk)
        def _():
            from_pc = block_of(step) < blk_q
            block = _tn(jnp.where(from_pc, dpc_ref[...], dqg_ref[...]), h_ref[...])
            for t in range(0, seq, chunk):
                d = jnp.where(from_pc, dpc_ref[t:t + chunk, :], dqg_ref[t:t + chunk, :])
                dh_acc[t:t + chunk, :] += _nn(d, wt_ref[...])

            @pl.when(owner_of(step) == c)
            def _():
                keep[slot_of(step)] = block

            @pl.when(owner_of(step) != c)
            def _():
                @pl.when(step >= 2 * PAIR_RING)
                def _():
                    pair_copy(step - 2 * PAIR_RING).wait_send()
                pbuf[step % PAIR_RING] = block.astype(BF16)
                pair_copy(step).start()

        i1 = step - LAG_PAIR

        @pl.when(owned(i1))
        def _():
            slot = slot_of(i1)
            pair_copy(i1).wait_recv()
            _accumulate(keep.at[slot], land.at[slot])
            for cond, u, n, chip, _ in pieces(block_of(i1)):
                @pl.when(cond & ((chip == ja) | (chip == jd)))
                def _(u=u, n=n):
                    _cast_rows(piece(keep, slot, u, n), piece(xbuf, slot, u, n))
                    h1_copy(slot, u, n).start()

        i2 = step - LAG_HOP1

        @pl.when(owned(i2))
        def _():
            slot = slot_of(i2)
            for cond, u, n, chip, local in pieces(block_of(i2)):
                @pl.when(cond & ((chip == j) | (chip == jb)))
                def _(u=u, n=n, chip=chip, local=local):
                    h1_copy(slot, u, n).wait_recv()
                    _accumulate(piece(keep, slot, u, n), piece(xbuf, slot, u, n))

                    @pl.when(chip == jb)
                    def _():
                        _cast_rows(piece(keep, slot, u, n), piece(xbuf, slot, u, n))
                        h2_copy(slot, u, n, local).start()

                @pl.when(cond & ((chip == ja) | (chip == jd)))
                def _(u=u, n=n):
                    h1_copy(slot, u, n).wait_send()

        i3 = step - LAG_HOP2

        @pl.when(owned(i3))
        def _():
            slot = slot_of(i3)
            for cond, u, n, chip, local in pieces(block_of(i3)):
                @pl.when(cond & (chip == j))
                def _(u=u, n=n, local=local):
                    h2_copy(slot, u, n, local).wait_recv()
                    _accumulate(piece(keep, slot, u, n), chip_rows_at(land2, local, n))
                    own_copy(slot, u, n, local).start()
                    sw_copy(slot, u, n, local).start()

                @pl.when(cond & (chip == jb))
                def _(u=u, n=n, local=local):
                    h2_copy(slot, u, n, local).wait_send()

        e = step - n_blk

        @pl.when((e >= 0) & (e < n_tiles))
        def _():
            dh = dh_acc[pl.ds(pl.multiple_of(e * tile, tile), tile), :]
            xv = x_ref[...]
            r = _rstd(xv)
            xhat = xv * r
            gni[...] += jnp.sum(dh * xhat, axis=0, keepdims=True)
            gx_ref[...] = _rms_bwd(dh * g_ref[...], xhat, r) + dx2_ref[...]

        others = [(dx, dy, dc) for dx in (0, 1) for dy in (0, 1) for dc in (0, 1)][1:]
        sends = [remote(small_land.at[me], small_land.at[me], sm_send.at[k], sm_recv.at[k],
                        (_xor(x_i, dx), _xor(y_i, dy), _xor(c, dc))) for k, (dx, dy, dc) in enumerate(others)]

        @pl.when(step == min(n_blk + n_tiles, n_steps - 1))
        def _():
            small_land[me] = small_ref[...]
            small_land[me, SMALL_NORM_IN:SMALL_NORM_IN + 1, :] = gni[...]
            for cp in sends:
                cp.start()

        @pl.when(step == n_steps - 1)
        def _():
            for i in range(n_blk):
                if i + 2 * PAIR_RING >= n_blk:
                    @pl.when(owner_of(i) != c)
                    def _(i=i):
                        pair_copy(i).wait_send()
                for _, u, n, chip, local in pieces(block_of(i)):
                    @pl.when((j == chip) & (c == owner_of(i)))
                    def _(i=i, u=u, n=n, local=local):
                        sw_copy(slot_of(i), u, n, local).wait_send()
                        own_copy(slot_of(i), u, n, local).wait()

                    @pl.when((j == chip) & (c != owner_of(i)))
                    def _(i=i, u=u, n=n, local=local):
                        sw_copy(slot_of(i), u, n, local).wait_recv()
            for cp in sends:
                cp.wait_recv()
            total = small_land[0]
            for dev in range(1, 8):
                total = total + small_land[dev]
            small_sum[...] = total
            for cp in sends:
                cp.wait_send()

    def blk_at(i):
        return block_of(jnp.clip(i, 0, n_blk - 1))

    last_pc_step = max(i for i in range(n_blk) if block_of(i) < blk_q)

    def next_block(i, in_pc):
        i = jnp.clip(i, 0, n_blk - 1)
        step = jnp.full_like(i, last_pc_step if in_pc else n_blk - 1)
        for ahead in reversed(range(N_CHIPS)):
            cand = jnp.minimum(i + ahead, n_blk - 1)
            step = jnp.where((block_of(cand) < blk_q) == in_pc, cand, step)
        return block_of(step)

    def dqg_block(i):
        b = next_block(i, False)
        q_blk = jnp.clip(b - blk_q, 0, blk_kv - blk_q - 1)
        ga_blk = (D_ATTN // GBLK) + jnp.clip(b - blk_ga, 0, n_blk - blk_ga - 1)
        return jnp.where(b < blk_kv, q_blk, jnp.where(b == blk_kv, 2 * D_ATTN // GBLK, ga_blk))

    def tok(i):
        return (jnp.clip(i - n_blk, 0, n_tiles - 1), 0)

    n_piece = n_slots * n_sub
    dma = pltpu.SemaphoreType.DMA
    return pl.pallas_call(
        body, name="bwd_in", grid=(n_steps,),
        out_shape=(jax.ShapeDtypeStruct((seq, D_MODEL), F32), jax.ShapeDtypeStruct(small.shape, F32),
                   jax.ShapeDtypeStruct((chip_rows, D_MODEL), F32)),
        in_specs=[pl.BlockSpec((seq, GBLK), lambda i: (0, next_block(i, True))),
                  pl.BlockSpec((seq, GBLK), lambda i: (0, dqg_block(i))),
                  pl.BlockSpec((GBLK, D_MODEL), lambda i: (blk_at(i), 0)),
                  _resident(h.shape),
                  pl.BlockSpec((tile, D_MODEL), tok), _resident((1, D_MODEL)), pl.BlockSpec((tile, D_MODEL), tok),
                  _resident(small.shape)],
        out_specs=(pl.BlockSpec((tile, D_MODEL), tok), pl.BlockSpec(small.shape, lambda i: (0, 0)),
                   pl.BlockSpec(memory_space=pl.ANY)),
        scratch_shapes=[pltpu.VMEM((seq, D_MODEL), F32), pltpu.VMEM((1, D_MODEL), F32),
                        pltpu.VMEM((n_slots, GBLK, D_MODEL), F32), pltpu.VMEM((PAIR_RING, GBLK, D_MODEL), BF16),
                        pltpu.VMEM((n_slots, GBLK, D_MODEL), BF16), pltpu.VMEM((n_slots, GBLK, D_MODEL), BF16),
                        pltpu.VMEM((chip_rows, D_MODEL), BF16), pltpu.VMEM((8,) + small.shape, F32),
                        dma((n_slots,)), dma((n_slots,)), dma((n_piece,)), dma((n_piece,)), dma((n_piece,)),
                        dma((n_piece,)), dma((n_piece,)), dma((n_piece,)), dma((7,)), dma((7,)), dma((n_piece,))],
        compiler_params=_params(62, ("arbitrary",)),
    )(dpc, dqg, wt, h, x, norm_in, dx2, small)


def _accumulate(dst_ref, src_ref, rows=16):
    def step(i, carry):
        sl = pl.ds(pl.multiple_of(i * rows, rows), rows)
        dst_ref[sl, :] = dst_ref[sl, :] + src_ref[sl, :].astype(F32)
        return carry
    lax.fori_loop(0, dst_ref.shape[0] // rows, step, 0)


def _rs_wout_scratch(gwo_shape):
    o_half, width = gwo_shape[0] // N_CHIPS // 2, gwo_shape[1]
    return [pltpu.VMEM((4, o_half, width), F32), pltpu.VMEM((4, o_half, width), BF16),
            pltpu.VMEM((4, o_half, width), BF16), pltpu.VMEM((2, o_half, width), BF16),
            pltpu.VMEM((o_half, width), BF16),
            pltpu.SemaphoreType.DMA((8,)), pltpu.SemaphoreType.DMA((8,)), pltpu.SemaphoreType.DMA((4,))]


def _rs_wout_stages(gwo_ref, gwo_sh, acc_o, sb_o, r1o, r2o, r3o, send_sems, recv_sems, local_sems):
    o_rows = gwo_ref.shape[0] // N_CHIPS
    o_half = o_rows // 2
    x, y, c = lax.axis_index("x"), lax.axis_index("y"), lax.axis_index("c")
    j = 2 * x + y
    pa = (_xor(x, 1 - c), _xor(y, c), c)
    pb = (_xor(x, c), _xor(y, 1 - c), c)
    sib = (x, y, 1 - c)
    ja = 2 * pa[0] + pa[1]
    jb = 2 * pb[0] + pb[1]
    jd = 3 - j
    order = (ja, jd, jb, j)
    sib_order = (jb, jd, ja, j)

    def rcopy(k, src, dst, to):
        return pltpu.make_async_remote_copy(src_ref=src, dst_ref=dst, send_sem=send_sems.at[k],
                                            recv_sem=recv_sems.at[k], device_id=to, device_id_type=MESH)

    def o_rows_of(chip, half):
        return gwo_ref.at[pl.ds(pl.multiple_of(chip * o_rows + half * o_half, 8), o_half), :]

    def loads(chips, half):
        return [pltpu.make_async_copy(o_rows_of(chip, half), acc_o.at[s], local_sems.at[s]) for s, chip in enumerate(chips)]

    def pair_send(s):
        return rcopy(s, sb_o.at[s], r1o.at[s], sib)

    def out_half(half):
        return gwo_sh.at[pl.ds(pl.multiple_of(half * o_half, 8), o_half), :]

    hop1 = [rcopy(4 + s, sb_o.at[s], r2o.at[s], pa) for s in range(2)]
    hop2 = rcopy(6, sb_o.at[2], r3o, pb)
    swap = rcopy(7, acc_o.at[3], out_half(c), sib)
    mine = pltpu.make_async_copy(acc_o.at[3], out_half(c), local_sems.at[0])

    def resend(s, copy):
        pair_send(s).wait_send()
        _cast_rows(acc_o.at[s], sb_o.at[s])
        copy.start()

    def stage_load():
        for cp in loads(sib_order, 1 - c):
            cp.start()

    def stage_pair():
        for s, (cp, mine_in) in enumerate(zip(loads(sib_order, 1 - c), loads(order, c))):
            cp.wait()
            _cast_rows(acc_o.at[s], sb_o.at[s])
            pair_send(s).start()
            mine_in.start()

    def stage_hop1():
        for s, cp in enumerate(loads(order, c)):
            cp.wait()
            pair_send(s).wait_recv()
            _accumulate(acc_o.at[s], r1o.at[s])
            if s < 2:
                resend(s, hop1[s])

    def stage_hop2():
        hop1[1].wait_recv()
        _accumulate(acc_o.at[2], r2o.at[1])
        resend(2, hop2)
        hop1[0].wait_recv()
        _accumulate(acc_o.at[3], r2o.at[0])

    def stage_final():
        hop2.wait_recv()
        _accumulate(acc_o.at[3], r3o)
        mine.start()
        swap.start()

    def stage_drain():
        rcopy(7, acc_o.at[3], out_half(1 - c), sib).wait_recv()
        for cp in [pair_send(3)] + hop1 + [hop2, swap]:
            cp.wait_send()
        mine.wait()

    return stage_load, stage_pair, stage_hop1, stage_hop2, stage_final, stage_drain


def _adamw_big(groups, passed):
    arrays = [a for group in groups for a in group[:4]]
    steps = [group[0].shape[0] // group[4] for group in groups]
    first = [sum(steps[:k]) for k in range(len(steps) + 1)]
    n = len(arrays)

    def body(*refs):
        ins, passed_in, outs, passed_out = refs[:n], refs[n], refs[n + 1:2 * n + 1], refs[2 * n + 1]
        i = pl.program_id(0)
        for k in range(len(groups)):
            @pl.when((i >= first[k]) & (i < first[k + 1]))
            def _(k=k):
                w_ref, g_ref, m_ref, v_ref = ins[4 * k:4 * k + 4]
                _adamw_update(w_ref, g_ref[...], m_ref, v_ref, *outs[4 * k:4 * k + 4])

        @pl.when(i < steps[0])
        def _():
            passed_out[...] = passed_in[...]

    def spec(k, rows, width):
        return pl.BlockSpec((rows, width), lambda i: (jnp.clip(i - first[k], 0, steps[k] - 1), 0))

    specs = [spec(k, group[4], group[0].shape[1]) for k, group in enumerate(groups) for _ in range(4)]
    specs.append(spec(0, passed.shape[0] // steps[0], passed.shape[1]))
    out = pl.pallas_call(
        body, name="adamw_big", grid=(first[-1],),
        out_shape=tuple(jax.ShapeDtypeStruct(a.shape, a.dtype) for a in arrays + [passed]),
        in_specs=specs, out_specs=tuple(specs),
        compiler_params=_params(40, ("arbitrary",)),
    )(*arrays, passed)
    return [tuple(out[4 * k:4 * k + 4]) for k in range(len(groups))], out[-1]


def _adamw_update(w_ref, gv, m_ref, v_ref, go_ref, d_ref, nm_ref, nv_ref, at=...):
    go_ref[at] = gv
    nm = ADAM_B1 * m_ref[at] + (1.0 - ADAM_B1) * gv
    nv = ADAM_B2 * v_ref[at] + (1.0 - ADAM_B2) * (gv * gv)
    m_hat = nm / (1.0 - ADAM_B1 ** ADAM_STEP)
    v_hat = nv / (1.0 - ADAM_B2 ** ADAM_STEP)
    d_ref[at] = -ADAM_LR * (m_hat / (jnp.sqrt(v_hat) + ADAM_EPS) + ADAM_WD * w_ref[at])
    nm_ref[at] = nm
    nv_ref[at] = nv


def _adamw_small(chip, small_sum, weights, grads_of, ms, vs):
    n = len(weights)

    def body(chip_ref, small_ref, *refs):
        ins, outs, loss_ref = refs[:3 * n], refs[3 * n:-1], refs[-1]
        for k in range(n):
            w_ref, m_ref, v_ref = ins[3 * k:3 * k + 3]
            for at, gv in grads_of[k](small_ref, chip_ref):
                _adamw_update(w_ref, gv, m_ref, v_ref, *outs[4 * k:4 * k + 4], at=at)
        loss_ref[...] = small_ref[SMALL_MISC:SMALL_MISC + 1, SMALL_LOSS_LANE:SMALL_LOSS_LANE + 1]

    flat = [a for group in zip(weights, ms, vs) for a in group]
    vmem = pl.BlockSpec(memory_space=pltpu.VMEM)
    out = pl.pallas_call(
        body, name="adamw_small",
        out_shape=tuple(jax.ShapeDtypeStruct(w.shape, F32) for w in weights for _ in range(4))
        + (jax.ShapeDtypeStruct((1, 1), F32),),
        in_specs=[pl.BlockSpec(memory_space=pltpu.SMEM)] + [vmem] * (1 + 3 * n), out_specs=(vmem,) * (4 * n + 1),
    )(chip, small_sum, *flat)
    return [tuple(out[4 * k:4 * k + 4]) for k in range(n)], out[-1][0, 0]


def kernel(x, norm_in, w_in, conv_w, attn_sinks, norm_conv_out, norm_attn_out, w_out, norm_final, loss_target, m_norm_in, m_w_in, m_conv_w, m_attn_sinks, m_norm_conv_out, m_norm_attn_out, m_w_out, m_norm_final, v_norm_in, v_w_in, v_conv_w, v_attn_sinks, v_norm_conv_out, v_norm_attn_out, v_w_out, v_norm_final):
    chip = 2 * lax.axis_index("x") + lax.axis_index("y")
    xs, target = x[0], loss_target[0]
    norm_final2 = norm_final.reshape(1, D_MODEL)
    w_in_t, m_w_in_t, v_w_in_t = w_in[0].T, m_w_in[0].T, v_w_in[0].T

    def from_hbm(*arrays):
        return tuple(pltpu.with_memory_space_constraint(a, pltpu.HBM) for a in arrays)

    h, pc, q, kv, ga, oc, ya, oa, probs, sink_probs, wt, cw, wo = _fwd_in(
        xs, norm_in, w_in_t, w_out[0], conv_w.transpose(1, 0, 2), norm_conv_out, attn_sinks, norm_attn_out)
    dx2, doa, gwo, dpc, small = _out_proj_loss(xs, oc, oa, wo, norm_final2, target, pc, cw, norm_conv_out)
    q, kv, ga, ya, doa, probs, gwo, small = from_hbm(q, kv, ga, ya, doa, probs, gwo, small)
    dqg, small, gwo_sh = _attn_bwd(q, kv, ga, ya, doa, probs, sink_probs, norm_attn_out, gwo, small)
    grad_x, small_sum, gwt_sh = _bwd_in(dpc, dqg, h, wt, xs, norm_in, dx2, small)

    (up_w_in, up_w_out), grad_x = _adamw_big(
        [from_hbm(w_in_t, gwt_sh, m_w_in_t, v_w_in_t) + (200,), from_hbm(w_out[0], gwo_sh, m_w_out[0], v_w_out[0]) + (128,)],
        *from_hbm(grad_x))
    up_w_in = tuple(o.T[None] for o in up_w_in)
    up_w_out = tuple(o[None] for o in up_w_out)

    def row_of(r):
        return lambda small_ref, chip_ref: [(..., small_ref[r:r + 1, :])]

    def sink_lanes(small_ref, chip_ref):
        return [(..., small_ref[SMALL_MISC:SMALL_MISC + 1, 0:N_HEADS])]

    def conv_taps(small_ref, chip_ref):
        width = D_CONV // N_CHIPS
        cols = pl.ds(pl.multiple_of(chip_ref[0] * width, width), width)
        return [(k, small_ref[pl.ds(SMALL_CONV_W + k, 1), cols]) for k in range(conv_w.shape[1])]

    def small_view(a):
        return a.transpose(1, 0, 2) if a.ndim == 3 else a.reshape(-1, a.shape[-1])

    small_w = (norm_in, conv_w, attn_sinks, norm_conv_out, norm_attn_out, norm_final)
    small_m = (m_norm_in, m_conv_w, m_attn_sinks, m_norm_conv_out, m_norm_attn_out, m_norm_final)
    small_v = (v_norm_in, v_conv_w, v_attn_sinks, v_norm_conv_out, v_norm_attn_out, v_norm_final)
    small_g = (row_of(SMALL_NORM_IN), conv_taps, sink_lanes, row_of(SMALL_NORM_CONV), row_of(SMALL_NORM_ATTN),
               row_of(SMALL_NORM_FINAL))
    w_views, m_views, v_views = (tuple(small_view(a) for a in group) for group in (small_w, small_m, small_v))
    up_small, loss = _adamw_small(chip.astype(jnp.int32).reshape(1), small_sum, w_views, small_g, m_views, v_views)
    up_small = [tuple(o.transpose(1, 0, 2) if w.ndim == 3 else o.reshape(w.shape) for o in up)
                for up, w in zip(up_small, small_w)]
    up_norm_in, up_conv_w, up_sinks, up_norm_conv, up_norm_attn, up_norm_final = up_small
    updates = (up_norm_in, up_w_in, up_conv_w, up_sinks, up_norm_conv, up_norm_attn, up_w_out, up_norm_final)
    grads_out, deltas, new_m, new_v = zip(*updates)
    return (loss, grad_x[None], *grads_out, *deltas, *new_m, *new_v)
```

```python
import jax
import jax.numpy as jnp
from jax import lax
from jax.experimental import pallas as pl
from jax.experimental.pallas import tpu as pltpu

F32 = jnp.float32
BF16 = jnp.bfloat16
MESH = pl.DeviceIdType.MESH

D_MODEL = 1024
D_CONV = 1024
D_ATTN = 1024
D_KV = 128
D_QG = 2 * D_ATTN + 2 * D_KV
D_MIX = D_CONV + D_ATTN
D_PC = 4 * D_CONV
D_IN_PROJ = D_PC + 2 * D_ATTN + 2 * D_KV
ROW_Q = D_PC
ROW_KV = ROW_Q + D_ATTN
ROW_GA = ROW_KV + 2 * D_KV
N_HEADS = 16
HEAD_DIM = 64
HEADS_PER_KV = 8
BLK = 128
N_CHIPS = 4
RMS_EPS = 1e-5
SCALE = HEAD_DIM ** -0.5
SLOPES = tuple(2.0 ** (-8.0 * (h + 1) / N_HEADS) for h in range(N_HEADS))

ADAM_LR, ADAM_B1, ADAM_B2, ADAM_EPS, ADAM_WD, ADAM_STEP = 0.001, 0.9, 0.999, 1e-08, 0.01, 10

SMALL_ROWS = 8
SMALL_NORM_IN, SMALL_NORM_CONV, SMALL_NORM_ATTN, SMALL_NORM_FINAL, SMALL_CONV_W, SMALL_MISC = 0, 1, 2, 3, 4, 7
SMALL_LOSS_LANE = N_HEADS

TOK_TILE = 256
PC_PIECE = 512
MIB = 1 << 20


def _params(vmem_mib, semantics=None):
    return pltpu.CompilerParams(dimension_semantics=semantics, vmem_limit_bytes=vmem_mib * MIB)


def _nn(a, b):
    return jnp.dot(a, b, preferred_element_type=F32)


def _nt(a, b):
    return lax.dot_general(a, b, (((1,), (1,)), ((), ())), preferred_element_type=F32)


def _tn(a, b):
    return lax.dot_general(a, b, (((0,), (0,)), ((), ())), preferred_element_type=F32)


def _rstd(v):
    return lax.rsqrt(jnp.mean(v * v, axis=-1, keepdims=True) + RMS_EPS)


def _rms_bwd(g, xhat, rstd):
    return rstd * (g - xhat * jnp.mean(g * xhat, axis=-1, keepdims=True))


def _silu_and_grad(g):
    s = jax.nn.sigmoid(g)
    return g * s, s * (1.0 + g * (1.0 - s))


def _resident(shape):
    return pl.BlockSpec(shape, lambda *_: (0,) * len(shape), pipeline_mode=pl.Buffered(1))


def _xor(a, b):
    return a + b - 2 * a * b


def _cast_rows(src_ref, dst_ref, rows=32):
    def step(i, carry):
        sl = pl.ds(pl.multiple_of(i * rows, rows), rows)
        dst_ref[sl, :] = src_ref[sl, :].astype(dst_ref.dtype)
        return carry
    lax.fori_loop(0, src_ref.shape[0] // rows, step, 0)


def _ag_scratch(shard_shape):
    rows, width = shard_shape
    return [pltpu.VMEM((rows, width), F32), pltpu.VMEM((rows, width), BF16), pltpu.VMEM((3, rows // 2, width), BF16),
            pltpu.SemaphoreType.DMA((6,)), pltpu.SemaphoreType.DMA((6,)), pltpu.SemaphoreType.DMA((4,))]


def _ag_stages(sh_ref, out, f32_buf, own, land, send_sems, recv_sems, local_sems):
    rows = sh_ref.shape[0]
    half = rows // 2
    x, y, c = lax.axis_index("x"), lax.axis_index("y"), lax.axis_index("c")
    j = 2 * x + y
    p1 = (_xor(x, c), _xor(y, 1 - c), c)
    p2 = (_xor(x, 1 - c), _xor(y, c), c)
    sib = (x, y, 1 - c)
    j1 = 2 * p1[0] + p1[1]
    j2 = 2 * p2[0] + p2[1]
    j3 = 3 - j

    def rows_of(chip, hf):
        return out.at[pl.ds(pl.multiple_of(chip * rows + hf * half, 16), half), :]

    def rcopy(k, src, dst, to):
        return pltpu.make_async_remote_copy(src_ref=src, dst_ref=dst, send_sem=send_sems.at[k],
                                            recv_sem=recv_sems.at[k], device_id=to, device_id_type=MESH)

    my_half = own.at[pl.ds(pl.multiple_of(c * half, 16), half), :]
    hop1 = rcopy(0, my_half, land.at[0], p1)
    hop2_own = rcopy(1, my_half, land.at[1], p2)
    hop2_fwd = rcopy(2, land.at[0], land.at[2], p2)
    swaps = [rcopy(3 + s, land.at[s], rows_of(chip, c), sib) for s, chip in enumerate((j1, j2, j3))]
    keeps = [pltpu.make_async_copy(land.at[s], rows_of(chip, c), local_sems.at[1 + s]) for s, chip in enumerate((j1, j2, j3))]
    load = pltpu.make_async_copy(sh_ref, f32_buf, local_sems.at[0])
    own_out = pltpu.make_async_copy(own, out.at[pl.ds(pl.multiple_of(j * rows, 16), rows), :], local_sems.at[0])

    def stage_load():
        load.start()

    def stage_send():
        load.wait()
        _cast_rows(f32_buf, own)
        own_out.start()
        hop1.start()

    def stage_forward():
        hop1.wait_recv()
        hop2_own.start()
        hop2_fwd.start()
        swaps[0].start()
        keeps[0].start()

    def stage_publish():
        hop2_own.wait_recv()
        swaps[1].start()
        keeps[1].start()
        hop2_fwd.wait_recv()
        swaps[2].start()
        keeps[2].start()

    def stage_drain():
        for s, chip in enumerate((j2, j1, j3)):
            rcopy(3 + s, my_half, rows_of(chip, 1 - c), sib).wait_recv()
        for cp in [hop1, hop2_own, hop2_fwd] + swaps:
            cp.wait_send()
        for cp in [own_out] + keeps:
            cp.wait()

    return stage_load, stage_send, stage_forward, stage_publish, stage_drain


AG_CAST_ROWS = 400


def _gather_resident(sh_ref, out, f32_buf, send_sems, recv_sems, local_sems, after_first_hop):
    rows = sh_ref.shape[0]
    half = rows // 2
    x, y, c = lax.axis_index("x"), lax.axis_index("y"), lax.axis_index("c")
    j = 2 * x + y
    p1 = (_xor(x, c), _xor(y, 1 - c), c)
    p2 = (_xor(x, 1 - c), _xor(y, c), c)
    sib = (x, y, 1 - c)
    j1 = 2 * p1[0] + p1[1]
    j2 = 2 * p2[0] + p2[1]
    j3 = 3 - j

    def rows_of(chip, hf):
        return out.at[pl.ds(pl.multiple_of(chip * rows + hf * half, 16), half), :]

    def send(k, chip, to):
        return pltpu.make_async_remote_copy(src_ref=rows_of(chip, c), dst_ref=rows_of(chip, c), send_sem=send_sems.at[k],
                                            recv_sem=recv_sems.at[k], device_id=to, device_id_type=MESH)

    per_half = half // AG_CAST_ROWS

    chunks = [c * per_half + k for k in range(per_half)] + [(1 - c) * per_half + k for k in range(per_half)]

    def load(n):
        lo = pl.multiple_of(chunks[n] * AG_CAST_ROWS, 16)
        return pltpu.make_async_copy(sh_ref.at[pl.ds(lo, AG_CAST_ROWS), :], f32_buf.at[n % 2], local_sems.at[n % 2])

    def send_piece(k, chip, q, to):
        at = out.at[pl.ds(pl.multiple_of(chip * rows + c * half + q * (half // 2), 16), half // 2), :]
        return pltpu.make_async_remote_copy(src_ref=at, dst_ref=at, send_sem=send_sems.at[k], recv_sem=recv_sems.at[k],
                                            device_id=to, device_id_type=MESH)

    forwards = [send_piece(2, j1, 0, p2), send_piece(6, j1, 1, p2)]
    sends = [send(0, j, p1), send(1, j, p2)] + forwards + [send(3, j1, sib)]
    load(0).start()
    load(1).start()
    for n in range(len(chunks)):
        load(n).wait()
        lo = pl.multiple_of(chunks[n] * AG_CAST_ROWS, 16)
        _cast_rows(f32_buf.at[n % 2], out.at[pl.ds(pl.multiple_of(j * rows + lo, 16), AG_CAST_ROWS), :], rows=16)
        if n + 2 < len(chunks):
            load(n + 2).start()
        if n == per_half - 1:
            sends[0].start()
            sends[1].start()
    sends[0].wait_recv()
    for cp in sends[2:]:
        cp.start()
    after_first_hop()
    sends[1].wait_recv()
    sends.append(send(4, j2, sib))
    sends[-1].start()
    for q, (k, arrived) in enumerate(zip((5, 7), forwards)):
        arrived.wait_recv()
        sends.append(send_piece(k, j3, q, sib))
        sends[-1].start()
    for k in (3, 4):
        send(k, j, sib).wait_recv()
    for k in (5, 7):
        send_piece(k, j, 0, sib).wait_recv()
    for cp in sends:
        cp.wait_send()


def _fwd_in(x, norm_in, wt_sh, wo_sh, cw_sh, norm_conv_out, sinks, norm_attn_out):
    seq = x.shape[0]
    tile = TOK_TILE
    n_tiles = seq // tile
    stage_steps = (0, n_tiles // 4, (5 * n_tiles) // 8, n_tiles - 1)
    blocks = tile // BLK
    cw_cols = cw_sh.shape[-1]

    def body(x_ref, g_ref, wtsh_ref, wo_ref, cwsh_ref, gn_ref, sink_ref, gna_ref,
             h_ref, pc_ref, q_ref, kv_ref, ga_ref, oc_ref, ya_ref, oa_ref, pr_ref, sp_ref, wt_out, cw_ref, wo_out,
             zbuf, kv_last, wt_ref, f32_buf, cw_land, wt_send, wt_recv, wt_local, cw_send, cw_recv, cw_local, *ag_scratch):
        step = pl.program_id(0)
        to_hbm = pltpu.make_async_copy(wt_ref, wt_out, wt_local.at[0])
        load_wo, *stages = _ag_stages(wo_ref, wo_out, *ag_scratch)

        @pl.when(step == 0)
        def _():
            load_wo()
            zbuf[0:8, :] = jnp.zeros((8, D_CONV), F32)
            kv_last[...] = jnp.zeros_like(kv_last)
            x_i, y_i, c = lax.axis_index("x"), lax.axis_index("y"), lax.axis_index("c")
            j = 2 * x_i + y_i
            p1 = (_xor(x_i, c), _xor(y_i, 1 - c), c)
            p2 = (_xor(x_i, 1 - c), _xor(y_i, c), c)
            j1 = 2 * p1[0] + p1[1]

            def cw_copy(k, src, chip, to):
                return pltpu.make_async_remote_copy(src_ref=src, dst_ref=cw_land.at[chip], send_sem=cw_send.at[k],
                                                    recv_sem=cw_recv.at[k], device_id=to, device_id_type=MESH)

            mine = pltpu.make_async_copy(cwsh_ref, cw_land.at[j], cw_local.at[0])
            mine.start()
            first = cw_copy(0, cwsh_ref, j, p1)
            first.start()
            second = [cw_copy(1, cwsh_ref, j, p2), cw_copy(2, cw_land.at[j1], j1, p2)]

            def cw_second_hop():
                first.wait_recv()
                for cp in second:
                    cp.start()

            _gather_resident(wtsh_ref, wt_ref, f32_buf, wt_send, wt_recv, wt_local, cw_second_hop)
            to_hbm.start()
            for cp in second:
                cp.wait_recv()
            for cp in [first] + second:
                cp.wait_send()
            mine.wait()
            for chip in range(N_CHIPS):
                for tap in range(cw_sh.shape[0]):
                    cw_ref[tap:tap + 1, chip * cw_cols:(chip + 1) * cw_cols] = cw_land[chip, tap]

        for at, stage in zip(stage_steps[:-1], stages[:-1]):
            pl.when(step == at)(stage)

        def attention(b):
            rows = pl.ds(b * BLK, BLK)
            kv_prev = kv_last if b == 0 else kv_ref.at[pl.ds((b - 1) * BLK, BLK), :]
            return _attn_forward(q_ref.at[rows, :], kv_ref.at[rows, :], kv_prev, ga_ref.at[rows, :], sink_ref, gna_ref,
                                 _band_geometry(step * blocks + b), ya_ref.at[rows, :], oa_ref.at[rows, :],
                                 pr_ref.at[rows, :], sp_ref.at[rows, :])

        xv = x_ref[...]
        h = (xv * _rstd(xv) * g_ref[...]).astype(BF16)
        h_ref[...] = h
        q_ref[...] = _nt(h, wt_ref[ROW_Q:ROW_KV, :])
        kv_ref[...] = _nt(h, wt_ref[ROW_KV:ROW_GA, :])
        ga_ref[...] = _nt(h, wt_ref[ROW_GA:D_IN_PROJ, :])
        attention_blocks = [attention(b) for b in range(blocks)]
        for lo in range(0, D_PC, PC_PIECE):
            pc_ref[:, lo:lo + PC_PIECE] = _nt(h, wt_ref[lo:lo + PC_PIECE, :])
            for stages_of_block in attention_blocks:
                next(stages_of_block, None)
        for stages_of_block in attention_blocks:
            for _ in stages_of_block:
                pass
        kv_last[...] = kv_ref[tile - BLK:tile, :]

        cb, _, _, _, _, _, conv = _conv_core(pc_ref, zbuf, cw_ref)
        yc = cb * conv
        silu, _ = _silu_and_grad(pc_ref[:, 3 * D_CONV:4 * D_CONV])
        oc_ref[...] = (yc * _rstd(yc) * gn_ref[...] * silu).astype(BF16)
        zbuf[0:8, :] = zbuf[tile:tile + 8, :]

        @pl.when(step == stage_steps[-1])
        def _():
            stages[-1]()
            to_hbm.wait()

    def row(width):
        return pl.BlockSpec((tile, width), lambda i: (i, 0))

    any_spec = pl.BlockSpec(memory_space=pl.ANY)
    dma = pltpu.SemaphoreType.DMA
    wt_shape = (N_CHIPS * wt_sh.shape[0], wt_sh.shape[1])
    cw_shape = (cw_sh.shape[0], N_CHIPS * cw_cols)
    return pl.pallas_call(
        body, name="fwd_in", grid=(n_tiles,),
        out_shape=(jax.ShapeDtypeStruct((seq, D_MODEL), BF16), jax.ShapeDtypeStruct((seq, D_PC), F32),
                   jax.ShapeDtypeStruct((seq, D_ATTN), F32), jax.ShapeDtypeStruct((seq, 2 * D_KV), F32),
                   jax.ShapeDtypeStruct((seq, D_ATTN), F32), jax.ShapeDtypeStruct((seq, D_CONV), BF16),
                   pltpu.HBM((seq, D_ATTN), F32), pltpu.HBM((seq, D_ATTN), BF16),
                   pltpu.HBM((seq, N_HEADS * BLK), BF16), pltpu.HBM((seq, 128), F32),
                   pltpu.HBM(wt_shape, BF16), jax.ShapeDtypeStruct(cw_shape, F32),
                   jax.ShapeDtypeStruct((N_CHIPS * wo_sh.shape[0], wo_sh.shape[1]), BF16)),
        in_specs=[row(D_MODEL), _resident((1, D_MODEL)), any_spec, any_spec, any_spec, _resident((1, D_CONV)),
                  pl.BlockSpec(memory_space=pltpu.SMEM), _resident((1, D_ATTN))],
        out_specs=(row(D_MODEL), row(D_PC), row(D_ATTN), row(2 * D_KV), row(D_ATTN), row(D_CONV), row(D_ATTN),
                   row(D_ATTN), row(N_HEADS * BLK), row(128), any_spec, pl.BlockSpec(cw_shape, lambda i: (0, 0)),
                   any_spec),
        scratch_shapes=[pltpu.VMEM((tile + 8, D_CONV), F32), pltpu.VMEM((BLK, 2 * D_KV), F32),
                        pltpu.VMEM(wt_shape, BF16), pltpu.VMEM((2, AG_CAST_ROWS, wt_sh.shape[1]), F32),
                        pltpu.VMEM((N_CHIPS,) + cw_sh.shape, F32),
                        dma((8,)), dma((8,)), dma((2,)), dma((3,)), dma((3,)), dma((1,))] + _ag_scratch(wo_sh.shape),
        compiler_params=_params(62, ("arbitrary",)),
    )(x, norm_in, wt_sh, wo_sh, cw_sh, norm_conv_out, sinks, norm_attn_out)


def _conv_core(pc_ref, zbuf, cw_ref):
    tile = pc_ref.shape[0]
    cb = pc_ref[:, 0:D_CONV]
    cc = pc_ref[:, D_CONV:2 * D_CONV]
    cu = pc_ref[:, 2 * D_CONV:3 * D_CONV]
    z = cc * cu
    zbuf[8:tile + 8, :] = z
    z1 = zbuf[7:tile + 7, :]
    z2 = zbuf[6:tile + 6, :]
    conv = cw_ref[0:1, :] * z2 + cw_ref[1:2, :] * z1 + cw_ref[2:3, :] * z
    return cb, cc, cu, z, z1, z2, conv


def _band_geometry(block_index):
    qi = lax.broadcasted_iota(jnp.int32, (BLK, BLK), 0)
    kp = lax.broadcasted_iota(jnp.int32, (BLK, BLK), 1)
    use_cur = kp <= qi
    dist = jnp.where(use_cur, qi - kp, qi - kp + BLK).astype(F32)
    valid = use_cur | (block_index > 0)
    return use_cur, dist, valid


def _block_diag(cur, prev, group):
    lane = lax.broadcasted_iota(jnp.int32, cur.shape, 1)

    def halves(t):
        other = pltpu.roll(t, 64, 1)
        lo, hi = (t, other) if group == 0 else (other, t)
        return jnp.where(lane < 64, lo, 0.0), jnp.where(lane >= 64, hi, 0.0)

    return jnp.concatenate(halves(cur) + halves(prev), axis=0).astype(BF16)


def _merge(s4, use_cur):
    return (jnp.where(use_cur, s4[:, 0:BLK], s4[:, 2 * BLK:3 * BLK]),
            jnp.where(use_cur, s4[:, BLK:2 * BLK], s4[:, 3 * BLK:4 * BLK]))


def _split(a, b, use_cur):
    return jnp.concatenate([jnp.where(use_cur, a, 0.0), jnp.where(use_cur, b, 0.0),
                            jnp.where(use_cur, 0.0, a), jnp.where(use_cur, 0.0, b)], axis=1)


def _softmax_head(s, head, sink, dist, valid):
    sc = jnp.where(valid, s - SLOPES[head] * dist, -jnp.inf)
    m = jnp.maximum(jnp.max(sc, axis=-1, keepdims=True), sink)
    p = jnp.exp(sc - m)
    es = jnp.exp(sink - m)
    inv = 1.0 / (jnp.sum(p, axis=-1, keepdims=True) + es)
    return p * inv, es * inv


def _attn_operands(q_ref, kvc_ref, kvp_ref):
    groups = range(N_HEADS // HEADS_PER_KV)
    kbd = [_block_diag(kvc_ref[:, 0:D_KV], kvp_ref[:, 0:D_KV], g) for g in groups]
    vbd = [_block_diag(kvc_ref[:, D_KV:2 * D_KV], kvp_ref[:, D_KV:2 * D_KV], g) for g in groups]
    qps = [(q_ref[:, j * 128:(j + 1) * 128] * SCALE).astype(BF16) for j in range(N_HEADS // 2)]
    return qps, kbd, vbd


def _attn_forward(q_ref, kvc_ref, kvp_ref, ga_ref, sink_ref, gn_ref, geometry, ya_ref, oa_ref, pr_ref, sp_ref):
    use_cur, dist, valid = geometry
    _, kbd, vbd = operands = _attn_operands(q_ref, kvc_ref, kvp_ref)
    yield
    scores = []
    for j, qp in enumerate(operands[0]):
        scores += _merge(_nt(qp, kbd[j // 4]), use_cur)
    yield
    sinks = [sink_ref[0, h] for h in range(N_HEADS)]
    scores = [jnp.where(valid, s - SLOPES[h] * dist, -jnp.inf) for h, s in enumerate(scores)]
    maxes = [jnp.maximum(jnp.max(s, axis=-1, keepdims=True), sinks[h]) for h, s in enumerate(scores)]
    yield
    exps = [jnp.exp(s - m) for s, m in zip(scores, maxes)]
    sink_exps = [jnp.exp(sinks[h] - m) for h, m in enumerate(maxes)]
    yield
    invs = [1.0 / (jnp.sum(e, axis=-1, keepdims=True) + se) for e, se in zip(exps, sink_exps)]
    probs = [e * inv for e, inv in zip(exps, invs)]
    pr_ref[...] = jnp.concatenate(probs, axis=1).astype(BF16)
    lane = lax.broadcasted_iota(jnp.int32, (BLK, 128), 1)
    sp_ref[...] = sum(jnp.where(lane == h, se * inv, 0.0) for h, (se, inv) in enumerate(zip(sink_exps, invs)))
    yield
    p4s = [_split(probs[2 * j], probs[2 * j + 1], use_cur).astype(BF16) for j in range(N_HEADS // 2)]
    ya = jnp.concatenate([_nn(p4, vbd[j // 4]) for j, p4 in enumerate(p4s)], axis=1)
    ya_ref[...] = ya
    yield
    silu, _ = _silu_and_grad(ga_ref[...])
    oa_ref[...] = (ya * _rstd(ya) * gn_ref[...] * silu).astype(BF16)


def _out_proj_loss(x, oc, oa, wo, norm_final, target, pc, conv_w, norm_conv_out):
    seq = x.shape[0]
    tile = TOK_TILE
    n_tiles = seq // tile

    def body(x_ref, oc_ref, oa_ref, wo_ref, gf_ref, t_ref, pc_ref, hcc_ref, hcu_ref, cw_ref, gn_ref,
             dx2_ref, doa_ref, gwo_ref, dpc_ref, small_ref, loss_acc, zbuf, dbuf):
        step = pl.program_id(0)

        def small_add(row, value):
            small_ref[row:row + 1, :] += jnp.sum(value, axis=0, keepdims=True)

        @pl.when(step == 0)
        def _():
            gwo_ref[...] = jnp.zeros_like(gwo_ref)
            small_ref[...] = jnp.zeros_like(small_ref)
            loss_acc[...] = jnp.zeros_like(loss_acc)
            dbuf[tile:tile + 8, :] = jnp.zeros((8, D_CONV), F32)

        oc, oa = oc_ref[...], oa_ref[...]
        x2 = x_ref[...] + _nn(oc, wo_ref[0:D_CONV, :]) + _nn(oa, wo_ref[D_CONV:D_MIX, :])
        r = _rstd(x2)
        xhat = x2 * r
        err = xhat * gf_ref[...] - t_ref[...]
        loss_acc[...] += jnp.sum(err * err, axis=0, keepdims=True) * (0.5 / D_MODEL)
        dy = err * (1.0 / D_MODEL)
        small_add(SMALL_NORM_FINAL, dy * xhat)
        dx2 = _rms_bwd(dy * gf_ref[...], xhat, r)
        dx2_ref[...] = dx2
        db = dx2.astype(BF16)
        do = _nt(db, wo_ref[0:D_CONV, :])
        doa_ref[...] = _nt(db, wo_ref[D_CONV:D_MIX, :])
        gwo_ref[0:D_CONV, :] += _tn(oc, db)
        gwo_ref[D_CONV:D_MIX, :] += _tn(oa, db)

        is_first_tile = step == n_tiles - 1
        zbuf[0:8, :] = jnp.where(is_first_tile, 0.0, hcc_ref[...] * hcu_ref[...])
        cb, cc, cu, z, z1, z2, conv = _conv_core(pc_ref, zbuf, cw_ref)
        silu, dsilu = _silu_and_grad(pc_ref[:, 3 * D_CONV:4 * D_CONV])
        yc = cb * conv
        rc = _rstd(yc)
        chat = yc * rc
        dn = do * silu
        dpc_ref[:, 3 * D_CONV:4 * D_CONV] = (do * (chat * gn_ref[...]) * dsilu).astype(BF16)
        small_add(SMALL_NORM_CONV, dn * chat)
        dyc = _rms_bwd(dn * gn_ref[...], chat, rc)
        dpc_ref[:, 0:D_CONV] = (dyc * conv).astype(BF16)
        dconv = dyc * cb
        small_add(SMALL_CONV_W, dconv * z2)
        small_add(SMALL_CONV_W + 1, dconv * z1)
        small_add(SMALL_CONV_W + 2, dconv * z)
        dbuf[0:tile, :] = dconv
        dz = cw_ref[2:3, :] * dconv + cw_ref[1:2, :] * dbuf[1:tile + 1, :] + cw_ref[0:1, :] * dbuf[2:tile + 2, :]
        dpc_ref[:, D_CONV:2 * D_CONV] = (dz * cu).astype(BF16)
        dpc_ref[:, 2 * D_CONV:3 * D_CONV] = (dz * cc).astype(BF16)
        dbuf[tile:tile + 8, :] = dbuf[0:8, :]

        @pl.when(step == n_tiles - 1)
        def _():
            lane = lax.broadcasted_iota(jnp.int32, (1, D_MODEL), 1)
            small_ref[SMALL_MISC:SMALL_MISC + 1, :] = jnp.where(lane == SMALL_LOSS_LANE, jnp.sum(loss_acc[...]), 0.0)

    def rev(i):
        return n_tiles - 1 - i

    def row(width):
        return pl.BlockSpec((tile, width), lambda i: (rev(i), 0))

    def halo(col_block):
        return pl.BlockSpec((8, D_CONV), lambda i: (jnp.maximum(rev(i) * (tile // 8) - 1, 0), col_block))

    def const(shape):
        return pl.BlockSpec(shape, lambda i: (0, 0))

    return pl.pallas_call(
        body, name="out_proj_loss", grid=(n_tiles,),
        out_shape=(jax.ShapeDtypeStruct((seq, D_MODEL), F32), jax.ShapeDtypeStruct((seq, D_ATTN), F32),
                   jax.ShapeDtypeStruct((D_MIX, D_MODEL), F32), jax.ShapeDtypeStruct((seq, D_PC), BF16),
                   jax.ShapeDtypeStruct((SMALL_ROWS, D_MODEL), F32)),
        in_specs=[row(D_MODEL), row(D_CONV), row(D_ATTN), _resident(wo.shape), _resident((1, D_MODEL)), row(D_MODEL),
                  row(D_PC), halo(1), halo(2), _resident(conv_w.shape), _resident((1, D_CONV))],
        out_specs=(row(D_MODEL), row(D_ATTN), const((D_MIX, D_MODEL)), row(D_PC), const((SMALL_ROWS, D_MODEL))),
        scratch_shapes=[pltpu.VMEM((1, D_MODEL), F32), pltpu.VMEM((tile + 8, D_CONV), F32),
                        pltpu.VMEM((tile + 8, D_CONV), F32)],
        compiler_params=_params(62, ("arbitrary",)),
    )(x, oc, oa, wo, norm_final, target, pc, pc, pc, conv_w, norm_conv_out)


def _attn_bwd(q, kv, ga, ya, doa, probs, sink_probs, norm_attn_out, gwo, small):
    seq = q.shape[0]
    per_step = 2
    n_steps = seq // (per_step * BLK)
    stage_steps = (0, 1, n_steps // 4, (5 * n_steps) // 8, n_steps - 1, n_steps - 1)

    def body(q_ref, kvc_ref, kvp_ref, ga_ref, ya_ref, doa_ref, pr_ref, sp_ref, gn_ref, gwo_ref, small_ref,
             dqg_ref, small_out, gwo_sh, gna_ref, gs_ref, carry, dya_buf, *rs_scratch):
        step = pl.program_id(0)
        rs_stages = _rs_wout_stages(gwo_ref, gwo_sh, *rs_scratch)
        for at, stage in zip(stage_steps[:-1], rs_stages[:-1]):
            pl.when(step == at)(stage)

        @pl.when(step == 0)
        def _():
            gna_ref[...] = jnp.zeros_like(gna_ref)
            gs_ref[...] = jnp.zeros_like(gs_ref)
            carry[...] = jnp.zeros_like(carry)

        lane = lax.broadcasted_iota(jnp.int32, (BLK, 128), 1)

        def fold(bd):
            return (jnp.where(lane < 64, bd[0:BLK], 0.0) + jnp.where(lane >= 64, bd[BLK:2 * BLK], 0.0),
                    jnp.where(lane < 64, bd[2 * BLK:3 * BLK], 0.0) + jnp.where(lane >= 64, bd[3 * BLK:4 * BLK], 0.0))

        def one_block(b):
            rows = slice(b * BLK, (b + 1) * BLK)
            kv_prev = kvp_ref if b == 0 else kvc_ref.at[(b - 1) * BLK:b * BLK, :]
            ya = ya_ref[rows, :]
            r = _rstd(ya)
            xhat = ya * r
            silu, dsilu = _silu_and_grad(ga_ref[rows, :])
            do = doa_ref[rows, :]
            dn = do * silu
            dqg_ref[rows, D_ATTN:2 * D_ATTN] = (do * (xhat * gn_ref[...]) * dsilu).astype(BF16)
            gna_ref[...] += jnp.sum(dn * xhat, axis=0, keepdims=True)
            dya_buf[rows, :] = _rms_bwd(dn * gn_ref[...], xhat, r).astype(BF16)

            use_cur = _band_geometry(per_step * (n_steps - 1 - step) + b)[0]
            pairs = range(N_HEADS // 2)
            qps, kbd, vbd = _attn_operands(q_ref.at[rows, :], kvc_ref.at[rows, :], kv_prev)
            probs = [pr_ref[rows, h * BLK:(h + 1) * BLK].astype(F32) for h in range(N_HEADS)]
            dyps = [dya_buf[rows, j * 128:(j + 1) * 128] for j in pairs]
            dps = []
            for j in pairs:
                dps += _merge(_nt(dyps[j], vbd[j // 4]), use_cur)
            deltas = [jnp.sum(p * dp, axis=-1, keepdims=True) for p, dp in zip(probs, dps)]
            dss = [p * (dp - delta) for p, dp, delta in zip(probs, dps, deltas)]
            delta_lanes = sum(jnp.where(lane == h, deltas[h], 0.0) for h in range(N_HEADS))
            gs_ref[...] -= jnp.sum(sp_ref[rows, :] * delta_lanes, axis=0, keepdims=True)
            ds4s = [_split(dss[2 * j], dss[2 * j + 1], use_cur).astype(BF16) for j in pairs]
            p4s = [_split(probs[2 * j], probs[2 * j + 1], use_cur).astype(BF16) for j in pairs]
            dqg_ref[rows, 0:D_ATTN] = jnp.concatenate([_nn(ds4s[j], kbd[j // 4]) * SCALE for j in pairs], axis=1).astype(BF16)
            sums = []
            for group in range(N_HEADS // HEADS_PER_KV):
                acc = [jnp.zeros((BLK, 128), F32) for _ in range(4)]
                for j in range(group * 4, group * 4 + 4):
                    for slot, part in enumerate(fold(_tn(ds4s[j], qps[j])) + fold(_tn(p4s[j], dyps[j]))):
                        acc[slot] = acc[slot] + part
                sums.append([a + pltpu.roll(a, 64, 1) for a in acc])
            dk_cur, dk_prev, dv_cur, dv_prev = (jnp.where(lane < 64, lo, hi) for lo, hi in zip(sums[0], sums[1]))
            dqg_ref[rows, 2 * D_ATTN:2 * D_ATTN + 2 * D_KV] = (jnp.concatenate([dk_cur, dv_cur], axis=1) + carry[...]).astype(BF16)
            carry[...] = jnp.concatenate([dk_prev, dv_prev], axis=1)

        for b in reversed(range(per_step)):
            one_block(b)

        @pl.when(step == n_steps - 1)
        def _():
            small_out[...] = small_ref[...]
            small_out[SMALL_NORM_ATTN:SMALL_NORM_ATTN + 1, :] = gna_ref[...]
            small_out[SMALL_MISC:SMALL_MISC + 1, 0:128] = small_ref[SMALL_MISC:SMALL_MISC + 1, 0:128] + gs_ref[...]

        pl.when(step == stage_steps[-1])(rs_stages[-1])

    def rows(width):
        return pl.BlockSpec((per_step * BLK, width), lambda i: (n_steps - 1 - i, 0))

    kv_prev = pl.BlockSpec((BLK, 2 * D_KV), lambda i: (jnp.maximum(per_step * (n_steps - 1 - i) - 1, 0), 0))
    any_spec = pl.BlockSpec(memory_space=pl.ANY)
    return pl.pallas_call(
        body, name="attn_bwd", grid=(n_steps,),
        out_shape=(jax.ShapeDtypeStruct((seq, D_QG), BF16), pltpu.HBM(small.shape, F32),
                   pltpu.HBM((gwo.shape[0] // N_CHIPS, gwo.shape[1]), F32)),
        in_specs=[rows(D_ATTN), rows(2 * D_KV), kv_prev, rows(D_ATTN), rows(D_ATTN), rows(D_ATTN),
                  rows(N_HEADS * BLK), rows(128), _resident((1, D_ATTN)), any_spec, _resident(small.shape)],
        out_specs=(rows(D_QG), pl.BlockSpec(small.shape, lambda i: (0, 0)), any_spec),
        scratch_shapes=[pltpu.VMEM((1, D_ATTN), F32), pltpu.VMEM((1, 128), F32),
                        pltpu.VMEM((BLK, 2 * D_KV), F32), pltpu.VMEM((per_step * BLK, D_ATTN), BF16)] + _rs_wout_scratch(gwo.shape),
        compiler_params=_params(44, ("arbitrary",)),
    )(q, kv, kv, ga, ya, doa, probs, sink_probs, norm_attn_out, gwo, small)


GBLK = 256
GSUB = 64
PAIR_RING = 4
LAG_PAIR, LAG_HOP1, LAG_HOP2 = 1, 7, 13


def _bwd_in(dpc, dqg, h, wt, x, norm_in, dx2, small):
    seq = x.shape[0]
    n_blk = D_IN_PROJ // GBLK
    per_chip = n_blk // N_CHIPS
    n_slots = (n_blk + 1) // 2
    n_sub = GBLK // GSUB
    chip_rows = D_IN_PROJ // N_CHIPS
    tile = TOK_TILE
    n_tiles = seq // tile
    n_steps = n_blk + max(n_tiles, LAG_HOP2 + 1)
    chunk = min(seq, 512)
    blk_q, blk_kv, blk_ga = ROW_Q // GBLK, ROW_KV // GBLK, ROW_GA // GBLK

    def block_of(i):
        k = i % N_CHIPS
        robin = per_chip * ((k % 2) * 2 + k // 2) + i // N_CHIPS
        if isinstance(i, int):
            return robin if i < per_chip * N_CHIPS else i
        return jnp.where(i < per_chip * N_CHIPS, robin, i)

    def owner_of(i):
        return (i // N_CHIPS) % 2

    def slot_of(i):
        return (i // (2 * N_CHIPS)) * N_CHIPS + i % N_CHIPS

    def body(dpc_ref, dqg_ref, wt_ref, h_ref, x_ref, g_ref, dx2_ref, small_ref, gx_ref, small_sum, gwt_sh,
             dh_acc, gni, keep, pbuf, xbuf, land, land2, small_land,
             pair_send, pair_recv, h1_send, h1_recv, h2_send, h2_recv, sw_send, sw_recv, sm_send, sm_recv, out_sem):
        step = pl.program_id(0)
        x_i, y_i, c = lax.axis_index("x"), lax.axis_index("y"), lax.axis_index("c")
        me = 4 * x_i + 2 * y_i + c
        j = 2 * x_i + y_i
        pa = (_xor(x_i, 1 - c), _xor(y_i, c), c)
        pb = (_xor(x_i, c), _xor(y_i, 1 - c), c)
        sib = (x_i, y_i, 1 - c)
        ja = 2 * pa[0] + pa[1]
        jb = 2 * pb[0] + pb[1]
        jd = 3 - j

        def remote(src, dst, send, recv, to):
            return pltpu.make_async_remote_copy(src_ref=src, dst_ref=dst, send_sem=send, recv_sem=recv,
                                                device_id=to, device_id_type=MESH)

        def piece(ref, slot, u, n):
            return ref.at[slot, pl.ds(u * GSUB, n * GSUB), :]

        def chip_rows_at(ref, local, n):
            return ref.at[pl.ds(pl.multiple_of(local, GSUB), n * GSUB), :]

        def pair_copy(i):
            slot = slot_of(i)
            return remote(pbuf.at[i % PAIR_RING], land.at[slot], pair_send.at[slot], pair_recv.at[slot], sib)

        def h1_copy(slot, u, n):
            k = slot * n_sub + u
            return remote(piece(xbuf, slot, u, n), piece(xbuf, slot, u, n), h1_send.at[k], h1_recv.at[k], pa)

        def h2_copy(slot, u, n, local):
            k = slot * n_sub + u
            return remote(piece(xbuf, slot, u, n), chip_rows_at(land2, local, n), h2_send.at[k], h2_recv.at[k], pb)

        def sw_copy(slot, u, n, local):
            k = slot * n_sub + u
            return remote(piece(keep, slot, u, n), chip_rows_at(gwt_sh, local, n), sw_send.at[k], sw_recv.at[k], sib)

        def own_copy(slot, u, n, local):
            return pltpu.make_async_copy(piece(keep, slot, u, n), chip_rows_at(gwt_sh, local, n), out_sem.at[slot * n_sub + u])

        def owned(i):
            return (i >= 0) & (i < n_blk) & (owner_of(i) == c)

        def chip_of(blk, u):
            row = blk * GBLK + u * GSUB
            chip = row // chip_rows
            return chip, row - chip * chip_rows

        def pieces(blk):
            first, local = chip_of(blk, 0)
            whole = first == chip_of(blk, n_sub - 1)[0]
            if isinstance(blk, int):
                return [(True, 0, n_sub, first, local)] if whole else [(True, u, 1) + chip_of(blk, u) for u in range(n_sub)]
            return [(whole, 0, n_sub, first, local)] + [(jnp.logical_not(whole), u, 1) + chip_of(blk, u) for u in range(n_sub)]

        @pl.when(step == 0)
        def _():
            dh_acc[...] = jnp.zeros_like(dh_acc)
            gni[...] = jnp.zeros_like(gni)

        @pl.when(step < n_blk)
        def _():
            from_pc = block_of(step) < blk_q
            block = _tn(jnp.where(from_pc, dpc_ref[...], dqg_ref[...]), h_ref[...])
            for t in range(0, seq, chunk):
                d = jnp.where(from_pc, dpc_ref[t:t + chunk, :], dqg_ref[t:t + chunk, :])
                dh_acc[t:t + chunk, :] += _nn(d, wt_ref[...])

            @pl.when(owner_of(step) == c)
            def _():
                keep[slot_of(step)] = block

            @pl.when(owner_of(step) != c)
            def _():
                @pl.when(step >= 2 * PAIR_RING)
                def _():
                    pair_copy(step - 2 * PAIR_RING).wait_send()
                pbuf[step % PAIR_RING] = block.astype(BF16)
                pair_copy(step).start()

        i1 = step - LAG_PAIR

        @pl.when(owned(i1))
        def _():
            slot = slot_of(i1)
            pair_copy(i1).wait_recv()
            _accumulate(keep.at[slot], land.at[slot])
            for cond, u, n, chip, _ in pieces(block_of(i1)):
                @pl.when(cond & ((chip == ja) | (chip == jd)))
                def _(u=u, n=n):
                    _cast_rows(piece(keep, slot, u, n), piece(xbuf, slot, u, n))
                    h1_copy(slot, u, n).start()

        i2 = step - LAG_HOP1

        @pl.when(owned(i2))
        def _():
            slot = slot_of(i2)
            for cond, u, n, chip, local in pieces(block_of(i2)):
                @pl.when(cond & ((chip == j) | (chip == jb)))
                def _(u=u, n=n, chip=chip, local=local):
                    h1_copy(slot, u, n).wait_recv()
                    _accumulate(piece(keep, slot, u, n), piece(xbuf, slot, u, n))

                    @pl.when(chip == jb)
                    def _():
                        _cast_rows(piece(keep, slot, u, n), piece(xbuf, slot, u, n))
                        h2_copy(slot, u, n, local).start()

                @pl.when(cond & ((chip == ja) | (chip == jd)))
                def _(u=u, n=n):
                    h1_copy(slot, u, n).wait_send()

        i3 = step - LAG_HOP2

        @pl.when(owned(i3))
        def _():
            slot = slot_of(i3)
            for cond, u, n, chip, local in pieces(block_of(i3)):
                @pl.when(cond & (chip == j))
                def _(u=u, n=n, local=local):
                    h2_copy(slot, u, n, local).wait_recv()
                    _accumulate(piece(keep, slot, u, n), chip_rows_at(land2, local, n))
                    own_copy(slot, u, n, local).start()
                    sw_copy(slot, u, n, local).start()

                @pl.when(cond & (chip == jb))
                def _(u=u, n=n, local=local):
                    h2_copy(slot, u, n, local).wait_send()

        e = step - n_blk

        @pl.when((e >= 0) & (e < n_tiles))
        def _():
            dh = dh_acc[pl.ds(pl.multiple_of(e * tile, tile), tile), :]
            xv = x_ref[...]
            r = _rstd(xv)
            xhat = xv * r
            gni[...] += jnp.sum(dh * xhat, axis=0, keepdims=True)
            gx_ref[...] = _rms_bwd(dh * g_ref[...], xhat, r) + dx2_ref[...]

        others = [(dx, dy, dc) for dx in (0, 1) for dy in (0, 1) for dc in (0, 1)][1:]
        sends = [remote(small_land.at[me], small_land.at[me], sm_send.at[k], sm_recv.at[k],
                        (_xor(x_i, dx), _xor(y_i, dy), _xor(c, dc))) for k, (dx, dy, dc) in enumerate(others)]

        @pl.when(step == min(n_blk + n_tiles, n_steps - 1))
        def _():
            small_land[me] = small_ref[...]
            small_land[me, SMALL_NORM_IN:SMALL_NORM_IN + 1, :] = gni[...]
            for cp in sends:
                cp.start()

        @pl.when(step == n_steps - 1)
        def _():
            for i in range(n_blk):
                if i + 2 * PAIR_RING >= n_blk:
                    @pl.when(owner_of(i) != c)
                    def _(i=i):
                        pair_copy(i).wait_send()
                for _, u, n, chip, local in pieces(block_of(i)):
                    @pl.when((j == chip) & (c == owner_of(i)))
                    def _(i=i, u=u, n=n, local=local):
                        sw_copy(slot_of(i), u, n, local).wait_send()
                        own_copy(slot_of(i), u, n, local).wait()

                    @pl.when((j == chip) & (c != owner_of(i)))
                    def _(i=i, u=u, n=n, local=local):
                        sw_copy(slot_of(i), u, n, local).wait_recv()
            for cp in sends:
                cp.wait_recv()
            total = small_land[0]
            for dev in range(1, 8):
                total = total + small_land[dev]
            small_sum[...] = total
            for cp in sends:
                cp.wait_send()

    def blk_at(i):
        return block_of(jnp.clip(i, 0, n_blk - 1))

    last_pc_step = max(i for i in range(n_blk) if block_of(i) < blk_q)

    def next_block(i, in_pc):
        i = jnp.clip(i, 0, n_blk - 1)
        step = jnp.full_like(i, last_pc_step if in_pc else n_blk - 1)
        for ahead in reversed(range(N_CHIPS)):
            cand = jnp.minimum(i + ahead, n_blk - 1)
            step = jnp.where((block_of(cand) < blk_q) == in_pc, cand, step)
        return block_of(step)

    def dqg_block(i):
        b = next_block(i, False)
        q_blk = jnp.clip(b - blk_q, 0, blk_kv - blk_q - 1)
        ga_blk = (D_ATTN // GBLK) + jnp.clip(b - blk_ga, 0, n_blk - blk_ga - 1)
        return jnp.where(b < blk_kv, q_blk, jnp.where(b == blk_kv, 2 * D_ATTN // GBLK, ga_blk))

    def tok(i):
        return (jnp.clip(i - n_blk, 0, n_tiles - 1), 0)

    n_piece = n_slots * n_sub
    dma = pltpu.SemaphoreType.DMA
    return pl.pallas_call(
        body, name="bwd_in", grid=(n_steps,),
        out_shape=(jax.ShapeDtypeStruct((seq, D_MODEL), F32), jax.ShapeDtypeStruct(small.shape, F32),
                   jax.ShapeDtypeStruct((chip_rows, D_MODEL), F32)),
        in_specs=[pl.BlockSpec((seq, GBLK), lambda i: (0, next_block(i, True))),
                  pl.BlockSpec((seq, GBLK), lambda i: (0, dqg_block(i))),
                  pl.BlockSpec((GBLK, D_MODEL), lambda i: (blk_at(i), 0)),
                  _resident(h.shape),
                  pl.BlockSpec((tile, D_MODEL), tok), _resident((1, D_MODEL)), pl.BlockSpec((tile, D_MODEL), tok),
                  _resident(small.shape)],
        out_specs=(pl.BlockSpec((tile, D_MODEL), tok), pl.BlockSpec(small.shape, lambda i: (0, 0)),
                   pl.BlockSpec(memory_space=pl.ANY)),
        scratch_shapes=[pltpu.VMEM((seq, D_MODEL), F32), pltpu.VMEM((1, D_MODEL), F32),
                        pltpu.VMEM((n_slots, GBLK, D_MODEL), F32), pltpu.VMEM((PAIR_RING, GBLK, D_MODEL), BF16),
                        pltpu.VMEM((n_slots, GBLK, D_MODEL), BF16), pltpu.VMEM((n_slots, GBLK, D_MODEL), BF16),
                        pltpu.VMEM((chip_rows, D_MODEL), BF16), pltpu.VMEM((8,) + small.shape, F32),
                        dma((n_slots,)), dma((n_slots,)), dma((n_piece,)), dma((n_piece,)), dma((n_piece,)),
                        dma((n_piece,)), dma((n_piece,)), dma((n_piece,)), dma((7,)), dma((7,)), dma((n_piece,))],
        compiler_params=_params(62, ("arbitrary",)),
    )(dpc, dqg, wt, h, x, norm_in, dx2, small)


def _accumulate(dst_ref, src_ref, rows=16):
    def step(i, carry):
        sl = pl.ds(pl.multiple_of(i * rows, rows), rows)
        dst_ref[sl, :] = dst_ref[sl, :] + src_ref[sl, :].astype(F32)
        return carry
    lax.fori_loop(0, dst_ref.shape[0] // rows, step, 0)


def _rs_wout_scratch(gwo_shape):
    o_half, width = gwo_shape[0] // N_CHIPS // 2, gwo_shape[1]
    return [pltpu.VMEM((4, o_half, width), F32), pltpu.VMEM((4, o_half, width), BF16),
            pltpu.VMEM((4, o_half, width), BF16), pltpu.VMEM((2, o_half, width), BF16),
            pltpu.VMEM((o_half, width), BF16),
            pltpu.SemaphoreType.DMA((8,)), pltpu.SemaphoreType.DMA((8,)), pltpu.SemaphoreType.DMA((4,))]


def _rs_wout_stages(gwo_ref, gwo_sh, acc_o, sb_o, r1o, r2o, r3o, send_sems, recv_sems, local_sems):
    o_rows = gwo_ref.shape[0] // N_CHIPS
    o_half = o_rows // 2
    x, y, c = lax.axis_index("x"), lax.axis_index("y"), lax.axis_index("c")
    j = 2 * x + y
    pa = (_xor(x, 1 - c), _xor(y, c), c)
    pb = (_xor(x, c), _xor(y, 1 - c), c)
    sib = (x, y, 1 - c)
    ja = 2 * pa[0] + pa[1]
    jb = 2 * pb[0] + pb[1]
    jd = 3 - j
    order = (ja, jd, jb, j)
    sib_order = (jb, jd, ja, j)

    def rcopy(k, src, dst, to):
        return pltpu.make_async_remote_copy(src_ref=src, dst_ref=dst, send_sem=send_sems.at[k],
                                            recv_sem=recv_sems.at[k], device_id=to, device_id_type=MESH)

    def o_rows_of(chip, half):
        return gwo_ref.at[pl.ds(pl.multiple_of(chip * o_rows + half * o_half, 8), o_half), :]

    def loads(chips, half):
        return [pltpu.make_async_copy(o_rows_of(chip, half), acc_o.at[s], local_sems.at[s]) for s, chip in enumerate(chips)]

    def pair_send(s):
        return rcopy(s, sb_o.at[s], r1o.at[s], sib)

    def out_half(half):
        return gwo_sh.at[pl.ds(pl.multiple_of(half * o_half, 8), o_half), :]

    hop1 = [rcopy(4 + s, sb_o.at[s], r2o.at[s], pa) for s in range(2)]
    hop2 = rcopy(6, sb_o.at[2], r3o, pb)
    swap = rcopy(7, acc_o.at[3], out_half(c), sib)
    mine = pltpu.make_async_copy(acc_o.at[3], out_half(c), local_sems.at[0])

    def resend(s, copy):
        pair_send(s).wait_send()
        _cast_rows(acc_o.at[s], sb_o.at[s])
        copy.start()

    def stage_load():
        for cp in loads(sib_order, 1 - c):
            cp.start()

    def stage_pair():
        for s, (cp, mine_in) in enumerate(zip(loads(sib_order, 1 - c), loads(order, c))):
            cp.wait()
            _cast_rows(acc_o.at[s], sb_o.at[s])
            pair_send(s).start()
            mine_in.start()

    def stage_hop1():
        for s, cp in enumerate(loads(order, c)):
            cp.wait()
            pair_send(s).wait_recv()
            _accumulate(acc_o.at[s], r1o.at[s])
            if s < 2:
                resend(s, hop1[s])

    def stage_hop2():
        hop1[1].wait_recv()
        _accumulate(acc_o.at[2], r2o.at[1])
        resend(2, hop2)
        hop1[0].wait_recv()
        _accumulate(acc_o.at[3], r2o.at[0])

    def stage_final():
        hop2.wait_recv()
        _accumulate(acc_o.at[3], r3o)
        mine.start()
        swap.start()

    def stage_drain():
        rcopy(7, acc_o.at[3], out_half(1 - c), sib).wait_recv()
        for cp in [pair_send(3)] + hop1 + [hop2, swap]:
            cp.wait_send()
        mine.wait()

    return stage_load, stage_pair, stage_hop1, stage_hop2, stage_final, stage_drain


def _adamw_big(groups, passed):
    arrays = [a for group in groups for a in group[:4]]
    steps = [group[0].shape[0] // group[4] for group in groups]
    first = [sum(steps[:k]) for k in range(len(steps) + 1)]
    n = len(arrays)

    def body(*refs):
        ins, passed_in, outs, passed_out = refs[:n], refs[n], refs[n + 1:2 * n + 1], refs[2 * n + 1]
        i = pl.program_id(0)
        for k in range(len(groups)):
            @pl.when((i >= first[k]) & (i < first[k + 1]))
            def _(k=k):
                w_ref, g_ref, m_ref, v_ref = ins[4 * k:4 * k + 4]
                _adamw_update(w_ref, g_ref[...], m_ref, v_ref, *outs[4 * k:4 * k + 4])

        @pl.when(i < steps[0])
        def _():
            passed_out[...] = passed_in[...]

    def spec(k, rows, width):
        return pl.BlockSpec((rows, width), lambda i: (jnp.clip(i - first[k], 0, steps[k] - 1), 0))

    specs = [spec(k, group[4], group[0].shape[1]) for k, group in enumerate(groups) for _ in range(4)]
    specs.append(spec(0, passed.shape[0] // steps[0], passed.shape[1]))
    out = pl.pallas_call(
        body, name="adamw_big", grid=(first[-1],),
        out_shape=tuple(jax.ShapeDtypeStruct(a.shape, a.dtype) for a in arrays + [passed]),
        in_specs=specs, out_specs=tuple(specs),
        compiler_params=_params(40, ("arbitrary",)),
    )(*arrays, passed)
    return [tuple(out[4 * k:4 * k + 4]) for k in range(len(groups))], out[-1]


def _adamw_update(w_ref, gv, m_ref, v_ref, go_ref, d_ref, nm_ref, nv_ref, at=...):
    go_ref[at] = gv
    nm = ADAM_B1 * m_ref[at] + (1.0 - ADAM_B1) * gv
    nv = ADAM_B2 * v_ref[at] + (1.0 - ADAM_B2) * (gv * gv)
    m_hat = nm / (1.0 - ADAM_B1 ** ADAM_STEP)
    v_hat = nv / (1.0 - ADAM_B2 ** ADAM_STEP)
    d_ref[at] = -ADAM_LR * (m_hat / (jnp.sqrt(v_hat) + ADAM_EPS) + ADAM_WD * w_ref[at])
    nm_ref[at] = nm
    nv_ref[at] = nv


def _adamw_small(chip, small_sum, weights, grads_of, ms, vs):
    n = len(weights)

    def body(chip_ref, small_ref, *refs):
        ins, outs, loss_ref = refs[:3 * n], refs[3 * n:-1], refs[-1]
        for k in range(n):
            w_ref, m_ref, v_ref = ins[3 * k:3 * k + 3]
            for at, gv in grads_of[k](small_ref, chip_ref):
                _adamw_update(w_ref, gv, m_ref, v_ref, *outs[4 * k:4 * k + 4], at=at)
        loss_ref[...] = small_ref[SMALL_MISC:SMALL_MISC + 1, SMALL_LOSS_LANE:SMALL_LOSS_LANE + 1]

    flat = [a for group in zip(weights, ms, vs) for a in group]
    vmem = pl.BlockSpec(memory_space=pltpu.VMEM)
    out = pl.pallas_call(
        body, name="adamw_small",
        out_shape=tuple(jax.ShapeDtypeStruct(w.shape, F32) for w in weights for _ in range(4))
        + (jax.ShapeDtypeStruct((1, 1), F32),),
        in_specs=[pl.BlockSpec(memory_space=pltpu.SMEM)] + [vmem] * (1 + 3 * n), out_specs=(vmem,) * (4 * n + 1),
    )(chip, small_sum, *flat)
    return [tuple(out[4 * k:4 * k + 4]) for k in range(n)], out[-1][0, 0]


def kernel(x, norm_in, w_in, conv_w, attn_sinks, norm_conv_out, norm_attn_out, w_out, norm_final, loss_target, m_norm_in, m_w_in, m_conv_w, m_attn_sinks, m_norm_conv_out, m_norm_attn_out, m_w_out, m_norm_final, v_norm_in, v_w_in, v_conv_w, v_attn_sinks, v_norm_conv_out, v_norm_attn_out, v_w_out, v_norm_final):
    chip = 2 * lax.axis_index("x") + lax.axis_index("y")
    xs, target = x[0], loss_target[0]
    norm_final2 = norm_final.reshape(1, D_MODEL)
    w_in_t, m_w_in_t, v_w_in_t = w_in[0].T, m_w_in[0].T, v_w_in[0].T

    def from_hbm(*arrays):
        return tuple(pltpu.with_memory_space_constraint(a, pltpu.HBM) for a in arrays)

    h, pc, q, kv, ga, oc, ya, oa, probs, sink_probs, wt, cw, wo = _fwd_in(
        xs, norm_in, w_in_t, w_out[0], conv_w.transpose(1, 0, 2), norm_conv_out, attn_sinks, norm_attn_out)
    dx2, doa, gwo, dpc, small = _out_proj_loss(xs, oc, oa, wo, norm_final2, target, pc, cw, norm_conv_out)
    q, kv, ga, ya, doa, probs, gwo, small = from_hbm(q, kv, ga, ya, doa, probs, gwo, small)
    dqg, small, gwo_sh = _attn_bwd(q, kv, ga, ya, doa, probs, sink_probs, norm_attn_out, gwo, small)
    grad_x, small_sum, gwt_sh = _bwd_in(dpc, dqg, h, wt, xs, norm_in, dx2, small)

    (up_w_in, up_w_out), grad_x = _adamw_big(
        [from_hbm(w_in_t, gwt_sh, m_w_in_t, v_w_in_t) + (200,), from_hbm(w_out[0], gwo_sh, m_w_out[0], v_w_out[0]) + (128,)],
        *from_hbm(grad_x))
    up_w_in = tuple(o.T[None] for o in up_w_in)
    up_w_out = tuple(o[None] for o in up_w_out)

    def row_of(r):
        return lambda small_ref, chip_ref: [(..., small_ref[r:r + 1, :])]

    def sink_lanes(small_ref, chip_ref):
        return [(..., small_ref[SMALL_MISC:SMALL_MISC + 1, 0:N_HEADS])]

    def conv_taps(small_ref, chip_ref):
        width = D_CONV // N_CHIPS
        cols = pl.ds(pl.multiple_of(chip_ref[0] * width, width), width)
        return [(k, small_ref[pl.ds(SMALL_CONV_W + k, 1), cols]) for k in range(conv_w.shape[1])]

    def small_view(a):
        return a.transpose(1, 0, 2) if a.ndim == 3 else a.reshape(-1, a.shape[-1])

    small_w = (norm_in, conv_w, attn_sinks, norm_conv_out, norm_attn_out, norm_final)
    small_m = (m_norm_in, m_conv_w, m_attn_sinks, m_norm_conv_out, m_norm_attn_out, m_norm_final)
    small_v = (v_norm_in, v_conv_w, v_attn_sinks, v_norm_conv_out, v_norm_attn_out, v_norm_final)
    small_g = (row_of(SMALL_NORM_IN), conv_taps, sink_lanes, row_of(SMALL_NORM_CONV), row_of(SMALL_NORM_ATTN),
               row_of(SMALL_NORM_FINAL))
    w_views, m_views, v_views = (tuple(small_view(a) for a in group) for group in (small_w, small_m, small_v))
    up_small, loss = _adamw_small(chip.astype(jnp.int32).reshape(1), small_sum, w_views, small_g, m_views, v_views)
    up_small = [tuple(o.transpose(1, 0, 2) if w.ndim == 3 else o.reshape(w.shape) for o in up)
                for up, w in zip(up_small, small_w)]
    up_norm_in, up_conv_w, up_sinks, up_norm_conv, up_norm_attn, up_norm_final = up_small
    updates = (up_norm_in, up_w_in, up_conv_w, up_sinks, up_norm_conv, up_norm_attn, up_w_out, up_norm_final)
    grads_out, deltas, new_m, new_v = zip(*updates)
    return (loss, grad_x[None], *grads_out, *deltas, *new_m, *new_v)
```

```python
import jax
import jax.numpy as jnp
from jax import lax
from jax.experimental import pallas as pl
from jax.experimental.pallas import tpu as pltpu

F32 = jnp.float32
BF16 = jnp.bfloat16
MESH = pl.DeviceIdType.MESH

D_MODEL = 1024
D_CONV = 1024
D_ATTN = 1024
D_KV = 128
D_QG = 2 * D_ATTN + 2 * D_KV
D_MIX = D_CONV + D_ATTN
D_PC = 4 * D_CONV
D_IN_PROJ = D_PC + 2 * D_ATTN + 2 * D_KV
ROW_Q = D_PC
ROW_KV = ROW_Q + D_ATTN
ROW_GA = ROW_KV + 2 * D_KV
N_HEADS = 16
HEAD_DIM = 64
HEADS_PER_KV = 8
BLK = 128
N_CHIPS = 4
RMS_EPS = 1e-5
SCALE = HEAD_DIM ** -0.5
SLOPES = tuple(2.0 ** (-8.0 * (h + 1) / N_HEADS) for h in range(N_HEADS))

ADAM_LR, ADAM_B1, ADAM_B2, ADAM_EPS, ADAM_WD, ADAM_STEP = 0.001, 0.9, 0.999, 1e-08, 0.01, 10

SMALL_ROWS = 8
SMALL_NORM_IN, SMALL_NORM_CONV, SMALL_NORM_ATTN, SMALL_NORM_FINAL, SMALL_CONV_W, SMALL_MISC = 0, 1, 2, 3, 4, 7
SMALL_LOSS_LANE = N_HEADS

TOK_TILE = 256
PC_PIECE = 512
MIB = 1 << 20


def _params(vmem_mib, semantics=None):
    return pltpu.CompilerParams(dimension_semantics=semantics, vmem_limit_bytes=vmem_mib * MIB)


def _nn(a, b):
    return jnp.dot(a, b, preferred_element_type=F32)


def _nt(a, b):
    return lax.dot_general(a, b, (((1,), (1,)), ((), ())), preferred_element_type=F32)


def _tn(a, b):
    return lax.dot_general(a, b, (((0,), (0,)), ((), ())), preferred_element_type=F32)


def _rstd(v):
    return lax.rsqrt(jnp.mean(v * v, axis=-1, keepdims=True) + RMS_EPS)


def _rms_bwd(g, xhat, rstd):
    return rstd * (g - xhat * jnp.mean(g * xhat, axis=-1, keepdims=True))


def _silu_and_grad(g):
    s = jax.nn.sigmoid(g)
    return g * s, s * (1.0 + g * (1.0 - s))


def _resident(shape):
    return pl.BlockSpec(shape, lambda *_: (0,) * len(shape), pipeline_mode=pl.Buffered(1))


def _xor(a, b):
    return a + b - 2 * a * b


def _cast_rows(src_ref, dst_ref, rows=32):
    def step(i, carry):
        sl = pl.ds(pl.multiple_of(i * rows, rows), rows)
        dst_ref[sl, :] = src_ref[sl, :].astype(dst_ref.dtype)
        return carry
    lax.fori_loop(0, src_ref.shape[0] // rows, step, 0)


def _ag_scratch(shard_shape):
    rows, width = shard_shape
    return [pltpu.VMEM((rows, width), F32), pltpu.VMEM((rows, width), BF16), pltpu.VMEM((3, rows // 2, width), BF16),
            pltpu.SemaphoreType.DMA((6,)), pltpu.SemaphoreType.DMA((6,)), pltpu.SemaphoreType.DMA((4,))]


def _ag_stages(sh_ref, out, f32_buf, own, land, send_sems, recv_sems, local_sems):
    rows = sh_ref.shape[0]
    half = rows // 2
    x, y, c = lax.axis_index("x"), lax.axis_index("y"), lax.axis_index("c")
    j = 2 * x + y
    p1 = (_xor(x, c), _xor(y, 1 - c), c)
    p2 = (_xor(x, 1 - c), _xor(y, c), c)
    sib = (x, y, 1 - c)
    j1 = 2 * p1[0] + p1[1]
    j2 = 2 * p2[0] + p2[1]
    j3 = 3 - j

    def rows_of(chip, hf):
        return out.at[pl.ds(pl.multiple_of(chip * rows + hf * half, 16), half), :]

    def rcopy(k, src, dst, to):
        return pltpu.make_async_remote_copy(src_ref=src, dst_ref=dst, send_sem=send_sems.at[k],
                                            recv_sem=recv_sems.at[k], device_id=to, device_id_type=MESH)

    my_half = own.at[pl.ds(pl.multiple_of(c * half, 16), half), :]
    hop1 = rcopy(0, my_half, land.at[0], p1)
    hop2_own = rcopy(1, my_half, land.at[1], p2)
    hop2_fwd = rcopy(2, land.at[0], land.at[2], p2)
    swaps = [rcopy(3 + s, land.at[s], rows_of(chip, c), sib) for s, chip in enumerate((j1, j2, j3))]
    keeps = [pltpu.make_async_copy(land.at[s], rows_of(chip, c), local_sems.at[1 + s]) for s, chip in enumerate((j1, j2, j3))]
    load = pltpu.make_async_copy(sh_ref, f32_buf, local_sems.at[0])
    own_out = pltpu.make_async_copy(own, out.at[pl.ds(pl.multiple_of(j * rows, 16), rows), :], local_sems.at[0])

    def stage_load():
        load.start()

    def stage_send():
        load.wait()
        _cast_rows(f32_buf, own)
        own_out.start()
        hop1.start()

    def stage_forward():
        hop1.wait_recv()
        hop2_own.start()
        hop2_fwd.start()
        swaps[0].start()
        keeps[0].start()

    def stage_publish():
        hop2_own.wait_recv()
        swaps[1].start()
        keeps[1].start()
        hop2_fwd.wait_recv()
        swaps[2].start()
        keeps[2].start()

    def stage_drain():
        for s, chip in enumerate((j2, j1, j3)):
            rcopy(3 + s, my_half, rows_of(chip, 1 - c), sib).wait_recv()
        for cp in [hop1, hop2_own, hop2_fwd] + swaps:
            cp.wait_send()
        for cp in [own_out] + keeps:
            cp.wait()

    return stage_load, stage_send, stage_forward, stage_publish, stage_drain


AG_CAST_ROWS = 400


def _gather_resident(sh_ref, out, f32_buf, send_sems, recv_sems, local_sems, after_first_hop):
    rows = sh_ref.shape[0]
    half = rows // 2
    x, y, c = lax.axis_index("x"), lax.axis_index("y"), lax.axis_index("c")
    j = 2 * x + y
    p1 = (_xor(x, c), _xor(y, 1 - c), c)
    p2 = (_xor(x, 1 - c), _xor(y, c), c)
    sib = (x, y, 1 - c)
    j1 = 2 * p1[0] + p1[1]
    j2 = 2 * p2[0] + p2[1]
    j3 = 3 - j

    def rows_of(chip, hf):
        return out.at[pl.ds(pl.multiple_of(chip * rows + hf * half, 16), half), :]

    def send(k, chip, to):
        return pltpu.make_async_remote_copy(src_ref=rows_of(chip, c), dst_ref=rows_of(chip, c), send_sem=send_sems.at[k],
                                            recv_sem=recv_sems.at[k], device_id=to, device_id_type=MESH)

    per_half = half // AG_CAST_ROWS

    chunks = [c * per_half + k for k in range(per_half)] + [(1 - c) * per_half + k for k in range(per_half)]

    def load(n):
        lo = pl.multiple_of(chunks[n] * AG_CAST_ROWS, 16)
        return pltpu.make_async_copy(sh_ref.at[pl.ds(lo, AG_CAST_ROWS), :], f32_buf.at[n % 2], local_sems.at[n % 2])

    def send_piece(k, chip, q, to):
        at = out.at[pl.ds(pl.multiple_of(chip * rows + c * half + q * (half // 2), 16), half // 2), :]
        return pltpu.make_async_remote_copy(src_ref=at, dst_ref=at, send_sem=send_sems.at[k], recv_sem=recv_sems.at[k],
                                            device_id=to, device_id_type=MESH)

    forwards = [send_piece(2, j1, 0, p2), send_piece(6, j1, 1, p2)]
    sends = [send(0, j, p1), send(1, j, p2)] + forwards + [send(3, j1, sib)]
    load(0).start()
    load(1).start()
    for n in range(len(chunks)):
        load(n).wait()
        lo = pl.multiple_of(chunks[n] * AG_CAST_ROWS, 16)
        _cast_rows(f32_buf.at[n % 2], out.at[pl.ds(pl.multiple_of(j * rows + lo, 16), AG_CAST_ROWS), :], rows=16)
        if n + 2 < len(chunks):
            load(n + 2).start()
        if n == per_half - 1:
            sends[0].start()
            sends[1].start()
    sends[0].wait_recv()
    for cp in sends[2:]:
        cp.start()
    after_first_hop()
    sends[1].wait_recv()
    sends.append(send(4, j2, sib))
    sends[-1].start()
    for q, (k, arrived) in enumerate(zip((5, 7), forwards)):
        arrived.wait_recv()
        sends.append(send_piece(k, j3, q, sib))
        sends[-1].start()
    for k in (3, 4):
        send(k, j, sib).wait_recv()
    for k in (5, 7):
        send_piece(k, j, 0, sib).wait_recv()
    for cp in sends:
        cp.wait_send()


def _fwd_in(x, norm_in, wt_sh, wo_sh, cw_sh, norm_conv_out, sinks, norm_attn_out):
    seq = x.shape[0]
    tile = TOK_TILE
    n_tiles = seq // tile
    stage_steps = (0, n_tiles // 4, (5 * n_tiles) // 8, n_tiles - 1)
    blocks = tile // BLK
    cw_cols = cw_sh.shape[-1]

    def body(x_ref, g_ref, wtsh_ref, wo_ref, cwsh_ref, gn_ref, sink_ref, gna_ref,
             h_ref, pc_ref, q_ref, kv_ref, ga_ref, oc_ref, ya_ref, oa_ref, pr_ref, sp_ref, wt_out, cw_ref, wo_out,
             zbuf, kv_last, wt_ref, f32_buf, cw_land, wt_send, wt_recv, wt_local, cw_send, cw_recv, cw_local, *ag_scratch):
        step = pl.program_id(0)
        to_hbm = pltpu.make_async_copy(wt_ref, wt_out, wt_local.at[0])
        load_wo, *stages = _ag_stages(wo_ref, wo_out, *ag_scratch)

        @pl.when(step == 0)
        def _():
            load_wo()
            zbuf[0:8, :] = jnp.zeros((8, D_CONV), F32)
            kv_last[...] = jnp.zeros_like(kv_last)
            x_i, y_i, c = lax.axis_index("x"), lax.axis_index("y"), lax.axis_index("c")
            j = 2 * x_i + y_i
            p1 = (_xor(x_i, c), _xor(y_i, 1 - c), c)
            p2 = (_xor(x_i, 1 - c), _xor(y_i, c), c)
            j1 = 2 * p1[0] + p1[1]

            def cw_copy(k, src, chip, to):
                return pltpu.make_async_remote_copy(src_ref=src, dst_ref=cw_land.at[chip], send_sem=cw_send.at[k],
                                                    recv_sem=cw_recv.at[k], device_id=to, device_id_type=MESH)

            mine = pltpu.make_async_copy(cwsh_ref, cw_land.at[j], cw_local.at[0])
            mine.start()
            first = cw_copy(0, cwsh_ref, j, p1)
            first.start()
            second = [cw_copy(1, cwsh_ref, j, p2), cw_copy(2, cw_land.at[j1], j1, p2)]

            def cw_second_hop():
                first.wait_recv()
                for cp in second:
                    cp.start()

            _gather_resident(wtsh_ref, wt_ref, f32_buf, wt_send, wt_recv, wt_local, cw_second_hop)
            to_hbm.start()
            for cp in second:
                cp.wait_recv()
            for cp in [first] + second:
                cp.wait_send()
            mine.wait()
            for chip in range(N_CHIPS):
                for tap in range(cw_sh.shape[0]):
                    cw_ref[tap:tap + 1, chip * cw_cols:(chip + 1) * cw_cols] = cw_land[chip, tap]

        for at, stage in zip(stage_steps[:-1], stages[:-1]):
            pl.when(step == at)(stage)

        def attention(b):
            rows = pl.ds(b * BLK, BLK)
            kv_prev = kv_last if b == 0 else kv_ref.at[pl.ds((b - 1) * BLK, BLK), :]
            return _attn_forward(q_ref.at[rows, :], kv_ref.at[rows, :], kv_prev, ga_ref.at[rows, :], sink_ref, gna_ref,
                                 _band_geometry(step * blocks + b), ya_ref.at[rows, :], oa_ref.at[rows, :],
                                 pr_ref.at[rows, :], sp_ref.at[rows, :])

        xv = x_ref[...]
        h = (xv * _rstd(xv) * g_ref[...]).astype(BF16)
        h_ref[...] = h
        q_ref[...] = _nt(h, wt_ref[ROW_Q:ROW_KV, :])
        kv_ref[...] = _nt(h, wt_ref[ROW_KV:ROW_GA, :])
        ga_ref[...] = _nt(h, wt_ref[ROW_GA:D_IN_PROJ, :])
        attention_blocks = [attention(b) for b in range(blocks)]
        for lo in range(0, D_PC, PC_PIECE):
            pc_ref[:, lo:lo + PC_PIECE] = _nt(h, wt_ref[lo:lo + PC_PIECE, :])
            for stages_of_block in attention_blocks:
                next(stages_of_block, None)
        for stages_of_block in attention_blocks:
            for _ in stages_of_block:
                pass
        kv_last[...] = kv_ref[tile - BLK:tile, :]

        cb, _, _, _, _, _, conv = _conv_core(pc_ref, zbuf, cw_ref)
        yc = cb * conv
        silu, _ = _silu_and_grad(pc_ref[:, 3 * D_CONV:4 * D_CONV])
        oc_ref[...] = (yc * _rstd(yc) * gn_ref[...] * silu).astype(BF16)
        zbuf[0:8, :] = zbuf[tile:tile + 8, :]

        @pl.when(step == stage_steps[-1])
        def _():
            stages[-1]()
            to_hbm.wait()

    def row(width):
        return pl.BlockSpec((tile, width), lambda i: (i, 0))

    any_spec = pl.BlockSpec(memory_space=pl.ANY)
    dma = pltpu.SemaphoreType.DMA
    wt_shape = (N_CHIPS * wt_sh.shape[0], wt_sh.shape[1])
    cw_shape = (cw_sh.shape[0], N_CHIPS * cw_cols)
    return pl.pallas_call(
        body, name="fwd_in", grid=(n_tiles,),
        out_shape=(jax.ShapeDtypeStruct((seq, D_MODEL), BF16), jax.ShapeDtypeStruct((seq, D_PC), F32),
                   jax.ShapeDtypeStruct((seq, D_ATTN), F32), jax.ShapeDtypeStruct((seq, 2 * D_KV), F32),
                   jax.ShapeDtypeStruct((seq, D_ATTN), F32), jax.ShapeDtypeStruct((seq, D_CONV), BF16),
                   pltpu.HBM((seq, D_ATTN), F32), pltpu.HBM((seq, D_ATTN), BF16),
                   pltpu.HBM((seq, N_HEADS * BLK), BF16), pltpu.HBM((seq, 128), F32),
                   pltpu.HBM(wt_shape, BF16), jax.ShapeDtypeStruct(cw_shape, F32),
                   jax.ShapeDtypeStruct((N_CHIPS * wo_sh.shape[0], wo_sh.shape[1]), BF16)),
        in_specs=[row(D_MODEL), _resident((1, D_MODEL)), any_spec, any_spec, any_spec, _resident((1, D_CONV)),
                  pl.BlockSpec(memory_space=pltpu.SMEM), _resident((1, D_ATTN))],
        out_specs=(row(D_MODEL), row(D_PC), row(D_ATTN), row(2 * D_KV), row(D_ATTN), row(D_CONV), row(D_ATTN),
                   row(D_ATTN), row(N_HEADS * BLK), row(128), any_spec, pl.BlockSpec(cw_shape, lambda i: (0, 0)),
                   any_spec),
        scratch_shapes=[pltpu.VMEM((tile + 8, D_CONV), F32), pltpu.VMEM((BLK, 2 * D_KV), F32),
                        pltpu.VMEM(wt_shape, BF16), pltpu.VMEM((2, AG_CAST_ROWS, wt_sh.shape[1]), F32),
                        pltpu.VMEM((N_CHIPS,) + cw_sh.shape, F32),
                        dma((8,)), dma((8,)), dma((2,)), dma((3,)), dma((3,)), dma((1,))] + _ag_scratch(wo_sh.shape),
        compiler_params=_params(62, ("arbitrary",)),
    )(x, norm_in, wt_sh, wo_sh, cw_sh, norm_conv_out, sinks, norm_attn_out)


def _conv_core(pc_ref, zbuf, cw_ref):
    tile = pc_ref.shape[0]
    cb = pc_ref[:, 0:D_CONV]
    cc = pc_ref[:, D_CONV:2 * D_CONV]
    cu = pc_ref[:, 2 * D_CONV:3 * D_CONV]
    z = cc * cu
    zbuf[8:tile + 8, :] = z
    z1 = zbuf[7:tile + 7, :]
    z2 = zbuf[6:tile + 6, :]
    conv = cw_ref[0:1, :] * z2 + cw_ref[1:2, :] * z1 + cw_ref[2:3, :] * z
    return cb, cc, cu, z, z1, z2, conv


def _band_geometry(block_index):
    qi = lax.broadcasted_iota(jnp.int32, (BLK, BLK), 0)
    kp = lax.broadcasted_iota(jnp.int32, (BLK, BLK), 1)
    use_cur = kp <= qi
    dist = jnp.where(use_cur, qi - kp, qi - kp + BLK).astype(F32)
    valid = use_cur | (block_index > 0)
    return use_cur, dist, valid


def _block_diag(cur, prev, group):
    lane = lax.broadcasted_iota(jnp.int32, cur.shape, 1)

    def halves(t):
        other = pltpu.roll(t, 64, 1)
        lo, hi = (t, other) if group == 0 else (other, t)
        return jnp.where(lane < 64, lo, 0.0), jnp.where(lane >= 64, hi, 0.0)

    return jnp.concatenate(halves(cur) + halves(prev), axis=0).astype(BF16)


def _merge(s4, use_cur):
    return (jnp.where(use_cur, s4[:, 0:BLK], s4[:, 2 * BLK:3 * BLK]),
            jnp.where(use_cur, s4[:, BLK:2 * BLK], s4[:, 3 * BLK:4 * BLK]))


def _split(a, b, use_cur):
    return jnp.concatenate([jnp.where(use_cur, a, 0.0), jnp.where(use_cur, b, 0.0),
                            jnp.where(use_cur, 0.0, a), jnp.where(use_cur, 0.0, b)], axis=1)


def _softmax_head(s, head, sink, dist, valid):
    sc = jnp.where(valid, s - SLOPES[head] * dist, -jnp.inf)
    m = jnp.maximum(jnp.max(sc, axis=-1, keepdims=True), sink)
    p = jnp.exp(sc - m)
    es = jnp.exp(sink - m)
    inv = 1.0 / (jnp.sum(p, axis=-1, keepdims=True) + es)
    return p * inv, es * inv


def _attn_operands(q_ref, kvc_ref, kvp_ref):
    groups = range(N_HEADS // HEADS_PER_KV)
    kbd = [_block_diag(kvc_ref[:, 0:D_KV], kvp_ref[:, 0:D_KV], g) for g in groups]
    vbd = [_block_diag(kvc_ref[:, D_KV:2 * D_KV], kvp_ref[:, D_KV:2 * D_KV], g) for g in groups]
    qps = [(q_ref[:, j * 128:(j + 1) * 128] * SCALE).astype(BF16) for j in range(N_HEADS // 2)]
    return qps, kbd, vbd


def _attn_forward(q_ref, kvc_ref, kvp_ref, ga_ref, sink_ref, gn_ref, geometry, ya_ref, oa_ref, pr_ref, sp_ref):
    use_cur, dist, valid = geometry
    _, kbd, vbd = operands = _attn_operands(q_ref, kvc_ref, kvp_ref)
    yield
    scores = []
    for j, qp in enumerate(operands[0]):
        scores += _merge(_nt(qp, kbd[j // 4]), use_cur)
    yield
    sinks = [sink_ref[0, h] for h in range(N_HEADS)]
    scores = [jnp.where(valid, s - SLOPES[h] * dist, -jnp.inf) for h, s in enumerate(scores)]
    maxes = [jnp.maximum(jnp.max(s, axis=-1, keepdims=True), sinks[h]) for h, s in enumerate(scores)]
    yield
    exps = [jnp.exp(s - m) for s, m in zip(scores, maxes)]
    sink_exps = [jnp.exp(sinks[h] - m) for h, m in enumerate(maxes)]
    yield
    invs = [1.0 / (jnp.sum(e, axis=-1, keepdims=True) + se) for e, se in zip(exps, sink_exps)]
    probs = [e * inv for e, inv in zip(exps, invs)]
    pr_ref[...] = jnp.concatenate(probs, axis=1).astype(BF16)
    lane = lax.broadcasted_iota(jnp.int32, (BLK, 128), 1)
    sp_ref[...] = sum(jnp.where(lane == h, se * inv, 0.0) for h, (se, inv) in enumerate(zip(sink_exps, invs)))
    yield
    p4s = [_split(probs[2 * j], probs[2 * j + 1], use_cur).astype(BF16) for j in range(N_HEADS // 2)]
    ya = jnp.concatenate([_nn(p4, vbd[j // 4]) for j, p4 in enumerate(p4s)], axis=1)
    ya_ref[...] = ya
    yield
    silu, _ = _silu_and_grad(ga_ref[...])
    oa_ref[...] = (ya * _rstd(ya) * gn_ref[...] * silu).astype(BF16)


def _out_proj_loss(x, oc, oa, wo, norm_final, target, pc, conv_w, norm_conv_out):
    seq = x.shape[0]
    tile = TOK_TILE
    n_tiles = seq // tile

    def body(x_ref, oc_ref, oa_ref, wo_ref, gf_ref, t_ref, pc_ref, hcc_ref, hcu_ref, cw_ref, gn_ref,
             dx2_ref, doa_ref, gwo_ref, dpc_ref, small_ref, loss_acc, zbuf, dbuf):
        step = pl.program_id(0)

        def small_add(row, value):
            small_ref[row:row + 1, :] += jnp.sum(value, axis=0, keepdims=True)

        @pl.when(step == 0)
        def _():
            gwo_ref[...] = jnp.zeros_like(gwo_ref)
            small_ref[...] = jnp.zeros_like(small_ref)
            loss_acc[...] = jnp.zeros_like(loss_acc)
            dbuf[tile:tile + 8, :] = jnp.zeros((8, D_CONV), F32)

        oc, oa = oc_ref[...], oa_ref[...]
        x2 = x_ref[...] + _nn(oc, wo_ref[0:D_CONV, :]) + _nn(oa, wo_ref[D_CONV:D_MIX, :])
        r = _rstd(x2)
        xhat = x2 * r
        err = xhat * gf_ref[...] - t_ref[...]
        loss_acc[...] += jnp.sum(err * err, axis=0, keepdims=True) * (0.5 / D_MODEL)
        dy = err * (1.0 / D_MODEL)
        small_add(SMALL_NORM_FINAL, dy * xhat)
        dx2 = _rms_bwd(dy * gf_ref[...], xhat, r)
        dx2_ref[...] = dx2
        db = dx2.astype(BF16)
        do = _nt(db, wo_ref[0:D_CONV, :])
        doa_ref[...] = _nt(db, wo_ref[D_CONV:D_MIX, :])
        gwo_ref[0:D_CONV, :] += _tn(oc, db)
        gwo_ref[D_CONV:D_MIX, :] += _tn(oa, db)

        is_first_tile = step == n_tiles - 1
        zbuf[0:8, :] = jnp.where(is_first_tile, 0.0, hcc_ref[...] * hcu_ref[...])
        cb, cc, cu, z, z1, z2, conv = _conv_core(pc_ref, zbuf, cw_ref)
        silu, dsilu = _silu_and_grad(pc_ref[:, 3 * D_CONV:4 * D_CONV])
        yc = cb * conv
        rc = _rstd(yc)
        chat = yc * rc
        dn = do * silu
        dpc_ref[:, 3 * D_CONV:4 * D_CONV] = (do * (chat * gn_ref[...]) * dsilu).astype(BF16)
        small_add(SMALL_NORM_CONV, dn * chat)
        dyc = _rms_bwd(dn * gn_ref[...], chat, rc)
        dpc_ref[:, 0:D_CONV] = (dyc * conv).astype(BF16)
        dconv = dyc * cb
        small_add(SMALL_CONV_W, dconv * z2)
        small_add(SMALL_CONV_W + 1, dconv * z1)
        small_add(SMALL_CONV_W + 2, dconv * z)
        dbuf[0:tile, :] = dconv
        dz = cw_ref[2:3, :] * dconv + cw_ref[1:2, :] * dbuf[1:tile + 1, :] + cw_ref[0:1, :] * dbuf[2:tile + 2, :]
        dpc_ref[:, D_CONV:2 * D_CONV] = (dz * cu).astype(BF16)
        dpc_ref[:, 2 * D_CONV:3 * D_CONV] = (dz * cc).astype(BF16)
        dbuf[tile:tile + 8, :] = dbuf[0:8, :]

        @pl.when(step == n_tiles - 1)
        def _():
            lane = lax.broadcasted_iota(jnp.int32, (1, D_MODEL), 1)
            small_ref[SMALL_MISC:SMALL_MISC + 1, :] = jnp.where(lane == SMALL_LOSS_LANE, jnp.sum(loss_acc[...]), 0.0)

    def rev(i):
        return n_tiles - 1 - i

    def row(width):
        return pl.BlockSpec((tile, width), lambda i: (rev(i), 0))

    def halo(col_block):
        return pl.BlockSpec((8, D_CONV), lambda i: (jnp.maximum(rev(i) * (tile // 8) - 1, 0), col_block))

    def const(shape):
        return pl.BlockSpec(shape, lambda i: (0, 0))

    return pl.pallas_call(
        body, name="out_proj_loss", grid=(n_tiles,),
        out_shape=(jax.ShapeDtypeStruct((seq, D_MODEL), F32), jax.ShapeDtypeStruct((seq, D_ATTN), F32),
                   jax.ShapeDtypeStruct((D_MIX, D_MODEL), F32), jax.ShapeDtypeStruct((seq, D_PC), BF16),
                   jax.ShapeDtypeStruct((SMALL_ROWS, D_MODEL), F32)),
        in_specs=[row(D_MODEL), row(D_CONV), row(D_ATTN), _resident(wo.shape), _resident((1, D_MODEL)), row(D_MODEL),
                  row(D_PC), halo(1), halo(2), _resident(conv_w.shape), _resident((1, D_CONV))],
        out_specs=(row(D_MODEL), row(D_ATTN), const((D_MIX, D_MODEL)), row(D_PC), const((SMALL_ROWS, D_MODEL))),
        scratch_shapes=[pltpu.VMEM((1, D_MODEL), F32), pltpu.VMEM((tile + 8, D_CONV), F32),
                        pltpu.VMEM((tile + 8, D_CONV), F32)],
        compiler_params=_params(62, ("arbitrary",)),
    )(x, oc, oa, wo, norm_final, target, pc, pc, pc, conv_w, norm_conv_out)


def _attn_bwd(q, kv, ga, ya, doa, probs, sink_probs, norm_attn_out, gwo, small):
    seq = q.shape[0]
    per_step = 2
    n_steps = seq // (per_step * BLK)
    stage_steps = (0, 1, n_steps // 4, (5 * n_steps) // 8, n_steps - 1, n_steps - 1)

    def body(q_ref, kvc_ref, kvp_ref, ga_ref, ya_ref, doa_ref, pr_ref, sp_ref, gn_ref, gwo_ref, small_ref,
             dqg_ref, small_out, gwo_sh, gna_ref, gs_ref, carry, dya_buf, *rs_scratch):
        step = pl.program_id(0)
        rs_stages = _rs_wout_stages(gwo_ref, gwo_sh, *rs_scratch)
        for at, stage in zip(stage_steps[:-1], rs_stages[:-1]):
            pl.when(step == at)(stage)

        @pl.when(step == 0)
        def _():
            gna_ref[...] = jnp.zeros_like(gna_ref)
            gs_ref[...] = jnp.zeros_like(gs_ref)
            carry[...] = jnp.zeros_like(carry)

        lane = lax.broadcasted_iota(jnp.int32, (BLK, 128), 1)

        def fold(bd):
            return (jnp.where(lane < 64, bd[0:BLK], 0.0) + jnp.where(lane >= 64, bd[BLK:2 * BLK], 0.0),
                    jnp.where(lane < 64, bd[2 * BLK:3 * BLK], 0.0) + jnp.where(lane >= 64, bd[3 * BLK:4 * BLK], 0.0))

        def one_block(b):
            rows = slice(b * BLK, (b + 1) * BLK)
            kv_prev = kvp_ref if b == 0 else kvc_ref.at[(b - 1) * BLK:b * BLK, :]
            ya = ya_ref[rows, :]
            r = _rstd(ya)
            xhat = ya * r
            silu, dsilu = _silu_and_grad(ga_ref[rows, :])
            do = doa_ref[rows, :]
            dn = do * silu
            dqg_ref[rows, D_ATTN:2 * D_ATTN] = (do * (xhat * gn_ref[...]) * dsilu).astype(BF16)
            gna_ref[...] += jnp.sum(dn * xhat, axis=0, keepdims=True)
            dya_buf[rows, :] = _rms_bwd(dn * gn_ref[...], xhat, r).astype(BF16)

            use_cur = _band_geometry(per_step * (n_steps - 1 - step) + b)[0]
            pairs = range(N_HEADS // 2)
            qps, kbd, vbd = _attn_operands(q_ref.at[rows, :], kvc_ref.at[rows, :], kv_prev)
            probs = [pr_ref[rows, h * BLK:(h + 1) * BLK].astype(F32) for h in range(N_HEADS)]
            dyps = [dya_buf[rows, j * 128:(j + 1) * 128] for j in pairs]
            dps = []
            for j in pairs:
                dps += _merge(_nt(dyps[j], vbd[j // 4]), use_cur)
            deltas = [jnp.sum(p * dp, axis=-1, keepdims=True) for p, dp in zip(probs, dps)]
            dss = [p * (dp - delta) for p, dp, delta in zip(probs, dps, deltas)]
            delta_lanes = sum(jnp.where(lane == h, deltas[h], 0.0) for h in range(N_HEADS))
            gs_ref[...] -= jnp.sum(sp_ref[rows, :] * delta_lanes, axis=0, keepdims=True)
            ds4s = [_split(dss[2 * j], dss[2 * j + 1], use_cur).astype(BF16) for j in pairs]
            p4s = [_split(probs[2 * j], probs[2 * j + 1], use_cur).astype(BF16) for j in pairs]
            dqg_ref[rows, 0:D_ATTN] = jnp.concatenate([_nn(ds4s[j], kbd[j // 4]) * SCALE for j in pairs], axis=1).astype(BF16)
            sums = []
            for group in range(N_HEADS // HEADS_PER_KV):
                acc = [jnp.zeros((BLK, 128), F32) for _ in range(4)]
                for j in range(group * 4, group * 4 + 4):
                    for slot, part in enumerate(fold(_tn(ds4s[j], qps[j])) + fold(_tn(p4s[j], dyps[j]))):
                        acc[slot] = acc[slot] + part
                sums.append([a + pltpu.roll(a, 64, 1) for a in acc])
            dk_cur, dk_prev, dv_cur, dv_prev = (jnp.where(lane < 64, lo, hi) for lo, hi in zip(sums[0], sums[1]))
            dqg_ref[rows, 2 * D_ATTN:2 * D_ATTN + 2 * D_KV] = (jnp.concatenate([dk_cur, dv_cur], axis=1) + carry[...]).astype(BF16)
            carry[...] = jnp.concatenate([dk_prev, dv_prev], axis=1)

        for b in reversed(range(per_step)):
            one_block(b)

        @pl.when(step == n_steps - 1)
        def _():
            small_out[...] = small_ref[...]
            small_out[SMALL_NORM_ATTN:SMALL_NORM_ATTN + 1, :] = gna_ref[...]
            small_out[SMALL_MISC:SMALL_MISC + 1, 0:128] = small_ref[SMALL_MISC:SMALL_MISC + 1, 0:128] + gs_ref[...]

        pl.when(step == stage_steps[-1])(rs_stages[-1])

    def rows(width):
        return pl.BlockSpec((per_step * BLK, width), lambda i: (n_steps - 1 - i, 0))

    kv_prev = pl.BlockSpec((BLK, 2 * D_KV), lambda i: (jnp.maximum(per_step * (n_steps - 1 - i) - 1, 0), 0))
    any_spec = pl.BlockSpec(memory_space=pl.ANY)
    return pl.pallas_call(
        body, name="attn_bwd", grid=(n_steps,),
        out_shape=(jax.ShapeDtypeStruct((seq, D_QG), BF16), pltpu.HBM(small.shape, F32),
                   pltpu.HBM((gwo.shape[0] // N_CHIPS, gwo.shape[1]), F32)),
        in_specs=[rows(D_ATTN), rows(2 * D_KV), kv_prev, rows(D_ATTN), rows(D_ATTN), rows(D_ATTN),
                  rows(N_HEADS * BLK), rows(128), _resident((1, D_ATTN)), any_spec, _resident(small.shape)],
        out_specs=(rows(D_QG), pl.BlockSpec(small.shape, lambda i: (0, 0)), any_spec),
        scratch_shapes=[pltpu.VMEM((1, D_ATTN), F32), pltpu.VMEM((1, 128), F32),
                        pltpu.VMEM((BLK, 2 * D_KV), F32), pltpu.VMEM((per_step * BLK, D_ATTN), BF16)] + _rs_wout_scratch(gwo.shape),
        compiler_params=_params(44, ("arbitrary",)),
    )(q, kv, kv, ga, ya, doa, probs, sink_probs, norm_attn_out, gwo, small)


GBLK = 256
GSUB = 64
PAIR_RING = 4
LAG_PAIR, LAG_HOP1, LAG_HOP2 = 1, 7, 13


def _bwd_in(dpc, dqg, h, wt, x, norm_in, dx2, small):
    seq = x.shape[0]
    n_blk = D_IN_PROJ // GBLK
    per_chip = n_blk // N_CHIPS
    n_slots = (n_blk + 1) // 2
    n_sub = GBLK // GSUB
    chip_rows = D_IN_PROJ // N_CHIPS
    tile = TOK_TILE
    n_tiles = seq // tile
    n_steps = n_blk + max(n_tiles, LAG_HOP2 + 1)
    chunk = min(seq, 512)
    blk_q, blk_kv, blk_ga = ROW_Q // GBLK, ROW_KV // GBLK, ROW_GA // GBLK

    def block_of(i):
        k = i % N_CHIPS
        robin = per_chip * ((k % 2) * 2 + k // 2) + i // N_CHIPS
        if isinstance(i, int):
            return robin if i < per_chip * N_CHIPS else i
        return jnp.where(i < per_chip * N_CHIPS, robin, i)

    def owner_of(i):
        return (i // N_CHIPS) % 2

    def slot_of(i):
        return (i // (2 * N_CHIPS)) * N_CHIPS + i % N_CHIPS

    def body(dpc_ref, dqg_ref, wt_ref, h_ref, x_ref, g_ref, dx2_ref, small_ref, gx_ref, small_sum, gwt_sh,
             dh_acc, gni, keep, pbuf, xbuf, land, land2, small_land,
             pair_send, pair_recv, h1_send, h1_recv, h2_send, h2_recv, sw_send, sw_recv, sm_send, sm_recv, out_sem):
        step = pl.program_id(0)
        x_i, y_i, c = lax.axis_index("x"), lax.axis_index("y"), lax.axis_index("c")
        me = 4 * x_i + 2 * y_i + c
        j = 2 * x_i + y_i
        pa = (_xor(x_i, 1 - c), _xor(y_i, c), c)
        pb = (_xor(x_i, c), _xor(y_i, 1 - c), c)
        sib = (x_i, y_i, 1 - c)
        ja = 2 * pa[0] + pa[1]
        jb = 2 * pb[0] + pb[1]
        jd = 3 - j

        def remote(src, dst, send, recv, to):
            return pltpu.make_async_remote_copy(src_ref=src, dst_ref=dst, send_sem=send, recv_sem=recv,
                                                device_id=to, device_id_type=MESH)

        def piece(ref, slot, u, n):
            return ref.at[slot, pl.ds(u * GSUB, n * GSUB), :]

        def chip_rows_at(ref, local, n):
            return ref.at[pl.ds(pl.multiple_of(local, GSUB), n * GSUB), :]

        def pair_copy(i):
            slot = slot_of(i)
            return remote(pbuf.at[i % PAIR_RING], land.at[slot], pair_send.at[slot], pair_recv.at[slot], sib)

        def h1_copy(slot, u, n):
            k = slot * n_sub + u
            return remote(piece(xbuf, slot, u, n), piece(xbuf, slot, u, n), h1_send.at[k], h1_recv.at[k], pa)

        def h2_copy(slot, u, n, local):
            k = slot * n_sub + u
            return remote(piece(xbuf, slot, u, n), chip_rows_at(land2, local, n), h2_send.at[k], h2_recv.at[k], pb)

        def sw_copy(slot, u, n, local):
            k = slot * n_sub + u
            return remote(piece(keep, slot, u, n), chip_rows_at(gwt_sh, local, n), sw_send.at[k], sw_recv.at[k], sib)

        def own_copy(slot, u, n, local):
            return pltpu.make_async_copy(piece(keep, slot, u, n), chip_rows_at(gwt_sh, local, n), out_sem.at[slot * n_sub + u])

        def owned(i):
            return (i >= 0) & (i < n_blk) & (owner_of(i) == c)

        def chip_of(blk, u):
            row = blk * GBLK + u * GSUB
            chip = row // chip_rows
            return chip, row - chip * chip_rows

        def pieces(blk):
            first, local = chip_of(blk, 0)
            whole = first == chip_of(blk, n_sub - 1)[0]
            if isinstance(blk, int):
                return [(True, 0, n_sub, first, local)] if whole else [(True, u, 1) + chip_of(blk, u) for u in range(n_sub)]
            return [(whole, 0, n_sub, first, local)] + [(jnp.logical_not(whole), u, 1) + chip_of(blk, u) for u in range(n_sub)]

        @pl.when(step == 0)
        def _():
            dh_acc[...] = jnp.zeros_like(dh_acc)
            gni[...] = jnp.zeros_like(gni)

        @pl.when(step < n_blk)
        def _():
            from_pc = block_of(step) < blk_q
            block = _tn(jnp.where(from_pc, dpc_ref[...], dqg_ref[...]), h_ref[...])
            for t in range(0, seq, chunk):
                d = jnp.where(from_pc, dpc_ref[t:t + chunk, :], dqg_ref[t:t + chunk, :])
                dh_acc[t:t + chunk, :] += _nn(d, wt_ref[...])

            @pl.when(owner_of(step) == c)
            def _():
                keep[slot_of(step)] = block

            @pl.when(owner_of(step) != c)
            def _():
                @pl.when(step >= 2 * PAIR_RING)
                def _():
                    pair_copy(step - 2 * PAIR_RING).wait_send()
                pbuf[step % PAIR_RING] = block.astype(BF16)
                pair_copy(step).start()

        i1 = step - LAG_PAIR

        @pl.when(owned(i1))
        def _():
            slot = slot_of(i1)
            pair_copy(i1).wait_recv()
            _accumulate(keep.at[slot], land.at[slot])
            for cond, u, n, chip, _ in pieces(block_of(i1)):
                @pl.when(cond & ((chip == ja) | (chip == jd)))
                def _(u=u, n=n):
                    _cast_rows(piece(keep, slot, u, n), piece(xbuf, slot, u, n))
                    h1_copy(slot, u, n).start()

        i2 = step - LAG_HOP1

        @pl.when(owned(i2))
        def _():
            slot = slot_of(i2)
            for cond, u, n, chip, local in pieces(block_of(i2)):
                @pl.when(cond & ((chip == j) | (chip == jb)))
                def _(u=u, n=n, chip=chip, local=local):
                    h1_copy(slot, u, n).wait_recv()
                    _accumulate(piece(keep, slot, u, n), piece(xbuf, slot, u, n))

                    @pl.when(chip == jb)
                    def _():
                        _cast_rows(piece(keep, slot, u, n), piece(xbuf, slot, u, n))
                        h2_copy(slot, u, n, local).start()

                @pl.when(cond & ((chip == ja) | (chip == jd)))
                def _(u=u, n=n):
                    h1_copy(slot, u, n).wait_send()

        i3 = step - LAG_HOP2

        @pl.when(owned(i3))
        def _():
            slot = slot_of(i3)
            for cond, u, n, chip, local in pieces(block_of(i3)):
                @pl.when(cond & (chip == j))
                def _(u=u, n=n, local=local):
                    h2_copy(slot, u, n, local).wait_recv()
                    _accumulate(piece(keep, slot, u, n), chip_rows_at(land2, local, n))
                    own_copy(slot, u, n, local).start()
                    sw_copy(slot, u, n, local).start()

                @pl.when(cond & (chip == jb))
                def _(u=u, n=n, local=local):
                    h2_copy(slot, u, n, local).wait_send()

        e = step - n_blk

        @pl.when((e >= 0) & (e < n_tiles))
        def _():
            dh = dh_acc[pl.ds(pl.multiple_of(e * tile, tile), tile), :]
            xv = x_ref[...]
            r = _rstd(xv)
            xhat = xv * r
            gni[...] += jnp.sum(dh * xhat, axis=0, keepdims=True)
            gx_ref[...] = _rms_bwd(dh * g_ref[...], xhat, r) + dx2_ref[...]

        others = [(dx, dy, dc) for dx in (0, 1) for dy in (0, 1) for dc in (0, 1)][1:]
        sends = [remote(small_land.at[me], small_land.at[me], sm_send.at[k], sm_recv.at[k],
                        (_xor(x_i, dx), _xor(y_i, dy), _xor(c, dc))) for k, (dx, dy, dc) in enumerate(others)]

        @pl.when(step == min(n_blk + n_tiles, n_steps - 1))
        def _():
            small_land[me] = small_ref[...]
            small_land[me, SMALL_NORM_IN:SMALL_NORM_IN + 1, :] = gni[...]
            for cp in sends:
                cp.start()

        @pl.when(step == n_steps - 1)
        def _():
            for i in range(n_blk):
                if i + 2 * PAIR_RING >= n_blk:
                    @pl.when(owner_of(i) != c)
                    def _(i=i):
                        pair_copy(i).wait_send()
                for _, u, n, chip, local in pieces(block_of(i)):
                    @pl.when((j == chip) & (c == owner_of(i)))
                    def _(i=i, u=u, n=n, local=local):
                        sw_copy(slot_of(i), u, n, local).wait_send()
                        own_copy(slot_of(i), u, n, local).wait()

                    @pl.when((j == chip) & (c != owner_of(i)))
                    def _(i=i, u=u, n=n, local=local):
                        sw_copy(slot_of(i), u, n, local).wait_recv()
            for cp in sends:
                cp.wait_recv()
            total = small_land[0]
            for dev in range(1, 8):
                total = total + small_land[dev]
            small_sum[...] = total
            for cp in sends:
                cp.wait_send()

    def blk_at(i):
        return block_of(jnp.clip(i, 0, n_blk - 1))

    last_pc_step = max(i for i in range(n_blk) if block_of(i) < blk_q)

    def next_block(i, in_pc):
        i = jnp.clip(i, 0, n_blk - 1)
        step = jnp.full_like(i, last_pc_step if in_pc else n_blk - 1)
        for ahead in reversed(range(N_CHIPS)):
            cand = jnp.minimum(i + ahead, n_blk - 1)
            step = jnp.where((block_of(cand) < blk_q) == in_pc, cand, step)
        return block_of(step)

    def dqg_block(i):
        b = next_block(i, False)
        q_blk = jnp.clip(b - blk_q, 0, blk_kv - blk_q - 1)
        ga_blk = (D_ATTN // GBLK) + jnp.clip(b - blk_ga, 0, n_blk - blk_ga - 1)
        return jnp.where(b < blk_kv, q_blk, jnp.where(b == blk_kv, 2 * D_ATTN // GBLK, ga_blk))

    def tok(i):
        return (jnp.clip(i - n_blk, 0, n_tiles - 1), 0)

    n_piece = n_slots * n_sub
    dma = pltpu.SemaphoreType.DMA
    return pl.pallas_call(
        body, name="bwd_in", grid=(n_steps,),
        out_shape=(jax.ShapeDtypeStruct((seq, D_MODEL), F32), jax.ShapeDtypeStruct(small.shape, F32),
                   jax.ShapeDtypeStruct((chip_rows, D_MODEL), F32)),
        in_specs=[pl.BlockSpec((seq, GBLK), lambda i: (0, next_block(i, True))),
                  pl.BlockSpec((seq, GBLK), lambda i: (0, dqg_block(i))),
                  pl.BlockSpec((GBLK, D_MODEL), lambda i: (blk_at(i), 0)),
                  _resident(h.shape),
                  pl.BlockSpec((tile, D_MODEL), tok), _resident((1, D_MODEL)), pl.BlockSpec((tile, D_MODEL), tok),
                  _resident(small.shape)],
        out_specs=(pl.BlockSpec((tile, D_MODEL), tok), pl.BlockSpec(small.shape, lambda i: (0, 0)),
                   pl.BlockSpec(memory_space=pl.ANY)),
        scratch_shapes=[pltpu.VMEM((seq, D_MODEL), F32), pltpu.VMEM((1, D_MODEL), F32),
                        pltpu.VMEM((n_slots, GBLK, D_MODEL), F32), pltpu.VMEM((PAIR_RING, GBLK, D_MODEL), BF16),
                        pltpu.VMEM((n_slots, GBLK, D_MODEL), BF16), pltpu.VMEM((n_slots, GBLK, D_MODEL), BF16),
                        pltpu.VMEM((chip_rows, D_MODEL), BF16), pltpu.VMEM((8,) + small.shape, F32),
                        dma((n_slots,)), dma((n_slots,)), dma((n_piece,)), dma((n_piece,)), dma((n_piece,)),
                        dma((n_piece,)), dma((n_piece,)), dma((n_piece,)), dma((7,)), dma((7,)), dma((n_piece,))],
        compiler_params=_params(62, ("arbitrary",)),
    )(dpc, dqg, wt, h, x, norm_in, dx2, small)


def _accumulate(dst_ref, src_ref, rows=16):
    def step(i, carry):
        sl = pl.ds(pl.multiple_of(i * rows, rows), rows)
        dst_ref[sl, :] = dst_ref[sl, :] + src_ref[sl, :].astype(F32)
        return carry
    lax.fori_loop(0, dst_ref.shape[0] // rows, step, 0)


def _rs_wout_scratch(gwo_shape):
    o_half, width = gwo_shape[0] // N_CHIPS // 2, gwo_shape[1]
    return [pltpu.VMEM((4, o_half, width), F32), pltpu.VMEM((4, o_half, width), BF16),
            pltpu.VMEM((4, o_half, width), BF16), pltpu.VMEM((2, o_half, width), BF16),
            pltpu.VMEM((o_half, width), BF16),
            pltpu.SemaphoreType.DMA((8,)), pltpu.SemaphoreType.DMA((8,)), pltpu.SemaphoreType.DMA((4,))]


def _rs_wout_stages(gwo_ref, gwo_sh, acc_o, sb_o, r1o, r2o, r3o, send_sems, recv_sems, local_sems):
    o_rows = gwo_ref.shape[0] // N_CHIPS
    o_half = o_rows // 2
    x, y, c = lax.axis_index("x"), lax.axis_index("y"), lax.axis_index("c")
    j = 2 * x + y
    pa = (_xor(x, 1 - c), _xor(y, c), c)
    pb = (_xor(x, c), _xor(y, 1 - c), c)
    sib = (x, y, 1 - c)
    ja = 2 * pa[0] + pa[1]
    jb = 2 * pb[0] + pb[1]
    jd = 3 - j
    order = (ja, jd, jb, j)
    sib_order = (jb, jd, ja, j)

    def rcopy(k, src, dst, to):
        return pltpu.make_async_remote_copy(src_ref=src, dst_ref=dst, send_sem=send_sems.at[k],
                                            recv_sem=recv_sems.at[k], device_id=to, device_id_type=MESH)

    def o_rows_of(chip, half):
        return gwo_ref.at[pl.ds(pl.multiple_of(chip * o_rows + half * o_half, 8), o_half), :]

    def loads(chips, half):
        return [pltpu.make_async_copy(o_rows_of(chip, half), acc_o.at[s], local_sems.at[s]) for s, chip in enumerate(chips)]

    def pair_send(s):
        return rcopy(s, sb_o.at[s], r1o.at[s], sib)

    def out_half(half):
        return gwo_sh.at[pl.ds(pl.multiple_of(half * o_half, 8), o_half), :]

    hop1 = [rcopy(4 + s, sb_o.at[s], r2o.at[s], pa) for s in range(2)]
    hop2 = rcopy(6, sb_o.at[2], r3o, pb)
    swap = rcopy(7, acc_o.at[3], out_half(c), sib)
    mine = pltpu.make_async_copy(acc_o.at[3], out_half(c), local_sems.at[0])

    def resend(s, copy):
        pair_send(s).wait_send()
        _cast_rows(acc_o.at[s], sb_o.at[s])
        copy.start()

    def stage_load():
        for cp in loads(sib_order, 1 - c):
            cp.start()

    def stage_pair():
        for s, (cp, mine_in) in enumerate(zip(loads(sib_order, 1 - c), loads(order, c))):
            cp.wait()
            _cast_rows(acc_o.at[s], sb_o.at[s])
            pair_send(s).start()
            mine_in.start()

    def stage_hop1():
        for s, cp in enumerate(loads(order, c)):
            cp.wait()
            pair_send(s).wait_recv()
            _accumulate(acc_o.at[s], r1o.at[s])
            if s < 2:
                resend(s, hop1[s])

    def stage_hop2():
        hop1[1].wait_recv()
        _accumulate(acc_o.at[2], r2o.at[1])
        resend(2, hop2)
        hop1[0].wait_recv()
        _accumulate(acc_o.at[3], r2o.at[0])

    def stage_final():
        hop2.wait_recv()
        _accumulate(acc_o.at[3], r3o)
        mine.start()
        swap.start()

    def stage_drain():
        rcopy(7, acc_o.at[3], out_half(1 - c), sib).wait_recv()
        for cp in [pair_send(3)] + hop1 + [hop2, swap]:
            cp.wait_send()
        mine.wait()

    return stage_load, stage_pair, stage_hop1, stage_hop2, stage_final, stage_drain


def _adamw_all(groups, passed, chip, small_sum, weights, grads_of, ms, vs):
    arrays = [a for group in groups for a in group[:4]]
    steps = [group[0].shape[0] // group[4] for group in groups]
    first = [sum(steps[:k]) for k in range(len(steps) + 1)]
    n, ns = len(arrays), len(weights)
    small_in = [a for group in zip(weights, ms, vs) for a in group]

    def body(chip_ref, *refs):
        ins, passed_in, small_ref, sins = refs[:n], refs[n], refs[n + 1], refs[n + 2:n + 2 + 3 * ns]
        outs = refs[n + 2 + 3 * ns:]
        big_outs, passed_out, souts, loss_ref = outs[:n], outs[n], outs[n + 1:n + 1 + 4 * ns], outs[n + 1 + 4 * ns]
        i = pl.program_id(0)
        for k in range(len(groups)):
            @pl.when((i >= first[k]) & (i < first[k + 1]))
            def _(k=k):
                w_ref, g_ref, m_ref, v_ref = ins[4 * k:4 * k + 4]
                _adamw_update(w_ref, g_ref[...], m_ref, v_ref, *big_outs[4 * k:4 * k + 4])

        @pl.when(i < steps[0])
        def _():
            passed_out[...] = passed_in[...]

        @pl.when(i == 0)
        def _():
            for k in range(ns):
                w_ref, m_ref, v_ref = sins[3 * k:3 * k + 3]
                for at, gv in grads_of[k](small_ref, chip_ref):
                    _adamw_update(w_ref, gv, m_ref, v_ref, *souts[4 * k:4 * k + 4], at=at)
            loss_ref[...] = small_ref[SMALL_MISC:SMALL_MISC + 1, SMALL_LOSS_LANE:SMALL_LOSS_LANE + 1]

    def spec(k, rows, width):
        return pl.BlockSpec((rows, width), lambda i: (jnp.clip(i - first[k], 0, steps[k] - 1), 0))

    def whole(shape):
        return pl.BlockSpec(shape, lambda i: (0,) * len(shape))

    specs = [spec(k, group[4], group[0].shape[1]) for k, group in enumerate(groups) for _ in range(4)]
    specs.append(spec(0, passed.shape[0] // steps[0], passed.shape[1]))
    small_out_shapes = [w.shape for w in weights for _ in range(4)] + [(1, 1)]
    out = pl.pallas_call(
        body, name="adamw", grid=(first[-1],),
        out_shape=tuple(jax.ShapeDtypeStruct(a.shape, a.dtype) for a in arrays + [passed])
        + tuple(jax.ShapeDtypeStruct(shape, F32) for shape in small_out_shapes),
        in_specs=[pl.BlockSpec(memory_space=pltpu.SMEM)] + specs + [whole(small_sum.shape)]
        + [whole(a.shape) for a in small_in],
        out_specs=tuple(specs) + tuple(whole(shape) for shape in small_out_shapes),
        compiler_params=_params(40, ("arbitrary",)),
    )(chip, *arrays, passed, small_sum, *small_in)
    big, small = out[:n], out[n + 1:n + 1 + 4 * ns]
    return ([tuple(big[4 * k:4 * k + 4]) for k in range(len(groups))], out[n],
            [tuple(small[4 * k:4 * k + 4]) for k in range(ns)], out[-1][0, 0])


def _adamw_update(w_ref, gv, m_ref, v_ref, go_ref, d_ref, nm_ref, nv_ref, at=...):
    go_ref[at] = gv
    nm = ADAM_B1 * m_ref[at] + (1.0 - ADAM_B1) * gv
    nv = ADAM_B2 * v_ref[at] + (1.0 - ADAM_B2) * (gv * gv)
    m_hat = nm / (1.0 - ADAM_B1 ** ADAM_STEP)
    v_hat = nv / (1.0 - ADAM_B2 ** ADAM_STEP)
    d_ref[at] = -ADAM_LR * (m_hat / (jnp.sqrt(v_hat) + ADAM_EPS) + ADAM_WD * w_ref[at])
    nm_ref[at] = nm
    nv_ref[at] = nv


def kernel(x, norm_in, w_in, conv_w, attn_sinks, norm_conv_out, norm_attn_out, w_out, norm_final, loss_target, m_norm_in, m_w_in, m_conv_w, m_attn_sinks, m_norm_conv_out, m_norm_attn_out, m_w_out, m_norm_final, v_norm_in, v_w_in, v_conv_w, v_attn_sinks, v_norm_conv_out, v_norm_attn_out, v_w_out, v_norm_final):
    chip = 2 * lax.axis_index("x") + lax.axis_index("y")
    xs, target = x[0], loss_target[0]
    norm_final2 = norm_final.reshape(1, D_MODEL)
    w_in_t, m_w_in_t, v_w_in_t = w_in[0].T, m_w_in[0].T, v_w_in[0].T

    def from_hbm(*arrays):
        return tuple(pltpu.with_memory_space_constraint(a, pltpu.HBM) for a in arrays)

    h, pc, q, kv, ga, oc, ya, oa, probs, sink_probs, wt, cw, wo = _fwd_in(
        xs, norm_in, w_in_t, w_out[0], conv_w.transpose(1, 0, 2), norm_conv_out, attn_sinks, norm_attn_out)
    dx2, doa, gwo, dpc, small = _out_proj_loss(xs, oc, oa, wo, norm_final2, target, pc, cw, norm_conv_out)
    q, kv, ga, ya, doa, probs, gwo, small = from_hbm(q, kv, ga, ya, doa, probs, gwo, small)
    dqg, small, gwo_sh = _attn_bwd(q, kv, ga, ya, doa, probs, sink_probs, norm_attn_out, gwo, small)
    grad_x, small_sum, gwt_sh = _bwd_in(dpc, dqg, h, wt, xs, norm_in, dx2, small)

    def row_of(r):
        return lambda small_ref, chip_ref: [(..., small_ref[r:r + 1, :])]

    def sink_lanes(small_ref, chip_ref):
        return [(..., small_ref[SMALL_MISC:SMALL_MISC + 1, 0:N_HEADS])]

    def conv_taps(small_ref, chip_ref):
        width = D_CONV // N_CHIPS
        cols = pl.ds(pl.multiple_of(chip_ref[0] * width, width), width)
        return [(k, small_ref[pl.ds(SMALL_CONV_W + k, 1), cols]) for k in range(conv_w.shape[1])]

    def small_view(a):
        return a.transpose(1, 0, 2) if a.ndim == 3 else a.reshape(-1, a.shape[-1])

    small_w = (norm_in, conv_w, attn_sinks, norm_conv_out, norm_attn_out, norm_final)
    small_m = (m_norm_in, m_conv_w, m_attn_sinks, m_norm_conv_out, m_norm_attn_out, m_norm_final)
    small_v = (v_norm_in, v_conv_w, v_attn_sinks, v_norm_conv_out, v_norm_attn_out, v_norm_final)
    small_g = (row_of(SMALL_NORM_IN), conv_taps, sink_lanes, row_of(SMALL_NORM_CONV), row_of(SMALL_NORM_ATTN),
               row_of(SMALL_NORM_FINAL))
    w_views, m_views, v_views = (tuple(small_view(a) for a in group) for group in (small_w, small_m, small_v))
    (up_w_in, up_w_out), grad_x, up_small, loss = _adamw_all(
        [from_hbm(w_in_t, gwt_sh, m_w_in_t, v_w_in_t) + (200,), from_hbm(w_out[0], gwo_sh, m_w_out[0], v_w_out[0]) + (128,)],
        *from_hbm(grad_x), chip.astype(jnp.int32).reshape(1), small_sum, w_views, small_g, m_views, v_views)
    up_w_in = tuple(o.T[None] for o in up_w_in)
    up_w_out = tuple(o[None] for o in up_w_out)
    up_small = [tuple(o.transpose(1, 0, 2) if w.ndim == 3 else o.reshape(w.shape) for o in up)
                for up, w in zip(up_small, small_w)]
    up_norm_in, up_conv_w, up_sinks, up_norm_conv, up_norm_attn, up_norm_final = up_small
    updates = (up_norm_in, up_w_in, up_conv_w, up_sinks, up_norm_conv, up_norm_attn, up_w_out, up_norm_final)
    grads_out, deltas, new_m, new_v = zip(*updates)
    return (loss, grad_x[None], *grads_out, *deltas, *new_m, *new_v)
```

```python
import jax
import jax.numpy as jnp
from jax import lax
from jax.experimental import pallas as pl
from jax.experimental.pallas import tpu as pltpu

F32 = jnp.float32
BF16 = jnp.bfloat16
MESH = pl.DeviceIdType.MESH

D_MODEL = 1024
D_CONV = 1024
D_ATTN = 1024
D_KV = 128
D_QG = 2 * D_ATTN + 2 * D_KV
D_MIX = D_CONV + D_ATTN
D_PC = 4 * D_CONV
D_IN_PROJ = D_PC + 2 * D_ATTN + 2 * D_KV
ROW_Q = D_PC
ROW_KV = ROW_Q + D_ATTN
ROW_GA = ROW_KV + 2 * D_KV
N_HEADS = 16
HEAD_DIM = 64
HEADS_PER_KV = 8
BLK = 128
N_CHIPS = 4
RMS_EPS = 1e-5
SCALE = HEAD_DIM ** -0.5
SLOPES = tuple(2.0 ** (-8.0 * (h + 1) / N_HEADS) for h in range(N_HEADS))

ADAM_LR, ADAM_B1, ADAM_B2, ADAM_EPS, ADAM_WD, ADAM_STEP = 0.001, 0.9, 0.999, 1e-08, 0.01, 10

SMALL_ROWS = 8
SMALL_NORM_IN, SMALL_NORM_CONV, SMALL_NORM_ATTN, SMALL_NORM_FINAL, SMALL_CONV_W, SMALL_MISC = 0, 1, 2, 3, 4, 7
SMALL_LOSS_LANE = N_HEADS

TOK_TILE = 256
PC_PIECE = 512
MIB = 1 << 20


def _params(vmem_mib, semantics=None):
    return pltpu.CompilerParams(dimension_semantics=semantics, vmem_limit_bytes=vmem_mib * MIB)


def _nn(a, b):
    return jnp.dot(a, b, preferred_element_type=F32)


def _nt(a, b):
    return lax.dot_general(a, b, (((1,), (1,)), ((), ())), preferred_element_type=F32)


def _tn(a, b):
    return lax.dot_general(a, b, (((0,), (0,)), ((), ())), preferred_element_type=F32)


def _rstd(v):
    return lax.rsqrt(jnp.mean(v * v, axis=-1, keepdims=True) + RMS_EPS)


def _rms_bwd(g, xhat, rstd):
    return rstd * (g - xhat * jnp.mean(g * xhat, axis=-1, keepdims=True))


def _silu_and_grad(g):
    s = jax.nn.sigmoid(g)
    return g * s, s * (1.0 + g * (1.0 - s))


def _resident(shape):
    return pl.BlockSpec(shape, lambda *_: (0,) * len(shape), pipeline_mode=pl.Buffered(1))


def _xor(a, b):
    return a + b - 2 * a * b


def _cast_rows(src_ref, dst_ref, rows=32):
    def step(i, carry):
        sl = pl.ds(pl.multiple_of(i * rows, rows), rows)
        dst_ref[sl, :] = src_ref[sl, :].astype(dst_ref.dtype)
        return carry
    lax.fori_loop(0, src_ref.shape[0] // rows, step, 0)


def _ag_scratch(shard_shape):
    rows, width = shard_shape
    return [pltpu.VMEM((rows, width), F32), pltpu.VMEM((rows, width), BF16), pltpu.VMEM((3, rows // 2, width), BF16),
            pltpu.SemaphoreType.DMA((6,)), pltpu.SemaphoreType.DMA((6,)), pltpu.SemaphoreType.DMA((4,))]


def _ag_stages(sh_ref, out, f32_buf, own, land, send_sems, recv_sems, local_sems):
    rows = sh_ref.shape[0]
    half = rows // 2
    x, y, c = lax.axis_index("x"), lax.axis_index("y"), lax.axis_index("c")
    j = 2 * x + y
    p1 = (_xor(x, c), _xor(y, 1 - c), c)
    p2 = (_xor(x, 1 - c), _xor(y, c), c)
    sib = (x, y, 1 - c)
    j1 = 2 * p1[0] + p1[1]
    j2 = 2 * p2[0] + p2[1]
    j3 = 3 - j

    def rows_of(chip, hf):
        return out.at[pl.ds(pl.multiple_of(chip * rows + hf * half, 16), half), :]

    def rcopy(k, src, dst, to):
        return pltpu.make_async_remote_copy(src_ref=src, dst_ref=dst, send_sem=send_sems.at[k],
                                            recv_sem=recv_sems.at[k], device_id=to, device_id_type=MESH)

    my_half = own.at[pl.ds(pl.multiple_of(c * half, 16), half), :]
    hop1 = rcopy(0, my_half, land.at[0], p1)
    hop2_own = rcopy(1, my_half, land.at[1], p2)
    hop2_fwd = rcopy(2, land.at[0], land.at[2], p2)
    swaps = [rcopy(3 + s, land.at[s], rows_of(chip, c), sib) for s, chip in enumerate((j1, j2, j3))]
    keeps = [pltpu.make_async_copy(land.at[s], rows_of(chip, c), local_sems.at[1 + s]) for s, chip in enumerate((j1, j2, j3))]
    load = pltpu.make_async_copy(sh_ref, f32_buf, local_sems.at[0])
    own_out = pltpu.make_async_copy(own, out.at[pl.ds(pl.multiple_of(j * rows, 16), rows), :], local_sems.at[0])

    def stage_load():
        load.start()

    def stage_send():
        load.wait()
        _cast_rows(f32_buf, own)
        own_out.start()
        hop1.start()

    def stage_forward():
        hop1.wait_recv()
        hop2_own.start()
        hop2_fwd.start()
        swaps[0].start()
        keeps[0].start()

    def stage_publish():
        hop2_own.wait_recv()
        swaps[1].start()
        keeps[1].start()
        hop2_fwd.wait_recv()
        swaps[2].start()
        keeps[2].start()

    def stage_drain():
        for s, chip in enumerate((j2, j1, j3)):
            rcopy(3 + s, my_half, rows_of(chip, 1 - c), sib).wait_recv()
        for cp in [hop1, hop2_own, hop2_fwd] + swaps:
            cp.wait_send()
        for cp in [own_out] + keeps:
            cp.wait()

    return stage_load, stage_send, stage_forward, stage_publish, stage_drain


AG_CAST_ROWS = 400


def _gather_resident(sh_ref, out, f32_buf, send_sems, recv_sems, local_sems, after_first_hop):
    rows = sh_ref.shape[0]
    half = rows // 2
    x, y, c = lax.axis_index("x"), lax.axis_index("y"), lax.axis_index("c")
    j = 2 * x + y
    p1 = (_xor(x, c), _xor(y, 1 - c), c)
    p2 = (_xor(x, 1 - c), _xor(y, c), c)
    sib = (x, y, 1 - c)
    j1 = 2 * p1[0] + p1[1]
    j2 = 2 * p2[0] + p2[1]
    j3 = 3 - j

    def rows_of(chip, hf):
        return out.at[pl.ds(pl.multiple_of(chip * rows + hf * half, 16), half), :]

    def send(k, chip, to):
        return pltpu.make_async_remote_copy(src_ref=rows_of(chip, c), dst_ref=rows_of(chip, c), send_sem=send_sems.at[k],
                                            recv_sem=recv_sems.at[k], device_id=to, device_id_type=MESH)

    per_half = half // AG_CAST_ROWS

    chunks = [c * per_half + k for k in range(per_half)] + [(1 - c) * per_half + k for k in range(per_half)]

    def load(n):
        lo = pl.multiple_of(chunks[n] * AG_CAST_ROWS, 16)
        return pltpu.make_async_copy(sh_ref.at[pl.ds(lo, AG_CAST_ROWS), :], f32_buf.at[n % 2], local_sems.at[n % 2])

    def send_piece(k, chip, q, to):
        at = out.at[pl.ds(pl.multiple_of(chip * rows + c * half + q * (half // 2), 16), half // 2), :]
        return pltpu.make_async_remote_copy(src_ref=at, dst_ref=at, send_sem=send_sems.at[k], recv_sem=recv_sems.at[k],
                                            device_id=to, device_id_type=MESH)

    forwards = [send_piece(2, j1, 0, p2), send_piece(6, j1, 1, p2)]
    sends = [send(0, j, p1), send(1, j, p2)] + forwards + [send(3, j1, sib)]
    load(0).start()
    load(1).start()
    for n in range(len(chunks)):
        load(n).wait()
        lo = pl.multiple_of(chunks[n] * AG_CAST_ROWS, 16)
        _cast_rows(f32_buf.at[n % 2], out.at[pl.ds(pl.multiple_of(j * rows + lo, 16), AG_CAST_ROWS), :], rows=16)
        if n + 2 < len(chunks):
            load(n + 2).start()
        if n == per_half - 1:
            sends[0].start()
            sends[1].start()
    sends[0].wait_recv()
    for cp in sends[2:]:
        cp.start()
    after_first_hop()
    sends[1].wait_recv()
    sends.append(send(4, j2, sib))
    sends[-1].start()
    for q, (k, arrived) in enumerate(zip((5, 7), forwards)):
        arrived.wait_recv()
        sends.append(send_piece(k, j3, q, sib))
        sends[-1].start()
    for k in (3, 4):
        send(k, j, sib).wait_recv()
    for k in (5, 7):
        send_piece(k, j, 0, sib).wait_recv()
    for cp in sends:
        cp.wait_send()


def _fwd_in(x, norm_in, wt_sh, wo_sh, cw_sh, norm_conv_out, sinks, norm_attn_out):
    seq = x.shape[0]
    tile = TOK_TILE
    n_tiles = seq // tile
    stage_steps = (0, n_tiles // 4, (5 * n_tiles) // 8, n_tiles - 1)
    blocks = tile // BLK
    cw_cols = cw_sh.shape[-1]

    def body(x_ref, g_ref, wtsh_ref, wo_ref, cwsh_ref, gn_ref, sink_ref, gna_ref,
             h_ref, pc_ref, q_ref, kv_ref, ga_ref, oc_ref, ya_ref, oa_ref, pr_ref, sp_ref, wt_out, cw_ref, wo_out,
             zbuf, kv_last, wt_ref, f32_buf, cw_land, wt_send, wt_recv, wt_local, cw_send, cw_recv, cw_local, *ag_scratch):
        step = pl.program_id(0)
        to_hbm = pltpu.make_async_copy(wt_ref, wt_out, wt_local.at[0])
        load_wo, *stages = _ag_stages(wo_ref, wo_out, *ag_scratch)

        @pl.when(step == 0)
        def _():
            load_wo()
            zbuf[0:8, :] = jnp.zeros((8, D_CONV), F32)
            kv_last[...] = jnp.zeros_like(kv_last)
            x_i, y_i, c = lax.axis_index("x"), lax.axis_index("y"), lax.axis_index("c")
            j = 2 * x_i + y_i
            p1 = (_xor(x_i, c), _xor(y_i, 1 - c), c)
            p2 = (_xor(x_i, 1 - c), _xor(y_i, c), c)
            j1 = 2 * p1[0] + p1[1]

            def cw_copy(k, src, chip, to):
                return pltpu.make_async_remote_copy(src_ref=src, dst_ref=cw_land.at[chip], send_sem=cw_send.at[k],
                                                    recv_sem=cw_recv.at[k], device_id=to, device_id_type=MESH)

            mine = pltpu.make_async_copy(cwsh_ref, cw_land.at[j], cw_local.at[0])
            mine.start()
            first = cw_copy(0, cwsh_ref, j, p1)
            first.start()
            second = [cw_copy(1, cwsh_ref, j, p2), cw_copy(2, cw_land.at[j1], j1, p2)]

            def cw_second_hop():
                first.wait_recv()
                for cp in second:
                    cp.start()

            _gather_resident(wtsh_ref, wt_ref, f32_buf, wt_send, wt_recv, wt_local, cw_second_hop)
            to_hbm.start()
            for cp in second:
                cp.wait_recv()
            for cp in [first] + second:
                cp.wait_send()
            mine.wait()
            for chip in range(N_CHIPS):
                for tap in range(cw_sh.shape[0]):
                    cw_ref[tap:tap + 1, chip * cw_cols:(chip + 1) * cw_cols] = cw_land[chip, tap]

        for at, stage in zip(stage_steps[:-1], stages[:-1]):
            pl.when(step == at)(stage)

        def attention(b):
            rows = pl.ds(b * BLK, BLK)
            kv_prev = kv_last if b == 0 else kv_ref.at[pl.ds((b - 1) * BLK, BLK), :]
            return _attn_forward(q_ref.at[rows, :], kv_ref.at[rows, :], kv_prev, ga_ref.at[rows, :], sink_ref, gna_ref,
                                 _band_geometry(step * blocks + b), ya_ref.at[rows, :], oa_ref.at[rows, :],
                                 pr_ref.at[rows, :], sp_ref.at[rows, :])

        xv = x_ref[...]
        h = (xv * _rstd(xv) * g_ref[...]).astype(BF16)
        h_ref[...] = h
        q_ref[...] = _nt(h, wt_ref[ROW_Q:ROW_KV, :])
        kv_ref[...] = _nt(h, wt_ref[ROW_KV:ROW_GA, :])
        ga_ref[...] = _nt(h, wt_ref[ROW_GA:D_IN_PROJ, :])
        attention_blocks = [attention(b) for b in range(blocks)]
        for lo in range(0, D_PC, PC_PIECE):
            pc_ref[:, lo:lo + PC_PIECE] = _nt(h, wt_ref[lo:lo + PC_PIECE, :])
            for stages_of_block in attention_blocks:
                next(stages_of_block, None)
        for stages_of_block in attention_blocks:
            for _ in stages_of_block:
                pass
        kv_last[...] = kv_ref[tile - BLK:tile, :]

        cb, _, _, _, _, _, conv = _conv_core(pc_ref, zbuf, cw_ref)
        yc = cb * conv
        silu, _ = _silu_and_grad(pc_ref[:, 3 * D_CONV:4 * D_CONV])
        oc_ref[...] = (yc * _rstd(yc) * gn_ref[...] * silu).astype(BF16)
        zbuf[0:8, :] = zbuf[tile:tile + 8, :]

        @pl.when(step == stage_steps[-1])
        def _():
            stages[-1]()
            to_hbm.wait()

    def row(width):
        return pl.BlockSpec((tile, width), lambda i: (i, 0))

    any_spec = pl.BlockSpec(memory_space=pl.ANY)
    dma = pltpu.SemaphoreType.DMA
    wt_shape = (N_CHIPS * wt_sh.shape[0], wt_sh.shape[1])
    cw_shape = (cw_sh.shape[0], N_CHIPS * cw_cols)
    return pl.pallas_call(
        body, name="fwd_in", grid=(n_tiles,),
        out_shape=(jax.ShapeDtypeStruct((seq, D_MODEL), BF16), jax.ShapeDtypeStruct((seq, D_PC), F32),
                   jax.ShapeDtypeStruct((seq, D_ATTN), F32), jax.ShapeDtypeStruct((seq, 2 * D_KV), F32),
                   jax.ShapeDtypeStruct((seq, D_ATTN), F32), jax.ShapeDtypeStruct((seq, D_CONV), BF16),
                   pltpu.HBM((seq, D_ATTN), F32), pltpu.HBM((seq, D_ATTN), BF16),
                   pltpu.HBM((seq, N_HEADS * BLK), BF16), pltpu.HBM((seq, 128), F32),
                   pltpu.HBM(wt_shape, BF16), jax.ShapeDtypeStruct(cw_shape, F32),
                   jax.ShapeDtypeStruct((N_CHIPS * wo_sh.shape[0], wo_sh.shape[1]), BF16)),
        in_specs=[row(D_MODEL), _resident((1, D_MODEL)), any_spec, any_spec, any_spec, _resident((1, D_CONV)),
                  pl.BlockSpec(memory_space=pltpu.SMEM), _resident((1, D_ATTN))],
        out_specs=(row(D_MODEL), row(D_PC), row(D_ATTN), row(2 * D_KV), row(D_ATTN), row(D_CONV), row(D_ATTN),
                   row(D_ATTN), row(N_HEADS * BLK), row(128), any_spec, pl.BlockSpec(cw_shape, lambda i: (0, 0)),
                   any_spec),
        scratch_shapes=[pltpu.VMEM((tile + 8, D_CONV), F32), pltpu.VMEM((BLK, 2 * D_KV), F32),
                        pltpu.VMEM(wt_shape, BF16), pltpu.VMEM((2, AG_CAST_ROWS, wt_sh.shape[1]), F32),
                        pltpu.VMEM((N_CHIPS,) + cw_sh.shape, F32),
                        dma((8,)), dma((8,)), dma((2,)), dma((3,)), dma((3,)), dma((1,))] + _ag_scratch(wo_sh.shape),
        compiler_params=_params(62, ("arbitrary",)),
    )(x, norm_in, wt_sh, wo_sh, cw_sh, norm_conv_out, sinks, norm_attn_out)


def _conv_core(pc_ref, zbuf, cw_ref):
    tile = pc_ref.shape[0]
    cb = pc_ref[:, 0:D_CONV]
    cc = pc_ref[:, D_CONV:2 * D_CONV]
    cu = pc_ref[:, 2 * D_CONV:3 * D_CONV]
    z = cc * cu
    zbuf[8:tile + 8, :] = z
    z1 = zbuf[7:tile + 7, :]
    z2 = zbuf[6:tile + 6, :]
    conv = cw_ref[0:1, :] * z2 + cw_ref[1:2, :] * z1 + cw_ref[2:3, :] * z
    return cb, cc, cu, z, z1, z2, conv


def _band_geometry(block_index):
    qi = lax.broadcasted_iota(jnp.int32, (BLK, BLK), 0)
    kp = lax.broadcasted_iota(jnp.int32, (BLK, BLK), 1)
    use_cur = kp <= qi
    dist = jnp.where(use_cur, qi - kp, qi - kp + BLK).astype(F32)
    valid = use_cur | (block_index > 0)
    return use_cur, dist, valid


def _block_diag(cur, prev, group):
    lane = lax.broadcasted_iota(jnp.int32, cur.shape, 1)

    def halves(t):
        other = pltpu.roll(t, 64, 1)
        lo, hi = (t, other) if group == 0 else (other, t)
        return jnp.where(lane < 64, lo, 0.0), jnp.where(lane >= 64, hi, 0.0)

    return jnp.concatenate(halves(cur) + halves(prev), axis=0).astype(BF16)


def _merge(s4, use_cur):
    return (jnp.where(use_cur, s4[:, 0:BLK], s4[:, 2 * BLK:3 * BLK]),
            jnp.where(use_cur, s4[:, BLK:2 * BLK], s4[:, 3 * BLK:4 * BLK]))


def _split(a, b, use_cur):
    return jnp.concatenate([jnp.where(use_cur, a, 0.0), jnp.where(use_cur, b, 0.0),
                            jnp.where(use_cur, 0.0, a), jnp.where(use_cur, 0.0, b)], axis=1)


def _softmax_head(s, head, sink, dist, valid):
    sc = jnp.where(valid, s - SLOPES[head] * dist, -jnp.inf)
    m = jnp.maximum(jnp.max(sc, axis=-1, keepdims=True), sink)
    p = jnp.exp(sc - m)
    es = jnp.exp(sink - m)
    inv = 1.0 / (jnp.sum(p, axis=-1, keepdims=True) + es)
    return p * inv, es * inv


def _attn_operands(q_ref, kvc_ref, kvp_ref):
    groups = range(N_HEADS // HEADS_PER_KV)
    kbd = [_block_diag(kvc_ref[:, 0:D_KV], kvp_ref[:, 0:D_KV], g) for g in groups]
    vbd = [_block_diag(kvc_ref[:, D_KV:2 * D_KV], kvp_ref[:, D_KV:2 * D_KV], g) for g in groups]
    qps = [(q_ref[:, j * 128:(j + 1) * 128] * SCALE).astype(BF16) for j in range(N_HEADS // 2)]
    return qps, kbd, vbd


def _attn_forward(q_ref, kvc_ref, kvp_ref, ga_ref, sink_ref, gn_ref, geometry, ya_ref, oa_ref, pr_ref, sp_ref):
    use_cur, dist, valid = geometry
    _, kbd, vbd = operands = _attn_operands(q_ref, kvc_ref, kvp_ref)
    yield
    scores = []
    for j, qp in enumerate(operands[0]):
        scores += _merge(_nt(qp, kbd[j // 4]), use_cur)
    yield
    sinks = [sink_ref[0, h] for h in range(N_HEADS)]
    scores = [jnp.where(valid, s - SLOPES[h] * dist, -jnp.inf) for h, s in enumerate(scores)]
    maxes = [jnp.maximum(jnp.max(s, axis=-1, keepdims=True), sinks[h]) for h, s in enumerate(scores)]
    yield
    exps = [jnp.exp(s - m) for s, m in zip(scores, maxes)]
    sink_exps = [jnp.exp(sinks[h] - m) for h, m in enumerate(maxes)]
    yield
    invs = [1.0 / (jnp.sum(e, axis=-1, keepdims=True) + se) for e, se in zip(exps, sink_exps)]
    probs = [e * inv for e, inv in zip(exps, invs)]
    pr_ref[...] = jnp.concatenate(probs, axis=1).astype(BF16)
    lane = lax.broadcasted_iota(jnp.int32, (BLK, 128), 1)
    sp_ref[...] = sum(jnp.where(lane == h, se * inv, 0.0) for h, (se, inv) in enumerate(zip(sink_exps, invs)))
    yield
    p4s = [_split(probs[2 * j], probs[2 * j + 1], use_cur).astype(BF16) for j in range(N_HEADS // 2)]
    ya = jnp.concatenate([_nn(p4, vbd[j // 4]) for j, p4 in enumerate(p4s)], axis=1)
    ya_ref[...] = ya
    yield
    silu, _ = _silu_and_grad(ga_ref[...])
    oa_ref[...] = (ya * _rstd(ya) * gn_ref[...] * silu).astype(BF16)


def _out_proj_loss(x, oc, oa, wo, norm_final, target, pc, conv_w, norm_conv_out):
    seq = x.shape[0]
    tile = TOK_TILE
    n_tiles = seq // tile

    def body(x_ref, oc_ref, oa_ref, wo_ref, gf_ref, t_ref, pc_ref, hcc_ref, hcu_ref, cw_ref, gn_ref,
             dx2_ref, doa_ref, gwo_ref, dpc_ref, small_ref, loss_acc, zbuf, dbuf):
        step = pl.program_id(0)

        def small_add(row, value):
            small_ref[row:row + 1, :] += jnp.sum(value, axis=0, keepdims=True)

        @pl.when(step == 0)
        def _():
            gwo_ref[...] = jnp.zeros_like(gwo_ref)
            small_ref[...] = jnp.zeros_like(small_ref)
            loss_acc[...] = jnp.zeros_like(loss_acc)
            dbuf[tile:tile + 8, :] = jnp.zeros((8, D_CONV), F32)

        oc, oa = oc_ref[...], oa_ref[...]
        x2 = x_ref[...] + _nn(oc, wo_ref[0:D_CONV, :]) + _nn(oa, wo_ref[D_CONV:D_MIX, :])
        r = _rstd(x2)
        xhat = x2 * r
        err = xhat * gf_ref[...] - t_ref[...]
        loss_acc[...] += jnp.sum(err * err, axis=0, keepdims=True) * (0.5 / D_MODEL)
        dy = err * (1.0 / D_MODEL)
        small_add(SMALL_NORM_FINAL, dy * xhat)
        dx2 = _rms_bwd(dy * gf_ref[...], xhat, r)
        dx2_ref[...] = dx2
        db = dx2.astype(BF16)
        do = _nt(db, wo_ref[0:D_CONV, :])
        doa_ref[...] = _nt(db, wo_ref[D_CONV:D_MIX, :])
        gwo_ref[0:D_CONV, :] += _tn(oc, db)
        gwo_ref[D_CONV:D_MIX, :] += _tn(oa, db)

        is_first_tile = step == n_tiles - 1
        zbuf[0:8, :] = jnp.where(is_first_tile, 0.0, hcc_ref[...] * hcu_ref[...])
        cb, cc, cu, z, z1, z2, conv = _conv_core(pc_ref, zbuf, cw_ref)
        silu, dsilu = _silu_and_grad(pc_ref[:, 3 * D_CONV:4 * D_CONV])
        yc = cb * conv
        rc = _rstd(yc)
        chat = yc * rc
        dn = do * silu
        dpc_ref[:, 3 * D_CONV:4 * D_CONV] = (do * (chat * gn_ref[...]) * dsilu).astype(BF16)
        small_add(SMALL_NORM_CONV, dn * chat)
        dyc = _rms_bwd(dn * gn_ref[...], chat, rc)
        dpc_ref[:, 0:D_CONV] = (dyc * conv).astype(BF16)
        dconv = dyc * cb
        small_add(SMALL_CONV_W, dconv * z2)
        small_add(SMALL_CONV_W + 1, dconv * z1)
        small_add(SMALL_CONV_W + 2, dconv * z)
        dbuf[0:tile, :] = dconv
        dz = cw_ref[2:3, :] * dconv + cw_ref[1:2, :] * dbuf[1:tile + 1, :] + cw_ref[0:1, :] * dbuf[2:tile + 2, :]
        dpc_ref[:, D_CONV:2 * D_CONV] = (dz * cu).astype(BF16)
        dpc_ref[:, 2 * D_CONV:3 * D_CONV] = (dz * cc).astype(BF16)
        dbuf[tile:tile + 8, :] = dbuf[0:8, :]

        @pl.when(step == n_tiles - 1)
        def _():
            lane = lax.broadcasted_iota(jnp.int32, (1, D_MODEL), 1)
            small_ref[SMALL_MISC:SMALL_MISC + 1, :] = jnp.where(lane == SMALL_LOSS_LANE, jnp.sum(loss_acc[...]), 0.0)

    def rev(i):
        return n_tiles - 1 - i

    def row(width):
        return pl.BlockSpec((tile, width), lambda i: (rev(i), 0))

    def halo(col_block):
        return pl.BlockSpec((8, D_CONV), lambda i: (jnp.maximum(rev(i) * (tile // 8) - 1, 0), col_block))

    def const(shape):
        return pl.BlockSpec(shape, lambda i: (0, 0))

    return pl.pallas_call(
        body, name="out_proj_loss", grid=(n_tiles,),
        out_shape=(jax.ShapeDtypeStruct((seq, D_MODEL), F32), jax.ShapeDtypeStruct((seq, D_ATTN), F32),
                   jax.ShapeDtypeStruct((D_MIX, D_MODEL), F32), jax.ShapeDtypeStruct((seq, D_PC), BF16),
                   jax.ShapeDtypeStruct((SMALL_ROWS, D_MODEL), F32)),
        in_specs=[row(D_MODEL), row(D_CONV), row(D_ATTN), _resident(wo.shape), _resident((1, D_MODEL)), row(D_MODEL),
                  row(D_PC), halo(1), halo(2), _resident(conv_w.shape), _resident((1, D_CONV))],
        out_specs=(row(D_MODEL), row(D_ATTN), const((D_MIX, D_MODEL)), row(D_PC), const((SMALL_ROWS, D_MODEL))),
        scratch_shapes=[pltpu.VMEM((1, D_MODEL), F32), pltpu.VMEM((tile + 8, D_CONV), F32),
                        pltpu.VMEM((tile + 8, D_CONV), F32)],
        compiler_params=_params(62, ("arbitrary",)),
    )(x, oc, oa, wo, norm_final, target, pc, pc, pc, conv_w, norm_conv_out)


def _attn_bwd(q, kv, ga, ya, doa, probs, sink_probs, norm_attn_out, gwo, small):
    seq = q.shape[0]
    per_step = 2
    n_steps = seq // (per_step * BLK)
    stage_steps = (0, 1, n_steps // 4, (5 * n_steps) // 8, n_steps - 1, n_steps - 1)

    def body(q_ref, kvc_ref, kvp_ref, ga_ref, ya_ref, doa_ref, pr_ref, sp_ref, gn_ref, gwo_ref, small_ref,
             dqg_ref, small_out, gwo_sh, gna_ref, gs_ref, carry, dya_buf, *rs_scratch):
        step = pl.program_id(0)
        rs_stages = _rs_wout_stages(gwo_ref, gwo_sh, *rs_scratch)
        for at, stage in zip(stage_steps[:-1], rs_stages[:-1]):
            pl.when(step == at)(stage)

        @pl.when(step == 0)
        def _():
            gna_ref[...] = jnp.zeros_like(gna_ref)
            gs_ref[...] = jnp.zeros_like(gs_ref)
            carry[...] = jnp.zeros_like(carry)

        lane = lax.broadcasted_iota(jnp.int32, (BLK, 128), 1)

        def fold(bd):
            return (jnp.where(lane < 64, bd[0:BLK], 0.0) + jnp.where(lane >= 64, bd[BLK:2 * BLK], 0.0),
                    jnp.where(lane < 64, bd[2 * BLK:3 * BLK], 0.0) + jnp.where(lane >= 64, bd[3 * BLK:4 * BLK], 0.0))

        def one_block(b):
            rows = slice(b * BLK, (b + 1) * BLK)
            kv_prev = kvp_ref if b == 0 else kvc_ref.at[(b - 1) * BLK:b * BLK, :]
            ya = ya_ref[rows, :]
            r = _rstd(ya)
            xhat = ya * r
            silu, dsilu = _silu_and_grad(ga_ref[rows, :])
            do = doa_ref[rows, :]
            dn = do * silu
            dqg_ref[rows, D_ATTN:2 * D_ATTN] = (do * (xhat * gn_ref[...]) * dsilu).astype(BF16)
            gna_ref[...] += jnp.sum(dn * xhat, axis=0, keepdims=True)
            dya_buf[rows, :] = _rms_bwd(dn * gn_ref[...], xhat, r).astype(BF16)

            use_cur = _band_geometry(per_step * (n_steps - 1 - step) + b)[0]
            pairs = range(N_HEADS // 2)
            qps, kbd, vbd = _attn_operands(q_ref.at[rows, :], kvc_ref.at[rows, :], kv_prev)
            probs = [pr_ref[rows, h * BLK:(h + 1) * BLK].astype(F32) for h in range(N_HEADS)]
            dyps = [dya_buf[rows, j * 128:(j + 1) * 128] for j in pairs]
            dps = []
            for j in pairs:
                dps += _merge(_nt(dyps[j], vbd[j // 4]), use_cur)
            deltas = [jnp.sum(p * dp, axis=-1, keepdims=True) for p, dp in zip(probs, dps)]
            dss = [p * (dp - delta) for p, dp, delta in zip(probs, dps, deltas)]
            delta_lanes = sum(jnp.where(lane == h, deltas[h], 0.0) for h in range(N_HEADS))
            gs_ref[...] -= jnp.sum(sp_ref[rows, :] * delta_lanes, axis=0, keepdims=True)
            ds4s = [_split(dss[2 * j], dss[2 * j + 1], use_cur).astype(BF16) for j in pairs]
            p4s = [_split(probs[2 * j], probs[2 * j + 1], use_cur).astype(BF16) for j in pairs]
            dqg_ref[rows, 0:D_ATTN] = jnp.concatenate([_nn(ds4s[j], kbd[j // 4]) * SCALE for j in pairs], axis=1).astype(BF16)
            sums = []
            for group in range(N_HEADS // HEADS_PER_KV):
                acc = [jnp.zeros((BLK, 128), F32) for _ in range(4)]
                for j in range(group * 4, group * 4 + 4):
                    for slot, part in enumerate(fold(_tn(ds4s[j], qps[j])) + fold(_tn(p4s[j], dyps[j]))):
                        acc[slot] = acc[slot] + part
                sums.append([a + pltpu.roll(a, 64, 1) for a in acc])
            dk_cur, dk_prev, dv_cur, dv_prev = (jnp.where(lane < 64, lo, hi) for lo, hi in zip(sums[0], sums[1]))
            dqg_ref[rows, 2 * D_ATTN:2 * D_ATTN + 2 * D_KV] = (jnp.concatenate([dk_cur, dv_cur], axis=1) + carry[...]).astype(BF16)
            carry[...] = jnp.concatenate([dk_prev, dv_prev], axis=1)

        for b in reversed(range(per_step)):
            one_block(b)

        @pl.when(step == n_steps - 1)
        def _():
            small_out[...] = small_ref[...]
            small_out[SMALL_NORM_ATTN:SMALL_NORM_ATTN + 1, :] = gna_ref[...]
            small_out[SMALL_MISC:SMALL_MISC + 1, 0:128] = small_ref[SMALL_MISC:SMALL_MISC + 1, 0:128] + gs_ref[...]

        pl.when(step == stage_steps[-1])(rs_stages[-1])

    def rows(width):
        return pl.BlockSpec((per_step * BLK, width), lambda i: (n_steps - 1 - i, 0))

    kv_prev = pl.BlockSpec((BLK, 2 * D_KV), lambda i: (jnp.maximum(per_step * (n_steps - 1 - i) - 1, 0), 0))
    any_spec = pl.BlockSpec(memory_space=pl.ANY)
    return pl.pallas_call(
        body, name="attn_bwd", grid=(n_steps,),
        out_shape=(jax.ShapeDtypeStruct((seq, D_QG), BF16), pltpu.HBM(small.shape, F32),
                   pltpu.HBM((gwo.shape[0] // N_CHIPS, gwo.shape[1]), F32)),
        in_specs=[rows(D_ATTN), rows(2 * D_KV), kv_prev, rows(D_ATTN), rows(D_ATTN), rows(D_ATTN),
                  rows(N_HEADS * BLK), rows(128), _resident((1, D_ATTN)), any_spec, _resident(small.shape)],
        out_specs=(rows(D_QG), pl.BlockSpec(small.shape, lambda i: (0, 0)), any_spec),
        scratch_shapes=[pltpu.VMEM((1, D_ATTN), F32), pltpu.VMEM((1, 128), F32),
                        pltpu.VMEM((BLK, 2 * D_KV), F32), pltpu.VMEM((per_step * BLK, D_ATTN), BF16)] + _rs_wout_scratch(gwo.shape),
        compiler_params=_params(44, ("arbitrary",)),
    )(q, kv, kv, ga, ya, doa, probs, sink_probs, norm_attn_out, gwo, small)


GBLK = 256
GSUB = 64
PAIR_RING = 4
LAG_PAIR, LAG_HOP1, LAG_HOP2 = 1, 7, 13
AHEAD = 3


def _bwd_in(dpc, dqg, h, wt, x, norm_in, dx2, small):
    seq = x.shape[0]
    n_blk = D_IN_PROJ // GBLK
    per_chip = n_blk // N_CHIPS
    n_slots = (n_blk + 1) // 2
    n_sub = GBLK // GSUB
    chip_rows = D_IN_PROJ // N_CHIPS
    tile = TOK_TILE
    n_tiles = seq // tile
    n_steps = n_blk + max(n_tiles, LAG_HOP2 + 1)
    chunk = min(seq, 512)
    blk_q, blk_kv, blk_ga = ROW_Q // GBLK, ROW_KV // GBLK, ROW_GA // GBLK

    def block_of(i):
        k = i % N_CHIPS
        robin = per_chip * ((k % 2) * 2 + k // 2) + i // N_CHIPS
        if isinstance(i, int):
            return robin if i < per_chip * N_CHIPS else i
        return jnp.where(i < per_chip * N_CHIPS, robin, i)

    def owner_of(i):
        return (i // N_CHIPS) % 2

    def slot_of(i):
        return (i // (2 * N_CHIPS)) * N_CHIPS + i % N_CHIPS

    def body(dpc_ref, dqg_ref, wt_ref, h_ref, x_ref, g_ref, dx2_ref, small_ref, gx_ref, small_sum, gwt_sh,
             dh_acc, gni, keep, pbuf, xbuf, land, land2, small_land,
             pair_send, pair_recv, h1_send, h1_recv, h2_send, h2_recv, sw_send, sw_recv, sm_send, sm_recv, out_sem,
             x_in, dx2_in, in_sem):
        step = pl.program_id(0)
        x_i, y_i, c = lax.axis_index("x"), lax.axis_index("y"), lax.axis_index("c")
        me = 4 * x_i + 2 * y_i + c
        j = 2 * x_i + y_i
        pa = (_xor(x_i, 1 - c), _xor(y_i, c), c)
        pb = (_xor(x_i, c), _xor(y_i, 1 - c), c)
        sib = (x_i, y_i, 1 - c)
        ja = 2 * pa[0] + pa[1]
        jb = 2 * pb[0] + pb[1]
        jd = 3 - j

        def remote(src, dst, send, recv, to):
            return pltpu.make_async_remote_copy(src_ref=src, dst_ref=dst, send_sem=send, recv_sem=recv,
                                                device_id=to, device_id_type=MESH)

        def piece(ref, slot, u, n):
            return ref.at[slot, pl.ds(u * GSUB, n * GSUB), :]

        def chip_rows_at(ref, local, n):
            return ref.at[pl.ds(pl.multiple_of(local, GSUB), n * GSUB), :]

        def pair_copy(i):
            slot = slot_of(i)
            return remote(pbuf.at[i % PAIR_RING], land.at[slot], pair_send.at[slot], pair_recv.at[slot], sib)

        def h1_copy(slot, u, n):
            k = slot * n_sub + u
            return remote(piece(xbuf, slot, u, n), piece(xbuf, slot, u, n), h1_send.at[k], h1_recv.at[k], pa)

        def h2_copy(slot, u, n, local):
            k = slot * n_sub + u
            return remote(piece(xbuf, slot, u, n), chip_rows_at(land2, local, n), h2_send.at[k], h2_recv.at[k], pb)

        def sw_copy(slot, u, n, local):
            k = slot * n_sub + u
            return remote(piece(keep, slot, u, n), chip_rows_at(gwt_sh, local, n), sw_send.at[k], sw_recv.at[k], sib)

        def own_copy(slot, u, n, local):
            return pltpu.make_async_copy(piece(keep, slot, u, n), chip_rows_at(gwt_sh, local, n), out_sem.at[slot * n_sub + u])

        def owned(i):
            return (i >= 0) & (i < n_blk) & (owner_of(i) == c)

        def chip_of(blk, u):
            row = blk * GBLK + u * GSUB
            chip = row // chip_rows
            return chip, row - chip * chip_rows

        def pieces(blk):
            first, local = chip_of(blk, 0)
            whole = first == chip_of(blk, n_sub - 1)[0]
            if isinstance(blk, int):
                return [(True, 0, n_sub, first, local)] if whole else [(True, u, 1) + chip_of(blk, u) for u in range(n_sub)]
            return [(whole, 0, n_sub, first, local)] + [(jnp.logical_not(whole), u, 1) + chip_of(blk, u) for u in range(n_sub)]

        @pl.when(step == 0)
        def _():
            dh_acc[...] = jnp.zeros_like(dh_acc)
            gni[...] = jnp.zeros_like(gni)

        @pl.when(step < n_blk)
        def _():
            from_pc = block_of(step) < blk_q
            block = _tn(jnp.where(from_pc, dpc_ref[...], dqg_ref[...]), h_ref[...])
            for t in range(0, seq, chunk):
                d = jnp.where(from_pc, dpc_ref[t:t + chunk, :], dqg_ref[t:t + chunk, :])
                dh_acc[t:t + chunk, :] += _nn(d, wt_ref[...])

            @pl.when(owner_of(step) == c)
            def _():
                keep[slot_of(step)] = block

            @pl.when(owner_of(step) != c)
            def _():
                @pl.when(step >= 2 * PAIR_RING)
                def _():
                    pair_copy(step - 2 * PAIR_RING).wait_send()
                pbuf[step % PAIR_RING] = block.astype(BF16)
                pair_copy(step).start()

        i1 = step - LAG_PAIR

        @pl.when(owned(i1))
        def _():
            slot = slot_of(i1)
            pair_copy(i1).wait_recv()
            _accumulate(keep.at[slot], land.at[slot])
            for cond, u, n, chip, _ in pieces(block_of(i1)):
                @pl.when(cond & ((chip == ja) | (chip == jd)))
                def _(u=u, n=n):
                    _cast_rows(piece(keep, slot, u, n), piece(xbuf, slot, u, n))
                    h1_copy(slot, u, n).start()

        i2 = step - LAG_HOP1

        @pl.when(owned(i2))
        def _():
            slot = slot_of(i2)
            for cond, u, n, chip, local in pieces(block_of(i2)):
                @pl.when(cond & ((chip == j) | (chip == jb)))
                def _(u=u, n=n, chip=chip, local=local):
                    h1_copy(slot, u, n).wait_recv()
                    _accumulate(piece(keep, slot, u, n), piece(xbuf, slot, u, n))

                    @pl.when(chip == jb)
                    def _():
                        _cast_rows(piece(keep, slot, u, n), piece(xbuf, slot, u, n))
                        h2_copy(slot, u, n, local).start()

                @pl.when(cond & ((chip == ja) | (chip == jd)))
                def _(u=u, n=n):
                    h1_copy(slot, u, n).wait_send()

        i3 = step - LAG_HOP2

        @pl.when(owned(i3))
        def _():
            slot = slot_of(i3)
            for cond, u, n, chip, local in pieces(block_of(i3)):
                @pl.when(cond & (chip == j))
                def _(u=u, n=n, local=local):
                    h2_copy(slot, u, n, local).wait_recv()
                    _accumulate(piece(keep, slot, u, n), chip_rows_at(land2, local, n))
                    own_copy(slot, u, n, local).start()
                    sw_copy(slot, u, n, local).start()

                @pl.when(cond & (chip == jb))
                def _(u=u, n=n, local=local):
                    h2_copy(slot, u, n, local).wait_send()

        e = step - n_blk

        def fetches(t):
            rows = pl.ds(pl.multiple_of(t * tile, tile), tile)
            return [pltpu.make_async_copy(x_ref.at[rows, :], x_in.at[t % AHEAD], in_sem.at[t % AHEAD]),
                    pltpu.make_async_copy(dx2_ref.at[rows, :], dx2_in.at[t % AHEAD], in_sem.at[AHEAD + t % AHEAD])]

        @pl.when((e + AHEAD - 1 >= 0) & (e + AHEAD - 1 < n_tiles))
        def _():
            for cp in fetches(e + AHEAD - 1):
                cp.start()

        @pl.when((e >= 0) & (e < n_tiles))
        def _():
            for cp in fetches(e):
                cp.wait()
            dh = dh_acc[pl.ds(pl.multiple_of(e * tile, tile), tile), :]
            xv = x_in[e % AHEAD]
            r = _rstd(xv)
            xhat = xv * r
            gni[...] += jnp.sum(dh * xhat, axis=0, keepdims=True)
            gx_ref[...] = _rms_bwd(dh * g_ref[...], xhat, r) + dx2_in[e % AHEAD]

        others = [(dx, dy, dc) for dx in (0, 1) for dy in (0, 1) for dc in (0, 1)][1:]
        sends = [remote(small_land.at[me], small_land.at[me], sm_send.at[k], sm_recv.at[k],
                        (_xor(x_i, dx), _xor(y_i, dy), _xor(c, dc))) for k, (dx, dy, dc) in enumerate(others)]

        @pl.when(step == min(n_blk + n_tiles, n_steps - 1))
        def _():
            small_land[me] = small_ref[...]
            small_land[me, SMALL_NORM_IN:SMALL_NORM_IN + 1, :] = gni[...]
            for cp in sends:
                cp.start()

        @pl.when(step == n_steps - 1)
        def _():
            for i in range(n_blk):
                if i + 2 * PAIR_RING >= n_blk:
                    @pl.when(owner_of(i) != c)
                    def _(i=i):
                        pair_copy(i).wait_send()
                for _, u, n, chip, local in pieces(block_of(i)):
                    @pl.when((j == chip) & (c == owner_of(i)))
                    def _(i=i, u=u, n=n, local=local):
                        sw_copy(slot_of(i), u, n, local).wait_send()
                        own_copy(slot_of(i), u, n, local).wait()

                    @pl.when((j == chip) & (c != owner_of(i)))
                    def _(i=i, u=u, n=n, local=local):
                        sw_copy(slot_of(i), u, n, local).wait_recv()
            for cp in sends:
                cp.wait_recv()
            total = small_land[0]
            for dev in range(1, 8):
                total = total + small_land[dev]
            small_sum[...] = total
            for cp in sends:
                cp.wait_send()

    def blk_at(i):
        return block_of(jnp.clip(i, 0, n_blk - 1))

    last_pc_step = max(i for i in range(n_blk) if block_of(i) < blk_q)

    def next_block(i, in_pc):
        i = jnp.clip(i, 0, n_blk - 1)
        step = jnp.full_like(i, last_pc_step if in_pc else n_blk - 1)
        for ahead in reversed(range(N_CHIPS)):
            cand = jnp.minimum(i + ahead, n_blk - 1)
            step = jnp.where((block_of(cand) < blk_q) == in_pc, cand, step)
        return block_of(step)

    def dqg_block(i):
        b = next_block(i, False)
        q_blk = jnp.clip(b - blk_q, 0, blk_kv - blk_q - 1)
        ga_blk = (D_ATTN // GBLK) + jnp.clip(b - blk_ga, 0, n_blk - blk_ga - 1)
        return jnp.where(b < blk_kv, q_blk, jnp.where(b == blk_kv, 2 * D_ATTN // GBLK, ga_blk))

    def tok(i):
        return (jnp.clip(i - n_blk, 0, n_tiles - 1), 0)


    n_piece = n_slots * n_sub
    dma = pltpu.SemaphoreType.DMA
    return pl.pallas_call(
        body, name="bwd_in", grid=(n_steps,),
        out_shape=(jax.ShapeDtypeStruct((seq, D_MODEL), F32), jax.ShapeDtypeStruct(small.shape, F32),
                   jax.ShapeDtypeStruct((chip_rows, D_MODEL), F32)),
        in_specs=[pl.BlockSpec((seq, GBLK), lambda i: (0, next_block(i, True))),
                  pl.BlockSpec((seq, GBLK), lambda i: (0, dqg_block(i))),
                  pl.BlockSpec((GBLK, D_MODEL), lambda i: (blk_at(i), 0)),
                  _resident(h.shape),
                  pl.BlockSpec(memory_space=pl.ANY), _resident((1, D_MODEL)), pl.BlockSpec(memory_space=pl.ANY),
                  _resident(small.shape)],
        out_specs=(pl.BlockSpec((tile, D_MODEL), tok), pl.BlockSpec(small.shape, lambda i: (0, 0)),
                   pl.BlockSpec(memory_space=pl.ANY)),
        scratch_shapes=[pltpu.VMEM((seq, D_MODEL), F32), pltpu.VMEM((1, D_MODEL), F32),
                        pltpu.VMEM((n_slots, GBLK, D_MODEL), F32), pltpu.VMEM((PAIR_RING, GBLK, D_MODEL), BF16),
                        pltpu.VMEM((n_slots, GBLK, D_MODEL), BF16), pltpu.VMEM((n_slots, GBLK, D_MODEL), BF16),
                        pltpu.VMEM((chip_rows, D_MODEL), BF16), pltpu.VMEM((8,) + small.shape, F32),
                        dma((n_slots,)), dma((n_slots,)), dma((n_piece,)), dma((n_piece,)), dma((n_piece,)),
                        dma((n_piece,)), dma((n_piece,)), dma((n_piece,)), dma((7,)), dma((7,)), dma((n_piece,)),
                        pltpu.VMEM((AHEAD, tile, D_MODEL), F32), pltpu.VMEM((AHEAD, tile, D_MODEL), F32),
                        dma((2 * AHEAD,))],
        compiler_params=_params(62, ("arbitrary",)),
    )(dpc, dqg, wt, h, x, norm_in, dx2, small)


def _accumulate(dst_ref, src_ref, rows=16):
    def step(i, carry):
        sl = pl.ds(pl.multiple_of(i * rows, rows), rows)
        dst_ref[sl, :] = dst_ref[sl, :] + src_ref[sl, :].astype(F32)
        return carry
    lax.fori_loop(0, dst_ref.shape[0] // rows, step, 0)


def _rs_wout_scratch(gwo_shape):
    o_half, width = gwo_shape[0] // N_CHIPS // 2, gwo_shape[1]
    return [pltpu.VMEM((4, o_half, width), F32), pltpu.VMEM((4, o_half, width), BF16),
            pltpu.VMEM((4, o_half, width), BF16), pltpu.VMEM((2, o_half, width), BF16),
            pltpu.VMEM((o_half, width), BF16),
            pltpu.SemaphoreType.DMA((8,)), pltpu.SemaphoreType.DMA((8,)), pltpu.SemaphoreType.DMA((4,))]


def _rs_wout_stages(gwo_ref, gwo_sh, acc_o, sb_o, r1o, r2o, r3o, send_sems, recv_sems, local_sems):
    o_rows = gwo_ref.shape[0] // N_CHIPS
    o_half = o_rows // 2
    x, y, c = lax.axis_index("x"), lax.axis_index("y"), lax.axis_index("c")
    j = 2 * x + y
    pa = (_xor(x, 1 - c), _xor(y, c), c)
    pb = (_xor(x, c), _xor(y, 1 - c), c)
    sib = (x, y, 1 - c)
    ja = 2 * pa[0] + pa[1]
    jb = 2 * pb[0] + pb[1]
    jd = 3 - j
    order = (ja, jd, jb, j)
    sib_order = (jb, jd, ja, j)

    def rcopy(k, src, dst, to):
        return pltpu.make_async_remote_copy(src_ref=src, dst_ref=dst, send_sem=send_sems.at[k],
                                            recv_sem=recv_sems.at[k], device_id=to, device_id_type=MESH)

    def o_rows_of(chip, half):
        return gwo_ref.at[pl.ds(pl.multiple_of(chip * o_rows + half * o_half, 8), o_half), :]

    def loads(chips, half):
        return [pltpu.make_async_copy(o_rows_of(chip, half), acc_o.at[s], local_sems.at[s]) for s, chip in enumerate(chips)]

    def pair_send(s):
        return rcopy(s, sb_o.at[s], r1o.at[s], sib)

    def out_half(half):
        return gwo_sh.at[pl.ds(pl.multiple_of(half * o_half, 8), o_half), :]

    hop1 = [rcopy(4 + s, sb_o.at[s], r2o.at[s], pa) for s in range(2)]
    hop2 = rcopy(6, sb_o.at[2], r3o, pb)
    swap = rcopy(7, acc_o.at[3], out_half(c), sib)
    mine = pltpu.make_async_copy(acc_o.at[3], out_half(c), local_sems.at[0])

    def resend(s, copy):
        pair_send(s).wait_send()
        _cast_rows(acc_o.at[s], sb_o.at[s])
        copy.start()

    def stage_load():
        for cp in loads(sib_order, 1 - c):
            cp.start()

    def stage_pair():
        for s, (cp, mine_in) in enumerate(zip(loads(sib_order, 1 - c), loads(order, c))):
            cp.wait()
            _cast_rows(acc_o.at[s], sb_o.at[s])
            pair_send(s).start()
            mine_in.start()

    def stage_hop1():
        for s, cp in enumerate(loads(order, c)):
            cp.wait()
            pair_send(s).wait_recv()
            _accumulate(acc_o.at[s], r1o.at[s])
            if s < 2:
                resend(s, hop1[s])

    def stage_hop2():
        hop1[1].wait_recv()
        _accumulate(acc_o.at[2], r2o.at[1])
        resend(2, hop2)
        hop1[0].wait_recv()
        _accumulate(acc_o.at[3], r2o.at[0])

    def stage_final():
        hop2.wait_recv()
        _accumulate(acc_o.at[3], r3o)
        mine.start()
        swap.start()

    def stage_drain():
        rcopy(7, acc_o.at[3], out_half(1 - c), sib).wait_recv()
        for cp in [pair_send(3)] + hop1 + [hop2, swap]:
            cp.wait_send()
        mine.wait()

    return stage_load, stage_pair, stage_hop1, stage_hop2, stage_final, stage_drain


def _adamw_big(groups, passed):
    arrays = [a for group in groups for a in group[:4]]
    steps = [group[0].shape[0] // group[4] for group in groups]
    first = [sum(steps[:k]) for k in range(len(steps) + 1)]
    n = len(arrays)

    def body(*refs):
        ins, passed_in, outs, passed_out = refs[:n], refs[n], refs[n + 1:2 * n + 1], refs[2 * n + 1]
        i = pl.program_id(0)
        for k in range(len(groups)):
            @pl.when((i >= first[k]) & (i < first[k + 1]))
            def _(k=k):
                w_ref, g_ref, m_ref, v_ref = ins[4 * k:4 * k + 4]
                _adamw_update(w_ref, g_ref[...], m_ref, v_ref, *outs[4 * k:4 * k + 4])

        @pl.when(i < steps[0])
        def _():
            passed_out[...] = passed_in[...]

    def spec(k, rows, width):
        return pl.BlockSpec((rows, width), lambda i: (jnp.clip(i - first[k], 0, steps[k] - 1), 0))

    specs = [spec(k, group[4], group[0].shape[1]) for k, group in enumerate(groups) for _ in range(4)]
    specs.append(spec(0, passed.shape[0] // steps[0], passed.shape[1]))
    out = pl.pallas_call(
        body, name="adamw_big", grid=(first[-1],),
        out_shape=tuple(jax.ShapeDtypeStruct(a.shape, a.dtype) for a in arrays + [passed]),
        in_specs=specs, out_specs=tuple(specs),
        compiler_params=_params(40, ("arbitrary",)),
    )(*arrays, passed)
    return [tuple(out[4 * k:4 * k + 4]) for k in range(len(groups))], out[-1]


def _adamw_update(w_ref, gv, m_ref, v_ref, go_ref, d_ref, nm_ref, nv_ref, at=...):
    go_ref[at] = gv
    nm = ADAM_B1 * m_ref[at] + (1.0 - ADAM_B1) * gv
    nv = ADAM_B2 * v_ref[at] + (1.0 - ADAM_B2) * (gv * gv)
    m_hat = nm / (1.0 - ADAM_B1 ** ADAM_STEP)
    v_hat = nv / (1.0 - ADAM_B2 ** ADAM_STEP)
    d_ref[at] = -ADAM_LR * (m_hat / (jnp.sqrt(v_hat) + ADAM_EPS) + ADAM_WD * w_ref[at])
    nm_ref[at] = nm
    nv_ref[at] = nv


def _adamw_small(chip, small_sum, weights, grads_of, ms, vs):
    n = len(weights)

    def body(chip_ref, small_ref, *refs):
        ins, outs, loss_ref = refs[:3 * n], refs[3 * n:-1], refs[-1]
        for k in range(n):
            w_ref, m_ref, v_ref = ins[3 * k:3 * k + 3]
            for at, gv in grads_of[k](small_ref, chip_ref):
                _adamw_update(w_ref, gv, m_ref, v_ref, *outs[4 * k:4 * k + 4], at=at)
        loss_ref[...] = small_ref[SMALL_MISC:SMALL_MISC + 1, SMALL_LOSS_LANE:SMALL_LOSS_LANE + 1]

    flat = [a for group in zip(weights, ms, vs) for a in group]
    vmem = pl.BlockSpec(memory_space=pltpu.VMEM)
    out = pl.pallas_call(
        body, name="adamw_small",
        out_shape=tuple(jax.ShapeDtypeStruct(w.shape, F32) for w in weights for _ in range(4))
        + (jax.ShapeDtypeStruct((1, 1), F32),),
        in_specs=[pl.BlockSpec(memory_space=pltpu.SMEM)] + [vmem] * (1 + 3 * n), out_specs=(vmem,) * (4 * n + 1),
    )(chip, small_sum, *flat)
    return [tuple(out[4 * k:4 * k + 4]) for k in range(n)], out[-1][0, 0]


def kernel(x, norm_in, w_in, conv_w, attn_sinks, norm_conv_out, norm_attn_out, w_out, norm_final, loss_target, m_norm_in, m_w_in, m_conv_w, m_attn_sinks, m_norm_conv_out, m_norm_attn_out, m_w_out, m_norm_final, v_norm_in, v_w_in, v_conv_w, v_attn_sinks, v_norm_conv_out, v_norm_attn_out, v_w_out, v_norm_final):
    chip = 2 * lax.axis_index("x") + lax.axis_index("y")
    xs, target = x[0], loss_target[0]
    norm_final2 = norm_final.reshape(1, D_MODEL)
    w_in_t, m_w_in_t, v_w_in_t = w_in[0].T, m_w_in[0].T, v_w_in[0].T

    def from_hbm(*arrays):
        return tuple(pltpu.with_memory_space_constraint(a, pltpu.HBM) for a in arrays)

    h, pc, q, kv, ga, oc, ya, oa, probs, sink_probs, wt, cw, wo = _fwd_in(
        xs, norm_in, w_in_t, w_out[0], conv_w.transpose(1, 0, 2), norm_conv_out, attn_sinks, norm_attn_out)
    dx2, doa, gwo, dpc, small = _out_proj_loss(xs, oc, oa, wo, norm_final2, target, pc, cw, norm_conv_out)
    q, kv, ga, ya, doa, probs, gwo, small = from_hbm(q, kv, ga, ya, doa, probs, gwo, small)
    dqg, small, gwo_sh = _attn_bwd(q, kv, ga, ya, doa, probs, sink_probs, norm_attn_out, gwo, small)
    grad_x, small_sum, gwt_sh = _bwd_in(dpc, dqg, h, wt, xs, norm_in, dx2, small)

    (up_w_in, up_w_out), grad_x = _adamw_big(
        [from_hbm(w_in_t, gwt_sh, m_w_in_t, v_w_in_t) + (200,), from_hbm(w_out[0], gwo_sh, m_w_out[0], v_w_out[0]) + (128,)],
        *from_hbm(grad_x))
    up_w_in = tuple(o.T[None] for o in up_w_in)
    up_w_out = tuple(o[None] for o in up_w_out)

    def row_of(r):
        return lambda small_ref, chip_ref: [(..., small_ref[r:r + 1, :])]

    def sink_lanes(small_ref, chip_ref):
        return [(..., small_ref[SMALL_MISC:SMALL_MISC + 1, 0:N_HEADS])]

    def conv_taps(small_ref, chip_ref):
        width = D_CONV // N_CHIPS
        cols = pl.ds(pl.multiple_of(chip_ref[0] * width, width), width)
        return [(k, small_ref[pl.ds(SMALL_CONV_W + k, 1), cols]) for k in range(conv_w.shape[1])]

    def small_view(a):
        return a.transpose(1, 0, 2) if a.ndim == 3 else a.reshape(-1, a.shape[-1])

    small_w = (norm_in, conv_w, attn_sinks, norm_conv_out, norm_attn_out, norm_final)
    small_m = (m_norm_in, m_conv_w, m_attn_sinks, m_norm_conv_out, m_norm_attn_out, m_norm_final)
    small_v = (v_norm_in, v_conv_w, v_attn_sinks, v_norm_conv_out, v_norm_attn_out, v_norm_final)
    small_g = (row_of(SMALL_NORM_IN), conv_taps, sink_lanes, row_of(SMALL_NORM_CONV), row_of(SMALL_NORM_ATTN),
               row_of(SMALL_NORM_FINAL))
    w_views, m_views, v_views = (tuple(small_view(a) for a in group) for group in (small_w, small_m, small_v))
    up_small, loss = _adamw_small(chip.astype(jnp.int32).reshape(1), small_sum, w_views, small_g, m_views, v_views)
    up_small = [tuple(o.transpose(1, 0, 2) if w.ndim == 3 else o.reshape(w.shape) for o in up)
                for up, w in zip(up_small, small_w)]
    up_norm_in, up_conv_w, up_sinks, up_norm_conv, up_norm_attn, up_norm_final = up_small
    updates = (up_norm_in, up_w_in, up_conv_w, up_sinks, up_norm_conv, up_norm_attn, up_w_out, up_norm_final)
    grads_out, deltas, new_m, new_v = zip(*updates)
    return (loss, grad_x[None], *grads_out, *deltas, *new_m, *new_v)
```

```python
import jax
import jax.numpy as jnp
from jax import lax
from jax.experimental import pallas as pl
from jax.experimental.pallas import tpu as pltpu

F32 = jnp.float32
BF16 = jnp.bfloat16
MESH = pl.DeviceIdType.MESH

D_MODEL = 1024
D_CONV = 1024
D_ATTN = 1024
D_KV = 128
D_QG = 2 * D_ATTN + 2 * D_KV
D_MIX = D_CONV + D_ATTN
D_PC = 4 * D_CONV
D_IN_PROJ = D_PC + 2 * D_ATTN + 2 * D_KV
ROW_Q = D_PC
ROW_KV = ROW_Q + D_ATTN
ROW_GA = ROW_KV + 2 * D_KV
N_HEADS = 16
HEAD_DIM = 64
HEADS_PER_KV = 8
BLK = 128
N_CHIPS = 4
RMS_EPS = 1e-5
SCALE = HEAD_DIM ** -0.5
SLOPES = tuple(2.0 ** (-8.0 * (h + 1) / N_HEADS) for h in range(N_HEADS))

ADAM_LR, ADAM_B1, ADAM_B2, ADAM_EPS, ADAM_WD, ADAM_STEP = 0.001, 0.9, 0.999, 1e-08, 0.01, 10

SMALL_ROWS = 8
SMALL_NORM_IN, SMALL_NORM_CONV, SMALL_NORM_ATTN, SMALL_NORM_FINAL, SMALL_CONV_W, SMALL_MISC = 0, 1, 2, 3, 4, 7
SMALL_LOSS_LANE = N_HEADS

TOK_TILE = 256
PC_PIECE = 512
MIB = 1 << 20


def _params(vmem_mib, semantics=None):
    return pltpu.CompilerParams(dimension_semantics=semantics, vmem_limit_bytes=vmem_mib * MIB)


def _nn(a, b):
    return jnp.dot(a, b, preferred_element_type=F32)


def _nt(a, b):
    return lax.dot_general(a, b, (((1,), (1,)), ((), ())), preferred_element_type=F32)


def _tn(a, b):
    return lax.dot_general(a, b, (((0,), (0,)), ((), ())), preferred_element_type=F32)


def _rstd(v):
    return lax.rsqrt(jnp.mean(v * v, axis=-1, keepdims=True) + RMS_EPS)


def _rms_bwd(g, xhat, rstd):
    return rstd * (g - xhat * jnp.mean(g * xhat, axis=-1, keepdims=True))


def _silu_and_grad(g):
    s = jax.nn.sigmoid(g)
    return g * s, s * (1.0 + g * (1.0 - s))


def _resident(shape):
    return pl.BlockSpec(shape, lambda *_: (0,) * len(shape), pipeline_mode=pl.Buffered(1))


def _xor(a, b):
    return a + b - 2 * a * b


def _cast_rows(src_ref, dst_ref, rows=32):
    def step(i, carry):
        sl = pl.ds(pl.multiple_of(i * rows, rows), rows)
        dst_ref[sl, :] = src_ref[sl, :].astype(dst_ref.dtype)
        return carry
    lax.fori_loop(0, src_ref.shape[0] // rows, step, 0)


def _ag_scratch(shard_shape):
    rows, width = shard_shape
    return [pltpu.VMEM((rows, width), F32), pltpu.VMEM((rows, width), BF16), pltpu.VMEM((3, rows // 2, width), BF16),
            pltpu.SemaphoreType.DMA((6,)), pltpu.SemaphoreType.DMA((6,)), pltpu.SemaphoreType.DMA((4,))]


def _ag_stages(sh_ref, out, f32_buf, own, land, send_sems, recv_sems, local_sems):
    rows = sh_ref.shape[0]
    half = rows // 2
    x, y, c = lax.axis_index("x"), lax.axis_index("y"), lax.axis_index("c")
    j = 2 * x + y
    p1 = (_xor(x, c), _xor(y, 1 - c), c)
    p2 = (_xor(x, 1 - c), _xor(y, c), c)
    sib = (x, y, 1 - c)
    j1 = 2 * p1[0] + p1[1]
    j2 = 2 * p2[0] + p2[1]
    j3 = 3 - j

    def rows_of(chip, hf):
        return out.at[pl.ds(pl.multiple_of(chip * rows + hf * half, 16), half), :]

    def rcopy(k, src, dst, to):
        return pltpu.make_async_remote_copy(src_ref=src, dst_ref=dst, send_sem=send_sems.at[k],
                                            recv_sem=recv_sems.at[k], device_id=to, device_id_type=MESH)

    my_half = own.at[pl.ds(pl.multiple_of(c * half, 16), half), :]
    hop1 = rcopy(0, my_half, land.at[0], p1)
    hop2_own = rcopy(1, my_half, land.at[1], p2)
    hop2_fwd = rcopy(2, land.at[0], land.at[2], p2)
    swaps = [rcopy(3 + s, land.at[s], rows_of(chip, c), sib) for s, chip in enumerate((j1, j2, j3))]
    keeps = [pltpu.make_async_copy(land.at[s], rows_of(chip, c), local_sems.at[1 + s]) for s, chip in enumerate((j1, j2, j3))]
    load = pltpu.make_async_copy(sh_ref, f32_buf, local_sems.at[0])
    own_out = pltpu.make_async_copy(own, out.at[pl.ds(pl.multiple_of(j * rows, 16), rows), :], local_sems.at[0])

    def stage_load():
        load.start()

    def stage_send():
        load.wait()
        _cast_rows(f32_buf, own)
        own_out.start()
        hop1.start()

    def stage_forward():
        hop1.wait_recv()
        hop2_own.start()
        hop2_fwd.start()
        swaps[0].start()
        keeps[0].start()

    def stage_publish():
        hop2_own.wait_recv()
        swaps[1].start()
        keeps[1].start()
        hop2_fwd.wait_recv()
        swaps[2].start()
        keeps[2].start()

    def stage_drain():
        for s, chip in enumerate((j2, j1, j3)):
            rcopy(3 + s, my_half, rows_of(chip, 1 - c), sib).wait_recv()
        for cp in [hop1, hop2_own, hop2_fwd] + swaps:
            cp.wait_send()
        for cp in [own_out] + keeps:
            cp.wait()

    return stage_load, stage_send, stage_forward, stage_publish, stage_drain


AG_CAST_ROWS = 400


def _gather_resident(sh_ref, out, f32_buf, send_sems, recv_sems, local_sems, after_first_hop):
    rows = sh_ref.shape[0]
    half = rows // 2
    x, y, c = lax.axis_index("x"), lax.axis_index("y"), lax.axis_index("c")
    j = 2 * x + y
    p1 = (_xor(x, c), _xor(y, 1 - c), c)
    p2 = (_xor(x, 1 - c), _xor(y, c), c)
    sib = (x, y, 1 - c)
    j1 = 2 * p1[0] + p1[1]
    j2 = 2 * p2[0] + p2[1]
    j3 = 3 - j

    def rows_of(chip, hf):
        return out.at[pl.ds(pl.multiple_of(chip * rows + hf * half, 16), half), :]

    def send(k, chip, to):
        return pltpu.make_async_remote_copy(src_ref=rows_of(chip, c), dst_ref=rows_of(chip, c), send_sem=send_sems.at[k],
                                            recv_sem=recv_sems.at[k], device_id=to, device_id_type=MESH)

    per_half = half // AG_CAST_ROWS

    chunks = [c * per_half + k for k in range(per_half)] + [(1 - c) * per_half + k for k in range(per_half)]

    def load(n):
        lo = pl.multiple_of(chunks[n] * AG_CAST_ROWS, 16)
        return pltpu.make_async_copy(sh_ref.at[pl.ds(lo, AG_CAST_ROWS), :], f32_buf.at[n % 2], local_sems.at[n % 2])

    def send_piece(k, chip, q, to):
        at = out.at[pl.ds(pl.multiple_of(chip * rows + c * half + q * (half // 2), 16), half // 2), :]
        return pltpu.make_async_remote_copy(src_ref=at, dst_ref=at, send_sem=send_sems.at[k], recv_sem=recv_sems.at[k],
                                            device_id=to, device_id_type=MESH)

    forwards = [send_piece(2, j1, 0, p2), send_piece(6, j1, 1, p2)]
    sends = [send(0, j, p1), send(1, j, p2)] + forwards + [send(3, j1, sib)]
    load(0).start()
    load(1).start()
    for n in range(len(chunks)):
        load(n).wait()
        lo = pl.multiple_of(chunks[n] * AG_CAST_ROWS, 16)
        _cast_rows(f32_buf.at[n % 2], out.at[pl.ds(pl.multiple_of(j * rows + lo, 16), AG_CAST_ROWS), :], rows=16)
        if n + 2 < len(chunks):
            load(n + 2).start()
        if n == per_half - 1:
            sends[0].start()
            sends[1].start()
    sends[0].wait_recv()
    for cp in sends[2:]:
        cp.start()
    after_first_hop()
    sends[1].wait_recv()
    sends.append(send(4, j2, sib))
    sends[-1].start()
    for q, (k, arrived) in enumerate(zip((5, 7), forwards)):
        arrived.wait_recv()
        sends.append(send_piece(k, j3, q, sib))
        sends[-1].start()
    for k in (3, 4):
        send(k, j, sib).wait_recv()
    for k in (5, 7):
        send_piece(k, j, 0, sib).wait_recv()
    for cp in sends:
        cp.wait_send()


def _fwd_in(x, norm_in, wt_sh, wo_sh, cw_sh, norm_conv_out, sinks, norm_attn_out):
    seq = x.shape[0]
    tile = TOK_TILE
    n_tiles = seq // tile
    stage_steps = (0, n_tiles // 4, (5 * n_tiles) // 8, n_tiles - 1)
    blocks = tile // BLK
    cw_cols = cw_sh.shape[-1]

    def body(x_ref, g_ref, wtsh_ref, wo_ref, cwsh_ref, gn_ref, sink_ref, gna_ref,
             h_ref, pc_ref, q_ref, kv_ref, ga_ref, oc_ref, ya_ref, oa_ref, pr_ref, sp_ref, wt_out, cw_ref, wo_out,
             zbuf, kv_last, wt_ref, f32_buf, cw_land, wt_send, wt_recv, wt_local, cw_send, cw_recv, cw_local, *ag_scratch):
        step = pl.program_id(0)
        to_hbm = pltpu.make_async_copy(wt_ref, wt_out, wt_local.at[0])
        load_wo, *stages = _ag_stages(wo_ref, wo_out, *ag_scratch)

        @pl.when(step == 0)
        def _():
            load_wo()
            zbuf[0:8, :] = jnp.zeros((8, D_CONV), F32)
            kv_last[...] = jnp.zeros_like(kv_last)
            x_i, y_i, c = lax.axis_index("x"), lax.axis_index("y"), lax.axis_index("c")
            j = 2 * x_i + y_i
            p1 = (_xor(x_i, c), _xor(y_i, 1 - c), c)
            p2 = (_xor(x_i, 1 - c), _xor(y_i, c), c)
            j1 = 2 * p1[0] + p1[1]

            def cw_copy(k, src, chip, to):
                return pltpu.make_async_remote_copy(src_ref=src, dst_ref=cw_land.at[chip], send_sem=cw_send.at[k],
                                                    recv_sem=cw_recv.at[k], device_id=to, device_id_type=MESH)

            mine = pltpu.make_async_copy(cwsh_ref, cw_land.at[j], cw_local.at[0])
            mine.start()
            first = cw_copy(0, cwsh_ref, j, p1)
            first.start()
            second = [cw_copy(1, cwsh_ref, j, p2), cw_copy(2, cw_land.at[j1], j1, p2)]

            def cw_second_hop():
                first.wait_recv()
                for cp in second:
                    cp.start()

            _gather_resident(wtsh_ref, wt_ref, f32_buf, wt_send, wt_recv, wt_local, cw_second_hop)
            to_hbm.start()
            for cp in second:
                cp.wait_recv()
            for cp in [first] + second:
                cp.wait_send()
            mine.wait()
            for chip in range(N_CHIPS):
                for tap in range(cw_sh.shape[0]):
                    cw_ref[tap:tap + 1, chip * cw_cols:(chip + 1) * cw_cols] = cw_land[chip, tap]

        for at, stage in zip(stage_steps[:-1], stages[:-1]):
            pl.when(step == at)(stage)

        def attention(b):
            rows = pl.ds(b * BLK, BLK)
            kv_prev = kv_last if b == 0 else kv_ref.at[pl.ds((b - 1) * BLK, BLK), :]
            return _attn_forward(q_ref.at[rows, :], kv_ref.at[rows, :], kv_prev, ga_ref.at[rows, :], sink_ref, gna_ref,
                                 _band_geometry(step * blocks + b), ya_ref.at[rows, :], oa_ref.at[rows, :],
                                 pr_ref.at[rows, :], sp_ref.at[rows, :])

        xv = x_ref[...]
        h = (xv * _rstd(xv) * g_ref[...]).astype(BF16)
        h_ref[...] = h
        q_ref[...] = _nt(h, wt_ref[ROW_Q:ROW_KV, :])
        kv_ref[...] = _nt(h, wt_ref[ROW_KV:ROW_GA, :])
        ga_ref[...] = _nt(h, wt_ref[ROW_GA:D_IN_PROJ, :])
        attention_blocks = [attention(b) for b in range(blocks)]
        for lo in range(0, D_PC, PC_PIECE):
            pc_ref[:, lo:lo + PC_PIECE] = _nt(h, wt_ref[lo:lo + PC_PIECE, :])
            for stages_of_block in attention_blocks:
                next(stages_of_block, None)
        for stages_of_block in attention_blocks:
            for _ in stages_of_block:
                pass
        kv_last[...] = kv_ref[tile - BLK:tile, :]

        cb, _, _, _, _, _, conv = _conv_core(pc_ref, zbuf, cw_ref)
        yc = cb * conv
        silu, _ = _silu_and_grad(pc_ref[:, 3 * D_CONV:4 * D_CONV])
        oc_ref[...] = (yc * _rstd(yc) * gn_ref[...] * silu).astype(BF16)
        zbuf[0:8, :] = zbuf[tile:tile + 8, :]

        @pl.when(step == stage_steps[-1])
        def _():
            stages[-1]()
            to_hbm.wait()

    def row(width):
        return pl.BlockSpec((tile, width), lambda i: (i, 0))

    any_spec = pl.BlockSpec(memory_space=pl.ANY)
    dma = pltpu.SemaphoreType.DMA
    wt_shape = (N_CHIPS * wt_sh.shape[0], wt_sh.shape[1])
    cw_shape = (cw_sh.shape[0], N_CHIPS * cw_cols)
    return pl.pallas_call(
        body, name="fwd_in", grid=(n_tiles,),
        out_shape=(jax.ShapeDtypeStruct((seq, D_MODEL), BF16), jax.ShapeDtypeStruct((seq, D_PC), F32),
                   jax.ShapeDtypeStruct((seq, D_ATTN), F32), jax.ShapeDtypeStruct((seq, 2 * D_KV), F32),
                   jax.ShapeDtypeStruct((seq, D_ATTN), F32), jax.ShapeDtypeStruct((seq, D_CONV), BF16),
                   pltpu.HBM((seq, D_ATTN), F32), pltpu.HBM((seq, D_ATTN), BF16),
                   pltpu.HBM((seq, N_HEADS * BLK), BF16), pltpu.HBM((seq, 128), F32),
                   pltpu.HBM(wt_shape, BF16), jax.ShapeDtypeStruct(cw_shape, F32),
                   jax.ShapeDtypeStruct((N_CHIPS * wo_sh.shape[0], wo_sh.shape[1]), BF16)),
        in_specs=[row(D_MODEL), _resident((1, D_MODEL)), any_spec, any_spec, any_spec, _resident((1, D_CONV)),
                  pl.BlockSpec(memory_space=pltpu.SMEM), _resident((1, D_ATTN))],
        out_specs=(row(D_MODEL), row(D_PC), row(D_ATTN), row(2 * D_KV), row(D_ATTN), row(D_CONV), row(D_ATTN),
                   row(D_ATTN), row(N_HEADS * BLK), row(128), any_spec, pl.BlockSpec(cw_shape, lambda i: (0, 0)),
                   any_spec),
        scratch_shapes=[pltpu.VMEM((tile + 8, D_CONV), F32), pltpu.VMEM((BLK, 2 * D_KV), F32),
                        pltpu.VMEM(wt_shape, BF16), pltpu.VMEM((2, AG_CAST_ROWS, wt_sh.shape[1]), F32),
                        pltpu.VMEM((N_CHIPS,) + cw_sh.shape, F32),
                        dma((8,)), dma((8,)), dma((2,)), dma((3,)), dma((3,)), dma((1,))] + _ag_scratch(wo_sh.shape),
        compiler_params=_params(62, ("arbitrary",)),
    )(x, norm_in, wt_sh, wo_sh, cw_sh, norm_conv_out, sinks, norm_attn_out)


def _conv_core(pc_ref, zbuf, cw_ref):
    tile = pc_ref.shape[0]
    cb = pc_ref[:, 0:D_CONV]
    cc = pc_ref[:, D_CONV:2 * D_CONV]
    cu = pc_ref[:, 2 * D_CONV:3 * D_CONV]
    z = cc * cu
    zbuf[8:tile + 8, :] = z
    z1 = zbuf[7:tile + 7, :]
    z2 = zbuf[6:tile + 6, :]
    conv = cw_ref[0:1, :] * z2 + cw_ref[1:2, :] * z1 + cw_ref[2:3, :] * z
    return cb, cc, cu, z, z1, z2, conv


def _band_geometry(block_index):
    qi = lax.broadcasted_iota(jnp.int32, (BLK, BLK), 0)
    kp = lax.broadcasted_iota(jnp.int32, (BLK, BLK), 1)
    use_cur = kp <= qi
    dist = jnp.where(use_cur, qi - kp, qi - kp + BLK).astype(F32)
    valid = use_cur | (block_index > 0)
    return use_cur, dist, valid


def _block_diag(cur, prev, group):
    lane = lax.broadcasted_iota(jnp.int32, cur.shape, 1)

    def halves(t):
        other = pltpu.roll(t, 64, 1)
        lo, hi = (t, other) if group == 0 else (other, t)
        return jnp.where(lane < 64, lo, 0.0), jnp.where(lane >= 64, hi, 0.0)

    return jnp.concatenate(halves(cur) + halves(prev), axis=0).astype(BF16)


def _merge(s4, use_cur):
    return (jnp.where(use_cur, s4[:, 0:BLK], s4[:, 2 * BLK:3 * BLK]),
            jnp.where(use_cur, s4[:, BLK:2 * BLK], s4[:, 3 * BLK:4 * BLK]))


def _split(a, b, use_cur):
    return jnp.concatenate([jnp.where(use_cur, a, 0.0), jnp.where(use_cur, b, 0.0),
                            jnp.where(use_cur, 0.0, a), jnp.where(use_cur, 0.0, b)], axis=1)


def _softmax_head(s, head, sink, dist, valid):
    sc = jnp.where(valid, s - SLOPES[head] * dist, -jnp.inf)
    m = jnp.maximum(jnp.max(sc, axis=-1, keepdims=True), sink)
    p = jnp.exp(sc - m)
    es = jnp.exp(sink - m)
    inv = 1.0 / (jnp.sum(p, axis=-1, keepdims=True) + es)
    return p * inv, es * inv


def _attn_operands(q_ref, kvc_ref, kvp_ref):
    groups = range(N_HEADS // HEADS_PER_KV)
    kbd = [_block_diag(kvc_ref[:, 0:D_KV], kvp_ref[:, 0:D_KV], g) for g in groups]
    vbd = [_block_diag(kvc_ref[:, D_KV:2 * D_KV], kvp_ref[:, D_KV:2 * D_KV], g) for g in groups]
    qps = [(q_ref[:, j * 128:(j + 1) * 128] * SCALE).astype(BF16) for j in range(N_HEADS // 2)]
    return qps, kbd, vbd


def _attn_forward(q_ref, kvc_ref, kvp_ref, ga_ref, sink_ref, gn_ref, geometry, ya_ref, oa_ref, pr_ref, sp_ref):
    use_cur, dist, valid = geometry
    _, kbd, vbd = operands = _attn_operands(q_ref, kvc_ref, kvp_ref)
    yield
    scores = []
    for j, qp in enumerate(operands[0]):
        scores += _merge(_nt(qp, kbd[j // 4]), use_cur)
    yield
    sinks = [sink_ref[0, h] for h in range(N_HEADS)]
    scores = [jnp.where(valid, s - SLOPES[h] * dist, -jnp.inf) for h, s in enumerate(scores)]
    maxes = [jnp.maximum(jnp.max(s, axis=-1, keepdims=True), sinks[h]) for h, s in enumerate(scores)]
    yield
    exps = [jnp.exp(s - m) for s, m in zip(scores, maxes)]
    sink_exps = [jnp.exp(sinks[h] - m) for h, m in enumerate(maxes)]
    yield
    invs = [1.0 / (jnp.sum(e, axis=-1, keepdims=True) + se) for e, se in zip(exps, sink_exps)]
    probs = [e * inv for e, inv in zip(exps, invs)]
    pr_ref[...] = jnp.concatenate(probs, axis=1).astype(BF16)
    lane = lax.broadcasted_iota(jnp.int32, (BLK, 128), 1)
    sp_ref[...] = sum(jnp.where(lane == h, se * inv, 0.0) for h, (se, inv) in enumerate(zip(sink_exps, invs)))
    yield
    p4s = [_split(probs[2 * j], probs[2 * j + 1], use_cur).astype(BF16) for j in range(N_HEADS // 2)]
    ya = jnp.concatenate([_nn(p4, vbd[j // 4]) for j, p4 in enumerate(p4s)], axis=1)
    ya_ref[...] = ya
    yield
    silu, _ = _silu_and_grad(ga_ref[...])
    oa_ref[...] = (ya * _rstd(ya) * gn_ref[...] * silu).astype(BF16)


def _out_proj_loss(x, oc, oa, wo, norm_final, target, pc, conv_w, norm_conv_out):
    seq = x.shape[0]
    tile = TOK_TILE
    n_tiles = seq // tile

    def body(x_ref, oc_ref, oa_ref, wo_ref, gf_ref, t_ref, pc_ref, hcc_ref, hcu_ref, cw_ref, gn_ref,
             dx2_ref, doa_ref, gwo_ref, dpc_ref, small_ref, loss_acc, zbuf, dbuf):
        step = pl.program_id(0)

        def small_add(row, value):
            small_ref[row:row + 1, :] += jnp.sum(value, axis=0, keepdims=True)

        @pl.when(step == 0)
        def _():
            gwo_ref[...] = jnp.zeros_like(gwo_ref)
            small_ref[...] = jnp.zeros_like(small_ref)
            loss_acc[...] = jnp.zeros_like(loss_acc)
            dbuf[tile:tile + 8, :] = jnp.zeros((8, D_CONV), F32)

        oc, oa = oc_ref[...], oa_ref[...]
        x2 = x_ref[...] + _nn(oc, wo_ref[0:D_CONV, :]) + _nn(oa, wo_ref[D_CONV:D_MIX, :])
        r = _rstd(x2)
        xhat = x2 * r
        err = xhat * gf_ref[...] - t_ref[...]
        loss_acc[...] += jnp.sum(err * err, axis=0, keepdims=True) * (0.5 / D_MODEL)
        dy = err * (1.0 / D_MODEL)
        small_add(SMALL_NORM_FINAL, dy * xhat)
        dx2 = _rms_bwd(dy * gf_ref[...], xhat, r)
        dx2_ref[...] = dx2
        db = dx2.astype(BF16)
        do = _nt(db, wo_ref[0:D_CONV, :])
        doa_ref[...] = _nt(db, wo_ref[D_CONV:D_MIX, :])
        gwo_ref[0:D_CONV, :] += _tn(oc, db)
        gwo_ref[D_CONV:D_MIX, :] += _tn(oa, db)

        is_first_tile = step == n_tiles - 1
        zbuf[0:8, :] = jnp.where(is_first_tile, 0.0, hcc_ref[...] * hcu_ref[...])
        cb, cc, cu, z, z1, z2, conv = _conv_core(pc_ref, zbuf, cw_ref)
        silu, dsilu = _silu_and_grad(pc_ref[:, 3 * D_CONV:4 * D_CONV])
        yc = cb * conv
        rc = _rstd(yc)
        chat = yc * rc
        dn = do * silu
        dpc_ref[:, 3 * D_CONV:4 * D_CONV] = (do * (chat * gn_ref[...]) * dsilu).astype(BF16)
        small_add(SMALL_NORM_CONV, dn * chat)
        dyc = _rms_bwd(dn * gn_ref[...], chat, rc)
        dpc_ref[:, 0:D_CONV] = (dyc * conv).astype(BF16)
        dconv = dyc * cb
        small_add(SMALL_CONV_W, dconv * z2)
        small_add(SMALL_CONV_W + 1, dconv * z1)
        small_add(SMALL_CONV_W + 2, dconv * z)
        dbuf[0:tile, :] = dconv
        dz = cw_ref[2:3, :] * dconv + cw_ref[1:2, :] * dbuf[1:tile + 1, :] + cw_ref[0:1, :] * dbuf[2:tile + 2, :]
        dpc_ref[:, D_CONV:2 * D_CONV] = (dz * cu).astype(BF16)
        dpc_ref[:, 2 * D_CONV:3 * D_CONV] = (dz * cc).astype(BF16)
        dbuf[tile:tile + 8, :] = dbuf[0:8, :]

        @pl.when(step == n_tiles - 1)
        def _():
            lane = lax.broadcasted_iota(jnp.int32, (1, D_MODEL), 1)
            small_ref[SMALL_MISC:SMALL_MISC + 1, :] = jnp.where(lane == SMALL_LOSS_LANE, jnp.sum(loss_acc[...]), 0.0)

    def rev(i):
        return n_tiles - 1 - i

    def row(width):
        return pl.BlockSpec((tile, width), lambda i: (rev(i), 0))

    def halo(col_block):
        return pl.BlockSpec((8, D_CONV), lambda i: (jnp.maximum(rev(i) * (tile // 8) - 1, 0), col_block))

    def const(shape):
        return pl.BlockSpec(shape, lambda i: (0, 0))

    return pl.pallas_call(
        body, name="out_proj_loss", grid=(n_tiles,),
        out_shape=(jax.ShapeDtypeStruct((seq, D_MODEL), F32), jax.ShapeDtypeStruct((seq, D_ATTN), F32),
                   jax.ShapeDtypeStruct((D_MIX, D_MODEL), F32), jax.ShapeDtypeStruct((seq, D_PC), BF16),
                   jax.ShapeDtypeStruct((SMALL_ROWS, D_MODEL), F32)),
        in_specs=[row(D_MODEL), row(D_CONV), row(D_ATTN), _resident(wo.shape), _resident((1, D_MODEL)), row(D_MODEL),
                  row(D_PC), halo(1), halo(2), _resident(conv_w.shape), _resident((1, D_CONV))],
        out_specs=(row(D_MODEL), row(D_ATTN), const((D_MIX, D_MODEL)), row(D_PC), const((SMALL_ROWS, D_MODEL))),
        scratch_shapes=[pltpu.VMEM((1, D_MODEL), F32), pltpu.VMEM((tile + 8, D_CONV), F32),
                        pltpu.VMEM((tile + 8, D_CONV), F32)],
        compiler_params=_params(62, ("arbitrary",)),
    )(x, oc, oa, wo, norm_final, target, pc, pc, pc, conv_w, norm_conv_out)


def _attn_bwd(q, kv, ga, ya, doa, probs, sink_probs, norm_attn_out, gwo, small):
    seq = q.shape[0]
    per_step = 2
    n_steps = seq // (per_step * BLK)
    stage_steps = (0, 1, n_steps // 4, (5 * n_steps) // 8, n_steps - 1, n_steps - 1)

    def body(q_ref, kvc_ref, kvp_ref, ga_ref, ya_ref, doa_ref, pr_ref, sp_ref, gn_ref, gwo_ref, small_ref,
             dqg_ref, small_out, gwo_sh, gna_ref, gs_ref, carry, dya_buf, *rs_scratch):
        step = pl.program_id(0)
        rs_stages = _rs_wout_stages(gwo_ref, gwo_sh, *rs_scratch)
        for at, stage in zip(stage_steps[:-1], rs_stages[:-1]):
            pl.when(step == at)(stage)

        @pl.when(step == 0)
        def _():
            gna_ref[...] = jnp.zeros_like(gna_ref)
            gs_ref[...] = jnp.zeros_like(gs_ref)
            carry[...] = jnp.zeros_like(carry)

        lane = lax.broadcasted_iota(jnp.int32, (BLK, 128), 1)

        def fold(bd):
            return (jnp.where(lane < 64, bd[0:BLK], 0.0) + jnp.where(lane >= 64, bd[BLK:2 * BLK], 0.0),
                    jnp.where(lane < 64, bd[2 * BLK:3 * BLK], 0.0) + jnp.where(lane >= 64, bd[3 * BLK:4 * BLK], 0.0))

        def one_block(b):
            rows = slice(b * BLK, (b + 1) * BLK)
            kv_prev = kvp_ref if b == 0 else kvc_ref.at[(b - 1) * BLK:b * BLK, :]
            ya = ya_ref[rows, :]
            r = _rstd(ya)
            xhat = ya * r
            silu, dsilu = _silu_and_grad(ga_ref[rows, :])
            do = doa_ref[rows, :]
            dn = do * silu
            dqg_ref[rows, D_ATTN:2 * D_ATTN] = (do * (xhat * gn_ref[...]) * dsilu).astype(BF16)
            gna_ref[...] += jnp.sum(dn * xhat, axis=0, keepdims=True)
            dya_buf[rows, :] = _rms_bwd(dn * gn_ref[...], xhat, r).astype(BF16)

            use_cur = _band_geometry(per_step * (n_steps - 1 - step) + b)[0]
            pairs = range(N_HEADS // 2)
            qps, kbd, vbd = _attn_operands(q_ref.at[rows, :], kvc_ref.at[rows, :], kv_prev)
            probs = [pr_ref[rows, h * BLK:(h + 1) * BLK].astype(F32) for h in range(N_HEADS)]
            dyps = [dya_buf[rows, j * 128:(j + 1) * 128] for j in pairs]
            dps = []
            for j in pairs:
                dps += _merge(_nt(dyps[j], vbd[j // 4]), use_cur)
            deltas = [jnp.sum(p * dp, axis=-1, keepdims=True) for p, dp in zip(probs, dps)]
            dss = [p * (dp - delta) for p, dp, delta in zip(probs, dps, deltas)]
            delta_lanes = sum(jnp.where(lane == h, deltas[h], 0.0) for h in range(N_HEADS))
            gs_ref[...] -= jnp.sum(sp_ref[rows, :] * delta_lanes, axis=0, keepdims=True)
            ds4s = [_split(dss[2 * j], dss[2 * j + 1], use_cur).astype(BF16) for j in pairs]
            p4s = [_split(probs[2 * j], probs[2 * j + 1], use_cur).astype(BF16) for j in pairs]
            dqg_ref[rows, 0:D_ATTN] = jnp.concatenate([_nn(ds4s[j], kbd[j // 4]) * SCALE for j in pairs], axis=1).astype(BF16)
            sums = []
            for group in range(N_HEADS // HEADS_PER_KV):
                acc = [jnp.zeros((BLK, 128), F32) for _ in range(4)]
                for j in range(group * 4, group * 4 + 4):
                    for slot, part in enumerate(fold(_tn(ds4s[j], qps[j])) + fold(_tn(p4s[j], dyps[j]))):
                        acc[slot] = acc[slot] + part
                sums.append([a + pltpu.roll(a, 64, 1) for a in acc])
            dk_cur, dk_prev, dv_cur, dv_prev = (jnp.where(lane < 64, lo, hi) for lo, hi in zip(sums[0], sums[1]))
            dqg_ref[rows, 2 * D_ATTN:2 * D_ATTN + 2 * D_KV] = (jnp.concatenate([dk_cur, dv_cur], axis=1) + carry[...]).astype(BF16)
            carry[...] = jnp.concatenate([dk_prev, dv_prev], axis=1)

        for b in reversed(range(per_step)):
            one_block(b)

        @pl.when(step == n_steps - 1)
        def _():
            small_out[...] = small_ref[...]
            small_out[SMALL_NORM_ATTN:SMALL_NORM_ATTN + 1, :] = gna_ref[...]
            small_out[SMALL_MISC:SMALL_MISC + 1, 0:128] = small_ref[SMALL_MISC:SMALL_MISC + 1, 0:128] + gs_ref[...]

        pl.when(step == stage_steps[-1])(rs_stages[-1])

    def rows(width):
        return pl.BlockSpec((per_step * BLK, width), lambda i: (n_steps - 1 - i, 0))

    kv_prev = pl.BlockSpec((BLK, 2 * D_KV), lambda i: (jnp.maximum(per_step * (n_steps - 1 - i) - 1, 0), 0))
    any_spec = pl.BlockSpec(memory_space=pl.ANY)
    return pl.pallas_call(
        body, name="attn_bwd", grid=(n_steps,),
        out_shape=(jax.ShapeDtypeStruct((seq, D_QG), BF16), pltpu.HBM(small.shape, F32),
                   pltpu.HBM((gwo.shape[0] // N_CHIPS, gwo.shape[1]), F32)),
        in_specs=[rows(D_ATTN), rows(2 * D_KV), kv_prev, rows(D_ATTN), rows(D_ATTN), rows(D_ATTN),
                  rows(N_HEADS * BLK), rows(128), _resident((1, D_ATTN)), any_spec, _resident(small.shape)],
        out_specs=(rows(D_QG), pl.BlockSpec(small.shape, lambda i: (0, 0)), any_spec),
        scratch_shapes=[pltpu.VMEM((1, D_ATTN), F32), pltpu.VMEM((1, 128), F32),
                        pltpu.VMEM((BLK, 2 * D_KV), F32), pltpu.VMEM((per_step * BLK, D_ATTN), BF16)] + _rs_wout_scratch(gwo.shape),
        compiler_params=_params(44, ("arbitrary",)),
    )(q, kv, kv, ga, ya, doa, probs, sink_probs, norm_attn_out, gwo, small)


GBLK = 256
GSUB = 64
PAIR_RING = 4
LAG_PAIR, LAG_HOP1, LAG_HOP2 = 1, 9, 17


def _bwd_in(dpc, dqg, h, wt, x, norm_in, dx2, small):
    seq = x.shape[0]
    n_blk = D_IN_PROJ // GBLK
    per_chip = n_blk // N_CHIPS
    n_slots = (n_blk + 1) // 2
    n_sub = GBLK // GSUB
    chip_rows = D_IN_PROJ // N_CHIPS
    tile = TOK_TILE
    n_tiles = seq // tile
    n_steps = n_blk + max(n_tiles, LAG_HOP2 + 1)
    chunk = min(seq, 512)
    blk_q, blk_kv, blk_ga = ROW_Q // GBLK, ROW_KV // GBLK, ROW_GA // GBLK

    def block_of(i):
        k = i % N_CHIPS
        robin = per_chip * ((k % 2) * 2 + k // 2) + i // N_CHIPS
        if isinstance(i, int):
            return robin if i < per_chip * N_CHIPS else i
        return jnp.where(i < per_chip * N_CHIPS, robin, i)

    def owner_of(i):
        return (i // N_CHIPS) % 2

    def slot_of(i):
        return (i // (2 * N_CHIPS)) * N_CHIPS + i % N_CHIPS

    def body(dpc_ref, dqg_ref, wt_ref, h_ref, x_ref, g_ref, dx2_ref, small_ref, gx_ref, small_sum, gwt_sh,
             dh_acc, gni, keep, pbuf, xbuf, land, land2, small_land,
             pair_send, pair_recv, h1_send, h1_recv, h2_send, h2_recv, sw_send, sw_recv, sm_send, sm_recv, out_sem):
        step = pl.program_id(0)
        x_i, y_i, c = lax.axis_index("x"), lax.axis_index("y"), lax.axis_index("c")
        me = 4 * x_i + 2 * y_i + c
        j = 2 * x_i + y_i
        pa = (_xor(x_i, 1 - c), _xor(y_i, c), c)
        pb = (_xor(x_i, c), _xor(y_i, 1 - c), c)
        sib = (x_i, y_i, 1 - c)
        ja = 2 * pa[0] + pa[1]
        jb = 2 * pb[0] + pb[1]
        jd = 3 - j

        def remote(src, dst, send, recv, to):
            return pltpu.make_async_remote_copy(src_ref=src, dst_ref=dst, send_sem=send, recv_sem=recv,
                                                device_id=to, device_id_type=MESH)

        def piece(ref, slot, u, n):
            return ref.at[slot, pl.ds(u * GSUB, n * GSUB), :]

        def chip_rows_at(ref, local, n):
            return ref.at[pl.ds(pl.multiple_of(local, GSUB), n * GSUB), :]

        def pair_copy(i):
            slot = slot_of(i)
            return remote(pbuf.at[i % PAIR_RING], land.at[slot], pair_send.at[slot], pair_recv.at[slot], sib)

        def h1_copy(slot, u, n):
            k = slot * n_sub + u
            return remote(piece(xbuf, slot, u, n), piece(xbuf, slot, u, n), h1_send.at[k], h1_recv.at[k], pa)

        def h2_copy(slot, u, n, local):
            k = slot * n_sub + u
            return remote(piece(xbuf, slot, u, n), chip_rows_at(land2, local, n), h2_send.at[k], h2_recv.at[k], pb)

        def sw_copy(slot, u, n, local):
            k = slot * n_sub + u
            return remote(piece(keep, slot, u, n), chip_rows_at(gwt_sh, local, n), sw_send.at[k], sw_recv.at[k], sib)

        def own_copy(slot, u, n, local):
            return pltpu.make_async_copy(piece(keep, slot, u, n), chip_rows_at(gwt_sh, local, n), out_sem.at[slot * n_sub + u])

        def owned(i):
            return (i >= 0) & (i < n_blk) & (owner_of(i) == c)

        def chip_of(blk, u):
            row = blk * GBLK + u * GSUB
            chip = row // chip_rows
            return chip, row - chip * chip_rows

        def pieces(blk):
            first, local = chip_of(blk, 0)
            whole = first == chip_of(blk, n_sub - 1)[0]
            if isinstance(blk, int):
                return [(True, 0, n_sub, first, local)] if whole else [(True, u, 1) + chip_of(blk, u) for u in range(n_sub)]
            return [(whole, 0, n_sub, first, local)] + [(jnp.logical_not(whole), u, 1) + chip_of(blk, u) for u in range(n_sub)]

        @pl.when(step == 0)
        def _():
            dh_acc[...] = jnp.zeros_like(dh_acc)
            gni[...] = jnp.zeros_like(gni)

        @pl.when(step < n_blk)
        def _():
            from_pc = block_of(step) < blk_q
            block = _tn(jnp.where(from_pc, dpc_ref[...], dqg_ref[...]), h_ref[...])
            for t in range(0, seq, chunk):
                d = jnp.where(from_pc, dpc_ref[t:t + chunk, :], dqg_ref[t:t + chunk, :])
                dh_acc[t:t + chunk, :] += _nn(d, wt_ref[...])

            @pl.when(owner_of(step) == c)
            def _():
                keep[slot_of(step)] = block

            @pl.when(owner_of(step) != c)
            def _():
                @pl.when(step >= 2 * PAIR_RING)
                def _():
                    pair_copy(step - 2 * PAIR_RING).wait_send()
                pbuf[step % PAIR_RING] = block.astype(BF16)
                pair_copy(step).start()

        i1 = step - LAG_PAIR

        @pl.when(owned(i1))
        def _():
            slot = slot_of(i1)
            pair_copy(i1).wait_recv()
            _accumulate(keep.at[slot], land.at[slot])
            for cond, u, n, chip, _ in pieces(block_of(i1)):
                @pl.when(cond & ((chip == ja) | (chip == jd)))
                def _(u=u, n=n):
                    _cast_rows(piece(keep, slot, u, n), piece(xbuf, slot, u, n))
                    h1_copy(slot, u, n).start()

        i2 = step - LAG_HOP1

        @pl.when(owned(i2))
        def _():
            slot = slot_of(i2)
            for cond, u, n, chip, local in pieces(block_of(i2)):
                @pl.when(cond & ((chip == j) | (chip == jb)))
                def _(u=u, n=n, chip=chip, local=local):
                    h1_copy(slot, u, n).wait_recv()
                    _accumulate(piece(keep, slot, u, n), piece(xbuf, slot, u, n))

                    @pl.when(chip == jb)
                    def _():
                        _cast_rows(piece(keep, slot, u, n), piece(xbuf, slot, u, n))
                        h2_copy(slot, u, n, local).start()

                @pl.when(cond & ((chip == ja) | (chip == jd)))
                def _(u=u, n=n):
                    h1_copy(slot, u, n).wait_send()

        i3 = step - LAG_HOP2

        @pl.when(owned(i3))
        def _():
            slot = slot_of(i3)
            for cond, u, n, chip, local in pieces(block_of(i3)):
                @pl.when(cond & (chip == j))
                def _(u=u, n=n, local=local):
                    h2_copy(slot, u, n, local).wait_recv()
                    _accumulate(piece(keep, slot, u, n), chip_rows_at(land2, local, n))
                    own_copy(slot, u, n, local).start()
                    sw_copy(slot, u, n, local).start()

                @pl.when(cond & (chip == jb))
                def _(u=u, n=n, local=local):
                    h2_copy(slot, u, n, local).wait_send()

        e = step - n_blk

        @pl.when((e >= 0) & (e < n_tiles))
        def _():
            dh = dh_acc[pl.ds(pl.multiple_of(e * tile, tile), tile), :]
            xv = x_ref[...]
            r = _rstd(xv)
            xhat = xv * r
            gni[...] += jnp.sum(dh * xhat, axis=0, keepdims=True)
            gx_ref[...] = _rms_bwd(dh * g_ref[...], xhat, r) + dx2_ref[...]

        others = [(dx, dy, dc) for dx in (0, 1) for dy in (0, 1) for dc in (0, 1)][1:]
        sends = [remote(small_land.at[me], small_land.at[me], sm_send.at[k], sm_recv.at[k],
                        (_xor(x_i, dx), _xor(y_i, dy), _xor(c, dc))) for k, (dx, dy, dc) in enumerate(others)]

        @pl.when(step == min(n_blk + n_tiles, n_steps - 1))
        def _():
            small_land[me] = small_ref[...]
            small_land[me, SMALL_NORM_IN:SMALL_NORM_IN + 1, :] = gni[...]
            for cp in sends:
                cp.start()

        @pl.when(step == n_steps - 1)
        def _():
            for i in range(n_blk):
                if i + 2 * PAIR_RING >= n_blk:
                    @pl.when(owner_of(i) != c)
                    def _(i=i):
                        pair_copy(i).wait_send()
                for _, u, n, chip, local in pieces(block_of(i)):
                    @pl.when((j == chip) & (c == owner_of(i)))
                    def _(i=i, u=u, n=n, local=local):
                        sw_copy(slot_of(i), u, n, local).wait_send()
                        own_copy(slot_of(i), u, n, local).wait()

                    @pl.when((j == chip) & (c != owner_of(i)))
                    def _(i=i, u=u, n=n, local=local):
                        sw_copy(slot_of(i), u, n, local).wait_recv()
            for cp in sends:
                cp.wait_recv()
            total = small_land[0]
            for dev in range(1, 8):
                total = total + small_land[dev]
            small_sum[...] = total
            for cp in sends:
                cp.wait_send()

    def blk_at(i):
        return block_of(jnp.clip(i, 0, n_blk - 1))

    last_pc_step = max(i for i in range(n_blk) if block_of(i) < blk_q)

    def next_block(i, in_pc):
        i = jnp.clip(i, 0, n_blk - 1)
        step = jnp.full_like(i, last_pc_step if in_pc else n_blk - 1)
        for ahead in reversed(range(N_CHIPS)):
            cand = jnp.minimum(i + ahead, n_blk - 1)
            step = jnp.where((block_of(cand) < blk_q) == in_pc, cand, step)
        return block_of(step)

    def dqg_block(i):
        b = next_block(i, False)
        q_blk = jnp.clip(b - blk_q, 0, blk_kv - blk_q - 1)
        ga_blk = (D_ATTN // GBLK) + jnp.clip(b - blk_ga, 0, n_blk - blk_ga - 1)
        return jnp.where(b < blk_kv, q_blk, jnp.where(b == blk_kv, 2 * D_ATTN // GBLK, ga_blk))

    def tok(i):
        return (jnp.clip(i - n_blk, 0, n_tiles - 1), 0)

    n_piece = n_slots * n_sub
    dma = pltpu.SemaphoreType.DMA
    return pl.pallas_call(
        body, name="bwd_in", grid=(n_steps,),
        out_shape=(jax.ShapeDtypeStruct((seq, D_MODEL), F32), jax.ShapeDtypeStruct(small.shape, F32),
                   jax.ShapeDtypeStruct((chip_rows, D_MODEL), F32)),
        in_specs=[pl.BlockSpec((seq, GBLK), lambda i: (0, next_block(i, True))),
                  pl.BlockSpec((seq, GBLK), lambda i: (0, dqg_block(i))),
                  pl.BlockSpec((GBLK, D_MODEL), lambda i: (blk_at(i), 0)),
                  _resident(h.shape),
                  pl.BlockSpec((tile, D_MODEL), tok), _resident((1, D_MODEL)), pl.BlockSpec((tile, D_MODEL), tok),
                  _resident(small.shape)],
        out_specs=(pl.BlockSpec((tile, D_MODEL), tok), pl.BlockSpec(small.shape, lambda i: (0, 0)),
                   pl.BlockSpec(memory_space=pl.ANY)),
        scratch_shapes=[pltpu.VMEM((seq, D_MODEL), F32), pltpu.VMEM((1, D_MODEL), F32),
                        pltpu.VMEM((n_slots, GBLK, D_MODEL), F32), pltpu.VMEM((PAIR_RING, GBLK, D_MODEL), BF16),
                        pltpu.VMEM((n_slots, GBLK, D_MODEL), BF16), pltpu.VMEM((n_slots, GBLK, D_MODEL), BF16),
                        pltpu.VMEM((chip_rows, D_MODEL), BF16), pltpu.VMEM((8,) + small.shape, F32),
                        dma((n_slots,)), dma((n_slots,)), dma((n_piece,)), dma((n_piece,)), dma((n_piece,)),
                        dma((n_piece,)), dma((n_piece,)), dma((n_piece,)), dma((7,)), dma((7,)), dma((n_piece,))],
        compiler_params=_params(62, ("arbitrary",)),
    )(dpc, dqg, wt, h, x, norm_in, dx2, small)


def _accumulate(dst_ref, src_ref, rows=16):
    def step(i, carry):
        sl = pl.ds(pl.multiple_of(i * rows, rows), rows)
        dst_ref[sl, :] = dst_ref[sl, :] + src_ref[sl, :].astype(F32)
        return carry
    lax.fori_loop(0, dst_ref.shape[0] // rows, step, 0)


def _rs_wout_scratch(gwo_shape):
    o_half, width = gwo_shape[0] // N_CHIPS // 2, gwo_shape[1]
    return [pltpu.VMEM((4, o_half, width), F32), pltpu.VMEM((4, o_half, width), BF16),
            pltpu.VMEM((4, o_half, width), BF16), pltpu.VMEM((2, o_half, width), BF16),
            pltpu.VMEM((o_half, width), BF16),
            pltpu.SemaphoreType.DMA((8,)), pltpu.SemaphoreType.DMA((8,)), pltpu.SemaphoreType.DMA((4,))]


def _rs_wout_stages(gwo_ref, gwo_sh, acc_o, sb_o, r1o, r2o, r3o, send_sems, recv_sems, local_sems):
    o_rows = gwo_ref.shape[0] // N_CHIPS
    o_half = o_rows // 2
    x, y, c = lax.axis_index("x"), lax.axis_index("y"), lax.axis_index("c")
    j = 2 * x + y
    pa = (_xor(x, 1 - c), _xor(y, c), c)
    pb = (_xor(x, c), _xor(y, 1 - c), c)
    sib = (x, y, 1 - c)
    ja = 2 * pa[0] + pa[1]
    jb = 2 * pb[0] + pb[1]
    jd = 3 - j
    order = (ja, jd, jb, j)
    sib_order = (jb, jd, ja, j)

    def rcopy(k, src, dst, to):
        return pltpu.make_async_remote_copy(src_ref=src, dst_ref=dst, send_sem=send_sems.at[k],
                                            recv_sem=recv_sems.at[k], device_id=to, device_id_type=MESH)

    def o_rows_of(chip, half):
        return gwo_ref.at[pl.ds(pl.multiple_of(chip * o_rows + half * o_half, 8), o_half), :]

    def loads(chips, half):
        return [pltpu.make_async_copy(o_rows_of(chip, half), acc_o.at[s], local_sems.at[s]) for s, chip in enumerate(chips)]

    def pair_send(s):
        return rcopy(s, sb_o.at[s], r1o.at[s], sib)

    def out_half(half):
        return gwo_sh.at[pl.ds(pl.multiple_of(half * o_half, 8), o_half), :]

    hop1 = [rcopy(4 + s, sb_o.at[s], r2o.at[s], pa) for s in range(2)]
    hop2 = rcopy(6, sb_o.at[2], r3o, pb)
    swap = rcopy(7, acc_o.at[3], out_half(c), sib)
    mine = pltpu.make_async_copy(acc_o.at[3], out_half(c), local_sems.at[0])

    def resend(s, copy):
        pair_send(s).wait_send()
        _cast_rows(acc_o.at[s], sb_o.at[s])
        copy.start()

    def stage_load():
        for cp in loads(sib_order, 1 - c):
            cp.start()

    def stage_pair():
        for s, (cp, mine_in) in enumerate(zip(loads(sib_order, 1 - c), loads(order, c))):
            cp.wait()
            _cast_rows(acc_o.at[s], sb_o.at[s])
            pair_send(s).start()
            mine_in.start()

    def stage_hop1():
        for s, cp in enumerate(loads(order, c)):
            cp.wait()
            pair_send(s).wait_recv()
            _accumulate(acc_o.at[s], r1o.at[s])
            if s < 2:
                resend(s, hop1[s])

    def stage_hop2():
        hop1[1].wait_recv()
        _accumulate(acc_o.at[2], r2o.at[1])
        resend(2, hop2)
        hop1[0].wait_recv()
        _accumulate(acc_o.at[3], r2o.at[0])

    def stage_final():
        hop2.wait_recv()
        _accumulate(acc_o.at[3], r3o)
        mine.start()
        swap.start()

    def stage_drain():
        rcopy(7, acc_o.at[3], out_half(1 - c), sib).wait_recv()
        for cp in [pair_send(3)] + hop1 + [hop2, swap]:
            cp.wait_send()
        mine.wait()

    return stage_load, stage_pair, stage_hop1, stage_hop2, stage_final, stage_drain


def _adamw_big(groups, passed):
    arrays = [a for group in groups for a in group[:4]]
    steps = [group[0].shape[0] // group[4] for group in groups]
    first = [sum(steps[:k]) for k in range(len(steps) + 1)]
    n = len(arrays)

    def body(*refs):
        ins, passed_in, outs, passed_out = refs[:n], refs[n], refs[n + 1:2 * n + 1], refs[2 * n + 1]
        i = pl.program_id(0)
        for k in range(len(groups)):
            @pl.when((i >= first[k]) & (i < first[k + 1]))
            def _(k=k):
                w_ref, g_ref, m_ref, v_ref = ins[4 * k:4 * k + 4]
                _adamw_update(w_ref, g_ref[...], m_ref, v_ref, *outs[4 * k:4 * k + 4])

        @pl.when(i < steps[0])
        def _():
            passed_out[...] = passed_in[...]

    def spec(k, rows, width):
        return pl.BlockSpec((rows, width), lambda i: (jnp.clip(i - first[k], 0, steps[k] - 1), 0))

    specs = [spec(k, group[4], group[0].shape[1]) for k, group in enumerate(groups) for _ in range(4)]
    specs.append(spec(0, passed.shape[0] // steps[0], passed.shape[1]))
    out = pl.pallas_call(
        body, name="adamw_big", grid=(first[-1],),
        out_shape=tuple(jax.ShapeDtypeStruct(a.shape, a.dtype) for a in arrays + [passed]),
        in_specs=specs, out_specs=tuple(specs),
        compiler_params=_params(40, ("arbitrary",)),
    )(*arrays, passed)
    return [tuple(out[4 * k:4 * k + 4]) for k in range(len(groups))], out[-1]


def _adamw_update(w_ref, gv, m_ref, v_ref, go_ref, d_ref, nm_ref, nv_ref, at=...):
    go_ref[at] = gv
    nm = ADAM_B1 * m_ref[at] + (1.0 - ADAM_B1) * gv
    nv = ADAM_B2 * v_ref[at] + (1.0 - ADAM_B2) * (gv * gv)
    m_hat = nm / (1.0 - ADAM_B1 ** ADAM_STEP)
    v_hat = nv / (1.0 - ADAM_B2 ** ADAM_STEP)
    d_ref[at] = -ADAM_LR * (m_hat / (jnp.sqrt(v_hat) + ADAM_EPS) + ADAM_WD * w_ref[at])
    nm_ref[at] = nm
    nv_ref[at] = nv


def _adamw_small(chip, small_sum, weights, grads_of, ms, vs):
    n = len(weights)

    def body(chip_ref, small_ref, *refs):
        ins, outs, loss_ref = refs[:3 * n], refs[3 * n:-1], refs[-1]
        for k in range(n):
            w_ref, m_ref, v_ref = ins[3 * k:3 * k + 3]
            for at, gv in grads_of[k](small_ref, chip_ref):
                _adamw_update(w_ref, gv, m_ref, v_ref, *outs[4 * k:4 * k + 4], at=at)
        loss_ref[...] = small_ref[SMALL_MISC:SMALL_MISC + 1, SMALL_LOSS_LANE:SMALL_LOSS_LANE + 1]

    flat = [a for group in zip(weights, ms, vs) for a in group]
    vmem = pl.BlockSpec(memory_space=pltpu.VMEM)
    out = pl.pallas_call(
        body, name="adamw_small",
        out_shape=tuple(jax.ShapeDtypeStruct(w.shape, F32) for w in weights for _ in range(4))
        + (jax.ShapeDtypeStruct((1, 1), F32),),
        in_specs=[pl.BlockSpec(memory_space=pltpu.SMEM)] + [vmem] * (1 + 3 * n), out_specs=(vmem,) * (4 * n + 1),
    )(chip, small_sum, *flat)
    return [tuple(out[4 * k:4 * k + 4]) for k in range(n)], out[-1][0, 0]


def kernel(x, norm_in, w_in, conv_w, attn_sinks, norm_conv_out, norm_attn_out, w_out, norm_final, loss_target, m_norm_in, m_w_in, m_conv_w, m_attn_sinks, m_norm_conv_out, m_norm_attn_out, m_w_out, m_norm_final, v_norm_in, v_w_in, v_conv_w, v_attn_sinks, v_norm_conv_out, v_norm_attn_out, v_w_out, v_norm_final):
    chip = 2 * lax.axis_index("x") + lax.axis_index("y")
    xs, target = x[0], loss_target[0]
    norm_final2 = norm_final.reshape(1, D_MODEL)
    w_in_t, m_w_in_t, v_w_in_t = w_in[0].T, m_w_in[0].T, v_w_in[0].T

    def from_hbm(*arrays):
        return tuple(pltpu.with_memory_space_constraint(a, pltpu.HBM) for a in arrays)

    h, pc, q, kv, ga, oc, ya, oa, probs, sink_probs, wt, cw, wo = _fwd_in(
        xs, norm_in, w_in_t, w_out[0], conv_w.transpose(1, 0, 2), norm_conv_out, attn_sinks, norm_attn_out)
    dx2, doa, gwo, dpc, small = _out_proj_loss(xs, oc, oa, wo, norm_final2, target, pc, cw, norm_conv_out)
    q, kv, ga, ya, doa, probs, gwo, small = from_hbm(q, kv, ga, ya, doa, probs, gwo, small)
    dqg, small, gwo_sh = _attn_bwd(q, kv, ga, ya, doa, probs, sink_probs, norm_attn_out, gwo, small)
    grad_x, small_sum, gwt_sh = _bwd_in(dpc, dqg, h, wt, xs, norm_in, dx2, small)

    (up_w_in, up_w_out), grad_x = _adamw_big(
        [from_hbm(w_in_t, gwt_sh, m_w_in_t, v_w_in_t) + (200,), from_hbm(w_out[0], gwo_sh, m_w_out[0], v_w_out[0]) + (128,)],
        *from_hbm(grad_x))
    up_w_in = tuple(o.T[None] for o in up_w_in)
    up_w_out = tuple(o[None] for o in up_w_out)

    def row_of(r):
        return lambda small_ref, chip_ref: [(..., small_ref[r:r + 1, :])]

    def sink_lanes(small_ref, chip_ref):
        return [(..., small_ref[SMALL_MISC:SMALL_MISC + 1, 0:N_HEADS])]

    def conv_taps(small_ref, chip_ref):
        width = D_CONV // N_CHIPS
        cols = pl.ds(pl.multiple_of(chip_ref[0] * width, width), width)
        return [(k, small_ref[pl.ds(SMALL_CONV_W + k, 1), cols]) for k in range(conv_w.shape[1])]

    def small_view(a):
        return a.transpose(1, 0, 2) if a.ndim == 3 else a.reshape(-1, a.shape[-1])

    small_w = (norm_in, conv_w, attn_sinks, norm_conv_out, norm_attn_out, norm_final)
    small_m = (m_norm_in, m_conv_w, m_attn_sinks, m_norm_conv_out, m_norm_attn_out, m_norm_final)
    small_v = (v_norm_in, v_conv_w, v_attn_sinks, v_norm_conv_out, v_norm_attn_out, v_norm_final)
    small_g = (row_of(SMALL_NORM_IN), conv_taps, sink_lanes, row_of(SMALL_NORM_CONV), row_of(SMALL_NORM_ATTN),
               row_of(SMALL_NORM_FINAL))
    w_views, m_views, v_views = (tuple(small_view(a) for a in group) for group in (small_w, small_m, small_v))
    up_small, loss = _adamw_small(chip.astype(jnp.int32).reshape(1), small_sum, w_views, small_g, m_views, v_views)
    up_small = [tuple(o.transpose(1, 0, 2) if w.ndim == 3 else o.reshape(w.shape) for o in up)
                for up, w in zip(up_small, small_w)]
    up_norm_in, up_conv_w, up_sinks, up_norm_conv, up_norm_attn, up_norm_final = up_small
    updates = (up_norm_in, up_w_in, up_conv_w, up_sinks, up_norm_conv, up_norm_attn, up_w_out, up_norm_final)
    grads_out, deltas, new_m, new_v = zip(*updates)
    return (loss, grad_x[None], *grads_out, *deltas, *new_m, *new_v)
```

```python
import jax
import jax.numpy as jnp
from jax import lax
from jax.experimental import pallas as pl
from jax.experimental.pallas import tpu as pltpu

F32 = jnp.float32
BF16 = jnp.bfloat16
MESH = pl.DeviceIdType.MESH

D_MODEL = 1024
D_CONV = 1024
D_ATTN = 1024
D_KV = 128
D_QG = 2 * D_ATTN + 2 * D_KV
D_MIX = D_CONV + D_ATTN
D_PC = 4 * D_CONV
D_IN_PROJ = D_PC + 2 * D_ATTN + 2 * D_KV
ROW_Q = D_PC
ROW_KV = ROW_Q + D_ATTN
ROW_GA = ROW_KV + 2 * D_KV
N_HEADS = 16
HEAD_DIM = 64
HEADS_PER_KV = 8
BLK = 128
N_CHIPS = 4
RMS_EPS = 1e-5
SCALE = HEAD_DIM ** -0.5
SLOPES = tuple(2.0 ** (-8.0 * (h + 1) / N_HEADS) for h in range(N_HEADS))

ADAM_LR, ADAM_B1, ADAM_B2, ADAM_EPS, ADAM_WD, ADAM_STEP = 0.001, 0.9, 0.999, 1e-08, 0.01, 10

SMALL_ROWS = 8
SMALL_NORM_IN, SMALL_NORM_CONV, SMALL_NORM_ATTN, SMALL_NORM_FINAL, SMALL_CONV_W, SMALL_MISC = 0, 1, 2, 3, 4, 7
SMALL_LOSS_LANE = N_HEADS

TOK_TILE = 256
PC_PIECE = 512
MIB = 1 << 20


def _params(vmem_mib, semantics=None):
    return pltpu.CompilerParams(dimension_semantics=semantics, vmem_limit_bytes=vmem_mib * MIB)


def _nn(a, b):
    return jnp.dot(a, b, preferred_element_type=F32)


def _nt(a, b):
    return lax.dot_general(a, b, (((1,), (1,)), ((), ())), preferred_element_type=F32)


def _tn(a, b):
    return lax.dot_general(a, b, (((0,), (0,)), ((), ())), preferred_element_type=F32)


def _rstd(v):
    return lax.rsqrt(jnp.mean(v * v, axis=-1, keepdims=True) + RMS_EPS)


def _rms_bwd(g, xhat, rstd):
    return rstd * (g - xhat * jnp.mean(g * xhat, axis=-1, keepdims=True))


def _silu_and_grad(g):
    s = jax.nn.sigmoid(g)
    return g * s, s * (1.0 + g * (1.0 - s))


def _resident(shape):
    return pl.BlockSpec(shape, lambda *_: (0,) * len(shape), pipeline_mode=pl.Buffered(1))


def _xor(a, b):
    return a + b - 2 * a * b


def _cast_rows(src_ref, dst_ref, rows=32):
    def step(i, carry):
        sl = pl.ds(pl.multiple_of(i * rows, rows), rows)
        dst_ref[sl, :] = src_ref[sl, :].astype(dst_ref.dtype)
        return carry
    lax.fori_loop(0, src_ref.shape[0] // rows, step, 0)


def _ag_scratch(shard_shape):
    rows, width = shard_shape
    return [pltpu.VMEM((rows, width), F32), pltpu.VMEM((rows, width), BF16), pltpu.VMEM((3, rows // 2, width), BF16),
            pltpu.SemaphoreType.DMA((6,)), pltpu.SemaphoreType.DMA((6,)), pltpu.SemaphoreType.DMA((4,))]


def _ag_stages(sh_ref, out, f32_buf, own, land, send_sems, recv_sems, local_sems):
    rows = sh_ref.shape[0]
    half = rows // 2
    x, y, c = lax.axis_index("x"), lax.axis_index("y"), lax.axis_index("c")
    j = 2 * x + y
    p1 = (_xor(x, c), _xor(y, 1 - c), c)
    p2 = (_xor(x, 1 - c), _xor(y, c), c)
    sib = (x, y, 1 - c)
    j1 = 2 * p1[0] + p1[1]
    j2 = 2 * p2[0] + p2[1]
    j3 = 3 - j

    def rows_of(chip, hf):
        return out.at[pl.ds(pl.multiple_of(chip * rows + hf * half, 16), half), :]

    def rcopy(k, src, dst, to):
        return pltpu.make_async_remote_copy(src_ref=src, dst_ref=dst, send_sem=send_sems.at[k],
                                            recv_sem=recv_sems.at[k], device_id=to, device_id_type=MESH)

    my_half = own.at[pl.ds(pl.multiple_of(c * half, 16), half), :]
    hop1 = rcopy(0, my_half, land.at[0], p1)
    hop2_own = rcopy(1, my_half, land.at[1], p2)
    hop2_fwd = rcopy(2, land.at[0], land.at[2], p2)
    swaps = [rcopy(3 + s, land.at[s], rows_of(chip, c), sib) for s, chip in enumerate((j1, j2, j3))]
    keeps = [pltpu.make_async_copy(land.at[s], rows_of(chip, c), local_sems.at[1 + s]) for s, chip in enumerate((j1, j2, j3))]
    load = pltpu.make_async_copy(sh_ref, f32_buf, local_sems.at[0])
    own_out = pltpu.make_async_copy(own, out.at[pl.ds(pl.multiple_of(j * rows, 16), rows), :], local_sems.at[0])

    def stage_load():
        load.start()

    def stage_send():
        load.wait()
        _cast_rows(f32_buf, own)
        own_out.start()
        hop1.start()

    def stage_forward():
        hop1.wait_recv()
        hop2_own.start()
        hop2_fwd.start()
        swaps[0].start()
        keeps[0].start()

    def stage_publish():
        hop2_own.wait_recv()
        swaps[1].start()
        keeps[1].start()
        hop2_fwd.wait_recv()
        swaps[2].start()
        keeps[2].start()

    def stage_drain():
        for s, chip in enumerate((j2, j1, j3)):
            rcopy(3 + s, my_half, rows_of(chip, 1 - c), sib).wait_recv()
        for cp in [hop1, hop2_own, hop2_fwd] + swaps:
            cp.wait_send()
        for cp in [own_out] + keeps:
            cp.wait()

    return stage_load, stage_send, stage_forward, stage_publish, stage_drain


AG_CAST_ROWS = 400


def _gather_resident(sh_ref, out, f32_buf, send_sems, recv_sems, local_sems, after_first_hop):
    rows = sh_ref.shape[0]
    half = rows // 2
    x, y, c = lax.axis_index("x"), lax.axis_index("y"), lax.axis_index("c")
    j = 2 * x + y
    p1 = (_xor(x, c), _xor(y, 1 - c), c)
    p2 = (_xor(x, 1 - c), _xor(y, c), c)
    sib = (x, y, 1 - c)
    j1 = 2 * p1[0] + p1[1]
    j2 = 2 * p2[0] + p2[1]
    j3 = 3 - j

    def rows_of(chip, hf):
        return out.at[pl.ds(pl.multiple_of(chip * rows + hf * half, 16), half), :]

    def send(k, chip, to):
        return pltpu.make_async_remote_copy(src_ref=rows_of(chip, c), dst_ref=rows_of(chip, c), send_sem=send_sems.at[k],
                                            recv_sem=recv_sems.at[k], device_id=to, device_id_type=MESH)

    per_half = half // AG_CAST_ROWS

    chunks = [c * per_half + k for k in range(per_half)] + [(1 - c) * per_half + k for k in range(per_half)]

    def load(n):
        lo = pl.multiple_of(chunks[n] * AG_CAST_ROWS, 16)
        return pltpu.make_async_copy(sh_ref.at[pl.ds(lo, AG_CAST_ROWS), :], f32_buf.at[n % 2], local_sems.at[n % 2])

    def send_piece(k, chip, q, to):
        at = out.at[pl.ds(pl.multiple_of(chip * rows + c * half + q * (half // 2), 16), half // 2), :]
        return pltpu.make_async_remote_copy(src_ref=at, dst_ref=at, send_sem=send_sems.at[k], recv_sem=recv_sems.at[k],
                                            device_id=to, device_id_type=MESH)

    forwards = [send_piece(2, j1, 0, p2), send_piece(6, j1, 1, p2)]
    sends = [send(0, j, p1), send(1, j, p2)] + forwards + [send(3, j1, sib)]
    load(0).start()
    load(1).start()
    for n in range(len(chunks)):
        load(n).wait()
        lo = pl.multiple_of(chunks[n] * AG_CAST_ROWS, 16)
        _cast_rows(f32_buf.at[n % 2], out.at[pl.ds(pl.multiple_of(j * rows + lo, 16), AG_CAST_ROWS), :], rows=16)
        if n + 2 < len(chunks):
            load(n + 2).start()
        if n == per_half - 1:
            sends[0].start()
            sends[1].start()
    sends[0].wait_recv()
    for cp in sends[2:]:
        cp.start()
    after_first_hop()
    sends[1].wait_recv()
    sends.append(send(4, j2, sib))
    sends[-1].start()
    for q, (k, arrived) in enumerate(zip((5, 7), forwards)):
        arrived.wait_recv()
        sends.append(send_piece(k, j3, q, sib))
        sends[-1].start()
    for k in (3, 4):
        send(k, j, sib).wait_recv()
    for k in (5, 7):
        send_piece(k, j, 0, sib).wait_recv()
    for cp in sends:
        cp.wait_send()


def _fwd_in(x, norm_in, wt_sh, wo_sh, cw_sh, norm_conv_out, sinks, norm_attn_out):
    seq = x.shape[0]
    tile = TOK_TILE
    n_tiles = seq // tile
    stage_steps = (0, n_tiles // 4, (5 * n_tiles) // 8, n_tiles - 1)
    blocks = tile // BLK
    cw_cols = cw_sh.shape[-1]

    def body(x_ref, g_ref, wtsh_ref, wo_ref, cwsh_ref, gn_ref, sink_ref, gna_ref,
             h_ref, pc_ref, q_ref, kv_ref, ga_ref, oc_ref, ya_ref, oa_ref, pr_ref, sp_ref, wt_out, cw_ref, wo_out,
             zbuf, kv_last, wt_ref, f32_buf, cw_land, wt_send, wt_recv, wt_local, cw_send, cw_recv, cw_local, *ag_scratch):
        step = pl.program_id(0)
        to_hbm = pltpu.make_async_copy(wt_ref, wt_out, wt_local.at[0])
        load_wo, *stages = _ag_stages(wo_ref, wo_out, *ag_scratch)

        @pl.when(step == 0)
        def _():
            load_wo()
            zbuf[0:8, :] = jnp.zeros((8, D_CONV), F32)
            kv_last[...] = jnp.zeros_like(kv_last)
            x_i, y_i, c = lax.axis_index("x"), lax.axis_index("y"), lax.axis_index("c")
            j = 2 * x_i + y_i
            p1 = (_xor(x_i, c), _xor(y_i, 1 - c), c)
            p2 = (_xor(x_i, 1 - c), _xor(y_i, c), c)
            j1 = 2 * p1[0] + p1[1]

            def cw_copy(k, src, chip, to):
                return pltpu.make_async_remote_copy(src_ref=src, dst_ref=cw_land.at[chip], send_sem=cw_send.at[k],
                                                    recv_sem=cw_recv.at[k], device_id=to, device_id_type=MESH)

            mine = pltpu.make_async_copy(cwsh_ref, cw_land.at[j], cw_local.at[0])
            mine.start()
            first = cw_copy(0, cwsh_ref, j, p1)
            first.start()
            second = [cw_copy(1, cwsh_ref, j, p2), cw_copy(2, cw_land.at[j1], j1, p2)]

            def cw_second_hop():
                first.wait_recv()
                for cp in second:
                    cp.start()

            _gather_resident(wtsh_ref, wt_ref, f32_buf, wt_send, wt_recv, wt_local, cw_second_hop)
            to_hbm.start()
            for cp in second:
                cp.wait_recv()
            for cp in [first] + second:
                cp.wait_send()
            mine.wait()
            for chip in range(N_CHIPS):
                for tap in range(cw_sh.shape[0]):
                    cw_ref[tap:tap + 1, chip * cw_cols:(chip + 1) * cw_cols] = cw_land[chip, tap]

        for at, stage in zip(stage_steps[:-1], stages[:-1]):
            pl.when(step == at)(stage)

        def attention(b):
            rows = pl.ds(b * BLK, BLK)
            kv_prev = kv_last if b == 0 else kv_ref.at[pl.ds((b - 1) * BLK, BLK), :]
            return _attn_forward(q_ref.at[rows, :], kv_ref.at[rows, :], kv_prev, ga_ref.at[rows, :], sink_ref, gna_ref,
                                 _band_geometry(step * blocks + b), ya_ref.at[rows, :], oa_ref.at[rows, :],
                                 pr_ref.at[rows, :], sp_ref.at[rows, :])

        xv = x_ref[...]
        h = (xv * _rstd(xv) * g_ref[...]).astype(BF16)
        h_ref[...] = h
        q_ref[...] = _nt(h, wt_ref[ROW_Q:ROW_KV, :])
        kv_ref[...] = _nt(h, wt_ref[ROW_KV:ROW_GA, :])
        ga_ref[...] = _nt(h, wt_ref[ROW_GA:D_IN_PROJ, :])
        attention_blocks = [attention(b) for b in range(blocks)]
        for lo in range(0, D_PC, PC_PIECE):
            pc_ref[:, lo:lo + PC_PIECE] = _nt(h, wt_ref[lo:lo + PC_PIECE, :])
            for stages_of_block in attention_blocks:
                next(stages_of_block, None)
        for stages_of_block in attention_blocks:
            for _ in stages_of_block:
                pass
        kv_last[...] = kv_ref[tile - BLK:tile, :]

        cb, _, _, _, _, _, conv = _conv_core(pc_ref, zbuf, cw_ref)
        yc = cb * conv
        silu, _ = _silu_and_grad(pc_ref[:, 3 * D_CONV:4 * D_CONV])
        oc_ref[...] = (yc * _rstd(yc) * gn_ref[...] * silu).astype(BF16)
        zbuf[0:8, :] = zbuf[tile:tile + 8, :]

        @pl.when(step == stage_steps[-1])
        def _():
            stages[-1]()
            to_hbm.wait()

    def row(width):
        return pl.BlockSpec((tile, width), lambda i: (i, 0))

    any_spec = pl.BlockSpec(memory_space=pl.ANY)
    dma = pltpu.SemaphoreType.DMA
    wt_shape = (N_CHIPS * wt_sh.shape[0], wt_sh.shape[1])
    cw_shape = (cw_sh.shape[0], N_CHIPS * cw_cols)
    return pl.pallas_call(
        body, name="fwd_in", grid=(n_tiles,),
        out_shape=(jax.ShapeDtypeStruct((seq, D_MODEL), BF16), jax.ShapeDtypeStruct((seq, D_PC), F32),
                   jax.ShapeDtypeStruct((seq, D_ATTN), F32), jax.ShapeDtypeStruct((seq, 2 * D_KV), F32),
                   jax.ShapeDtypeStruct((seq, D_ATTN), F32), jax.ShapeDtypeStruct((seq, D_CONV), BF16),
                   pltpu.HBM((seq, D_ATTN), F32), pltpu.HBM((seq, D_ATTN), BF16),
                   pltpu.HBM((seq, N_HEADS * BLK), BF16), pltpu.HBM((seq, 128), F32),
                   pltpu.HBM(wt_shape, BF16), jax.ShapeDtypeStruct(cw_shape, F32),
                   jax.ShapeDtypeStruct((N_CHIPS * wo_sh.shape[0], wo_sh.shape[1]), BF16)),
        in_specs=[row(D_MODEL), _resident((1, D_MODEL)), any_spec, any_spec, any_spec, _resident((1, D_CONV)),
                  pl.BlockSpec(memory_space=pltpu.SMEM), _resident((1, D_ATTN))],
        out_specs=(row(D_MODEL), row(D_PC), row(D_ATTN), row(2 * D_KV), row(D_ATTN), row(D_CONV), row(D_ATTN),
                   row(D_ATTN), row(N_HEADS * BLK), row(128), any_spec, pl.BlockSpec(cw_shape, lambda i: (0, 0)),
                   any_spec),
        scratch_shapes=[pltpu.VMEM((tile + 8, D_CONV), F32), pltpu.VMEM((BLK, 2 * D_KV), F32),
                        pltpu.VMEM(wt_shape, BF16), pltpu.VMEM((2, AG_CAST_ROWS, wt_sh.shape[1]), F32),
                        pltpu.VMEM((N_CHIPS,) + cw_sh.shape, F32),
                        dma((8,)), dma((8,)), dma((2,)), dma((3,)), dma((3,)), dma((1,))] + _ag_scratch(wo_sh.shape),
        compiler_params=_params(62, ("arbitrary",)),
    )(x, norm_in, wt_sh, wo_sh, cw_sh, norm_conv_out, sinks, norm_attn_out)


def _conv_core(pc_ref, zbuf, cw_ref):
    tile = pc_ref.shape[0]
    cb = pc_ref[:, 0:D_CONV]
    cc = pc_ref[:, D_CONV:2 * D_CONV]
    cu = pc_ref[:, 2 * D_CONV:3 * D_CONV]
    z = cc * cu
    zbuf[8:tile + 8, :] = z
    z1 = zbuf[7:tile + 7, :]
    z2 = zbuf[6:tile + 6, :]
    conv = cw_ref[0:1, :] * z2 + cw_ref[1:2, :] * z1 + cw_ref[2:3, :] * z
    return cb, cc, cu, z, z1, z2, conv


def _band_geometry(block_index):
    qi = lax.broadcasted_iota(jnp.int32, (BLK, BLK), 0)
    kp = lax.broadcasted_iota(jnp.int32, (BLK, BLK), 1)
    use_cur = kp <= qi
    dist = jnp.where(use_cur, qi - kp, qi - kp + BLK).astype(F32)
    valid = use_cur | (block_index > 0)
    return use_cur, dist, valid


def _block_diag(cur, prev, group):
    lane = lax.broadcasted_iota(jnp.int32, cur.shape, 1)

    def halves(t):
        other = pltpu.roll(t, 64, 1)
        lo, hi = (t, other) if group == 0 else (other, t)
        return jnp.where(lane < 64, lo, 0.0), jnp.where(lane >= 64, hi, 0.0)

    return jnp.concatenate(halves(cur) + halves(prev), axis=0).astype(BF16)


def _merge(s4, use_cur):
    return (jnp.where(use_cur, s4[:, 0:BLK], s4[:, 2 * BLK:3 * BLK]),
            jnp.where(use_cur, s4[:, BLK:2 * BLK], s4[:, 3 * BLK:4 * BLK]))


def _split(a, b, use_cur):
    return jnp.concatenate([jnp.where(use_cur, a, 0.0), jnp.where(use_cur, b, 0.0),
                            jnp.where(use_cur, 0.0, a), jnp.where(use_cur, 0.0, b)], axis=1)


def _softmax_head(s, head, sink, dist, valid):
    sc = jnp.where(valid, s - SLOPES[head] * dist, -jnp.inf)
    m = jnp.maximum(jnp.max(sc, axis=-1, keepdims=True), sink)
    p = jnp.exp(sc - m)
    es = jnp.exp(sink - m)
    inv = 1.0 / (jnp.sum(p, axis=-1, keepdims=True) + es)
    return p * inv, es * inv


def _attn_operands(q_ref, kvc_ref, kvp_ref):
    groups = range(N_HEADS // HEADS_PER_KV)
    kbd = [_block_diag(kvc_ref[:, 0:D_KV], kvp_ref[:, 0:D_KV], g) for g in groups]
    vbd = [_block_diag(kvc_ref[:, D_KV:2 * D_KV], kvp_ref[:, D_KV:2 * D_KV], g) for g in groups]
    qps = [(q_ref[:, j * 128:(j + 1) * 128] * SCALE).astype(BF16) for j in range(N_HEADS // 2)]
    return qps, kbd, vbd


def _attn_forward(q_ref, kvc_ref, kvp_ref, ga_ref, sink_ref, gn_ref, geometry, ya_ref, oa_ref, pr_ref, sp_ref):
    use_cur, dist, valid = geometry
    _, kbd, vbd = operands = _attn_operands(q_ref, kvc_ref, kvp_ref)
    yield
    scores = []
    for j, qp in enumerate(operands[0]):
        scores += _merge(_nt(qp, kbd[j // 4]), use_cur)
    yield
    sinks = [sink_ref[0, h] for h in range(N_HEADS)]
    scores = [jnp.where(valid, s - SLOPES[h] * dist, -jnp.inf) for h, s in enumerate(scores)]
    maxes = [jnp.maximum(jnp.max(s, axis=-1, keepdims=True), sinks[h]) for h, s in enumerate(scores)]
    yield
    exps = [jnp.exp(s - m) for s, m in zip(scores, maxes)]
    sink_exps = [jnp.exp(sinks[h] - m) for h, m in enumerate(maxes)]
    yield
    invs = [1.0 / (jnp.sum(e, axis=-1, keepdims=True) + se) for e, se in zip(exps, sink_exps)]
    probs = [e * inv for e, inv in zip(exps, invs)]
    pr_ref[...] = jnp.concatenate(probs, axis=1).astype(BF16)
    lane = lax.broadcasted_iota(jnp.int32, (BLK, 128), 1)
    sp_ref[...] = sum(jnp.where(lane == h, se * inv, 0.0) for h, (se, inv) in enumerate(zip(sink_exps, invs)))
    yield
    p4s = [_split(probs[2 * j], probs[2 * j + 1], use_cur).astype(BF16) for j in range(N_HEADS // 2)]
    ya = jnp.concatenate([_nn(p4, vbd[j // 4]) for j, p4 in enumerate(p4s)], axis=1)
    ya_ref[...] = ya
    yield
    silu, _ = _silu_and_grad(ga_ref[...])
    oa_ref[...] = (ya * _rstd(ya) * gn_ref[...] * silu).astype(BF16)


def _out_proj_loss(x, oc, oa, wo, norm_final, target, pc, conv_w, norm_conv_out):
    seq = x.shape[0]
    tile = TOK_TILE
    n_tiles = seq // tile

    def body(x_ref, oc_ref, oa_ref, wo_ref, gf_ref, t_ref, pc_ref, hcc_ref, hcu_ref, cw_ref, gn_ref,
             dx2_ref, doa_ref, gwo_ref, dpc_ref, small_ref, loss_acc, zbuf, dbuf):
        step = pl.program_id(0)

        def small_add(row, value):
            small_ref[row:row + 1, :] += jnp.sum(value, axis=0, keepdims=True)

        @pl.when(step == 0)
        def _():
            gwo_ref[...] = jnp.zeros_like(gwo_ref)
            small_ref[...] = jnp.zeros_like(small_ref)
            loss_acc[...] = jnp.zeros_like(loss_acc)
            dbuf[tile:tile + 8, :] = jnp.zeros((8, D_CONV), F32)

        oc, oa = oc_ref[...], oa_ref[...]
        x2 = x_ref[...] + _nn(oc, wo_ref[0:D_CONV, :]) + _nn(oa, wo_ref[D_CONV:D_MIX, :])
        r = _rstd(x2)
        xhat = x2 * r
        err = xhat * gf_ref[...] - t_ref[...]
        loss_acc[...] += jnp.sum(err * err, axis=0, keepdims=True) * (0.5 / D_MODEL)
        dy = err * (1.0 / D_MODEL)
        small_add(SMALL_NORM_FINAL, dy * xhat)
        dx2 = _rms_bwd(dy * gf_ref[...], xhat, r)
        dx2_ref[...] = dx2
        db = dx2.astype(BF16)
        do = _nt(db, wo_ref[0:D_CONV, :])
        doa_ref[...] = _nt(db, wo_ref[D_CONV:D_MIX, :])
        gwo_ref[0:D_CONV, :] += _tn(oc, db)
        gwo_ref[D_CONV:D_MIX, :] += _tn(oa, db)

        is_first_tile = step == n_tiles - 1
        zbuf[0:8, :] = jnp.where(is_first_tile, 0.0, hcc_ref[...] * hcu_ref[...])
        cb, cc, cu, z, z1, z2, conv = _conv_core(pc_ref, zbuf, cw_ref)
        silu, dsilu = _silu_and_grad(pc_ref[:, 3 * D_CONV:4 * D_CONV])
        yc = cb * conv
        rc = _rstd(yc)
        chat = yc * rc
        dn = do * silu
        dpc_ref[:, 3 * D_CONV:4 * D_CONV] = (do * (chat * gn_ref[...]) * dsilu).astype(BF16)
        small_add(SMALL_NORM_CONV, dn * chat)
        dyc = _rms_bwd(dn * gn_ref[...], chat, rc)
        dpc_ref[:, 0:D_CONV] = (dyc * conv).astype(BF16)
        dconv = dyc * cb
        small_add(SMALL_CONV_W, dconv * z2)
        small_add(SMALL_CONV_W + 1, dconv * z1)
        small_add(SMALL_CONV_W + 2, dconv * z)
        dbuf[0:tile, :] = dconv
        dz = cw_ref[2:3, :] * dconv + cw_ref[1:2, :] * dbuf[1:tile + 1, :] + cw_ref[0:1, :] * dbuf[2:tile + 2, :]
        dpc_ref[:, D_CONV:2 * D_CONV] = (dz * cu).astype(BF16)
        dpc_ref[:, 2 * D_CONV:3 * D_CONV] = (dz * cc).astype(BF16)
        dbuf[tile:tile + 8, :] = dbuf[0:8, :]

        @pl.when(step == n_tiles - 1)
        def _():
            lane = lax.broadcasted_iota(jnp.int32, (1, D_MODEL), 1)
            small_ref[SMALL_MISC:SMALL_MISC + 1, :] = jnp.where(lane == SMALL_LOSS_LANE, jnp.sum(loss_acc[...]), 0.0)

    def rev(i):
        return n_tiles - 1 - i

    def row(width):
        return pl.BlockSpec((tile, width), lambda i: (rev(i), 0))

    def halo(col_block):
        return pl.BlockSpec((8, D_CONV), lambda i: (jnp.maximum(rev(i) * (tile // 8) - 1, 0), col_block))

    def const(shape):
        return pl.BlockSpec(shape, lambda i: (0, 0))

    return pl.pallas_call(
        body, name="out_proj_loss", grid=(n_tiles,),
        out_shape=(jax.ShapeDtypeStruct((seq, D_MODEL), F32), jax.ShapeDtypeStruct((seq, D_ATTN), F32),
                   jax.ShapeDtypeStruct((D_MIX, D_MODEL), F32), jax.ShapeDtypeStruct((seq, D_PC), BF16),
                   jax.ShapeDtypeStruct((SMALL_ROWS, D_MODEL), F32)),
        in_specs=[row(D_MODEL), row(D_CONV), row(D_ATTN), _resident(wo.shape), _resident((1, D_MODEL)), row(D_MODEL),
                  row(D_PC), halo(1), halo(2), _resident(conv_w.shape), _resident((1, D_CONV))],
        out_specs=(row(D_MODEL), row(D_ATTN), const((D_MIX, D_MODEL)), row(D_PC), const((SMALL_ROWS, D_MODEL))),
        scratch_shapes=[pltpu.VMEM((1, D_MODEL), F32), pltpu.VMEM((tile + 8, D_CONV), F32),
                        pltpu.VMEM((tile + 8, D_CONV), F32)],
        compiler_params=_params(62, ("arbitrary",)),
    )(x, oc, oa, wo, norm_final, target, pc, pc, pc, conv_w, norm_conv_out)


def _attn_bwd(q, kv, ga, ya, doa, probs, sink_probs, norm_attn_out, gwo, small):
    seq = q.shape[0]
    per_step = 2
    n_steps = seq // (per_step * BLK)
    stage_steps = (0, 1, n_steps // 4, (5 * n_steps) // 8, n_steps - 1, n_steps - 1)

    def body(q_ref, kvc_ref, kvp_ref, ga_ref, ya_ref, doa_ref, pr_ref, sp_ref, gn_ref, gwo_ref, small_ref,
             dqg_ref, small_out, gwo_sh, gna_ref, gs_ref, carry, dya_buf, *rs_scratch):
        step = pl.program_id(0)
        rs_stages = _rs_wout_stages(gwo_ref, gwo_sh, *rs_scratch)
        for at, stage in zip(stage_steps[:-1], rs_stages[:-1]):
            pl.when(step == at)(stage)

        @pl.when(step == 0)
        def _():
            gna_ref[...] = jnp.zeros_like(gna_ref)
            gs_ref[...] = jnp.zeros_like(gs_ref)
            carry[...] = jnp.zeros_like(carry)

        lane = lax.broadcasted_iota(jnp.int32, (BLK, 128), 1)

        def fold(bd):
            return (jnp.where(lane < 64, bd[0:BLK], 0.0) + jnp.where(lane >= 64, bd[BLK:2 * BLK], 0.0),
                    jnp.where(lane < 64, bd[2 * BLK:3 * BLK], 0.0) + jnp.where(lane >= 64, bd[3 * BLK:4 * BLK], 0.0))

        def one_block(b):
            rows = slice(b * BLK, (b + 1) * BLK)
            kv_prev = kvp_ref if b == 0 else kvc_ref.at[(b - 1) * BLK:b * BLK, :]
            ya = ya_ref[rows, :]
            r = _rstd(ya)
            xhat = ya * r
            silu, dsilu = _silu_and_grad(ga_ref[rows, :])
            do = doa_ref[rows, :]
            dn = do * silu
            dqg_ref[rows, D_ATTN:2 * D_ATTN] = (do * (xhat * gn_ref[...]) * dsilu).astype(BF16)
            gna_ref[...] += jnp.sum(dn * xhat, axis=0, keepdims=True)
            dya_buf[rows, :] = _rms_bwd(dn * gn_ref[...], xhat, r).astype(BF16)

            use_cur = _band_geometry(per_step * (n_steps - 1 - step) + b)[0]
            pairs = range(N_HEADS // 2)
            qps, kbd, vbd = _attn_operands(q_ref.at[rows, :], kvc_ref.at[rows, :], kv_prev)
            probs = [pr_ref[rows, h * BLK:(h + 1) * BLK].astype(F32) for h in range(N_HEADS)]
            dyps = [dya_buf[rows, j * 128:(j + 1) * 128] for j in pairs]
            dps = []
            for j in pairs:
                dps += _merge(_nt(dyps[j], vbd[j // 4]), use_cur)
            deltas = [jnp.sum(p * dp, axis=-1, keepdims=True) for p, dp in zip(probs, dps)]
            dss = [p * (dp - delta) for p, dp, delta in zip(probs, dps, deltas)]
            delta_lanes = sum(jnp.where(lane == h, deltas[h], 0.0) for h in range(N_HEADS))
            gs_ref[...] -= jnp.sum(sp_ref[rows, :] * delta_lanes, axis=0, keepdims=True)
            ds4s = [_split(dss[2 * j], dss[2 * j + 1], use_cur).astype(BF16) for j in pairs]
            p4s = [_split(probs[2 * j], probs[2 * j + 1], use_cur).astype(BF16) for j in pairs]
            dqg_ref[rows, 0:D_ATTN] = jnp.concatenate([_nn(ds4s[j], kbd[j // 4]) * SCALE for j in pairs], axis=1).astype(BF16)
            sums = []
            for group in range(N_HEADS // HEADS_PER_KV):
                acc = [jnp.zeros((BLK, 128), F32) for _ in range(4)]
                for j in range(group * 4, group * 4 + 4):
                    for slot, part in enumerate(fold(_tn(ds4s[j], qps[j])) + fold(_tn(p4s[j], dyps[j]))):
                        acc[slot] = acc[slot] + part
                sums.append([a + pltpu.roll(a, 64, 1) for a in acc])
            dk_cur, dk_prev, dv_cur, dv_prev = (jnp.where(lane < 64, lo, hi) for lo, hi in zip(sums[0], sums[1]))
            dqg_ref[rows, 2 * D_ATTN:2 * D_ATTN + 2 * D_KV] = (jnp.concatenate([dk_cur, dv_cur], axis=1) + carry[...]).astype(BF16)
            carry[...] = jnp.concatenate([dk_prev, dv_prev], axis=1)

        for b in reversed(range(per_step)):
            one_block(b)

        @pl.when(step == n_steps - 1)
        def _():
            small_out[...] = small_ref[...]
            small_out[SMALL_NORM_ATTN:SMALL_NORM_ATTN + 1, :] = gna_ref[...]
            small_out[SMALL_MISC:SMALL_MISC + 1, 0:128] = small_ref[SMALL_MISC:SMALL_MISC + 1, 0:128] + gs_ref[...]

        pl.when(step == stage_steps[-1])(rs_stages[-1])

    def rows(width):
        return pl.BlockSpec((per_step * BLK, width), lambda i: (n_steps - 1 - i, 0))

    kv_prev = pl.BlockSpec((BLK, 2 * D_KV), lambda i: (jnp.maximum(per_step * (n_steps - 1 - i) - 1, 0), 0))
    any_spec = pl.BlockSpec(memory_space=pl.ANY)
    return pl.pallas_call(
        body, name="attn_bwd", grid=(n_steps,),
        out_shape=(jax.ShapeDtypeStruct((seq, D_QG), BF16), pltpu.HBM(small.shape, F32),
                   pltpu.HBM((gwo.shape[0] // N_CHIPS, gwo.shape[1]), F32)),
        in_specs=[rows(D_ATTN), rows(2 * D_KV), kv_prev, rows(D_ATTN), rows(D_ATTN), rows(D_ATTN),
                  rows(N_HEADS * BLK), rows(128), _resident((1, D_ATTN)), any_spec, _resident(small.shape)],
        out_specs=(rows(D_QG), pl.BlockSpec(small.shape, lambda i: (0, 0)), any_spec),
        scratch_shapes=[pltpu.VMEM((1, D_ATTN), F32), pltpu.VMEM((1, 128), F32),
                        pltpu.VMEM((BLK, 2 * D_KV), F32), pltpu.VMEM((per_step * BLK, D_ATTN), BF16)] + _rs_wout_scratch(gwo.shape),
        compiler_params=_params(44, ("arbitrary",)),
    )(q, kv, kv, ga, ya, doa, probs, sink_probs, norm_attn_out, gwo, small)


GBLK = 256
GSUB = 64
PAIR_RING = 4
LAG_PAIR, LAG_HOP1, LAG_HOP2 = 1, 7, 13


def _bwd_in(dpc, dqg, h, wt, x, norm_in, dx2, small):
    seq = x.shape[0]
    n_blk = D_IN_PROJ // GBLK
    per_chip = n_blk // N_CHIPS
    n_slots = (n_blk + 1) // 2
    n_sub = GBLK // GSUB
    chip_rows = D_IN_PROJ // N_CHIPS
    tile = TOK_TILE
    n_tiles = seq // tile
    n_steps = n_blk + max(n_tiles, LAG_HOP2 + 1)
    chunk = min(seq, 512)
    blk_q, blk_kv, blk_ga = ROW_Q // GBLK, ROW_KV // GBLK, ROW_GA // GBLK

    def block_of(i):
        k = i % N_CHIPS
        robin = per_chip * ((k % 2) * 2 + k // 2) + i // N_CHIPS
        if isinstance(i, int):
            return robin if i < per_chip * N_CHIPS else i
        return jnp.where(i < per_chip * N_CHIPS, robin, i)

    def owner_of(i):
        return (i // N_CHIPS) % 2

    def slot_of(i):
        return (i // (2 * N_CHIPS)) * N_CHIPS + i % N_CHIPS

    def body(dpc_ref, dqg_ref, wt_ref, h_ref, x_ref, g_ref, dx2_ref, small_ref, gx_ref, small_sum, gwt_sh,
             dh_acc, gni, keep, pbuf, xbuf, land, land2, small_land,
             pair_send, pair_recv, h1_send, h1_recv, h2_send, h2_recv, sw_send, sw_recv, sm_send, sm_recv, out_sem):
        step = pl.program_id(0)
        x_i, y_i, c = lax.axis_index("x"), lax.axis_index("y"), lax.axis_index("c")
        me = 4 * x_i + 2 * y_i + c
        j = 2 * x_i + y_i
        pa = (_xor(x_i, 1 - c), _xor(y_i, c), c)
        pb = (_xor(x_i, c), _xor(y_i, 1 - c), c)
        sib = (x_i, y_i, 1 - c)
        ja = 2 * pa[0] + pa[1]
        jb = 2 * pb[0] + pb[1]
        jd = 3 - j

        def remote(src, dst, send, recv, to):
            return pltpu.make_async_remote_copy(src_ref=src, dst_ref=dst, send_sem=send, recv_sem=recv,
                                                device_id=to, device_id_type=MESH)

        def piece(ref, slot, u, n):
            return ref.at[slot, pl.ds(u * GSUB, n * GSUB), :]

        def chip_rows_at(ref, local, n):
            return ref.at[pl.ds(pl.multiple_of(local, GSUB), n * GSUB), :]

        def pair_copy(i):
            slot = slot_of(i)
            return remote(pbuf.at[i % PAIR_RING], land.at[slot], pair_send.at[slot], pair_recv.at[slot], sib)

        def h1_copy(slot, u, n):
            k = slot * n_sub + u
            return remote(piece(xbuf, slot, u, n), piece(xbuf, slot, u, n), h1_send.at[k], h1_recv.at[k], pa)

        def h2_copy(slot, u, n, local):
            k = slot * n_sub + u
            return remote(piece(xbuf, slot, u, n), chip_rows_at(land2, local, n), h2_send.at[k], h2_recv.at[k], pb)

        def sw_copy(slot, u, n, local):
            k = slot * n_sub + u
            return remote(piece(keep, slot, u, n), chip_rows_at(gwt_sh, local, n), sw_send.at[k], sw_recv.at[k], sib)

        def own_copy(slot, u, n, local):
            return pltpu.make_async_copy(piece(keep, slot, u, n), chip_rows_at(gwt_sh, local, n), out_sem.at[slot * n_sub + u])

        def owned(i):
            return (i >= 0) & (i < n_blk) & (owner_of(i) == c)

        def chip_of(blk, u):
            row = blk * GBLK + u * GSUB
            chip = row // chip_rows
            return chip, row - chip * chip_rows

        def pieces(blk):
            first, local = chip_of(blk, 0)
            whole = first == chip_of(blk, n_sub - 1)[0]
            if isinstance(blk, int):
                return [(True, 0, n_sub, first, local)] if whole else [(True, u, 1) + chip_of(blk, u) for u in range(n_sub)]
            return [(whole, 0, n_sub, first, local)] + [(jnp.logical_not(whole), u, 1) + chip_of(blk, u) for u in range(n_sub)]

        def for_pieces(blk, serve):
            ways = pieces(blk)
            serve(*ways[0])

            @pl.when(ways[1][0])
            def _():
                for way in ways[1:]:
                    serve(True, *way[1:])

        @pl.when(step == 0)
        def _():
            dh_acc[...] = jnp.zeros_like(dh_acc)
            gni[...] = jnp.zeros_like(gni)

        @pl.when(step < n_blk)
        def _():
            from_pc = block_of(step) < blk_q
            block = _tn(jnp.where(from_pc, dpc_ref[...], dqg_ref[...]), h_ref[...])
            for t in range(0, seq, chunk):
                d = jnp.where(from_pc, dpc_ref[t:t + chunk, :], dqg_ref[t:t + chunk, :])
                dh_acc[t:t + chunk, :] += _nn(d, wt_ref[...])

            @pl.when(owner_of(step) == c)
            def _():
                keep[slot_of(step)] = block

            @pl.when(owner_of(step) != c)
            def _():
                @pl.when(step >= 2 * PAIR_RING)
                def _():
                    pair_copy(step - 2 * PAIR_RING).wait_send()
                pbuf[step % PAIR_RING] = block.astype(BF16)
                pair_copy(step).start()

        i1 = step - LAG_PAIR

        @pl.when(owned(i1))
        def _():
            slot = slot_of(i1)
            pair_copy(i1).wait_recv()
            _accumulate(keep.at[slot], land.at[slot])
            def first_hop_out(cond, u, n, chip, _):
                @pl.when(cond & ((chip == ja) | (chip == jd)))
                def _():
                    _cast_rows(piece(keep, slot, u, n), piece(xbuf, slot, u, n))
                    h1_copy(slot, u, n).start()

            for_pieces(block_of(i1), first_hop_out)

        i2 = step - LAG_HOP1

        @pl.when(owned(i2))
        def _():
            slot = slot_of(i2)
            def first_hop_in(cond, u, n, chip, local):
                @pl.when(cond & ((chip == j) | (chip == jb)))
                def _():
                    h1_copy(slot, u, n).wait_recv()
                    _accumulate(piece(keep, slot, u, n), piece(xbuf, slot, u, n))

                    @pl.when(chip == jb)
                    def _():
                        _cast_rows(piece(keep, slot, u, n), piece(xbuf, slot, u, n))
                        h2_copy(slot, u, n, local).start()

                @pl.when(cond & ((chip == ja) | (chip == jd)))
                def _():
                    h1_copy(slot, u, n).wait_send()

            for_pieces(block_of(i2), first_hop_in)

        i3 = step - LAG_HOP2

        @pl.when(owned(i3))
        def _():
            slot = slot_of(i3)
            def second_hop_in(cond, u, n, chip, local):
                @pl.when(cond & (chip == j))
                def _():
                    h2_copy(slot, u, n, local).wait_recv()
                    _accumulate(piece(keep, slot, u, n), chip_rows_at(land2, local, n))
                    own_copy(slot, u, n, local).start()
                    sw_copy(slot, u, n, local).start()

                @pl.when(cond & (chip == jb))
                def _():
                    h2_copy(slot, u, n, local).wait_send()

            for_pieces(block_of(i3), second_hop_in)

        e = step - n_blk

        @pl.when((e >= 0) & (e < n_tiles))
        def _():
            dh = dh_acc[pl.ds(pl.multiple_of(e * tile, tile), tile), :]
            xv = x_ref[...]
            r = _rstd(xv)
            xhat = xv * r
            gni[...] += jnp.sum(dh * xhat, axis=0, keepdims=True)
            gx_ref[...] = _rms_bwd(dh * g_ref[...], xhat, r) + dx2_ref[...]

        others = [(dx, dy, dc) for dx in (0, 1) for dy in (0, 1) for dc in (0, 1)][1:]
        sends = [remote(small_land.at[me], small_land.at[me], sm_send.at[k], sm_recv.at[k],
                        (_xor(x_i, dx), _xor(y_i, dy), _xor(c, dc))) for k, (dx, dy, dc) in enumerate(others)]

        @pl.when(step == min(n_blk + n_tiles, n_steps - 1))
        def _():
            small_land[me] = small_ref[...]
            small_land[me, SMALL_NORM_IN:SMALL_NORM_IN + 1, :] = gni[...]
            for cp in sends:
                cp.start()

        @pl.when(step == n_steps - 1)
        def _():
            for i in range(n_blk):
                if i + 2 * PAIR_RING >= n_blk:
                    @pl.when(owner_of(i) != c)
                    def _(i=i):
                        pair_copy(i).wait_send()
                for _, u, n, chip, local in pieces(block_of(i)):
                    @pl.when((j == chip) & (c == owner_of(i)))
                    def _(i=i, u=u, n=n, local=local):
                        sw_copy(slot_of(i), u, n, local).wait_send()
                        own_copy(slot_of(i), u, n, local).wait()

                    @pl.when((j == chip) & (c != owner_of(i)))
                    def _(i=i, u=u, n=n, local=local):
                        sw_copy(slot_of(i), u, n, local).wait_recv()
            for cp in sends:
                cp.wait_recv()
            total = small_land[0]
            for dev in range(1, 8):
                total = total + small_land[dev]
            small_sum[...] = total
            for cp in sends:
                cp.wait_send()

    def blk_at(i):
        return block_of(jnp.clip(i, 0, n_blk - 1))

    last_pc_step = max(i for i in range(n_blk) if block_of(i) < blk_q)

    def next_block(i, in_pc):
        i = jnp.clip(i, 0, n_blk - 1)
        step = jnp.full_like(i, last_pc_step if in_pc else n_blk - 1)
        for ahead in reversed(range(N_CHIPS)):
            cand = jnp.minimum(i + ahead, n_blk - 1)
            step = jnp.where((block_of(cand) < blk_q) == in_pc, cand, step)
        return block_of(step)

    def dqg_block(i):
        b = next_block(i, False)
        q_blk = jnp.clip(b - blk_q, 0, blk_kv - blk_q - 1)
        ga_blk = (D_ATTN // GBLK) + jnp.clip(b - blk_ga, 0, n_blk - blk_ga - 1)
        return jnp.where(b < blk_kv, q_blk, jnp.where(b == blk_kv, 2 * D_ATTN // GBLK, ga_blk))

    def tok(i):
        return (jnp.clip(i - n_blk, 0, n_tiles - 1), 0)

    n_piece = n_slots * n_sub
    dma = pltpu.SemaphoreType.DMA
    return pl.pallas_call(
        body, name="bwd_in", grid=(n_steps,),
        out_shape=(jax.ShapeDtypeStruct((seq, D_MODEL), F32), jax.ShapeDtypeStruct(small.shape, F32),
                   jax.ShapeDtypeStruct((chip_rows, D_MODEL), F32)),
        in_specs=[pl.BlockSpec((seq, GBLK), lambda i: (0, next_block(i, True))),
                  pl.BlockSpec((seq, GBLK), lambda i: (0, dqg_block(i))),
                  pl.BlockSpec((GBLK, D_MODEL), lambda i: (blk_at(i), 0)),
                  _resident(h.shape),
                  pl.BlockSpec((tile, D_MODEL), tok), _resident((1, D_MODEL)), pl.BlockSpec((tile, D_MODEL), tok),
                  _resident(small.shape)],
        out_specs=(pl.BlockSpec((tile, D_MODEL), tok), pl.BlockSpec(small.shape, lambda i: (0, 0)),
                   pl.BlockSpec(memory_space=pl.ANY)),
        scratch_shapes=[pltpu.VMEM((seq, D_MODEL), F32), pltpu.VMEM((1, D_MODEL), F32),
                        pltpu.VMEM((n_slots, GBLK, D_MODEL), F32), pltpu.VMEM((PAIR_RING, GBLK, D_MODEL), BF16),
                        pltpu.VMEM((n_slots, GBLK, D_MODEL), BF16), pltpu.VMEM((n_slots, GBLK, D_MODEL), BF16),
                        pltpu.VMEM((chip_rows, D_MODEL), BF16), pltpu.VMEM((8,) + small.shape, F32),
                        dma((n_slots,)), dma((n_slots,)), dma((n_piece,)), dma((n_piece,)), dma((n_piece,)),
                        dma((n_piece,)), dma((n_piece,)), dma((n_piece,)), dma((7,)), dma((7,)), dma((n_piece,))],
        compiler_params=_params(62, ("arbitrary",)),
    )(dpc, dqg, wt, h, x, norm_in, dx2, small)


def _accumulate(dst_ref, src_ref, rows=16):
    def step(i, carry):
        sl = pl.ds(pl.multiple_of(i * rows, rows), rows)
        dst_ref[sl, :] = dst_ref[sl, :] + src_ref[sl, :].astype(F32)
        return carry
    lax.fori_loop(0, dst_ref.shape[0] // rows, step, 0)


def _rs_wout_scratch(gwo_shape):
    o_half, width = gwo_shape[0] // N_CHIPS // 2, gwo_shape[1]
    return [pltpu.VMEM((4, o_half, width), F32), pltpu.VMEM((4, o_half, width), BF16),
            pltpu.VMEM((4, o_half, width), BF16), pltpu.VMEM((2, o_half, width), BF16),
            pltpu.VMEM((o_half, width), BF16),
            pltpu.SemaphoreType.DMA((8,)), pltpu.SemaphoreType.DMA((8,)), pltpu.SemaphoreType.DMA((4,))]


def _rs_wout_stages(gwo_ref, gwo_sh, acc_o, sb_o, r1o, r2o, r3o, send_sems, recv_sems, local_sems):
    o_rows = gwo_ref.shape[0] // N_CHIPS
    o_half = o_rows // 2
    x, y, c = lax.axis_index("x"), lax.axis_index("y"), lax.axis_index("c")
    j = 2 * x + y
    pa = (_xor(x, 1 - c), _xor(y, c), c)
    pb = (_xor(x, c), _xor(y, 1 - c), c)
    sib = (x, y, 1 - c)
    ja = 2 * pa[0] + pa[1]
    jb = 2 * pb[0] + pb[1]
    jd = 3 - j
    order = (ja, jd, jb, j)
    sib_order = (jb, jd, ja, j)

    def rcopy(k, src, dst, to):
        return pltpu.make_async_remote_copy(src_ref=src, dst_ref=dst, send_sem=send_sems.at[k],
                                            recv_sem=recv_sems.at[k], device_id=to, device_id_type=MESH)

    def o_rows_of(chip, half):
        return gwo_ref.at[pl.ds(pl.multiple_of(chip * o_rows + half * o_half, 8), o_half), :]

    def loads(chips, half):
        return [pltpu.make_async_copy(o_rows_of(chip, half), acc_o.at[s], local_sems.at[s]) for s, chip in enumerate(chips)]

    def pair_send(s):
        return rcopy(s, sb_o.at[s], r1o.at[s], sib)

    def out_half(half):
        return gwo_sh.at[pl.ds(pl.multiple_of(half * o_half, 8), o_half), :]

    hop1 = [rcopy(4 + s, sb_o.at[s], r2o.at[s], pa) for s in range(2)]
    hop2 = rcopy(6, sb_o.at[2], r3o, pb)
    swap = rcopy(7, acc_o.at[3], out_half(c), sib)
    mine = pltpu.make_async_copy(acc_o.at[3], out_half(c), local_sems.at[0])

    def resend(s, copy):
        pair_send(s).wait_send()
        _cast_rows(acc_o.at[s], sb_o.at[s])
        copy.start()

    def stage_load():
        for cp in loads(sib_order, 1 - c):
            cp.start()

    def stage_pair():
        for s, (cp, mine_in) in enumerate(zip(loads(sib_order, 1 - c), loads(order, c))):
            cp.wait()
            _cast_rows(acc_o.at[s], sb_o.at[s])
            pair_send(s).start()
            mine_in.start()

    def stage_hop1():
        for s, cp in enumerate(loads(order, c)):
            cp.wait()
            pair_send(s).wait_recv()
            _accumulate(acc_o.at[s], r1o.at[s])
            if s < 2:
                resend(s, hop1[s])

    def stage_hop2():
        hop1[1].wait_recv()
        _accumulate(acc_o.at[2], r2o.at[1])
        resend(2, hop2)
        hop1[0].wait_recv()
        _accumulate(acc_o.at[3], r2o.at[0])

    def stage_final():
        hop2.wait_recv()
        _accumulate(acc_o.at[3], r3o)
        mine.start()
        swap.start()

    def stage_drain():
        rcopy(7, acc_o.at[3], out_half(1 - c), sib).wait_recv()
        for cp in [pair_send(3)] + hop1 + [hop2, swap]:
            cp.wait_send()
        mine.wait()

    return stage_load, stage_pair, stage_hop1, stage_hop2, stage_final, stage_drain


def _adamw_big(groups, passed):
    arrays = [a for group in groups for a in group[:4]]
    steps = [group[0].shape[0] // group[4] for group in groups]
    first = [sum(steps[:k]) for k in range(len(steps) + 1)]
    n = len(arrays)

    def body(*refs):
        ins, passed_in, outs, passed_out = refs[:n], refs[n], refs[n + 1:2 * n + 1], refs[2 * n + 1]
        i = pl.program_id(0)
        for k in range(len(groups)):
            @pl.when((i >= first[k]) & (i < first[k + 1]))
            def _(k=k):
                w_ref, g_ref, m_ref, v_ref = ins[4 * k:4 * k + 4]
                _adamw_update(w_ref, g_ref[...], m_ref, v_ref, *outs[4 * k:4 * k + 4])

        @pl.when(i < steps[0])
        def _():
            passed_out[...] = passed_in[...]

    def spec(k, rows, width):
        return pl.BlockSpec((rows, width), lambda i: (jnp.clip(i - first[k], 0, steps[k] - 1), 0))

    specs = [spec(k, group[4], group[0].shape[1]) for k, group in enumerate(groups) for _ in range(4)]
    specs.append(spec(0, passed.shape[0] // steps[0], passed.shape[1]))
    out = pl.pallas_call(
        body, name="adamw_big", grid=(first[-1],),
        out_shape=tuple(jax.ShapeDtypeStruct(a.shape, a.dtype) for a in arrays + [passed]),
        in_specs=specs, out_specs=tuple(specs),
        compiler_params=_params(40, ("arbitrary",)),
    )(*arrays, passed)
    return [tuple(out[4 * k:4 * k + 4]) for k in range(len(groups))], out[-1]


def _adamw_update(w_ref, gv, m_ref, v_ref, go_ref, d_ref, nm_ref, nv_ref, at=...):
    go_ref[at] = gv
    nm = ADAM_B1 * m_ref[at] + (1.0 - ADAM_B1) * gv
    nv = ADAM_B2 * v_ref[at] + (1.0 - ADAM_B2) * (gv * gv)
    m_hat = nm / (1.0 - ADAM_B1 ** ADAM_STEP)
    v_hat = nv / (1.0 - ADAM_B2 ** ADAM_STEP)
    d_ref[at] = -ADAM_LR * (m_hat / (jnp.sqrt(v_hat) + ADAM_EPS) + ADAM_WD * w_ref[at])
    nm_ref[at] = nm
    nv_ref[at] = nv


def _adamw_small(chip, small_sum, weights, grads_of, ms, vs):
    n = len(weights)

    def body(chip_ref, small_ref, *refs):
        ins, outs, loss_ref = refs[:3 * n], refs[3 * n:-1], refs[-1]
        for k in range(n):
            w_ref, m_ref, v_ref = ins[3 * k:3 * k + 3]
            for at, gv in grads_of[k](small_ref, chip_ref):
                _adamw_update(w_ref, gv, m_ref, v_ref, *outs[4 * k:4 * k + 4], at=at)
        loss_ref[...] = small_ref[SMALL_MISC:SMALL_MISC + 1, SMALL_LOSS_LANE:SMALL_LOSS_LANE + 1]

    flat = [a for group in zip(weights, ms, vs) for a in group]
    vmem = pl.BlockSpec(memory_space=pltpu.VMEM)
    out = pl.pallas_call(
        body, name="adamw_small",
        out_shape=tuple(jax.ShapeDtypeStruct(w.shape, F32) for w in weights for _ in range(4))
        + (jax.ShapeDtypeStruct((1, 1), F32),),
        in_specs=[pl.BlockSpec(memory_space=pltpu.SMEM)] + [vmem] * (1 + 3 * n), out_specs=(vmem,) * (4 * n + 1),
    )(chip, small_sum, *flat)
    return [tuple(out[4 * k:4 * k + 4]) for k in range(n)], out[-1][0, 0]


def kernel(x, norm_in, w_in, conv_w, attn_sinks, norm_conv_out, norm_attn_out, w_out, norm_final, loss_target, m_norm_in, m_w_in, m_conv_w, m_attn_sinks, m_norm_conv_out, m_norm_attn_out, m_w_out, m_norm_final, v_norm_in, v_w_in, v_conv_w, v_attn_sinks, v_norm_conv_out, v_norm_attn_out, v_w_out, v_norm_final):
    chip = 2 * lax.axis_index("x") + lax.axis_index("y")
    xs, target = x[0], loss_target[0]
    norm_final2 = norm_final.reshape(1, D_MODEL)
    w_in_t, m_w_in_t, v_w_in_t = w_in[0].T, m_w_in[0].T, v_w_in[0].T

    def from_hbm(*arrays):
        return tuple(pltpu.with_memory_space_constraint(a, pltpu.HBM) for a in arrays)

    h, pc, q, kv, ga, oc, ya, oa, probs, sink_probs, wt, cw, wo = _fwd_in(
        xs, norm_in, w_in_t, w_out[0], conv_w.transpose(1, 0, 2), norm_conv_out, attn_sinks, norm_attn_out)
    dx2, doa, gwo, dpc, small = _out_proj_loss(xs, oc, oa, wo, norm_final2, target, pc, cw, norm_conv_out)
    q, kv, ga, ya, doa, probs, gwo, small = from_hbm(q, kv, ga, ya, doa, probs, gwo, small)
    dqg, small, gwo_sh = _attn_bwd(q, kv, ga, ya, doa, probs, sink_probs, norm_attn_out, gwo, small)
    grad_x, small_sum, gwt_sh = _bwd_in(dpc, dqg, h, wt, xs, norm_in, dx2, small)

    (up_w_in, up_w_out), grad_x = _adamw_big(
        [from_hbm(w_in_t, gwt_sh, m_w_in_t, v_w_in_t) + (200,), from_hbm(w_out[0], gwo_sh, m_w_out[0], v_w_out[0]) + (128,)],
        *from_hbm(grad_x))
    up_w_in = tuple(o.T[None] for o in up_w_in)
    up_w_out = tuple(o[None] for o in up_w_out)

    def row_of(r):
        return lambda small_ref, chip_ref: [(..., small_ref[r:r + 1, :])]

    def sink_lanes(small_ref, chip_ref):
        return [(..., small_ref[SMALL_MISC:SMALL_MISC + 1, 0:N_HEADS])]

    def conv_taps(small_ref, chip_ref):
        width = D_CONV // N_CHIPS
        cols = pl.ds(pl.multiple_of(chip_ref[0] * width, width), width)
        return [(k, small_ref[pl.ds(SMALL_CONV_W + k, 1), cols]) for k in range(conv_w.shape[1])]

    def small_view(a):
        return a.transpose(1, 0, 2) if a.ndim == 3 else a.reshape(-1, a.shape[-1])

    small_w = (norm_in, conv_w, attn_sinks, norm_conv_out, norm_attn_out, norm_final)
    small_m = (m_norm_in, m_conv_w, m_attn_sinks, m_norm_conv_out, m_norm_attn_out, m_norm_final)
    small_v = (v_norm_in, v_conv_w, v_attn_sinks, v_norm_conv_out, v_norm_attn_out, v_norm_final)
    small_g = (row_of(SMALL_NORM_IN), conv_taps, sink_lanes, row_of(SMALL_NORM_CONV), row_of(SMALL_NORM_ATTN),
               row_of(SMALL_NORM_FINAL))
    w_views, m_views, v_views = (tuple(small_view(a) for a in group) for group in (small_w, small_m, small_v))
    up_small, loss = _adamw_small(chip.astype(jnp.int32).reshape(1), small_sum, w_views, small_g, m_views, v_views)
    up_small = [tuple(o.transpose(1, 0, 2) if w.ndim == 3 else o.reshape(w.shape) for o in up)
                for up, w in zip(up_small, small_w)]
    up_norm_in, up_conv_w, up_sinks, up_norm_conv, up_norm_attn, up_norm_final = up_small
    updates = (up_norm_in, up_w_in, up_conv_w, up_sinks, up_norm_conv, up_norm_attn, up_w_out, up_norm_final)
    grads_out, deltas, new_m, new_v = zip(*updates)
    return (loss, grad_x[None], *grads_out, *deltas, *new_m, *new_v)
```

```python
import jax
import jax.numpy as jnp
from jax import lax
from jax.experimental import pallas as pl
from jax.experimental.pallas import tpu as pltpu

F32 = jnp.float32
BF16 = jnp.bfloat16
MESH = pl.DeviceIdType.MESH
SIDE = 1

D_MODEL = 1024
D_CONV = 1024
D_ATTN = 1024
D_KV = 128
D_QG = 2 * D_ATTN + 2 * D_KV
D_MIX = D_CONV + D_ATTN
D_PC = 4 * D_CONV
D_IN_PROJ = D_PC + 2 * D_ATTN + 2 * D_KV
ROW_Q = D_PC
ROW_KV = ROW_Q + D_ATTN
ROW_GA = ROW_KV + 2 * D_KV
N_HEADS = 16
HEAD_DIM = 64
HEADS_PER_KV = 8
BLK = 128
N_CHIPS = 4
RMS_EPS = 1e-5
SCALE = HEAD_DIM ** -0.5
SLOPES = tuple(2.0 ** (-8.0 * (h + 1) / N_HEADS) for h in range(N_HEADS))

ADAM_LR, ADAM_B1, ADAM_B2, ADAM_EPS, ADAM_WD, ADAM_STEP = 0.001, 0.9, 0.999, 1e-08, 0.01, 10

SMALL_ROWS = 8
SMALL_NORM_IN, SMALL_NORM_CONV, SMALL_NORM_ATTN, SMALL_NORM_FINAL, SMALL_CONV_W, SMALL_MISC = 0, 1, 2, 3, 4, 7
SMALL_LOSS_LANE = N_HEADS

TOK_TILE = 256
PC_PIECE = 512
MIB = 1 << 20


def _params(vmem_mib, semantics=None):
    return pltpu.CompilerParams(dimension_semantics=semantics, vmem_limit_bytes=vmem_mib * MIB)


def _nn(a, b):
    return jnp.dot(a, b, preferred_element_type=F32)


def _nt(a, b):
    return lax.dot_general(a, b, (((1,), (1,)), ((), ())), preferred_element_type=F32)


def _tn(a, b):
    return lax.dot_general(a, b, (((0,), (0,)), ((), ())), preferred_element_type=F32)


def _rstd(v):
    return lax.rsqrt(jnp.mean(v * v, axis=-1, keepdims=True) + RMS_EPS)


def _rms_bwd(g, xhat, rstd):
    return rstd * (g - xhat * jnp.mean(g * xhat, axis=-1, keepdims=True))


def _silu_and_grad(g):
    s = jax.nn.sigmoid(g)
    return g * s, s * (1.0 + g * (1.0 - s))


def _resident(shape):
    return pl.BlockSpec(shape, lambda *_: (0,) * len(shape), pipeline_mode=pl.Buffered(1))


def _xor(a, b):
    return a + b - 2 * a * b


def _cast_rows(src_ref, dst_ref, rows=32):
    def step(i, carry):
        sl = pl.ds(pl.multiple_of(i * rows, rows), rows)
        dst_ref[sl, :] = src_ref[sl, :].astype(dst_ref.dtype)
        return carry
    lax.fori_loop(0, src_ref.shape[0] // rows, step, 0)


def _ag_scratch(shard_shape):
    rows, width = shard_shape
    return [pltpu.VMEM((rows, width), F32), pltpu.VMEM((rows, width), BF16), pltpu.VMEM((3, rows // 2, width), BF16),
            pltpu.SemaphoreType.DMA((6,)), pltpu.SemaphoreType.DMA((6,)), pltpu.SemaphoreType.DMA((4,))]


def _ag_stages(sh_ref, out, f32_buf, own, land, send_sems, recv_sems, local_sems):
    rows = sh_ref.shape[0]
    half = rows // 2
    x, y, c = lax.axis_index("x"), lax.axis_index("y"), lax.axis_index("c")
    j = 2 * x + y
    p1 = (_xor(x, c), _xor(y, 1 - c), c)
    p2 = (_xor(x, 1 - c), _xor(y, c), c)
    sib = (x, y, 1 - c)
    j1 = 2 * p1[0] + p1[1]
    j2 = 2 * p2[0] + p2[1]
    j3 = 3 - j

    def rows_of(chip, hf):
        return out.at[pl.ds(pl.multiple_of(chip * rows + hf * half, 16), half), :]

    def rcopy(k, src, dst, to):
        return pltpu.make_async_remote_copy(src_ref=src, dst_ref=dst, send_sem=send_sems.at[k],
                                            recv_sem=recv_sems.at[k], device_id=to, device_id_type=MESH)

    my_half = own.at[pl.ds(pl.multiple_of(c * half, 16), half), :]
    hop1 = rcopy(0, my_half, land.at[0], p1)
    hop2_own = rcopy(1, my_half, land.at[1], p2)
    hop2_fwd = rcopy(2, land.at[0], land.at[2], p2)
    swaps = [rcopy(3 + s, land.at[s], rows_of(chip, c), sib) for s, chip in enumerate((j1, j2, j3))]
    keeps = [pltpu.make_async_copy(land.at[s], rows_of(chip, c), local_sems.at[1 + s]) for s, chip in enumerate((j1, j2, j3))]
    load = pltpu.make_async_copy(sh_ref, f32_buf, local_sems.at[0])
    own_out = pltpu.make_async_copy(own, out.at[pl.ds(pl.multiple_of(j * rows, 16), rows), :], local_sems.at[0])

    def stage_load():
        load.start()

    def stage_send():
        load.wait()
        _cast_rows(f32_buf, own)
        own_out.start(priority=SIDE)
        hop1.start()

    def stage_forward():
        hop1.wait_recv()
        hop2_own.start()
        hop2_fwd.start()
        swaps[0].start()
        keeps[0].start(priority=SIDE)

    def stage_publish():
        hop2_own.wait_recv()
        swaps[1].start()
        keeps[1].start(priority=SIDE)
        hop2_fwd.wait_recv()
        swaps[2].start()
        keeps[2].start(priority=SIDE)

    def stage_drain():
        for s, chip in enumerate((j2, j1, j3)):
            rcopy(3 + s, my_half, rows_of(chip, 1 - c), sib).wait_recv()
        for cp in [hop1, hop2_own, hop2_fwd] + swaps:
            cp.wait_send()
        for cp in [own_out] + keeps:
            cp.wait()

    return stage_load, stage_send, stage_forward, stage_publish, stage_drain


AG_CAST_ROWS = 400


def _gather_resident(sh_ref, out, f32_buf, send_sems, recv_sems, local_sems, after_first_hop):
    rows = sh_ref.shape[0]
    half = rows // 2
    x, y, c = lax.axis_index("x"), lax.axis_index("y"), lax.axis_index("c")
    j = 2 * x + y
    p1 = (_xor(x, c), _xor(y, 1 - c), c)
    p2 = (_xor(x, 1 - c), _xor(y, c), c)
    sib = (x, y, 1 - c)
    j1 = 2 * p1[0] + p1[1]
    j2 = 2 * p2[0] + p2[1]
    j3 = 3 - j

    def rows_of(chip, hf):
        return out.at[pl.ds(pl.multiple_of(chip * rows + hf * half, 16), half), :]

    def send(k, chip, to):
        return pltpu.make_async_remote_copy(src_ref=rows_of(chip, c), dst_ref=rows_of(chip, c), send_sem=send_sems.at[k],
                                            recv_sem=recv_sems.at[k], device_id=to, device_id_type=MESH)

    per_half = half // AG_CAST_ROWS

    chunks = [c * per_half + k for k in range(per_half)] + [(1 - c) * per_half + k for k in range(per_half)]

    def load(n):
        lo = pl.multiple_of(chunks[n] * AG_CAST_ROWS, 16)
        return pltpu.make_async_copy(sh_ref.at[pl.ds(lo, AG_CAST_ROWS), :], f32_buf.at[n % 2], local_sems.at[n % 2])

    def send_piece(k, chip, q, to):
        at = out.at[pl.ds(pl.multiple_of(chip * rows + c * half + q * (half // 2), 16), half // 2), :]
        return pltpu.make_async_remote_copy(src_ref=at, dst_ref=at, send_sem=send_sems.at[k], recv_sem=recv_sems.at[k],
                                            device_id=to, device_id_type=MESH)

    forwards = [send_piece(2, j1, 0, p2), send_piece(6, j1, 1, p2)]
    sends = [send(0, j, p1), send(1, j, p2)] + forwards + [send(3, j1, sib)]
    load(0).start()
    load(1).start()
    for n in range(len(chunks)):
        load(n).wait()
        lo = pl.multiple_of(chunks[n] * AG_CAST_ROWS, 16)
        _cast_rows(f32_buf.at[n % 2], out.at[pl.ds(pl.multiple_of(j * rows + lo, 16), AG_CAST_ROWS), :], rows=16)
        if n + 2 < len(chunks):
            load(n + 2).start()
        if n == per_half - 1:
            sends[0].start()
            sends[1].start()
    sends[0].wait_recv()
    for cp in sends[2:]:
        cp.start()
    after_first_hop()
    sends[1].wait_recv()
    sends.append(send(4, j2, sib))
    sends[-1].start()
    for q, (k, arrived) in enumerate(zip((5, 7), forwards)):
        arrived.wait_recv()
        sends.append(send_piece(k, j3, q, sib))
        sends[-1].start()
    for k in (3, 4):
        send(k, j, sib).wait_recv()
    for k in (5, 7):
        send_piece(k, j, 0, sib).wait_recv()
    for cp in sends:
        cp.wait_send()


def _fwd_in(x, norm_in, wt_sh, wo_sh, cw_sh, norm_conv_out, sinks, norm_attn_out):
    seq = x.shape[0]
    tile = TOK_TILE
    n_tiles = seq // tile
    stage_steps = (0, n_tiles // 4, (5 * n_tiles) // 8, n_tiles - 1)
    blocks = tile // BLK
    cw_cols = cw_sh.shape[-1]

    def body(x_ref, g_ref, wtsh_ref, wo_ref, cwsh_ref, gn_ref, sink_ref, gna_ref,
             h_ref, pc_ref, q_ref, kv_ref, ga_ref, oc_ref, ya_ref, oa_ref, pr_ref, sp_ref, wt_out, cw_ref, wo_out,
             zbuf, kv_last, wt_ref, f32_buf, cw_land, wt_send, wt_recv, wt_local, cw_send, cw_recv, cw_local, *ag_scratch):
        step = pl.program_id(0)
        to_hbm = pltpu.make_async_copy(wt_ref, wt_out, wt_local.at[0])
        load_wo, *stages = _ag_stages(wo_ref, wo_out, *ag_scratch)

        @pl.when(step == 0)
        def _():
            load_wo()
            zbuf[0:8, :] = jnp.zeros((8, D_CONV), F32)
            kv_last[...] = jnp.zeros_like(kv_last)
            x_i, y_i, c = lax.axis_index("x"), lax.axis_index("y"), lax.axis_index("c")
            j = 2 * x_i + y_i
            p1 = (_xor(x_i, c), _xor(y_i, 1 - c), c)
            p2 = (_xor(x_i, 1 - c), _xor(y_i, c), c)
            j1 = 2 * p1[0] + p1[1]

            def cw_copy(k, src, chip, to):
                return pltpu.make_async_remote_copy(src_ref=src, dst_ref=cw_land.at[chip], send_sem=cw_send.at[k],
                                                    recv_sem=cw_recv.at[k], device_id=to, device_id_type=MESH)

            mine = pltpu.make_async_copy(cwsh_ref, cw_land.at[j], cw_local.at[0])
            mine.start()
            first = cw_copy(0, cwsh_ref, j, p1)
            first.start()
            second = [cw_copy(1, cwsh_ref, j, p2), cw_copy(2, cw_land.at[j1], j1, p2)]

            def cw_second_hop():
                first.wait_recv()
                for cp in second:
                    cp.start()

            _gather_resident(wtsh_ref, wt_ref, f32_buf, wt_send, wt_recv, wt_local, cw_second_hop)
            to_hbm.start(priority=SIDE)
            for cp in second:
                cp.wait_recv()
            for cp in [first] + second:
                cp.wait_send()
            mine.wait()
            for chip in range(N_CHIPS):
                for tap in range(cw_sh.shape[0]):
                    cw_ref[tap:tap + 1, chip * cw_cols:(chip + 1) * cw_cols] = cw_land[chip, tap]

        for at, stage in zip(stage_steps[:-1], stages[:-1]):
            pl.when(step == at)(stage)

        def attention(b):
            rows = pl.ds(b * BLK, BLK)
            kv_prev = kv_last if b == 0 else kv_ref.at[pl.ds((b - 1) * BLK, BLK), :]
            return _attn_forward(q_ref.at[rows, :], kv_ref.at[rows, :], kv_prev, ga_ref.at[rows, :], sink_ref, gna_ref,
                                 _band_geometry(step * blocks + b), ya_ref.at[rows, :], oa_ref.at[rows, :],
                                 pr_ref.at[rows, :], sp_ref.at[rows, :])

        xv = x_ref[...]
        h = (xv * _rstd(xv) * g_ref[...]).astype(BF16)
        h_ref[...] = h
        q_ref[...] = _nt(h, wt_ref[ROW_Q:ROW_KV, :])
        kv_ref[...] = _nt(h, wt_ref[ROW_KV:ROW_GA, :])
        ga_ref[...] = _nt(h, wt_ref[ROW_GA:D_IN_PROJ, :])
        attention_blocks = [attention(b) for b in range(blocks)]
        for lo in range(0, D_PC, PC_PIECE):
            pc_ref[:, lo:lo + PC_PIECE] = _nt(h, wt_ref[lo:lo + PC_PIECE, :])
            for stages_of_block in attention_blocks:
                next(stages_of_block, None)
        for stages_of_block in attention_blocks:
            for _ in stages_of_block:
                pass
        kv_last[...] = kv_ref[tile - BLK:tile, :]

        cb, _, _, _, _, _, conv = _conv_core(pc_ref, zbuf, cw_ref)
        yc = cb * conv
        silu, _ = _silu_and_grad(pc_ref[:, 3 * D_CONV:4 * D_CONV])
        oc_ref[...] = (yc * _rstd(yc) * gn_ref[...] * silu).astype(BF16)
        zbuf[0:8, :] = zbuf[tile:tile + 8, :]

        @pl.when(step == stage_steps[-1])
        def _():
            stages[-1]()
            to_hbm.wait()

    def row(width):
        return pl.BlockSpec((tile, width), lambda i: (i, 0))

    any_spec = pl.BlockSpec(memory_space=pl.ANY)
    dma = pltpu.SemaphoreType.DMA
    wt_shape = (N_CHIPS * wt_sh.shape[0], wt_sh.shape[1])
    cw_shape = (cw_sh.shape[0], N_CHIPS * cw_cols)
    return pl.pallas_call(
        body, name="fwd_in", grid=(n_tiles,),
        out_shape=(jax.ShapeDtypeStruct((seq, D_MODEL), BF16), jax.ShapeDtypeStruct((seq, D_PC), F32),
                   jax.ShapeDtypeStruct((seq, D_ATTN), F32), jax.ShapeDtypeStruct((seq, 2 * D_KV), F32),
                   jax.ShapeDtypeStruct((seq, D_ATTN), F32), jax.ShapeDtypeStruct((seq, D_CONV), BF16),
                   pltpu.HBM((seq, D_ATTN), F32), pltpu.HBM((seq, D_ATTN), BF16),
                   pltpu.HBM((seq, N_HEADS * BLK), BF16), pltpu.HBM((seq, 128), F32),
                   pltpu.HBM(wt_shape, BF16), jax.ShapeDtypeStruct(cw_shape, F32),
                   jax.ShapeDtypeStruct((N_CHIPS * wo_sh.shape[0], wo_sh.shape[1]), BF16)),
        in_specs=[row(D_MODEL), _resident((1, D_MODEL)), any_spec, any_spec, any_spec, _resident((1, D_CONV)),
                  pl.BlockSpec(memory_space=pltpu.SMEM), _resident((1, D_ATTN))],
        out_specs=(row(D_MODEL), row(D_PC), row(D_ATTN), row(2 * D_KV), row(D_ATTN), row(D_CONV), row(D_ATTN),
                   row(D_ATTN), row(N_HEADS * BLK), row(128), any_spec, pl.BlockSpec(cw_shape, lambda i: (0, 0)),
                   any_spec),
        scratch_shapes=[pltpu.VMEM((tile + 8, D_CONV), F32), pltpu.VMEM((BLK, 2 * D_KV), F32),
                        pltpu.VMEM(wt_shape, BF16), pltpu.VMEM((2, AG_CAST_ROWS, wt_sh.shape[1]), F32),
                        pltpu.VMEM((N_CHIPS,) + cw_sh.shape, F32),
                        dma((8,)), dma((8,)), dma((2,)), dma((3,)), dma((3,)), dma((1,))] + _ag_scratch(wo_sh.shape),
        compiler_params=_params(62, ("arbitrary",)),
    )(x, norm_in, wt_sh, wo_sh, cw_sh, norm_conv_out, sinks, norm_attn_out)


def _conv_core(pc_ref, zbuf, cw_ref):
    tile = pc_ref.shape[0]
    cb = pc_ref[:, 0:D_CONV]
    cc = pc_ref[:, D_CONV:2 * D_CONV]
    cu = pc_ref[:, 2 * D_CONV:3 * D_CONV]
    z = cc * cu
    zbuf[8:tile + 8, :] = z
    z1 = zbuf[7:tile + 7, :]
    z2 = zbuf[6:tile + 6, :]
    conv = cw_ref[0:1, :] * z2 + cw_ref[1:2, :] * z1 + cw_ref[2:3, :] * z
    return cb, cc, cu, z, z1, z2, conv


def _band_geometry(block_index):
    qi = lax.broadcasted_iota(jnp.int32, (BLK, BLK), 0)
    kp = lax.broadcasted_iota(jnp.int32, (BLK, BLK), 1)
    use_cur = kp <= qi
    dist = jnp.where(use_cur, qi - kp, qi - kp + BLK).astype(F32)
    valid = use_cur | (block_index > 0)
    return use_cur, dist, valid


def _block_diag(cur, prev, group):
    lane = lax.broadcasted_iota(jnp.int32, cur.shape, 1)

    def halves(t):
        other = pltpu.roll(t, 64, 1)
        lo, hi = (t, other) if group == 0 else (other, t)
        return jnp.where(lane < 64, lo, 0.0), jnp.where(lane >= 64, hi, 0.0)

    return jnp.concatenate(halves(cur) + halves(prev), axis=0).astype(BF16)


def _merge(s4, use_cur):
    return (jnp.where(use_cur, s4[:, 0:BLK], s4[:, 2 * BLK:3 * BLK]),
            jnp.where(use_cur, s4[:, BLK:2 * BLK], s4[:, 3 * BLK:4 * BLK]))


def _split(a, b, use_cur):
    return jnp.concatenate([jnp.where(use_cur, a, 0.0), jnp.where(use_cur, b, 0.0),
                            jnp.where(use_cur, 0.0, a), jnp.where(use_cur, 0.0, b)], axis=1)


def _softmax_head(s, head, sink, dist, valid):
    sc = jnp.where(valid, s - SLOPES[head] * dist, -jnp.inf)
    m = jnp.maximum(jnp.max(sc, axis=-1, keepdims=True), sink)
    p = jnp.exp(sc - m)
    es = jnp.exp(sink - m)
    inv = 1.0 / (jnp.sum(p, axis=-1, keepdims=True) + es)
    return p * inv, es * inv


def _attn_operands(q_ref, kvc_ref, kvp_ref):
    groups = range(N_HEADS // HEADS_PER_KV)
    kbd = [_block_diag(kvc_ref[:, 0:D_KV], kvp_ref[:, 0:D_KV], g) for g in groups]
    vbd = [_block_diag(kvc_ref[:, D_KV:2 * D_KV], kvp_ref[:, D_KV:2 * D_KV], g) for g in groups]
    qps = [(q_ref[:, j * 128:(j + 1) * 128] * SCALE).astype(BF16) for j in range(N_HEADS // 2)]
    return qps, kbd, vbd


def _attn_forward(q_ref, kvc_ref, kvp_ref, ga_ref, sink_ref, gn_ref, geometry, ya_ref, oa_ref, pr_ref, sp_ref):
    use_cur, dist, valid = geometry
    _, kbd, vbd = operands = _attn_operands(q_ref, kvc_ref, kvp_ref)
    yield
    scores = []
    for j, qp in enumerate(operands[0]):
        scores += _merge(_nt(qp, kbd[j // 4]), use_cur)
    yield
    sinks = [sink_ref[0, h] for h in range(N_HEADS)]
    scores = [jnp.where(valid, s - SLOPES[h] * dist, -jnp.inf) for h, s in enumerate(scores)]
    maxes = [jnp.maximum(jnp.max(s, axis=-1, keepdims=True), sinks[h]) for h, s in enumerate(scores)]
    yield
    exps = [jnp.exp(s - m) for s, m in zip(scores, maxes)]
    sink_exps = [jnp.exp(sinks[h] - m) for h, m in enumerate(maxes)]
    yield
    invs = [1.0 / (jnp.sum(e, axis=-1, keepdims=True) + se) for e, se in zip(exps, sink_exps)]
    probs = [e * inv for e, inv in zip(exps, invs)]
    pr_ref[...] = jnp.concatenate(probs, axis=1).astype(BF16)
    lane = lax.broadcasted_iota(jnp.int32, (BLK, 128), 1)
    sp_ref[...] = sum(jnp.where(lane == h, se * inv, 0.0) for h, (se, inv) in enumerate(zip(sink_exps, invs)))
    yield
    p4s = [_split(probs[2 * j], probs[2 * j + 1], use_cur).astype(BF16) for j in range(N_HEADS // 2)]
    ya = jnp.concatenate([_nn(p4, vbd[j // 4]) for j, p4 in enumerate(p4s)], axis=1)
    ya_ref[...] = ya
    yield
    silu, _ = _silu_and_grad(ga_ref[...])
    oa_ref[...] = (ya * _rstd(ya) * gn_ref[...] * silu).astype(BF16)


def _out_proj_loss(x, oc, oa, wo, norm_final, target, pc, conv_w, norm_conv_out):
    seq = x.shape[0]
    tile = TOK_TILE
    n_tiles = seq // tile

    def body(x_ref, oc_ref, oa_ref, wo_ref, gf_ref, t_ref, pc_ref, hcc_ref, hcu_ref, cw_ref, gn_ref,
             dx2_ref, doa_ref, gwo_ref, dpc_ref, small_ref, loss_acc, zbuf, dbuf):
        step = pl.program_id(0)

        def small_add(row, value):
            small_ref[row:row + 1, :] += jnp.sum(value, axis=0, keepdims=True)

        @pl.when(step == 0)
        def _():
            gwo_ref[...] = jnp.zeros_like(gwo_ref)
            small_ref[...] = jnp.zeros_like(small_ref)
            loss_acc[...] = jnp.zeros_like(loss_acc)
            dbuf[tile:tile + 8, :] = jnp.zeros((8, D_CONV), F32)

        oc, oa = oc_ref[...], oa_ref[...]
        x2 = x_ref[...] + _nn(oc, wo_ref[0:D_CONV, :]) + _nn(oa, wo_ref[D_CONV:D_MIX, :])
        r = _rstd(x2)
        xhat = x2 * r
        err = xhat * gf_ref[...] - t_ref[...]
        loss_acc[...] += jnp.sum(err * err, axis=0, keepdims=True) * (0.5 / D_MODEL)
        dy = err * (1.0 / D_MODEL)
        small_add(SMALL_NORM_FINAL, dy * xhat)
        dx2 = _rms_bwd(dy * gf_ref[...], xhat, r)
        dx2_ref[...] = dx2
        db = dx2.astype(BF16)
        do = _nt(db, wo_ref[0:D_CONV, :])
        doa_ref[...] = _nt(db, wo_ref[D_CONV:D_MIX, :])
        gwo_ref[0:D_CONV, :] += _tn(oc, db)
        gwo_ref[D_CONV:D_MIX, :] += _tn(oa, db)

        is_first_tile = step == n_tiles - 1
        zbuf[0:8, :] = jnp.where(is_first_tile, 0.0, hcc_ref[...] * hcu_ref[...])
        cb, cc, cu, z, z1, z2, conv = _conv_core(pc_ref, zbuf, cw_ref)
        silu, dsilu = _silu_and_grad(pc_ref[:, 3 * D_CONV:4 * D_CONV])
        yc = cb * conv
        rc = _rstd(yc)
        chat = yc * rc
        dn = do * silu
        dpc_ref[:, 3 * D_CONV:4 * D_CONV] = (do * (chat * gn_ref[...]) * dsilu).astype(BF16)
        small_add(SMALL_NORM_CONV, dn * chat)
        dyc = _rms_bwd(dn * gn_ref[...], chat, rc)
        dpc_ref[:, 0:D_CONV] = (dyc * conv).astype(BF16)
        dconv = dyc * cb
        small_add(SMALL_CONV_W, dconv * z2)
        small_add(SMALL_CONV_W + 1, dconv * z1)
        small_add(SMALL_CONV_W + 2, dconv * z)
        dbuf[0:tile, :] = dconv
        dz = cw_ref[2:3, :] * dconv + cw_ref[1:2, :] * dbuf[1:tile + 1, :] + cw_ref[0:1, :] * dbuf[2:tile + 2, :]
        dpc_ref[:, D_CONV:2 * D_CONV] = (dz * cu).astype(BF16)
        dpc_ref[:, 2 * D_CONV:3 * D_CONV] = (dz * cc).astype(BF16)
        dbuf[tile:tile + 8, :] = dbuf[0:8, :]

        @pl.when(step == n_tiles - 1)
        def _():
            lane = lax.broadcasted_iota(jnp.int32, (1, D_MODEL), 1)
            small_ref[SMALL_MISC:SMALL_MISC + 1, :] = jnp.where(lane == SMALL_LOSS_LANE, jnp.sum(loss_acc[...]), 0.0)

    def rev(i):
        return n_tiles - 1 - i

    def row(width):
        return pl.BlockSpec((tile, width), lambda i: (rev(i), 0))

    def halo(col_block):
        return pl.BlockSpec((8, D_CONV), lambda i: (jnp.maximum(rev(i) * (tile // 8) - 1, 0), col_block))

    def const(shape):
        return pl.BlockSpec(shape, lambda i: (0, 0))

    return pl.pallas_call(
        body, name="out_proj_loss", grid=(n_tiles,),
        out_shape=(jax.ShapeDtypeStruct((seq, D_MODEL), F32), jax.ShapeDtypeStruct((seq, D_ATTN), F32),
                   jax.ShapeDtypeStruct((D_MIX, D_MODEL), F32), jax.ShapeDtypeStruct((seq, D_PC), BF16),
                   jax.ShapeDtypeStruct((SMALL_ROWS, D_MODEL), F32)),
        in_specs=[row(D_MODEL), row(D_CONV), row(D_ATTN), _resident(wo.shape), _resident((1, D_MODEL)), row(D_MODEL),
                  row(D_PC), halo(1), halo(2), _resident(conv_w.shape), _resident((1, D_CONV))],
        out_specs=(row(D_MODEL), row(D_ATTN), const((D_MIX, D_MODEL)), row(D_PC), const((SMALL_ROWS, D_MODEL))),
        scratch_shapes=[pltpu.VMEM((1, D_MODEL), F32), pltpu.VMEM((tile + 8, D_CONV), F32),
                        pltpu.VMEM((tile + 8, D_CONV), F32)],
        compiler_params=_params(62, ("arbitrary",)),
    )(x, oc, oa, wo, norm_final, target, pc, pc, pc, conv_w, norm_conv_out)


def _attn_bwd(q, kv, ga, ya, doa, probs, sink_probs, norm_attn_out, gwo, small):
    seq = q.shape[0]
    per_step = 2
    n_steps = seq // (per_step * BLK)
    stage_steps = (0, 1, n_steps // 4, (5 * n_steps) // 8, n_steps - 1, n_steps - 1)

    def body(q_ref, kvc_ref, kvp_ref, ga_ref, ya_ref, doa_ref, pr_ref, sp_ref, gn_ref, gwo_ref, small_ref,
             dqg_ref, small_out, gwo_sh, gna_ref, gs_ref, carry, dya_buf, *rs_scratch):
        step = pl.program_id(0)
        rs_stages = _rs_wout_stages(gwo_ref, gwo_sh, *rs_scratch)
        for at, stage in zip(stage_steps[:-1], rs_stages[:-1]):
            pl.when(step == at)(stage)

        @pl.when(step == 0)
        def _():
            gna_ref[...] = jnp.zeros_like(gna_ref)
            gs_ref[...] = jnp.zeros_like(gs_ref)
            carry[...] = jnp.zeros_like(carry)

        lane = lax.broadcasted_iota(jnp.int32, (BLK, 128), 1)

        def fold(bd):
            return (jnp.where(lane < 64, bd[0:BLK], 0.0) + jnp.where(lane >= 64, bd[BLK:2 * BLK], 0.0),
                    jnp.where(lane < 64, bd[2 * BLK:3 * BLK], 0.0) + jnp.where(lane >= 64, bd[3 * BLK:4 * BLK], 0.0))

        def one_block(b):
            rows = slice(b * BLK, (b + 1) * BLK)
            kv_prev = kvp_ref if b == 0 else kvc_ref.at[(b - 1) * BLK:b * BLK, :]
            ya = ya_ref[rows, :]
            r = _rstd(ya)
            xhat = ya * r
            silu, dsilu = _silu_and_grad(ga_ref[rows, :])
            do = doa_ref[rows, :]
            dn = do * silu
            dqg_ref[rows, D_ATTN:2 * D_ATTN] = (do * (xhat * gn_ref[...]) * dsilu).astype(BF16)
            gna_ref[...] += jnp.sum(dn * xhat, axis=0, keepdims=True)
            dya_buf[rows, :] = _rms_bwd(dn * gn_ref[...], xhat, r).astype(BF16)

            use_cur = _band_geometry(per_step * (n_steps - 1 - step) + b)[0]
            pairs = range(N_HEADS // 2)
            qps, kbd, vbd = _attn_operands(q_ref.at[rows, :], kvc_ref.at[rows, :], kv_prev)
            probs = [pr_ref[rows, h * BLK:(h + 1) * BLK].astype(F32) for h in range(N_HEADS)]
            dyps = [dya_buf[rows, j * 128:(j + 1) * 128] for j in pairs]
            dps = []
            for j in pairs:
                dps += _merge(_nt(dyps[j], vbd[j // 4]), use_cur)
            deltas = [jnp.sum(p * dp, axis=-1, keepdims=True) for p, dp in zip(probs, dps)]
            dss = [p * (dp - delta) for p, dp, delta in zip(probs, dps, deltas)]
            delta_lanes = sum(jnp.where(lane == h, deltas[h], 0.0) for h in range(N_HEADS))
            gs_ref[...] -= jnp.sum(sp_ref[rows, :] * delta_lanes, axis=0, keepdims=True)
            ds4s = [_split(dss[2 * j], dss[2 * j + 1], use_cur).astype(BF16) for j in pairs]
            p4s = [_split(probs[2 * j], probs[2 * j + 1], use_cur).astype(BF16) for j in pairs]
            dqg_ref[rows, 0:D_ATTN] = jnp.concatenate([_nn(ds4s[j], kbd[j // 4]) * SCALE for j in pairs], axis=1).astype(BF16)
            sums = []
            for group in range(N_HEADS // HEADS_PER_KV):
                acc = [jnp.zeros((BLK, 128), F32) for _ in range(4)]
                for j in range(group * 4, group * 4 + 4):
                    for slot, part in enumerate(fold(_tn(ds4s[j], qps[j])) + fold(_tn(p4s[j], dyps[j]))):
                        acc[slot] = acc[slot] + part
                sums.append([a + pltpu.roll(a, 64, 1) for a in acc])
            dk_cur, dk_prev, dv_cur, dv_prev = (jnp.where(lane < 64, lo, hi) for lo, hi in zip(sums[0], sums[1]))
            dqg_ref[rows, 2 * D_ATTN:2 * D_ATTN + 2 * D_KV] = (jnp.concatenate([dk_cur, dv_cur], axis=1) + carry[...]).astype(BF16)
            carry[...] = jnp.concatenate([dk_prev, dv_prev], axis=1)

        for b in reversed(range(per_step)):
            one_block(b)

        @pl.when(step == n_steps - 1)
        def _():
            small_out[...] = small_ref[...]
            small_out[SMALL_NORM_ATTN:SMALL_NORM_ATTN + 1, :] = gna_ref[...]
            small_out[SMALL_MISC:SMALL_MISC + 1, 0:128] = small_ref[SMALL_MISC:SMALL_MISC + 1, 0:128] + gs_ref[...]

        pl.when(step == stage_steps[-1])(rs_stages[-1])

    def rows(width):
        return pl.BlockSpec((per_step * BLK, width), lambda i: (n_steps - 1 - i, 0))

    kv_prev = pl.BlockSpec((BLK, 2 * D_KV), lambda i: (jnp.maximum(per_step * (n_steps - 1 - i) - 1, 0), 0))
    any_spec = pl.BlockSpec(memory_space=pl.ANY)
    return pl.pallas_call(
        body, name="attn_bwd", grid=(n_steps,),
        out_shape=(jax.ShapeDtypeStruct((seq, D_QG), BF16), pltpu.HBM(small.shape, F32),
                   pltpu.HBM((gwo.shape[0] // N_CHIPS, gwo.shape[1]), F32)),
        in_specs=[rows(D_ATTN), rows(2 * D_KV), kv_prev, rows(D_ATTN), rows(D_ATTN), rows(D_ATTN),
                  rows(N_HEADS * BLK), rows(128), _resident((1, D_ATTN)), any_spec, _resident(small.shape)],
        out_specs=(rows(D_QG), pl.BlockSpec(small.shape, lambda i: (0, 0)), any_spec),
        scratch_shapes=[pltpu.VMEM((1, D_ATTN), F32), pltpu.VMEM((1, 128), F32),
                        pltpu.VMEM((BLK, 2 * D_KV), F32), pltpu.VMEM((per_step * BLK, D_ATTN), BF16)] + _rs_wout_scratch(gwo.shape),
        compiler_params=_params(44, ("arbitrary",)),
    )(q, kv, kv, ga, ya, doa, probs, sink_probs, norm_attn_out, gwo, small)


GBLK = 256
GSUB = 64
PAIR_RING = 4
LAG_PAIR, LAG_HOP1, LAG_HOP2 = 1, 7, 13


def _bwd_in(dpc, dqg, h, wt, x, norm_in, dx2, small):
    seq = x.shape[0]
    n_blk = D_IN_PROJ // GBLK
    per_chip = n_blk // N_CHIPS
    n_slots = (n_blk + 1) // 2
    n_sub = GBLK // GSUB
    chip_rows = D_IN_PROJ // N_CHIPS
    tile = TOK_TILE
    n_tiles = seq // tile
    n_steps = n_blk + max(n_tiles, LAG_HOP2 + 1)
    chunk = min(seq, 512)
    blk_q, blk_kv, blk_ga = ROW_Q // GBLK, ROW_KV // GBLK, ROW_GA // GBLK

    def block_of(i):
        k = i % N_CHIPS
        robin = per_chip * ((k % 2) * 2 + k // 2) + i // N_CHIPS
        if isinstance(i, int):
            return robin if i < per_chip * N_CHIPS else i
        return jnp.where(i < per_chip * N_CHIPS, robin, i)

    def owner_of(i):
        return (i // N_CHIPS) % 2

    def slot_of(i):
        return (i // (2 * N_CHIPS)) * N_CHIPS + i % N_CHIPS

    def body(dpc_ref, dqg_ref, wt_ref, h_ref, x_ref, g_ref, dx2_ref, small_ref, gx_ref, small_sum, gwt_sh,
             dh_acc, gni, keep, pbuf, xbuf, land, land2, small_land,
             pair_send, pair_recv, h1_send, h1_recv, h2_send, h2_recv, sw_send, sw_recv, sm_send, sm_recv, out_sem):
        step = pl.program_id(0)
        x_i, y_i, c = lax.axis_index("x"), lax.axis_index("y"), lax.axis_index("c")
        me = 4 * x_i + 2 * y_i + c
        j = 2 * x_i + y_i
        pa = (_xor(x_i, 1 - c), _xor(y_i, c), c)
        pb = (_xor(x_i, c), _xor(y_i, 1 - c), c)
        sib = (x_i, y_i, 1 - c)
        ja = 2 * pa[0] + pa[1]
        jb = 2 * pb[0] + pb[1]
        jd = 3 - j

        def remote(src, dst, send, recv, to):
            return pltpu.make_async_remote_copy(src_ref=src, dst_ref=dst, send_sem=send, recv_sem=recv,
                                                device_id=to, device_id_type=MESH)

        def piece(ref, slot, u, n):
            return ref.at[slot, pl.ds(u * GSUB, n * GSUB), :]

        def chip_rows_at(ref, local, n):
            return ref.at[pl.ds(pl.multiple_of(local, GSUB), n * GSUB), :]

        def pair_copy(i):
            slot = slot_of(i)
            return remote(pbuf.at[i % PAIR_RING], land.at[slot], pair_send.at[slot], pair_recv.at[slot], sib)

        def h1_copy(slot, u, n):
            k = slot * n_sub + u
            return remote(piece(xbuf, slot, u, n), piece(xbuf, slot, u, n), h1_send.at[k], h1_recv.at[k], pa)

        def h2_copy(slot, u, n, local):
            k = slot * n_sub + u
            return remote(piece(xbuf, slot, u, n), chip_rows_at(land2, local, n), h2_send.at[k], h2_recv.at[k], pb)

        def sw_copy(slot, u, n, local):
            k = slot * n_sub + u
            return remote(piece(keep, slot, u, n), chip_rows_at(gwt_sh, local, n), sw_send.at[k], sw_recv.at[k], sib)

        def own_copy(slot, u, n, local):
            return pltpu.make_async_copy(piece(keep, slot, u, n), chip_rows_at(gwt_sh, local, n), out_sem.at[slot * n_sub + u])

        def owned(i):
            return (i >= 0) & (i < n_blk) & (owner_of(i) == c)

        def chip_of(blk, u):
            row = blk * GBLK + u * GSUB
            chip = row // chip_rows
            return chip, row - chip * chip_rows

        def pieces(blk):
            first, local = chip_of(blk, 0)
            whole = first == chip_of(blk, n_sub - 1)[0]
            if isinstance(blk, int):
                return [(True, 0, n_sub, first, local)] if whole else [(True, u, 1) + chip_of(blk, u) for u in range(n_sub)]
            return [(whole, 0, n_sub, first, local)] + [(jnp.logical_not(whole), u, 1) + chip_of(blk, u) for u in range(n_sub)]

        @pl.when(step == 0)
        def _():
            dh_acc[...] = jnp.zeros_like(dh_acc)
            gni[...] = jnp.zeros_like(gni)

        @pl.when(step < n_blk)
        def _():
            from_pc = block_of(step) < blk_q
            block = _tn(jnp.where(from_pc, dpc_ref[...], dqg_ref[...]), h_ref[...])
            for t in range(0, seq, chunk):
                d = jnp.where(from_pc, dpc_ref[t:t + chunk, :], dqg_ref[t:t + chunk, :])
                dh_acc[t:t + chunk, :] += _nn(d, wt_ref[...])

            @pl.when(owner_of(step) == c)
            def _():
                keep[slot_of(step)] = block

            @pl.when(owner_of(step) != c)
            def _():
                @pl.when(step >= 2 * PAIR_RING)
                def _():
                    pair_copy(step - 2 * PAIR_RING).wait_send()
                pbuf[step % PAIR_RING] = block.astype(BF16)
                pair_copy(step).start()

        i1 = step - LAG_PAIR

        @pl.when(owned(i1))
        def _():
            slot = slot_of(i1)
            pair_copy(i1).wait_recv()
            _accumulate(keep.at[slot], land.at[slot])
            for cond, u, n, chip, _ in pieces(block_of(i1)):
                @pl.when(cond & ((chip == ja) | (chip == jd)))
                def _(u=u, n=n):
                    _cast_rows(piece(keep, slot, u, n), piece(xbuf, slot, u, n))
                    h1_copy(slot, u, n).start()

        i2 = step - LAG_HOP1

        @pl.when(owned(i2))
        def _():
            slot = slot_of(i2)
            for cond, u, n, chip, local in pieces(block_of(i2)):
                @pl.when(cond & ((chip == j) | (chip == jb)))
                def _(u=u, n=n, chip=chip, local=local):
                    h1_copy(slot, u, n).wait_recv()
                    _accumulate(piece(keep, slot, u, n), piece(xbuf, slot, u, n))

                    @pl.when(chip == jb)
                    def _():
                        _cast_rows(piece(keep, slot, u, n), piece(xbuf, slot, u, n))
                        h2_copy(slot, u, n, local).start()

                @pl.when(cond & ((chip == ja) | (chip == jd)))
                def _(u=u, n=n):
                    h1_copy(slot, u, n).wait_send()

        i3 = step - LAG_HOP2

        @pl.when(owned(i3))
        def _():
            slot = slot_of(i3)
            for cond, u, n, chip, local in pieces(block_of(i3)):
                @pl.when(cond & (chip == j))
                def _(u=u, n=n, local=local):
                    h2_copy(slot, u, n, local).wait_recv()
                    _accumulate(piece(keep, slot, u, n), chip_rows_at(land2, local, n))
                    own_copy(slot, u, n, local).start(priority=SIDE)
                    sw_copy(slot, u, n, local).start()

                @pl.when(cond & (chip == jb))
                def _(u=u, n=n, local=local):
                    h2_copy(slot, u, n, local).wait_send()

        e = step - n_blk

        @pl.when((e >= 0) & (e < n_tiles))
        def _():
            dh = dh_acc[pl.ds(pl.multiple_of(e * tile, tile), tile), :]
            xv = x_ref[...]
            r = _rstd(xv)
            xhat = xv * r
            gni[...] += jnp.sum(dh * xhat, axis=0, keepdims=True)
            gx_ref[...] = _rms_bwd(dh * g_ref[...], xhat, r) + dx2_ref[...]

        others = [(dx, dy, dc) for dx in (0, 1) for dy in (0, 1) for dc in (0, 1)][1:]
        sends = [remote(small_land.at[me], small_land.at[me], sm_send.at[k], sm_recv.at[k],
                        (_xor(x_i, dx), _xor(y_i, dy), _xor(c, dc))) for k, (dx, dy, dc) in enumerate(others)]

        @pl.when(step == min(n_blk + n_tiles, n_steps - 1))
        def _():
            small_land[me] = small_ref[...]
            small_land[me, SMALL_NORM_IN:SMALL_NORM_IN + 1, :] = gni[...]
            for cp in sends:
                cp.start()

        @pl.when(step == n_steps - 1)
        def _():
            for i in range(n_blk):
                if i + 2 * PAIR_RING >= n_blk:
                    @pl.when(owner_of(i) != c)
                    def _(i=i):
                        pair_copy(i).wait_send()
                for _, u, n, chip, local in pieces(block_of(i)):
                    @pl.when((j == chip) & (c == owner_of(i)))
                    def _(i=i, u=u, n=n, local=local):
                        sw_copy(slot_of(i), u, n, local).wait_send()
                        own_copy(slot_of(i), u, n, local).wait()

                    @pl.when((j == chip) & (c != owner_of(i)))
                    def _(i=i, u=u, n=n, local=local):
                        sw_copy(slot_of(i), u, n, local).wait_recv()
            for cp in sends:
                cp.wait_recv()
            total = small_land[0]
            for dev in range(1, 8):
                total = total + small_land[dev]
            small_sum[...] = total
            for cp in sends:
                cp.wait_send()

    def blk_at(i):
        return block_of(jnp.clip(i, 0, n_blk - 1))

    last_pc_step = max(i for i in range(n_blk) if block_of(i) < blk_q)

    def next_block(i, in_pc):
        i = jnp.clip(i, 0, n_blk - 1)
        step = jnp.full_like(i, last_pc_step if in_pc else n_blk - 1)
        for ahead in reversed(range(N_CHIPS)):
            cand = jnp.minimum(i + ahead, n_blk - 1)
            step = jnp.where((block_of(cand) < blk_q) == in_pc, cand, step)
        return block_of(step)

    def dqg_block(i):
        b = next_block(i, False)
        q_blk = jnp.clip(b - blk_q, 0, blk_kv - blk_q - 1)
        ga_blk = (D_ATTN // GBLK) + jnp.clip(b - blk_ga, 0, n_blk - blk_ga - 1)
        return jnp.where(b < blk_kv, q_blk, jnp.where(b == blk_kv, 2 * D_ATTN // GBLK, ga_blk))

    def tok(i):
        return (jnp.clip(i - n_blk, 0, n_tiles - 1), 0)

    n_piece = n_slots * n_sub
    dma = pltpu.SemaphoreType.DMA
    return pl.pallas_call(
        body, name="bwd_in", grid=(n_steps,),
        out_shape=(jax.ShapeDtypeStruct((seq, D_MODEL), F32), jax.ShapeDtypeStruct(small.shape, F32),
                   jax.ShapeDtypeStruct((chip_rows, D_MODEL), F32)),
        in_specs=[pl.BlockSpec((seq, GBLK), lambda i: (0, next_block(i, True))),
                  pl.BlockSpec((seq, GBLK), lambda i: (0, dqg_block(i))),
                  pl.BlockSpec((GBLK, D_MODEL), lambda i: (blk_at(i), 0)),
                  _resident(h.shape),
                  pl.BlockSpec((tile, D_MODEL), tok), _resident((1, D_MODEL)), pl.BlockSpec((tile, D_MODEL), tok),
                  _resident(small.shape)],
        out_specs=(pl.BlockSpec((tile, D_MODEL), tok), pl.BlockSpec(small.shape, lambda i: (0, 0)),
                   pl.BlockSpec(memory_space=pl.ANY)),
        scratch_shapes=[pltpu.VMEM((seq, D_MODEL), F32), pltpu.VMEM((1, D_MODEL), F32),
                        pltpu.VMEM((n_slots, GBLK, D_MODEL), F32), pltpu.VMEM((PAIR_RING, GBLK, D_MODEL), BF16),
                        pltpu.VMEM((n_slots, GBLK, D_MODEL), BF16), pltpu.VMEM((n_slots, GBLK, D_MODEL), BF16),
                        pltpu.VMEM((chip_rows, D_MODEL), BF16), pltpu.VMEM((8,) + small.shape, F32),
                        dma((n_slots,)), dma((n_slots,)), dma((n_piece,)), dma((n_piece,)), dma((n_piece,)),
                        dma((n_piece,)), dma((n_piece,)), dma((n_piece,)), dma((7,)), dma((7,)), dma((n_piece,))],
        compiler_params=_params(62, ("arbitrary",)),
    )(dpc, dqg, wt, h, x, norm_in, dx2, small)


def _accumulate(dst_ref, src_ref, rows=16):
    def step(i, carry):
        sl = pl.ds(pl.multiple_of(i * rows, rows), rows)
        dst_ref[sl, :] = dst_ref[sl, :] + src_ref[sl, :].astype(F32)
        return carry
    lax.fori_loop(0, dst_ref.shape[0] // rows, step, 0)


def _rs_wout_scratch(gwo_shape):
    o_half, width = gwo_shape[0] // N_CHIPS // 2, gwo_shape[1]
    return [pltpu.VMEM((4, o_half, width), F32), pltpu.VMEM((4, o_half, width), BF16),
            pltpu.VMEM((4, o_half, width), BF16), pltpu.VMEM((2, o_half, width), BF16),
            pltpu.VMEM((o_half, width), BF16),
            pltpu.SemaphoreType.DMA((8,)), pltpu.SemaphoreType.DMA((8,)), pltpu.SemaphoreType.DMA((4,))]


def _rs_wout_stages(gwo_ref, gwo_sh, acc_o, sb_o, r1o, r2o, r3o, send_sems, recv_sems, local_sems):
    o_rows = gwo_ref.shape[0] // N_CHIPS
    o_half = o_rows // 2
    x, y, c = lax.axis_index("x"), lax.axis_index("y"), lax.axis_index("c")
    j = 2 * x + y
    pa = (_xor(x, 1 - c), _xor(y, c), c)
    pb = (_xor(x, c), _xor(y, 1 - c), c)
    sib = (x, y, 1 - c)
    ja = 2 * pa[0] + pa[1]
    jb = 2 * pb[0] + pb[1]
    jd = 3 - j
    order = (ja, jd, jb, j)
    sib_order = (jb, jd, ja, j)

    def rcopy(k, src, dst, to):
        return pltpu.make_async_remote_copy(src_ref=src, dst_ref=dst, send_sem=send_sems.at[k],
                                            recv_sem=recv_sems.at[k], device_id=to, device_id_type=MESH)

    def o_rows_of(chip, half):
        return gwo_ref.at[pl.ds(pl.multiple_of(chip * o_rows + half * o_half, 8), o_half), :]

    def loads(chips, half):
        return [pltpu.make_async_copy(o_rows_of(chip, half), acc_o.at[s], local_sems.at[s]) for s, chip in enumerate(chips)]

    def pair_send(s):
        return rcopy(s, sb_o.at[s], r1o.at[s], sib)

    def out_half(half):
        return gwo_sh.at[pl.ds(pl.multiple_of(half * o_half, 8), o_half), :]

    hop1 = [rcopy(4 + s, sb_o.at[s], r2o.at[s], pa) for s in range(2)]
    hop2 = rcopy(6, sb_o.at[2], r3o, pb)
    swap = rcopy(7, acc_o.at[3], out_half(c), sib)
    mine = pltpu.make_async_copy(acc_o.at[3], out_half(c), local_sems.at[0])

    def resend(s, copy):
        pair_send(s).wait_send()
        _cast_rows(acc_o.at[s], sb_o.at[s])
        copy.start()

    def stage_load():
        for cp in loads(sib_order, 1 - c):
            cp.start(priority=SIDE)

    def stage_pair():
        for s, (cp, mine_in) in enumerate(zip(loads(sib_order, 1 - c), loads(order, c))):
            cp.wait()
            _cast_rows(acc_o.at[s], sb_o.at[s])
            pair_send(s).start()
            mine_in.start(priority=SIDE)

    def stage_hop1():
        for s, cp in enumerate(loads(order, c)):
            cp.wait()
            pair_send(s).wait_recv()
            _accumulate(acc_o.at[s], r1o.at[s])
            if s < 2:
                resend(s, hop1[s])

    def stage_hop2():
        hop1[1].wait_recv()
        _accumulate(acc_o.at[2], r2o.at[1])
        resend(2, hop2)
        hop1[0].wait_recv()
        _accumulate(acc_o.at[3], r2o.at[0])

    def stage_final():
        hop2.wait_recv()
        _accumulate(acc_o.at[3], r3o)
        mine.start(priority=SIDE)
        swap.start()

    def stage_drain():
        rcopy(7, acc_o.at[3], out_half(1 - c), sib).wait_recv()
        for cp in [pair_send(3)] + hop1 + [hop2, swap]:
            cp.wait_send()
        mine.wait()

    return stage_load, stage_pair, stage_hop1, stage_hop2, stage_final, stage_drain


def _adamw_big(groups, passed):
    arrays = [a for group in groups for a in group[:4]]
    steps = [group[0].shape[0] // group[4] for group in groups]
    first = [sum(steps[:k]) for k in range(len(steps) + 1)]
    n = len(arrays)

    def body(*refs):
        ins, passed_in, outs, passed_out = refs[:n], refs[n], refs[n + 1:2 * n + 1], refs[2 * n + 1]
        i = pl.program_id(0)
        for k in range(len(groups)):
            @pl.when((i >= first[k]) & (i < first[k + 1]))
            def _(k=k):
                w_ref, g_ref, m_ref, v_ref = ins[4 * k:4 * k + 4]
                _adamw_update(w_ref, g_ref[...], m_ref, v_ref, *outs[4 * k:4 * k + 4])

        @pl.when(i < steps[0])
        def _():
            passed_out[...] = passed_in[...]

    def spec(k, rows, width):
        return pl.BlockSpec((rows, width), lambda i: (jnp.clip(i - first[k], 0, steps[k] - 1), 0))

    specs = [spec(k, group[4], group[0].shape[1]) for k, group in enumerate(groups) for _ in range(4)]
    specs.append(spec(0, passed.shape[0] // steps[0], passed.shape[1]))
    out = pl.pallas_call(
        body, name="adamw_big", grid=(first[-1],),
        out_shape=tuple(jax.ShapeDtypeStruct(a.shape, a.dtype) for a in arrays + [passed]),
        in_specs=specs, out_specs=tuple(specs),
        compiler_params=_params(40, ("arbitrary",)),
    )(*arrays, passed)
    return [tuple(out[4 * k:4 * k + 4]) for k in range(len(groups))], out[-1]


def _adamw_update(w_ref, gv, m_ref, v_ref, go_ref, d_ref, nm_ref, nv_ref, at=...):
    go_ref[at] = gv
    nm = ADAM_B1 * m_ref[at] + (1.0 - ADAM_B1) * gv
    nv = ADAM_B2 * v_ref[at] + (1.0 - ADAM_B2) * (gv * gv)
    m_hat = nm / (1.0 - ADAM_B1 ** ADAM_STEP)
    v_hat = nv / (1.0 - ADAM_B2 ** ADAM_STEP)
    d_ref[at] = -ADAM_LR * (m_hat / (jnp.sqrt(v_hat) + ADAM_EPS) + ADAM_WD * w_ref[at])
    nm_ref[at] = nm
    nv_ref[at] = nv


def _adamw_small(chip, small_sum, weights, grads_of, ms, vs):
    n = len(weights)

    def body(chip_ref, small_ref, *refs):
        ins, outs, loss_ref = refs[:3 * n], refs[3 * n:-1], refs[-1]
        for k in range(n):
            w_ref, m_ref, v_ref = ins[3 * k:3 * k + 3]
            for at, gv in grads_of[k](small_ref, chip_ref):
                _adamw_update(w_ref, gv, m_ref, v_ref, *outs[4 * k:4 * k + 4], at=at)
        loss_ref[...] = small_ref[SMALL_MISC:SMALL_MISC + 1, SMALL_LOSS_LANE:SMALL_LOSS_LANE + 1]

    flat = [a for group in zip(weights, ms, vs) for a in group]
    vmem = pl.BlockSpec(memory_space=pltpu.VMEM)
    out = pl.pallas_call(
        body, name="adamw_small",
        out_shape=tuple(jax.ShapeDtypeStruct(w.shape, F32) for w in weights for _ in range(4))
        + (jax.ShapeDtypeStruct((1, 1), F32),),
        in_specs=[pl.BlockSpec(memory_space=pltpu.SMEM)] + [vmem] * (1 + 3 * n), out_specs=(vmem,) * (4 * n + 1),
    )(chip, small_sum, *flat)
    return [tuple(out[4 * k:4 * k + 4]) for k in range(n)], out[-1][0, 0]


def kernel(x, norm_in, w_in, conv_w, attn_sinks, norm_conv_out, norm_attn_out, w_out, norm_final, loss_target, m_norm_in, m_w_in, m_conv_w, m_attn_sinks, m_norm_conv_out, m_norm_attn_out, m_w_out, m_norm_final, v_norm_in, v_w_in, v_conv_w, v_attn_sinks, v_norm_conv_out, v_norm_attn_out, v_w_out, v_norm_final):
    chip = 2 * lax.axis_index("x") + lax.axis_index("y")
    xs, target = x[0], loss_target[0]
    norm_final2 = norm_final.reshape(1, D_MODEL)
    w_in_t, m_w_in_t, v_w_in_t = w_in[0].T, m_w_in[0].T, v_w_in[0].T

    def from_hbm(*arrays):
        return tuple(pltpu.with_memory_space_constraint(a, pltpu.HBM) for a in arrays)

    h, pc, q, kv, ga, oc, ya, oa, probs, sink_probs, wt, cw, wo = _fwd_in(
        xs, norm_in, w_in_t, w_out[0], conv_w.transpose(1, 0, 2), norm_conv_out, attn_sinks, norm_attn_out)
    dx2, doa, gwo, dpc, small = _out_proj_loss(xs, oc, oa, wo, norm_final2, target, pc, cw, norm_conv_out)
    q, kv, ga, ya, doa, probs, gwo, small = from_hbm(q, kv, ga, ya, doa, probs, gwo, small)
    dqg, small, gwo_sh = _attn_bwd(q, kv, ga, ya, doa, probs, sink_probs, norm_attn_out, gwo, small)
    grad_x, small_sum, gwt_sh = _bwd_in(dpc, dqg, h, wt, xs, norm_in, dx2, small)

    (up_w_in, up_w_out), grad_x = _adamw_big(
        [from_hbm(w_in_t, gwt_sh, m_w_in_t, v_w_in_t) + (200,), from_hbm(w_out[0], gwo_sh, m_w_out[0], v_w_out[0]) + (128,)],
        *from_hbm(grad_x))
    up_w_in = tuple(o.T[None] for o in up_w_in)
    up_w_out = tuple(o[None] for o in up_w_out)

    def row_of(r):
        return lambda small_ref, chip_ref: [(..., small_ref[r:r + 1, :])]

    def sink_lanes(small_ref, chip_ref):
        return [(..., small_ref[SMALL_MISC:SMALL_MISC + 1, 0:N_HEADS])]

    def conv_taps(small_ref, chip_ref):
        width = D_CONV // N_CHIPS
        cols = pl.ds(pl.multiple_of(chip_ref[0] * width, width), width)
        return [(k, small_ref[pl.ds(SMALL_CONV_W + k, 1), cols]) for k in range(conv_w.shape[1])]

    def small_view(a):
        return a.transpose(1, 0, 2) if a.ndim == 3 else a.reshape(-1, a.shape[-1])

    small_w = (norm_in, conv_w, attn_sinks, norm_conv_out, norm_attn_out, norm_final)
    small_m = (m_norm_in, m_conv_w, m_attn_sinks, m_norm_conv_out, m_norm_attn_out, m_norm_final)
    small_v = (v_norm_in, v_conv_w, v_attn_sinks, v_norm_conv_out, v_norm_attn_out, v_norm_final)
    small_g = (row_of(SMALL_NORM_IN), conv_taps, sink_lanes, row_of(SMALL_NORM_CONV), row_of(SMALL_NORM_ATTN),
               row_of(SMALL_NORM_FINAL))
    w_views, m_views, v_views = (tuple(small_view(a) for a in group) for group in (small_w, small_m, small_v))
    up_small, loss = _adamw_small(chip.astype(jnp.int32).reshape(1), small_sum, w_views, small_g, m_views, v_views)
    up_small = [tuple(o.transpose(1, 0, 2) if w.ndim == 3 else o.reshape(w.shape) for o in up)
                for up, w in zip(up_small, small_w)]
    up_norm_in, up_conv_w, up_sinks, up_norm_conv, up_norm_attn, up_norm_final = up_small
    updates = (up_norm_in, up_w_in, up_conv_w, up_sinks, up_norm_conv, up_norm_attn, up_w_out, up_norm_final)
    grads_out, deltas, new_m, new_v = zip(*updates)
    return (loss, grad_x[None], *grads_out, *deltas, *new_m, *new_v)
```

```python
import jax
import jax.numpy as jnp
from jax import lax
from jax.experimental import pallas as pl
from jax.experimental.pallas import tpu as pltpu

F32 = jnp.float32
BF16 = jnp.bfloat16
MESH = pl.DeviceIdType.MESH

D_MODEL = 1024
D_CONV = 1024
D_ATTN = 1024
D_KV = 128
D_QG = 2 * D_ATTN + 2 * D_KV
D_MIX = D_CONV + D_ATTN
D_PC = 4 * D_CONV
D_IN_PROJ = D_PC + 2 * D_ATTN + 2 * D_KV
ROW_Q = D_PC
ROW_KV = ROW_Q + D_ATTN
ROW_GA = ROW_KV + 2 * D_KV
N_HEADS = 16
HEAD_DIM = 64
HEADS_PER_KV = 8
BLK = 128
N_CHIPS = 4
RMS_EPS = 1e-5
SCALE = HEAD_DIM ** -0.5
SLOPES = tuple(2.0 ** (-8.0 * (h + 1) / N_HEADS) for h in range(N_HEADS))

ADAM_LR, ADAM_B1, ADAM_B2, ADAM_EPS, ADAM_WD, ADAM_STEP = 0.001, 0.9, 0.999, 1e-08, 0.01, 10

SMALL_ROWS = 8
SMALL_NORM_IN, SMALL_NORM_CONV, SMALL_NORM_ATTN, SMALL_NORM_FINAL, SMALL_CONV_W, SMALL_MISC = 0, 1, 2, 3, 4, 7
SMALL_LOSS_LANE = N_HEADS

TOK_TILE = 256
PC_PIECE = 512
MIB = 1 << 20


def _params(vmem_mib, semantics=None):
    return pltpu.CompilerParams(dimension_semantics=semantics, vmem_limit_bytes=vmem_mib * MIB)


def _nn(a, b):
    return jnp.dot(a, b, preferred_element_type=F32)


def _nt(a, b):
    return lax.dot_general(a, b, (((1,), (1,)), ((), ())), preferred_element_type=F32)


def _tn(a, b):
    return lax.dot_general(a, b, (((0,), (0,)), ((), ())), preferred_element_type=F32)


def _rstd(v):
    return lax.rsqrt(jnp.mean(v * v, axis=-1, keepdims=True) + RMS_EPS)


def _rms_bwd(g, xhat, rstd):
    return rstd * (g - xhat * jnp.mean(g * xhat, axis=-1, keepdims=True))


def _silu_and_grad(g):
    s = jax.nn.sigmoid(g)
    return g * s, s * (1.0 + g * (1.0 - s))


def _resident(shape):
    return pl.BlockSpec(shape, lambda *_: (0,) * len(shape), pipeline_mode=pl.Buffered(1))


def _xor(a, b):
    return a + b - 2 * a * b


def _cast_rows(src_ref, dst_ref, rows=32):
    def step(i, carry):
        sl = pl.ds(pl.multiple_of(i * rows, rows), rows)
        dst_ref[sl, :] = src_ref[sl, :].astype(dst_ref.dtype)
        return carry
    lax.fori_loop(0, src_ref.shape[0] // rows, step, 0)


def _ag_scratch(shard_shape):
    rows, width = shard_shape
    return [pltpu.VMEM((rows, width), F32), pltpu.VMEM((rows, width), BF16), pltpu.VMEM((3, rows // 2, width), BF16),
            pltpu.SemaphoreType.DMA((6,)), pltpu.SemaphoreType.DMA((6,)), pltpu.SemaphoreType.DMA((4,))]


def _ag_stages(sh_ref, out, f32_buf, own, land, send_sems, recv_sems, local_sems):
    rows = sh_ref.shape[0]
    half = rows // 2
    x, y, c = lax.axis_index("x"), lax.axis_index("y"), lax.axis_index("c")
    j = 2 * x + y
    p1 = (_xor(x, c), _xor(y, 1 - c), c)
    p2 = (_xor(x, 1 - c), _xor(y, c), c)
    sib = (x, y, 1 - c)
    j1 = 2 * p1[0] + p1[1]
    j2 = 2 * p2[0] + p2[1]
    j3 = 3 - j

    def rows_of(chip, hf):
        return out.at[pl.ds(pl.multiple_of(chip * rows + hf * half, 16), half), :]

    def rcopy(k, src, dst, to):
        return pltpu.make_async_remote_copy(src_ref=src, dst_ref=dst, send_sem=send_sems.at[k],
                                            recv_sem=recv_sems.at[k], device_id=to, device_id_type=MESH)

    my_half = own.at[pl.ds(pl.multiple_of(c * half, 16), half), :]
    hop1 = rcopy(0, my_half, land.at[0], p1)
    hop2_own = rcopy(1, my_half, land.at[1], p2)
    hop2_fwd = rcopy(2, land.at[0], land.at[2], p2)
    swaps = [rcopy(3 + s, land.at[s], rows_of(chip, c), sib) for s, chip in enumerate((j1, j2, j3))]
    keeps = [pltpu.make_async_copy(land.at[s], rows_of(chip, c), local_sems.at[1 + s]) for s, chip in enumerate((j1, j2, j3))]
    load = pltpu.make_async_copy(sh_ref, f32_buf, local_sems.at[0])
    own_out = pltpu.make_async_copy(own, out.at[pl.ds(pl.multiple_of(j * rows, 16), rows), :], local_sems.at[0])

    def stage_load():
        load.start()

    def stage_send():
        load.wait()
        _cast_rows(f32_buf, own)
        own_out.start()
        hop1.start()

    def stage_forward():
        hop1.wait_recv()
        hop2_own.start()
        hop2_fwd.start()
        swaps[0].start()
        keeps[0].start()

    def stage_publish():
        hop2_own.wait_recv()
        swaps[1].start()
        keeps[1].start()
        hop2_fwd.wait_recv()
        swaps[2].start()
        keeps[2].start()

    def stage_drain():
        for s, chip in enumerate((j2, j1, j3)):
            rcopy(3 + s, my_half, rows_of(chip, 1 - c), sib).wait_recv()
        for cp in [hop1, hop2_own, hop2_fwd] + swaps:
            cp.wait_send()
        for cp in [own_out] + keeps:
            cp.wait()

    return stage_load, stage_send, stage_forward, stage_publish, stage_drain


AG_CAST_ROWS = 400


def _gather_resident(sh_ref, out, f32_buf, send_sems, recv_sems, local_sems, after_first_hop):
    rows = sh_ref.shape[0]
    half = rows // 2
    x, y, c = lax.axis_index("x"), lax.axis_index("y"), lax.axis_index("c")
    j = 2 * x + y
    p1 = (_xor(x, c), _xor(y, 1 - c), c)
    p2 = (_xor(x, 1 - c), _xor(y, c), c)
    sib = (x, y, 1 - c)
    j1 = 2 * p1[0] + p1[1]
    j2 = 2 * p2[0] + p2[1]
    j3 = 3 - j

    def rows_of(chip, hf):
        return out.at[pl.ds(pl.multiple_of(chip * rows + hf * half, 16), half), :]

    def send(k, chip, to):
        return pltpu.make_async_remote_copy(src_ref=rows_of(chip, c), dst_ref=rows_of(chip, c), send_sem=send_sems.at[k],
                                            recv_sem=recv_sems.at[k], device_id=to, device_id_type=MESH)

    per_half = half // AG_CAST_ROWS

    chunks = [c * per_half + k for k in range(per_half)] + [(1 - c) * per_half + k for k in range(per_half)]

    def load(n):
        lo = pl.multiple_of(chunks[n] * AG_CAST_ROWS, 16)
        return pltpu.make_async_copy(sh_ref.at[pl.ds(lo, AG_CAST_ROWS), :], f32_buf.at[n % 2], local_sems.at[n % 2])

    def send_piece(k, chip, q, to):
        at = out.at[pl.ds(pl.multiple_of(chip * rows + c * half + q * (half // 2), 16), half // 2), :]
        return pltpu.make_async_remote_copy(src_ref=at, dst_ref=at, send_sem=send_sems.at[k], recv_sem=recv_sems.at[k],
                                            device_id=to, device_id_type=MESH)

    forwards = [send_piece(2, j1, 0, p2), send_piece(6, j1, 1, p2)]
    sends = [send(0, j, p1), send(1, j, p2)] + forwards + [send(3, j1, sib)]
    load(0).start()
    load(1).start()
    for n in range(len(chunks)):
        load(n).wait()
        lo = pl.multiple_of(chunks[n] * AG_CAST_ROWS, 16)
        _cast_rows(f32_buf.at[n % 2], out.at[pl.ds(pl.multiple_of(j * rows + lo, 16), AG_CAST_ROWS), :], rows=16)
        if n + 2 < len(chunks):
            load(n + 2).start()
        if n == per_half - 1:
            sends[0].start()
            sends[1].start()
    sends[0].wait_recv()
    for cp in sends[2:]:
        cp.start()
    after_first_hop()
    sends[1].wait_recv()
    sends.append(send(4, j2, sib))
    sends[-1].start()
    for q, (k, arrived) in enumerate(zip((5, 7), forwards)):
        arrived.wait_recv()
        sends.append(send_piece(k, j3, q, sib))
        sends[-1].start()
    for k in (3, 4):
        send(k, j, sib).wait_recv()
    for k in (5, 7):
        send_piece(k, j, 0, sib).wait_recv()
    for cp in sends:
        cp.wait_send()


def _fwd_in(x, norm_in, wt_sh, wo_sh, cw_sh, norm_conv_out, sinks, norm_attn_out):
    seq = x.shape[0]
    tile = TOK_TILE
    n_tiles = seq // tile
    stage_steps = (0, n_tiles // 4, (5 * n_tiles) // 8, n_tiles - 1)
    blocks = tile // BLK
    cw_cols = cw_sh.shape[-1]

    def body(x_ref, g_ref, wtsh_ref, wo_ref, cwsh_ref, gn_ref, sink_ref, gna_ref,
             h_ref, pc_ref, q_ref, kv_ref, ga_ref, oc_ref, ya_ref, oa_ref, pr_ref, sp_ref, wt_out, cw_ref, wo_out,
             zbuf, kv_last, wt_ref, f32_buf, cw_land, wt_send, wt_recv, wt_local, cw_send, cw_recv, cw_local, *ag_scratch):
        step = pl.program_id(0)
        to_hbm = pltpu.make_async_copy(wt_ref, wt_out, wt_local.at[0])
        load_wo, *stages = _ag_stages(wo_ref, wo_out, *ag_scratch)

        @pl.when(step == 0)
        def _():
            load_wo()
            zbuf[0:8, :] = jnp.zeros((8, D_CONV), F32)
            kv_last[...] = jnp.zeros_like(kv_last)
            x_i, y_i, c = lax.axis_index("x"), lax.axis_index("y"), lax.axis_index("c")
            j = 2 * x_i + y_i
            p1 = (_xor(x_i, c), _xor(y_i, 1 - c), c)
            p2 = (_xor(x_i, 1 - c), _xor(y_i, c), c)
            j1 = 2 * p1[0] + p1[1]

            def cw_copy(k, src, chip, to):
                return pltpu.make_async_remote_copy(src_ref=src, dst_ref=cw_land.at[chip], send_sem=cw_send.at[k],
                                                    recv_sem=cw_recv.at[k], device_id=to, device_id_type=MESH)

            mine = pltpu.make_async_copy(cwsh_ref, cw_land.at[j], cw_local.at[0])
            mine.start()
            first = cw_copy(0, cwsh_ref, j, p1)
            first.start()
            second = [cw_copy(1, cwsh_ref, j, p2), cw_copy(2, cw_land.at[j1], j1, p2)]

            def cw_second_hop():
                first.wait_recv()
                for cp in second:
                    cp.start()

            _gather_resident(wtsh_ref, wt_ref, f32_buf, wt_send, wt_recv, wt_local, cw_second_hop)
            to_hbm.start()
            for cp in second:
                cp.wait_recv()
            for cp in [first] + second:
                cp.wait_send()
            mine.wait()
            for chip in range(N_CHIPS):
                for tap in range(cw_sh.shape[0]):
                    cw_ref[tap:tap + 1, chip * cw_cols:(chip + 1) * cw_cols] = cw_land[chip, tap]

        for at, stage in zip(stage_steps[:-1], stages[:-1]):
            pl.when(step == at)(stage)

        def attention(b):
            rows = pl.ds(b * BLK, BLK)
            kv_prev = kv_last if b == 0 else kv_ref.at[pl.ds((b - 1) * BLK, BLK), :]
            return _attn_forward(q_ref.at[rows, :], kv_ref.at[rows, :], kv_prev, ga_ref.at[rows, :], sink_ref, gna_ref,
                                 _band_geometry(step * blocks + b), ya_ref.at[rows, :], oa_ref.at[rows, :],
                                 pr_ref.at[rows, :], sp_ref.at[rows, :])

        xv = x_ref[...]
        h = (xv * _rstd(xv) * g_ref[...]).astype(BF16)
        h_ref[...] = h
        q_ref[...] = _nt(h, wt_ref[ROW_Q:ROW_KV, :])
        kv_ref[...] = _nt(h, wt_ref[ROW_KV:ROW_GA, :])
        ga_ref[...] = _nt(h, wt_ref[ROW_GA:D_IN_PROJ, :])
        attention_blocks = [attention(b) for b in range(blocks)]
        for lo in range(0, D_PC, PC_PIECE):
            pc_ref[:, lo:lo + PC_PIECE] = _nt(h, wt_ref[lo:lo + PC_PIECE, :])
            for stages_of_block in attention_blocks:
                next(stages_of_block, None)
        for stages_of_block in attention_blocks:
            for _ in stages_of_block:
                pass
        kv_last[...] = kv_ref[tile - BLK:tile, :]

        cb, _, _, _, _, _, conv = _conv_core(pc_ref, zbuf, cw_ref)
        yc = cb * conv
        silu, _ = _silu_and_grad(pc_ref[:, 3 * D_CONV:4 * D_CONV])
        oc_ref[...] = (yc * _rstd(yc) * gn_ref[...] * silu).astype(BF16)
        zbuf[0:8, :] = zbuf[tile:tile + 8, :]

        @pl.when(step == stage_steps[-1])
        def _():
            stages[-1]()
            to_hbm.wait()

    def row(width):
        return pl.BlockSpec((tile, width), lambda i: (i, 0))

    any_spec = pl.BlockSpec(memory_space=pl.ANY)
    dma = pltpu.SemaphoreType.DMA
    wt_shape = (N_CHIPS * wt_sh.shape[0], wt_sh.shape[1])
    cw_shape = (cw_sh.shape[0], N_CHIPS * cw_cols)
    return pl.pallas_call(
        body, name="fwd_in", grid=(n_tiles,),
        out_shape=(jax.ShapeDtypeStruct((seq, D_MODEL), BF16), jax.ShapeDtypeStruct((seq, D_PC), F32),
                   jax.ShapeDtypeStruct((seq, D_ATTN), F32), jax.ShapeDtypeStruct((seq, 2 * D_KV), F32),
                   jax.ShapeDtypeStruct((seq, D_ATTN), F32), jax.ShapeDtypeStruct((seq, D_CONV), BF16),
                   pltpu.HBM((seq, D_ATTN), F32), pltpu.HBM((seq, D_ATTN), BF16),
                   pltpu.HBM((seq, N_HEADS * BLK), BF16), pltpu.HBM((seq, 128), F32),
                   pltpu.HBM(wt_shape, BF16), jax.ShapeDtypeStruct(cw_shape, F32),
                   jax.ShapeDtypeStruct((N_CHIPS * wo_sh.shape[0], wo_sh.shape[1]), BF16)),
        in_specs=[row(D_MODEL), _resident((1, D_MODEL)), any_spec, any_spec, any_spec, _resident((1, D_CONV)),
                  pl.BlockSpec(memory_space=pltpu.SMEM), _resident((1, D_ATTN))],
        out_specs=(row(D_MODEL), row(D_PC), row(D_ATTN), row(2 * D_KV), row(D_ATTN), row(D_CONV), row(D_ATTN),
                   row(D_ATTN), row(N_HEADS * BLK), row(128), any_spec, pl.BlockSpec(cw_shape, lambda i: (0, 0)),
                   any_spec),
        scratch_shapes=[pltpu.VMEM((tile + 8, D_CONV), F32), pltpu.VMEM((BLK, 2 * D_KV), F32),
                        pltpu.VMEM(wt_shape, BF16), pltpu.VMEM((2, AG_CAST_ROWS, wt_sh.shape[1]), F32),
                        pltpu.VMEM((N_CHIPS,) + cw_sh.shape, F32),
                        dma((8,)), dma((8,)), dma((2,)), dma((3,)), dma((3,)), dma((1,))] + _ag_scratch(wo_sh.shape),
        compiler_params=_params(62, ("arbitrary",)),
    )(x, norm_in, wt_sh, wo_sh, cw_sh, norm_conv_out, sinks, norm_attn_out)


def _conv_core(pc_ref, zbuf, cw_ref):
    tile = pc_ref.shape[0]
    cb = pc_ref[:, 0:D_CONV]
    cc = pc_ref[:, D_CONV:2 * D_CONV]
    cu = pc_ref[:, 2 * D_CONV:3 * D_CONV]
    z = cc * cu
    zbuf[8:tile + 8, :] = z
    z1 = zbuf[7:tile + 7, :]
    z2 = zbuf[6:tile + 6, :]
    conv = cw_ref[0:1, :] * z2 + cw_ref[1:2, :] * z1 + cw_ref[2:3, :] * z
    return cb, cc, cu, z, z1, z2, conv


def _band_geometry(block_index):
    qi = lax.broadcasted_iota(jnp.int32, (BLK, BLK), 0)
    kp = lax.broadcasted_iota(jnp.int32, (BLK, BLK), 1)
    use_cur = kp <= qi
    dist = jnp.where(use_cur, qi - kp, qi - kp + BLK).astype(F32)
    valid = use_cur | (block_index > 0)
    return use_cur, dist, valid


def _block_diag(cur, prev, group):
    lane = lax.broadcasted_iota(jnp.int32, cur.shape, 1)

    def halves(t):
        other = pltpu.roll(t, 64, 1)
        lo, hi = (t, other) if group == 0 else (other, t)
        return jnp.where(lane < 64, lo, 0.0), jnp.where(lane >= 64, hi, 0.0)

    return jnp.concatenate(halves(cur) + halves(prev), axis=0).astype(BF16)


def _merge(s4, use_cur):
    return (jnp.where(use_cur, s4[:, 0:BLK], s4[:, 2 * BLK:3 * BLK]),
            jnp.where(use_cur, s4[:, BLK:2 * BLK], s4[:, 3 * BLK:4 * BLK]))


def _split(a, b, use_cur):
    return jnp.concatenate([jnp.where(use_cur, a, 0.0), jnp.where(use_cur, b, 0.0),
                            jnp.where(use_cur, 0.0, a), jnp.where(use_cur, 0.0, b)], axis=1)


def _softmax_head(s, head, sink, dist, valid):
    sc = jnp.where(valid, s - SLOPES[head] * dist, -jnp.inf)
    m = jnp.maximum(jnp.max(sc, axis=-1, keepdims=True), sink)
    p = jnp.exp(sc - m)
    es = jnp.exp(sink - m)
    inv = 1.0 / (jnp.sum(p, axis=-1, keepdims=True) + es)
    return p * inv, es * inv


def _attn_operands(q_ref, kvc_ref, kvp_ref):
    groups = range(N_HEADS // HEADS_PER_KV)
    kbd = [_block_diag(kvc_ref[:, 0:D_KV], kvp_ref[:, 0:D_KV], g) for g in groups]
    vbd = [_block_diag(kvc_ref[:, D_KV:2 * D_KV], kvp_ref[:, D_KV:2 * D_KV], g) for g in groups]
    qps = [(q_ref[:, j * 128:(j + 1) * 128] * SCALE).astype(BF16) for j in range(N_HEADS // 2)]
    return qps, kbd, vbd


def _attn_forward(q_ref, kvc_ref, kvp_ref, ga_ref, sink_ref, gn_ref, geometry, ya_ref, oa_ref, pr_ref, sp_ref):
    use_cur, dist, valid = geometry
    _, kbd, vbd = operands = _attn_operands(q_ref, kvc_ref, kvp_ref)
    yield
    scores = []
    for j, qp in enumerate(operands[0]):
        scores += _merge(_nt(qp, kbd[j // 4]), use_cur)
    yield
    sinks = [sink_ref[0, h] for h in range(N_HEADS)]
    scores = [jnp.where(valid, s - SLOPES[h] * dist, -jnp.inf) for h, s in enumerate(scores)]
    maxes = [jnp.maximum(jnp.max(s, axis=-1, keepdims=True), sinks[h]) for h, s in enumerate(scores)]
    yield
    exps = [jnp.exp(s - m) for s, m in zip(scores, maxes)]
    sink_exps = [jnp.exp(sinks[h] - m) for h, m in enumerate(maxes)]
    yield
    invs = [1.0 / (jnp.sum(e, axis=-1, keepdims=True) + se) for e, se in zip(exps, sink_exps)]
    probs = [e * inv for e, inv in zip(exps, invs)]
    pr_ref[...] = jnp.concatenate(probs, axis=1).astype(BF16)
    lane = lax.broadcasted_iota(jnp.int32, (BLK, 128), 1)
    sp_ref[...] = sum(jnp.where(lane == h, se * inv, 0.0) for h, (se, inv) in enumerate(zip(sink_exps, invs)))
    yield
    p4s = [_split(probs[2 * j], probs[2 * j + 1], use_cur).astype(BF16) for j in range(N_HEADS // 2)]
    ya = jnp.concatenate([_nn(p4, vbd[j // 4]) for j, p4 in enumerate(p4s)], axis=1)
    ya_ref[...] = ya
    yield
    silu, _ = _silu_and_grad(ga_ref[...])
    oa_ref[...] = (ya * _rstd(ya) * gn_ref[...] * silu).astype(BF16)


def _out_proj_loss(x, oc, oa, wo, norm_final, target, pc, conv_w, norm_conv_out):
    seq = x.shape[0]
    tile = TOK_TILE
    n_tiles = seq // tile

    def body(x_ref, oc_ref, oa_ref, wo_ref, gf_ref, t_ref, pc_ref, hcc_ref, hcu_ref, cw_ref, gn_ref,
             dx2_ref, doa_ref, gwo_ref, dpc_ref, small_ref, loss_acc, zbuf, dbuf):
        step = pl.program_id(0)

        def small_add(row, value):
            small_ref[row:row + 1, :] += jnp.sum(value, axis=0, keepdims=True)

        @pl.when(step == 0)
        def _():
            gwo_ref[...] = jnp.zeros_like(gwo_ref)
            small_ref[...] = jnp.zeros_like(small_ref)
            loss_acc[...] = jnp.zeros_like(loss_acc)
            dbuf[tile:tile + 8, :] = jnp.zeros((8, D_CONV), F32)

        x2 = x_ref[...] + _nn(oc_ref[...], wo_ref[0:D_CONV, :]) + _nn(oa_ref[...], wo_ref[D_CONV:D_MIX, :])
        r = _rstd(x2)
        xhat = x2 * r
        err = xhat * gf_ref[...] - t_ref[...]
        loss_acc[...] += jnp.sum(err * err, axis=0, keepdims=True) * (0.5 / D_MODEL)
        dy = err * (1.0 / D_MODEL)
        small_add(SMALL_NORM_FINAL, dy * xhat)
        dx2 = _rms_bwd(dy * gf_ref[...], xhat, r)
        dx2_ref[...] = dx2
        db = dx2.astype(BF16)
        do = _nt(db, wo_ref[0:D_CONV, :])
        doa_ref[...] = _nt(db, wo_ref[D_CONV:D_MIX, :])
        gwo_ref[0:D_CONV, :] += _tn(oc_ref[...], db)
        gwo_ref[D_CONV:D_MIX, :] += _tn(oa_ref[...], db)

        is_first_tile = step == n_tiles - 1
        zbuf[0:8, :] = jnp.where(is_first_tile, 0.0, hcc_ref[...] * hcu_ref[...])
        cb, cc, cu, z, z1, z2, conv = _conv_core(pc_ref, zbuf, cw_ref)
        silu, dsilu = _silu_and_grad(pc_ref[:, 3 * D_CONV:4 * D_CONV])
        yc = cb * conv
        rc = _rstd(yc)
        chat = yc * rc
        dn = do * silu
        dpc_ref[:, 3 * D_CONV:4 * D_CONV] = (do * (chat * gn_ref[...]) * dsilu).astype(BF16)
        small_add(SMALL_NORM_CONV, dn * chat)
        dyc = _rms_bwd(dn * gn_ref[...], chat, rc)
        dpc_ref[:, 0:D_CONV] = (dyc * conv).astype(BF16)
        dconv = dyc * cb
        small_add(SMALL_CONV_W, dconv * z2)
        small_add(SMALL_CONV_W + 1, dconv * z1)
        small_add(SMALL_CONV_W + 2, dconv * z)
        dbuf[0:tile, :] = dconv
        dz = cw_ref[2:3, :] * dconv + cw_ref[1:2, :] * dbuf[1:tile + 1, :] + cw_ref[0:1, :] * dbuf[2:tile + 2, :]
        dpc_ref[:, D_CONV:2 * D_CONV] = (dz * cu).astype(BF16)
        dpc_ref[:, 2 * D_CONV:3 * D_CONV] = (dz * cc).astype(BF16)
        dbuf[tile:tile + 8, :] = dbuf[0:8, :]

        @pl.when(step == n_tiles - 1)
        def _():
            lane = lax.broadcasted_iota(jnp.int32, (1, D_MODEL), 1)
            small_ref[SMALL_MISC:SMALL_MISC + 1, :] = jnp.where(lane == SMALL_LOSS_LANE, jnp.sum(loss_acc[...]), 0.0)

    def rev(i):
        return n_tiles - 1 - i

    def row(width):
        return pl.BlockSpec((tile, width), lambda i: (rev(i), 0))

    def halo(col_block):
        return pl.BlockSpec((8, D_CONV), lambda i: (jnp.maximum(rev(i) * (tile // 8) - 1, 0), col_block))

    def const(shape):
        return pl.BlockSpec(shape, lambda i: (0, 0))

    return pl.pallas_call(
        body, name="out_proj_loss", grid=(n_tiles,),
        out_shape=(jax.ShapeDtypeStruct((seq, D_MODEL), F32), jax.ShapeDtypeStruct((seq, D_ATTN), F32),
                   jax.ShapeDtypeStruct((D_MIX, D_MODEL), F32), jax.ShapeDtypeStruct((seq, D_PC), BF16),
                   jax.ShapeDtypeStruct((SMALL_ROWS, D_MODEL), F32)),
        in_specs=[row(D_MODEL), row(D_CONV), row(D_ATTN), _resident(wo.shape), _resident((1, D_MODEL)), row(D_MODEL),
                  row(D_PC), halo(1), halo(2), _resident(conv_w.shape), _resident((1, D_CONV))],
        out_specs=(row(D_MODEL), row(D_ATTN), const((D_MIX, D_MODEL)), row(D_PC), const((SMALL_ROWS, D_MODEL))),
        scratch_shapes=[pltpu.VMEM((1, D_MODEL), F32), pltpu.VMEM((tile + 8, D_CONV), F32),
                        pltpu.VMEM((tile + 8, D_CONV), F32)],
        compiler_params=_params(62, ("arbitrary",)),
    )(x, oc, oa, wo, norm_final, target, pc, pc, pc, conv_w, norm_conv_out)


def _attn_bwd(q, kv, ga, ya, doa, probs, sink_probs, norm_attn_out, gwo, small):
    seq = q.shape[0]
    per_step = 2
    n_steps = seq // (per_step * BLK)
    stage_steps = (0, 1, n_steps // 4, (5 * n_steps) // 8, n_steps - 1, n_steps - 1)

    def body(q_ref, kvc_ref, kvp_ref, ga_ref, ya_ref, doa_ref, pr_ref, sp_ref, gn_ref, gwo_ref, small_ref,
             dqg_ref, small_out, gwo_sh, gna_ref, gs_ref, carry, dya_buf, *rs_scratch):
        step = pl.program_id(0)
        rs_stages = _rs_wout_stages(gwo_ref, gwo_sh, *rs_scratch)
        for at, stage in zip(stage_steps[:-1], rs_stages[:-1]):
            pl.when(step == at)(stage)

        @pl.when(step == 0)
        def _():
            gna_ref[...] = jnp.zeros_like(gna_ref)
            gs_ref[...] = jnp.zeros_like(gs_ref)
            carry[...] = jnp.zeros_like(carry)

        lane = lax.broadcasted_iota(jnp.int32, (BLK, 128), 1)

        def fold(bd):
            return (jnp.where(lane < 64, bd[0:BLK], 0.0) + jnp.where(lane >= 64, bd[BLK:2 * BLK], 0.0),
                    jnp.where(lane < 64, bd[2 * BLK:3 * BLK], 0.0) + jnp.where(lane >= 64, bd[3 * BLK:4 * BLK], 0.0))

        def one_block(b):
            rows = slice(b * BLK, (b + 1) * BLK)
            kv_prev = kvp_ref if b == 0 else kvc_ref.at[(b - 1) * BLK:b * BLK, :]
            ya = ya_ref[rows, :]
            r = _rstd(ya)
            xhat = ya * r
            silu, dsilu = _silu_and_grad(ga_ref[rows, :])
            do = doa_ref[rows, :]
            dn = do * silu
            dqg_ref[rows, D_ATTN:2 * D_ATTN] = (do * (xhat * gn_ref[...]) * dsilu).astype(BF16)
            gna_ref[...] += jnp.sum(dn * xhat, axis=0, keepdims=True)
            dya_buf[rows, :] = _rms_bwd(dn * gn_ref[...], xhat, r).astype(BF16)

            use_cur = _band_geometry(per_step * (n_steps - 1 - step) + b)[0]
            pairs = range(N_HEADS // 2)
            qps, kbd, vbd = _attn_operands(q_ref.at[rows, :], kvc_ref.at[rows, :], kv_prev)
            probs = [pr_ref[rows, h * BLK:(h + 1) * BLK].astype(F32) for h in range(N_HEADS)]
            dyps = [dya_buf[rows, j * 128:(j + 1) * 128] for j in pairs]
            dps = []
            for j in pairs:
                dps += _merge(_nt(dyps[j], vbd[j // 4]), use_cur)
            deltas = [jnp.sum(p * dp, axis=-1, keepdims=True) for p, dp in zip(probs, dps)]
            dss = [p * (dp - delta) for p, dp, delta in zip(probs, dps, deltas)]
            delta_lanes = sum(jnp.where(lane == h, deltas[h], 0.0) for h in range(N_HEADS))
            gs_ref[...] -= jnp.sum(sp_ref[rows, :] * delta_lanes, axis=0, keepdims=True)
            ds4s = [_split(dss[2 * j], dss[2 * j + 1], use_cur).astype(BF16) for j in pairs]
            p4s = [_split(probs[2 * j], probs[2 * j + 1], use_cur).astype(BF16) for j in pairs]
            dqg_ref[rows, 0:D_ATTN] = jnp.concatenate([_nn(ds4s[j], kbd[j // 4]) * SCALE for j in pairs], axis=1).astype(BF16)
            sums = []
            for group in range(N_HEADS // HEADS_PER_KV):
                acc = [jnp.zeros((BLK, 128), F32) for _ in range(4)]
                for j in range(group * 4, group * 4 + 4):
                    for slot, part in enumerate(fold(_tn(ds4s[j], qps[j])) + fold(_tn(p4s[j], dyps[j]))):
                        acc[slot] = acc[slot] + part
                sums.append([a + pltpu.roll(a, 64, 1) for a in acc])
            dk_cur, dk_prev, dv_cur, dv_prev = (jnp.where(lane < 64, lo, hi) for lo, hi in zip(sums[0], sums[1]))
            dqg_ref[rows, 2 * D_ATTN:2 * D_ATTN + 2 * D_KV] = (jnp.concatenate([dk_cur, dv_cur], axis=1) + carry[...]).astype(BF16)
            carry[...] = jnp.concatenate([dk_prev, dv_prev], axis=1)

        for b in reversed(range(per_step)):
            one_block(b)

        @pl.when(step == n_steps - 1)
        def _():
            small_out[...] = small_ref[...]
            small_out[SMALL_NORM_ATTN:SMALL_NORM_ATTN + 1, :] = gna_ref[...]
            small_out[SMALL_MISC:SMALL_MISC + 1, 0:128] = small_ref[SMALL_MISC:SMALL_MISC + 1, 0:128] + gs_ref[...]

        pl.when(step == stage_steps[-1])(rs_stages[-1])

    def rows(width):
        return pl.BlockSpec((per_step * BLK, width), lambda i: (n_steps - 1 - i, 0))

    kv_prev = pl.BlockSpec((BLK, 2 * D_KV), lambda i: (jnp.maximum(per_step * (n_steps - 1 - i) - 1, 0), 0))
    any_spec = pl.BlockSpec(memory_space=pl.ANY)
    return pl.pallas_call(
        body, name="attn_bwd", grid=(n_steps,),
        out_shape=(jax.ShapeDtypeStruct((seq, D_QG), BF16), pltpu.HBM(small.shape, F32),
                   pltpu.HBM((gwo.shape[0] // N_CHIPS, gwo.shape[1]), F32)),
        in_specs=[rows(D_ATTN), rows(2 * D_KV), kv_prev, rows(D_ATTN), rows(D_ATTN), rows(D_ATTN),
                  rows(N_HEADS * BLK), rows(128), _resident((1, D_ATTN)), any_spec, _resident(small.shape)],
        out_specs=(rows(D_QG), pl.BlockSpec(small.shape, lambda i: (0, 0)), any_spec),
        scratch_shapes=[pltpu.VMEM((1, D_ATTN), F32), pltpu.VMEM((1, 128), F32),
                        pltpu.VMEM((BLK, 2 * D_KV), F32), pltpu.VMEM((per_step * BLK, D_ATTN), BF16)] + _rs_wout_scratch(gwo.shape),
        compiler_params=_params(44, ("arbitrary",)),
    )(q, kv, kv, ga, ya, doa, probs, sink_probs, norm_attn_out, gwo, small)


GBLK = 256
GSUB = 64
PAIR_RING = 4
LAG_PAIR, LAG_HOP1, LAG_HOP2 = 1, 7, 13


def _bwd_in(dpc, dqg, h, wt, x, norm_in, dx2, small):
    seq = x.shape[0]
    n_blk = D_IN_PROJ // GBLK
    per_chip = n_blk // N_CHIPS
    n_slots = (n_blk + 1) // 2
    n_sub = GBLK // GSUB
    chip_rows = D_IN_PROJ // N_CHIPS
    tile = TOK_TILE
    n_tiles = seq // tile
    n_steps = n_blk + max(n_tiles, LAG_HOP2 + 1)
    chunk = min(seq, 512)
    blk_q, blk_kv, blk_ga = ROW_Q // GBLK, ROW_KV // GBLK, ROW_GA // GBLK

    def block_of(i):
        k = i % N_CHIPS
        robin = per_chip * ((k % 2) * 2 + k // 2) + i // N_CHIPS
        if isinstance(i, int):
            return robin if i < per_chip * N_CHIPS else i
        return jnp.where(i < per_chip * N_CHIPS, robin, i)

    def owner_of(i):
        return (i // N_CHIPS) % 2

    def slot_of(i):
        return (i // (2 * N_CHIPS)) * N_CHIPS + i % N_CHIPS

    def body(dpc_ref, dqg_ref, wt_ref, h_ref, x_ref, g_ref, dx2_ref, small_ref, gx_ref, small_sum, gwt_sh,
             dh_acc, gni, keep, pbuf, xbuf, land, land2, small_land,
             pair_send, pair_recv, h1_send, h1_recv, h2_send, h2_recv, sw_send, sw_recv, sm_send, sm_recv, out_sem):
        step = pl.program_id(0)
        x_i, y_i, c = lax.axis_index("x"), lax.axis_index("y"), lax.axis_index("c")
        me = 4 * x_i + 2 * y_i + c
        j = 2 * x_i + y_i
        pa = (_xor(x_i, 1 - c), _xor(y_i, c), c)
        pb = (_xor(x_i, c), _xor(y_i, 1 - c), c)
        sib = (x_i, y_i, 1 - c)
        ja = 2 * pa[0] + pa[1]
        jb = 2 * pb[0] + pb[1]
        jd = 3 - j

        def remote(src, dst, send, recv, to):
            return pltpu.make_async_remote_copy(src_ref=src, dst_ref=dst, send_sem=send, recv_sem=recv,
                                                device_id=to, device_id_type=MESH)

        def piece(ref, slot, u, n):
            return ref.at[slot, pl.ds(u * GSUB, n * GSUB), :]

        def chip_rows_at(ref, local, n):
            return ref.at[pl.ds(pl.multiple_of(local, GSUB), n * GSUB), :]

        def pair_copy(i):
            slot = slot_of(i)
            return remote(pbuf.at[i % PAIR_RING], land.at[slot], pair_send.at[slot], pair_recv.at[slot], sib)

        def h1_copy(slot, u, n):
            k = slot * n_sub + u
            return remote(piece(xbuf, slot, u, n), piece(xbuf, slot, u, n), h1_send.at[k], h1_recv.at[k], pa)

        def h2_copy(slot, u, n, local):
            k = slot * n_sub + u
            return remote(piece(xbuf, slot, u, n), chip_rows_at(land2, local, n), h2_send.at[k], h2_recv.at[k], pb)

        def sw_copy(slot, u, n, local):
            k = slot * n_sub + u
            return remote(piece(keep, slot, u, n), chip_rows_at(gwt_sh, local, n), sw_send.at[k], sw_recv.at[k], sib)

        def own_copy(slot, u, n, local):
            return pltpu.make_async_copy(piece(keep, slot, u, n), chip_rows_at(gwt_sh, local, n), out_sem.at[slot * n_sub + u])

        def owned(i):
            return (i >= 0) & (i < n_blk) & (owner_of(i) == c)

        def chip_of(blk, u):
            row = blk * GBLK + u * GSUB
            chip = row // chip_rows
            return chip, row - chip * chip_rows

        def pieces(blk):
            first, local = chip_of(blk, 0)
            whole = first == chip_of(blk, n_sub - 1)[0]
            if isinstance(blk, int):
                return [(True, 0, n_sub, first, local)] if whole else [(True, u, 1) + chip_of(blk, u) for u in range(n_sub)]
            return [(whole, 0, n_sub, first, local)] + [(jnp.logical_not(whole), u, 1) + chip_of(blk, u) for u in range(n_sub)]

        @pl.when(step == 0)
        def _():
            dh_acc[...] = jnp.zeros_like(dh_acc)
            gni[...] = jnp.zeros_like(gni)

        @pl.when(step < n_blk)
        def _():
            from_pc = block_of(step) < blk_q
            block = _tn(jnp.where(from_pc, dpc_ref[...], dqg_ref[...]), h_ref[...])
            for t in range(0, seq, chunk):
                d = jnp.where(from_pc, dpc_ref[t:t + chunk, :], dqg_ref[t:t + chunk, :])
                dh_acc[t:t + chunk, :] += _nn(d, wt_ref[...])

            @pl.when(owner_of(step) == c)
            def _():
                keep[slot_of(step)] = block

            @pl.when(owner_of(step) != c)
            def _():
                @pl.when(step >= 2 * PAIR_RING)
                def _():
                    pair_copy(step - 2 * PAIR_RING).wait_send()
                pbuf[step % PAIR_RING] = block.astype(BF16)
                pair_copy(step).start()

        i1 = step - LAG_PAIR

        @pl.when(owned(i1))
        def _():
            slot = slot_of(i1)
            pair_copy(i1).wait_recv()
            _accumulate(keep.at[slot], land.at[slot])
            for cond, u, n, chip, _ in pieces(block_of(i1)):
                @pl.when(cond & ((chip == ja) | (chip == jd)))
                def _(u=u, n=n):
                    _cast_rows(piece(keep, slot, u, n), piece(xbuf, slot, u, n))
                    h1_copy(slot, u, n).start()

        i2 = step - LAG_HOP1

        @pl.when(owned(i2))
        def _():
            slot = slot_of(i2)
            for cond, u, n, chip, local in pieces(block_of(i2)):
                @pl.when(cond & ((chip == j) | (chip == jb)))
                def _(u=u, n=n, chip=chip, local=local):
                    h1_copy(slot, u, n).wait_recv()
                    _accumulate(piece(keep, slot, u, n), piece(xbuf, slot, u, n))

                    @pl.when(chip == jb)
                    def _():
                        _cast_rows(piece(keep, slot, u, n), piece(xbuf, slot, u, n))
                        h2_copy(slot, u, n, local).start()

                @pl.when(cond & ((chip == ja) | (chip == jd)))
                def _(u=u, n=n):
                    h1_copy(slot, u, n).wait_send()

        i3 = step - LAG_HOP2

        @pl.when(owned(i3))
        def _():
            slot = slot_of(i3)
            for cond, u, n, chip, local in pieces(block_of(i3)):
                @pl.when(cond & (chip == j))
                def _(u=u, n=n, local=local):
                    h2_copy(slot, u, n, local).wait_recv()
                    _accumulate(piece(keep, slot, u, n), chip_rows_at(land2, local, n))
                    own_copy(slot, u, n, local).start()
                    sw_copy(slot, u, n, local).start()

                @pl.when(cond & (chip == jb))
                def _(u=u, n=n, local=local):
                    h2_copy(slot, u, n, local).wait_send()

        e = step - n_blk

        @pl.when((e >= 0) & (e < n_tiles))
        def _():
            dh = dh_acc[pl.ds(pl.multiple_of(e * tile, tile), tile), :]
            xv = x_ref[...]
            r = _rstd(xv)
            xhat = xv * r
            gni[...] += jnp.sum(dh * xhat, axis=0, keepdims=True)
            gx_ref[...] = _rms_bwd(dh * g_ref[...], xhat, r) + dx2_ref[...]

        others = [(dx, dy, dc) for dx in (0, 1) for dy in (0, 1) for dc in (0, 1)][1:]
        sends = [remote(small_land.at[me], small_land.at[me], sm_send.at[k], sm_recv.at[k],
                        (_xor(x_i, dx), _xor(y_i, dy), _xor(c, dc))) for k, (dx, dy, dc) in enumerate(others)]

        @pl.when(step == min(n_blk + n_tiles, n_steps - 1))
        def _():
            small_land[me] = small_ref[...]
            small_land[me, SMALL_NORM_IN:SMALL_NORM_IN + 1, :] = gni[...]
            for cp in sends:
                cp.start()

        @pl.when(step == n_steps - 1)
        def _():
            for i in range(n_blk):
                if i + 2 * PAIR_RING >= n_blk:
                    @pl.when(owner_of(i) != c)
                    def _(i=i):
                        pair_copy(i).wait_send()
                for _, u, n, chip, local in pieces(block_of(i)):
                    @pl.when((j == chip) & (c == owner_of(i)))
                    def _(i=i, u=u, n=n, local=local):
                        sw_copy(slot_of(i), u, n, local).wait_send()
                        own_copy(slot_of(i), u, n, local).wait()

                    @pl.when((j == chip) & (c != owner_of(i)))
                    def _(i=i, u=u, n=n, local=local):
                        sw_copy(slot_of(i), u, n, local).wait_recv()
            for cp in sends:
                cp.wait_recv()
            total = small_land[0]
            for dev in range(1, 8):
                total = total + small_land[dev]
            small_sum[...] = total
            for cp in sends:
                cp.wait_send()

    def blk_at(i):
        return block_of(jnp.clip(i, 0, n_blk - 1))

    last_pc_step = max(i for i in range(n_blk) if block_of(i) < blk_q)

    def next_block(i, in_pc):
        i = jnp.clip(i, 0, n_blk - 1)
        step = jnp.full_like(i, last_pc_step if in_pc else n_blk - 1)
        for ahead in reversed(range(N_CHIPS)):
            cand = jnp.minimum(i + ahead, n_blk - 1)
            step = jnp.where((block_of(cand) < blk_q) == in_pc, cand, step)
        return block_of(step)

    def dqg_block(i):
        b = next_block(i, False)
        q_blk = jnp.clip(b - blk_q, 0, blk_kv - blk_q - 1)
        ga_blk = (D_ATTN // GBLK) + jnp.clip(b - blk_ga, 0, n_blk - blk_ga - 1)
        return jnp.where(b < blk_kv, q_blk, jnp.where(b == blk_kv, 2 * D_ATTN // GBLK, ga_blk))

    def tok(i):
        return (jnp.clip(i - n_blk, 0, n_tiles - 1), 0)

    n_piece = n_slots * n_sub
    dma = pltpu.SemaphoreType.DMA
    return pl.pallas_call(
        body, name="bwd_in", grid=(n_steps,),
        out_shape=(jax.ShapeDtypeStruct((seq, D_MODEL), F32), jax.ShapeDtypeStruct(small.shape, F32),
                   jax.ShapeDtypeStruct((chip_rows, D_MODEL), F32)),
        in_specs=[pl.BlockSpec((seq, GBLK), lambda i: (0, next_block(i, True))),
                  pl.BlockSpec((seq, GBLK), lambda i: (0, dqg_block(i))),
                  pl.BlockSpec((GBLK, D_MODEL), lambda i: (blk_at(i), 0)),
                  _resident(h.shape),
                  pl.BlockSpec((tile, D_MODEL), tok), _resident((1, D_MODEL)), pl.BlockSpec((tile, D_MODEL), tok),
                  _resident(small.shape)],
        out_specs=(pl.BlockSpec((tile, D_MODEL), tok), pl.BlockSpec(small.shape, lambda i: (0, 0)),
                   pl.BlockSpec(memory_space=pl.ANY)),
        scratch_shapes=[pltpu.VMEM((seq, D_MODEL), F32), pltpu.VMEM((1, D_MODEL), F32),
                        pltpu.VMEM((n_slots, GBLK, D_MODEL), F32), pltpu.VMEM((PAIR_RING, GBLK, D_MODEL), BF16),
                        pltpu.VMEM((n_slots, GBLK, D_MODEL), BF16), pltpu.VMEM((n_slots, GBLK, D_MODEL), BF16),
                        pltpu.VMEM((chip_rows, D_MODEL), BF16), pltpu.VMEM((8,) + small.shape, F32),
                        dma((n_slots,)), dma((n_slots,)), dma((n_piece,)), dma((n_piece,)), dma((n_piece,)),
                        dma((n_piece,)), dma((n_piece,)), dma((n_piece,)), dma((7,)), dma((7,)), dma((n_piece,))],
        compiler_params=_params(62, ("arbitrary",)),
    )(dpc, dqg, wt, h, x, norm_in, dx2, small)


def _accumulate(dst_ref, src_ref, rows=16):
    def step(i, carry):
        sl = pl.ds(pl.multiple_of(i * rows, rows), rows)
        dst_ref[sl, :] = dst_ref[sl, :] + src_ref[sl, :].astype(F32)
        return carry
    lax.fori_loop(0, dst_ref.shape[0] // rows, step, 0)


def _rs_wout_scratch(gwo_shape):
    o_half, width = gwo_shape[0] // N_CHIPS // 2, gwo_shape[1]
    return [pltpu.VMEM((4, o_half, width), F32), pltpu.VMEM((4, o_half, width), BF16),
            pltpu.VMEM((4, o_half, width), BF16), pltpu.VMEM((2, o_half, width), BF16),
            pltpu.VMEM((o_half, width), BF16),
            pltpu.SemaphoreType.DMA((8,)), pltpu.SemaphoreType.DMA((8,)), pltpu.SemaphoreType.DMA((4,))]


def _rs_wout_stages(gwo_ref, gwo_sh, acc_o, sb_o, r1o, r2o, r3o, send_sems, recv_sems, local_sems):
    o_rows = gwo_ref.shape[0] // N_CHIPS
    o_half = o_rows // 2
    x, y, c = lax.axis_index("x"), lax.axis_index("y"), lax.axis_index("c")
    j = 2 * x + y
    pa = (_xor(x, 1 - c), _xor(y, c), c)
    pb = (_xor(x, c), _xor(y, 1 - c), c)
    sib = (x, y, 1 - c)
    ja = 2 * pa[0] + pa[1]
    jb = 2 * pb[0] + pb[1]
    jd = 3 - j
    order = (ja, jd, jb, j)
    sib_order = (jb, jd, ja, j)

    def rcopy(k, src, dst, to):
        return pltpu.make_async_remote_copy(src_ref=src, dst_ref=dst, send_sem=send_sems.at[k],
                                            recv_sem=recv_sems.at[k], device_id=to, device_id_type=MESH)

    def o_rows_of(chip, half):
        return gwo_ref.at[pl.ds(pl.multiple_of(chip * o_rows + half * o_half, 8), o_half), :]

    def loads(chips, half):
        return [pltpu.make_async_copy(o_rows_of(chip, half), acc_o.at[s], local_sems.at[s]) for s, chip in enumerate(chips)]

    def pair_send(s):
        return rcopy(s, sb_o.at[s], r1o.at[s], sib)

    def out_half(half):
        return gwo_sh.at[pl.ds(pl.multiple_of(half * o_half, 8), o_half), :]

    hop1 = [rcopy(4 + s, sb_o.at[s], r2o.at[s], pa) for s in range(2)]
    hop2 = rcopy(6, sb_o.at[2], r3o, pb)
    swap = rcopy(7, acc_o.at[3], out_half(c), sib)
    mine = pltpu.make_async_copy(acc_o.at[3], out_half(c), local_sems.at[0])

    def resend(s, copy):
        pair_send(s).wait_send()
        _cast_rows(acc_o.at[s], sb_o.at[s])
        copy.start()

    def stage_load():
        for cp in loads(sib_order, 1 - c):
            cp.start()

    def stage_pair():
        for s, (cp, mine_in) in enumerate(zip(loads(sib_order, 1 - c), loads(order, c))):
            cp.wait()
            _cast_rows(acc_o.at[s], sb_o.at[s])
            pair_send(s).start()
            mine_in.start()

    def stage_hop1():
        for s, cp in enumerate(loads(order, c)):
            cp.wait()
            pair_send(s).wait_recv()
            _accumulate(acc_o.at[s], r1o.at[s])
            if s < 2:
                resend(s, hop1[s])

    def stage_hop2():
        hop1[1].wait_recv()
        _accumulate(acc_o.at[2], r2o.at[1])
        resend(2, hop2)
        hop1[0].wait_recv()
        _accumulate(acc_o.at[3], r2o.at[0])

    def stage_final():
        hop2.wait_recv()
        _accumulate(acc_o.at[3], r3o)
        mine.start()
        swap.start()

    def stage_drain():
        rcopy(7, acc_o.at[3], out_half(1 - c), sib).wait_recv()
        for cp in [pair_send(3)] + hop1 + [hop2, swap]:
            cp.wait_send()
        mine.wait()

    return stage_load, stage_pair, stage_hop1, stage_hop2, stage_final, stage_drain


def _adamw_big(groups, passed):
    arrays = [a for group in groups for a in group[:4]]
    steps = [group[0].shape[0] // group[4] for group in groups]
    first = [sum(steps[:k]) for k in range(len(steps) + 1)]
    n = len(arrays)

    def body(*refs):
        ins, passed_in, outs, passed_out = refs[:n], refs[n], refs[n + 1:2 * n + 1], refs[2 * n + 1]
        i = pl.program_id(0)
        for k in range(len(groups)):
            @pl.when((i >= first[k]) & (i < first[k + 1]))
            def _(k=k):
                w_ref, g_ref, m_ref, v_ref = ins[4 * k:4 * k + 4]
                _adamw_update(w_ref, g_ref[...], m_ref, v_ref, *outs[4 * k:4 * k + 4])

        @pl.when(i < steps[0])
        def _():
            passed_out[...] = passed_in[...]

    def spec(k, rows, width):
        return pl.BlockSpec((rows, width), lambda i: (jnp.clip(i - first[k], 0, steps[k] - 1), 0))

    specs = [spec(k, group[4], group[0].shape[1]) for k, group in enumerate(groups) for _ in range(4)]
    specs.append(spec(0, passed.shape[0] // steps[0], passed.shape[1]))
    out = pl.pallas_call(
        body, name="adamw_big", grid=(first[-1],),
        out_shape=tuple(jax.ShapeDtypeStruct(a.shape, a.dtype) for a in arrays + [passed]),
        in_specs=specs, out_specs=tuple(specs),
        compiler_params=_params(40, ("arbitrary",)),
    )(*arrays, passed)
    return [tuple(out[4 * k:4 * k + 4]) for k in range(len(groups))], out[-1]


def _adamw_update(w_ref, gv, m_ref, v_ref, go_ref, d_ref, nm_ref, nv_ref, at=...):
    go_ref[at] = gv
    nm = ADAM_B1 * m_ref[at] + (1.0 - ADAM_B1) * gv
    nv = ADAM_B2 * v_ref[at] + (1.0 - ADAM_B2) * (gv * gv)
    m_hat = nm / (1.0 - ADAM_B1 ** ADAM_STEP)
    v_hat = nv / (1.0 - ADAM_B2 ** ADAM_STEP)
    d_ref[at] = -ADAM_LR * (m_hat / (jnp.sqrt(v_hat) + ADAM_EPS) + ADAM_WD * w_ref[at])
    nm_ref[at] = nm
    nv_ref[at] = nv


def _adamw_small(chip, small_sum, weights, grads_of, ms, vs):
    n = len(weights)

    def body(chip_ref, small_ref, *refs):
        ins, outs, loss_ref = refs[:3 * n], refs[3 * n:-1], refs[-1]
        for k in range(n):
            w_ref, m_ref, v_ref = ins[3 * k:3 * k + 3]
            for at, gv in grads_of[k](small_ref, chip_ref):
                _adamw_update(w_ref, gv, m_ref, v_ref, *outs[4 * k:4 * k + 4], at=at)
        loss_ref[...] = small_ref[SMALL_MISC:SMALL_MISC + 1, SMALL_LOSS_LANE:SMALL_LOSS_LANE + 1]

    flat = [a for group in zip(weights, ms, vs) for a in group]
    vmem = pl.BlockSpec(memory_space=pltpu.VMEM)
    out = pl.pallas_call(
        body, name="adamw_small",
        out_shape=tuple(jax.ShapeDtypeStruct(w.shape, F32) for w in weights for _ in range(4))
        + (jax.ShapeDtypeStruct((1, 1), F32),),
        in_specs=[pl.BlockSpec(memory_space=pltpu.SMEM)] + [vmem] * (1 + 3 * n), out_specs=(vmem,) * (4 * n + 1),
    )(chip, small_sum, *flat)
    return [tuple(out[4 * k:4 * k + 4]) for k in range(n)], out[-1][0, 0]


def kernel(x, norm_in, w_in, conv_w, attn_sinks, norm_conv_out, norm_attn_out, w_out, norm_final, loss_target, m_norm_in, m_w_in, m_conv_w, m_attn_sinks, m_norm_conv_out, m_norm_attn_out, m_w_out, m_norm_final, v_norm_in, v_w_in, v_conv_w, v_attn_sinks, v_norm_conv_out, v_norm_attn_out, v_w_out, v_norm_final):
    chip = 2 * lax.axis_index("x") + lax.axis_index("y")
    xs, target = x[0], loss_target[0]
    norm_final2 = norm_final.reshape(1, D_MODEL)
    w_in_t, m_w_in_t, v_w_in_t = w_in[0].T, m_w_in[0].T, v_w_in[0].T

    def from_hbm(*arrays):
        return tuple(pltpu.with_memory_space_constraint(a, pltpu.HBM) for a in arrays)

    h, pc, q, kv, ga, oc, ya, oa, probs, sink_probs, wt, cw, wo = _fwd_in(
        xs, norm_in, w_in_t, w_out[0], conv_w.transpose(1, 0, 2), norm_conv_out, attn_sinks, norm_attn_out)
    dx2, doa, gwo, dpc, small = _out_proj_loss(xs, oc, oa, wo, norm_final2, target, pc, cw, norm_conv_out)
    q, kv, ga, ya, doa, probs, gwo, small = from_hbm(q, kv, ga, ya, doa, probs, gwo, small)
    dqg, small, gwo_sh = _attn_bwd(q, kv, ga, ya, doa, probs, sink_probs, norm_attn_out, gwo, small)
    grad_x, small_sum, gwt_sh = _bwd_in(dpc, dqg, h, wt, xs, norm_in, dx2, small)

    (up_w_in, up_w_out), grad_x = _adamw_big(
        [from_hbm(w_in_t, gwt_sh, m_w_in_t, v_w_in_t) + (200,), from_hbm(w_out[0], gwo_sh, m_w_out[0], v_w_out[0]) + (128,)],
        *from_hbm(grad_x))
    up_w_in = tuple(o.T[None] for o in up_w_in)
    up_w_out = tuple(o[None] for o in up_w_out)

    def row_of(r):
        return lambda small_ref, chip_ref: [(..., small_ref[r:r + 1, :])]

    def sink_lanes(small_ref, chip_ref):
        return [(..., small_ref[SMALL_MISC:SMALL_MISC + 1, 0:N_HEADS])]

    def conv_taps(small_ref, chip_ref):
        width = D_CONV // N_CHIPS
        cols = pl.ds(pl.multiple_of(chip_ref[0] * width, width), width)
        return [(k, small_ref[pl.ds(SMALL_CONV_W + k, 1), cols]) for k in range(conv_w.shape[1])]

    def small_view(a):
        return a.transpose(1, 0, 2) if a.ndim == 3 else a.reshape(-1, a.shape[-1])

    small_w = (norm_in, conv_w, attn_sinks, norm_conv_out, norm_attn_out, norm_final)
    small_m = (m_norm_in, m_conv_w, m_attn_sinks, m_norm_conv_out, m_norm_attn_out, m_norm_final)
    small_v = (v_norm_in, v_conv_w, v_attn_sinks, v_norm_conv_out, v_norm_attn_out, v_norm_final)
    small_g = (row_of(SMALL_NORM_IN), conv_taps, sink_lanes, row_of(SMALL_NORM_CONV), row_of(SMALL_NORM_ATTN),
               row_of(SMALL_NORM_FINAL))
    w_views, m_views, v_views = (tuple(small_view(a) for a in group) for group in (small_w, small_m, small_v))
    up_small, loss = _adamw_small(chip.astype(jnp.int32).reshape(1), small_sum, w_views, small_g, m_views, v_views)
    up_small = [tuple(o.transpose(1, 0, 2) if w.ndim == 3 else o.reshape(w.shape) for o in up)
                for up, w in zip(up_small, small_w)]
    up_norm_in, up_conv_w, up_sinks, up_norm_conv, up_norm_attn, up_norm_final = up_small
    updates = (up_norm_in, up_w_in, up_conv_w, up_sinks, up_norm_conv, up_norm_attn, up_w_out, up_norm_final)
    grads_out, deltas, new_m, new_v = zip(*updates)
    return (loss, grad_x[None], *grads_out, *deltas, *new_m, *new_v)
```
